```python
import math
import jax, jax.numpy as jnp
from jax import lax
import numpy as np

D_MODEL = 1024
BATCH = 8
SEQ = 4096
DEPTH = 2

N_MEM = 256
N_MIXERS = 2
D_MIX = D_MODEL
XA_HEADS = 4
XA_HEAD_DIM = D_MODEL // 16
D_XA = XA_HEADS * XA_HEAD_DIM
D_TOK = D_MIX - D_XA
CONV_WIDTH = 3
ML_HEADS = 4
ML_HEAD_DIM = D_TOK // ML_HEADS
ML_CHUNK = 64
QK_CONV_WIDTH = 4
D_FF = 256 * int(math.ceil(8 * D_MODEL / 3 / 256))
LN_EPS = 1e-5
DEEPNORM_ALPHA = (2.0 * DEPTH) ** 0.25
DEEPNORM_BETA = (8.0 * DEPTH) ** -0.25
N_CONV_LAYERS = (DEPTH + 1) // 2
N_MLSTM_LAYERS = DEPTH // 2
D_IN_CONV = 3 * D_TOK + D_XA
D_IN_MLSTM = 4 * D_TOK + 2 * ML_HEADS + D_XA

kernel_name = "hybrid_shortconv_mlstm_memxattn_macaron_deepnorm"


def layer_norm(x, g, b):
    xf = x.astype(jnp.float32)
    mu = jnp.mean(xf, -1, keepdims=True)
    var = jnp.mean(jnp.square(xf - mu), -1, keepdims=True)
    y = (xf - mu) * lax.rsqrt(var + LN_EPS)
    return (y * g.astype(jnp.float32) + b.astype(jnp.float32)).astype(x.dtype)


def swiglu(x, w_gate, w_up, w_down):
    return (jax.nn.silu(x @ w_gate) * (x @ w_up)) @ w_down


def causal_dwconv(u, w):
    width = w.shape[0]
    s = u.shape[1]
    up = jnp.pad(u, ((0, 0), (width - 1, 0), (0, 0)))
    out = up[:, width - 1:width - 1 + s] * w[width - 1]
    for j in range(width - 1):
        out = out + up[:, j:j + s] * w[j]
    return out


def memory_cross_attention(q, mem_kv):
    b, s, _ = q.shape
    q = q.reshape(b, s, XA_HEADS, XA_HEAD_DIM)
    k, v = jnp.split(mem_kv, 2, -1)
    k = k.reshape(b, N_MEM, XA_HEADS, XA_HEAD_DIM)
    v = v.reshape(b, N_MEM, XA_HEADS, XA_HEAD_DIM)
    scores = jnp.einsum('bshd,bmhd->bhsm', q, k).astype(jnp.float32) * (XA_HEAD_DIM ** -0.5)
    p = jax.nn.softmax(scores, -1).astype(v.dtype)
    o = jnp.einsum('bhsm,bmhd->bshd', p, v)
    return o.reshape(b, s, D_XA)


def short_conv_mixer(u, conv_w):
    b_gate, c_gate, x_in = jnp.split(u, 3, -1)
    return b_gate * causal_dwconv(c_gate * x_in, conv_w)


def mlstm_chunkwise(q, k, v, log_i, log_f):
    b, s, _ = q.shape
    h_, dh, L = ML_HEADS, ML_HEAD_DIM, ML_CHUNK
    nc = s // L

    def to_chunks(t):
        return t.astype(jnp.float32).reshape(b, nc, L, h_, dh).transpose(1, 0, 3, 2, 4)

    def gate_chunks(g):
        return g.astype(jnp.float32).reshape(b, nc, L, h_).transpose(1, 0, 3, 2)

    qc, kc, vc = to_chunks(q), to_chunks(k) * (dh ** -0.5), to_chunks(v)
    ic, fc = gate_chunks(log_i), gate_chunks(log_f)
    causal = jnp.tril(jnp.ones((L, L), dtype=bool))

    def step(carry, xs):
        c_st, n_st, m_st = carry
        q_, k_, v_, li, lf = xs
        bcum = jnp.cumsum(lf, -1)
        log_d = bcum[..., :, None] - bcum[..., None, :] + li[..., None, :]
        log_d = jnp.where(causal, log_d, -jnp.inf)
        log_inter = bcum + m_st[..., None]
        m_t = jnp.maximum(log_inter, jnp.max(log_d, -1))
        w_intra = jnp.exp(log_d - m_t[..., None])
        w_inter = jnp.exp(log_inter - m_t)
        sc = jnp.einsum('bhtd,bhsd->bhts', q_, k_) * w_intra
        num = (jnp.einsum('bhts,bhse->bhte', sc, v_)
               + w_inter[..., None] * jnp.einsum('bhtd,bhde->bhte', q_, c_st))
        den = jnp.sum(sc, -1) + w_inter * jnp.einsum('bhtd,bhd->bht', q_, n_st)
        h = num / jnp.maximum(jnp.abs(den), jnp.exp(-m_t))[..., None]
        b_last = bcum[..., -1]
        log_w = b_last[..., None] - bcum + li
        m_new = jnp.maximum(b_last + m_st, jnp.max(log_w, -1))
        w_k = jnp.exp(log_w - m_new[..., None])
        decay = jnp.exp(b_last + m_st - m_new)
        c_new = decay[..., None, None] * c_st + jnp.einsum('bhs,bhsd,bhse->bhde', w_k, k_, v_)
        n_new = decay[..., None] * n_st + jnp.einsum('bhs,bhsd->bhd', w_k, k_)
        return (c_new, n_new, m_new), h

    init = (jnp.zeros((b, h_, dh, dh), jnp.float32),
            jnp.zeros((b, h_, dh), jnp.float32),
            jnp.zeros((b, h_), jnp.float32))
    _, hs = lax.scan(step, init, (qc, kc, vc, ic, fc))
    return hs.transpose(1, 0, 3, 2, 4).reshape(b, s, h_, dh)


def mlstm_mixer(u, b_gates, qk_conv_w, head_norm_g):
    b, s, _ = u.shape
    qk, v, o_pre, gates = jnp.split(u, [2 * D_TOK, 3 * D_TOK, 4 * D_TOK], -1)
    qk = jax.nn.silu(causal_dwconv(qk, qk_conv_w))
    q, k = jnp.split(qk, 2, -1)
    gates = gates.astype(jnp.float32) + b_gates.astype(jnp.float32)
    log_i = gates[..., :ML_HEADS]
    log_f = jax.nn.log_sigmoid(gates[..., ML_HEADS:])
    h = mlstm_chunkwise(q, k, v, log_i, log_f)
    mu = jnp.mean(h, -1, keepdims=True)
    var = jnp.mean(jnp.square(h - mu), -1, keepdims=True)
    h = (h - mu) * lax.rsqrt(var + LN_EPS) * head_norm_g.astype(jnp.float32)
    return jax.nn.sigmoid(o_pre) * h.reshape(b, s, D_TOK).astype(u.dtype)


def _fwd_setup_inputs(seed: int = 0) -> dict:
    key = jax.random.key(seed)
    ks = jax.random.split(key, 20)
    nrm = jax.random.normal
    f32 = jnp.float32
    x = nrm(ks[0], (BATCH, SEQ, D_MODEL), f32)
    mem = nrm(ks[1], (BATCH, N_MEM, D_MODEL), f32)
    ln_g = 1.0 + 0.02 * nrm(ks[2], (DEPTH, 3, D_MODEL), f32)
    ln_b = 0.02 * nrm(ks[3], (DEPTH, 3, D_MODEL), f32)
    ffn_w_gate = nrm(ks[4], (DEPTH, 2, D_MODEL, D_FF), f32) * D_MODEL ** -0.5
    ffn_w_up = nrm(ks[5], (DEPTH, 2, D_MODEL, D_FF), f32) * D_MODEL ** -0.5
    ffn_w_down = nrm(ks[6], (DEPTH, 2, D_FF, D_MODEL), f32) * (D_FF ** -0.5 * DEEPNORM_BETA)
    w_kv_mem = nrm(ks[7], (DEPTH, D_MODEL, 2 * D_XA), f32) * D_MODEL ** -0.5
    w_out = nrm(ks[8], (DEPTH, D_MIX, D_MODEL), f32) * (D_MIX ** -0.5 * DEEPNORM_BETA)
    w_in_conv = nrm(ks[9], (N_CONV_LAYERS, D_MODEL, D_IN_CONV), f32) * D_MODEL ** -0.5
    conv_w = nrm(ks[10], (N_CONV_LAYERS, CONV_WIDTH, D_TOK), f32) * CONV_WIDTH ** -0.5
    w_in_mlstm = nrm(ks[11], (N_MLSTM_LAYERS, D_MODEL, D_IN_MLSTM), f32) * D_MODEL ** -0.5
    w_in_mlstm = w_in_mlstm.at[:, :, 4 * D_TOK:4 * D_TOK + 2 * ML_HEADS].multiply(0.1)
    b_i = 0.1 * nrm(ks[12], (N_MLSTM_LAYERS, ML_HEADS), f32)
    b_f = jnp.linspace(3.0, 6.0, ML_HEADS, dtype=f32) + 0.1 * nrm(ks[13], (N_MLSTM_LAYERS, ML_HEADS), f32)
    b_gates = jnp.concatenate([b_i, b_f], -1)
    qk_conv_w = nrm(ks[14], (N_MLSTM_LAYERS, QK_CONV_WIDTH, 2 * D_TOK), f32) * QK_CONV_WIDTH ** -0.5
    head_norm_g = 1.0 + 0.02 * nrm(ks[15], (N_MLSTM_LAYERS, ML_HEADS, ML_HEAD_DIM), f32)
    return {"x": x, "mem": mem, "ln_g": ln_g, "ln_b": ln_b,
            "ffn_w_gate": ffn_w_gate, "ffn_w_up": ffn_w_up, "ffn_w_down": ffn_w_down,
            "w_kv_mem": w_kv_mem, "w_out": w_out,
            "w_in_conv": w_in_conv, "conv_w": conv_w,
            "w_in_mlstm": w_in_mlstm, "b_gates": b_gates,
            "qk_conv_w": qk_conv_w, "head_norm_g": head_norm_g}


def _fwd_reference(x, mem, ln_g, ln_b, ffn_w_gate, ffn_w_up, ffn_w_down, w_kv_mem, w_out,
              w_in_conv, conv_w, w_in_mlstm, b_gates, qk_conv_w, head_norm_g):
    alpha = DEEPNORM_ALPHA
    for l in range(DEPTH):
        x = layer_norm(alpha * x + 0.5 * swiglu(x, ffn_w_gate[l, 0], ffn_w_up[l, 0], ffn_w_down[l, 0]),
                       ln_g[l, 0], ln_b[l, 0])
        mem_kv = mem @ w_kv_mem[l]
        j = l // N_MIXERS
        if l % N_MIXERS == 0:
            u = x @ w_in_conv[j]
            tok = short_conv_mixer(u[..., :3 * D_TOK], conv_w[j])
        else:
            u = x @ w_in_mlstm[j]
            tok = mlstm_mixer(u[..., :4 * D_TOK + 2 * ML_HEADS], b_gates[j], qk_conv_w[j], head_norm_g[j])
        xa = memory_cross_attention(u[..., -D_XA:], mem_kv)
        mix = jnp.concatenate([tok, xa], -1) @ w_out[l]
        x = layer_norm(alpha * x + mix, ln_g[l, 1], ln_b[l, 1])
        x = layer_norm(alpha * x + 0.5 * swiglu(x, ffn_w_gate[l, 1], ffn_w_up[l, 1], ffn_w_down[l, 1]),
                       ln_g[l, 2], ln_b[l, 2])
    return x


import jax as _jax
import jax.numpy as _jnp

TWIN_FORMAT = 'train_step'
FWD_PARAMS = ['x', 'mem', 'ln_g', 'ln_b', 'ffn_w_gate', 'ffn_w_up', 'ffn_w_down', 'w_kv_mem', 'w_out', 'w_in_conv', 'conv_w', 'w_in_mlstm', 'b_gates', 'qk_conv_w', 'head_norm_g']
TWIN_WEIGHTS = ['ln_g', 'ln_b', 'ffn_w_gate', 'ffn_w_up', 'ffn_w_down', 'w_kv_mem', 'w_out', 'w_in_conv', 'conv_w', 'w_in_mlstm', 'b_gates', 'qk_conv_w', 'head_norm_g']
TWIN_DIFF_INPUT = 'x'
TWIN_INPUTS = ['x', 'mem', 'ln_g', 'ln_b', 'ffn_w_gate', 'ffn_w_up', 'ffn_w_down', 'w_kv_mem', 'w_out', 'w_in_conv', 'conv_w', 'w_in_mlstm', 'b_gates', 'qk_conv_w', 'head_norm_g', 'loss_target', 'm_ln_g', 'm_ln_b', 'm_ffn_w_gate', 'm_ffn_w_up', 'm_ffn_w_down', 'm_w_kv_mem', 'm_w_out', 'm_w_in_conv', 'm_conv_w', 'm_w_in_mlstm', 'm_b_gates', 'm_qk_conv_w', 'm_head_norm_g', 'v_ln_g', 'v_ln_b', 'v_ffn_w_gate', 'v_ffn_w_up', 'v_ffn_w_down', 'v_w_kv_mem', 'v_w_out', 'v_w_in_conv', 'v_conv_w', 'v_w_in_mlstm', 'v_b_gates', 'v_qk_conv_w', 'v_head_norm_g']
TWIN_OUTPUTS = ['loss', 'grad_x', 'grad_ln_g', 'grad_ln_b', 'grad_ffn_w_gate', 'grad_ffn_w_up', 'grad_ffn_w_down', 'grad_w_kv_mem', 'grad_w_out', 'grad_w_in_conv', 'grad_conv_w', 'grad_w_in_mlstm', 'grad_b_gates', 'grad_qk_conv_w', 'grad_head_norm_g', 'delta_ln_g', 'delta_ln_b', 'delta_ffn_w_gate', 'delta_ffn_w_up', 'delta_ffn_w_down', 'delta_w_kv_mem', 'delta_w_out', 'delta_w_in_conv', 'delta_conv_w', 'delta_w_in_mlstm', 'delta_b_gates', 'delta_qk_conv_w', 'delta_head_norm_g', 'new_m_ln_g', 'new_m_ln_b', 'new_m_ffn_w_gate', 'new_m_ffn_w_up', 'new_m_ffn_w_down', 'new_m_w_kv_mem', 'new_m_w_out', 'new_m_w_in_conv', 'new_m_conv_w', 'new_m_w_in_mlstm', 'new_m_b_gates', 'new_m_qk_conv_w', 'new_m_head_norm_g', 'new_v_ln_g', 'new_v_ln_b', 'new_v_ffn_w_gate', 'new_v_ffn_w_up', 'new_v_ffn_w_down', 'new_v_w_kv_mem', 'new_v_w_out', 'new_v_w_in_conv', 'new_v_conv_w', 'new_v_w_in_mlstm', 'new_v_b_gates', 'new_v_qk_conv_w', 'new_v_head_norm_g']
TWIN_LEAF_KINDS = {'loss': 'loss', 'grad_x': 'grad_x', 'grad_ln_g': 'grad_w', 'grad_ln_b': 'grad_w', 'grad_ffn_w_gate': 'grad_w', 'grad_ffn_w_up': 'grad_w', 'grad_ffn_w_down': 'grad_w', 'grad_w_kv_mem': 'grad_w', 'grad_w_out': 'grad_w', 'grad_w_in_conv': 'grad_w', 'grad_conv_w': 'grad_w', 'grad_w_in_mlstm': 'grad_w', 'grad_b_gates': 'grad_w', 'grad_qk_conv_w': 'grad_w', 'grad_head_norm_g': 'grad_w', 'delta_ln_g': 'delta_w', 'delta_ln_b': 'delta_w', 'delta_ffn_w_gate': 'delta_w', 'delta_ffn_w_up': 'delta_w', 'delta_ffn_w_down': 'delta_w', 'delta_w_kv_mem': 'delta_w', 'delta_w_out': 'delta_w', 'delta_w_in_conv': 'delta_w', 'delta_conv_w': 'delta_w', 'delta_w_in_mlstm': 'delta_w', 'delta_b_gates': 'delta_w', 'delta_qk_conv_w': 'delta_w', 'delta_head_norm_g': 'delta_w', 'new_m_ln_g': 'new_m', 'new_m_ln_b': 'new_m', 'new_m_ffn_w_gate': 'new_m', 'new_m_ffn_w_up': 'new_m', 'new_m_ffn_w_down': 'new_m', 'new_m_w_kv_mem': 'new_m', 'new_m_w_out': 'new_m', 'new_m_w_in_conv': 'new_m', 'new_m_conv_w': 'new_m', 'new_m_w_in_mlstm': 'new_m', 'new_m_b_gates': 'new_m', 'new_m_qk_conv_w': 'new_m', 'new_m_head_norm_g': 'new_m', 'new_v_ln_g': 'new_v', 'new_v_ln_b': 'new_v', 'new_v_ffn_w_gate': 'new_v', 'new_v_ffn_w_up': 'new_v', 'new_v_ffn_w_down': 'new_v', 'new_v_w_kv_mem': 'new_v', 'new_v_w_out': 'new_v', 'new_v_w_in_conv': 'new_v', 'new_v_conv_w': 'new_v', 'new_v_w_in_mlstm': 'new_v', 'new_v_b_gates': 'new_v', 'new_v_qk_conv_w': 'new_v', 'new_v_head_norm_g': 'new_v'}


def _forward(args):
    return _fwd_reference(*[args[k] for k in FWD_PARAMS])


def _output_shape():
    out = _jax.eval_shape(lambda: _forward(_fwd_setup_inputs(0)))
    return out.shape, out.dtype

N_MICROBATCH = 1
ADAM_LR = 0.001
ADAM_B1 = 0.9
ADAM_B2 = 0.999
ADAM_EPS = 1e-08
ADAM_WD = 0.01
ADAM_STEP = 10
PER_EXAMPLE_BATCH_AXIS = {'x': 0, 'mem': 0, 'loss_target': 0}
SHARED_INPUTS = []
_WEIGHT_DTYPES = {'ln_g': _jnp.float32, 'ln_b': _jnp.float32, 'ffn_w_gate': _jnp.float32, 'ffn_w_up': _jnp.float32, 'ffn_w_down': _jnp.float32, 'w_kv_mem': _jnp.float32, 'w_out': _jnp.float32, 'w_in_conv': _jnp.float32, 'conv_w': _jnp.float32, 'w_in_mlstm': _jnp.float32, 'b_gates': _jnp.float32, 'qk_conv_w': _jnp.float32, 'head_norm_g': _jnp.float32}
MOMENT_SCALE = {'ln_g': 1.309322e+01, 'ln_b': 1.357420e+00, 'ffn_w_gate': 1.200015e-02, 'ffn_w_up': 1.164364e-02, 'ffn_w_down': 3.859393e-02, 'w_kv_mem': 8.610002e-03, 'w_out': 9.190896e-02, 'w_in_conv': 5.811473e-02, 'conv_w': 6.273443e-02, 'w_in_mlstm': 3.211481e-02, 'b_gates': 1.845450e-01, 'qk_conv_w': 2.011896e-02, 'head_norm_g': 4.393306e-02}


def _to_microbatches(a, axis):
    t = _jnp.moveaxis(a, axis, 0)
    t = t.reshape((N_MICROBATCH, t.shape[0] // N_MICROBATCH) + t.shape[1:])
    return _jnp.moveaxis(t, 1, axis + 1)


def setup_inputs(seed: int = 0) -> dict:
    inp = _fwd_setup_inputs(seed)
    key = _jax.random.fold_in(_jax.random.key(seed), 7919)
    shape, _ = _output_shape()
    out = dict(inp)
    out["loss_target"] = _jax.random.normal(_jax.random.fold_in(key, 0), shape, _jnp.float32)
    for i, name in enumerate(TWIN_WEIGHTS):
        w = inp[name].astype(_jnp.float32)
        if MOMENT_SCALE is None:
            s = _jnp.sqrt(_jnp.mean(_jnp.square(w)) + 1e-30)
        else:
            s = MOMENT_SCALE[name]
        km, kv = _jax.random.split(_jax.random.fold_in(key, i + 1))
        out[name] = w
        out["m_" + name] = s * _jax.random.normal(km, w.shape, _jnp.float32)
        out["v_" + name] = (s * s) * _jax.random.uniform(kv, w.shape, _jnp.float32, 0.5, 1.5)
    if N_MICROBATCH > 1:
        for name, axis in PER_EXAMPLE_BATCH_AXIS.items():
            out[name] = _to_microbatches(out[name], axis)
    return {'x': out['x'], 'mem': out['mem'], 'ln_g': out['ln_g'], 'ln_b': out['ln_b'], 'ffn_w_gate': out['ffn_w_gate'], 'ffn_w_up': out['ffn_w_up'], 'ffn_w_down': out['ffn_w_down'], 'w_kv_mem': out['w_kv_mem'], 'w_out': out['w_out'], 'w_in_conv': out['w_in_conv'], 'conv_w': out['conv_w'], 'w_in_mlstm': out['w_in_mlstm'], 'b_gates': out['b_gates'], 'qk_conv_w': out['qk_conv_w'], 'head_norm_g': out['head_norm_g'], 'loss_target': out['loss_target'], 'm_ln_g': out['m_ln_g'], 'm_ln_b': out['m_ln_b'], 'm_ffn_w_gate': out['m_ffn_w_gate'], 'm_ffn_w_up': out['m_ffn_w_up'], 'm_ffn_w_down': out['m_ffn_w_down'], 'm_w_kv_mem': out['m_w_kv_mem'], 'm_w_out': out['m_w_out'], 'm_w_in_conv': out['m_w_in_conv'], 'm_conv_w': out['m_conv_w'], 'm_w_in_mlstm': out['m_w_in_mlstm'], 'm_b_gates': out['m_b_gates'], 'm_qk_conv_w': out['m_qk_conv_w'], 'm_head_norm_g': out['m_head_norm_g'], 'v_ln_g': out['v_ln_g'], 'v_ln_b': out['v_ln_b'], 'v_ffn_w_gate': out['v_ffn_w_gate'], 'v_ffn_w_up': out['v_ffn_w_up'], 'v_ffn_w_down': out['v_ffn_w_down'], 'v_w_kv_mem': out['v_w_kv_mem'], 'v_w_out': out['v_w_out'], 'v_w_in_conv': out['v_w_in_conv'], 'v_conv_w': out['v_conv_w'], 'v_w_in_mlstm': out['v_w_in_mlstm'], 'v_b_gates': out['v_b_gates'], 'v_qk_conv_w': out['v_qk_conv_w'], 'v_head_norm_g': out['v_head_norm_g']}


def _loss(weights, diff, rest, loss_target):
    with _jax.named_scope("forward"):
        args = {**rest, TWIN_DIFF_INPUT: diff, **{k: w.astype(_WEIGHT_DTYPES[k]) for k, w in weights.items()}}
        y = _forward(args)
    with _jax.named_scope("loss_head"):
        err = _jnp.square(y.astype(_jnp.float32) - loss_target)
        return 0.5 * _jnp.sum(_jnp.mean(err, axis=-1)) if err.ndim else 0.5 * err


def _adamw(w, g, m, v):
    m = ADAM_B1 * m + (1.0 - ADAM_B1) * g
    v = ADAM_B2 * v + (1.0 - ADAM_B2) * _jnp.square(g)
    m_hat = m / (1.0 - ADAM_B1 ** ADAM_STEP)
    v_hat = v / (1.0 - ADAM_B2 ** ADAM_STEP)
    delta = -ADAM_LR * (m_hat / (_jnp.sqrt(v_hat) + ADAM_EPS) + ADAM_WD * w)
    return delta, m, v


def reference(x, mem, ln_g, ln_b, ffn_w_gate, ffn_w_up, ffn_w_down, w_kv_mem, w_out, w_in_conv, conv_w, w_in_mlstm, b_gates, qk_conv_w, head_norm_g, loss_target, m_ln_g, m_ln_b, m_ffn_w_gate, m_ffn_w_up, m_ffn_w_down, m_w_kv_mem, m_w_out, m_w_in_conv, m_conv_w, m_w_in_mlstm, m_b_gates, m_qk_conv_w, m_head_norm_g, v_ln_g, v_ln_b, v_ffn_w_gate, v_ffn_w_up, v_ffn_w_down, v_w_kv_mem, v_w_out, v_w_in_conv, v_conv_w, v_w_in_mlstm, v_b_gates, v_qk_conv_w, v_head_norm_g):
    given = dict(x=x, mem=mem, ln_g=ln_g, ln_b=ln_b, ffn_w_gate=ffn_w_gate, ffn_w_up=ffn_w_up, ffn_w_down=ffn_w_down, w_kv_mem=w_kv_mem, w_out=w_out, w_in_conv=w_in_conv, conv_w=conv_w, w_in_mlstm=w_in_mlstm, b_gates=b_gates, qk_conv_w=qk_conv_w, head_norm_g=head_norm_g, loss_target=loss_target, m_ln_g=m_ln_g, m_ln_b=m_ln_b, m_ffn_w_gate=m_ffn_w_gate, m_ffn_w_up=m_ffn_w_up, m_ffn_w_down=m_ffn_w_down, m_w_kv_mem=m_w_kv_mem, m_w_out=m_w_out, m_w_in_conv=m_w_in_conv, m_conv_w=m_conv_w, m_w_in_mlstm=m_w_in_mlstm, m_b_gates=m_b_gates, m_qk_conv_w=m_qk_conv_w, m_head_norm_g=m_head_norm_g, v_ln_g=v_ln_g, v_ln_b=v_ln_b, v_ffn_w_gate=v_ffn_w_gate, v_ffn_w_up=v_ffn_w_up, v_ffn_w_down=v_ffn_w_down, v_w_kv_mem=v_w_kv_mem, v_w_out=v_w_out, v_w_in_conv=v_w_in_conv, v_conv_w=v_conv_w, v_w_in_mlstm=v_w_in_mlstm, v_b_gates=v_b_gates, v_qk_conv_w=v_qk_conv_w, v_head_norm_g=v_head_norm_g)
    weights = {n: given[n] for n in TWIN_WEIGHTS}
    shared = {n: given[n] for n in SHARED_INPUTS}
    per_example = {n: given[n] for n in ['x', 'mem']}
    grad_fn = _jax.value_and_grad(_loss, argnums=(0, 1))

    def one_microbatch(ex, loss_target):
        ex = dict(ex)
        diff = ex.pop(TWIN_DIFF_INPUT)
        return grad_fn(weights, diff, {**shared, **ex}, loss_target)

    if N_MICROBATCH == 1:
        loss, (grad_w, grad_x) = one_microbatch(per_example, given["loss_target"])
    else:
        def body(carry, xs):
            loss_sum, grad_sum = carry
            l_k, (gw_k, gx_k) = one_microbatch(xs[0], xs[1])
            with _jax.named_scope("update"):
                return (loss_sum + l_k, _jax.tree.map(_jnp.add, grad_sum, gw_k)), gx_k

        init = (_jnp.zeros((), _jnp.float32), _jax.tree.map(_jnp.zeros_like, weights))
        (loss, grad_w), grad_x = _jax.lax.scan(body, init, (per_example, given["loss_target"]))
    with _jax.named_scope("update"):
        delta_w, new_m, new_v = {}, {}, {}
        for n in TWIN_WEIGHTS:
            delta_w[n], new_m[n], new_v[n] = _adamw(weights[n], grad_w[n], given["m_" + n], given["v_" + n])
    return (loss, grad_x, *[grad_w[n] for n in TWIN_WEIGHTS], *[delta_w[n] for n in TWIN_WEIGHTS],
            *[new_m[n] for n in TWIN_WEIGHTS], *[new_v[n] for n in TWIN_WEIGHTS])
```

```python
import functools
import math

import jax
import jax.numpy as jnp
from jax import lax
from jax.experimental import pallas as pl
from jax.experimental.pallas import tpu as pltpu

F32 = jnp.float32
BF16 = jnp.bfloat16
SDS = jax.ShapeDtypeStruct

D_MODEL = 1024
DEPTH = 2
N_MEM = 256
XA_HEADS = 4
XA_HEAD_DIM = 64
D_XA = 256
D_TOK = 768
ML_HEADS = 4
ML_HEAD_DIM = 192
ML_CHUNK = 64
D_FF = 2816
LN_EPS = 1e-5
ALPHA = (2.0 * DEPTH) ** 0.25
N_CHIPS = 4
N_DEV = 8
FF_SHARD = D_FF // N_CHIPS
GROUP = 256
NEG = -1e30

ADAM_LR = 0.001
ADAM_B1 = 0.9
ADAM_B2 = 0.999
ADAM_EPS = 1e-08
ADAM_WD = 0.01
ADAM_STEP = 10

VMEM_LIMIT = 56 * 1024 * 1024

NN = ((1,), (0,))
NT = ((1,), (1,))
TN = ((0,), (0,))
MESH = pl.DeviceIdType.MESH


def _dot(a, b, dims):
    return lax.dot_general(a, b, (dims, ((), ())), preferred_element_type=F32)


def _bdot(a, b, ca, cb):
    return lax.dot_general(a, b, (((ca,), (cb,)), ((0,), (0,))), preferred_element_type=F32)


def _sigmoid(x):
    return 1.0 / (1.0 + jnp.exp(-x))


def _params(sem, vmem=VMEM_LIMIT):
    return pltpu.CompilerParams(dimension_semantics=sem, vmem_limit_bytes=vmem)


def _tile(n, want):
    t = min(n, want)
    assert n % t == 0, (n, t)
    return t


def ffn_up(xb, wg, wu):
    S, K = xb.shape
    G, _, N = wg.shape
    ts = _tile(S, 1024)

    def body(x_ref, wg_ref, wu_ref, g_ref, u_ref, h_ref):
        x = x_ref[...]
        g = _dot(x, wg_ref[0], NN)
        u = _dot(x, wu_ref[0], NN)
        g_ref[0] = g
        u_ref[0] = u
        h_ref[0] = (g * _sigmoid(g) * u).astype(BF16)

    wspec = pl.BlockSpec((1, K, N), lambda g, s: (g, 0, 0))
    ospec = pl.BlockSpec((1, ts, N), lambda g, s: (g, s, 0))
    return pl.pallas_call(
        body, name="ffn_up", grid=(G, S // ts),
        in_specs=[pl.BlockSpec((ts, K), lambda g, s: (s, 0)), wspec, wspec],
        out_specs=[ospec, ospec, ospec],
        out_shape=[SDS((G, S, N), F32), SDS((G, S, N), F32), SDS((G, S, N), BF16)],
        compiler_params=_params(("parallel", "parallel")),
    )(xb, wg, wu)


def proj(xb, w, name):
    S, K = xb.shape
    G, _, N = w.shape
    ts = _tile(S, 1024)

    def body(x_ref, w_ref, y_ref):
        y_ref[0] = _dot(x_ref[...], w_ref[0], NN)

    return pl.pallas_call(
        body, name=name, grid=(G, S // ts),
        in_specs=[pl.BlockSpec((ts, K), lambda g, s: (s, 0)), pl.BlockSpec((1, K, N), lambda g, s: (g, 0, 0))],
        out_specs=pl.BlockSpec((1, ts, N), lambda g, s: (g, s, 0)),
        out_shape=SDS((G, S, N), F32),
        compiler_params=_params(("parallel", "parallel")),
    )(xb, w)


def contract_ln(a, w, xres, gamma, beta, scale, name):
    G, S, Kg = a.shape
    N = w.shape[2]
    ts = _tile(S, 512)

    def body(a_ref, w_ref, x_ref, g_ref, b_ref, z_ref, xn_ref, xb_ref, acc):
        g = pl.program_id(1)

        @pl.when(g == 0)
        def _():
            acc[...] = jnp.zeros_like(acc)

        acc[...] += _dot(a_ref[0], w_ref[0], NN)

        @pl.when(g == G - 1)
        def _():
            z = ALPHA * x_ref[...] + scale * acc[...]
            mu = jnp.mean(z, axis=-1, keepdims=True)
            zc = z - mu
            var = jnp.mean(zc * zc, axis=-1, keepdims=True)
            xn = zc * lax.rsqrt(var + LN_EPS) * g_ref[...] + b_ref[...]
            z_ref[...] = z
            xn_ref[...] = xn
            xb_ref[...] = xn.astype(BF16)

    row = pl.BlockSpec((ts, N), lambda s, g: (s, 0))
    vec = pl.BlockSpec((1, N), lambda s, g: (0, 0))
    return pl.pallas_call(
        body, name=name, grid=(S // ts, G),
        in_specs=[pl.BlockSpec((1, ts, Kg), lambda s, g: (g, s, 0)), pl.BlockSpec((1, Kg, N), lambda s, g: (g, 0, 0)),
                  row, vec, vec],
        out_specs=[row, row, row],
        out_shape=[SDS((S, N), F32), SDS((S, N), F32), SDS((S, N), BF16)],
        scratch_shapes=[pltpu.VMEM((ts, N), F32)],
        compiler_params=_params(("parallel", "arbitrary")),
    )(a, w, xres, gamma, beta)


def ln_bwd(dx, z, gamma, out_scale, name):
    S, N = dx.shape
    ts = _tile(S, 512)

    def body(dx_ref, z_ref, g_ref, dz_ref, dzb_ref, dg_ref, db_ref):
        @pl.when(pl.program_id(0) == 0)
        def _():
            dg_ref[...] = jnp.zeros_like(dg_ref)
            db_ref[...] = jnp.zeros_like(db_ref)

        z = z_ref[...]
        mu = jnp.mean(z, axis=-1, keepdims=True)
        zc = z - mu
        var = jnp.mean(zc * zc, axis=-1, keepdims=True)
        rstd = lax.rsqrt(var + LN_EPS)
        xhat = zc * rstd
        dxv = dx_ref[...]
        dg_ref[...] += jnp.sum(dxv * xhat, axis=0, keepdims=True)
        db_ref[...] += jnp.sum(dxv, axis=0, keepdims=True)
        dxh = dxv * g_ref[...]
        m1 = jnp.mean(dxh, axis=-1, keepdims=True)
        m2 = jnp.mean(dxh * xhat, axis=-1, keepdims=True)
        dz = rstd * (dxh - m1 - xhat * m2)
        dz_ref[...] = dz
        dzb_ref[...] = (out_scale * dz).astype(BF16)

    row = pl.BlockSpec((ts, N), lambda s: (s, 0))
    vec = pl.BlockSpec((1, N), lambda s: (0, 0))
    return pl.pallas_call(
        body, name=name, grid=(S // ts,),
        in_specs=[row, row, vec],
        out_specs=[row, row, vec, vec],
        out_shape=[SDS((S, N), F32), SDS((S, N), BF16), SDS((1, N), F32), SDS((1, N), F32)],
        compiler_params=_params(("arbitrary",)),
    )(dx, z, gamma)


def ffn_bwd_dh(dyb, wd, g1, u1):
    S, K = dyb.shape
    G, N, _ = wd.shape
    ts = _tile(S, 1024)

    def body(dy_ref, w_ref, g_ref, u_ref, dg_ref, du_ref):
        dh = _dot(dy_ref[...], w_ref[0], NT)
        g = g_ref[0]
        sig = _sigmoid(g)
        silu = g * sig
        dg_ref[0] = (dh * u_ref[0] * (sig * (1.0 + g * (1.0 - sig)))).astype(BF16)
        du_ref[0] = (dh * silu).astype(BF16)

    gspec = pl.BlockSpec((1, ts, N), lambda g, s: (g, s, 0))
    return pl.pallas_call(
        body, name="ffn_bwd_dh", grid=(G, S // ts),
        in_specs=[pl.BlockSpec((ts, K), lambda g, s: (s, 0)), pl.BlockSpec((1, N, K), lambda g, s: (g, 0, 0)), gspec, gspec],
        out_specs=[gspec, gspec],
        out_shape=[SDS((G, S, N), BF16), SDS((G, S, N), BF16)],
        compiler_params=_params(("parallel", "parallel")),
    )(dyb, wd, g1, u1)


def proj_t(dyb, w, name):
    S, N = dyb.shape
    G, Kg, _ = w.shape
    ts = _tile(S, 1024)

    def body(dy_ref, w_ref, da_ref):
        da_ref[0] = _dot(dy_ref[...], w_ref[0], NT)

    return pl.pallas_call(
        body, name=name, grid=(G, S // ts),
        in_specs=[pl.BlockSpec((ts, N), lambda g, s: (s, 0)), pl.BlockSpec((1, Kg, N), lambda g, s: (g, 0, 0))],
        out_specs=pl.BlockSpec((1, ts, Kg), lambda g, s: (g, s, 0)),
        out_shape=SDS((G, S, Kg), F32),
        compiler_params=_params(("parallel", "parallel")),
    )(dyb, w)


def contract_t(pairs, res, name):
    n = len(pairs)
    G, S, Ng = pairs[0][0].shape
    K = pairs[0][1].shape[1]
    ts = _tile(S, 512)

    def body(*refs):
        ins, r_ref, o_ref, acc = refs[:2 * n], refs[2 * n], refs[2 * n + 1], refs[2 * n + 2]
        g = pl.program_id(1)

        @pl.when(g == 0)
        def _():
            acc[...] = ALPHA * r_ref[...]

        for p in range(n):
            acc[...] += _dot(ins[2 * p][0], ins[2 * p + 1][0], NT)

        @pl.when(g == G - 1)
        def _():
            o_ref[...] = acc[...]

    in_specs, args = [], []
    for da, w in pairs:
        in_specs += [pl.BlockSpec((1, ts, Ng), lambda s, g: (g, s, 0)), pl.BlockSpec((1, K, Ng), lambda s, g: (g, 0, 0))]
        args += [da, w]
    row = pl.BlockSpec((ts, K), lambda s, g: (s, 0))
    return pl.pallas_call(
        body, name=name, grid=(S // ts, G),
        in_specs=in_specs + [row], out_specs=row,
        out_shape=SDS((S, K), F32),
        scratch_shapes=[pltpu.VMEM((ts, K), F32)],
        compiler_params=_params(("parallel", "arbitrary")),
    )(*args, res)


def wgrad(a, b, out_dtype, name):
    ga, gb = a.ndim == 3, b.ndim == 3
    G = a.shape[0] if ga else b.shape[0]
    S, K = a.shape[-2:]
    N = b.shape[-1]
    ts = _tile(S, 1024)
    ns = S // ts

    def body(a_ref, b_ref, o_ref, acc):
        s = pl.program_id(1)

        @pl.when(s == 0)
        def _():
            acc[...] = jnp.zeros_like(acc)

        av = a_ref[0] if ga else a_ref[...]
        bv = b_ref[0] if gb else b_ref[...]
        acc[...] += _dot(av, bv, TN)

        @pl.when(s == ns - 1)
        def _():
            o_ref[0] = acc[...].astype(out_dtype)

    aspec = pl.BlockSpec((1, ts, K), lambda g, s: (g, s, 0)) if ga else pl.BlockSpec((ts, K), lambda g, s: (s, 0))
    bspec = pl.BlockSpec((1, ts, N), lambda g, s: (g, s, 0)) if gb else pl.BlockSpec((ts, N), lambda g, s: (s, 0))
    return pl.pallas_call(
        body, name=name, grid=(G, ns),
        in_specs=[aspec, bspec],
        out_specs=pl.BlockSpec((1, K, N), lambda g, s: (g, 0, 0)),
        out_shape=SDS((G, K, N), out_dtype),
        scratch_shapes=[pltpu.VMEM((K, N), F32)],
        compiler_params=_params(("parallel", "arbitrary")),
    )(a, b)


def loss_grad(xn, tgt):
    S, N = xn.shape
    ts = _tile(S, 512)

    def body(x_ref, t_ref, l_ref, dx_ref):
        @pl.when(pl.program_id(0) == 0)
        def _():
            l_ref[...] = jnp.zeros_like(l_ref)

        e = x_ref[...] - t_ref[...]
        dx_ref[...] = e * (1.0 / N)
        l_ref[...] += 0.5 * jnp.sum(jnp.mean(e * e, axis=-1, keepdims=True), axis=0, keepdims=True)

    row = pl.BlockSpec((ts, N), lambda s: (s, 0))
    return pl.pallas_call(
        body, name="loss_grad", grid=(S // ts,),
        in_specs=[row, row],
        out_specs=[pl.BlockSpec((1, 1), lambda s: (0, 0)), row],
        out_shape=[SDS((1, 1), F32), SDS((S, N), F32)],
        compiler_params=_params(("arbitrary",)),
    )(xn, tgt)


def _shift_down(x, k):
    if k == 0:
        return x
    rows = lax.broadcasted_iota(jnp.int32, x.shape, 0)
    return jnp.where(rows >= k, pltpu.roll(x, k, 0), 0.0)


def _shift_up(x, k):
    if k == 0:
        return x
    n = x.shape[0]
    rows = lax.broadcasted_iota(jnp.int32, x.shape, 0)
    return jnp.where(rows < n - k, pltpu.roll(x, n - k, 0), 0.0)


LANES = 128


def conv_mixer_fwd(u, cw):
    _, S, _ = u.shape
    nh = GROUP // LANES

    def body(b_ref, c_ref, x_ref, w_ref, o_ref):
        p = c_ref[0] * x_ref[0]
        w = w_ref[0]
        conv = w[2:3] * p + w[1:2] * _shift_down(p, 1) + w[0:1] * _shift_down(p, 2)
        o_ref[0] = (b_ref[0] * conv).astype(BF16)

    def uspec(off):
        return pl.BlockSpec((1, S, LANES), lambda g, h: (g + off, 0, h))

    return pl.pallas_call(
        body, name="conv_mixer_fwd", grid=(3, nh),
        in_specs=[uspec(0), uspec(3), uspec(6), pl.BlockSpec((1, 8, LANES), lambda g, h: (g, 0, h))],
        out_specs=pl.BlockSpec((1, S, LANES), lambda g, h: (g, 0, h)),
        out_shape=SDS((3, S, GROUP), BF16),
        compiler_params=_params(("parallel", "parallel")),
    )(u, u, u, cw)


def conv_mixer_bwd(u, cw, dm):
    _, S, _ = u.shape
    nh = GROUP // LANES

    def body(b_ref, c_ref, x_ref, w_ref, d_ref, db_ref, dc_ref, dx_ref, dw_ref):
        cg, xi = c_ref[0], x_ref[0]
        p = cg * xi
        p1, p2 = _shift_down(p, 1), _shift_down(p, 2)
        w = w_ref[0]
        conv = w[2:3] * p + w[1:2] * p1 + w[0:1] * p2
        dt = d_ref[0]
        db_ref[0] = (dt * conv).astype(BF16)
        dcv = dt * b_ref[0]
        dp = w[2:3] * dcv + w[1:2] * _shift_up(dcv, 1) + w[0:1] * _shift_up(dcv, 2)
        dc_ref[0] = (dp * xi).astype(BF16)
        dx_ref[0] = (dp * cg).astype(BF16)
        dw = jnp.concatenate([jnp.sum(dcv * p2, axis=0, keepdims=True), jnp.sum(dcv * p1, axis=0, keepdims=True),
                              jnp.sum(dcv * p, axis=0, keepdims=True), jnp.zeros((5, LANES), F32)], axis=0)
        dw_ref[0] = dw

    def uspec(off):
        return pl.BlockSpec((1, S, LANES), lambda g, h: (g + off, 0, h))

    ospec = pl.BlockSpec((1, S, LANES), lambda g, h: (g, 0, h))
    wspec = pl.BlockSpec((1, 8, LANES), lambda g, h: (g, 0, h))
    return pl.pallas_call(
        body, name="conv_mixer_bwd", grid=(3, nh),
        in_specs=[uspec(0), uspec(3), uspec(6), wspec, ospec],
        out_specs=[ospec, ospec, ospec, wspec],
        out_shape=[SDS((3, S, GROUP), BF16)] * 3 + [SDS((3, 8, GROUP), F32)],
        compiler_params=_params(("parallel", "parallel")),
    )(u, u, u, cw, dm)


def qk_conv_fwd(u, qw):
    _, S, _ = u.shape
    nh = GROUP // LANES

    def body(u_ref, w_ref, o_ref):
        x = u_ref[0]
        w = w_ref[0]
        pre = w[3:4] * x + w[2:3] * _shift_down(x, 1) + w[1:2] * _shift_down(x, 2) + w[0:1] * _shift_down(x, 3)
        o_ref[0] = pre * _sigmoid(pre)

    spec = pl.BlockSpec((1, S, LANES), lambda g, h: (g, 0, h))
    return pl.pallas_call(
        body, name="qk_conv_fwd", grid=(8, nh),
        in_specs=[spec, pl.BlockSpec((1, 8, LANES), lambda g, h: (g, 0, h))],
        out_specs=spec,
        out_shape=SDS((8, S, GROUP), F32),
        compiler_params=_params(("parallel", "parallel")),
    )(u, qw)


def qk_conv_bwd(u, qw, dqk):
    _, S, _ = u.shape
    nh = GROUP // LANES

    def body(u_ref, w_ref, d_ref, du_ref, dw_ref):
        x = u_ref[0]
        w = w_ref[0]
        x1, x2, x3 = _shift_down(x, 1), _shift_down(x, 2), _shift_down(x, 3)
        pre = w[3:4] * x + w[2:3] * x1 + w[1:2] * x2 + w[0:1] * x3
        sig = _sigmoid(pre)
        dpre = d_ref[0] * (sig * (1.0 + pre * (1.0 - sig)))
        du = w[3:4] * dpre + w[2:3] * _shift_up(dpre, 1) + w[1:2] * _shift_up(dpre, 2) + w[0:1] * _shift_up(dpre, 3)
        du_ref[0] = du.astype(BF16)
        dw = jnp.concatenate([jnp.sum(dpre * x3, axis=0, keepdims=True), jnp.sum(dpre * x2, axis=0, keepdims=True),
                              jnp.sum(dpre * x1, axis=0, keepdims=True), jnp.sum(dpre * x, axis=0, keepdims=True),
                              jnp.zeros((4, LANES), F32)], axis=0)
        dw_ref[0] = dw

    spec = pl.BlockSpec((1, S, LANES), lambda g, h: (g, 0, h))
    wspec = pl.BlockSpec((1, 8, LANES), lambda g, h: (g, 0, h))
    return pl.pallas_call(
        body, name="qk_conv_bwd", grid=(8, nh),
        in_specs=[spec, wspec, spec],
        out_specs=[spec, wspec],
        out_shape=[SDS((8, S, GROUP), BF16), SDS((8, 8, GROUP), F32)],
        compiler_params=_params(("parallel", "parallel")),
    )(u, qw, dqk)


def _head_masks():
    lane = lax.broadcasted_iota(jnp.int32, (1, D_XA), 1)
    return [(lane >= h * XA_HEAD_DIM) & (lane < (h + 1) * XA_HEAD_DIM) for h in range(XA_HEADS)]


def xattn_fwd(u, qg, kv):
    _, S, _ = u.shape
    ts = _tile(S, 512)
    scale = XA_HEAD_DIM ** -0.5

    def body(q_ref, kv_ref, o_ref):
        q = q_ref[0]
        k = kv_ref[0].astype(BF16)
        v = kv_ref[1]
        o = jnp.zeros((ts, D_XA), F32)
        for m in _head_masks():
            s = _dot(jnp.where(m, q, 0.0).astype(BF16), k, NT) * scale
            s = s - jnp.max(s, axis=-1, keepdims=True)
            e = jnp.exp(s)
            p = e / jnp.sum(e, axis=-1, keepdims=True)
            o = o + _dot(p.astype(BF16), jnp.where(m, v, 0.0).astype(BF16), NN)
        o_ref[0] = o.astype(BF16)

    return pl.pallas_call(
        body, name="xattn_fwd", grid=(S // ts,),
        in_specs=[pl.BlockSpec((1, ts, GROUP), lambda s: (qg, s, 0)), pl.BlockSpec((2, N_MEM, GROUP), lambda s: (0, 0, 0))],
        out_specs=pl.BlockSpec((1, ts, GROUP), lambda s: (0, s, 0)),
        out_shape=SDS((1, S, GROUP), BF16),
        compiler_params=_params(("parallel",)),
    )(u, kv)


def xattn_bwd(u, qg, kv, dm, dg):
    _, S, _ = u.shape
    ts = _tile(S, 512)
    scale = XA_HEAD_DIM ** -0.5

    def body(q_ref, kv_ref, do_ref, dq_ref, dkv_ref):
        @pl.when(pl.program_id(0) == 0)
        def _():
            dkv_ref[...] = jnp.zeros_like(dkv_ref)

        q = q_ref[0]
        k = kv_ref[0]
        v = kv_ref[1]
        kb = k.astype(BF16)
        do = do_ref[0]
        dq = jnp.zeros((ts, D_XA), F32)
        dk = jnp.zeros((N_MEM, D_XA), F32)
        dv = jnp.zeros((N_MEM, D_XA), F32)
        for m in _head_masks():
            qm = jnp.where(m, q, 0.0).astype(BF16)
            s = _dot(qm, kb, NT) * scale
            s = s - jnp.max(s, axis=-1, keepdims=True)
            e = jnp.exp(s)
            p = e / jnp.sum(e, axis=-1, keepdims=True)
            dom = jnp.where(m, do, 0.0).astype(BF16)
            dp = _dot(dom, jnp.where(m, v, 0.0).astype(BF16), NT)
            ds = (p * (dp - jnp.sum(dp * p, axis=-1, keepdims=True)) * scale).astype(BF16)
            dq = dq + _dot(ds, jnp.where(m, k, 0.0).astype(BF16), NN)
            dk = dk + _dot(ds, qm, TN)
            dv = dv + _dot(p.astype(BF16), dom, TN)
        dq_ref[0] = dq.astype(BF16)
        dkv_ref[0] += dk
        dkv_ref[1] += dv

    return pl.pallas_call(
        body, name="xattn_bwd", grid=(S // ts,),
        in_specs=[pl.BlockSpec((1, ts, GROUP), lambda s: (qg, s, 0)), pl.BlockSpec((2, N_MEM, GROUP), lambda s: (0, 0, 0)),
                  pl.BlockSpec((1, ts, GROUP), lambda s: (dg, s, 0))],
        out_specs=[pl.BlockSpec((1, ts, GROUP), lambda s: (0, s, 0)), pl.BlockSpec((2, N_MEM, GROUP), lambda s: (0, 0, 0))],
        out_shape=[SDS((1, S, GROUP), BF16), SDS((2, N_MEM, GROUP), F32)],
        compiler_params=_params(("arbitrary",)),
    )(u, kv, dm)


ML_BLOCK_CHUNKS = 4
H4 = ML_HEADS
L = ML_CHUNK
NLANE = ML_HEAD_DIM


def _chunk_consts():
    r = lax.broadcasted_iota(jnp.int32, (1, L, L), 1)
    c = lax.broadcasted_iota(jnp.int32, (1, L, L), 2)
    return r >= c, r <= c, r == c


def _gate_cols(gb):
    lane = lax.broadcasted_iota(jnp.int32, gb.shape, 1)
    li = jnp.stack([jnp.sum(jnp.where(lane == h, gb, 0.0), axis=1, keepdims=True) for h in range(H4)])
    gf = jnp.stack([jnp.sum(jnp.where(lane == H4 + h, gb, 0.0), axis=1, keepdims=True) for h in range(H4)])
    return li, gf


def _log_sigmoid(x):
    return jnp.minimum(x, 0.0) - jnp.log(1.0 + jnp.exp(-jnp.abs(x)))


def _chunk_forward(q, k, v_aug, li_col, lf_col, c_prev, m_prev):
    tri, tri_t, eye = _chunk_consts()
    lf_row = jnp.sum(jnp.where(eye, lf_col, 0.0), axis=1, keepdims=True)
    li_row = jnp.sum(jnp.where(eye, li_col, 0.0), axis=1, keepdims=True)
    bcum_col = jnp.sum(jnp.where(tri, lf_row, 0.0), axis=2, keepdims=True)
    bcum_row = jnp.sum(jnp.where(tri_t, lf_col, 0.0), axis=1, keepdims=True)
    log_d = jnp.where(tri, bcum_col - bcum_row + li_row, NEG)
    log_inter = bcum_col + m_prev
    m_t = jnp.maximum(log_inter, jnp.max(log_d, axis=2, keepdims=True))
    w_intra = jnp.exp(log_d - m_t)
    w_inter = jnp.exp(log_inter - m_t)
    sc = _bdot(q, k, 2, 2) * w_intra
    qc = _bdot(q, c_prev, 2, 1)
    num = _bdot(sc, v_aug, 2, 1) + w_inter * qc
    lane = lax.broadcasted_iota(jnp.int32, num.shape, 2)
    den = jnp.sum(jnp.where(lane == NLANE, num, 0.0), axis=2, keepdims=True)
    e_m = jnp.exp(-m_t)
    b_last = jnp.sum(lf_row, axis=2, keepdims=True)
    log_w = b_last - bcum_col + li_col
    m_new = jnp.maximum(b_last + m_prev, jnp.max(log_w, axis=1, keepdims=True))
    w_k = jnp.exp(log_w - m_new)
    decay = jnp.exp(b_last + m_prev - m_new)
    return dict(w_intra=w_intra, w_inter=w_inter, sc=sc, qc=qc, num=num, den=den, e_m=e_m, lane=lane,
                w_k=w_k, decay=decay, m_new=m_new)


def mlstm_fwd(qk, u, bg):
    _, S, _ = qk.shape
    nc = S // L
    cb = min(ML_BLOCK_CHUNKS, nc)
    rows = cb * L
    kscale = ML_HEAD_DIM ** -0.5

    def body(qk_ref, v_ref, g_ref, bg_ref, h_ref, cst_ref, mst_ref, c_sc, m_sc):
        @pl.when(pl.program_id(0) == 0)
        def _():
            c_sc[...] = jnp.zeros_like(c_sc)
            m_sc[...] = jnp.zeros_like(m_sc)

        for c in range(cb):
            sl = pl.ds(c * L, L)
            q = qk_ref[0:H4, sl, :]
            k = qk_ref[H4:2 * H4, sl, :] * kscale
            v = v_ref[:, sl, :]
            lane = lax.broadcasted_iota(jnp.int32, v.shape, 2)
            v_aug = jnp.where(lane == NLANE, 1.0, v)
            li_col, gf = _gate_cols(g_ref[0, sl, :] + bg_ref[...])
            lf_col = _log_sigmoid(gf)
            c_prev = c_sc[...]
            m_prev = m_sc[...]
            f = _chunk_forward(q, k, v_aug, li_col, lf_col, c_prev, m_prev)
            r = 1.0 / jnp.maximum(jnp.abs(f["den"]), f["e_m"])
            h_ref[:, sl, :] = jnp.where(lane < NLANE, f["num"] * r, 0.0)
            cst_ref[c] = c_prev
            mst_ref[c] = jnp.broadcast_to(m_prev, (H4, 1, LANES))
            c_sc[...] = f["decay"] * c_prev + _bdot(k * f["w_k"], v_aug, 1, 1)
            m_sc[...] = f["m_new"]

    def hspec(blk):
        return pl.BlockSpec((H4, rows, GROUP), lambda i: (blk, i, 0))

    return pl.pallas_call(
        body, name="mlstm_fwd", grid=(nc // cb,),
        in_specs=[pl.BlockSpec((2 * H4, rows, GROUP), lambda i: (0, i, 0)), hspec(2),
                  pl.BlockSpec((1, rows, GROUP), lambda i: (17, i, 0)), pl.BlockSpec((1, GROUP), lambda i: (0, 0))],
        out_specs=[hspec(0), pl.BlockSpec((cb, H4, GROUP, GROUP), lambda i: (i, 0, 0, 0)),
                   pl.BlockSpec((cb, H4, 1, LANES), lambda i: (i, 0, 0, 0))],
        out_shape=[SDS((H4, S, GROUP), F32), SDS((nc, H4, GROUP, GROUP), F32), SDS((nc, H4, 1, LANES), F32)],
        scratch_shapes=[pltpu.VMEM((H4, GROUP, GROUP), F32), pltpu.VMEM((H4, 1, 1), F32)],
        compiler_params=_params(("arbitrary",)),
    )(qk, u, u, bg)


def mlstm_bwd(qk, u, bg, cst, mst, dh):
    _, S, _ = qk.shape
    nc = S // L
    cb = min(ML_BLOCK_CHUNKS, nc)
    rows = cb * L
    nb = nc // cb
    kscale = ML_HEAD_DIM ** -0.5

    def body(qk_ref, v_ref, g_ref, bg_ref, cst_ref, mst_ref, dh_ref, dqk_ref, dv_ref, dg_ref, dbg_ref, dc_sc):
        @pl.when(pl.program_id(0) == 0)
        def _():
            dc_sc[...] = jnp.zeros_like(dc_sc)
            dbg_ref[...] = jnp.zeros_like(dbg_ref)

        tri, tri_t, eye = _chunk_consts()
        for c in reversed(range(cb)):
            sl = pl.ds(c * L, L)
            q = qk_ref[0:H4, sl, :]
            k = qk_ref[H4:2 * H4, sl, :] * kscale
            v = v_ref[:, sl, :]
            lane = lax.broadcasted_iota(jnp.int32, v.shape, 2)
            v_aug = jnp.where(lane == NLANE, 1.0, v)
            li_col, gf = _gate_cols(g_ref[0, sl, :] + bg_ref[...])
            lf_col = _log_sigmoid(gf)
            c_prev = cst_ref[c]
            m_prev = mst_ref[c][:, :, 0:1]
            f = _chunk_forward(q, k, v_aug, li_col, lf_col, c_prev, m_prev)
            w_intra, w_inter, sc, num, den, e_m = f["w_intra"], f["w_inter"], f["sc"], f["num"], f["den"], f["e_m"]
            absd = jnp.abs(den)
            r = 1.0 / jnp.maximum(absd, e_m)
            dhv = dh_ref[:, sl, :]
            s1 = jnp.sum(jnp.where(lane < NLANE, dhv * num, 0.0), axis=2, keepdims=True)
            dden = jnp.where(absd > e_m, -s1 * r * r * jnp.sign(den), 0.0)
            dnum = jnp.where(lane == NLANE, dden, jnp.where(lane < NLANE, dhv * r, 0.0))
            dsc = _bdot(dnum, v_aug, 2, 2)
            dv = _bdot(sc, dnum, 1, 1)
            gmat = dsc * sc
            dqk = dsc * w_intra
            dq = _bdot(dqk, k, 2, 1) + w_inter * _bdot(dnum, c_prev, 2, 2)
            dk = _bdot(dqk, q, 1, 1)
            dc_prev = _bdot(q * w_inter, dnum, 1, 1)
            dlog_inter = jnp.sum(dnum * f["qc"], axis=2, keepdims=True) * w_inter
            dbcum_col = dlog_inter + jnp.sum(gmat, axis=2, keepdims=True)
            g_row = jnp.sum(gmat, axis=1, keepdims=True)
            dcn = dc_sc[...]
            w_k, decay = f["w_k"], f["decay"]
            kw = k * w_k
            dc_prev = dc_prev + decay * dcn
            db_last = jnp.sum(jnp.sum(dcn * c_prev, axis=2, keepdims=True), axis=1, keepdims=True) * decay
            dkw = _bdot(v_aug, dcn, 2, 2)
            dv = dv + _bdot(kw, dcn, 2, 1)
            dk = dk + dkw * w_k
            dlogw = jnp.sum(dkw * k, axis=2, keepdims=True) * w_k
            db_last = db_last + jnp.sum(dlogw, axis=1, keepdims=True)
            dbcum_col = dbcum_col - dlogw
            rowi = lax.broadcasted_iota(jnp.int32, (1, L, 1), 1)
            dbcum_col = dbcum_col + jnp.where(rowi == L - 1, db_last, 0.0)
            dbcum_row = jnp.sum(jnp.where(eye, dbcum_col, 0.0), axis=1, keepdims=True) - g_row
            dlf_col = jnp.sum(jnp.where(tri_t, dbcum_row, 0.0), axis=2, keepdims=True)
            dli_col = dlogw + jnp.sum(jnp.where(eye, g_row, 0.0), axis=2, keepdims=True)
            dgf_col = dlf_col * _sigmoid(-gf)
            lane_g = lax.broadcasted_iota(jnp.int32, (L, GROUP), 1)
            dg = jnp.zeros((L, GROUP), F32)
            for h in range(H4):
                dg = dg + jnp.where(lane_g == h, dli_col[h], 0.0) + jnp.where(lane_g == H4 + h, dgf_col[h], 0.0)
            dqk_ref[0:H4, sl, :] = dq
            dqk_ref[H4:2 * H4, sl, :] = dk * kscale
            dv_ref[:, sl, :] = jnp.where(lane < NLANE, dv, 0.0).astype(BF16)
            dg_ref[0, sl, :] = dg.astype(BF16)
            dbg_ref[...] += jnp.sum(dg, axis=0, keepdims=True)
            dc_sc[...] = dc_prev

    def hspec(blk):
        return pl.BlockSpec((H4, rows, GROUP), lambda i: (blk, nb - 1 - i, 0))

    gspec = pl.BlockSpec((1, rows, GROUP), lambda i: (17, nb - 1 - i, 0))
    qkspec = pl.BlockSpec((2 * H4, rows, GROUP), lambda i: (0, nb - 1 - i, 0))
    return pl.pallas_call(
        body, name="mlstm_bwd", grid=(nb,),
        in_specs=[qkspec, hspec(2), gspec, pl.BlockSpec((1, GROUP), lambda i: (0, 0)),
                  pl.BlockSpec((cb, H4, GROUP, GROUP), lambda i: (nb - 1 - i, 0, 0, 0)),
                  pl.BlockSpec((cb, H4, 1, LANES), lambda i: (nb - 1 - i, 0, 0, 0)), hspec(0)],
        out_specs=[qkspec, hspec(0), pl.BlockSpec((1, rows, GROUP), lambda i: (0, nb - 1 - i, 0)),
                   pl.BlockSpec((1, GROUP), lambda i: (0, 0))],
        out_shape=[SDS((2 * H4, S, GROUP), F32), SDS((H4, S, GROUP), BF16),
                   SDS((1, S, GROUP), BF16), SDS((1, GROUP), F32)],
        scratch_shapes=[pltpu.VMEM((H4, GROUP, GROUP), F32)],
        compiler_params=_params(("arbitrary",)),
    )(qk, u, u, bg, cst, mst, dh)


def head_norm_fwd(hm, u, hg):
    _, S, _ = hm.shape
    ts = _tile(S, 512)

    def body(h_ref, o_ref, g_ref, t_ref):
        h = h_ref[0]
        lane = lax.broadcasted_iota(jnp.int32, h.shape, 1)
        valid = lane < ML_HEAD_DIM
        mu = jnp.sum(h, axis=-1, keepdims=True) * (1.0 / ML_HEAD_DIM)
        hc = jnp.where(valid, h - mu, 0.0)
        var = jnp.sum(hc * hc, axis=-1, keepdims=True) * (1.0 / ML_HEAD_DIM)
        hn = hc * lax.rsqrt(var + LN_EPS) * g_ref[0]
        t_ref[0] = (_sigmoid(o_ref[0]) * hn).astype(BF16)

    return pl.pallas_call(
        body, name="head_norm_fwd", grid=(H4, S // ts),
        in_specs=[pl.BlockSpec((1, ts, GROUP), lambda h, s: (h, s, 0)), pl.BlockSpec((1, ts, GROUP), lambda h, s: (12 + h, s, 0)),
                  pl.BlockSpec((1, 1, GROUP), lambda h, s: (h, 0, 0))],
        out_specs=pl.BlockSpec((1, ts, GROUP), lambda h, s: (h, s, 0)),
        out_shape=SDS((H4, S, GROUP), BF16),
        compiler_params=_params(("parallel", "parallel")),
    )(hm, u, hg)


def head_norm_bwd(hm, u, hg, dm):
    _, S, _ = hm.shape
    ts = _tile(S, 512)

    def body(h_ref, o_ref, g_ref, d_ref, dh_ref, do_ref, dg_ref):
        @pl.when(pl.program_id(1) == 0)
        def _():
            dg_ref[...] = jnp.zeros_like(dg_ref)

        h = h_ref[0]
        lane = lax.broadcasted_iota(jnp.int32, h.shape, 1)
        valid = lane < ML_HEAD_DIM
        inv = 1.0 / ML_HEAD_DIM
        mu = jnp.sum(h, axis=-1, keepdims=True) * inv
        hc = jnp.where(valid, h - mu, 0.0)
        var = jnp.sum(hc * hc, axis=-1, keepdims=True) * inv
        rstd = lax.rsqrt(var + LN_EPS)
        xhat = hc * rstd
        g = g_ref[0]
        sig = _sigmoid(o_ref[0])
        dt = jnp.where(valid, d_ref[0], 0.0)
        do_ref[0] = (dt * xhat * g * sig * (1.0 - sig)).astype(BF16)
        dhn = dt * sig
        dg_ref[0] += jnp.sum(dhn * xhat, axis=0, keepdims=True)
        dxh = dhn * g
        m1 = jnp.sum(dxh, axis=-1, keepdims=True) * inv
        m2 = jnp.sum(dxh * xhat, axis=-1, keepdims=True) * inv
        dh_ref[0] = jnp.where(valid, rstd * (dxh - m1 - xhat * m2), 0.0)

    spec = pl.BlockSpec((1, ts, GROUP), lambda h, s: (h, s, 0))
    gspec = pl.BlockSpec((1, 1, GROUP), lambda h, s: (h, 0, 0))
    return pl.pallas_call(
        body, name="head_norm_bwd", grid=(H4, S // ts),
        in_specs=[spec, pl.BlockSpec((1, ts, GROUP), lambda h, s: (12 + h, s, 0)), gspec, spec],
        out_specs=[spec, spec, gspec],
        out_shape=[SDS((H4, S, GROUP), F32), SDS((H4, S, GROUP), BF16), SDS((H4, 1, GROUP), F32)],
        compiler_params=_params(("parallel", "arbitrary")),
    )(hm, u, hg, dm)


def adamw(w, g, m, v, name):
    R, C = w.shape
    tr = R if R <= 512 else next(d for d in (512, 256, 128, 64, 32, 16, 8) if R % d == 0)
    c1 = 1.0 / (1.0 - ADAM_B1 ** ADAM_STEP)
    c2 = 1.0 / (1.0 - ADAM_B2 ** ADAM_STEP)

    def body(w_ref, g_ref, m_ref, v_ref, d_ref, nm_ref, nv_ref):
        gv = g_ref[...]
        nm = ADAM_B1 * m_ref[...] + (1.0 - ADAM_B1) * gv
        nv = ADAM_B2 * v_ref[...] + (1.0 - ADAM_B2) * (gv * gv)
        d_ref[...] = -ADAM_LR * ((nm * c1) / (jnp.sqrt(nv * c2) + ADAM_EPS) + ADAM_WD * w_ref[...])
        nm_ref[...] = nm
        nv_ref[...] = nv

    spec = pl.BlockSpec((tr, C), lambda i: (i, 0))
    return pl.pallas_call(
        body, name=name, grid=(R // tr,),
        in_specs=[spec] * 4, out_specs=[spec] * 3,
        out_shape=[SDS((R, C), F32)] * 3,
        compiler_params=_params(("parallel",)),
    )(w, g, m, v)


HBM = pl.BlockSpec(memory_space=pl.ANY)


def _position():
    x, y, c = lax.axis_index("x"), lax.axis_index("y"), lax.axis_index("c")
    return x, y, c, [(1 - x, y), (x, 1 - y), (1 - x, 1 - y)]


def gather_shards(items):
    arrays = []
    for a, _ in items:
        if not any(a is b for b in arrays):
            arrays.append(a)
    src_of = [next(i for i, b in enumerate(arrays) if b is a) for a, _ in items]
    n_in, n = len(arrays), len(items)
    shapes = [a.shape[len(p):] for a, p in items]

    def body(*refs):
        ins, outs = refs[:n_in], refs[n_in:n_in + n]
        send, recv, fsend, frecv, lsem = refs[n_in + n:]
        x, y, c, chips = _position()
        me = 2 * x + y
        started, local = [], []
        for t, (_, prefix) in enumerate(items):
            src = ins[src_of[t]].at[prefix] if prefix else ins[src_of[t]]
            half = shapes[t][0] // 2
            mine = pltpu.make_async_copy(src, outs[t].at[me], lsem.at[t])
            mine.start()
            local.append(mine)
            for j, (cx, cy) in enumerate(chips):
                cp = pltpu.make_async_remote_copy(
                    src_ref=src.at[pl.ds(c * half, half)], dst_ref=outs[t].at[me, pl.ds(c * half, half)],
                    send_sem=send.at[3 * t + j], recv_sem=recv.at[3 * t + j], device_id=(cx, cy, c), device_id_type=MESH)
                cp.start()
                started.append(cp)
        for t in range(n):
            half = shapes[t][0] // 2
            for j, (cx, cy) in enumerate(chips):
                piece = outs[t].at[2 * cx + cy, pl.ds(c * half, half)]
                pltpu.make_async_remote_copy(src_ref=piece, dst_ref=piece, send_sem=send.at[3 * t + j], recv_sem=recv.at[3 * t + j],
                                             device_id=(cx, cy, c), device_id_type=MESH).wait_recv()
                fw = pltpu.make_async_remote_copy(src_ref=piece, dst_ref=piece, send_sem=fsend.at[3 * t + j],
                                                  recv_sem=frecv.at[3 * t + j], device_id=(x, y, 1 - c), device_id_type=MESH)
                fw.start()
                started.append(fw)
        for t in range(n):
            half = shapes[t][0] // 2
            for j, (cx, cy) in enumerate(chips):
                piece = outs[t].at[2 * cx + cy, pl.ds((1 - c) * half, half)]
                pltpu.make_async_remote_copy(src_ref=piece, dst_ref=piece, send_sem=fsend.at[3 * t + j], recv_sem=frecv.at[3 * t + j],
                                             device_id=(x, y, 1 - c), device_id_type=MESH).wait_recv()
        for cp in started:
            cp.wait_send()
        for cp in local:
            cp.wait()

    return pl.pallas_call(
        body, name="gather_shards",
        in_specs=[HBM] * n_in, out_specs=[HBM] * n,
        out_shape=[SDS((N_CHIPS,) + tuple(s), a.dtype) for s, (a, _) in zip(shapes, items)],
        scratch_shapes=[pltpu.SemaphoreType.DMA((3 * n,))] * 4 + [pltpu.SemaphoreType.DMA((n,))],
    )(*arrays)


def _flip(k, x, y, c):
    return ((1 - x) if k & 4 else x, (1 - y) if k & 2 else y, (1 - c) if k & 1 else c)


def small_allgather(v, reduce):
    R, C = v.shape

    def body(v_ref, o_ref, *scratch):
        if reduce:
            buf, send, recv = scratch
        else:
            buf, (send, recv) = o_ref, scratch
        x, y, c, _ = _position()
        me = 4 * x + 2 * y + c
        buf[me] = v_ref[...]
        sends = []
        for k in range(1, N_DEV):
            cp = pltpu.make_async_remote_copy(src_ref=v_ref, dst_ref=buf.at[me], send_sem=send.at[k - 1], recv_sem=recv.at[k - 1],
                                              device_id=_flip(k, x, y, c), device_id_type=MESH)
            cp.start()
            sends.append(cp)
        for k in range(1, N_DEV):
            px, py, pc = _flip(k, x, y, c)
            pltpu.make_async_remote_copy(src_ref=v_ref, dst_ref=buf.at[4 * px + 2 * py + pc], send_sem=send.at[k - 1],
                                         recv_sem=recv.at[k - 1], device_id=(px, py, pc), device_id_type=MESH).wait_recv()
        for cp in sends:
            cp.wait_send()
        if reduce:
            acc = buf[0]
            for i in range(1, N_DEV):
                acc = acc + buf[i]
            o_ref[...] = acc

    vm = pl.BlockSpec(memory_space=pltpu.VMEM)
    sems = [pltpu.SemaphoreType.DMA((N_DEV - 1,)), pltpu.SemaphoreType.DMA((N_DEV - 1,))]
    return pl.pallas_call(
        body, name="small_allreduce" if reduce else "small_allgather",
        in_specs=[vm], out_specs=vm,
        out_shape=SDS((R, C) if reduce else (N_DEV, R, C), F32),
        scratch_shapes=([pltpu.VMEM((N_DEV, R, C), F32)] if reduce else []) + sems,
    )(v)


def rs_exchange_sibling(gs):
    n = len(gs)

    def body(*refs):
        ins, outs, send, recv = refs[:n], refs[n:2 * n], refs[2 * n], refs[2 * n + 1]
        x, y, c, _ = _position()
        cps = []
        for t in range(n):
            cp = pltpu.make_async_remote_copy(src_ref=ins[t].at[:, 1 - c], dst_ref=outs[t], send_sem=send.at[t], recv_sem=recv.at[t],
                                              device_id=(x, y, 1 - c), device_id_type=MESH)
            cp.start()
            cps.append(cp)
        for cp in cps:
            cp.wait()

    return pl.pallas_call(
        body, name="rs_exchange_sibling", in_specs=[HBM] * n, out_specs=[HBM] * n,
        out_shape=[SDS((g.shape[0],) + g.shape[2:], g.dtype) for g in gs],
        scratch_shapes=[pltpu.SemaphoreType.DMA((n,)), pltpu.SemaphoreType.DMA((n,))],
    )(*gs)


ROW_SPLIT = 4


def rs_pair_add(gs, rs, c):
    n = len(gs)

    def body(c_ref, *refs):
        for t in range(n):
            refs[2 * n + t][0] = (refs[t][0, 0].astype(F32) + refs[n + t][0].astype(F32)).astype(BF16)

    in_specs, out_specs, out_shape = [], [], []
    for g in gs:
        _, _, h, C = g.shape
        in_specs.append(pl.BlockSpec((1, 1, h // ROW_SPLIT, C), lambda j, r, c_ref: (j, c_ref[0], r, 0)))
    for g in gs:
        _, _, h, C = g.shape
        spec = pl.BlockSpec((1, h // ROW_SPLIT, C), lambda j, r, c_ref: (j, r, 0))
        in_specs.append(spec)
        out_specs.append(spec)
        out_shape.append(SDS((N_CHIPS, h, C), BF16))
    return pl.pallas_call(
        body, name="rs_pair_add",
        grid_spec=pltpu.PrefetchScalarGridSpec(num_scalar_prefetch=1, grid=(N_CHIPS, ROW_SPLIT), in_specs=in_specs, out_specs=out_specs),
        out_shape=out_shape, compiler_params=_params(("parallel", "parallel")),
    )(c, *gs, *rs)


def rs_exchange_chips(ps):
    n = len(ps)

    def body(*refs):
        ins, outs, send, recv = refs[:n], refs[n:2 * n], refs[2 * n], refs[2 * n + 1]
        x, y, c, chips = _position()
        cps = []
        for t in range(n):
            for j, (cx, cy) in enumerate(chips):
                cp = pltpu.make_async_remote_copy(src_ref=ins[t].at[2 * cx + cy], dst_ref=outs[t].at[j], send_sem=send.at[3 * t + j],
                                                  recv_sem=recv.at[3 * t + j], device_id=(cx, cy, c), device_id_type=MESH)
                cp.start()
                cps.append(cp)
        for cp in cps:
            cp.wait()

    return pl.pallas_call(
        body, name="rs_exchange_chips", in_specs=[HBM] * n, out_specs=[HBM] * n,
        out_shape=[SDS((3,) + p.shape[1:], p.dtype) for p in ps],
        scratch_shapes=[pltpu.SemaphoreType.DMA((3 * n,)), pltpu.SemaphoreType.DMA((3 * n,))],
    )(*ps)


def rs_chip_add(ps, qs, me):
    n = len(ps)

    def body(me_ref, *refs):
        for t in range(n):
            q = refs[n + t]
            refs[2 * n + t][...] = ((refs[t][0].astype(F32) + q[0].astype(F32)) + q[1].astype(F32)) + q[2].astype(F32)

    in_specs, out_specs, out_shape = [], [], []
    for p in ps:
        _, h, C = p.shape
        in_specs.append(pl.BlockSpec((1, h // ROW_SPLIT, C), lambda r, me_ref: (me_ref[0], r, 0)))
    for p in ps:
        _, h, C = p.shape
        in_specs.append(pl.BlockSpec((3, h // ROW_SPLIT, C), lambda r, me_ref: (0, r, 0)))
        out_specs.append(pl.BlockSpec((h // ROW_SPLIT, C), lambda r, me_ref: (r, 0)))
        out_shape.append(SDS((h, C), F32))
    return pl.pallas_call(
        body, name="rs_chip_add",
        grid_spec=pltpu.PrefetchScalarGridSpec(num_scalar_prefetch=1, grid=(ROW_SPLIT,), in_specs=in_specs, out_specs=out_specs),
        out_shape=out_shape, compiler_params=_params(("parallel",)),
    )(me, *ps, *qs)


def rs_share(rs):
    n = len(rs)

    def body(*refs):
        ins, outs, send, recv, lsem = refs[:n], refs[n:2 * n], refs[2 * n], refs[2 * n + 1], refs[2 * n + 2]
        x, y, c, _ = _position()
        cps, local = [], []
        for t in range(n):
            mine = pltpu.make_async_copy(ins[t], outs[t].at[c], lsem.at[t])
            mine.start()
            local.append(mine)
            cp = pltpu.make_async_remote_copy(src_ref=ins[t], dst_ref=outs[t].at[c], send_sem=send.at[t], recv_sem=recv.at[t],
                                              device_id=(x, y, 1 - c), device_id_type=MESH)
            cp.start()
            cps.append(cp)
        for cp in cps:
            cp.wait()
        for cp in local:
            cp.wait()

    return pl.pallas_call(
        body, name="rs_share", in_specs=[HBM] * n, out_specs=[HBM] * n,
        out_shape=[SDS((2,) + r.shape, r.dtype) for r in rs],
        scratch_shapes=[pltpu.SemaphoreType.DMA((n,))] * 3,
    )(*rs)


def reduce_scatter(gs):
    x, y, c = lax.axis_index("x"), lax.axis_index("y"), lax.axis_index("c")
    g5 = [g.reshape(N_CHIPS, 2, g.shape[1] // 2, g.shape[2]) for g in gs]
    from_sibling = rs_exchange_sibling(g5)
    pair = rs_pair_add(g5, from_sibling, jnp.reshape(c, (1,)).astype(jnp.int32))
    from_chips = rs_exchange_chips(pair)
    half = rs_chip_add(pair, from_chips, jnp.reshape(2 * x + y, (1,)).astype(jnp.int32))
    both = rs_share(half)
    return [b.reshape(g.shape[1], g.shape[2]) for b, g in zip(both, gs)]


def _pad_last(a, n):
    return jnp.pad(a, [(0, 0)] * (a.ndim - 1) + [(0, n - a.shape[-1])])


def _heads_to_groups(w):
    k = w.shape[0]
    return _pad_last(w.reshape(k, ML_HEADS, ML_HEAD_DIM).transpose(1, 0, 2), GROUP)


def _groups_to_heads(g):
    return g[:, :, :ML_HEAD_DIM].transpose(1, 0, 2).reshape(g.shape[1], D_TOK)


def _cols_to_groups(w):
    k, n = w.shape
    return w.reshape(k, n // GROUP, GROUP).transpose(1, 0, 2)


def _groups_to_cols(g):
    n, k, _ = g.shape
    return g.transpose(1, 0, 2).reshape(k, n * GROUP)


def _chips_to_cols(a):
    return a.transpose(1, 0, 2).reshape(a.shape[1], -1)


def _cols_to_chips(w):
    k, n = w.shape
    return w.reshape(k, N_CHIPS, n // N_CHIPS).transpose(1, 0, 2)


def _mlstm_in_groups(w):
    parts = [_heads_to_groups(w[:, i * D_TOK:(i + 1) * D_TOK]) for i in range(4)]
    gates = _pad_last(w[:, 4 * D_TOK:4 * D_TOK + 2 * ML_HEADS], GROUP)[None]
    qmem = w[:, 4 * D_TOK + 2 * ML_HEADS:][None]
    return jnp.concatenate(parts + [qmem, gates], axis=0)


def _mlstm_in_ungroup(g):
    parts = [_groups_to_heads(g[4 * i:4 * i + 4]) for i in range(4)]
    return jnp.concatenate(parts + [g[17][:, :2 * ML_HEADS], g[16]], axis=1)


def _taps_to_groups(w, width):
    taps = w.shape[0]
    g = _pad_last(w.reshape(taps, -1, width), GROUP).transpose(1, 0, 2)
    return jnp.pad(g, ((0, 0), (0, 8 - taps), (0, 0)))


def _groups_to_taps(g, taps, width):
    return g[:, :taps, :width].transpose(1, 0, 2).reshape(taps, -1)


SMALL_ROWS = 24
SMALL_IN_COLS = 384
SMALL_OUT_COLS = 1536


def _local_step(x, mem, tgt, P):
    xb = x.astype(BF16)
    memb = mem.astype(BF16)
    saved = []
    X, Xb = x, xb
    for l in range(DEPTH):
        s = {}
        s["x0b"] = Xb
        s["g1a"], s["u1a"], s["ha"] = ffn_up(Xb, P["wg"][l][0], P["wu"][l][0])
        s["z1"], X1, X1b = contract_ln(s["ha"], P["wd"][l][0], X, P["ln_g"][l][0], P["ln_b"][l][0], 0.5, "ffn_down_ln")
        s["x1b"] = X1b
        u = proj(X1b, P["win"][l], "mixer_in")
        kv = proj(memb, P["wkv"][l], "mem_kv")
        s["u"], s["kv"] = u, kv
        if l % 2 == 0:
            tok = conv_mixer_fwd(u, P["convw"])
            qg = 9
        else:
            s["qk"] = qk_conv_fwd(u, P["qkw"])
            s["hm"], s["cst"], s["mst"] = mlstm_fwd(s["qk"], u, P["bg"])
            tok = head_norm_fwd(s["hm"], u, P["hg"])
            qg = 16
        xa = xattn_fwd(u, qg, kv)
        s["m"] = jnp.concatenate([tok, xa], axis=0)
        s["z2"], X2, X2b = contract_ln(s["m"], P["wout"][l], X1, P["ln_g"][l][1], P["ln_b"][l][1], 1.0, "mixer_out_ln")
        s["x2b"] = X2b
        s["g1b"], s["u1b"], s["hb"] = ffn_up(X2b, P["wg"][l][1], P["wu"][l][1])
        s["z3"], X, Xb = contract_ln(s["hb"], P["wd"][l][1], X2, P["ln_g"][l][2], P["ln_b"][l][2], 0.5, "ffn_down_ln")
        saved.append(s)

    loss, dX = loss_grad(X, tgt)

    G = {"ln_g": [[None] * 3 for _ in range(DEPTH)], "ln_b": [[None] * 3 for _ in range(DEPTH)],
         "wg": [[None, None] for _ in range(DEPTH)], "wu": [[None, None] for _ in range(DEPTH)],
         "wd": [[None, None] for _ in range(DEPTH)], "wkv": [None] * DEPTH, "wout": [None] * DEPTH, "win": [None] * DEPTH}

    def ffn_backward(l, i, dX, z, xinb, g1, u1, h):
        dz, dyb, G["ln_g"][l][2 * i], G["ln_b"][l][2 * i] = ln_bwd(dX, z, P["ln_g"][l][2 * i], 0.5, "ffn_ln_bwd")
        dgb, dub = ffn_bwd_dh(dyb, P["wd"][l][i], g1, u1)
        G["wd"][l][i] = wgrad(h, dyb, BF16, "wgrad_down")
        G["wg"][l][i] = wgrad(xinb, dgb, BF16, "wgrad_gate")
        G["wu"][l][i] = wgrad(xinb, dub, BF16, "wgrad_up")
        return contract_t([(dgb, P["wg"][l][i]), (dub, P["wu"][l][i])], dz, "ffn_bwd_dx")

    for l in reversed(range(DEPTH)):
        s = saved[l]
        dX = ffn_backward(l, 1, dX, s["z3"], s["x2b"], s["g1b"], s["u1b"], s["hb"])
        dz2, dz2b, G["ln_g"][l][1], G["ln_b"][l][1] = ln_bwd(dX, s["z2"], P["ln_g"][l][1], 1.0, "mixer_ln_bwd")
        dm = proj_t(dz2b, P["wout"][l], "mixer_out_bwd")
        G["wout"][l] = wgrad(s["m"], dz2b, BF16, "wgrad_out")
        u, kv = s["u"], s["kv"]
        if l % 2 == 0:
            db, dc, dxi, G["convw"] = conv_mixer_bwd(u, P["convw"], dm)
            dq, dkv = xattn_bwd(u, 9, kv, dm, 3)
            du = jnp.concatenate([db, dc, dxi, dq], axis=0)
        else:
            dh, do, G["hg"] = head_norm_bwd(s["hm"], u, P["hg"], dm)
            dqk, dv, dgate, G["bg"] = mlstm_bwd(s["qk"], u, P["bg"], s["cst"], s["mst"], dh)
            duqk, G["qkw"] = qk_conv_bwd(u, P["qkw"], dqk)
            dq, dkv = xattn_bwd(u, 16, kv, dm, 4)
            du = jnp.concatenate([duqk, dv, do, dq, dgate], axis=0)
        G["win"][l] = wgrad(s["x1b"], du, BF16, "wgrad_in")
        G["wkv"][l] = wgrad(memb, dkv.astype(BF16), BF16, "wgrad_kv")
        dX = contract_t([(du, P["win"][l])], dz2, "mixer_in_bwd")
        dX = ffn_backward(l, 0, dX, s["z1"], s["x0b"], s["g1a"], s["u1a"], s["ha"])
    return loss, dX, G


def kernel(x, mem, ln_g, ln_b, ffn_w_gate, ffn_w_up, ffn_w_down, w_kv_mem, w_out, w_in_conv, conv_w, w_in_mlstm, b_gates, qk_conv_w, head_norm_g, loss_target, m_ln_g, m_ln_b, m_ffn_w_gate, m_ffn_w_up, m_ffn_w_down, m_w_kv_mem, m_w_out, m_w_in_conv, m_conv_w, m_w_in_mlstm, m_b_gates, m_qk_conv_w, m_head_norm_g, v_ln_g, v_ln_b, v_ffn_w_gate, v_ffn_w_up, v_ffn_w_down, v_w_kv_mem, v_w_out, v_w_in_conv, v_conv_w, v_w_in_mlstm, v_b_gates, v_qk_conv_w, v_head_norm_g):
    cx, cy = lax.axis_index("x"), lax.axis_index("y")
    chip = 2 * cx + cy

    big = {"wg": ffn_w_gate.astype(BF16), "wu": ffn_w_up.astype(BF16), "wd": ffn_w_down.astype(BF16),
           "wkv": w_kv_mem.astype(BF16), "wout": w_out.astype(BF16), "winc": w_in_conv.astype(BF16), "winm": w_in_mlstm.astype(BF16)}
    items, where = [], {}
    for name in ("wg", "wu", "wd"):
        for l in range(DEPTH):
            for i in range(2):
                where[(name, l, i)] = len(items)
                items.append((big[name], (l, i)))
    for name, prefix in (("wkv", ()), ("wout", ()), ("winc", (0,)), ("winm", (0,))):
        where[name] = len(items)
        items.append((big[name], prefix))
    gathered = gather_shards(items)

    small = jnp.zeros((SMALL_ROWS, SMALL_IN_COLS), F32)
    small = small.at[0:6, 0:256].set(ln_g.reshape(6, 256)).at[6:12, 0:256].set(ln_b.reshape(6, 256))
    small = small.at[12:15, 0:192].set(conv_w[0]).at[16:20, 0:384].set(qk_conv_w[0])
    smalls = small_allgather(small, reduce=False)[0::2]
    ln_g_full = _chips_to_cols(smalls[:, 0:6, 0:256]).reshape(DEPTH, 3, 1, D_MODEL)
    ln_b_full = _chips_to_cols(smalls[:, 6:12, 0:256]).reshape(DEPTH, 3, 1, D_MODEL)
    conv_w_full = _chips_to_cols(smalls[:, 12:15, 0:192])
    qk_w_full = _chips_to_cols(smalls[:, 16:20, 0:384])

    P = {"wg": [[gathered[where[("wg", l, i)]] for i in range(2)] for l in range(DEPTH)],
         "wu": [[gathered[where[("wu", l, i)]] for i in range(2)] for l in range(DEPTH)],
         "wd": [[gathered[where[("wd", l, i)]] for i in range(2)] for l in range(DEPTH)],
         "ln_g": ln_g_full, "ln_b": ln_b_full}
    wkv_all, wout_all = gathered[where["wkv"]], gathered[where["wout"]]
    P["wkv"] = [_cols_to_groups(wkv_all[:, l].reshape(D_MODEL, 2 * D_XA)) for l in range(DEPTH)]
    wout1 = wout_all[:, 1].reshape(D_MODEL, D_MODEL)
    wout1_tok = jnp.pad(wout1[:D_TOK].reshape(ML_HEADS, ML_HEAD_DIM, D_MODEL), ((0, 0), (0, GROUP - ML_HEAD_DIM), (0, 0)))
    P["wout"] = [wout_all[:, 0], jnp.concatenate([wout1_tok, wout1[D_TOK:][None]], axis=0)]
    P["win"] = [_cols_to_groups(_chips_to_cols(gathered[where["winc"]])), _mlstm_in_groups(_chips_to_cols(gathered[where["winm"]]))]
    P["convw"] = _taps_to_groups(conv_w_full, GROUP)
    P["qkw"] = _taps_to_groups(qk_w_full, ML_HEAD_DIM)
    P["bg"] = _pad_last(b_gates, GROUP)
    P["hg"] = _pad_last(head_norm_g[0], GROUP)[:, None, :]

    loss, grad_x, G = _local_step(x[0], mem[0], loss_target[0], P)

    gs, slot = [], {}
    for name in ("wg", "wu", "wd"):
        for l in range(DEPTH):
            for i in range(2):
                slot[(name, l, i)] = len(gs)
                gs.append(G[name][l][i])
    dwkv = jnp.stack([_groups_to_cols(G["wkv"][l]).reshape(N_CHIPS, D_MODEL // N_CHIPS, 2 * D_XA) for l in range(DEPTH)], axis=1)
    dwout1 = jnp.concatenate([G["wout"][1][:ML_HEADS, :ML_HEAD_DIM].reshape(D_TOK, D_MODEL), G["wout"][1][ML_HEADS]], axis=0)
    dwout = jnp.stack([G["wout"][0], dwout1.reshape(N_CHIPS, D_MODEL // N_CHIPS, D_MODEL)], axis=1)
    slot["wkv"], slot["wout"], slot["winc"], slot["winm"] = len(gs), len(gs) + 1, len(gs) + 2, len(gs) + 3
    gs += [dwkv.reshape(N_CHIPS, 2 * (D_MODEL // N_CHIPS), 2 * D_XA), dwout.reshape(N_CHIPS, 2 * (D_MODEL // N_CHIPS), D_MODEL),
           _cols_to_chips(_groups_to_cols(G["win"][0])), _cols_to_chips(_mlstm_in_ungroup(G["win"][1]))]
    red = reduce_scatter(gs)

    sg = jnp.zeros((SMALL_ROWS, SMALL_OUT_COLS), F32)
    dln_g = jnp.concatenate([G["ln_g"][l][k] for l in range(DEPTH) for k in range(3)], axis=0)
    dln_b = jnp.concatenate([G["ln_b"][l][k] for l in range(DEPTH) for k in range(3)], axis=0)
    sg = sg.at[0:6, 0:D_MODEL].set(dln_g).at[6:12, 0:D_MODEL].set(dln_b)
    sg = sg.at[12:15, 0:D_TOK].set(_groups_to_taps(G["convw"], 3, GROUP))
    sg = sg.at[15:16, 0:8].set(G["bg"][:, 0:8]).at[15:16, 8:9].set(loss)
    sg = sg.at[16:20, 0:2 * D_TOK].set(_groups_to_taps(G["qkw"], 4, ML_HEAD_DIM))
    sg = sg.at[20:24, 0:ML_HEAD_DIM].set(G["hg"][:, 0, :ML_HEAD_DIM])
    tot = small_allgather(sg, reduce=True)

    grads = {
        "ln_g": lax.dynamic_slice(tot[0:6, 0:D_MODEL], (0, chip * 256), (6, 256)).reshape(DEPTH, 3, 256),
        "ln_b": lax.dynamic_slice(tot[6:12, 0:D_MODEL], (0, chip * 256), (6, 256)).reshape(DEPTH, 3, 256),
        "ffn_w_gate": jnp.stack([jnp.stack([red[slot[("wg", l, i)]] for i in range(2)]) for l in range(DEPTH)]),
        "ffn_w_up": jnp.stack([jnp.stack([red[slot[("wu", l, i)]] for i in range(2)]) for l in range(DEPTH)]),
        "ffn_w_down": jnp.stack([jnp.stack([red[slot[("wd", l, i)]] for i in range(2)]) for l in range(DEPTH)]),
        "w_kv_mem": red[slot["wkv"]].reshape(DEPTH, D_MODEL // N_CHIPS, 2 * D_XA),
        "w_out": red[slot["wout"]].reshape(DEPTH, D_MODEL // N_CHIPS, D_MODEL),
        "w_in_conv": red[slot["winc"]][None],
        "conv_w": lax.dynamic_slice(tot[12:15, 0:D_TOK], (0, chip * 192), (3, 192))[None],
        "w_in_mlstm": red[slot["winm"]][None],
        "b_gates": tot[15:16, 0:8],
        "qk_conv_w": lax.dynamic_slice(tot[16:20, 0:2 * D_TOK], (0, chip * 384), (4, 384))[None],
        "head_norm_g": tot[20:24, 0:ML_HEAD_DIM][None],
    }
    loss_total = tot[15, 8]

    weights = {"ln_g": ln_g, "ln_b": ln_b, "ffn_w_gate": ffn_w_gate, "ffn_w_up": ffn_w_up, "ffn_w_down": ffn_w_down,
               "w_kv_mem": w_kv_mem, "w_out": w_out, "w_in_conv": w_in_conv, "conv_w": conv_w, "w_in_mlstm": w_in_mlstm,
               "b_gates": b_gates, "qk_conv_w": qk_conv_w, "head_norm_g": head_norm_g}
    ms = {"ln_g": m_ln_g, "ln_b": m_ln_b, "ffn_w_gate": m_ffn_w_gate, "ffn_w_up": m_ffn_w_up, "ffn_w_down": m_ffn_w_down,
          "w_kv_mem": m_w_kv_mem, "w_out": m_w_out, "w_in_conv": m_w_in_conv, "conv_w": m_conv_w, "w_in_mlstm": m_w_in_mlstm,
          "b_gates": m_b_gates, "qk_conv_w": m_qk_conv_w, "head_norm_g": m_head_norm_g}
    vs = {"ln_g": v_ln_g, "ln_b": v_ln_b, "ffn_w_gate": v_ffn_w_gate, "ffn_w_up": v_ffn_w_up, "ffn_w_down": v_ffn_w_down,
          "w_kv_mem": v_w_kv_mem, "w_out": v_w_out, "w_in_conv": v_w_in_conv, "conv_w": v_conv_w, "w_in_mlstm": v_w_in_mlstm,
          "b_gates": v_b_gates, "qk_conv_w": v_qk_conv_w, "head_norm_g": v_head_norm_g}
    names = list(weights)
    deltas, new_m, new_v = [], [], []
    for nme in names:
        w = weights[nme]
        shp = w.shape
        two = (math.prod(shp[:-1]), shp[-1])
        d, nm, nv = adamw(w.reshape(two), grads[nme].reshape(two), ms[nme].reshape(two), vs[nme].reshape(two), "adamw_" + nme)
        deltas.append(d.reshape(shp))
        new_m.append(nm.reshape(shp))
        new_v.append(nv.reshape(shp))
    return (loss_total, grad_x[None], *[grads[nme] for nme in names], *deltas, *new_m, *new_v)
```

```python
import functools
import math

import jax
import jax.numpy as jnp
from jax import lax
from jax.experimental import pallas as pl
from jax.experimental.pallas import tpu as pltpu

F32 = jnp.float32
BF16 = jnp.bfloat16
SDS = jax.ShapeDtypeStruct

D_MODEL = 1024
DEPTH = 2
N_MEM = 256
XA_HEADS = 4
XA_HEAD_DIM = 64
D_XA = 256
D_TOK = 768
ML_HEADS = 4
ML_HEAD_DIM = 192
ML_CHUNK = 64
D_FF = 2816
LN_EPS = 1e-5
ALPHA = (2.0 * DEPTH) ** 0.25
N_CHIPS = 4
N_DEV = 8
FF_SHARD = D_FF // N_CHIPS
GROUP = 256
NEG = -1e30

ADAM_LR = 0.001
ADAM_B1 = 0.9
ADAM_B2 = 0.999
ADAM_EPS = 1e-08
ADAM_WD = 0.01
ADAM_STEP = 10

VMEM_LIMIT = 56 * 1024 * 1024

NN = ((1,), (0,))
NT = ((1,), (1,))
TN = ((0,), (0,))
MESH = pl.DeviceIdType.MESH


def _dot(a, b, dims):
    return lax.dot_general(a, b, (dims, ((), ())), preferred_element_type=F32)


def _bdot(a, b, ca, cb):
    return lax.dot_general(a, b, (((ca,), (cb,)), ((0,), (0,))), preferred_element_type=F32)


def _sigmoid(x):
    return 1.0 / (1.0 + jnp.exp(-x))


def _params(sem, vmem=VMEM_LIMIT):
    return pltpu.CompilerParams(dimension_semantics=sem, vmem_limit_bytes=vmem)


def _tile(n, want):
    t = min(n, want)
    assert n % t == 0, (n, t)
    return t


def ffn_up(xb, wg, wu):
    S, K = xb.shape
    G, _, N = wg.shape
    ts = _tile(S, 1024)

    def body(x_ref, wg_ref, wu_ref, g_ref, u_ref, h_ref):
        x = x_ref[...]
        g = _dot(x, wg_ref[0], NN)
        u = _dot(x, wu_ref[0], NN)
        g_ref[0] = g
        u_ref[0] = u
        h_ref[0] = (g * _sigmoid(g) * u).astype(BF16)

    wspec = pl.BlockSpec((1, K, N), lambda g, s: (g, 0, 0))
    ospec = pl.BlockSpec((1, ts, N), lambda g, s: (g, s, 0))
    return pl.pallas_call(
        body, name="ffn_up", grid=(G, S // ts),
        in_specs=[pl.BlockSpec((ts, K), lambda g, s: (s, 0)), wspec, wspec],
        out_specs=[ospec, ospec, ospec],
        out_shape=[SDS((G, S, N), F32), SDS((G, S, N), F32), SDS((G, S, N), BF16)],
        compiler_params=_params(("parallel", "parallel")),
    )(xb, wg, wu)


def proj(xb, w, name):
    S, K = xb.shape
    G, _, N = w.shape
    ts = _tile(S, 1024)

    def body(x_ref, w_ref, y_ref):
        y_ref[0] = _dot(x_ref[...], w_ref[0], NN)

    return pl.pallas_call(
        body, name=name, grid=(G, S // ts),
        in_specs=[pl.BlockSpec((ts, K), lambda g, s: (s, 0)), pl.BlockSpec((1, K, N), lambda g, s: (g, 0, 0))],
        out_specs=pl.BlockSpec((1, ts, N), lambda g, s: (g, s, 0)),
        out_shape=SDS((G, S, N), F32),
        compiler_params=_params(("parallel", "parallel")),
    )(xb, w)


def contract_ln(a, w, xres, gamma, beta, scale, name):
    G, S, Kg = a.shape
    N = w.shape[2]
    ts = _tile(S, 512)

    def body(a_ref, w_ref, x_ref, g_ref, b_ref, z_ref, xn_ref, xb_ref, acc):
        g = pl.program_id(1)

        @pl.when(g == 0)
        def _():
            acc[...] = jnp.zeros_like(acc)

        acc[...] += _dot(a_ref[0], w_ref[0], NN)

        @pl.when(g == G - 1)
        def _():
            z = ALPHA * x_ref[...] + scale * acc[...]
            mu = jnp.mean(z, axis=-1, keepdims=True)
            zc = z - mu
            var = jnp.mean(zc * zc, axis=-1, keepdims=True)
            xn = zc * lax.rsqrt(var + LN_EPS) * g_ref[...] + b_ref[...]
            z_ref[...] = z
            xn_ref[...] = xn
            xb_ref[...] = xn.astype(BF16)

    row = pl.BlockSpec((ts, N), lambda s, g: (s, 0))
    vec = pl.BlockSpec((1, N), lambda s, g: (0, 0))
    return pl.pallas_call(
        body, name=name, grid=(S // ts, G),
        in_specs=[pl.BlockSpec((1, ts, Kg), lambda s, g: (g, s, 0)), pl.BlockSpec((1, Kg, N), lambda s, g: (g, 0, 0)),
                  row, vec, vec],
        out_specs=[row, row, row],
        out_shape=[SDS((S, N), F32), SDS((S, N), F32), SDS((S, N), BF16)],
        scratch_shapes=[pltpu.VMEM((ts, N), F32)],
        compiler_params=_params(("parallel", "arbitrary")),
    )(a, w, xres, gamma, beta)


def ln_bwd(dx, z, gamma, out_scale, name):
    S, N = dx.shape
    ts = _tile(S, 512)

    def body(dx_ref, z_ref, g_ref, dz_ref, dzb_ref, dg_ref, db_ref):
        @pl.when(pl.program_id(0) == 0)
        def _():
            dg_ref[...] = jnp.zeros_like(dg_ref)
            db_ref[...] = jnp.zeros_like(db_ref)

        z = z_ref[...]
        mu = jnp.mean(z, axis=-1, keepdims=True)
        zc = z - mu
        var = jnp.mean(zc * zc, axis=-1, keepdims=True)
        rstd = lax.rsqrt(var + LN_EPS)
        xhat = zc * rstd
        dxv = dx_ref[...]
        dg_ref[...] += jnp.sum(dxv * xhat, axis=0, keepdims=True)
        db_ref[...] += jnp.sum(dxv, axis=0, keepdims=True)
        dxh = dxv * g_ref[...]
        m1 = jnp.mean(dxh, axis=-1, keepdims=True)
        m2 = jnp.mean(dxh * xhat, axis=-1, keepdims=True)
        dz = rstd * (dxh - m1 - xhat * m2)
        dz_ref[...] = dz
        dzb_ref[...] = (out_scale * dz).astype(BF16)

    row = pl.BlockSpec((ts, N), lambda s: (s, 0))
    vec = pl.BlockSpec((1, N), lambda s: (0, 0))
    return pl.pallas_call(
        body, name=name, grid=(S // ts,),
        in_specs=[row, row, vec],
        out_specs=[row, row, vec, vec],
        out_shape=[SDS((S, N), F32), SDS((S, N), BF16), SDS((1, N), F32), SDS((1, N), F32)],
        compiler_params=_params(("arbitrary",)),
    )(dx, z, gamma)


def ffn_bwd_dh(dyb, wd, g1, u1):
    S, K = dyb.shape
    G, N, _ = wd.shape
    ts = _tile(S, 1024)

    def body(dy_ref, w_ref, g_ref, u_ref, dg_ref, du_ref):
        dh = _dot(dy_ref[...], w_ref[0], NT)
        g = g_ref[0]
        sig = _sigmoid(g)
        silu = g * sig
        dg_ref[0] = (dh * u_ref[0] * (sig * (1.0 + g * (1.0 - sig)))).astype(BF16)
        du_ref[0] = (dh * silu).astype(BF16)

    gspec = pl.BlockSpec((1, ts, N), lambda g, s: (g, s, 0))
    return pl.pallas_call(
        body, name="ffn_bwd_dh", grid=(G, S // ts),
        in_specs=[pl.BlockSpec((ts, K), lambda g, s: (s, 0)), pl.BlockSpec((1, N, K), lambda g, s: (g, 0, 0)), gspec, gspec],
        out_specs=[gspec, gspec],
        out_shape=[SDS((G, S, N), BF16), SDS((G, S, N), BF16)],
        compiler_params=_params(("parallel", "parallel")),
    )(dyb, wd, g1, u1)


def proj_t(dyb, w, name):
    S, N = dyb.shape
    G, Kg, _ = w.shape
    ts = _tile(S, 1024)

    def body(dy_ref, w_ref, da_ref):
        da_ref[0] = _dot(dy_ref[...], w_ref[0], NT)

    return pl.pallas_call(
        body, name=name, grid=(G, S // ts),
        in_specs=[pl.BlockSpec((ts, N), lambda g, s: (s, 0)), pl.BlockSpec((1, Kg, N), lambda g, s: (g, 0, 0))],
        out_specs=pl.BlockSpec((1, ts, Kg), lambda g, s: (g, s, 0)),
        out_shape=SDS((G, S, Kg), F32),
        compiler_params=_params(("parallel", "parallel")),
    )(dyb, w)


def contract_t(pairs, res, name):
    n = len(pairs)
    G, S, Ng = pairs[0][0].shape
    K = pairs[0][1].shape[1]
    ts = _tile(S, 512)

    def body(*refs):
        ins, r_ref, o_ref, acc = refs[:2 * n], refs[2 * n], refs[2 * n + 1], refs[2 * n + 2]
        g = pl.program_id(1)

        @pl.when(g == 0)
        def _():
            acc[...] = ALPHA * r_ref[...]

        for p in range(n):
            acc[...] += _dot(ins[2 * p][0], ins[2 * p + 1][0], NT)

        @pl.when(g == G - 1)
        def _():
            o_ref[...] = acc[...]

    in_specs, args = [], []
    for da, w in pairs:
        in_specs += [pl.BlockSpec((1, ts, Ng), lambda s, g: (g, s, 0)), pl.BlockSpec((1, K, Ng), lambda s, g: (g, 0, 0))]
        args += [da, w]
    row = pl.BlockSpec((ts, K), lambda s, g: (s, 0))
    return pl.pallas_call(
        body, name=name, grid=(S // ts, G),
        in_specs=in_specs + [row], out_specs=row,
        out_shape=SDS((S, K), F32),
        scratch_shapes=[pltpu.VMEM((ts, K), F32)],
        compiler_params=_params(("parallel", "arbitrary")),
    )(*args, res)


def wgrad(a, b, out_dtype, name):
    ga, gb = a.ndim == 3, b.ndim == 3
    G = a.shape[0] if ga else b.shape[0]
    S, K = a.shape[-2:]
    N = b.shape[-1]
    ts = _tile(S, 1024)
    ns = S // ts

    def body(a_ref, b_ref, o_ref, acc):
        s = pl.program_id(1)

        @pl.when(s == 0)
        def _():
            acc[...] = jnp.zeros_like(acc)

        av = a_ref[0] if ga else a_ref[...]
        bv = b_ref[0] if gb else b_ref[...]
        acc[...] += _dot(av, bv, TN)

        @pl.when(s == ns - 1)
        def _():
            o_ref[0] = acc[...].astype(out_dtype)

    aspec = pl.BlockSpec((1, ts, K), lambda g, s: (g, s, 0)) if ga else pl.BlockSpec((ts, K), lambda g, s: (s, 0))
    bspec = pl.BlockSpec((1, ts, N), lambda g, s: (g, s, 0)) if gb else pl.BlockSpec((ts, N), lambda g, s: (s, 0))
    return pl.pallas_call(
        body, name=name, grid=(G, ns),
        in_specs=[aspec, bspec],
        out_specs=pl.BlockSpec((1, K, N), lambda g, s: (g, 0, 0)),
        out_shape=SDS((G, K, N), out_dtype),
        scratch_shapes=[pltpu.VMEM((K, N), F32)],
        compiler_params=_params(("parallel", "arbitrary")),
    )(a, b)


def loss_grad(xn, tgt):
    S, N = xn.shape
    ts = _tile(S, 512)

    def body(x_ref, t_ref, l_ref, dx_ref):
        @pl.when(pl.program_id(0) == 0)
        def _():
            l_ref[...] = jnp.zeros_like(l_ref)

        e = x_ref[...] - t_ref[...]
        dx_ref[...] = e * (1.0 / N)
        l_ref[...] += 0.5 * jnp.sum(jnp.mean(e * e, axis=-1, keepdims=True), axis=0, keepdims=True)

    row = pl.BlockSpec((ts, N), lambda s: (s, 0))
    return pl.pallas_call(
        body, name="loss_grad", grid=(S // ts,),
        in_specs=[row, row],
        out_specs=[pl.BlockSpec((1, 1), lambda s: (0, 0)), row],
        out_shape=[SDS((1, 1), F32), SDS((S, N), F32)],
        compiler_params=_params(("arbitrary",)),
    )(xn, tgt)


def _shift_down(x, k):
    if k == 0:
        return x
    rows = lax.broadcasted_iota(jnp.int32, x.shape, 0)
    return jnp.where(rows >= k, pltpu.roll(x, k, 0), 0.0)


def _shift_up(x, k):
    if k == 0:
        return x
    n = x.shape[0]
    rows = lax.broadcasted_iota(jnp.int32, x.shape, 0)
    return jnp.where(rows < n - k, pltpu.roll(x, n - k, 0), 0.0)


LANES = 128


def conv_mixer_fwd(u, cw):
    _, S, _ = u.shape
    nh = GROUP // LANES

    def body(b_ref, c_ref, x_ref, w_ref, o_ref):
        p = c_ref[0] * x_ref[0]
        w = w_ref[0]
        conv = w[2:3] * p + w[1:2] * _shift_down(p, 1) + w[0:1] * _shift_down(p, 2)
        o_ref[0] = (b_ref[0] * conv).astype(BF16)

    def uspec(off):
        return pl.BlockSpec((1, S, LANES), lambda g, h: (g + off, 0, h))

    return pl.pallas_call(
        body, name="conv_mixer_fwd", grid=(3, nh),
        in_specs=[uspec(0), uspec(3), uspec(6), pl.BlockSpec((1, 8, LANES), lambda g, h: (g, 0, h))],
        out_specs=pl.BlockSpec((1, S, LANES), lambda g, h: (g, 0, h)),
        out_shape=SDS((3, S, GROUP), BF16),
        compiler_params=_params(("parallel", "parallel")),
    )(u, u, u, cw)


def conv_mixer_bwd(u, cw, dm):
    _, S, _ = u.shape
    nh = GROUP // LANES

    def body(b_ref, c_ref, x_ref, w_ref, d_ref, db_ref, dc_ref, dx_ref, dw_ref):
        cg, xi = c_ref[0], x_ref[0]
        p = cg * xi
        p1, p2 = _shift_down(p, 1), _shift_down(p, 2)
        w = w_ref[0]
        conv = w[2:3] * p + w[1:2] * p1 + w[0:1] * p2
        dt = d_ref[0]
        db_ref[0] = (dt * conv).astype(BF16)
        dcv = dt * b_ref[0]
        dp = w[2:3] * dcv + w[1:2] * _shift_up(dcv, 1) + w[0:1] * _shift_up(dcv, 2)
        dc_ref[0] = (dp * xi).astype(BF16)
        dx_ref[0] = (dp * cg).astype(BF16)
        dw = jnp.concatenate([jnp.sum(dcv * p2, axis=0, keepdims=True), jnp.sum(dcv * p1, axis=0, keepdims=True),
                              jnp.sum(dcv * p, axis=0, keepdims=True), jnp.zeros((5, LANES), F32)], axis=0)
        dw_ref[0] = dw

    def uspec(off):
        return pl.BlockSpec((1, S, LANES), lambda g, h: (g + off, 0, h))

    ospec = pl.BlockSpec((1, S, LANES), lambda g, h: (g, 0, h))
    wspec = pl.BlockSpec((1, 8, LANES), lambda g, h: (g, 0, h))
    return pl.pallas_call(
        body, name="conv_mixer_bwd", grid=(3, nh),
        in_specs=[uspec(0), uspec(3), uspec(6), wspec, ospec],
        out_specs=[ospec, ospec, ospec, wspec],
        out_shape=[SDS((3, S, GROUP), BF16)] * 3 + [SDS((3, 8, GROUP), F32)],
        compiler_params=_params(("parallel", "parallel")),
    )(u, u, u, cw, dm)


def qk_conv_fwd(u, qw):
    _, S, _ = u.shape
    nh = GROUP // LANES

    def body(u_ref, w_ref, o_ref):
        x = u_ref[0]
        w = w_ref[0]
        pre = w[3:4] * x + w[2:3] * _shift_down(x, 1) + w[1:2] * _shift_down(x, 2) + w[0:1] * _shift_down(x, 3)
        o_ref[0] = pre * _sigmoid(pre)

    spec = pl.BlockSpec((1, S, LANES), lambda g, h: (g, 0, h))
    return pl.pallas_call(
        body, name="qk_conv_fwd", grid=(8, nh),
        in_specs=[spec, pl.BlockSpec((1, 8, LANES), lambda g, h: (g, 0, h))],
        out_specs=spec,
        out_shape=SDS((8, S, GROUP), F32),
        compiler_params=_params(("parallel", "parallel")),
    )(u, qw)


def qk_conv_bwd(u, qw, dqk):
    _, S, _ = u.shape
    nh = GROUP // LANES

    def body(u_ref, w_ref, d_ref, du_ref, dw_ref):
        x = u_ref[0]
        w = w_ref[0]
        x1, x2, x3 = _shift_down(x, 1), _shift_down(x, 2), _shift_down(x, 3)
        pre = w[3:4] * x + w[2:3] * x1 + w[1:2] * x2 + w[0:1] * x3
        sig = _sigmoid(pre)
        dpre = d_ref[0] * (sig * (1.0 + pre * (1.0 - sig)))
        du = w[3:4] * dpre + w[2:3] * _shift_up(dpre, 1) + w[1:2] * _shift_up(dpre, 2) + w[0:1] * _shift_up(dpre, 3)
        du_ref[0] = du.astype(BF16)
        dw = jnp.concatenate([jnp.sum(dpre * x3, axis=0, keepdims=True), jnp.sum(dpre * x2, axis=0, keepdims=True),
                              jnp.sum(dpre * x1, axis=0, keepdims=True), jnp.sum(dpre * x, axis=0, keepdims=True),
                              jnp.zeros((4, LANES), F32)], axis=0)
        dw_ref[0] = dw

    spec = pl.BlockSpec((1, S, LANES), lambda g, h: (g, 0, h))
    wspec = pl.BlockSpec((1, 8, LANES), lambda g, h: (g, 0, h))
    return pl.pallas_call(
        body, name="qk_conv_bwd", grid=(8, nh),
        in_specs=[spec, wspec, spec],
        out_specs=[spec, wspec],
        out_shape=[SDS((8, S, GROUP), BF16), SDS((8, 8, GROUP), F32)],
        compiler_params=_params(("parallel", "parallel")),
    )(u, qw, dqk)


def _head_masks():
    lane = lax.broadcasted_iota(jnp.int32, (1, D_XA), 1)
    return [(lane >= h * XA_HEAD_DIM) & (lane < (h + 1) * XA_HEAD_DIM) for h in range(XA_HEADS)]


def xattn_fwd(u, qg, kv):
    _, S, _ = u.shape
    ts = _tile(S, 512)
    scale = XA_HEAD_DIM ** -0.5

    def body(q_ref, kv_ref, o_ref):
        q = q_ref[0]
        k = kv_ref[0].astype(BF16)
        v = kv_ref[1]
        o = jnp.zeros((ts, D_XA), F32)
        for m in _head_masks():
            s = _dot(jnp.where(m, q, 0.0).astype(BF16), k, NT) * scale
            s = s - jnp.max(s, axis=-1, keepdims=True)
            e = jnp.exp(s)
            p = e / jnp.sum(e, axis=-1, keepdims=True)
            o = o + _dot(p.astype(BF16), jnp.where(m, v, 0.0).astype(BF16), NN)
        o_ref[0] = o.astype(BF16)

    return pl.pallas_call(
        body, name="xattn_fwd", grid=(S // ts,),
        in_specs=[pl.BlockSpec((1, ts, GROUP), lambda s: (qg, s, 0)), pl.BlockSpec((2, N_MEM, GROUP), lambda s: (0, 0, 0))],
        out_specs=pl.BlockSpec((1, ts, GROUP), lambda s: (0, s, 0)),
        out_shape=SDS((1, S, GROUP), BF16),
        compiler_params=_params(("parallel",)),
    )(u, kv)


def xattn_bwd(u, qg, kv, dm, dg):
    _, S, _ = u.shape
    ts = _tile(S, 512)
    scale = XA_HEAD_DIM ** -0.5

    def body(q_ref, kv_ref, do_ref, dq_ref, dkv_ref):
        @pl.when(pl.program_id(0) == 0)
        def _():
            dkv_ref[...] = jnp.zeros_like(dkv_ref)

        q = q_ref[0]
        k = kv_ref[0]
        v = kv_ref[1]
        kb = k.astype(BF16)
        do = do_ref[0]
        dq = jnp.zeros((ts, D_XA), F32)
        dk = jnp.zeros((N_MEM, D_XA), F32)
        dv = jnp.zeros((N_MEM, D_XA), F32)
        for m in _head_masks():
            qm = jnp.where(m, q, 0.0).astype(BF16)
            s = _dot(qm, kb, NT) * scale
            s = s - jnp.max(s, axis=-1, keepdims=True)
            e = jnp.exp(s)
            p = e / jnp.sum(e, axis=-1, keepdims=True)
            dom = jnp.where(m, do, 0.0).astype(BF16)
            dp = _dot(dom, jnp.where(m, v, 0.0).astype(BF16), NT)
            ds = (p * (dp - jnp.sum(dp * p, axis=-1, keepdims=True)) * scale).astype(BF16)
            dq = dq + _dot(ds, jnp.where(m, k, 0.0).astype(BF16), NN)
            dk = dk + _dot(ds, qm, TN)
            dv = dv + _dot(p.astype(BF16), dom, TN)
        dq_ref[0] = dq.astype(BF16)
        dkv_ref[0] += dk
        dkv_ref[1] += dv

    return pl.pallas_call(
        body, name="xattn_bwd", grid=(S // ts,),
        in_specs=[pl.BlockSpec((1, ts, GROUP), lambda s: (qg, s, 0)), pl.BlockSpec((2, N_MEM, GROUP), lambda s: (0, 0, 0)),
                  pl.BlockSpec((1, ts, GROUP), lambda s: (dg, s, 0))],
        out_specs=[pl.BlockSpec((1, ts, GROUP), lambda s: (0, s, 0)), pl.BlockSpec((2, N_MEM, GROUP), lambda s: (0, 0, 0))],
        out_shape=[SDS((1, S, GROUP), BF16), SDS((2, N_MEM, GROUP), F32)],
        compiler_params=_params(("arbitrary",)),
    )(u, kv, dm)


ML_BLOCK_CHUNKS = 4
H4 = ML_HEADS
L = ML_CHUNK
NLANE = ML_HEAD_DIM


def _chunk_consts():
    r = lax.broadcasted_iota(jnp.int32, (1, L, L), 1)
    c = lax.broadcasted_iota(jnp.int32, (1, L, L), 2)
    return r >= c, r <= c, r == c


def _gate_cols(gb):
    lane = lax.broadcasted_iota(jnp.int32, gb.shape, 1)
    li = jnp.stack([jnp.sum(jnp.where(lane == h, gb, 0.0), axis=1, keepdims=True) for h in range(H4)])
    gf = jnp.stack([jnp.sum(jnp.where(lane == H4 + h, gb, 0.0), axis=1, keepdims=True) for h in range(H4)])
    return li, gf


def _log_sigmoid(x):
    return jnp.minimum(x, 0.0) - jnp.log(1.0 + jnp.exp(-jnp.abs(x)))


def _chunk_forward(q, k, v_aug, li_col, lf_col, c_prev, m_prev):
    tri, tri_t, eye = _chunk_consts()
    lf_row = jnp.sum(jnp.where(eye, lf_col, 0.0), axis=1, keepdims=True)
    li_row = jnp.sum(jnp.where(eye, li_col, 0.0), axis=1, keepdims=True)
    bcum_col = jnp.sum(jnp.where(tri, lf_row, 0.0), axis=2, keepdims=True)
    bcum_row = jnp.sum(jnp.where(tri_t, lf_col, 0.0), axis=1, keepdims=True)
    log_d = jnp.where(tri, bcum_col - bcum_row + li_row, NEG)
    log_inter = bcum_col + m_prev
    m_t = jnp.maximum(log_inter, jnp.max(log_d, axis=2, keepdims=True))
    w_intra = jnp.exp(log_d - m_t)
    w_inter = jnp.exp(log_inter - m_t)
    sc = _bdot(q, k, 2, 2) * w_intra
    qc = _bdot(q, c_prev, 2, 1)
    num = _bdot(sc, v_aug, 2, 1) + w_inter * qc
    lane = lax.broadcasted_iota(jnp.int32, num.shape, 2)
    den = jnp.sum(jnp.where(lane == NLANE, num, 0.0), axis=2, keepdims=True)
    e_m = jnp.exp(-m_t)
    b_last = jnp.sum(lf_row, axis=2, keepdims=True)
    log_w = b_last - bcum_col + li_col
    m_new = jnp.maximum(b_last + m_prev, jnp.max(log_w, axis=1, keepdims=True))
    w_k = jnp.exp(log_w - m_new)
    decay = jnp.exp(b_last + m_prev - m_new)
    return dict(w_intra=w_intra, w_inter=w_inter, sc=sc, qc=qc, num=num, den=den, e_m=e_m, lane=lane,
                w_k=w_k, decay=decay, m_new=m_new)


def mlstm_fwd(qk, u, bg):
    _, S, _ = qk.shape
    nc = S // L
    cb = min(ML_BLOCK_CHUNKS, nc)
    rows = cb * L
    kscale = ML_HEAD_DIM ** -0.5

    def body(qk_ref, v_ref, g_ref, bg_ref, h_ref, cst_ref, mst_ref, c_sc, m_sc):
        @pl.when(pl.program_id(0) == 0)
        def _():
            c_sc[...] = jnp.zeros_like(c_sc)
            m_sc[...] = jnp.zeros_like(m_sc)

        for c in range(cb):
            sl = pl.ds(c * L, L)
            q = qk_ref[0:H4, sl, :]
            k = qk_ref[H4:2 * H4, sl, :] * kscale
            v = v_ref[:, sl, :]
            lane = lax.broadcasted_iota(jnp.int32, v.shape, 2)
            v_aug = jnp.where(lane == NLANE, 1.0, v)
            li_col, gf = _gate_cols(g_ref[0, sl, :] + bg_ref[...])
            lf_col = _log_sigmoid(gf)
            c_prev = c_sc[...]
            m_prev = m_sc[...]
            f = _chunk_forward(q, k, v_aug, li_col, lf_col, c_prev, m_prev)
            r = 1.0 / jnp.maximum(jnp.abs(f["den"]), f["e_m"])
            h_ref[:, sl, :] = jnp.where(lane < NLANE, f["num"] * r, 0.0)
            cst_ref[c] = c_prev
            mst_ref[c] = jnp.broadcast_to(m_prev, (H4, 1, LANES))
            c_sc[...] = f["decay"] * c_prev + _bdot(k * f["w_k"], v_aug, 1, 1)
            m_sc[...] = f["m_new"]

    def hspec(blk):
        return pl.BlockSpec((H4, rows, GROUP), lambda i: (blk, i, 0))

    return pl.pallas_call(
        body, name="mlstm_fwd", grid=(nc // cb,),
        in_specs=[pl.BlockSpec((2 * H4, rows, GROUP), lambda i: (0, i, 0)), hspec(2),
                  pl.BlockSpec((1, rows, GROUP), lambda i: (17, i, 0)), pl.BlockSpec((1, GROUP), lambda i: (0, 0))],
        out_specs=[hspec(0), pl.BlockSpec((cb, H4, GROUP, GROUP), lambda i: (i, 0, 0, 0)),
                   pl.BlockSpec((cb, H4, 1, LANES), lambda i: (i, 0, 0, 0))],
        out_shape=[SDS((H4, S, GROUP), F32), SDS((nc, H4, GROUP, GROUP), F32), SDS((nc, H4, 1, LANES), F32)],
        scratch_shapes=[pltpu.VMEM((H4, GROUP, GROUP), F32), pltpu.VMEM((H4, 1, 1), F32)],
        compiler_params=_params(("arbitrary",)),
    )(qk, u, u, bg)


def mlstm_bwd(qk, u, bg, cst, mst, dh):
    _, S, _ = qk.shape
    nc = S // L
    cb = min(ML_BLOCK_CHUNKS, nc)
    rows = cb * L
    nb = nc // cb
    kscale = ML_HEAD_DIM ** -0.5

    def body(qk_ref, v_ref, g_ref, bg_ref, cst_ref, mst_ref, dh_ref, dqk_ref, dv_ref, dg_ref, dbg_ref, dc_sc):
        @pl.when(pl.program_id(0) == 0)
        def _():
            dc_sc[...] = jnp.zeros_like(dc_sc)
            dbg_ref[...] = jnp.zeros_like(dbg_ref)

        tri, tri_t, eye = _chunk_consts()
        for c in reversed(range(cb)):
            sl = pl.ds(c * L, L)
            q = qk_ref[0:H4, sl, :]
            k = qk_ref[H4:2 * H4, sl, :] * kscale
            v = v_ref[:, sl, :]
            lane = lax.broadcasted_iota(jnp.int32, v.shape, 2)
            v_aug = jnp.where(lane == NLANE, 1.0, v)
            li_col, gf = _gate_cols(g_ref[0, sl, :] + bg_ref[...])
            lf_col = _log_sigmoid(gf)
            c_prev = cst_ref[c]
            m_prev = mst_ref[c][:, :, 0:1]
            f = _chunk_forward(q, k, v_aug, li_col, lf_col, c_prev, m_prev)
            w_intra, w_inter, sc, num, den, e_m = f["w_intra"], f["w_inter"], f["sc"], f["num"], f["den"], f["e_m"]
            absd = jnp.abs(den)
            r = 1.0 / jnp.maximum(absd, e_m)
            dhv = dh_ref[:, sl, :]
            s1 = jnp.sum(jnp.where(lane < NLANE, dhv * num, 0.0), axis=2, keepdims=True)
            dden = jnp.where(absd > e_m, -s1 * r * r * jnp.sign(den), 0.0)
            dnum = jnp.where(lane == NLANE, dden, jnp.where(lane < NLANE, dhv * r, 0.0))
            dsc = _bdot(dnum, v_aug, 2, 2)
            dv = _bdot(sc, dnum, 1, 1)
            gmat = dsc * sc
            dqk = dsc * w_intra
            dq = _bdot(dqk, k, 2, 1) + w_inter * _bdot(dnum, c_prev, 2, 2)
            dk = _bdot(dqk, q, 1, 1)
            dc_prev = _bdot(q * w_inter, dnum, 1, 1)
            dlog_inter = jnp.sum(dnum * f["qc"], axis=2, keepdims=True) * w_inter
            dbcum_col = dlog_inter + jnp.sum(gmat, axis=2, keepdims=True)
            g_row = jnp.sum(gmat, axis=1, keepdims=True)
            dcn = dc_sc[...]
            w_k, decay = f["w_k"], f["decay"]
            kw = k * w_k
            dc_prev = dc_prev + decay * dcn
            db_last = jnp.sum(jnp.sum(dcn * c_prev, axis=2, keepdims=True), axis=1, keepdims=True) * decay
            dkw = _bdot(v_aug, dcn, 2, 2)
            dv = dv + _bdot(kw, dcn, 2, 1)
            dk = dk + dkw * w_k
            dlogw = jnp.sum(dkw * k, axis=2, keepdims=True) * w_k
            db_last = db_last + jnp.sum(dlogw, axis=1, keepdims=True)
            dbcum_col = dbcum_col - dlogw
            rowi = lax.broadcasted_iota(jnp.int32, (1, L, 1), 1)
            dbcum_col = dbcum_col + jnp.where(rowi == L - 1, db_last, 0.0)
            dbcum_row = jnp.sum(jnp.where(eye, dbcum_col, 0.0), axis=1, keepdims=True) - g_row
            dlf_col = jnp.sum(jnp.where(tri_t, dbcum_row, 0.0), axis=2, keepdims=True)
            dli_col = dlogw + jnp.sum(jnp.where(eye, g_row, 0.0), axis=2, keepdims=True)
            dgf_col = dlf_col * _sigmoid(-gf)
            lane_g = lax.broadcasted_iota(jnp.int32, (L, GROUP), 1)
            dg = jnp.zeros((L, GROUP), F32)
            for h in range(H4):
                dg = dg + jnp.where(lane_g == h, dli_col[h], 0.0) + jnp.where(lane_g == H4 + h, dgf_col[h], 0.0)
            dqk_ref[0:H4, sl, :] = dq
            dqk_ref[H4:2 * H4, sl, :] = dk * kscale
            dv_ref[:, sl, :] = jnp.where(lane < NLANE, dv, 0.0).astype(BF16)
            dg_ref[0, sl, :] = dg.astype(BF16)
            dbg_ref[...] += jnp.sum(dg, axis=0, keepdims=True)
            dc_sc[...] = dc_prev

    def hspec(blk):
        return pl.BlockSpec((H4, rows, GROUP), lambda i: (blk, nb - 1 - i, 0))

    gspec = pl.BlockSpec((1, rows, GROUP), lambda i: (17, nb - 1 - i, 0))
    qkspec = pl.BlockSpec((2 * H4, rows, GROUP), lambda i: (0, nb - 1 - i, 0))
    return pl.pallas_call(
        body, name="mlstm_bwd", grid=(nb,),
        in_specs=[qkspec, hspec(2), gspec, pl.BlockSpec((1, GROUP), lambda i: (0, 0)),
                  pl.BlockSpec((cb, H4, GROUP, GROUP), lambda i: (nb - 1 - i, 0, 0, 0)),
                  pl.BlockSpec((cb, H4, 1, LANES), lambda i: (nb - 1 - i, 0, 0, 0)), hspec(0)],
        out_specs=[qkspec, hspec(0), pl.BlockSpec((1, rows, GROUP), lambda i: (0, nb - 1 - i, 0)),
                   pl.BlockSpec((1, GROUP), lambda i: (0, 0))],
        out_shape=[SDS((2 * H4, S, GROUP), F32), SDS((H4, S, GROUP), BF16),
                   SDS((1, S, GROUP), BF16), SDS((1, GROUP), F32)],
        scratch_shapes=[pltpu.VMEM((H4, GROUP, GROUP), F32)],
        compiler_params=_params(("arbitrary",)),
    )(qk, u, u, bg, cst, mst, dh)


def head_norm_fwd(hm, u, hg):
    _, S, _ = hm.shape
    ts = _tile(S, 512)

    def body(h_ref, o_ref, g_ref, t_ref):
        h = h_ref[0]
        lane = lax.broadcasted_iota(jnp.int32, h.shape, 1)
        valid = lane < ML_HEAD_DIM
        mu = jnp.sum(h, axis=-1, keepdims=True) * (1.0 / ML_HEAD_DIM)
        hc = jnp.where(valid, h - mu, 0.0)
        var = jnp.sum(hc * hc, axis=-1, keepdims=True) * (1.0 / ML_HEAD_DIM)
        hn = hc * lax.rsqrt(var + LN_EPS) * g_ref[0]
        t_ref[0] = (_sigmoid(o_ref[0]) * hn).astype(BF16)

    return pl.pallas_call(
        body, name="head_norm_fwd", grid=(H4, S // ts),
        in_specs=[pl.BlockSpec((1, ts, GROUP), lambda h, s: (h, s, 0)), pl.BlockSpec((1, ts, GROUP), lambda h, s: (12 + h, s, 0)),
                  pl.BlockSpec((1, 1, GROUP), lambda h, s: (h, 0, 0))],
        out_specs=pl.BlockSpec((1, ts, GROUP), lambda h, s: (h, s, 0)),
        out_shape=SDS((H4, S, GROUP), BF16),
        compiler_params=_params(("parallel", "parallel")),
    )(hm, u, hg)


def head_norm_bwd(hm, u, hg, dm):
    _, S, _ = hm.shape
    ts = _tile(S, 512)

    def body(h_ref, o_ref, g_ref, d_ref, dh_ref, do_ref, dg_ref):
        @pl.when(pl.program_id(1) == 0)
        def _():
            dg_ref[...] = jnp.zeros_like(dg_ref)

        h = h_ref[0]
        lane = lax.broadcasted_iota(jnp.int32, h.shape, 1)
        valid = lane < ML_HEAD_DIM
        inv = 1.0 / ML_HEAD_DIM
        mu = jnp.sum(h, axis=-1, keepdims=True) * inv
        hc = jnp.where(valid, h - mu, 0.0)
        var = jnp.sum(hc * hc, axis=-1, keepdims=True) * inv
        rstd = lax.rsqrt(var + LN_EPS)
        xhat = hc * rstd
        g = g_ref[0]
        sig = _sigmoid(o_ref[0])
        dt = jnp.where(valid, d_ref[0], 0.0)
        do_ref[0] = (dt * xhat * g * sig * (1.0 - sig)).astype(BF16)
        dhn = dt * sig
        dg_ref[0] += jnp.sum(dhn * xhat, axis=0, keepdims=True)
        dxh = dhn * g
        m1 = jnp.sum(dxh, axis=-1, keepdims=True) * inv
        m2 = jnp.sum(dxh * xhat, axis=-1, keepdims=True) * inv
        dh_ref[0] = jnp.where(valid, rstd * (dxh - m1 - xhat * m2), 0.0)

    spec = pl.BlockSpec((1, ts, GROUP), lambda h, s: (h, s, 0))
    gspec = pl.BlockSpec((1, 1, GROUP), lambda h, s: (h, 0, 0))
    return pl.pallas_call(
        body, name="head_norm_bwd", grid=(H4, S // ts),
        in_specs=[spec, pl.BlockSpec((1, ts, GROUP), lambda h, s: (12 + h, s, 0)), gspec, spec],
        out_specs=[spec, spec, gspec],
        out_shape=[SDS((H4, S, GROUP), F32), SDS((H4, S, GROUP), BF16), SDS((H4, 1, GROUP), F32)],
        compiler_params=_params(("parallel", "arbitrary")),
    )(hm, u, hg, dm)


def adamw(w, g, m, v, name):
    R, C = w.shape
    tr = R if R <= 512 else next(d for d in (512, 256, 128, 64, 32, 16, 8) if R % d == 0)
    c1 = 1.0 / (1.0 - ADAM_B1 ** ADAM_STEP)
    c2 = 1.0 / (1.0 - ADAM_B2 ** ADAM_STEP)

    def body(w_ref, g_ref, m_ref, v_ref, d_ref, nm_ref, nv_ref):
        gv = g_ref[...]
        nm = ADAM_B1 * m_ref[...] + (1.0 - ADAM_B1) * gv
        nv = ADAM_B2 * v_ref[...] + (1.0 - ADAM_B2) * (gv * gv)
        d_ref[...] = -ADAM_LR * ((nm * c1) / (jnp.sqrt(nv * c2) + ADAM_EPS) + ADAM_WD * w_ref[...])
        nm_ref[...] = nm
        nv_ref[...] = nv

    spec = pl.BlockSpec((tr, C), lambda i: (i, 0))
    return pl.pallas_call(
        body, name=name, grid=(R // tr,),
        in_specs=[spec] * 4, out_specs=[spec] * 3,
        out_shape=[SDS((R, C), F32)] * 3,
        compiler_params=_params(("parallel",)),
    )(w, g, m, v)


HBM = pl.BlockSpec(memory_space=pl.ANY)
ROW_SPLIT = 4


def _position():
    x, y, c = lax.axis_index("x"), lax.axis_index("y"), lax.axis_index("c")
    return x, y, c, [(1 - x, y), (x, 1 - y), (1 - x, 1 - y)]


def _unique(items):
    arrays = []
    for a, _ in items:
        if not any(a is b for b in arrays):
            arrays.append(a)
    return arrays, [next(i for i, b in enumerate(arrays) if b is a) for a, _ in items]


def place_own(items, me):
    arrays, src_of = _unique(items)
    n = len(items)
    shapes = [a.shape[len(p):] for a, p in items]

    def body(me_ref, *refs):
        for t in range(n):
            refs[n + t][0] = refs[t][(0,) * len(items[t][1])]

    in_specs, out_specs = [], []
    for (a, p), shp in zip(items, shapes):
        blk = shp[:-2] + (shp[-2] // ROW_SPLIT, shp[-1])
        lead = (0,) * (len(shp) - 2)
        in_specs.append(pl.BlockSpec((1,) * len(p) + blk, functools.partial(lambda r, me_ref, p, lead: p + lead + (r, 0), p=p, lead=lead)))
        out_specs.append(pl.BlockSpec((1,) + blk, functools.partial(lambda r, me_ref, lead: (me_ref[0],) + lead + (r, 0), lead=lead)))
    return pl.pallas_call(
        body, name="place_own",
        grid_spec=pltpu.PrefetchScalarGridSpec(num_scalar_prefetch=1, grid=(ROW_SPLIT,), in_specs=in_specs, out_specs=out_specs),
        out_shape=[SDS((N_CHIPS,) + tuple(shp), a.dtype) for shp, (a, _) in zip(shapes, items)],
        compiler_params=_params(("parallel",)),
    )(me, *[arrays[i] for i in src_of])


def gather_shards(items, placed):
    arrays, src_of = _unique(items)
    n_in, n = len(arrays), len(items)
    shapes = [a.shape[len(p):] for a, p in items]

    def body(*refs):
        ins, outs = refs[:n_in], refs[n_in + n:n_in + 2 * n]
        send, recv, fsend, frecv = refs[n_in + 2 * n:]
        x, y, c, chips = _position()
        me = 2 * x + y
        started = []
        for t, (_, prefix) in enumerate(items):
            src = ins[src_of[t]].at[prefix] if prefix else ins[src_of[t]]
            half = shapes[t][0] // 2
            for j, (cx, cy) in enumerate(chips):
                cp = pltpu.make_async_remote_copy(
                    src_ref=src.at[pl.ds(c * half, half)], dst_ref=outs[t].at[me, pl.ds(c * half, half)],
                    send_sem=send.at[3 * t + j], recv_sem=recv.at[3 * t + j], device_id=(cx, cy, c), device_id_type=MESH)
                cp.start()
                started.append(cp)
        for t in range(n):
            half = shapes[t][0] // 2
            for j, (cx, cy) in enumerate(chips):
                piece = outs[t].at[2 * cx + cy, pl.ds(c * half, half)]
                pltpu.make_async_remote_copy(src_ref=piece, dst_ref=piece, send_sem=send.at[3 * t + j], recv_sem=recv.at[3 * t + j],
                                             device_id=(cx, cy, c), device_id_type=MESH).wait_recv()
                fw = pltpu.make_async_remote_copy(src_ref=piece, dst_ref=piece, send_sem=fsend.at[3 * t + j],
                                                  recv_sem=frecv.at[3 * t + j], device_id=(x, y, 1 - c), device_id_type=MESH)
                fw.start()
                started.append(fw)
        for t in range(n):
            half = shapes[t][0] // 2
            for j, (cx, cy) in enumerate(chips):
                piece = outs[t].at[2 * cx + cy, pl.ds((1 - c) * half, half)]
                pltpu.make_async_remote_copy(src_ref=piece, dst_ref=piece, send_sem=fsend.at[3 * t + j], recv_sem=frecv.at[3 * t + j],
                                             device_id=(x, y, 1 - c), device_id_type=MESH).wait_recv()
        for cp in started:
            cp.wait_send()

    return pl.pallas_call(
        body, name="gather_shards",
        in_specs=[HBM] * (n_in + n), out_specs=[HBM] * n,
        out_shape=[SDS(p.shape, p.dtype) for p in placed],
        input_output_aliases={n_in + t: t for t in range(n)},
        scratch_shapes=[pltpu.SemaphoreType.DMA((3 * n,))] * 4,
    )(*arrays, *placed)


def _flip(k, x, y, c):
    return ((1 - x) if k & 4 else x, (1 - y) if k & 2 else y, (1 - c) if k & 1 else c)


def small_allgather(v, reduce):
    R, C = v.shape

    def body(v_ref, o_ref, *scratch):
        if reduce:
            buf, send, recv = scratch
        else:
            buf, (send, recv) = o_ref, scratch
        x, y, c, _ = _position()
        me = 4 * x + 2 * y + c
        buf[me] = v_ref[...]
        sends = []
        for k in range(1, N_DEV):
            cp = pltpu.make_async_remote_copy(src_ref=v_ref, dst_ref=buf.at[me], send_sem=send.at[k - 1], recv_sem=recv.at[k - 1],
                                              device_id=_flip(k, x, y, c), device_id_type=MESH)
            cp.start()
            sends.append(cp)
        for k in range(1, N_DEV):
            px, py, pc = _flip(k, x, y, c)
            pltpu.make_async_remote_copy(src_ref=v_ref, dst_ref=buf.at[4 * px + 2 * py + pc], send_sem=send.at[k - 1],
                                         recv_sem=recv.at[k - 1], device_id=(px, py, pc), device_id_type=MESH).wait_recv()
        for cp in sends:
            cp.wait_send()
        if reduce:
            acc = buf[0]
            for i in range(1, N_DEV):
                acc = acc + buf[i]
            o_ref[...] = acc

    vm = pl.BlockSpec(memory_space=pltpu.VMEM)
    sems = [pltpu.SemaphoreType.DMA((N_DEV - 1,)), pltpu.SemaphoreType.DMA((N_DEV - 1,))]
    return pl.pallas_call(
        body, name="small_allreduce" if reduce else "small_allgather",
        in_specs=[vm], out_specs=vm,
        out_shape=SDS((R, C) if reduce else (N_DEV, R, C), F32),
        scratch_shapes=([pltpu.VMEM((N_DEV, R, C), F32)] if reduce else []) + sems,
    )(v)


def rs_exchange_sibling(gs):
    n = len(gs)

    def body(*refs):
        ins, outs, send, recv = refs[:n], refs[n:2 * n], refs[2 * n], refs[2 * n + 1]
        x, y, c, _ = _position()
        cps = []
        for t in range(n):
            cp = pltpu.make_async_remote_copy(src_ref=ins[t].at[:, 1 - c], dst_ref=outs[t], send_sem=send.at[t], recv_sem=recv.at[t],
                                              device_id=(x, y, 1 - c), device_id_type=MESH)
            cp.start()
            cps.append(cp)
        for cp in cps:
            cp.wait()

    return pl.pallas_call(
        body, name="rs_exchange_sibling", in_specs=[HBM] * n, out_specs=[HBM] * n,
        out_shape=[SDS((g.shape[0],) + g.shape[2:], g.dtype) for g in gs],
        scratch_shapes=[pltpu.SemaphoreType.DMA((n,)), pltpu.SemaphoreType.DMA((n,))],
    )(*gs)


def rs_pair_add(gs, rs, c):
    n = len(gs)

    def body(c_ref, *refs):
        for t in range(n):
            refs[2 * n + t][0] = (refs[t][0, 0].astype(F32) + refs[n + t][0].astype(F32)).astype(BF16)

    in_specs, out_specs, out_shape = [], [], []
    for g in gs:
        _, _, h, C = g.shape
        in_specs.append(pl.BlockSpec((1, 1, h // ROW_SPLIT, C), lambda j, r, c_ref: (j, c_ref[0], r, 0)))
    for g in gs:
        _, _, h, C = g.shape
        spec = pl.BlockSpec((1, h // ROW_SPLIT, C), lambda j, r, c_ref: (j, r, 0))
        in_specs.append(spec)
        out_specs.append(spec)
        out_shape.append(SDS((N_CHIPS, h, C), BF16))
    return pl.pallas_call(
        body, name="rs_pair_add",
        grid_spec=pltpu.PrefetchScalarGridSpec(num_scalar_prefetch=1, grid=(N_CHIPS, ROW_SPLIT), in_specs=in_specs, out_specs=out_specs),
        out_shape=out_shape, compiler_params=_params(("parallel", "parallel")),
    )(c, *gs, *rs)


def rs_exchange_chips(ps):
    n = len(ps)

    def body(*refs):
        ins, outs, send, recv = refs[:n], refs[n:2 * n], refs[2 * n], refs[2 * n + 1]
        x, y, c, chips = _position()
        cps = []
        for t in range(n):
            for j, (cx, cy) in enumerate(chips):
                cp = pltpu.make_async_remote_copy(src_ref=ins[t].at[2 * cx + cy], dst_ref=outs[t].at[j], send_sem=send.at[3 * t + j],
                                                  recv_sem=recv.at[3 * t + j], device_id=(cx, cy, c), device_id_type=MESH)
                cp.start()
                cps.append(cp)
        for cp in cps:
            cp.wait()

    return pl.pallas_call(
        body, name="rs_exchange_chips", in_specs=[HBM] * n, out_specs=[HBM] * n,
        out_shape=[SDS((3,) + p.shape[1:], p.dtype) for p in ps],
        scratch_shapes=[pltpu.SemaphoreType.DMA((3 * n,)), pltpu.SemaphoreType.DMA((3 * n,))],
    )(*ps)


def rs_chip_add(ps, qs, me_c):
    n = len(ps)

    def body(me_ref, *refs):
        for t in range(n):
            q = refs[n + t]
            refs[2 * n + t][0] = ((refs[t][0].astype(F32) + q[0].astype(F32)) + q[1].astype(F32)) + q[2].astype(F32)

    in_specs, out_specs, out_shape = [], [], []
    for p in ps:
        _, h, C = p.shape
        in_specs.append(pl.BlockSpec((1, h // ROW_SPLIT, C), lambda r, me_ref: (me_ref[0], r, 0)))
    for p in ps:
        _, h, C = p.shape
        in_specs.append(pl.BlockSpec((3, h // ROW_SPLIT, C), lambda r, me_ref: (0, r, 0)))
        out_specs.append(pl.BlockSpec((1, h // ROW_SPLIT, C), lambda r, me_ref: (me_ref[1], r, 0)))
        out_shape.append(SDS((2, h, C), F32))
    return pl.pallas_call(
        body, name="rs_chip_add",
        grid_spec=pltpu.PrefetchScalarGridSpec(num_scalar_prefetch=1, grid=(ROW_SPLIT,), in_specs=in_specs, out_specs=out_specs),
        out_shape=out_shape, compiler_params=_params(("parallel",)),
    )(me_c, *ps, *qs)


def rs_share(rs):
    n = len(rs)

    def body(*refs):
        outs, send, recv = refs[n:2 * n], refs[2 * n], refs[2 * n + 1]
        x, y, c, _ = _position()
        cps = []
        for t in range(n):
            cp = pltpu.make_async_remote_copy(src_ref=outs[t].at[c], dst_ref=outs[t].at[c], send_sem=send.at[t], recv_sem=recv.at[t],
                                              device_id=(x, y, 1 - c), device_id_type=MESH)
            cp.start()
            cps.append(cp)
        for cp in cps:
            cp.wait()

    return pl.pallas_call(
        body, name="rs_share", in_specs=[HBM] * n, out_specs=[HBM] * n,
        out_shape=[SDS(r.shape, r.dtype) for r in rs],
        input_output_aliases={t: t for t in range(n)},
        scratch_shapes=[pltpu.SemaphoreType.DMA((n,))] * 2,
    )(*rs)


def reduce_scatter(gs):
    x, y, c = lax.axis_index("x"), lax.axis_index("y"), lax.axis_index("c")
    g5 = [g.reshape(N_CHIPS, 2, g.shape[1] // 2, g.shape[2]) for g in gs]
    from_sibling = rs_exchange_sibling(g5)
    pair = rs_pair_add(g5, from_sibling, jnp.reshape(c, (1,)).astype(jnp.int32))
    from_chips = rs_exchange_chips(pair)
    half = rs_chip_add(pair, from_chips, jnp.stack([2 * x + y, c]).astype(jnp.int32))
    both = rs_share(half)
    return [b.reshape(g.shape[1], g.shape[2]) for b, g in zip(both, gs)]


def _pad_last(a, n):
    return jnp.pad(a, [(0, 0)] * (a.ndim - 1) + [(0, n - a.shape[-1])])


def _heads_to_groups(w):
    k = w.shape[0]
    return _pad_last(w.reshape(k, ML_HEADS, ML_HEAD_DIM).transpose(1, 0, 2), GROUP)


def _groups_to_heads(g):
    return g[:, :, :ML_HEAD_DIM].transpose(1, 0, 2).reshape(g.shape[1], D_TOK)


def _cols_to_groups(w):
    k, n = w.shape
    return w.reshape(k, n // GROUP, GROUP).transpose(1, 0, 2)


def _groups_to_cols(g):
    n, k, _ = g.shape
    return g.transpose(1, 0, 2).reshape(k, n * GROUP)


def _chips_to_cols(a):
    return a.transpose(1, 0, 2).reshape(a.shape[1], -1)


def _cols_to_chips(w):
    k, n = w.shape
    return w.reshape(k, N_CHIPS, n // N_CHIPS).transpose(1, 0, 2)


def _mlstm_in_groups(w):
    parts = [_heads_to_groups(w[:, i * D_TOK:(i + 1) * D_TOK]) for i in range(4)]
    gates = _pad_last(w[:, 4 * D_TOK:4 * D_TOK + 2 * ML_HEADS], GROUP)[None]
    qmem = w[:, 4 * D_TOK + 2 * ML_HEADS:][None]
    return jnp.concatenate(parts + [qmem, gates], axis=0)


def _mlstm_in_ungroup(g):
    parts = [_groups_to_heads(g[4 * i:4 * i + 4]) for i in range(4)]
    return jnp.concatenate(parts + [g[17][:, :2 * ML_HEADS], g[16]], axis=1)


def _taps_to_groups(w, width):
    taps = w.shape[0]
    g = _pad_last(w.reshape(taps, -1, width), GROUP).transpose(1, 0, 2)
    return jnp.pad(g, ((0, 0), (0, 8 - taps), (0, 0)))


def _groups_to_taps(g, taps, width):
    return g[:, :taps, :width].transpose(1, 0, 2).reshape(taps, -1)


SMALL_ROWS = 24
SMALL_IN_COLS = 384
SMALL_OUT_COLS = 1536


def _local_step(x, mem, tgt, P):
    xb = x.astype(BF16)
    memb = mem.astype(BF16)
    saved = []
    X, Xb = x, xb
    for l in range(DEPTH):
        s = {}
        s["x0b"] = Xb
        s["g1a"], s["u1a"], s["ha"] = ffn_up(Xb, P["wg"][l][0], P["wu"][l][0])
        s["z1"], X1, X1b = contract_ln(s["ha"], P["wd"][l][0], X, P["ln_g"][l][0], P["ln_b"][l][0], 0.5, "ffn_down_ln")
        s["x1b"] = X1b
        u = proj(X1b, P["win"][l], "mixer_in")
        kv = proj(memb, P["wkv"][l], "mem_kv")
        s["u"], s["kv"] = u, kv
        if l % 2 == 0:
            tok = conv_mixer_fwd(u, P["convw"])
            qg = 9
        else:
            s["qk"] = qk_conv_fwd(u, P["qkw"])
            s["hm"], s["cst"], s["mst"] = mlstm_fwd(s["qk"], u, P["bg"])
            tok = head_norm_fwd(s["hm"], u, P["hg"])
            qg = 16
        xa = xattn_fwd(u, qg, kv)
        s["m"] = jnp.concatenate([tok, xa], axis=0)
        s["z2"], X2, X2b = contract_ln(s["m"], P["wout"][l], X1, P["ln_g"][l][1], P["ln_b"][l][1], 1.0, "mixer_out_ln")
        s["x2b"] = X2b
        s["g1b"], s["u1b"], s["hb"] = ffn_up(X2b, P["wg"][l][1], P["wu"][l][1])
        s["z3"], X, Xb = contract_ln(s["hb"], P["wd"][l][1], X2, P["ln_g"][l][2], P["ln_b"][l][2], 0.5, "ffn_down_ln")
        saved.append(s)

    loss, dX = loss_grad(X, tgt)

    G = {"ln_g": [[None] * 3 for _ in range(DEPTH)], "ln_b": [[None] * 3 for _ in range(DEPTH)],
         "wg": [[None, None] for _ in range(DEPTH)], "wu": [[None, None] for _ in range(DEPTH)],
         "wd": [[None, None] for _ in range(DEPTH)], "wkv": [None] * DEPTH, "wout": [None] * DEPTH, "win": [None] * DEPTH}

    def ffn_backward(l, i, dX, z, xinb, g1, u1, h):
        dz, dyb, G["ln_g"][l][2 * i], G["ln_b"][l][2 * i] = ln_bwd(dX, z, P["ln_g"][l][2 * i], 0.5, "ffn_ln_bwd")
        dgb, dub = ffn_bwd_dh(dyb, P["wd"][l][i], g1, u1)
        G["wd"][l][i] = wgrad(h, dyb, BF16, "wgrad_down")
        G["wg"][l][i] = wgrad(xinb, dgb, BF16, "wgrad_gate")
        G["wu"][l][i] = wgrad(xinb, dub, BF16, "wgrad_up")
        return contract_t([(dgb, P["wg"][l][i]), (dub, P["wu"][l][i])], dz, "ffn_bwd_dx")

    for l in reversed(range(DEPTH)):
        s = saved[l]
        dX = ffn_backward(l, 1, dX, s["z3"], s["x2b"], s["g1b"], s["u1b"], s["hb"])
        dz2, dz2b, G["ln_g"][l][1], G["ln_b"][l][1] = ln_bwd(dX, s["z2"], P["ln_g"][l][1], 1.0, "mixer_ln_bwd")
        dm = proj_t(dz2b, P["wout"][l], "mixer_out_bwd")
        G["wout"][l] = wgrad(s["m"], dz2b, BF16, "wgrad_out")
        u, kv = s["u"], s["kv"]
        if l % 2 == 0:
            db, dc, dxi, G["convw"] = conv_mixer_bwd(u, P["convw"], dm)
            dq, dkv = xattn_bwd(u, 9, kv, dm, 3)
            du = jnp.concatenate([db, dc, dxi, dq], axis=0)
        else:
            dh, do, G["hg"] = head_norm_bwd(s["hm"], u, P["hg"], dm)
            dqk, dv, dgate, G["bg"] = mlstm_bwd(s["qk"], u, P["bg"], s["cst"], s["mst"], dh)
            duqk, G["qkw"] = qk_conv_bwd(u, P["qkw"], dqk)
            dq, dkv = xattn_bwd(u, 16, kv, dm, 4)
            du = jnp.concatenate([duqk, dv, do, dq, dgate], axis=0)
        G["win"][l] = wgrad(s["x1b"], du, BF16, "wgrad_in")
        G["wkv"][l] = wgrad(memb, dkv.astype(BF16), BF16, "wgrad_kv")
        dX = contract_t([(du, P["win"][l])], dz2, "mixer_in_bwd")
        dX = ffn_backward(l, 0, dX, s["z1"], s["x0b"], s["g1a"], s["u1a"], s["ha"])
    return loss, dX, G


def kernel(x, mem, ln_g, ln_b, ffn_w_gate, ffn_w_up, ffn_w_down, w_kv_mem, w_out, w_in_conv, conv_w, w_in_mlstm, b_gates, qk_conv_w, head_norm_g, loss_target, m_ln_g, m_ln_b, m_ffn_w_gate, m_ffn_w_up, m_ffn_w_down, m_w_kv_mem, m_w_out, m_w_in_conv, m_conv_w, m_w_in_mlstm, m_b_gates, m_qk_conv_w, m_head_norm_g, v_ln_g, v_ln_b, v_ffn_w_gate, v_ffn_w_up, v_ffn_w_down, v_w_kv_mem, v_w_out, v_w_in_conv, v_conv_w, v_w_in_mlstm, v_b_gates, v_qk_conv_w, v_head_norm_g):
    cx, cy = lax.axis_index("x"), lax.axis_index("y")
    chip = 2 * cx + cy

    big = {"wg": ffn_w_gate.astype(BF16), "wu": ffn_w_up.astype(BF16), "wd": ffn_w_down.astype(BF16),
           "wkv": w_kv_mem.astype(BF16), "wout": w_out.astype(BF16), "winc": w_in_conv.astype(BF16), "winm": w_in_mlstm.astype(BF16)}
    items, where = [], {}
    for name in ("wg", "wu", "wd"):
        for l in range(DEPTH):
            for i in range(2):
                where[(name, l, i)] = len(items)
                items.append((big[name], (l, i)))
    for name, prefix in (("wkv", ()), ("wout", ()), ("winc", (0,)), ("winm", (0,))):
        where[name] = len(items)
        items.append((big[name], prefix))
    gathered = gather_shards(items, place_own(items, jnp.reshape(chip, (1,)).astype(jnp.int32)))

    small = jnp.zeros((SMALL_ROWS, SMALL_IN_COLS), F32)
    small = small.at[0:6, 0:256].set(ln_g.reshape(6, 256)).at[6:12, 0:256].set(ln_b.reshape(6, 256))
    small = small.at[12:15, 0:192].set(conv_w[0]).at[16:20, 0:384].set(qk_conv_w[0])
    smalls = small_allgather(small, reduce=False)[0::2]
    ln_g_full = _chips_to_cols(smalls[:, 0:6, 0:256]).reshape(DEPTH, 3, 1, D_MODEL)
    ln_b_full = _chips_to_cols(smalls[:, 6:12, 0:256]).reshape(DEPTH, 3, 1, D_MODEL)
    conv_w_full = _chips_to_cols(smalls[:, 12:15, 0:192])
    qk_w_full = _chips_to_cols(smalls[:, 16:20, 0:384])

    P = {"wg": [[gathered[where[("wg", l, i)]] for i in range(2)] for l in range(DEPTH)],
         "wu": [[gathered[where[("wu", l, i)]] for i in range(2)] for l in range(DEPTH)],
         "wd": [[gathered[where[("wd", l, i)]] for i in range(2)] for l in range(DEPTH)],
         "ln_g": ln_g_full, "ln_b": ln_b_full}
    wkv_all, wout_all = gathered[where["wkv"]], gathered[where["wout"]]
    P["wkv"] = [_cols_to_groups(wkv_all[:, l].reshape(D_MODEL, 2 * D_XA)) for l in range(DEPTH)]
    wout1 = wout_all[:, 1].reshape(D_MODEL, D_MODEL)
    wout1_tok = jnp.pad(wout1[:D_TOK].reshape(ML_HEADS, ML_HEAD_DIM, D_MODEL), ((0, 0), (0, GROUP - ML_HEAD_DIM), (0, 0)))
    P["wout"] = [wout_all[:, 0], jnp.concatenate([wout1_tok, wout1[D_TOK:][None]], axis=0)]
    P["win"] = [_cols_to_groups(_chips_to_cols(gathered[where["winc"]])), _mlstm_in_groups(_chips_to_cols(gathered[where["winm"]]))]
    P["convw"] = _taps_to_groups(conv_w_full, GROUP)
    P["qkw"] = _taps_to_groups(qk_w_full, ML_HEAD_DIM)
    P["bg"] = _pad_last(b_gates, GROUP)
    P["hg"] = _pad_last(head_norm_g[0], GROUP)[:, None, :]

    loss, grad_x, G = _local_step(x[0], mem[0], loss_target[0], P)

    gs, slot = [], {}
    for name in ("wg", "wu", "wd"):
        for l in range(DEPTH):
            for i in range(2):
                slot[(name, l, i)] = len(gs)
                gs.append(G[name][l][i])
    dwkv = jnp.stack([_groups_to_cols(G["wkv"][l]).reshape(N_CHIPS, D_MODEL // N_CHIPS, 2 * D_XA) for l in range(DEPTH)], axis=1)
    dwout1 = jnp.concatenate([G["wout"][1][:ML_HEADS, :ML_HEAD_DIM].reshape(D_TOK, D_MODEL), G["wout"][1][ML_HEADS]], axis=0)
    dwout = jnp.stack([G["wout"][0], dwout1.reshape(N_CHIPS, D_MODEL // N_CHIPS, D_MODEL)], axis=1)
    slot["wkv"], slot["wout"], slot["winc"], slot["winm"] = len(gs), len(gs) + 1, len(gs) + 2, len(gs) + 3
    gs += [dwkv.reshape(N_CHIPS, 2 * (D_MODEL // N_CHIPS), 2 * D_XA), dwout.reshape(N_CHIPS, 2 * (D_MODEL // N_CHIPS), D_MODEL),
           _cols_to_chips(_groups_to_cols(G["win"][0])), _cols_to_chips(_mlstm_in_ungroup(G["win"][1]))]
    red = reduce_scatter(gs)

    sg = jnp.zeros((SMALL_ROWS, SMALL_OUT_COLS), F32)
    dln_g = jnp.concatenate([G["ln_g"][l][k] for l in range(DEPTH) for k in range(3)], axis=0)
    dln_b = jnp.concatenate([G["ln_b"][l][k] for l in range(DEPTH) for k in range(3)], axis=0)
    sg = sg.at[0:6, 0:D_MODEL].set(dln_g).at[6:12, 0:D_MODEL].set(dln_b)
    sg = sg.at[12:15, 0:D_TOK].set(_groups_to_taps(G["convw"], 3, GROUP))
    sg = sg.at[15:16, 0:8].set(G["bg"][:, 0:8]).at[15:16, 8:9].set(loss)
    sg = sg.at[16:20, 0:2 * D_TOK].set(_groups_to_taps(G["qkw"], 4, ML_HEAD_DIM))
    sg = sg.at[20:24, 0:ML_HEAD_DIM].set(G["hg"][:, 0, :ML_HEAD_DIM])
    tot = small_allgather(sg, reduce=True)

    grads = {
        "ln_g": lax.dynamic_slice(tot[0:6, 0:D_MODEL], (0, chip * 256), (6, 256)).reshape(DEPTH, 3, 256),
        "ln_b": lax.dynamic_slice(tot[6:12, 0:D_MODEL], (0, chip * 256), (6, 256)).reshape(DEPTH, 3, 256),
        "ffn_w_gate": jnp.stack([jnp.stack([red[slot[("wg", l, i)]] for i in range(2)]) for l in range(DEPTH)]),
        "ffn_w_up": jnp.stack([jnp.stack([red[slot[("wu", l, i)]] for i in range(2)]) for l in range(DEPTH)]),
        "ffn_w_down": jnp.stack([jnp.stack([red[slot[("wd", l, i)]] for i in range(2)]) for l in range(DEPTH)]),
        "w_kv_mem": red[slot["wkv"]].reshape(DEPTH, D_MODEL // N_CHIPS, 2 * D_XA),
        "w_out": red[slot["wout"]].reshape(DEPTH, D_MODEL // N_CHIPS, D_MODEL),
        "w_in_conv": red[slot["winc"]][None],
        "conv_w": lax.dynamic_slice(tot[12:15, 0:D_TOK], (0, chip * 192), (3, 192))[None],
        "w_in_mlstm": red[slot["winm"]][None],
        "b_gates": tot[15:16, 0:8],
        "qk_conv_w": lax.dynamic_slice(tot[16:20, 0:2 * D_TOK], (0, chip * 384), (4, 384))[None],
        "head_norm_g": tot[20:24, 0:ML_HEAD_DIM][None],
    }
    loss_total = tot[15, 8]

    weights = {"ln_g": ln_g, "ln_b": ln_b, "ffn_w_gate": ffn_w_gate, "ffn_w_up": ffn_w_up, "ffn_w_down": ffn_w_down,
               "w_kv_mem": w_kv_mem, "w_out": w_out, "w_in_conv": w_in_conv, "conv_w": conv_w, "w_in_mlstm": w_in_mlstm,
               "b_gates": b_gates, "qk_conv_w": qk_conv_w, "head_norm_g": head_norm_g}
    ms = {"ln_g": m_ln_g, "ln_b": m_ln_b, "ffn_w_gate": m_ffn_w_gate, "ffn_w_up": m_ffn_w_up, "ffn_w_down": m_ffn_w_down,
          "w_kv_mem": m_w_kv_mem, "w_out": m_w_out, "w_in_conv": m_w_in_conv, "conv_w": m_conv_w, "w_in_mlstm": m_w_in_mlstm,
          "b_gates": m_b_gates, "qk_conv_w": m_qk_conv_w, "head_norm_g": m_head_norm_g}
    vs = {"ln_g": v_ln_g, "ln_b": v_ln_b, "ffn_w_gate": v_ffn_w_gate, "ffn_w_up": v_ffn_w_up, "ffn_w_down": v_ffn_w_down,
          "w_kv_mem": v_w_kv_mem, "w_out": v_w_out, "w_in_conv": v_w_in_conv, "conv_w": v_conv_w, "w_in_mlstm": v_w_in_mlstm,
          "b_gates": v_b_gates, "qk_conv_w": v_qk_conv_w, "head_norm_g": v_head_norm_g}
    names = list(weights)
    deltas, new_m, new_v = [], [], []
    for nme in names:
        w = weights[nme]
        shp = w.shape
        two = (math.prod(shp[:-1]), shp[-1])
        d, nm, nv = adamw(w.reshape(two), grads[nme].reshape(two), ms[nme].reshape(two), vs[nme].reshape(two), "adamw_" + nme)
        deltas.append(d.reshape(shp))
        new_m.append(nm.reshape(shp))
        new_v.append(nv.reshape(shp))
    return (loss_total, grad_x[None], *[grads[nme] for nme in names], *deltas, *new_m, *new_v)
```

```python
import functools
import math

import jax
import jax.numpy as jnp
from jax import lax
from jax.experimental import pallas as pl
from jax.experimental.pallas import tpu as pltpu

F32 = jnp.float32
BF16 = jnp.bfloat16
SDS = jax.ShapeDtypeStruct

D_MODEL = 1024
DEPTH = 2
N_MEM = 256
XA_HEADS = 4
XA_HEAD_DIM = 64
D_XA = 256
D_TOK = 768
ML_HEADS = 4
ML_HEAD_DIM = 192
ML_CHUNK = 64
D_FF = 2816
LN_EPS = 1e-5
ALPHA = (2.0 * DEPTH) ** 0.25
N_CHIPS = 4
N_DEV = 8
FF_SHARD = D_FF // N_CHIPS
GROUP = 256
NEG = -1e30

ADAM_LR = 0.001
ADAM_B1 = 0.9
ADAM_B2 = 0.999
ADAM_EPS = 1e-08
ADAM_WD = 0.01
ADAM_STEP = 10

VMEM_LIMIT = 56 * 1024 * 1024

NN = ((1,), (0,))
NT = ((1,), (1,))
TN = ((0,), (0,))
MESH = pl.DeviceIdType.MESH


def _dot(a, b, dims):
    return lax.dot_general(a, b, (dims, ((), ())), preferred_element_type=F32)


def _bdot(a, b, ca, cb):
    dims = (((ca,), (cb,)), ((0,), (0,)))
    ah, bh = a.astype(BF16), b.astype(BF16)
    al, bl = (a - ah.astype(F32)).astype(BF16), (b - bh.astype(F32)).astype(BF16)
    dot = functools.partial(lax.dot_general, dimension_numbers=dims, preferred_element_type=F32)
    return dot(ah, bh) + dot(al, bh) + dot(ah, bl)


def _sigmoid(x):
    return 1.0 / (1.0 + jnp.exp(-x))


def _params(sem, vmem=VMEM_LIMIT):
    return pltpu.CompilerParams(dimension_semantics=sem, vmem_limit_bytes=vmem)


def _tile(n, want):
    t = min(n, want)
    assert n % t == 0, (n, t)
    return t


def ffn_up(xb, wg, wu):
    S, K = xb.shape
    G, _, N = wg.shape
    ts = _tile(S, 1024)

    def body(x_ref, wg_ref, wu_ref, g_ref, u_ref, h_ref):
        x = x_ref[...]
        g = _dot(x, wg_ref[0], NN)
        u = _dot(x, wu_ref[0], NN)
        g_ref[0] = g
        u_ref[0] = u
        h_ref[0] = (g * _sigmoid(g) * u).astype(BF16)

    wspec = pl.BlockSpec((1, K, N), lambda g, s: (g, 0, 0))
    ospec = pl.BlockSpec((1, ts, N), lambda g, s: (g, s, 0))
    return pl.pallas_call(
        body, name="ffn_up", grid=(G, S // ts),
        in_specs=[pl.BlockSpec((ts, K), lambda g, s: (s, 0)), wspec, wspec],
        out_specs=[ospec, ospec, ospec],
        out_shape=[SDS((G, S, N), F32), SDS((G, S, N), F32), SDS((G, S, N), BF16)],
        compiler_params=_params(("parallel", "parallel")),
    )(xb, wg, wu)


def proj(xb, w, name):
    S, K = xb.shape
    G, _, N = w.shape
    ts = _tile(S, 1024)

    def body(x_ref, w_ref, y_ref):
        y_ref[0] = _dot(x_ref[...], w_ref[0], NN)

    return pl.pallas_call(
        body, name=name, grid=(G, S // ts),
        in_specs=[pl.BlockSpec((ts, K), lambda g, s: (s, 0)), pl.BlockSpec((1, K, N), lambda g, s: (g, 0, 0))],
        out_specs=pl.BlockSpec((1, ts, N), lambda g, s: (g, s, 0)),
        out_shape=SDS((G, S, N), F32),
        compiler_params=_params(("parallel", "parallel")),
    )(xb, w)


def contract_ln(a, w, xres, gamma, beta, scale, name):
    G, S, Kg = a.shape
    N = w.shape[2]
    ts = _tile(S, 512)

    def body(a_ref, w_ref, x_ref, g_ref, b_ref, z_ref, xn_ref, xb_ref, acc):
        g = pl.program_id(1)

        @pl.when(g == 0)
        def _():
            acc[...] = jnp.zeros_like(acc)

        acc[...] += _dot(a_ref[0], w_ref[0], NN)

        @pl.when(g == G - 1)
        def _():
            z = ALPHA * x_ref[...] + scale * acc[...]
            mu = jnp.mean(z, axis=-1, keepdims=True)
            zc = z - mu
            var = jnp.mean(zc * zc, axis=-1, keepdims=True)
            xn = zc * lax.rsqrt(var + LN_EPS) * g_ref[...] + b_ref[...]
            z_ref[...] = z
            xn_ref[...] = xn
            xb_ref[...] = xn.astype(BF16)

    row = pl.BlockSpec((ts, N), lambda s, g: (s, 0))
    vec = pl.BlockSpec((1, N), lambda s, g: (0, 0))
    return pl.pallas_call(
        body, name=name, grid=(S // ts, G),
        in_specs=[pl.BlockSpec((1, ts, Kg), lambda s, g: (g, s, 0)), pl.BlockSpec((1, Kg, N), lambda s, g: (g, 0, 0)),
                  row, vec, vec],
        out_specs=[row, row, row],
        out_shape=[SDS((S, N), F32), SDS((S, N), F32), SDS((S, N), BF16)],
        scratch_shapes=[pltpu.VMEM((ts, N), F32)],
        compiler_params=_params(("parallel", "arbitrary")),
    )(a, w, xres, gamma, beta)


def ln_bwd(dx, z, gamma, out_scale, name):
    S, N = dx.shape
    ts = _tile(S, 512)

    def body(dx_ref, z_ref, g_ref, dz_ref, dzb_ref, dg_ref, db_ref):
        @pl.when(pl.program_id(0) == 0)
        def _():
            dg_ref[...] = jnp.zeros_like(dg_ref)
            db_ref[...] = jnp.zeros_like(db_ref)

        z = z_ref[...]
        mu = jnp.mean(z, axis=-1, keepdims=True)
        zc = z - mu
        var = jnp.mean(zc * zc, axis=-1, keepdims=True)
        rstd = lax.rsqrt(var + LN_EPS)
        xhat = zc * rstd
        dxv = dx_ref[...]
        dg_ref[...] += jnp.sum(dxv * xhat, axis=0, keepdims=True)
        db_ref[...] += jnp.sum(dxv, axis=0, keepdims=True)
        dxh = dxv * g_ref[...]
        m1 = jnp.mean(dxh, axis=-1, keepdims=True)
        m2 = jnp.mean(dxh * xhat, axis=-1, keepdims=True)
        dz = rstd * (dxh - m1 - xhat * m2)
        dz_ref[...] = dz
        dzb_ref[...] = (out_scale * dz).astype(BF16)

    row = pl.BlockSpec((ts, N), lambda s: (s, 0))
    vec = pl.BlockSpec((1, N), lambda s: (0, 0))
    return pl.pallas_call(
        body, name=name, grid=(S // ts,),
        in_specs=[row, row, vec],
        out_specs=[row, row, vec, vec],
        out_shape=[SDS((S, N), F32), SDS((S, N), BF16), SDS((1, N), F32), SDS((1, N), F32)],
        compiler_params=_params(("arbitrary",)),
    )(dx, z, gamma)


def ffn_bwd_dh(dyb, wd, g1, u1):
    S, K = dyb.shape
    G, N, _ = wd.shape
    ts = _tile(S, 1024)

    def body(dy_ref, w_ref, g_ref, u_ref, dg_ref, du_ref):
        dh = _dot(dy_ref[...], w_ref[0], NT)
        g = g_ref[0]
        sig = _sigmoid(g)
        silu = g * sig
        dg_ref[0] = (dh * u_ref[0] * (sig * (1.0 + g * (1.0 - sig)))).astype(BF16)
        du_ref[0] = (dh * silu).astype(BF16)

    gspec = pl.BlockSpec((1, ts, N), lambda g, s: (g, s, 0))
    return pl.pallas_call(
        body, name="ffn_bwd_dh", grid=(G, S // ts),
        in_specs=[pl.BlockSpec((ts, K), lambda g, s: (s, 0)), pl.BlockSpec((1, N, K), lambda g, s: (g, 0, 0)), gspec, gspec],
        out_specs=[gspec, gspec],
        out_shape=[SDS((G, S, N), BF16), SDS((G, S, N), BF16)],
        compiler_params=_params(("parallel", "parallel")),
    )(dyb, wd, g1, u1)


def proj_t(dyb, w, name):
    S, N = dyb.shape
    G, Kg, _ = w.shape
    ts = _tile(S, 1024)

    def body(dy_ref, w_ref, da_ref):
        da_ref[0] = _dot(dy_ref[...], w_ref[0], NT)

    return pl.pallas_call(
        body, name=name, grid=(G, S // ts),
        in_specs=[pl.BlockSpec((ts, N), lambda g, s: (s, 0)), pl.BlockSpec((1, Kg, N), lambda g, s: (g, 0, 0))],
        out_specs=pl.BlockSpec((1, ts, Kg), lambda g, s: (g, s, 0)),
        out_shape=SDS((G, S, Kg), F32),
        compiler_params=_params(("parallel", "parallel")),
    )(dyb, w)


def contract_t(pairs, res, name):
    n = len(pairs)
    G, S, Ng = pairs[0][0].shape
    K = pairs[0][1].shape[1]
    ts = _tile(S, 512)

    def body(*refs):
        ins, r_ref, o_ref, acc = refs[:2 * n], refs[2 * n], refs[2 * n + 1], refs[2 * n + 2]
        g = pl.program_id(1)

        @pl.when(g == 0)
        def _():
            acc[...] = ALPHA * r_ref[...]

        for p in range(n):
            acc[...] += _dot(ins[2 * p][0], ins[2 * p + 1][0], NT)

        @pl.when(g == G - 1)
        def _():
            o_ref[...] = acc[...]

    in_specs, args = [], []
    for da, w in pairs:
        in_specs += [pl.BlockSpec((1, ts, Ng), lambda s, g: (g, s, 0)), pl.BlockSpec((1, K, Ng), lambda s, g: (g, 0, 0))]
        args += [da, w]
    row = pl.BlockSpec((ts, K), lambda s, g: (s, 0))
    return pl.pallas_call(
        body, name=name, grid=(S // ts, G),
        in_specs=in_specs + [row], out_specs=row,
        out_shape=SDS((S, K), F32),
        scratch_shapes=[pltpu.VMEM((ts, K), F32)],
        compiler_params=_params(("parallel", "arbitrary")),
    )(*args, res)


def wgrad(a, b, out_dtype, name):
    ga, gb = a.ndim == 3, b.ndim == 3
    G = a.shape[0] if ga else b.shape[0]
    S, K = a.shape[-2:]
    N = b.shape[-1]
    ts = _tile(S, 1024)
    ns = S // ts

    def body(a_ref, b_ref, o_ref, acc):
        s = pl.program_id(1)

        @pl.when(s == 0)
        def _():
            acc[...] = jnp.zeros_like(acc)

        av = a_ref[0] if ga else a_ref[...]
        bv = b_ref[0] if gb else b_ref[...]
        acc[...] += _dot(av, bv, TN)

        @pl.when(s == ns - 1)
        def _():
            o_ref[0] = acc[...].astype(out_dtype)

    aspec = pl.BlockSpec((1, ts, K), lambda g, s: (g, s, 0)) if ga else pl.BlockSpec((ts, K), lambda g, s: (s, 0))
    bspec = pl.BlockSpec((1, ts, N), lambda g, s: (g, s, 0)) if gb else pl.BlockSpec((ts, N), lambda g, s: (s, 0))
    return pl.pallas_call(
        body, name=name, grid=(G, ns),
        in_specs=[aspec, bspec],
        out_specs=pl.BlockSpec((1, K, N), lambda g, s: (g, 0, 0)),
        out_shape=SDS((G, K, N), out_dtype),
        scratch_shapes=[pltpu.VMEM((K, N), F32)],
        compiler_params=_params(("parallel", "arbitrary")),
    )(a, b)


def loss_grad(xn, tgt):
    S, N = xn.shape
    ts = _tile(S, 512)

    def body(x_ref, t_ref, l_ref, dx_ref):
        @pl.when(pl.program_id(0) == 0)
        def _():
            l_ref[...] = jnp.zeros_like(l_ref)

        e = x_ref[...] - t_ref[...]
        dx_ref[...] = e * (1.0 / N)
        l_ref[...] += 0.5 * jnp.sum(jnp.mean(e * e, axis=-1, keepdims=True), axis=0, keepdims=True)

    row = pl.BlockSpec((ts, N), lambda s: (s, 0))
    return pl.pallas_call(
        body, name="loss_grad", grid=(S // ts,),
        in_specs=[row, row],
        out_specs=[pl.BlockSpec((1, 1), lambda s: (0, 0)), row],
        out_shape=[SDS((1, 1), F32), SDS((S, N), F32)],
        compiler_params=_params(("arbitrary",)),
    )(xn, tgt)


def _shift_down(x, k):
    if k == 0:
        return x
    rows = lax.broadcasted_iota(jnp.int32, x.shape, 0)
    return jnp.where(rows >= k, pltpu.roll(x, k, 0), 0.0)


def _shift_up(x, k):
    if k == 0:
        return x
    n = x.shape[0]
    rows = lax.broadcasted_iota(jnp.int32, x.shape, 0)
    return jnp.where(rows < n - k, pltpu.roll(x, n - k, 0), 0.0)


LANES = 128


def conv_mixer_fwd(u, cw):
    _, S, _ = u.shape
    nh = GROUP // LANES

    def body(b_ref, c_ref, x_ref, w_ref, o_ref):
        p = c_ref[0] * x_ref[0]
        w = w_ref[0]
        conv = w[2:3] * p + w[1:2] * _shift_down(p, 1) + w[0:1] * _shift_down(p, 2)
        o_ref[0] = (b_ref[0] * conv).astype(BF16)

    def uspec(off):
        return pl.BlockSpec((1, S, LANES), lambda g, h: (g + off, 0, h))

    return pl.pallas_call(
        body, name="conv_mixer_fwd", grid=(3, nh),
        in_specs=[uspec(0), uspec(3), uspec(6), pl.BlockSpec((1, 8, LANES), lambda g, h: (g, 0, h))],
        out_specs=pl.BlockSpec((1, S, LANES), lambda g, h: (g, 0, h)),
        out_shape=SDS((3, S, GROUP), BF16),
        compiler_params=_params(("parallel", "parallel")),
    )(u, u, u, cw)


def conv_mixer_bwd(u, cw, dm):
    _, S, _ = u.shape
    nh = GROUP // LANES

    def body(b_ref, c_ref, x_ref, w_ref, d_ref, db_ref, dc_ref, dx_ref, dw_ref):
        cg, xi = c_ref[0], x_ref[0]
        p = cg * xi
        p1, p2 = _shift_down(p, 1), _shift_down(p, 2)
        w = w_ref[0]
        conv = w[2:3] * p + w[1:2] * p1 + w[0:1] * p2
        dt = d_ref[0]
        db_ref[0] = (dt * conv).astype(BF16)
        dcv = dt * b_ref[0]
        dp = w[2:3] * dcv + w[1:2] * _shift_up(dcv, 1) + w[0:1] * _shift_up(dcv, 2)
        dc_ref[0] = (dp * xi).astype(BF16)
        dx_ref[0] = (dp * cg).astype(BF16)
        dw = jnp.concatenate([jnp.sum(dcv * p2, axis=0, keepdims=True), jnp.sum(dcv * p1, axis=0, keepdims=True),
                              jnp.sum(dcv * p, axis=0, keepdims=True), jnp.zeros((5, LANES), F32)], axis=0)
        dw_ref[0] = dw

    def uspec(off):
        return pl.BlockSpec((1, S, LANES), lambda g, h: (g + off, 0, h))

    ospec = pl.BlockSpec((1, S, LANES), lambda g, h: (g, 0, h))
    wspec = pl.BlockSpec((1, 8, LANES), lambda g, h: (g, 0, h))
    return pl.pallas_call(
        body, name="conv_mixer_bwd", grid=(3, nh),
        in_specs=[uspec(0), uspec(3), uspec(6), wspec, ospec],
        out_specs=[ospec, ospec, ospec, wspec],
        out_shape=[SDS((3, S, GROUP), BF16)] * 3 + [SDS((3, 8, GROUP), F32)],
        compiler_params=_params(("parallel", "parallel")),
    )(u, u, u, cw, dm)


def qk_conv_fwd(u, qw):
    _, S, _ = u.shape
    nh = GROUP // LANES

    def body(u_ref, w_ref, o_ref):
        x = u_ref[0]
        w = w_ref[0]
        pre = w[3:4] * x + w[2:3] * _shift_down(x, 1) + w[1:2] * _shift_down(x, 2) + w[0:1] * _shift_down(x, 3)
        o_ref[0] = pre * _sigmoid(pre)

    spec = pl.BlockSpec((1, S, LANES), lambda g, h: (g, 0, h))
    return pl.pallas_call(
        body, name="qk_conv_fwd", grid=(8, nh),
        in_specs=[spec, pl.BlockSpec((1, 8, LANES), lambda g, h: (g, 0, h))],
        out_specs=spec,
        out_shape=SDS((8, S, GROUP), F32),
        compiler_params=_params(("parallel", "parallel")),
    )(u, qw)


def qk_conv_bwd(u, qw, dqk):
    _, S, _ = u.shape
    nh = GROUP // LANES

    def body(u_ref, w_ref, d_ref, du_ref, dw_ref):
        x = u_ref[0]
        w = w_ref[0]
        x1, x2, x3 = _shift_down(x, 1), _shift_down(x, 2), _shift_down(x, 3)
        pre = w[3:4] * x + w[2:3] * x1 + w[1:2] * x2 + w[0:1] * x3
        sig = _sigmoid(pre)
        dpre = d_ref[0] * (sig * (1.0 + pre * (1.0 - sig)))
        du = w[3:4] * dpre + w[2:3] * _shift_up(dpre, 1) + w[1:2] * _shift_up(dpre, 2) + w[0:1] * _shift_up(dpre, 3)
        du_ref[0] = du.astype(BF16)
        dw = jnp.concatenate([jnp.sum(dpre * x3, axis=0, keepdims=True), jnp.sum(dpre * x2, axis=0, keepdims=True),
                              jnp.sum(dpre * x1, axis=0, keepdims=True), jnp.sum(dpre * x, axis=0, keepdims=True),
                              jnp.zeros((4, LANES), F32)], axis=0)
        dw_ref[0] = dw

    spec = pl.BlockSpec((1, S, LANES), lambda g, h: (g, 0, h))
    wspec = pl.BlockSpec((1, 8, LANES), lambda g, h: (g, 0, h))
    return pl.pallas_call(
        body, name="qk_conv_bwd", grid=(8, nh),
        in_specs=[spec, wspec, spec],
        out_specs=[spec, wspec],
        out_shape=[SDS((8, S, GROUP), BF16), SDS((8, 8, GROUP), F32)],
        compiler_params=_params(("parallel", "parallel")),
    )(u, qw, dqk)


def _head_masks():
    lane = lax.broadcasted_iota(jnp.int32, (1, D_XA), 1)
    return [(lane >= h * XA_HEAD_DIM) & (lane < (h + 1) * XA_HEAD_DIM) for h in range(XA_HEADS)]


def xattn_fwd(u, qg, kv):
    _, S, _ = u.shape
    ts = _tile(S, 512)
    scale = XA_HEAD_DIM ** -0.5

    def body(q_ref, kv_ref, o_ref):
        q = q_ref[0]
        k = kv_ref[0].astype(BF16)
        v = kv_ref[1]
        o = jnp.zeros((ts, D_XA), F32)
        for m in _head_masks():
            s = _dot(jnp.where(m, q, 0.0).astype(BF16), k, NT) * scale
            s = s - jnp.max(s, axis=-1, keepdims=True)
            e = jnp.exp(s)
            p = e / jnp.sum(e, axis=-1, keepdims=True)
            o = o + _dot(p.astype(BF16), jnp.where(m, v, 0.0).astype(BF16), NN)
        o_ref[0] = o.astype(BF16)

    return pl.pallas_call(
        body, name="xattn_fwd", grid=(S // ts,),
        in_specs=[pl.BlockSpec((1, ts, GROUP), lambda s: (qg, s, 0)), pl.BlockSpec((2, N_MEM, GROUP), lambda s: (0, 0, 0))],
        out_specs=pl.BlockSpec((1, ts, GROUP), lambda s: (0, s, 0)),
        out_shape=SDS((1, S, GROUP), BF16),
        compiler_params=_params(("parallel",)),
    )(u, kv)


def xattn_bwd(u, qg, kv, dm, dg):
    _, S, _ = u.shape
    ts = _tile(S, 512)
    scale = XA_HEAD_DIM ** -0.5

    def body(q_ref, kv_ref, do_ref, dq_ref, dkv_ref):
        @pl.when(pl.program_id(0) == 0)
        def _():
            dkv_ref[...] = jnp.zeros_like(dkv_ref)

        q = q_ref[0]
        k = kv_ref[0]
        v = kv_ref[1]
        kb = k.astype(BF16)
        do = do_ref[0]
        dq = jnp.zeros((ts, D_XA), F32)
        dk = jnp.zeros((N_MEM, D_XA), F32)
        dv = jnp.zeros((N_MEM, D_XA), F32)
        for m in _head_masks():
            qm = jnp.where(m, q, 0.0).astype(BF16)
            s = _dot(qm, kb, NT) * scale
            s = s - jnp.max(s, axis=-1, keepdims=True)
            e = jnp.exp(s)
            p = e / jnp.sum(e, axis=-1, keepdims=True)
            dom = jnp.where(m, do, 0.0).astype(BF16)
            dp = _dot(dom, jnp.where(m, v, 0.0).astype(BF16), NT)
            ds = (p * (dp - jnp.sum(dp * p, axis=-1, keepdims=True)) * scale).astype(BF16)
            dq = dq + _dot(ds, jnp.where(m, k, 0.0).astype(BF16), NN)
            dk = dk + _dot(ds, qm, TN)
            dv = dv + _dot(p.astype(BF16), dom, TN)
        dq_ref[0] = dq.astype(BF16)
        dkv_ref[0] += dk
        dkv_ref[1] += dv

    return pl.pallas_call(
        body, name="xattn_bwd", grid=(S // ts,),
        in_specs=[pl.BlockSpec((1, ts, GROUP), lambda s: (qg, s, 0)), pl.BlockSpec((2, N_MEM, GROUP), lambda s: (0, 0, 0)),
                  pl.BlockSpec((1, ts, GROUP), lambda s: (dg, s, 0))],
        out_specs=[pl.BlockSpec((1, ts, GROUP), lambda s: (0, s, 0)), pl.BlockSpec((2, N_MEM, GROUP), lambda s: (0, 0, 0))],
        out_shape=[SDS((1, S, GROUP), BF16), SDS((2, N_MEM, GROUP), F32)],
        compiler_params=_params(("arbitrary",)),
    )(u, kv, dm)


ML_BLOCK_CHUNKS = 4
H4 = ML_HEADS
L = ML_CHUNK
NLANE = ML_HEAD_DIM


def _chunk_consts():
    r = lax.broadcasted_iota(jnp.int32, (1, L, L), 1)
    c = lax.broadcasted_iota(jnp.int32, (1, L, L), 2)
    return r >= c, r <= c, r == c


def _gate_cols(gb):
    lane = lax.broadcasted_iota(jnp.int32, gb.shape, 1)
    li = jnp.stack([jnp.sum(jnp.where(lane == h, gb, 0.0), axis=1, keepdims=True) for h in range(H4)])
    gf = jnp.stack([jnp.sum(jnp.where(lane == H4 + h, gb, 0.0), axis=1, keepdims=True) for h in range(H4)])
    return li, gf


def _log_sigmoid(x):
    return jnp.minimum(x, 0.0) - jnp.log(1.0 + jnp.exp(-jnp.abs(x)))


def _chunk_forward(q, k, v_aug, li_col, lf_col, c_prev, m_prev):
    tri, tri_t, eye = _chunk_consts()
    lf_row = jnp.sum(jnp.where(eye, lf_col, 0.0), axis=1, keepdims=True)
    li_row = jnp.sum(jnp.where(eye, li_col, 0.0), axis=1, keepdims=True)
    bcum_col = jnp.sum(jnp.where(tri, lf_row, 0.0), axis=2, keepdims=True)
    bcum_row = jnp.sum(jnp.where(tri_t, lf_col, 0.0), axis=1, keepdims=True)
    log_d = jnp.where(tri, bcum_col - bcum_row + li_row, NEG)
    log_inter = bcum_col + m_prev
    m_t = jnp.maximum(log_inter, jnp.max(log_d, axis=2, keepdims=True))
    w_intra = jnp.exp(log_d - m_t)
    w_inter = jnp.exp(log_inter - m_t)
    sc = _bdot(q, k, 2, 2) * w_intra
    qc = _bdot(q, c_prev, 2, 1)
    num = _bdot(sc, v_aug, 2, 1) + w_inter * qc
    lane = lax.broadcasted_iota(jnp.int32, num.shape, 2)
    den = jnp.sum(jnp.where(lane == NLANE, num, 0.0), axis=2, keepdims=True)
    e_m = jnp.exp(-m_t)
    b_last = jnp.sum(lf_row, axis=2, keepdims=True)
    log_w = b_last - bcum_col + li_col
    m_new = jnp.maximum(b_last + m_prev, jnp.max(log_w, axis=1, keepdims=True))
    w_k = jnp.exp(log_w - m_new)
    decay = jnp.exp(b_last + m_prev - m_new)
    return dict(w_intra=w_intra, w_inter=w_inter, sc=sc, qc=qc, num=num, den=den, e_m=e_m, lane=lane,
                w_k=w_k, decay=decay, m_new=m_new)


def mlstm_fwd(qk, u, bg):
    _, S, _ = qk.shape
    nc = S // L
    cb = min(ML_BLOCK_CHUNKS, nc)
    rows = cb * L
    kscale = ML_HEAD_DIM ** -0.5

    def body(qk_ref, v_ref, g_ref, bg_ref, h_ref, cst_ref, mst_ref, c_sc, m_sc):
        @pl.when(pl.program_id(0) == 0)
        def _():
            c_sc[...] = jnp.zeros_like(c_sc)
            m_sc[...] = jnp.zeros_like(m_sc)

        for c in range(cb):
            sl = pl.ds(c * L, L)
            q = qk_ref[0:H4, sl, :]
            k = qk_ref[H4:2 * H4, sl, :] * kscale
            v = v_ref[:, sl, :]
            lane = lax.broadcasted_iota(jnp.int32, v.shape, 2)
            v_aug = jnp.where(lane == NLANE, 1.0, v)
            li_col, gf = _gate_cols(g_ref[0, sl, :] + bg_ref[...])
            lf_col = _log_sigmoid(gf)
            c_prev = c_sc[...]
            m_prev = m_sc[...]
            f = _chunk_forward(q, k, v_aug, li_col, lf_col, c_prev, m_prev)
            r = 1.0 / jnp.maximum(jnp.abs(f["den"]), f["e_m"])
            h_ref[:, sl, :] = jnp.where(lane < NLANE, f["num"] * r, 0.0)
            cst_ref[c] = c_prev
            mst_ref[c] = jnp.broadcast_to(m_prev, (H4, 1, LANES))
            c_sc[...] = f["decay"] * c_prev + _bdot(k * f["w_k"], v_aug, 1, 1)
            m_sc[...] = f["m_new"]

    def hspec(blk):
        return pl.BlockSpec((H4, rows, GROUP), lambda i: (blk, i, 0))

    return pl.pallas_call(
        body, name="mlstm_fwd", grid=(nc // cb,),
        in_specs=[pl.BlockSpec((2 * H4, rows, GROUP), lambda i: (0, i, 0)), hspec(2),
                  pl.BlockSpec((1, rows, GROUP), lambda i: (17, i, 0)), pl.BlockSpec((1, GROUP), lambda i: (0, 0))],
        out_specs=[hspec(0), pl.BlockSpec((cb, H4, GROUP, GROUP), lambda i: (i, 0, 0, 0)),
                   pl.BlockSpec((cb, H4, 1, LANES), lambda i: (i, 0, 0, 0))],
        out_shape=[SDS((H4, S, GROUP), F32), SDS((nc, H4, GROUP, GROUP), F32), SDS((nc, H4, 1, LANES), F32)],
        scratch_shapes=[pltpu.VMEM((H4, GROUP, GROUP), F32), pltpu.VMEM((H4, 1, 1), F32)],
        compiler_params=_params(("arbitrary",)),
    )(qk, u, u, bg)


def mlstm_bwd(qk, u, bg, cst, mst, dh):
    _, S, _ = qk.shape
    nc = S // L
    cb = min(ML_BLOCK_CHUNKS, nc)
    rows = cb * L
    nb = nc // cb
    kscale = ML_HEAD_DIM ** -0.5

    def body(qk_ref, v_ref, g_ref, bg_ref, cst_ref, mst_ref, dh_ref, dqk_ref, dv_ref, dg_ref, dbg_ref, dc_sc):
        @pl.when(pl.program_id(0) == 0)
        def _():
            dc_sc[...] = jnp.zeros_like(dc_sc)
            dbg_ref[...] = jnp.zeros_like(dbg_ref)

        tri, tri_t, eye = _chunk_consts()
        for c in reversed(range(cb)):
            sl = pl.ds(c * L, L)
            q = qk_ref[0:H4, sl, :]
            k = qk_ref[H4:2 * H4, sl, :] * kscale
            v = v_ref[:, sl, :]
            lane = lax.broadcasted_iota(jnp.int32, v.shape, 2)
            v_aug = jnp.where(lane == NLANE, 1.0, v)
            li_col, gf = _gate_cols(g_ref[0, sl, :] + bg_ref[...])
            lf_col = _log_sigmoid(gf)
            c_prev = cst_ref[c]
            m_prev = mst_ref[c][:, :, 0:1]
            f = _chunk_forward(q, k, v_aug, li_col, lf_col, c_prev, m_prev)
            w_intra, w_inter, sc, num, den, e_m = f["w_intra"], f["w_inter"], f["sc"], f["num"], f["den"], f["e_m"]
            absd = jnp.abs(den)
            r = 1.0 / jnp.maximum(absd, e_m)
            dhv = dh_ref[:, sl, :]
            s1 = jnp.sum(jnp.where(lane < NLANE, dhv * num, 0.0), axis=2, keepdims=True)
            dden = jnp.where(absd > e_m, -s1 * r * r * jnp.sign(den), 0.0)
            dnum = jnp.where(lane == NLANE, dden, jnp.where(lane < NLANE, dhv * r, 0.0))
            dsc = _bdot(dnum, v_aug, 2, 2)
            dv = _bdot(sc, dnum, 1, 1)
            gmat = dsc * sc
            dqk = dsc * w_intra
            dq = _bdot(dqk, k, 2, 1) + w_inter * _bdot(dnum, c_prev, 2, 2)
            dk = _bdot(dqk, q, 1, 1)
            dc_prev = _bdot(q * w_inter, dnum, 1, 1)
            dlog_inter = jnp.sum(dnum * f["qc"], axis=2, keepdims=True) * w_inter
            dbcum_col = dlog_inter + jnp.sum(gmat, axis=2, keepdims=True)
            g_row = jnp.sum(gmat, axis=1, keepdims=True)
            dcn = dc_sc[...]
            w_k, decay = f["w_k"], f["decay"]
            kw = k * w_k
            dc_prev = dc_prev + decay * dcn
            db_last = jnp.sum(jnp.sum(dcn * c_prev, axis=2, keepdims=True), axis=1, keepdims=True) * decay
            dkw = _bdot(v_aug, dcn, 2, 2)
            dv = dv + _bdot(kw, dcn, 2, 1)
            dk = dk + dkw * w_k
            dlogw = jnp.sum(dkw * k, axis=2, keepdims=True) * w_k
            db_last = db_last + jnp.sum(dlogw, axis=1, keepdims=True)
            dbcum_col = dbcum_col - dlogw
            rowi = lax.broadcasted_iota(jnp.int32, (1, L, 1), 1)
            dbcum_col = dbcum_col + jnp.where(rowi == L - 1, db_last, 0.0)
            dbcum_row = jnp.sum(jnp.where(eye, dbcum_col, 0.0), axis=1, keepdims=True) - g_row
            dlf_col = jnp.sum(jnp.where(tri_t, dbcum_row, 0.0), axis=2, keepdims=True)
            dli_col = dlogw + jnp.sum(jnp.where(eye, g_row, 0.0), axis=2, keepdims=True)
            dgf_col = dlf_col * _sigmoid(-gf)
            lane_g = lax.broadcasted_iota(jnp.int32, (L, GROUP), 1)
            dg = jnp.zeros((L, GROUP), F32)
            for h in range(H4):
                dg = dg + jnp.where(lane_g == h, dli_col[h], 0.0) + jnp.where(lane_g == H4 + h, dgf_col[h], 0.0)
            dqk_ref[0:H4, sl, :] = dq
            dqk_ref[H4:2 * H4, sl, :] = dk * kscale
            dv_ref[:, sl, :] = jnp.where(lane < NLANE, dv, 0.0).astype(BF16)
            dg_ref[0, sl, :] = dg.astype(BF16)
            dbg_ref[...] += jnp.sum(dg, axis=0, keepdims=True)
            dc_sc[...] = dc_prev

    def hspec(blk):
        return pl.BlockSpec((H4, rows, GROUP), lambda i: (blk, nb - 1 - i, 0))

    gspec = pl.BlockSpec((1, rows, GROUP), lambda i: (17, nb - 1 - i, 0))
    qkspec = pl.BlockSpec((2 * H4, rows, GROUP), lambda i: (0, nb - 1 - i, 0))
    return pl.pallas_call(
        body, name="mlstm_bwd", grid=(nb,),
        in_specs=[qkspec, hspec(2), gspec, pl.BlockSpec((1, GROUP), lambda i: (0, 0)),
                  pl.BlockSpec((cb, H4, GROUP, GROUP), lambda i: (nb - 1 - i, 0, 0, 0)),
                  pl.BlockSpec((cb, H4, 1, LANES), lambda i: (nb - 1 - i, 0, 0, 0)), hspec(0)],
        out_specs=[qkspec, hspec(0), pl.BlockSpec((1, rows, GROUP), lambda i: (0, nb - 1 - i, 0)),
                   pl.BlockSpec((1, GROUP), lambda i: (0, 0))],
        out_shape=[SDS((2 * H4, S, GROUP), F32), SDS((H4, S, GROUP), BF16),
                   SDS((1, S, GROUP), BF16), SDS((1, GROUP), F32)],
        scratch_shapes=[pltpu.VMEM((H4, GROUP, GROUP), F32)],
        compiler_params=_params(("arbitrary",)),
    )(qk, u, u, bg, cst, mst, dh)


def head_norm_fwd(hm, u, hg):
    _, S, _ = hm.shape
    ts = _tile(S, 512)

    def body(h_ref, o_ref, g_ref, t_ref):
        h = h_ref[0]
        lane = lax.broadcasted_iota(jnp.int32, h.shape, 1)
        valid = lane < ML_HEAD_DIM
        mu = jnp.sum(h, axis=-1, keepdims=True) * (1.0 / ML_HEAD_DIM)
        hc = jnp.where(valid, h - mu, 0.0)
        var = jnp.sum(hc * hc, axis=-1, keepdims=True) * (1.0 / ML_HEAD_DIM)
        hn = hc * lax.rsqrt(var + LN_EPS) * g_ref[0]
        t_ref[0] = (_sigmoid(o_ref[0]) * hn).astype(BF16)

    return pl.pallas_call(
        body, name="head_norm_fwd", grid=(H4, S // ts),
        in_specs=[pl.BlockSpec((1, ts, GROUP), lambda h, s: (h, s, 0)), pl.BlockSpec((1, ts, GROUP), lambda h, s: (12 + h, s, 0)),
                  pl.BlockSpec((1, 1, GROUP), lambda h, s: (h, 0, 0))],
        out_specs=pl.BlockSpec((1, ts, GROUP), lambda h, s: (h, s, 0)),
        out_shape=SDS((H4, S, GROUP), BF16),
        compiler_params=_params(("parallel", "parallel")),
    )(hm, u, hg)


def head_norm_bwd(hm, u, hg, dm):
    _, S, _ = hm.shape
    ts = _tile(S, 512)

    def body(h_ref, o_ref, g_ref, d_ref, dh_ref, do_ref, dg_ref):
        @pl.when(pl.program_id(1) == 0)
        def _():
            dg_ref[...] = jnp.zeros_like(dg_ref)

        h = h_ref[0]
        lane = lax.broadcasted_iota(jnp.int32, h.shape, 1)
        valid = lane < ML_HEAD_DIM
        inv = 1.0 / ML_HEAD_DIM
        mu = jnp.sum(h, axis=-1, keepdims=True) * inv
        hc = jnp.where(valid, h - mu, 0.0)
        var = jnp.sum(hc * hc, axis=-1, keepdims=True) * inv
        rstd = lax.rsqrt(var + LN_EPS)
        xhat = hc * rstd
        g = g_ref[0]
        sig = _sigmoid(o_ref[0])
        dt = jnp.where(valid, d_ref[0], 0.0)
        do_ref[0] = (dt * xhat * g * sig * (1.0 - sig)).astype(BF16)
        dhn = dt * sig
        dg_ref[0] += jnp.sum(dhn * xhat, axis=0, keepdims=True)
        dxh = dhn * g
        m1 = jnp.sum(dxh, axis=-1, keepdims=True) * inv
        m2 = jnp.sum(dxh * xhat, axis=-1, keepdims=True) * inv
        dh_ref[0] = jnp.where(valid, rstd * (dxh - m1 - xhat * m2), 0.0)

    spec = pl.BlockSpec((1, ts, GROUP), lambda h, s: (h, s, 0))
    gspec = pl.BlockSpec((1, 1, GROUP), lambda h, s: (h, 0, 0))
    return pl.pallas_call(
        body, name="head_norm_bwd", grid=(H4, S // ts),
        in_specs=[spec, pl.BlockSpec((1, ts, GROUP), lambda h, s: (12 + h, s, 0)), gspec, spec],
        out_specs=[spec, spec, gspec],
        out_shape=[SDS((H4, S, GROUP), F32), SDS((H4, S, GROUP), BF16), SDS((H4, 1, GROUP), F32)],
        compiler_params=_params(("parallel", "arbitrary")),
    )(hm, u, hg, dm)


def adamw(w, g, m, v, name):
    R, C = w.shape
    tr = R if R <= 512 else next(d for d in (512, 256, 128, 64, 32, 16, 8) if R % d == 0)
    c1 = 1.0 / (1.0 - ADAM_B1 ** ADAM_STEP)
    c2 = 1.0 / (1.0 - ADAM_B2 ** ADAM_STEP)

    def body(w_ref, g_ref, m_ref, v_ref, d_ref, nm_ref, nv_ref):
        gv = g_ref[...]
        nm = ADAM_B1 * m_ref[...] + (1.0 - ADAM_B1) * gv
        nv = ADAM_B2 * v_ref[...] + (1.0 - ADAM_B2) * (gv * gv)
        d_ref[...] = -ADAM_LR * ((nm * c1) / (jnp.sqrt(nv * c2) + ADAM_EPS) + ADAM_WD * w_ref[...])
        nm_ref[...] = nm
        nv_ref[...] = nv

    spec = pl.BlockSpec((tr, C), lambda i: (i, 0))
    return pl.pallas_call(
        body, name=name, grid=(R // tr,),
        in_specs=[spec] * 4, out_specs=[spec] * 3,
        out_shape=[SDS((R, C), F32)] * 3,
        compiler_params=_params(("parallel",)),
    )(w, g, m, v)


HBM = pl.BlockSpec(memory_space=pl.ANY)
ROW_SPLIT = 4


def _position():
    x, y, c = lax.axis_index("x"), lax.axis_index("y"), lax.axis_index("c")
    return x, y, c, [(1 - x, y), (x, 1 - y), (1 - x, 1 - y)]


def _unique(items):
    arrays = []
    for a, _ in items:
        if not any(a is b for b in arrays):
            arrays.append(a)
    return arrays, [next(i for i, b in enumerate(arrays) if b is a) for a, _ in items]


def place_own(items, me):
    arrays, src_of = _unique(items)
    n = len(items)
    shapes = [a.shape[len(p):] for a, p in items]

    def body(me_ref, *refs):
        for t in range(n):
            refs[n + t][...] = jnp.zeros_like(refs[n + t])
            refs[n + t][me_ref[0]] = refs[t][(0,) * len(items[t][1])]

    in_specs, out_specs = [], []
    for (a, p), shp in zip(items, shapes):
        blk = shp[:-2] + (shp[-2] // ROW_SPLIT, shp[-1])
        lead = (0,) * (len(shp) - 2)
        in_specs.append(pl.BlockSpec((1,) * len(p) + blk, functools.partial(lambda r, me_ref, p, lead: p + lead + (r, 0), p=p, lead=lead)))
        out_specs.append(pl.BlockSpec((N_CHIPS,) + blk, functools.partial(lambda r, me_ref, lead: (0,) + lead + (r, 0), lead=lead)))
    return pl.pallas_call(
        body, name="place_own",
        grid_spec=pltpu.PrefetchScalarGridSpec(num_scalar_prefetch=1, grid=(ROW_SPLIT,), in_specs=in_specs, out_specs=out_specs),
        out_shape=[SDS((N_CHIPS,) + tuple(shp), a.dtype) for shp, (a, _) in zip(shapes, items)],
        compiler_params=_params(("parallel",)),
    )(me, *[arrays[i] for i in src_of])


SEM = pl.BlockSpec(memory_space=pltpu.SEMAPHORE)
IN_HBM = pl.BlockSpec(memory_space=pltpu.HBM)
DATAFLOW = pltpu.SideEffectType.DATAFLOW_SIDE_EFFECTING


def split_start(bufs, plan, n_copies, after, name):
    n = len(bufs)

    def body(*refs):
        send, recv, token = refs[n + 1], refs[n + 2], refs[-1]
        x, y, c, chips = _position()
        for k, (src, dst, dev) in enumerate(plan(refs[:n], x, y, c, chips)):
            pltpu.make_async_remote_copy(src_ref=src, dst_ref=dst, send_sem=send.at[k], recv_sem=recv.at[k],
                                         device_id=dev, device_id_type=MESH).start()
        token[...] = jnp.zeros_like(token)

    out = pl.pallas_call(
        body, name=name,
        out_shape=(pltpu.SemaphoreType.DMA((n_copies,)), pltpu.SemaphoreType.DMA((n_copies,)),
                   *[pltpu.HBM(b.shape, b.dtype) for b in bufs], SDS((8, LANES), F32)),
        in_specs=[IN_HBM] * n + [pl.BlockSpec(memory_space=pl.ANY)],
        out_specs=(SEM, SEM, *[IN_HBM] * n, pl.BlockSpec(memory_space=pltpu.VMEM)),
        input_output_aliases={i: 2 + i for i in range(n)},
        compiler_params=pltpu.CompilerParams(has_side_effects=DATAFLOW),
    )(*[pltpu.with_memory_space_constraint(b, pltpu.HBM) for b in bufs], after)
    return out[0], out[1], list(out[2:2 + n]), out[-1]


def split_wait(send, recv, bufs, plan, after, name):
    n = len(bufs)

    def body(*refs):
        send_ref, recv_ref = refs[n], refs[n + 1]
        x, y, c, chips = _position()
        for k, (src, dst, dev) in enumerate(plan(refs[:n], x, y, c, chips)):
            cp = pltpu.make_async_remote_copy(src_ref=src, dst_ref=dst, send_sem=send_ref.at[k], recv_sem=recv_ref.at[k],
                                              device_id=dev, device_id_type=MESH)
            cp.wait_send()
            cp.wait_recv()

    return list(pl.pallas_call(
        body, name=name, out_shape=tuple(pltpu.HBM(b.shape, b.dtype) for b in bufs),
        in_specs=[IN_HBM] * n + [SEM, SEM, pl.BlockSpec(memory_space=pl.ANY)], out_specs=tuple([IN_HBM] * n),
        input_output_aliases={i: i for i in range(n)},
        compiler_params=pltpu.CompilerParams(has_side_effects=DATAFLOW),
    )(*bufs, send, recv, after))


def _gather_plan(shapes, landing):
    n = len(shapes)

    def plan(refs, x, y, c, chips):
        out = []
        for t in range(n):
            half = shapes[t][0] // 2
            rows = pl.ds(c * half, half)
            for cx, cy in chips:
                slot = 2 * cx + cy if landing else 2 * x + y
                out.append((refs[t].at[rows], refs[n + t].at[slot, rows], (cx, cy, c)))
        return out

    return plan


def gather_start(shards, placed, after, name):
    shapes = [s.shape for s in shards]
    send, recv, bufs, token = split_start(shards + placed, _gather_plan(shapes, False), 3 * len(shards), after, name)
    return (send, recv, bufs, shapes), token


def gather_wait(state, after, name):
    send, recv, bufs, shapes = state
    return split_wait(send, recv, bufs, _gather_plan(shapes, True), after, name)[len(shapes):]


def gather_pass_on(placed, shapes, name):
    n = len(placed)

    def body(*refs):
        outs, send, recv = refs[n:2 * n], refs[2 * n], refs[2 * n + 1]
        x, y, c, chips = _position()
        cps = []
        for t in range(n):
            half = shapes[t][0] // 2
            for j, (cx, cy) in enumerate(chips):
                piece = outs[t].at[2 * cx + cy, pl.ds(c * half, half)]
                cp = pltpu.make_async_remote_copy(src_ref=piece, dst_ref=piece, send_sem=send.at[3 * t + j], recv_sem=recv.at[3 * t + j],
                                                  device_id=(x, y, 1 - c), device_id_type=MESH)
                cp.start()
                cps.append(cp)
        for t in range(n):
            half = shapes[t][0] // 2
            for j, (cx, cy) in enumerate(chips):
                piece = outs[t].at[2 * cx + cy, pl.ds((1 - c) * half, half)]
                pltpu.make_async_remote_copy(src_ref=piece, dst_ref=piece, send_sem=send.at[3 * t + j], recv_sem=recv.at[3 * t + j],
                                             device_id=(x, y, 1 - c), device_id_type=MESH).wait_recv()
        for cp in cps:
            cp.wait_send()

    return pl.pallas_call(
        body, name=name,
        in_specs=[HBM] * n, out_specs=[HBM] * n,
        out_shape=[SDS(p.shape, p.dtype) for p in placed],
        input_output_aliases={t: t for t in range(n)},
        scratch_shapes=[pltpu.SemaphoreType.DMA((3 * n,))] * 2,
    )(*placed)


def _flip(k, x, y, c):
    return ((1 - x) if k & 4 else x, (1 - y) if k & 2 else y, (1 - c) if k & 1 else c)


def small_allgather(v, reduce):
    R, C = v.shape

    def body(v_ref, o_ref, *scratch):
        if reduce:
            buf, send, recv = scratch
        else:
            buf, (send, recv) = o_ref, scratch
        x, y, c, _ = _position()
        me = 4 * x + 2 * y + c
        buf[me] = v_ref[...]
        sends = []
        for k in range(1, N_DEV):
            cp = pltpu.make_async_remote_copy(src_ref=v_ref, dst_ref=buf.at[me], send_sem=send.at[k - 1], recv_sem=recv.at[k - 1],
                                              device_id=_flip(k, x, y, c), device_id_type=MESH)
            cp.start()
            sends.append(cp)
        for k in range(1, N_DEV):
            px, py, pc = _flip(k, x, y, c)
            pltpu.make_async_remote_copy(src_ref=v_ref, dst_ref=buf.at[4 * px + 2 * py + pc], send_sem=send.at[k - 1],
                                         recv_sem=recv.at[k - 1], device_id=(px, py, pc), device_id_type=MESH).wait_recv()
        for cp in sends:
            cp.wait_send()
        if reduce:
            acc = buf[0]
            for i in range(1, N_DEV):
                acc = acc + buf[i]
            o_ref[...] = acc

    vm = pl.BlockSpec(memory_space=pltpu.VMEM)
    sems = [pltpu.SemaphoreType.DMA((N_DEV - 1,)), pltpu.SemaphoreType.DMA((N_DEV - 1,))]
    return pl.pallas_call(
        body, name="small_allreduce" if reduce else "small_allgather",
        in_specs=[vm], out_specs=vm,
        out_shape=SDS((R, C) if reduce else (N_DEV, R, C), F32),
        scratch_shapes=([pltpu.VMEM((N_DEV, R, C), F32)] if reduce else []) + sems,
    )(v)


def rs_exchange_sibling(gs):
    n = len(gs)

    def body(*refs):
        ins, outs, send, recv = refs[:n], refs[n:2 * n], refs[2 * n], refs[2 * n + 1]
        x, y, c, _ = _position()
        cps = []
        for t in range(n):
            cp = pltpu.make_async_remote_copy(src_ref=ins[t].at[:, 1 - c], dst_ref=outs[t], send_sem=send.at[t], recv_sem=recv.at[t],
                                              device_id=(x, y, 1 - c), device_id_type=MESH)
            cp.start()
            cps.append(cp)
        for cp in cps:
            cp.wait()

    return pl.pallas_call(
        body, name="rs_exchange_sibling", in_specs=[HBM] * n, out_specs=[HBM] * n,
        out_shape=[SDS((g.shape[0],) + g.shape[2:], g.dtype) for g in gs],
        scratch_shapes=[pltpu.SemaphoreType.DMA((n,)), pltpu.SemaphoreType.DMA((n,))],
    )(*gs)


def rs_pair_add(gs, rs, c):
    n = len(gs)

    def body(c_ref, *refs):
        for t in range(n):
            refs[2 * n + t][0] = (refs[t][0, 0].astype(F32) + refs[n + t][0].astype(F32)).astype(BF16)

    in_specs, out_specs, out_shape = [], [], []
    for g in gs:
        _, _, h, C = g.shape
        in_specs.append(pl.BlockSpec((1, 1, h // ROW_SPLIT, C), lambda j, r, c_ref: (j, c_ref[0], r, 0)))
    for g in gs:
        _, _, h, C = g.shape
        spec = pl.BlockSpec((1, h // ROW_SPLIT, C), lambda j, r, c_ref: (j, r, 0))
        in_specs.append(spec)
        out_specs.append(spec)
        out_shape.append(SDS((N_CHIPS, h, C), BF16))
    return pl.pallas_call(
        body, name="rs_pair_add",
        grid_spec=pltpu.PrefetchScalarGridSpec(num_scalar_prefetch=1, grid=(N_CHIPS, ROW_SPLIT), in_specs=in_specs, out_specs=out_specs),
        out_shape=out_shape, compiler_params=_params(("parallel", "parallel")),
    )(c, *gs, *rs)


def _rs_plan(n):
    def plan(refs, x, y, c, chips):
        return [(refs[t].at[2 * cx + cy], refs[n + t].at[j], (cx, cy, c)) for t in range(n) for j, (cx, cy) in enumerate(chips)]

    return plan


def rs_chip_add(ps, qs, me_c):
    n = len(ps)

    def body(me_ref, *refs):
        for t in range(n):
            q = refs[n + t]
            refs[2 * n + t][...] = jnp.zeros_like(refs[2 * n + t])
            refs[2 * n + t][me_ref[1]] = ((refs[t][0].astype(F32) + q[0].astype(F32)) + q[1].astype(F32)) + q[2].astype(F32)

    in_specs, out_specs, out_shape = [], [], []
    for p in ps:
        _, h, C = p.shape
        in_specs.append(pl.BlockSpec((1, h // ROW_SPLIT, C), lambda r, me_ref: (me_ref[0], r, 0)))
    for p in ps:
        _, h, C = p.shape
        in_specs.append(pl.BlockSpec((3, h // ROW_SPLIT, C), lambda r, me_ref: (0, r, 0)))
        out_specs.append(pl.BlockSpec((2, h // ROW_SPLIT, C), lambda r, me_ref: (0, r, 0)))
        out_shape.append(SDS((2, h, C), F32))
    return pl.pallas_call(
        body, name="rs_chip_add",
        grid_spec=pltpu.PrefetchScalarGridSpec(num_scalar_prefetch=1, grid=(ROW_SPLIT,), in_specs=in_specs, out_specs=out_specs),
        out_shape=out_shape, compiler_params=_params(("parallel",)),
    )(me_c, *ps, *qs)


def rs_share(rs):
    n = len(rs)

    def body(*refs):
        outs, send, recv = refs[n:2 * n], refs[2 * n], refs[2 * n + 1]
        x, y, c, _ = _position()
        cps = []
        for t in range(n):
            cp = pltpu.make_async_remote_copy(src_ref=outs[t].at[c], dst_ref=outs[t].at[c], send_sem=send.at[t], recv_sem=recv.at[t],
                                              device_id=(x, y, 1 - c), device_id_type=MESH)
            cp.start()
            cps.append(cp)
        for cp in cps:
            cp.wait()

    return pl.pallas_call(
        body, name="rs_share", in_specs=[HBM] * n, out_specs=[HBM] * n,
        out_shape=[SDS(r.shape, r.dtype) for r in rs],
        input_output_aliases={t: t for t in range(n)},
        scratch_shapes=[pltpu.SemaphoreType.DMA((n,))] * 2,
    )(*rs)


def rs_begin(gs, name):
    c = lax.axis_index("c")
    n = len(gs)
    g5 = [g.reshape(N_CHIPS, 2, g.shape[1] // 2, g.shape[2]) for g in gs]
    from_sibling = rs_exchange_sibling(g5)
    pair = rs_pair_add(g5, from_sibling, jnp.reshape(c, (1,)).astype(jnp.int32))
    lands = [jnp.zeros((3,) + p.shape[1:], p.dtype) for p in pair]
    send, recv, bufs, token = split_start(list(pair) + lands, _rs_plan(n), 3 * n, from_sibling[0], name)
    return (send, recv, bufs, [g.shape for g in gs]), token


def rs_end(state, after, name):
    x, y, c = lax.axis_index("x"), lax.axis_index("y"), lax.axis_index("c")
    send, recv, bufs, shapes = state
    n = len(shapes)
    bufs = split_wait(send, recv, bufs, _rs_plan(n), after, name)
    half = rs_chip_add(bufs[:n], bufs[n:], jnp.stack([2 * x + y, c]).astype(jnp.int32))
    both = rs_share(half)
    return [b.reshape(s[1], s[2]) for b, s in zip(both, shapes)]


def _pad_last(a, n):
    return jnp.pad(a, [(0, 0)] * (a.ndim - 1) + [(0, n - a.shape[-1])])


def _heads_to_groups(w):
    k = w.shape[0]
    return _pad_last(w.reshape(k, ML_HEADS, ML_HEAD_DIM).transpose(1, 0, 2), GROUP)


def _groups_to_heads(g):
    return g[:, :, :ML_HEAD_DIM].transpose(1, 0, 2).reshape(g.shape[1], D_TOK)


def _cols_to_groups(w):
    k, n = w.shape
    return w.reshape(k, n // GROUP, GROUP).transpose(1, 0, 2)


def _groups_to_cols(g):
    n, k, _ = g.shape
    return g.transpose(1, 0, 2).reshape(k, n * GROUP)


def _chips_to_cols(a):
    return a.transpose(1, 0, 2).reshape(a.shape[1], -1)


def _cols_to_chips(w):
    k, n = w.shape
    return w.reshape(k, N_CHIPS, n // N_CHIPS).transpose(1, 0, 2)


def _mlstm_in_groups(w):
    parts = [_heads_to_groups(w[:, i * D_TOK:(i + 1) * D_TOK]) for i in range(4)]
    gates = _pad_last(w[:, 4 * D_TOK:4 * D_TOK + 2 * ML_HEADS], GROUP)[None]
    qmem = w[:, 4 * D_TOK + 2 * ML_HEADS:][None]
    return jnp.concatenate(parts + [qmem, gates], axis=0)


def _mlstm_in_ungroup(g):
    parts = [_groups_to_heads(g[4 * i:4 * i + 4]) for i in range(4)]
    return jnp.concatenate(parts + [g[17][:, :2 * ML_HEADS], g[16]], axis=1)


def _taps_to_groups(w, width):
    taps = w.shape[0]
    g = _pad_last(w.reshape(taps, -1, width), GROUP).transpose(1, 0, 2)
    return jnp.pad(g, ((0, 0), (0, 8 - taps), (0, 0)))


def _groups_to_taps(g, taps, width):
    return g[:, :taps, :width].transpose(1, 0, 2).reshape(taps, -1)


SMALL_ROWS = 24
SMALL_IN_COLS = 384
SMALL_OUT_COLS = 1536


class _Gathered:
    def __init__(self, srcs, groups, me):
        keys = [k for g in groups for k in g]
        placed = dict(zip(keys, place_own([(srcs[k], ()) for k in keys], me)))
        self.groups, self.states, self.ready = groups, [], {}
        self.group_of = {k: gi for gi, g in enumerate(groups) for k in g}
        token = me
        for gi, g in enumerate(groups):
            state, token = gather_start([srcs[k] for k in g], [placed[k] for k in g], token, f"gather_start_{gi}")
            self.states.append(state)
        self.started = token

    def _get(self, key, after):
        gi = self.group_of[key]
        if gi not in self.ready:
            got = gather_wait(self.states[gi], after if gi else self.started, f"gather_wait_{gi}")
            self.ready[gi] = dict(zip(self.groups[gi], gather_pass_on(got, self.states[gi][3], f"gather_pass_on_{gi}")))
        return self.ready[gi][key]

    def ffn(self, l, i, after):
        return tuple(self._get((n, l, i), after) for n in ("wg", "wu", "wd"))

    def mixer(self, l, after):
        win = _chips_to_cols(self._get(("win", l), after))
        win = _cols_to_groups(win) if l % 2 == 0 else _mlstm_in_groups(win)
        wkv = _cols_to_groups(self._get(("wkv", l), after).reshape(D_MODEL, 2 * D_XA))
        wout = self._get(("wout", l), after)
        if l % 2:
            wout = wout.reshape(D_MODEL, D_MODEL)
            tok = jnp.pad(wout[:D_TOK].reshape(ML_HEADS, ML_HEAD_DIM, D_MODEL), ((0, 0), (0, GROUP - ML_HEAD_DIM), (0, 0)))
            wout = jnp.concatenate([tok, wout[D_TOK:][None]], axis=0)
        return win, wkv, wout


class _GradSink:
    def __init__(self):
        self.queue, self.done, self.count = [], {}, 0

    @staticmethod
    def _by_chip(key, g):
        if key[0] == "wkv":
            return _groups_to_cols(g).reshape(N_CHIPS, D_MODEL // N_CHIPS, 2 * D_XA)
        if key[0] == "win":
            return _cols_to_chips(_groups_to_cols(g) if key[1] % 2 == 0 else _mlstm_in_ungroup(g))
        if key[0] == "wout" and key[1] % 2:
            full = jnp.concatenate([g[:ML_HEADS, :ML_HEAD_DIM].reshape(D_TOK, D_MODEL), g[ML_HEADS]], axis=0)
            return full.reshape(N_CHIPS, D_MODEL // N_CHIPS, D_MODEL)
        return g

    def begin(self, grads):
        keys = list(grads)
        state, token = rs_begin([self._by_chip(k, grads[k]) for k in keys], f"rs_start_{self.count}")
        self.queue.append((keys, state, self.count))
        self.count += 1
        return token

    def end(self, after):
        keys, state, i = self.queue.pop(0)
        self.done.update(zip(keys, rs_end(state, after, f"rs_wait_{i}")))


def _local_step(x, mem, tgt, P, weights, sink):
    memb = mem.astype(BF16)
    saved = []
    X, Xb = x, x.astype(BF16)
    after = Xb
    for l in range(DEPTH):
        s = {}
        s["x0b"] = Xb
        s["wa"] = weights.ffn(l, 0, after)
        s["g1a"], s["u1a"], s["ha"] = ffn_up(Xb, s["wa"][0], s["wa"][1])
        s["z1"], X1, X1b = contract_ln(s["ha"], s["wa"][2], X, P["ln_g"][l][0], P["ln_b"][l][0], 0.5, "ffn_down_ln")
        s["x1b"] = X1b
        s["wm"] = win, wkv, wout = weights.mixer(l, X1b)
        u = proj(X1b, win, "mixer_in")
        kv = proj(memb, wkv, "mem_kv")
        s["u"], s["kv"] = u, kv
        if l % 2 == 0:
            tok = conv_mixer_fwd(u, P["convw"])
            qg = 9
        else:
            s["qk"] = qk_conv_fwd(u, P["qkw"])
            s["hm"], s["cst"], s["mst"] = mlstm_fwd(s["qk"], u, P["bg"])
            tok = head_norm_fwd(s["hm"], u, P["hg"])
            qg = 16
        xa = xattn_fwd(u, qg, kv)
        s["m"] = jnp.concatenate([tok, xa], axis=0)
        s["z2"], X2, X2b = contract_ln(s["m"], wout, X1, P["ln_g"][l][1], P["ln_b"][l][1], 1.0, "mixer_out_ln")
        s["x2b"] = X2b
        s["wb"] = weights.ffn(l, 1, X2b)
        s["g1b"], s["u1b"], s["hb"] = ffn_up(X2b, s["wb"][0], s["wb"][1])
        s["z3"], X, Xb = contract_ln(s["hb"], s["wb"][2], X2, P["ln_g"][l][2], P["ln_b"][l][2], 0.5, "ffn_down_ln")
        after = Xb
        saved.append(s)

    loss, dX = loss_grad(X, tgt)

    G = {"ln_g": [[None] * 3 for _ in range(DEPTH)], "ln_b": [[None] * 3 for _ in range(DEPTH)]}
    pin = [jnp.zeros((1, 1), F32)]

    def ffn_backward(l, i, dX, z, xinb, g1, u1, h, w):
        k = 2 * i
        dz, dyb, G["ln_g"][l][k], G["ln_b"][l][k] = ln_bwd(dX, z, P["ln_g"][l][k] + pin[0], 0.5, "ffn_ln_bwd")
        dgb, dub = ffn_bwd_dh(dyb, w[2], g1, u1)
        grads = {("wd", l, i): wgrad(h, dyb, BF16, "wgrad_down"), ("wg", l, i): wgrad(xinb, dgb, BF16, "wgrad_gate"),
                 ("wu", l, i): wgrad(xinb, dub, BF16, "wgrad_up")}
        return contract_t([(dgb, w[0]), (dub, w[1])], dz, "ffn_bwd_dx"), grads

    pending = 0
    for l in reversed(range(DEPTH)):
        s = saved[l]
        win, wkv, wout = s["wm"]
        dX, grads = ffn_backward(l, 1, dX, s["z3"], s["x2b"], s["g1b"], s["u1b"], s["hb"], s["wb"])
        if pending:
            sink.end(dX)
        dz2, dz2b, G["ln_g"][l][1], G["ln_b"][l][1] = ln_bwd(dX, s["z2"], P["ln_g"][l][1], 1.0, "mixer_ln_bwd")
        dm = proj_t(dz2b, wout, "mixer_out_bwd")
        grads[("wout", l)] = wgrad(s["m"], dz2b, BF16, "wgrad_out")
        u, kv = s["u"], s["kv"]
        if l % 2 == 0:
            db, dc, dxi, G["convw"] = conv_mixer_bwd(u, P["convw"], dm)
            dq, dkv = xattn_bwd(u, 9, kv, dm, 3)
            du = jnp.concatenate([db, dc, dxi, dq], axis=0)
        else:
            dh, do, G["hg"] = head_norm_bwd(s["hm"], u, P["hg"], dm)
            dqk, dv, dgate, G["bg"] = mlstm_bwd(s["qk"], u, P["bg"], s["cst"], s["mst"], dh)
            duqk, G["qkw"] = qk_conv_bwd(u, P["qkw"], dqk)
            dq, dkv = xattn_bwd(u, 16, kv, dm, 4)
            du = jnp.concatenate([duqk, dv, do, dq, dgate], axis=0)
        grads[("win", l)] = wgrad(s["x1b"], du, BF16, "wgrad_in")
        grads[("wkv", l)] = wgrad(memb, dkv.astype(BF16), BF16, "wgrad_kv")
        dX = contract_t([(du, win)], dz2, "mixer_in_bwd")
        pin[0] = sink.begin(grads)[0:1, 0:1]
        dX, grads = ffn_backward(l, 0, dX, s["z1"], s["x0b"], s["g1a"], s["u1a"], s["ha"], s["wa"])
        sink.end(dX)
        pin[0] = sink.begin(grads)[0:1, 0:1]
        pending = 1
    sink.end(dX)
    return loss, dX, G


def kernel(x, mem, ln_g, ln_b, ffn_w_gate, ffn_w_up, ffn_w_down, w_kv_mem, w_out, w_in_conv, conv_w, w_in_mlstm, b_gates, qk_conv_w, head_norm_g, loss_target, m_ln_g, m_ln_b, m_ffn_w_gate, m_ffn_w_up, m_ffn_w_down, m_w_kv_mem, m_w_out, m_w_in_conv, m_conv_w, m_w_in_mlstm, m_b_gates, m_qk_conv_w, m_head_norm_g, v_ln_g, v_ln_b, v_ffn_w_gate, v_ffn_w_up, v_ffn_w_down, v_w_kv_mem, v_w_out, v_w_in_conv, v_conv_w, v_w_in_mlstm, v_b_gates, v_qk_conv_w, v_head_norm_g):
    cx, cy = lax.axis_index("x"), lax.axis_index("y")
    chip = 2 * cx + cy

    srcs = {}
    for l in range(DEPTH):
        for i in range(2):
            srcs[("wg", l, i)] = ffn_w_gate[l, i].astype(BF16)
            srcs[("wu", l, i)] = ffn_w_up[l, i].astype(BF16)
            srcs[("wd", l, i)] = ffn_w_down[l, i].astype(BF16)
        srcs[("wkv", l)] = w_kv_mem[l].astype(BF16)
        srcs[("wout", l)] = w_out[l].astype(BF16)
    srcs[("win", 0)] = w_in_conv[0].astype(BF16)
    srcs[("win", 1)] = w_in_mlstm[0].astype(BF16)
    ffn_keys = lambda l, i: [("wg", l, i), ("wu", l, i), ("wd", l, i)]
    mixer_keys = lambda l: [("win", l), ("wkv", l), ("wout", l)]
    groups = [ffn_keys(0, 0), mixer_keys(0) + mixer_keys(1) + ffn_keys(0, 1), ffn_keys(1, 0), ffn_keys(1, 1)]
    gathered = _Gathered(srcs, groups, jnp.reshape(chip, (1,)).astype(jnp.int32))

    small = jnp.zeros((SMALL_ROWS, SMALL_IN_COLS), F32)
    small = small.at[0:6, 0:256].set(ln_g.reshape(6, 256)).at[6:12, 0:256].set(ln_b.reshape(6, 256))
    small = small.at[12:15, 0:192].set(conv_w[0]).at[16:20, 0:384].set(qk_conv_w[0])
    smalls = small_allgather(small, reduce=False)[0::2]
    ln_g_full = _chips_to_cols(smalls[:, 0:6, 0:256]).reshape(DEPTH, 3, 1, D_MODEL)
    ln_b_full = _chips_to_cols(smalls[:, 6:12, 0:256]).reshape(DEPTH, 3, 1, D_MODEL)
    conv_w_full = _chips_to_cols(smalls[:, 12:15, 0:192])
    qk_w_full = _chips_to_cols(smalls[:, 16:20, 0:384])

    P = {"ln_g": ln_g_full, "ln_b": ln_b_full, "convw": _taps_to_groups(conv_w_full, GROUP),
         "qkw": _taps_to_groups(qk_w_full, ML_HEAD_DIM), "bg": _pad_last(b_gates, GROUP),
         "hg": _pad_last(head_norm_g[0], GROUP)[:, None, :]}

    sink = _GradSink()
    loss, grad_x, G = _local_step(x[0], mem[0], loss_target[0], P, gathered, sink)
    red = sink.done

    sg = jnp.zeros((SMALL_ROWS, SMALL_OUT_COLS), F32)
    dln_g = jnp.concatenate([G["ln_g"][l][k] for l in range(DEPTH) for k in range(3)], axis=0)
    dln_b = jnp.concatenate([G["ln_b"][l][k] for l in range(DEPTH) for k in range(3)], axis=0)
    sg = sg.at[0:6, 0:D_MODEL].set(dln_g).at[6:12, 0:D_MODEL].set(dln_b)
    sg = sg.at[12:15, 0:D_TOK].set(_groups_to_taps(G["convw"], 3, GROUP))
    sg = sg.at[15:16, 0:8].set(G["bg"][:, 0:8]).at[15:16, 8:9].set(loss)
    sg = sg.at[16:20, 0:2 * D_TOK].set(_groups_to_taps(G["qkw"], 4, ML_HEAD_DIM))
    sg = sg.at[20:24, 0:ML_HEAD_DIM].set(G["hg"][:, 0, :ML_HEAD_DIM])
    tot = small_allgather(sg, reduce=True)

    grads = {
        "ln_g": lax.dynamic_slice(tot[0:6, 0:D_MODEL], (0, chip * 256), (6, 256)).reshape(DEPTH, 3, 256),
        "ln_b": lax.dynamic_slice(tot[6:12, 0:D_MODEL], (0, chip * 256), (6, 256)).reshape(DEPTH, 3, 256),
        "ffn_w_gate": jnp.stack([jnp.stack([red[("wg", l, i)] for i in range(2)]) for l in range(DEPTH)]),
        "ffn_w_up": jnp.stack([jnp.stack([red[("wu", l, i)] for i in range(2)]) for l in range(DEPTH)]),
        "ffn_w_down": jnp.stack([jnp.stack([red[("wd", l, i)] for i in range(2)]) for l in range(DEPTH)]),
        "w_kv_mem": jnp.stack([red[("wkv", l)] for l in range(DEPTH)]),
        "w_out": jnp.stack([red[("wout", l)] for l in range(DEPTH)]),
        "w_in_conv": red[("win", 0)][None],
        "conv_w": lax.dynamic_slice(tot[12:15, 0:D_TOK], (0, chip * 192), (3, 192))[None],
        "w_in_mlstm": red[("win", 1)][None],
        "b_gates": tot[15:16, 0:8],
        "qk_conv_w": lax.dynamic_slice(tot[16:20, 0:2 * D_TOK], (0, chip * 384), (4, 384))[None],
        "head_norm_g": tot[20:24, 0:ML_HEAD_DIM][None],
    }
    loss_total = tot[15, 8]

    weights = {"ln_g": ln_g, "ln_b": ln_b, "ffn_w_gate": ffn_w_gate, "ffn_w_up": ffn_w_up, "ffn_w_down": ffn_w_down,
               "w_kv_mem": w_kv_mem, "w_out": w_out, "w_in_conv": w_in_conv, "conv_w": conv_w, "w_in_mlstm": w_in_mlstm,
               "b_gates": b_gates, "qk_conv_w": qk_conv_w, "head_norm_g": head_norm_g}
    ms = {"ln_g": m_ln_g, "ln_b": m_ln_b, "ffn_w_gate": m_ffn_w_gate, "ffn_w_up": m_ffn_w_up, "ffn_w_down": m_ffn_w_down,
          "w_kv_mem": m_w_kv_mem, "w_out": m_w_out, "w_in_conv": m_w_in_conv, "conv_w": m_conv_w, "w_in_mlstm": m_w_in_mlstm,
          "b_gates": m_b_gates, "qk_conv_w": m_qk_conv_w, "head_norm_g": m_head_norm_g}
    vs = {"ln_g": v_ln_g, "ln_b": v_ln_b, "ffn_w_gate": v_ffn_w_gate, "ffn_w_up": v_ffn_w_up, "ffn_w_down": v_ffn_w_down,
          "w_kv_mem": v_w_kv_mem, "w_out": v_w_out, "w_in_conv": v_w_in_conv, "conv_w": v_conv_w, "w_in_mlstm": v_w_in_mlstm,
          "b_gates": v_b_gates, "qk_conv_w": v_qk_conv_w, "head_norm_g": v_head_norm_g}
    names = list(weights)
    deltas, new_m, new_v = [], [], []
    for nme in names:
        w = weights[nme]
        shp = w.shape
        two = (math.prod(shp[:-1]), shp[-1])
        d, nm, nv = adamw(w.reshape(two), grads[nme].reshape(two), ms[nme].reshape(two), vs[nme].reshape(two), "adamw_" + nme)
        deltas.append(d.reshape(shp))
        new_m.append(nm.reshape(shp))
        new_v.append(nv.reshape(shp))
    return (loss_total, grad_x[None], *[grads[nme] for nme in names], *deltas, *new_m, *new_v)
```

```python
import functools
import math

import jax
import jax.numpy as jnp
from jax import lax
from jax.experimental import pallas as pl
from jax.experimental.pallas import tpu as pltpu

F32 = jnp.float32
BF16 = jnp.bfloat16
SDS = jax.ShapeDtypeStruct

D_MODEL = 1024
DEPTH = 2
N_MEM = 256
XA_HEADS = 4
XA_HEAD_DIM = 64
D_XA = 256
D_TOK = 768
ML_HEADS = 4
ML_HEAD_DIM = 192
ML_CHUNK = 64
D_FF = 2816
LN_EPS = 1e-5
ALPHA = (2.0 * DEPTH) ** 0.25
N_CHIPS = 4
N_DEV = 8
FF_SHARD = D_FF // N_CHIPS
GROUP = 256
NEG = -1e30

ADAM_LR = 0.001
ADAM_B1 = 0.9
ADAM_B2 = 0.999
ADAM_EPS = 1e-08
ADAM_WD = 0.01
ADAM_STEP = 10

VMEM_LIMIT = 56 * 1024 * 1024

NN = ((1,), (0,))
NT = ((1,), (1,))
TN = ((0,), (0,))
MESH = pl.DeviceIdType.MESH


def _dot(a, b, dims):
    return lax.dot_general(a, b, (dims, ((), ())), preferred_element_type=F32)


def _bdot(a, b, ca, cb):
    dims = (((ca,), (cb,)), ((0,), (0,)))
    ah, bh = a.astype(BF16), b.astype(BF16)
    al, bl = (a - ah.astype(F32)).astype(BF16), (b - bh.astype(F32)).astype(BF16)
    dot = functools.partial(lax.dot_general, dimension_numbers=dims, preferred_element_type=F32)
    return dot(ah, bh) + dot(al, bh) + dot(ah, bl)


def _sigmoid(x):
    return 1.0 / (1.0 + jnp.exp(-x))


def _params(sem, vmem=VMEM_LIMIT):
    return pltpu.CompilerParams(dimension_semantics=sem, vmem_limit_bytes=vmem)


def _tile(n, want):
    t = min(n, want)
    assert n % t == 0, (n, t)
    return t


def _layer_norm(z, gamma, beta):
    mu = jnp.mean(z, axis=-1, keepdims=True)
    zc = z - mu
    var = jnp.mean(zc * zc, axis=-1, keepdims=True)
    return zc * lax.rsqrt(var + LN_EPS) * gamma + beta


def _group_block(G, want):
    return max(d for d in range(1, max(1, min(G, want)) + 1) if G % d == 0)


def ffn_fwd(xb, x, wg, wu, wd, gamma, beta):
    S, K = xb.shape
    G, _, N = wg.shape
    ts = _tile(S, 512)

    def body(xb_ref, x_ref, wg_ref, wu_ref, wd_ref, gm_ref, bt_ref, g_ref, u_ref, h_ref, z_ref, xn_ref, xnb_ref):
        j = pl.program_id(1)
        xv = xb_ref[...]
        g = _dot(xv, wg_ref[0], NN)
        u = _dot(xv, wu_ref[0], NN)
        h = (g * _sigmoid(g) * u).astype(BF16)
        g_ref[0] = g
        u_ref[0] = u
        h_ref[0] = h
        y = _dot(h, wd_ref[0], NN)

        @pl.when(j == 0)
        def _():
            z_ref[...] = y

        @pl.when(j > 0)
        def _():
            z_ref[...] += y

        @pl.when(j == G - 1)
        def _():
            z = ALPHA * x_ref[...] + 0.5 * z_ref[...]
            xn = _layer_norm(z, gm_ref[...], bt_ref[...])
            z_ref[...] = z
            xn_ref[...] = xn
            xnb_ref[...] = xn.astype(BF16)

    row = pl.BlockSpec((ts, K), lambda s, j: (s, 0))
    vec = pl.BlockSpec((1, K), lambda s, j: (0, 0))
    wspec = pl.BlockSpec((1, K, N), lambda s, j: (j, 0, 0))
    ospec = pl.BlockSpec((1, ts, N), lambda s, j: (j, s, 0))
    return pl.pallas_call(
        body, name="ffn_fwd", grid=(S // ts, G),
        in_specs=[row, row, wspec, wspec, pl.BlockSpec((1, N, K), lambda s, j: (j, 0, 0)), vec, vec],
        out_specs=[ospec, ospec, ospec, row, row, row],
        out_shape=[SDS((G, S, N), F32), SDS((G, S, N), F32), SDS((G, S, N), BF16),
                   SDS((S, K), F32), SDS((S, K), F32), SDS((S, K), BF16)],
        compiler_params=_params(("parallel", "arbitrary")),
    )(xb, x, wg, wu, wd, gamma, beta)


def proj(xb, w, name):
    S, K = xb.shape
    G, _, N = w.shape
    ts = _tile(S, 1024)
    gb = _group_block(G, 6)

    def body(x_ref, w_ref, y_ref):
        xv = x_ref[...]
        for j in range(gb):
            y_ref[j] = _dot(xv, w_ref[j], NN)

    return pl.pallas_call(
        body, name=name, grid=(S // ts, G // gb),
        in_specs=[pl.BlockSpec((ts, K), lambda s, g: (s, 0)), pl.BlockSpec((gb, K, N), lambda s, g: (g, 0, 0))],
        out_specs=pl.BlockSpec((gb, ts, N), lambda s, g: (g, s, 0)),
        out_shape=SDS((G, S, N), F32),
        compiler_params=_params(("parallel", "parallel")),
    )(xb, w)


def contract_ln(a, w, xres, gamma, beta, scale, name):
    G, S, Kg = a.shape
    N = w.shape[2]
    ts = _tile(S, 512)

    def body(a_ref, w_ref, x_ref, g_ref, b_ref, z_ref, xn_ref, xb_ref):
        acc = _dot(a_ref[0], w_ref[0], NN)
        for j in range(1, G):
            acc = acc + _dot(a_ref[j], w_ref[j], NN)
        z = ALPHA * x_ref[...] + scale * acc
        xn = _layer_norm(z, g_ref[...], b_ref[...])
        z_ref[...] = z
        xn_ref[...] = xn
        xb_ref[...] = xn.astype(BF16)

    row = pl.BlockSpec((ts, N), lambda s: (s, 0))
    vec = pl.BlockSpec((1, N), lambda s: (0, 0))
    return pl.pallas_call(
        body, name=name, grid=(S // ts,),
        in_specs=[pl.BlockSpec((G, ts, Kg), lambda s: (0, s, 0)), pl.BlockSpec((G, Kg, N), lambda s: (0, 0, 0)), row, vec, vec],
        out_specs=[row, row, row],
        out_shape=[SDS((S, N), F32), SDS((S, N), F32), SDS((S, N), BF16)],
        compiler_params=_params(("parallel",)),
    )(a, w, xres, gamma, beta)


def ln_bwd(dx, z, gamma, out_scale, name):
    S, N = dx.shape
    ts = _tile(S, 512)

    def body(dx_ref, z_ref, g_ref, dz_ref, dzb_ref, dg_ref, db_ref):
        @pl.when(pl.program_id(0) == 0)
        def _():
            dg_ref[...] = jnp.zeros_like(dg_ref)
            db_ref[...] = jnp.zeros_like(db_ref)

        z = z_ref[...]
        mu = jnp.mean(z, axis=-1, keepdims=True)
        zc = z - mu
        var = jnp.mean(zc * zc, axis=-1, keepdims=True)
        rstd = lax.rsqrt(var + LN_EPS)
        xhat = zc * rstd
        dxv = dx_ref[...]
        dg_ref[...] += jnp.sum(dxv * xhat, axis=0, keepdims=True)
        db_ref[...] += jnp.sum(dxv, axis=0, keepdims=True)
        dxh = dxv * g_ref[...]
        m1 = jnp.mean(dxh, axis=-1, keepdims=True)
        m2 = jnp.mean(dxh * xhat, axis=-1, keepdims=True)
        dz = rstd * (dxh - m1 - xhat * m2)
        dz_ref[...] = dz
        dzb_ref[...] = (out_scale * dz).astype(BF16)

    row = pl.BlockSpec((ts, N), lambda s: (s, 0))
    vec = pl.BlockSpec((1, N), lambda s: (0, 0))
    return pl.pallas_call(
        body, name=name, grid=(S // ts,),
        in_specs=[row, row, vec],
        out_specs=[row, row, vec, vec],
        out_shape=[SDS((S, N), F32), SDS((S, N), BF16), SDS((1, N), F32), SDS((1, N), F32)],
        compiler_params=_params(("arbitrary",)),
    )(dx, z, gamma)


def ffn_bwd(dyb, dz, wd, wg, wu, g1, u1):
    S, K = dyb.shape
    G, N, _ = wd.shape
    ts = _tile(S, 512)

    def body(dy_ref, dz_ref, wd_ref, wg_ref, wu_ref, g_ref, u_ref, dg_ref, du_ref, dx_ref):
        j = pl.program_id(1)
        dh = _dot(dy_ref[...], wd_ref[0], NT)
        g = g_ref[0]
        sig = _sigmoid(g)
        dg = (dh * u_ref[0] * (sig * (1.0 + g * (1.0 - sig)))).astype(BF16)
        du = (dh * (g * sig)).astype(BF16)
        dg_ref[0] = dg
        du_ref[0] = du
        part = _dot(dg, wg_ref[0], NT) + _dot(du, wu_ref[0], NT)

        @pl.when(j == 0)
        def _():
            dx_ref[...] = ALPHA * dz_ref[...] + part

        @pl.when(j > 0)
        def _():
            dx_ref[...] += part

    row = pl.BlockSpec((ts, K), lambda s, j: (s, 0))
    gspec = pl.BlockSpec((1, ts, N), lambda s, j: (j, s, 0))
    wspec = pl.BlockSpec((1, K, N), lambda s, j: (j, 0, 0))
    return pl.pallas_call(
        body, name="ffn_bwd", grid=(S // ts, G),
        in_specs=[row, row, pl.BlockSpec((1, N, K), lambda s, j: (j, 0, 0)), wspec, wspec, gspec, gspec],
        out_specs=[gspec, gspec, row],
        out_shape=[SDS((G, S, N), BF16), SDS((G, S, N), BF16), SDS((S, K), F32)],
        compiler_params=_params(("parallel", "arbitrary")),
    )(dyb, dz, wd, wg, wu, g1, u1)


def proj_t(dyb, w, name):
    S, N = dyb.shape
    G, Kg, _ = w.shape
    ts = _tile(S, 1024)

    def body(dy_ref, w_ref, da_ref):
        dy = dy_ref[...]
        for j in range(G):
            da_ref[j] = _dot(dy, w_ref[j], NT)

    return pl.pallas_call(
        body, name=name, grid=(S // ts,),
        in_specs=[pl.BlockSpec((ts, N), lambda s: (s, 0)), pl.BlockSpec((G, Kg, N), lambda s: (0, 0, 0))],
        out_specs=pl.BlockSpec((G, ts, Kg), lambda s: (0, s, 0)),
        out_shape=SDS((G, S, Kg), F32),
        compiler_params=_params(("parallel",)),
    )(dyb, w)


def contract_t(da, w, res, name):
    G, S, Ng = da.shape
    K = w.shape[1]
    ts = _tile(S, 512)
    gb = _group_block(G, 6)

    def body(da_ref, w_ref, r_ref, o_ref):
        g = pl.program_id(1)
        part = _dot(da_ref[0], w_ref[0], NT)
        for j in range(1, gb):
            part = part + _dot(da_ref[j], w_ref[j], NT)

        @pl.when(g == 0)
        def _():
            o_ref[...] = ALPHA * r_ref[...] + part

        @pl.when(g > 0)
        def _():
            o_ref[...] += part

    row = pl.BlockSpec((ts, K), lambda s, g: (s, 0))
    return pl.pallas_call(
        body, name=name, grid=(S // ts, G // gb),
        in_specs=[pl.BlockSpec((gb, ts, Ng), lambda s, g: (g, s, 0)), pl.BlockSpec((gb, K, Ng), lambda s, g: (g, 0, 0)), row],
        out_specs=row,
        out_shape=SDS((S, K), F32),
        compiler_params=_params(("parallel", "arbitrary")),
    )(da, w, res)


WGRAD_ACC_ELEMS = 6 * 1024 * 256


def wgrad(a, b, out_dtype, name):
    ga, gb = a.ndim == 3, b.ndim == 3
    G = a.shape[0] if ga else b.shape[0]
    S, K = a.shape[-2:]
    N = b.shape[-1]
    ts = _tile(S, 1024)
    ns = S // ts
    ng = _group_block(G, WGRAD_ACC_ELEMS // (K * N))

    def body(a_ref, b_ref, o_ref, acc):
        s = pl.program_id(1)

        @pl.when(s == 0)
        def _():
            acc[...] = jnp.zeros_like(acc)

        for j in range(ng):
            acc[j] += _dot(a_ref[j] if ga else a_ref[...], b_ref[j] if gb else b_ref[...], TN)

        @pl.when(s == ns - 1)
        def _():
            o_ref[...] = acc[...].astype(out_dtype)

    aspec = pl.BlockSpec((ng, ts, K), lambda g, s: (g, s, 0)) if ga else pl.BlockSpec((ts, K), lambda g, s: (s, 0))
    bspec = pl.BlockSpec((ng, ts, N), lambda g, s: (g, s, 0)) if gb else pl.BlockSpec((ts, N), lambda g, s: (s, 0))
    return pl.pallas_call(
        body, name=name, grid=(G // ng, ns),
        in_specs=[aspec, bspec],
        out_specs=pl.BlockSpec((ng, K, N), lambda g, s: (g, 0, 0)),
        out_shape=SDS((G, K, N), out_dtype),
        scratch_shapes=[pltpu.VMEM((ng, K, N), F32)],
        compiler_params=_params(("parallel", "arbitrary")),
    )(a, b)


def loss_grad(xn, tgt):
    S, N = xn.shape
    ts = _tile(S, 512)

    def body(x_ref, t_ref, l_ref, dx_ref):
        @pl.when(pl.program_id(0) == 0)
        def _():
            l_ref[...] = jnp.zeros_like(l_ref)

        e = x_ref[...] - t_ref[...]
        dx_ref[...] = e * (1.0 / N)
        l_ref[...] += 0.5 * jnp.sum(jnp.mean(e * e, axis=-1, keepdims=True), axis=0, keepdims=True)

    row = pl.BlockSpec((ts, N), lambda s: (s, 0))
    return pl.pallas_call(
        body, name="loss_grad", grid=(S // ts,),
        in_specs=[row, row],
        out_specs=[pl.BlockSpec((1, 1), lambda s: (0, 0)), row],
        out_shape=[SDS((1, 1), F32), SDS((S, N), F32)],
        compiler_params=_params(("arbitrary",)),
    )(xn, tgt)


def _shift_down(x, k):
    if k == 0:
        return x
    rows = lax.broadcasted_iota(jnp.int32, x.shape, 0)
    return jnp.where(rows >= k, pltpu.roll(x, k, 0), 0.0)


def _shift_up(x, k):
    if k == 0:
        return x
    n = x.shape[0]
    rows = lax.broadcasted_iota(jnp.int32, x.shape, 0)
    return jnp.where(rows < n - k, pltpu.roll(x, n - k, 0), 0.0)


LANES = 128


def conv_mixer_fwd(u, cw):
    _, S, _ = u.shape
    nh = GROUP // LANES

    def body(b_ref, c_ref, x_ref, w_ref, o_ref):
        p = c_ref[0] * x_ref[0]
        w = w_ref[0]
        conv = w[2:3] * p + w[1:2] * _shift_down(p, 1) + w[0:1] * _shift_down(p, 2)
        o_ref[0] = (b_ref[0] * conv).astype(BF16)

    def uspec(off):
        return pl.BlockSpec((1, S, LANES), lambda g, h: (g + off, 0, h))

    return pl.pallas_call(
        body, name="conv_mixer_fwd", grid=(3, nh),
        in_specs=[uspec(0), uspec(3), uspec(6), pl.BlockSpec((1, 8, LANES), lambda g, h: (g, 0, h))],
        out_specs=pl.BlockSpec((1, S, LANES), lambda g, h: (g, 0, h)),
        out_shape=SDS((3, S, GROUP), BF16),
        compiler_params=_params(("parallel", "parallel")),
    )(u, u, u, cw)


def conv_mixer_bwd(u, cw, dm):
    _, S, _ = u.shape
    nh = GROUP // LANES

    def body(b_ref, c_ref, x_ref, w_ref, d_ref, db_ref, dc_ref, dx_ref, dw_ref):
        cg, xi = c_ref[0], x_ref[0]
        p = cg * xi
        p1, p2 = _shift_down(p, 1), _shift_down(p, 2)
        w = w_ref[0]
        conv = w[2:3] * p + w[1:2] * p1 + w[0:1] * p2
        dt = d_ref[0]
        db_ref[0] = (dt * conv).astype(BF16)
        dcv = dt * b_ref[0]
        dp = w[2:3] * dcv + w[1:2] * _shift_up(dcv, 1) + w[0:1] * _shift_up(dcv, 2)
        dc_ref[0] = (dp * xi).astype(BF16)
        dx_ref[0] = (dp * cg).astype(BF16)
        dw = jnp.concatenate([jnp.sum(dcv * p2, axis=0, keepdims=True), jnp.sum(dcv * p1, axis=0, keepdims=True),
                              jnp.sum(dcv * p, axis=0, keepdims=True), jnp.zeros((5, LANES), F32)], axis=0)
        dw_ref[0] = dw

    def uspec(off):
        return pl.BlockSpec((1, S, LANES), lambda g, h: (g + off, 0, h))

    ospec = pl.BlockSpec((1, S, LANES), lambda g, h: (g, 0, h))
    wspec = pl.BlockSpec((1, 8, LANES), lambda g, h: (g, 0, h))
    return pl.pallas_call(
        body, name="conv_mixer_bwd", grid=(3, nh),
        in_specs=[uspec(0), uspec(3), uspec(6), wspec, ospec],
        out_specs=[ospec, ospec, ospec, wspec],
        out_shape=[SDS((3, S, GROUP), BF16)] * 3 + [SDS((3, 8, GROUP), F32)],
        compiler_params=_params(("parallel", "parallel")),
    )(u, u, u, cw, dm)


def qk_conv_fwd(u, qw):
    _, S, _ = u.shape
    nh = GROUP // LANES

    def body(u_ref, w_ref, o_ref):
        x = u_ref[0]
        w = w_ref[0]
        pre = w[3:4] * x + w[2:3] * _shift_down(x, 1) + w[1:2] * _shift_down(x, 2) + w[0:1] * _shift_down(x, 3)
        o_ref[0] = pre * _sigmoid(pre)

    spec = pl.BlockSpec((1, S, LANES), lambda g, h: (g, 0, h))
    return pl.pallas_call(
        body, name="qk_conv_fwd", grid=(8, nh),
        in_specs=[spec, pl.BlockSpec((1, 8, LANES), lambda g, h: (g, 0, h))],
        out_specs=spec,
        out_shape=SDS((8, S, GROUP), F32),
        compiler_params=_params(("parallel", "parallel")),
    )(u, qw)


def qk_conv_bwd(u, qw, dqk):
    _, S, _ = u.shape
    nh = GROUP // LANES

    def body(u_ref, w_ref, d_ref, du_ref, dw_ref):
        x = u_ref[0]
        w = w_ref[0]
        x1, x2, x3 = _shift_down(x, 1), _shift_down(x, 2), _shift_down(x, 3)
        pre = w[3:4] * x + w[2:3] * x1 + w[1:2] * x2 + w[0:1] * x3
        sig = _sigmoid(pre)
        dpre = d_ref[0] * (sig * (1.0 + pre * (1.0 - sig)))
        du = w[3:4] * dpre + w[2:3] * _shift_up(dpre, 1) + w[1:2] * _shift_up(dpre, 2) + w[0:1] * _shift_up(dpre, 3)
        du_ref[0] = du.astype(BF16)
        dw = jnp.concatenate([jnp.sum(dpre * x3, axis=0, keepdims=True), jnp.sum(dpre * x2, axis=0, keepdims=True),
                              jnp.sum(dpre * x1, axis=0, keepdims=True), jnp.sum(dpre * x, axis=0, keepdims=True),
                              jnp.zeros((4, LANES), F32)], axis=0)
        dw_ref[0] = dw

    spec = pl.BlockSpec((1, S, LANES), lambda g, h: (g, 0, h))
    wspec = pl.BlockSpec((1, 8, LANES), lambda g, h: (g, 0, h))
    return pl.pallas_call(
        body, name="qk_conv_bwd", grid=(8, nh),
        in_specs=[spec, wspec, spec],
        out_specs=[spec, wspec],
        out_shape=[SDS((8, S, GROUP), BF16), SDS((8, 8, GROUP), F32)],
        compiler_params=_params(("parallel", "parallel")),
    )(u, qw, dqk)


def _head_masks():
    lane = lax.broadcasted_iota(jnp.int32, (1, D_XA), 1)
    return [(lane >= h * XA_HEAD_DIM) & (lane < (h + 1) * XA_HEAD_DIM) for h in range(XA_HEADS)]


def xattn_fwd(u, qg, kv):
    _, S, _ = u.shape
    ts = _tile(S, 512)
    scale = XA_HEAD_DIM ** -0.5

    def body(q_ref, kv_ref, o_ref):
        q = q_ref[0]
        k = kv_ref[0].astype(BF16)
        v = kv_ref[1]
        o = jnp.zeros((ts, D_XA), F32)
        for m in _head_masks():
            s = _dot(jnp.where(m, q, 0.0).astype(BF16), k, NT) * scale
            s = s - jnp.max(s, axis=-1, keepdims=True)
            e = jnp.exp(s)
            p = e / jnp.sum(e, axis=-1, keepdims=True)
            o = o + _dot(p.astype(BF16), jnp.where(m, v, 0.0).astype(BF16), NN)
        o_ref[0] = o.astype(BF16)

    return pl.pallas_call(
        body, name="xattn_fwd", grid=(S // ts,),
        in_specs=[pl.BlockSpec((1, ts, GROUP), lambda s: (qg, s, 0)), pl.BlockSpec((2, N_MEM, GROUP), lambda s: (0, 0, 0))],
        out_specs=pl.BlockSpec((1, ts, GROUP), lambda s: (0, s, 0)),
        out_shape=SDS((1, S, GROUP), BF16),
        compiler_params=_params(("parallel",)),
    )(u, kv)


def xattn_bwd(u, qg, kv, dm, dg):
    _, S, _ = u.shape
    ts = _tile(S, 512)
    scale = XA_HEAD_DIM ** -0.5

    def body(q_ref, kv_ref, do_ref, dq_ref, dkv_ref):
        @pl.when(pl.program_id(0) == 0)
        def _():
            dkv_ref[...] = jnp.zeros_like(dkv_ref)

        q = q_ref[0]
        k = kv_ref[0]
        v = kv_ref[1]
        kb = k.astype(BF16)
        do = do_ref[0]
        dq = jnp.zeros((ts, D_XA), F32)
        dk = jnp.zeros((N_MEM, D_XA), F32)
        dv = jnp.zeros((N_MEM, D_XA), F32)
        for m in _head_masks():
            qm = jnp.where(m, q, 0.0).astype(BF16)
            s = _dot(qm, kb, NT) * scale
            s = s - jnp.max(s, axis=-1, keepdims=True)
            e = jnp.exp(s)
            p = e / jnp.sum(e, axis=-1, keepdims=True)
            dom = jnp.where(m, do, 0.0).astype(BF16)
            dp = _dot(dom, jnp.where(m, v, 0.0).astype(BF16), NT)
            ds = (p * (dp - jnp.sum(dp * p, axis=-1, keepdims=True)) * scale).astype(BF16)
            dq = dq + _dot(ds, jnp.where(m, k, 0.0).astype(BF16), NN)
            dk = dk + _dot(ds, qm, TN)
            dv = dv + _dot(p.astype(BF16), dom, TN)
        dq_ref[0] = dq.astype(BF16)
        dkv_ref[0] += dk
        dkv_ref[1] += dv

    return pl.pallas_call(
        body, name="xattn_bwd", grid=(S // ts,),
        in_specs=[pl.BlockSpec((1, ts, GROUP), lambda s: (qg, s, 0)), pl.BlockSpec((2, N_MEM, GROUP), lambda s: (0, 0, 0)),
                  pl.BlockSpec((1, ts, GROUP), lambda s: (dg, s, 0))],
        out_specs=[pl.BlockSpec((1, ts, GROUP), lambda s: (0, s, 0)), pl.BlockSpec((2, N_MEM, GROUP), lambda s: (0, 0, 0))],
        out_shape=[SDS((1, S, GROUP), BF16), SDS((2, N_MEM, GROUP), F32)],
        compiler_params=_params(("arbitrary",)),
    )(u, kv, dm)


ML_BLOCK_CHUNKS = 4
H4 = ML_HEADS
L = ML_CHUNK
NLANE = ML_HEAD_DIM


def _chunk_consts():
    r = lax.broadcasted_iota(jnp.int32, (1, L, L), 1)
    c = lax.broadcasted_iota(jnp.int32, (1, L, L), 2)
    return r >= c, r <= c, r == c


def _gate_cols(gb):
    lane = lax.broadcasted_iota(jnp.int32, gb.shape, 1)
    li = jnp.stack([jnp.sum(jnp.where(lane == h, gb, 0.0), axis=1, keepdims=True) for h in range(H4)])
    gf = jnp.stack([jnp.sum(jnp.where(lane == H4 + h, gb, 0.0), axis=1, keepdims=True) for h in range(H4)])
    return li, gf


def _log_sigmoid(x):
    return jnp.minimum(x, 0.0) - jnp.log(1.0 + jnp.exp(-jnp.abs(x)))


def _chunk_forward(q, k, v_aug, li_col, lf_col, c_prev, m_prev):
    tri, tri_t, eye = _chunk_consts()
    lf_row = jnp.sum(jnp.where(eye, lf_col, 0.0), axis=1, keepdims=True)
    li_row = jnp.sum(jnp.where(eye, li_col, 0.0), axis=1, keepdims=True)
    bcum_col = jnp.sum(jnp.where(tri, lf_row, 0.0), axis=2, keepdims=True)
    bcum_row = jnp.sum(jnp.where(tri_t, lf_col, 0.0), axis=1, keepdims=True)
    log_d = jnp.where(tri, bcum_col - bcum_row + li_row, NEG)
    log_inter = bcum_col + m_prev
    m_t = jnp.maximum(log_inter, jnp.max(log_d, axis=2, keepdims=True))
    w_intra = jnp.exp(log_d - m_t)
    w_inter = jnp.exp(log_inter - m_t)
    sc = _bdot(q, k, 2, 2) * w_intra
    qc = _bdot(q, c_prev, 2, 1)
    num = _bdot(sc, v_aug, 2, 1) + w_inter * qc
    lane = lax.broadcasted_iota(jnp.int32, num.shape, 2)
    den = jnp.sum(jnp.where(lane == NLANE, num, 0.0), axis=2, keepdims=True)
    e_m = jnp.exp(-m_t)
    b_last = jnp.sum(lf_row, axis=2, keepdims=True)
    log_w = b_last - bcum_col + li_col
    m_new = jnp.maximum(b_last + m_prev, jnp.max(log_w, axis=1, keepdims=True))
    w_k = jnp.exp(log_w - m_new)
    decay = jnp.exp(b_last + m_prev - m_new)
    return dict(w_intra=w_intra, w_inter=w_inter, sc=sc, qc=qc, num=num, den=den, e_m=e_m, lane=lane,
                w_k=w_k, decay=decay, m_new=m_new)


def mlstm_fwd(qk, u, bg):
    _, S, _ = qk.shape
    nc = S // L
    cb = min(ML_BLOCK_CHUNKS, nc)
    rows = cb * L
    kscale = ML_HEAD_DIM ** -0.5

    def body(qk_ref, v_ref, g_ref, bg_ref, h_ref, cst_ref, mst_ref, c_sc, m_sc):
        @pl.when(pl.program_id(0) == 0)
        def _():
            c_sc[...] = jnp.zeros_like(c_sc)
            m_sc[...] = jnp.zeros_like(m_sc)

        for c in range(cb):
            sl = pl.ds(c * L, L)
            q = qk_ref[0:H4, sl, :]
            k = qk_ref[H4:2 * H4, sl, :] * kscale
            v = v_ref[:, sl, :]
            lane = lax.broadcasted_iota(jnp.int32, v.shape, 2)
            v_aug = jnp.where(lane == NLANE, 1.0, v)
            li_col, gf = _gate_cols(g_ref[0, sl, :] + bg_ref[...])
            lf_col = _log_sigmoid(gf)
            c_prev = c_sc[...]
            m_prev = m_sc[...]
            f = _chunk_forward(q, k, v_aug, li_col, lf_col, c_prev, m_prev)
            r = 1.0 / jnp.maximum(jnp.abs(f["den"]), f["e_m"])
            h_ref[:, sl, :] = jnp.where(lane < NLANE, f["num"] * r, 0.0)
            cst_ref[c] = c_prev
            mst_ref[c] = jnp.broadcast_to(m_prev, (H4, 1, LANES))
            c_sc[...] = f["decay"] * c_prev + _bdot(k * f["w_k"], v_aug, 1, 1)
            m_sc[...] = f["m_new"]

    def hspec(blk):
        return pl.BlockSpec((H4, rows, GROUP), lambda i: (blk, i, 0))

    return pl.pallas_call(
        body, name="mlstm_fwd", grid=(nc // cb,),
        in_specs=[pl.BlockSpec((2 * H4, rows, GROUP), lambda i: (0, i, 0)), hspec(2),
                  pl.BlockSpec((1, rows, GROUP), lambda i: (17, i, 0)), pl.BlockSpec((1, GROUP), lambda i: (0, 0))],
        out_specs=[hspec(0), pl.BlockSpec((cb, H4, GROUP, GROUP), lambda i: (i, 0, 0, 0)),
                   pl.BlockSpec((cb, H4, 1, LANES), lambda i: (i, 0, 0, 0))],
        out_shape=[SDS((H4, S, GROUP), F32), SDS((nc, H4, GROUP, GROUP), F32), SDS((nc, H4, 1, LANES), F32)],
        scratch_shapes=[pltpu.VMEM((H4, GROUP, GROUP), F32), pltpu.VMEM((H4, 1, 1), F32)],
        compiler_params=_params(("arbitrary",)),
    )(qk, u, u, bg)


def mlstm_bwd(qk, u, bg, cst, mst, dh):
    _, S, _ = qk.shape
    nc = S // L
    cb = min(ML_BLOCK_CHUNKS, nc)
    rows = cb * L
    nb = nc // cb
    kscale = ML_HEAD_DIM ** -0.5

    def body(qk_ref, v_ref, g_ref, bg_ref, cst_ref, mst_ref, dh_ref, dqk_ref, dv_ref, dg_ref, dbg_ref, dc_sc):
        @pl.when(pl.program_id(0) == 0)
        def _():
            dc_sc[...] = jnp.zeros_like(dc_sc)
            dbg_ref[...] = jnp.zeros_like(dbg_ref)

        tri, tri_t, eye = _chunk_consts()
        for c in reversed(range(cb)):
            sl = pl.ds(c * L, L)
            q = qk_ref[0:H4, sl, :]
            k = qk_ref[H4:2 * H4, sl, :] * kscale
            v = v_ref[:, sl, :]
            lane = lax.broadcasted_iota(jnp.int32, v.shape, 2)
            v_aug = jnp.where(lane == NLANE, 1.0, v)
            li_col, gf = _gate_cols(g_ref[0, sl, :] + bg_ref[...])
            lf_col = _log_sigmoid(gf)
            c_prev = cst_ref[c]
            m_prev = mst_ref[c][:, :, 0:1]
            f = _chunk_forward(q, k, v_aug, li_col, lf_col, c_prev, m_prev)
            w_intra, w_inter, sc, num, den, e_m = f["w_intra"], f["w_inter"], f["sc"], f["num"], f["den"], f["e_m"]
            absd = jnp.abs(den)
            r = 1.0 / jnp.maximum(absd, e_m)
            dhv = dh_ref[:, sl, :]
            s1 = jnp.sum(jnp.where(lane < NLANE, dhv * num, 0.0), axis=2, keepdims=True)
            dden = jnp.where(absd > e_m, -s1 * r * r * jnp.sign(den), 0.0)
            dnum = jnp.where(lane == NLANE, dden, jnp.where(lane < NLANE, dhv * r, 0.0))
            dsc = _bdot(dnum, v_aug, 2, 2)
            dv = _bdot(sc, dnum, 1, 1)
            gmat = dsc * sc
            dqk = dsc * w_intra
            dq = _bdot(dqk, k, 2, 1) + w_inter * _bdot(dnum, c_prev, 2, 2)
            dk = _bdot(dqk, q, 1, 1)
            dc_prev = _bdot(q * w_inter, dnum, 1, 1)
            dlog_inter = jnp.sum(dnum * f["qc"], axis=2, keepdims=True) * w_inter
            dbcum_col = dlog_inter + jnp.sum(gmat, axis=2, keepdims=True)
            g_row = jnp.sum(gmat, axis=1, keepdims=True)
            dcn = dc_sc[...]
            w_k, decay = f["w_k"], f["decay"]
            kw = k * w_k
            dc_prev = dc_prev + decay * dcn
            db_last = jnp.sum(jnp.sum(dcn * c_prev, axis=2, keepdims=True), axis=1, keepdims=True) * decay
            dkw = _bdot(v_aug, dcn, 2, 2)
            dv = dv + _bdot(kw, dcn, 2, 1)
            dk = dk + dkw * w_k
            dlogw = jnp.sum(dkw * k, axis=2, keepdims=True) * w_k
            db_last = db_last + jnp.sum(dlogw, axis=1, keepdims=True)
            dbcum_col = dbcum_col - dlogw
            rowi = lax.broadcasted_iota(jnp.int32, (1, L, 1), 1)
            dbcum_col = dbcum_col + jnp.where(rowi == L - 1, db_last, 0.0)
            dbcum_row = jnp.sum(jnp.where(eye, dbcum_col, 0.0), axis=1, keepdims=True) - g_row
            dlf_col = jnp.sum(jnp.where(tri_t, dbcum_row, 0.0), axis=2, keepdims=True)
            dli_col = dlogw + jnp.sum(jnp.where(eye, g_row, 0.0), axis=2, keepdims=True)
            dgf_col = dlf_col * _sigmoid(-gf)
            lane_g = lax.broadcasted_iota(jnp.int32, (L, GROUP), 1)
            dg = jnp.zeros((L, GROUP), F32)
            for h in range(H4):
                dg = dg + jnp.where(lane_g == h, dli_col[h], 0.0) + jnp.where(lane_g == H4 + h, dgf_col[h], 0.0)
            dqk_ref[0:H4, sl, :] = dq
            dqk_ref[H4:2 * H4, sl, :] = dk * kscale
            dv_ref[:, sl, :] = jnp.where(lane < NLANE, dv, 0.0).astype(BF16)
            dg_ref[0, sl, :] = dg.astype(BF16)
            dbg_ref[...] += jnp.sum(dg, axis=0, keepdims=True)
            dc_sc[...] = dc_prev

    def hspec(blk):
        return pl.BlockSpec((H4, rows, GROUP), lambda i: (blk, nb - 1 - i, 0))

    gspec = pl.BlockSpec((1, rows, GROUP), lambda i: (17, nb - 1 - i, 0))
    qkspec = pl.BlockSpec((2 * H4, rows, GROUP), lambda i: (0, nb - 1 - i, 0))
    return pl.pallas_call(
        body, name="mlstm_bwd", grid=(nb,),
        in_specs=[qkspec, hspec(2), gspec, pl.BlockSpec((1, GROUP), lambda i: (0, 0)),
                  pl.BlockSpec((cb, H4, GROUP, GROUP), lambda i: (nb - 1 - i, 0, 0, 0)),
                  pl.BlockSpec((cb, H4, 1, LANES), lambda i: (nb - 1 - i, 0, 0, 0)), hspec(0)],
        out_specs=[qkspec, hspec(0), pl.BlockSpec((1, rows, GROUP), lambda i: (0, nb - 1 - i, 0)),
                   pl.BlockSpec((1, GROUP), lambda i: (0, 0))],
        out_shape=[SDS((2 * H4, S, GROUP), F32), SDS((H4, S, GROUP), BF16),
                   SDS((1, S, GROUP), BF16), SDS((1, GROUP), F32)],
        scratch_shapes=[pltpu.VMEM((H4, GROUP, GROUP), F32)],
        compiler_params=_params(("arbitrary",)),
    )(qk, u, u, bg, cst, mst, dh)


def head_norm_fwd(hm, u, hg):
    _, S, _ = hm.shape
    ts = _tile(S, 512)

    def body(h_ref, o_ref, g_ref, t_ref):
        h = h_ref[0]
        lane = lax.broadcasted_iota(jnp.int32, h.shape, 1)
        valid = lane < ML_HEAD_DIM
        mu = jnp.sum(h, axis=-1, keepdims=True) * (1.0 / ML_HEAD_DIM)
        hc = jnp.where(valid, h - mu, 0.0)
        var = jnp.sum(hc * hc, axis=-1, keepdims=True) * (1.0 / ML_HEAD_DIM)
        hn = hc * lax.rsqrt(var + LN_EPS) * g_ref[0]
        t_ref[0] = (_sigmoid(o_ref[0]) * hn).astype(BF16)

    return pl.pallas_call(
        body, name="head_norm_fwd", grid=(H4, S // ts),
        in_specs=[pl.BlockSpec((1, ts, GROUP), lambda h, s: (h, s, 0)), pl.BlockSpec((1, ts, GROUP), lambda h, s: (12 + h, s, 0)),
                  pl.BlockSpec((1, 1, GROUP), lambda h, s: (h, 0, 0))],
        out_specs=pl.BlockSpec((1, ts, GROUP), lambda h, s: (h, s, 0)),
        out_shape=SDS((H4, S, GROUP), BF16),
        compiler_params=_params(("parallel", "parallel")),
    )(hm, u, hg)


def head_norm_bwd(hm, u, hg, dm):
    _, S, _ = hm.shape
    ts = _tile(S, 512)

    def body(h_ref, o_ref, g_ref, d_ref, dh_ref, do_ref, dg_ref):
        @pl.when(pl.program_id(1) == 0)
        def _():
            dg_ref[...] = jnp.zeros_like(dg_ref)

        h = h_ref[0]
        lane = lax.broadcasted_iota(jnp.int32, h.shape, 1)
        valid = lane < ML_HEAD_DIM
        inv = 1.0 / ML_HEAD_DIM
        mu = jnp.sum(h, axis=-1, keepdims=True) * inv
        hc = jnp.where(valid, h - mu, 0.0)
        var = jnp.sum(hc * hc, axis=-1, keepdims=True) * inv
        rstd = lax.rsqrt(var + LN_EPS)
        xhat = hc * rstd
        g = g_ref[0]
        sig = _sigmoid(o_ref[0])
        dt = jnp.where(valid, d_ref[0], 0.0)
        do_ref[0] = (dt * xhat * g * sig * (1.0 - sig)).astype(BF16)
        dhn = dt * sig
        dg_ref[0] += jnp.sum(dhn * xhat, axis=0, keepdims=True)
        dxh = dhn * g
        m1 = jnp.sum(dxh, axis=-1, keepdims=True) * inv
        m2 = jnp.sum(dxh * xhat, axis=-1, keepdims=True) * inv
        dh_ref[0] = jnp.where(valid, rstd * (dxh - m1 - xhat * m2), 0.0)

    spec = pl.BlockSpec((1, ts, GROUP), lambda h, s: (h, s, 0))
    gspec = pl.BlockSpec((1, 1, GROUP), lambda h, s: (h, 0, 0))
    return pl.pallas_call(
        body, name="head_norm_bwd", grid=(H4, S // ts),
        in_specs=[spec, pl.BlockSpec((1, ts, GROUP), lambda h, s: (12 + h, s, 0)), gspec, spec],
        out_specs=[spec, spec, gspec],
        out_shape=[SDS((H4, S, GROUP), F32), SDS((H4, S, GROUP), BF16), SDS((H4, 1, GROUP), F32)],
        compiler_params=_params(("parallel", "arbitrary")),
    )(hm, u, hg, dm)


def adamw(w, g, m, v, name):
    R, C = w.shape
    tr = R if R <= 512 else next(d for d in (512, 256, 128, 64, 32, 16, 8) if R % d == 0)
    c1 = 1.0 / (1.0 - ADAM_B1 ** ADAM_STEP)
    c2 = 1.0 / (1.0 - ADAM_B2 ** ADAM_STEP)

    def body(w_ref, g_ref, m_ref, v_ref, d_ref, nm_ref, nv_ref):
        gv = g_ref[...]
        nm = ADAM_B1 * m_ref[...] + (1.0 - ADAM_B1) * gv
        nv = ADAM_B2 * v_ref[...] + (1.0 - ADAM_B2) * (gv * gv)
        d_ref[...] = -ADAM_LR * ((nm * c1) / (jnp.sqrt(nv * c2) + ADAM_EPS) + ADAM_WD * w_ref[...])
        nm_ref[...] = nm
        nv_ref[...] = nv

    spec = pl.BlockSpec((tr, C), lambda i: (i, 0))
    return pl.pallas_call(
        body, name=name, grid=(R // tr,),
        in_specs=[spec] * 4, out_specs=[spec] * 3,
        out_shape=[SDS((R, C), F32)] * 3,
        compiler_params=_params(("parallel",)),
    )(w, g, m, v)


HBM = pl.BlockSpec(memory_space=pl.ANY)
ROW_SPLIT = 4


def _position():
    x, y, c = lax.axis_index("x"), lax.axis_index("y"), lax.axis_index("c")
    return x, y, c, [(1 - x, y), (x, 1 - y), (1 - x, 1 - y)]


def _unique(items):
    arrays = []
    for a, _ in items:
        if not any(a is b for b in arrays):
            arrays.append(a)
    return arrays, [next(i for i, b in enumerate(arrays) if b is a) for a, _ in items]


def place_own(items, me):
    arrays, src_of = _unique(items)
    n = len(items)
    shapes = [a.shape[len(p):] for a, p in items]

    def body(me_ref, *refs):
        for t in range(n):
            refs[n + t][...] = jnp.zeros_like(refs[n + t])
            refs[n + t][me_ref[0]] = refs[t][(0,) * len(items[t][1])]

    in_specs, out_specs = [], []
    for (a, p), shp in zip(items, shapes):
        blk = shp[:-2] + (shp[-2] // ROW_SPLIT, shp[-1])
        lead = (0,) * (len(shp) - 2)
        in_specs.append(pl.BlockSpec((1,) * len(p) + blk, functools.partial(lambda r, me_ref, p, lead: p + lead + (r, 0), p=p, lead=lead)))
        out_specs.append(pl.BlockSpec((N_CHIPS,) + blk, functools.partial(lambda r, me_ref, lead: (0,) + lead + (r, 0), lead=lead)))
    return pl.pallas_call(
        body, name="place_own",
        grid_spec=pltpu.PrefetchScalarGridSpec(num_scalar_prefetch=1, grid=(ROW_SPLIT,), in_specs=in_specs, out_specs=out_specs),
        out_shape=[SDS((N_CHIPS,) + tuple(shp), a.dtype) for shp, (a, _) in zip(shapes, items)],
        compiler_params=_params(("parallel",)),
    )(me, *[arrays[i] for i in src_of])


SEM = pl.BlockSpec(memory_space=pltpu.SEMAPHORE)
IN_HBM = pl.BlockSpec(memory_space=pltpu.HBM)
DATAFLOW = pltpu.SideEffectType.DATAFLOW_SIDE_EFFECTING


def split_start(bufs, plan, n_copies, after, name):
    n = len(bufs)

    def body(*refs):
        send, recv, token = refs[n + 1], refs[n + 2], refs[-1]
        x, y, c, chips = _position()
        for k, (src, dst, dev) in enumerate(plan(refs[:n], x, y, c, chips)):
            pltpu.make_async_remote_copy(src_ref=src, dst_ref=dst, send_sem=send.at[k], recv_sem=recv.at[k],
                                         device_id=dev, device_id_type=MESH).start()
        token[...] = jnp.zeros_like(token)

    out = pl.pallas_call(
        body, name=name,
        out_shape=(pltpu.SemaphoreType.DMA((n_copies,)), pltpu.SemaphoreType.DMA((n_copies,)),
                   *[pltpu.HBM(b.shape, b.dtype) for b in bufs], SDS((8, LANES), F32)),
        in_specs=[IN_HBM] * n + [pl.BlockSpec(memory_space=pl.ANY)],
        out_specs=(SEM, SEM, *[IN_HBM] * n, pl.BlockSpec(memory_space=pltpu.VMEM)),
        input_output_aliases={i: 2 + i for i in range(n)},
        compiler_params=pltpu.CompilerParams(has_side_effects=DATAFLOW),
    )(*[pltpu.with_memory_space_constraint(b, pltpu.HBM) for b in bufs], after)
    return out[0], out[1], list(out[2:2 + n]), out[-1]


def split_wait(send, recv, bufs, plan, after, name):
    n = len(bufs)

    def body(*refs):
        send_ref, recv_ref = refs[n], refs[n + 1]
        x, y, c, chips = _position()
        for k, (src, dst, dev) in enumerate(plan(refs[:n], x, y, c, chips)):
            cp = pltpu.make_async_remote_copy(src_ref=src, dst_ref=dst, send_sem=send_ref.at[k], recv_sem=recv_ref.at[k],
                                              device_id=dev, device_id_type=MESH)
            cp.wait_send()
            cp.wait_recv()

    return list(pl.pallas_call(
        body, name=name, out_shape=tuple(pltpu.HBM(b.shape, b.dtype) for b in bufs),
        in_specs=[IN_HBM] * n + [SEM, SEM, pl.BlockSpec(memory_space=pl.ANY)], out_specs=tuple([IN_HBM] * n),
        input_output_aliases={i: i for i in range(n)},
        compiler_params=pltpu.CompilerParams(has_side_effects=DATAFLOW),
    )(*bufs, send, recv, after))


def _gather_plan(shapes, landing):
    n = len(shapes)

    def plan(refs, x, y, c, chips):
        out = []
        for t in range(n):
            half = shapes[t][0] // 2
            rows = pl.ds(c * half, half)
            for cx, cy in chips:
                slot = 2 * cx + cy if landing else 2 * x + y
                out.append((refs[t].at[rows], refs[n + t].at[slot, rows], (cx, cy, c)))
        return out

    return plan


def gather_start(shards, placed, after, name):
    shapes = [s.shape for s in shards]
    send, recv, bufs, token = split_start(shards + placed, _gather_plan(shapes, False), 3 * len(shards), after, name)
    return (send, recv, bufs, shapes), token


def gather_wait(state, after, name):
    send, recv, bufs, shapes = state
    return split_wait(send, recv, bufs, _gather_plan(shapes, True), after, name)[len(shapes):]


def gather_pass_on(placed, shapes, name):
    n = len(placed)

    def body(*refs):
        outs, send, recv = refs[n:2 * n], refs[2 * n], refs[2 * n + 1]
        x, y, c, chips = _position()
        cps = []
        for t in range(n):
            half = shapes[t][0] // 2
            for j, (cx, cy) in enumerate(chips):
                piece = outs[t].at[2 * cx + cy, pl.ds(c * half, half)]
                cp = pltpu.make_async_remote_copy(src_ref=piece, dst_ref=piece, send_sem=send.at[3 * t + j], recv_sem=recv.at[3 * t + j],
                                                  device_id=(x, y, 1 - c), device_id_type=MESH)
                cp.start()
                cps.append(cp)
        for t in range(n):
            half = shapes[t][0] // 2
            for j, (cx, cy) in enumerate(chips):
                piece = outs[t].at[2 * cx + cy, pl.ds((1 - c) * half, half)]
                pltpu.make_async_remote_copy(src_ref=piece, dst_ref=piece, send_sem=send.at[3 * t + j], recv_sem=recv.at[3 * t + j],
                                             device_id=(x, y, 1 - c), device_id_type=MESH).wait_recv()
        for cp in cps:
            cp.wait_send()

    return pl.pallas_call(
        body, name=name,
        in_specs=[HBM] * n, out_specs=[HBM] * n,
        out_shape=[SDS(p.shape, p.dtype) for p in placed],
        input_output_aliases={t: t for t in range(n)},
        scratch_shapes=[pltpu.SemaphoreType.DMA((3 * n,))] * 2,
    )(*placed)


def _flip(k, x, y, c):
    return ((1 - x) if k & 4 else x, (1 - y) if k & 2 else y, (1 - c) if k & 1 else c)


def small_allgather(v, reduce):
    R, C = v.shape

    def body(v_ref, o_ref, *scratch):
        if reduce:
            buf, send, recv = scratch
        else:
            buf, (send, recv) = o_ref, scratch
        x, y, c, _ = _position()
        me = 4 * x + 2 * y + c
        buf[me] = v_ref[...]
        sends = []
        for k in range(1, N_DEV):
            cp = pltpu.make_async_remote_copy(src_ref=v_ref, dst_ref=buf.at[me], send_sem=send.at[k - 1], recv_sem=recv.at[k - 1],
                                              device_id=_flip(k, x, y, c), device_id_type=MESH)
            cp.start()
            sends.append(cp)
        for k in range(1, N_DEV):
            px, py, pc = _flip(k, x, y, c)
            pltpu.make_async_remote_copy(src_ref=v_ref, dst_ref=buf.at[4 * px + 2 * py + pc], send_sem=send.at[k - 1],
                                         recv_sem=recv.at[k - 1], device_id=(px, py, pc), device_id_type=MESH).wait_recv()
        for cp in sends:
            cp.wait_send()
        if reduce:
            acc = buf[0]
            for i in range(1, N_DEV):
                acc = acc + buf[i]
            o_ref[...] = acc

    vm = pl.BlockSpec(memory_space=pltpu.VMEM)
    sems = [pltpu.SemaphoreType.DMA((N_DEV - 1,)), pltpu.SemaphoreType.DMA((N_DEV - 1,))]
    return pl.pallas_call(
        body, name="small_allreduce" if reduce else "small_allgather",
        in_specs=[vm], out_specs=vm,
        out_shape=SDS((R, C) if reduce else (N_DEV, R, C), F32),
        scratch_shapes=([pltpu.VMEM((N_DEV, R, C), F32)] if reduce else []) + sems,
    )(v)


def rs_exchange_sibling(gs):
    n = len(gs)

    def body(*refs):
        ins, outs, send, recv = refs[:n], refs[n:2 * n], refs[2 * n], refs[2 * n + 1]
        x, y, c, _ = _position()
        cps = []
        for t in range(n):
            cp = pltpu.make_async_remote_copy(src_ref=ins[t].at[:, 1 - c], dst_ref=outs[t], send_sem=send.at[t], recv_sem=recv.at[t],
                                              device_id=(x, y, 1 - c), device_id_type=MESH)
            cp.start()
            cps.append(cp)
        for cp in cps:
            cp.wait()

    return pl.pallas_call(
        body, name="rs_exchange_sibling", in_specs=[HBM] * n, out_specs=[HBM] * n,
        out_shape=[SDS((g.shape[0],) + g.shape[2:], g.dtype) for g in gs],
        scratch_shapes=[pltpu.SemaphoreType.DMA((n,)), pltpu.SemaphoreType.DMA((n,))],
    )(*gs)


def rs_pair_add(gs, rs, c):
    n = len(gs)

    def body(c_ref, *refs):
        for t in range(n):
            refs[2 * n + t][0] = (refs[t][0, 0].astype(F32) + refs[n + t][0].astype(F32)).astype(BF16)

    in_specs, out_specs, out_shape = [], [], []
    for g in gs:
        _, _, h, C = g.shape
        in_specs.append(pl.BlockSpec((1, 1, h // ROW_SPLIT, C), lambda j, r, c_ref: (j, c_ref[0], r, 0)))
    for g in gs:
        _, _, h, C = g.shape
        spec = pl.BlockSpec((1, h // ROW_SPLIT, C), lambda j, r, c_ref: (j, r, 0))
        in_specs.append(spec)
        out_specs.append(spec)
        out_shape.append(SDS((N_CHIPS, h, C), BF16))
    return pl.pallas_call(
        body, name="rs_pair_add",
        grid_spec=pltpu.PrefetchScalarGridSpec(num_scalar_prefetch=1, grid=(N_CHIPS, ROW_SPLIT), in_specs=in_specs, out_specs=out_specs),
        out_shape=out_shape, compiler_params=_params(("parallel", "parallel")),
    )(c, *gs, *rs)


def _rs_plan(n):
    def plan(refs, x, y, c, chips):
        return [(refs[t].at[2 * cx + cy], refs[n + t].at[j], (cx, cy, c)) for t in range(n) for j, (cx, cy) in enumerate(chips)]

    return plan


def rs_chip_add(ps, qs, me_c):
    n = len(ps)

    def body(me_ref, *refs):
        for t in range(n):
            q = refs[n + t]
            refs[2 * n + t][...] = jnp.zeros_like(refs[2 * n + t])
            refs[2 * n + t][me_ref[1]] = ((refs[t][0].astype(F32) + q[0].astype(F32)) + q[1].astype(F32)) + q[2].astype(F32)

    in_specs, out_specs, out_shape = [], [], []
    for p in ps:
        _, h, C = p.shape
        in_specs.append(pl.BlockSpec((1, h // ROW_SPLIT, C), lambda r, me_ref: (me_ref[0], r, 0)))
    for p in ps:
        _, h, C = p.shape
        in_specs.append(pl.BlockSpec((3, h // ROW_SPLIT, C), lambda r, me_ref: (0, r, 0)))
        out_specs.append(pl.BlockSpec((2, h // ROW_SPLIT, C), lambda r, me_ref: (0, r, 0)))
        out_shape.append(SDS((2, h, C), F32))
    return pl.pallas_call(
        body, name="rs_chip_add",
        grid_spec=pltpu.PrefetchScalarGridSpec(num_scalar_prefetch=1, grid=(ROW_SPLIT,), in_specs=in_specs, out_specs=out_specs),
        out_shape=out_shape, compiler_params=_params(("parallel",)),
    )(me_c, *ps, *qs)


def rs_share(rs):
    n = len(rs)

    def body(*refs):
        outs, send, recv = refs[n:2 * n], refs[2 * n], refs[2 * n + 1]
        x, y, c, _ = _position()
        cps = []
        for t in range(n):
            cp = pltpu.make_async_remote_copy(src_ref=outs[t].at[c], dst_ref=outs[t].at[c], send_sem=send.at[t], recv_sem=recv.at[t],
                                              device_id=(x, y, 1 - c), device_id_type=MESH)
            cp.start()
            cps.append(cp)
        for cp in cps:
            cp.wait()

    return pl.pallas_call(
        body, name="rs_share", in_specs=[HBM] * n, out_specs=[HBM] * n,
        out_shape=[SDS(r.shape, r.dtype) for r in rs],
        input_output_aliases={t: t for t in range(n)},
        scratch_shapes=[pltpu.SemaphoreType.DMA((n,))] * 2,
    )(*rs)


def rs_begin(gs, name):
    c = lax.axis_index("c")
    n = len(gs)
    g5 = [g.reshape(N_CHIPS, 2, g.shape[1] // 2, g.shape[2]) for g in gs]
    from_sibling = rs_exchange_sibling(g5)
    pair = rs_pair_add(g5, from_sibling, jnp.reshape(c, (1,)).astype(jnp.int32))
    lands = [jnp.zeros((3,) + p.shape[1:], p.dtype) for p in pair]
    send, recv, bufs, token = split_start(list(pair) + lands, _rs_plan(n), 3 * n, from_sibling[0], name)
    return (send, recv, bufs, [g.shape for g in gs]), token


def rs_end(state, after, name):
    x, y, c = lax.axis_index("x"), lax.axis_index("y"), lax.axis_index("c")
    send, recv, bufs, shapes = state
    n = len(shapes)
    bufs = split_wait(send, recv, bufs, _rs_plan(n), after, name)
    half = rs_chip_add(bufs[:n], bufs[n:], jnp.stack([2 * x + y, c]).astype(jnp.int32))
    both = rs_share(half)
    return [b.reshape(s[1], s[2]) for b, s in zip(both, shapes)]


def _pad_last(a, n):
    return jnp.pad(a, [(0, 0)] * (a.ndim - 1) + [(0, n - a.shape[-1])])


def _heads_to_groups(w):
    k = w.shape[0]
    return _pad_last(w.reshape(k, ML_HEADS, ML_HEAD_DIM).transpose(1, 0, 2), GROUP)


def _groups_to_heads(g):
    return g[:, :, :ML_HEAD_DIM].transpose(1, 0, 2).reshape(g.shape[1], D_TOK)


def _cols_to_groups(w):
    k, n = w.shape
    return w.reshape(k, n // GROUP, GROUP).transpose(1, 0, 2)


def _groups_to_cols(g):
    n, k, _ = g.shape
    return g.transpose(1, 0, 2).reshape(k, n * GROUP)


def _chips_to_cols(a):
    return a.transpose(1, 0, 2).reshape(a.shape[1], -1)


def _cols_to_chips(w):
    k, n = w.shape
    return w.reshape(k, N_CHIPS, n // N_CHIPS).transpose(1, 0, 2)


def _mlstm_in_groups(w):
    parts = [_heads_to_groups(w[:, i * D_TOK:(i + 1) * D_TOK]) for i in range(4)]
    gates = _pad_last(w[:, 4 * D_TOK:4 * D_TOK + 2 * ML_HEADS], GROUP)[None]
    qmem = w[:, 4 * D_TOK + 2 * ML_HEADS:][None]
    return jnp.concatenate(parts + [qmem, gates], axis=0)


def _mlstm_in_ungroup(g):
    parts = [_groups_to_heads(g[4 * i:4 * i + 4]) for i in range(4)]
    return jnp.concatenate(parts + [g[17][:, :2 * ML_HEADS], g[16]], axis=1)


def _taps_to_groups(w, width):
    taps = w.shape[0]
    g = _pad_last(w.reshape(taps, -1, width), GROUP).transpose(1, 0, 2)
    return jnp.pad(g, ((0, 0), (0, 8 - taps), (0, 0)))


def _groups_to_taps(g, taps, width):
    return g[:, :taps, :width].transpose(1, 0, 2).reshape(taps, -1)


SMALL_ROWS = 24
SMALL_IN_COLS = 384
SMALL_OUT_COLS = 1536


class _Gathered:
    def __init__(self, srcs, groups, me):
        keys = [k for g in groups for k in g]
        placed = dict(zip(keys, place_own([(srcs[k], ()) for k in keys], me)))
        self.groups, self.states, self.ready = groups, [], {}
        self.group_of = {k: gi for gi, g in enumerate(groups) for k in g}
        token = me
        for gi, g in enumerate(groups):
            state, token = gather_start([srcs[k] for k in g], [placed[k] for k in g], token, f"gather_start_{gi}")
            self.states.append(state)
        self.started = token

    def _get(self, key, after):
        gi = self.group_of[key]
        if gi not in self.ready:
            got = gather_wait(self.states[gi], after if gi else self.started, f"gather_wait_{gi}")
            self.ready[gi] = dict(zip(self.groups[gi], gather_pass_on(got, self.states[gi][3], f"gather_pass_on_{gi}")))
        return self.ready[gi][key]

    def ffn(self, l, i, after):
        return tuple(self._get((n, l, i), after) for n in ("wg", "wu", "wd"))

    def mixer(self, l, after):
        win = _chips_to_cols(self._get(("win", l), after))
        win = _cols_to_groups(win) if l % 2 == 0 else _mlstm_in_groups(win)
        wkv = _cols_to_groups(self._get(("wkv", l), after).reshape(D_MODEL, 2 * D_XA))
        wout = self._get(("wout", l), after)
        if l % 2:
            wout = wout.reshape(D_MODEL, D_MODEL)
            tok = jnp.pad(wout[:D_TOK].reshape(ML_HEADS, ML_HEAD_DIM, D_MODEL), ((0, 0), (0, GROUP - ML_HEAD_DIM), (0, 0)))
            wout = jnp.concatenate([tok, wout[D_TOK:][None]], axis=0)
        return win, wkv, wout


class _GradSink:
    def __init__(self):
        self.queue, self.done, self.count = [], {}, 0

    @staticmethod
    def _by_chip(key, g):
        if key[0] == "wkv":
            return _groups_to_cols(g).reshape(N_CHIPS, D_MODEL // N_CHIPS, 2 * D_XA)
        if key[0] == "win":
            return _cols_to_chips(_groups_to_cols(g) if key[1] % 2 == 0 else _mlstm_in_ungroup(g))
        if key[0] == "wout" and key[1] % 2:
            full = jnp.concatenate([g[:ML_HEADS, :ML_HEAD_DIM].reshape(D_TOK, D_MODEL), g[ML_HEADS]], axis=0)
            return full.reshape(N_CHIPS, D_MODEL // N_CHIPS, D_MODEL)
        return g

    def begin(self, grads):
        keys = list(grads)
        state, token = rs_begin([self._by_chip(k, grads[k]) for k in keys], f"rs_start_{self.count}")
        self.queue.append((keys, state, self.count))
        self.count += 1
        return token

    def end(self, after):
        keys, state, i = self.queue.pop(0)
        self.done.update(zip(keys, rs_end(state, after, f"rs_wait_{i}")))


def _local_step(x, mem, tgt, P, weights, sink):
    memb = mem.astype(BF16)
    saved = []
    X, Xb = x, x.astype(BF16)
    after = Xb
    for l in range(DEPTH):
        s = {}
        s["x0b"] = Xb
        s["wa"] = weights.ffn(l, 0, after)
        s["g1a"], s["u1a"], s["ha"], s["z1"], X1, X1b = ffn_fwd(Xb, X, *s["wa"], P["ln_g"][l][0], P["ln_b"][l][0])
        s["x1b"] = X1b
        s["wm"] = win, wkv, wout = weights.mixer(l, X1b)
        u = proj(X1b, win, "mixer_in")
        kv = proj(memb, wkv, "mem_kv")
        s["u"], s["kv"] = u, kv
        if l % 2 == 0:
            tok = conv_mixer_fwd(u, P["convw"])
            qg = 9
        else:
            s["qk"] = qk_conv_fwd(u, P["qkw"])
            s["hm"], s["cst"], s["mst"] = mlstm_fwd(s["qk"], u, P["bg"])
            tok = head_norm_fwd(s["hm"], u, P["hg"])
            qg = 16
        xa = xattn_fwd(u, qg, kv)
        s["m"] = jnp.concatenate([tok, xa], axis=0)
        s["z2"], X2, X2b = contract_ln(s["m"], wout, X1, P["ln_g"][l][1], P["ln_b"][l][1], 1.0, "mixer_out_ln")
        s["x2b"] = X2b
        s["wb"] = weights.ffn(l, 1, X2b)
        s["g1b"], s["u1b"], s["hb"], s["z3"], X, Xb = ffn_fwd(X2b, X2, *s["wb"], P["ln_g"][l][2], P["ln_b"][l][2])
        after = Xb
        saved.append(s)

    loss, dX = loss_grad(X, tgt)

    G = {"ln_g": [[None] * 3 for _ in range(DEPTH)], "ln_b": [[None] * 3 for _ in range(DEPTH)]}
    pin = [jnp.zeros((1, 1), F32)]

    def ffn_backward(l, i, dX, z, xinb, g1, u1, h, w):
        k = 2 * i
        dz, dyb, G["ln_g"][l][k], G["ln_b"][l][k] = ln_bwd(dX, z, P["ln_g"][l][k] + pin[0], 0.5, "ffn_ln_bwd")
        dgb, dub, dx = ffn_bwd(dyb, dz, w[2], w[0], w[1], g1, u1)
        grads = {("wd", l, i): wgrad(h, dyb, BF16, "wgrad_down"), ("wg", l, i): wgrad(xinb, dgb, BF16, "wgrad_gate"),
                 ("wu", l, i): wgrad(xinb, dub, BF16, "wgrad_up")}
        return dx, grads

    pending = 0
    for l in reversed(range(DEPTH)):
        s = saved[l]
        win, wkv, wout = s["wm"]
        dX, grads = ffn_backward(l, 1, dX, s["z3"], s["x2b"], s["g1b"], s["u1b"], s["hb"], s["wb"])
        if pending:
            sink.end(dX)
        dz2, dz2b, G["ln_g"][l][1], G["ln_b"][l][1] = ln_bwd(dX, s["z2"], P["ln_g"][l][1], 1.0, "mixer_ln_bwd")
        dm = proj_t(dz2b, wout, "mixer_out_bwd")
        grads[("wout", l)] = wgrad(s["m"], dz2b, BF16, "wgrad_out")
        u, kv = s["u"], s["kv"]
        if l % 2 == 0:
            db, dc, dxi, G["convw"] = conv_mixer_bwd(u, P["convw"], dm)
            dq, dkv = xattn_bwd(u, 9, kv, dm, 3)
            du = jnp.concatenate([db, dc, dxi, dq], axis=0)
        else:
            dh, do, G["hg"] = head_norm_bwd(s["hm"], u, P["hg"], dm)
            dqk, dv, dgate, G["bg"] = mlstm_bwd(s["qk"], u, P["bg"], s["cst"], s["mst"], dh)
            duqk, G["qkw"] = qk_conv_bwd(u, P["qkw"], dqk)
            dq, dkv = xattn_bwd(u, 16, kv, dm, 4)
            du = jnp.concatenate([duqk, dv, do, dq, dgate], axis=0)
        grads[("win", l)] = wgrad(s["x1b"], du, BF16, "wgrad_in")
        grads[("wkv", l)] = wgrad(memb, dkv.astype(BF16), BF16, "wgrad_kv")
        dX = contract_t(du, win, dz2, "mixer_in_bwd")
        pin[0] = sink.begin(grads)[0:1, 0:1]
        dX, grads = ffn_backward(l, 0, dX, s["z1"], s["x0b"], s["g1a"], s["u1a"], s["ha"], s["wa"])
        sink.end(dX)
        pin[0] = sink.begin(grads)[0:1, 0:1]
        pending = 1
    sink.end(dX)
    return loss, dX, G


def kernel(x, mem, ln_g, ln_b, ffn_w_gate, ffn_w_up, ffn_w_down, w_kv_mem, w_out, w_in_conv, conv_w, w_in_mlstm, b_gates, qk_conv_w, head_norm_g, loss_target, m_ln_g, m_ln_b, m_ffn_w_gate, m_ffn_w_up, m_ffn_w_down, m_w_kv_mem, m_w_out, m_w_in_conv, m_conv_w, m_w_in_mlstm, m_b_gates, m_qk_conv_w, m_head_norm_g, v_ln_g, v_ln_b, v_ffn_w_gate, v_ffn_w_up, v_ffn_w_down, v_w_kv_mem, v_w_out, v_w_in_conv, v_conv_w, v_w_in_mlstm, v_b_gates, v_qk_conv_w, v_head_norm_g):
    cx, cy = lax.axis_index("x"), lax.axis_index("y")
    chip = 2 * cx + cy

    srcs = {}
    for l in range(DEPTH):
        for i in range(2):
            srcs[("wg", l, i)] = ffn_w_gate[l, i].astype(BF16)
            srcs[("wu", l, i)] = ffn_w_up[l, i].astype(BF16)
            srcs[("wd", l, i)] = ffn_w_down[l, i].astype(BF16)
        srcs[("wkv", l)] = w_kv_mem[l].astype(BF16)
        srcs[("wout", l)] = w_out[l].astype(BF16)
    srcs[("win", 0)] = w_in_conv[0].astype(BF16)
    srcs[("win", 1)] = w_in_mlstm[0].astype(BF16)
    ffn_keys = lambda l, i: [("wg", l, i), ("wu", l, i), ("wd", l, i)]
    mixer_keys = lambda l: [("win", l), ("wkv", l), ("wout", l)]
    groups = [ffn_keys(0, 0), mixer_keys(0) + mixer_keys(1) + ffn_keys(0, 1), ffn_keys(1, 0), ffn_keys(1, 1)]
    gathered = _Gathered(srcs, groups, jnp.reshape(chip, (1,)).astype(jnp.int32))

    small = jnp.zeros((SMALL_ROWS, SMALL_IN_COLS), F32)
    small = small.at[0:6, 0:256].set(ln_g.reshape(6, 256)).at[6:12, 0:256].set(ln_b.reshape(6, 256))
    small = small.at[12:15, 0:192].set(conv_w[0]).at[16:20, 0:384].set(qk_conv_w[0])
    smalls = small_allgather(small, reduce=False)[0::2]
    ln_g_full = _chips_to_cols(smalls[:, 0:6, 0:256]).reshape(DEPTH, 3, 1, D_MODEL)
    ln_b_full = _chips_to_cols(smalls[:, 6:12, 0:256]).reshape(DEPTH, 3, 1, D_MODEL)
    conv_w_full = _chips_to_cols(smalls[:, 12:15, 0:192])
    qk_w_full = _chips_to_cols(smalls[:, 16:20, 0:384])

    P = {"ln_g": ln_g_full, "ln_b": ln_b_full, "convw": _taps_to_groups(conv_w_full, GROUP),
         "qkw": _taps_to_groups(qk_w_full, ML_HEAD_DIM), "bg": _pad_last(b_gates, GROUP),
         "hg": _pad_last(head_norm_g[0], GROUP)[:, None, :]}

    sink = _GradSink()
    loss, grad_x, G = _local_step(x[0], mem[0], loss_target[0], P, gathered, sink)
    red = sink.done

    sg = jnp.zeros((SMALL_ROWS, SMALL_OUT_COLS), F32)
    dln_g = jnp.concatenate([G["ln_g"][l][k] for l in range(DEPTH) for k in range(3)], axis=0)
    dln_b = jnp.concatenate([G["ln_b"][l][k] for l in range(DEPTH) for k in range(3)], axis=0)
    sg = sg.at[0:6, 0:D_MODEL].set(dln_g).at[6:12, 0:D_MODEL].set(dln_b)
    sg = sg.at[12:15, 0:D_TOK].set(_groups_to_taps(G["convw"], 3, GROUP))
    sg = sg.at[15:16, 0:8].set(G["bg"][:, 0:8]).at[15:16, 8:9].set(loss)
    sg = sg.at[16:20, 0:2 * D_TOK].set(_groups_to_taps(G["qkw"], 4, ML_HEAD_DIM))
    sg = sg.at[20:24, 0:ML_HEAD_DIM].set(G["hg"][:, 0, :ML_HEAD_DIM])
    tot = small_allgather(sg, reduce=True)

    grads = {
        "ln_g": lax.dynamic_slice(tot[0:6, 0:D_MODEL], (0, chip * 256), (6, 256)).reshape(DEPTH, 3, 256),
        "ln_b": lax.dynamic_slice(tot[6:12, 0:D_MODEL], (0, chip * 256), (6, 256)).reshape(DEPTH, 3, 256),
        "ffn_w_gate": jnp.stack([jnp.stack([red[("wg", l, i)] for i in range(2)]) for l in range(DEPTH)]),
        "ffn_w_up": jnp.stack([jnp.stack([red[("wu", l, i)] for i in range(2)]) for l in range(DEPTH)]),
        "ffn_w_down": jnp.stack([jnp.stack([red[("wd", l, i)] for i in range(2)]) for l in range(DEPTH)]),
        "w_kv_mem": jnp.stack([red[("wkv", l)] for l in range(DEPTH)]),
        "w_out": jnp.stack([red[("wout", l)] for l in range(DEPTH)]),
        "w_in_conv": red[("win", 0)][None],
        "conv_w": lax.dynamic_slice(tot[12:15, 0:D_TOK], (0, chip * 192), (3, 192))[None],
        "w_in_mlstm": red[("win", 1)][None],
        "b_gates": tot[15:16, 0:8],
        "qk_conv_w": lax.dynamic_slice(tot[16:20, 0:2 * D_TOK], (0, chip * 384), (4, 384))[None],
        "head_norm_g": tot[20:24, 0:ML_HEAD_DIM][None],
    }
    loss_total = tot[15, 8]

    weights = {"ln_g": ln_g, "ln_b": ln_b, "ffn_w_gate": ffn_w_gate, "ffn_w_up": ffn_w_up, "ffn_w_down": ffn_w_down,
               "w_kv_mem": w_kv_mem, "w_out": w_out, "w_in_conv": w_in_conv, "conv_w": conv_w, "w_in_mlstm": w_in_mlstm,
               "b_gates": b_gates, "qk_conv_w": qk_conv_w, "head_norm_g": head_norm_g}
    ms = {"ln_g": m_ln_g, "ln_b": m_ln_b, "ffn_w_gate": m_ffn_w_gate, "ffn_w_up": m_ffn_w_up, "ffn_w_down": m_ffn_w_down,
          "w_kv_mem": m_w_kv_mem, "w_out": m_w_out, "w_in_conv": m_w_in_conv, "conv_w": m_conv_w, "w_in_mlstm": m_w_in_mlstm,
          "b_gates": m_b_gates, "qk_conv_w": m_qk_conv_w, "head_norm_g": m_head_norm_g}
    vs = {"ln_g": v_ln_g, "ln_b": v_ln_b, "ffn_w_gate": v_ffn_w_gate, "ffn_w_up": v_ffn_w_up, "ffn_w_down": v_ffn_w_down,
          "w_kv_mem": v_w_kv_mem, "w_out": v_w_out, "w_in_conv": v_w_in_conv, "conv_w": v_conv_w, "w_in_mlstm": v_w_in_mlstm,
          "b_gates": v_b_gates, "qk_conv_w": v_qk_conv_w, "head_norm_g": v_head_norm_g}
    names = list(weights)
    deltas, new_m, new_v = [], [], []
    for nme in names:
        w = weights[nme]
        shp = w.shape
        two = (math.prod(shp[:-1]), shp[-1])
        d, nm, nv = adamw(w.reshape(two), grads[nme].reshape(two), ms[nme].reshape(two), vs[nme].reshape(two), "adamw_" + nme)
        deltas.append(d.reshape(shp))
        new_m.append(nm.reshape(shp))
        new_v.append(nv.reshape(shp))
    return (loss_total, grad_x[None], *[grads[nme] for nme in names], *deltas, *new_m, *new_v)
```

```python
import functools
import math

import jax
import jax.numpy as jnp
from jax import lax
from jax.experimental import pallas as pl
from jax.experimental.pallas import tpu as pltpu

F32 = jnp.float32
BF16 = jnp.bfloat16
SDS = jax.ShapeDtypeStruct

D_MODEL = 1024
DEPTH = 2
N_MEM = 256
XA_HEADS = 4
XA_HEAD_DIM = 64
D_XA = 256
D_TOK = 768
ML_HEADS = 4
ML_HEAD_DIM = 192
ML_CHUNK = 64
D_FF = 2816
LN_EPS = 1e-5
ALPHA = (2.0 * DEPTH) ** 0.25
N_CHIPS = 4
N_DEV = 8
FF_SHARD = D_FF // N_CHIPS
GROUP = 256
NEG = -1e30

ADAM_LR = 0.001
ADAM_B1 = 0.9
ADAM_B2 = 0.999
ADAM_EPS = 1e-08
ADAM_WD = 0.01
ADAM_STEP = 10

VMEM_LIMIT = 56 * 1024 * 1024

NN = ((1,), (0,))
NT = ((1,), (1,))
TN = ((0,), (0,))
MESH = pl.DeviceIdType.MESH


def _dot(a, b, dims):
    return lax.dot_general(a, b, (dims, ((), ())), preferred_element_type=F32)


def _bdot(a, b, ca, cb):
    dims = (((ca,), (cb,)), ((0,), (0,)))
    ah, bh = a.astype(BF16), b.astype(BF16)
    al, bl = (a - ah.astype(F32)).astype(BF16), (b - bh.astype(F32)).astype(BF16)
    dot = functools.partial(lax.dot_general, dimension_numbers=dims, preferred_element_type=F32)
    return dot(ah, bh) + dot(al, bh) + dot(ah, bl)


def _sigmoid(x):
    return 1.0 / (1.0 + jnp.exp(-x))


def _params(sem, vmem=VMEM_LIMIT):
    return pltpu.CompilerParams(dimension_semantics=sem, vmem_limit_bytes=vmem)


def _tile(n, want):
    t = min(n, want)
    assert n % t == 0, (n, t)
    return t


def _layer_norm(z, gamma, beta):
    mu = jnp.mean(z, axis=-1, keepdims=True)
    zc = z - mu
    var = jnp.mean(zc * zc, axis=-1, keepdims=True)
    return zc * lax.rsqrt(var + LN_EPS) * gamma + beta


def _resident(shape):
    return pl.BlockSpec(shape, lambda *_: (0,) * len(shape), pipeline_mode=pl.Buffered(1))


def _group_block(G, want):
    return max(d for d in range(1, max(1, min(G, want)) + 1) if G % d == 0)


def ffn_fwd(xb, x, wg, wu, wd, gamma, beta):
    S, K = xb.shape
    G, _, N = wg.shape
    ts = _tile(S, 512)

    def body(xb_ref, x_ref, wg_ref, wu_ref, wd_ref, gm_ref, bt_ref, g_ref, u_ref, h_ref, z_ref, xn_ref, xnb_ref):
        j = pl.program_id(1)
        xv = xb_ref[...]
        g = _dot(xv, wg_ref[j], NN)
        u = _dot(xv, wu_ref[j], NN)
        h = (g * _sigmoid(g) * u).astype(BF16)
        g_ref[0] = g.astype(BF16)
        u_ref[0] = u.astype(BF16)
        h_ref[0] = h
        y = _dot(h, wd_ref[j], NN)

        @pl.when(j == 0)
        def _():
            z_ref[...] = y

        @pl.when(j > 0)
        def _():
            z_ref[...] += y

        @pl.when(j == G - 1)
        def _():
            z = ALPHA * x_ref[...] + 0.5 * z_ref[...]
            xn = _layer_norm(z, gm_ref[...], bt_ref[...])
            z_ref[...] = z
            xn_ref[...] = xn
            xnb_ref[...] = xn.astype(BF16)

    row = pl.BlockSpec((ts, K), lambda s, j: (s, 0))
    vec = pl.BlockSpec((1, K), lambda s, j: (0, 0))
    wspec = _resident((G, K, N))
    ospec = pl.BlockSpec((1, ts, N), lambda s, j: (j, s, 0))
    return pl.pallas_call(
        body, name="ffn_fwd", grid=(S // ts, G),
        in_specs=[row, row, wspec, wspec, _resident((G, N, K)), vec, vec],
        out_specs=[ospec, ospec, ospec, row, row, row],
        out_shape=[SDS((G, S, N), BF16), SDS((G, S, N), BF16), SDS((G, S, N), BF16),
                   SDS((S, K), F32), SDS((S, K), F32), SDS((S, K), BF16)],
        compiler_params=_params(("parallel", "arbitrary")),
    )(xb, x, wg, wu, wd, gamma, beta)


def proj(xb, w, name):
    S, K = xb.shape
    G, _, N = w.shape
    ts = _tile(S, 1024)
    gb = _group_block(G, 6)

    def body(x_ref, w_ref, y_ref):
        xv = x_ref[...]
        for j in range(gb):
            y_ref[j] = _dot(xv, w_ref[j], NN)

    return pl.pallas_call(
        body, name=name, grid=(S // ts, G // gb),
        in_specs=[pl.BlockSpec((ts, K), lambda s, g: (s, 0)), pl.BlockSpec((gb, K, N), lambda s, g: (g, 0, 0))],
        out_specs=pl.BlockSpec((gb, ts, N), lambda s, g: (g, s, 0)),
        out_shape=SDS((G, S, N), F32),
        compiler_params=_params(("parallel", "parallel")),
    )(xb, w)


def contract_ln(a, w, xres, gamma, beta, scale, name):
    G, S, Kg = a.shape
    N = w.shape[2]
    ts = _tile(S, 512)

    def body(a_ref, w_ref, x_ref, g_ref, b_ref, z_ref, xn_ref, xb_ref):
        acc = _dot(a_ref[0], w_ref[0], NN)
        for j in range(1, G):
            acc = acc + _dot(a_ref[j], w_ref[j], NN)
        z = ALPHA * x_ref[...] + scale * acc
        xn = _layer_norm(z, g_ref[...], b_ref[...])
        z_ref[...] = z
        xn_ref[...] = xn
        xb_ref[...] = xn.astype(BF16)

    row = pl.BlockSpec((ts, N), lambda s: (s, 0))
    vec = pl.BlockSpec((1, N), lambda s: (0, 0))
    return pl.pallas_call(
        body, name=name, grid=(S // ts,),
        in_specs=[pl.BlockSpec((G, ts, Kg), lambda s: (0, s, 0)), pl.BlockSpec((G, Kg, N), lambda s: (0, 0, 0)), row, vec, vec],
        out_specs=[row, row, row],
        out_shape=[SDS((S, N), F32), SDS((S, N), F32), SDS((S, N), BF16)],
        compiler_params=_params(("parallel",)),
    )(a, w, xres, gamma, beta)


def ln_bwd(dx, z, gamma, out_scale, name):
    S, N = dx.shape
    ts = _tile(S, 512)

    def body(dx_ref, z_ref, g_ref, dz_ref, dzb_ref, dg_ref, db_ref):
        @pl.when(pl.program_id(0) == 0)
        def _():
            dg_ref[...] = jnp.zeros_like(dg_ref)
            db_ref[...] = jnp.zeros_like(db_ref)

        z = z_ref[...]
        mu = jnp.mean(z, axis=-1, keepdims=True)
        zc = z - mu
        var = jnp.mean(zc * zc, axis=-1, keepdims=True)
        rstd = lax.rsqrt(var + LN_EPS)
        xhat = zc * rstd
        dxv = dx_ref[...]
        dg_ref[...] += jnp.sum(dxv * xhat, axis=0, keepdims=True)
        db_ref[...] += jnp.sum(dxv, axis=0, keepdims=True)
        dxh = dxv * g_ref[...]
        m1 = jnp.mean(dxh, axis=-1, keepdims=True)
        m2 = jnp.mean(dxh * xhat, axis=-1, keepdims=True)
        dz = rstd * (dxh - m1 - xhat * m2)
        dz_ref[...] = dz
        dzb_ref[...] = (out_scale * dz).astype(BF16)

    row = pl.BlockSpec((ts, N), lambda s: (s, 0))
    vec = pl.BlockSpec((1, N), lambda s: (0, 0))
    return pl.pallas_call(
        body, name=name, grid=(S // ts,),
        in_specs=[row, row, vec],
        out_specs=[row, row, vec, vec],
        out_shape=[SDS((S, N), F32), SDS((S, N), BF16), SDS((1, N), F32), SDS((1, N), F32)],
        compiler_params=_params(("arbitrary",)),
    )(dx, z, gamma)


def ffn_bwd(dyb, dz, wd, wg, wu, g1, u1):
    S, K = dyb.shape
    G, N, _ = wd.shape
    ts = _tile(S, 512)

    def body(dy_ref, dz_ref, wd_ref, wg_ref, wu_ref, g_ref, u_ref, dg_ref, du_ref, dx_ref):
        j = pl.program_id(1)
        dh = _dot(dy_ref[...], wd_ref[j], NT)
        g = g_ref[0].astype(F32)
        sig = _sigmoid(g)
        dg = (dh * u_ref[0].astype(F32) * (sig * (1.0 + g * (1.0 - sig)))).astype(BF16)
        du = (dh * (g * sig)).astype(BF16)
        dg_ref[0] = dg
        du_ref[0] = du
        part = _dot(dg, wg_ref[j], NT) + _dot(du, wu_ref[j], NT)

        @pl.when(j == 0)
        def _():
            dx_ref[...] = ALPHA * dz_ref[...] + part

        @pl.when(j > 0)
        def _():
            dx_ref[...] += part

    row = pl.BlockSpec((ts, K), lambda s, j: (s, 0))
    gspec = pl.BlockSpec((1, ts, N), lambda s, j: (j, s, 0))
    wspec = _resident((G, K, N))
    return pl.pallas_call(
        body, name="ffn_bwd", grid=(S // ts, G),
        in_specs=[row, row, _resident((G, N, K)), wspec, wspec, gspec, gspec],
        out_specs=[gspec, gspec, row],
        out_shape=[SDS((G, S, N), BF16), SDS((G, S, N), BF16), SDS((S, K), F32)],
        compiler_params=_params(("parallel", "arbitrary")),
    )(dyb, dz, wd, wg, wu, g1, u1)


def proj_t(dyb, w, name):
    S, N = dyb.shape
    G, Kg, _ = w.shape
    ts = _tile(S, 1024)

    def body(dy_ref, w_ref, da_ref):
        dy = dy_ref[...]
        for j in range(G):
            da_ref[j] = _dot(dy, w_ref[j], NT)

    return pl.pallas_call(
        body, name=name, grid=(S // ts,),
        in_specs=[pl.BlockSpec((ts, N), lambda s: (s, 0)), pl.BlockSpec((G, Kg, N), lambda s: (0, 0, 0))],
        out_specs=pl.BlockSpec((G, ts, Kg), lambda s: (0, s, 0)),
        out_shape=SDS((G, S, Kg), F32),
        compiler_params=_params(("parallel",)),
    )(dyb, w)


def contract_t(da, w, res, name):
    G, S, Ng = da.shape
    K = w.shape[1]
    ts = _tile(S, 512)
    gb = _group_block(G, 6)

    def body(da_ref, w_ref, r_ref, o_ref):
        g = pl.program_id(1)
        part = _dot(da_ref[0], w_ref[0], NT)
        for j in range(1, gb):
            part = part + _dot(da_ref[j], w_ref[j], NT)

        @pl.when(g == 0)
        def _():
            o_ref[...] = ALPHA * r_ref[...] + part

        @pl.when(g > 0)
        def _():
            o_ref[...] += part

    row = pl.BlockSpec((ts, K), lambda s, g: (s, 0))
    return pl.pallas_call(
        body, name=name, grid=(S // ts, G // gb),
        in_specs=[pl.BlockSpec((gb, ts, Ng), lambda s, g: (g, s, 0)), pl.BlockSpec((gb, K, Ng), lambda s, g: (g, 0, 0)), row],
        out_specs=row,
        out_shape=SDS((S, K), F32),
        compiler_params=_params(("parallel", "arbitrary")),
    )(da, w, res)


WGRAD_ACC_ELEMS = 6 * 1024 * 256


def wgrad(a, b, out_dtype, name):
    ga, gb = a.ndim == 3, b.ndim == 3
    G = a.shape[0] if ga else b.shape[0]
    S, K = a.shape[-2:]
    N = b.shape[-1]
    ts = _tile(S, 1024)
    ns = S // ts
    ng = _group_block(G, WGRAD_ACC_ELEMS // (K * N))

    def body(a_ref, b_ref, o_ref, acc):
        s = pl.program_id(1)

        @pl.when(s == 0)
        def _():
            acc[...] = jnp.zeros_like(acc)

        for j in range(ng):
            acc[j] += _dot(a_ref[j] if ga else a_ref[...], b_ref[j] if gb else b_ref[...], TN)

        @pl.when(s == ns - 1)
        def _():
            o_ref[...] = acc[...].astype(out_dtype)

    aspec = pl.BlockSpec((ng, ts, K), lambda g, s: (g, s, 0)) if ga else pl.BlockSpec((ts, K), lambda g, s: (s, 0))
    bspec = pl.BlockSpec((ng, ts, N), lambda g, s: (g, s, 0)) if gb else pl.BlockSpec((ts, N), lambda g, s: (s, 0))
    return pl.pallas_call(
        body, name=name, grid=(G // ng, ns),
        in_specs=[aspec, bspec],
        out_specs=pl.BlockSpec((ng, K, N), lambda g, s: (g, 0, 0)),
        out_shape=SDS((G, K, N), out_dtype),
        scratch_shapes=[pltpu.VMEM((ng, K, N), F32)],
        compiler_params=_params(("parallel", "arbitrary")),
    )(a, b)


def loss_grad(xn, tgt):
    S, N = xn.shape
    ts = _tile(S, 512)

    def body(x_ref, t_ref, l_ref, dx_ref):
        @pl.when(pl.program_id(0) == 0)
        def _():
            l_ref[...] = jnp.zeros_like(l_ref)

        e = x_ref[...] - t_ref[...]
        dx_ref[...] = e * (1.0 / N)
        l_ref[...] += 0.5 * jnp.sum(jnp.mean(e * e, axis=-1, keepdims=True), axis=0, keepdims=True)

    row = pl.BlockSpec((ts, N), lambda s: (s, 0))
    return pl.pallas_call(
        body, name="loss_grad", grid=(S // ts,),
        in_specs=[row, row],
        out_specs=[pl.BlockSpec((1, 1), lambda s: (0, 0)), row],
        out_shape=[SDS((1, 1), F32), SDS((S, N), F32)],
        compiler_params=_params(("arbitrary",)),
    )(xn, tgt)


def _shift_down(x, k):
    if k == 0:
        return x
    rows = lax.broadcasted_iota(jnp.int32, x.shape, 0)
    return jnp.where(rows >= k, pltpu.roll(x, k, 0), 0.0)


def _shift_up(x, k):
    if k == 0:
        return x
    n = x.shape[0]
    rows = lax.broadcasted_iota(jnp.int32, x.shape, 0)
    return jnp.where(rows < n - k, pltpu.roll(x, n - k, 0), 0.0)


LANES = 128


def conv_mixer_fwd(u, cw):
    _, S, _ = u.shape
    nh = GROUP // LANES

    def body(b_ref, c_ref, x_ref, w_ref, o_ref):
        p = c_ref[0] * x_ref[0]
        w = w_ref[0]
        conv = w[2:3] * p + w[1:2] * _shift_down(p, 1) + w[0:1] * _shift_down(p, 2)
        o_ref[0] = (b_ref[0] * conv).astype(BF16)

    def uspec(off):
        return pl.BlockSpec((1, S, LANES), lambda g, h: (g + off, 0, h))

    return pl.pallas_call(
        body, name="conv_mixer_fwd", grid=(3, nh),
        in_specs=[uspec(0), uspec(3), uspec(6), pl.BlockSpec((1, 8, LANES), lambda g, h: (g, 0, h))],
        out_specs=pl.BlockSpec((1, S, LANES), lambda g, h: (g, 0, h)),
        out_shape=SDS((3, S, GROUP), BF16),
        compiler_params=_params(("parallel", "parallel")),
    )(u, u, u, cw)


def conv_mixer_bwd(u, cw, dm):
    _, S, _ = u.shape
    nh = GROUP // LANES

    def body(b_ref, c_ref, x_ref, w_ref, d_ref, db_ref, dc_ref, dx_ref, dw_ref):
        cg, xi = c_ref[0], x_ref[0]
        p = cg * xi
        p1, p2 = _shift_down(p, 1), _shift_down(p, 2)
        w = w_ref[0]
        conv = w[2:3] * p + w[1:2] * p1 + w[0:1] * p2
        dt = d_ref[0]
        db_ref[0] = (dt * conv).astype(BF16)
        dcv = dt * b_ref[0]
        dp = w[2:3] * dcv + w[1:2] * _shift_up(dcv, 1) + w[0:1] * _shift_up(dcv, 2)
        dc_ref[0] = (dp * xi).astype(BF16)
        dx_ref[0] = (dp * cg).astype(BF16)
        dw = jnp.concatenate([jnp.sum(dcv * p2, axis=0, keepdims=True), jnp.sum(dcv * p1, axis=0, keepdims=True),
                              jnp.sum(dcv * p, axis=0, keepdims=True), jnp.zeros((5, LANES), F32)], axis=0)
        dw_ref[0] = dw

    def uspec(off):
        return pl.BlockSpec((1, S, LANES), lambda g, h: (g + off, 0, h))

    ospec = pl.BlockSpec((1, S, LANES), lambda g, h: (g, 0, h))
    wspec = pl.BlockSpec((1, 8, LANES), lambda g, h: (g, 0, h))
    return pl.pallas_call(
        body, name="conv_mixer_bwd", grid=(3, nh),
        in_specs=[uspec(0), uspec(3), uspec(6), wspec, ospec],
        out_specs=[ospec, ospec, ospec, wspec],
        out_shape=[SDS((3, S, GROUP), BF16)] * 3 + [SDS((3, 8, GROUP), F32)],
        compiler_params=_params(("parallel", "parallel")),
    )(u, u, u, cw, dm)


def qk_conv_fwd(u, qw):
    _, S, _ = u.shape
    nh = GROUP // LANES

    def body(u_ref, w_ref, o_ref):
        x = u_ref[0]
        w = w_ref[0]
        pre = w[3:4] * x + w[2:3] * _shift_down(x, 1) + w[1:2] * _shift_down(x, 2) + w[0:1] * _shift_down(x, 3)
        o_ref[0] = pre * _sigmoid(pre)

    spec = pl.BlockSpec((1, S, LANES), lambda g, h: (g, 0, h))
    return pl.pallas_call(
        body, name="qk_conv_fwd", grid=(8, nh),
        in_specs=[spec, pl.BlockSpec((1, 8, LANES), lambda g, h: (g, 0, h))],
        out_specs=spec,
        out_shape=SDS((8, S, GROUP), F32),
        compiler_params=_params(("parallel", "parallel")),
    )(u, qw)


def qk_conv_bwd(u, qw, dqk):
    _, S, _ = u.shape
    nh = GROUP // LANES

    def body(u_ref, w_ref, d_ref, du_ref, dw_ref):
        x = u_ref[0]
        w = w_ref[0]
        x1, x2, x3 = _shift_down(x, 1), _shift_down(x, 2), _shift_down(x, 3)
        pre = w[3:4] * x + w[2:3] * x1 + w[1:2] * x2 + w[0:1] * x3
        sig = _sigmoid(pre)
        dpre = d_ref[0] * (sig * (1.0 + pre * (1.0 - sig)))
        du = w[3:4] * dpre + w[2:3] * _shift_up(dpre, 1) + w[1:2] * _shift_up(dpre, 2) + w[0:1] * _shift_up(dpre, 3)
        du_ref[0] = du.astype(BF16)
        dw = jnp.concatenate([jnp.sum(dpre * x3, axis=0, keepdims=True), jnp.sum(dpre * x2, axis=0, keepdims=True),
                              jnp.sum(dpre * x1, axis=0, keepdims=True), jnp.sum(dpre * x, axis=0, keepdims=True),
                              jnp.zeros((4, LANES), F32)], axis=0)
        dw_ref[0] = dw

    spec = pl.BlockSpec((1, S, LANES), lambda g, h: (g, 0, h))
    wspec = pl.BlockSpec((1, 8, LANES), lambda g, h: (g, 0, h))
    return pl.pallas_call(
        body, name="qk_conv_bwd", grid=(8, nh),
        in_specs=[spec, wspec, spec],
        out_specs=[spec, wspec],
        out_shape=[SDS((8, S, GROUP), BF16), SDS((8, 8, GROUP), F32)],
        compiler_params=_params(("parallel", "parallel")),
    )(u, qw, dqk)


def _head_masks():
    lane = lax.broadcasted_iota(jnp.int32, (1, D_XA), 1)
    return [(lane >= h * XA_HEAD_DIM) & (lane < (h + 1) * XA_HEAD_DIM) for h in range(XA_HEADS)]


def xattn_fwd(u, qg, kv):
    _, S, _ = u.shape
    ts = _tile(S, 512)
    scale = XA_HEAD_DIM ** -0.5

    def body(q_ref, kv_ref, o_ref):
        q = q_ref[0]
        k = kv_ref[0].astype(BF16)
        v = kv_ref[1]
        o = jnp.zeros((ts, D_XA), F32)
        for m in _head_masks():
            s = _dot(jnp.where(m, q, 0.0).astype(BF16), k, NT) * scale
            s = s - jnp.max(s, axis=-1, keepdims=True)
            e = jnp.exp(s)
            p = e / jnp.sum(e, axis=-1, keepdims=True)
            o = o + _dot(p.astype(BF16), jnp.where(m, v, 0.0).astype(BF16), NN)
        o_ref[0] = o.astype(BF16)

    return pl.pallas_call(
        body, name="xattn_fwd", grid=(S // ts,),
        in_specs=[pl.BlockSpec((1, ts, GROUP), lambda s: (qg, s, 0)), pl.BlockSpec((2, N_MEM, GROUP), lambda s: (0, 0, 0))],
        out_specs=pl.BlockSpec((1, ts, GROUP), lambda s: (0, s, 0)),
        out_shape=SDS((1, S, GROUP), BF16),
        compiler_params=_params(("parallel",)),
    )(u, kv)


def xattn_bwd(u, qg, kv, dm, dg):
    _, S, _ = u.shape
    ts = _tile(S, 512)
    scale = XA_HEAD_DIM ** -0.5

    def body(q_ref, kv_ref, do_ref, dq_ref, dkv_ref):
        @pl.when(pl.program_id(0) == 0)
        def _():
            dkv_ref[...] = jnp.zeros_like(dkv_ref)

        q = q_ref[0]
        k = kv_ref[0]
        v = kv_ref[1]
        kb = k.astype(BF16)
        do = do_ref[0]
        dq = jnp.zeros((ts, D_XA), F32)
        dk = jnp.zeros((N_MEM, D_XA), F32)
        dv = jnp.zeros((N_MEM, D_XA), F32)
        for m in _head_masks():
            qm = jnp.where(m, q, 0.0).astype(BF16)
            s = _dot(qm, kb, NT) * scale
            s = s - jnp.max(s, axis=-1, keepdims=True)
            e = jnp.exp(s)
            p = e / jnp.sum(e, axis=-1, keepdims=True)
            dom = jnp.where(m, do, 0.0).astype(BF16)
            dp = _dot(dom, jnp.where(m, v, 0.0).astype(BF16), NT)
            ds = (p * (dp - jnp.sum(dp * p, axis=-1, keepdims=True)) * scale).astype(BF16)
            dq = dq + _dot(ds, jnp.where(m, k, 0.0).astype(BF16), NN)
            dk = dk + _dot(ds, qm, TN)
            dv = dv + _dot(p.astype(BF16), dom, TN)
        dq_ref[0] = dq.astype(BF16)
        dkv_ref[0] += dk
        dkv_ref[1] += dv

    return pl.pallas_call(
        body, name="xattn_bwd", grid=(S // ts,),
        in_specs=[pl.BlockSpec((1, ts, GROUP), lambda s: (qg, s, 0)), pl.BlockSpec((2, N_MEM, GROUP), lambda s: (0, 0, 0)),
                  pl.BlockSpec((1, ts, GROUP), lambda s: (dg, s, 0))],
        out_specs=[pl.BlockSpec((1, ts, GROUP), lambda s: (0, s, 0)), pl.BlockSpec((2, N_MEM, GROUP), lambda s: (0, 0, 0))],
        out_shape=[SDS((1, S, GROUP), BF16), SDS((2, N_MEM, GROUP), F32)],
        compiler_params=_params(("arbitrary",)),
    )(u, kv, dm)


ML_BLOCK_CHUNKS = 4
H4 = ML_HEADS
L = ML_CHUNK
NLANE = ML_HEAD_DIM


def _chunk_consts():
    r = lax.broadcasted_iota(jnp.int32, (1, L, L), 1)
    c = lax.broadcasted_iota(jnp.int32, (1, L, L), 2)
    return r >= c, r <= c, r == c


def _gate_cols(gb):
    lane = lax.broadcasted_iota(jnp.int32, gb.shape, 1)
    li = jnp.stack([jnp.sum(jnp.where(lane == h, gb, 0.0), axis=1, keepdims=True) for h in range(H4)])
    gf = jnp.stack([jnp.sum(jnp.where(lane == H4 + h, gb, 0.0), axis=1, keepdims=True) for h in range(H4)])
    return li, gf


def _log_sigmoid(x):
    return jnp.minimum(x, 0.0) - jnp.log(1.0 + jnp.exp(-jnp.abs(x)))


def _chunk_forward(q, k, v_aug, li_col, lf_col, c_prev, m_prev):
    tri, tri_t, eye = _chunk_consts()
    lf_row = jnp.sum(jnp.where(eye, lf_col, 0.0), axis=1, keepdims=True)
    li_row = jnp.sum(jnp.where(eye, li_col, 0.0), axis=1, keepdims=True)
    bcum_col = jnp.sum(jnp.where(tri, lf_row, 0.0), axis=2, keepdims=True)
    bcum_row = jnp.sum(jnp.where(tri_t, lf_col, 0.0), axis=1, keepdims=True)
    log_d = jnp.where(tri, bcum_col - bcum_row + li_row, NEG)
    log_inter = bcum_col + m_prev
    m_t = jnp.maximum(log_inter, jnp.max(log_d, axis=2, keepdims=True))
    w_intra = jnp.exp(log_d - m_t)
    w_inter = jnp.exp(log_inter - m_t)
    sc = _bdot(q, k, 2, 2) * w_intra
    qc = _bdot(q, c_prev, 2, 1)
    num = _bdot(sc, v_aug, 2, 1) + w_inter * qc
    lane = lax.broadcasted_iota(jnp.int32, num.shape, 2)
    den = jnp.sum(jnp.where(lane == NLANE, num, 0.0), axis=2, keepdims=True)
    e_m = jnp.exp(-m_t)
    b_last = jnp.sum(lf_row, axis=2, keepdims=True)
    log_w = b_last - bcum_col + li_col
    m_new = jnp.maximum(b_last + m_prev, jnp.max(log_w, axis=1, keepdims=True))
    w_k = jnp.exp(log_w - m_new)
    decay = jnp.exp(b_last + m_prev - m_new)
    return dict(w_intra=w_intra, w_inter=w_inter, sc=sc, qc=qc, num=num, den=den, e_m=e_m, lane=lane,
                w_k=w_k, decay=decay, m_new=m_new)


def mlstm_fwd(qk, u, bg):
    _, S, _ = qk.shape
    nc = S // L
    cb = min(ML_BLOCK_CHUNKS, nc)
    rows = cb * L
    kscale = ML_HEAD_DIM ** -0.5

    def body(qk_ref, v_ref, g_ref, bg_ref, h_ref, cst_ref, mst_ref, c_sc, m_sc):
        @pl.when(pl.program_id(0) == 0)
        def _():
            c_sc[...] = jnp.zeros_like(c_sc)
            m_sc[...] = jnp.zeros_like(m_sc)

        for c in range(cb):
            sl = pl.ds(c * L, L)
            q = qk_ref[0:H4, sl, :]
            k = qk_ref[H4:2 * H4, sl, :] * kscale
            v = v_ref[:, sl, :]
            lane = lax.broadcasted_iota(jnp.int32, v.shape, 2)
            v_aug = jnp.where(lane == NLANE, 1.0, v)
            li_col, gf = _gate_cols(g_ref[0, sl, :] + bg_ref[...])
            lf_col = _log_sigmoid(gf)
            c_prev = c_sc[...]
            m_prev = m_sc[...]
            f = _chunk_forward(q, k, v_aug, li_col, lf_col, c_prev, m_prev)
            r = 1.0 / jnp.maximum(jnp.abs(f["den"]), f["e_m"])
            h_ref[:, sl, :] = jnp.where(lane < NLANE, f["num"] * r, 0.0)
            cst_ref[c] = c_prev
            mst_ref[c] = jnp.broadcast_to(m_prev, (H4, 1, LANES))
            c_sc[...] = f["decay"] * c_prev + _bdot(k * f["w_k"], v_aug, 1, 1)
            m_sc[...] = f["m_new"]

    def hspec(blk):
        return pl.BlockSpec((H4, rows, GROUP), lambda i: (blk, i, 0))

    return pl.pallas_call(
        body, name="mlstm_fwd", grid=(nc // cb,),
        in_specs=[pl.BlockSpec((2 * H4, rows, GROUP), lambda i: (0, i, 0)), hspec(2),
                  pl.BlockSpec((1, rows, GROUP), lambda i: (17, i, 0)), pl.BlockSpec((1, GROUP), lambda i: (0, 0))],
        out_specs=[hspec(0), pl.BlockSpec((cb, H4, GROUP, GROUP), lambda i: (i, 0, 0, 0)),
                   pl.BlockSpec((cb, H4, 1, LANES), lambda i: (i, 0, 0, 0))],
        out_shape=[SDS((H4, S, GROUP), F32), SDS((nc, H4, GROUP, GROUP), F32), SDS((nc, H4, 1, LANES), F32)],
        scratch_shapes=[pltpu.VMEM((H4, GROUP, GROUP), F32), pltpu.VMEM((H4, 1, 1), F32)],
        compiler_params=_params(("arbitrary",)),
    )(qk, u, u, bg)


def mlstm_bwd(qk, u, bg, cst, mst, dh):
    _, S, _ = qk.shape
    nc = S // L
    cb = min(ML_BLOCK_CHUNKS, nc)
    rows = cb * L
    nb = nc // cb
    kscale = ML_HEAD_DIM ** -0.5

    def body(qk_ref, v_ref, g_ref, bg_ref, cst_ref, mst_ref, dh_ref, dqk_ref, dv_ref, dg_ref, dbg_ref, dc_sc):
        @pl.when(pl.program_id(0) == 0)
        def _():
            dc_sc[...] = jnp.zeros_like(dc_sc)
            dbg_ref[...] = jnp.zeros_like(dbg_ref)

        tri, tri_t, eye = _chunk_consts()
        for c in reversed(range(cb)):
            sl = pl.ds(c * L, L)
            q = qk_ref[0:H4, sl, :]
            k = qk_ref[H4:2 * H4, sl, :] * kscale
            v = v_ref[:, sl, :]
            lane = lax.broadcasted_iota(jnp.int32, v.shape, 2)
            v_aug = jnp.where(lane == NLANE, 1.0, v)
            li_col, gf = _gate_cols(g_ref[0, sl, :] + bg_ref[...])
            lf_col = _log_sigmoid(gf)
            c_prev = cst_ref[c]
            m_prev = mst_ref[c][:, :, 0:1]
            f = _chunk_forward(q, k, v_aug, li_col, lf_col, c_prev, m_prev)
            w_intra, w_inter, sc, num, den, e_m = f["w_intra"], f["w_inter"], f["sc"], f["num"], f["den"], f["e_m"]
            absd = jnp.abs(den)
            r = 1.0 / jnp.maximum(absd, e_m)
            dhv = dh_ref[:, sl, :]
            s1 = jnp.sum(jnp.where(lane < NLANE, dhv * num, 0.0), axis=2, keepdims=True)
            dden = jnp.where(absd > e_m, -s1 * r * r * jnp.sign(den), 0.0)
            dnum = jnp.where(lane == NLANE, dden, jnp.where(lane < NLANE, dhv * r, 0.0))
            dsc = _bdot(dnum, v_aug, 2, 2)
            dv = _bdot(sc, dnum, 1, 1)
            gmat = dsc * sc
            dqk = dsc * w_intra
            dq = _bdot(dqk, k, 2, 1) + w_inter * _bdot(dnum, c_prev, 2, 2)
            dk = _bdot(dqk, q, 1, 1)
            dc_prev = _bdot(q * w_inter, dnum, 1, 1)
            dlog_inter = jnp.sum(dnum * f["qc"], axis=2, keepdims=True) * w_inter
            dbcum_col = dlog_inter + jnp.sum(gmat, axis=2, keepdims=True)
            g_row = jnp.sum(gmat, axis=1, keepdims=True)
            dcn = dc_sc[...]
            w_k, decay = f["w_k"], f["decay"]
            kw = k * w_k
            dc_prev = dc_prev + decay * dcn
            db_last = jnp.sum(jnp.sum(dcn * c_prev, axis=2, keepdims=True), axis=1, keepdims=True) * decay
            dkw = _bdot(v_aug, dcn, 2, 2)
            dv = dv + _bdot(kw, dcn, 2, 1)
            dk = dk + dkw * w_k
            dlogw = jnp.sum(dkw * k, axis=2, keepdims=True) * w_k
            db_last = db_last + jnp.sum(dlogw, axis=1, keepdims=True)
            dbcum_col = dbcum_col - dlogw
            rowi = lax.broadcasted_iota(jnp.int32, (1, L, 1), 1)
            dbcum_col = dbcum_col + jnp.where(rowi == L - 1, db_last, 0.0)
            dbcum_row = jnp.sum(jnp.where(eye, dbcum_col, 0.0), axis=1, keepdims=True) - g_row
            dlf_col = jnp.sum(jnp.where(tri_t, dbcum_row, 0.0), axis=2, keepdims=True)
            dli_col = dlogw + jnp.sum(jnp.where(eye, g_row, 0.0), axis=2, keepdims=True)
            dgf_col = dlf_col * _sigmoid(-gf)
            lane_g = lax.broadcasted_iota(jnp.int32, (L, GROUP), 1)
            dg = jnp.zeros((L, GROUP), F32)
            for h in range(H4):
                dg = dg + jnp.where(lane_g == h, dli_col[h], 0.0) + jnp.where(lane_g == H4 + h, dgf_col[h], 0.0)
            dqk_ref[0:H4, sl, :] = dq
            dqk_ref[H4:2 * H4, sl, :] = dk * kscale
            dv_ref[:, sl, :] = jnp.where(lane < NLANE, dv, 0.0).astype(BF16)
            dg_ref[0, sl, :] = dg.astype(BF16)
            dbg_ref[...] += jnp.sum(dg, axis=0, keepdims=True)
            dc_sc[...] = dc_prev

    def hspec(blk):
        return pl.BlockSpec((H4, rows, GROUP), lambda i: (blk, nb - 1 - i, 0))

    gspec = pl.BlockSpec((1, rows, GROUP), lambda i: (17, nb - 1 - i, 0))
    qkspec = pl.BlockSpec((2 * H4, rows, GROUP), lambda i: (0, nb - 1 - i, 0))
    return pl.pallas_call(
        body, name="mlstm_bwd", grid=(nb,),
        in_specs=[qkspec, hspec(2), gspec, pl.BlockSpec((1, GROUP), lambda i: (0, 0)),
                  pl.BlockSpec((cb, H4, GROUP, GROUP), lambda i: (nb - 1 - i, 0, 0, 0)),
                  pl.BlockSpec((cb, H4, 1, LANES), lambda i: (nb - 1 - i, 0, 0, 0)), hspec(0)],
        out_specs=[qkspec, hspec(0), pl.BlockSpec((1, rows, GROUP), lambda i: (0, nb - 1 - i, 0)),
                   pl.BlockSpec((1, GROUP), lambda i: (0, 0))],
        out_shape=[SDS((2 * H4, S, GROUP), F32), SDS((H4, S, GROUP), BF16),
                   SDS((1, S, GROUP), BF16), SDS((1, GROUP), F32)],
        scratch_shapes=[pltpu.VMEM((H4, GROUP, GROUP), F32)],
        compiler_params=_params(("arbitrary",)),
    )(qk, u, u, bg, cst, mst, dh)


def head_norm_fwd(hm, u, hg):
    _, S, _ = hm.shape
    ts = _tile(S, 512)

    def body(h_ref, o_ref, g_ref, t_ref):
        h = h_ref[0]
        lane = lax.broadcasted_iota(jnp.int32, h.shape, 1)
        valid = lane < ML_HEAD_DIM
        mu = jnp.sum(h, axis=-1, keepdims=True) * (1.0 / ML_HEAD_DIM)
        hc = jnp.where(valid, h - mu, 0.0)
        var = jnp.sum(hc * hc, axis=-1, keepdims=True) * (1.0 / ML_HEAD_DIM)
        hn = hc * lax.rsqrt(var + LN_EPS) * g_ref[0]
        t_ref[0] = (_sigmoid(o_ref[0]) * hn).astype(BF16)

    return pl.pallas_call(
        body, name="head_norm_fwd", grid=(H4, S // ts),
        in_specs=[pl.BlockSpec((1, ts, GROUP), lambda h, s: (h, s, 0)), pl.BlockSpec((1, ts, GROUP), lambda h, s: (12 + h, s, 0)),
                  pl.BlockSpec((1, 1, GROUP), lambda h, s: (h, 0, 0))],
        out_specs=pl.BlockSpec((1, ts, GROUP), lambda h, s: (h, s, 0)),
        out_shape=SDS((H4, S, GROUP), BF16),
        compiler_params=_params(("parallel", "parallel")),
    )(hm, u, hg)


def head_norm_bwd(hm, u, hg, dm):
    _, S, _ = hm.shape
    ts = _tile(S, 512)

    def body(h_ref, o_ref, g_ref, d_ref, dh_ref, do_ref, dg_ref):
        @pl.when(pl.program_id(1) == 0)
        def _():
            dg_ref[...] = jnp.zeros_like(dg_ref)

        h = h_ref[0]
        lane = lax.broadcasted_iota(jnp.int32, h.shape, 1)
        valid = lane < ML_HEAD_DIM
        inv = 1.0 / ML_HEAD_DIM
        mu = jnp.sum(h, axis=-1, keepdims=True) * inv
        hc = jnp.where(valid, h - mu, 0.0)
        var = jnp.sum(hc * hc, axis=-1, keepdims=True) * inv
        rstd = lax.rsqrt(var + LN_EPS)
        xhat = hc * rstd
        g = g_ref[0]
        sig = _sigmoid(o_ref[0])
        dt = jnp.where(valid, d_ref[0], 0.0)
        do_ref[0] = (dt * xhat * g * sig * (1.0 - sig)).astype(BF16)
        dhn = dt * sig
        dg_ref[0] += jnp.sum(dhn * xhat, axis=0, keepdims=True)
        dxh = dhn * g
        m1 = jnp.sum(dxh, axis=-1, keepdims=True) * inv
        m2 = jnp.sum(dxh * xhat, axis=-1, keepdims=True) * inv
        dh_ref[0] = jnp.where(valid, rstd * (dxh - m1 - xhat * m2), 0.0)

    spec = pl.BlockSpec((1, ts, GROUP), lambda h, s: (h, s, 0))
    gspec = pl.BlockSpec((1, 1, GROUP), lambda h, s: (h, 0, 0))
    return pl.pallas_call(
        body, name="head_norm_bwd", grid=(H4, S // ts),
        in_specs=[spec, pl.BlockSpec((1, ts, GROUP), lambda h, s: (12 + h, s, 0)), gspec, spec],
        out_specs=[spec, spec, gspec],
        out_shape=[SDS((H4, S, GROUP), F32), SDS((H4, S, GROUP), BF16), SDS((H4, 1, GROUP), F32)],
        compiler_params=_params(("parallel", "arbitrary")),
    )(hm, u, hg, dm)


def adamw(w, g, m, v, name):
    R, C = w.shape
    tr = R if R <= 512 else next(d for d in (512, 256, 128, 64, 32, 16, 8) if R % d == 0)
    c1 = 1.0 / (1.0 - ADAM_B1 ** ADAM_STEP)
    c2 = 1.0 / (1.0 - ADAM_B2 ** ADAM_STEP)

    def body(w_ref, g_ref, m_ref, v_ref, d_ref, nm_ref, nv_ref):
        gv = g_ref[...]
        nm = ADAM_B1 * m_ref[...] + (1.0 - ADAM_B1) * gv
        nv = ADAM_B2 * v_ref[...] + (1.0 - ADAM_B2) * (gv * gv)
        d_ref[...] = -ADAM_LR * ((nm * c1) / (jnp.sqrt(nv * c2) + ADAM_EPS) + ADAM_WD * w_ref[...])
        nm_ref[...] = nm
        nv_ref[...] = nv

    spec = pl.BlockSpec((tr, C), lambda i: (i, 0))
    return pl.pallas_call(
        body, name=name, grid=(R // tr,),
        in_specs=[spec] * 4, out_specs=[spec] * 3,
        out_shape=[SDS((R, C), F32)] * 3,
        compiler_params=_params(("parallel",)),
    )(w, g, m, v)


HBM = pl.BlockSpec(memory_space=pl.ANY)
ROW_SPLIT = 4


def _position():
    x, y, c = lax.axis_index("x"), lax.axis_index("y"), lax.axis_index("c")
    return x, y, c, [(1 - x, y), (x, 1 - y), (1 - x, 1 - y)]


def _unique(items):
    arrays = []
    for a, _ in items:
        if not any(a is b for b in arrays):
            arrays.append(a)
    return arrays, [next(i for i, b in enumerate(arrays) if b is a) for a, _ in items]


def place_own(items, me):
    arrays, src_of = _unique(items)
    n = len(items)
    shapes = [a.shape[len(p):] for a, p in items]

    def body(me_ref, *refs):
        for t in range(n):
            refs[n + t][...] = jnp.zeros_like(refs[n + t])
            refs[n + t][me_ref[0]] = refs[t][(0,) * len(items[t][1])]

    in_specs, out_specs = [], []
    for (a, p), shp in zip(items, shapes):
        blk = shp[:-2] + (shp[-2] // ROW_SPLIT, shp[-1])
        lead = (0,) * (len(shp) - 2)
        in_specs.append(pl.BlockSpec((1,) * len(p) + blk, functools.partial(lambda r, me_ref, p, lead: p + lead + (r, 0), p=p, lead=lead)))
        out_specs.append(pl.BlockSpec((N_CHIPS,) + blk, functools.partial(lambda r, me_ref, lead: (0,) + lead + (r, 0), lead=lead)))
    return pl.pallas_call(
        body, name="place_own",
        grid_spec=pltpu.PrefetchScalarGridSpec(num_scalar_prefetch=1, grid=(ROW_SPLIT,), in_specs=in_specs, out_specs=out_specs),
        out_shape=[SDS((N_CHIPS,) + tuple(shp), a.dtype) for shp, (a, _) in zip(shapes, items)],
        compiler_params=_params(("parallel",)),
    )(me, *[arrays[i] for i in src_of])


SEM = pl.BlockSpec(memory_space=pltpu.SEMAPHORE)
IN_HBM = pl.BlockSpec(memory_space=pltpu.HBM)
DATAFLOW = pltpu.SideEffectType.DATAFLOW_SIDE_EFFECTING


def split_start(bufs, plan, n_copies, after, name):
    n = len(bufs)

    def body(*refs):
        send, recv, token = refs[n + 1], refs[n + 2], refs[-1]
        x, y, c, chips = _position()
        for k, (src, dst, dev) in enumerate(plan(refs[:n], x, y, c, chips)):
            pltpu.make_async_remote_copy(src_ref=src, dst_ref=dst, send_sem=send.at[k], recv_sem=recv.at[k],
                                         device_id=dev, device_id_type=MESH).start()
        token[...] = jnp.zeros_like(token)

    out = pl.pallas_call(
        body, name=name,
        out_shape=(pltpu.SemaphoreType.DMA((n_copies,)), pltpu.SemaphoreType.DMA((n_copies,)),
                   *[pltpu.HBM(b.shape, b.dtype) for b in bufs], SDS((8, LANES), F32)),
        in_specs=[IN_HBM] * n + [pl.BlockSpec(memory_space=pl.ANY)],
        out_specs=(SEM, SEM, *[IN_HBM] * n, pl.BlockSpec(memory_space=pltpu.VMEM)),
        input_output_aliases={i: 2 + i for i in range(n)},
        compiler_params=pltpu.CompilerParams(has_side_effects=DATAFLOW),
    )(*[pltpu.with_memory_space_constraint(b, pltpu.HBM) for b in bufs], after)
    return out[0], out[1], list(out[2:2 + n]), out[-1]


def split_wait(send, recv, bufs, plan, after, name):
    n = len(bufs)

    def body(*refs):
        send_ref, recv_ref = refs[n], refs[n + 1]
        x, y, c, chips = _position()
        for k, (src, dst, dev) in enumerate(plan(refs[:n], x, y, c, chips)):
            cp = pltpu.make_async_remote_copy(src_ref=src, dst_ref=dst, send_sem=send_ref.at[k], recv_sem=recv_ref.at[k],
                                              device_id=dev, device_id_type=MESH)
            cp.wait_send()
            cp.wait_recv()

    return list(pl.pallas_call(
        body, name=name, out_shape=tuple(pltpu.HBM(b.shape, b.dtype) for b in bufs),
        in_specs=[IN_HBM] * n + [SEM, SEM, pl.BlockSpec(memory_space=pl.ANY)], out_specs=tuple([IN_HBM] * n),
        input_output_aliases={i: i for i in range(n)},
        compiler_params=pltpu.CompilerParams(has_side_effects=DATAFLOW),
    )(*bufs, send, recv, after))


def _gather_plan(shapes, landing):
    n = len(shapes)

    def plan(refs, x, y, c, chips):
        out = []
        for t in range(n):
            half = shapes[t][0] // 2
            rows = pl.ds(c * half, half)
            for cx, cy in chips:
                slot = 2 * cx + cy if landing else 2 * x + y
                out.append((refs[t].at[rows], refs[n + t].at[slot, rows], (cx, cy, c)))
        return out

    return plan


def gather_start(shards, placed, after, name):
    shapes = [s.shape for s in shards]
    send, recv, bufs, token = split_start(shards + placed, _gather_plan(shapes, False), 3 * len(shards), after, name)
    return (send, recv, bufs, shapes), token


def gather_wait(state, after, name):
    send, recv, bufs, shapes = state
    return split_wait(send, recv, bufs, _gather_plan(shapes, True), after, name)[len(shapes):]


def gather_pass_on(placed, shapes, name):
    n = len(placed)

    def body(*refs):
        outs, send, recv = refs[n:2 * n], refs[2 * n], refs[2 * n + 1]
        x, y, c, chips = _position()
        cps = []
        for t in range(n):
            half = shapes[t][0] // 2
            for j, (cx, cy) in enumerate(chips):
                piece = outs[t].at[2 * cx + cy, pl.ds(c * half, half)]
                cp = pltpu.make_async_remote_copy(src_ref=piece, dst_ref=piece, send_sem=send.at[3 * t + j], recv_sem=recv.at[3 * t + j],
                                                  device_id=(x, y, 1 - c), device_id_type=MESH)
                cp.start()
                cps.append(cp)
        for t in range(n):
            half = shapes[t][0] // 2
            for j, (cx, cy) in enumerate(chips):
                piece = outs[t].at[2 * cx + cy, pl.ds((1 - c) * half, half)]
                pltpu.make_async_remote_copy(src_ref=piece, dst_ref=piece, send_sem=send.at[3 * t + j], recv_sem=recv.at[3 * t + j],
                                             device_id=(x, y, 1 - c), device_id_type=MESH).wait_recv()
        for cp in cps:
            cp.wait_send()

    return pl.pallas_call(
        body, name=name,
        in_specs=[HBM] * n, out_specs=[HBM] * n,
        out_shape=[SDS(p.shape, p.dtype) for p in placed],
        input_output_aliases={t: t for t in range(n)},
        scratch_shapes=[pltpu.SemaphoreType.DMA((3 * n,))] * 2,
    )(*placed)


def _flip(k, x, y, c):
    return ((1 - x) if k & 4 else x, (1 - y) if k & 2 else y, (1 - c) if k & 1 else c)


def small_allgather(v, reduce):
    R, C = v.shape

    def body(v_ref, o_ref, *scratch):
        if reduce:
            buf, send, recv = scratch
        else:
            buf, (send, recv) = o_ref, scratch
        x, y, c, _ = _position()
        me = 4 * x + 2 * y + c
        buf[me] = v_ref[...]
        sends = []
        for k in range(1, N_DEV):
            cp = pltpu.make_async_remote_copy(src_ref=v_ref, dst_ref=buf.at[me], send_sem=send.at[k - 1], recv_sem=recv.at[k - 1],
                                              device_id=_flip(k, x, y, c), device_id_type=MESH)
            cp.start()
            sends.append(cp)
        for k in range(1, N_DEV):
            px, py, pc = _flip(k, x, y, c)
            pltpu.make_async_remote_copy(src_ref=v_ref, dst_ref=buf.at[4 * px + 2 * py + pc], send_sem=send.at[k - 1],
                                         recv_sem=recv.at[k - 1], device_id=(px, py, pc), device_id_type=MESH).wait_recv()
        for cp in sends:
            cp.wait_send()
        if reduce:
            acc = buf[0]
            for i in range(1, N_DEV):
                acc = acc + buf[i]
            o_ref[...] = acc

    vm = pl.BlockSpec(memory_space=pltpu.VMEM)
    sems = [pltpu.SemaphoreType.DMA((N_DEV - 1,)), pltpu.SemaphoreType.DMA((N_DEV - 1,))]
    return pl.pallas_call(
        body, name="small_allreduce" if reduce else "small_allgather",
        in_specs=[vm], out_specs=vm,
        out_shape=SDS((R, C) if reduce else (N_DEV, R, C), F32),
        scratch_shapes=([pltpu.VMEM((N_DEV, R, C), F32)] if reduce else []) + sems,
    )(v)


def rs_exchange_sibling(gs):
    n = len(gs)

    def body(*refs):
        ins, outs, send, recv = refs[:n], refs[n:2 * n], refs[2 * n], refs[2 * n + 1]
        x, y, c, _ = _position()
        cps = []
        for t in range(n):
            cp = pltpu.make_async_remote_copy(src_ref=ins[t].at[:, 1 - c], dst_ref=outs[t], send_sem=send.at[t], recv_sem=recv.at[t],
                                              device_id=(x, y, 1 - c), device_id_type=MESH)
            cp.start()
            cps.append(cp)
        for cp in cps:
            cp.wait()

    return pl.pallas_call(
        body, name="rs_exchange_sibling", in_specs=[HBM] * n, out_specs=[HBM] * n,
        out_shape=[SDS((g.shape[0],) + g.shape[2:], g.dtype) for g in gs],
        scratch_shapes=[pltpu.SemaphoreType.DMA((n,)), pltpu.SemaphoreType.DMA((n,))],
    )(*gs)


def rs_pair_add(gs, rs, c):
    n = len(gs)

    def body(c_ref, *refs):
        for t in range(n):
            refs[2 * n + t][0] = (refs[t][0, 0].astype(F32) + refs[n + t][0].astype(F32)).astype(BF16)

    in_specs, out_specs, out_shape = [], [], []
    for g in gs:
        _, _, h, C = g.shape
        in_specs.append(pl.BlockSpec((1, 1, h // ROW_SPLIT, C), lambda j, r, c_ref: (j, c_ref[0], r, 0)))
    for g in gs:
        _, _, h, C = g.shape
        spec = pl.BlockSpec((1, h // ROW_SPLIT, C), lambda j, r, c_ref: (j, r, 0))
        in_specs.append(spec)
        out_specs.append(spec)
        out_shape.append(SDS((N_CHIPS, h, C), BF16))
    return pl.pallas_call(
        body, name="rs_pair_add",
        grid_spec=pltpu.PrefetchScalarGridSpec(num_scalar_prefetch=1, grid=(N_CHIPS, ROW_SPLIT), in_specs=in_specs, out_specs=out_specs),
        out_shape=out_shape, compiler_params=_params(("parallel", "parallel")),
    )(c, *gs, *rs)


def _rs_plan(n):
    def plan(refs, x, y, c, chips):
        return [(refs[t].at[2 * cx + cy], refs[n + t].at[j], (cx, cy, c)) for t in range(n) for j, (cx, cy) in enumerate(chips)]

    return plan


def rs_chip_add(ps, qs, me_c):
    n = len(ps)

    def body(me_ref, *refs):
        for t in range(n):
            q = refs[n + t]
            refs[2 * n + t][...] = jnp.zeros_like(refs[2 * n + t])
            refs[2 * n + t][me_ref[1]] = ((refs[t][0].astype(F32) + q[0].astype(F32)) + q[1].astype(F32)) + q[2].astype(F32)

    in_specs, out_specs, out_shape = [], [], []
    for p in ps:
        _, h, C = p.shape
        in_specs.append(pl.BlockSpec((1, h // ROW_SPLIT, C), lambda r, me_ref: (me_ref[0], r, 0)))
    for p in ps:
        _, h, C = p.shape
        in_specs.append(pl.BlockSpec((3, h // ROW_SPLIT, C), lambda r, me_ref: (0, r, 0)))
        out_specs.append(pl.BlockSpec((2, h // ROW_SPLIT, C), lambda r, me_ref: (0, r, 0)))
        out_shape.append(SDS((2, h, C), F32))
    return pl.pallas_call(
        body, name="rs_chip_add",
        grid_spec=pltpu.PrefetchScalarGridSpec(num_scalar_prefetch=1, grid=(ROW_SPLIT,), in_specs=in_specs, out_specs=out_specs),
        out_shape=out_shape, compiler_params=_params(("parallel",)),
    )(me_c, *ps, *qs)


def rs_share(rs):
    n = len(rs)

    def body(*refs):
        outs, send, recv = refs[n:2 * n], refs[2 * n], refs[2 * n + 1]
        x, y, c, _ = _position()
        cps = []
        for t in range(n):
            cp = pltpu.make_async_remote_copy(src_ref=outs[t].at[c], dst_ref=outs[t].at[c], send_sem=send.at[t], recv_sem=recv.at[t],
                                              device_id=(x, y, 1 - c), device_id_type=MESH)
            cp.start()
            cps.append(cp)
        for cp in cps:
            cp.wait()

    return pl.pallas_call(
        body, name="rs_share", in_specs=[HBM] * n, out_specs=[HBM] * n,
        out_shape=[SDS(r.shape, r.dtype) for r in rs],
        input_output_aliases={t: t for t in range(n)},
        scratch_shapes=[pltpu.SemaphoreType.DMA((n,))] * 2,
    )(*rs)


def rs_begin(gs, name):
    c = lax.axis_index("c")
    n = len(gs)
    g5 = [g.reshape(N_CHIPS, 2, g.shape[1] // 2, g.shape[2]) for g in gs]
    from_sibling = rs_exchange_sibling(g5)
    pair = rs_pair_add(g5, from_sibling, jnp.reshape(c, (1,)).astype(jnp.int32))
    lands = [jnp.zeros((3,) + p.shape[1:], p.dtype) for p in pair]
    send, recv, bufs, token = split_start(list(pair) + lands, _rs_plan(n), 3 * n, from_sibling[0], name)
    return (send, recv, bufs, [g.shape for g in gs]), token


def rs_end(state, after, name):
    x, y, c = lax.axis_index("x"), lax.axis_index("y"), lax.axis_index("c")
    send, recv, bufs, shapes = state
    n = len(shapes)
    bufs = split_wait(send, recv, bufs, _rs_plan(n), after, name)
    half = rs_chip_add(bufs[:n], bufs[n:], jnp.stack([2 * x + y, c]).astype(jnp.int32))
    both = rs_share(half)
    return [b.reshape(s[1], s[2]) for b, s in zip(both, shapes)]


def _pad_last(a, n):
    return jnp.pad(a, [(0, 0)] * (a.ndim - 1) + [(0, n - a.shape[-1])])


def _heads_to_groups(w):
    k = w.shape[0]
    return _pad_last(w.reshape(k, ML_HEADS, ML_HEAD_DIM).transpose(1, 0, 2), GROUP)


def _groups_to_heads(g):
    return g[:, :, :ML_HEAD_DIM].transpose(1, 0, 2).reshape(g.shape[1], D_TOK)


def _cols_to_groups(w):
    k, n = w.shape
    return w.reshape(k, n // GROUP, GROUP).transpose(1, 0, 2)


def _groups_to_cols(g):
    n, k, _ = g.shape
    return g.transpose(1, 0, 2).reshape(k, n * GROUP)


def _chips_to_cols(a):
    return a.transpose(1, 0, 2).reshape(a.shape[1], -1)


def _cols_to_chips(w):
    k, n = w.shape
    return w.reshape(k, N_CHIPS, n // N_CHIPS).transpose(1, 0, 2)


def _mlstm_in_groups(w):
    parts = [_heads_to_groups(w[:, i * D_TOK:(i + 1) * D_TOK]) for i in range(4)]
    gates = _pad_last(w[:, 4 * D_TOK:4 * D_TOK + 2 * ML_HEADS], GROUP)[None]
    qmem = w[:, 4 * D_TOK + 2 * ML_HEADS:][None]
    return jnp.concatenate(parts + [qmem, gates], axis=0)


def _mlstm_in_ungroup(g):
    parts = [_groups_to_heads(g[4 * i:4 * i + 4]) for i in range(4)]
    return jnp.concatenate(parts + [g[17][:, :2 * ML_HEADS], g[16]], axis=1)


def _taps_to_groups(w, width):
    taps = w.shape[0]
    g = _pad_last(w.reshape(taps, -1, width), GROUP).transpose(1, 0, 2)
    return jnp.pad(g, ((0, 0), (0, 8 - taps), (0, 0)))


def _groups_to_taps(g, taps, width):
    return g[:, :taps, :width].transpose(1, 0, 2).reshape(taps, -1)


SMALL_ROWS = 24
SMALL_IN_COLS = 384
SMALL_OUT_COLS = 1536


class _Gathered:
    def __init__(self, srcs, groups, me):
        keys = [k for g in groups for k in g]
        placed = dict(zip(keys, place_own([(srcs[k], ()) for k in keys], me)))
        self.groups, self.states, self.ready = groups, [], {}
        self.group_of = {k: gi for gi, g in enumerate(groups) for k in g}
        token = me
        for gi, g in enumerate(groups):
            state, token = gather_start([srcs[k] for k in g], [placed[k] for k in g], token, f"gather_start_{gi}")
            self.states.append(state)
        self.started = token

    def _get(self, key, after):
        gi = self.group_of[key]
        if gi not in self.ready:
            got = gather_wait(self.states[gi], after if gi else self.started, f"gather_wait_{gi}")
            self.ready[gi] = dict(zip(self.groups[gi], gather_pass_on(got, self.states[gi][3], f"gather_pass_on_{gi}")))
        return self.ready[gi][key]

    def ffn(self, l, i, after):
        return tuple(self._get((n, l, i), after) for n in ("wg", "wu", "wd"))

    def mixer(self, l, after):
        win = _chips_to_cols(self._get(("win", l), after))
        win = _cols_to_groups(win) if l % 2 == 0 else _mlstm_in_groups(win)
        wkv = _cols_to_groups(self._get(("wkv", l), after).reshape(D_MODEL, 2 * D_XA))
        wout = self._get(("wout", l), after)
        if l % 2:
            wout = wout.reshape(D_MODEL, D_MODEL)
            tok = jnp.pad(wout[:D_TOK].reshape(ML_HEADS, ML_HEAD_DIM, D_MODEL), ((0, 0), (0, GROUP - ML_HEAD_DIM), (0, 0)))
            wout = jnp.concatenate([tok, wout[D_TOK:][None]], axis=0)
        return win, wkv, wout


class _GradSink:
    def __init__(self):
        self.queue, self.done, self.count = [], {}, 0

    @staticmethod
    def _by_chip(key, g):
        if key[0] == "wkv":
            return _groups_to_cols(g).reshape(N_CHIPS, D_MODEL // N_CHIPS, 2 * D_XA)
        if key[0] == "win":
            return _cols_to_chips(_groups_to_cols(g) if key[1] % 2 == 0 else _mlstm_in_ungroup(g))
        if key[0] == "wout" and key[1] % 2:
            full = jnp.concatenate([g[:ML_HEADS, :ML_HEAD_DIM].reshape(D_TOK, D_MODEL), g[ML_HEADS]], axis=0)
            return full.reshape(N_CHIPS, D_MODEL // N_CHIPS, D_MODEL)
        return g

    def begin(self, grads):
        keys = list(grads)
        state, token = rs_begin([self._by_chip(k, grads[k]) for k in keys], f"rs_start_{self.count}")
        self.queue.append((keys, state, self.count))
        self.count += 1
        return token

    def end(self, after):
        keys, state, i = self.queue.pop(0)
        self.done.update(zip(keys, rs_end(state, after, f"rs_wait_{i}")))


def _local_step(x, mem, tgt, P, weights, sink):
    memb = mem.astype(BF16)
    saved = []
    X, Xb = x, x.astype(BF16)
    after = Xb
    for l in range(DEPTH):
        s = {}
        s["x0b"] = Xb
        s["wa"] = weights.ffn(l, 0, after)
        s["g1a"], s["u1a"], s["ha"], s["z1"], X1, X1b = ffn_fwd(Xb, X, *s["wa"], P["ln_g"][l][0], P["ln_b"][l][0])
        s["x1b"] = X1b
        s["wm"] = win, wkv, wout = weights.mixer(l, X1b)
        u = proj(X1b, win, "mixer_in")
        kv = proj(memb, wkv, "mem_kv")
        s["u"], s["kv"] = u, kv
        if l % 2 == 0:
            tok = conv_mixer_fwd(u, P["convw"])
            qg = 9
        else:
            s["qk"] = qk_conv_fwd(u, P["qkw"])
            s["hm"], s["cst"], s["mst"] = mlstm_fwd(s["qk"], u, P["bg"])
            tok = head_norm_fwd(s["hm"], u, P["hg"])
            qg = 16
        xa = xattn_fwd(u, qg, kv)
        s["m"] = jnp.concatenate([tok, xa], axis=0)
        s["z2"], X2, X2b = contract_ln(s["m"], wout, X1, P["ln_g"][l][1], P["ln_b"][l][1], 1.0, "mixer_out_ln")
        s["x2b"] = X2b
        s["wb"] = weights.ffn(l, 1, X2b)
        s["g1b"], s["u1b"], s["hb"], s["z3"], X, Xb = ffn_fwd(X2b, X2, *s["wb"], P["ln_g"][l][2], P["ln_b"][l][2])
        after = Xb
        saved.append(s)

    loss, dX = loss_grad(X, tgt)

    G = {"ln_g": [[None] * 3 for _ in range(DEPTH)], "ln_b": [[None] * 3 for _ in range(DEPTH)]}
    pin = [jnp.zeros((1, 1), F32)]

    def ffn_backward(l, i, dX, z, xinb, g1, u1, h, w):
        k = 2 * i
        dz, dyb, G["ln_g"][l][k], G["ln_b"][l][k] = ln_bwd(dX, z, P["ln_g"][l][k] + pin[0], 0.5, "ffn_ln_bwd")
        dgb, dub, dx = ffn_bwd(dyb, dz, w[2], w[0], w[1], g1, u1)
        grads = {("wd", l, i): wgrad(h, dyb, BF16, "wgrad_down"), ("wg", l, i): wgrad(xinb, dgb, BF16, "wgrad_gate"),
                 ("wu", l, i): wgrad(xinb, dub, BF16, "wgrad_up")}
        return dx, grads

    pending = 0
    for l in reversed(range(DEPTH)):
        s = saved[l]
        win, wkv, wout = s["wm"]
        dX, grads = ffn_backward(l, 1, dX, s["z3"], s["x2b"], s["g1b"], s["u1b"], s["hb"], s["wb"])
        if pending:
            sink.end(dX)
        dz2, dz2b, G["ln_g"][l][1], G["ln_b"][l][1] = ln_bwd(dX, s["z2"], P["ln_g"][l][1], 1.0, "mixer_ln_bwd")
        dm = proj_t(dz2b, wout, "mixer_out_bwd")
        grads[("wout", l)] = wgrad(s["m"], dz2b, BF16, "wgrad_out")
        u, kv = s["u"], s["kv"]
        if l % 2 == 0:
            db, dc, dxi, G["convw"] = conv_mixer_bwd(u, P["convw"], dm)
            dq, dkv = xattn_bwd(u, 9, kv, dm, 3)
            du = jnp.concatenate([db, dc, dxi, dq], axis=0)
        else:
            dh, do, G["hg"] = head_norm_bwd(s["hm"], u, P["hg"], dm)
            dqk, dv, dgate, G["bg"] = mlstm_bwd(s["qk"], u, P["bg"], s["cst"], s["mst"], dh)
            duqk, G["qkw"] = qk_conv_bwd(u, P["qkw"], dqk)
            dq, dkv = xattn_bwd(u, 16, kv, dm, 4)
            du = jnp.concatenate([duqk, dv, do, dq, dgate], axis=0)
        grads[("win", l)] = wgrad(s["x1b"], du, BF16, "wgrad_in")
        grads[("wkv", l)] = wgrad(memb, dkv.astype(BF16), BF16, "wgrad_kv")
        dX = contract_t(du, win, dz2, "mixer_in_bwd")
        pin[0] = sink.begin(grads)[0:1, 0:1]
        dX, grads = ffn_backward(l, 0, dX, s["z1"], s["x0b"], s["g1a"], s["u1a"], s["ha"], s["wa"])
        sink.end(dX)
        pin[0] = sink.begin(grads)[0:1, 0:1]
        pending = 1
    sink.end(dX)
    return loss, dX, G


def kernel(x, mem, ln_g, ln_b, ffn_w_gate, ffn_w_up, ffn_w_down, w_kv_mem, w_out, w_in_conv, conv_w, w_in_mlstm, b_gates, qk_conv_w, head_norm_g, loss_target, m_ln_g, m_ln_b, m_ffn_w_gate, m_ffn_w_up, m_ffn_w_down, m_w_kv_mem, m_w_out, m_w_in_conv, m_conv_w, m_w_in_mlstm, m_b_gates, m_qk_conv_w, m_head_norm_g, v_ln_g, v_ln_b, v_ffn_w_gate, v_ffn_w_up, v_ffn_w_down, v_w_kv_mem, v_w_out, v_w_in_conv, v_conv_w, v_w_in_mlstm, v_b_gates, v_qk_conv_w, v_head_norm_g):
    cx, cy = lax.axis_index("x"), lax.axis_index("y")
    chip = 2 * cx + cy

    srcs = {}
    for l in range(DEPTH):
        for i in range(2):
            srcs[("wg", l, i)] = ffn_w_gate[l, i].astype(BF16)
            srcs[("wu", l, i)] = ffn_w_up[l, i].astype(BF16)
            srcs[("wd", l, i)] = ffn_w_down[l, i].astype(BF16)
        srcs[("wkv", l)] = w_kv_mem[l].astype(BF16)
        srcs[("wout", l)] = w_out[l].astype(BF16)
    srcs[("win", 0)] = w_in_conv[0].astype(BF16)
    srcs[("win", 1)] = w_in_mlstm[0].astype(BF16)
    ffn_keys = lambda l, i: [("wg", l, i), ("wu", l, i), ("wd", l, i)]
    mixer_keys = lambda l: [("win", l), ("wkv", l), ("wout", l)]
    groups = [ffn_keys(0, 0), mixer_keys(0) + mixer_keys(1) + ffn_keys(0, 1), ffn_keys(1, 0), ffn_keys(1, 1)]
    gathered = _Gathered(srcs, groups, jnp.reshape(chip, (1,)).astype(jnp.int32))

    small = jnp.zeros((SMALL_ROWS, SMALL_IN_COLS), F32)
    small = small.at[0:6, 0:256].set(ln_g.reshape(6, 256)).at[6:12, 0:256].set(ln_b.reshape(6, 256))
    small = small.at[12:15, 0:192].set(conv_w[0]).at[16:20, 0:384].set(qk_conv_w[0])
    smalls = small_allgather(small, reduce=False)[0::2]
    ln_g_full = _chips_to_cols(smalls[:, 0:6, 0:256]).reshape(DEPTH, 3, 1, D_MODEL)
    ln_b_full = _chips_to_cols(smalls[:, 6:12, 0:256]).reshape(DEPTH, 3, 1, D_MODEL)
    conv_w_full = _chips_to_cols(smalls[:, 12:15, 0:192])
    qk_w_full = _chips_to_cols(smalls[:, 16:20, 0:384])

    P = {"ln_g": ln_g_full, "ln_b": ln_b_full, "convw": _taps_to_groups(conv_w_full, GROUP),
         "qkw": _taps_to_groups(qk_w_full, ML_HEAD_DIM), "bg": _pad_last(b_gates, GROUP),
         "hg": _pad_last(head_norm_g[0], GROUP)[:, None, :]}

    sink = _GradSink()
    loss, grad_x, G = _local_step(x[0], mem[0], loss_target[0], P, gathered, sink)
    red = sink.done

    sg = jnp.zeros((SMALL_ROWS, SMALL_OUT_COLS), F32)
    dln_g = jnp.concatenate([G["ln_g"][l][k] for l in range(DEPTH) for k in range(3)], axis=0)
    dln_b = jnp.concatenate([G["ln_b"][l][k] for l in range(DEPTH) for k in range(3)], axis=0)
    sg = sg.at[0:6, 0:D_MODEL].set(dln_g).at[6:12, 0:D_MODEL].set(dln_b)
    sg = sg.at[12:15, 0:D_TOK].set(_groups_to_taps(G["convw"], 3, GROUP))
    sg = sg.at[15:16, 0:8].set(G["bg"][:, 0:8]).at[15:16, 8:9].set(loss)
    sg = sg.at[16:20, 0:2 * D_TOK].set(_groups_to_taps(G["qkw"], 4, ML_HEAD_DIM))
    sg = sg.at[20:24, 0:ML_HEAD_DIM].set(G["hg"][:, 0, :ML_HEAD_DIM])
    tot = small_allgather(sg, reduce=True)

    grads = {
        "ln_g": lax.dynamic_slice(tot[0:6, 0:D_MODEL], (0, chip * 256), (6, 256)).reshape(DEPTH, 3, 256),
        "ln_b": lax.dynamic_slice(tot[6:12, 0:D_MODEL], (0, chip * 256), (6, 256)).reshape(DEPTH, 3, 256),
        "ffn_w_gate": jnp.stack([jnp.stack([red[("wg", l, i)] for i in range(2)]) for l in range(DEPTH)]),
        "ffn_w_up": jnp.stack([jnp.stack([red[("wu", l, i)] for i in range(2)]) for l in range(DEPTH)]),
        "ffn_w_down": jnp.stack([jnp.stack([red[("wd", l, i)] for i in range(2)]) for l in range(DEPTH)]),
        "w_kv_mem": jnp.stack([red[("wkv", l)] for l in range(DEPTH)]),
        "w_out": jnp.stack([red[("wout", l)] for l in range(DEPTH)]),
        "w_in_conv": red[("win", 0)][None],
        "conv_w": lax.dynamic_slice(tot[12:15, 0:D_TOK], (0, chip * 192), (3, 192))[None],
        "w_in_mlstm": red[("win", 1)][None],
        "b_gates": tot[15:16, 0:8],
        "qk_conv_w": lax.dynamic_slice(tot[16:20, 0:2 * D_TOK], (0, chip * 384), (4, 384))[None],
        "head_norm_g": tot[20:24, 0:ML_HEAD_DIM][None],
    }
    loss_total = tot[15, 8]

    weights = {"ln_g": ln_g, "ln_b": ln_b, "ffn_w_gate": ffn_w_gate, "ffn_w_up": ffn_w_up, "ffn_w_down": ffn_w_down,
               "w_kv_mem": w_kv_mem, "w_out": w_out, "w_in_conv": w_in_conv, "conv_w": conv_w, "w_in_mlstm": w_in_mlstm,
               "b_gates": b_gates, "qk_conv_w": qk_conv_w, "head_norm_g": head_norm_g}
    ms = {"ln_g": m_ln_g, "ln_b": m_ln_b, "ffn_w_gate": m_ffn_w_gate, "ffn_w_up": m_ffn_w_up, "ffn_w_down": m_ffn_w_down,
          "w_kv_mem": m_w_kv_mem, "w_out": m_w_out, "w_in_conv": m_w_in_conv, "conv_w": m_conv_w, "w_in_mlstm": m_w_in_mlstm,
          "b_gates": m_b_gates, "qk_conv_w": m_qk_conv_w, "head_norm_g": m_head_norm_g}
    vs = {"ln_g": v_ln_g, "ln_b": v_ln_b, "ffn_w_gate": v_ffn_w_gate, "ffn_w_up": v_ffn_w_up, "ffn_w_down": v_ffn_w_down,
          "w_kv_mem": v_w_kv_mem, "w_out": v_w_out, "w_in_conv": v_w_in_conv, "conv_w": v_conv_w, "w_in_mlstm": v_w_in_mlstm,
          "b_gates": v_b_gates, "qk_conv_w": v_qk_conv_w, "head_norm_g": v_head_norm_g}
    names = list(weights)
    deltas, new_m, new_v = [], [], []
    for nme in names:
        w = weights[nme]
        shp = w.shape
        two = (math.prod(shp[:-1]), shp[-1])
        d, nm, nv = adamw(w.reshape(two), grads[nme].reshape(two), ms[nme].reshape(two), vs[nme].reshape(two), "adamw_" + nme)
        deltas.append(d.reshape(shp))
        new_m.append(nm.reshape(shp))
        new_v.append(nv.reshape(shp))
    return (loss_total, grad_x[None], *[grads[nme] for nme in names], *deltas, *new_m, *new_v)
```

```python
import functools
import math

import jax
import jax.numpy as jnp
from jax import lax
from jax.experimental import pallas as pl
from jax.experimental.pallas import tpu as pltpu

F32 = jnp.float32
BF16 = jnp.bfloat16
SDS = jax.ShapeDtypeStruct

D_MODEL = 1024
DEPTH = 2
N_MEM = 256
XA_HEADS = 4
XA_HEAD_DIM = 64
D_XA = 256
D_TOK = 768
ML_HEADS = 4
ML_HEAD_DIM = 192
ML_CHUNK = 64
D_FF = 2816
LN_EPS = 1e-5
ALPHA = (2.0 * DEPTH) ** 0.25
N_CHIPS = 4
N_DEV = 8
FF_SHARD = D_FF // N_CHIPS
GROUP = 256
NEG = -1e30

ADAM_LR = 0.001
ADAM_B1 = 0.9
ADAM_B2 = 0.999
ADAM_EPS = 1e-08
ADAM_WD = 0.01
ADAM_STEP = 10

VMEM_LIMIT = 56 * 1024 * 1024

NN = ((1,), (0,))
NT = ((1,), (1,))
TN = ((0,), (0,))
MESH = pl.DeviceIdType.MESH


def _dot(a, b, dims):
    return lax.dot_general(a, b, (dims, ((), ())), preferred_element_type=F32)


def _bdot(a, b, ca, cb):
    dims = (((ca,), (cb,)), ((0,), (0,)))
    ah, bh = a.astype(BF16), b.astype(BF16)
    al, bl = (a - ah.astype(F32)).astype(BF16), (b - bh.astype(F32)).astype(BF16)
    dot = functools.partial(lax.dot_general, dimension_numbers=dims, preferred_element_type=F32)
    return dot(ah, bh) + dot(al, bh) + dot(ah, bl)


def _sigmoid(x):
    return 1.0 / (1.0 + jnp.exp(-x))


def _params(sem, vmem=VMEM_LIMIT):
    return pltpu.CompilerParams(dimension_semantics=sem, vmem_limit_bytes=vmem)


def _tile(n, want):
    t = min(n, want)
    assert n % t == 0, (n, t)
    return t


def _layer_norm(z, gamma, beta):
    mu = jnp.mean(z, axis=-1, keepdims=True)
    zc = z - mu
    var = jnp.mean(zc * zc, axis=-1, keepdims=True)
    return zc * lax.rsqrt(var + LN_EPS) * gamma + beta


def _resident(shape):
    return pl.BlockSpec(shape, lambda *_: (0,) * len(shape), pipeline_mode=pl.Buffered(1))


def _group_block(G, want):
    return max(d for d in range(1, max(1, min(G, want)) + 1) if G % d == 0)


def ffn_fwd(xb, x, wg, wu, wd, gamma, beta):
    S, K = xb.shape
    G, _, N = wg.shape
    ts = _tile(S, 512)

    def body(xb_ref, x_ref, wg_ref, wu_ref, wd_ref, gm_ref, bt_ref, g_ref, u_ref, h_ref, z_ref, xn_ref, xnb_ref):
        j = pl.program_id(1)
        xv = xb_ref[...]
        g = _dot(xv, wg_ref[j], NN)
        u = _dot(xv, wu_ref[j], NN)
        h = (g * _sigmoid(g) * u).astype(BF16)
        g_ref[0] = g.astype(BF16)
        u_ref[0] = u.astype(BF16)
        h_ref[0] = h
        y = _dot(h, wd_ref[j], NN)

        @pl.when(j == 0)
        def _():
            z_ref[...] = y

        @pl.when(j > 0)
        def _():
            z_ref[...] += y

        @pl.when(j == G - 1)
        def _():
            z = ALPHA * x_ref[...] + 0.5 * z_ref[...]
            xn = _layer_norm(z, gm_ref[...], bt_ref[...])
            z_ref[...] = z
            xn_ref[...] = xn
            xnb_ref[...] = xn.astype(BF16)

    row = pl.BlockSpec((ts, K), lambda s, j: (s, 0))
    vec = pl.BlockSpec((1, K), lambda s, j: (0, 0))
    wspec = _resident((G, K, N))
    ospec = pl.BlockSpec((1, ts, N), lambda s, j: (j, s, 0))
    return pl.pallas_call(
        body, name="ffn_fwd", grid=(S // ts, G),
        in_specs=[row, row, wspec, wspec, _resident((G, N, K)), vec, vec],
        out_specs=[ospec, ospec, ospec, row, row, row],
        out_shape=[SDS((G, S, N), BF16), SDS((G, S, N), BF16), SDS((G, S, N), BF16),
                   SDS((S, K), F32), SDS((S, K), F32), SDS((S, K), BF16)],
        compiler_params=_params(("parallel", "arbitrary")),
    )(xb, x, wg, wu, wd, gamma, beta)


def proj(xb, w, name):
    S, K = xb.shape
    G, _, N = w.shape
    ts = _tile(S, 1024)
    gb = _group_block(G, 6)

    def body(x_ref, w_ref, y_ref):
        xv = x_ref[...]
        for j in range(gb):
            y_ref[j] = _dot(xv, w_ref[j], NN)

    return pl.pallas_call(
        body, name=name, grid=(S // ts, G // gb),
        in_specs=[pl.BlockSpec((ts, K), lambda s, g: (s, 0)), pl.BlockSpec((gb, K, N), lambda s, g: (g, 0, 0))],
        out_specs=pl.BlockSpec((gb, ts, N), lambda s, g: (g, s, 0)),
        out_shape=SDS((G, S, N), F32),
        compiler_params=_params(("parallel", "parallel")),
    )(xb, w)


def contract_ln(a, w, xres, gamma, beta, scale, name):
    G, S, Kg = a.shape
    N = w.shape[2]
    ts = _tile(S, 512)

    def body(a_ref, w_ref, x_ref, g_ref, b_ref, z_ref, xn_ref, xb_ref):
        acc = _dot(a_ref[0], w_ref[0], NN)
        for j in range(1, G):
            acc = acc + _dot(a_ref[j], w_ref[j], NN)
        z = ALPHA * x_ref[...] + scale * acc
        xn = _layer_norm(z, g_ref[...], b_ref[...])
        z_ref[...] = z
        xn_ref[...] = xn
        xb_ref[...] = xn.astype(BF16)

    row = pl.BlockSpec((ts, N), lambda s: (s, 0))
    vec = pl.BlockSpec((1, N), lambda s: (0, 0))
    return pl.pallas_call(
        body, name=name, grid=(S // ts,),
        in_specs=[pl.BlockSpec((G, ts, Kg), lambda s: (0, s, 0)), pl.BlockSpec((G, Kg, N), lambda s: (0, 0, 0)), row, vec, vec],
        out_specs=[row, row, row],
        out_shape=[SDS((S, N), F32), SDS((S, N), F32), SDS((S, N), BF16)],
        compiler_params=_params(("parallel",)),
    )(a, w, xres, gamma, beta)


def ln_bwd(dx, z, gamma, out_scale, name):
    S, N = dx.shape
    ts = _tile(S, 512)

    def body(dx_ref, z_ref, g_ref, dz_ref, dzb_ref, dg_ref, db_ref):
        @pl.when(pl.program_id(0) == 0)
        def _():
            dg_ref[...] = jnp.zeros_like(dg_ref)
            db_ref[...] = jnp.zeros_like(db_ref)

        z = z_ref[...]
        mu = jnp.mean(z, axis=-1, keepdims=True)
        zc = z - mu
        var = jnp.mean(zc * zc, axis=-1, keepdims=True)
        rstd = lax.rsqrt(var + LN_EPS)
        xhat = zc * rstd
        dxv = dx_ref[...]
        dg_ref[...] += jnp.sum(dxv * xhat, axis=0, keepdims=True)
        db_ref[...] += jnp.sum(dxv, axis=0, keepdims=True)
        dxh = dxv * g_ref[...]
        m1 = jnp.mean(dxh, axis=-1, keepdims=True)
        m2 = jnp.mean(dxh * xhat, axis=-1, keepdims=True)
        dz = rstd * (dxh - m1 - xhat * m2)
        dz_ref[...] = dz
        dzb_ref[...] = (out_scale * dz).astype(BF16)

    row = pl.BlockSpec((ts, N), lambda s: (s, 0))
    vec = pl.BlockSpec((1, N), lambda s: (0, 0))
    return pl.pallas_call(
        body, name=name, grid=(S // ts,),
        in_specs=[row, row, vec],
        out_specs=[row, row, vec, vec],
        out_shape=[SDS((S, N), F32), SDS((S, N), BF16), SDS((1, N), F32), SDS((1, N), F32)],
        compiler_params=_params(("arbitrary",)),
    )(dx, z, gamma)


def ffn_bwd(dyb, dz, wd, wg, wu, g1, u1):
    S, K = dyb.shape
    G, N, _ = wd.shape
    ts = _tile(S, 512)

    def body(dy_ref, dz_ref, wd_ref, wg_ref, wu_ref, g_ref, u_ref, dg_ref, du_ref, dx_ref):
        j = pl.program_id(1)
        dh = _dot(dy_ref[...], wd_ref[j], NT)
        g = g_ref[0].astype(F32)
        sig = _sigmoid(g)
        dg = (dh * u_ref[0].astype(F32) * (sig * (1.0 + g * (1.0 - sig)))).astype(BF16)
        du = (dh * (g * sig)).astype(BF16)
        dg_ref[0] = dg
        du_ref[0] = du
        part = _dot(dg, wg_ref[j], NT) + _dot(du, wu_ref[j], NT)

        @pl.when(j == 0)
        def _():
            dx_ref[...] = ALPHA * dz_ref[...] + part

        @pl.when(j > 0)
        def _():
            dx_ref[...] += part

    row = pl.BlockSpec((ts, K), lambda s, j: (s, 0))
    gspec = pl.BlockSpec((1, ts, N), lambda s, j: (j, s, 0))
    wspec = _resident((G, K, N))
    return pl.pallas_call(
        body, name="ffn_bwd", grid=(S // ts, G),
        in_specs=[row, row, _resident((G, N, K)), wspec, wspec, gspec, gspec],
        out_specs=[gspec, gspec, row],
        out_shape=[SDS((G, S, N), BF16), SDS((G, S, N), BF16), SDS((S, K), F32)],
        compiler_params=_params(("parallel", "arbitrary")),
    )(dyb, dz, wd, wg, wu, g1, u1)


def proj_t(dyb, w, name):
    S, N = dyb.shape
    G, Kg, _ = w.shape
    ts = _tile(S, 1024)

    def body(dy_ref, w_ref, da_ref):
        dy = dy_ref[...]
        for j in range(G):
            da_ref[j] = _dot(dy, w_ref[j], NT)

    return pl.pallas_call(
        body, name=name, grid=(S // ts,),
        in_specs=[pl.BlockSpec((ts, N), lambda s: (s, 0)), pl.BlockSpec((G, Kg, N), lambda s: (0, 0, 0))],
        out_specs=pl.BlockSpec((G, ts, Kg), lambda s: (0, s, 0)),
        out_shape=SDS((G, S, Kg), F32),
        compiler_params=_params(("parallel",)),
    )(dyb, w)


def contract_t(da, w, res, name):
    G, S, Ng = da.shape
    K = w.shape[1]
    ts = _tile(S, 512)
    gb = _group_block(G, 6)

    def body(da_ref, w_ref, r_ref, o_ref):
        g = pl.program_id(1)
        part = _dot(da_ref[0], w_ref[0], NT)
        for j in range(1, gb):
            part = part + _dot(da_ref[j], w_ref[j], NT)

        @pl.when(g == 0)
        def _():
            o_ref[...] = ALPHA * r_ref[...] + part

        @pl.when(g > 0)
        def _():
            o_ref[...] += part

    row = pl.BlockSpec((ts, K), lambda s, g: (s, 0))
    return pl.pallas_call(
        body, name=name, grid=(S // ts, G // gb),
        in_specs=[pl.BlockSpec((gb, ts, Ng), lambda s, g: (g, s, 0)), pl.BlockSpec((gb, K, Ng), lambda s, g: (g, 0, 0)), row],
        out_specs=row,
        out_shape=SDS((S, K), F32),
        compiler_params=_params(("parallel", "arbitrary")),
    )(da, w, res)


WGRAD_ACC_ELEMS = 6 * 1024 * 256


def wgrad(a, b, out_dtype, name):
    ga, gb = a.ndim == 3, b.ndim == 3
    G = a.shape[0] if ga else b.shape[0]
    S, K = a.shape[-2:]
    N = b.shape[-1]
    ts = _tile(S, 1024)
    ns = S // ts
    ng = _group_block(G, WGRAD_ACC_ELEMS // (K * N))

    def body(a_ref, b_ref, o_ref, acc):
        s = pl.program_id(1)

        @pl.when(s == 0)
        def _():
            acc[...] = jnp.zeros_like(acc)

        for j in range(ng):
            acc[j] += _dot(a_ref[j] if ga else a_ref[...], b_ref[j] if gb else b_ref[...], TN)

        @pl.when(s == ns - 1)
        def _():
            o_ref[...] = acc[...].astype(out_dtype)

    aspec = pl.BlockSpec((ng, ts, K), lambda g, s: (g, s, 0)) if ga else pl.BlockSpec((ts, K), lambda g, s: (s, 0))
    bspec = pl.BlockSpec((ng, ts, N), lambda g, s: (g, s, 0)) if gb else pl.BlockSpec((ts, N), lambda g, s: (s, 0))
    return pl.pallas_call(
        body, name=name, grid=(G // ng, ns),
        in_specs=[aspec, bspec],
        out_specs=pl.BlockSpec((ng, K, N), lambda g, s: (g, 0, 0)),
        out_shape=SDS((G, K, N), out_dtype),
        scratch_shapes=[pltpu.VMEM((ng, K, N), F32)],
        compiler_params=_params(("parallel", "arbitrary")),
    )(a, b)


def loss_grad(xn, tgt):
    S, N = xn.shape
    ts = _tile(S, 512)

    def body(x_ref, t_ref, l_ref, dx_ref):
        @pl.when(pl.program_id(0) == 0)
        def _():
            l_ref[...] = jnp.zeros_like(l_ref)

        e = x_ref[...] - t_ref[...]
        dx_ref[...] = e * (1.0 / N)
        l_ref[...] += 0.5 * jnp.sum(jnp.mean(e * e, axis=-1, keepdims=True), axis=0, keepdims=True)

    row = pl.BlockSpec((ts, N), lambda s: (s, 0))
    return pl.pallas_call(
        body, name="loss_grad", grid=(S // ts,),
        in_specs=[row, row],
        out_specs=[pl.BlockSpec((1, 1), lambda s: (0, 0)), row],
        out_shape=[SDS((1, 1), F32), SDS((S, N), F32)],
        compiler_params=_params(("arbitrary",)),
    )(xn, tgt)


def _shift_down(x, k):
    if k == 0:
        return x
    rows = lax.broadcasted_iota(jnp.int32, x.shape, 0)
    return jnp.where(rows >= k, pltpu.roll(x, k, 0), 0.0)


def _shift_up(x, k):
    if k == 0:
        return x
    n = x.shape[0]
    rows = lax.broadcasted_iota(jnp.int32, x.shape, 0)
    return jnp.where(rows < n - k, pltpu.roll(x, n - k, 0), 0.0)


LANES = 128


def conv_mixer_fwd(u, cw):
    _, S, _ = u.shape
    nh = GROUP // LANES

    def body(b_ref, c_ref, x_ref, w_ref, o_ref):
        p = c_ref[0] * x_ref[0]
        w = w_ref[0]
        conv = w[2:3] * p + w[1:2] * _shift_down(p, 1) + w[0:1] * _shift_down(p, 2)
        o_ref[0] = (b_ref[0] * conv).astype(BF16)

    def uspec(off):
        return pl.BlockSpec((1, S, LANES), lambda g, h: (g + off, 0, h))

    return pl.pallas_call(
        body, name="conv_mixer_fwd", grid=(3, nh),
        in_specs=[uspec(0), uspec(3), uspec(6), pl.BlockSpec((1, 8, LANES), lambda g, h: (g, 0, h))],
        out_specs=pl.BlockSpec((1, S, LANES), lambda g, h: (g, 0, h)),
        out_shape=SDS((3, S, GROUP), BF16),
        compiler_params=_params(("parallel", "parallel")),
    )(u, u, u, cw)


def conv_mixer_bwd(u, cw, dm):
    _, S, _ = u.shape
    nh = GROUP // LANES

    def body(b_ref, c_ref, x_ref, w_ref, d_ref, db_ref, dc_ref, dx_ref, dw_ref):
        cg, xi = c_ref[0], x_ref[0]
        p = cg * xi
        p1, p2 = _shift_down(p, 1), _shift_down(p, 2)
        w = w_ref[0]
        conv = w[2:3] * p + w[1:2] * p1 + w[0:1] * p2
        dt = d_ref[0]
        db_ref[0] = (dt * conv).astype(BF16)
        dcv = dt * b_ref[0]
        dp = w[2:3] * dcv + w[1:2] * _shift_up(dcv, 1) + w[0:1] * _shift_up(dcv, 2)
        dc_ref[0] = (dp * xi).astype(BF16)
        dx_ref[0] = (dp * cg).astype(BF16)
        dw = jnp.concatenate([jnp.sum(dcv * p2, axis=0, keepdims=True), jnp.sum(dcv * p1, axis=0, keepdims=True),
                              jnp.sum(dcv * p, axis=0, keepdims=True), jnp.zeros((5, LANES), F32)], axis=0)
        dw_ref[0] = dw

    def uspec(off):
        return pl.BlockSpec((1, S, LANES), lambda g, h: (g + off, 0, h))

    ospec = pl.BlockSpec((1, S, LANES), lambda g, h: (g, 0, h))
    wspec = pl.BlockSpec((1, 8, LANES), lambda g, h: (g, 0, h))
    return pl.pallas_call(
        body, name="conv_mixer_bwd", grid=(3, nh),
        in_specs=[uspec(0), uspec(3), uspec(6), wspec, ospec],
        out_specs=[ospec, ospec, ospec, wspec],
        out_shape=[SDS((3, S, GROUP), BF16)] * 3 + [SDS((3, 8, GROUP), F32)],
        compiler_params=_params(("parallel", "parallel")),
    )(u, u, u, cw, dm)


def qk_conv_fwd(u, qw):
    _, S, _ = u.shape
    nh = GROUP // LANES

    def body(u_ref, w_ref, o_ref):
        x = u_ref[0]
        w = w_ref[0]
        pre = w[3:4] * x + w[2:3] * _shift_down(x, 1) + w[1:2] * _shift_down(x, 2) + w[0:1] * _shift_down(x, 3)
        o_ref[0] = pre * _sigmoid(pre)

    spec = pl.BlockSpec((1, S, LANES), lambda g, h: (g, 0, h))
    return pl.pallas_call(
        body, name="qk_conv_fwd", grid=(8, nh),
        in_specs=[spec, pl.BlockSpec((1, 8, LANES), lambda g, h: (g, 0, h))],
        out_specs=spec,
        out_shape=SDS((8, S, GROUP), F32),
        compiler_params=_params(("parallel", "parallel")),
    )(u, qw)


def qk_conv_bwd(u, qw, dqk):
    _, S, _ = u.shape
    nh = GROUP // LANES

    def body(u_ref, w_ref, d_ref, du_ref, dw_ref):
        x = u_ref[0]
        w = w_ref[0]
        x1, x2, x3 = _shift_down(x, 1), _shift_down(x, 2), _shift_down(x, 3)
        pre = w[3:4] * x + w[2:3] * x1 + w[1:2] * x2 + w[0:1] * x3
        sig = _sigmoid(pre)
        dpre = d_ref[0] * (sig * (1.0 + pre * (1.0 - sig)))
        du = w[3:4] * dpre + w[2:3] * _shift_up(dpre, 1) + w[1:2] * _shift_up(dpre, 2) + w[0:1] * _shift_up(dpre, 3)
        du_ref[0] = du.astype(BF16)
        dw = jnp.concatenate([jnp.sum(dpre * x3, axis=0, keepdims=True), jnp.sum(dpre * x2, axis=0, keepdims=True),
                              jnp.sum(dpre * x1, axis=0, keepdims=True), jnp.sum(dpre * x, axis=0, keepdims=True),
                              jnp.zeros((4, LANES), F32)], axis=0)
        dw_ref[0] = dw

    spec = pl.BlockSpec((1, S, LANES), lambda g, h: (g, 0, h))
    wspec = pl.BlockSpec((1, 8, LANES), lambda g, h: (g, 0, h))
    return pl.pallas_call(
        body, name="qk_conv_bwd", grid=(8, nh),
        in_specs=[spec, wspec, spec],
        out_specs=[spec, wspec],
        out_shape=[SDS((8, S, GROUP), BF16), SDS((8, 8, GROUP), F32)],
        compiler_params=_params(("parallel", "parallel")),
    )(u, qw, dqk)


def _head_masks():
    lane = lax.broadcasted_iota(jnp.int32, (1, D_XA), 1)
    return [(lane >= h * XA_HEAD_DIM) & (lane < (h + 1) * XA_HEAD_DIM) for h in range(XA_HEADS)]


def xattn_fwd(u, qg, kv):
    _, S, _ = u.shape
    ts = _tile(S, 512)
    scale = XA_HEAD_DIM ** -0.5

    def body(q_ref, kv_ref, o_ref):
        q = q_ref[0]
        k = kv_ref[0].astype(BF16)
        v = kv_ref[1]
        o = jnp.zeros((ts, D_XA), F32)
        for m in _head_masks():
            s = _dot(jnp.where(m, q, 0.0).astype(BF16), k, NT) * scale
            s = s - jnp.max(s, axis=-1, keepdims=True)
            e = jnp.exp(s)
            p = e / jnp.sum(e, axis=-1, keepdims=True)
            o = o + _dot(p.astype(BF16), jnp.where(m, v, 0.0).astype(BF16), NN)
        o_ref[0] = o.astype(BF16)

    return pl.pallas_call(
        body, name="xattn_fwd", grid=(S // ts,),
        in_specs=[pl.BlockSpec((1, ts, GROUP), lambda s: (qg, s, 0)), pl.BlockSpec((2, N_MEM, GROUP), lambda s: (0, 0, 0))],
        out_specs=pl.BlockSpec((1, ts, GROUP), lambda s: (0, s, 0)),
        out_shape=SDS((1, S, GROUP), BF16),
        compiler_params=_params(("parallel",)),
    )(u, kv)


def xattn_bwd(u, qg, kv, dm, dg):
    _, S, _ = u.shape
    ts = _tile(S, 512)
    scale = XA_HEAD_DIM ** -0.5

    def body(q_ref, kv_ref, do_ref, dq_ref, dkv_ref):
        @pl.when(pl.program_id(0) == 0)
        def _():
            dkv_ref[...] = jnp.zeros_like(dkv_ref)

        q = q_ref[0]
        k = kv_ref[0]
        v = kv_ref[1]
        kb = k.astype(BF16)
        do = do_ref[0]
        dq = jnp.zeros((ts, D_XA), F32)
        dk = jnp.zeros((N_MEM, D_XA), F32)
        dv = jnp.zeros((N_MEM, D_XA), F32)
        for m in _head_masks():
            qm = jnp.where(m, q, 0.0).astype(BF16)
            s = _dot(qm, kb, NT) * scale
            s = s - jnp.max(s, axis=-1, keepdims=True)
            e = jnp.exp(s)
            p = e / jnp.sum(e, axis=-1, keepdims=True)
            dom = jnp.where(m, do, 0.0).astype(BF16)
            dp = _dot(dom, jnp.where(m, v, 0.0).astype(BF16), NT)
            ds = (p * (dp - jnp.sum(dp * p, axis=-1, keepdims=True)) * scale).astype(BF16)
            dq = dq + _dot(ds, jnp.where(m, k, 0.0).astype(BF16), NN)
            dk = dk + _dot(ds, qm, TN)
            dv = dv + _dot(p.astype(BF16), dom, TN)
        dq_ref[0] = dq.astype(BF16)
        dkv_ref[0] += dk
        dkv_ref[1] += dv

    return pl.pallas_call(
        body, name="xattn_bwd", grid=(S // ts,),
        in_specs=[pl.BlockSpec((1, ts, GROUP), lambda s: (qg, s, 0)), pl.BlockSpec((2, N_MEM, GROUP), lambda s: (0, 0, 0)),
                  pl.BlockSpec((1, ts, GROUP), lambda s: (dg, s, 0))],
        out_specs=[pl.BlockSpec((1, ts, GROUP), lambda s: (0, s, 0)), pl.BlockSpec((2, N_MEM, GROUP), lambda s: (0, 0, 0))],
        out_shape=[SDS((1, S, GROUP), BF16), SDS((2, N_MEM, GROUP), F32)],
        compiler_params=_params(("arbitrary",)),
    )(u, kv, dm)


ML_BLOCK_CHUNKS = 4
H4 = ML_HEADS
L = ML_CHUNK
NLANE = ML_HEAD_DIM


def _chunk_consts():
    r = lax.broadcasted_iota(jnp.int32, (1, L, L), 1)
    c = lax.broadcasted_iota(jnp.int32, (1, L, L), 2)
    return r >= c, r <= c, r == c


def _gate_cols(gb):
    lane = lax.broadcasted_iota(jnp.int32, gb.shape, 1)
    li = jnp.stack([jnp.sum(jnp.where(lane == h, gb, 0.0), axis=1, keepdims=True) for h in range(H4)])
    gf = jnp.stack([jnp.sum(jnp.where(lane == H4 + h, gb, 0.0), axis=1, keepdims=True) for h in range(H4)])
    return li, gf


def _log_sigmoid(x):
    return jnp.minimum(x, 0.0) - jnp.log(1.0 + jnp.exp(-jnp.abs(x)))


def _chunk_forward(q, k, v_aug, li_col, lf_col, c_prev, m_prev):
    tri, tri_t, eye = _chunk_consts()
    lf_row = jnp.sum(jnp.where(eye, lf_col, 0.0), axis=1, keepdims=True)
    li_row = jnp.sum(jnp.where(eye, li_col, 0.0), axis=1, keepdims=True)
    bcum_col = jnp.sum(jnp.where(tri, lf_row, 0.0), axis=2, keepdims=True)
    bcum_row = jnp.sum(jnp.where(tri_t, lf_col, 0.0), axis=1, keepdims=True)
    log_d = jnp.where(tri, bcum_col - bcum_row + li_row, NEG)
    log_inter = bcum_col + m_prev
    m_t = jnp.maximum(log_inter, jnp.max(log_d, axis=2, keepdims=True))
    w_intra = jnp.exp(log_d - m_t)
    w_inter = jnp.exp(log_inter - m_t)
    sc = _bdot(q, k, 2, 2) * w_intra
    qc = _bdot(q, c_prev, 2, 1)
    num = _bdot(sc, v_aug, 2, 1) + w_inter * qc
    lane = lax.broadcasted_iota(jnp.int32, num.shape, 2)
    den = jnp.sum(jnp.where(lane == NLANE, num, 0.0), axis=2, keepdims=True)
    e_m = jnp.exp(-m_t)
    b_last = jnp.sum(lf_row, axis=2, keepdims=True)
    log_w = b_last - bcum_col + li_col
    m_new = jnp.maximum(b_last + m_prev, jnp.max(log_w, axis=1, keepdims=True))
    w_k = jnp.exp(log_w - m_new)
    decay = jnp.exp(b_last + m_prev - m_new)
    return dict(w_intra=w_intra, w_inter=w_inter, sc=sc, qc=qc, num=num, den=den, e_m=e_m, lane=lane,
                w_k=w_k, decay=decay, m_new=m_new)


def mlstm_fwd(qk, u, bg):
    _, S, _ = qk.shape
    nc = S // L
    cb = min(ML_BLOCK_CHUNKS, nc)
    rows = cb * L
    kscale = ML_HEAD_DIM ** -0.5

    def body(qk_ref, v_ref, g_ref, bg_ref, h_ref, cst_ref, mst_ref, c_sc, m_sc):
        @pl.when(pl.program_id(0) == 0)
        def _():
            c_sc[...] = jnp.zeros_like(c_sc)
            m_sc[...] = jnp.zeros_like(m_sc)

        for c in range(cb):
            sl = pl.ds(c * L, L)
            q = qk_ref[0:H4, sl, :]
            k = qk_ref[H4:2 * H4, sl, :] * kscale
            v = v_ref[:, sl, :]
            lane = lax.broadcasted_iota(jnp.int32, v.shape, 2)
            v_aug = jnp.where(lane == NLANE, 1.0, v)
            li_col, gf = _gate_cols(g_ref[0, sl, :] + bg_ref[...])
            lf_col = _log_sigmoid(gf)
            c_prev = c_sc[...]
            m_prev = m_sc[...]
            f = _chunk_forward(q, k, v_aug, li_col, lf_col, c_prev, m_prev)
            r = 1.0 / jnp.maximum(jnp.abs(f["den"]), f["e_m"])
            h_ref[:, sl, :] = jnp.where(lane < NLANE, f["num"] * r, 0.0)
            cst_ref[c] = c_prev
            mst_ref[c] = jnp.broadcast_to(m_prev, (H4, 1, LANES))
            c_sc[...] = f["decay"] * c_prev + _bdot(k * f["w_k"], v_aug, 1, 1)
            m_sc[...] = f["m_new"]

    def hspec(blk):
        return pl.BlockSpec((H4, rows, GROUP), lambda i: (blk, i, 0))

    return pl.pallas_call(
        body, name="mlstm_fwd", grid=(nc // cb,),
        in_specs=[pl.BlockSpec((2 * H4, rows, GROUP), lambda i: (0, i, 0)), hspec(2),
                  pl.BlockSpec((1, rows, GROUP), lambda i: (17, i, 0)), pl.BlockSpec((1, GROUP), lambda i: (0, 0))],
        out_specs=[hspec(0), pl.BlockSpec((cb, H4, GROUP, GROUP), lambda i: (i, 0, 0, 0)),
                   pl.BlockSpec((cb, H4, 1, LANES), lambda i: (i, 0, 0, 0))],
        out_shape=[SDS((H4, S, GROUP), F32), SDS((nc, H4, GROUP, GROUP), F32), SDS((nc, H4, 1, LANES), F32)],
        scratch_shapes=[pltpu.VMEM((H4, GROUP, GROUP), F32), pltpu.VMEM((H4, 1, 1), F32)],
        compiler_params=_params(("arbitrary",)),
    )(qk, u, u, bg)


def mlstm_bwd(qk, u, bg, cst, mst, dh):
    _, S, _ = qk.shape
    nc = S // L
    cb = min(ML_BLOCK_CHUNKS, nc)
    rows = cb * L
    nb = nc // cb
    kscale = ML_HEAD_DIM ** -0.5

    def body(qk_ref, v_ref, g_ref, bg_ref, cst_ref, mst_ref, dh_ref, dqk_ref, dv_ref, dg_ref, dbg_ref, dc_sc):
        @pl.when(pl.program_id(0) == 0)
        def _():
            dc_sc[...] = jnp.zeros_like(dc_sc)
            dbg_ref[...] = jnp.zeros_like(dbg_ref)

        tri, tri_t, eye = _chunk_consts()
        for c in reversed(range(cb)):
            sl = pl.ds(c * L, L)
            q = qk_ref[0:H4, sl, :]
            k = qk_ref[H4:2 * H4, sl, :] * kscale
            v = v_ref[:, sl, :]
            lane = lax.broadcasted_iota(jnp.int32, v.shape, 2)
            v_aug = jnp.where(lane == NLANE, 1.0, v)
            li_col, gf = _gate_cols(g_ref[0, sl, :] + bg_ref[...])
            lf_col = _log_sigmoid(gf)
            c_prev = cst_ref[c]
            m_prev = mst_ref[c][:, :, 0:1]
            f = _chunk_forward(q, k, v_aug, li_col, lf_col, c_prev, m_prev)
            w_intra, w_inter, sc, num, den, e_m = f["w_intra"], f["w_inter"], f["sc"], f["num"], f["den"], f["e_m"]
            absd = jnp.abs(den)
            r = 1.0 / jnp.maximum(absd, e_m)
            dhv = dh_ref[:, sl, :]
            s1 = jnp.sum(jnp.where(lane < NLANE, dhv * num, 0.0), axis=2, keepdims=True)
            dden = jnp.where(absd > e_m, -s1 * r * r * jnp.sign(den), 0.0)
            dnum = jnp.where(lane == NLANE, dden, jnp.where(lane < NLANE, dhv * r, 0.0))
            dsc = _bdot(dnum, v_aug, 2, 2)
            dv = _bdot(sc, dnum, 1, 1)
            gmat = dsc * sc
            dqk = dsc * w_intra
            dq = _bdot(dqk, k, 2, 1) + w_inter * _bdot(dnum, c_prev, 2, 2)
            dk = _bdot(dqk, q, 1, 1)
            dc_prev = _bdot(q * w_inter, dnum, 1, 1)
            dlog_inter = jnp.sum(dnum * f["qc"], axis=2, keepdims=True) * w_inter
            dbcum_col = dlog_inter + jnp.sum(gmat, axis=2, keepdims=True)
            g_row = jnp.sum(gmat, axis=1, keepdims=True)
            dcn = dc_sc[...]
            w_k, decay = f["w_k"], f["decay"]
            kw = k * w_k
            dc_prev = dc_prev + decay * dcn
            db_last = jnp.sum(jnp.sum(dcn * c_prev, axis=2, keepdims=True), axis=1, keepdims=True) * decay
            dkw = _bdot(v_aug, dcn, 2, 2)
            dv = dv + _bdot(kw, dcn, 2, 1)
            dk = dk + dkw * w_k
            dlogw = jnp.sum(dkw * k, axis=2, keepdims=True) * w_k
            db_last = db_last + jnp.sum(dlogw, axis=1, keepdims=True)
            dbcum_col = dbcum_col - dlogw
            rowi = lax.broadcasted_iota(jnp.int32, (1, L, 1), 1)
            dbcum_col = dbcum_col + jnp.where(rowi == L - 1, db_last, 0.0)
            dbcum_row = jnp.sum(jnp.where(eye, dbcum_col, 0.0), axis=1, keepdims=True) - g_row
            dlf_col = jnp.sum(jnp.where(tri_t, dbcum_row, 0.0), axis=2, keepdims=True)
            dli_col = dlogw + jnp.sum(jnp.where(eye, g_row, 0.0), axis=2, keepdims=True)
            dgf_col = dlf_col * _sigmoid(-gf)
            lane_g = lax.broadcasted_iota(jnp.int32, (L, GROUP), 1)
            dg = jnp.zeros((L, GROUP), F32)
            for h in range(H4):
                dg = dg + jnp.where(lane_g == h, dli_col[h], 0.0) + jnp.where(lane_g == H4 + h, dgf_col[h], 0.0)
            dqk_ref[0:H4, sl, :] = dq
            dqk_ref[H4:2 * H4, sl, :] = dk * kscale
            dv_ref[:, sl, :] = jnp.where(lane < NLANE, dv, 0.0).astype(BF16)
            dg_ref[0, sl, :] = dg.astype(BF16)
            dbg_ref[...] += jnp.sum(dg, axis=0, keepdims=True)
            dc_sc[...] = dc_prev

    def hspec(blk):
        return pl.BlockSpec((H4, rows, GROUP), lambda i: (blk, nb - 1 - i, 0))

    gspec = pl.BlockSpec((1, rows, GROUP), lambda i: (17, nb - 1 - i, 0))
    qkspec = pl.BlockSpec((2 * H4, rows, GROUP), lambda i: (0, nb - 1 - i, 0))
    return pl.pallas_call(
        body, name="mlstm_bwd", grid=(nb,),
        in_specs=[qkspec, hspec(2), gspec, pl.BlockSpec((1, GROUP), lambda i: (0, 0)),
                  pl.BlockSpec((cb, H4, GROUP, GROUP), lambda i: (nb - 1 - i, 0, 0, 0)),
                  pl.BlockSpec((cb, H4, 1, LANES), lambda i: (nb - 1 - i, 0, 0, 0)), hspec(0)],
        out_specs=[qkspec, hspec(0), pl.BlockSpec((1, rows, GROUP), lambda i: (0, nb - 1 - i, 0)),
                   pl.BlockSpec((1, GROUP), lambda i: (0, 0))],
        out_shape=[SDS((2 * H4, S, GROUP), F32), SDS((H4, S, GROUP), BF16),
                   SDS((1, S, GROUP), BF16), SDS((1, GROUP), F32)],
        scratch_shapes=[pltpu.VMEM((H4, GROUP, GROUP), F32)],
        compiler_params=_params(("arbitrary",)),
    )(qk, u, u, bg, cst, mst, dh)


def head_norm_fwd(hm, u, hg):
    _, S, _ = hm.shape
    ts = _tile(S, 512)

    def body(h_ref, o_ref, g_ref, t_ref):
        h = h_ref[0]
        lane = lax.broadcasted_iota(jnp.int32, h.shape, 1)
        valid = lane < ML_HEAD_DIM
        mu = jnp.sum(h, axis=-1, keepdims=True) * (1.0 / ML_HEAD_DIM)
        hc = jnp.where(valid, h - mu, 0.0)
        var = jnp.sum(hc * hc, axis=-1, keepdims=True) * (1.0 / ML_HEAD_DIM)
        hn = hc * lax.rsqrt(var + LN_EPS) * g_ref[0]
        t_ref[0] = (_sigmoid(o_ref[0]) * hn).astype(BF16)

    return pl.pallas_call(
        body, name="head_norm_fwd", grid=(H4, S // ts),
        in_specs=[pl.BlockSpec((1, ts, GROUP), lambda h, s: (h, s, 0)), pl.BlockSpec((1, ts, GROUP), lambda h, s: (12 + h, s, 0)),
                  pl.BlockSpec((1, 1, GROUP), lambda h, s: (h, 0, 0))],
        out_specs=pl.BlockSpec((1, ts, GROUP), lambda h, s: (h, s, 0)),
        out_shape=SDS((H4, S, GROUP), BF16),
        compiler_params=_params(("parallel", "parallel")),
    )(hm, u, hg)


def head_norm_bwd(hm, u, hg, dm):
    _, S, _ = hm.shape
    ts = _tile(S, 512)

    def body(h_ref, o_ref, g_ref, d_ref, dh_ref, do_ref, dg_ref):
        @pl.when(pl.program_id(1) == 0)
        def _():
            dg_ref[...] = jnp.zeros_like(dg_ref)

        h = h_ref[0]
        lane = lax.broadcasted_iota(jnp.int32, h.shape, 1)
        valid = lane < ML_HEAD_DIM
        inv = 1.0 / ML_HEAD_DIM
        mu = jnp.sum(h, axis=-1, keepdims=True) * inv
        hc = jnp.where(valid, h - mu, 0.0)
        var = jnp.sum(hc * hc, axis=-1, keepdims=True) * inv
        rstd = lax.rsqrt(var + LN_EPS)
        xhat = hc * rstd
        g = g_ref[0]
        sig = _sigmoid(o_ref[0])
        dt = jnp.where(valid, d_ref[0], 0.0)
        do_ref[0] = (dt * xhat * g * sig * (1.0 - sig)).astype(BF16)
        dhn = dt * sig
        dg_ref[0] += jnp.sum(dhn * xhat, axis=0, keepdims=True)
        dxh = dhn * g
        m1 = jnp.sum(dxh, axis=-1, keepdims=True) * inv
        m2 = jnp.sum(dxh * xhat, axis=-1, keepdims=True) * inv
        dh_ref[0] = jnp.where(valid, rstd * (dxh - m1 - xhat * m2), 0.0)

    spec = pl.BlockSpec((1, ts, GROUP), lambda h, s: (h, s, 0))
    gspec = pl.BlockSpec((1, 1, GROUP), lambda h, s: (h, 0, 0))
    return pl.pallas_call(
        body, name="head_norm_bwd", grid=(H4, S // ts),
        in_specs=[spec, pl.BlockSpec((1, ts, GROUP), lambda h, s: (12 + h, s, 0)), gspec, spec],
        out_specs=[spec, spec, gspec],
        out_shape=[SDS((H4, S, GROUP), F32), SDS((H4, S, GROUP), BF16), SDS((H4, 1, GROUP), F32)],
        compiler_params=_params(("parallel", "arbitrary")),
    )(hm, u, hg, dm)


def _adamw_math(w, g, m, v):
    c1 = 1.0 / (1.0 - ADAM_B1 ** ADAM_STEP)
    c2 = 1.0 / (1.0 - ADAM_B2 ** ADAM_STEP)
    nm = ADAM_B1 * m + (1.0 - ADAM_B1) * g
    nv = ADAM_B2 * v + (1.0 - ADAM_B2) * (g * g)
    return -ADAM_LR * ((nm * c1) / (jnp.sqrt(nv * c2) + ADAM_EPS) + ADAM_WD * w), nm, nv


def _row_tile(R, cap=512):
    return R if R <= cap else max(d for d in range(8, cap + 1, 8) if R % d == 0)


def adamw_into(w, m, v, g, outs, idx, name):
    R, C = g.shape
    tr = _row_tile(R)
    lead = (0,) * len(idx)

    def body(w_ref, m_ref, v_ref, g_ref, *rest):
        go_ref, d_ref, nm_ref, nv_ref = rest[-4:]
        gv = g_ref[...]
        d, nm, nv = _adamw_math(w_ref[lead], gv, m_ref[lead], v_ref[lead])
        go_ref[lead] = gv
        d_ref[lead] = d
        nm_ref[lead] = nm
        nv_ref[lead] = nv

    blk = pl.BlockSpec((1,) * len(idx) + (tr, C), lambda r: idx + (r, 0))
    in_specs, args, aliases = [blk, blk, blk, pl.BlockSpec((tr, C), lambda r: (r, 0))], [w, m, v, g], {}
    if outs is not None:
        in_specs += [pl.BlockSpec(memory_space=pl.ANY)] * 4
        args += list(outs)
        aliases = {4 + i: i for i in range(4)}
    return pl.pallas_call(
        body, name=name, grid=(R // tr,),
        in_specs=in_specs, out_specs=[blk] * 4, out_shape=[SDS(w.shape, F32)] * 4,
        input_output_aliases=aliases, compiler_params=_params(("parallel",)),
    )(*args)


def adamw(w, g, m, v, name):
    R, C = w.shape
    tr = _row_tile(R)

    def body(w_ref, g_ref, m_ref, v_ref, d_ref, nm_ref, nv_ref):
        d_ref[...], nm_ref[...], nv_ref[...] = _adamw_math(w_ref[...], g_ref[...], m_ref[...], v_ref[...])

    spec = pl.BlockSpec((tr, C), lambda i: (i, 0))
    return pl.pallas_call(
        body, name=name, grid=(R // tr,),
        in_specs=[spec] * 4, out_specs=[spec] * 3,
        out_shape=[SDS((R, C), F32)] * 3,
        compiler_params=_params(("parallel",)),
    )(w, g, m, v)


HBM = pl.BlockSpec(memory_space=pl.ANY)
ROW_SPLIT = 4


def _position():
    x, y, c = lax.axis_index("x"), lax.axis_index("y"), lax.axis_index("c")
    return x, y, c, [(1 - x, y), (x, 1 - y), (1 - x, 1 - y)]


def _unique(items):
    arrays = []
    for a, _ in items:
        if not any(a is b for b in arrays):
            arrays.append(a)
    return arrays, [next(i for i, b in enumerate(arrays) if b is a) for a, _ in items]


def place_own(items, me):
    arrays, src_of = _unique(items)
    n = len(items)
    shapes = [a.shape[len(p):] for a, p in items]

    def body(me_ref, *refs):
        for t in range(n):
            refs[n + t][...] = jnp.zeros_like(refs[n + t])
            refs[n + t][me_ref[0]] = refs[t][(0,) * len(items[t][1])]

    in_specs, out_specs = [], []
    for (a, p), shp in zip(items, shapes):
        blk = shp[:-2] + (shp[-2] // ROW_SPLIT, shp[-1])
        lead = (0,) * (len(shp) - 2)
        in_specs.append(pl.BlockSpec((1,) * len(p) + blk, functools.partial(lambda r, me_ref, p, lead: p + lead + (r, 0), p=p, lead=lead)))
        out_specs.append(pl.BlockSpec((N_CHIPS,) + blk, functools.partial(lambda r, me_ref, lead: (0,) + lead + (r, 0), lead=lead)))
    return pl.pallas_call(
        body, name="place_own",
        grid_spec=pltpu.PrefetchScalarGridSpec(num_scalar_prefetch=1, grid=(ROW_SPLIT,), in_specs=in_specs, out_specs=out_specs),
        out_shape=[SDS((N_CHIPS,) + tuple(shp), a.dtype) for shp, (a, _) in zip(shapes, items)],
        compiler_params=_params(("parallel",)),
    )(me, *[arrays[i] for i in src_of])


SEM = pl.BlockSpec(memory_space=pltpu.SEMAPHORE)
IN_HBM = pl.BlockSpec(memory_space=pltpu.HBM)
DATAFLOW = pltpu.SideEffectType.DATAFLOW_SIDE_EFFECTING


def split_start(bufs, plan, n_copies, after, name):
    n = len(bufs)

    def body(*refs):
        send, recv, token = refs[n + 1], refs[n + 2], refs[-1]
        x, y, c, chips = _position()
        for k, (src, dst, dev) in enumerate(plan(refs[:n], x, y, c, chips)):
            pltpu.make_async_remote_copy(src_ref=src, dst_ref=dst, send_sem=send.at[k], recv_sem=recv.at[k],
                                         device_id=dev, device_id_type=MESH).start()
        token[...] = jnp.zeros_like(token)

    out = pl.pallas_call(
        body, name=name,
        out_shape=(pltpu.SemaphoreType.DMA((n_copies,)), pltpu.SemaphoreType.DMA((n_copies,)),
                   *[pltpu.HBM(b.shape, b.dtype) for b in bufs], SDS((8, LANES), F32)),
        in_specs=[IN_HBM] * n + [pl.BlockSpec(memory_space=pl.ANY)],
        out_specs=(SEM, SEM, *[IN_HBM] * n, pl.BlockSpec(memory_space=pltpu.VMEM)),
        input_output_aliases={i: 2 + i for i in range(n)},
        compiler_params=pltpu.CompilerParams(has_side_effects=DATAFLOW),
    )(*[pltpu.with_memory_space_constraint(b, pltpu.HBM) for b in bufs], after)
    return out[0], out[1], list(out[2:2 + n]), out[-1]


def split_wait(send, recv, bufs, plan, after, name):
    n = len(bufs)

    def body(*refs):
        send_ref, recv_ref = refs[n], refs[n + 1]
        x, y, c, chips = _position()
        for k, (src, dst, dev) in enumerate(plan(refs[:n], x, y, c, chips)):
            cp = pltpu.make_async_remote_copy(src_ref=src, dst_ref=dst, send_sem=send_ref.at[k], recv_sem=recv_ref.at[k],
                                              device_id=dev, device_id_type=MESH)
            cp.wait_send()
            cp.wait_recv()

    return list(pl.pallas_call(
        body, name=name, out_shape=tuple(pltpu.HBM(b.shape, b.dtype) for b in bufs),
        in_specs=[IN_HBM] * n + [SEM, SEM, pl.BlockSpec(memory_space=pl.ANY)], out_specs=tuple([IN_HBM] * n),
        input_output_aliases={i: i for i in range(n)},
        compiler_params=pltpu.CompilerParams(has_side_effects=DATAFLOW),
    )(*bufs, send, recv, after))


def _gather_plan(shapes, landing):
    n = len(shapes)

    def plan(refs, x, y, c, chips):
        out = []
        for t in range(n):
            half = shapes[t][0] // 2
            rows = pl.ds(c * half, half)
            for cx, cy in chips:
                slot = 2 * cx + cy if landing else 2 * x + y
                out.append((refs[t].at[rows], refs[n + t].at[slot, rows], (cx, cy, c)))
        return out

    return plan


def gather_start(shards, placed, after, name):
    shapes = [s.shape for s in shards]
    send, recv, bufs, token = split_start(shards + placed, _gather_plan(shapes, False), 3 * len(shards), after, name)
    return (send, recv, bufs, shapes), token


def gather_wait(state, after, name):
    send, recv, bufs, shapes = state
    return split_wait(send, recv, bufs, _gather_plan(shapes, True), after, name)[len(shapes):]


def gather_pass_on(placed, shapes, name):
    n = len(placed)

    def body(*refs):
        outs, send, recv = refs[n:2 * n], refs[2 * n], refs[2 * n + 1]
        x, y, c, chips = _position()
        cps = []
        for t in range(n):
            half = shapes[t][0] // 2
            for j, (cx, cy) in enumerate(chips):
                piece = outs[t].at[2 * cx + cy, pl.ds(c * half, half)]
                cp = pltpu.make_async_remote_copy(src_ref=piece, dst_ref=piece, send_sem=send.at[3 * t + j], recv_sem=recv.at[3 * t + j],
                                                  device_id=(x, y, 1 - c), device_id_type=MESH)
                cp.start()
                cps.append(cp)
        for t in range(n):
            half = shapes[t][0] // 2
            for j, (cx, cy) in enumerate(chips):
                piece = outs[t].at[2 * cx + cy, pl.ds((1 - c) * half, half)]
                pltpu.make_async_remote_copy(src_ref=piece, dst_ref=piece, send_sem=send.at[3 * t + j], recv_sem=recv.at[3 * t + j],
                                             device_id=(x, y, 1 - c), device_id_type=MESH).wait_recv()
        for cp in cps:
            cp.wait_send()

    return pl.pallas_call(
        body, name=name,
        in_specs=[HBM] * n, out_specs=[HBM] * n,
        out_shape=[SDS(p.shape, p.dtype) for p in placed],
        input_output_aliases={t: t for t in range(n)},
        scratch_shapes=[pltpu.SemaphoreType.DMA((3 * n,))] * 2,
    )(*placed)


def _flip(k, x, y, c):
    return ((1 - x) if k & 4 else x, (1 - y) if k & 2 else y, (1 - c) if k & 1 else c)


def small_allgather(v, reduce):
    R, C = v.shape

    def body(v_ref, o_ref, *scratch):
        if reduce:
            buf, send, recv = scratch
        else:
            buf, (send, recv) = o_ref, scratch
        x, y, c, _ = _position()
        me = 4 * x + 2 * y + c
        buf[me] = v_ref[...]
        sends = []
        for k in range(1, N_DEV):
            cp = pltpu.make_async_remote_copy(src_ref=v_ref, dst_ref=buf.at[me], send_sem=send.at[k - 1], recv_sem=recv.at[k - 1],
                                              device_id=_flip(k, x, y, c), device_id_type=MESH)
            cp.start()
            sends.append(cp)
        for k in range(1, N_DEV):
            px, py, pc = _flip(k, x, y, c)
            pltpu.make_async_remote_copy(src_ref=v_ref, dst_ref=buf.at[4 * px + 2 * py + pc], send_sem=send.at[k - 1],
                                         recv_sem=recv.at[k - 1], device_id=(px, py, pc), device_id_type=MESH).wait_recv()
        for cp in sends:
            cp.wait_send()
        if reduce:
            acc = buf[0]
            for i in range(1, N_DEV):
                acc = acc + buf[i]
            o_ref[...] = acc

    vm = pl.BlockSpec(memory_space=pltpu.VMEM)
    sems = [pltpu.SemaphoreType.DMA((N_DEV - 1,)), pltpu.SemaphoreType.DMA((N_DEV - 1,))]
    return pl.pallas_call(
        body, name="small_allreduce" if reduce else "small_allgather",
        in_specs=[vm], out_specs=vm,
        out_shape=SDS((R, C) if reduce else (N_DEV, R, C), F32),
        scratch_shapes=([pltpu.VMEM((N_DEV, R, C), F32)] if reduce else []) + sems,
    )(v)


def rs_exchange_sibling(gs):
    n = len(gs)

    def body(*refs):
        ins, outs, send, recv = refs[:n], refs[n:2 * n], refs[2 * n], refs[2 * n + 1]
        x, y, c, _ = _position()
        cps = []
        for t in range(n):
            cp = pltpu.make_async_remote_copy(src_ref=ins[t].at[:, 1 - c], dst_ref=outs[t], send_sem=send.at[t], recv_sem=recv.at[t],
                                              device_id=(x, y, 1 - c), device_id_type=MESH)
            cp.start()
            cps.append(cp)
        for cp in cps:
            cp.wait()

    return pl.pallas_call(
        body, name="rs_exchange_sibling", in_specs=[HBM] * n, out_specs=[HBM] * n,
        out_shape=[SDS((g.shape[0],) + g.shape[2:], g.dtype) for g in gs],
        scratch_shapes=[pltpu.SemaphoreType.DMA((n,)), pltpu.SemaphoreType.DMA((n,))],
    )(*gs)


def rs_pair_add(gs, rs, c):
    n = len(gs)

    def body(c_ref, *refs):
        for t in range(n):
            refs[2 * n + t][0] = (refs[t][0, 0].astype(F32) + refs[n + t][0].astype(F32)).astype(BF16)

    in_specs, out_specs, out_shape = [], [], []
    for g in gs:
        _, _, h, C = g.shape
        in_specs.append(pl.BlockSpec((1, 1, h // ROW_SPLIT, C), lambda j, r, c_ref: (j, c_ref[0], r, 0)))
    for g in gs:
        _, _, h, C = g.shape
        spec = pl.BlockSpec((1, h // ROW_SPLIT, C), lambda j, r, c_ref: (j, r, 0))
        in_specs.append(spec)
        out_specs.append(spec)
        out_shape.append(SDS((N_CHIPS, h, C), BF16))
    return pl.pallas_call(
        body, name="rs_pair_add",
        grid_spec=pltpu.PrefetchScalarGridSpec(num_scalar_prefetch=1, grid=(N_CHIPS, ROW_SPLIT), in_specs=in_specs, out_specs=out_specs),
        out_shape=out_shape, compiler_params=_params(("parallel", "parallel")),
    )(c, *gs, *rs)


def _rs_plan(n):
    def plan(refs, x, y, c, chips):
        return [(refs[t].at[2 * cx + cy], refs[n + t].at[j], (cx, cy, c)) for t in range(n) for j, (cx, cy) in enumerate(chips)]

    return plan


def rs_chip_add(ps, qs, me_c):
    n = len(ps)

    def body(me_ref, *refs):
        for t in range(n):
            q = refs[n + t]
            refs[2 * n + t][...] = jnp.zeros_like(refs[2 * n + t])
            refs[2 * n + t][me_ref[1]] = ((refs[t][0].astype(F32) + q[0].astype(F32)) + q[1].astype(F32)) + q[2].astype(F32)

    in_specs, out_specs, out_shape = [], [], []
    for p in ps:
        _, h, C = p.shape
        in_specs.append(pl.BlockSpec((1, h // ROW_SPLIT, C), lambda r, me_ref: (me_ref[0], r, 0)))
    for p in ps:
        _, h, C = p.shape
        in_specs.append(pl.BlockSpec((3, h // ROW_SPLIT, C), lambda r, me_ref: (0, r, 0)))
        out_specs.append(pl.BlockSpec((2, h // ROW_SPLIT, C), lambda r, me_ref: (0, r, 0)))
        out_shape.append(SDS((2, h, C), F32))
    return pl.pallas_call(
        body, name="rs_chip_add",
        grid_spec=pltpu.PrefetchScalarGridSpec(num_scalar_prefetch=1, grid=(ROW_SPLIT,), in_specs=in_specs, out_specs=out_specs),
        out_shape=out_shape, compiler_params=_params(("parallel",)),
    )(me_c, *ps, *qs)


def rs_share(rs):
    n = len(rs)

    def body(*refs):
        outs, send, recv = refs[n:2 * n], refs[2 * n], refs[2 * n + 1]
        x, y, c, _ = _position()
        cps = []
        for t in range(n):
            cp = pltpu.make_async_remote_copy(src_ref=outs[t].at[c], dst_ref=outs[t].at[c], send_sem=send.at[t], recv_sem=recv.at[t],
                                              device_id=(x, y, 1 - c), device_id_type=MESH)
            cp.start()
            cps.append(cp)
        for cp in cps:
            cp.wait()

    return pl.pallas_call(
        body, name="rs_share", in_specs=[HBM] * n, out_specs=[HBM] * n,
        out_shape=[SDS(r.shape, r.dtype) for r in rs],
        input_output_aliases={t: t for t in range(n)},
        scratch_shapes=[pltpu.SemaphoreType.DMA((n,))] * 2,
    )(*rs)


def rs_begin(gs, name):
    c = lax.axis_index("c")
    n = len(gs)
    g5 = [g.reshape(N_CHIPS, 2, g.shape[1] // 2, g.shape[2]) for g in gs]
    from_sibling = rs_exchange_sibling(g5)
    pair = rs_pair_add(g5, from_sibling, jnp.reshape(c, (1,)).astype(jnp.int32))
    lands = [jnp.zeros((3,) + p.shape[1:], p.dtype) for p in pair]
    send, recv, bufs, token = split_start(list(pair) + lands, _rs_plan(n), 3 * n, from_sibling[0], name)
    return (send, recv, bufs, [g.shape for g in gs]), token


def rs_end(state, after, name):
    x, y, c = lax.axis_index("x"), lax.axis_index("y"), lax.axis_index("c")
    send, recv, bufs, shapes = state
    n = len(shapes)
    bufs = split_wait(send, recv, bufs, _rs_plan(n), after, name)
    half = rs_chip_add(bufs[:n], bufs[n:], jnp.stack([2 * x + y, c]).astype(jnp.int32))
    both = rs_share(half)
    return [b.reshape(s[1], s[2]) for b, s in zip(both, shapes)]


def _pad_last(a, n):
    return jnp.pad(a, [(0, 0)] * (a.ndim - 1) + [(0, n - a.shape[-1])])


def _heads_to_groups(w):
    k = w.shape[0]
    return _pad_last(w.reshape(k, ML_HEADS, ML_HEAD_DIM).transpose(1, 0, 2), GROUP)


def _groups_to_heads(g):
    return g[:, :, :ML_HEAD_DIM].transpose(1, 0, 2).reshape(g.shape[1], D_TOK)


def _cols_to_groups(w):
    k, n = w.shape
    return w.reshape(k, n // GROUP, GROUP).transpose(1, 0, 2)


def _groups_to_cols(g):
    n, k, _ = g.shape
    return g.transpose(1, 0, 2).reshape(k, n * GROUP)


def _chips_to_cols(a):
    return a.transpose(1, 0, 2).reshape(a.shape[1], -1)


def _cols_to_chips(w):
    k, n = w.shape
    return w.reshape(k, N_CHIPS, n // N_CHIPS).transpose(1, 0, 2)


def _mlstm_in_groups(w):
    parts = [_heads_to_groups(w[:, i * D_TOK:(i + 1) * D_TOK]) for i in range(4)]
    gates = _pad_last(w[:, 4 * D_TOK:4 * D_TOK + 2 * ML_HEADS], GROUP)[None]
    qmem = w[:, 4 * D_TOK + 2 * ML_HEADS:][None]
    return jnp.concatenate(parts + [qmem, gates], axis=0)


def _mlstm_in_ungroup(g):
    parts = [_groups_to_heads(g[4 * i:4 * i + 4]) for i in range(4)]
    return jnp.concatenate(parts + [g[17][:, :2 * ML_HEADS], g[16]], axis=1)


def _taps_to_groups(w, width):
    taps = w.shape[0]
    g = _pad_last(w.reshape(taps, -1, width), GROUP).transpose(1, 0, 2)
    return jnp.pad(g, ((0, 0), (0, 8 - taps), (0, 0)))


def _groups_to_taps(g, taps, width):
    return g[:, :taps, :width].transpose(1, 0, 2).reshape(taps, -1)


SMALL_IN_COLS = 384
SMALL_OUT_COLS = 1536
SECTION = 8


class _Gathered:
    def __init__(self, srcs, groups, me):
        keys = [k for g in groups for k in g]
        placed = dict(zip(keys, place_own([(srcs[k], ()) for k in keys], me)))
        self.groups, self.states, self.ready = groups, [], {}
        self.group_of = {k: gi for gi, g in enumerate(groups) for k in g}
        token = me
        for gi, g in enumerate(groups):
            state, token = gather_start([srcs[k] for k in g], [placed[k] for k in g], token, f"gather_start_{gi}")
            self.states.append(state)
        self.started = token

    def _get(self, key, after):
        gi = self.group_of[key]
        if gi not in self.ready:
            got = gather_wait(self.states[gi], after if gi else self.started, f"gather_wait_{gi}")
            self.ready[gi] = dict(zip(self.groups[gi], gather_pass_on(got, self.states[gi][3], f"gather_pass_on_{gi}")))
        return self.ready[gi][key]

    def ffn(self, l, i, after):
        return tuple(self._get((n, l, i), after) for n in ("wg", "wu", "wd"))

    def mixer(self, l, after):
        win = _chips_to_cols(self._get(("win", l), after))
        win = _cols_to_groups(win) if l % 2 == 0 else _mlstm_in_groups(win)
        wkv = _cols_to_groups(self._get(("wkv", l), after).reshape(D_MODEL, 2 * D_XA))
        wout = self._get(("wout", l), after)
        if l % 2:
            wout = wout.reshape(D_MODEL, D_MODEL)
            tok = jnp.pad(wout[:D_TOK].reshape(ML_HEADS, ML_HEAD_DIM, D_MODEL), ((0, 0), (0, GROUP - ML_HEAD_DIM), (0, 0)))
            wout = jnp.concatenate([tok, wout[D_TOK:][None]], axis=0)
        return win, wkv, wout


class _GradSink:
    def __init__(self, apply):
        self.queue, self.apply, self.count = [], apply, 0

    @staticmethod
    def _by_chip(key, g):
        if key[0] == "wkv":
            return _groups_to_cols(g).reshape(N_CHIPS, D_MODEL // N_CHIPS, 2 * D_XA)
        if key[0] == "win":
            return _cols_to_chips(_groups_to_cols(g) if key[1] % 2 == 0 else _mlstm_in_ungroup(g))
        if key[0] == "wout" and key[1] % 2:
            full = jnp.concatenate([g[:ML_HEADS, :ML_HEAD_DIM].reshape(D_TOK, D_MODEL), g[ML_HEADS]], axis=0)
            return full.reshape(N_CHIPS, D_MODEL // N_CHIPS, D_MODEL)
        return g

    def begin(self, grads):
        keys = list(grads)
        state, token = rs_begin([self._by_chip(k, grads[k]) for k in keys], f"rs_start_{self.count}")
        self.queue.append((keys, state, self.count))
        self.count += 1
        return token

    def end(self, after):
        keys, state, i = self.queue.pop(0)
        for key, g in zip(keys, rs_end(state, after, f"rs_wait_{i}")):
            self.apply(key, g)


def _local_step(x, mem, tgt, P, weights, sink):
    memb = mem.astype(BF16)
    saved = []
    X, Xb = x, x.astype(BF16)
    after = Xb
    for l in range(DEPTH):
        s = {}
        s["x0b"] = Xb
        s["wa"] = weights.ffn(l, 0, after)
        s["g1a"], s["u1a"], s["ha"], s["z1"], X1, X1b = ffn_fwd(Xb, X, *s["wa"], P["ln_g"][l][0], P["ln_b"][l][0])
        s["x1b"] = X1b
        s["wm"] = win, wkv, wout = weights.mixer(l, X1b)
        u = proj(X1b, win, "mixer_in")
        kv = proj(memb, wkv, "mem_kv")
        s["u"], s["kv"] = u, kv
        if l % 2 == 0:
            tok = conv_mixer_fwd(u, P["convw"])
            qg = 9
        else:
            s["qk"] = qk_conv_fwd(u, P["qkw"])
            s["hm"], s["cst"], s["mst"] = mlstm_fwd(s["qk"], u, P["bg"])
            tok = head_norm_fwd(s["hm"], u, P["hg"])
            qg = 16
        xa = xattn_fwd(u, qg, kv)
        s["m"] = jnp.concatenate([tok, xa], axis=0)
        s["z2"], X2, X2b = contract_ln(s["m"], wout, X1, P["ln_g"][l][1], P["ln_b"][l][1], 1.0, "mixer_out_ln")
        s["x2b"] = X2b
        s["wb"] = weights.ffn(l, 1, X2b)
        s["g1b"], s["u1b"], s["hb"], s["z3"], X, Xb = ffn_fwd(X2b, X2, *s["wb"], P["ln_g"][l][2], P["ln_b"][l][2])
        after = Xb
        saved.append(s)

    loss, dX = loss_grad(X, tgt)

    G = {"ln_g": [[None] * 3 for _ in range(DEPTH)], "ln_b": [[None] * 3 for _ in range(DEPTH)]}
    pin = [jnp.zeros((1, 1), F32)]

    def ffn_backward(l, i, dX, z, xinb, g1, u1, h, w):
        k = 2 * i
        dz, dyb, G["ln_g"][l][k], G["ln_b"][l][k] = ln_bwd(dX, z, P["ln_g"][l][k] + pin[0], 0.5, "ffn_ln_bwd")
        dgb, dub, dx = ffn_bwd(dyb, dz, w[2], w[0], w[1], g1, u1)
        grads = {("wd", l, i): wgrad(h, dyb, BF16, "wgrad_down"), ("wg", l, i): wgrad(dgb, xinb, BF16, "wgrad_gate"),
                 ("wu", l, i): wgrad(dub, xinb, BF16, "wgrad_up")}
        return dx, grads

    pending = 0
    for l in reversed(range(DEPTH)):
        s = saved[l]
        win, wkv, wout = s["wm"]
        dX, grads = ffn_backward(l, 1, dX, s["z3"], s["x2b"], s["g1b"], s["u1b"], s["hb"], s["wb"])
        if pending:
            sink.end(dX)
        dz2, dz2b, G["ln_g"][l][1], G["ln_b"][l][1] = ln_bwd(dX, s["z2"], P["ln_g"][l][1], 1.0, "mixer_ln_bwd")
        dm = proj_t(dz2b, wout, "mixer_out_bwd")
        grads[("wout", l)] = wgrad(s["m"], dz2b, BF16, "wgrad_out")
        u, kv = s["u"], s["kv"]
        if l % 2 == 0:
            db, dc, dxi, G["convw"] = conv_mixer_bwd(u, P["convw"], dm)
            dq, dkv = xattn_bwd(u, 9, kv, dm, 3)
            du = jnp.concatenate([db, dc, dxi, dq], axis=0)
        else:
            dh, do, G["hg"] = head_norm_bwd(s["hm"], u, P["hg"], dm)
            dqk, dv, dgate, G["bg"] = mlstm_bwd(s["qk"], u, P["bg"], s["cst"], s["mst"], dh)
            duqk, G["qkw"] = qk_conv_bwd(u, P["qkw"], dqk)
            dq, dkv = xattn_bwd(u, 16, kv, dm, 4)
            du = jnp.concatenate([duqk, dv, do, dq, dgate], axis=0)
        grads[("win", l)] = wgrad(s["x1b"], du, BF16, "wgrad_in")
        grads[("wkv", l)] = wgrad(memb, dkv.astype(BF16), BF16, "wgrad_kv")
        dX = contract_t(du, win, dz2, "mixer_in_bwd")
        pin[0] = sink.begin(grads)[0:1, 0:1]
        dX, grads = ffn_backward(l, 0, dX, s["z1"], s["x0b"], s["g1a"], s["u1a"], s["ha"], s["wa"])
        sink.end(dX)
        pin[0] = sink.begin(grads)[0:1, 0:1]
        pending = 1
    sink.end(dX)
    return loss, dX, G


def kernel(x, mem, ln_g, ln_b, ffn_w_gate, ffn_w_up, ffn_w_down, w_kv_mem, w_out, w_in_conv, conv_w, w_in_mlstm, b_gates, qk_conv_w, head_norm_g, loss_target, m_ln_g, m_ln_b, m_ffn_w_gate, m_ffn_w_up, m_ffn_w_down, m_w_kv_mem, m_w_out, m_w_in_conv, m_conv_w, m_w_in_mlstm, m_b_gates, m_qk_conv_w, m_head_norm_g, v_ln_g, v_ln_b, v_ffn_w_gate, v_ffn_w_up, v_ffn_w_down, v_w_kv_mem, v_w_out, v_w_in_conv, v_conv_w, v_w_in_mlstm, v_b_gates, v_qk_conv_w, v_head_norm_g):
    cx, cy = lax.axis_index("x"), lax.axis_index("y")
    chip = 2 * cx + cy

    srcs = {}
    for l in range(DEPTH):
        for i in range(2):
            srcs[("wg", l, i)] = ffn_w_gate[l, i].astype(BF16)
            srcs[("wu", l, i)] = ffn_w_up[l, i].astype(BF16)
            srcs[("wd", l, i)] = ffn_w_down[l, i].astype(BF16)
        srcs[("wkv", l)] = w_kv_mem[l].astype(BF16)
        srcs[("wout", l)] = w_out[l].astype(BF16)
    srcs[("win", 0)] = w_in_conv[0].astype(BF16)
    srcs[("win", 1)] = w_in_mlstm[0].astype(BF16)
    ffn_keys = lambda l, i: [("wg", l, i), ("wu", l, i), ("wd", l, i)]
    mixer_keys = lambda l: [("win", l), ("wkv", l), ("wout", l)]
    groups = [ffn_keys(0, 0), mixer_keys(0) + mixer_keys(1) + ffn_keys(0, 1), ffn_keys(1, 0), ffn_keys(1, 1)]
    gathered = _Gathered(srcs, groups, jnp.reshape(chip, (1,)).astype(jnp.int32))

    def section(a, width):
        a = a.reshape(-1, a.shape[-1])
        return jnp.pad(a, ((0, SECTION - a.shape[0]), (0, width - a.shape[1])))

    small = jnp.concatenate([section(a, SMALL_IN_COLS) for a in (ln_g, ln_b, conv_w, qk_conv_w)], axis=0)
    smalls = small_allgather(small, reduce=False)[0::2]
    ln_g_full = _chips_to_cols(smalls[:, 0:6, 0:256]).reshape(DEPTH, 3, 1, D_MODEL)
    ln_b_full = _chips_to_cols(smalls[:, 8:14, 0:256]).reshape(DEPTH, 3, 1, D_MODEL)
    conv_w_full = _chips_to_cols(smalls[:, 16:19, 0:192])
    qk_w_full = _chips_to_cols(smalls[:, 24:28, 0:384])

    P = {"ln_g": ln_g_full, "ln_b": ln_b_full, "convw": _taps_to_groups(conv_w_full, GROUP),
         "qkw": _taps_to_groups(qk_w_full, ML_HEAD_DIM), "bg": _pad_last(b_gates, GROUP),
         "hg": _pad_last(head_norm_g[0], GROUP)[:, None, :]}

    weights = {"ln_g": ln_g, "ln_b": ln_b, "ffn_w_gate": ffn_w_gate, "ffn_w_up": ffn_w_up, "ffn_w_down": ffn_w_down,
               "w_kv_mem": w_kv_mem, "w_out": w_out, "w_in_conv": w_in_conv, "conv_w": conv_w, "w_in_mlstm": w_in_mlstm,
               "b_gates": b_gates, "qk_conv_w": qk_conv_w, "head_norm_g": head_norm_g}
    ms = {"ln_g": m_ln_g, "ln_b": m_ln_b, "ffn_w_gate": m_ffn_w_gate, "ffn_w_up": m_ffn_w_up, "ffn_w_down": m_ffn_w_down,
          "w_kv_mem": m_w_kv_mem, "w_out": m_w_out, "w_in_conv": m_w_in_conv, "conv_w": m_conv_w, "w_in_mlstm": m_w_in_mlstm,
          "b_gates": m_b_gates, "qk_conv_w": m_qk_conv_w, "head_norm_g": m_head_norm_g}
    vs = {"ln_g": v_ln_g, "ln_b": v_ln_b, "ffn_w_gate": v_ffn_w_gate, "ffn_w_up": v_ffn_w_up, "ffn_w_down": v_ffn_w_down,
          "w_kv_mem": v_w_kv_mem, "w_out": v_w_out, "w_in_conv": v_w_in_conv, "conv_w": v_conv_w, "w_in_mlstm": v_w_in_mlstm,
          "b_gates": v_b_gates, "qk_conv_w": v_qk_conv_w, "head_norm_g": v_head_norm_g}
    names = list(weights)
    owner = {"wg": ("ffn_w_gate", True), "wu": ("ffn_w_up", True), "wd": ("ffn_w_down", False), "wkv": ("w_kv_mem", False),
             "wout": ("w_out", False), "win": None}
    updated = {}

    def apply(key, g):
        name, transposed = owner[key[0]] or (("w_in_conv", "w_in_mlstm")[key[1]], False)
        idx = (0,) if key[0] == "win" else tuple(key[1:])
        view = (lambda a: jnp.swapaxes(a, -1, -2)) if transposed else (lambda a: a)
        updated[name] = adamw_into(view(weights[name]), view(ms[name]), view(vs[name]), g, updated.get(name), idx,
                                   "adamw_" + name + "_" + "_".join(map(str, idx)))

    sink = _GradSink(apply)
    loss, grad_x, G = _local_step(x[0], mem[0], loss_target[0], P, gathered, sink)

    dln_g = jnp.concatenate([G["ln_g"][l][k] for l in range(DEPTH) for k in range(3)], axis=0)
    dln_b = jnp.concatenate([G["ln_b"][l][k] for l in range(DEPTH) for k in range(3)], axis=0)
    lane = lax.broadcasted_iota(jnp.int32, (1, GROUP), 1)
    misc = jnp.where(lane < 8, G["bg"], 0.0) + jnp.where(lane == 8, loss, 0.0)
    parts = (dln_g, dln_b, _groups_to_taps(G["convw"], 3, GROUP), misc, _groups_to_taps(G["qkw"], 4, ML_HEAD_DIM),
             G["hg"][:, 0, :ML_HEAD_DIM])
    tot = small_allgather(jnp.concatenate([section(a, SMALL_OUT_COLS) for a in parts], axis=0), reduce=True)
    loss_total = tot[24, 8]

    small_grads = {
        "ln_g": lax.dynamic_slice(tot[0:6, 0:D_MODEL], (0, chip * 256), (6, 256)).reshape(DEPTH, 3, 256),
        "ln_b": lax.dynamic_slice(tot[8:14, 0:D_MODEL], (0, chip * 256), (6, 256)).reshape(DEPTH, 3, 256),
        "conv_w": lax.dynamic_slice(tot[16:19, 0:D_TOK], (0, chip * 192), (3, 192))[None],
        "b_gates": tot[24:25, 0:8],
        "qk_conv_w": lax.dynamic_slice(tot[32:36, 0:2 * D_TOK], (0, chip * 384), (4, 384))[None],
        "head_norm_g": tot[40:44, 0:ML_HEAD_DIM][None],
    }
    grads, deltas, new_m, new_v = [], [], [], []
    for nme in names:
        if nme in updated:
            back = (lambda a: jnp.swapaxes(a, -1, -2)) if nme in ("ffn_w_gate", "ffn_w_up") else (lambda a: a)
            g, d, nm, nv = (back(a) for a in updated[nme])
        else:
            w, g = weights[nme], small_grads[nme]
            two = (math.prod(w.shape[:-1]), w.shape[-1])
            d, nm, nv = (a.reshape(w.shape) for a in adamw(w.reshape(two), g.reshape(two), ms[nme].reshape(two),
                                                           vs[nme].reshape(two), "adamw_" + nme))
        grads.append(g)
        deltas.append(d)
        new_m.append(nm)
        new_v.append(nv)
    return (loss_total, grad_x[None], *grads, *deltas, *new_m, *new_v)
```

```python
import functools
import math

import jax
import jax.numpy as jnp
from jax import lax
from jax.experimental import pallas as pl
from jax.experimental.pallas import tpu as pltpu

F32 = jnp.float32
BF16 = jnp.bfloat16
SDS = jax.ShapeDtypeStruct

D_MODEL = 1024
DEPTH = 2
N_MEM = 256
XA_HEADS = 4
XA_HEAD_DIM = 64
D_XA = 256
D_TOK = 768
ML_HEADS = 4
ML_HEAD_DIM = 192
ML_CHUNK = 64
D_FF = 2816
LN_EPS = 1e-5
ALPHA = (2.0 * DEPTH) ** 0.25
N_CHIPS = 4
N_DEV = 8
FF_SHARD = D_FF // N_CHIPS
GROUP = 256
NEG = -1e30

ADAM_LR = 0.001
ADAM_B1 = 0.9
ADAM_B2 = 0.999
ADAM_EPS = 1e-08
ADAM_WD = 0.01
ADAM_STEP = 10

VMEM_LIMIT = 56 * 1024 * 1024

NN = ((1,), (0,))
NT = ((1,), (1,))
TN = ((0,), (0,))
MESH = pl.DeviceIdType.MESH


def _dot(a, b, dims):
    return lax.dot_general(a, b, (dims, ((), ())), preferred_element_type=F32)


def _bdot(a, b, ca, cb):
    dims = (((ca,), (cb,)), ((0,), (0,)))
    ah, bh = a.astype(BF16), b.astype(BF16)
    al, bl = (a - ah.astype(F32)).astype(BF16), (b - bh.astype(F32)).astype(BF16)
    dot = functools.partial(lax.dot_general, dimension_numbers=dims, preferred_element_type=F32)
    return dot(ah, bh) + dot(al, bh) + dot(ah, bl)


def _sigmoid(x):
    return 1.0 / (1.0 + jnp.exp(-x))


def _params(sem, vmem=VMEM_LIMIT):
    return pltpu.CompilerParams(dimension_semantics=sem, vmem_limit_bytes=vmem)


def _tile(n, want):
    t = min(n, want)
    assert n % t == 0, (n, t)
    return t


def _layer_norm(z, gamma, beta):
    mu = jnp.mean(z, axis=-1, keepdims=True)
    zc = z - mu
    var = jnp.mean(zc * zc, axis=-1, keepdims=True)
    return zc * lax.rsqrt(var + LN_EPS) * gamma + beta


def _resident(shape):
    return pl.BlockSpec(shape, lambda *_: (0,) * len(shape), pipeline_mode=pl.Buffered(1))


def _group_block(G, want):
    return max(d for d in range(1, max(1, min(G, want)) + 1) if G % d == 0)


def ffn_fwd(xb, x, wg, wu, wd, gamma, beta):
    S, K = xb.shape
    G, _, N = wg.shape
    ts = _tile(S, 512)

    def body(xb_ref, x_ref, wg_ref, wu_ref, wd_ref, gm_ref, bt_ref, g_ref, u_ref, h_ref, z_ref, xn_ref, xnb_ref):
        j = pl.program_id(1)
        xv = xb_ref[...]
        g = _dot(xv, wg_ref[j], NN)
        u = _dot(xv, wu_ref[j], NN)
        h = (g * _sigmoid(g) * u).astype(BF16)
        g_ref[0] = g.astype(BF16)
        u_ref[0] = u.astype(BF16)
        h_ref[0] = h
        y = _dot(h, wd_ref[j], NN)

        @pl.when(j == 0)
        def _():
            z_ref[...] = y

        @pl.when(j > 0)
        def _():
            z_ref[...] += y

        @pl.when(j == G - 1)
        def _():
            z = ALPHA * x_ref[...] + 0.5 * z_ref[...]
            xn = _layer_norm(z, gm_ref[...], bt_ref[...])
            z_ref[...] = z
            xn_ref[...] = xn
            xnb_ref[...] = xn.astype(BF16)

    row = pl.BlockSpec((ts, K), lambda s, j: (s, 0))
    vec = pl.BlockSpec((1, K), lambda s, j: (0, 0))
    wspec = _resident((G, K, N))
    ospec = pl.BlockSpec((1, ts, N), lambda s, j: (j, s, 0))
    return pl.pallas_call(
        body, name="ffn_fwd", grid=(S // ts, G),
        in_specs=[row, row, wspec, wspec, _resident((G, N, K)), vec, vec],
        out_specs=[ospec, ospec, ospec, row, row, row],
        out_shape=[SDS((G, S, N), BF16), SDS((G, S, N), BF16), SDS((G, S, N), BF16),
                   SDS((S, K), F32), SDS((S, K), F32), SDS((S, K), BF16)],
        compiler_params=_params(("parallel", "arbitrary")),
    )(xb, x, wg, wu, wd, gamma, beta)


def proj(xb, w, name):
    S, K = xb.shape
    G, _, N = w.shape
    ts = _tile(S, 1024)
    gb = _group_block(G, 6)

    def body(x_ref, w_ref, y_ref):
        xv = x_ref[...]
        for j in range(gb):
            y_ref[j] = _dot(xv, w_ref[j], NN)

    return pl.pallas_call(
        body, name=name, grid=(S // ts, G // gb),
        in_specs=[pl.BlockSpec((ts, K), lambda s, g: (s, 0)), pl.BlockSpec((gb, K, N), lambda s, g: (g, 0, 0))],
        out_specs=pl.BlockSpec((gb, ts, N), lambda s, g: (g, s, 0)),
        out_shape=SDS((G, S, N), F32),
        compiler_params=_params(("parallel", "parallel")),
    )(xb, w)


def contract_ln(a, w, xres, gamma, beta, scale, name):
    G, S, Kg = a.shape
    N = w.shape[2]
    ts = _tile(S, 512)

    def body(a_ref, w_ref, x_ref, g_ref, b_ref, z_ref, xn_ref, xb_ref):
        acc = _dot(a_ref[0], w_ref[0], NN)
        for j in range(1, G):
            acc = acc + _dot(a_ref[j], w_ref[j], NN)
        z = ALPHA * x_ref[...] + scale * acc
        xn = _layer_norm(z, g_ref[...], b_ref[...])
        z_ref[...] = z
        xn_ref[...] = xn
        xb_ref[...] = xn.astype(BF16)

    row = pl.BlockSpec((ts, N), lambda s: (s, 0))
    vec = pl.BlockSpec((1, N), lambda s: (0, 0))
    return pl.pallas_call(
        body, name=name, grid=(S // ts,),
        in_specs=[pl.BlockSpec((G, ts, Kg), lambda s: (0, s, 0)), pl.BlockSpec((G, Kg, N), lambda s: (0, 0, 0)), row, vec, vec],
        out_specs=[row, row, row],
        out_shape=[SDS((S, N), F32), SDS((S, N), F32), SDS((S, N), BF16)],
        compiler_params=_params(("parallel",)),
    )(a, w, xres, gamma, beta)


def ln_bwd(dx, z, gamma, out_scale, name):
    S, N = dx.shape
    ts = _tile(S, 512)

    def body(dx_ref, z_ref, g_ref, dz_ref, dzb_ref, dg_ref, db_ref):
        @pl.when(pl.program_id(0) == 0)
        def _():
            dg_ref[...] = jnp.zeros_like(dg_ref)
            db_ref[...] = jnp.zeros_like(db_ref)

        z = z_ref[...]
        mu = jnp.mean(z, axis=-1, keepdims=True)
        zc = z - mu
        var = jnp.mean(zc * zc, axis=-1, keepdims=True)
        rstd = lax.rsqrt(var + LN_EPS)
        xhat = zc * rstd
        dxv = dx_ref[...]
        dg_ref[...] += jnp.sum(dxv * xhat, axis=0, keepdims=True)
        db_ref[...] += jnp.sum(dxv, axis=0, keepdims=True)
        dxh = dxv * g_ref[...]
        m1 = jnp.mean(dxh, axis=-1, keepdims=True)
        m2 = jnp.mean(dxh * xhat, axis=-1, keepdims=True)
        dz = rstd * (dxh - m1 - xhat * m2)
        dz_ref[...] = dz
        dzb_ref[...] = (out_scale * dz).astype(BF16)

    row = pl.BlockSpec((ts, N), lambda s: (s, 0))
    vec = pl.BlockSpec((1, N), lambda s: (0, 0))
    return pl.pallas_call(
        body, name=name, grid=(S // ts,),
        in_specs=[row, row, vec],
        out_specs=[row, row, vec, vec],
        out_shape=[SDS((S, N), F32), SDS((S, N), BF16), SDS((1, N), F32), SDS((1, N), F32)],
        compiler_params=_params(("arbitrary",)),
    )(dx, z, gamma)


def ffn_bwd(dyb, dz, wd, wg, wu, g1, u1):
    S, K = dyb.shape
    G, N, _ = wd.shape
    ts = _tile(S, 512)

    def body(dy_ref, dz_ref, wd_ref, wg_ref, wu_ref, g_ref, u_ref, dg_ref, du_ref, dx_ref):
        j = pl.program_id(1)
        dh = _dot(dy_ref[...], wd_ref[j], NT)
        g = g_ref[0].astype(F32)
        sig = _sigmoid(g)
        dg = (dh * u_ref[0].astype(F32) * (sig * (1.0 + g * (1.0 - sig)))).astype(BF16)
        du = (dh * (g * sig)).astype(BF16)
        dg_ref[0] = dg
        du_ref[0] = du
        part = _dot(dg, wg_ref[j], NT) + _dot(du, wu_ref[j], NT)

        @pl.when(j == 0)
        def _():
            dx_ref[...] = ALPHA * dz_ref[...] + part

        @pl.when(j > 0)
        def _():
            dx_ref[...] += part

    row = pl.BlockSpec((ts, K), lambda s, j: (s, 0))
    gspec = pl.BlockSpec((1, ts, N), lambda s, j: (j, s, 0))
    wspec = _resident((G, K, N))
    return pl.pallas_call(
        body, name="ffn_bwd", grid=(S // ts, G),
        in_specs=[row, row, _resident((G, N, K)), wspec, wspec, gspec, gspec],
        out_specs=[gspec, gspec, row],
        out_shape=[SDS((G, S, N), BF16), SDS((G, S, N), BF16), SDS((S, K), F32)],
        compiler_params=_params(("parallel", "arbitrary")),
    )(dyb, dz, wd, wg, wu, g1, u1)


def proj_t(dyb, w, name):
    S, N = dyb.shape
    G, Kg, _ = w.shape
    ts = _tile(S, 1024)

    def body(dy_ref, w_ref, da_ref):
        dy = dy_ref[...]
        for j in range(G):
            da_ref[j] = _dot(dy, w_ref[j], NT)

    return pl.pallas_call(
        body, name=name, grid=(S // ts,),
        in_specs=[pl.BlockSpec((ts, N), lambda s: (s, 0)), pl.BlockSpec((G, Kg, N), lambda s: (0, 0, 0))],
        out_specs=pl.BlockSpec((G, ts, Kg), lambda s: (0, s, 0)),
        out_shape=SDS((G, S, Kg), F32),
        compiler_params=_params(("parallel",)),
    )(dyb, w)


def contract_t(da, w, res, name):
    G, S, Ng = da.shape
    K = w.shape[1]
    ts = _tile(S, 512)
    gb = _group_block(G, 6)

    def body(da_ref, w_ref, r_ref, o_ref):
        g = pl.program_id(1)
        part = _dot(da_ref[0], w_ref[0], NT)
        for j in range(1, gb):
            part = part + _dot(da_ref[j], w_ref[j], NT)

        @pl.when(g == 0)
        def _():
            o_ref[...] = ALPHA * r_ref[...] + part

        @pl.when(g > 0)
        def _():
            o_ref[...] += part

    row = pl.BlockSpec((ts, K), lambda s, g: (s, 0))
    return pl.pallas_call(
        body, name=name, grid=(S // ts, G // gb),
        in_specs=[pl.BlockSpec((gb, ts, Ng), lambda s, g: (g, s, 0)), pl.BlockSpec((gb, K, Ng), lambda s, g: (g, 0, 0)), row],
        out_specs=row,
        out_shape=SDS((S, K), F32),
        compiler_params=_params(("parallel", "arbitrary")),
    )(da, w, res)


WGRAD_ACC_ELEMS = 6 * 1024 * 256


def wgrad(a, b, out_dtype, name):
    ga, gb = a.ndim == 3, b.ndim == 3
    G = a.shape[0] if ga else b.shape[0]
    S, K = a.shape[-2:]
    N = b.shape[-1]
    ts = _tile(S, 1024)
    ns = S // ts
    ng = _group_block(G, WGRAD_ACC_ELEMS // (K * N))

    def body(a_ref, b_ref, o_ref, acc):
        s = pl.program_id(1)

        @pl.when(s == 0)
        def _():
            acc[...] = jnp.zeros_like(acc)

        for j in range(ng):
            acc[j] += _dot(a_ref[j] if ga else a_ref[...], b_ref[j] if gb else b_ref[...], TN)

        @pl.when(s == ns - 1)
        def _():
            o_ref[...] = acc[...].astype(out_dtype)

    aspec = pl.BlockSpec((ng, ts, K), lambda g, s: (g, s, 0)) if ga else pl.BlockSpec((ts, K), lambda g, s: (s, 0))
    bspec = pl.BlockSpec((ng, ts, N), lambda g, s: (g, s, 0)) if gb else pl.BlockSpec((ts, N), lambda g, s: (s, 0))
    return pl.pallas_call(
        body, name=name, grid=(G // ng, ns),
        in_specs=[aspec, bspec],
        out_specs=pl.BlockSpec((ng, K, N), lambda g, s: (g, 0, 0)),
        out_shape=SDS((G, K, N), out_dtype),
        scratch_shapes=[pltpu.VMEM((ng, K, N), F32)],
        compiler_params=_params(("parallel", "arbitrary")),
    )(a, b)


def loss_grad(xn, tgt):
    S, N = xn.shape
    ts = _tile(S, 512)

    def body(x_ref, t_ref, l_ref, dx_ref):
        @pl.when(pl.program_id(0) == 0)
        def _():
            l_ref[...] = jnp.zeros_like(l_ref)

        e = x_ref[...] - t_ref[...]
        dx_ref[...] = e * (1.0 / N)
        l_ref[...] += 0.5 * jnp.sum(jnp.mean(e * e, axis=-1, keepdims=True), axis=0, keepdims=True)

    row = pl.BlockSpec((ts, N), lambda s: (s, 0))
    return pl.pallas_call(
        body, name="loss_grad", grid=(S // ts,),
        in_specs=[row, row],
        out_specs=[pl.BlockSpec((1, 1), lambda s: (0, 0)), row],
        out_shape=[SDS((1, 1), F32), SDS((S, N), F32)],
        compiler_params=_params(("arbitrary",)),
    )(xn, tgt)


def _shift_down(x, k):
    if k == 0:
        return x
    rows = lax.broadcasted_iota(jnp.int32, x.shape, 0)
    return jnp.where(rows >= k, pltpu.roll(x, k, 0), 0.0)


def _shift_up(x, k):
    if k == 0:
        return x
    n = x.shape[0]
    rows = lax.broadcasted_iota(jnp.int32, x.shape, 0)
    return jnp.where(rows < n - k, pltpu.roll(x, n - k, 0), 0.0)


LANES = 128


def conv_mixer_fwd(u, cw):
    _, S, _ = u.shape
    nh = GROUP // LANES

    def body(b_ref, c_ref, x_ref, w_ref, o_ref):
        p = c_ref[0] * x_ref[0]
        w = w_ref[0]
        conv = w[2:3] * p + w[1:2] * _shift_down(p, 1) + w[0:1] * _shift_down(p, 2)
        o_ref[0] = (b_ref[0] * conv).astype(BF16)

    def uspec(off):
        return pl.BlockSpec((1, S, LANES), lambda g, h: (g + off, 0, h))

    return pl.pallas_call(
        body, name="conv_mixer_fwd", grid=(3, nh),
        in_specs=[uspec(0), uspec(3), uspec(6), pl.BlockSpec((1, 8, LANES), lambda g, h: (g, 0, h))],
        out_specs=pl.BlockSpec((1, S, LANES), lambda g, h: (g, 0, h)),
        out_shape=SDS((3, S, GROUP), BF16),
        compiler_params=_params(("parallel", "parallel")),
    )(u, u, u, cw)


def conv_mixer_bwd(u, cw, dm):
    _, S, _ = u.shape
    nh = GROUP // LANES

    def body(b_ref, c_ref, x_ref, w_ref, d_ref, db_ref, dc_ref, dx_ref, dw_ref):
        cg, xi = c_ref[0], x_ref[0]
        p = cg * xi
        p1, p2 = _shift_down(p, 1), _shift_down(p, 2)
        w = w_ref[0]
        conv = w[2:3] * p + w[1:2] * p1 + w[0:1] * p2
        dt = d_ref[0]
        db_ref[0] = (dt * conv).astype(BF16)
        dcv = dt * b_ref[0]
        dp = w[2:3] * dcv + w[1:2] * _shift_up(dcv, 1) + w[0:1] * _shift_up(dcv, 2)
        dc_ref[0] = (dp * xi).astype(BF16)
        dx_ref[0] = (dp * cg).astype(BF16)
        dw = jnp.concatenate([jnp.sum(dcv * p2, axis=0, keepdims=True), jnp.sum(dcv * p1, axis=0, keepdims=True),
                              jnp.sum(dcv * p, axis=0, keepdims=True), jnp.zeros((5, LANES), F32)], axis=0)
        dw_ref[0] = dw

    def uspec(off):
        return pl.BlockSpec((1, S, LANES), lambda g, h: (g + off, 0, h))

    ospec = pl.BlockSpec((1, S, LANES), lambda g, h: (g, 0, h))
    wspec = pl.BlockSpec((1, 8, LANES), lambda g, h: (g, 0, h))
    return pl.pallas_call(
        body, name="conv_mixer_bwd", grid=(3, nh),
        in_specs=[uspec(0), uspec(3), uspec(6), wspec, ospec],
        out_specs=[ospec, ospec, ospec, wspec],
        out_shape=[SDS((3, S, GROUP), BF16)] * 3 + [SDS((3, 8, GROUP), F32)],
        compiler_params=_params(("parallel", "parallel")),
    )(u, u, u, cw, dm)


def qk_conv_fwd(u, qw):
    _, S, _ = u.shape
    nh = GROUP // LANES

    def body(u_ref, w_ref, o_ref):
        x = u_ref[0]
        w = w_ref[0]
        pre = w[3:4] * x + w[2:3] * _shift_down(x, 1) + w[1:2] * _shift_down(x, 2) + w[0:1] * _shift_down(x, 3)
        o_ref[0] = pre * _sigmoid(pre)

    spec = pl.BlockSpec((1, S, LANES), lambda g, h: (g, 0, h))
    return pl.pallas_call(
        body, name="qk_conv_fwd", grid=(8, nh),
        in_specs=[spec, pl.BlockSpec((1, 8, LANES), lambda g, h: (g, 0, h))],
        out_specs=spec,
        out_shape=SDS((8, S, GROUP), F32),
        compiler_params=_params(("parallel", "parallel")),
    )(u, qw)


def qk_conv_bwd(u, qw, dqk):
    _, S, _ = u.shape
    nh = GROUP // LANES

    def body(u_ref, w_ref, d_ref, du_ref, dw_ref):
        x = u_ref[0]
        w = w_ref[0]
        x1, x2, x3 = _shift_down(x, 1), _shift_down(x, 2), _shift_down(x, 3)
        pre = w[3:4] * x + w[2:3] * x1 + w[1:2] * x2 + w[0:1] * x3
        sig = _sigmoid(pre)
        dpre = d_ref[0] * (sig * (1.0 + pre * (1.0 - sig)))
        du = w[3:4] * dpre + w[2:3] * _shift_up(dpre, 1) + w[1:2] * _shift_up(dpre, 2) + w[0:1] * _shift_up(dpre, 3)
        du_ref[0] = du.astype(BF16)
        dw = jnp.concatenate([jnp.sum(dpre * x3, axis=0, keepdims=True), jnp.sum(dpre * x2, axis=0, keepdims=True),
                              jnp.sum(dpre * x1, axis=0, keepdims=True), jnp.sum(dpre * x, axis=0, keepdims=True),
                              jnp.zeros((4, LANES), F32)], axis=0)
        dw_ref[0] = dw

    spec = pl.BlockSpec((1, S, LANES), lambda g, h: (g, 0, h))
    wspec = pl.BlockSpec((1, 8, LANES), lambda g, h: (g, 0, h))
    return pl.pallas_call(
        body, name="qk_conv_bwd", grid=(8, nh),
        in_specs=[spec, wspec, spec],
        out_specs=[spec, wspec],
        out_shape=[SDS((8, S, GROUP), BF16), SDS((8, 8, GROUP), F32)],
        compiler_params=_params(("parallel", "parallel")),
    )(u, qw, dqk)


def _head_masks():
    lane = lax.broadcasted_iota(jnp.int32, (1, D_XA), 1)
    return [(lane >= h * XA_HEAD_DIM) & (lane < (h + 1) * XA_HEAD_DIM) for h in range(XA_HEADS)]


def xattn_fwd(u, qg, kv):
    _, S, _ = u.shape
    ts = _tile(S, 512)
    scale = XA_HEAD_DIM ** -0.5

    def body(q_ref, kv_ref, o_ref):
        q = q_ref[0]
        k = kv_ref[0].astype(BF16)
        v = kv_ref[1]
        o = jnp.zeros((ts, D_XA), F32)
        for m in _head_masks():
            s = _dot(jnp.where(m, q, 0.0).astype(BF16), k, NT) * scale
            s = s - jnp.max(s, axis=-1, keepdims=True)
            e = jnp.exp(s)
            p = e / jnp.sum(e, axis=-1, keepdims=True)
            o = o + _dot(p.astype(BF16), jnp.where(m, v, 0.0).astype(BF16), NN)
        o_ref[0] = o.astype(BF16)

    return pl.pallas_call(
        body, name="xattn_fwd", grid=(S // ts,),
        in_specs=[pl.BlockSpec((1, ts, GROUP), lambda s: (qg, s, 0)), pl.BlockSpec((2, N_MEM, GROUP), lambda s: (0, 0, 0))],
        out_specs=pl.BlockSpec((1, ts, GROUP), lambda s: (0, s, 0)),
        out_shape=SDS((1, S, GROUP), BF16),
        compiler_params=_params(("parallel",)),
    )(u, kv)


def xattn_bwd(u, qg, kv, dm, dg):
    _, S, _ = u.shape
    ts = _tile(S, 512)
    scale = XA_HEAD_DIM ** -0.5

    def body(q_ref, kv_ref, do_ref, dq_ref, dkv_ref):
        @pl.when(pl.program_id(0) == 0)
        def _():
            dkv_ref[...] = jnp.zeros_like(dkv_ref)

        q = q_ref[0]
        k = kv_ref[0]
        v = kv_ref[1]
        kb = k.astype(BF16)
        do = do_ref[0]
        dq = jnp.zeros((ts, D_XA), F32)
        dk = jnp.zeros((N_MEM, D_XA), F32)
        dv = jnp.zeros((N_MEM, D_XA), F32)
        for m in _head_masks():
            qm = jnp.where(m, q, 0.0).astype(BF16)
            s = _dot(qm, kb, NT) * scale
            s = s - jnp.max(s, axis=-1, keepdims=True)
            e = jnp.exp(s)
            p = e / jnp.sum(e, axis=-1, keepdims=True)
            dom = jnp.where(m, do, 0.0).astype(BF16)
            dp = _dot(dom, jnp.where(m, v, 0.0).astype(BF16), NT)
            ds = (p * (dp - jnp.sum(dp * p, axis=-1, keepdims=True)) * scale).astype(BF16)
            dq = dq + _dot(ds, jnp.where(m, k, 0.0).astype(BF16), NN)
            dk = dk + _dot(ds, qm, TN)
            dv = dv + _dot(p.astype(BF16), dom, TN)
        dq_ref[0] = dq.astype(BF16)
        dkv_ref[0] += dk
        dkv_ref[1] += dv

    return pl.pallas_call(
        body, name="xattn_bwd", grid=(S // ts,),
        in_specs=[pl.BlockSpec((1, ts, GROUP), lambda s: (qg, s, 0)), pl.BlockSpec((2, N_MEM, GROUP), lambda s: (0, 0, 0)),
                  pl.BlockSpec((1, ts, GROUP), lambda s: (dg, s, 0))],
        out_specs=[pl.BlockSpec((1, ts, GROUP), lambda s: (0, s, 0)), pl.BlockSpec((2, N_MEM, GROUP), lambda s: (0, 0, 0))],
        out_shape=[SDS((1, S, GROUP), BF16), SDS((2, N_MEM, GROUP), F32)],
        compiler_params=_params(("arbitrary",)),
    )(u, kv, dm)


ML_BLOCK_CHUNKS = 4
H4 = ML_HEADS
L = ML_CHUNK
NLANE = ML_HEAD_DIM


def _chunk_consts():
    r = lax.broadcasted_iota(jnp.int32, (1, L, L), 1)
    c = lax.broadcasted_iota(jnp.int32, (1, L, L), 2)
    return r >= c, r <= c, r == c


def _gate_cols(gb):
    lane = lax.broadcasted_iota(jnp.int32, gb.shape, 1)
    li = jnp.stack([jnp.sum(jnp.where(lane == h, gb, 0.0), axis=1, keepdims=True) for h in range(H4)])
    gf = jnp.stack([jnp.sum(jnp.where(lane == H4 + h, gb, 0.0), axis=1, keepdims=True) for h in range(H4)])
    return li, gf


def _log_sigmoid(x):
    return jnp.minimum(x, 0.0) - jnp.log(1.0 + jnp.exp(-jnp.abs(x)))


def _chunk_forward(q, k, v_aug, li_col, lf_col, c_prev, m_prev):
    tri, tri_t, eye = _chunk_consts()
    lf_row = jnp.sum(jnp.where(eye, lf_col, 0.0), axis=1, keepdims=True)
    li_row = jnp.sum(jnp.where(eye, li_col, 0.0), axis=1, keepdims=True)
    bcum_col = jnp.sum(jnp.where(tri, lf_row, 0.0), axis=2, keepdims=True)
    bcum_row = jnp.sum(jnp.where(tri_t, lf_col, 0.0), axis=1, keepdims=True)
    log_d = jnp.where(tri, bcum_col - bcum_row + li_row, NEG)
    log_inter = bcum_col + m_prev
    m_t = jnp.maximum(log_inter, jnp.max(log_d, axis=2, keepdims=True))
    w_intra = jnp.exp(log_d - m_t)
    w_inter = jnp.exp(log_inter - m_t)
    sc = _bdot(q, k, 2, 2) * w_intra
    qc = _bdot(q, c_prev, 2, 1)
    num = _bdot(sc, v_aug, 2, 1) + w_inter * qc
    lane = lax.broadcasted_iota(jnp.int32, num.shape, 2)
    den = jnp.sum(jnp.where(lane == NLANE, num, 0.0), axis=2, keepdims=True)
    e_m = jnp.exp(-m_t)
    b_last = jnp.sum(lf_row, axis=2, keepdims=True)
    log_w = b_last - bcum_col + li_col
    m_new = jnp.maximum(b_last + m_prev, jnp.max(log_w, axis=1, keepdims=True))
    w_k = jnp.exp(log_w - m_new)
    decay = jnp.exp(b_last + m_prev - m_new)
    return dict(w_intra=w_intra, w_inter=w_inter, sc=sc, qc=qc, num=num, den=den, e_m=e_m, lane=lane,
                w_k=w_k, decay=decay, m_new=m_new)


def mlstm_fwd(qk, u, bg):
    _, S, _ = qk.shape
    nc = S // L
    cb = min(ML_BLOCK_CHUNKS, nc)
    rows = cb * L
    kscale = ML_HEAD_DIM ** -0.5

    def body(qk_ref, v_ref, g_ref, bg_ref, h_ref, cst_ref, mst_ref, c_sc, m_sc):
        @pl.when(pl.program_id(0) == 0)
        def _():
            c_sc[...] = jnp.zeros_like(c_sc)
            m_sc[...] = jnp.zeros_like(m_sc)

        for c in range(cb):
            sl = pl.ds(c * L, L)
            q = qk_ref[0:H4, sl, :]
            k = qk_ref[H4:2 * H4, sl, :] * kscale
            v = v_ref[:, sl, :]
            lane = lax.broadcasted_iota(jnp.int32, v.shape, 2)
            v_aug = jnp.where(lane == NLANE, 1.0, v)
            li_col, gf = _gate_cols(g_ref[0, sl, :] + bg_ref[...])
            lf_col = _log_sigmoid(gf)
            c_prev = c_sc[...]
            m_prev = m_sc[...]
            f = _chunk_forward(q, k, v_aug, li_col, lf_col, c_prev, m_prev)
            r = 1.0 / jnp.maximum(jnp.abs(f["den"]), f["e_m"])
            h_ref[:, sl, :] = jnp.where(lane < NLANE, f["num"] * r, 0.0)
            cst_ref[c] = c_prev
            mst_ref[c] = jnp.broadcast_to(m_prev, (H4, 1, LANES))
            c_sc[...] = f["decay"] * c_prev + _bdot(k * f["w_k"], v_aug, 1, 1)
            m_sc[...] = f["m_new"]

    def hspec(blk):
        return pl.BlockSpec((H4, rows, GROUP), lambda i: (blk, i, 0))

    return pl.pallas_call(
        body, name="mlstm_fwd", grid=(nc // cb,),
        in_specs=[pl.BlockSpec((2 * H4, rows, GROUP), lambda i: (0, i, 0)), hspec(2),
                  pl.BlockSpec((1, rows, GROUP), lambda i: (17, i, 0)), pl.BlockSpec((1, GROUP), lambda i: (0, 0))],
        out_specs=[hspec(0), pl.BlockSpec((cb, H4, GROUP, GROUP), lambda i: (i, 0, 0, 0)),
                   pl.BlockSpec((cb, H4, 1, LANES), lambda i: (i, 0, 0, 0))],
        out_shape=[SDS((H4, S, GROUP), F32), SDS((nc, H4, GROUP, GROUP), F32), SDS((nc, H4, 1, LANES), F32)],
        scratch_shapes=[pltpu.VMEM((H4, GROUP, GROUP), F32), pltpu.VMEM((H4, 1, 1), F32)],
        compiler_params=_params(("arbitrary",)),
    )(qk, u, u, bg)


def mlstm_bwd(qk, u, bg, cst, mst, dh):
    _, S, _ = qk.shape
    nc = S // L
    cb = min(ML_BLOCK_CHUNKS, nc)
    rows = cb * L
    nb = nc // cb
    kscale = ML_HEAD_DIM ** -0.5

    def body(qk_ref, v_ref, g_ref, bg_ref, cst_ref, mst_ref, dh_ref, dqk_ref, dv_ref, dg_ref, dbg_ref, dc_sc):
        @pl.when(pl.program_id(0) == 0)
        def _():
            dc_sc[...] = jnp.zeros_like(dc_sc)
            dbg_ref[...] = jnp.zeros_like(dbg_ref)

        tri, tri_t, eye = _chunk_consts()
        for c in reversed(range(cb)):
            sl = pl.ds(c * L, L)
            q = qk_ref[0:H4, sl, :]
            k = qk_ref[H4:2 * H4, sl, :] * kscale
            v = v_ref[:, sl, :]
            lane = lax.broadcasted_iota(jnp.int32, v.shape, 2)
            v_aug = jnp.where(lane == NLANE, 1.0, v)
            li_col, gf = _gate_cols(g_ref[0, sl, :] + bg_ref[...])
            lf_col = _log_sigmoid(gf)
            c_prev = cst_ref[c]
            m_prev = mst_ref[c][:, :, 0:1]
            f = _chunk_forward(q, k, v_aug, li_col, lf_col, c_prev, m_prev)
            w_intra, w_inter, sc, num, den, e_m = f["w_intra"], f["w_inter"], f["sc"], f["num"], f["den"], f["e_m"]
            absd = jnp.abs(den)
            r = 1.0 / jnp.maximum(absd, e_m)
            dhv = dh_ref[:, sl, :]
            s1 = jnp.sum(jnp.where(lane < NLANE, dhv * num, 0.0), axis=2, keepdims=True)
            dden = jnp.where(absd > e_m, -s1 * r * r * jnp.sign(den), 0.0)
            dnum = jnp.where(lane == NLANE, dden, jnp.where(lane < NLANE, dhv * r, 0.0))
            dsc = _bdot(dnum, v_aug, 2, 2)
            dv = _bdot(sc, dnum, 1, 1)
            gmat = dsc * sc
            dqk = dsc * w_intra
            dq = _bdot(dqk, k, 2, 1) + w_inter * _bdot(dnum, c_prev, 2, 2)
            dk = _bdot(dqk, q, 1, 1)
            dc_prev = _bdot(q * w_inter, dnum, 1, 1)
            dlog_inter = jnp.sum(dnum * f["qc"], axis=2, keepdims=True) * w_inter
            dbcum_col = dlog_inter + jnp.sum(gmat, axis=2, keepdims=True)
            g_row = jnp.sum(gmat, axis=1, keepdims=True)
            dcn = dc_sc[...]
            w_k, decay = f["w_k"], f["decay"]
            kw = k * w_k
            dc_prev = dc_prev + decay * dcn
            db_last = jnp.sum(jnp.sum(dcn * c_prev, axis=2, keepdims=True), axis=1, keepdims=True) * decay
            dkw = _bdot(v_aug, dcn, 2, 2)
            dv = dv + _bdot(kw, dcn, 2, 1)
            dk = dk + dkw * w_k
            dlogw = jnp.sum(dkw * k, axis=2, keepdims=True) * w_k
            db_last = db_last + jnp.sum(dlogw, axis=1, keepdims=True)
            dbcum_col = dbcum_col - dlogw
            rowi = lax.broadcasted_iota(jnp.int32, (1, L, 1), 1)
            dbcum_col = dbcum_col + jnp.where(rowi == L - 1, db_last, 0.0)
            dbcum_row = jnp.sum(jnp.where(eye, dbcum_col, 0.0), axis=1, keepdims=True) - g_row
            dlf_col = jnp.sum(jnp.where(tri_t, dbcum_row, 0.0), axis=2, keepdims=True)
            dli_col = dlogw + jnp.sum(jnp.where(eye, g_row, 0.0), axis=2, keepdims=True)
            dgf_col = dlf_col * _sigmoid(-gf)
            lane_g = lax.broadcasted_iota(jnp.int32, (L, GROUP), 1)
            dg = jnp.zeros((L, GROUP), F32)
            for h in range(H4):
                dg = dg + jnp.where(lane_g == h, dli_col[h], 0.0) + jnp.where(lane_g == H4 + h, dgf_col[h], 0.0)
            dqk_ref[0:H4, sl, :] = dq
            dqk_ref[H4:2 * H4, sl, :] = dk * kscale
            dv_ref[:, sl, :] = jnp.where(lane < NLANE, dv, 0.0).astype(BF16)
            dg_ref[0, sl, :] = dg.astype(BF16)
            dbg_ref[...] += jnp.sum(dg, axis=0, keepdims=True)
            dc_sc[...] = dc_prev

    def hspec(blk):
        return pl.BlockSpec((H4, rows, GROUP), lambda i: (blk, nb - 1 - i, 0))

    gspec = pl.BlockSpec((1, rows, GROUP), lambda i: (17, nb - 1 - i, 0))
    qkspec = pl.BlockSpec((2 * H4, rows, GROUP), lambda i: (0, nb - 1 - i, 0))
    return pl.pallas_call(
        body, name="mlstm_bwd", grid=(nb,),
        in_specs=[qkspec, hspec(2), gspec, pl.BlockSpec((1, GROUP), lambda i: (0, 0)),
                  pl.BlockSpec((cb, H4, GROUP, GROUP), lambda i: (nb - 1 - i, 0, 0, 0)),
                  pl.BlockSpec((cb, H4, 1, LANES), lambda i: (nb - 1 - i, 0, 0, 0)), hspec(0)],
        out_specs=[qkspec, hspec(0), pl.BlockSpec((1, rows, GROUP), lambda i: (0, nb - 1 - i, 0)),
                   pl.BlockSpec((1, GROUP), lambda i: (0, 0))],
        out_shape=[SDS((2 * H4, S, GROUP), F32), SDS((H4, S, GROUP), BF16),
                   SDS((1, S, GROUP), BF16), SDS((1, GROUP), F32)],
        scratch_shapes=[pltpu.VMEM((H4, GROUP, GROUP), F32)],
        compiler_params=_params(("arbitrary",)),
    )(qk, u, u, bg, cst, mst, dh)


def head_norm_fwd(hm, u, hg):
    _, S, _ = hm.shape
    ts = _tile(S, 512)

    def body(h_ref, o_ref, g_ref, t_ref):
        h = h_ref[0]
        lane = lax.broadcasted_iota(jnp.int32, h.shape, 1)
        valid = lane < ML_HEAD_DIM
        mu = jnp.sum(h, axis=-1, keepdims=True) * (1.0 / ML_HEAD_DIM)
        hc = jnp.where(valid, h - mu, 0.0)
        var = jnp.sum(hc * hc, axis=-1, keepdims=True) * (1.0 / ML_HEAD_DIM)
        hn = hc * lax.rsqrt(var + LN_EPS) * g_ref[0]
        t_ref[0] = (_sigmoid(o_ref[0]) * hn).astype(BF16)

    return pl.pallas_call(
        body, name="head_norm_fwd", grid=(H4, S // ts),
        in_specs=[pl.BlockSpec((1, ts, GROUP), lambda h, s: (h, s, 0)), pl.BlockSpec((1, ts, GROUP), lambda h, s: (12 + h, s, 0)),
                  pl.BlockSpec((1, 1, GROUP), lambda h, s: (h, 0, 0))],
        out_specs=pl.BlockSpec((1, ts, GROUP), lambda h, s: (h, s, 0)),
        out_shape=SDS((H4, S, GROUP), BF16),
        compiler_params=_params(("parallel", "parallel")),
    )(hm, u, hg)


def head_norm_bwd(hm, u, hg, dm):
    _, S, _ = hm.shape
    ts = _tile(S, 512)

    def body(h_ref, o_ref, g_ref, d_ref, dh_ref, do_ref, dg_ref):
        @pl.when(pl.program_id(1) == 0)
        def _():
            dg_ref[...] = jnp.zeros_like(dg_ref)

        h = h_ref[0]
        lane = lax.broadcasted_iota(jnp.int32, h.shape, 1)
        valid = lane < ML_HEAD_DIM
        inv = 1.0 / ML_HEAD_DIM
        mu = jnp.sum(h, axis=-1, keepdims=True) * inv
        hc = jnp.where(valid, h - mu, 0.0)
        var = jnp.sum(hc * hc, axis=-1, keepdims=True) * inv
        rstd = lax.rsqrt(var + LN_EPS)
        xhat = hc * rstd
        g = g_ref[0]
        sig = _sigmoid(o_ref[0])
        dt = jnp.where(valid, d_ref[0], 0.0)
        do_ref[0] = (dt * xhat * g * sig * (1.0 - sig)).astype(BF16)
        dhn = dt * sig
        dg_ref[0] += jnp.sum(dhn * xhat, axis=0, keepdims=True)
        dxh = dhn * g
        m1 = jnp.sum(dxh, axis=-1, keepdims=True) * inv
        m2 = jnp.sum(dxh * xhat, axis=-1, keepdims=True) * inv
        dh_ref[0] = jnp.where(valid, rstd * (dxh - m1 - xhat * m2), 0.0)

    spec = pl.BlockSpec((1, ts, GROUP), lambda h, s: (h, s, 0))
    gspec = pl.BlockSpec((1, 1, GROUP), lambda h, s: (h, 0, 0))
    return pl.pallas_call(
        body, name="head_norm_bwd", grid=(H4, S // ts),
        in_specs=[spec, pl.BlockSpec((1, ts, GROUP), lambda h, s: (12 + h, s, 0)), gspec, spec],
        out_specs=[spec, spec, gspec],
        out_shape=[SDS((H4, S, GROUP), F32), SDS((H4, S, GROUP), BF16), SDS((H4, 1, GROUP), F32)],
        compiler_params=_params(("parallel", "arbitrary")),
    )(hm, u, hg, dm)


def _adamw_math(w, g, m, v):
    c1 = 1.0 / (1.0 - ADAM_B1 ** ADAM_STEP)
    c2 = 1.0 / (1.0 - ADAM_B2 ** ADAM_STEP)
    nm = ADAM_B1 * m + (1.0 - ADAM_B1) * g
    nv = ADAM_B2 * v + (1.0 - ADAM_B2) * (g * g)
    return -ADAM_LR * ((nm * c1) / (jnp.sqrt(nv * c2) + ADAM_EPS) + ADAM_WD * w), nm, nv


def _row_tile(R, cap=512):
    return R if R <= cap else max(d for d in range(8, cap + 1, 8) if R % d == 0)


def adamw_into(w, m, v, g, outs, idx, name):
    R, C = g.shape
    tr = _row_tile(R)
    lead = (0,) * len(idx)

    def body(w_ref, m_ref, v_ref, g_ref, *rest):
        go_ref, d_ref, nm_ref, nv_ref, token = rest[-5:]
        token[...] = jnp.zeros_like(token)
        gv = g_ref[...]
        d, nm, nv = _adamw_math(w_ref[lead], gv, m_ref[lead], v_ref[lead])
        go_ref[lead] = gv
        d_ref[lead] = d
        nm_ref[lead] = nm
        nv_ref[lead] = nv

    blk = pl.BlockSpec((1,) * len(idx) + (tr, C), lambda r: idx + (r, 0))
    in_specs, args, aliases = [blk, blk, blk, pl.BlockSpec((tr, C), lambda r: (r, 0))], [w, m, v, g], {}
    if outs is not None:
        in_specs += [pl.BlockSpec(memory_space=pl.ANY)] * 4
        args += list(outs)
        aliases = {4 + i: i for i in range(4)}
    out = pl.pallas_call(
        body, name=name, grid=(R // tr,),
        in_specs=in_specs, out_specs=[blk] * 4 + [pl.BlockSpec((8, LANES), lambda r: (0, 0))],
        out_shape=[SDS(w.shape, F32)] * 4 + [SDS((8, LANES), F32)],
        input_output_aliases=aliases, compiler_params=_params(("arbitrary",)),
    )(*args)
    return out[:4], out[4]


def adamw(w, g, m, v, name):
    R, C = w.shape
    tr = _row_tile(R)

    def body(w_ref, g_ref, m_ref, v_ref, d_ref, nm_ref, nv_ref):
        d_ref[...], nm_ref[...], nv_ref[...] = _adamw_math(w_ref[...], g_ref[...], m_ref[...], v_ref[...])

    spec = pl.BlockSpec((tr, C), lambda i: (i, 0))
    return pl.pallas_call(
        body, name=name, grid=(R // tr,),
        in_specs=[spec] * 4, out_specs=[spec] * 3,
        out_shape=[SDS((R, C), F32)] * 3,
        compiler_params=_params(("parallel",)),
    )(w, g, m, v)


HBM = pl.BlockSpec(memory_space=pl.ANY)
ROW_SPLIT = 4


def _position():
    x, y, c = lax.axis_index("x"), lax.axis_index("y"), lax.axis_index("c")
    return x, y, c, [(1 - x, y), (x, 1 - y), (1 - x, 1 - y)]


def _unique(items):
    arrays = []
    for a, _ in items:
        if not any(a is b for b in arrays):
            arrays.append(a)
    return arrays, [next(i for i, b in enumerate(arrays) if b is a) for a, _ in items]


def place_own(items, me):
    arrays, src_of = _unique(items)
    n = len(items)
    shapes = [a.shape[len(p):] for a, p in items]

    def body(me_ref, *refs):
        for t in range(n):
            refs[n + t][...] = jnp.zeros_like(refs[n + t])
            refs[n + t][me_ref[0]] = refs[t][(0,) * len(items[t][1])]

    in_specs, out_specs = [], []
    for (a, p), shp in zip(items, shapes):
        blk = shp[:-2] + (shp[-2] // ROW_SPLIT, shp[-1])
        lead = (0,) * (len(shp) - 2)
        in_specs.append(pl.BlockSpec((1,) * len(p) + blk, functools.partial(lambda r, me_ref, p, lead: p + lead + (r, 0), p=p, lead=lead)))
        out_specs.append(pl.BlockSpec((N_CHIPS,) + blk, functools.partial(lambda r, me_ref, lead: (0,) + lead + (r, 0), lead=lead)))
    return pl.pallas_call(
        body, name="place_own",
        grid_spec=pltpu.PrefetchScalarGridSpec(num_scalar_prefetch=1, grid=(ROW_SPLIT,), in_specs=in_specs, out_specs=out_specs),
        out_shape=[SDS((N_CHIPS,) + tuple(shp), a.dtype) for shp, (a, _) in zip(shapes, items)],
        compiler_params=_params(("parallel",)),
    )(me, *[arrays[i] for i in src_of])


SEM = pl.BlockSpec(memory_space=pltpu.SEMAPHORE)
IN_HBM = pl.BlockSpec(memory_space=pltpu.HBM)
DATAFLOW = pltpu.SideEffectType.DATAFLOW_SIDE_EFFECTING


def split_start(bufs, plan, n_copies, after, name):
    n = len(bufs)

    def body(*refs):
        send, recv, token = refs[n + 1], refs[n + 2], refs[-1]
        x, y, c, chips = _position()
        for k, (src, dst, dev) in enumerate(plan(refs[:n], x, y, c, chips)):
            pltpu.make_async_remote_copy(src_ref=src, dst_ref=dst, send_sem=send.at[k], recv_sem=recv.at[k],
                                         device_id=dev, device_id_type=MESH).start()
        token[...] = jnp.zeros_like(token)

    out = pl.pallas_call(
        body, name=name,
        out_shape=(pltpu.SemaphoreType.DMA((n_copies,)), pltpu.SemaphoreType.DMA((n_copies,)),
                   *[pltpu.HBM(b.shape, b.dtype) for b in bufs], SDS((8, LANES), F32)),
        in_specs=[IN_HBM] * n + [pl.BlockSpec(memory_space=pl.ANY)],
        out_specs=(SEM, SEM, *[IN_HBM] * n, pl.BlockSpec(memory_space=pltpu.VMEM)),
        input_output_aliases={i: 2 + i for i in range(n)},
        compiler_params=pltpu.CompilerParams(has_side_effects=DATAFLOW),
    )(*[pltpu.with_memory_space_constraint(b, pltpu.HBM) for b in bufs], after)
    return out[0], out[1], list(out[2:2 + n]), out[-1]


def split_wait(send, recv, bufs, plan, after, name):
    n = len(bufs)

    def body(*refs):
        send_ref, recv_ref = refs[n], refs[n + 1]
        x, y, c, chips = _position()
        for k, (src, dst, dev) in enumerate(plan(refs[:n], x, y, c, chips)):
            cp = pltpu.make_async_remote_copy(src_ref=src, dst_ref=dst, send_sem=send_ref.at[k], recv_sem=recv_ref.at[k],
                                              device_id=dev, device_id_type=MESH)
            cp.wait_send()
            cp.wait_recv()

    return list(pl.pallas_call(
        body, name=name, out_shape=tuple(pltpu.HBM(b.shape, b.dtype) for b in bufs),
        in_specs=[IN_HBM] * n + [SEM, SEM, pl.BlockSpec(memory_space=pl.ANY)], out_specs=tuple([IN_HBM] * n),
        input_output_aliases={i: i for i in range(n)},
        compiler_params=pltpu.CompilerParams(has_side_effects=DATAFLOW),
    )(*bufs, send, recv, after))


def _gather_plan(shapes, landing):
    n = len(shapes)

    def plan(refs, x, y, c, chips):
        out = []
        for t in range(n):
            half = shapes[t][0] // 2
            rows = pl.ds(c * half, half)
            for cx, cy in chips:
                slot = 2 * cx + cy if landing else 2 * x + y
                out.append((refs[t].at[rows], refs[n + t].at[slot, rows], (cx, cy, c)))
        return out

    return plan


def gather_start(shards, placed, after, name):
    shapes = [s.shape for s in shards]
    send, recv, bufs, token = split_start(shards + placed, _gather_plan(shapes, False), 3 * len(shards), after, name)
    return (send, recv, bufs, shapes), token


def gather_wait(state, after, name):
    send, recv, bufs, shapes = state
    return split_wait(send, recv, bufs, _gather_plan(shapes, True), after, name)[len(shapes):]


def gather_pass_on(placed, shapes, name):
    n = len(placed)

    def body(*refs):
        outs, send, recv = refs[n:2 * n], refs[2 * n], refs[2 * n + 1]
        x, y, c, chips = _position()
        cps = []
        for t in range(n):
            half = shapes[t][0] // 2
            for j, (cx, cy) in enumerate(chips):
                piece = outs[t].at[2 * cx + cy, pl.ds(c * half, half)]
                cp = pltpu.make_async_remote_copy(src_ref=piece, dst_ref=piece, send_sem=send.at[3 * t + j], recv_sem=recv.at[3 * t + j],
                                                  device_id=(x, y, 1 - c), device_id_type=MESH)
                cp.start()
                cps.append(cp)
        for t in range(n):
            half = shapes[t][0] // 2
            for j, (cx, cy) in enumerate(chips):
                piece = outs[t].at[2 * cx + cy, pl.ds((1 - c) * half, half)]
                pltpu.make_async_remote_copy(src_ref=piece, dst_ref=piece, send_sem=send.at[3 * t + j], recv_sem=recv.at[3 * t + j],
                                             device_id=(x, y, 1 - c), device_id_type=MESH).wait_recv()
        for cp in cps:
            cp.wait_send()

    return pl.pallas_call(
        body, name=name,
        in_specs=[HBM] * n, out_specs=[HBM] * n,
        out_shape=[SDS(p.shape, p.dtype) for p in placed],
        input_output_aliases={t: t for t in range(n)},
        scratch_shapes=[pltpu.SemaphoreType.DMA((3 * n,))] * 2,
    )(*placed)


def _flip(k, x, y, c):
    return ((1 - x) if k & 4 else x, (1 - y) if k & 2 else y, (1 - c) if k & 1 else c)


def small_allgather(v, reduce):
    R, C = v.shape

    def body(v_ref, o_ref, *scratch):
        if reduce:
            buf, send, recv = scratch
        else:
            buf, (send, recv) = o_ref, scratch
        x, y, c, _ = _position()
        me = 4 * x + 2 * y + c
        buf[me] = v_ref[...]
        sends = []
        for k in range(1, N_DEV):
            cp = pltpu.make_async_remote_copy(src_ref=v_ref, dst_ref=buf.at[me], send_sem=send.at[k - 1], recv_sem=recv.at[k - 1],
                                              device_id=_flip(k, x, y, c), device_id_type=MESH)
            cp.start()
            sends.append(cp)
        for k in range(1, N_DEV):
            px, py, pc = _flip(k, x, y, c)
            pltpu.make_async_remote_copy(src_ref=v_ref, dst_ref=buf.at[4 * px + 2 * py + pc], send_sem=send.at[k - 1],
                                         recv_sem=recv.at[k - 1], device_id=(px, py, pc), device_id_type=MESH).wait_recv()
        for cp in sends:
            cp.wait_send()
        if reduce:
            acc = buf[0]
            for i in range(1, N_DEV):
                acc = acc + buf[i]
            o_ref[...] = acc

    vm = pl.BlockSpec(memory_space=pltpu.VMEM)
    sems = [pltpu.SemaphoreType.DMA((N_DEV - 1,)), pltpu.SemaphoreType.DMA((N_DEV - 1,))]
    return pl.pallas_call(
        body, name="small_allreduce" if reduce else "small_allgather",
        in_specs=[vm], out_specs=vm,
        out_shape=SDS((R, C) if reduce else (N_DEV, R, C), F32),
        scratch_shapes=([pltpu.VMEM((N_DEV, R, C), F32)] if reduce else []) + sems,
    )(v)


def rs_exchange_sibling(gs):
    n = len(gs)

    def body(*refs):
        ins, outs, send, recv = refs[:n], refs[n:2 * n], refs[2 * n], refs[2 * n + 1]
        x, y, c, _ = _position()
        cps = []
        for t in range(n):
            cp = pltpu.make_async_remote_copy(src_ref=ins[t].at[:, 1 - c], dst_ref=outs[t], send_sem=send.at[t], recv_sem=recv.at[t],
                                              device_id=(x, y, 1 - c), device_id_type=MESH)
            cp.start()
            cps.append(cp)
        for cp in cps:
            cp.wait()

    return pl.pallas_call(
        body, name="rs_exchange_sibling", in_specs=[HBM] * n, out_specs=[HBM] * n,
        out_shape=[SDS((g.shape[0],) + g.shape[2:], g.dtype) for g in gs],
        scratch_shapes=[pltpu.SemaphoreType.DMA((n,)), pltpu.SemaphoreType.DMA((n,))],
    )(*gs)


def rs_pair_add(gs, rs, c):
    n = len(gs)

    def body(c_ref, *refs):
        for t in range(n):
            refs[2 * n + t][0] = (refs[t][0, 0].astype(F32) + refs[n + t][0].astype(F32)).astype(BF16)

    in_specs, out_specs, out_shape = [], [], []
    for g in gs:
        _, _, h, C = g.shape
        in_specs.append(pl.BlockSpec((1, 1, h // ROW_SPLIT, C), lambda j, r, c_ref: (j, c_ref[0], r, 0)))
    for g in gs:
        _, _, h, C = g.shape
        spec = pl.BlockSpec((1, h // ROW_SPLIT, C), lambda j, r, c_ref: (j, r, 0))
        in_specs.append(spec)
        out_specs.append(spec)
        out_shape.append(SDS((N_CHIPS, h, C), BF16))
    return pl.pallas_call(
        body, name="rs_pair_add",
        grid_spec=pltpu.PrefetchScalarGridSpec(num_scalar_prefetch=1, grid=(N_CHIPS, ROW_SPLIT), in_specs=in_specs, out_specs=out_specs),
        out_shape=out_shape, compiler_params=_params(("parallel", "parallel")),
    )(c, *gs, *rs)


def _rs_plan(n):
    def plan(refs, x, y, c, chips):
        return [(refs[t].at[2 * cx + cy], refs[n + t].at[j], (cx, cy, c)) for t in range(n) for j, (cx, cy) in enumerate(chips)]

    return plan


def rs_chip_add(ps, qs, me_c):
    n = len(ps)

    def body(me_ref, *refs):
        for t in range(n):
            q = refs[n + t]
            refs[2 * n + t][...] = jnp.zeros_like(refs[2 * n + t])
            refs[2 * n + t][me_ref[1]] = ((refs[t][0].astype(F32) + q[0].astype(F32)) + q[1].astype(F32)) + q[2].astype(F32)

    in_specs, out_specs, out_shape = [], [], []
    for p in ps:
        _, h, C = p.shape
        in_specs.append(pl.BlockSpec((1, h // ROW_SPLIT, C), lambda r, me_ref: (me_ref[0], r, 0)))
    for p in ps:
        _, h, C = p.shape
        in_specs.append(pl.BlockSpec((3, h // ROW_SPLIT, C), lambda r, me_ref: (0, r, 0)))
        out_specs.append(pl.BlockSpec((2, h // ROW_SPLIT, C), lambda r, me_ref: (0, r, 0)))
        out_shape.append(SDS((2, h, C), F32))
    return pl.pallas_call(
        body, name="rs_chip_add",
        grid_spec=pltpu.PrefetchScalarGridSpec(num_scalar_prefetch=1, grid=(ROW_SPLIT,), in_specs=in_specs, out_specs=out_specs),
        out_shape=out_shape, compiler_params=_params(("parallel",)),
    )(me_c, *ps, *qs)


def rs_share(rs):
    n = len(rs)

    def body(*refs):
        outs, send, recv = refs[n:2 * n], refs[2 * n], refs[2 * n + 1]
        x, y, c, _ = _position()
        cps = []
        for t in range(n):
            cp = pltpu.make_async_remote_copy(src_ref=outs[t].at[c], dst_ref=outs[t].at[c], send_sem=send.at[t], recv_sem=recv.at[t],
                                              device_id=(x, y, 1 - c), device_id_type=MESH)
            cp.start()
            cps.append(cp)
        for cp in cps:
            cp.wait()

    return pl.pallas_call(
        body, name="rs_share", in_specs=[HBM] * n, out_specs=[HBM] * n,
        out_shape=[SDS(r.shape, r.dtype) for r in rs],
        input_output_aliases={t: t for t in range(n)},
        scratch_shapes=[pltpu.SemaphoreType.DMA((n,))] * 2,
    )(*rs)


def rs_begin(gs, after, name):
    c = lax.axis_index("c")
    n = len(gs)
    g5 = [g.reshape(N_CHIPS, 2, g.shape[1] // 2, g.shape[2]) for g in gs]
    from_sibling = rs_exchange_sibling(g5)
    pair = rs_pair_add(g5, from_sibling, jnp.reshape(c, (1,)).astype(jnp.int32))
    lands = [jnp.zeros((3,) + p.shape[1:], p.dtype) for p in pair]
    send, recv, bufs, token = split_start(list(pair) + lands, _rs_plan(n), 3 * n, from_sibling[0] if after is None else after, name)
    return (send, recv, bufs, [g.shape for g in gs]), token


def rs_end(state, after, name):
    x, y, c = lax.axis_index("x"), lax.axis_index("y"), lax.axis_index("c")
    send, recv, bufs, shapes = state
    n = len(shapes)
    bufs = split_wait(send, recv, bufs, _rs_plan(n), after, name)
    half = rs_chip_add(bufs[:n], bufs[n:], jnp.stack([2 * x + y, c]).astype(jnp.int32))
    both = rs_share(half)
    return [b.reshape(s[1], s[2]) for b, s in zip(both, shapes)]


def _pad_last(a, n):
    return jnp.pad(a, [(0, 0)] * (a.ndim - 1) + [(0, n - a.shape[-1])])


def _heads_to_groups(w):
    k = w.shape[0]
    return _pad_last(w.reshape(k, ML_HEADS, ML_HEAD_DIM).transpose(1, 0, 2), GROUP)


def _groups_to_heads(g):
    return g[:, :, :ML_HEAD_DIM].transpose(1, 0, 2).reshape(g.shape[1], D_TOK)


def _cols_to_groups(w):
    k, n = w.shape
    return w.reshape(k, n // GROUP, GROUP).transpose(1, 0, 2)


def _groups_to_cols(g):
    n, k, _ = g.shape
    return g.transpose(1, 0, 2).reshape(k, n * GROUP)


def _chips_to_cols(a):
    return a.transpose(1, 0, 2).reshape(a.shape[1], -1)


def _cols_to_chips(w):
    k, n = w.shape
    return w.reshape(k, N_CHIPS, n // N_CHIPS).transpose(1, 0, 2)


def _mlstm_in_groups(w):
    parts = [_heads_to_groups(w[:, i * D_TOK:(i + 1) * D_TOK]) for i in range(4)]
    gates = _pad_last(w[:, 4 * D_TOK:4 * D_TOK + 2 * ML_HEADS], GROUP)[None]
    qmem = w[:, 4 * D_TOK + 2 * ML_HEADS:][None]
    return jnp.concatenate(parts + [qmem, gates], axis=0)


def _mlstm_in_ungroup(g):
    parts = [_groups_to_heads(g[4 * i:4 * i + 4]) for i in range(4)]
    return jnp.concatenate(parts + [g[17][:, :2 * ML_HEADS], g[16]], axis=1)


def _taps_to_groups(w, width):
    taps = w.shape[0]
    g = _pad_last(w.reshape(taps, -1, width), GROUP).transpose(1, 0, 2)
    return jnp.pad(g, ((0, 0), (0, 8 - taps), (0, 0)))


def _groups_to_taps(g, taps, width):
    return g[:, :taps, :width].transpose(1, 0, 2).reshape(taps, -1)


SMALL_IN_COLS = 384
SMALL_OUT_COLS = 1536
SECTION = 8


class _Gathered:
    def __init__(self, srcs, groups, me):
        keys = [k for g in groups for k in g]
        placed = dict(zip(keys, place_own([(srcs[k], ()) for k in keys], me)))
        self.groups, self.states, self.ready = groups, [], {}
        self.group_of = {k: gi for gi, g in enumerate(groups) for k in g}
        token = me
        for gi, g in enumerate(groups):
            state, token = gather_start([srcs[k] for k in g], [placed[k] for k in g], token, f"gather_start_{gi}")
            self.states.append(state)
        self.started = token

    def _get(self, key, after):
        gi = self.group_of[key]
        if gi not in self.ready:
            got = gather_wait(self.states[gi], after if gi else self.started, f"gather_wait_{gi}")
            self.ready[gi] = dict(zip(self.groups[gi], gather_pass_on(got, self.states[gi][3], f"gather_pass_on_{gi}")))
        return self.ready[gi][key]

    def ffn(self, l, i, after):
        return tuple(self._get((n, l, i), after) for n in ("wg", "wu", "wd"))

    def mixer(self, l, after):
        win = _chips_to_cols(self._get(("win", l), after))
        win = _cols_to_groups(win) if l % 2 == 0 else _mlstm_in_groups(win)
        wkv = _cols_to_groups(self._get(("wkv", l), after).reshape(D_MODEL, 2 * D_XA))
        wout = self._get(("wout", l), after)
        if l % 2:
            wout = wout.reshape(D_MODEL, D_MODEL)
            tok = jnp.pad(wout[:D_TOK].reshape(ML_HEADS, ML_HEAD_DIM, D_MODEL), ((0, 0), (0, GROUP - ML_HEAD_DIM), (0, 0)))
            wout = jnp.concatenate([tok, wout[D_TOK:][None]], axis=0)
        return win, wkv, wout


class _GradSink:
    def __init__(self, apply):
        self.queue, self.apply, self.count, self.done = [], apply, 0, None

    @staticmethod
    def _by_chip(key, g):
        if key[0] == "wkv":
            return _groups_to_cols(g).reshape(N_CHIPS, D_MODEL // N_CHIPS, 2 * D_XA)
        if key[0] == "win":
            return _cols_to_chips(_groups_to_cols(g) if key[1] % 2 == 0 else _mlstm_in_ungroup(g))
        if key[0] == "wout" and key[1] % 2:
            full = jnp.concatenate([g[:ML_HEADS, :ML_HEAD_DIM].reshape(D_TOK, D_MODEL), g[ML_HEADS]], axis=0)
            return full.reshape(N_CHIPS, D_MODEL // N_CHIPS, D_MODEL)
        return g

    def push(self, grads):
        keys = list(grads)
        state, token = rs_begin([self._by_chip(k, grads[k]) for k in keys], self.done, f"rs_start_{self.count}")
        if self.queue:
            self._finish(token)
        self.queue.append((keys, state, self.count))
        self.count += 1
        return token

    def flush(self):
        self._finish(self.done)

    def _finish(self, after):
        keys, state, i = self.queue.pop(0)
        for key, g in zip(keys, rs_end(state, after, f"rs_wait_{i}")):
            self.done = self.apply(key, g)


def _local_step(x, mem, tgt, P, weights, sink):
    memb = mem.astype(BF16)
    saved = []
    X, Xb = x, x.astype(BF16)
    after = Xb
    for l in range(DEPTH):
        s = {}
        s["x0b"] = Xb
        s["wa"] = weights.ffn(l, 0, after)
        s["g1a"], s["u1a"], s["ha"], s["z1"], X1, X1b = ffn_fwd(Xb, X, *s["wa"], P["ln_g"][l][0], P["ln_b"][l][0])
        s["x1b"] = X1b
        s["wm"] = win, wkv, wout = weights.mixer(l, X1b)
        u = proj(X1b, win, "mixer_in")
        kv = proj(memb, wkv, "mem_kv")
        s["u"], s["kv"] = u, kv
        if l % 2 == 0:
            tok = conv_mixer_fwd(u, P["convw"])
            qg = 9
        else:
            s["qk"] = qk_conv_fwd(u, P["qkw"])
            s["hm"], s["cst"], s["mst"] = mlstm_fwd(s["qk"], u, P["bg"])
            tok = head_norm_fwd(s["hm"], u, P["hg"])
            qg = 16
        xa = xattn_fwd(u, qg, kv)
        s["m"] = jnp.concatenate([tok, xa], axis=0)
        s["z2"], X2, X2b = contract_ln(s["m"], wout, X1, P["ln_g"][l][1], P["ln_b"][l][1], 1.0, "mixer_out_ln")
        s["x2b"] = X2b
        s["wb"] = weights.ffn(l, 1, X2b)
        s["g1b"], s["u1b"], s["hb"], s["z3"], X, Xb = ffn_fwd(X2b, X2, *s["wb"], P["ln_g"][l][2], P["ln_b"][l][2])
        after = Xb
        saved.append(s)

    loss, dX = loss_grad(X, tgt)

    G = {"ln_g": [[None] * 3 for _ in range(DEPTH)], "ln_b": [[None] * 3 for _ in range(DEPTH)]}
    pin = [jnp.zeros((1, 1), F32)]

    def ffn_backward(l, i, dX, z, xinb, g1, u1, h, w):
        k = 2 * i
        dz, dyb, G["ln_g"][l][k], G["ln_b"][l][k] = ln_bwd(dX, z, P["ln_g"][l][k] + pin[0], 0.5, "ffn_ln_bwd")
        dgb, dub, dx = ffn_bwd(dyb, dz, w[2], w[0], w[1], g1, u1)
        grads = {("wd", l, i): wgrad(h, dyb, BF16, "wgrad_down"), ("wg", l, i): wgrad(dgb, xinb, BF16, "wgrad_gate"),
                 ("wu", l, i): wgrad(dub, xinb, BF16, "wgrad_up")}
        return dx, grads

    for l in reversed(range(DEPTH)):
        s = saved[l]
        win, wkv, wout = s["wm"]
        dX, grads = ffn_backward(l, 1, dX, s["z3"], s["x2b"], s["g1b"], s["u1b"], s["hb"], s["wb"])
        dz2, dz2b, G["ln_g"][l][1], G["ln_b"][l][1] = ln_bwd(dX, s["z2"], P["ln_g"][l][1], 1.0, "mixer_ln_bwd")
        dm = proj_t(dz2b, wout, "mixer_out_bwd")
        grads[("wout", l)] = wgrad(s["m"], dz2b, BF16, "wgrad_out")
        u, kv = s["u"], s["kv"]
        if l % 2 == 0:
            db, dc, dxi, G["convw"] = conv_mixer_bwd(u, P["convw"], dm)
            dq, dkv = xattn_bwd(u, 9, kv, dm, 3)
            du = jnp.concatenate([db, dc, dxi, dq], axis=0)
        else:
            dh, do, G["hg"] = head_norm_bwd(s["hm"], u, P["hg"], dm)
            dqk, dv, dgate, G["bg"] = mlstm_bwd(s["qk"], u, P["bg"], s["cst"], s["mst"], dh)
            duqk, G["qkw"] = qk_conv_bwd(u, P["qkw"], dqk)
            dq, dkv = xattn_bwd(u, 16, kv, dm, 4)
            du = jnp.concatenate([duqk, dv, do, dq, dgate], axis=0)
        grads[("win", l)] = wgrad(s["x1b"], du, BF16, "wgrad_in")
        grads[("wkv", l)] = wgrad(memb, dkv.astype(BF16), BF16, "wgrad_kv")
        dX = contract_t(du, win, dz2, "mixer_in_bwd")
        pin[0] = sink.push(grads)[0:1, 0:1]
        dX, grads = ffn_backward(l, 0, dX, s["z1"], s["x0b"], s["g1a"], s["u1a"], s["ha"], s["wa"])
        pin[0] = sink.push(grads)[0:1, 0:1]
    sink.flush()
    return loss, dX, G


def kernel(x, mem, ln_g, ln_b, ffn_w_gate, ffn_w_up, ffn_w_down, w_kv_mem, w_out, w_in_conv, conv_w, w_in_mlstm, b_gates, qk_conv_w, head_norm_g, loss_target, m_ln_g, m_ln_b, m_ffn_w_gate, m_ffn_w_up, m_ffn_w_down, m_w_kv_mem, m_w_out, m_w_in_conv, m_conv_w, m_w_in_mlstm, m_b_gates, m_qk_conv_w, m_head_norm_g, v_ln_g, v_ln_b, v_ffn_w_gate, v_ffn_w_up, v_ffn_w_down, v_w_kv_mem, v_w_out, v_w_in_conv, v_conv_w, v_w_in_mlstm, v_b_gates, v_qk_conv_w, v_head_norm_g):
    cx, cy = lax.axis_index("x"), lax.axis_index("y")
    chip = 2 * cx + cy

    srcs = {}
    for l in range(DEPTH):
        for i in range(2):
            srcs[("wg", l, i)] = ffn_w_gate[l, i].astype(BF16)
            srcs[("wu", l, i)] = ffn_w_up[l, i].astype(BF16)
            srcs[("wd", l, i)] = ffn_w_down[l, i].astype(BF16)
        srcs[("wkv", l)] = w_kv_mem[l].astype(BF16)
        srcs[("wout", l)] = w_out[l].astype(BF16)
    srcs[("win", 0)] = w_in_conv[0].astype(BF16)
    srcs[("win", 1)] = w_in_mlstm[0].astype(BF16)
    ffn_keys = lambda l, i: [("wg", l, i), ("wu", l, i), ("wd", l, i)]
    mixer_keys = lambda l: [("win", l), ("wkv", l), ("wout", l)]
    groups = [ffn_keys(0, 0), mixer_keys(0) + mixer_keys(1) + ffn_keys(0, 1), ffn_keys(1, 0), ffn_keys(1, 1)]
    gathered = _Gathered(srcs, groups, jnp.reshape(chip, (1,)).astype(jnp.int32))

    def section(a, width):
        a = a.reshape(-1, a.shape[-1])
        return jnp.pad(a, ((0, SECTION - a.shape[0]), (0, width - a.shape[1])))

    small = jnp.concatenate([section(a, SMALL_IN_COLS) for a in (ln_g, ln_b, conv_w, qk_conv_w)], axis=0)
    smalls = small_allgather(small, reduce=False)[0::2]
    ln_g_full = _chips_to_cols(smalls[:, 0:6, 0:256]).reshape(DEPTH, 3, 1, D_MODEL)
    ln_b_full = _chips_to_cols(smalls[:, 8:14, 0:256]).reshape(DEPTH, 3, 1, D_MODEL)
    conv_w_full = _chips_to_cols(smalls[:, 16:19, 0:192])
    qk_w_full = _chips_to_cols(smalls[:, 24:28, 0:384])

    P = {"ln_g": ln_g_full, "ln_b": ln_b_full, "convw": _taps_to_groups(conv_w_full, GROUP),
         "qkw": _taps_to_groups(qk_w_full, ML_HEAD_DIM), "bg": _pad_last(b_gates, GROUP),
         "hg": _pad_last(head_norm_g[0], GROUP)[:, None, :]}

    weights = {"ln_g": ln_g, "ln_b": ln_b, "ffn_w_gate": ffn_w_gate, "ffn_w_up": ffn_w_up, "ffn_w_down": ffn_w_down,
               "w_kv_mem": w_kv_mem, "w_out": w_out, "w_in_conv": w_in_conv, "conv_w": conv_w, "w_in_mlstm": w_in_mlstm,
               "b_gates": b_gates, "qk_conv_w": qk_conv_w, "head_norm_g": head_norm_g}
    ms = {"ln_g": m_ln_g, "ln_b": m_ln_b, "ffn_w_gate": m_ffn_w_gate, "ffn_w_up": m_ffn_w_up, "ffn_w_down": m_ffn_w_down,
          "w_kv_mem": m_w_kv_mem, "w_out": m_w_out, "w_in_conv": m_w_in_conv, "conv_w": m_conv_w, "w_in_mlstm": m_w_in_mlstm,
          "b_gates": m_b_gates, "qk_conv_w": m_qk_conv_w, "head_norm_g": m_head_norm_g}
    vs = {"ln_g": v_ln_g, "ln_b": v_ln_b, "ffn_w_gate": v_ffn_w_gate, "ffn_w_up": v_ffn_w_up, "ffn_w_down": v_ffn_w_down,
          "w_kv_mem": v_w_kv_mem, "w_out": v_w_out, "w_in_conv": v_w_in_conv, "conv_w": v_conv_w, "w_in_mlstm": v_w_in_mlstm,
          "b_gates": v_b_gates, "qk_conv_w": v_qk_conv_w, "head_norm_g": v_head_norm_g}
    names = list(weights)
    owner = {"wg": ("ffn_w_gate", True), "wu": ("ffn_w_up", True), "wd": ("ffn_w_down", False), "wkv": ("w_kv_mem", False),
             "wout": ("w_out", False), "win": None}
    updated = {}

    def apply(key, g):
        name, transposed = owner[key[0]] or (("w_in_conv", "w_in_mlstm")[key[1]], False)
        idx = (0,) if key[0] == "win" else tuple(key[1:])
        view = (lambda a: jnp.swapaxes(a, -1, -2)) if transposed else (lambda a: a)
        updated[name], token = adamw_into(view(weights[name]), view(ms[name]), view(vs[name]), g, updated.get(name), idx,
                                          "adamw_" + name + "_" + "_".join(map(str, idx)))
        return token

    sink = _GradSink(apply)
    loss, grad_x, G = _local_step(x[0], mem[0], loss_target[0], P, gathered, sink)

    dln_g = jnp.concatenate([G["ln_g"][l][k] for l in range(DEPTH) for k in range(3)], axis=0)
    dln_b = jnp.concatenate([G["ln_b"][l][k] for l in range(DEPTH) for k in range(3)], axis=0)
    lane = lax.broadcasted_iota(jnp.int32, (1, GROUP), 1)
    misc = jnp.where(lane < 8, G["bg"], 0.0) + jnp.where(lane == 8, loss, 0.0)
    parts = (dln_g, dln_b, _groups_to_taps(G["convw"], 3, GROUP), misc, _groups_to_taps(G["qkw"], 4, ML_HEAD_DIM),
             G["hg"][:, 0, :ML_HEAD_DIM])
    tot = small_allgather(jnp.concatenate([section(a, SMALL_OUT_COLS) for a in parts], axis=0), reduce=True)
    loss_total = tot[24, 8]

    small_grads = {
        "ln_g": lax.dynamic_slice(tot[0:6, 0:D_MODEL], (0, chip * 256), (6, 256)).reshape(DEPTH, 3, 256),
        "ln_b": lax.dynamic_slice(tot[8:14, 0:D_MODEL], (0, chip * 256), (6, 256)).reshape(DEPTH, 3, 256),
        "conv_w": lax.dynamic_slice(tot[16:19, 0:D_TOK], (0, chip * 192), (3, 192))[None],
        "b_gates": tot[24:25, 0:8],
        "qk_conv_w": lax.dynamic_slice(tot[32:36, 0:2 * D_TOK], (0, chip * 384), (4, 384))[None],
        "head_norm_g": tot[40:44, 0:ML_HEAD_DIM][None],
    }
    grads, deltas, new_m, new_v = [], [], [], []
    for nme in names:
        if nme in updated:
            back = (lambda a: jnp.swapaxes(a, -1, -2)) if nme in ("ffn_w_gate", "ffn_w_up") else (lambda a: a)
            g, d, nm, nv = (back(a) for a in updated[nme])
        else:
            w, g = weights[nme], small_grads[nme]
            two = (math.prod(w.shape[:-1]), w.shape[-1])
            d, nm, nv = (a.reshape(w.shape) for a in adamw(w.reshape(two), g.reshape(two), ms[nme].reshape(two),
                                                           vs[nme].reshape(two), "adamw_" + nme))
        grads.append(g)
        deltas.append(d)
        new_m.append(nm)
        new_v.append(nv)
    return (loss_total, grad_x[None], *grads, *deltas, *new_m, *new_v)
```

```python
import functools
import math

import jax
import jax.numpy as jnp
from jax import lax
from jax.experimental import pallas as pl
from jax.experimental.pallas import tpu as pltpu

F32 = jnp.float32
BF16 = jnp.bfloat16
SDS = jax.ShapeDtypeStruct

D_MODEL = 1024
DEPTH = 2
N_MEM = 256
XA_HEADS = 4
XA_HEAD_DIM = 64
D_XA = 256
D_TOK = 768
ML_HEADS = 4
ML_HEAD_DIM = 192
ML_CHUNK = 64
D_FF = 2816
LN_EPS = 1e-5
ALPHA = (2.0 * DEPTH) ** 0.25
N_CHIPS = 4
N_DEV = 8
FF_SHARD = D_FF // N_CHIPS
GROUP = 256
NEG = -1e30

ADAM_LR = 0.001
ADAM_B1 = 0.9
ADAM_B2 = 0.999
ADAM_EPS = 1e-08
ADAM_WD = 0.01
ADAM_STEP = 10

VMEM_LIMIT = 56 * 1024 * 1024

NN = ((1,), (0,))
NT = ((1,), (1,))
TN = ((0,), (0,))
MESH = pl.DeviceIdType.MESH


def _dot(a, b, dims):
    return lax.dot_general(a, b, (dims, ((), ())), preferred_element_type=F32)


def _bdot(a, b, ca, cb):
    dims = (((ca,), (cb,)), ((0,), (0,)))
    ah, bh = a.astype(BF16), b.astype(BF16)
    al, bl = (a - ah.astype(F32)).astype(BF16), (b - bh.astype(F32)).astype(BF16)
    dot = functools.partial(lax.dot_general, dimension_numbers=dims, preferred_element_type=F32)
    return dot(ah, bh) + dot(al, bh) + dot(ah, bl)


def _sigmoid(x):
    return 1.0 / (1.0 + jnp.exp(-x))


def _params(sem, vmem=VMEM_LIMIT):
    return pltpu.CompilerParams(dimension_semantics=sem, vmem_limit_bytes=vmem)


def _tile(n, want):
    t = min(n, want)
    assert n % t == 0, (n, t)
    return t


def _layer_norm(z, gamma, beta):
    mu = jnp.mean(z, axis=-1, keepdims=True)
    zc = z - mu
    var = jnp.mean(zc * zc, axis=-1, keepdims=True)
    return zc * lax.rsqrt(var + LN_EPS) * gamma + beta


def _resident(shape):
    return pl.BlockSpec(shape, lambda *_: (0,) * len(shape), pipeline_mode=pl.Buffered(1))


def _group_block(G, want):
    return max(d for d in range(1, max(1, min(G, want)) + 1) if G % d == 0)


def ffn_fwd(xb, x, wg, wu, wd, gamma, beta):
    S, K = xb.shape
    G, _, N = wg.shape
    ts = _tile(S, 512)

    def body(xb_ref, x_ref, wg_ref, wu_ref, wd_ref, gm_ref, bt_ref, g_ref, u_ref, h_ref, z_ref, xn_ref, xnb_ref):
        j = pl.program_id(1)
        xv = xb_ref[...]
        g = _dot(xv, wg_ref[j], NN)
        u = _dot(xv, wu_ref[j], NN)
        h = (g * _sigmoid(g) * u).astype(BF16)
        g_ref[0] = g.astype(BF16)
        u_ref[0] = u.astype(BF16)
        h_ref[0] = h
        y = _dot(h, wd_ref[j], NN)

        @pl.when(j == 0)
        def _():
            z_ref[...] = y

        @pl.when(j > 0)
        def _():
            z_ref[...] += y

        @pl.when(j == G - 1)
        def _():
            z = ALPHA * x_ref[...] + 0.5 * z_ref[...]
            xn = _layer_norm(z, gm_ref[...], bt_ref[...])
            z_ref[...] = z
            xn_ref[...] = xn
            xnb_ref[...] = xn.astype(BF16)

    row = pl.BlockSpec((ts, K), lambda s, j: (s, 0))
    vec = pl.BlockSpec((1, K), lambda s, j: (0, 0))
    wspec = _resident((G, K, N))
    ospec = pl.BlockSpec((1, ts, N), lambda s, j: (j, s, 0))
    return pl.pallas_call(
        body, name="ffn_fwd", grid=(S // ts, G),
        in_specs=[row, row, wspec, wspec, _resident((G, N, K)), vec, vec],
        out_specs=[ospec, ospec, ospec, row, row, row],
        out_shape=[SDS((G, S, N), BF16), SDS((G, S, N), BF16), SDS((G, S, N), BF16),
                   SDS((S, K), F32), SDS((S, K), F32), SDS((S, K), BF16)],
        compiler_params=_params(("parallel", "arbitrary")),
    )(xb, x, wg, wu, wd, gamma, beta)


def proj(xb, w, name):
    S, K = xb.shape
    G, _, N = w.shape
    ts = _tile(S, 1024)
    gb = _group_block(G, 6)

    def body(x_ref, w_ref, y_ref):
        xv = x_ref[...]
        for j in range(gb):
            y_ref[j] = _dot(xv, w_ref[j], NN)

    return pl.pallas_call(
        body, name=name, grid=(S // ts, G // gb),
        in_specs=[pl.BlockSpec((ts, K), lambda s, g: (s, 0)), pl.BlockSpec((gb, K, N), lambda s, g: (g, 0, 0))],
        out_specs=pl.BlockSpec((gb, ts, N), lambda s, g: (g, s, 0)),
        out_shape=SDS((G, S, N), F32),
        compiler_params=_params(("parallel", "parallel")),
    )(xb, w)


def contract_ln(a, w, xres, gamma, beta, scale, name):
    G, S, Kg = a.shape
    N = w.shape[2]
    ts = _tile(S, 512)

    def body(a_ref, w_ref, x_ref, g_ref, b_ref, z_ref, xn_ref, xb_ref):
        acc = _dot(a_ref[0], w_ref[0], NN)
        for j in range(1, G):
            acc = acc + _dot(a_ref[j], w_ref[j], NN)
        z = ALPHA * x_ref[...] + scale * acc
        xn = _layer_norm(z, g_ref[...], b_ref[...])
        z_ref[...] = z
        xn_ref[...] = xn
        xb_ref[...] = xn.astype(BF16)

    row = pl.BlockSpec((ts, N), lambda s: (s, 0))
    vec = pl.BlockSpec((1, N), lambda s: (0, 0))
    return pl.pallas_call(
        body, name=name, grid=(S // ts,),
        in_specs=[pl.BlockSpec((G, ts, Kg), lambda s: (0, s, 0)), pl.BlockSpec((G, Kg, N), lambda s: (0, 0, 0)), row, vec, vec],
        out_specs=[row, row, row],
        out_shape=[SDS((S, N), F32), SDS((S, N), F32), SDS((S, N), BF16)],
        compiler_params=_params(("parallel",)),
    )(a, w, xres, gamma, beta)


def ln_bwd(dx, z, gamma, out_scale, name):
    S, N = dx.shape
    ts = _tile(S, 512)

    def body(dx_ref, z_ref, g_ref, dz_ref, dzb_ref, dg_ref, db_ref):
        @pl.when(pl.program_id(0) == 0)
        def _():
            dg_ref[...] = jnp.zeros_like(dg_ref)
            db_ref[...] = jnp.zeros_like(db_ref)

        z = z_ref[...]
        mu = jnp.mean(z, axis=-1, keepdims=True)
        zc = z - mu
        var = jnp.mean(zc * zc, axis=-1, keepdims=True)
        rstd = lax.rsqrt(var + LN_EPS)
        xhat = zc * rstd
        dxv = dx_ref[...]
        dg_ref[...] += jnp.sum(dxv * xhat, axis=0, keepdims=True)
        db_ref[...] += jnp.sum(dxv, axis=0, keepdims=True)
        dxh = dxv * g_ref[...]
        m1 = jnp.mean(dxh, axis=-1, keepdims=True)
        m2 = jnp.mean(dxh * xhat, axis=-1, keepdims=True)
        dz = rstd * (dxh - m1 - xhat * m2)
        dz_ref[...] = dz
        dzb_ref[...] = (out_scale * dz).astype(BF16)

    row = pl.BlockSpec((ts, N), lambda s: (s, 0))
    vec = pl.BlockSpec((1, N), lambda s: (0, 0))
    return pl.pallas_call(
        body, name=name, grid=(S // ts,),
        in_specs=[row, row, vec],
        out_specs=[row, row, vec, vec],
        out_shape=[SDS((S, N), F32), SDS((S, N), BF16), SDS((1, N), F32), SDS((1, N), F32)],
        compiler_params=_params(("arbitrary",)),
    )(dx, z, gamma)


def ffn_bwd(dyb, dz, wd, wg, wu, g1, u1):
    S, K = dyb.shape
    G, N, _ = wd.shape
    ts = _tile(S, 512)

    def body(dy_ref, dz_ref, wd_ref, wg_ref, wu_ref, g_ref, u_ref, dg_ref, du_ref, dx_ref):
        j = pl.program_id(1)
        dh = _dot(dy_ref[...], wd_ref[j], NT)
        g = g_ref[0].astype(F32)
        sig = _sigmoid(g)
        dg = (dh * u_ref[0].astype(F32) * (sig * (1.0 + g * (1.0 - sig)))).astype(BF16)
        du = (dh * (g * sig)).astype(BF16)
        dg_ref[0] = dg
        du_ref[0] = du
        part = _dot(dg, wg_ref[j], NT) + _dot(du, wu_ref[j], NT)

        @pl.when(j == 0)
        def _():
            dx_ref[...] = ALPHA * dz_ref[...] + part

        @pl.when(j > 0)
        def _():
            dx_ref[...] += part

    row = pl.BlockSpec((ts, K), lambda s, j: (s, 0))
    gspec = pl.BlockSpec((1, ts, N), lambda s, j: (j, s, 0))
    wspec = _resident((G, K, N))
    return pl.pallas_call(
        body, name="ffn_bwd", grid=(S // ts, G),
        in_specs=[row, row, _resident((G, N, K)), wspec, wspec, gspec, gspec],
        out_specs=[gspec, gspec, row],
        out_shape=[SDS((G, S, N), BF16), SDS((G, S, N), BF16), SDS((S, K), F32)],
        compiler_params=_params(("parallel", "arbitrary")),
    )(dyb, dz, wd, wg, wu, g1, u1)


def proj_t(dyb, w, name):
    S, N = dyb.shape
    G, Kg, _ = w.shape
    ts = _tile(S, 1024)

    def body(dy_ref, w_ref, da_ref):
        dy = dy_ref[...]
        for j in range(G):
            da_ref[j] = _dot(dy, w_ref[j], NT)

    return pl.pallas_call(
        body, name=name, grid=(S // ts,),
        in_specs=[pl.BlockSpec((ts, N), lambda s: (s, 0)), pl.BlockSpec((G, Kg, N), lambda s: (0, 0, 0))],
        out_specs=pl.BlockSpec((G, ts, Kg), lambda s: (0, s, 0)),
        out_shape=SDS((G, S, Kg), F32),
        compiler_params=_params(("parallel",)),
    )(dyb, w)


def contract_t(da, w, res, name):
    G, S, Ng = da.shape
    K = w.shape[1]
    ts = _tile(S, 512)
    gb = _group_block(G, 6)

    def body(da_ref, w_ref, r_ref, o_ref):
        g = pl.program_id(1)
        part = _dot(da_ref[0], w_ref[0], NT)
        for j in range(1, gb):
            part = part + _dot(da_ref[j], w_ref[j], NT)

        @pl.when(g == 0)
        def _():
            o_ref[...] = ALPHA * r_ref[...] + part

        @pl.when(g > 0)
        def _():
            o_ref[...] += part

    row = pl.BlockSpec((ts, K), lambda s, g: (s, 0))
    return pl.pallas_call(
        body, name=name, grid=(S // ts, G // gb),
        in_specs=[pl.BlockSpec((gb, ts, Ng), lambda s, g: (g, s, 0)), pl.BlockSpec((gb, K, Ng), lambda s, g: (g, 0, 0)), row],
        out_specs=row,
        out_shape=SDS((S, K), F32),
        compiler_params=_params(("parallel", "arbitrary")),
    )(da, w, res)


WGRAD_ACC_ELEMS = 6 * 1024 * 256


def wgrad(a, b, out_dtype, name):
    ga, gb = a.ndim == 3, b.ndim == 3
    G = a.shape[0] if ga else b.shape[0]
    S, K = a.shape[-2:]
    N = b.shape[-1]
    ts = _tile(S, 1024)
    ns = S // ts
    ng = _group_block(G, WGRAD_ACC_ELEMS // (K * N))

    def body(a_ref, b_ref, o_ref, acc):
        s = pl.program_id(1)

        @pl.when(s == 0)
        def _():
            acc[...] = jnp.zeros_like(acc)

        for j in range(ng):
            acc[j] += _dot(a_ref[j] if ga else a_ref[...], b_ref[j] if gb else b_ref[...], TN)

        @pl.when(s == ns - 1)
        def _():
            o_ref[...] = acc[...].astype(out_dtype)

    aspec = pl.BlockSpec((ng, ts, K), lambda g, s: (g, s, 0)) if ga else pl.BlockSpec((ts, K), lambda g, s: (s, 0))
    bspec = pl.BlockSpec((ng, ts, N), lambda g, s: (g, s, 0)) if gb else pl.BlockSpec((ts, N), lambda g, s: (s, 0))
    return pl.pallas_call(
        body, name=name, grid=(G // ng, ns),
        in_specs=[aspec, bspec],
        out_specs=pl.BlockSpec((ng, K, N), lambda g, s: (g, 0, 0)),
        out_shape=SDS((G, K, N), out_dtype),
        scratch_shapes=[pltpu.VMEM((ng, K, N), F32)],
        compiler_params=_params(("parallel", "arbitrary")),
    )(a, b)


def loss_grad(xn, tgt):
    S, N = xn.shape
    ts = _tile(S, 512)

    def body(x_ref, t_ref, l_ref, dx_ref):
        @pl.when(pl.program_id(0) == 0)
        def _():
            l_ref[...] = jnp.zeros_like(l_ref)

        e = x_ref[...] - t_ref[...]
        dx_ref[...] = e * (1.0 / N)
        l_ref[...] += 0.5 * jnp.sum(jnp.mean(e * e, axis=-1, keepdims=True), axis=0, keepdims=True)

    row = pl.BlockSpec((ts, N), lambda s: (s, 0))
    return pl.pallas_call(
        body, name="loss_grad", grid=(S // ts,),
        in_specs=[row, row],
        out_specs=[pl.BlockSpec((1, 1), lambda s: (0, 0)), row],
        out_shape=[SDS((1, 1), F32), SDS((S, N), F32)],
        compiler_params=_params(("arbitrary",)),
    )(xn, tgt)


def _shift_down(x, k):
    if k == 0:
        return x
    rows = lax.broadcasted_iota(jnp.int32, x.shape, 0)
    return jnp.where(rows >= k, pltpu.roll(x, k, 0), 0.0)


def _shift_up(x, k):
    if k == 0:
        return x
    n = x.shape[0]
    rows = lax.broadcasted_iota(jnp.int32, x.shape, 0)
    return jnp.where(rows < n - k, pltpu.roll(x, n - k, 0), 0.0)


LANES = 128


def conv_mixer_fwd(u, cw):
    _, S, _ = u.shape
    nh = GROUP // LANES

    def body(b_ref, c_ref, x_ref, w_ref, o_ref):
        p = c_ref[0] * x_ref[0]
        w = w_ref[0]
        conv = w[2:3] * p + w[1:2] * _shift_down(p, 1) + w[0:1] * _shift_down(p, 2)
        o_ref[0] = (b_ref[0] * conv).astype(BF16)

    def uspec(off):
        return pl.BlockSpec((1, S, LANES), lambda g, h: (g + off, 0, h))

    return pl.pallas_call(
        body, name="conv_mixer_fwd", grid=(3, nh),
        in_specs=[uspec(0), uspec(3), uspec(6), pl.BlockSpec((1, 8, LANES), lambda g, h: (g, 0, h))],
        out_specs=pl.BlockSpec((1, S, LANES), lambda g, h: (g, 0, h)),
        out_shape=SDS((3, S, GROUP), BF16),
        compiler_params=_params(("parallel", "parallel")),
    )(u, u, u, cw)


def conv_mixer_bwd(u, cw, dm):
    _, S, _ = u.shape
    nh = GROUP // LANES

    def body(b_ref, c_ref, x_ref, w_ref, d_ref, db_ref, dc_ref, dx_ref, dw_ref):
        cg, xi = c_ref[0], x_ref[0]
        p = cg * xi
        p1, p2 = _shift_down(p, 1), _shift_down(p, 2)
        w = w_ref[0]
        conv = w[2:3] * p + w[1:2] * p1 + w[0:1] * p2
        dt = d_ref[0]
        db_ref[0] = (dt * conv).astype(BF16)
        dcv = dt * b_ref[0]
        dp = w[2:3] * dcv + w[1:2] * _shift_up(dcv, 1) + w[0:1] * _shift_up(dcv, 2)
        dc_ref[0] = (dp * xi).astype(BF16)
        dx_ref[0] = (dp * cg).astype(BF16)
        dw = jnp.concatenate([jnp.sum(dcv * p2, axis=0, keepdims=True), jnp.sum(dcv * p1, axis=0, keepdims=True),
                              jnp.sum(dcv * p, axis=0, keepdims=True), jnp.zeros((5, LANES), F32)], axis=0)
        dw_ref[0] = dw

    def uspec(off):
        return pl.BlockSpec((1, S, LANES), lambda g, h: (g + off, 0, h))

    ospec = pl.BlockSpec((1, S, LANES), lambda g, h: (g, 0, h))
    wspec = pl.BlockSpec((1, 8, LANES), lambda g, h: (g, 0, h))
    return pl.pallas_call(
        body, name="conv_mixer_bwd", grid=(3, nh),
        in_specs=[uspec(0), uspec(3), uspec(6), wspec, ospec],
        out_specs=[ospec, ospec, ospec, wspec],
        out_shape=[SDS((3, S, GROUP), BF16)] * 3 + [SDS((3, 8, GROUP), F32)],
        compiler_params=_params(("parallel", "parallel")),
    )(u, u, u, cw, dm)


def qk_conv_fwd(u, qw):
    _, S, _ = u.shape
    nh = GROUP // LANES

    def body(u_ref, w_ref, o_ref):
        x = u_ref[0]
        w = w_ref[0]
        pre = w[3:4] * x + w[2:3] * _shift_down(x, 1) + w[1:2] * _shift_down(x, 2) + w[0:1] * _shift_down(x, 3)
        o_ref[0] = pre * _sigmoid(pre)

    spec = pl.BlockSpec((1, S, LANES), lambda g, h: (g, 0, h))
    return pl.pallas_call(
        body, name="qk_conv_fwd", grid=(8, nh),
        in_specs=[spec, pl.BlockSpec((1, 8, LANES), lambda g, h: (g, 0, h))],
        out_specs=spec,
        out_shape=SDS((8, S, GROUP), F32),
        compiler_params=_params(("parallel", "parallel")),
    )(u, qw)


def qk_conv_bwd(u, qw, dqk):
    _, S, _ = u.shape
    nh = GROUP // LANES

    def body(u_ref, w_ref, d_ref, du_ref, dw_ref):
        x = u_ref[0]
        w = w_ref[0]
        x1, x2, x3 = _shift_down(x, 1), _shift_down(x, 2), _shift_down(x, 3)
        pre = w[3:4] * x + w[2:3] * x1 + w[1:2] * x2 + w[0:1] * x3
        sig = _sigmoid(pre)
        dpre = d_ref[0] * (sig * (1.0 + pre * (1.0 - sig)))
        du = w[3:4] * dpre + w[2:3] * _shift_up(dpre, 1) + w[1:2] * _shift_up(dpre, 2) + w[0:1] * _shift_up(dpre, 3)
        du_ref[0] = du.astype(BF16)
        dw = jnp.concatenate([jnp.sum(dpre * x3, axis=0, keepdims=True), jnp.sum(dpre * x2, axis=0, keepdims=True),
                              jnp.sum(dpre * x1, axis=0, keepdims=True), jnp.sum(dpre * x, axis=0, keepdims=True),
                              jnp.zeros((4, LANES), F32)], axis=0)
        dw_ref[0] = dw

    spec = pl.BlockSpec((1, S, LANES), lambda g, h: (g, 0, h))
    wspec = pl.BlockSpec((1, 8, LANES), lambda g, h: (g, 0, h))
    return pl.pallas_call(
        body, name="qk_conv_bwd", grid=(8, nh),
        in_specs=[spec, wspec, spec],
        out_specs=[spec, wspec],
        out_shape=[SDS((8, S, GROUP), BF16), SDS((8, 8, GROUP), F32)],
        compiler_params=_params(("parallel", "parallel")),
    )(u, qw, dqk)


def _head_masks():
    lane = lax.broadcasted_iota(jnp.int32, (1, D_XA), 1)
    return [(lane >= h * XA_HEAD_DIM) & (lane < (h + 1) * XA_HEAD_DIM) for h in range(XA_HEADS)]


def xattn_fwd(u, qg, kv):
    _, S, _ = u.shape
    ts = _tile(S, 512)
    scale = XA_HEAD_DIM ** -0.5

    def body(q_ref, kv_ref, o_ref):
        q = q_ref[0]
        k = kv_ref[0].astype(BF16)
        v = kv_ref[1]
        o = jnp.zeros((ts, D_XA), F32)
        for m in _head_masks():
            s = _dot(jnp.where(m, q, 0.0).astype(BF16), k, NT) * scale
            s = s - jnp.max(s, axis=-1, keepdims=True)
            e = jnp.exp(s)
            p = e / jnp.sum(e, axis=-1, keepdims=True)
            o = o + _dot(p.astype(BF16), jnp.where(m, v, 0.0).astype(BF16), NN)
        o_ref[0] = o.astype(BF16)

    return pl.pallas_call(
        body, name="xattn_fwd", grid=(S // ts,),
        in_specs=[pl.BlockSpec((1, ts, GROUP), lambda s: (qg, s, 0)), pl.BlockSpec((2, N_MEM, GROUP), lambda s: (0, 0, 0))],
        out_specs=pl.BlockSpec((1, ts, GROUP), lambda s: (0, s, 0)),
        out_shape=SDS((1, S, GROUP), BF16),
        compiler_params=_params(("parallel",)),
    )(u, kv)


def xattn_bwd(u, qg, kv, dm, dg):
    _, S, _ = u.shape
    ts = _tile(S, 512)
    scale = XA_HEAD_DIM ** -0.5

    def body(q_ref, kv_ref, do_ref, dq_ref, dkv_ref):
        @pl.when(pl.program_id(0) == 0)
        def _():
            dkv_ref[...] = jnp.zeros_like(dkv_ref)

        q = q_ref[0]
        k = kv_ref[0]
        v = kv_ref[1]
        kb = k.astype(BF16)
        do = do_ref[0]
        dq = jnp.zeros((ts, D_XA), F32)
        dk = jnp.zeros((N_MEM, D_XA), F32)
        dv = jnp.zeros((N_MEM, D_XA), F32)
        for m in _head_masks():
            qm = jnp.where(m, q, 0.0).astype(BF16)
            s = _dot(qm, kb, NT) * scale
            s = s - jnp.max(s, axis=-1, keepdims=True)
            e = jnp.exp(s)
            p = e / jnp.sum(e, axis=-1, keepdims=True)
            dom = jnp.where(m, do, 0.0).astype(BF16)
            dp = _dot(dom, jnp.where(m, v, 0.0).astype(BF16), NT)
            ds = (p * (dp - jnp.sum(dp * p, axis=-1, keepdims=True)) * scale).astype(BF16)
            dq = dq + _dot(ds, jnp.where(m, k, 0.0).astype(BF16), NN)
            dk = dk + _dot(ds, qm, TN)
            dv = dv + _dot(p.astype(BF16), dom, TN)
        dq_ref[0] = dq.astype(BF16)
        dkv_ref[0] += dk
        dkv_ref[1] += dv

    return pl.pallas_call(
        body, name="xattn_bwd", grid=(S // ts,),
        in_specs=[pl.BlockSpec((1, ts, GROUP), lambda s: (qg, s, 0)), pl.BlockSpec((2, N_MEM, GROUP), lambda s: (0, 0, 0)),
                  pl.BlockSpec((1, ts, GROUP), lambda s: (dg, s, 0))],
        out_specs=[pl.BlockSpec((1, ts, GROUP), lambda s: (0, s, 0)), pl.BlockSpec((2, N_MEM, GROUP), lambda s: (0, 0, 0))],
        out_shape=[SDS((1, S, GROUP), BF16), SDS((2, N_MEM, GROUP), F32)],
        compiler_params=_params(("arbitrary",)),
    )(u, kv, dm)


ML_BLOCK_CHUNKS = 4
H4 = ML_HEADS
L = ML_CHUNK
NLANE = ML_HEAD_DIM


def _chunk_consts():
    r = lax.broadcasted_iota(jnp.int32, (1, L, L), 1)
    c = lax.broadcasted_iota(jnp.int32, (1, L, L), 2)
    return r >= c, r <= c, r == c


def _gate_cols(gb):
    lane = lax.broadcasted_iota(jnp.int32, gb.shape, 1)
    li = jnp.stack([jnp.sum(jnp.where(lane == h, gb, 0.0), axis=1, keepdims=True) for h in range(H4)])
    gf = jnp.stack([jnp.sum(jnp.where(lane == H4 + h, gb, 0.0), axis=1, keepdims=True) for h in range(H4)])
    return li, gf


def _log_sigmoid(x):
    return jnp.minimum(x, 0.0) - jnp.log(1.0 + jnp.exp(-jnp.abs(x)))


def _chunk_forward(q, k, v_aug, li_col, lf_col, c_prev, m_prev):
    tri, tri_t, eye = _chunk_consts()
    lf_row = jnp.sum(jnp.where(eye, lf_col, 0.0), axis=1, keepdims=True)
    li_row = jnp.sum(jnp.where(eye, li_col, 0.0), axis=1, keepdims=True)
    bcum_col = jnp.sum(jnp.where(tri, lf_row, 0.0), axis=2, keepdims=True)
    bcum_row = jnp.sum(jnp.where(tri_t, lf_col, 0.0), axis=1, keepdims=True)
    log_d = jnp.where(tri, bcum_col - bcum_row + li_row, NEG)
    log_inter = bcum_col + m_prev
    m_t = jnp.maximum(log_inter, jnp.max(log_d, axis=2, keepdims=True))
    w_intra = jnp.exp(log_d - m_t)
    w_inter = jnp.exp(log_inter - m_t)
    sc = _bdot(q, k, 2, 2) * w_intra
    qc = _bdot(q, c_prev, 2, 1)
    num = _bdot(sc, v_aug, 2, 1) + w_inter * qc
    lane = lax.broadcasted_iota(jnp.int32, num.shape, 2)
    den = jnp.sum(jnp.where(lane == NLANE, num, 0.0), axis=2, keepdims=True)
    e_m = jnp.exp(-m_t)
    b_last = jnp.sum(lf_row, axis=2, keepdims=True)
    log_w = b_last - bcum_col + li_col
    m_new = jnp.maximum(b_last + m_prev, jnp.max(log_w, axis=1, keepdims=True))
    w_k = jnp.exp(log_w - m_new)
    decay = jnp.exp(b_last + m_prev - m_new)
    return dict(w_intra=w_intra, w_inter=w_inter, sc=sc, qc=qc, num=num, den=den, e_m=e_m, lane=lane,
                w_k=w_k, decay=decay, m_new=m_new)


def mlstm_fwd(qk, u, bg):
    _, S, _ = qk.shape
    nc = S // L
    cb = min(ML_BLOCK_CHUNKS, nc)
    rows = cb * L
    kscale = ML_HEAD_DIM ** -0.5

    def body(qk_ref, v_ref, g_ref, bg_ref, h_ref, cst_ref, mst_ref, c_sc, m_sc):
        @pl.when(pl.program_id(0) == 0)
        def _():
            c_sc[...] = jnp.zeros_like(c_sc)
            m_sc[...] = jnp.zeros_like(m_sc)

        for c in range(cb):
            sl = pl.ds(c * L, L)
            q = qk_ref[0:H4, sl, :]
            k = qk_ref[H4:2 * H4, sl, :] * kscale
            v = v_ref[:, sl, :]
            lane = lax.broadcasted_iota(jnp.int32, v.shape, 2)
            v_aug = jnp.where(lane == NLANE, 1.0, v)
            li_col, gf = _gate_cols(g_ref[0, sl, :] + bg_ref[...])
            lf_col = _log_sigmoid(gf)
            c_prev = c_sc[...]
            m_prev = m_sc[...]
            f = _chunk_forward(q, k, v_aug, li_col, lf_col, c_prev, m_prev)
            r = 1.0 / jnp.maximum(jnp.abs(f["den"]), f["e_m"])
            h_ref[:, sl, :] = jnp.where(lane < NLANE, f["num"] * r, 0.0)
            cst_ref[c] = c_prev
            mst_ref[c] = jnp.broadcast_to(m_prev, (H4, 1, LANES))
            c_sc[...] = f["decay"] * c_prev + _bdot(k * f["w_k"], v_aug, 1, 1)
            m_sc[...] = f["m_new"]

    def hspec(blk):
        return pl.BlockSpec((H4, rows, GROUP), lambda i: (blk, i, 0))

    return pl.pallas_call(
        body, name="mlstm_fwd", grid=(nc // cb,),
        in_specs=[pl.BlockSpec((2 * H4, rows, GROUP), lambda i: (0, i, 0)), hspec(2),
                  pl.BlockSpec((1, rows, GROUP), lambda i: (17, i, 0)), pl.BlockSpec((1, GROUP), lambda i: (0, 0))],
        out_specs=[hspec(0), pl.BlockSpec((cb, H4, GROUP, GROUP), lambda i: (i, 0, 0, 0)),
                   pl.BlockSpec((cb, H4, 1, LANES), lambda i: (i, 0, 0, 0))],
        out_shape=[SDS((H4, S, GROUP), F32), SDS((nc, H4, GROUP, GROUP), F32), SDS((nc, H4, 1, LANES), F32)],
        scratch_shapes=[pltpu.VMEM((H4, GROUP, GROUP), F32), pltpu.VMEM((H4, 1, 1), F32)],
        compiler_params=_params(("arbitrary",)),
    )(qk, u, u, bg)


def mlstm_bwd(qk, u, bg, cst, mst, dh):
    _, S, _ = qk.shape
    nc = S // L
    cb = min(ML_BLOCK_CHUNKS, nc)
    rows = cb * L
    nb = nc // cb
    kscale = ML_HEAD_DIM ** -0.5

    def body(qk_ref, v_ref, g_ref, bg_ref, cst_ref, mst_ref, dh_ref, dqk_ref, dv_ref, dg_ref, dbg_ref, dc_sc):
        @pl.when(pl.program_id(0) == 0)
        def _():
            dc_sc[...] = jnp.zeros_like(dc_sc)
            dbg_ref[...] = jnp.zeros_like(dbg_ref)

        tri, tri_t, eye = _chunk_consts()
        for c in reversed(range(cb)):
            sl = pl.ds(c * L, L)
            q = qk_ref[0:H4, sl, :]
            k = qk_ref[H4:2 * H4, sl, :] * kscale
            v = v_ref[:, sl, :]
            lane = lax.broadcasted_iota(jnp.int32, v.shape, 2)
            v_aug = jnp.where(lane == NLANE, 1.0, v)
            li_col, gf = _gate_cols(g_ref[0, sl, :] + bg_ref[...])
            lf_col = _log_sigmoid(gf)
            c_prev = cst_ref[c]
            m_prev = mst_ref[c][:, :, 0:1]
            f = _chunk_forward(q, k, v_aug, li_col, lf_col, c_prev, m_prev)
            w_intra, w_inter, sc, num, den, e_m = f["w_intra"], f["w_inter"], f["sc"], f["num"], f["den"], f["e_m"]
            absd = jnp.abs(den)
            r = 1.0 / jnp.maximum(absd, e_m)
            dhv = dh_ref[:, sl, :]
            s1 = jnp.sum(jnp.where(lane < NLANE, dhv * num, 0.0), axis=2, keepdims=True)
            dden = jnp.where(absd > e_m, -s1 * r * r * jnp.sign(den), 0.0)
            dnum = jnp.where(lane == NLANE, dden, jnp.where(lane < NLANE, dhv * r, 0.0))
            dsc = _bdot(dnum, v_aug, 2, 2)
            dv = _bdot(sc, dnum, 1, 1)
            gmat = dsc * sc
            dqk = dsc * w_intra
            dq = _bdot(dqk, k, 2, 1) + w_inter * _bdot(dnum, c_prev, 2, 2)
            dk = _bdot(dqk, q, 1, 1)
            dc_prev = _bdot(q * w_inter, dnum, 1, 1)
            dlog_inter = jnp.sum(dnum * f["qc"], axis=2, keepdims=True) * w_inter
            dbcum_col = dlog_inter + jnp.sum(gmat, axis=2, keepdims=True)
            g_row = jnp.sum(gmat, axis=1, keepdims=True)
            dcn = dc_sc[...]
            w_k, decay = f["w_k"], f["decay"]
            kw = k * w_k
            dc_prev = dc_prev + decay * dcn
            db_last = jnp.sum(jnp.sum(dcn * c_prev, axis=2, keepdims=True), axis=1, keepdims=True) * decay
            dkw = _bdot(v_aug, dcn, 2, 2)
            dv = dv + _bdot(kw, dcn, 2, 1)
            dk = dk + dkw * w_k
            dlogw = jnp.sum(dkw * k, axis=2, keepdims=True) * w_k
            db_last = db_last + jnp.sum(dlogw, axis=1, keepdims=True)
            dbcum_col = dbcum_col - dlogw
            rowi = lax.broadcasted_iota(jnp.int32, (1, L, 1), 1)
            dbcum_col = dbcum_col + jnp.where(rowi == L - 1, db_last, 0.0)
            dbcum_row = jnp.sum(jnp.where(eye, dbcum_col, 0.0), axis=1, keepdims=True) - g_row
            dlf_col = jnp.sum(jnp.where(tri_t, dbcum_row, 0.0), axis=2, keepdims=True)
            dli_col = dlogw + jnp.sum(jnp.where(eye, g_row, 0.0), axis=2, keepdims=True)
            dgf_col = dlf_col * _sigmoid(-gf)
            lane_g = lax.broadcasted_iota(jnp.int32, (L, GROUP), 1)
            dg = jnp.zeros((L, GROUP), F32)
            for h in range(H4):
                dg = dg + jnp.where(lane_g == h, dli_col[h], 0.0) + jnp.where(lane_g == H4 + h, dgf_col[h], 0.0)
            dqk_ref[0:H4, sl, :] = dq
            dqk_ref[H4:2 * H4, sl, :] = dk * kscale
            dv_ref[:, sl, :] = jnp.where(lane < NLANE, dv, 0.0).astype(BF16)
            dg_ref[0, sl, :] = dg.astype(BF16)
            dbg_ref[...] += jnp.sum(dg, axis=0, keepdims=True)
            dc_sc[...] = dc_prev

    def hspec(blk):
        return pl.BlockSpec((H4, rows, GROUP), lambda i: (blk, nb - 1 - i, 0))

    gspec = pl.BlockSpec((1, rows, GROUP), lambda i: (17, nb - 1 - i, 0))
    qkspec = pl.BlockSpec((2 * H4, rows, GROUP), lambda i: (0, nb - 1 - i, 0))
    return pl.pallas_call(
        body, name="mlstm_bwd", grid=(nb,),
        in_specs=[qkspec, hspec(2), gspec, pl.BlockSpec((1, GROUP), lambda i: (0, 0)),
                  pl.BlockSpec((cb, H4, GROUP, GROUP), lambda i: (nb - 1 - i, 0, 0, 0)),
                  pl.BlockSpec((cb, H4, 1, LANES), lambda i: (nb - 1 - i, 0, 0, 0)), hspec(0)],
        out_specs=[qkspec, hspec(0), pl.BlockSpec((1, rows, GROUP), lambda i: (0, nb - 1 - i, 0)),
                   pl.BlockSpec((1, GROUP), lambda i: (0, 0))],
        out_shape=[SDS((2 * H4, S, GROUP), F32), SDS((H4, S, GROUP), BF16),
                   SDS((1, S, GROUP), BF16), SDS((1, GROUP), F32)],
        scratch_shapes=[pltpu.VMEM((H4, GROUP, GROUP), F32)],
        compiler_params=_params(("arbitrary",)),
    )(qk, u, u, bg, cst, mst, dh)


def head_norm_fwd(hm, u, hg):
    _, S, _ = hm.shape
    ts = _tile(S, 512)

    def body(h_ref, o_ref, g_ref, t_ref):
        h = h_ref[0]
        lane = lax.broadcasted_iota(jnp.int32, h.shape, 1)
        valid = lane < ML_HEAD_DIM
        mu = jnp.sum(h, axis=-1, keepdims=True) * (1.0 / ML_HEAD_DIM)
        hc = jnp.where(valid, h - mu, 0.0)
        var = jnp.sum(hc * hc, axis=-1, keepdims=True) * (1.0 / ML_HEAD_DIM)
        hn = hc * lax.rsqrt(var + LN_EPS) * g_ref[0]
        t_ref[0] = (_sigmoid(o_ref[0]) * hn).astype(BF16)

    return pl.pallas_call(
        body, name="head_norm_fwd", grid=(H4, S // ts),
        in_specs=[pl.BlockSpec((1, ts, GROUP), lambda h, s: (h, s, 0)), pl.BlockSpec((1, ts, GROUP), lambda h, s: (12 + h, s, 0)),
                  pl.BlockSpec((1, 1, GROUP), lambda h, s: (h, 0, 0))],
        out_specs=pl.BlockSpec((1, ts, GROUP), lambda h, s: (h, s, 0)),
        out_shape=SDS((H4, S, GROUP), BF16),
        compiler_params=_params(("parallel", "parallel")),
    )(hm, u, hg)


def head_norm_bwd(hm, u, hg, dm):
    _, S, _ = hm.shape
    ts = _tile(S, 512)

    def body(h_ref, o_ref, g_ref, d_ref, dh_ref, do_ref, dg_ref):
        @pl.when(pl.program_id(1) == 0)
        def _():
            dg_ref[...] = jnp.zeros_like(dg_ref)

        h = h_ref[0]
        lane = lax.broadcasted_iota(jnp.int32, h.shape, 1)
        valid = lane < ML_HEAD_DIM
        inv = 1.0 / ML_HEAD_DIM
        mu = jnp.sum(h, axis=-1, keepdims=True) * inv
        hc = jnp.where(valid, h - mu, 0.0)
        var = jnp.sum(hc * hc, axis=-1, keepdims=True) * inv
        rstd = lax.rsqrt(var + LN_EPS)
        xhat = hc * rstd
        g = g_ref[0]
        sig = _sigmoid(o_ref[0])
        dt = jnp.where(valid, d_ref[0], 0.0)
        do_ref[0] = (dt * xhat * g * sig * (1.0 - sig)).astype(BF16)
        dhn = dt * sig
        dg_ref[0] += jnp.sum(dhn * xhat, axis=0, keepdims=True)
        dxh = dhn * g
        m1 = jnp.sum(dxh, axis=-1, keepdims=True) * inv
        m2 = jnp.sum(dxh * xhat, axis=-1, keepdims=True) * inv
        dh_ref[0] = jnp.where(valid, rstd * (dxh - m1 - xhat * m2), 0.0)

    spec = pl.BlockSpec((1, ts, GROUP), lambda h, s: (h, s, 0))
    gspec = pl.BlockSpec((1, 1, GROUP), lambda h, s: (h, 0, 0))
    return pl.pallas_call(
        body, name="head_norm_bwd", grid=(H4, S // ts),
        in_specs=[spec, pl.BlockSpec((1, ts, GROUP), lambda h, s: (12 + h, s, 0)), gspec, spec],
        out_specs=[spec, spec, gspec],
        out_shape=[SDS((H4, S, GROUP), F32), SDS((H4, S, GROUP), BF16), SDS((H4, 1, GROUP), F32)],
        compiler_params=_params(("parallel", "arbitrary")),
    )(hm, u, hg, dm)


def _adamw_math(w, g, m, v):
    c1 = 1.0 / (1.0 - ADAM_B1 ** ADAM_STEP)
    c2 = 1.0 / (1.0 - ADAM_B2 ** ADAM_STEP)
    nm = ADAM_B1 * m + (1.0 - ADAM_B1) * g
    nv = ADAM_B2 * v + (1.0 - ADAM_B2) * (g * g)
    return -ADAM_LR * ((nm * c1) / (jnp.sqrt(nv * c2) + ADAM_EPS) + ADAM_WD * w), nm, nv


def _row_tile(R, cap=512):
    return R if R <= cap else max(d for d in range(8, cap + 1, 8) if R % d == 0)


def adamw_into(w, m, v, g, outs, idx, after, name):
    R, C = g.shape
    tr = _row_tile(R)
    lead = (0,) * len(idx)

    def body(w_ref, m_ref, v_ref, g_ref, *rest):
        go_ref, d_ref, nm_ref, nv_ref, token = rest[-5:]
        token[...] = jnp.zeros_like(token)
        gv = g_ref[...]
        d, nm, nv = _adamw_math(w_ref[lead], gv, m_ref[lead], v_ref[lead])
        go_ref[lead] = gv
        d_ref[lead] = d
        nm_ref[lead] = nm
        nv_ref[lead] = nv

    blk = pl.BlockSpec((1,) * len(idx) + (tr, C), lambda r: idx + (r, 0))
    any_space = pl.BlockSpec(memory_space=pl.ANY)
    in_specs, args, aliases = [blk, blk, blk, pl.BlockSpec((tr, C), lambda r: (r, 0)), any_space], [w, m, v, g, g if after is None else after], {}
    if outs is not None:
        in_specs += [any_space] * 4
        args += list(outs)
        aliases = {5 + i: i for i in range(4)}
    out = pl.pallas_call(
        body, name=name, grid=(R // tr,),
        in_specs=in_specs, out_specs=[blk] * 4 + [pl.BlockSpec((8, LANES), lambda r: (0, 0))],
        out_shape=[SDS(w.shape, F32)] * 4 + [SDS((8, LANES), F32)],
        input_output_aliases=aliases, compiler_params=_params(("arbitrary",)),
    )(*args)
    return out[:4], out[4]


def adamw(w, g, m, v, name):
    R, C = w.shape
    tr = _row_tile(R)

    def body(w_ref, g_ref, m_ref, v_ref, d_ref, nm_ref, nv_ref):
        d_ref[...], nm_ref[...], nv_ref[...] = _adamw_math(w_ref[...], g_ref[...], m_ref[...], v_ref[...])

    spec = pl.BlockSpec((tr, C), lambda i: (i, 0))
    return pl.pallas_call(
        body, name=name, grid=(R // tr,),
        in_specs=[spec] * 4, out_specs=[spec] * 3,
        out_shape=[SDS((R, C), F32)] * 3,
        compiler_params=_params(("parallel",)),
    )(w, g, m, v)


HBM = pl.BlockSpec(memory_space=pl.ANY)
ROW_SPLIT = 4


def _position():
    x, y, c = lax.axis_index("x"), lax.axis_index("y"), lax.axis_index("c")
    return x, y, c, [(1 - x, y), (x, 1 - y), (1 - x, 1 - y)]


def _unique(items):
    arrays = []
    for a, _ in items:
        if not any(a is b for b in arrays):
            arrays.append(a)
    return arrays, [next(i for i, b in enumerate(arrays) if b is a) for a, _ in items]


def place_own(items, me, after, name):
    arrays, src_of = _unique(items)
    n = len(items)
    shapes = [a.shape[len(p):] for a, p in items]

    def body(me_ref, *refs):
        for t in range(n):
            refs[n + 1 + t][...] = jnp.zeros_like(refs[n + 1 + t])
            refs[n + 1 + t][me_ref[0]] = refs[t][(0,) * len(items[t][1])]

    in_specs, out_specs = [], []
    for (a, p), shp in zip(items, shapes):
        blk = shp[:-2] + (shp[-2] // ROW_SPLIT, shp[-1])
        lead = (0,) * (len(shp) - 2)
        in_specs.append(pl.BlockSpec((1,) * len(p) + blk, functools.partial(lambda r, me_ref, p, lead: p + lead + (r, 0), p=p, lead=lead)))
        out_specs.append(pl.BlockSpec((N_CHIPS,) + blk, functools.partial(lambda r, me_ref, lead: (0,) + lead + (r, 0), lead=lead)))
    in_specs.append(pl.BlockSpec(memory_space=pl.ANY))
    return pl.pallas_call(
        body, name=name,
        grid_spec=pltpu.PrefetchScalarGridSpec(num_scalar_prefetch=1, grid=(ROW_SPLIT,), in_specs=in_specs, out_specs=out_specs),
        out_shape=[SDS((N_CHIPS,) + tuple(shp), a.dtype) for shp, (a, _) in zip(shapes, items)],
        compiler_params=_params(("parallel",)),
    )(me, *[arrays[i] for i in src_of], after)


SEM = pl.BlockSpec(memory_space=pltpu.SEMAPHORE)
IN_HBM = pl.BlockSpec(memory_space=pltpu.HBM)
DATAFLOW = pltpu.SideEffectType.DATAFLOW_SIDE_EFFECTING


def split_start(bufs, plan, n_copies, after, name):
    n = len(bufs)

    def body(*refs):
        send, recv, token = refs[n + 1], refs[n + 2], refs[-1]
        x, y, c, chips = _position()
        for k, (src, dst, dev) in enumerate(plan(refs[:n], x, y, c, chips)):
            pltpu.make_async_remote_copy(src_ref=src, dst_ref=dst, send_sem=send.at[k], recv_sem=recv.at[k],
                                         device_id=dev, device_id_type=MESH).start()
        token[...] = jnp.zeros_like(token)

    out = pl.pallas_call(
        body, name=name,
        out_shape=(pltpu.SemaphoreType.DMA((n_copies,)), pltpu.SemaphoreType.DMA((n_copies,)),
                   *[pltpu.HBM(b.shape, b.dtype) for b in bufs], SDS((8, LANES), F32)),
        in_specs=[IN_HBM] * n + [pl.BlockSpec(memory_space=pl.ANY)],
        out_specs=(SEM, SEM, *[IN_HBM] * n, pl.BlockSpec(memory_space=pltpu.VMEM)),
        input_output_aliases={i: 2 + i for i in range(n)},
        compiler_params=pltpu.CompilerParams(has_side_effects=DATAFLOW),
    )(*[pltpu.with_memory_space_constraint(b, pltpu.HBM) for b in bufs], after)
    return out[0], out[1], list(out[2:2 + n]), out[-1]


def split_wait(send, recv, bufs, plan, after, name):
    n = len(bufs)

    def body(*refs):
        send_ref, recv_ref = refs[n], refs[n + 1]
        x, y, c, chips = _position()
        for k, (src, dst, dev) in enumerate(plan(refs[:n], x, y, c, chips)):
            cp = pltpu.make_async_remote_copy(src_ref=src, dst_ref=dst, send_sem=send_ref.at[k], recv_sem=recv_ref.at[k],
                                              device_id=dev, device_id_type=MESH)
            cp.wait_send()
            cp.wait_recv()

    return list(pl.pallas_call(
        body, name=name, out_shape=tuple(pltpu.HBM(b.shape, b.dtype) for b in bufs),
        in_specs=[IN_HBM] * n + [SEM, SEM, pl.BlockSpec(memory_space=pl.ANY)], out_specs=tuple([IN_HBM] * n),
        input_output_aliases={i: i for i in range(n)},
        compiler_params=pltpu.CompilerParams(has_side_effects=DATAFLOW),
    )(*bufs, send, recv, after))


def _gather_plan(shapes, landing):
    n = len(shapes)

    def plan(refs, x, y, c, chips):
        out = []
        for t in range(n):
            half = shapes[t][0] // 2
            rows = pl.ds(c * half, half)
            for cx, cy in chips:
                slot = 2 * cx + cy if landing else 2 * x + y
                out.append((refs[t].at[rows], refs[n + t].at[slot, rows], (cx, cy, c)))
        return out

    return plan


def gather_start(shards, placed, after, name):
    shapes = [s.shape for s in shards]
    send, recv, bufs, token = split_start(list(shards) + list(placed), _gather_plan(shapes, False), 3 * len(shards), after, name)
    return (send, recv, bufs, shapes), token


def gather_wait(state, after, name):
    send, recv, bufs, shapes = state
    return split_wait(send, recv, bufs, _gather_plan(shapes, True), after, name)[len(shapes):]


def gather_pass_on(placed, shapes, name):
    n = len(placed)

    def body(*refs):
        outs, send, recv = refs[n:2 * n], refs[2 * n], refs[2 * n + 1]
        x, y, c, chips = _position()
        cps = []
        for t in range(n):
            half = shapes[t][0] // 2
            for j, (cx, cy) in enumerate(chips):
                piece = outs[t].at[2 * cx + cy, pl.ds(c * half, half)]
                cp = pltpu.make_async_remote_copy(src_ref=piece, dst_ref=piece, send_sem=send.at[3 * t + j], recv_sem=recv.at[3 * t + j],
                                                  device_id=(x, y, 1 - c), device_id_type=MESH)
                cp.start()
                cps.append(cp)
        for t in range(n):
            half = shapes[t][0] // 2
            for j, (cx, cy) in enumerate(chips):
                piece = outs[t].at[2 * cx + cy, pl.ds((1 - c) * half, half)]
                pltpu.make_async_remote_copy(src_ref=piece, dst_ref=piece, send_sem=send.at[3 * t + j], recv_sem=recv.at[3 * t + j],
                                             device_id=(x, y, 1 - c), device_id_type=MESH).wait_recv()
        for cp in cps:
            cp.wait_send()

    return pl.pallas_call(
        body, name=name,
        in_specs=[HBM] * n, out_specs=[HBM] * n,
        out_shape=[SDS(p.shape, p.dtype) for p in placed],
        input_output_aliases={t: t for t in range(n)},
        scratch_shapes=[pltpu.SemaphoreType.DMA((3 * n,))] * 2,
    )(*placed)


def _flip(k, x, y, c):
    return ((1 - x) if k & 4 else x, (1 - y) if k & 2 else y, (1 - c) if k & 1 else c)


def small_allgather(v, reduce):
    R, C = v.shape

    def body(v_ref, o_ref, *scratch):
        if reduce:
            buf, send, recv = scratch
        else:
            buf, (send, recv) = o_ref, scratch
        x, y, c, _ = _position()
        me = 4 * x + 2 * y + c
        buf[me] = v_ref[...]
        sends = []
        for k in range(1, N_DEV):
            cp = pltpu.make_async_remote_copy(src_ref=v_ref, dst_ref=buf.at[me], send_sem=send.at[k - 1], recv_sem=recv.at[k - 1],
                                              device_id=_flip(k, x, y, c), device_id_type=MESH)
            cp.start()
            sends.append(cp)
        for k in range(1, N_DEV):
            px, py, pc = _flip(k, x, y, c)
            pltpu.make_async_remote_copy(src_ref=v_ref, dst_ref=buf.at[4 * px + 2 * py + pc], send_sem=send.at[k - 1],
                                         recv_sem=recv.at[k - 1], device_id=(px, py, pc), device_id_type=MESH).wait_recv()
        for cp in sends:
            cp.wait_send()
        if reduce:
            acc = buf[0]
            for i in range(1, N_DEV):
                acc = acc + buf[i]
            o_ref[...] = acc

    vm = pl.BlockSpec(memory_space=pltpu.VMEM)
    sems = [pltpu.SemaphoreType.DMA((N_DEV - 1,)), pltpu.SemaphoreType.DMA((N_DEV - 1,))]
    return pl.pallas_call(
        body, name="small_allreduce" if reduce else "small_allgather",
        in_specs=[vm], out_specs=vm,
        out_shape=SDS((R, C) if reduce else (N_DEV, R, C), F32),
        scratch_shapes=([pltpu.VMEM((N_DEV, R, C), F32)] if reduce else []) + sems,
    )(v)


def rs_exchange_sibling(gs):
    n = len(gs)

    def body(*refs):
        ins, outs, send, recv = refs[:n], refs[n:2 * n], refs[2 * n], refs[2 * n + 1]
        x, y, c, _ = _position()
        cps = []
        for t in range(n):
            cp = pltpu.make_async_remote_copy(src_ref=ins[t].at[:, 1 - c], dst_ref=outs[t], send_sem=send.at[t], recv_sem=recv.at[t],
                                              device_id=(x, y, 1 - c), device_id_type=MESH)
            cp.start()
            cps.append(cp)
        for cp in cps:
            cp.wait()

    return pl.pallas_call(
        body, name="rs_exchange_sibling", in_specs=[HBM] * n, out_specs=[HBM] * n,
        out_shape=[SDS((g.shape[0],) + g.shape[2:], g.dtype) for g in gs],
        scratch_shapes=[pltpu.SemaphoreType.DMA((n,)), pltpu.SemaphoreType.DMA((n,))],
    )(*gs)


def rs_pair_add(gs, rs, c):
    n = len(gs)

    def body(c_ref, *refs):
        for t in range(n):
            refs[2 * n + t][0] = (refs[t][0, 0].astype(F32) + refs[n + t][0].astype(F32)).astype(BF16)

    in_specs, out_specs, out_shape = [], [], []
    for g in gs:
        _, _, h, C = g.shape
        in_specs.append(pl.BlockSpec((1, 1, h // ROW_SPLIT, C), lambda j, r, c_ref: (j, c_ref[0], r, 0)))
    for g in gs:
        _, _, h, C = g.shape
        spec = pl.BlockSpec((1, h // ROW_SPLIT, C), lambda j, r, c_ref: (j, r, 0))
        in_specs.append(spec)
        out_specs.append(spec)
        out_shape.append(SDS((N_CHIPS, h, C), BF16))
    return pl.pallas_call(
        body, name="rs_pair_add",
        grid_spec=pltpu.PrefetchScalarGridSpec(num_scalar_prefetch=1, grid=(N_CHIPS, ROW_SPLIT), in_specs=in_specs, out_specs=out_specs),
        out_shape=out_shape, compiler_params=_params(("parallel", "parallel")),
    )(c, *gs, *rs)


def _rs_plan(n):
    def plan(refs, x, y, c, chips):
        return [(refs[t].at[2 * cx + cy], refs[n + t].at[j], (cx, cy, c)) for t in range(n) for j, (cx, cy) in enumerate(chips)]

    return plan


def rs_chip_add(ps, qs, me_c):
    n = len(ps)

    def body(me_ref, *refs):
        for t in range(n):
            q = refs[n + t]
            refs[2 * n + t][...] = jnp.zeros_like(refs[2 * n + t])
            refs[2 * n + t][me_ref[1]] = ((refs[t][0].astype(F32) + q[0].astype(F32)) + q[1].astype(F32)) + q[2].astype(F32)

    in_specs, out_specs, out_shape = [], [], []
    for p in ps:
        _, h, C = p.shape
        in_specs.append(pl.BlockSpec((1, h // ROW_SPLIT, C), lambda r, me_ref: (me_ref[0], r, 0)))
    for p in ps:
        _, h, C = p.shape
        in_specs.append(pl.BlockSpec((3, h // ROW_SPLIT, C), lambda r, me_ref: (0, r, 0)))
        out_specs.append(pl.BlockSpec((2, h // ROW_SPLIT, C), lambda r, me_ref: (0, r, 0)))
        out_shape.append(SDS((2, h, C), F32))
    return pl.pallas_call(
        body, name="rs_chip_add",
        grid_spec=pltpu.PrefetchScalarGridSpec(num_scalar_prefetch=1, grid=(ROW_SPLIT,), in_specs=in_specs, out_specs=out_specs),
        out_shape=out_shape, compiler_params=_params(("parallel",)),
    )(me_c, *ps, *qs)


def rs_share(rs):
    n = len(rs)

    def body(*refs):
        outs, send, recv = refs[n:2 * n], refs[2 * n], refs[2 * n + 1]
        x, y, c, _ = _position()
        cps = []
        for t in range(n):
            cp = pltpu.make_async_remote_copy(src_ref=outs[t].at[c], dst_ref=outs[t].at[c], send_sem=send.at[t], recv_sem=recv.at[t],
                                              device_id=(x, y, 1 - c), device_id_type=MESH)
            cp.start()
            cps.append(cp)
        for cp in cps:
            cp.wait()

    return pl.pallas_call(
        body, name="rs_share", in_specs=[HBM] * n, out_specs=[HBM] * n,
        out_shape=[SDS(r.shape, r.dtype) for r in rs],
        input_output_aliases={t: t for t in range(n)},
        scratch_shapes=[pltpu.SemaphoreType.DMA((n,))] * 2,
    )(*rs)


def rs_begin(gs, after, name):
    c = lax.axis_index("c")
    n = len(gs)
    g5 = [g.reshape(N_CHIPS, 2, g.shape[1] // 2, g.shape[2]) for g in gs]
    from_sibling = rs_exchange_sibling(g5)
    pair = rs_pair_add(g5, from_sibling, jnp.reshape(c, (1,)).astype(jnp.int32))
    lands = [jnp.zeros((3,) + p.shape[1:], p.dtype) for p in pair]
    send, recv, bufs, token = split_start(list(pair) + lands, _rs_plan(n), 3 * n, from_sibling[0] if after is None else after, name)
    return (send, recv, bufs, [g.shape for g in gs]), token


def rs_end(state, after, name):
    x, y, c = lax.axis_index("x"), lax.axis_index("y"), lax.axis_index("c")
    send, recv, bufs, shapes = state
    n = len(shapes)
    bufs = split_wait(send, recv, bufs, _rs_plan(n), after, name)
    half = rs_chip_add(bufs[:n], bufs[n:], jnp.stack([2 * x + y, c]).astype(jnp.int32))
    both = rs_share(half)
    return [b.reshape(s[1], s[2]) for b, s in zip(both, shapes)]


def _pad_last(a, n):
    return jnp.pad(a, [(0, 0)] * (a.ndim - 1) + [(0, n - a.shape[-1])])


def _heads_to_groups(w):
    k = w.shape[0]
    return _pad_last(w.reshape(k, ML_HEADS, ML_HEAD_DIM).transpose(1, 0, 2), GROUP)


def _groups_to_heads(g):
    return g[:, :, :ML_HEAD_DIM].transpose(1, 0, 2).reshape(g.shape[1], D_TOK)


def _cols_to_groups(w):
    k, n = w.shape
    return w.reshape(k, n // GROUP, GROUP).transpose(1, 0, 2)


def _groups_to_cols(g):
    n, k, _ = g.shape
    return g.transpose(1, 0, 2).reshape(k, n * GROUP)


def _chips_to_cols(a):
    return a.transpose(1, 0, 2).reshape(a.shape[1], -1)


def _cols_to_chips(w):
    k, n = w.shape
    return w.reshape(k, N_CHIPS, n // N_CHIPS).transpose(1, 0, 2)


def _mlstm_in_groups(w):
    parts = [_heads_to_groups(w[:, i * D_TOK:(i + 1) * D_TOK]) for i in range(4)]
    gates = _pad_last(w[:, 4 * D_TOK:4 * D_TOK + 2 * ML_HEADS], GROUP)[None]
    qmem = w[:, 4 * D_TOK + 2 * ML_HEADS:][None]
    return jnp.concatenate(parts + [qmem, gates], axis=0)


def _mlstm_in_ungroup(g):
    parts = [_groups_to_heads(g[4 * i:4 * i + 4]) for i in range(4)]
    return jnp.concatenate(parts + [g[17][:, :2 * ML_HEADS], g[16]], axis=1)


def _taps_to_groups(w, width):
    taps = w.shape[0]
    g = _pad_last(w.reshape(taps, -1, width), GROUP).transpose(1, 0, 2)
    return jnp.pad(g, ((0, 0), (0, 8 - taps), (0, 0)))


def _groups_to_taps(g, taps, width):
    return g[:, :taps, :width].transpose(1, 0, 2).reshape(taps, -1)


SMALL_IN_COLS = 384
SMALL_OUT_COLS = 1536
SECTION = 8


class _Gathered:
    def __init__(self, srcs, groups, me):
        self.groups, self.states, self.ready = groups, [], {}
        self.group_of = {k: gi for gi, g in enumerate(groups) for k in g}
        token = me
        for gi, g in enumerate(groups):
            placed = place_own([(srcs[k], ()) for k in g], me, token, f"place_own_{gi}")
            state, token = gather_start([srcs[k] for k in g], placed, token, f"gather_start_{gi}")
            self.states.append(state)
        self.started = token

    def _get(self, key, after):
        gi = self.group_of[key]
        if gi not in self.ready:
            got = gather_wait(self.states[gi], after if gi else self.started, f"gather_wait_{gi}")
            self.ready[gi] = dict(zip(self.groups[gi], gather_pass_on(got, self.states[gi][3], f"gather_pass_on_{gi}")))
        return self.ready[gi][key]

    def ffn(self, l, i, after):
        return tuple(self._get((n, l, i), after) for n in ("wg", "wu", "wd"))

    def mixer(self, l, after):
        win = _chips_to_cols(self._get(("win", l), after))
        win = _cols_to_groups(win) if l % 2 == 0 else _mlstm_in_groups(win)
        wkv = _cols_to_groups(self._get(("wkv", l), after).reshape(D_MODEL, 2 * D_XA))
        wout = self._get(("wout", l), after)
        if l % 2:
            wout = wout.reshape(D_MODEL, D_MODEL)
            tok = jnp.pad(wout[:D_TOK].reshape(ML_HEADS, ML_HEAD_DIM, D_MODEL), ((0, 0), (0, GROUP - ML_HEAD_DIM), (0, 0)))
            wout = jnp.concatenate([tok, wout[D_TOK:][None]], axis=0)
        return win, wkv, wout


class _GradSink:
    def __init__(self, apply):
        self.queue, self.apply, self.count, self.done = [], apply, 0, None

    @staticmethod
    def _by_chip(key, g):
        if key[0] == "wkv":
            return _groups_to_cols(g).reshape(N_CHIPS, D_MODEL // N_CHIPS, 2 * D_XA)
        if key[0] == "win":
            return _cols_to_chips(_groups_to_cols(g) if key[1] % 2 == 0 else _mlstm_in_ungroup(g))
        if key[0] == "wout" and key[1] % 2:
            full = jnp.concatenate([g[:ML_HEADS, :ML_HEAD_DIM].reshape(D_TOK, D_MODEL), g[ML_HEADS]], axis=0)
            return full.reshape(N_CHIPS, D_MODEL // N_CHIPS, D_MODEL)
        return g

    def push(self, grads):
        keys = list(grads)
        state, token = rs_begin([self._by_chip(k, grads[k]) for k in keys], self.done, f"rs_start_{self.count}")
        if self.queue:
            self._finish(token)
        self.queue.append((keys, state, self.count))
        self.count += 1
        return token

    def flush(self):
        self._finish(self.done)

    def _finish(self, after):
        keys, state, i = self.queue.pop(0)
        for key, g in zip(keys, rs_end(state, after, f"rs_wait_{i}")):
            self.done = self.apply(key, g, self.done)


def _local_step(x, mem, tgt, P, weights, sink):
    memb = mem.astype(BF16)
    saved = []
    X, Xb = x, x.astype(BF16)
    after = Xb
    for l in range(DEPTH):
        s = {}
        s["x0b"] = Xb
        s["wa"] = weights.ffn(l, 0, after)
        s["g1a"], s["u1a"], s["ha"], s["z1"], X1, X1b = ffn_fwd(Xb, X, *s["wa"], P["ln_g"][l][0], P["ln_b"][l][0])
        s["x1b"] = X1b
        s["wm"] = win, wkv, wout = weights.mixer(l, X1b)
        u = proj(X1b, win, "mixer_in")
        kv = proj(memb, wkv, "mem_kv")
        s["u"], s["kv"] = u, kv
        if l % 2 == 0:
            tok = conv_mixer_fwd(u, P["convw"])
            qg = 9
        else:
            s["qk"] = qk_conv_fwd(u, P["qkw"])
            s["hm"], s["cst"], s["mst"] = mlstm_fwd(s["qk"], u, P["bg"])
            tok = head_norm_fwd(s["hm"], u, P["hg"])
            qg = 16
        xa = xattn_fwd(u, qg, kv)
        s["m"] = jnp.concatenate([tok, xa], axis=0)
        s["z2"], X2, X2b = contract_ln(s["m"], wout, X1, P["ln_g"][l][1], P["ln_b"][l][1], 1.0, "mixer_out_ln")
        s["x2b"] = X2b
        s["wb"] = weights.ffn(l, 1, X2b)
        s["g1b"], s["u1b"], s["hb"], s["z3"], X, Xb = ffn_fwd(X2b, X2, *s["wb"], P["ln_g"][l][2], P["ln_b"][l][2])
        after = Xb
        saved.append(s)

    loss, dX = loss_grad(X, tgt)

    G = {"ln_g": [[None] * 3 for _ in range(DEPTH)], "ln_b": [[None] * 3 for _ in range(DEPTH)]}
    pin = [jnp.zeros((1, 1), F32)]

    def ffn_backward(l, i, dX, z, xinb, g1, u1, h, w):
        k = 2 * i
        dz, dyb, G["ln_g"][l][k], G["ln_b"][l][k] = ln_bwd(dX, z, P["ln_g"][l][k] + pin[0], 0.5, "ffn_ln_bwd")
        dgb, dub, dx = ffn_bwd(dyb, dz, w[2], w[0], w[1], g1, u1)
        grads = {("wd", l, i): wgrad(h, dyb, BF16, "wgrad_down"), ("wg", l, i): wgrad(dgb, xinb, BF16, "wgrad_gate"),
                 ("wu", l, i): wgrad(dub, xinb, BF16, "wgrad_up")}
        return dx, grads

    for l in reversed(range(DEPTH)):
        s = saved[l]
        win, wkv, wout = s["wm"]
        dX, grads = ffn_backward(l, 1, dX, s["z3"], s["x2b"], s["g1b"], s["u1b"], s["hb"], s["wb"])
        dz2, dz2b, G["ln_g"][l][1], G["ln_b"][l][1] = ln_bwd(dX, s["z2"], P["ln_g"][l][1], 1.0, "mixer_ln_bwd")
        dm = proj_t(dz2b, wout, "mixer_out_bwd")
        grads[("wout", l)] = wgrad(s["m"], dz2b, BF16, "wgrad_out")
        u, kv = s["u"], s["kv"]
        if l % 2 == 0:
            db, dc, dxi, G["convw"] = conv_mixer_bwd(u, P["convw"], dm)
            dq, dkv = xattn_bwd(u, 9, kv, dm, 3)
            du = jnp.concatenate([db, dc, dxi, dq], axis=0)
        else:
            dh, do, G["hg"] = head_norm_bwd(s["hm"], u, P["hg"], dm)
            dqk, dv, dgate, G["bg"] = mlstm_bwd(s["qk"], u, P["bg"], s["cst"], s["mst"], dh)
            duqk, G["qkw"] = qk_conv_bwd(u, P["qkw"], dqk)
            dq, dkv = xattn_bwd(u, 16, kv, dm, 4)
            du = jnp.concatenate([duqk, dv, do, dq, dgate], axis=0)
        grads[("win", l)] = wgrad(s["x1b"], du, BF16, "wgrad_in")
        grads[("wkv", l)] = wgrad(memb, dkv.astype(BF16), BF16, "wgrad_kv")
        dX = contract_t(du, win, dz2, "mixer_in_bwd")
        pin[0] = sink.push(grads)[0:1, 0:1]
        dX, grads = ffn_backward(l, 0, dX, s["z1"], s["x0b"], s["g1a"], s["u1a"], s["ha"], s["wa"])
        pin[0] = sink.push(grads)[0:1, 0:1]
    sink.flush()
    return loss, dX, G


def kernel(x, mem, ln_g, ln_b, ffn_w_gate, ffn_w_up, ffn_w_down, w_kv_mem, w_out, w_in_conv, conv_w, w_in_mlstm, b_gates, qk_conv_w, head_norm_g, loss_target, m_ln_g, m_ln_b, m_ffn_w_gate, m_ffn_w_up, m_ffn_w_down, m_w_kv_mem, m_w_out, m_w_in_conv, m_conv_w, m_w_in_mlstm, m_b_gates, m_qk_conv_w, m_head_norm_g, v_ln_g, v_ln_b, v_ffn_w_gate, v_ffn_w_up, v_ffn_w_down, v_w_kv_mem, v_w_out, v_w_in_conv, v_conv_w, v_w_in_mlstm, v_b_gates, v_qk_conv_w, v_head_norm_g):
    cx, cy = lax.axis_index("x"), lax.axis_index("y")
    chip = 2 * cx + cy

    srcs = {}
    for l in range(DEPTH):
        for i in range(2):
            srcs[("wg", l, i)] = ffn_w_gate[l, i].astype(BF16)
            srcs[("wu", l, i)] = ffn_w_up[l, i].astype(BF16)
            srcs[("wd", l, i)] = ffn_w_down[l, i].astype(BF16)
        srcs[("wkv", l)] = w_kv_mem[l].astype(BF16)
        srcs[("wout", l)] = w_out[l].astype(BF16)
    srcs[("win", 0)] = w_in_conv[0].astype(BF16)
    srcs[("win", 1)] = w_in_mlstm[0].astype(BF16)
    ffn_keys = lambda l, i: [("wg", l, i), ("wu", l, i), ("wd", l, i)]
    mixer_keys = lambda l: [("win", l), ("wkv", l), ("wout", l)]
    groups = [ffn_keys(0, 0), mixer_keys(0) + mixer_keys(1), ffn_keys(0, 1), ffn_keys(1, 0), ffn_keys(1, 1)]
    gathered = _Gathered(srcs, groups, jnp.reshape(chip, (1,)).astype(jnp.int32))

    def section(a, width):
        a = a.reshape(-1, a.shape[-1])
        return jnp.pad(a, ((0, SECTION - a.shape[0]), (0, width - a.shape[1])))

    small = jnp.concatenate([section(a, SMALL_IN_COLS) for a in (ln_g, ln_b, conv_w, qk_conv_w)], axis=0)
    smalls = small_allgather(small, reduce=False)[0::2]
    ln_g_full = _chips_to_cols(smalls[:, 0:6, 0:256]).reshape(DEPTH, 3, 1, D_MODEL)
    ln_b_full = _chips_to_cols(smalls[:, 8:14, 0:256]).reshape(DEPTH, 3, 1, D_MODEL)
    conv_w_full = _chips_to_cols(smalls[:, 16:19, 0:192])
    qk_w_full = _chips_to_cols(smalls[:, 24:28, 0:384])

    P = {"ln_g": ln_g_full, "ln_b": ln_b_full, "convw": _taps_to_groups(conv_w_full, GROUP),
         "qkw": _taps_to_groups(qk_w_full, ML_HEAD_DIM), "bg": _pad_last(b_gates, GROUP),
         "hg": _pad_last(head_norm_g[0], GROUP)[:, None, :]}

    weights = {"ln_g": ln_g, "ln_b": ln_b, "ffn_w_gate": ffn_w_gate, "ffn_w_up": ffn_w_up, "ffn_w_down": ffn_w_down,
               "w_kv_mem": w_kv_mem, "w_out": w_out, "w_in_conv": w_in_conv, "conv_w": conv_w, "w_in_mlstm": w_in_mlstm,
               "b_gates": b_gates, "qk_conv_w": qk_conv_w, "head_norm_g": head_norm_g}
    ms = {"ln_g": m_ln_g, "ln_b": m_ln_b, "ffn_w_gate": m_ffn_w_gate, "ffn_w_up": m_ffn_w_up, "ffn_w_down": m_ffn_w_down,
          "w_kv_mem": m_w_kv_mem, "w_out": m_w_out, "w_in_conv": m_w_in_conv, "conv_w": m_conv_w, "w_in_mlstm": m_w_in_mlstm,
          "b_gates": m_b_gates, "qk_conv_w": m_qk_conv_w, "head_norm_g": m_head_norm_g}
    vs = {"ln_g": v_ln_g, "ln_b": v_ln_b, "ffn_w_gate": v_ffn_w_gate, "ffn_w_up": v_ffn_w_up, "ffn_w_down": v_ffn_w_down,
          "w_kv_mem": v_w_kv_mem, "w_out": v_w_out, "w_in_conv": v_w_in_conv, "conv_w": v_conv_w, "w_in_mlstm": v_w_in_mlstm,
          "b_gates": v_b_gates, "qk_conv_w": v_qk_conv_w, "head_norm_g": v_head_norm_g}
    names = list(weights)
    owner = {"wg": ("ffn_w_gate", True), "wu": ("ffn_w_up", True), "wd": ("ffn_w_down", False), "wkv": ("w_kv_mem", False),
             "wout": ("w_out", False), "win": None}
    updated = {}

    def apply(key, g, after):
        name, transposed = owner[key[0]] or (("w_in_conv", "w_in_mlstm")[key[1]], False)
        idx = (0,) if key[0] == "win" else tuple(key[1:])
        view = (lambda a: jnp.swapaxes(a, -1, -2)) if transposed else (lambda a: a)
        updated[name], token = adamw_into(view(weights[name]), view(ms[name]), view(vs[name]), g, updated.get(name), idx, after,
                                          "adamw_" + name + "_" + "_".join(map(str, idx)))
        return token

    sink = _GradSink(apply)
    loss, grad_x, G = _local_step(x[0], mem[0], loss_target[0], P, gathered, sink)

    dln_g = jnp.concatenate([G["ln_g"][l][k] for l in range(DEPTH) for k in range(3)], axis=0)
    dln_b = jnp.concatenate([G["ln_b"][l][k] for l in range(DEPTH) for k in range(3)], axis=0)
    lane = lax.broadcasted_iota(jnp.int32, (1, GROUP), 1)
    misc = jnp.where(lane < 8, G["bg"], 0.0) + jnp.where(lane == 8, loss, 0.0) + sink.done[0:1, 0:1]
    parts = (dln_g, dln_b, _groups_to_taps(G["convw"], 3, GROUP), misc, _groups_to_taps(G["qkw"], 4, ML_HEAD_DIM),
             G["hg"][:, 0, :ML_HEAD_DIM])
    tot = small_allgather(jnp.concatenate([section(a, SMALL_OUT_COLS) for a in parts], axis=0), reduce=True)
    loss_total = tot[24, 8]

    small_grads = {
        "ln_g": lax.dynamic_slice(tot[0:6, 0:D_MODEL], (0, chip * 256), (6, 256)).reshape(DEPTH, 3, 256),
        "ln_b": lax.dynamic_slice(tot[8:14, 0:D_MODEL], (0, chip * 256), (6, 256)).reshape(DEPTH, 3, 256),
        "conv_w": lax.dynamic_slice(tot[16:19, 0:D_TOK], (0, chip * 192), (3, 192))[None],
        "b_gates": tot[24:25, 0:8],
        "qk_conv_w": lax.dynamic_slice(tot[32:36, 0:2 * D_TOK], (0, chip * 384), (4, 384))[None],
        "head_norm_g": tot[40:44, 0:ML_HEAD_DIM][None],
    }
    grads, deltas, new_m, new_v = [], [], [], []
    for nme in names:
        if nme in updated:
            back = (lambda a: jnp.swapaxes(a, -1, -2)) if nme in ("ffn_w_gate", "ffn_w_up") else (lambda a: a)
            g, d, nm, nv = (back(a) for a in updated[nme])
        else:
            w, g = weights[nme], small_grads[nme]
            two = (math.prod(w.shape[:-1]), w.shape[-1])
            d, nm, nv = (a.reshape(w.shape) for a in adamw(w.reshape(two), g.reshape(two), ms[nme].reshape(two),
                                                           vs[nme].reshape(two), "adamw_" + nme))
        grads.append(g)
        deltas.append(d)
        new_m.append(nm)
        new_v.append(nv)
    return (loss_total, grad_x[None], *grads, *deltas, *new_m, *new_v)
```

```python
import functools
import math

import jax
import jax.numpy as jnp
from jax import lax
from jax.experimental import pallas as pl
from jax.experimental.pallas import tpu as pltpu

F32 = jnp.float32
BF16 = jnp.bfloat16
SDS = jax.ShapeDtypeStruct

D_MODEL = 1024
DEPTH = 2
N_MEM = 256
XA_HEADS = 4
XA_HEAD_DIM = 64
D_XA = 256
D_TOK = 768
ML_HEADS = 4
ML_HEAD_DIM = 192
ML_CHUNK = 64
D_FF = 2816
LN_EPS = 1e-5
ALPHA = (2.0 * DEPTH) ** 0.25
N_CHIPS = 4
N_DEV = 8
FF_SHARD = D_FF // N_CHIPS
GROUP = 256
NEG = -1e30

ADAM_LR = 0.001
ADAM_B1 = 0.9
ADAM_B2 = 0.999
ADAM_EPS = 1e-08
ADAM_WD = 0.01
ADAM_STEP = 10

VMEM_LIMIT = 56 * 1024 * 1024

NN = ((1,), (0,))
NT = ((1,), (1,))
TN = ((0,), (0,))
MESH = pl.DeviceIdType.MESH


def _dot(a, b, dims):
    return lax.dot_general(a, b, (dims, ((), ())), preferred_element_type=F32)


def _bdot(a, b, ca, cb):
    dims = (((ca,), (cb,)), ((0,), (0,)))
    ah, bh = a.astype(BF16), b.astype(BF16)
    al, bl = (a - ah.astype(F32)).astype(BF16), (b - bh.astype(F32)).astype(BF16)
    dot = functools.partial(lax.dot_general, dimension_numbers=dims, preferred_element_type=F32)
    return dot(ah, bh) + dot(al, bh) + dot(ah, bl)


def _bdot1(a, b, ca, cb):
    return lax.dot_general(a.astype(BF16), b.astype(BF16), (((ca,), (cb,)), ((0,), (0,))), preferred_element_type=F32)


def _sigmoid(x):
    return 1.0 / (1.0 + jnp.exp(-x))


def _params(sem, vmem=VMEM_LIMIT):
    return pltpu.CompilerParams(dimension_semantics=sem, vmem_limit_bytes=vmem)


def _tile(n, want):
    t = min(n, want)
    assert n % t == 0, (n, t)
    return t


def _layer_norm(z, gamma, beta):
    mu = jnp.mean(z, axis=-1, keepdims=True)
    zc = z - mu
    var = jnp.mean(zc * zc, axis=-1, keepdims=True)
    return zc * lax.rsqrt(var + LN_EPS) * gamma + beta


def _column_halves(n):
    mid = -(-n // (2 * 128)) * 128
    return ((0, mid), (mid, n))


def _resident(shape):
    return pl.BlockSpec(shape, lambda *_: (0,) * len(shape), pipeline_mode=pl.Buffered(1))


def _group_block(G, want):
    return max(d for d in range(1, max(1, min(G, want)) + 1) if G % d == 0)


def ffn_fwd(xb, x, wg, wu, wd, gamma, beta):
    S, K = xb.shape
    G, _, N = wg.shape
    ts = _tile(S, 512)

    def body(xb_ref, x_ref, wg_ref, wu_ref, wd_ref, gm_ref, bt_ref, g_ref, u_ref, h_ref, z_ref, xn_ref, xnb_ref):
        j = pl.program_id(1)
        xv = xb_ref[...]
        g = _dot(xv, wg_ref[j], NN)
        u = _dot(xv, wu_ref[j], NN)
        h = (g * _sigmoid(g) * u).astype(BF16)
        g_ref[0] = g.astype(BF16)
        u_ref[0] = u.astype(BF16)
        h_ref[0] = h
        y = _dot(h, wd_ref[j], NN)

        @pl.when(j == 0)
        def _():
            z_ref[...] = y

        @pl.when(j > 0)
        def _():
            z_ref[...] += y

        @pl.when(j == G - 1)
        def _():
            z = ALPHA * x_ref[...] + 0.5 * z_ref[...]
            xn = _layer_norm(z, gm_ref[...], bt_ref[...])
            z_ref[...] = z
            xn_ref[...] = xn
            xnb_ref[...] = xn.astype(BF16)

    row = pl.BlockSpec((ts, K), lambda s, j: (s, 0))
    vec = pl.BlockSpec((1, K), lambda s, j: (0, 0))
    wspec = _resident((G, K, N))
    ospec = pl.BlockSpec((1, ts, N), lambda s, j: (j, s, 0))
    return pl.pallas_call(
        body, name="ffn_fwd", grid=(S // ts, G),
        in_specs=[row, row, wspec, wspec, _resident((G, N, K)), vec, vec],
        out_specs=[ospec, ospec, ospec, row, row, row],
        out_shape=[SDS((G, S, N), BF16), SDS((G, S, N), BF16), SDS((G, S, N), BF16),
                   SDS((S, K), F32), SDS((S, K), F32), SDS((S, K), BF16)],
        compiler_params=_params(("parallel", "arbitrary")),
    )(xb, x, wg, wu, wd, gamma, beta)


def proj(xb, w, name):
    S, K = xb.shape
    G, _, N = w.shape
    ts = _tile(S, 1024)
    gb = _group_block(G, 6)

    def body(x_ref, w_ref, y_ref):
        xv = x_ref[...]
        for j in range(gb):
            y_ref[j] = _dot(xv, w_ref[j], NN)

    return pl.pallas_call(
        body, name=name, grid=(S // ts, G // gb),
        in_specs=[pl.BlockSpec((ts, K), lambda s, g: (s, 0)), pl.BlockSpec((gb, K, N), lambda s, g: (g, 0, 0))],
        out_specs=pl.BlockSpec((gb, ts, N), lambda s, g: (g, s, 0)),
        out_shape=SDS((G, S, N), F32),
        compiler_params=_params(("parallel", "parallel")),
    )(xb, w)


def contract_ln(a, w, xres, gamma, beta, scale, name):
    G, S, Kg = a.shape
    N = w.shape[2]
    ts = _tile(S, 512)

    def body(a_ref, w_ref, x_ref, g_ref, b_ref, z_ref, xn_ref, xb_ref):
        acc = _dot(a_ref[0], w_ref[0], NN)
        for j in range(1, G):
            acc = acc + _dot(a_ref[j], w_ref[j], NN)
        z = ALPHA * x_ref[...] + scale * acc
        xn = _layer_norm(z, g_ref[...], b_ref[...])
        z_ref[...] = z
        xn_ref[...] = xn
        xb_ref[...] = xn.astype(BF16)

    row = pl.BlockSpec((ts, N), lambda s: (s, 0))
    vec = pl.BlockSpec((1, N), lambda s: (0, 0))
    return pl.pallas_call(
        body, name=name, grid=(S // ts,),
        in_specs=[pl.BlockSpec((G, ts, Kg), lambda s: (0, s, 0)), pl.BlockSpec((G, Kg, N), lambda s: (0, 0, 0)), row, vec, vec],
        out_specs=[row, row, row],
        out_shape=[SDS((S, N), F32), SDS((S, N), F32), SDS((S, N), BF16)],
        compiler_params=_params(("parallel",)),
    )(a, w, xres, gamma, beta)


def ln_bwd(dx, z, gamma, out_scale, name):
    S, N = dx.shape
    ts = _tile(S, 512)

    def body(dx_ref, z_ref, g_ref, dz_ref, dzb_ref, dg_ref, db_ref):
        @pl.when(pl.program_id(0) == 0)
        def _():
            dg_ref[...] = jnp.zeros_like(dg_ref)
            db_ref[...] = jnp.zeros_like(db_ref)

        z = z_ref[...]
        mu = jnp.mean(z, axis=-1, keepdims=True)
        zc = z - mu
        var = jnp.mean(zc * zc, axis=-1, keepdims=True)
        rstd = lax.rsqrt(var + LN_EPS)
        xhat = zc * rstd
        dxv = dx_ref[...]
        dg_ref[...] += jnp.sum(dxv * xhat, axis=0, keepdims=True)
        db_ref[...] += jnp.sum(dxv, axis=0, keepdims=True)
        dxh = dxv * g_ref[...]
        m1 = jnp.mean(dxh, axis=-1, keepdims=True)
        m2 = jnp.mean(dxh * xhat, axis=-1, keepdims=True)
        dz = rstd * (dxh - m1 - xhat * m2)
        dz_ref[...] = dz
        dzb_ref[...] = (out_scale * dz).astype(BF16)

    row = pl.BlockSpec((ts, N), lambda s: (s, 0))
    vec = pl.BlockSpec((1, N), lambda s: (0, 0))
    return pl.pallas_call(
        body, name=name, grid=(S // ts,),
        in_specs=[row, row, vec],
        out_specs=[row, row, vec, vec],
        out_shape=[SDS((S, N), F32), SDS((S, N), BF16), SDS((1, N), F32), SDS((1, N), F32)],
        compiler_params=_params(("arbitrary",)),
    )(dx, z, gamma)


def ffn_bwd(dyb, dz, wd, wg, wu, g1, u1):
    S, K = dyb.shape
    G, N, _ = wd.shape
    ts = _tile(S, 512)

    def body(dy_ref, dz_ref, wd_ref, wg_ref, wu_ref, g_ref, u_ref, dg_ref, du_ref, dx_ref):
        j = pl.program_id(1)
        dy = dy_ref[...]
        part = None
        for a, b in _column_halves(N):
            dh = _dot(dy, wd_ref[j, a:b, :], NT)
            g = g_ref[0, :, a:b].astype(F32)
            sig = _sigmoid(g)
            dg = (dh * u_ref[0, :, a:b].astype(F32) * (sig * (1.0 + g * (1.0 - sig)))).astype(BF16)
            du = (dh * (g * sig)).astype(BF16)
            dg_ref[0, :, a:b] = dg
            du_ref[0, :, a:b] = du
            p = _dot(dg, wg_ref[j, :, a:b], NT) + _dot(du, wu_ref[j, :, a:b], NT)
            part = p if part is None else part + p

        @pl.when(j == 0)
        def _():
            dx_ref[...] = ALPHA * dz_ref[...] + part

        @pl.when(j > 0)
        def _():
            dx_ref[...] += part

    row = pl.BlockSpec((ts, K), lambda s, j: (s, 0))
    gspec = pl.BlockSpec((1, ts, N), lambda s, j: (j, s, 0))
    wspec = _resident((G, K, N))
    return pl.pallas_call(
        body, name="ffn_bwd", grid=(S // ts, G),
        in_specs=[row, row, _resident((G, N, K)), wspec, wspec, gspec, gspec],
        out_specs=[gspec, gspec, row],
        out_shape=[SDS((G, S, N), BF16), SDS((G, S, N), BF16), SDS((S, K), F32)],
        compiler_params=_params(("parallel", "arbitrary")),
    )(dyb, dz, wd, wg, wu, g1, u1)


def proj_t(dyb, w, name):
    S, N = dyb.shape
    G, Kg, _ = w.shape
    ts = _tile(S, 1024)

    def body(dy_ref, w_ref, da_ref):
        dy = dy_ref[...]
        for j in range(G):
            da_ref[j] = _dot(dy, w_ref[j], NT)

    return pl.pallas_call(
        body, name=name, grid=(S // ts,),
        in_specs=[pl.BlockSpec((ts, N), lambda s: (s, 0)), pl.BlockSpec((G, Kg, N), lambda s: (0, 0, 0))],
        out_specs=pl.BlockSpec((G, ts, Kg), lambda s: (0, s, 0)),
        out_shape=SDS((G, S, Kg), F32),
        compiler_params=_params(("parallel",)),
    )(dyb, w)


def contract_t(da, w, res, name):
    G, S, Ng = da.shape
    K = w.shape[1]
    ts = _tile(S, 512)
    gb = _group_block(G, 6)

    def body(da_ref, w_ref, r_ref, o_ref):
        g = pl.program_id(1)
        part = _dot(da_ref[0], w_ref[0], NT)
        for j in range(1, gb):
            part = part + _dot(da_ref[j], w_ref[j], NT)

        @pl.when(g == 0)
        def _():
            o_ref[...] = ALPHA * r_ref[...] + part

        @pl.when(g > 0)
        def _():
            o_ref[...] += part

    row = pl.BlockSpec((ts, K), lambda s, g: (s, 0))
    return pl.pallas_call(
        body, name=name, grid=(S // ts, G // gb),
        in_specs=[pl.BlockSpec((gb, ts, Ng), lambda s, g: (g, s, 0)), pl.BlockSpec((gb, K, Ng), lambda s, g: (g, 0, 0)), row],
        out_specs=row,
        out_shape=SDS((S, K), F32),
        compiler_params=_params(("parallel", "arbitrary")),
    )(da, w, res)


WGRAD_ACC_ELEMS = 6 * 1024 * 256


def wgrad(a, b, out_dtype, name):
    ga, gb = a.ndim == 3, b.ndim == 3
    G = a.shape[0] if ga else b.shape[0]
    S, K = a.shape[-2:]
    N = b.shape[-1]
    ts = _tile(S, 1024)
    ns = S // ts
    ng = _group_block(G, WGRAD_ACC_ELEMS // (K * N))

    def body(a_ref, b_ref, o_ref, acc):
        s = pl.program_id(1)

        @pl.when(s == 0)
        def _():
            acc[...] = jnp.zeros_like(acc)

        for j in range(ng):
            acc[j] += _dot(a_ref[j] if ga else a_ref[...], b_ref[j] if gb else b_ref[...], TN)

        @pl.when(s == ns - 1)
        def _():
            o_ref[...] = acc[...].astype(out_dtype)

    aspec = pl.BlockSpec((ng, ts, K), lambda g, s: (g, s, 0)) if ga else pl.BlockSpec((ts, K), lambda g, s: (s, 0))
    bspec = pl.BlockSpec((ng, ts, N), lambda g, s: (g, s, 0)) if gb else pl.BlockSpec((ts, N), lambda g, s: (s, 0))
    return pl.pallas_call(
        body, name=name, grid=(G // ng, ns),
        in_specs=[aspec, bspec],
        out_specs=pl.BlockSpec((ng, K, N), lambda g, s: (g, 0, 0)),
        out_shape=SDS((G, K, N), out_dtype),
        scratch_shapes=[pltpu.VMEM((ng, K, N), F32)],
        compiler_params=_params(("parallel", "arbitrary")),
    )(a, b)


def loss_grad(xn, tgt):
    S, N = xn.shape
    ts = _tile(S, 512)

    def body(x_ref, t_ref, l_ref, dx_ref):
        @pl.when(pl.program_id(0) == 0)
        def _():
            l_ref[...] = jnp.zeros_like(l_ref)

        e = x_ref[...] - t_ref[...]
        dx_ref[...] = e * (1.0 / N)
        l_ref[...] += 0.5 * jnp.sum(jnp.mean(e * e, axis=-1, keepdims=True), axis=0, keepdims=True)

    row = pl.BlockSpec((ts, N), lambda s: (s, 0))
    return pl.pallas_call(
        body, name="loss_grad", grid=(S // ts,),
        in_specs=[row, row],
        out_specs=[pl.BlockSpec((1, 1), lambda s: (0, 0)), row],
        out_shape=[SDS((1, 1), F32), SDS((S, N), F32)],
        compiler_params=_params(("arbitrary",)),
    )(xn, tgt)


def _shift_down(x, k):
    if k == 0:
        return x
    rows = lax.broadcasted_iota(jnp.int32, x.shape, 0)
    return jnp.where(rows >= k, pltpu.roll(x, k, 0), 0.0)


def _shift_up(x, k):
    if k == 0:
        return x
    n = x.shape[0]
    rows = lax.broadcasted_iota(jnp.int32, x.shape, 0)
    return jnp.where(rows < n - k, pltpu.roll(x, n - k, 0), 0.0)


LANES = 128


def conv_mixer_fwd(u, cw):
    _, S, _ = u.shape
    nh = GROUP // LANES

    def body(b_ref, c_ref, x_ref, w_ref, o_ref):
        p = c_ref[0] * x_ref[0]
        w = w_ref[0]
        conv = w[2:3] * p + w[1:2] * _shift_down(p, 1) + w[0:1] * _shift_down(p, 2)
        o_ref[0] = (b_ref[0] * conv).astype(BF16)

    def uspec(off):
        return pl.BlockSpec((1, S, LANES), lambda g, h: (g + off, 0, h))

    return pl.pallas_call(
        body, name="conv_mixer_fwd", grid=(3, nh),
        in_specs=[uspec(0), uspec(3), uspec(6), pl.BlockSpec((1, 8, LANES), lambda g, h: (g, 0, h))],
        out_specs=pl.BlockSpec((1, S, LANES), lambda g, h: (g, 0, h)),
        out_shape=SDS((3, S, GROUP), BF16),
        compiler_params=_params(("parallel", "parallel")),
    )(u, u, u, cw)


def conv_mixer_bwd(u, cw, dm):
    _, S, _ = u.shape
    nh = GROUP // LANES

    def body(b_ref, c_ref, x_ref, w_ref, d_ref, db_ref, dc_ref, dx_ref, dw_ref):
        cg, xi = c_ref[0], x_ref[0]
        p = cg * xi
        p1, p2 = _shift_down(p, 1), _shift_down(p, 2)
        w = w_ref[0]
        conv = w[2:3] * p + w[1:2] * p1 + w[0:1] * p2
        dt = d_ref[0]
        db_ref[0] = (dt * conv).astype(BF16)
        dcv = dt * b_ref[0]
        dp = w[2:3] * dcv + w[1:2] * _shift_up(dcv, 1) + w[0:1] * _shift_up(dcv, 2)
        dc_ref[0] = (dp * xi).astype(BF16)
        dx_ref[0] = (dp * cg).astype(BF16)
        dw = jnp.concatenate([jnp.sum(dcv * p2, axis=0, keepdims=True), jnp.sum(dcv * p1, axis=0, keepdims=True),
                              jnp.sum(dcv * p, axis=0, keepdims=True), jnp.zeros((5, LANES), F32)], axis=0)
        dw_ref[0] = dw

    def uspec(off):
        return pl.BlockSpec((1, S, LANES), lambda g, h: (g + off, 0, h))

    ospec = pl.BlockSpec((1, S, LANES), lambda g, h: (g, 0, h))
    wspec = pl.BlockSpec((1, 8, LANES), lambda g, h: (g, 0, h))
    return pl.pallas_call(
        body, name="conv_mixer_bwd", grid=(3, nh),
        in_specs=[uspec(0), uspec(3), uspec(6), wspec, ospec],
        out_specs=[ospec, ospec, ospec, wspec],
        out_shape=[SDS((3, S, GROUP), BF16)] * 3 + [SDS((3, 8, GROUP), F32)],
        compiler_params=_params(("parallel", "parallel")),
    )(u, u, u, cw, dm)


def qk_conv_fwd(u, qw):
    _, S, _ = u.shape
    nh = GROUP // LANES

    def body(u_ref, w_ref, o_ref):
        x = u_ref[0]
        w = w_ref[0]
        pre = w[3:4] * x + w[2:3] * _shift_down(x, 1) + w[1:2] * _shift_down(x, 2) + w[0:1] * _shift_down(x, 3)
        o_ref[0] = pre * _sigmoid(pre)

    spec = pl.BlockSpec((1, S, LANES), lambda g, h: (g, 0, h))
    return pl.pallas_call(
        body, name="qk_conv_fwd", grid=(8, nh),
        in_specs=[spec, pl.BlockSpec((1, 8, LANES), lambda g, h: (g, 0, h))],
        out_specs=spec,
        out_shape=SDS((8, S, GROUP), F32),
        compiler_params=_params(("parallel", "parallel")),
    )(u, qw)


def qk_conv_bwd(u, qw, dqk):
    _, S, _ = u.shape
    nh = GROUP // LANES

    def body(u_ref, w_ref, d_ref, du_ref, dw_ref):
        x = u_ref[0]
        w = w_ref[0]
        x1, x2, x3 = _shift_down(x, 1), _shift_down(x, 2), _shift_down(x, 3)
        pre = w[3:4] * x + w[2:3] * x1 + w[1:2] * x2 + w[0:1] * x3
        sig = _sigmoid(pre)
        dpre = d_ref[0] * (sig * (1.0 + pre * (1.0 - sig)))
        du = w[3:4] * dpre + w[2:3] * _shift_up(dpre, 1) + w[1:2] * _shift_up(dpre, 2) + w[0:1] * _shift_up(dpre, 3)
        du_ref[0] = du.astype(BF16)
        dw = jnp.concatenate([jnp.sum(dpre * x3, axis=0, keepdims=True), jnp.sum(dpre * x2, axis=0, keepdims=True),
                              jnp.sum(dpre * x1, axis=0, keepdims=True), jnp.sum(dpre * x, axis=0, keepdims=True),
                              jnp.zeros((4, LANES), F32)], axis=0)
        dw_ref[0] = dw

    spec = pl.BlockSpec((1, S, LANES), lambda g, h: (g, 0, h))
    wspec = pl.BlockSpec((1, 8, LANES), lambda g, h: (g, 0, h))
    return pl.pallas_call(
        body, name="qk_conv_bwd", grid=(8, nh),
        in_specs=[spec, wspec, spec],
        out_specs=[spec, wspec],
        out_shape=[SDS((8, S, GROUP), BF16), SDS((8, 8, GROUP), F32)],
        compiler_params=_params(("parallel", "parallel")),
    )(u, qw, dqk)


def _head_masks():
    lane = lax.broadcasted_iota(jnp.int32, (1, D_XA), 1)
    return [(lane >= h * XA_HEAD_DIM) & (lane < (h + 1) * XA_HEAD_DIM) for h in range(XA_HEADS)]


def xattn_fwd(u, qg, kv):
    _, S, _ = u.shape
    ts = _tile(S, 512)
    scale = XA_HEAD_DIM ** -0.5

    def body(q_ref, kv_ref, o_ref):
        q = q_ref[0]
        k = kv_ref[0].astype(BF16)
        v = kv_ref[1]
        o = jnp.zeros((ts, D_XA), F32)
        for m in _head_masks():
            s = _dot(jnp.where(m, q, 0.0).astype(BF16), k, NT) * scale
            s = s - jnp.max(s, axis=-1, keepdims=True)
            e = jnp.exp(s)
            p = e / jnp.sum(e, axis=-1, keepdims=True)
            o = o + _dot(p.astype(BF16), jnp.where(m, v, 0.0).astype(BF16), NN)
        o_ref[0] = o.astype(BF16)

    return pl.pallas_call(
        body, name="xattn_fwd", grid=(S // ts,),
        in_specs=[pl.BlockSpec((1, ts, GROUP), lambda s: (qg, s, 0)), pl.BlockSpec((2, N_MEM, GROUP), lambda s: (0, 0, 0))],
        out_specs=pl.BlockSpec((1, ts, GROUP), lambda s: (0, s, 0)),
        out_shape=SDS((1, S, GROUP), BF16),
        compiler_params=_params(("parallel",)),
    )(u, kv)


def xattn_bwd(u, qg, kv, dm, dg):
    _, S, _ = u.shape
    ts = _tile(S, 512)
    scale = XA_HEAD_DIM ** -0.5

    def body(q_ref, kv_ref, do_ref, dq_ref, dkv_ref):
        @pl.when(pl.program_id(0) == 0)
        def _():
            dkv_ref[...] = jnp.zeros_like(dkv_ref)

        q = q_ref[0]
        k = kv_ref[0]
        v = kv_ref[1]
        kb = k.astype(BF16)
        do = do_ref[0]
        dq = jnp.zeros((ts, D_XA), F32)
        dk = jnp.zeros((N_MEM, D_XA), F32)
        dv = jnp.zeros((N_MEM, D_XA), F32)
        for m in _head_masks():
            qm = jnp.where(m, q, 0.0).astype(BF16)
            s = _dot(qm, kb, NT) * scale
            s = s - jnp.max(s, axis=-1, keepdims=True)
            e = jnp.exp(s)
            p = e / jnp.sum(e, axis=-1, keepdims=True)
            dom = jnp.where(m, do, 0.0).astype(BF16)
            dp = _dot(dom, jnp.where(m, v, 0.0).astype(BF16), NT)
            ds = (p * (dp - jnp.sum(dp * p, axis=-1, keepdims=True)) * scale).astype(BF16)
            dq = dq + _dot(ds, jnp.where(m, k, 0.0).astype(BF16), NN)
            dk = dk + _dot(ds, qm, TN)
            dv = dv + _dot(p.astype(BF16), dom, TN)
        dq_ref[0] = dq.astype(BF16)
        dkv_ref[0] += dk
        dkv_ref[1] += dv

    return pl.pallas_call(
        body, name="xattn_bwd", grid=(S // ts,),
        in_specs=[pl.BlockSpec((1, ts, GROUP), lambda s: (qg, s, 0)), pl.BlockSpec((2, N_MEM, GROUP), lambda s: (0, 0, 0)),
                  pl.BlockSpec((1, ts, GROUP), lambda s: (dg, s, 0))],
        out_specs=[pl.BlockSpec((1, ts, GROUP), lambda s: (0, s, 0)), pl.BlockSpec((2, N_MEM, GROUP), lambda s: (0, 0, 0))],
        out_shape=[SDS((1, S, GROUP), BF16), SDS((2, N_MEM, GROUP), F32)],
        compiler_params=_params(("arbitrary",)),
    )(u, kv, dm)


ML_BLOCK_CHUNKS = 4
H4 = ML_HEADS
L = ML_CHUNK
NLANE = ML_HEAD_DIM


def _chunk_consts():
    r = lax.broadcasted_iota(jnp.int32, (1, L, L), 1)
    c = lax.broadcasted_iota(jnp.int32, (1, L, L), 2)
    return r >= c, r <= c, r == c


def _gate_cols(gb):
    lane = lax.broadcasted_iota(jnp.int32, gb.shape, 1)
    li = jnp.stack([jnp.sum(jnp.where(lane == h, gb, 0.0), axis=1, keepdims=True) for h in range(H4)])
    gf = jnp.stack([jnp.sum(jnp.where(lane == H4 + h, gb, 0.0), axis=1, keepdims=True) for h in range(H4)])
    return li, gf


def _log_sigmoid(x):
    return jnp.minimum(x, 0.0) - jnp.log(1.0 + jnp.exp(-jnp.abs(x)))


def _chunk_forward(q, k, v_aug, li_col, lf_col, c_prev, m_prev):
    tri, tri_t, eye = _chunk_consts()
    lf_row = jnp.sum(jnp.where(eye, lf_col, 0.0), axis=1, keepdims=True)
    li_row = jnp.sum(jnp.where(eye, li_col, 0.0), axis=1, keepdims=True)
    bcum_col = jnp.sum(jnp.where(tri, lf_row, 0.0), axis=2, keepdims=True)
    bcum_row = jnp.sum(jnp.where(tri_t, lf_col, 0.0), axis=1, keepdims=True)
    log_d = jnp.where(tri, bcum_col - bcum_row + li_row, NEG)
    log_inter = bcum_col + m_prev
    m_t = jnp.maximum(log_inter, jnp.max(log_d, axis=2, keepdims=True))
    w_intra = jnp.exp(log_d - m_t)
    w_inter = jnp.exp(log_inter - m_t)
    sc = _bdot(q, k, 2, 2) * w_intra
    qc = _bdot1(q, c_prev, 2, 1)
    num = _bdot(sc, v_aug, 2, 1) + w_inter * qc
    lane = lax.broadcasted_iota(jnp.int32, num.shape, 2)
    den = jnp.sum(jnp.where(lane == NLANE, num, 0.0), axis=2, keepdims=True)
    e_m = jnp.exp(-m_t)
    b_last = jnp.sum(lf_row, axis=2, keepdims=True)
    log_w = b_last - bcum_col + li_col
    m_new = jnp.maximum(b_last + m_prev, jnp.max(log_w, axis=1, keepdims=True))
    w_k = jnp.exp(log_w - m_new)
    decay = jnp.exp(b_last + m_prev - m_new)
    return dict(w_intra=w_intra, w_inter=w_inter, sc=sc, qc=qc, num=num, den=den, e_m=e_m, lane=lane,
                w_k=w_k, decay=decay, m_new=m_new)


def mlstm_fwd(qk, u, bg):
    _, S, _ = qk.shape
    nc = S // L
    cb = min(ML_BLOCK_CHUNKS, nc)
    rows = cb * L
    kscale = ML_HEAD_DIM ** -0.5

    def body(qk_ref, v_ref, g_ref, bg_ref, h_ref, cst_ref, mst_ref, c_sc, m_sc):
        @pl.when(pl.program_id(0) == 0)
        def _():
            c_sc[...] = jnp.zeros_like(c_sc)
            m_sc[...] = jnp.zeros_like(m_sc)

        for c in range(cb):
            sl = pl.ds(c * L, L)
            q = qk_ref[0:H4, sl, :]
            k = qk_ref[H4:2 * H4, sl, :] * kscale
            v = v_ref[:, sl, :]
            lane = lax.broadcasted_iota(jnp.int32, v.shape, 2)
            v_aug = jnp.where(lane == NLANE, 1.0, v)
            li_col, gf = _gate_cols(g_ref[0, sl, :] + bg_ref[...])
            lf_col = _log_sigmoid(gf)
            c_prev = c_sc[...]
            m_prev = m_sc[...]
            f = _chunk_forward(q, k, v_aug, li_col, lf_col, c_prev, m_prev)
            r = 1.0 / jnp.maximum(jnp.abs(f["den"]), f["e_m"])
            h_ref[:, sl, :] = jnp.where(lane < NLANE, f["num"] * r, 0.0)
            cst_ref[c] = c_prev
            mst_ref[c] = jnp.broadcast_to(m_prev, (H4, 1, LANES))
            c_sc[...] = f["decay"] * c_prev + _bdot(k * f["w_k"], v_aug, 1, 1)
            m_sc[...] = f["m_new"]

    def hspec(blk):
        return pl.BlockSpec((H4, rows, GROUP), lambda i: (blk, i, 0))

    return pl.pallas_call(
        body, name="mlstm_fwd", grid=(nc // cb,),
        in_specs=[pl.BlockSpec((2 * H4, rows, GROUP), lambda i: (0, i, 0)), hspec(2),
                  pl.BlockSpec((1, rows, GROUP), lambda i: (17, i, 0)), pl.BlockSpec((1, GROUP), lambda i: (0, 0))],
        out_specs=[hspec(0), pl.BlockSpec((cb, H4, GROUP, GROUP), lambda i: (i, 0, 0, 0)),
                   pl.BlockSpec((cb, H4, 1, LANES), lambda i: (i, 0, 0, 0))],
        out_shape=[SDS((H4, S, GROUP), F32), SDS((nc, H4, GROUP, GROUP), F32), SDS((nc, H4, 1, LANES), F32)],
        scratch_shapes=[pltpu.VMEM((H4, GROUP, GROUP), F32), pltpu.VMEM((H4, 1, 1), F32)],
        compiler_params=_params(("arbitrary",)),
    )(qk, u, u, bg)


def mlstm_bwd(qk, u, bg, cst, mst, dh):
    _, S, _ = qk.shape
    nc = S // L
    cb = min(ML_BLOCK_CHUNKS, nc)
    rows = cb * L
    nb = nc // cb
    kscale = ML_HEAD_DIM ** -0.5

    def body(qk_ref, v_ref, g_ref, bg_ref, cst_ref, mst_ref, dh_ref, dqk_ref, dv_ref, dg_ref, dbg_ref, dc_sc):
        @pl.when(pl.program_id(0) == 0)
        def _():
            dc_sc[...] = jnp.zeros_like(dc_sc)
            dbg_ref[...] = jnp.zeros_like(dbg_ref)

        tri, tri_t, eye = _chunk_consts()
        for c in reversed(range(cb)):
            sl = pl.ds(c * L, L)
            q = qk_ref[0:H4, sl, :]
            k = qk_ref[H4:2 * H4, sl, :] * kscale
            v = v_ref[:, sl, :]
            lane = lax.broadcasted_iota(jnp.int32, v.shape, 2)
            v_aug = jnp.where(lane == NLANE, 1.0, v)
            li_col, gf = _gate_cols(g_ref[0, sl, :] + bg_ref[...])
            lf_col = _log_sigmoid(gf)
            c_prev = cst_ref[c]
            m_prev = mst_ref[c][:, :, 0:1]
            f = _chunk_forward(q, k, v_aug, li_col, lf_col, c_prev, m_prev)
            w_intra, w_inter, sc, num, den, e_m = f["w_intra"], f["w_inter"], f["sc"], f["num"], f["den"], f["e_m"]
            absd = jnp.abs(den)
            r = 1.0 / jnp.maximum(absd, e_m)
            dhv = dh_ref[:, sl, :]
            s1 = jnp.sum(jnp.where(lane < NLANE, dhv * num, 0.0), axis=2, keepdims=True)
            dden = jnp.where(absd > e_m, -s1 * r * r * jnp.sign(den), 0.0)
            dnum = jnp.where(lane == NLANE, dden, jnp.where(lane < NLANE, dhv * r, 0.0))
            dsc = _bdot1(dnum, v_aug, 2, 2)
            dv = _bdot1(sc, dnum, 1, 1)
            gmat = dsc * sc
            dqk = dsc * w_intra
            dq = _bdot1(dqk, k, 2, 1) + w_inter * _bdot1(dnum, c_prev, 2, 2)
            dk = _bdot1(dqk, q, 1, 1)
            dc_prev = _bdot(q * w_inter, dnum, 1, 1)
            dlog_inter = jnp.sum(dnum * f["qc"], axis=2, keepdims=True) * w_inter
            dbcum_col = dlog_inter + jnp.sum(gmat, axis=2, keepdims=True)
            g_row = jnp.sum(gmat, axis=1, keepdims=True)
            dcn = dc_sc[...]
            w_k, decay = f["w_k"], f["decay"]
            kw = k * w_k
            dc_prev = dc_prev + decay * dcn
            db_last = jnp.sum(jnp.sum(dcn * c_prev, axis=2, keepdims=True), axis=1, keepdims=True) * decay
            dkw = _bdot(v_aug, dcn, 2, 2)
            dv = dv + _bdot1(kw, dcn, 2, 1)
            dk = dk + dkw * w_k
            dlogw = jnp.sum(dkw * k, axis=2, keepdims=True) * w_k
            db_last = db_last + jnp.sum(dlogw, axis=1, keepdims=True)
            dbcum_col = dbcum_col - dlogw
            rowi = lax.broadcasted_iota(jnp.int32, (1, L, 1), 1)
            dbcum_col = dbcum_col + jnp.where(rowi == L - 1, db_last, 0.0)
            dbcum_row = jnp.sum(jnp.where(eye, dbcum_col, 0.0), axis=1, keepdims=True) - g_row
            dlf_col = jnp.sum(jnp.where(tri_t, dbcum_row, 0.0), axis=2, keepdims=True)
            dli_col = dlogw + jnp.sum(jnp.where(eye, g_row, 0.0), axis=2, keepdims=True)
            dgf_col = dlf_col * _sigmoid(-gf)
            lane_g = lax.broadcasted_iota(jnp.int32, (L, GROUP), 1)
            dg = jnp.zeros((L, GROUP), F32)
            for h in range(H4):
                dg = dg + jnp.where(lane_g == h, dli_col[h], 0.0) + jnp.where(lane_g == H4 + h, dgf_col[h], 0.0)
            dqk_ref[0:H4, sl, :] = dq
            dqk_ref[H4:2 * H4, sl, :] = dk * kscale
            dv_ref[:, sl, :] = jnp.where(lane < NLANE, dv, 0.0).astype(BF16)
            dg_ref[0, sl, :] = dg.astype(BF16)
            dbg_ref[...] += jnp.sum(dg, axis=0, keepdims=True)
            dc_sc[...] = dc_prev

    def hspec(blk):
        return pl.BlockSpec((H4, rows, GROUP), lambda i: (blk, nb - 1 - i, 0))

    gspec = pl.BlockSpec((1, rows, GROUP), lambda i: (17, nb - 1 - i, 0))
    qkspec = pl.BlockSpec((2 * H4, rows, GROUP), lambda i: (0, nb - 1 - i, 0))
    return pl.pallas_call(
        body, name="mlstm_bwd", grid=(nb,),
        in_specs=[qkspec, hspec(2), gspec, pl.BlockSpec((1, GROUP), lambda i: (0, 0)),
                  pl.BlockSpec((cb, H4, GROUP, GROUP), lambda i: (nb - 1 - i, 0, 0, 0)),
                  pl.BlockSpec((cb, H4, 1, LANES), lambda i: (nb - 1 - i, 0, 0, 0)), hspec(0)],
        out_specs=[qkspec, hspec(0), pl.BlockSpec((1, rows, GROUP), lambda i: (0, nb - 1 - i, 0)),
                   pl.BlockSpec((1, GROUP), lambda i: (0, 0))],
        out_shape=[SDS((2 * H4, S, GROUP), F32), SDS((H4, S, GROUP), BF16),
                   SDS((1, S, GROUP), BF16), SDS((1, GROUP), F32)],
        scratch_shapes=[pltpu.VMEM((H4, GROUP, GROUP), F32)],
        compiler_params=_params(("arbitrary",)),
    )(qk, u, u, bg, cst, mst, dh)


def head_norm_fwd(hm, u, hg):
    _, S, _ = hm.shape
    ts = _tile(S, 512)

    def body(h_ref, o_ref, g_ref, t_ref):
        h = h_ref[0]
        lane = lax.broadcasted_iota(jnp.int32, h.shape, 1)
        valid = lane < ML_HEAD_DIM
        mu = jnp.sum(h, axis=-1, keepdims=True) * (1.0 / ML_HEAD_DIM)
        hc = jnp.where(valid, h - mu, 0.0)
        var = jnp.sum(hc * hc, axis=-1, keepdims=True) * (1.0 / ML_HEAD_DIM)
        hn = hc * lax.rsqrt(var + LN_EPS) * g_ref[0]
        t_ref[0] = (_sigmoid(o_ref[0]) * hn).astype(BF16)

    return pl.pallas_call(
        body, name="head_norm_fwd", grid=(H4, S // ts),
        in_specs=[pl.BlockSpec((1, ts, GROUP), lambda h, s: (h, s, 0)), pl.BlockSpec((1, ts, GROUP), lambda h, s: (12 + h, s, 0)),
                  pl.BlockSpec((1, 1, GROUP), lambda h, s: (h, 0, 0))],
        out_specs=pl.BlockSpec((1, ts, GROUP), lambda h, s: (h, s, 0)),
        out_shape=SDS((H4, S, GROUP), BF16),
        compiler_params=_params(("parallel", "parallel")),
    )(hm, u, hg)


def head_norm_bwd(hm, u, hg, dm):
    _, S, _ = hm.shape
    ts = _tile(S, 512)

    def body(h_ref, o_ref, g_ref, d_ref, dh_ref, do_ref, dg_ref):
        @pl.when(pl.program_id(1) == 0)
        def _():
            dg_ref[...] = jnp.zeros_like(dg_ref)

        h = h_ref[0]
        lane = lax.broadcasted_iota(jnp.int32, h.shape, 1)
        valid = lane < ML_HEAD_DIM
        inv = 1.0 / ML_HEAD_DIM
        mu = jnp.sum(h, axis=-1, keepdims=True) * inv
        hc = jnp.where(valid, h - mu, 0.0)
        var = jnp.sum(hc * hc, axis=-1, keepdims=True) * inv
        rstd = lax.rsqrt(var + LN_EPS)
        xhat = hc * rstd
        g = g_ref[0]
        sig = _sigmoid(o_ref[0])
        dt = jnp.where(valid, d_ref[0], 0.0)
        do_ref[0] = (dt * xhat * g * sig * (1.0 - sig)).astype(BF16)
        dhn = dt * sig
        dg_ref[0] += jnp.sum(dhn * xhat, axis=0, keepdims=True)
        dxh = dhn * g
        m1 = jnp.sum(dxh, axis=-1, keepdims=True) * inv
        m2 = jnp.sum(dxh * xhat, axis=-1, keepdims=True) * inv
        dh_ref[0] = jnp.where(valid, rstd * (dxh - m1 - xhat * m2), 0.0)

    spec = pl.BlockSpec((1, ts, GROUP), lambda h, s: (h, s, 0))
    gspec = pl.BlockSpec((1, 1, GROUP), lambda h, s: (h, 0, 0))
    return pl.pallas_call(
        body, name="head_norm_bwd", grid=(H4, S // ts),
        in_specs=[spec, pl.BlockSpec((1, ts, GROUP), lambda h, s: (12 + h, s, 0)), gspec, spec],
        out_specs=[spec, spec, gspec],
        out_shape=[SDS((H4, S, GROUP), F32), SDS((H4, S, GROUP), BF16), SDS((H4, 1, GROUP), F32)],
        compiler_params=_params(("parallel", "arbitrary")),
    )(hm, u, hg, dm)


def _adamw_math(w, g, m, v):
    c1 = 1.0 / (1.0 - ADAM_B1 ** ADAM_STEP)
    c2 = 1.0 / (1.0 - ADAM_B2 ** ADAM_STEP)
    nm = ADAM_B1 * m + (1.0 - ADAM_B1) * g
    nv = ADAM_B2 * v + (1.0 - ADAM_B2) * (g * g)
    return -ADAM_LR * ((nm * c1) / (jnp.sqrt(nv * c2) + ADAM_EPS) + ADAM_WD * w), nm, nv


def _row_tile(R, cap=512):
    return R if R <= cap else max(d for d in range(8, cap + 1, 8) if R % d == 0)


def adamw_into(w, m, v, g, outs, idx, after, name):
    R, C = g.shape
    tr = _row_tile(R)
    lead = (0,) * len(idx)

    def body(w_ref, m_ref, v_ref, g_ref, *rest):
        go_ref, d_ref, nm_ref, nv_ref, token = rest[-5:]
        token[...] = jnp.zeros_like(token)
        gv = g_ref[...]
        d, nm, nv = _adamw_math(w_ref[lead], gv, m_ref[lead], v_ref[lead])
        go_ref[lead] = gv
        d_ref[lead] = d
        nm_ref[lead] = nm
        nv_ref[lead] = nv

    blk = pl.BlockSpec((1,) * len(idx) + (tr, C), lambda r: idx + (r, 0))
    any_space = pl.BlockSpec(memory_space=pl.ANY)
    in_specs, args, aliases = [blk, blk, blk, pl.BlockSpec((tr, C), lambda r: (r, 0)), any_space], [w, m, v, g, g if after is None else after], {}
    if outs is not None:
        in_specs += [any_space] * 4
        args += list(outs)
        aliases = {5 + i: i for i in range(4)}
    out = pl.pallas_call(
        body, name=name, grid=(R // tr,),
        in_specs=in_specs, out_specs=[blk] * 4 + [pl.BlockSpec((8, LANES), lambda r: (0, 0))],
        out_shape=[SDS(w.shape, F32)] * 4 + [SDS((8, LANES), F32)],
        input_output_aliases=aliases, compiler_params=_params(("arbitrary",)),
    )(*args)
    return out[:4], out[4]


def adamw(w, g, m, v, name):
    R, C = w.shape
    tr = _row_tile(R)

    def body(w_ref, g_ref, m_ref, v_ref, d_ref, nm_ref, nv_ref):
        d_ref[...], nm_ref[...], nv_ref[...] = _adamw_math(w_ref[...], g_ref[...], m_ref[...], v_ref[...])

    spec = pl.BlockSpec((tr, C), lambda i: (i, 0))
    return pl.pallas_call(
        body, name=name, grid=(R // tr,),
        in_specs=[spec] * 4, out_specs=[spec] * 3,
        out_shape=[SDS((R, C), F32)] * 3,
        compiler_params=_params(("parallel",)),
    )(w, g, m, v)


HBM = pl.BlockSpec(memory_space=pl.ANY)
ROW_SPLIT = 4


def _position():
    x, y, c = lax.axis_index("x"), lax.axis_index("y"), lax.axis_index("c")
    return x, y, c, [(1 - x, y), (x, 1 - y), (1 - x, 1 - y)]


def _unique(items):
    arrays = []
    for a, _ in items:
        if not any(a is b for b in arrays):
            arrays.append(a)
    return arrays, [next(i for i, b in enumerate(arrays) if b is a) for a, _ in items]


def place_own(items, me, after, name):
    arrays, src_of = _unique(items)
    n = len(items)
    shapes = [a.shape[len(p):] for a, p in items]

    def body(me_ref, *refs):
        for t in range(n):
            refs[n + 1 + t][...] = jnp.zeros_like(refs[n + 1 + t])
            refs[n + 1 + t][me_ref[0]] = refs[t][(0,) * len(items[t][1])]

    in_specs, out_specs = [], []
    for (a, p), shp in zip(items, shapes):
        blk = shp[:-2] + (shp[-2] // ROW_SPLIT, shp[-1])
        lead = (0,) * (len(shp) - 2)
        in_specs.append(pl.BlockSpec((1,) * len(p) + blk, functools.partial(lambda r, me_ref, p, lead: p + lead + (r, 0), p=p, lead=lead)))
        out_specs.append(pl.BlockSpec((N_CHIPS,) + blk, functools.partial(lambda r, me_ref, lead: (0,) + lead + (r, 0), lead=lead)))
    in_specs.append(pl.BlockSpec(memory_space=pl.ANY))
    return pl.pallas_call(
        body, name=name,
        grid_spec=pltpu.PrefetchScalarGridSpec(num_scalar_prefetch=1, grid=(ROW_SPLIT,), in_specs=in_specs, out_specs=out_specs),
        out_shape=[SDS((N_CHIPS,) + tuple(shp), a.dtype) for shp, (a, _) in zip(shapes, items)],
        compiler_params=_params(("parallel",)),
    )(me, *[arrays[i] for i in src_of], after)


SEM = pl.BlockSpec(memory_space=pltpu.SEMAPHORE)
IN_HBM = pl.BlockSpec(memory_space=pltpu.HBM)
DATAFLOW = pltpu.SideEffectType.DATAFLOW_SIDE_EFFECTING


def split_start(bufs, plan, n_copies, after, name):
    n = len(bufs)

    def body(*refs):
        send, recv, token = refs[n + 1], refs[n + 2], refs[-1]
        x, y, c, chips = _position()
        for k, (src, dst, dev) in enumerate(plan(refs[:n], x, y, c, chips)):
            pltpu.make_async_remote_copy(src_ref=src, dst_ref=dst, send_sem=send.at[k], recv_sem=recv.at[k],
                                         device_id=dev, device_id_type=MESH).start()
        token[...] = jnp.zeros_like(token)

    out = pl.pallas_call(
        body, name=name,
        out_shape=(pltpu.SemaphoreType.DMA((n_copies,)), pltpu.SemaphoreType.DMA((n_copies,)),
                   *[pltpu.HBM(b.shape, b.dtype) for b in bufs], SDS((8, LANES), F32)),
        in_specs=[IN_HBM] * n + [pl.BlockSpec(memory_space=pl.ANY)],
        out_specs=(SEM, SEM, *[IN_HBM] * n, pl.BlockSpec(memory_space=pltpu.VMEM)),
        input_output_aliases={i: 2 + i for i in range(n)},
        compiler_params=pltpu.CompilerParams(has_side_effects=DATAFLOW),
    )(*[pltpu.with_memory_space_constraint(b, pltpu.HBM) for b in bufs], after)
    return out[0], out[1], list(out[2:2 + n]), out[-1]


def split_wait(send, recv, bufs, plan, after, name):
    n = len(bufs)

    def body(*refs):
        send_ref, recv_ref = refs[n], refs[n + 1]
        x, y, c, chips = _position()
        for k, (src, dst, dev) in enumerate(plan(refs[:n], x, y, c, chips)):
            cp = pltpu.make_async_remote_copy(src_ref=src, dst_ref=dst, send_sem=send_ref.at[k], recv_sem=recv_ref.at[k],
                                              device_id=dev, device_id_type=MESH)
            cp.wait_send()
            cp.wait_recv()

    return list(pl.pallas_call(
        body, name=name, out_shape=tuple(pltpu.HBM(b.shape, b.dtype) for b in bufs),
        in_specs=[IN_HBM] * n + [SEM, SEM, pl.BlockSpec(memory_space=pl.ANY)], out_specs=tuple([IN_HBM] * n),
        input_output_aliases={i: i for i in range(n)},
        compiler_params=pltpu.CompilerParams(has_side_effects=DATAFLOW),
    )(*bufs, send, recv, after))


def _gather_plan(shapes, landing):
    n = len(shapes)

    def plan(refs, x, y, c, chips):
        out = []
        for t in range(n):
            half = shapes[t][0] // 2
            rows = pl.ds(c * half, half)
            for cx, cy in chips:
                slot = 2 * cx + cy if landing else 2 * x + y
                out.append((refs[t].at[rows], refs[n + t].at[slot, rows], (cx, cy, c)))
        return out

    return plan


def gather_start(shards, placed, after, name):
    shapes = [s.shape for s in shards]
    send, recv, bufs, token = split_start(list(shards) + list(placed), _gather_plan(shapes, False), 3 * len(shards), after, name)
    return (send, recv, bufs, shapes), token


def gather_wait(state, after, name):
    send, recv, bufs, shapes = state
    return split_wait(send, recv, bufs, _gather_plan(shapes, True), after, name)[len(shapes):]


def gather_pass_on(placed, shapes, name):
    n = len(placed)

    def body(*refs):
        outs, send, recv = refs[n:2 * n], refs[2 * n], refs[2 * n + 1]
        x, y, c, chips = _position()
        cps = []
        for t in range(n):
            half = shapes[t][0] // 2
            for j, (cx, cy) in enumerate(chips):
                piece = outs[t].at[2 * cx + cy, pl.ds(c * half, half)]
                cp = pltpu.make_async_remote_copy(src_ref=piece, dst_ref=piece, send_sem=send.at[3 * t + j], recv_sem=recv.at[3 * t + j],
                                                  device_id=(x, y, 1 - c), device_id_type=MESH)
                cp.start()
                cps.append(cp)
        for t in range(n):
            half = shapes[t][0] // 2
            for j, (cx, cy) in enumerate(chips):
                piece = outs[t].at[2 * cx + cy, pl.ds((1 - c) * half, half)]
                pltpu.make_async_remote_copy(src_ref=piece, dst_ref=piece, send_sem=send.at[3 * t + j], recv_sem=recv.at[3 * t + j],
                                             device_id=(x, y, 1 - c), device_id_type=MESH).wait_recv()
        for cp in cps:
            cp.wait_send()

    return pl.pallas_call(
        body, name=name,
        in_specs=[HBM] * n, out_specs=[HBM] * n,
        out_shape=[SDS(p.shape, p.dtype) for p in placed],
        input_output_aliases={t: t for t in range(n)},
        scratch_shapes=[pltpu.SemaphoreType.DMA((3 * n,))] * 2,
    )(*placed)


def _flip(k, x, y, c):
    return ((1 - x) if k & 4 else x, (1 - y) if k & 2 else y, (1 - c) if k & 1 else c)


def small_allgather(v, reduce):
    R, C = v.shape

    def body(v_ref, o_ref, *scratch):
        if reduce:
            buf, send, recv = scratch
        else:
            buf, (send, recv) = o_ref, scratch
        x, y, c, _ = _position()
        me = 4 * x + 2 * y + c
        buf[me] = v_ref[...]
        sends = []
        for k in range(1, N_DEV):
            cp = pltpu.make_async_remote_copy(src_ref=v_ref, dst_ref=buf.at[me], send_sem=send.at[k - 1], recv_sem=recv.at[k - 1],
                                              device_id=_flip(k, x, y, c), device_id_type=MESH)
            cp.start()
            sends.append(cp)
        for k in range(1, N_DEV):
            px, py, pc = _flip(k, x, y, c)
            pltpu.make_async_remote_copy(src_ref=v_ref, dst_ref=buf.at[4 * px + 2 * py + pc], send_sem=send.at[k - 1],
                                         recv_sem=recv.at[k - 1], device_id=(px, py, pc), device_id_type=MESH).wait_recv()
        for cp in sends:
            cp.wait_send()
        if reduce:
            acc = buf[0]
            for i in range(1, N_DEV):
                acc = acc + buf[i]
            o_ref[...] = acc

    vm = pl.BlockSpec(memory_space=pltpu.VMEM)
    sems = [pltpu.SemaphoreType.DMA((N_DEV - 1,)), pltpu.SemaphoreType.DMA((N_DEV - 1,))]
    return pl.pallas_call(
        body, name="small_allreduce" if reduce else "small_allgather",
        in_specs=[vm], out_specs=vm,
        out_shape=SDS((R, C) if reduce else (N_DEV, R, C), F32),
        scratch_shapes=([pltpu.VMEM((N_DEV, R, C), F32)] if reduce else []) + sems,
    )(v)


def rs_exchange_sibling(gs):
    n = len(gs)

    def body(*refs):
        ins, outs, send, recv = refs[:n], refs[n:2 * n], refs[2 * n], refs[2 * n + 1]
        x, y, c, _ = _position()
        cps = []
        for t in range(n):
            cp = pltpu.make_async_remote_copy(src_ref=ins[t].at[:, 1 - c], dst_ref=outs[t], send_sem=send.at[t], recv_sem=recv.at[t],
                                              device_id=(x, y, 1 - c), device_id_type=MESH)
            cp.start()
            cps.append(cp)
        for cp in cps:
            cp.wait()

    return pl.pallas_call(
        body, name="rs_exchange_sibling", in_specs=[HBM] * n, out_specs=[HBM] * n,
        out_shape=[SDS((g.shape[0],) + g.shape[2:], g.dtype) for g in gs],
        scratch_shapes=[pltpu.SemaphoreType.DMA((n,)), pltpu.SemaphoreType.DMA((n,))],
    )(*gs)


def rs_pair_add(gs, rs, c):
    n = len(gs)

    def body(c_ref, *refs):
        for t in range(n):
            refs[2 * n + t][0] = (refs[t][0, 0].astype(F32) + refs[n + t][0].astype(F32)).astype(BF16)

    in_specs, out_specs, out_shape = [], [], []
    for g in gs:
        _, _, h, C = g.shape
        in_specs.append(pl.BlockSpec((1, 1, h // ROW_SPLIT, C), lambda j, r, c_ref: (j, c_ref[0], r, 0)))
    for g in gs:
        _, _, h, C = g.shape
        spec = pl.BlockSpec((1, h // ROW_SPLIT, C), lambda j, r, c_ref: (j, r, 0))
        in_specs.append(spec)
        out_specs.append(spec)
        out_shape.append(SDS((N_CHIPS, h, C), BF16))
    return pl.pallas_call(
        body, name="rs_pair_add",
        grid_spec=pltpu.PrefetchScalarGridSpec(num_scalar_prefetch=1, grid=(N_CHIPS, ROW_SPLIT), in_specs=in_specs, out_specs=out_specs),
        out_shape=out_shape, compiler_params=_params(("parallel", "parallel")),
    )(c, *gs, *rs)


def _rs_plan(n):
    def plan(refs, x, y, c, chips):
        return [(refs[t].at[2 * cx + cy], refs[n + t].at[j], (cx, cy, c)) for t in range(n) for j, (cx, cy) in enumerate(chips)]

    return plan


def rs_chip_add(ps, qs, me_c):
    n = len(ps)

    def body(me_ref, *refs):
        for t in range(n):
            q = refs[n + t]
            refs[2 * n + t][...] = jnp.zeros_like(refs[2 * n + t])
            refs[2 * n + t][me_ref[1]] = ((refs[t][0].astype(F32) + q[0].astype(F32)) + q[1].astype(F32)) + q[2].astype(F32)

    in_specs, out_specs, out_shape = [], [], []
    for p in ps:
        _, h, C = p.shape
        in_specs.append(pl.BlockSpec((1, h // ROW_SPLIT, C), lambda r, me_ref: (me_ref[0], r, 0)))
    for p in ps:
        _, h, C = p.shape
        in_specs.append(pl.BlockSpec((3, h // ROW_SPLIT, C), lambda r, me_ref: (0, r, 0)))
        out_specs.append(pl.BlockSpec((2, h // ROW_SPLIT, C), lambda r, me_ref: (0, r, 0)))
        out_shape.append(SDS((2, h, C), F32))
    return pl.pallas_call(
        body, name="rs_chip_add",
        grid_spec=pltpu.PrefetchScalarGridSpec(num_scalar_prefetch=1, grid=(ROW_SPLIT,), in_specs=in_specs, out_specs=out_specs),
        out_shape=out_shape, compiler_params=_params(("parallel",)),
    )(me_c, *ps, *qs)


def rs_share(rs):
    n = len(rs)

    def body(*refs):
        outs, send, recv = refs[n:2 * n], refs[2 * n], refs[2 * n + 1]
        x, y, c, _ = _position()
        cps = []
        for t in range(n):
            cp = pltpu.make_async_remote_copy(src_ref=outs[t].at[c], dst_ref=outs[t].at[c], send_sem=send.at[t], recv_sem=recv.at[t],
                                              device_id=(x, y, 1 - c), device_id_type=MESH)
            cp.start()
            cps.append(cp)
        for cp in cps:
            cp.wait()

    return pl.pallas_call(
        body, name="rs_share", in_specs=[HBM] * n, out_specs=[HBM] * n,
        out_shape=[SDS(r.shape, r.dtype) for r in rs],
        input_output_aliases={t: t for t in range(n)},
        scratch_shapes=[pltpu.SemaphoreType.DMA((n,))] * 2,
    )(*rs)


def rs_begin(gs, after, name):
    c = lax.axis_index("c")
    n = len(gs)
    g5 = [g.reshape(N_CHIPS, 2, g.shape[1] // 2, g.shape[2]) for g in gs]
    from_sibling = rs_exchange_sibling(g5)
    pair = rs_pair_add(g5, from_sibling, jnp.reshape(c, (1,)).astype(jnp.int32))
    lands = [jnp.zeros((3,) + p.shape[1:], p.dtype) for p in pair]
    send, recv, bufs, token = split_start(list(pair) + lands, _rs_plan(n), 3 * n, from_sibling[0] if after is None else after, name)
    return (send, recv, bufs, [g.shape for g in gs]), token


def rs_end(state, after, name):
    x, y, c = lax.axis_index("x"), lax.axis_index("y"), lax.axis_index("c")
    send, recv, bufs, shapes = state
    n = len(shapes)
    bufs = split_wait(send, recv, bufs, _rs_plan(n), after, name)
    half = rs_chip_add(bufs[:n], bufs[n:], jnp.stack([2 * x + y, c]).astype(jnp.int32))
    both = rs_share(half)
    return [b.reshape(s[1], s[2]) for b, s in zip(both, shapes)]


def _pad_last(a, n):
    return jnp.pad(a, [(0, 0)] * (a.ndim - 1) + [(0, n - a.shape[-1])])


def _heads_to_groups(w):
    k = w.shape[0]
    return _pad_last(w.reshape(k, ML_HEADS, ML_HEAD_DIM).transpose(1, 0, 2), GROUP)


def _groups_to_heads(g):
    return g[:, :, :ML_HEAD_DIM].transpose(1, 0, 2).reshape(g.shape[1], D_TOK)


def _cols_to_groups(w):
    k, n = w.shape
    return w.reshape(k, n // GROUP, GROUP).transpose(1, 0, 2)


def _groups_to_cols(g):
    n, k, _ = g.shape
    return g.transpose(1, 0, 2).reshape(k, n * GROUP)


def _chips_to_cols(a):
    return a.transpose(1, 0, 2).reshape(a.shape[1], -1)


def _cols_to_chips(w):
    k, n = w.shape
    return w.reshape(k, N_CHIPS, n // N_CHIPS).transpose(1, 0, 2)


def _mlstm_in_groups(w):
    parts = [_heads_to_groups(w[:, i * D_TOK:(i + 1) * D_TOK]) for i in range(4)]
    gates = _pad_last(w[:, 4 * D_TOK:4 * D_TOK + 2 * ML_HEADS], GROUP)[None]
    qmem = w[:, 4 * D_TOK + 2 * ML_HEADS:][None]
    return jnp.concatenate(parts + [qmem, gates], axis=0)


def _mlstm_in_ungroup(g):
    parts = [_groups_to_heads(g[4 * i:4 * i + 4]) for i in range(4)]
    return jnp.concatenate(parts + [g[17][:, :2 * ML_HEADS], g[16]], axis=1)


def _taps_to_groups(w, width):
    taps = w.shape[0]
    g = _pad_last(w.reshape(taps, -1, width), GROUP).transpose(1, 0, 2)
    return jnp.pad(g, ((0, 0), (0, 8 - taps), (0, 0)))


def _groups_to_taps(g, taps, width):
    return g[:, :taps, :width].transpose(1, 0, 2).reshape(taps, -1)


SMALL_IN_COLS = 384
SMALL_OUT_COLS = 1536
SECTION = 8


class _Gathered:
    def __init__(self, srcs, groups, me):
        self.groups, self.states, self.ready = groups, [], {}
        self.group_of = {k: gi for gi, g in enumerate(groups) for k in g}
        token = me
        for gi, g in enumerate(groups):
            placed = place_own([(srcs[k], ()) for k in g], me, token, f"place_own_{gi}")
            state, token = gather_start([srcs[k] for k in g], placed, token, f"gather_start_{gi}")
            self.states.append(state)
        self.started = token

    def _get(self, key, after):
        gi = self.group_of[key]
        if gi not in self.ready:
            got = gather_wait(self.states[gi], after if gi else self.started, f"gather_wait_{gi}")
            self.ready[gi] = dict(zip(self.groups[gi], gather_pass_on(got, self.states[gi][3], f"gather_pass_on_{gi}")))
        return self.ready[gi][key]

    def ffn(self, l, i, after):
        return tuple(self._get((n, l, i), after) for n in ("wg", "wu", "wd"))

    def mixer(self, l, after):
        win = _chips_to_cols(self._get(("win", l), after))
        win = _cols_to_groups(win) if l % 2 == 0 else _mlstm_in_groups(win)
        wkv = _cols_to_groups(self._get(("wkv", l), after).reshape(D_MODEL, 2 * D_XA))
        wout = self._get(("wout", l), after)
        if l % 2:
            wout = wout.reshape(D_MODEL, D_MODEL)
            tok = jnp.pad(wout[:D_TOK].reshape(ML_HEADS, ML_HEAD_DIM, D_MODEL), ((0, 0), (0, GROUP - ML_HEAD_DIM), (0, 0)))
            wout = jnp.concatenate([tok, wout[D_TOK:][None]], axis=0)
        return win, wkv, wout


class _GradSink:
    def __init__(self, apply):
        self.queue, self.apply, self.count, self.done = [], apply, 0, None

    @staticmethod
    def _by_chip(key, g):
        if key[0] == "wkv":
            return _groups_to_cols(g).reshape(N_CHIPS, D_MODEL // N_CHIPS, 2 * D_XA)
        if key[0] == "win":
            return _cols_to_chips(_groups_to_cols(g) if key[1] % 2 == 0 else _mlstm_in_ungroup(g))
        if key[0] == "wout" and key[1] % 2:
            full = jnp.concatenate([g[:ML_HEADS, :ML_HEAD_DIM].reshape(D_TOK, D_MODEL), g[ML_HEADS]], axis=0)
            return full.reshape(N_CHIPS, D_MODEL // N_CHIPS, D_MODEL)
        return g

    def push(self, grads):
        keys = list(grads)
        state, token = rs_begin([self._by_chip(k, grads[k]) for k in keys], self.done, f"rs_start_{self.count}")
        if self.queue:
            self._finish(token)
        self.queue.append((keys, state, self.count))
        self.count += 1
        return token

    def flush(self):
        self._finish(self.done)

    def _finish(self, after):
        keys, state, i = self.queue.pop(0)
        for key, g in zip(keys, rs_end(state, after, f"rs_wait_{i}")):
            self.done = self.apply(key, g, self.done)


def _local_step(x, mem, tgt, P, weights, sink):
    memb = mem.astype(BF16)
    saved = []
    X, Xb = x, x.astype(BF16)
    after = Xb
    for l in range(DEPTH):
        s = {}
        s["x0b"] = Xb
        s["wa"] = weights.ffn(l, 0, after)
        s["g1a"], s["u1a"], s["ha"], s["z1"], X1, X1b = ffn_fwd(Xb, X, *s["wa"], P["ln_g"][l][0], P["ln_b"][l][0])
        s["x1b"] = X1b
        s["wm"] = win, wkv, wout = weights.mixer(l, X1b)
        u = proj(X1b, win, "mixer_in")
        kv = proj(memb, wkv, "mem_kv")
        s["u"], s["kv"] = u, kv
        if l % 2 == 0:
            tok = conv_mixer_fwd(u, P["convw"])
            qg = 9
        else:
            s["qk"] = qk_conv_fwd(u, P["qkw"])
            s["hm"], s["cst"], s["mst"] = mlstm_fwd(s["qk"], u, P["bg"])
            tok = head_norm_fwd(s["hm"], u, P["hg"])
            qg = 16
        xa = xattn_fwd(u, qg, kv)
        s["m"] = jnp.concatenate([tok, xa], axis=0)
        s["z2"], X2, X2b = contract_ln(s["m"], wout, X1, P["ln_g"][l][1], P["ln_b"][l][1], 1.0, "mixer_out_ln")
        s["x2b"] = X2b
        s["wb"] = weights.ffn(l, 1, X2b)
        s["g1b"], s["u1b"], s["hb"], s["z3"], X, Xb = ffn_fwd(X2b, X2, *s["wb"], P["ln_g"][l][2], P["ln_b"][l][2])
        after = Xb
        saved.append(s)

    loss, dX = loss_grad(X, tgt)

    G = {"ln_g": [[None] * 3 for _ in range(DEPTH)], "ln_b": [[None] * 3 for _ in range(DEPTH)]}
    pin = [jnp.zeros((1, 1), F32)]

    def ffn_backward(l, i, dX, z, xinb, g1, u1, h, w):
        k = 2 * i
        dz, dyb, G["ln_g"][l][k], G["ln_b"][l][k] = ln_bwd(dX, z, P["ln_g"][l][k] + pin[0], 0.5, "ffn_ln_bwd")
        dgb, dub, dx = ffn_bwd(dyb, dz, w[2], w[0], w[1], g1, u1)
        grads = {("wd", l, i): wgrad(h, dyb, BF16, "wgrad_down"), ("wg", l, i): wgrad(dgb, xinb, BF16, "wgrad_gate"),
                 ("wu", l, i): wgrad(dub, xinb, BF16, "wgrad_up")}
        return dx, grads

    for l in reversed(range(DEPTH)):
        s = saved[l]
        win, wkv, wout = s["wm"]
        dX, grads = ffn_backward(l, 1, dX, s["z3"], s["x2b"], s["g1b"], s["u1b"], s["hb"], s["wb"])
        dz2, dz2b, G["ln_g"][l][1], G["ln_b"][l][1] = ln_bwd(dX, s["z2"], P["ln_g"][l][1], 1.0, "mixer_ln_bwd")
        dm = proj_t(dz2b, wout, "mixer_out_bwd")
        grads[("wout", l)] = wgrad(s["m"], dz2b, BF16, "wgrad_out")
        u, kv = s["u"], s["kv"]
        if l % 2 == 0:
            db, dc, dxi, G["convw"] = conv_mixer_bwd(u, P["convw"], dm)
            dq, dkv = xattn_bwd(u, 9, kv, dm, 3)
            du = jnp.concatenate([db, dc, dxi, dq], axis=0)
        else:
            dh, do, G["hg"] = head_norm_bwd(s["hm"], u, P["hg"], dm)
            dqk, dv, dgate, G["bg"] = mlstm_bwd(s["qk"], u, P["bg"], s["cst"], s["mst"], dh)
            duqk, G["qkw"] = qk_conv_bwd(u, P["qkw"], dqk)
            dq, dkv = xattn_bwd(u, 16, kv, dm, 4)
            du = jnp.concatenate([duqk, dv, do, dq, dgate], axis=0)
        grads[("win", l)] = wgrad(s["x1b"], du, BF16, "wgrad_in")
        grads[("wkv", l)] = wgrad(memb, dkv.astype(BF16), BF16, "wgrad_kv")
        dX = contract_t(du, win, dz2, "mixer_in_bwd")
        pin[0] = sink.push(grads)[0:1, 0:1]
        dX, grads = ffn_backward(l, 0, dX, s["z1"], s["x0b"], s["g1a"], s["u1a"], s["ha"], s["wa"])
        pin[0] = sink.push(grads)[0:1, 0:1]
    sink.flush()
    return loss, dX, G


def kernel(x, mem, ln_g, ln_b, ffn_w_gate, ffn_w_up, ffn_w_down, w_kv_mem, w_out, w_in_conv, conv_w, w_in_mlstm, b_gates, qk_conv_w, head_norm_g, loss_target, m_ln_g, m_ln_b, m_ffn_w_gate, m_ffn_w_up, m_ffn_w_down, m_w_kv_mem, m_w_out, m_w_in_conv, m_conv_w, m_w_in_mlstm, m_b_gates, m_qk_conv_w, m_head_norm_g, v_ln_g, v_ln_b, v_ffn_w_gate, v_ffn_w_up, v_ffn_w_down, v_w_kv_mem, v_w_out, v_w_in_conv, v_conv_w, v_w_in_mlstm, v_b_gates, v_qk_conv_w, v_head_norm_g):
    cx, cy = lax.axis_index("x"), lax.axis_index("y")
    chip = 2 * cx + cy

    srcs = {}
    for l in range(DEPTH):
        for i in range(2):
            srcs[("wg", l, i)] = ffn_w_gate[l, i].astype(BF16)
            srcs[("wu", l, i)] = ffn_w_up[l, i].astype(BF16)
            srcs[("wd", l, i)] = ffn_w_down[l, i].astype(BF16)
        srcs[("wkv", l)] = w_kv_mem[l].astype(BF16)
        srcs[("wout", l)] = w_out[l].astype(BF16)
    srcs[("win", 0)] = w_in_conv[0].astype(BF16)
    srcs[("win", 1)] = w_in_mlstm[0].astype(BF16)
    ffn_keys = lambda l, i: [("wg", l, i), ("wu", l, i), ("wd", l, i)]
    mixer_keys = lambda l: [("win", l), ("wkv", l), ("wout", l)]
    groups = [ffn_keys(0, 0), mixer_keys(0) + mixer_keys(1), ffn_keys(0, 1), ffn_keys(1, 0), ffn_keys(1, 1)]
    gathered = _Gathered(srcs, groups, jnp.reshape(chip, (1,)).astype(jnp.int32))

    def section(a, width):
        a = a.reshape(-1, a.shape[-1])
        return jnp.pad(a, ((0, SECTION - a.shape[0]), (0, width - a.shape[1])))

    small = jnp.concatenate([section(a, SMALL_IN_COLS) for a in (ln_g, ln_b, conv_w, qk_conv_w)], axis=0)
    smalls = small_allgather(small, reduce=False)[0::2]
    ln_g_full = _chips_to_cols(smalls[:, 0:6, 0:256]).reshape(DEPTH, 3, 1, D_MODEL)
    ln_b_full = _chips_to_cols(smalls[:, 8:14, 0:256]).reshape(DEPTH, 3, 1, D_MODEL)
    conv_w_full = _chips_to_cols(smalls[:, 16:19, 0:192])
    qk_w_full = _chips_to_cols(smalls[:, 24:28, 0:384])

    P = {"ln_g": ln_g_full, "ln_b": ln_b_full, "convw": _taps_to_groups(conv_w_full, GROUP),
         "qkw": _taps_to_groups(qk_w_full, ML_HEAD_DIM), "bg": _pad_last(b_gates, GROUP),
         "hg": _pad_last(head_norm_g[0], GROUP)[:, None, :]}

    weights = {"ln_g": ln_g, "ln_b": ln_b, "ffn_w_gate": ffn_w_gate, "ffn_w_up": ffn_w_up, "ffn_w_down": ffn_w_down,
               "w_kv_mem": w_kv_mem, "w_out": w_out, "w_in_conv": w_in_conv, "conv_w": conv_w, "w_in_mlstm": w_in_mlstm,
               "b_gates": b_gates, "qk_conv_w": qk_conv_w, "head_norm_g": head_norm_g}
    ms = {"ln_g": m_ln_g, "ln_b": m_ln_b, "ffn_w_gate": m_ffn_w_gate, "ffn_w_up": m_ffn_w_up, "ffn_w_down": m_ffn_w_down,
          "w_kv_mem": m_w_kv_mem, "w_out": m_w_out, "w_in_conv": m_w_in_conv, "conv_w": m_conv_w, "w_in_mlstm": m_w_in_mlstm,
          "b_gates": m_b_gates, "qk_conv_w": m_qk_conv_w, "head_norm_g": m_head_norm_g}
    vs = {"ln_g": v_ln_g, "ln_b": v_ln_b, "ffn_w_gate": v_ffn_w_gate, "ffn_w_up": v_ffn_w_up, "ffn_w_down": v_ffn_w_down,
          "w_kv_mem": v_w_kv_mem, "w_out": v_w_out, "w_in_conv": v_w_in_conv, "conv_w": v_conv_w, "w_in_mlstm": v_w_in_mlstm,
          "b_gates": v_b_gates, "qk_conv_w": v_qk_conv_w, "head_norm_g": v_head_norm_g}
    names = list(weights)
    owner = {"wg": ("ffn_w_gate", True), "wu": ("ffn_w_up", True), "wd": ("ffn_w_down", False), "wkv": ("w_kv_mem", False),
             "wout": ("w_out", False), "win": None}
    updated = {}

    def apply(key, g, after):
        name, transposed = owner[key[0]] or (("w_in_conv", "w_in_mlstm")[key[1]], False)
        idx = (0,) if key[0] == "win" else tuple(key[1:])
        view = (lambda a: jnp.swapaxes(a, -1, -2)) if transposed else (lambda a: a)
        updated[name], token = adamw_into(view(weights[name]), view(ms[name]), view(vs[name]), g, updated.get(name), idx, after,
                                          "adamw_" + name + "_" + "_".join(map(str, idx)))
        return token

    sink = _GradSink(apply)
    loss, grad_x, G = _local_step(x[0], mem[0], loss_target[0], P, gathered, sink)

    dln_g = jnp.concatenate([G["ln_g"][l][k] for l in range(DEPTH) for k in range(3)], axis=0)
    dln_b = jnp.concatenate([G["ln_b"][l][k] for l in range(DEPTH) for k in range(3)], axis=0)
    lane = lax.broadcasted_iota(jnp.int32, (1, GROUP), 1)
    misc = jnp.where(lane < 8, G["bg"], 0.0) + jnp.where(lane == 8, loss, 0.0) + sink.done[0:1, 0:1]
    parts = (dln_g, dln_b, _groups_to_taps(G["convw"], 3, GROUP), misc, _groups_to_taps(G["qkw"], 4, ML_HEAD_DIM),
             G["hg"][:, 0, :ML_HEAD_DIM])
    tot = small_allgather(jnp.concatenate([section(a, SMALL_OUT_COLS) for a in parts], axis=0), reduce=True)
    loss_total = tot[24, 8]

    small_grads = {
        "ln_g": lax.dynamic_slice(tot[0:6, 0:D_MODEL], (0, chip * 256), (6, 256)).reshape(DEPTH, 3, 256),
        "ln_b": lax.dynamic_slice(tot[8:14, 0:D_MODEL], (0, chip * 256), (6, 256)).reshape(DEPTH, 3, 256),
        "conv_w": lax.dynamic_slice(tot[16:19, 0:D_TOK], (0, chip * 192), (3, 192))[None],
        "b_gates": tot[24:25, 0:8],
        "qk_conv_w": lax.dynamic_slice(tot[32:36, 0:2 * D_TOK], (0, chip * 384), (4, 384))[None],
        "head_norm_g": tot[40:44, 0:ML_HEAD_DIM][None],
    }
    grads, deltas, new_m, new_v = [], [], [], []
    for nme in names:
        if nme in updated:
            back = (lambda a: jnp.swapaxes(a, -1, -2)) if nme in ("ffn_w_gate", "ffn_w_up") else (lambda a: a)
            g, d, nm, nv = (back(a) for a in updated[nme])
        else:
            w, g = weights[nme], small_grads[nme]
            two = (math.prod(w.shape[:-1]), w.shape[-1])
            d, nm, nv = (a.reshape(w.shape) for a in adamw(w.reshape(two), g.reshape(two), ms[nme].reshape(two),
                                                           vs[nme].reshape(two), "adamw_" + nme))
        grads.append(g)
        deltas.append(d)
        new_m.append(nm)
        new_v.append(nv)
    return (loss_total, grad_x[None], *grads, *deltas, *new_m, *new_v)
```

```python
import functools
import math

import jax
import jax.numpy as jnp
from jax import lax
from jax.experimental import pallas as pl
from jax.experimental.pallas import tpu as pltpu

F32 = jnp.float32
BF16 = jnp.bfloat16
SDS = jax.ShapeDtypeStruct

D_MODEL = 1024
DEPTH = 2
N_MEM = 256
XA_HEADS = 4
XA_HEAD_DIM = 64
D_XA = 256
D_TOK = 768
ML_HEADS = 4
ML_HEAD_DIM = 192
ML_CHUNK = 64
D_FF = 2816
LN_EPS = 1e-5
ALPHA = (2.0 * DEPTH) ** 0.25
N_CHIPS = 4
N_DEV = 8
FF_SHARD = D_FF // N_CHIPS
GROUP = 256
NEG = -1e30

ADAM_LR = 0.001
ADAM_B1 = 0.9
ADAM_B2 = 0.999
ADAM_EPS = 1e-08
ADAM_WD = 0.01
ADAM_STEP = 10

VMEM_LIMIT = 56 * 1024 * 1024

NN = ((1,), (0,))
NT = ((1,), (1,))
TN = ((0,), (0,))
MESH = pl.DeviceIdType.MESH


def _dot(a, b, dims):
    return lax.dot_general(a, b, (dims, ((), ())), preferred_element_type=F32)


def _bdot(a, b, ca, cb):
    dims = (((ca,), (cb,)), ((0,), (0,)))
    ah, bh = a.astype(BF16), b.astype(BF16)
    al, bl = (a - ah.astype(F32)).astype(BF16), (b - bh.astype(F32)).astype(BF16)
    dot = functools.partial(lax.dot_general, dimension_numbers=dims, preferred_element_type=F32)
    return dot(ah, bh) + dot(al, bh) + dot(ah, bl)


def _bdot1(a, b, ca, cb):
    return lax.dot_general(a.astype(BF16), b.astype(BF16), (((ca,), (cb,)), ((0,), (0,))), preferred_element_type=F32)


def _sigmoid(x):
    return 1.0 / (1.0 + jnp.exp(-x))


def _params(sem, vmem=VMEM_LIMIT):
    return pltpu.CompilerParams(dimension_semantics=sem, vmem_limit_bytes=vmem)


def _tile(n, want):
    t = min(n, want)
    assert n % t == 0, (n, t)
    return t


def _layer_norm(z, gamma, beta):
    mu = jnp.mean(z, axis=-1, keepdims=True)
    zc = z - mu
    var = jnp.mean(zc * zc, axis=-1, keepdims=True)
    return zc * lax.rsqrt(var + LN_EPS) * gamma + beta


def _column_halves(n):
    mid = -(-n // (2 * 128)) * 128
    return ((0, mid), (mid, n))


def _resident(shape):
    return pl.BlockSpec(shape, lambda *_: (0,) * len(shape), pipeline_mode=pl.Buffered(1))


def _group_block(G, want):
    return max(d for d in range(1, max(1, min(G, want)) + 1) if G % d == 0)


def ffn_fwd(xb, x, wg, wu, wd, gamma, beta):
    S, K = xb.shape
    G, N, _ = wg.shape
    ts = _tile(S, 512)

    def body(xb_ref, x_ref, wg_ref, wu_ref, wd_ref, gm_ref, bt_ref, g_ref, u_ref, h_ref, z_ref, xn_ref, xnb_ref):
        j = pl.program_id(1)
        xv = xb_ref[...]
        g = _dot(xv, wg_ref[j], NT)
        u = _dot(xv, wu_ref[j], NT)
        h = (g * _sigmoid(g) * u).astype(BF16)
        g_ref[0] = g.astype(BF16)
        u_ref[0] = u.astype(BF16)
        h_ref[0] = h
        y = _dot(h, wd_ref[j], NN)

        @pl.when(j == 0)
        def _():
            z_ref[...] = y

        @pl.when(j > 0)
        def _():
            z_ref[...] += y

        @pl.when(j == G - 1)
        def _():
            z = ALPHA * x_ref[...] + 0.5 * z_ref[...]
            xn = _layer_norm(z, gm_ref[...], bt_ref[...])
            z_ref[...] = z
            xn_ref[...] = xn
            xnb_ref[...] = xn.astype(BF16)

    row = pl.BlockSpec((ts, K), lambda s, j: (s, 0))
    vec = pl.BlockSpec((1, K), lambda s, j: (0, 0))
    wspec = _resident((G, N, K))
    ospec = pl.BlockSpec((1, ts, N), lambda s, j: (j, s, 0))
    return pl.pallas_call(
        body, name="ffn_fwd", grid=(S // ts, G),
        in_specs=[row, row, wspec, wspec, wspec, vec, vec],
        out_specs=[ospec, ospec, ospec, row, row, row],
        out_shape=[SDS((G, S, N), BF16), SDS((G, S, N), BF16), SDS((G, S, N), BF16),
                   SDS((S, K), F32), SDS((S, K), F32), SDS((S, K), BF16)],
        compiler_params=_params(("parallel", "arbitrary")),
    )(xb, x, wg, wu, wd, gamma, beta)


def proj(xb, w, name):
    S, K = xb.shape
    G, _, N = w.shape
    ts = _tile(S, 1024)
    gb = _group_block(G, 6)

    def body(x_ref, w_ref, y_ref):
        xv = x_ref[...]
        for j in range(gb):
            y_ref[j] = _dot(xv, w_ref[j], NN)

    return pl.pallas_call(
        body, name=name, grid=(S // ts, G // gb),
        in_specs=[pl.BlockSpec((ts, K), lambda s, g: (s, 0)), pl.BlockSpec((gb, K, N), lambda s, g: (g, 0, 0))],
        out_specs=pl.BlockSpec((gb, ts, N), lambda s, g: (g, s, 0)),
        out_shape=SDS((G, S, N), F32),
        compiler_params=_params(("parallel", "parallel")),
    )(xb, w)


def contract_ln(a, w, xres, gamma, beta, scale, name):
    G, S, Kg = a.shape
    N = w.shape[2]
    ts = _tile(S, 512)

    def body(a_ref, w_ref, x_ref, g_ref, b_ref, z_ref, xn_ref, xb_ref):
        acc = _dot(a_ref[0], w_ref[0], NN)
        for j in range(1, G):
            acc = acc + _dot(a_ref[j], w_ref[j], NN)
        z = ALPHA * x_ref[...] + scale * acc
        xn = _layer_norm(z, g_ref[...], b_ref[...])
        z_ref[...] = z
        xn_ref[...] = xn
        xb_ref[...] = xn.astype(BF16)

    row = pl.BlockSpec((ts, N), lambda s: (s, 0))
    vec = pl.BlockSpec((1, N), lambda s: (0, 0))
    return pl.pallas_call(
        body, name=name, grid=(S // ts,),
        in_specs=[pl.BlockSpec((G, ts, Kg), lambda s: (0, s, 0)), pl.BlockSpec((G, Kg, N), lambda s: (0, 0, 0)), row, vec, vec],
        out_specs=[row, row, row],
        out_shape=[SDS((S, N), F32), SDS((S, N), F32), SDS((S, N), BF16)],
        compiler_params=_params(("parallel",)),
    )(a, w, xres, gamma, beta)


def ln_bwd(dx, z, gamma, out_scale, name):
    S, N = dx.shape
    ts = _tile(S, 512)

    def body(dx_ref, z_ref, g_ref, dz_ref, dzb_ref, dg_ref, db_ref):
        @pl.when(pl.program_id(0) == 0)
        def _():
            dg_ref[...] = jnp.zeros_like(dg_ref)
            db_ref[...] = jnp.zeros_like(db_ref)

        z = z_ref[...]
        mu = jnp.mean(z, axis=-1, keepdims=True)
        zc = z - mu
        var = jnp.mean(zc * zc, axis=-1, keepdims=True)
        rstd = lax.rsqrt(var + LN_EPS)
        xhat = zc * rstd
        dxv = dx_ref[...]
        dg_ref[...] += jnp.sum(dxv * xhat, axis=0, keepdims=True)
        db_ref[...] += jnp.sum(dxv, axis=0, keepdims=True)
        dxh = dxv * g_ref[...]
        m1 = jnp.mean(dxh, axis=-1, keepdims=True)
        m2 = jnp.mean(dxh * xhat, axis=-1, keepdims=True)
        dz = rstd * (dxh - m1 - xhat * m2)
        dz_ref[...] = dz
        dzb_ref[...] = (out_scale * dz).astype(BF16)

    row = pl.BlockSpec((ts, N), lambda s: (s, 0))
    vec = pl.BlockSpec((1, N), lambda s: (0, 0))
    return pl.pallas_call(
        body, name=name, grid=(S // ts,),
        in_specs=[row, row, vec],
        out_specs=[row, row, vec, vec],
        out_shape=[SDS((S, N), F32), SDS((S, N), BF16), SDS((1, N), F32), SDS((1, N), F32)],
        compiler_params=_params(("arbitrary",)),
    )(dx, z, gamma)


def ffn_bwd(dyb, dz, wd, wg, wu, g1, u1):
    S, K = dyb.shape
    G, N, _ = wd.shape
    ts = _tile(S, 512)

    def body(dy_ref, dz_ref, wd_ref, wg_ref, wu_ref, g_ref, u_ref, dg_ref, du_ref, dx_ref):
        j = pl.program_id(1)
        dy = dy_ref[...]
        part = None
        for a, b in _column_halves(N):
            dh = _dot(dy, wd_ref[j, a:b, :], NT)
            g = g_ref[0, :, a:b].astype(F32)
            sig = _sigmoid(g)
            dg = (dh * u_ref[0, :, a:b].astype(F32) * (sig * (1.0 + g * (1.0 - sig)))).astype(BF16)
            du = (dh * (g * sig)).astype(BF16)
            dg_ref[0, :, a:b] = dg
            du_ref[0, :, a:b] = du
            p = _dot(dg, wg_ref[j, a:b, :], NN) + _dot(du, wu_ref[j, a:b, :], NN)
            part = p if part is None else part + p

        @pl.when(j == 0)
        def _():
            dx_ref[...] = ALPHA * dz_ref[...] + part

        @pl.when(j > 0)
        def _():
            dx_ref[...] += part

    row = pl.BlockSpec((ts, K), lambda s, j: (s, 0))
    gspec = pl.BlockSpec((1, ts, N), lambda s, j: (j, s, 0))
    wspec = _resident((G, N, K))
    return pl.pallas_call(
        body, name="ffn_bwd", grid=(S // ts, G),
        in_specs=[row, row, wspec, wspec, wspec, gspec, gspec],
        out_specs=[gspec, gspec, row],
        out_shape=[SDS((G, S, N), BF16), SDS((G, S, N), BF16), SDS((S, K), F32)],
        compiler_params=_params(("parallel", "arbitrary")),
    )(dyb, dz, wd, wg, wu, g1, u1)


def proj_t(dyb, w, name):
    S, N = dyb.shape
    G, Kg, _ = w.shape
    ts = _tile(S, 1024)

    def body(dy_ref, w_ref, da_ref):
        dy = dy_ref[...]
        for j in range(G):
            da_ref[j] = _dot(dy, w_ref[j], NT)

    return pl.pallas_call(
        body, name=name, grid=(S // ts,),
        in_specs=[pl.BlockSpec((ts, N), lambda s: (s, 0)), pl.BlockSpec((G, Kg, N), lambda s: (0, 0, 0))],
        out_specs=pl.BlockSpec((G, ts, Kg), lambda s: (0, s, 0)),
        out_shape=SDS((G, S, Kg), F32),
        compiler_params=_params(("parallel",)),
    )(dyb, w)


def contract_t(da, w, res, name):
    G, S, Ng = da.shape
    K = w.shape[1]
    ts = _tile(S, 512)
    gb = _group_block(G, 6)

    def body(da_ref, w_ref, r_ref, o_ref):
        g = pl.program_id(1)
        part = _dot(da_ref[0], w_ref[0], NT)
        for j in range(1, gb):
            part = part + _dot(da_ref[j], w_ref[j], NT)

        @pl.when(g == 0)
        def _():
            o_ref[...] = ALPHA * r_ref[...] + part

        @pl.when(g > 0)
        def _():
            o_ref[...] += part

    row = pl.BlockSpec((ts, K), lambda s, g: (s, 0))
    return pl.pallas_call(
        body, name=name, grid=(S // ts, G // gb),
        in_specs=[pl.BlockSpec((gb, ts, Ng), lambda s, g: (g, s, 0)), pl.BlockSpec((gb, K, Ng), lambda s, g: (g, 0, 0)), row],
        out_specs=row,
        out_shape=SDS((S, K), F32),
        compiler_params=_params(("parallel", "arbitrary")),
    )(da, w, res)


WGRAD_ACC_ELEMS = 6 * 1024 * 256


def wgrad(a, b, out_dtype, name):
    ga, gb = a.ndim == 3, b.ndim == 3
    G = a.shape[0] if ga else b.shape[0]
    S, K = a.shape[-2:]
    N = b.shape[-1]
    ts = _tile(S, 1024)
    ns = S // ts
    ng = _group_block(G, WGRAD_ACC_ELEMS // (K * N))

    def body(a_ref, b_ref, o_ref, acc):
        s = pl.program_id(1)

        @pl.when(s == 0)
        def _():
            acc[...] = jnp.zeros_like(acc)

        for j in range(ng):
            acc[j] += _dot(a_ref[j] if ga else a_ref[...], b_ref[j] if gb else b_ref[...], TN)

        @pl.when(s == ns - 1)
        def _():
            o_ref[...] = acc[...].astype(out_dtype)

    aspec = pl.BlockSpec((ng, ts, K), lambda g, s: (g, s, 0)) if ga else pl.BlockSpec((ts, K), lambda g, s: (s, 0))
    bspec = pl.BlockSpec((ng, ts, N), lambda g, s: (g, s, 0)) if gb else pl.BlockSpec((ts, N), lambda g, s: (s, 0))
    return pl.pallas_call(
        body, name=name, grid=(G // ng, ns),
        in_specs=[aspec, bspec],
        out_specs=pl.BlockSpec((ng, K, N), lambda g, s: (g, 0, 0)),
        out_shape=SDS((G, K, N), out_dtype),
        scratch_shapes=[pltpu.VMEM((ng, K, N), F32)],
        compiler_params=_params(("parallel", "arbitrary")),
    )(a, b)


def loss_grad(xn, tgt):
    S, N = xn.shape
    ts = _tile(S, 512)

    def body(x_ref, t_ref, l_ref, dx_ref):
        @pl.when(pl.program_id(0) == 0)
        def _():
            l_ref[...] = jnp.zeros_like(l_ref)

        e = x_ref[...] - t_ref[...]
        dx_ref[...] = e * (1.0 / N)
        l_ref[...] += 0.5 * jnp.sum(jnp.mean(e * e, axis=-1, keepdims=True), axis=0, keepdims=True)

    row = pl.BlockSpec((ts, N), lambda s: (s, 0))
    return pl.pallas_call(
        body, name="loss_grad", grid=(S // ts,),
        in_specs=[row, row],
        out_specs=[pl.BlockSpec((1, 1), lambda s: (0, 0)), row],
        out_shape=[SDS((1, 1), F32), SDS((S, N), F32)],
        compiler_params=_params(("arbitrary",)),
    )(xn, tgt)


def _shift_down(x, k):
    if k == 0:
        return x
    rows = lax.broadcasted_iota(jnp.int32, x.shape, 0)
    return jnp.where(rows >= k, pltpu.roll(x, k, 0), 0.0)


def _shift_up(x, k):
    if k == 0:
        return x
    n = x.shape[0]
    rows = lax.broadcasted_iota(jnp.int32, x.shape, 0)
    return jnp.where(rows < n - k, pltpu.roll(x, n - k, 0), 0.0)


LANES = 128


def conv_mixer_fwd(u, cw):
    _, S, _ = u.shape
    nh = GROUP // LANES

    def body(b_ref, c_ref, x_ref, w_ref, o_ref):
        p = c_ref[0] * x_ref[0]
        w = w_ref[0]
        conv = w[2:3] * p + w[1:2] * _shift_down(p, 1) + w[0:1] * _shift_down(p, 2)
        o_ref[0] = (b_ref[0] * conv).astype(BF16)

    def uspec(off):
        return pl.BlockSpec((1, S, LANES), lambda g, h: (g + off, 0, h))

    return pl.pallas_call(
        body, name="conv_mixer_fwd", grid=(3, nh),
        in_specs=[uspec(0), uspec(3), uspec(6), pl.BlockSpec((1, 8, LANES), lambda g, h: (g, 0, h))],
        out_specs=pl.BlockSpec((1, S, LANES), lambda g, h: (g, 0, h)),
        out_shape=SDS((3, S, GROUP), BF16),
        compiler_params=_params(("parallel", "parallel")),
    )(u, u, u, cw)


def conv_mixer_bwd(u, cw, dm):
    _, S, _ = u.shape
    nh = GROUP // LANES

    def body(b_ref, c_ref, x_ref, w_ref, d_ref, db_ref, dc_ref, dx_ref, dw_ref):
        cg, xi = c_ref[0], x_ref[0]
        p = cg * xi
        p1, p2 = _shift_down(p, 1), _shift_down(p, 2)
        w = w_ref[0]
        conv = w[2:3] * p + w[1:2] * p1 + w[0:1] * p2
        dt = d_ref[0]
        db_ref[0] = (dt * conv).astype(BF16)
        dcv = dt * b_ref[0]
        dp = w[2:3] * dcv + w[1:2] * _shift_up(dcv, 1) + w[0:1] * _shift_up(dcv, 2)
        dc_ref[0] = (dp * xi).astype(BF16)
        dx_ref[0] = (dp * cg).astype(BF16)
        dw = jnp.concatenate([jnp.sum(dcv * p2, axis=0, keepdims=True), jnp.sum(dcv * p1, axis=0, keepdims=True),
                              jnp.sum(dcv * p, axis=0, keepdims=True), jnp.zeros((5, LANES), F32)], axis=0)
        dw_ref[0] = dw

    def uspec(off):
        return pl.BlockSpec((1, S, LANES), lambda g, h: (g + off, 0, h))

    ospec = pl.BlockSpec((1, S, LANES), lambda g, h: (g, 0, h))
    wspec = pl.BlockSpec((1, 8, LANES), lambda g, h: (g, 0, h))
    return pl.pallas_call(
        body, name="conv_mixer_bwd", grid=(3, nh),
        in_specs=[uspec(0), uspec(3), uspec(6), wspec, ospec],
        out_specs=[ospec, ospec, ospec, wspec],
        out_shape=[SDS((3, S, GROUP), BF16)] * 3 + [SDS((3, 8, GROUP), F32)],
        compiler_params=_params(("parallel", "parallel")),
    )(u, u, u, cw, dm)


def qk_conv_fwd(u, qw):
    _, S, _ = u.shape
    nh = GROUP // LANES

    def body(u_ref, w_ref, o_ref):
        x = u_ref[0]
        w = w_ref[0]
        pre = w[3:4] * x + w[2:3] * _shift_down(x, 1) + w[1:2] * _shift_down(x, 2) + w[0:1] * _shift_down(x, 3)
        o_ref[0] = pre * _sigmoid(pre)

    spec = pl.BlockSpec((1, S, LANES), lambda g, h: (g, 0, h))
    return pl.pallas_call(
        body, name="qk_conv_fwd", grid=(8, nh),
        in_specs=[spec, pl.BlockSpec((1, 8, LANES), lambda g, h: (g, 0, h))],
        out_specs=spec,
        out_shape=SDS((8, S, GROUP), F32),
        compiler_params=_params(("parallel", "parallel")),
    )(u, qw)


def qk_conv_bwd(u, qw, dqk):
    _, S, _ = u.shape
    nh = GROUP // LANES

    def body(u_ref, w_ref, d_ref, du_ref, dw_ref):
        x = u_ref[0]
        w = w_ref[0]
        x1, x2, x3 = _shift_down(x, 1), _shift_down(x, 2), _shift_down(x, 3)
        pre = w[3:4] * x + w[2:3] * x1 + w[1:2] * x2 + w[0:1] * x3
        sig = _sigmoid(pre)
        dpre = d_ref[0] * (sig * (1.0 + pre * (1.0 - sig)))
        du = w[3:4] * dpre + w[2:3] * _shift_up(dpre, 1) + w[1:2] * _shift_up(dpre, 2) + w[0:1] * _shift_up(dpre, 3)
        du_ref[0] = du.astype(BF16)
        dw = jnp.concatenate([jnp.sum(dpre * x3, axis=0, keepdims=True), jnp.sum(dpre * x2, axis=0, keepdims=True),
                              jnp.sum(dpre * x1, axis=0, keepdims=True), jnp.sum(dpre * x, axis=0, keepdims=True),
                              jnp.zeros((4, LANES), F32)], axis=0)
        dw_ref[0] = dw

    spec = pl.BlockSpec((1, S, LANES), lambda g, h: (g, 0, h))
    wspec = pl.BlockSpec((1, 8, LANES), lambda g, h: (g, 0, h))
    return pl.pallas_call(
        body, name="qk_conv_bwd", grid=(8, nh),
        in_specs=[spec, wspec, spec],
        out_specs=[spec, wspec],
        out_shape=[SDS((8, S, GROUP), BF16), SDS((8, 8, GROUP), F32)],
        compiler_params=_params(("parallel", "parallel")),
    )(u, qw, dqk)


def _head_masks():
    lane = lax.broadcasted_iota(jnp.int32, (1, D_XA), 1)
    return [(lane >= h * XA_HEAD_DIM) & (lane < (h + 1) * XA_HEAD_DIM) for h in range(XA_HEADS)]


def xattn_fwd(u, qg, kv):
    _, S, _ = u.shape
    ts = _tile(S, 512)
    scale = XA_HEAD_DIM ** -0.5

    def body(q_ref, kv_ref, o_ref):
        q = q_ref[0]
        k = kv_ref[0].astype(BF16)
        v = kv_ref[1]
        o = jnp.zeros((ts, D_XA), F32)
        for m in _head_masks():
            s = _dot(jnp.where(m, q, 0.0).astype(BF16), k, NT) * scale
            s = s - jnp.max(s, axis=-1, keepdims=True)
            e = jnp.exp(s)
            p = e / jnp.sum(e, axis=-1, keepdims=True)
            o = o + _dot(p.astype(BF16), jnp.where(m, v, 0.0).astype(BF16), NN)
        o_ref[0] = o.astype(BF16)

    return pl.pallas_call(
        body, name="xattn_fwd", grid=(S // ts,),
        in_specs=[pl.BlockSpec((1, ts, GROUP), lambda s: (qg, s, 0)), pl.BlockSpec((2, N_MEM, GROUP), lambda s: (0, 0, 0))],
        out_specs=pl.BlockSpec((1, ts, GROUP), lambda s: (0, s, 0)),
        out_shape=SDS((1, S, GROUP), BF16),
        compiler_params=_params(("parallel",)),
    )(u, kv)


def xattn_bwd(u, qg, kv, dm, dg):
    _, S, _ = u.shape
    ts = _tile(S, 512)
    scale = XA_HEAD_DIM ** -0.5

    def body(q_ref, kv_ref, do_ref, dq_ref, dkv_ref):
        @pl.when(pl.program_id(0) == 0)
        def _():
            dkv_ref[...] = jnp.zeros_like(dkv_ref)

        q = q_ref[0]
        k = kv_ref[0]
        v = kv_ref[1]
        kb = k.astype(BF16)
        do = do_ref[0]
        dq = jnp.zeros((ts, D_XA), F32)
        dk = jnp.zeros((N_MEM, D_XA), F32)
        dv = jnp.zeros((N_MEM, D_XA), F32)
        for m in _head_masks():
            qm = jnp.where(m, q, 0.0).astype(BF16)
            s = _dot(qm, kb, NT) * scale
            s = s - jnp.max(s, axis=-1, keepdims=True)
            e = jnp.exp(s)
            p = e / jnp.sum(e, axis=-1, keepdims=True)
            dom = jnp.where(m, do, 0.0).astype(BF16)
            dp = _dot(dom, jnp.where(m, v, 0.0).astype(BF16), NT)
            ds = (p * (dp - jnp.sum(dp * p, axis=-1, keepdims=True)) * scale).astype(BF16)
            dq = dq + _dot(ds, jnp.where(m, k, 0.0).astype(BF16), NN)
            dk = dk + _dot(ds, qm, TN)
            dv = dv + _dot(p.astype(BF16), dom, TN)
        dq_ref[0] = dq.astype(BF16)
        dkv_ref[0] += dk
        dkv_ref[1] += dv

    return pl.pallas_call(
        body, name="xattn_bwd", grid=(S // ts,),
        in_specs=[pl.BlockSpec((1, ts, GROUP), lambda s: (qg, s, 0)), pl.BlockSpec((2, N_MEM, GROUP), lambda s: (0, 0, 0)),
                  pl.BlockSpec((1, ts, GROUP), lambda s: (dg, s, 0))],
        out_specs=[pl.BlockSpec((1, ts, GROUP), lambda s: (0, s, 0)), pl.BlockSpec((2, N_MEM, GROUP), lambda s: (0, 0, 0))],
        out_shape=[SDS((1, S, GROUP), BF16), SDS((2, N_MEM, GROUP), F32)],
        compiler_params=_params(("arbitrary",)),
    )(u, kv, dm)


ML_BLOCK_CHUNKS = 4
H4 = ML_HEADS
L = ML_CHUNK
NLANE = ML_HEAD_DIM


def _chunk_consts():
    r = lax.broadcasted_iota(jnp.int32, (1, L, L), 1)
    c = lax.broadcasted_iota(jnp.int32, (1, L, L), 2)
    return r >= c, r <= c, r == c


def _gate_cols(gb):
    lane = lax.broadcasted_iota(jnp.int32, gb.shape, 1)
    li = jnp.stack([jnp.sum(jnp.where(lane == h, gb, 0.0), axis=1, keepdims=True) for h in range(H4)])
    gf = jnp.stack([jnp.sum(jnp.where(lane == H4 + h, gb, 0.0), axis=1, keepdims=True) for h in range(H4)])
    return li, gf


def _log_sigmoid(x):
    return jnp.minimum(x, 0.0) - jnp.log(1.0 + jnp.exp(-jnp.abs(x)))


def _chunk_forward(q, k, v_aug, li_col, lf_col, c_prev, m_prev):
    tri, tri_t, eye = _chunk_consts()
    lf_row = jnp.sum(jnp.where(eye, lf_col, 0.0), axis=1, keepdims=True)
    li_row = jnp.sum(jnp.where(eye, li_col, 0.0), axis=1, keepdims=True)
    bcum_col = jnp.sum(jnp.where(tri, lf_row, 0.0), axis=2, keepdims=True)
    bcum_row = jnp.sum(jnp.where(tri_t, lf_col, 0.0), axis=1, keepdims=True)
    log_d = jnp.where(tri, bcum_col - bcum_row + li_row, NEG)
    log_inter = bcum_col + m_prev
    m_t = jnp.maximum(log_inter, jnp.max(log_d, axis=2, keepdims=True))
    w_intra = jnp.exp(log_d - m_t)
    w_inter = jnp.exp(log_inter - m_t)
    sc = _bdot(q, k, 2, 2) * w_intra
    qc = _bdot1(q, c_prev, 2, 1)
    num = _bdot(sc, v_aug, 2, 1) + w_inter * qc
    lane = lax.broadcasted_iota(jnp.int32, num.shape, 2)
    den = jnp.sum(jnp.where(lane == NLANE, num, 0.0), axis=2, keepdims=True)
    e_m = jnp.exp(-m_t)
    b_last = jnp.sum(lf_row, axis=2, keepdims=True)
    log_w = b_last - bcum_col + li_col
    m_new = jnp.maximum(b_last + m_prev, jnp.max(log_w, axis=1, keepdims=True))
    w_k = jnp.exp(log_w - m_new)
    decay = jnp.exp(b_last + m_prev - m_new)
    return dict(w_intra=w_intra, w_inter=w_inter, sc=sc, qc=qc, num=num, den=den, e_m=e_m, lane=lane,
                w_k=w_k, decay=decay, m_new=m_new)


def mlstm_fwd(qk, u, bg):
    _, S, _ = qk.shape
    nc = S // L
    cb = min(ML_BLOCK_CHUNKS, nc)
    rows = cb * L
    kscale = ML_HEAD_DIM ** -0.5

    def body(qk_ref, v_ref, g_ref, bg_ref, h_ref, cst_ref, mst_ref, c_sc, m_sc):
        @pl.when(pl.program_id(0) == 0)
        def _():
            c_sc[...] = jnp.zeros_like(c_sc)
            m_sc[...] = jnp.zeros_like(m_sc)

        for c in range(cb):
            sl = pl.ds(c * L, L)
            q = qk_ref[0:H4, sl, :]
            k = qk_ref[H4:2 * H4, sl, :] * kscale
            v = v_ref[:, sl, :]
            lane = lax.broadcasted_iota(jnp.int32, v.shape, 2)
            v_aug = jnp.where(lane == NLANE, 1.0, v)
            li_col, gf = _gate_cols(g_ref[0, sl, :] + bg_ref[...])
            lf_col = _log_sigmoid(gf)
            c_prev = c_sc[...]
            m_prev = m_sc[...]
            f = _chunk_forward(q, k, v_aug, li_col, lf_col, c_prev, m_prev)
            r = 1.0 / jnp.maximum(jnp.abs(f["den"]), f["e_m"])
            h_ref[:, sl, :] = jnp.where(lane < NLANE, f["num"] * r, 0.0)
            cst_ref[c] = c_prev
            mst_ref[c] = jnp.broadcast_to(m_prev, (H4, 1, LANES))
            c_sc[...] = f["decay"] * c_prev + _bdot(k * f["w_k"], v_aug, 1, 1)
            m_sc[...] = f["m_new"]

    def hspec(blk):
        return pl.BlockSpec((H4, rows, GROUP), lambda i: (blk, i, 0))

    return pl.pallas_call(
        body, name="mlstm_fwd", grid=(nc // cb,),
        in_specs=[pl.BlockSpec((2 * H4, rows, GROUP), lambda i: (0, i, 0)), hspec(2),
                  pl.BlockSpec((1, rows, GROUP), lambda i: (17, i, 0)), pl.BlockSpec((1, GROUP), lambda i: (0, 0))],
        out_specs=[hspec(0), pl.BlockSpec((cb, H4, GROUP, GROUP), lambda i: (i, 0, 0, 0)),
                   pl.BlockSpec((cb, H4, 1, LANES), lambda i: (i, 0, 0, 0))],
        out_shape=[SDS((H4, S, GROUP), F32), SDS((nc, H4, GROUP, GROUP), F32), SDS((nc, H4, 1, LANES), F32)],
        scratch_shapes=[pltpu.VMEM((H4, GROUP, GROUP), F32), pltpu.VMEM((H4, 1, 1), F32)],
        compiler_params=_params(("arbitrary",)),
    )(qk, u, u, bg)


def mlstm_bwd(qk, u, bg, cst, mst, dh):
    _, S, _ = qk.shape
    nc = S // L
    cb = min(ML_BLOCK_CHUNKS, nc)
    rows = cb * L
    nb = nc // cb
    kscale = ML_HEAD_DIM ** -0.5

    def body(qk_ref, v_ref, g_ref, bg_ref, cst_ref, mst_ref, dh_ref, dqk_ref, dv_ref, dg_ref, dbg_ref, dc_sc):
        @pl.when(pl.program_id(0) == 0)
        def _():
            dc_sc[...] = jnp.zeros_like(dc_sc)
            dbg_ref[...] = jnp.zeros_like(dbg_ref)

        tri, tri_t, eye = _chunk_consts()
        for c in reversed(range(cb)):
            sl = pl.ds(c * L, L)
            q = qk_ref[0:H4, sl, :]
            k = qk_ref[H4:2 * H4, sl, :] * kscale
            v = v_ref[:, sl, :]
            lane = lax.broadcasted_iota(jnp.int32, v.shape, 2)
            v_aug = jnp.where(lane == NLANE, 1.0, v)
            li_col, gf = _gate_cols(g_ref[0, sl, :] + bg_ref[...])
            lf_col = _log_sigmoid(gf)
            c_prev = cst_ref[c]
            m_prev = mst_ref[c][:, :, 0:1]
            f = _chunk_forward(q, k, v_aug, li_col, lf_col, c_prev, m_prev)
            w_intra, w_inter, sc, num, den, e_m = f["w_intra"], f["w_inter"], f["sc"], f["num"], f["den"], f["e_m"]
            absd = jnp.abs(den)
            r = 1.0 / jnp.maximum(absd, e_m)
            dhv = dh_ref[:, sl, :]
            s1 = jnp.sum(jnp.where(lane < NLANE, dhv * num, 0.0), axis=2, keepdims=True)
            dden = jnp.where(absd > e_m, -s1 * r * r * jnp.sign(den), 0.0)
            dnum = jnp.where(lane == NLANE, dden, jnp.where(lane < NLANE, dhv * r, 0.0))
            dsc = _bdot1(dnum, v_aug, 2, 2)
            dv = _bdot1(sc, dnum, 1, 1)
            gmat = dsc * sc
            dqk = dsc * w_intra
            dq = _bdot1(dqk, k, 2, 1) + w_inter * _bdot1(dnum, c_prev, 2, 2)
            dk = _bdot1(dqk, q, 1, 1)
            dc_prev = _bdot(q * w_inter, dnum, 1, 1)
            dlog_inter = jnp.sum(dnum * f["qc"], axis=2, keepdims=True) * w_inter
            dbcum_col = dlog_inter + jnp.sum(gmat, axis=2, keepdims=True)
            g_row = jnp.sum(gmat, axis=1, keepdims=True)
            dcn = dc_sc[...]
            w_k, decay = f["w_k"], f["decay"]
            kw = k * w_k
            dc_prev = dc_prev + decay * dcn
            db_last = jnp.sum(jnp.sum(dcn * c_prev, axis=2, keepdims=True), axis=1, keepdims=True) * decay
            dkw = _bdot(v_aug, dcn, 2, 2)
            dv = dv + _bdot1(kw, dcn, 2, 1)
            dk = dk + dkw * w_k
            dlogw = jnp.sum(dkw * k, axis=2, keepdims=True) * w_k
            db_last = db_last + jnp.sum(dlogw, axis=1, keepdims=True)
            dbcum_col = dbcum_col - dlogw
            rowi = lax.broadcasted_iota(jnp.int32, (1, L, 1), 1)
            dbcum_col = dbcum_col + jnp.where(rowi == L - 1, db_last, 0.0)
            dbcum_row = jnp.sum(jnp.where(eye, dbcum_col, 0.0), axis=1, keepdims=True) - g_row
            dlf_col = jnp.sum(jnp.where(tri_t, dbcum_row, 0.0), axis=2, keepdims=True)
            dli_col = dlogw + jnp.sum(jnp.where(eye, g_row, 0.0), axis=2, keepdims=True)
            dgf_col = dlf_col * _sigmoid(-gf)
            lane_g = lax.broadcasted_iota(jnp.int32, (L, GROUP), 1)
            dg = jnp.zeros((L, GROUP), F32)
            for h in range(H4):
                dg = dg + jnp.where(lane_g == h, dli_col[h], 0.0) + jnp.where(lane_g == H4 + h, dgf_col[h], 0.0)
            dqk_ref[0:H4, sl, :] = dq
            dqk_ref[H4:2 * H4, sl, :] = dk * kscale
            dv_ref[:, sl, :] = jnp.where(lane < NLANE, dv, 0.0).astype(BF16)
            dg_ref[0, sl, :] = dg.astype(BF16)
            dbg_ref[...] += jnp.sum(dg, axis=0, keepdims=True)
            dc_sc[...] = dc_prev

    def hspec(blk):
        return pl.BlockSpec((H4, rows, GROUP), lambda i: (blk, nb - 1 - i, 0))

    gspec = pl.BlockSpec((1, rows, GROUP), lambda i: (17, nb - 1 - i, 0))
    qkspec = pl.BlockSpec((2 * H4, rows, GROUP), lambda i: (0, nb - 1 - i, 0))
    return pl.pallas_call(
        body, name="mlstm_bwd", grid=(nb,),
        in_specs=[qkspec, hspec(2), gspec, pl.BlockSpec((1, GROUP), lambda i: (0, 0)),
                  pl.BlockSpec((cb, H4, GROUP, GROUP), lambda i: (nb - 1 - i, 0, 0, 0)),
                  pl.BlockSpec((cb, H4, 1, LANES), lambda i: (nb - 1 - i, 0, 0, 0)), hspec(0)],
        out_specs=[qkspec, hspec(0), pl.BlockSpec((1, rows, GROUP), lambda i: (0, nb - 1 - i, 0)),
                   pl.BlockSpec((1, GROUP), lambda i: (0, 0))],
        out_shape=[SDS((2 * H4, S, GROUP), F32), SDS((H4, S, GROUP), BF16),
                   SDS((1, S, GROUP), BF16), SDS((1, GROUP), F32)],
        scratch_shapes=[pltpu.VMEM((H4, GROUP, GROUP), F32)],
        compiler_params=_params(("arbitrary",)),
    )(qk, u, u, bg, cst, mst, dh)


def head_norm_fwd(hm, u, hg):
    _, S, _ = hm.shape
    ts = _tile(S, 512)

    def body(h_ref, o_ref, g_ref, t_ref):
        h = h_ref[0]
        lane = lax.broadcasted_iota(jnp.int32, h.shape, 1)
        valid = lane < ML_HEAD_DIM
        mu = jnp.sum(h, axis=-1, keepdims=True) * (1.0 / ML_HEAD_DIM)
        hc = jnp.where(valid, h - mu, 0.0)
        var = jnp.sum(hc * hc, axis=-1, keepdims=True) * (1.0 / ML_HEAD_DIM)
        hn = hc * lax.rsqrt(var + LN_EPS) * g_ref[0]
        t_ref[0] = (_sigmoid(o_ref[0]) * hn).astype(BF16)

    return pl.pallas_call(
        body, name="head_norm_fwd", grid=(H4, S // ts),
        in_specs=[pl.BlockSpec((1, ts, GROUP), lambda h, s: (h, s, 0)), pl.BlockSpec((1, ts, GROUP), lambda h, s: (12 + h, s, 0)),
                  pl.BlockSpec((1, 1, GROUP), lambda h, s: (h, 0, 0))],
        out_specs=pl.BlockSpec((1, ts, GROUP), lambda h, s: (h, s, 0)),
        out_shape=SDS((H4, S, GROUP), BF16),
        compiler_params=_params(("parallel", "parallel")),
    )(hm, u, hg)


def head_norm_bwd(hm, u, hg, dm):
    _, S, _ = hm.shape
    ts = _tile(S, 512)

    def body(h_ref, o_ref, g_ref, d_ref, dh_ref, do_ref, dg_ref):
        @pl.when(pl.program_id(1) == 0)
        def _():
            dg_ref[...] = jnp.zeros_like(dg_ref)

        h = h_ref[0]
        lane = lax.broadcasted_iota(jnp.int32, h.shape, 1)
        valid = lane < ML_HEAD_DIM
        inv = 1.0 / ML_HEAD_DIM
        mu = jnp.sum(h, axis=-1, keepdims=True) * inv
        hc = jnp.where(valid, h - mu, 0.0)
        var = jnp.sum(hc * hc, axis=-1, keepdims=True) * inv
        rstd = lax.rsqrt(var + LN_EPS)
        xhat = hc * rstd
        g = g_ref[0]
        sig = _sigmoid(o_ref[0])
        dt = jnp.where(valid, d_ref[0], 0.0)
        do_ref[0] = (dt * xhat * g * sig * (1.0 - sig)).astype(BF16)
        dhn = dt * sig
        dg_ref[0] += jnp.sum(dhn * xhat, axis=0, keepdims=True)
        dxh = dhn * g
        m1 = jnp.sum(dxh, axis=-1, keepdims=True) * inv
        m2 = jnp.sum(dxh * xhat, axis=-1, keepdims=True) * inv
        dh_ref[0] = jnp.where(valid, rstd * (dxh - m1 - xhat * m2), 0.0)

    spec = pl.BlockSpec((1, ts, GROUP), lambda h, s: (h, s, 0))
    gspec = pl.BlockSpec((1, 1, GROUP), lambda h, s: (h, 0, 0))
    return pl.pallas_call(
        body, name="head_norm_bwd", grid=(H4, S // ts),
        in_specs=[spec, pl.BlockSpec((1, ts, GROUP), lambda h, s: (12 + h, s, 0)), gspec, spec],
        out_specs=[spec, spec, gspec],
        out_shape=[SDS((H4, S, GROUP), F32), SDS((H4, S, GROUP), BF16), SDS((H4, 1, GROUP), F32)],
        compiler_params=_params(("parallel", "arbitrary")),
    )(hm, u, hg, dm)


def _adamw_math(w, g, m, v):
    c1 = 1.0 / (1.0 - ADAM_B1 ** ADAM_STEP)
    c2 = 1.0 / (1.0 - ADAM_B2 ** ADAM_STEP)
    nm = ADAM_B1 * m + (1.0 - ADAM_B1) * g
    nv = ADAM_B2 * v + (1.0 - ADAM_B2) * (g * g)
    return -ADAM_LR * ((nm * c1) / (jnp.sqrt(nv * c2) + ADAM_EPS) + ADAM_WD * w), nm, nv


def _row_tile(R, cap=512):
    return R if R <= cap else max(d for d in range(8, cap + 1, 8) if R % d == 0)


def adamw_into(w, m, v, g, outs, idx, after, name):
    R, C = g.shape
    tr = _row_tile(R)
    lead = (0,) * len(idx)

    def body(w_ref, m_ref, v_ref, g_ref, *rest):
        go_ref, d_ref, nm_ref, nv_ref, token = rest[-5:]
        token[...] = jnp.zeros_like(token)
        gv = g_ref[...]
        d, nm, nv = _adamw_math(w_ref[lead], gv, m_ref[lead], v_ref[lead])
        go_ref[lead] = gv
        d_ref[lead] = d
        nm_ref[lead] = nm
        nv_ref[lead] = nv

    blk = pl.BlockSpec((1,) * len(idx) + (tr, C), lambda r: idx + (r, 0))
    any_space = pl.BlockSpec(memory_space=pl.ANY)
    in_specs, args, aliases = [blk, blk, blk, pl.BlockSpec((tr, C), lambda r: (r, 0)), any_space], [w, m, v, g, g if after is None else after], {}
    if outs is not None:
        in_specs += [any_space] * 4
        args += list(outs)
        aliases = {5 + i: i for i in range(4)}
    out = pl.pallas_call(
        body, name=name, grid=(R // tr,),
        in_specs=in_specs, out_specs=[blk] * 4 + [pl.BlockSpec((8, LANES), lambda r: (0, 0))],
        out_shape=[SDS(w.shape, F32)] * 4 + [SDS((8, LANES), F32)],
        input_output_aliases=aliases, compiler_params=_params(("arbitrary",)),
    )(*args)
    return out[:4], out[4]


def adamw(w, g, m, v, name):
    R, C = w.shape
    tr = _row_tile(R)

    def body(w_ref, g_ref, m_ref, v_ref, d_ref, nm_ref, nv_ref):
        d_ref[...], nm_ref[...], nv_ref[...] = _adamw_math(w_ref[...], g_ref[...], m_ref[...], v_ref[...])

    spec = pl.BlockSpec((tr, C), lambda i: (i, 0))
    return pl.pallas_call(
        body, name=name, grid=(R // tr,),
        in_specs=[spec] * 4, out_specs=[spec] * 3,
        out_shape=[SDS((R, C), F32)] * 3,
        compiler_params=_params(("parallel",)),
    )(w, g, m, v)


HBM = pl.BlockSpec(memory_space=pl.ANY)
ROW_SPLIT = 4


def _position():
    x, y, c = lax.axis_index("x"), lax.axis_index("y"), lax.axis_index("c")
    return x, y, c, [(1 - x, y), (x, 1 - y), (1 - x, 1 - y)]


def _unique(items):
    arrays = []
    for a, _ in items:
        if not any(a is b for b in arrays):
            arrays.append(a)
    return arrays, [next(i for i, b in enumerate(arrays) if b is a) for a, _ in items]


def place_own(items, me, after, name):
    arrays, src_of = _unique(items)
    n = len(items)
    shapes = [a.shape[len(p):] for a, p in items]

    def body(me_ref, *refs):
        for t in range(n):
            refs[n + 1 + t][0] = refs[t][(0,) * len(items[t][1])]

    in_specs, out_specs = [], []
    for (a, p), shp in zip(items, shapes):
        blk = shp[:-2] + (shp[-2] // ROW_SPLIT, shp[-1])
        lead = (0,) * (len(shp) - 2)
        in_specs.append(pl.BlockSpec((1,) * len(p) + blk, functools.partial(lambda r, me_ref, p, lead: p + lead + (r, 0), p=p, lead=lead)))
        out_specs.append(pl.BlockSpec((1,) + blk, functools.partial(lambda r, me_ref, lead: (me_ref[0],) + lead + (r, 0), lead=lead)))
    in_specs.append(pl.BlockSpec(memory_space=pl.ANY))
    return pl.pallas_call(
        body, name=name,
        grid_spec=pltpu.PrefetchScalarGridSpec(num_scalar_prefetch=1, grid=(ROW_SPLIT,), in_specs=in_specs, out_specs=out_specs),
        out_shape=[SDS((N_CHIPS,) + tuple(shp), a.dtype) for shp, (a, _) in zip(shapes, items)],
        compiler_params=_params(("parallel",)),
    )(me, *[arrays[i] for i in src_of], after)


SEM = pl.BlockSpec(memory_space=pltpu.SEMAPHORE)
IN_HBM = pl.BlockSpec(memory_space=pltpu.HBM)
DATAFLOW = pltpu.SideEffectType.DATAFLOW_SIDE_EFFECTING


def split_start(bufs, plan, n_copies, after, name):
    n = len(bufs)

    def body(*refs):
        send, recv, token = refs[n + 1], refs[n + 2], refs[-1]
        x, y, c, chips = _position()
        for k, (src, dst, dev) in enumerate(plan(refs[:n], x, y, c, chips)):
            pltpu.make_async_remote_copy(src_ref=src, dst_ref=dst, send_sem=send.at[k], recv_sem=recv.at[k],
                                         device_id=dev, device_id_type=MESH).start()
        token[...] = jnp.zeros_like(token)

    out = pl.pallas_call(
        body, name=name,
        out_shape=(pltpu.SemaphoreType.DMA((n_copies,)), pltpu.SemaphoreType.DMA((n_copies,)),
                   *[pltpu.HBM(b.shape, b.dtype) for b in bufs], SDS((8, LANES), F32)),
        in_specs=[IN_HBM] * n + [pl.BlockSpec(memory_space=pl.ANY)],
        out_specs=(SEM, SEM, *[IN_HBM] * n, pl.BlockSpec(memory_space=pltpu.VMEM)),
        input_output_aliases={i: 2 + i for i in range(n)},
        compiler_params=pltpu.CompilerParams(has_side_effects=DATAFLOW),
    )(*[pltpu.with_memory_space_constraint(b, pltpu.HBM) for b in bufs], after)
    return out[0], out[1], list(out[2:2 + n]), out[-1]


def split_wait(send, recv, bufs, plan, after, name):
    n = len(bufs)

    def body(*refs):
        send_ref, recv_ref = refs[n], refs[n + 1]
        x, y, c, chips = _position()
        for k, (src, dst, dev) in enumerate(plan(refs[:n], x, y, c, chips)):
            cp = pltpu.make_async_remote_copy(src_ref=src, dst_ref=dst, send_sem=send_ref.at[k], recv_sem=recv_ref.at[k],
                                              device_id=dev, device_id_type=MESH)
            cp.wait_send()
            cp.wait_recv()

    return list(pl.pallas_call(
        body, name=name, out_shape=tuple(pltpu.HBM(b.shape, b.dtype) for b in bufs),
        in_specs=[IN_HBM] * n + [SEM, SEM, pl.BlockSpec(memory_space=pl.ANY)], out_specs=tuple([IN_HBM] * n),
        input_output_aliases={i: i for i in range(n)},
        compiler_params=pltpu.CompilerParams(has_side_effects=DATAFLOW),
    )(*bufs, send, recv, after))


def _gather_plan(shapes, landing):
    n = len(shapes)

    def plan(refs, x, y, c, chips):
        out = []
        for t in range(n):
            half = shapes[t][0] // 2
            rows = pl.ds(c * half, half)
            for cx, cy in chips:
                slot = 2 * cx + cy if landing else 2 * x + y
                out.append((refs[t].at[rows], refs[n + t].at[slot, rows], (cx, cy, c)))
        return out

    return plan


def gather_start(shards, placed, after, name):
    shapes = [s.shape for s in shards]
    send, recv, bufs, token = split_start(list(shards) + list(placed), _gather_plan(shapes, False), 3 * len(shards), after, name)
    return (send, recv, bufs, shapes), token


def gather_wait(state, after, name):
    send, recv, bufs, shapes = state
    return split_wait(send, recv, bufs, _gather_plan(shapes, True), after, name)[len(shapes):]


def gather_pass_on(placed, shapes, name):
    n = len(placed)

    def body(*refs):
        outs, send, recv = refs[n:2 * n], refs[2 * n], refs[2 * n + 1]
        x, y, c, chips = _position()
        cps = []
        for t in range(n):
            half = shapes[t][0] // 2
            for j, (cx, cy) in enumerate(chips):
                piece = outs[t].at[2 * cx + cy, pl.ds(c * half, half)]
                cp = pltpu.make_async_remote_copy(src_ref=piece, dst_ref=piece, send_sem=send.at[3 * t + j], recv_sem=recv.at[3 * t + j],
                                                  device_id=(x, y, 1 - c), device_id_type=MESH)
                cp.start()
                cps.append(cp)
        for t in range(n):
            half = shapes[t][0] // 2
            for j, (cx, cy) in enumerate(chips):
                piece = outs[t].at[2 * cx + cy, pl.ds((1 - c) * half, half)]
                pltpu.make_async_remote_copy(src_ref=piece, dst_ref=piece, send_sem=send.at[3 * t + j], recv_sem=recv.at[3 * t + j],
                                             device_id=(x, y, 1 - c), device_id_type=MESH).wait_recv()
        for cp in cps:
            cp.wait_send()

    return pl.pallas_call(
        body, name=name,
        in_specs=[HBM] * n, out_specs=[HBM] * n,
        out_shape=[SDS(p.shape, p.dtype) for p in placed],
        input_output_aliases={t: t for t in range(n)},
        scratch_shapes=[pltpu.SemaphoreType.DMA((3 * n,))] * 2,
    )(*placed)


def _flip(k, x, y, c):
    return ((1 - x) if k & 4 else x, (1 - y) if k & 2 else y, (1 - c) if k & 1 else c)


def small_allgather(v, reduce):
    R, C = v.shape

    def body(v_ref, o_ref, *scratch):
        if reduce:
            buf, send, recv = scratch
        else:
            buf, (send, recv) = o_ref, scratch
        x, y, c, _ = _position()
        me = 4 * x + 2 * y + c
        buf[me] = v_ref[...]
        sends = []
        for k in range(1, N_DEV):
            cp = pltpu.make_async_remote_copy(src_ref=v_ref, dst_ref=buf.at[me], send_sem=send.at[k - 1], recv_sem=recv.at[k - 1],
                                              device_id=_flip(k, x, y, c), device_id_type=MESH)
            cp.start()
            sends.append(cp)
        for k in range(1, N_DEV):
            px, py, pc = _flip(k, x, y, c)
            pltpu.make_async_remote_copy(src_ref=v_ref, dst_ref=buf.at[4 * px + 2 * py + pc], send_sem=send.at[k - 1],
                                         recv_sem=recv.at[k - 1], device_id=(px, py, pc), device_id_type=MESH).wait_recv()
        for cp in sends:
            cp.wait_send()
        if reduce:
            acc = buf[0]
            for i in range(1, N_DEV):
                acc = acc + buf[i]
            o_ref[...] = acc

    vm = pl.BlockSpec(memory_space=pltpu.VMEM)
    sems = [pltpu.SemaphoreType.DMA((N_DEV - 1,)), pltpu.SemaphoreType.DMA((N_DEV - 1,))]
    return pl.pallas_call(
        body, name="small_allreduce" if reduce else "small_allgather",
        in_specs=[vm], out_specs=vm,
        out_shape=SDS((R, C) if reduce else (N_DEV, R, C), F32),
        scratch_shapes=([pltpu.VMEM((N_DEV, R, C), F32)] if reduce else []) + sems,
    )(v)


def rs_exchange_sibling(gs):
    n = len(gs)

    def body(*refs):
        ins, outs, send, recv = refs[:n], refs[n:2 * n], refs[2 * n], refs[2 * n + 1]
        x, y, c, _ = _position()
        cps = []
        for t in range(n):
            cp = pltpu.make_async_remote_copy(src_ref=ins[t].at[:, 1 - c], dst_ref=outs[t], send_sem=send.at[t], recv_sem=recv.at[t],
                                              device_id=(x, y, 1 - c), device_id_type=MESH)
            cp.start()
            cps.append(cp)
        for cp in cps:
            cp.wait()

    return pl.pallas_call(
        body, name="rs_exchange_sibling", in_specs=[HBM] * n, out_specs=[HBM] * n,
        out_shape=[SDS((g.shape[0],) + g.shape[2:], g.dtype) for g in gs],
        scratch_shapes=[pltpu.SemaphoreType.DMA((n,)), pltpu.SemaphoreType.DMA((n,))],
    )(*gs)


def rs_pair_add(gs, rs, c):
    n = len(gs)

    def body(c_ref, *refs):
        for t in range(n):
            refs[2 * n + t][0] = (refs[t][0, 0].astype(F32) + refs[n + t][0].astype(F32)).astype(BF16)

    in_specs, out_specs, out_shape = [], [], []
    for g in gs:
        _, _, h, C = g.shape
        in_specs.append(pl.BlockSpec((1, 1, h // ROW_SPLIT, C), lambda j, r, c_ref: (j, c_ref[0], r, 0)))
    for g in gs:
        _, _, h, C = g.shape
        spec = pl.BlockSpec((1, h // ROW_SPLIT, C), lambda j, r, c_ref: (j, r, 0))
        in_specs.append(spec)
        out_specs.append(spec)
        out_shape.append(SDS((N_CHIPS, h, C), BF16))
    return pl.pallas_call(
        body, name="rs_pair_add",
        grid_spec=pltpu.PrefetchScalarGridSpec(num_scalar_prefetch=1, grid=(N_CHIPS, ROW_SPLIT), in_specs=in_specs, out_specs=out_specs),
        out_shape=out_shape, compiler_params=_params(("parallel", "parallel")),
    )(c, *gs, *rs)


def _rs_plan(n):
    def plan(refs, x, y, c, chips):
        return [(refs[t].at[2 * cx + cy], refs[n + t].at[j], (cx, cy, c)) for t in range(n) for j, (cx, cy) in enumerate(chips)]

    return plan


def rs_chip_add(ps, qs, me_c):
    n = len(ps)

    def body(me_ref, *refs):
        for t in range(n):
            q = refs[n + t]
            refs[2 * n + t][0] = ((refs[t][0].astype(F32) + q[0].astype(F32)) + q[1].astype(F32)) + q[2].astype(F32)

    in_specs, out_specs, out_shape = [], [], []
    for p in ps:
        _, h, C = p.shape
        in_specs.append(pl.BlockSpec((1, h // ROW_SPLIT, C), lambda r, me_ref: (me_ref[0], r, 0)))
    for p in ps:
        _, h, C = p.shape
        in_specs.append(pl.BlockSpec((3, h // ROW_SPLIT, C), lambda r, me_ref: (0, r, 0)))
        out_specs.append(pl.BlockSpec((1, h // ROW_SPLIT, C), lambda r, me_ref: (me_ref[1], r, 0)))
        out_shape.append(SDS((2, h, C), F32))
    return pl.pallas_call(
        body, name="rs_chip_add",
        grid_spec=pltpu.PrefetchScalarGridSpec(num_scalar_prefetch=1, grid=(ROW_SPLIT,), in_specs=in_specs, out_specs=out_specs),
        out_shape=out_shape, compiler_params=_params(("parallel",)),
    )(me_c, *ps, *qs)


def rs_share(rs):
    n = len(rs)

    def body(*refs):
        outs, send, recv = refs[n:2 * n], refs[2 * n], refs[2 * n + 1]
        x, y, c, _ = _position()
        cps = []
        for t in range(n):
            cp = pltpu.make_async_remote_copy(src_ref=outs[t].at[c], dst_ref=outs[t].at[c], send_sem=send.at[t], recv_sem=recv.at[t],
                                              device_id=(x, y, 1 - c), device_id_type=MESH)
            cp.start()
            cps.append(cp)
        for cp in cps:
            cp.wait()

    return pl.pallas_call(
        body, name="rs_share", in_specs=[HBM] * n, out_specs=[HBM] * n,
        out_shape=[SDS(r.shape, r.dtype) for r in rs],
        input_output_aliases={t: t for t in range(n)},
        scratch_shapes=[pltpu.SemaphoreType.DMA((n,))] * 2,
    )(*rs)


def rs_begin(gs, after, name):
    c = lax.axis_index("c")
    n = len(gs)
    g5 = [g.reshape(N_CHIPS, 2, g.shape[1] // 2, g.shape[2]) for g in gs]
    from_sibling = rs_exchange_sibling(g5)
    pair = rs_pair_add(g5, from_sibling, jnp.reshape(c, (1,)).astype(jnp.int32))
    lands = [lax.empty((3,) + p.shape[1:], p.dtype) for p in pair]
    send, recv, bufs, token = split_start(list(pair) + lands, _rs_plan(n), 3 * n, from_sibling[0] if after is None else after, name)
    return (send, recv, bufs, [g.shape for g in gs]), token


def rs_end(state, after, name):
    x, y, c = lax.axis_index("x"), lax.axis_index("y"), lax.axis_index("c")
    send, recv, bufs, shapes = state
    n = len(shapes)
    bufs = split_wait(send, recv, bufs, _rs_plan(n), after, name)
    half = rs_chip_add(bufs[:n], bufs[n:], jnp.stack([2 * x + y, c]).astype(jnp.int32))
    both = rs_share(half)
    return [b.reshape(s[1], s[2]) for b, s in zip(both, shapes)]


def _pad_last(a, n):
    return jnp.pad(a, [(0, 0)] * (a.ndim - 1) + [(0, n - a.shape[-1])])


def _heads_to_groups(w):
    k = w.shape[0]
    return _pad_last(w.reshape(k, ML_HEADS, ML_HEAD_DIM).transpose(1, 0, 2), GROUP)


def _groups_to_heads(g):
    return g[:, :, :ML_HEAD_DIM].transpose(1, 0, 2).reshape(g.shape[1], D_TOK)


def _cols_to_groups(w):
    k, n = w.shape
    return w.reshape(k, n // GROUP, GROUP).transpose(1, 0, 2)


def _groups_to_cols(g):
    n, k, _ = g.shape
    return g.transpose(1, 0, 2).reshape(k, n * GROUP)


def _chips_to_cols(a):
    return a.transpose(1, 0, 2).reshape(a.shape[1], -1)


def _cols_to_chips(w):
    k, n = w.shape
    return w.reshape(k, N_CHIPS, n // N_CHIPS).transpose(1, 0, 2)


def _mlstm_in_groups(w):
    parts = [_heads_to_groups(w[:, i * D_TOK:(i + 1) * D_TOK]) for i in range(4)]
    gates = _pad_last(w[:, 4 * D_TOK:4 * D_TOK + 2 * ML_HEADS], GROUP)[None]
    qmem = w[:, 4 * D_TOK + 2 * ML_HEADS:][None]
    return jnp.concatenate(parts + [qmem, gates], axis=0)


def _mlstm_in_ungroup(g):
    parts = [_groups_to_heads(g[4 * i:4 * i + 4]) for i in range(4)]
    return jnp.concatenate(parts + [g[17][:, :2 * ML_HEADS], g[16]], axis=1)


def _taps_to_groups(w, width):
    taps = w.shape[0]
    g = _pad_last(w.reshape(taps, -1, width), GROUP).transpose(1, 0, 2)
    return jnp.pad(g, ((0, 0), (0, 8 - taps), (0, 0)))


def _groups_to_taps(g, taps, width):
    return g[:, :taps, :width].transpose(1, 0, 2).reshape(taps, -1)


SMALL_IN_COLS = 384
SMALL_OUT_COLS = 1536
SECTION = 8


class _Gathered:
    def __init__(self, srcs, groups, me):
        self.groups, self.states, self.ready = groups, [], {}
        self.group_of = {k: gi for gi, g in enumerate(groups) for k in g}
        token = me
        for gi, g in enumerate(groups):
            placed = place_own([(srcs[k], ()) for k in g], me, token, f"place_own_{gi}")
            state, token = gather_start([srcs[k] for k in g], placed, token, f"gather_start_{gi}")
            self.states.append(state)
        self.started = token

    def _get(self, key, after):
        gi = self.group_of[key]
        if gi not in self.ready:
            got = gather_wait(self.states[gi], after if gi else self.started, f"gather_wait_{gi}")
            self.ready[gi] = dict(zip(self.groups[gi], gather_pass_on(got, self.states[gi][3], f"gather_pass_on_{gi}")))
        return self.ready[gi][key]

    def ffn(self, l, i, after):
        return tuple(self._get((n, l, i), after) for n in ("wg", "wu", "wd"))

    def mixer(self, l, after):
        win = _chips_to_cols(self._get(("win", l), after))
        win = _cols_to_groups(win) if l % 2 == 0 else _mlstm_in_groups(win)
        wkv = _cols_to_groups(self._get(("wkv", l), after).reshape(D_MODEL, 2 * D_XA))
        wout = self._get(("wout", l), after)
        if l % 2:
            wout = wout.reshape(D_MODEL, D_MODEL)
            tok = jnp.pad(wout[:D_TOK].reshape(ML_HEADS, ML_HEAD_DIM, D_MODEL), ((0, 0), (0, GROUP - ML_HEAD_DIM), (0, 0)))
            wout = jnp.concatenate([tok, wout[D_TOK:][None]], axis=0)
        return win, wkv, wout


class _GradSink:
    def __init__(self, apply):
        self.queue, self.apply, self.count, self.done = [], apply, 0, None

    @staticmethod
    def _by_chip(key, g):
        if key[0] == "wkv":
            return _groups_to_cols(g).reshape(N_CHIPS, D_MODEL // N_CHIPS, 2 * D_XA)
        if key[0] == "win":
            return _cols_to_chips(_groups_to_cols(g) if key[1] % 2 == 0 else _mlstm_in_ungroup(g))
        if key[0] == "wout" and key[1] % 2:
            full = jnp.concatenate([g[:ML_HEADS, :ML_HEAD_DIM].reshape(D_TOK, D_MODEL), g[ML_HEADS]], axis=0)
            return full.reshape(N_CHIPS, D_MODEL // N_CHIPS, D_MODEL)
        return g

    def push(self, grads):
        keys = list(grads)
        state, token = rs_begin([self._by_chip(k, grads[k]) for k in keys], self.done, f"rs_start_{self.count}")
        if self.queue:
            self._finish(token)
        self.queue.append((keys, state, self.count))
        self.count += 1
        return token

    def flush(self):
        self._finish(self.done)

    def _finish(self, after):
        keys, state, i = self.queue.pop(0)
        for key, g in zip(keys, rs_end(state, after, f"rs_wait_{i}")):
            self.done = self.apply(key, g, self.done)


def _local_step(x, mem, tgt, P, weights, sink):
    memb = mem.astype(BF16)
    saved = []
    X, Xb = x, x.astype(BF16)
    after = Xb
    for l in range(DEPTH):
        s = {}
        s["x0b"] = Xb
        s["wa"] = weights.ffn(l, 0, after)
        s["g1a"], s["u1a"], s["ha"], s["z1"], X1, X1b = ffn_fwd(Xb, X, *s["wa"], P["ln_g"][l][0], P["ln_b"][l][0])
        s["x1b"] = X1b
        s["wm"] = win, wkv, wout = weights.mixer(l, X1b)
        u = proj(X1b, win, "mixer_in")
        kv = proj(memb, wkv, "mem_kv")
        s["u"], s["kv"] = u, kv
        if l % 2 == 0:
            tok = conv_mixer_fwd(u, P["convw"])
            qg = 9
        else:
            s["qk"] = qk_conv_fwd(u, P["qkw"])
            s["hm"], s["cst"], s["mst"] = mlstm_fwd(s["qk"], u, P["bg"])
            tok = head_norm_fwd(s["hm"], u, P["hg"])
            qg = 16
        xa = xattn_fwd(u, qg, kv)
        s["m"] = jnp.concatenate([tok, xa], axis=0)
        s["z2"], X2, X2b = contract_ln(s["m"], wout, X1, P["ln_g"][l][1], P["ln_b"][l][1], 1.0, "mixer_out_ln")
        s["x2b"] = X2b
        s["wb"] = weights.ffn(l, 1, X2b)
        s["g1b"], s["u1b"], s["hb"], s["z3"], X, Xb = ffn_fwd(X2b, X2, *s["wb"], P["ln_g"][l][2], P["ln_b"][l][2])
        after = Xb
        saved.append(s)

    loss, dX = loss_grad(X, tgt)

    G = {"ln_g": [[None] * 3 for _ in range(DEPTH)], "ln_b": [[None] * 3 for _ in range(DEPTH)]}
    pin = [jnp.zeros((1, 1), F32)]

    def ffn_backward(l, i, dX, z, xinb, g1, u1, h, w):
        k = 2 * i
        dz, dyb, G["ln_g"][l][k], G["ln_b"][l][k] = ln_bwd(dX, z, P["ln_g"][l][k] + pin[0], 0.5, "ffn_ln_bwd")
        dgb, dub, dx = ffn_bwd(dyb, dz, w[2], w[0], w[1], g1, u1)
        grads = {("wd", l, i): wgrad(h, dyb, BF16, "wgrad_down"), ("wg", l, i): wgrad(dgb, xinb, BF16, "wgrad_gate"),
                 ("wu", l, i): wgrad(dub, xinb, BF16, "wgrad_up")}
        return dx, grads

    for l in reversed(range(DEPTH)):
        s = saved[l]
        win, wkv, wout = s["wm"]
        dX, grads = ffn_backward(l, 1, dX, s["z3"], s["x2b"], s["g1b"], s["u1b"], s["hb"], s["wb"])
        dz2, dz2b, G["ln_g"][l][1], G["ln_b"][l][1] = ln_bwd(dX, s["z2"], P["ln_g"][l][1], 1.0, "mixer_ln_bwd")
        dm = proj_t(dz2b, wout, "mixer_out_bwd")
        grads[("wout", l)] = wgrad(s["m"], dz2b, BF16, "wgrad_out")
        u, kv = s["u"], s["kv"]
        if l % 2 == 0:
            db, dc, dxi, G["convw"] = conv_mixer_bwd(u, P["convw"], dm)
            dq, dkv = xattn_bwd(u, 9, kv, dm, 3)
            du = jnp.concatenate([db, dc, dxi, dq], axis=0)
        else:
            dh, do, G["hg"] = head_norm_bwd(s["hm"], u, P["hg"], dm)
            dqk, dv, dgate, G["bg"] = mlstm_bwd(s["qk"], u, P["bg"], s["cst"], s["mst"], dh)
            duqk, G["qkw"] = qk_conv_bwd(u, P["qkw"], dqk)
            dq, dkv = xattn_bwd(u, 16, kv, dm, 4)
            du = jnp.concatenate([duqk, dv, do, dq, dgate], axis=0)
        grads[("win", l)] = wgrad(s["x1b"], du, BF16, "wgrad_in")
        grads[("wkv", l)] = wgrad(memb, dkv.astype(BF16), BF16, "wgrad_kv")
        dX = contract_t(du, win, dz2, "mixer_in_bwd")
        pin[0] = sink.push(grads)[0:1, 0:1]
        dX, grads = ffn_backward(l, 0, dX, s["z1"], s["x0b"], s["g1a"], s["u1a"], s["ha"], s["wa"])
        pin[0] = sink.push(grads)[0:1, 0:1]
    sink.flush()
    return loss, dX, G


def kernel(x, mem, ln_g, ln_b, ffn_w_gate, ffn_w_up, ffn_w_down, w_kv_mem, w_out, w_in_conv, conv_w, w_in_mlstm, b_gates, qk_conv_w, head_norm_g, loss_target, m_ln_g, m_ln_b, m_ffn_w_gate, m_ffn_w_up, m_ffn_w_down, m_w_kv_mem, m_w_out, m_w_in_conv, m_conv_w, m_w_in_mlstm, m_b_gates, m_qk_conv_w, m_head_norm_g, v_ln_g, v_ln_b, v_ffn_w_gate, v_ffn_w_up, v_ffn_w_down, v_w_kv_mem, v_w_out, v_w_in_conv, v_conv_w, v_w_in_mlstm, v_b_gates, v_qk_conv_w, v_head_norm_g):
    cx, cy = lax.axis_index("x"), lax.axis_index("y")
    chip = 2 * cx + cy

    srcs = {}
    for l in range(DEPTH):
        for i in range(2):
            srcs[("wg", l, i)] = jnp.swapaxes(ffn_w_gate[l, i], 0, 1).astype(BF16)
            srcs[("wu", l, i)] = jnp.swapaxes(ffn_w_up[l, i], 0, 1).astype(BF16)
            srcs[("wd", l, i)] = ffn_w_down[l, i].astype(BF16)
        srcs[("wkv", l)] = w_kv_mem[l].astype(BF16)
        srcs[("wout", l)] = w_out[l].astype(BF16)
    srcs[("win", 0)] = w_in_conv[0].astype(BF16)
    srcs[("win", 1)] = w_in_mlstm[0].astype(BF16)
    ffn_keys = lambda l, i: [("wg", l, i), ("wu", l, i), ("wd", l, i)]
    mixer_keys = lambda l: [("win", l), ("wkv", l), ("wout", l)]
    groups = [ffn_keys(0, 0), mixer_keys(0) + mixer_keys(1), ffn_keys(0, 1), ffn_keys(1, 0), ffn_keys(1, 1)]
    gathered = _Gathered(srcs, groups, jnp.reshape(chip, (1,)).astype(jnp.int32))

    def section(a, width):
        a = a.reshape(-1, a.shape[-1])
        return jnp.pad(a, ((0, SECTION - a.shape[0]), (0, width - a.shape[1])))

    small = jnp.concatenate([section(a, SMALL_IN_COLS) for a in (ln_g, ln_b, conv_w, qk_conv_w)], axis=0)
    smalls = small_allgather(small, reduce=False)[0::2]
    ln_g_full = _chips_to_cols(smalls[:, 0:6, 0:256]).reshape(DEPTH, 3, 1, D_MODEL)
    ln_b_full = _chips_to_cols(smalls[:, 8:14, 0:256]).reshape(DEPTH, 3, 1, D_MODEL)
    conv_w_full = _chips_to_cols(smalls[:, 16:19, 0:192])
    qk_w_full = _chips_to_cols(smalls[:, 24:28, 0:384])

    P = {"ln_g": ln_g_full, "ln_b": ln_b_full, "convw": _taps_to_groups(conv_w_full, GROUP),
         "qkw": _taps_to_groups(qk_w_full, ML_HEAD_DIM), "bg": _pad_last(b_gates, GROUP),
         "hg": _pad_last(head_norm_g[0], GROUP)[:, None, :]}

    weights = {"ln_g": ln_g, "ln_b": ln_b, "ffn_w_gate": ffn_w_gate, "ffn_w_up": ffn_w_up, "ffn_w_down": ffn_w_down,
               "w_kv_mem": w_kv_mem, "w_out": w_out, "w_in_conv": w_in_conv, "conv_w": conv_w, "w_in_mlstm": w_in_mlstm,
               "b_gates": b_gates, "qk_conv_w": qk_conv_w, "head_norm_g": head_norm_g}
    ms = {"ln_g": m_ln_g, "ln_b": m_ln_b, "ffn_w_gate": m_ffn_w_gate, "ffn_w_up": m_ffn_w_up, "ffn_w_down": m_ffn_w_down,
          "w_kv_mem": m_w_kv_mem, "w_out": m_w_out, "w_in_conv": m_w_in_conv, "conv_w": m_conv_w, "w_in_mlstm": m_w_in_mlstm,
          "b_gates": m_b_gates, "qk_conv_w": m_qk_conv_w, "head_norm_g": m_head_norm_g}
    vs = {"ln_g": v_ln_g, "ln_b": v_ln_b, "ffn_w_gate": v_ffn_w_gate, "ffn_w_up": v_ffn_w_up, "ffn_w_down": v_ffn_w_down,
          "w_kv_mem": v_w_kv_mem, "w_out": v_w_out, "w_in_conv": v_w_in_conv, "conv_w": v_conv_w, "w_in_mlstm": v_w_in_mlstm,
          "b_gates": v_b_gates, "qk_conv_w": v_qk_conv_w, "head_norm_g": v_head_norm_g}
    names = list(weights)
    owner = {"wg": ("ffn_w_gate", True), "wu": ("ffn_w_up", True), "wd": ("ffn_w_down", False), "wkv": ("w_kv_mem", False),
             "wout": ("w_out", False), "win": None}
    updated = {}

    def apply(key, g, after):
        name, transposed = owner[key[0]] or (("w_in_conv", "w_in_mlstm")[key[1]], False)
        idx = (0,) if key[0] == "win" else tuple(key[1:])
        view = (lambda a: jnp.swapaxes(a, -1, -2)) if transposed else (lambda a: a)
        updated[name], token = adamw_into(view(weights[name]), view(ms[name]), view(vs[name]), g, updated.get(name), idx, after,
                                          "adamw_" + name + "_" + "_".join(map(str, idx)))
        return token

    sink = _GradSink(apply)
    loss, grad_x, G = _local_step(x[0], mem[0], loss_target[0], P, gathered, sink)

    dln_g = jnp.concatenate([G["ln_g"][l][k] for l in range(DEPTH) for k in range(3)], axis=0)
    dln_b = jnp.concatenate([G["ln_b"][l][k] for l in range(DEPTH) for k in range(3)], axis=0)
    lane = lax.broadcasted_iota(jnp.int32, (1, GROUP), 1)
    misc = jnp.where(lane < 8, G["bg"], 0.0) + jnp.where(lane == 8, loss, 0.0) + sink.done[0:1, 0:1]
    parts = (dln_g, dln_b, _groups_to_taps(G["convw"], 3, GROUP), misc, _groups_to_taps(G["qkw"], 4, ML_HEAD_DIM),
             G["hg"][:, 0, :ML_HEAD_DIM])
    tot = small_allgather(jnp.concatenate([section(a, SMALL_OUT_COLS) for a in parts], axis=0), reduce=True)
    loss_total = tot[24, 8]

    small_grads = {
        "ln_g": lax.dynamic_slice(tot[0:6, 0:D_MODEL], (0, chip * 256), (6, 256)).reshape(DEPTH, 3, 256),
        "ln_b": lax.dynamic_slice(tot[8:14, 0:D_MODEL], (0, chip * 256), (6, 256)).reshape(DEPTH, 3, 256),
        "conv_w": lax.dynamic_slice(tot[16:19, 0:D_TOK], (0, chip * 192), (3, 192))[None],
        "b_gates": tot[24:25, 0:8],
        "qk_conv_w": lax.dynamic_slice(tot[32:36, 0:2 * D_TOK], (0, chip * 384), (4, 384))[None],
        "head_norm_g": tot[40:44, 0:ML_HEAD_DIM][None],
    }
    grads, deltas, new_m, new_v = [], [], [], []
    for nme in names:
        if nme in updated:
            back = (lambda a: jnp.swapaxes(a, -1, -2)) if nme in ("ffn_w_gate", "ffn_w_up") else (lambda a: a)
            g, d, nm, nv = (back(a) for a in updated[nme])
        else:
            w, g = weights[nme], small_grads[nme]
            two = (math.prod(w.shape[:-1]), w.shape[-1])
            d, nm, nv = (a.reshape(w.shape) for a in adamw(w.reshape(two), g.reshape(two), ms[nme].reshape(two),
                                                           vs[nme].reshape(two), "adamw_" + nme))
        grads.append(g)
        deltas.append(d)
        new_m.append(nm)
        new_v.append(nv)
    return (loss_total, grad_x[None], *grads, *deltas, *new_m, *new_v)
```

```python
import functools
import math

import jax
import jax.numpy as jnp
from jax import lax
from jax.experimental import pallas as pl
from jax.experimental.pallas import tpu as pltpu

F32 = jnp.float32
BF16 = jnp.bfloat16
SDS = jax.ShapeDtypeStruct

D_MODEL = 1024
DEPTH = 2
N_MEM = 256
XA_HEADS = 4
XA_HEAD_DIM = 64
D_XA = 256
D_TOK = 768
ML_HEADS = 4
ML_HEAD_DIM = 192
ML_CHUNK = 64
D_FF = 2816
LN_EPS = 1e-5
ALPHA = (2.0 * DEPTH) ** 0.25
N_CHIPS = 4
N_DEV = 8
FF_SHARD = D_FF // N_CHIPS
GROUP = 256
NEG = -1e30

ADAM_LR = 0.001
ADAM_B1 = 0.9
ADAM_B2 = 0.999
ADAM_EPS = 1e-08
ADAM_WD = 0.01
ADAM_STEP = 10

VMEM_LIMIT = 56 * 1024 * 1024

NN = ((1,), (0,))
NT = ((1,), (1,))
TN = ((0,), (0,))
MESH = pl.DeviceIdType.MESH


def _dot(a, b, dims):
    return lax.dot_general(a, b, (dims, ((), ())), preferred_element_type=F32)


def _bdot(a, b, ca, cb):
    dims = (((ca,), (cb,)), ((0,), (0,)))
    ah, bh = a.astype(BF16), b.astype(BF16)
    al, bl = (a - ah.astype(F32)).astype(BF16), (b - bh.astype(F32)).astype(BF16)
    dot = functools.partial(lax.dot_general, dimension_numbers=dims, preferred_element_type=F32)
    return dot(ah, bh) + dot(al, bh) + dot(ah, bl)


def _bdot1(a, b, ca, cb):
    return lax.dot_general(a.astype(BF16), b.astype(BF16), (((ca,), (cb,)), ((0,), (0,))), preferred_element_type=F32)


def _sigmoid(x):
    return 1.0 / (1.0 + jnp.exp(-x))


def _params(sem, vmem=VMEM_LIMIT):
    return pltpu.CompilerParams(dimension_semantics=sem, vmem_limit_bytes=vmem)


def _tile(n, want):
    t = min(n, want)
    assert n % t == 0, (n, t)
    return t


def _layer_norm(z, gamma, beta):
    mu = jnp.mean(z, axis=-1, keepdims=True)
    zc = z - mu
    var = jnp.mean(zc * zc, axis=-1, keepdims=True)
    return zc * lax.rsqrt(var + LN_EPS) * gamma + beta


def _column_halves(n):
    mid = -(-n // (2 * 128)) * 128
    return ((0, mid), (mid, n))


def _resident(shape):
    return pl.BlockSpec(shape, lambda *_: (0,) * len(shape), pipeline_mode=pl.Buffered(1))


def _group_block(G, want):
    return max(d for d in range(1, max(1, min(G, want)) + 1) if G % d == 0)


def ffn_fwd(xb, x, wg, wu, wd, gamma, beta):
    S, K = xb.shape
    G, N, _ = wg.shape
    ts = _tile(S, 512)

    def body(xb_ref, x_ref, wg_ref, wu_ref, wd_ref, gm_ref, bt_ref, g_ref, u_ref, h_ref, z_ref, xn_ref, xnb_ref):
        j = pl.program_id(1)
        xv = xb_ref[...]
        g = _dot(xv, wg_ref[j], NT)
        u = _dot(xv, wu_ref[j], NT)
        h = (g * _sigmoid(g) * u).astype(BF16)
        g_ref[0] = g.astype(BF16)
        u_ref[0] = u.astype(BF16)
        h_ref[0] = h
        y = _dot(h, wd_ref[j], NN)

        @pl.when(j == 0)
        def _():
            z_ref[...] = y

        @pl.when(j > 0)
        def _():
            z_ref[...] += y

        @pl.when(j == G - 1)
        def _():
            z = ALPHA * x_ref[...] + 0.5 * z_ref[...]
            xn = _layer_norm(z, gm_ref[...], bt_ref[...])
            z_ref[...] = z
            xn_ref[...] = xn
            xnb_ref[...] = xn.astype(BF16)

    row = pl.BlockSpec((ts, K), lambda s, j: (s, 0))
    vec = pl.BlockSpec((1, K), lambda s, j: (0, 0))
    wspec = _resident((G, N, K))
    ospec = pl.BlockSpec((1, ts, N), lambda s, j: (j, s, 0))
    return pl.pallas_call(
        body, name="ffn_fwd", grid=(S // ts, G),
        in_specs=[row, row, wspec, wspec, wspec, vec, vec],
        out_specs=[ospec, ospec, ospec, row, row, row],
        out_shape=[SDS((G, S, N), BF16), SDS((G, S, N), BF16), SDS((G, S, N), BF16),
                   SDS((S, K), F32), SDS((S, K), F32), SDS((S, K), BF16)],
        compiler_params=_params(("parallel", "arbitrary")),
    )(xb, x, wg, wu, wd, gamma, beta)


def proj(xb, w, name):
    S, K = xb.shape
    G, _, N = w.shape
    ts = _tile(S, 1024)
    gb = _group_block(G, 6)

    def body(x_ref, w_ref, y_ref):
        xv = x_ref[...]
        for j in range(gb):
            y_ref[j] = _dot(xv, w_ref[j], NN)

    return pl.pallas_call(
        body, name=name, grid=(S // ts, G // gb),
        in_specs=[pl.BlockSpec((ts, K), lambda s, g: (s, 0)), pl.BlockSpec((gb, K, N), lambda s, g: (g, 0, 0))],
        out_specs=pl.BlockSpec((gb, ts, N), lambda s, g: (g, s, 0)),
        out_shape=SDS((G, S, N), F32),
        compiler_params=_params(("parallel", "parallel")),
    )(xb, w)


def contract_ln(a, w, xres, gamma, beta, scale, name):
    G, S, Kg = a.shape
    N = w.shape[2]
    ts = _tile(S, 512)

    def body(a_ref, w_ref, x_ref, g_ref, b_ref, z_ref, xn_ref, xb_ref):
        acc = _dot(a_ref[0], w_ref[0], NN)
        for j in range(1, G):
            acc = acc + _dot(a_ref[j], w_ref[j], NN)
        z = ALPHA * x_ref[...] + scale * acc
        xn = _layer_norm(z, g_ref[...], b_ref[...])
        z_ref[...] = z
        xn_ref[...] = xn
        xb_ref[...] = xn.astype(BF16)

    row = pl.BlockSpec((ts, N), lambda s: (s, 0))
    vec = pl.BlockSpec((1, N), lambda s: (0, 0))
    return pl.pallas_call(
        body, name=name, grid=(S // ts,),
        in_specs=[pl.BlockSpec((G, ts, Kg), lambda s: (0, s, 0)), pl.BlockSpec((G, Kg, N), lambda s: (0, 0, 0)), row, vec, vec],
        out_specs=[row, row, row],
        out_shape=[SDS((S, N), F32), SDS((S, N), F32), SDS((S, N), BF16)],
        compiler_params=_params(("parallel",)),
    )(a, w, xres, gamma, beta)


def ln_bwd(dx, z, gamma, out_scale, name):
    S, N = dx.shape
    ts = _tile(S, 512)

    def body(dx_ref, z_ref, g_ref, dz_ref, dzb_ref, dg_ref, db_ref):
        @pl.when(pl.program_id(0) == 0)
        def _():
            dg_ref[...] = jnp.zeros_like(dg_ref)
            db_ref[...] = jnp.zeros_like(db_ref)

        z = z_ref[...]
        mu = jnp.mean(z, axis=-1, keepdims=True)
        zc = z - mu
        var = jnp.mean(zc * zc, axis=-1, keepdims=True)
        rstd = lax.rsqrt(var + LN_EPS)
        xhat = zc * rstd
        dxv = dx_ref[...]
        dg_ref[...] += jnp.sum(dxv * xhat, axis=0, keepdims=True)
        db_ref[...] += jnp.sum(dxv, axis=0, keepdims=True)
        dxh = dxv * g_ref[...]
        m1 = jnp.mean(dxh, axis=-1, keepdims=True)
        m2 = jnp.mean(dxh * xhat, axis=-1, keepdims=True)
        dz = rstd * (dxh - m1 - xhat * m2)
        dz_ref[...] = dz
        dzb_ref[...] = (out_scale * dz).astype(BF16)

    row = pl.BlockSpec((ts, N), lambda s: (s, 0))
    vec = pl.BlockSpec((1, N), lambda s: (0, 0))
    return pl.pallas_call(
        body, name=name, grid=(S // ts,),
        in_specs=[row, row, vec],
        out_specs=[row, row, vec, vec],
        out_shape=[SDS((S, N), F32), SDS((S, N), BF16), SDS((1, N), F32), SDS((1, N), F32)],
        compiler_params=_params(("arbitrary",)),
    )(dx, z, gamma)


def _layer_norm_bwd(dx, z, gamma):
    mu = jnp.mean(z, axis=-1, keepdims=True)
    zc = z - mu
    var = jnp.mean(zc * zc, axis=-1, keepdims=True)
    rstd = lax.rsqrt(var + LN_EPS)
    xhat = zc * rstd
    dxh = dx * gamma
    m1 = jnp.mean(dxh, axis=-1, keepdims=True)
    m2 = jnp.mean(dxh * xhat, axis=-1, keepdims=True)
    return rstd * (dxh - m1 - xhat * m2), jnp.sum(dx * xhat, axis=0, keepdims=True), jnp.sum(dx, axis=0, keepdims=True)


def ffn_bwd(dxn, z, gamma, wd, wg, wu, g1, u1):
    S, K = dxn.shape
    G, N, _ = wd.shape
    ts = _tile(S, 512)

    def body(dxn_ref, z_ref, gm_ref, wd_ref, wg_ref, wu_ref, g_ref, u_ref, dg_ref, du_ref, dx_ref, dy_ref, dgm_ref, dbt_ref):
        s, j = pl.program_id(0), pl.program_id(1)

        @pl.when((s == 0) & (j == 0))
        def _():
            dgm_ref[...] = jnp.zeros_like(dgm_ref)
            dbt_ref[...] = jnp.zeros_like(dbt_ref)

        @pl.when(j == 0)
        def _():
            dz, dgm, dbt = _layer_norm_bwd(dxn_ref[...], z_ref[...], gm_ref[...])
            dgm_ref[...] += dgm
            dbt_ref[...] += dbt
            dx_ref[...] = ALPHA * dz
            dy_ref[...] = (0.5 * dz).astype(BF16)

        dy = dy_ref[...]
        part = None
        for a, b in _column_halves(N):
            dh = _dot(dy, wd_ref[j, a:b, :], NT)
            g = g_ref[0, :, a:b].astype(F32)
            sig = _sigmoid(g)
            dg = (dh * u_ref[0, :, a:b].astype(F32) * (sig * (1.0 + g * (1.0 - sig)))).astype(BF16)
            du = (dh * (g * sig)).astype(BF16)
            dg_ref[0, :, a:b] = dg
            du_ref[0, :, a:b] = du
            p = _dot(dg, wg_ref[j, a:b, :], NN) + _dot(du, wu_ref[j, a:b, :], NN)
            part = p if part is None else part + p
        dx_ref[...] += part

    row = pl.BlockSpec((ts, K), lambda s, j: (s, 0))
    vec = pl.BlockSpec((1, K), lambda s, j: (0, 0))
    gspec = pl.BlockSpec((1, ts, N), lambda s, j: (j, s, 0))
    wspec = _resident((G, N, K))
    return pl.pallas_call(
        body, name="ffn_bwd", grid=(S // ts, G),
        in_specs=[row, row, vec, wspec, wspec, wspec, gspec, gspec],
        out_specs=[gspec, gspec, row, row, vec, vec],
        out_shape=[SDS((G, S, N), BF16), SDS((G, S, N), BF16), SDS((S, K), F32), SDS((S, K), BF16),
                   SDS((1, K), F32), SDS((1, K), F32)],
        compiler_params=_params(("arbitrary", "arbitrary")),
    )(dxn, z, gamma, wd, wg, wu, g1, u1)


def proj_t(dyb, w, name):
    S, N = dyb.shape
    G, Kg, _ = w.shape
    ts = _tile(S, 1024)

    def body(dy_ref, w_ref, da_ref):
        dy = dy_ref[...]
        for j in range(G):
            da_ref[j] = _dot(dy, w_ref[j], NT)

    return pl.pallas_call(
        body, name=name, grid=(S // ts,),
        in_specs=[pl.BlockSpec((ts, N), lambda s: (s, 0)), pl.BlockSpec((G, Kg, N), lambda s: (0, 0, 0))],
        out_specs=pl.BlockSpec((G, ts, Kg), lambda s: (0, s, 0)),
        out_shape=SDS((G, S, Kg), F32),
        compiler_params=_params(("parallel",)),
    )(dyb, w)


def contract_t(da, w, res, name):
    G, S, Ng = da.shape
    K = w.shape[1]
    ts = _tile(S, 512)
    gb = _group_block(G, 6)

    def body(da_ref, w_ref, r_ref, o_ref):
        g = pl.program_id(1)
        part = _dot(da_ref[0], w_ref[0], NT)
        for j in range(1, gb):
            part = part + _dot(da_ref[j], w_ref[j], NT)

        @pl.when(g == 0)
        def _():
            o_ref[...] = ALPHA * r_ref[...] + part

        @pl.when(g > 0)
        def _():
            o_ref[...] += part

    row = pl.BlockSpec((ts, K), lambda s, g: (s, 0))
    return pl.pallas_call(
        body, name=name, grid=(S // ts, G // gb),
        in_specs=[pl.BlockSpec((gb, ts, Ng), lambda s, g: (g, s, 0)), pl.BlockSpec((gb, K, Ng), lambda s, g: (g, 0, 0)), row],
        out_specs=row,
        out_shape=SDS((S, K), F32),
        compiler_params=_params(("parallel", "arbitrary")),
    )(da, w, res)


WGRAD_ACC_ELEMS = 6 * 1024 * 256


def wgrad(a, b, out_dtype, name):
    ga, gb = a.ndim == 3, b.ndim == 3
    G = a.shape[0] if ga else b.shape[0]
    S, K = a.shape[-2:]
    N = b.shape[-1]
    ts = _tile(S, 1024)
    ns = S // ts
    ng = _group_block(G, WGRAD_ACC_ELEMS // (K * N))

    def body(a_ref, b_ref, o_ref, acc):
        s = pl.program_id(1)

        @pl.when(s == 0)
        def _():
            acc[...] = jnp.zeros_like(acc)

        for j in range(ng):
            acc[j] += _dot(a_ref[j] if ga else a_ref[...], b_ref[j] if gb else b_ref[...], TN)

        @pl.when(s == ns - 1)
        def _():
            o_ref[...] = acc[...].astype(out_dtype)

    aspec = pl.BlockSpec((ng, ts, K), lambda g, s: (g, s, 0)) if ga else pl.BlockSpec((ts, K), lambda g, s: (s, 0))
    bspec = pl.BlockSpec((ng, ts, N), lambda g, s: (g, s, 0)) if gb else pl.BlockSpec((ts, N), lambda g, s: (s, 0))
    return pl.pallas_call(
        body, name=name, grid=(G // ng, ns),
        in_specs=[aspec, bspec],
        out_specs=pl.BlockSpec((ng, K, N), lambda g, s: (g, 0, 0)),
        out_shape=SDS((G, K, N), out_dtype),
        scratch_shapes=[pltpu.VMEM((ng, K, N), F32)],
        compiler_params=_params(("parallel", "arbitrary")),
    )(a, b)


def loss_grad(xn, tgt):
    S, N = xn.shape
    ts = _tile(S, 512)

    def body(x_ref, t_ref, l_ref, dx_ref):
        @pl.when(pl.program_id(0) == 0)
        def _():
            l_ref[...] = jnp.zeros_like(l_ref)

        e = x_ref[...] - t_ref[...]
        dx_ref[...] = e * (1.0 / N)
        l_ref[...] += 0.5 * jnp.sum(jnp.mean(e * e, axis=-1, keepdims=True), axis=0, keepdims=True)

    row = pl.BlockSpec((ts, N), lambda s: (s, 0))
    return pl.pallas_call(
        body, name="loss_grad", grid=(S // ts,),
        in_specs=[row, row],
        out_specs=[pl.BlockSpec((1, 1), lambda s: (0, 0)), row],
        out_shape=[SDS((1, 1), F32), SDS((S, N), F32)],
        compiler_params=_params(("arbitrary",)),
    )(xn, tgt)


def _shift_down(x, k):
    if k == 0:
        return x
    rows = lax.broadcasted_iota(jnp.int32, x.shape, 0)
    return jnp.where(rows >= k, pltpu.roll(x, k, 0), 0.0)


def _shift_up(x, k):
    if k == 0:
        return x
    n = x.shape[0]
    rows = lax.broadcasted_iota(jnp.int32, x.shape, 0)
    return jnp.where(rows < n - k, pltpu.roll(x, n - k, 0), 0.0)


LANES = 128


def conv_mixer_fwd(u, cw):
    _, S, _ = u.shape
    nh = GROUP // LANES

    def body(b_ref, c_ref, x_ref, w_ref, o_ref):
        p = c_ref[0] * x_ref[0]
        w = w_ref[0]
        conv = w[2:3] * p + w[1:2] * _shift_down(p, 1) + w[0:1] * _shift_down(p, 2)
        o_ref[0] = (b_ref[0] * conv).astype(BF16)

    def uspec(off):
        return pl.BlockSpec((1, S, LANES), lambda g, h: (g + off, 0, h))

    return pl.pallas_call(
        body, name="conv_mixer_fwd", grid=(3, nh),
        in_specs=[uspec(0), uspec(3), uspec(6), pl.BlockSpec((1, 8, LANES), lambda g, h: (g, 0, h))],
        out_specs=pl.BlockSpec((1, S, LANES), lambda g, h: (g, 0, h)),
        out_shape=SDS((3, S, GROUP), BF16),
        compiler_params=_params(("parallel", "parallel")),
    )(u, u, u, cw)


def conv_mixer_bwd(u, cw, dm):
    _, S, _ = u.shape
    nh = GROUP // LANES

    def body(b_ref, c_ref, x_ref, w_ref, d_ref, db_ref, dc_ref, dx_ref, dw_ref):
        cg, xi = c_ref[0], x_ref[0]
        p = cg * xi
        p1, p2 = _shift_down(p, 1), _shift_down(p, 2)
        w = w_ref[0]
        conv = w[2:3] * p + w[1:2] * p1 + w[0:1] * p2
        dt = d_ref[0]
        db_ref[0] = (dt * conv).astype(BF16)
        dcv = dt * b_ref[0]
        dp = w[2:3] * dcv + w[1:2] * _shift_up(dcv, 1) + w[0:1] * _shift_up(dcv, 2)
        dc_ref[0] = (dp * xi).astype(BF16)
        dx_ref[0] = (dp * cg).astype(BF16)
        dw = jnp.concatenate([jnp.sum(dcv * p2, axis=0, keepdims=True), jnp.sum(dcv * p1, axis=0, keepdims=True),
                              jnp.sum(dcv * p, axis=0, keepdims=True), jnp.zeros((5, LANES), F32)], axis=0)
        dw_ref[0] = dw

    def uspec(off):
        return pl.BlockSpec((1, S, LANES), lambda g, h: (g + off, 0, h))

    ospec = pl.BlockSpec((1, S, LANES), lambda g, h: (g, 0, h))
    wspec = pl.BlockSpec((1, 8, LANES), lambda g, h: (g, 0, h))
    return pl.pallas_call(
        body, name="conv_mixer_bwd", grid=(3, nh),
        in_specs=[uspec(0), uspec(3), uspec(6), wspec, ospec],
        out_specs=[ospec, ospec, ospec, wspec],
        out_shape=[SDS((3, S, GROUP), BF16)] * 3 + [SDS((3, 8, GROUP), F32)],
        compiler_params=_params(("parallel", "parallel")),
    )(u, u, u, cw, dm)


def qk_conv_fwd(u, qw):
    _, S, _ = u.shape
    nh = GROUP // LANES

    def body(u_ref, w_ref, o_ref):
        x = u_ref[0]
        w = w_ref[0]
        pre = w[3:4] * x + w[2:3] * _shift_down(x, 1) + w[1:2] * _shift_down(x, 2) + w[0:1] * _shift_down(x, 3)
        o_ref[0] = pre * _sigmoid(pre)

    spec = pl.BlockSpec((1, S, LANES), lambda g, h: (g, 0, h))
    return pl.pallas_call(
        body, name="qk_conv_fwd", grid=(8, nh),
        in_specs=[spec, pl.BlockSpec((1, 8, LANES), lambda g, h: (g, 0, h))],
        out_specs=spec,
        out_shape=SDS((8, S, GROUP), F32),
        compiler_params=_params(("parallel", "parallel")),
    )(u, qw)


def qk_conv_bwd(u, qw, dqk):
    _, S, _ = u.shape
    nh = GROUP // LANES

    def body(u_ref, w_ref, d_ref, du_ref, dw_ref):
        x = u_ref[0]
        w = w_ref[0]
        x1, x2, x3 = _shift_down(x, 1), _shift_down(x, 2), _shift_down(x, 3)
        pre = w[3:4] * x + w[2:3] * x1 + w[1:2] * x2 + w[0:1] * x3
        sig = _sigmoid(pre)
        dpre = d_ref[0] * (sig * (1.0 + pre * (1.0 - sig)))
        du = w[3:4] * dpre + w[2:3] * _shift_up(dpre, 1) + w[1:2] * _shift_up(dpre, 2) + w[0:1] * _shift_up(dpre, 3)
        du_ref[0] = du.astype(BF16)
        dw = jnp.concatenate([jnp.sum(dpre * x3, axis=0, keepdims=True), jnp.sum(dpre * x2, axis=0, keepdims=True),
                              jnp.sum(dpre * x1, axis=0, keepdims=True), jnp.sum(dpre * x, axis=0, keepdims=True),
                              jnp.zeros((4, LANES), F32)], axis=0)
        dw_ref[0] = dw

    spec = pl.BlockSpec((1, S, LANES), lambda g, h: (g, 0, h))
    wspec = pl.BlockSpec((1, 8, LANES), lambda g, h: (g, 0, h))
    return pl.pallas_call(
        body, name="qk_conv_bwd", grid=(8, nh),
        in_specs=[spec, wspec, spec],
        out_specs=[spec, wspec],
        out_shape=[SDS((8, S, GROUP), BF16), SDS((8, 8, GROUP), F32)],
        compiler_params=_params(("parallel", "parallel")),
    )(u, qw, dqk)


def _head_masks():
    lane = lax.broadcasted_iota(jnp.int32, (1, D_XA), 1)
    return [(lane >= h * XA_HEAD_DIM) & (lane < (h + 1) * XA_HEAD_DIM) for h in range(XA_HEADS)]


def xattn_fwd(u, qg, kv):
    _, S, _ = u.shape
    ts = _tile(S, 512)
    scale = XA_HEAD_DIM ** -0.5

    def body(q_ref, kv_ref, o_ref):
        q = q_ref[0]
        k = kv_ref[0].astype(BF16)
        v = kv_ref[1]
        o = jnp.zeros((ts, D_XA), F32)
        for m in _head_masks():
            s = _dot(jnp.where(m, q, 0.0).astype(BF16), k, NT) * scale
            s = s - jnp.max(s, axis=-1, keepdims=True)
            e = jnp.exp(s)
            p = e / jnp.sum(e, axis=-1, keepdims=True)
            o = o + _dot(p.astype(BF16), jnp.where(m, v, 0.0).astype(BF16), NN)
        o_ref[0] = o.astype(BF16)

    return pl.pallas_call(
        body, name="xattn_fwd", grid=(S // ts,),
        in_specs=[pl.BlockSpec((1, ts, GROUP), lambda s: (qg, s, 0)), pl.BlockSpec((2, N_MEM, GROUP), lambda s: (0, 0, 0))],
        out_specs=pl.BlockSpec((1, ts, GROUP), lambda s: (0, s, 0)),
        out_shape=SDS((1, S, GROUP), BF16),
        compiler_params=_params(("parallel",)),
    )(u, kv)


def xattn_bwd(u, qg, kv, dm, dg):
    _, S, _ = u.shape
    ts = _tile(S, 512)
    scale = XA_HEAD_DIM ** -0.5

    def body(q_ref, kv_ref, do_ref, dq_ref, dkv_ref):
        @pl.when(pl.program_id(0) == 0)
        def _():
            dkv_ref[...] = jnp.zeros_like(dkv_ref)

        q = q_ref[0]
        k = kv_ref[0]
        v = kv_ref[1]
        kb = k.astype(BF16)
        do = do_ref[0]
        dq = jnp.zeros((ts, D_XA), F32)
        dk = jnp.zeros((N_MEM, D_XA), F32)
        dv = jnp.zeros((N_MEM, D_XA), F32)
        for m in _head_masks():
            qm = jnp.where(m, q, 0.0).astype(BF16)
            s = _dot(qm, kb, NT) * scale
            s = s - jnp.max(s, axis=-1, keepdims=True)
            e = jnp.exp(s)
            p = e / jnp.sum(e, axis=-1, keepdims=True)
            dom = jnp.where(m, do, 0.0).astype(BF16)
            dp = _dot(dom, jnp.where(m, v, 0.0).astype(BF16), NT)
            ds = (p * (dp - jnp.sum(dp * p, axis=-1, keepdims=True)) * scale).astype(BF16)
            dq = dq + _dot(ds, jnp.where(m, k, 0.0).astype(BF16), NN)
            dk = dk + _dot(ds, qm, TN)
            dv = dv + _dot(p.astype(BF16), dom, TN)
        dq_ref[0] = dq.astype(BF16)
        dkv_ref[0] += dk
        dkv_ref[1] += dv

    return pl.pallas_call(
        body, name="xattn_bwd", grid=(S // ts,),
        in_specs=[pl.BlockSpec((1, ts, GROUP), lambda s: (qg, s, 0)), pl.BlockSpec((2, N_MEM, GROUP), lambda s: (0, 0, 0)),
                  pl.BlockSpec((1, ts, GROUP), lambda s: (dg, s, 0))],
        out_specs=[pl.BlockSpec((1, ts, GROUP), lambda s: (0, s, 0)), pl.BlockSpec((2, N_MEM, GROUP), lambda s: (0, 0, 0))],
        out_shape=[SDS((1, S, GROUP), BF16), SDS((2, N_MEM, GROUP), F32)],
        compiler_params=_params(("arbitrary",)),
    )(u, kv, dm)


ML_BLOCK_CHUNKS = 4
H4 = ML_HEADS
L = ML_CHUNK
NLANE = ML_HEAD_DIM


def _chunk_consts():
    r = lax.broadcasted_iota(jnp.int32, (1, L, L), 1)
    c = lax.broadcasted_iota(jnp.int32, (1, L, L), 2)
    return r >= c, r <= c, r == c


def _gate_cols(gb):
    lane = lax.broadcasted_iota(jnp.int32, gb.shape, 1)
    li = jnp.stack([jnp.sum(jnp.where(lane == h, gb, 0.0), axis=1, keepdims=True) for h in range(H4)])
    gf = jnp.stack([jnp.sum(jnp.where(lane == H4 + h, gb, 0.0), axis=1, keepdims=True) for h in range(H4)])
    return li, gf


def _log_sigmoid(x):
    return jnp.minimum(x, 0.0) - jnp.log(1.0 + jnp.exp(-jnp.abs(x)))


def _chunk_forward(q, k, v_aug, li_col, lf_col, c_prev, m_prev):
    tri, tri_t, eye = _chunk_consts()
    lf_row = jnp.sum(jnp.where(eye, lf_col, 0.0), axis=1, keepdims=True)
    li_row = jnp.sum(jnp.where(eye, li_col, 0.0), axis=1, keepdims=True)
    bcum_col = jnp.sum(jnp.where(tri, lf_row, 0.0), axis=2, keepdims=True)
    bcum_row = jnp.sum(jnp.where(tri_t, lf_col, 0.0), axis=1, keepdims=True)
    log_d = jnp.where(tri, bcum_col - bcum_row + li_row, NEG)
    log_inter = bcum_col + m_prev
    m_t = jnp.maximum(log_inter, jnp.max(log_d, axis=2, keepdims=True))
    w_intra = jnp.exp(log_d - m_t)
    w_inter = jnp.exp(log_inter - m_t)
    sc = _bdot(q, k, 2, 2) * w_intra
    qc = _bdot1(q, c_prev, 2, 1)
    num = _bdot(sc, v_aug, 2, 1) + w_inter * qc
    lane = lax.broadcasted_iota(jnp.int32, num.shape, 2)
    den = jnp.sum(jnp.where(lane == NLANE, num, 0.0), axis=2, keepdims=True)
    e_m = jnp.exp(-m_t)
    b_last = jnp.sum(lf_row, axis=2, keepdims=True)
    log_w = b_last - bcum_col + li_col
    m_new = jnp.maximum(b_last + m_prev, jnp.max(log_w, axis=1, keepdims=True))
    w_k = jnp.exp(log_w - m_new)
    decay = jnp.exp(b_last + m_prev - m_new)
    return dict(w_intra=w_intra, w_inter=w_inter, sc=sc, qc=qc, num=num, den=den, e_m=e_m, lane=lane,
                w_k=w_k, decay=decay, m_new=m_new)


def mlstm_fwd(qk, u, bg):
    _, S, _ = qk.shape
    nc = S // L
    cb = min(ML_BLOCK_CHUNKS, nc)
    rows = cb * L
    kscale = ML_HEAD_DIM ** -0.5

    def body(qk_ref, v_ref, g_ref, bg_ref, h_ref, cst_ref, mst_ref, c_sc, m_sc):
        @pl.when(pl.program_id(0) == 0)
        def _():
            c_sc[...] = jnp.zeros_like(c_sc)
            m_sc[...] = jnp.zeros_like(m_sc)

        for c in range(cb):
            sl = pl.ds(c * L, L)
            q = qk_ref[0:H4, sl, :]
            k = qk_ref[H4:2 * H4, sl, :] * kscale
            v = v_ref[:, sl, :]
            lane = lax.broadcasted_iota(jnp.int32, v.shape, 2)
            v_aug = jnp.where(lane == NLANE, 1.0, v)
            li_col, gf = _gate_cols(g_ref[0, sl, :] + bg_ref[...])
            lf_col = _log_sigmoid(gf)
            c_prev = c_sc[...]
            m_prev = m_sc[...]
            f = _chunk_forward(q, k, v_aug, li_col, lf_col, c_prev, m_prev)
            r = 1.0 / jnp.maximum(jnp.abs(f["den"]), f["e_m"])
            h_ref[:, sl, :] = jnp.where(lane < NLANE, f["num"] * r, 0.0)
            cst_ref[c] = c_prev
            mst_ref[c] = jnp.broadcast_to(m_prev, (H4, 1, LANES))
            c_sc[...] = f["decay"] * c_prev + _bdot(k * f["w_k"], v_aug, 1, 1)
            m_sc[...] = f["m_new"]

    def hspec(blk):
        return pl.BlockSpec((H4, rows, GROUP), lambda i: (blk, i, 0))

    return pl.pallas_call(
        body, name="mlstm_fwd", grid=(nc // cb,),
        in_specs=[pl.BlockSpec((2 * H4, rows, GROUP), lambda i: (0, i, 0)), hspec(2),
                  pl.BlockSpec((1, rows, GROUP), lambda i: (17, i, 0)), pl.BlockSpec((1, GROUP), lambda i: (0, 0))],
        out_specs=[hspec(0), pl.BlockSpec((cb, H4, GROUP, GROUP), lambda i: (i, 0, 0, 0)),
                   pl.BlockSpec((cb, H4, 1, LANES), lambda i: (i, 0, 0, 0))],
        out_shape=[SDS((H4, S, GROUP), F32), SDS((nc, H4, GROUP, GROUP), F32), SDS((nc, H4, 1, LANES), F32)],
        scratch_shapes=[pltpu.VMEM((H4, GROUP, GROUP), F32), pltpu.VMEM((H4, 1, 1), F32)],
        compiler_params=_params(("arbitrary",)),
    )(qk, u, u, bg)


def mlstm_bwd(qk, u, bg, cst, mst, dh):
    _, S, _ = qk.shape
    nc = S // L
    cb = min(ML_BLOCK_CHUNKS, nc)
    rows = cb * L
    nb = nc // cb
    kscale = ML_HEAD_DIM ** -0.5

    def body(qk_ref, v_ref, g_ref, bg_ref, cst_ref, mst_ref, dh_ref, dqk_ref, dv_ref, dg_ref, dbg_ref, dc_sc):
        @pl.when(pl.program_id(0) == 0)
        def _():
            dc_sc[...] = jnp.zeros_like(dc_sc)
            dbg_ref[...] = jnp.zeros_like(dbg_ref)

        tri, tri_t, eye = _chunk_consts()
        for c in reversed(range(cb)):
            sl = pl.ds(c * L, L)
            q = qk_ref[0:H4, sl, :]
            k = qk_ref[H4:2 * H4, sl, :] * kscale
            v = v_ref[:, sl, :]
            lane = lax.broadcasted_iota(jnp.int32, v.shape, 2)
            v_aug = jnp.where(lane == NLANE, 1.0, v)
            li_col, gf = _gate_cols(g_ref[0, sl, :] + bg_ref[...])
            lf_col = _log_sigmoid(gf)
            c_prev = cst_ref[c]
            m_prev = mst_ref[c][:, :, 0:1]
            f = _chunk_forward(q, k, v_aug, li_col, lf_col, c_prev, m_prev)
            w_intra, w_inter, sc, num, den, e_m = f["w_intra"], f["w_inter"], f["sc"], f["num"], f["den"], f["e_m"]
            absd = jnp.abs(den)
            r = 1.0 / jnp.maximum(absd, e_m)
            dhv = dh_ref[:, sl, :]
            s1 = jnp.sum(jnp.where(lane < NLANE, dhv * num, 0.0), axis=2, keepdims=True)
            dden = jnp.where(absd > e_m, -s1 * r * r * jnp.sign(den), 0.0)
            dnum = jnp.where(lane == NLANE, dden, jnp.where(lane < NLANE, dhv * r, 0.0))
            dsc = _bdot1(dnum, v_aug, 2, 2)
            dv = _bdot1(sc, dnum, 1, 1)
            gmat = dsc * sc
            dqk = dsc * w_intra
            dq = _bdot1(dqk, k, 2, 1) + w_inter * _bdot1(dnum, c_prev, 2, 2)
            dk = _bdot1(dqk, q, 1, 1)
            dc_prev = _bdot(q * w_inter, dnum, 1, 1)
            dlog_inter = jnp.sum(dnum * f["qc"], axis=2, keepdims=True) * w_inter
            dbcum_col = dlog_inter + jnp.sum(gmat, axis=2, keepdims=True)
            g_row = jnp.sum(gmat, axis=1, keepdims=True)
            dcn = dc_sc[...]
            w_k, decay = f["w_k"], f["decay"]
            kw = k * w_k
            dc_prev = dc_prev + decay * dcn
            db_last = jnp.sum(jnp.sum(dcn * c_prev, axis=2, keepdims=True), axis=1, keepdims=True) * decay
            dkw = _bdot(v_aug, dcn, 2, 2)
            dv = dv + _bdot1(kw, dcn, 2, 1)
            dk = dk + dkw * w_k
            dlogw = jnp.sum(dkw * k, axis=2, keepdims=True) * w_k
            db_last = db_last + jnp.sum(dlogw, axis=1, keepdims=True)
            dbcum_col = dbcum_col - dlogw
            rowi = lax.broadcasted_iota(jnp.int32, (1, L, 1), 1)
            dbcum_col = dbcum_col + jnp.where(rowi == L - 1, db_last, 0.0)
            dbcum_row = jnp.sum(jnp.where(eye, dbcum_col, 0.0), axis=1, keepdims=True) - g_row
            dlf_col = jnp.sum(jnp.where(tri_t, dbcum_row, 0.0), axis=2, keepdims=True)
            dli_col = dlogw + jnp.sum(jnp.where(eye, g_row, 0.0), axis=2, keepdims=True)
            dgf_col = dlf_col * _sigmoid(-gf)
            lane_g = lax.broadcasted_iota(jnp.int32, (L, GROUP), 1)
            dg = jnp.zeros((L, GROUP), F32)
            for h in range(H4):
                dg = dg + jnp.where(lane_g == h, dli_col[h], 0.0) + jnp.where(lane_g == H4 + h, dgf_col[h], 0.0)
            dqk_ref[0:H4, sl, :] = dq
            dqk_ref[H4:2 * H4, sl, :] = dk * kscale
            dv_ref[:, sl, :] = jnp.where(lane < NLANE, dv, 0.0).astype(BF16)
            dg_ref[0, sl, :] = dg.astype(BF16)
            dbg_ref[...] += jnp.sum(dg, axis=0, keepdims=True)
            dc_sc[...] = dc_prev

    def hspec(blk):
        return pl.BlockSpec((H4, rows, GROUP), lambda i: (blk, nb - 1 - i, 0))

    gspec = pl.BlockSpec((1, rows, GROUP), lambda i: (17, nb - 1 - i, 0))
    qkspec = pl.BlockSpec((2 * H4, rows, GROUP), lambda i: (0, nb - 1 - i, 0))
    return pl.pallas_call(
        body, name="mlstm_bwd", grid=(nb,),
        in_specs=[qkspec, hspec(2), gspec, pl.BlockSpec((1, GROUP), lambda i: (0, 0)),
                  pl.BlockSpec((cb, H4, GROUP, GROUP), lambda i: (nb - 1 - i, 0, 0, 0)),
                  pl.BlockSpec((cb, H4, 1, LANES), lambda i: (nb - 1 - i, 0, 0, 0)), hspec(0)],
        out_specs=[qkspec, hspec(0), pl.BlockSpec((1, rows, GROUP), lambda i: (0, nb - 1 - i, 0)),
                   pl.BlockSpec((1, GROUP), lambda i: (0, 0))],
        out_shape=[SDS((2 * H4, S, GROUP), F32), SDS((H4, S, GROUP), BF16),
                   SDS((1, S, GROUP), BF16), SDS((1, GROUP), F32)],
        scratch_shapes=[pltpu.VMEM((H4, GROUP, GROUP), F32)],
        compiler_params=_params(("arbitrary",)),
    )(qk, u, u, bg, cst, mst, dh)


def head_norm_fwd(hm, u, hg):
    _, S, _ = hm.shape
    ts = _tile(S, 512)

    def body(h_ref, o_ref, g_ref, t_ref):
        h = h_ref[0]
        lane = lax.broadcasted_iota(jnp.int32, h.shape, 1)
        valid = lane < ML_HEAD_DIM
        mu = jnp.sum(h, axis=-1, keepdims=True) * (1.0 / ML_HEAD_DIM)
        hc = jnp.where(valid, h - mu, 0.0)
        var = jnp.sum(hc * hc, axis=-1, keepdims=True) * (1.0 / ML_HEAD_DIM)
        hn = hc * lax.rsqrt(var + LN_EPS) * g_ref[0]
        t_ref[0] = (_sigmoid(o_ref[0]) * hn).astype(BF16)

    return pl.pallas_call(
        body, name="head_norm_fwd", grid=(H4, S // ts),
        in_specs=[pl.BlockSpec((1, ts, GROUP), lambda h, s: (h, s, 0)), pl.BlockSpec((1, ts, GROUP), lambda h, s: (12 + h, s, 0)),
                  pl.BlockSpec((1, 1, GROUP), lambda h, s: (h, 0, 0))],
        out_specs=pl.BlockSpec((1, ts, GROUP), lambda h, s: (h, s, 0)),
        out_shape=SDS((H4, S, GROUP), BF16),
        compiler_params=_params(("parallel", "parallel")),
    )(hm, u, hg)


def head_norm_bwd(hm, u, hg, dm):
    _, S, _ = hm.shape
    ts = _tile(S, 512)

    def body(h_ref, o_ref, g_ref, d_ref, dh_ref, do_ref, dg_ref):
        @pl.when(pl.program_id(1) == 0)
        def _():
            dg_ref[...] = jnp.zeros_like(dg_ref)

        h = h_ref[0]
        lane = lax.broadcasted_iota(jnp.int32, h.shape, 1)
        valid = lane < ML_HEAD_DIM
        inv = 1.0 / ML_HEAD_DIM
        mu = jnp.sum(h, axis=-1, keepdims=True) * inv
        hc = jnp.where(valid, h - mu, 0.0)
        var = jnp.sum(hc * hc, axis=-1, keepdims=True) * inv
        rstd = lax.rsqrt(var + LN_EPS)
        xhat = hc * rstd
        g = g_ref[0]
        sig = _sigmoid(o_ref[0])
        dt = jnp.where(valid, d_ref[0], 0.0)
        do_ref[0] = (dt * xhat * g * sig * (1.0 - sig)).astype(BF16)
        dhn = dt * sig
        dg_ref[0] += jnp.sum(dhn * xhat, axis=0, keepdims=True)
        dxh = dhn * g
        m1 = jnp.sum(dxh, axis=-1, keepdims=True) * inv
        m2 = jnp.sum(dxh * xhat, axis=-1, keepdims=True) * inv
        dh_ref[0] = jnp.where(valid, rstd * (dxh - m1 - xhat * m2), 0.0)

    spec = pl.BlockSpec((1, ts, GROUP), lambda h, s: (h, s, 0))
    gspec = pl.BlockSpec((1, 1, GROUP), lambda h, s: (h, 0, 0))
    return pl.pallas_call(
        body, name="head_norm_bwd", grid=(H4, S // ts),
        in_specs=[spec, pl.BlockSpec((1, ts, GROUP), lambda h, s: (12 + h, s, 0)), gspec, spec],
        out_specs=[spec, spec, gspec],
        out_shape=[SDS((H4, S, GROUP), F32), SDS((H4, S, GROUP), BF16), SDS((H4, 1, GROUP), F32)],
        compiler_params=_params(("parallel", "arbitrary")),
    )(hm, u, hg, dm)


def _adamw_math(w, g, m, v):
    c1 = 1.0 / (1.0 - ADAM_B1 ** ADAM_STEP)
    c2 = 1.0 / (1.0 - ADAM_B2 ** ADAM_STEP)
    nm = ADAM_B1 * m + (1.0 - ADAM_B1) * g
    nv = ADAM_B2 * v + (1.0 - ADAM_B2) * (g * g)
    return -ADAM_LR * ((nm * c1) / (jnp.sqrt(nv * c2) + ADAM_EPS) + ADAM_WD * w), nm, nv


def _row_tile(R, cap=512):
    return R if R <= cap else max(d for d in range(8, cap + 1, 8) if R % d == 0)


def adamw_into(w, m, v, g, outs, idx, after, name):
    R, C = g.shape
    tr = _row_tile(R)
    lead = (0,) * len(idx)

    def body(w_ref, m_ref, v_ref, g_ref, *rest):
        go_ref, d_ref, nm_ref, nv_ref, token = rest[-5:]
        token[...] = jnp.zeros_like(token)
        gv = g_ref[...]
        d, nm, nv = _adamw_math(w_ref[lead], gv, m_ref[lead], v_ref[lead])
        go_ref[lead] = gv
        d_ref[lead] = d
        nm_ref[lead] = nm
        nv_ref[lead] = nv

    blk = pl.BlockSpec((1,) * len(idx) + (tr, C), lambda r: idx + (r, 0))
    any_space = pl.BlockSpec(memory_space=pl.ANY)
    in_specs, args, aliases = [blk, blk, blk, pl.BlockSpec((tr, C), lambda r: (r, 0)), any_space], [w, m, v, g, g if after is None else after], {}
    if outs is not None:
        in_specs += [any_space] * 4
        args += list(outs)
        aliases = {5 + i: i for i in range(4)}
    out = pl.pallas_call(
        body, name=name, grid=(R // tr,),
        in_specs=in_specs, out_specs=[blk] * 4 + [pl.BlockSpec((8, LANES), lambda r: (0, 0))],
        out_shape=[SDS(w.shape, F32)] * 4 + [SDS((8, LANES), F32)],
        input_output_aliases=aliases, compiler_params=_params(("arbitrary",)),
    )(*args)
    return out[:4], out[4]


def adamw(w, g, m, v, name):
    R, C = w.shape
    tr = _row_tile(R)

    def body(w_ref, g_ref, m_ref, v_ref, d_ref, nm_ref, nv_ref):
        d_ref[...], nm_ref[...], nv_ref[...] = _adamw_math(w_ref[...], g_ref[...], m_ref[...], v_ref[...])

    spec = pl.BlockSpec((tr, C), lambda i: (i, 0))
    return pl.pallas_call(
        body, name=name, grid=(R // tr,),
        in_specs=[spec] * 4, out_specs=[spec] * 3,
        out_shape=[SDS((R, C), F32)] * 3,
        compiler_params=_params(("parallel",)),
    )(w, g, m, v)


HBM = pl.BlockSpec(memory_space=pl.ANY)
ROW_SPLIT = 4
PAIR_SPLIT = 1


def _position():
    x, y, c = lax.axis_index("x"), lax.axis_index("y"), lax.axis_index("c")
    return x, y, c, [(1 - x, y), (x, 1 - y), (1 - x, 1 - y)]


def _unique(items):
    arrays = []
    for a, _ in items:
        if not any(a is b for b in arrays):
            arrays.append(a)
    return arrays, [next(i for i, b in enumerate(arrays) if b is a) for a, _ in items]


def place_own(items, me, after, name):
    arrays, src_of = _unique(items)
    n = len(items)
    shapes = [a.shape[len(p):] for a, p in items]

    def body(me_ref, *refs):
        for t in range(n):
            refs[n + 1 + t][0] = refs[t][(0,) * len(items[t][1])]

    in_specs, out_specs = [], []
    for (a, p), shp in zip(items, shapes):
        blk = shp[:-2] + (shp[-2] // ROW_SPLIT, shp[-1])
        lead = (0,) * (len(shp) - 2)
        in_specs.append(pl.BlockSpec((1,) * len(p) + blk, functools.partial(lambda r, me_ref, p, lead: p + lead + (r, 0), p=p, lead=lead)))
        out_specs.append(pl.BlockSpec((1,) + blk, functools.partial(lambda r, me_ref, lead: (me_ref[0],) + lead + (r, 0), lead=lead)))
    in_specs.append(pl.BlockSpec(memory_space=pl.ANY))
    return pl.pallas_call(
        body, name=name,
        grid_spec=pltpu.PrefetchScalarGridSpec(num_scalar_prefetch=1, grid=(ROW_SPLIT,), in_specs=in_specs, out_specs=out_specs),
        out_shape=[SDS((N_CHIPS,) + tuple(shp), a.dtype) for shp, (a, _) in zip(shapes, items)],
        compiler_params=_params(("parallel",)),
    )(me, *[arrays[i] for i in src_of], after)


SEM = pl.BlockSpec(memory_space=pltpu.SEMAPHORE)
IN_HBM = pl.BlockSpec(memory_space=pltpu.HBM)
DATAFLOW = pltpu.SideEffectType.DATAFLOW_SIDE_EFFECTING


def split_start(bufs, plan, n_copies, after, name):
    n = len(bufs)

    def body(*refs):
        send, recv, token = refs[n + 1], refs[n + 2], refs[-1]
        x, y, c, chips = _position()
        for k, (src, dst, dev) in enumerate(plan(refs[:n], x, y, c, chips)):
            pltpu.make_async_remote_copy(src_ref=src, dst_ref=dst, send_sem=send.at[k], recv_sem=recv.at[k],
                                         device_id=dev, device_id_type=MESH).start()
        token[...] = jnp.zeros_like(token)

    out = pl.pallas_call(
        body, name=name,
        out_shape=(pltpu.SemaphoreType.DMA((n_copies,)), pltpu.SemaphoreType.DMA((n_copies,)),
                   *[pltpu.HBM(b.shape, b.dtype) for b in bufs], SDS((8, LANES), F32)),
        in_specs=[IN_HBM] * n + [pl.BlockSpec(memory_space=pl.ANY)],
        out_specs=(SEM, SEM, *[IN_HBM] * n, pl.BlockSpec(memory_space=pltpu.VMEM)),
        input_output_aliases={i: 2 + i for i in range(n)},
        compiler_params=pltpu.CompilerParams(has_side_effects=DATAFLOW),
    )(*[pltpu.with_memory_space_constraint(b, pltpu.HBM) for b in bufs], after)
    return out[0], out[1], list(out[2:2 + n]), out[-1]


def split_wait(send, recv, bufs, plan, after, name):
    n = len(bufs)

    def body(*refs):
        send_ref, recv_ref = refs[n], refs[n + 1]
        x, y, c, chips = _position()
        for k, (src, dst, dev) in enumerate(plan(refs[:n], x, y, c, chips)):
            cp = pltpu.make_async_remote_copy(src_ref=src, dst_ref=dst, send_sem=send_ref.at[k], recv_sem=recv_ref.at[k],
                                              device_id=dev, device_id_type=MESH)
            cp.wait_send()
            cp.wait_recv()

    return list(pl.pallas_call(
        body, name=name, out_shape=tuple(pltpu.HBM(b.shape, b.dtype) for b in bufs),
        in_specs=[IN_HBM] * n + [SEM, SEM, pl.BlockSpec(memory_space=pl.ANY)], out_specs=tuple([IN_HBM] * n),
        input_output_aliases={i: i for i in range(n)},
        compiler_params=pltpu.CompilerParams(has_side_effects=DATAFLOW),
    )(*bufs, send, recv, after))


def _gather_plan(shapes, landing):
    n = len(shapes)

    def plan(refs, x, y, c, chips):
        out = []
        for t in range(n):
            half = shapes[t][0] // 2
            rows = pl.ds(c * half, half)
            for cx, cy in chips:
                slot = 2 * cx + cy if landing else 2 * x + y
                out.append((refs[t].at[rows], refs[n + t].at[slot, rows], (cx, cy, c)))
        return out

    return plan


def gather_start(shards, placed, after, name):
    shapes = [s.shape for s in shards]
    send, recv, bufs, token = split_start(list(shards) + list(placed), _gather_plan(shapes, False), 3 * len(shards), after, name)
    return (send, recv, bufs, shapes), token


def gather_wait(state, after, name):
    send, recv, bufs, shapes = state
    return split_wait(send, recv, bufs, _gather_plan(shapes, True), after, name)[len(shapes):]


def gather_pass_on(placed, shapes, name):
    n = len(placed)

    def body(*refs):
        outs, send, recv = refs[n:2 * n], refs[2 * n], refs[2 * n + 1]
        x, y, c, chips = _position()
        cps = []
        for t in range(n):
            half = shapes[t][0] // 2
            for j, (cx, cy) in enumerate(chips):
                piece = outs[t].at[2 * cx + cy, pl.ds(c * half, half)]
                cp = pltpu.make_async_remote_copy(src_ref=piece, dst_ref=piece, send_sem=send.at[3 * t + j], recv_sem=recv.at[3 * t + j],
                                                  device_id=(x, y, 1 - c), device_id_type=MESH)
                cp.start()
                cps.append(cp)
        for t in range(n):
            half = shapes[t][0] // 2
            for j, (cx, cy) in enumerate(chips):
                piece = outs[t].at[2 * cx + cy, pl.ds((1 - c) * half, half)]
                pltpu.make_async_remote_copy(src_ref=piece, dst_ref=piece, send_sem=send.at[3 * t + j], recv_sem=recv.at[3 * t + j],
                                             device_id=(x, y, 1 - c), device_id_type=MESH).wait_recv()
        for cp in cps:
            cp.wait_send()

    return pl.pallas_call(
        body, name=name,
        in_specs=[HBM] * n, out_specs=[HBM] * n,
        out_shape=[SDS(p.shape, p.dtype) for p in placed],
        input_output_aliases={t: t for t in range(n)},
        scratch_shapes=[pltpu.SemaphoreType.DMA((3 * n,))] * 2,
    )(*placed)


def _flip(k, x, y, c):
    return ((1 - x) if k & 4 else x, (1 - y) if k & 2 else y, (1 - c) if k & 1 else c)


def small_allgather(v, reduce):
    R, C = v.shape

    def body(v_ref, o_ref, *scratch):
        if reduce:
            buf, send, recv = scratch
        else:
            buf, (send, recv) = o_ref, scratch
        x, y, c, _ = _position()
        me = 4 * x + 2 * y + c
        buf[me] = v_ref[...]
        sends = []
        for k in range(1, N_DEV):
            cp = pltpu.make_async_remote_copy(src_ref=v_ref, dst_ref=buf.at[me], send_sem=send.at[k - 1], recv_sem=recv.at[k - 1],
                                              device_id=_flip(k, x, y, c), device_id_type=MESH)
            cp.start()
            sends.append(cp)
        for k in range(1, N_DEV):
            px, py, pc = _flip(k, x, y, c)
            pltpu.make_async_remote_copy(src_ref=v_ref, dst_ref=buf.at[4 * px + 2 * py + pc], send_sem=send.at[k - 1],
                                         recv_sem=recv.at[k - 1], device_id=(px, py, pc), device_id_type=MESH).wait_recv()
        for cp in sends:
            cp.wait_send()
        if reduce:
            acc = buf[0]
            for i in range(1, N_DEV):
                acc = acc + buf[i]
            o_ref[...] = acc

    vm = pl.BlockSpec(memory_space=pltpu.VMEM)
    sems = [pltpu.SemaphoreType.DMA((N_DEV - 1,)), pltpu.SemaphoreType.DMA((N_DEV - 1,))]
    return pl.pallas_call(
        body, name="small_allreduce" if reduce else "small_allgather",
        in_specs=[vm], out_specs=vm,
        out_shape=SDS((R, C) if reduce else (N_DEV, R, C), F32),
        scratch_shapes=([pltpu.VMEM((N_DEV, R, C), F32)] if reduce else []) + sems,
    )(v)


def rs_exchange_sibling(gs):
    n = len(gs)

    def body(*refs):
        ins, outs, send, recv = refs[:n], refs[n:2 * n], refs[2 * n], refs[2 * n + 1]
        x, y, c, _ = _position()
        cps = []
        for t in range(n):
            cp = pltpu.make_async_remote_copy(src_ref=ins[t].at[:, 1 - c], dst_ref=outs[t], send_sem=send.at[t], recv_sem=recv.at[t],
                                              device_id=(x, y, 1 - c), device_id_type=MESH)
            cp.start()
            cps.append(cp)
        for cp in cps:
            cp.wait()

    return pl.pallas_call(
        body, name="rs_exchange_sibling", in_specs=[HBM] * n, out_specs=[HBM] * n,
        out_shape=[SDS((g.shape[0],) + g.shape[2:], g.dtype) for g in gs],
        scratch_shapes=[pltpu.SemaphoreType.DMA((n,)), pltpu.SemaphoreType.DMA((n,))],
    )(*gs)


def rs_pair_add(gs, rs, c):
    n = len(gs)

    def body(c_ref, *refs):
        for t in range(n):
            refs[2 * n + t][0] = (refs[t][0, 0].astype(F32) + refs[n + t][0].astype(F32)).astype(BF16)

    in_specs, out_specs, out_shape = [], [], []
    for g in gs:
        _, _, h, C = g.shape
        in_specs.append(pl.BlockSpec((1, 1, h // PAIR_SPLIT, C), lambda j, r, c_ref: (j, c_ref[0], r, 0)))
    for g in gs:
        _, _, h, C = g.shape
        spec = pl.BlockSpec((1, h // PAIR_SPLIT, C), lambda j, r, c_ref: (j, r, 0))
        in_specs.append(spec)
        out_specs.append(spec)
        out_shape.append(SDS((N_CHIPS, h, C), BF16))
    return pl.pallas_call(
        body, name="rs_pair_add",
        grid_spec=pltpu.PrefetchScalarGridSpec(num_scalar_prefetch=1, grid=(N_CHIPS, PAIR_SPLIT), in_specs=in_specs, out_specs=out_specs),
        out_shape=out_shape, compiler_params=_params(("parallel", "parallel")),
    )(c, *gs, *rs)


def _rs_plan(n):
    def plan(refs, x, y, c, chips):
        return [(refs[t].at[2 * cx + cy], refs[n + t].at[j], (cx, cy, c)) for t in range(n) for j, (cx, cy) in enumerate(chips)]

    return plan


def rs_chip_add(ps, qs, me_c):
    n = len(ps)

    def body(me_ref, *refs):
        for t in range(n):
            q = refs[n + t]
            refs[2 * n + t][0] = ((refs[t][0].astype(F32) + q[0].astype(F32)) + q[1].astype(F32)) + q[2].astype(F32)

    in_specs, out_specs, out_shape = [], [], []
    for p in ps:
        _, h, C = p.shape
        in_specs.append(pl.BlockSpec((1, h // ROW_SPLIT, C), lambda r, me_ref: (me_ref[0], r, 0)))
    for p in ps:
        _, h, C = p.shape
        in_specs.append(pl.BlockSpec((3, h // ROW_SPLIT, C), lambda r, me_ref: (0, r, 0)))
        out_specs.append(pl.BlockSpec((1, h // ROW_SPLIT, C), lambda r, me_ref: (me_ref[1], r, 0)))
        out_shape.append(SDS((2, h, C), F32))
    return pl.pallas_call(
        body, name="rs_chip_add",
        grid_spec=pltpu.PrefetchScalarGridSpec(num_scalar_prefetch=1, grid=(ROW_SPLIT,), in_specs=in_specs, out_specs=out_specs),
        out_shape=out_shape, compiler_params=_params(("parallel",)),
    )(me_c, *ps, *qs)


def rs_share(rs):
    n = len(rs)

    def body(*refs):
        outs, send, recv = refs[n:2 * n], refs[2 * n], refs[2 * n + 1]
        x, y, c, _ = _position()
        cps = []
        for t in range(n):
            cp = pltpu.make_async_remote_copy(src_ref=outs[t].at[c], dst_ref=outs[t].at[c], send_sem=send.at[t], recv_sem=recv.at[t],
                                              device_id=(x, y, 1 - c), device_id_type=MESH)
            cp.start()
            cps.append(cp)
        for cp in cps:
            cp.wait()

    return pl.pallas_call(
        body, name="rs_share", in_specs=[HBM] * n, out_specs=[HBM] * n,
        out_shape=[SDS(r.shape, r.dtype) for r in rs],
        input_output_aliases={t: t for t in range(n)},
        scratch_shapes=[pltpu.SemaphoreType.DMA((n,))] * 2,
    )(*rs)


def rs_begin(gs, after, name):
    c = lax.axis_index("c")
    n = len(gs)
    g5 = [g.reshape(N_CHIPS, 2, g.shape[1] // 2, g.shape[2]) for g in gs]
    from_sibling = rs_exchange_sibling(g5)
    pair = rs_pair_add(g5, from_sibling, jnp.reshape(c, (1,)).astype(jnp.int32))
    lands = [lax.empty((3,) + p.shape[1:], p.dtype) for p in pair]
    send, recv, bufs, token = split_start(list(pair) + lands, _rs_plan(n), 3 * n, from_sibling[0] if after is None else after, name)
    return (send, recv, bufs, [g.shape for g in gs]), token


def rs_end(state, after, name):
    x, y, c = lax.axis_index("x"), lax.axis_index("y"), lax.axis_index("c")
    send, recv, bufs, shapes = state
    n = len(shapes)
    bufs = split_wait(send, recv, bufs, _rs_plan(n), after, name)
    half = rs_chip_add(bufs[:n], bufs[n:], jnp.stack([2 * x + y, c]).astype(jnp.int32))
    both = rs_share(half)
    return [b.reshape(s[1], s[2]) for b, s in zip(both, shapes)]


def _pad_last(a, n):
    return jnp.pad(a, [(0, 0)] * (a.ndim - 1) + [(0, n - a.shape[-1])])


def _heads_to_groups(w):
    k = w.shape[0]
    return _pad_last(w.reshape(k, ML_HEADS, ML_HEAD_DIM).transpose(1, 0, 2), GROUP)


def _groups_to_heads(g):
    return g[:, :, :ML_HEAD_DIM].transpose(1, 0, 2).reshape(g.shape[1], D_TOK)


def _cols_to_groups(w):
    k, n = w.shape
    return w.reshape(k, n // GROUP, GROUP).transpose(1, 0, 2)


def _groups_to_cols(g):
    n, k, _ = g.shape
    return g.transpose(1, 0, 2).reshape(k, n * GROUP)


def _chips_to_cols(a):
    return a.transpose(1, 0, 2).reshape(a.shape[1], -1)


def _cols_to_chips(w):
    k, n = w.shape
    return w.reshape(k, N_CHIPS, n // N_CHIPS).transpose(1, 0, 2)


def _mlstm_in_groups(w):
    parts = [_heads_to_groups(w[:, i * D_TOK:(i + 1) * D_TOK]) for i in range(4)]
    gates = _pad_last(w[:, 4 * D_TOK:4 * D_TOK + 2 * ML_HEADS], GROUP)[None]
    qmem = w[:, 4 * D_TOK + 2 * ML_HEADS:][None]
    return jnp.concatenate(parts + [qmem, gates], axis=0)


def _mlstm_in_ungroup(g):
    parts = [_groups_to_heads(g[4 * i:4 * i + 4]) for i in range(4)]
    return jnp.concatenate(parts + [g[17][:, :2 * ML_HEADS], g[16]], axis=1)


def _taps_to_groups(w, width):
    taps = w.shape[0]
    g = _pad_last(w.reshape(taps, -1, width), GROUP).transpose(1, 0, 2)
    return jnp.pad(g, ((0, 0), (0, 8 - taps), (0, 0)))


def _groups_to_taps(g, taps, width):
    return g[:, :taps, :width].transpose(1, 0, 2).reshape(taps, -1)


SMALL_IN_COLS = 384
SMALL_OUT_COLS = 1536
SECTION = 8


class _Gathered:
    def __init__(self, srcs, groups, me):
        self.groups, self.states, self.ready = groups, [], {}
        self.group_of = {k: gi for gi, g in enumerate(groups) for k in g}
        token = me
        for gi, g in enumerate(groups):
            placed = place_own([(srcs[k], ()) for k in g], me, token, f"place_own_{gi}")
            state, token = gather_start([srcs[k] for k in g], placed, token, f"gather_start_{gi}")
            self.states.append(state)
        self.started = token

    def _get(self, key, after):
        gi = self.group_of[key]
        if gi not in self.ready:
            got = gather_wait(self.states[gi], after if gi else self.started, f"gather_wait_{gi}")
            self.ready[gi] = dict(zip(self.groups[gi], gather_pass_on(got, self.states[gi][3], f"gather_pass_on_{gi}")))
        return self.ready[gi][key]

    def ffn(self, l, i, after):
        return tuple(self._get((n, l, i), after) for n in ("wg", "wu", "wd"))

    def mixer(self, l, after):
        win = _chips_to_cols(self._get(("win", l), after))
        win = _cols_to_groups(win) if l % 2 == 0 else _mlstm_in_groups(win)
        wkv = _cols_to_groups(self._get(("wkv", l), after).reshape(D_MODEL, 2 * D_XA))
        wout = self._get(("wout", l), after)
        if l % 2:
            wout = wout.reshape(D_MODEL, D_MODEL)
            tok = jnp.pad(wout[:D_TOK].reshape(ML_HEADS, ML_HEAD_DIM, D_MODEL), ((0, 0), (0, GROUP - ML_HEAD_DIM), (0, 0)))
            wout = jnp.concatenate([tok, wout[D_TOK:][None]], axis=0)
        return win, wkv, wout


class _GradSink:
    def __init__(self, apply):
        self.queue, self.apply, self.count, self.done = [], apply, 0, None

    @staticmethod
    def _by_chip(key, g):
        if key[0] == "wkv":
            return _groups_to_cols(g).reshape(N_CHIPS, D_MODEL // N_CHIPS, 2 * D_XA)
        if key[0] == "win":
            return _cols_to_chips(_groups_to_cols(g) if key[1] % 2 == 0 else _mlstm_in_ungroup(g))
        if key[0] == "wout" and key[1] % 2:
            full = jnp.concatenate([g[:ML_HEADS, :ML_HEAD_DIM].reshape(D_TOK, D_MODEL), g[ML_HEADS]], axis=0)
            return full.reshape(N_CHIPS, D_MODEL // N_CHIPS, D_MODEL)
        return g

    def push(self, grads):
        keys = list(grads)
        state, token = rs_begin([self._by_chip(k, grads[k]) for k in keys], self.done, f"rs_start_{self.count}")
        if self.queue:
            self._finish(token)
        self.queue.append((keys, state, self.count))
        self.count += 1
        return token

    def flush(self):
        self._finish(self.done)

    def _finish(self, after):
        keys, state, i = self.queue.pop(0)
        for key, g in zip(keys, rs_end(state, after, f"rs_wait_{i}")):
            self.done = self.apply(key, g, self.done)


def _local_step(x, mem, tgt, P, weights, sink):
    memb = mem.astype(BF16)
    saved = []
    X, Xb = x, x.astype(BF16)
    after = Xb
    for l in range(DEPTH):
        s = {}
        s["x0b"] = Xb
        s["wa"] = weights.ffn(l, 0, after)
        s["g1a"], s["u1a"], s["ha"], s["z1"], X1, X1b = ffn_fwd(Xb, X, *s["wa"], P["ln_g"][l][0], P["ln_b"][l][0])
        s["x1b"] = X1b
        s["wm"] = win, wkv, wout = weights.mixer(l, X1b)
        u = proj(X1b, win, "mixer_in")
        kv = proj(memb, wkv, "mem_kv")
        s["u"], s["kv"] = u, kv
        if l % 2 == 0:
            tok = conv_mixer_fwd(u, P["convw"])
            qg = 9
        else:
            s["qk"] = qk_conv_fwd(u, P["qkw"])
            s["hm"], s["cst"], s["mst"] = mlstm_fwd(s["qk"], u, P["bg"])
            tok = head_norm_fwd(s["hm"], u, P["hg"])
            qg = 16
        xa = xattn_fwd(u, qg, kv)
        s["m"] = jnp.concatenate([tok, xa], axis=0)
        s["z2"], X2, X2b = contract_ln(s["m"], wout, X1, P["ln_g"][l][1], P["ln_b"][l][1], 1.0, "mixer_out_ln")
        s["x2b"] = X2b
        s["wb"] = weights.ffn(l, 1, X2b)
        s["g1b"], s["u1b"], s["hb"], s["z3"], X, Xb = ffn_fwd(X2b, X2, *s["wb"], P["ln_g"][l][2], P["ln_b"][l][2])
        after = Xb
        saved.append(s)

    loss, dX = loss_grad(X, tgt)

    G = {"ln_g": [[None] * 3 for _ in range(DEPTH)], "ln_b": [[None] * 3 for _ in range(DEPTH)]}
    pin = [jnp.zeros((1, 1), F32)]

    def ffn_backward(l, i, dX, z, xinb, g1, u1, h, w):
        k = 2 * i
        dgb, dub, dx, dyb, G["ln_g"][l][k], G["ln_b"][l][k] = ffn_bwd(dX, z, P["ln_g"][l][k] + pin[0], w[2], w[0], w[1], g1, u1)
        grads = {("wd", l, i): wgrad(h, dyb, BF16, "wgrad_down"), ("wg", l, i): wgrad(dgb, xinb, BF16, "wgrad_gate"),
                 ("wu", l, i): wgrad(dub, xinb, BF16, "wgrad_up")}
        return dx, grads

    for l in reversed(range(DEPTH)):
        s = saved[l]
        win, wkv, wout = s["wm"]
        dX, grads = ffn_backward(l, 1, dX, s["z3"], s["x2b"], s["g1b"], s["u1b"], s["hb"], s["wb"])
        dz2, dz2b, G["ln_g"][l][1], G["ln_b"][l][1] = ln_bwd(dX, s["z2"], P["ln_g"][l][1], 1.0, "mixer_ln_bwd")
        dm = proj_t(dz2b, wout, "mixer_out_bwd")
        grads[("wout", l)] = wgrad(s["m"], dz2b, BF16, "wgrad_out")
        u, kv = s["u"], s["kv"]
        if l % 2 == 0:
            db, dc, dxi, G["convw"] = conv_mixer_bwd(u, P["convw"], dm)
            dq, dkv = xattn_bwd(u, 9, kv, dm, 3)
            du = jnp.concatenate([db, dc, dxi, dq], axis=0)
        else:
            dh, do, G["hg"] = head_norm_bwd(s["hm"], u, P["hg"], dm)
            dqk, dv, dgate, G["bg"] = mlstm_bwd(s["qk"], u, P["bg"], s["cst"], s["mst"], dh)
            duqk, G["qkw"] = qk_conv_bwd(u, P["qkw"], dqk)
            dq, dkv = xattn_bwd(u, 16, kv, dm, 4)
            du = jnp.concatenate([duqk, dv, do, dq, dgate], axis=0)
        grads[("win", l)] = wgrad(s["x1b"], du, BF16, "wgrad_in")
        grads[("wkv", l)] = wgrad(memb, dkv.astype(BF16), BF16, "wgrad_kv")
        dX = contract_t(du, win, dz2, "mixer_in_bwd")
        pin[0] = sink.push(grads)[0:1, 0:1]
        dX, grads = ffn_backward(l, 0, dX, s["z1"], s["x0b"], s["g1a"], s["u1a"], s["ha"], s["wa"])
        pin[0] = sink.push(grads)[0:1, 0:1]
    sink.flush()
    return loss, dX, G


def kernel(x, mem, ln_g, ln_b, ffn_w_gate, ffn_w_up, ffn_w_down, w_kv_mem, w_out, w_in_conv, conv_w, w_in_mlstm, b_gates, qk_conv_w, head_norm_g, loss_target, m_ln_g, m_ln_b, m_ffn_w_gate, m_ffn_w_up, m_ffn_w_down, m_w_kv_mem, m_w_out, m_w_in_conv, m_conv_w, m_w_in_mlstm, m_b_gates, m_qk_conv_w, m_head_norm_g, v_ln_g, v_ln_b, v_ffn_w_gate, v_ffn_w_up, v_ffn_w_down, v_w_kv_mem, v_w_out, v_w_in_conv, v_conv_w, v_w_in_mlstm, v_b_gates, v_qk_conv_w, v_head_norm_g):
    cx, cy = lax.axis_index("x"), lax.axis_index("y")
    chip = 2 * cx + cy

    srcs = {}
    for l in range(DEPTH):
        for i in range(2):
            srcs[("wg", l, i)] = jnp.swapaxes(ffn_w_gate[l, i], 0, 1).astype(BF16)
            srcs[("wu", l, i)] = jnp.swapaxes(ffn_w_up[l, i], 0, 1).astype(BF16)
            srcs[("wd", l, i)] = ffn_w_down[l, i].astype(BF16)
        srcs[("wkv", l)] = w_kv_mem[l].astype(BF16)
        srcs[("wout", l)] = w_out[l].astype(BF16)
    srcs[("win", 0)] = w_in_conv[0].astype(BF16)
    srcs[("win", 1)] = w_in_mlstm[0].astype(BF16)
    ffn_keys = lambda l, i: [("wg", l, i), ("wu", l, i), ("wd", l, i)]
    mixer_keys = lambda l: [("win", l), ("wkv", l), ("wout", l)]
    groups = [ffn_keys(0, 0), mixer_keys(0) + mixer_keys(1), ffn_keys(0, 1), ffn_keys(1, 0), ffn_keys(1, 1)]
    gathered = _Gathered(srcs, groups, jnp.reshape(chip, (1,)).astype(jnp.int32))

    def section(a, width):
        a = a.reshape(-1, a.shape[-1])
        return jnp.pad(a, ((0, SECTION - a.shape[0]), (0, width - a.shape[1])))

    small = jnp.concatenate([section(a, SMALL_IN_COLS) for a in (ln_g, ln_b, conv_w, qk_conv_w)], axis=0)
    smalls = small_allgather(small, reduce=False)[0::2]
    ln_g_full = _chips_to_cols(smalls[:, 0:6, 0:256]).reshape(DEPTH, 3, 1, D_MODEL)
    ln_b_full = _chips_to_cols(smalls[:, 8:14, 0:256]).reshape(DEPTH, 3, 1, D_MODEL)
    conv_w_full = _chips_to_cols(smalls[:, 16:19, 0:192])
    qk_w_full = _chips_to_cols(smalls[:, 24:28, 0:384])

    P = {"ln_g": ln_g_full, "ln_b": ln_b_full, "convw": _taps_to_groups(conv_w_full, GROUP),
         "qkw": _taps_to_groups(qk_w_full, ML_HEAD_DIM), "bg": _pad_last(b_gates, GROUP),
         "hg": _pad_last(head_norm_g[0], GROUP)[:, None, :]}

    weights = {"ln_g": ln_g, "ln_b": ln_b, "ffn_w_gate": ffn_w_gate, "ffn_w_up": ffn_w_up, "ffn_w_down": ffn_w_down,
               "w_kv_mem": w_kv_mem, "w_out": w_out, "w_in_conv": w_in_conv, "conv_w": conv_w, "w_in_mlstm": w_in_mlstm,
               "b_gates": b_gates, "qk_conv_w": qk_conv_w, "head_norm_g": head_norm_g}
    ms = {"ln_g": m_ln_g, "ln_b": m_ln_b, "ffn_w_gate": m_ffn_w_gate, "ffn_w_up": m_ffn_w_up, "ffn_w_down": m_ffn_w_down,
          "w_kv_mem": m_w_kv_mem, "w_out": m_w_out, "w_in_conv": m_w_in_conv, "conv_w": m_conv_w, "w_in_mlstm": m_w_in_mlstm,
          "b_gates": m_b_gates, "qk_conv_w": m_qk_conv_w, "head_norm_g": m_head_norm_g}
    vs = {"ln_g": v_ln_g, "ln_b": v_ln_b, "ffn_w_gate": v_ffn_w_gate, "ffn_w_up": v_ffn_w_up, "ffn_w_down": v_ffn_w_down,
          "w_kv_mem": v_w_kv_mem, "w_out": v_w_out, "w_in_conv": v_w_in_conv, "conv_w": v_conv_w, "w_in_mlstm": v_w_in_mlstm,
          "b_gates": v_b_gates, "qk_conv_w": v_qk_conv_w, "head_norm_g": v_head_norm_g}
    names = list(weights)
    owner = {"wg": ("ffn_w_gate", True), "wu": ("ffn_w_up", True), "wd": ("ffn_w_down", False), "wkv": ("w_kv_mem", False),
             "wout": ("w_out", False), "win": None}
    updated = {}

    def apply(key, g, after):
        name, transposed = owner[key[0]] or (("w_in_conv", "w_in_mlstm")[key[1]], False)
        idx = (0,) if key[0] == "win" else tuple(key[1:])
        view = (lambda a: jnp.swapaxes(a, -1, -2)) if transposed else (lambda a: a)
        updated[name], token = adamw_into(view(weights[name]), view(ms[name]), view(vs[name]), g, updated.get(name), idx, after,
                                          "adamw_" + name + "_" + "_".join(map(str, idx)))
        return token

    sink = _GradSink(apply)
    loss, grad_x, G = _local_step(x[0], mem[0], loss_target[0], P, gathered, sink)

    dln_g = jnp.concatenate([G["ln_g"][l][k] for l in range(DEPTH) for k in range(3)], axis=0)
    dln_b = jnp.concatenate([G["ln_b"][l][k] for l in range(DEPTH) for k in range(3)], axis=0)
    lane = lax.broadcasted_iota(jnp.int32, (1, GROUP), 1)
    misc = jnp.where(lane < 8, G["bg"], 0.0) + jnp.where(lane == 8, loss, 0.0) + sink.done[0:1, 0:1]
    parts = (dln_g, dln_b, _groups_to_taps(G["convw"], 3, GROUP), misc, _groups_to_taps(G["qkw"], 4, ML_HEAD_DIM),
             G["hg"][:, 0, :ML_HEAD_DIM])
    tot = small_allgather(jnp.concatenate([section(a, SMALL_OUT_COLS) for a in parts], axis=0), reduce=True)
    loss_total = tot[24, 8]

    small_grads = {
        "ln_g": lax.dynamic_slice(tot[0:6, 0:D_MODEL], (0, chip * 256), (6, 256)).reshape(DEPTH, 3, 256),
        "ln_b": lax.dynamic_slice(tot[8:14, 0:D_MODEL], (0, chip * 256), (6, 256)).reshape(DEPTH, 3, 256),
        "conv_w": lax.dynamic_slice(tot[16:19, 0:D_TOK], (0, chip * 192), (3, 192))[None],
        "b_gates": tot[24:25, 0:8],
        "qk_conv_w": lax.dynamic_slice(tot[32:36, 0:2 * D_TOK], (0, chip * 384), (4, 384))[None],
        "head_norm_g": tot[40:44, 0:ML_HEAD_DIM][None],
    }
    grads, deltas, new_m, new_v = [], [], [], []
    for nme in names:
        if nme in updated:
            back = (lambda a: jnp.swapaxes(a, -1, -2)) if nme in ("ffn_w_gate", "ffn_w_up") else (lambda a: a)
            g, d, nm, nv = (back(a) for a in updated[nme])
        else:
            w, g = weights[nme], small_grads[nme]
            two = (math.prod(w.shape[:-1]), w.shape[-1])
            d, nm, nv = (a.reshape(w.shape) for a in adamw(w.reshape(two), g.reshape(two), ms[nme].reshape(two),
                                                           vs[nme].reshape(two), "adamw_" + nme))
        grads.append(g)
        deltas.append(d)
        new_m.append(nm)
        new_v.append(nv)
    return (loss_total, grad_x[None], *grads, *deltas, *new_m, *new_v)
```

```python
import functools
import math

import jax
import jax.numpy as jnp
from jax import lax
from jax.experimental import pallas as pl
from jax.experimental.pallas import tpu as pltpu

F32 = jnp.float32
BF16 = jnp.bfloat16
SDS = jax.ShapeDtypeStruct

D_MODEL = 1024
DEPTH = 2
N_MEM = 256
XA_HEADS = 4
XA_HEAD_DIM = 64
D_XA = 256
D_TOK = 768
ML_HEADS = 4
ML_HEAD_DIM = 192
ML_CHUNK = 64
D_FF = 2816
LN_EPS = 1e-5
ALPHA = (2.0 * DEPTH) ** 0.25
N_CHIPS = 4
N_DEV = 8
FF_SHARD = D_FF // N_CHIPS
GROUP = 256
NEG = -1e30

ADAM_LR = 0.001
ADAM_B1 = 0.9
ADAM_B2 = 0.999
ADAM_EPS = 1e-08
ADAM_WD = 0.01
ADAM_STEP = 10

VMEM_LIMIT = 56 * 1024 * 1024

NN = ((1,), (0,))
NT = ((1,), (1,))
TN = ((0,), (0,))
MESH = pl.DeviceIdType.MESH


def _dot(a, b, dims):
    return lax.dot_general(a, b, (dims, ((), ())), preferred_element_type=F32)


def _bdot(a, b, ca, cb):
    dims = (((ca,), (cb,)), ((0,), (0,)))
    ah, bh = a.astype(BF16), b.astype(BF16)
    al, bl = (a - ah.astype(F32)).astype(BF16), (b - bh.astype(F32)).astype(BF16)
    dot = functools.partial(lax.dot_general, dimension_numbers=dims, preferred_element_type=F32)
    return dot(ah, bh) + dot(al, bh) + dot(ah, bl)


def _bdot1(a, b, ca, cb):
    return lax.dot_general(a.astype(BF16), b.astype(BF16), (((ca,), (cb,)), ((0,), (0,))), preferred_element_type=F32)


def _sigmoid(x):
    return 1.0 / (1.0 + jnp.exp(-x))


def _params(sem, vmem=VMEM_LIMIT):
    return pltpu.CompilerParams(dimension_semantics=sem, vmem_limit_bytes=vmem)


def _tile(n, want):
    t = min(n, want)
    assert n % t == 0, (n, t)
    return t


def _layer_norm(z, gamma, beta):
    mu = jnp.mean(z, axis=-1, keepdims=True)
    zc = z - mu
    var = jnp.mean(zc * zc, axis=-1, keepdims=True)
    return zc * lax.rsqrt(var + LN_EPS) * gamma + beta


def _column_halves(n):
    mid = -(-n // (2 * 128)) * 128
    return ((0, mid), (mid, n))


def _resident(shape):
    return pl.BlockSpec(shape, lambda *_: (0,) * len(shape), pipeline_mode=pl.Buffered(1))


def _group_block(G, want):
    return max(d for d in range(1, max(1, min(G, want)) + 1) if G % d == 0)


def ffn_fwd(xb, x, wg, wu, wd, gamma, beta):
    S, K = xb.shape
    G, N, _ = wg.shape
    ts = _tile(S, 512)

    def body(xb_ref, x_ref, wg_ref, wu_ref, wd_ref, gm_ref, bt_ref, g_ref, u_ref, h_ref, z_ref, xn_ref, xnb_ref):
        j = pl.program_id(1)
        xv = xb_ref[...]
        g = _dot(xv, wg_ref[j], NT)
        u = _dot(xv, wu_ref[j], NT)
        h = (g * _sigmoid(g) * u).astype(BF16)
        g_ref[0] = g.astype(BF16)
        u_ref[0] = u.astype(BF16)
        h_ref[0] = h
        y = _dot(h, wd_ref[j], NN)

        @pl.when(j == 0)
        def _():
            z_ref[...] = y

        @pl.when(j > 0)
        def _():
            z_ref[...] += y

        @pl.when(j == G - 1)
        def _():
            z = ALPHA * x_ref[...] + 0.5 * z_ref[...]
            xn = _layer_norm(z, gm_ref[...], bt_ref[...])
            z_ref[...] = z
            xn_ref[...] = xn
            xnb_ref[...] = xn.astype(BF16)

    row = pl.BlockSpec((ts, K), lambda s, j: (s, 0))
    vec = pl.BlockSpec((1, K), lambda s, j: (0, 0))
    wspec = _resident((G, N, K))
    ospec = pl.BlockSpec((1, ts, N), lambda s, j: (j, s, 0))
    return pl.pallas_call(
        body, name="ffn_fwd", grid=(S // ts, G),
        in_specs=[row, row, wspec, wspec, wspec, vec, vec],
        out_specs=[ospec, ospec, ospec, row, row, row],
        out_shape=[SDS((G, S, N), BF16), SDS((G, S, N), BF16), SDS((G, S, N), BF16),
                   SDS((S, K), F32), SDS((S, K), F32), SDS((S, K), BF16)],
        compiler_params=_params(("parallel", "arbitrary")),
    )(xb, x, wg, wu, wd, gamma, beta)


def proj(xb, w, name):
    S, K = xb.shape
    G, _, N = w.shape
    ts = _tile(S, 1024)
    gb = _group_block(G, 6)

    def body(x_ref, w_ref, y_ref):
        xv = x_ref[...]
        for j in range(gb):
            y_ref[j] = _dot(xv, w_ref[j], NN)

    return pl.pallas_call(
        body, name=name, grid=(S // ts, G // gb),
        in_specs=[pl.BlockSpec((ts, K), lambda s, g: (s, 0)), pl.BlockSpec((gb, K, N), lambda s, g: (g, 0, 0))],
        out_specs=pl.BlockSpec((gb, ts, N), lambda s, g: (g, s, 0)),
        out_shape=SDS((G, S, N), F32),
        compiler_params=_params(("parallel", "parallel")),
    )(xb, w)


def contract_ln(a, w, xres, gamma, beta, scale, name):
    G, S, Kg = a.shape
    N = w.shape[2]
    ts = _tile(S, 512)

    def body(a_ref, w_ref, x_ref, g_ref, b_ref, z_ref, xn_ref, xb_ref):
        acc = _dot(a_ref[0], w_ref[0], NN)
        for j in range(1, G):
            acc = acc + _dot(a_ref[j], w_ref[j], NN)
        z = ALPHA * x_ref[...] + scale * acc
        xn = _layer_norm(z, g_ref[...], b_ref[...])
        z_ref[...] = z
        xn_ref[...] = xn
        xb_ref[...] = xn.astype(BF16)

    row = pl.BlockSpec((ts, N), lambda s: (s, 0))
    vec = pl.BlockSpec((1, N), lambda s: (0, 0))
    return pl.pallas_call(
        body, name=name, grid=(S // ts,),
        in_specs=[pl.BlockSpec((G, ts, Kg), lambda s: (0, s, 0)), pl.BlockSpec((G, Kg, N), lambda s: (0, 0, 0)), row, vec, vec],
        out_specs=[row, row, row],
        out_shape=[SDS((S, N), F32), SDS((S, N), F32), SDS((S, N), BF16)],
        compiler_params=_params(("parallel",)),
    )(a, w, xres, gamma, beta)


def _layer_norm_bwd(dx, z, gamma):
    mu = jnp.mean(z, axis=-1, keepdims=True)
    zc = z - mu
    var = jnp.mean(zc * zc, axis=-1, keepdims=True)
    rstd = lax.rsqrt(var + LN_EPS)
    xhat = zc * rstd
    dxh = dx * gamma
    m1 = jnp.mean(dxh, axis=-1, keepdims=True)
    m2 = jnp.mean(dxh * xhat, axis=-1, keepdims=True)
    return rstd * (dxh - m1 - xhat * m2), jnp.sum(dx * xhat, axis=0, keepdims=True), jnp.sum(dx, axis=0, keepdims=True)


def ffn_bwd(dxn, z, gamma, wd, wg, wu, g1, u1):
    S, K = dxn.shape
    G, N, _ = wd.shape
    ts = _tile(S, 512)

    def body(dxn_ref, z_ref, gm_ref, wd_ref, wg_ref, wu_ref, g_ref, u_ref, dg_ref, du_ref, dx_ref, dy_ref, dgm_ref, dbt_ref):
        s, j = pl.program_id(0), pl.program_id(1)

        @pl.when((s == 0) & (j == 0))
        def _():
            dgm_ref[...] = jnp.zeros_like(dgm_ref)
            dbt_ref[...] = jnp.zeros_like(dbt_ref)

        @pl.when(j == 0)
        def _():
            dz, dgm, dbt = _layer_norm_bwd(dxn_ref[...], z_ref[...], gm_ref[...])
            dgm_ref[...] += dgm
            dbt_ref[...] += dbt
            dx_ref[...] = ALPHA * dz
            dy_ref[...] = (0.5 * dz).astype(BF16)

        dy = dy_ref[...]
        part = None
        for a, b in _column_halves(N):
            dh = _dot(dy, wd_ref[j, a:b, :], NT)
            g = g_ref[0, :, a:b].astype(F32)
            sig = _sigmoid(g)
            dg = (dh * u_ref[0, :, a:b].astype(F32) * (sig * (1.0 + g * (1.0 - sig)))).astype(BF16)
            du = (dh * (g * sig)).astype(BF16)
            dg_ref[0, :, a:b] = dg
            du_ref[0, :, a:b] = du
            p = _dot(dg, wg_ref[j, a:b, :], NN) + _dot(du, wu_ref[j, a:b, :], NN)
            part = p if part is None else part + p
        dx_ref[...] += part

    row = pl.BlockSpec((ts, K), lambda s, j: (s, 0))
    vec = pl.BlockSpec((1, K), lambda s, j: (0, 0))
    gspec = pl.BlockSpec((1, ts, N), lambda s, j: (j, s, 0))
    wspec = _resident((G, N, K))
    return pl.pallas_call(
        body, name="ffn_bwd", grid=(S // ts, G),
        in_specs=[row, row, vec, wspec, wspec, wspec, gspec, gspec],
        out_specs=[gspec, gspec, row, row, vec, vec],
        out_shape=[SDS((G, S, N), BF16), SDS((G, S, N), BF16), SDS((S, K), F32), SDS((S, K), BF16),
                   SDS((1, K), F32), SDS((1, K), F32)],
        compiler_params=_params(("arbitrary", "arbitrary")),
    )(dxn, z, gamma, wd, wg, wu, g1, u1)


def mixer_out_bwd(dxn, z, gamma, w):
    S, N = dxn.shape
    G, Kg, _ = w.shape
    ts = _tile(S, 512)

    def body(dxn_ref, z_ref, gm_ref, w_ref, dm_ref, dz_ref, dzb_ref, dgm_ref, dbt_ref):
        @pl.when(pl.program_id(0) == 0)
        def _():
            dgm_ref[...] = jnp.zeros_like(dgm_ref)
            dbt_ref[...] = jnp.zeros_like(dbt_ref)

        dz, dgm, dbt = _layer_norm_bwd(dxn_ref[...], z_ref[...], gm_ref[...])
        dgm_ref[...] += dgm
        dbt_ref[...] += dbt
        dzb = dz.astype(BF16)
        dz_ref[...] = dz
        dzb_ref[...] = dzb
        for j in range(G):
            dm_ref[j] = _dot(dzb, w_ref[j], NT)

    row = pl.BlockSpec((ts, N), lambda s: (s, 0))
    vec = pl.BlockSpec((1, N), lambda s: (0, 0))
    return pl.pallas_call(
        body, name="mixer_out_bwd", grid=(S // ts,),
        in_specs=[row, row, vec, pl.BlockSpec((G, Kg, N), lambda s: (0, 0, 0))],
        out_specs=[pl.BlockSpec((G, ts, Kg), lambda s: (0, s, 0)), row, row, vec, vec],
        out_shape=[SDS((G, S, Kg), F32), SDS((S, N), F32), SDS((S, N), BF16), SDS((1, N), F32), SDS((1, N), F32)],
        compiler_params=_params(("arbitrary",)),
    )(dxn, z, gamma, w)


def contract_t(da, w, res, name):
    G, S, Ng = da.shape
    K = w.shape[1]
    ts = _tile(S, 512)
    gb = _group_block(G, 6)

    def body(da_ref, w_ref, r_ref, o_ref):
        g = pl.program_id(1)
        part = _dot(da_ref[0], w_ref[0], NT)
        for j in range(1, gb):
            part = part + _dot(da_ref[j], w_ref[j], NT)

        @pl.when(g == 0)
        def _():
            o_ref[...] = ALPHA * r_ref[...] + part

        @pl.when(g > 0)
        def _():
            o_ref[...] += part

    row = pl.BlockSpec((ts, K), lambda s, g: (s, 0))
    return pl.pallas_call(
        body, name=name, grid=(S // ts, G // gb),
        in_specs=[pl.BlockSpec((gb, ts, Ng), lambda s, g: (g, s, 0)), pl.BlockSpec((gb, K, Ng), lambda s, g: (g, 0, 0)), row],
        out_specs=row,
        out_shape=SDS((S, K), F32),
        compiler_params=_params(("parallel", "arbitrary")),
    )(da, w, res)


WGRAD_ACC_ELEMS = 6 * 1024 * 256


def wgrad(a, b, out_dtype, name):
    ga, gb = a.ndim == 3, b.ndim == 3
    G = a.shape[0] if ga else b.shape[0]
    S, K = a.shape[-2:]
    N = b.shape[-1]
    ts = _tile(S, 2048)
    ns = S // ts
    ng = _group_block(G, WGRAD_ACC_ELEMS // (K * N))

    def body(a_ref, b_ref, o_ref, acc):
        s = pl.program_id(1)

        @pl.when(s == 0)
        def _():
            acc[...] = jnp.zeros_like(acc)

        for j in range(ng):
            acc[j] += _dot(a_ref[j] if ga else a_ref[...], b_ref[j] if gb else b_ref[...], TN)

        @pl.when(s == ns - 1)
        def _():
            o_ref[...] = acc[...].astype(out_dtype)

    aspec = pl.BlockSpec((ng, ts, K), lambda g, s: (g, s, 0)) if ga else pl.BlockSpec((ts, K), lambda g, s: (s, 0))
    bspec = pl.BlockSpec((ng, ts, N), lambda g, s: (g, s, 0)) if gb else pl.BlockSpec((ts, N), lambda g, s: (s, 0))
    return pl.pallas_call(
        body, name=name, grid=(G // ng, ns),
        in_specs=[aspec, bspec],
        out_specs=pl.BlockSpec((ng, K, N), lambda g, s: (g, 0, 0)),
        out_shape=SDS((G, K, N), out_dtype),
        scratch_shapes=[pltpu.VMEM((ng, K, N), F32)],
        compiler_params=_params(("parallel", "arbitrary")),
    )(a, b)


def loss_grad(xn, tgt):
    S, N = xn.shape
    ts = _tile(S, 512)

    def body(x_ref, t_ref, l_ref, dx_ref):
        @pl.when(pl.program_id(0) == 0)
        def _():
            l_ref[...] = jnp.zeros_like(l_ref)

        e = x_ref[...] - t_ref[...]
        dx_ref[...] = e * (1.0 / N)
        l_ref[...] += 0.5 * jnp.sum(jnp.mean(e * e, axis=-1, keepdims=True), axis=0, keepdims=True)

    row = pl.BlockSpec((ts, N), lambda s: (s, 0))
    return pl.pallas_call(
        body, name="loss_grad", grid=(S // ts,),
        in_specs=[row, row],
        out_specs=[pl.BlockSpec((1, 1), lambda s: (0, 0)), row],
        out_shape=[SDS((1, 1), F32), SDS((S, N), F32)],
        compiler_params=_params(("arbitrary",)),
    )(xn, tgt)


def _shift_down(x, k):
    if k == 0:
        return x
    rows = lax.broadcasted_iota(jnp.int32, x.shape, 0)
    return jnp.where(rows >= k, pltpu.roll(x, k, 0), 0.0)


def _shift_up(x, k):
    if k == 0:
        return x
    n = x.shape[0]
    rows = lax.broadcasted_iota(jnp.int32, x.shape, 0)
    return jnp.where(rows < n - k, pltpu.roll(x, n - k, 0), 0.0)


LANES = 128


def conv_mixer_fwd(u, cw):
    _, S, _ = u.shape
    nh = GROUP // LANES

    def body(b_ref, c_ref, x_ref, w_ref, o_ref):
        p = c_ref[0] * x_ref[0]
        w = w_ref[0]
        conv = w[2:3] * p + w[1:2] * _shift_down(p, 1) + w[0:1] * _shift_down(p, 2)
        o_ref[0] = (b_ref[0] * conv).astype(BF16)

    def uspec(off):
        return pl.BlockSpec((1, S, LANES), lambda g, h: (g + off, 0, h))

    return pl.pallas_call(
        body, name="conv_mixer_fwd", grid=(3, nh),
        in_specs=[uspec(0), uspec(3), uspec(6), pl.BlockSpec((1, 8, LANES), lambda g, h: (g, 0, h))],
        out_specs=pl.BlockSpec((1, S, LANES), lambda g, h: (g, 0, h)),
        out_shape=SDS((3, S, GROUP), BF16),
        compiler_params=_params(("parallel", "parallel")),
    )(u, u, u, cw)


def conv_mixer_bwd(u, cw, dm):
    _, S, _ = u.shape
    nh = GROUP // LANES

    def body(b_ref, c_ref, x_ref, w_ref, d_ref, db_ref, dc_ref, dx_ref, dw_ref):
        cg, xi = c_ref[0], x_ref[0]
        p = cg * xi
        p1, p2 = _shift_down(p, 1), _shift_down(p, 2)
        w = w_ref[0]
        conv = w[2:3] * p + w[1:2] * p1 + w[0:1] * p2
        dt = d_ref[0]
        db_ref[0] = (dt * conv).astype(BF16)
        dcv = dt * b_ref[0]
        dp = w[2:3] * dcv + w[1:2] * _shift_up(dcv, 1) + w[0:1] * _shift_up(dcv, 2)
        dc_ref[0] = (dp * xi).astype(BF16)
        dx_ref[0] = (dp * cg).astype(BF16)
        dw = jnp.concatenate([jnp.sum(dcv * p2, axis=0, keepdims=True), jnp.sum(dcv * p1, axis=0, keepdims=True),
                              jnp.sum(dcv * p, axis=0, keepdims=True), jnp.zeros((5, LANES), F32)], axis=0)
        dw_ref[0] = dw

    def uspec(off):
        return pl.BlockSpec((1, S, LANES), lambda g, h: (g + off, 0, h))

    ospec = pl.BlockSpec((1, S, LANES), lambda g, h: (g, 0, h))
    wspec = pl.BlockSpec((1, 8, LANES), lambda g, h: (g, 0, h))
    return pl.pallas_call(
        body, name="conv_mixer_bwd", grid=(3, nh),
        in_specs=[uspec(0), uspec(3), uspec(6), wspec, ospec],
        out_specs=[ospec, ospec, ospec, wspec],
        out_shape=[SDS((3, S, GROUP), BF16)] * 3 + [SDS((3, 8, GROUP), F32)],
        compiler_params=_params(("parallel", "parallel")),
    )(u, u, u, cw, dm)


def qk_conv_fwd(u, qw):
    _, S, _ = u.shape
    nh = GROUP // LANES

    def body(u_ref, w_ref, o_ref):
        x = u_ref[0]
        w = w_ref[0]
        pre = w[3:4] * x + w[2:3] * _shift_down(x, 1) + w[1:2] * _shift_down(x, 2) + w[0:1] * _shift_down(x, 3)
        o_ref[0] = pre * _sigmoid(pre)

    spec = pl.BlockSpec((1, S, LANES), lambda g, h: (g, 0, h))
    return pl.pallas_call(
        body, name="qk_conv_fwd", grid=(8, nh),
        in_specs=[spec, pl.BlockSpec((1, 8, LANES), lambda g, h: (g, 0, h))],
        out_specs=spec,
        out_shape=SDS((8, S, GROUP), F32),
        compiler_params=_params(("parallel", "parallel")),
    )(u, qw)


def qk_conv_bwd(u, qw, dqk):
    _, S, _ = u.shape
    nh = GROUP // LANES

    def body(u_ref, w_ref, d_ref, du_ref, dw_ref):
        x = u_ref[0]
        w = w_ref[0]
        x1, x2, x3 = _shift_down(x, 1), _shift_down(x, 2), _shift_down(x, 3)
        pre = w[3:4] * x + w[2:3] * x1 + w[1:2] * x2 + w[0:1] * x3
        sig = _sigmoid(pre)
        dpre = d_ref[0] * (sig * (1.0 + pre * (1.0 - sig)))
        du = w[3:4] * dpre + w[2:3] * _shift_up(dpre, 1) + w[1:2] * _shift_up(dpre, 2) + w[0:1] * _shift_up(dpre, 3)
        du_ref[0] = du.astype(BF16)
        dw = jnp.concatenate([jnp.sum(dpre * x3, axis=0, keepdims=True), jnp.sum(dpre * x2, axis=0, keepdims=True),
                              jnp.sum(dpre * x1, axis=0, keepdims=True), jnp.sum(dpre * x, axis=0, keepdims=True),
                              jnp.zeros((4, LANES), F32)], axis=0)
        dw_ref[0] = dw

    spec = pl.BlockSpec((1, S, LANES), lambda g, h: (g, 0, h))
    wspec = pl.BlockSpec((1, 8, LANES), lambda g, h: (g, 0, h))
    return pl.pallas_call(
        body, name="qk_conv_bwd", grid=(8, nh),
        in_specs=[spec, wspec, spec],
        out_specs=[spec, wspec],
        out_shape=[SDS((8, S, GROUP), BF16), SDS((8, 8, GROUP), F32)],
        compiler_params=_params(("parallel", "parallel")),
    )(u, qw, dqk)


def _head_masks():
    lane = lax.broadcasted_iota(jnp.int32, (1, D_XA), 1)
    return [(lane >= h * XA_HEAD_DIM) & (lane < (h + 1) * XA_HEAD_DIM) for h in range(XA_HEADS)]


def xattn_fwd(u, qg, kv):
    _, S, _ = u.shape
    ts = _tile(S, 512)
    scale = XA_HEAD_DIM ** -0.5

    def body(q_ref, kv_ref, o_ref):
        q = q_ref[0]
        k = kv_ref[0].astype(BF16)
        v = kv_ref[1]
        o = jnp.zeros((ts, D_XA), F32)
        for m in _head_masks():
            s = _dot(jnp.where(m, q, 0.0).astype(BF16), k, NT) * scale
            s = s - jnp.max(s, axis=-1, keepdims=True)
            e = jnp.exp(s)
            p = e / jnp.sum(e, axis=-1, keepdims=True)
            o = o + _dot(p.astype(BF16), jnp.where(m, v, 0.0).astype(BF16), NN)
        o_ref[0] = o.astype(BF16)

    return pl.pallas_call(
        body, name="xattn_fwd", grid=(S // ts,),
        in_specs=[pl.BlockSpec((1, ts, GROUP), lambda s: (qg, s, 0)), pl.BlockSpec((2, N_MEM, GROUP), lambda s: (0, 0, 0))],
        out_specs=pl.BlockSpec((1, ts, GROUP), lambda s: (0, s, 0)),
        out_shape=SDS((1, S, GROUP), BF16),
        compiler_params=_params(("parallel",)),
    )(u, kv)


def xattn_bwd(u, qg, kv, dm, dg):
    _, S, _ = u.shape
    ts = _tile(S, 512)
    scale = XA_HEAD_DIM ** -0.5

    def body(q_ref, kv_ref, do_ref, dq_ref, dkv_ref):
        @pl.when(pl.program_id(0) == 0)
        def _():
            dkv_ref[...] = jnp.zeros_like(dkv_ref)

        q = q_ref[0]
        k = kv_ref[0]
        v = kv_ref[1]
        kb = k.astype(BF16)
        do = do_ref[0]
        dq = jnp.zeros((ts, D_XA), F32)
        dk = jnp.zeros((N_MEM, D_XA), F32)
        dv = jnp.zeros((N_MEM, D_XA), F32)
        for m in _head_masks():
            qm = jnp.where(m, q, 0.0).astype(BF16)
            s = _dot(qm, kb, NT) * scale
            s = s - jnp.max(s, axis=-1, keepdims=True)
            e = jnp.exp(s)
            p = e / jnp.sum(e, axis=-1, keepdims=True)
            dom = jnp.where(m, do, 0.0).astype(BF16)
            dp = _dot(dom, jnp.where(m, v, 0.0).astype(BF16), NT)
            ds = (p * (dp - jnp.sum(dp * p, axis=-1, keepdims=True)) * scale).astype(BF16)
            dq = dq + _dot(ds, jnp.where(m, k, 0.0).astype(BF16), NN)
            dk = dk + _dot(ds, qm, TN)
            dv = dv + _dot(p.astype(BF16), dom, TN)
        dq_ref[0] = dq.astype(BF16)
        dkv_ref[0] += dk
        dkv_ref[1] += dv

    return pl.pallas_call(
        body, name="xattn_bwd", grid=(S // ts,),
        in_specs=[pl.BlockSpec((1, ts, GROUP), lambda s: (qg, s, 0)), pl.BlockSpec((2, N_MEM, GROUP), lambda s: (0, 0, 0)),
                  pl.BlockSpec((1, ts, GROUP), lambda s: (dg, s, 0))],
        out_specs=[pl.BlockSpec((1, ts, GROUP), lambda s: (0, s, 0)), pl.BlockSpec((2, N_MEM, GROUP), lambda s: (0, 0, 0))],
        out_shape=[SDS((1, S, GROUP), BF16), SDS((2, N_MEM, GROUP), F32)],
        compiler_params=_params(("arbitrary",)),
    )(u, kv, dm)


ML_BLOCK_CHUNKS = 4
H4 = ML_HEADS
L = ML_CHUNK
NLANE = ML_HEAD_DIM


def _chunk_consts():
    r = lax.broadcasted_iota(jnp.int32, (1, L, L), 1)
    c = lax.broadcasted_iota(jnp.int32, (1, L, L), 2)
    return r >= c, r <= c, r == c


def _gate_cols(gb):
    lane = lax.broadcasted_iota(jnp.int32, gb.shape, 1)
    li = jnp.stack([jnp.sum(jnp.where(lane == h, gb, 0.0), axis=1, keepdims=True) for h in range(H4)])
    gf = jnp.stack([jnp.sum(jnp.where(lane == H4 + h, gb, 0.0), axis=1, keepdims=True) for h in range(H4)])
    return li, gf


def _log_sigmoid(x):
    return jnp.minimum(x, 0.0) - jnp.log(1.0 + jnp.exp(-jnp.abs(x)))


def _chunk_forward(q, k, v_aug, li_col, lf_col, c_prev, m_prev):
    tri, tri_t, eye = _chunk_consts()
    lf_row = jnp.sum(jnp.where(eye, lf_col, 0.0), axis=1, keepdims=True)
    li_row = jnp.sum(jnp.where(eye, li_col, 0.0), axis=1, keepdims=True)
    bcum_col = jnp.sum(jnp.where(tri, lf_row, 0.0), axis=2, keepdims=True)
    bcum_row = jnp.sum(jnp.where(tri_t, lf_col, 0.0), axis=1, keepdims=True)
    log_d = jnp.where(tri, bcum_col - bcum_row + li_row, NEG)
    log_inter = bcum_col + m_prev
    m_t = jnp.maximum(log_inter, jnp.max(log_d, axis=2, keepdims=True))
    w_intra = jnp.exp(log_d - m_t)
    w_inter = jnp.exp(log_inter - m_t)
    sc = _bdot(q, k, 2, 2) * w_intra
    qc = _bdot1(q, c_prev, 2, 1)
    num = _bdot(sc, v_aug, 2, 1) + w_inter * qc
    lane = lax.broadcasted_iota(jnp.int32, num.shape, 2)
    den = jnp.sum(jnp.where(lane == NLANE, num, 0.0), axis=2, keepdims=True)
    e_m = jnp.exp(-m_t)
    b_last = jnp.sum(lf_row, axis=2, keepdims=True)
    log_w = b_last - bcum_col + li_col
    m_new = jnp.maximum(b_last + m_prev, jnp.max(log_w, axis=1, keepdims=True))
    w_k = jnp.exp(log_w - m_new)
    decay = jnp.exp(b_last + m_prev - m_new)
    return dict(w_intra=w_intra, w_inter=w_inter, sc=sc, qc=qc, num=num, den=den, e_m=e_m, lane=lane,
                w_k=w_k, decay=decay, m_new=m_new)


def mlstm_fwd(qk, u, bg):
    _, S, _ = qk.shape
    nc = S // L
    cb = min(ML_BLOCK_CHUNKS, nc)
    rows = cb * L
    kscale = ML_HEAD_DIM ** -0.5

    def body(qk_ref, v_ref, g_ref, bg_ref, h_ref, cst_ref, mst_ref, c_sc, m_sc):
        @pl.when(pl.program_id(0) == 0)
        def _():
            c_sc[...] = jnp.zeros_like(c_sc)
            m_sc[...] = jnp.zeros_like(m_sc)

        for c in range(cb):
            sl = pl.ds(c * L, L)
            q = qk_ref[0:H4, sl, :]
            k = qk_ref[H4:2 * H4, sl, :] * kscale
            v = v_ref[:, sl, :]
            lane = lax.broadcasted_iota(jnp.int32, v.shape, 2)
            v_aug = jnp.where(lane == NLANE, 1.0, v)
            li_col, gf = _gate_cols(g_ref[0, sl, :] + bg_ref[...])
            lf_col = _log_sigmoid(gf)
            c_prev = c_sc[...]
            m_prev = m_sc[...]
            f = _chunk_forward(q, k, v_aug, li_col, lf_col, c_prev, m_prev)
            r = 1.0 / jnp.maximum(jnp.abs(f["den"]), f["e_m"])
            h_ref[:, sl, :] = jnp.where(lane < NLANE, f["num"] * r, 0.0)
            cst_ref[c] = c_prev
            mst_ref[c] = jnp.broadcast_to(m_prev, (H4, 1, LANES))
            c_sc[...] = f["decay"] * c_prev + _bdot(k * f["w_k"], v_aug, 1, 1)
            m_sc[...] = f["m_new"]

    def hspec(blk):
        return pl.BlockSpec((H4, rows, GROUP), lambda i: (blk, i, 0))

    return pl.pallas_call(
        body, name="mlstm_fwd", grid=(nc // cb,),
        in_specs=[pl.BlockSpec((2 * H4, rows, GROUP), lambda i: (0, i, 0)), hspec(2),
                  pl.BlockSpec((1, rows, GROUP), lambda i: (17, i, 0)), pl.BlockSpec((1, GROUP), lambda i: (0, 0))],
        out_specs=[hspec(0), pl.BlockSpec((cb, H4, GROUP, GROUP), lambda i: (i, 0, 0, 0)),
                   pl.BlockSpec((cb, H4, 1, LANES), lambda i: (i, 0, 0, 0))],
        out_shape=[SDS((H4, S, GROUP), F32), SDS((nc, H4, GROUP, GROUP), F32), SDS((nc, H4, 1, LANES), F32)],
        scratch_shapes=[pltpu.VMEM((H4, GROUP, GROUP), F32), pltpu.VMEM((H4, 1, 1), F32)],
        compiler_params=_params(("arbitrary",)),
    )(qk, u, u, bg)


def mlstm_bwd(qk, u, bg, cst, mst, dh):
    _, S, _ = qk.shape
    nc = S // L
    cb = min(ML_BLOCK_CHUNKS, nc)
    rows = cb * L
    nb = nc // cb
    kscale = ML_HEAD_DIM ** -0.5

    def body(qk_ref, v_ref, g_ref, bg_ref, cst_ref, mst_ref, dh_ref, dqk_ref, dv_ref, dg_ref, dbg_ref, dc_sc):
        @pl.when(pl.program_id(0) == 0)
        def _():
            dc_sc[...] = jnp.zeros_like(dc_sc)
            dbg_ref[...] = jnp.zeros_like(dbg_ref)

        tri, tri_t, eye = _chunk_consts()
        for c in reversed(range(cb)):
            sl = pl.ds(c * L, L)
            q = qk_ref[0:H4, sl, :]
            k = qk_ref[H4:2 * H4, sl, :] * kscale
            v = v_ref[:, sl, :]
            lane = lax.broadcasted_iota(jnp.int32, v.shape, 2)
            v_aug = jnp.where(lane == NLANE, 1.0, v)
            li_col, gf = _gate_cols(g_ref[0, sl, :] + bg_ref[...])
            lf_col = _log_sigmoid(gf)
            c_prev = cst_ref[c]
            m_prev = mst_ref[c][:, :, 0:1]
            f = _chunk_forward(q, k, v_aug, li_col, lf_col, c_prev, m_prev)
            w_intra, w_inter, sc, num, den, e_m = f["w_intra"], f["w_inter"], f["sc"], f["num"], f["den"], f["e_m"]
            absd = jnp.abs(den)
            r = 1.0 / jnp.maximum(absd, e_m)
            dhv = dh_ref[:, sl, :]
            s1 = jnp.sum(jnp.where(lane < NLANE, dhv * num, 0.0), axis=2, keepdims=True)
            dden = jnp.where(absd > e_m, -s1 * r * r * jnp.sign(den), 0.0)
            dnum = jnp.where(lane == NLANE, dden, jnp.where(lane < NLANE, dhv * r, 0.0))
            dsc = _bdot1(dnum, v_aug, 2, 2)
            dv = _bdot1(sc, dnum, 1, 1)
            gmat = dsc * sc
            dqk = dsc * w_intra
            dq = _bdot1(dqk, k, 2, 1) + w_inter * _bdot1(dnum, c_prev, 2, 2)
            dk = _bdot1(dqk, q, 1, 1)
            dc_prev = _bdot(q * w_inter, dnum, 1, 1)
            dlog_inter = jnp.sum(dnum * f["qc"], axis=2, keepdims=True) * w_inter
            dbcum_col = dlog_inter + jnp.sum(gmat, axis=2, keepdims=True)
            g_row = jnp.sum(gmat, axis=1, keepdims=True)
            dcn = dc_sc[...]
            w_k, decay = f["w_k"], f["decay"]
            kw = k * w_k
            dc_prev = dc_prev + decay * dcn
            db_last = jnp.sum(jnp.sum(dcn * c_prev, axis=2, keepdims=True), axis=1, keepdims=True) * decay
            dkw = _bdot(v_aug, dcn, 2, 2)
            dv = dv + _bdot1(kw, dcn, 2, 1)
            dk = dk + dkw * w_k
            dlogw = jnp.sum(dkw * k, axis=2, keepdims=True) * w_k
            db_last = db_last + jnp.sum(dlogw, axis=1, keepdims=True)
            dbcum_col = dbcum_col - dlogw
            rowi = lax.broadcasted_iota(jnp.int32, (1, L, 1), 1)
            dbcum_col = dbcum_col + jnp.where(rowi == L - 1, db_last, 0.0)
            dbcum_row = jnp.sum(jnp.where(eye, dbcum_col, 0.0), axis=1, keepdims=True) - g_row
            dlf_col = jnp.sum(jnp.where(tri_t, dbcum_row, 0.0), axis=2, keepdims=True)
            dli_col = dlogw + jnp.sum(jnp.where(eye, g_row, 0.0), axis=2, keepdims=True)
            dgf_col = dlf_col * _sigmoid(-gf)
            lane_g = lax.broadcasted_iota(jnp.int32, (L, GROUP), 1)
            dg = jnp.zeros((L, GROUP), F32)
            for h in range(H4):
                dg = dg + jnp.where(lane_g == h, dli_col[h], 0.0) + jnp.where(lane_g == H4 + h, dgf_col[h], 0.0)
            dqk_ref[0:H4, sl, :] = dq
            dqk_ref[H4:2 * H4, sl, :] = dk * kscale
            dv_ref[:, sl, :] = jnp.where(lane < NLANE, dv, 0.0).astype(BF16)
            dg_ref[0, sl, :] = dg.astype(BF16)
            dbg_ref[...] += jnp.sum(dg, axis=0, keepdims=True)
            dc_sc[...] = dc_prev

    def hspec(blk):
        return pl.BlockSpec((H4, rows, GROUP), lambda i: (blk, nb - 1 - i, 0))

    gspec = pl.BlockSpec((1, rows, GROUP), lambda i: (17, nb - 1 - i, 0))
    qkspec = pl.BlockSpec((2 * H4, rows, GROUP), lambda i: (0, nb - 1 - i, 0))
    return pl.pallas_call(
        body, name="mlstm_bwd", grid=(nb,),
        in_specs=[qkspec, hspec(2), gspec, pl.BlockSpec((1, GROUP), lambda i: (0, 0)),
                  pl.BlockSpec((cb, H4, GROUP, GROUP), lambda i: (nb - 1 - i, 0, 0, 0)),
                  pl.BlockSpec((cb, H4, 1, LANES), lambda i: (nb - 1 - i, 0, 0, 0)), hspec(0)],
        out_specs=[qkspec, hspec(0), pl.BlockSpec((1, rows, GROUP), lambda i: (0, nb - 1 - i, 0)),
                   pl.BlockSpec((1, GROUP), lambda i: (0, 0))],
        out_shape=[SDS((2 * H4, S, GROUP), F32), SDS((H4, S, GROUP), BF16),
                   SDS((1, S, GROUP), BF16), SDS((1, GROUP), F32)],
        scratch_shapes=[pltpu.VMEM((H4, GROUP, GROUP), F32)],
        compiler_params=_params(("arbitrary",)),
    )(qk, u, u, bg, cst, mst, dh)


def head_norm_fwd(hm, u, hg):
    _, S, _ = hm.shape
    ts = _tile(S, 512)

    def body(h_ref, o_ref, g_ref, t_ref):
        h = h_ref[0]
        lane = lax.broadcasted_iota(jnp.int32, h.shape, 1)
        valid = lane < ML_HEAD_DIM
        mu = jnp.sum(h, axis=-1, keepdims=True) * (1.0 / ML_HEAD_DIM)
        hc = jnp.where(valid, h - mu, 0.0)
        var = jnp.sum(hc * hc, axis=-1, keepdims=True) * (1.0 / ML_HEAD_DIM)
        hn = hc * lax.rsqrt(var + LN_EPS) * g_ref[0]
        t_ref[0] = (_sigmoid(o_ref[0]) * hn).astype(BF16)

    return pl.pallas_call(
        body, name="head_norm_fwd", grid=(H4, S // ts),
        in_specs=[pl.BlockSpec((1, ts, GROUP), lambda h, s: (h, s, 0)), pl.BlockSpec((1, ts, GROUP), lambda h, s: (12 + h, s, 0)),
                  pl.BlockSpec((1, 1, GROUP), lambda h, s: (h, 0, 0))],
        out_specs=pl.BlockSpec((1, ts, GROUP), lambda h, s: (h, s, 0)),
        out_shape=SDS((H4, S, GROUP), BF16),
        compiler_params=_params(("parallel", "parallel")),
    )(hm, u, hg)


def head_norm_bwd(hm, u, hg, dm):
    _, S, _ = hm.shape
    ts = _tile(S, 512)

    def body(h_ref, o_ref, g_ref, d_ref, dh_ref, do_ref, dg_ref):
        @pl.when(pl.program_id(1) == 0)
        def _():
            dg_ref[...] = jnp.zeros_like(dg_ref)

        h = h_ref[0]
        lane = lax.broadcasted_iota(jnp.int32, h.shape, 1)
        valid = lane < ML_HEAD_DIM
        inv = 1.0 / ML_HEAD_DIM
        mu = jnp.sum(h, axis=-1, keepdims=True) * inv
        hc = jnp.where(valid, h - mu, 0.0)
        var = jnp.sum(hc * hc, axis=-1, keepdims=True) * inv
        rstd = lax.rsqrt(var + LN_EPS)
        xhat = hc * rstd
        g = g_ref[0]
        sig = _sigmoid(o_ref[0])
        dt = jnp.where(valid, d_ref[0], 0.0)
        do_ref[0] = (dt * xhat * g * sig * (1.0 - sig)).astype(BF16)
        dhn = dt * sig
        dg_ref[0] += jnp.sum(dhn * xhat, axis=0, keepdims=True)
        dxh = dhn * g
        m1 = jnp.sum(dxh, axis=-1, keepdims=True) * inv
        m2 = jnp.sum(dxh * xhat, axis=-1, keepdims=True) * inv
        dh_ref[0] = jnp.where(valid, rstd * (dxh - m1 - xhat * m2), 0.0)

    spec = pl.BlockSpec((1, ts, GROUP), lambda h, s: (h, s, 0))
    gspec = pl.BlockSpec((1, 1, GROUP), lambda h, s: (h, 0, 0))
    return pl.pallas_call(
        body, name="head_norm_bwd", grid=(H4, S // ts),
        in_specs=[spec, pl.BlockSpec((1, ts, GROUP), lambda h, s: (12 + h, s, 0)), gspec, spec],
        out_specs=[spec, spec, gspec],
        out_shape=[SDS((H4, S, GROUP), F32), SDS((H4, S, GROUP), BF16), SDS((H4, 1, GROUP), F32)],
        compiler_params=_params(("parallel", "arbitrary")),
    )(hm, u, hg, dm)


def _adamw_math(w, g, m, v):
    c1 = 1.0 / (1.0 - ADAM_B1 ** ADAM_STEP)
    c2 = 1.0 / (1.0 - ADAM_B2 ** ADAM_STEP)
    nm = ADAM_B1 * m + (1.0 - ADAM_B1) * g
    nv = ADAM_B2 * v + (1.0 - ADAM_B2) * (g * g)
    return -ADAM_LR * ((nm * c1) / (jnp.sqrt(nv * c2) + ADAM_EPS) + ADAM_WD * w), nm, nv


def _row_tile(R, cap=512):
    return R if R <= cap else max(d for d in range(8, cap + 1, 8) if R % d == 0)


def adamw_into(w, m, v, g, outs, idx, after, name):
    R, C = g.shape
    tr = _row_tile(R)
    lead = (0,) * len(idx)

    def body(w_ref, m_ref, v_ref, g_ref, *rest):
        go_ref, d_ref, nm_ref, nv_ref, token = rest[-5:]
        token[...] = jnp.zeros_like(token)
        gv = g_ref[...]
        d, nm, nv = _adamw_math(w_ref[lead], gv, m_ref[lead], v_ref[lead])
        go_ref[lead] = gv
        d_ref[lead] = d
        nm_ref[lead] = nm
        nv_ref[lead] = nv

    blk = pl.BlockSpec((1,) * len(idx) + (tr, C), lambda r: idx + (r, 0))
    any_space = pl.BlockSpec(memory_space=pl.ANY)
    in_specs, args, aliases = [blk, blk, blk, pl.BlockSpec((tr, C), lambda r: (r, 0)), any_space], [w, m, v, g, g if after is None else after], {}
    if outs is not None:
        in_specs += [any_space] * 4
        args += list(outs)
        aliases = {5 + i: i for i in range(4)}
    out = pl.pallas_call(
        body, name=name, grid=(R // tr,),
        in_specs=in_specs, out_specs=[blk] * 4 + [pl.BlockSpec((8, LANES), lambda r: (0, 0))],
        out_shape=[SDS(w.shape, F32)] * 4 + [SDS((8, LANES), F32)],
        input_output_aliases=aliases, compiler_params=_params(("arbitrary",)),
    )(*args)
    return out[:4], out[4]


def adamw(w, g, m, v, name):
    R, C = w.shape
    tr = _row_tile(R)

    def body(w_ref, g_ref, m_ref, v_ref, d_ref, nm_ref, nv_ref):
        d_ref[...], nm_ref[...], nv_ref[...] = _adamw_math(w_ref[...], g_ref[...], m_ref[...], v_ref[...])

    spec = pl.BlockSpec((tr, C), lambda i: (i, 0))
    return pl.pallas_call(
        body, name=name, grid=(R // tr,),
        in_specs=[spec] * 4, out_specs=[spec] * 3,
        out_shape=[SDS((R, C), F32)] * 3,
        compiler_params=_params(("parallel",)),
    )(w, g, m, v)


HBM = pl.BlockSpec(memory_space=pl.ANY)
ROW_SPLIT = 4
PAIR_SPLIT = 1


def _position():
    x, y, c = lax.axis_index("x"), lax.axis_index("y"), lax.axis_index("c")
    return x, y, c, [(1 - x, y), (x, 1 - y), (1 - x, 1 - y)]


def _unique(items):
    arrays = []
    for a, _ in items:
        if not any(a is b for b in arrays):
            arrays.append(a)
    return arrays, [next(i for i, b in enumerate(arrays) if b is a) for a, _ in items]


def place_own(items, me, after, name):
    arrays, src_of = _unique(items)
    n = len(items)
    shapes = [a.shape[len(p):] for a, p in items]

    def body(me_ref, *refs):
        for t in range(n):
            refs[n + 1 + t][0] = refs[t][(0,) * len(items[t][1])]

    in_specs, out_specs = [], []
    for (a, p), shp in zip(items, shapes):
        blk = shp[:-2] + (shp[-2] // ROW_SPLIT, shp[-1])
        lead = (0,) * (len(shp) - 2)
        in_specs.append(pl.BlockSpec((1,) * len(p) + blk, functools.partial(lambda r, me_ref, p, lead: p + lead + (r, 0), p=p, lead=lead)))
        out_specs.append(pl.BlockSpec((1,) + blk, functools.partial(lambda r, me_ref, lead: (me_ref[0],) + lead + (r, 0), lead=lead)))
    in_specs.append(pl.BlockSpec(memory_space=pl.ANY))
    return pl.pallas_call(
        body, name=name,
        grid_spec=pltpu.PrefetchScalarGridSpec(num_scalar_prefetch=1, grid=(ROW_SPLIT,), in_specs=in_specs, out_specs=out_specs),
        out_shape=[SDS((N_CHIPS,) + tuple(shp), a.dtype) for shp, (a, _) in zip(shapes, items)],
        compiler_params=_params(("parallel",)),
    )(me, *[arrays[i] for i in src_of], after)


SEM = pl.BlockSpec(memory_space=pltpu.SEMAPHORE)
IN_HBM = pl.BlockSpec(memory_space=pltpu.HBM)
DATAFLOW = pltpu.SideEffectType.DATAFLOW_SIDE_EFFECTING


def split_start(bufs, plan, n_copies, after, name):
    n = len(bufs)

    def body(*refs):
        send, recv, token = refs[n + 1], refs[n + 2], refs[-1]
        x, y, c, chips = _position()
        for k, (src, dst, dev) in enumerate(plan(refs[:n], x, y, c, chips)):
            pltpu.make_async_remote_copy(src_ref=src, dst_ref=dst, send_sem=send.at[k], recv_sem=recv.at[k],
                                         device_id=dev, device_id_type=MESH).start()
        token[...] = jnp.zeros_like(token)

    out = pl.pallas_call(
        body, name=name,
        out_shape=(pltpu.SemaphoreType.DMA((n_copies,)), pltpu.SemaphoreType.DMA((n_copies,)),
                   *[pltpu.HBM(b.shape, b.dtype) for b in bufs], SDS((8, LANES), F32)),
        in_specs=[IN_HBM] * n + [pl.BlockSpec(memory_space=pl.ANY)],
        out_specs=(SEM, SEM, *[IN_HBM] * n, pl.BlockSpec(memory_space=pltpu.VMEM)),
        input_output_aliases={i: 2 + i for i in range(n)},
        compiler_params=pltpu.CompilerParams(has_side_effects=DATAFLOW),
    )(*[pltpu.with_memory_space_constraint(b, pltpu.HBM) for b in bufs], after)
    return out[0], out[1], list(out[2:2 + n]), out[-1]


def split_wait(send, recv, bufs, plan, after, name):
    n = len(bufs)

    def body(*refs):
        send_ref, recv_ref = refs[n], refs[n + 1]
        x, y, c, chips = _position()
        for k, (src, dst, dev) in enumerate(plan(refs[:n], x, y, c, chips)):
            cp = pltpu.make_async_remote_copy(src_ref=src, dst_ref=dst, send_sem=send_ref.at[k], recv_sem=recv_ref.at[k],
                                              device_id=dev, device_id_type=MESH)
            cp.wait_send()
            cp.wait_recv()

    return list(pl.pallas_call(
        body, name=name, out_shape=tuple(pltpu.HBM(b.shape, b.dtype) for b in bufs),
        in_specs=[IN_HBM] * n + [SEM, SEM, pl.BlockSpec(memory_space=pl.ANY)], out_specs=tuple([IN_HBM] * n),
        input_output_aliases={i: i for i in range(n)},
        compiler_params=pltpu.CompilerParams(has_side_effects=DATAFLOW),
    )(*bufs, send, recv, after))


def _gather_plan(shapes, landing):
    n = len(shapes)

    def plan(refs, x, y, c, chips):
        out = []
        for t in range(n):
            half = shapes[t][0] // 2
            rows = pl.ds(c * half, half)
            for cx, cy in chips:
                slot = 2 * cx + cy if landing else 2 * x + y
                out.append((refs[t].at[rows], refs[n + t].at[slot, rows], (cx, cy, c)))
        return out

    return plan


def gather_start(shards, placed, after, name):
    shapes = [s.shape for s in shards]
    send, recv, bufs, token = split_start(list(shards) + list(placed), _gather_plan(shapes, False), 3 * len(shards), after, name)
    return (send, recv, bufs, shapes), token


def gather_wait(state, after, name):
    send, recv, bufs, shapes = state
    return split_wait(send, recv, bufs, _gather_plan(shapes, True), after, name)[len(shapes):]


def gather_pass_on(placed, shapes, name):
    n = len(placed)

    def body(*refs):
        outs, send, recv = refs[n:2 * n], refs[2 * n], refs[2 * n + 1]
        x, y, c, chips = _position()
        cps = []
        for t in range(n):
            half = shapes[t][0] // 2
            for j, (cx, cy) in enumerate(chips):
                piece = outs[t].at[2 * cx + cy, pl.ds(c * half, half)]
                cp = pltpu.make_async_remote_copy(src_ref=piece, dst_ref=piece, send_sem=send.at[3 * t + j], recv_sem=recv.at[3 * t + j],
                                                  device_id=(x, y, 1 - c), device_id_type=MESH)
                cp.start()
                cps.append(cp)
        for t in range(n):
            half = shapes[t][0] // 2
            for j, (cx, cy) in enumerate(chips):
                piece = outs[t].at[2 * cx + cy, pl.ds((1 - c) * half, half)]
                pltpu.make_async_remote_copy(src_ref=piece, dst_ref=piece, send_sem=send.at[3 * t + j], recv_sem=recv.at[3 * t + j],
                                             device_id=(x, y, 1 - c), device_id_type=MESH).wait_recv()
        for cp in cps:
            cp.wait_send()

    return pl.pallas_call(
        body, name=name,
        in_specs=[HBM] * n, out_specs=[HBM] * n,
        out_shape=[SDS(p.shape, p.dtype) for p in placed],
        input_output_aliases={t: t for t in range(n)},
        scratch_shapes=[pltpu.SemaphoreType.DMA((3 * n,))] * 2,
    )(*placed)


def _flip(k, x, y, c):
    return ((1 - x) if k & 4 else x, (1 - y) if k & 2 else y, (1 - c) if k & 1 else c)


def small_allgather(v, reduce):
    R, C = v.shape

    def body(v_ref, o_ref, *scratch):
        if reduce:
            buf, send, recv = scratch
        else:
            buf, (send, recv) = o_ref, scratch
        x, y, c, _ = _position()
        me = 4 * x + 2 * y + c
        buf[me] = v_ref[...]
        sends = []
        for k in range(1, N_DEV):
            cp = pltpu.make_async_remote_copy(src_ref=v_ref, dst_ref=buf.at[me], send_sem=send.at[k - 1], recv_sem=recv.at[k - 1],
                                              device_id=_flip(k, x, y, c), device_id_type=MESH)
            cp.start()
            sends.append(cp)
        for k in range(1, N_DEV):
            px, py, pc = _flip(k, x, y, c)
            pltpu.make_async_remote_copy(src_ref=v_ref, dst_ref=buf.at[4 * px + 2 * py + pc], send_sem=send.at[k - 1],
                                         recv_sem=recv.at[k - 1], device_id=(px, py, pc), device_id_type=MESH).wait_recv()
        for cp in sends:
            cp.wait_send()
        if reduce:
            acc = buf[0]
            for i in range(1, N_DEV):
                acc = acc + buf[i]
            o_ref[...] = acc

    vm = pl.BlockSpec(memory_space=pltpu.VMEM)
    sems = [pltpu.SemaphoreType.DMA((N_DEV - 1,)), pltpu.SemaphoreType.DMA((N_DEV - 1,))]
    return pl.pallas_call(
        body, name="small_allreduce" if reduce else "small_allgather",
        in_specs=[vm], out_specs=vm,
        out_shape=SDS((R, C) if reduce else (N_DEV, R, C), F32),
        scratch_shapes=([pltpu.VMEM((N_DEV, R, C), F32)] if reduce else []) + sems,
    )(v)


def rs_exchange_sibling(gs):
    n = len(gs)

    def body(*refs):
        ins, outs, send, recv = refs[:n], refs[n:2 * n], refs[2 * n], refs[2 * n + 1]
        x, y, c, _ = _position()
        cps = []
        for t in range(n):
            cp = pltpu.make_async_remote_copy(src_ref=ins[t].at[:, 1 - c], dst_ref=outs[t], send_sem=send.at[t], recv_sem=recv.at[t],
                                              device_id=(x, y, 1 - c), device_id_type=MESH)
            cp.start()
            cps.append(cp)
        for cp in cps:
            cp.wait()

    return pl.pallas_call(
        body, name="rs_exchange_sibling", in_specs=[HBM] * n, out_specs=[HBM] * n,
        out_shape=[SDS((g.shape[0],) + g.shape[2:], g.dtype) for g in gs],
        scratch_shapes=[pltpu.SemaphoreType.DMA((n,)), pltpu.SemaphoreType.DMA((n,))],
    )(*gs)


def rs_pair_add(gs, rs, c):
    n = len(gs)

    def body(c_ref, *refs):
        for t in range(n):
            refs[2 * n + t][0] = (refs[t][0, 0].astype(F32) + refs[n + t][0].astype(F32)).astype(BF16)

    in_specs, out_specs, out_shape = [], [], []
    for g in gs:
        _, _, h, C = g.shape
        in_specs.append(pl.BlockSpec((1, 1, h // PAIR_SPLIT, C), lambda j, r, c_ref: (j, c_ref[0], r, 0)))
    for g in gs:
        _, _, h, C = g.shape
        spec = pl.BlockSpec((1, h // PAIR_SPLIT, C), lambda j, r, c_ref: (j, r, 0))
        in_specs.append(spec)
        out_specs.append(spec)
        out_shape.append(SDS((N_CHIPS, h, C), BF16))
    return pl.pallas_call(
        body, name="rs_pair_add",
        grid_spec=pltpu.PrefetchScalarGridSpec(num_scalar_prefetch=1, grid=(N_CHIPS, PAIR_SPLIT), in_specs=in_specs, out_specs=out_specs),
        out_shape=out_shape, compiler_params=_params(("parallel", "parallel")),
    )(c, *gs, *rs)


def _rs_plan(n):
    def plan(refs, x, y, c, chips):
        return [(refs[t].at[2 * cx + cy], refs[n + t].at[j], (cx, cy, c)) for t in range(n) for j, (cx, cy) in enumerate(chips)]

    return plan


def rs_chip_add(ps, qs, me_c):
    n = len(ps)

    def body(me_ref, *refs):
        for t in range(n):
            q = refs[n + t]
            refs[2 * n + t][0] = ((refs[t][0].astype(F32) + q[0].astype(F32)) + q[1].astype(F32)) + q[2].astype(F32)

    in_specs, out_specs, out_shape = [], [], []
    for p in ps:
        _, h, C = p.shape
        in_specs.append(pl.BlockSpec((1, h // ROW_SPLIT, C), lambda r, me_ref: (me_ref[0], r, 0)))
    for p in ps:
        _, h, C = p.shape
        in_specs.append(pl.BlockSpec((3, h // ROW_SPLIT, C), lambda r, me_ref: (0, r, 0)))
        out_specs.append(pl.BlockSpec((1, h // ROW_SPLIT, C), lambda r, me_ref: (me_ref[1], r, 0)))
        out_shape.append(SDS((2, h, C), F32))
    return pl.pallas_call(
        body, name="rs_chip_add",
        grid_spec=pltpu.PrefetchScalarGridSpec(num_scalar_prefetch=1, grid=(ROW_SPLIT,), in_specs=in_specs, out_specs=out_specs),
        out_shape=out_shape, compiler_params=_params(("parallel",)),
    )(me_c, *ps, *qs)


def rs_share(rs):
    n = len(rs)

    def body(*refs):
        outs, send, recv = refs[n:2 * n], refs[2 * n], refs[2 * n + 1]
        x, y, c, _ = _position()
        cps = []
        for t in range(n):
            cp = pltpu.make_async_remote_copy(src_ref=outs[t].at[c], dst_ref=outs[t].at[c], send_sem=send.at[t], recv_sem=recv.at[t],
                                              device_id=(x, y, 1 - c), device_id_type=MESH)
            cp.start()
            cps.append(cp)
        for cp in cps:
            cp.wait()

    return pl.pallas_call(
        body, name="rs_share", in_specs=[HBM] * n, out_specs=[HBM] * n,
        out_shape=[SDS(r.shape, r.dtype) for r in rs],
        input_output_aliases={t: t for t in range(n)},
        scratch_shapes=[pltpu.SemaphoreType.DMA((n,))] * 2,
    )(*rs)


def rs_begin(gs, after, name):
    c = lax.axis_index("c")
    n = len(gs)
    g5 = [g.reshape(N_CHIPS, 2, g.shape[1] // 2, g.shape[2]) for g in gs]
    from_sibling = rs_exchange_sibling(g5)
    pair = rs_pair_add(g5, from_sibling, jnp.reshape(c, (1,)).astype(jnp.int32))
    lands = [lax.empty((3,) + p.shape[1:], p.dtype) for p in pair]
    send, recv, bufs, token = split_start(list(pair) + lands, _rs_plan(n), 3 * n, from_sibling[0] if after is None else after, name)
    return (send, recv, bufs, [g.shape for g in gs]), token


def rs_end(state, after, name):
    x, y, c = lax.axis_index("x"), lax.axis_index("y"), lax.axis_index("c")
    send, recv, bufs, shapes = state
    n = len(shapes)
    bufs = split_wait(send, recv, bufs, _rs_plan(n), after, name)
    half = rs_chip_add(bufs[:n], bufs[n:], jnp.stack([2 * x + y, c]).astype(jnp.int32))
    both = rs_share(half)
    return [b.reshape(s[1], s[2]) for b, s in zip(both, shapes)]


def _pad_last(a, n):
    return jnp.pad(a, [(0, 0)] * (a.ndim - 1) + [(0, n - a.shape[-1])])


def _heads_to_groups(w):
    k = w.shape[0]
    return _pad_last(w.reshape(k, ML_HEADS, ML_HEAD_DIM).transpose(1, 0, 2), GROUP)


def _groups_to_heads(g):
    return g[:, :, :ML_HEAD_DIM].transpose(1, 0, 2).reshape(g.shape[1], D_TOK)


def _cols_to_groups(w):
    k, n = w.shape
    return w.reshape(k, n // GROUP, GROUP).transpose(1, 0, 2)


def _groups_to_cols(g):
    n, k, _ = g.shape
    return g.transpose(1, 0, 2).reshape(k, n * GROUP)


def _chips_to_cols(a):
    return a.transpose(1, 0, 2).reshape(a.shape[1], -1)


def _cols_to_chips(w):
    k, n = w.shape
    return w.reshape(k, N_CHIPS, n // N_CHIPS).transpose(1, 0, 2)


def _mlstm_in_groups(w):
    parts = [_heads_to_groups(w[:, i * D_TOK:(i + 1) * D_TOK]) for i in range(4)]
    gates = _pad_last(w[:, 4 * D_TOK:4 * D_TOK + 2 * ML_HEADS], GROUP)[None]
    qmem = w[:, 4 * D_TOK + 2 * ML_HEADS:][None]
    return jnp.concatenate(parts + [qmem, gates], axis=0)


def _mlstm_in_ungroup(g):
    parts = [_groups_to_heads(g[4 * i:4 * i + 4]) for i in range(4)]
    return jnp.concatenate(parts + [g[17][:, :2 * ML_HEADS], g[16]], axis=1)


def _taps_to_groups(w, width):
    taps = w.shape[0]
    g = _pad_last(w.reshape(taps, -1, width), GROUP).transpose(1, 0, 2)
    return jnp.pad(g, ((0, 0), (0, 8 - taps), (0, 0)))


def _groups_to_taps(g, taps, width):
    return g[:, :taps, :width].transpose(1, 0, 2).reshape(taps, -1)


SMALL_IN_COLS = 384
SMALL_OUT_COLS = 1536
SECTION = 8


class _Gathered:
    def __init__(self, srcs, groups, me):
        self.groups, self.states, self.ready = groups, [], {}
        self.group_of = {k: gi for gi, g in enumerate(groups) for k in g}
        token = me
        for gi, g in enumerate(groups):
            placed = place_own([(srcs[k], ()) for k in g], me, token, f"place_own_{gi}")
            state, token = gather_start([srcs[k] for k in g], placed, token, f"gather_start_{gi}")
            self.states.append(state)
        self.started = token

    def _get(self, key, after):
        gi = self.group_of[key]
        if gi not in self.ready:
            got = gather_wait(self.states[gi], after if gi else self.started, f"gather_wait_{gi}")
            self.ready[gi] = dict(zip(self.groups[gi], gather_pass_on(got, self.states[gi][3], f"gather_pass_on_{gi}")))
        return self.ready[gi][key]

    def ffn(self, l, i, after):
        return tuple(self._get((n, l, i), after) for n in ("wg", "wu", "wd"))

    def mixer(self, l, after):
        win = _chips_to_cols(self._get(("win", l), after))
        win = _cols_to_groups(win) if l % 2 == 0 else _mlstm_in_groups(win)
        wkv = _cols_to_groups(self._get(("wkv", l), after).reshape(D_MODEL, 2 * D_XA))
        wout = self._get(("wout", l), after)
        if l % 2:
            wout = wout.reshape(D_MODEL, D_MODEL)
            tok = jnp.pad(wout[:D_TOK].reshape(ML_HEADS, ML_HEAD_DIM, D_MODEL), ((0, 0), (0, GROUP - ML_HEAD_DIM), (0, 0)))
            wout = jnp.concatenate([tok, wout[D_TOK:][None]], axis=0)
        return win, wkv, wout


class _GradSink:
    def __init__(self, apply):
        self.queue, self.apply, self.count, self.done = [], apply, 0, None

    @staticmethod
    def _by_chip(key, g):
        if key[0] == "wkv":
            return _groups_to_cols(g).reshape(N_CHIPS, D_MODEL // N_CHIPS, 2 * D_XA)
        if key[0] == "win":
            return _cols_to_chips(_groups_to_cols(g) if key[1] % 2 == 0 else _mlstm_in_ungroup(g))
        if key[0] == "wout" and key[1] % 2:
            full = jnp.concatenate([g[:ML_HEADS, :ML_HEAD_DIM].reshape(D_TOK, D_MODEL), g[ML_HEADS]], axis=0)
            return full.reshape(N_CHIPS, D_MODEL // N_CHIPS, D_MODEL)
        return g

    def push(self, grads):
        keys = list(grads)
        state, token = rs_begin([self._by_chip(k, grads[k]) for k in keys], self.done, f"rs_start_{self.count}")
        if self.queue:
            self._finish(token)
        self.queue.append((keys, state, self.count))
        self.count += 1
        return token

    def flush(self):
        self._finish(self.done)

    def _finish(self, after):
        keys, state, i = self.queue.pop(0)
        for key, g in zip(keys, rs_end(state, after, f"rs_wait_{i}")):
            self.done = self.apply(key, g, self.done)


def _local_step(x, mem, tgt, P, weights, sink):
    memb = mem.astype(BF16)
    saved = []
    X, Xb = x, x.astype(BF16)
    after = Xb
    for l in range(DEPTH):
        s = {}
        s["x0b"] = Xb
        s["wa"] = weights.ffn(l, 0, after)
        s["g1a"], s["u1a"], s["ha"], s["z1"], X1, X1b = ffn_fwd(Xb, X, *s["wa"], P["ln_g"][l][0], P["ln_b"][l][0])
        s["x1b"] = X1b
        s["wm"] = win, wkv, wout = weights.mixer(l, X1b)
        u = proj(X1b, win, "mixer_in")
        kv = proj(memb, wkv, "mem_kv")
        s["u"], s["kv"] = u, kv
        if l % 2 == 0:
            tok = conv_mixer_fwd(u, P["convw"])
            qg = 9
        else:
            s["qk"] = qk_conv_fwd(u, P["qkw"])
            s["hm"], s["cst"], s["mst"] = mlstm_fwd(s["qk"], u, P["bg"])
            tok = head_norm_fwd(s["hm"], u, P["hg"])
            qg = 16
        xa = xattn_fwd(u, qg, kv)
        s["m"] = jnp.concatenate([tok, xa], axis=0)
        s["z2"], X2, X2b = contract_ln(s["m"], wout, X1, P["ln_g"][l][1], P["ln_b"][l][1], 1.0, "mixer_out_ln")
        s["x2b"] = X2b
        s["wb"] = weights.ffn(l, 1, X2b)
        s["g1b"], s["u1b"], s["hb"], s["z3"], X, Xb = ffn_fwd(X2b, X2, *s["wb"], P["ln_g"][l][2], P["ln_b"][l][2])
        after = Xb
        saved.append(s)

    loss, dX = loss_grad(X, tgt)

    G = {"ln_g": [[None] * 3 for _ in range(DEPTH)], "ln_b": [[None] * 3 for _ in range(DEPTH)]}
    pin = [jnp.zeros((1, 1), F32)]

    def ffn_backward(l, i, dX, z, xinb, g1, u1, h, w):
        k = 2 * i
        dgb, dub, dx, dyb, G["ln_g"][l][k], G["ln_b"][l][k] = ffn_bwd(dX, z, P["ln_g"][l][k] + pin[0], w[2], w[0], w[1], g1, u1)
        grads = {("wd", l, i): wgrad(h, dyb, BF16, "wgrad_down"), ("wg", l, i): wgrad(dgb, xinb, BF16, "wgrad_gate"),
                 ("wu", l, i): wgrad(dub, xinb, BF16, "wgrad_up")}
        return dx, grads

    for l in reversed(range(DEPTH)):
        s = saved[l]
        win, wkv, wout = s["wm"]
        dX, grads = ffn_backward(l, 1, dX, s["z3"], s["x2b"], s["g1b"], s["u1b"], s["hb"], s["wb"])
        dm, dz2, dz2b, G["ln_g"][l][1], G["ln_b"][l][1] = mixer_out_bwd(dX, s["z2"], P["ln_g"][l][1], wout)
        grads[("wout", l)] = wgrad(s["m"], dz2b, BF16, "wgrad_out")
        u, kv = s["u"], s["kv"]
        if l % 2 == 0:
            db, dc, dxi, G["convw"] = conv_mixer_bwd(u, P["convw"], dm)
            dq, dkv = xattn_bwd(u, 9, kv, dm, 3)
            du = jnp.concatenate([db, dc, dxi, dq], axis=0)
        else:
            dh, do, G["hg"] = head_norm_bwd(s["hm"], u, P["hg"], dm)
            dqk, dv, dgate, G["bg"] = mlstm_bwd(s["qk"], u, P["bg"], s["cst"], s["mst"], dh)
            duqk, G["qkw"] = qk_conv_bwd(u, P["qkw"], dqk)
            dq, dkv = xattn_bwd(u, 16, kv, dm, 4)
            du = jnp.concatenate([duqk, dv, do, dq, dgate], axis=0)
        grads[("win", l)] = wgrad(s["x1b"], du, BF16, "wgrad_in")
        grads[("wkv", l)] = wgrad(memb, dkv.astype(BF16), BF16, "wgrad_kv")
        dX = contract_t(du, win, dz2, "mixer_in_bwd")
        pin[0] = sink.push(grads)[0:1, 0:1]
        dX, grads = ffn_backward(l, 0, dX, s["z1"], s["x0b"], s["g1a"], s["u1a"], s["ha"], s["wa"])
        pin[0] = sink.push(grads)[0:1, 0:1]
    sink.flush()
    return loss, dX, G


def kernel(x, mem, ln_g, ln_b, ffn_w_gate, ffn_w_up, ffn_w_down, w_kv_mem, w_out, w_in_conv, conv_w, w_in_mlstm, b_gates, qk_conv_w, head_norm_g, loss_target, m_ln_g, m_ln_b, m_ffn_w_gate, m_ffn_w_up, m_ffn_w_down, m_w_kv_mem, m_w_out, m_w_in_conv, m_conv_w, m_w_in_mlstm, m_b_gates, m_qk_conv_w, m_head_norm_g, v_ln_g, v_ln_b, v_ffn_w_gate, v_ffn_w_up, v_ffn_w_down, v_w_kv_mem, v_w_out, v_w_in_conv, v_conv_w, v_w_in_mlstm, v_b_gates, v_qk_conv_w, v_head_norm_g):
    cx, cy = lax.axis_index("x"), lax.axis_index("y")
    chip = 2 * cx + cy

    srcs = {}
    for l in range(DEPTH):
        for i in range(2):
            srcs[("wg", l, i)] = jnp.swapaxes(ffn_w_gate[l, i], 0, 1).astype(BF16)
            srcs[("wu", l, i)] = jnp.swapaxes(ffn_w_up[l, i], 0, 1).astype(BF16)
            srcs[("wd", l, i)] = ffn_w_down[l, i].astype(BF16)
        srcs[("wkv", l)] = w_kv_mem[l].astype(BF16)
        srcs[("wout", l)] = w_out[l].astype(BF16)
    srcs[("win", 0)] = w_in_conv[0].astype(BF16)
    srcs[("win", 1)] = w_in_mlstm[0].astype(BF16)
    ffn_keys = lambda l, i: [("wg", l, i), ("wu", l, i), ("wd", l, i)]
    mixer_keys = lambda l: [("win", l), ("wkv", l), ("wout", l)]
    groups = [ffn_keys(0, 0), mixer_keys(0) + mixer_keys(1), ffn_keys(0, 1), ffn_keys(1, 0), ffn_keys(1, 1)]
    gathered = _Gathered(srcs, groups, jnp.reshape(chip, (1,)).astype(jnp.int32))

    def section(a, width):
        a = a.reshape(-1, a.shape[-1])
        return jnp.pad(a, ((0, SECTION - a.shape[0]), (0, width - a.shape[1])))

    small = jnp.concatenate([section(a, SMALL_IN_COLS) for a in (ln_g, ln_b, conv_w, qk_conv_w)], axis=0)
    smalls = small_allgather(small, reduce=False)[0::2]
    ln_g_full = _chips_to_cols(smalls[:, 0:6, 0:256]).reshape(DEPTH, 3, 1, D_MODEL)
    ln_b_full = _chips_to_cols(smalls[:, 8:14, 0:256]).reshape(DEPTH, 3, 1, D_MODEL)
    conv_w_full = _chips_to_cols(smalls[:, 16:19, 0:192])
    qk_w_full = _chips_to_cols(smalls[:, 24:28, 0:384])

    P = {"ln_g": ln_g_full, "ln_b": ln_b_full, "convw": _taps_to_groups(conv_w_full, GROUP),
         "qkw": _taps_to_groups(qk_w_full, ML_HEAD_DIM), "bg": _pad_last(b_gates, GROUP),
         "hg": _pad_last(head_norm_g[0], GROUP)[:, None, :]}

    weights = {"ln_g": ln_g, "ln_b": ln_b, "ffn_w_gate": ffn_w_gate, "ffn_w_up": ffn_w_up, "ffn_w_down": ffn_w_down,
               "w_kv_mem": w_kv_mem, "w_out": w_out, "w_in_conv": w_in_conv, "conv_w": conv_w, "w_in_mlstm": w_in_mlstm,
               "b_gates": b_gates, "qk_conv_w": qk_conv_w, "head_norm_g": head_norm_g}
    ms = {"ln_g": m_ln_g, "ln_b": m_ln_b, "ffn_w_gate": m_ffn_w_gate, "ffn_w_up": m_ffn_w_up, "ffn_w_down": m_ffn_w_down,
          "w_kv_mem": m_w_kv_mem, "w_out": m_w_out, "w_in_conv": m_w_in_conv, "conv_w": m_conv_w, "w_in_mlstm": m_w_in_mlstm,
          "b_gates": m_b_gates, "qk_conv_w": m_qk_conv_w, "head_norm_g": m_head_norm_g}
    vs = {"ln_g": v_ln_g, "ln_b": v_ln_b, "ffn_w_gate": v_ffn_w_gate, "ffn_w_up": v_ffn_w_up, "ffn_w_down": v_ffn_w_down,
          "w_kv_mem": v_w_kv_mem, "w_out": v_w_out, "w_in_conv": v_w_in_conv, "conv_w": v_conv_w, "w_in_mlstm": v_w_in_mlstm,
          "b_gates": v_b_gates, "qk_conv_w": v_qk_conv_w, "head_norm_g": v_head_norm_g}
    names = list(weights)
    owner = {"wg": ("ffn_w_gate", True), "wu": ("ffn_w_up", True), "wd": ("ffn_w_down", False), "wkv": ("w_kv_mem", False),
             "wout": ("w_out", False), "win": None}
    updated = {}

    def apply(key, g, after):
        name, transposed = owner[key[0]] or (("w_in_conv", "w_in_mlstm")[key[1]], False)
        idx = (0,) if key[0] == "win" else tuple(key[1:])
        view = (lambda a: jnp.swapaxes(a, -1, -2)) if transposed else (lambda a: a)
        updated[name], token = adamw_into(view(weights[name]), view(ms[name]), view(vs[name]), g, updated.get(name), idx, after,
                                          "adamw_" + name + "_" + "_".join(map(str, idx)))
        return token

    sink = _GradSink(apply)
    loss, grad_x, G = _local_step(x[0], mem[0], loss_target[0], P, gathered, sink)

    dln_g = jnp.concatenate([G["ln_g"][l][k] for l in range(DEPTH) for k in range(3)], axis=0)
    dln_b = jnp.concatenate([G["ln_b"][l][k] for l in range(DEPTH) for k in range(3)], axis=0)
    lane = lax.broadcasted_iota(jnp.int32, (1, GROUP), 1)
    misc = jnp.where(lane < 8, G["bg"], 0.0) + jnp.where(lane == 8, loss, 0.0) + sink.done[0:1, 0:1]
    parts = (dln_g, dln_b, _groups_to_taps(G["convw"], 3, GROUP), misc, _groups_to_taps(G["qkw"], 4, ML_HEAD_DIM),
             G["hg"][:, 0, :ML_HEAD_DIM])
    tot = small_allgather(jnp.concatenate([section(a, SMALL_OUT_COLS) for a in parts], axis=0), reduce=True)
    loss_total = tot[24, 8]

    small_grads = {
        "ln_g": lax.dynamic_slice(tot[0:6, 0:D_MODEL], (0, chip * 256), (6, 256)).reshape(DEPTH, 3, 256),
        "ln_b": lax.dynamic_slice(tot[8:14, 0:D_MODEL], (0, chip * 256), (6, 256)).reshape(DEPTH, 3, 256),
        "conv_w": lax.dynamic_slice(tot[16:19, 0:D_TOK], (0, chip * 192), (3, 192))[None],
        "b_gates": tot[24:25, 0:8],
        "qk_conv_w": lax.dynamic_slice(tot[32:36, 0:2 * D_TOK], (0, chip * 384), (4, 384))[None],
        "head_norm_g": tot[40:44, 0:ML_HEAD_DIM][None],
    }
    grads, deltas, new_m, new_v = [], [], [], []
    for nme in names:
        if nme in updated:
            back = (lambda a: jnp.swapaxes(a, -1, -2)) if nme in ("ffn_w_gate", "ffn_w_up") else (lambda a: a)
            g, d, nm, nv = (back(a) for a in updated[nme])
        else:
            w, g = weights[nme], small_grads[nme]
            two = (math.prod(w.shape[:-1]), w.shape[-1])
            d, nm, nv = (a.reshape(w.shape) for a in adamw(w.reshape(two), g.reshape(two), ms[nme].reshape(two),
                                                           vs[nme].reshape(two), "adamw_" + nme))
        grads.append(g)
        deltas.append(d)
        new_m.append(nm)
        new_v.append(nv)
    return (loss_total, grad_x[None], *grads, *deltas, *new_m, *new_v)
```

```python
import functools
import math

import jax
import jax.numpy as jnp
from jax import lax
from jax.experimental import pallas as pl
from jax.experimental.pallas import tpu as pltpu

F32 = jnp.float32
BF16 = jnp.bfloat16
SDS = jax.ShapeDtypeStruct

D_MODEL = 1024
DEPTH = 2
N_MEM = 256
XA_HEADS = 4
XA_HEAD_DIM = 64
D_XA = 256
D_TOK = 768
ML_HEADS = 4
ML_HEAD_DIM = 192
ML_CHUNK = 64
D_FF = 2816
LN_EPS = 1e-5
ALPHA = (2.0 * DEPTH) ** 0.25
N_CHIPS = 4
N_DEV = 8
FF_SHARD = D_FF // N_CHIPS
GROUP = 256
NEG = -1e30

ADAM_LR = 0.001
ADAM_B1 = 0.9
ADAM_B2 = 0.999
ADAM_EPS = 1e-08
ADAM_WD = 0.01
ADAM_STEP = 10

VMEM_LIMIT = 56 * 1024 * 1024

NN = ((1,), (0,))
NT = ((1,), (1,))
TN = ((0,), (0,))
MESH = pl.DeviceIdType.MESH


def _dot(a, b, dims):
    return lax.dot_general(a, b, (dims, ((), ())), preferred_element_type=F32)


def _bdot(a, b, ca, cb):
    dims = (((ca,), (cb,)), ((0,), (0,)))
    ah, bh = a.astype(BF16), b.astype(BF16)
    al, bl = (a - ah.astype(F32)).astype(BF16), (b - bh.astype(F32)).astype(BF16)
    dot = functools.partial(lax.dot_general, dimension_numbers=dims, preferred_element_type=F32)
    return dot(ah, bh) + dot(al, bh) + dot(ah, bl)


def _bdot1(a, b, ca, cb):
    return lax.dot_general(a.astype(BF16), b.astype(BF16), (((ca,), (cb,)), ((0,), (0,))), preferred_element_type=F32)


def _sigmoid(x):
    return 1.0 / (1.0 + jnp.exp(-x))


def _params(sem, vmem=VMEM_LIMIT):
    return pltpu.CompilerParams(dimension_semantics=sem, vmem_limit_bytes=vmem)


def _tile(n, want):
    t = min(n, want)
    assert n % t == 0, (n, t)
    return t


def _layer_norm(z, gamma, beta):
    mu = jnp.mean(z, axis=-1, keepdims=True)
    zc = z - mu
    var = jnp.mean(zc * zc, axis=-1, keepdims=True)
    return zc * lax.rsqrt(var + LN_EPS) * gamma + beta


def _column_halves(n):
    mid = -(-n // (2 * 128)) * 128
    return ((0, mid), (mid, n))


def _resident(shape):
    return pl.BlockSpec(shape, lambda *_: (0,) * len(shape), pipeline_mode=pl.Buffered(1))


def _group_block(G, want):
    return max(d for d in range(1, max(1, min(G, want)) + 1) if G % d == 0)


def ffn_fwd(xb, x, wg, wu, wd, gamma, beta):
    S, K = xb.shape
    G, N, _ = wg.shape
    ts = _tile(S, 512)

    def body(xb_ref, x_ref, wg_ref, wu_ref, wd_ref, gm_ref, bt_ref, g_ref, u_ref, h_ref, z_ref, xn_ref, xnb_ref):
        j = pl.program_id(1)
        xv = xb_ref[...]
        g = _dot(xv, wg_ref[j], NT)
        u = _dot(xv, wu_ref[j], NT)
        h = (g * _sigmoid(g) * u).astype(BF16)
        g_ref[0] = g.astype(BF16)
        u_ref[0] = u.astype(BF16)
        h_ref[0] = h
        y = _dot(h, wd_ref[j], NN)

        @pl.when(j == 0)
        def _():
            z_ref[...] = y

        @pl.when(j > 0)
        def _():
            z_ref[...] += y

        @pl.when(j == G - 1)
        def _():
            z = ALPHA * x_ref[...] + 0.5 * z_ref[...]
            xn = _layer_norm(z, gm_ref[...], bt_ref[...])
            z_ref[...] = z
            xn_ref[...] = xn
            xnb_ref[...] = xn.astype(BF16)

    row = pl.BlockSpec((ts, K), lambda s, j: (s, 0))
    vec = pl.BlockSpec((1, K), lambda s, j: (0, 0))
    wspec = _resident((G, N, K))
    ospec = pl.BlockSpec((1, ts, N), lambda s, j: (j, s, 0))
    return pl.pallas_call(
        body, name="ffn_fwd", grid=(S // ts, G),
        in_specs=[row, row, wspec, wspec, wspec, vec, vec],
        out_specs=[ospec, ospec, ospec, row, row, row],
        out_shape=[SDS((G, S, N), BF16), SDS((G, S, N), BF16), SDS((G, S, N), BF16),
                   SDS((S, K), F32), SDS((S, K), F32), SDS((S, K), BF16)],
        compiler_params=_params(("parallel", "arbitrary")),
    )(xb, x, wg, wu, wd, gamma, beta)


def proj(xb, w, name):
    S, K = xb.shape
    G, _, N = w.shape
    ts = _tile(S, 1024)
    gb = _group_block(G, 6)

    def body(x_ref, w_ref, y_ref):
        xv = x_ref[...]
        for j in range(gb):
            y_ref[j] = _dot(xv, w_ref[j], NN)

    return pl.pallas_call(
        body, name=name, grid=(S // ts, G // gb),
        in_specs=[pl.BlockSpec((ts, K), lambda s, g: (s, 0)), pl.BlockSpec((gb, K, N), lambda s, g: (g, 0, 0))],
        out_specs=pl.BlockSpec((gb, ts, N), lambda s, g: (g, s, 0)),
        out_shape=SDS((G, S, N), F32),
        compiler_params=_params(("parallel", "parallel")),
    )(xb, w)


def contract_ln(a, w, xres, gamma, beta, scale, name):
    G, S, Kg = a.shape
    N = w.shape[2]
    ts = _tile(S, 1024)

    def body(a_ref, w_ref, x_ref, g_ref, b_ref, z_ref, xn_ref, xb_ref):
        acc = _dot(a_ref[0], w_ref[0], NN)
        for j in range(1, G):
            acc = acc + _dot(a_ref[j], w_ref[j], NN)
        z = ALPHA * x_ref[...] + scale * acc
        xn = _layer_norm(z, g_ref[...], b_ref[...])
        z_ref[...] = z
        xn_ref[...] = xn
        xb_ref[...] = xn.astype(BF16)

    row = pl.BlockSpec((ts, N), lambda s: (s, 0))
    vec = pl.BlockSpec((1, N), lambda s: (0, 0))
    return pl.pallas_call(
        body, name=name, grid=(S // ts,),
        in_specs=[pl.BlockSpec((G, ts, Kg), lambda s: (0, s, 0)), pl.BlockSpec((G, Kg, N), lambda s: (0, 0, 0)), row, vec, vec],
        out_specs=[row, row, row],
        out_shape=[SDS((S, N), F32), SDS((S, N), F32), SDS((S, N), BF16)],
        compiler_params=_params(("parallel",)),
    )(a, w, xres, gamma, beta)


def _layer_norm_bwd(dx, z, gamma):
    mu = jnp.mean(z, axis=-1, keepdims=True)
    zc = z - mu
    var = jnp.mean(zc * zc, axis=-1, keepdims=True)
    rstd = lax.rsqrt(var + LN_EPS)
    xhat = zc * rstd
    dxh = dx * gamma
    m1 = jnp.mean(dxh, axis=-1, keepdims=True)
    m2 = jnp.mean(dxh * xhat, axis=-1, keepdims=True)
    return rstd * (dxh - m1 - xhat * m2), jnp.sum(dx * xhat, axis=0, keepdims=True), jnp.sum(dx, axis=0, keepdims=True)


def ffn_bwd(dxn, z, gamma, wd, wg, wu, g1, u1):
    S, K = dxn.shape
    G, N, _ = wd.shape
    ts = _tile(S, 512)

    def body(dxn_ref, z_ref, gm_ref, wd_ref, wg_ref, wu_ref, g_ref, u_ref, dg_ref, du_ref, dx_ref, dy_ref, dgm_ref, dbt_ref):
        s, j = pl.program_id(0), pl.program_id(1)

        @pl.when((s == 0) & (j == 0))
        def _():
            dgm_ref[...] = jnp.zeros_like(dgm_ref)
            dbt_ref[...] = jnp.zeros_like(dbt_ref)

        @pl.when(j == 0)
        def _():
            dz, dgm, dbt = _layer_norm_bwd(dxn_ref[...], z_ref[...], gm_ref[...])
            dgm_ref[...] += dgm
            dbt_ref[...] += dbt
            dx_ref[...] = ALPHA * dz
            dy_ref[...] = (0.5 * dz).astype(BF16)

        dy = dy_ref[...]
        part = None
        for a, b in _column_halves(N):
            dh = _dot(dy, wd_ref[j, a:b, :], NT)
            g = g_ref[0, :, a:b].astype(F32)
            sig = _sigmoid(g)
            dg = (dh * u_ref[0, :, a:b].astype(F32) * (sig * (1.0 + g * (1.0 - sig)))).astype(BF16)
            du = (dh * (g * sig)).astype(BF16)
            dg_ref[0, :, a:b] = dg
            du_ref[0, :, a:b] = du
            p = _dot(dg, wg_ref[j, a:b, :], NN) + _dot(du, wu_ref[j, a:b, :], NN)
            part = p if part is None else part + p
        dx_ref[...] += part

    row = pl.BlockSpec((ts, K), lambda s, j: (s, 0))
    vec = pl.BlockSpec((1, K), lambda s, j: (0, 0))
    gspec = pl.BlockSpec((1, ts, N), lambda s, j: (j, s, 0))
    wspec = _resident((G, N, K))
    return pl.pallas_call(
        body, name="ffn_bwd", grid=(S // ts, G),
        in_specs=[row, row, vec, wspec, wspec, wspec, gspec, gspec],
        out_specs=[gspec, gspec, row, row, vec, vec],
        out_shape=[SDS((G, S, N), BF16), SDS((G, S, N), BF16), SDS((S, K), F32), SDS((S, K), BF16),
                   SDS((1, K), F32), SDS((1, K), F32)],
        compiler_params=_params(("arbitrary", "arbitrary")),
    )(dxn, z, gamma, wd, wg, wu, g1, u1)


def mixer_out_bwd(dxn, z, gamma, w):
    S, N = dxn.shape
    G, Kg, _ = w.shape
    ts = _tile(S, 512)

    def body(dxn_ref, z_ref, gm_ref, w_ref, dm_ref, dz_ref, dzb_ref, dgm_ref, dbt_ref):
        @pl.when(pl.program_id(0) == 0)
        def _():
            dgm_ref[...] = jnp.zeros_like(dgm_ref)
            dbt_ref[...] = jnp.zeros_like(dbt_ref)

        dz, dgm, dbt = _layer_norm_bwd(dxn_ref[...], z_ref[...], gm_ref[...])
        dgm_ref[...] += dgm
        dbt_ref[...] += dbt
        dzb = dz.astype(BF16)
        dz_ref[...] = dz
        dzb_ref[...] = dzb
        for j in range(G):
            dm_ref[j] = _dot(dzb, w_ref[j], NT)

    row = pl.BlockSpec((ts, N), lambda s: (s, 0))
    vec = pl.BlockSpec((1, N), lambda s: (0, 0))
    return pl.pallas_call(
        body, name="mixer_out_bwd", grid=(S // ts,),
        in_specs=[row, row, vec, pl.BlockSpec((G, Kg, N), lambda s: (0, 0, 0))],
        out_specs=[pl.BlockSpec((G, ts, Kg), lambda s: (0, s, 0)), row, row, vec, vec],
        out_shape=[SDS((G, S, Kg), F32), SDS((S, N), F32), SDS((S, N), BF16), SDS((1, N), F32), SDS((1, N), F32)],
        compiler_params=_params(("arbitrary",)),
    )(dxn, z, gamma, w)


def contract_t(da, w, res, name):
    G, S, Ng = da.shape
    K = w.shape[1]
    ts = _tile(S, 512)
    gb = _group_block(G, 6)

    def body(da_ref, w_ref, r_ref, o_ref):
        g = pl.program_id(1)
        part = _dot(da_ref[0], w_ref[0], NT)
        for j in range(1, gb):
            part = part + _dot(da_ref[j], w_ref[j], NT)

        @pl.when(g == 0)
        def _():
            o_ref[...] = ALPHA * r_ref[...] + part

        @pl.when(g > 0)
        def _():
            o_ref[...] += part

    row = pl.BlockSpec((ts, K), lambda s, g: (s, 0))
    return pl.pallas_call(
        body, name=name, grid=(S // ts, G // gb),
        in_specs=[pl.BlockSpec((gb, ts, Ng), lambda s, g: (g, s, 0)), pl.BlockSpec((gb, K, Ng), lambda s, g: (g, 0, 0)), row],
        out_specs=row,
        out_shape=SDS((S, K), F32),
        compiler_params=_params(("parallel", "arbitrary")),
    )(da, w, res)


WGRAD_ACC_ELEMS = 6 * 1024 * 256


def wgrad(a, b, out_dtype, name):
    ga, gb = a.ndim == 3, b.ndim == 3
    G = a.shape[0] if ga else b.shape[0]
    S, K = a.shape[-2:]
    N = b.shape[-1]
    ts = _tile(S, 2048)
    ns = S // ts
    ng = _group_block(G, WGRAD_ACC_ELEMS // (K * N))

    def body(a_ref, b_ref, o_ref, acc):
        s = pl.program_id(1)

        @pl.when(s == 0)
        def _():
            acc[...] = jnp.zeros_like(acc)

        for j in range(ng):
            acc[j] += _dot(a_ref[j] if ga else a_ref[...], b_ref[j] if gb else b_ref[...], TN)

        @pl.when(s == ns - 1)
        def _():
            o_ref[...] = acc[...].astype(out_dtype)

    aspec = pl.BlockSpec((ng, ts, K), lambda g, s: (g, s, 0)) if ga else pl.BlockSpec((ts, K), lambda g, s: (s, 0))
    bspec = pl.BlockSpec((ng, ts, N), lambda g, s: (g, s, 0)) if gb else pl.BlockSpec((ts, N), lambda g, s: (s, 0))
    return pl.pallas_call(
        body, name=name, grid=(G // ng, ns),
        in_specs=[aspec, bspec],
        out_specs=pl.BlockSpec((ng, K, N), lambda g, s: (g, 0, 0)),
        out_shape=SDS((G, K, N), out_dtype),
        scratch_shapes=[pltpu.VMEM((ng, K, N), F32)],
        compiler_params=_params(("parallel", "arbitrary")),
    )(a, b)


def loss_grad(xn, tgt):
    S, N = xn.shape
    ts = _tile(S, 1024)

    def body(x_ref, t_ref, l_ref, dx_ref):
        @pl.when(pl.program_id(0) == 0)
        def _():
            l_ref[...] = jnp.zeros_like(l_ref)

        e = x_ref[...] - t_ref[...]
        dx_ref[...] = e * (1.0 / N)
        l_ref[...] += 0.5 * jnp.sum(jnp.mean(e * e, axis=-1, keepdims=True), axis=0, keepdims=True)

    row = pl.BlockSpec((ts, N), lambda s: (s, 0))
    return pl.pallas_call(
        body, name="loss_grad", grid=(S // ts,),
        in_specs=[row, row],
        out_specs=[pl.BlockSpec((1, 1), lambda s: (0, 0)), row],
        out_shape=[SDS((1, 1), F32), SDS((S, N), F32)],
        compiler_params=_params(("arbitrary",)),
    )(xn, tgt)


def _shift_down(x, k):
    if k == 0:
        return x
    rows = lax.broadcasted_iota(jnp.int32, x.shape, 0)
    return jnp.where(rows >= k, pltpu.roll(x, k, 0), 0.0)


def _shift_up(x, k):
    if k == 0:
        return x
    n = x.shape[0]
    rows = lax.broadcasted_iota(jnp.int32, x.shape, 0)
    return jnp.where(rows < n - k, pltpu.roll(x, n - k, 0), 0.0)


LANES = 128


def conv_mixer_fwd(u, cw):
    _, S, _ = u.shape
    nh = GROUP // LANES

    def body(b_ref, c_ref, x_ref, w_ref, o_ref):
        p = c_ref[0] * x_ref[0]
        w = w_ref[0]
        conv = w[2:3] * p + w[1:2] * _shift_down(p, 1) + w[0:1] * _shift_down(p, 2)
        o_ref[0] = (b_ref[0] * conv).astype(BF16)

    def uspec(off):
        return pl.BlockSpec((1, S, LANES), lambda g, h: (g + off, 0, h))

    return pl.pallas_call(
        body, name="conv_mixer_fwd", grid=(3, nh),
        in_specs=[uspec(0), uspec(3), uspec(6), pl.BlockSpec((1, 8, LANES), lambda g, h: (g, 0, h))],
        out_specs=pl.BlockSpec((1, S, LANES), lambda g, h: (g, 0, h)),
        out_shape=SDS((3, S, GROUP), BF16),
        compiler_params=_params(("parallel", "parallel")),
    )(u, u, u, cw)


def conv_mixer_bwd(u, cw, dm):
    _, S, _ = u.shape
    nh = GROUP // LANES

    def body(b_ref, c_ref, x_ref, w_ref, d_ref, db_ref, dc_ref, dx_ref, dw_ref):
        cg, xi = c_ref[0], x_ref[0]
        p = cg * xi
        p1, p2 = _shift_down(p, 1), _shift_down(p, 2)
        w = w_ref[0]
        conv = w[2:3] * p + w[1:2] * p1 + w[0:1] * p2
        dt = d_ref[0]
        db_ref[0] = (dt * conv).astype(BF16)
        dcv = dt * b_ref[0]
        dp = w[2:3] * dcv + w[1:2] * _shift_up(dcv, 1) + w[0:1] * _shift_up(dcv, 2)
        dc_ref[0] = (dp * xi).astype(BF16)
        dx_ref[0] = (dp * cg).astype(BF16)
        dw = jnp.concatenate([jnp.sum(dcv * p2, axis=0, keepdims=True), jnp.sum(dcv * p1, axis=0, keepdims=True),
                              jnp.sum(dcv * p, axis=0, keepdims=True), jnp.zeros((5, LANES), F32)], axis=0)
        dw_ref[0] = dw

    def uspec(off):
        return pl.BlockSpec((1, S, LANES), lambda g, h: (g + off, 0, h))

    ospec = pl.BlockSpec((1, S, LANES), lambda g, h: (g, 0, h))
    wspec = pl.BlockSpec((1, 8, LANES), lambda g, h: (g, 0, h))
    return pl.pallas_call(
        body, name="conv_mixer_bwd", grid=(3, nh),
        in_specs=[uspec(0), uspec(3), uspec(6), wspec, ospec],
        out_specs=[ospec, ospec, ospec, wspec],
        out_shape=[SDS((3, S, GROUP), BF16)] * 3 + [SDS((3, 8, GROUP), F32)],
        compiler_params=_params(("parallel", "parallel")),
    )(u, u, u, cw, dm)


def qk_conv_fwd(u, qw):
    _, S, _ = u.shape
    nh = GROUP // LANES

    def body(u_ref, w_ref, o_ref):
        x = u_ref[0]
        w = w_ref[0]
        pre = w[3:4] * x + w[2:3] * _shift_down(x, 1) + w[1:2] * _shift_down(x, 2) + w[0:1] * _shift_down(x, 3)
        o_ref[0] = pre * _sigmoid(pre)

    spec = pl.BlockSpec((1, S, LANES), lambda g, h: (g, 0, h))
    return pl.pallas_call(
        body, name="qk_conv_fwd", grid=(8, nh),
        in_specs=[spec, pl.BlockSpec((1, 8, LANES), lambda g, h: (g, 0, h))],
        out_specs=spec,
        out_shape=SDS((8, S, GROUP), F32),
        compiler_params=_params(("parallel", "parallel")),
    )(u, qw)


def qk_conv_bwd(u, qw, dqk):
    _, S, _ = u.shape
    nh = GROUP // LANES

    def body(u_ref, w_ref, d_ref, du_ref, dw_ref):
        x = u_ref[0]
        w = w_ref[0]
        x1, x2, x3 = _shift_down(x, 1), _shift_down(x, 2), _shift_down(x, 3)
        pre = w[3:4] * x + w[2:3] * x1 + w[1:2] * x2 + w[0:1] * x3
        sig = _sigmoid(pre)
        dpre = d_ref[0] * (sig * (1.0 + pre * (1.0 - sig)))
        du = w[3:4] * dpre + w[2:3] * _shift_up(dpre, 1) + w[1:2] * _shift_up(dpre, 2) + w[0:1] * _shift_up(dpre, 3)
        du_ref[0] = du.astype(BF16)
        dw = jnp.concatenate([jnp.sum(dpre * x3, axis=0, keepdims=True), jnp.sum(dpre * x2, axis=0, keepdims=True),
                              jnp.sum(dpre * x1, axis=0, keepdims=True), jnp.sum(dpre * x, axis=0, keepdims=True),
                              jnp.zeros((4, LANES), F32)], axis=0)
        dw_ref[0] = dw

    spec = pl.BlockSpec((1, S, LANES), lambda g, h: (g, 0, h))
    wspec = pl.BlockSpec((1, 8, LANES), lambda g, h: (g, 0, h))
    return pl.pallas_call(
        body, name="qk_conv_bwd", grid=(8, nh),
        in_specs=[spec, wspec, spec],
        out_specs=[spec, wspec],
        out_shape=[SDS((8, S, GROUP), BF16), SDS((8, 8, GROUP), F32)],
        compiler_params=_params(("parallel", "parallel")),
    )(u, qw, dqk)


def _head_masks():
    lane = lax.broadcasted_iota(jnp.int32, (1, D_XA), 1)
    return [(lane >= h * XA_HEAD_DIM) & (lane < (h + 1) * XA_HEAD_DIM) for h in range(XA_HEADS)]


def xattn_fwd(u, qg, kv):
    _, S, _ = u.shape
    ts = _tile(S, 1024)
    scale = XA_HEAD_DIM ** -0.5

    def body(q_ref, kv_ref, o_ref):
        q = q_ref[0]
        k = kv_ref[0].astype(BF16)
        v = kv_ref[1]
        o = jnp.zeros((ts, D_XA), F32)
        for m in _head_masks():
            s = _dot(jnp.where(m, q, 0.0).astype(BF16), k, NT) * scale
            s = s - jnp.max(s, axis=-1, keepdims=True)
            e = jnp.exp(s)
            p = e / jnp.sum(e, axis=-1, keepdims=True)
            o = o + _dot(p.astype(BF16), jnp.where(m, v, 0.0).astype(BF16), NN)
        o_ref[0] = o.astype(BF16)

    return pl.pallas_call(
        body, name="xattn_fwd", grid=(S // ts,),
        in_specs=[pl.BlockSpec((1, ts, GROUP), lambda s: (qg, s, 0)), pl.BlockSpec((2, N_MEM, GROUP), lambda s: (0, 0, 0))],
        out_specs=pl.BlockSpec((1, ts, GROUP), lambda s: (0, s, 0)),
        out_shape=SDS((1, S, GROUP), BF16),
        compiler_params=_params(("parallel",)),
    )(u, kv)


def xattn_bwd(u, qg, kv, dm, dg):
    _, S, _ = u.shape
    ts = _tile(S, 1024)
    scale = XA_HEAD_DIM ** -0.5

    def body(q_ref, kv_ref, do_ref, dq_ref, dkv_ref):
        @pl.when(pl.program_id(0) == 0)
        def _():
            dkv_ref[...] = jnp.zeros_like(dkv_ref)

        q = q_ref[0]
        k = kv_ref[0]
        v = kv_ref[1]
        kb = k.astype(BF16)
        do = do_ref[0]
        dq = jnp.zeros((ts, D_XA), F32)
        dk = jnp.zeros((N_MEM, D_XA), F32)
        dv = jnp.zeros((N_MEM, D_XA), F32)
        for m in _head_masks():
            qm = jnp.where(m, q, 0.0).astype(BF16)
            s = _dot(qm, kb, NT) * scale
            s = s - jnp.max(s, axis=-1, keepdims=True)
            e = jnp.exp(s)
            p = e / jnp.sum(e, axis=-1, keepdims=True)
            dom = jnp.where(m, do, 0.0).astype(BF16)
            dp = _dot(dom, jnp.where(m, v, 0.0).astype(BF16), NT)
            ds = (p * (dp - jnp.sum(dp * p, axis=-1, keepdims=True)) * scale).astype(BF16)
            dq = dq + _dot(ds, jnp.where(m, k, 0.0).astype(BF16), NN)
            dk = dk + _dot(ds, qm, TN)
            dv = dv + _dot(p.astype(BF16), dom, TN)
        dq_ref[0] = dq.astype(BF16)
        dkv_ref[0] += dk
        dkv_ref[1] += dv

    return pl.pallas_call(
        body, name="xattn_bwd", grid=(S // ts,),
        in_specs=[pl.BlockSpec((1, ts, GROUP), lambda s: (qg, s, 0)), pl.BlockSpec((2, N_MEM, GROUP), lambda s: (0, 0, 0)),
                  pl.BlockSpec((1, ts, GROUP), lambda s: (dg, s, 0))],
        out_specs=[pl.BlockSpec((1, ts, GROUP), lambda s: (0, s, 0)), pl.BlockSpec((2, N_MEM, GROUP), lambda s: (0, 0, 0))],
        out_shape=[SDS((1, S, GROUP), BF16), SDS((2, N_MEM, GROUP), F32)],
        compiler_params=_params(("arbitrary",)),
    )(u, kv, dm)


ML_BLOCK_CHUNKS = 4
H4 = ML_HEADS
L = ML_CHUNK
NLANE = ML_HEAD_DIM


def _chunk_consts():
    r = lax.broadcasted_iota(jnp.int32, (1, L, L), 1)
    c = lax.broadcasted_iota(jnp.int32, (1, L, L), 2)
    return r >= c, r <= c, r == c


def _gate_cols(gb):
    lane = lax.broadcasted_iota(jnp.int32, gb.shape, 1)
    li = jnp.stack([jnp.sum(jnp.where(lane == h, gb, 0.0), axis=1, keepdims=True) for h in range(H4)])
    gf = jnp.stack([jnp.sum(jnp.where(lane == H4 + h, gb, 0.0), axis=1, keepdims=True) for h in range(H4)])
    return li, gf


def _log_sigmoid(x):
    return jnp.minimum(x, 0.0) - jnp.log(1.0 + jnp.exp(-jnp.abs(x)))


def _chunk_forward(q, k, v_aug, li_col, lf_col, c_prev, m_prev):
    tri, tri_t, eye = _chunk_consts()
    lf_row = jnp.sum(jnp.where(eye, lf_col, 0.0), axis=1, keepdims=True)
    li_row = jnp.sum(jnp.where(eye, li_col, 0.0), axis=1, keepdims=True)
    bcum_col = jnp.sum(jnp.where(tri, lf_row, 0.0), axis=2, keepdims=True)
    bcum_row = jnp.sum(jnp.where(tri_t, lf_col, 0.0), axis=1, keepdims=True)
    log_d = jnp.where(tri, bcum_col - bcum_row + li_row, NEG)
    log_inter = bcum_col + m_prev
    m_t = jnp.maximum(log_inter, jnp.max(log_d, axis=2, keepdims=True))
    w_intra = jnp.exp(log_d - m_t)
    w_inter = jnp.exp(log_inter - m_t)
    sc = _bdot(q, k, 2, 2) * w_intra
    qc = _bdot1(q, c_prev, 2, 1)
    num = _bdot(sc, v_aug, 2, 1) + w_inter * qc
    lane = lax.broadcasted_iota(jnp.int32, num.shape, 2)
    den = jnp.sum(jnp.where(lane == NLANE, num, 0.0), axis=2, keepdims=True)
    e_m = jnp.exp(-m_t)
    b_last = jnp.sum(lf_row, axis=2, keepdims=True)
    log_w = b_last - bcum_col + li_col
    m_new = jnp.maximum(b_last + m_prev, jnp.max(log_w, axis=1, keepdims=True))
    w_k = jnp.exp(log_w - m_new)
    decay = jnp.exp(b_last + m_prev - m_new)
    return dict(w_intra=w_intra, w_inter=w_inter, sc=sc, qc=qc, num=num, den=den, e_m=e_m, lane=lane,
                w_k=w_k, decay=decay, m_new=m_new)


def mlstm_fwd(qk, u, bg):
    _, S, _ = qk.shape
    nc = S // L
    cb = min(ML_BLOCK_CHUNKS, nc)
    rows = cb * L
    kscale = ML_HEAD_DIM ** -0.5

    def body(qk_ref, v_ref, g_ref, bg_ref, h_ref, cst_ref, mst_ref, c_sc, m_sc):
        @pl.when(pl.program_id(0) == 0)
        def _():
            c_sc[...] = jnp.zeros_like(c_sc)
            m_sc[...] = jnp.zeros_like(m_sc)

        for c in range(cb):
            sl = pl.ds(c * L, L)
            q = qk_ref[0:H4, sl, :]
            k = qk_ref[H4:2 * H4, sl, :] * kscale
            v = v_ref[:, sl, :]
            lane = lax.broadcasted_iota(jnp.int32, v.shape, 2)
            v_aug = jnp.where(lane == NLANE, 1.0, v)
            li_col, gf = _gate_cols(g_ref[0, sl, :] + bg_ref[...])
            lf_col = _log_sigmoid(gf)
            c_prev = c_sc[...]
            m_prev = m_sc[...]
            f = _chunk_forward(q, k, v_aug, li_col, lf_col, c_prev, m_prev)
            r = 1.0 / jnp.maximum(jnp.abs(f["den"]), f["e_m"])
            h_ref[:, sl, :] = jnp.where(lane < NLANE, f["num"] * r, 0.0)
            cst_ref[c] = c_prev
            mst_ref[c] = jnp.broadcast_to(m_prev, (H4, 1, LANES))
            c_sc[...] = f["decay"] * c_prev + _bdot(k * f["w_k"], v_aug, 1, 1)
            m_sc[...] = f["m_new"]

    def hspec(blk):
        return pl.BlockSpec((H4, rows, GROUP), lambda i: (blk, i, 0))

    return pl.pallas_call(
        body, name="mlstm_fwd", grid=(nc // cb,),
        in_specs=[pl.BlockSpec((2 * H4, rows, GROUP), lambda i: (0, i, 0)), hspec(2),
                  pl.BlockSpec((1, rows, GROUP), lambda i: (17, i, 0)), pl.BlockSpec((1, GROUP), lambda i: (0, 0))],
        out_specs=[hspec(0), pl.BlockSpec((cb, H4, GROUP, GROUP), lambda i: (i, 0, 0, 0)),
                   pl.BlockSpec((cb, H4, 1, LANES), lambda i: (i, 0, 0, 0))],
        out_shape=[SDS((H4, S, GROUP), F32), SDS((nc, H4, GROUP, GROUP), F32), SDS((nc, H4, 1, LANES), F32)],
        scratch_shapes=[pltpu.VMEM((H4, GROUP, GROUP), F32), pltpu.VMEM((H4, 1, 1), F32)],
        compiler_params=_params(("arbitrary",)),
    )(qk, u, u, bg)


def mlstm_bwd(qk, u, bg, cst, mst, dh):
    _, S, _ = qk.shape
    nc = S // L
    cb = min(ML_BLOCK_CHUNKS, nc)
    rows = cb * L
    nb = nc // cb
    kscale = ML_HEAD_DIM ** -0.5

    def body(qk_ref, v_ref, g_ref, bg_ref, cst_ref, mst_ref, dh_ref, dqk_ref, dv_ref, dg_ref, dbg_ref, dc_sc):
        @pl.when(pl.program_id(0) == 0)
        def _():
            dc_sc[...] = jnp.zeros_like(dc_sc)
            dbg_ref[...] = jnp.zeros_like(dbg_ref)

        tri, tri_t, eye = _chunk_consts()
        for c in reversed(range(cb)):
            sl = pl.ds(c * L, L)
            q = qk_ref[0:H4, sl, :]
            k = qk_ref[H4:2 * H4, sl, :] * kscale
            v = v_ref[:, sl, :]
            lane = lax.broadcasted_iota(jnp.int32, v.shape, 2)
            v_aug = jnp.where(lane == NLANE, 1.0, v)
            li_col, gf = _gate_cols(g_ref[0, sl, :] + bg_ref[...])
            lf_col = _log_sigmoid(gf)
            c_prev = cst_ref[c]
            m_prev = mst_ref[c][:, :, 0:1]
            f = _chunk_forward(q, k, v_aug, li_col, lf_col, c_prev, m_prev)
            w_intra, w_inter, sc, num, den, e_m = f["w_intra"], f["w_inter"], f["sc"], f["num"], f["den"], f["e_m"]
            absd = jnp.abs(den)
            r = 1.0 / jnp.maximum(absd, e_m)
            dhv = dh_ref[:, sl, :]
            s1 = jnp.sum(jnp.where(lane < NLANE, dhv * num, 0.0), axis=2, keepdims=True)
            dden = jnp.where(absd > e_m, -s1 * r * r * jnp.sign(den), 0.0)
            dnum = jnp.where(lane == NLANE, dden, jnp.where(lane < NLANE, dhv * r, 0.0))
            dsc = _bdot1(dnum, v_aug, 2, 2)
            dv = _bdot1(sc, dnum, 1, 1)
            gmat = dsc * sc
            dqk = dsc * w_intra
            dq = _bdot1(dqk, k, 2, 1) + w_inter * _bdot1(dnum, c_prev, 2, 2)
            dk = _bdot1(dqk, q, 1, 1)
            dc_prev = _bdot(q * w_inter, dnum, 1, 1)
            dlog_inter = jnp.sum(dnum * f["qc"], axis=2, keepdims=True) * w_inter
            dbcum_col = dlog_inter + jnp.sum(gmat, axis=2, keepdims=True)
            g_row = jnp.sum(gmat, axis=1, keepdims=True)
            dcn = dc_sc[...]
            w_k, decay = f["w_k"], f["decay"]
            kw = k * w_k
            dc_prev = dc_prev + decay * dcn
            db_last = jnp.sum(jnp.sum(dcn * c_prev, axis=2, keepdims=True), axis=1, keepdims=True) * decay
            dkw = _bdot(v_aug, dcn, 2, 2)
            dv = dv + _bdot1(kw, dcn, 2, 1)
            dk = dk + dkw * w_k
            dlogw = jnp.sum(dkw * k, axis=2, keepdims=True) * w_k
            db_last = db_last + jnp.sum(dlogw, axis=1, keepdims=True)
            dbcum_col = dbcum_col - dlogw
            rowi = lax.broadcasted_iota(jnp.int32, (1, L, 1), 1)
            dbcum_col = dbcum_col + jnp.where(rowi == L - 1, db_last, 0.0)
            dbcum_row = jnp.sum(jnp.where(eye, dbcum_col, 0.0), axis=1, keepdims=True) - g_row
            dlf_col = jnp.sum(jnp.where(tri_t, dbcum_row, 0.0), axis=2, keepdims=True)
            dli_col = dlogw + jnp.sum(jnp.where(eye, g_row, 0.0), axis=2, keepdims=True)
            dgf_col = dlf_col * _sigmoid(-gf)
            lane_g = lax.broadcasted_iota(jnp.int32, (L, GROUP), 1)
            dg = jnp.zeros((L, GROUP), F32)
            for h in range(H4):
                dg = dg + jnp.where(lane_g == h, dli_col[h], 0.0) + jnp.where(lane_g == H4 + h, dgf_col[h], 0.0)
            dqk_ref[0:H4, sl, :] = dq
            dqk_ref[H4:2 * H4, sl, :] = dk * kscale
            dv_ref[:, sl, :] = jnp.where(lane < NLANE, dv, 0.0).astype(BF16)
            dg_ref[0, sl, :] = dg.astype(BF16)
            dbg_ref[...] += jnp.sum(dg, axis=0, keepdims=True)
            dc_sc[...] = dc_prev

    def hspec(blk):
        return pl.BlockSpec((H4, rows, GROUP), lambda i: (blk, nb - 1 - i, 0))

    gspec = pl.BlockSpec((1, rows, GROUP), lambda i: (17, nb - 1 - i, 0))
    qkspec = pl.BlockSpec((2 * H4, rows, GROUP), lambda i: (0, nb - 1 - i, 0))
    return pl.pallas_call(
        body, name="mlstm_bwd", grid=(nb,),
        in_specs=[qkspec, hspec(2), gspec, pl.BlockSpec((1, GROUP), lambda i: (0, 0)),
                  pl.BlockSpec((cb, H4, GROUP, GROUP), lambda i: (nb - 1 - i, 0, 0, 0)),
                  pl.BlockSpec((cb, H4, 1, LANES), lambda i: (nb - 1 - i, 0, 0, 0)), hspec(0)],
        out_specs=[qkspec, hspec(0), pl.BlockSpec((1, rows, GROUP), lambda i: (0, nb - 1 - i, 0)),
                   pl.BlockSpec((1, GROUP), lambda i: (0, 0))],
        out_shape=[SDS((2 * H4, S, GROUP), F32), SDS((H4, S, GROUP), BF16),
                   SDS((1, S, GROUP), BF16), SDS((1, GROUP), F32)],
        scratch_shapes=[pltpu.VMEM((H4, GROUP, GROUP), F32)],
        compiler_params=_params(("arbitrary",)),
    )(qk, u, u, bg, cst, mst, dh)


def head_norm_fwd(hm, u, hg):
    _, S, _ = hm.shape
    ts = _tile(S, 2048)

    def body(h_ref, o_ref, g_ref, t_ref):
        h = h_ref[0]
        lane = lax.broadcasted_iota(jnp.int32, h.shape, 1)
        valid = lane < ML_HEAD_DIM
        mu = jnp.sum(h, axis=-1, keepdims=True) * (1.0 / ML_HEAD_DIM)
        hc = jnp.where(valid, h - mu, 0.0)
        var = jnp.sum(hc * hc, axis=-1, keepdims=True) * (1.0 / ML_HEAD_DIM)
        hn = hc * lax.rsqrt(var + LN_EPS) * g_ref[0]
        t_ref[0] = (_sigmoid(o_ref[0]) * hn).astype(BF16)

    return pl.pallas_call(
        body, name="head_norm_fwd", grid=(H4, S // ts),
        in_specs=[pl.BlockSpec((1, ts, GROUP), lambda h, s: (h, s, 0)), pl.BlockSpec((1, ts, GROUP), lambda h, s: (12 + h, s, 0)),
                  pl.BlockSpec((1, 1, GROUP), lambda h, s: (h, 0, 0))],
        out_specs=pl.BlockSpec((1, ts, GROUP), lambda h, s: (h, s, 0)),
        out_shape=SDS((H4, S, GROUP), BF16),
        compiler_params=_params(("parallel", "parallel")),
    )(hm, u, hg)


def head_norm_bwd(hm, u, hg, dm):
    _, S, _ = hm.shape
    ts = _tile(S, 2048)

    def body(h_ref, o_ref, g_ref, d_ref, dh_ref, do_ref, dg_ref):
        @pl.when(pl.program_id(1) == 0)
        def _():
            dg_ref[...] = jnp.zeros_like(dg_ref)

        h = h_ref[0]
        lane = lax.broadcasted_iota(jnp.int32, h.shape, 1)
        valid = lane < ML_HEAD_DIM
        inv = 1.0 / ML_HEAD_DIM
        mu = jnp.sum(h, axis=-1, keepdims=True) * inv
        hc = jnp.where(valid, h - mu, 0.0)
        var = jnp.sum(hc * hc, axis=-1, keepdims=True) * inv
        rstd = lax.rsqrt(var + LN_EPS)
        xhat = hc * rstd
        g = g_ref[0]
        sig = _sigmoid(o_ref[0])
        dt = jnp.where(valid, d_ref[0], 0.0)
        do_ref[0] = (dt * xhat * g * sig * (1.0 - sig)).astype(BF16)
        dhn = dt * sig
        dg_ref[0] += jnp.sum(dhn * xhat, axis=0, keepdims=True)
        dxh = dhn * g
        m1 = jnp.sum(dxh, axis=-1, keepdims=True) * inv
        m2 = jnp.sum(dxh * xhat, axis=-1, keepdims=True) * inv
        dh_ref[0] = jnp.where(valid, rstd * (dxh - m1 - xhat * m2), 0.0)

    spec = pl.BlockSpec((1, ts, GROUP), lambda h, s: (h, s, 0))
    gspec = pl.BlockSpec((1, 1, GROUP), lambda h, s: (h, 0, 0))
    return pl.pallas_call(
        body, name="head_norm_bwd", grid=(H4, S // ts),
        in_specs=[spec, pl.BlockSpec((1, ts, GROUP), lambda h, s: (12 + h, s, 0)), gspec, spec],
        out_specs=[spec, spec, gspec],
        out_shape=[SDS((H4, S, GROUP), F32), SDS((H4, S, GROUP), BF16), SDS((H4, 1, GROUP), F32)],
        compiler_params=_params(("parallel", "arbitrary")),
    )(hm, u, hg, dm)


def _adamw_math(w, g, m, v):
    c1 = 1.0 / (1.0 - ADAM_B1 ** ADAM_STEP)
    c2 = 1.0 / (1.0 - ADAM_B2 ** ADAM_STEP)
    nm = ADAM_B1 * m + (1.0 - ADAM_B1) * g
    nv = ADAM_B2 * v + (1.0 - ADAM_B2) * (g * g)
    return -ADAM_LR * ((nm * c1) / (jnp.sqrt(nv * c2) + ADAM_EPS) + ADAM_WD * w), nm, nv


def _row_tile(R, cap=512):
    return R if R <= cap else max(d for d in range(8, cap + 1, 8) if R % d == 0)


def adamw_into(w, m, v, g, outs, idx, after, name):
    R, C = g.shape
    tr = _row_tile(R)
    lead = (0,) * len(idx)

    def body(w_ref, m_ref, v_ref, g_ref, *rest):
        go_ref, d_ref, nm_ref, nv_ref, token = rest[-5:]
        token[...] = jnp.zeros_like(token)
        gv = g_ref[...]
        d, nm, nv = _adamw_math(w_ref[lead], gv, m_ref[lead], v_ref[lead])
        go_ref[lead] = gv
        d_ref[lead] = d
        nm_ref[lead] = nm
        nv_ref[lead] = nv

    blk = pl.BlockSpec((1,) * len(idx) + (tr, C), lambda r: idx + (r, 0))
    any_space = pl.BlockSpec(memory_space=pl.ANY)
    in_specs, args, aliases = [blk, blk, blk, pl.BlockSpec((tr, C), lambda r: (r, 0)), any_space], [w, m, v, g, g if after is None else after], {}
    if outs is not None:
        in_specs += [any_space] * 4
        args += list(outs)
        aliases = {5 + i: i for i in range(4)}
    out = pl.pallas_call(
        body, name=name, grid=(R // tr,),
        in_specs=in_specs, out_specs=[blk] * 4 + [pl.BlockSpec((8, LANES), lambda r: (0, 0))],
        out_shape=[SDS(w.shape, F32)] * 4 + [SDS((8, LANES), F32)],
        input_output_aliases=aliases, compiler_params=_params(("arbitrary",)),
    )(*args)
    return out[:4], out[4]


def adamw(w, g, m, v, name):
    R, C = w.shape
    tr = _row_tile(R)

    def body(w_ref, g_ref, m_ref, v_ref, d_ref, nm_ref, nv_ref):
        d_ref[...], nm_ref[...], nv_ref[...] = _adamw_math(w_ref[...], g_ref[...], m_ref[...], v_ref[...])

    spec = pl.BlockSpec((tr, C), lambda i: (i, 0))
    return pl.pallas_call(
        body, name=name, grid=(R // tr,),
        in_specs=[spec] * 4, out_specs=[spec] * 3,
        out_shape=[SDS((R, C), F32)] * 3,
        compiler_params=_params(("parallel",)),
    )(w, g, m, v)


HBM = pl.BlockSpec(memory_space=pl.ANY)
ROW_SPLIT = 4
PAIR_SPLIT = 1


def _position():
    x, y, c = lax.axis_index("x"), lax.axis_index("y"), lax.axis_index("c")
    return x, y, c, [(1 - x, y), (x, 1 - y), (1 - x, 1 - y)]


def _unique(items):
    arrays = []
    for a, _ in items:
        if not any(a is b for b in arrays):
            arrays.append(a)
    return arrays, [next(i for i, b in enumerate(arrays) if b is a) for a, _ in items]


def place_own(items, me, after, name):
    arrays, src_of = _unique(items)
    n = len(items)
    shapes = [a.shape[len(p):] for a, p in items]

    def body(me_ref, *refs):
        for t in range(n):
            refs[n + 1 + t][0] = refs[t][(0,) * len(items[t][1])]

    in_specs, out_specs = [], []
    for (a, p), shp in zip(items, shapes):
        blk = shp[:-2] + (shp[-2] // ROW_SPLIT, shp[-1])
        lead = (0,) * (len(shp) - 2)
        in_specs.append(pl.BlockSpec((1,) * len(p) + blk, functools.partial(lambda r, me_ref, p, lead: p + lead + (r, 0), p=p, lead=lead)))
        out_specs.append(pl.BlockSpec((1,) + blk, functools.partial(lambda r, me_ref, lead: (me_ref[0],) + lead + (r, 0), lead=lead)))
    in_specs.append(pl.BlockSpec(memory_space=pl.ANY))
    return pl.pallas_call(
        body, name=name,
        grid_spec=pltpu.PrefetchScalarGridSpec(num_scalar_prefetch=1, grid=(ROW_SPLIT,), in_specs=in_specs, out_specs=out_specs),
        out_shape=[SDS((N_CHIPS,) + tuple(shp), a.dtype) for shp, (a, _) in zip(shapes, items)],
        compiler_params=_params(("parallel",)),
    )(me, *[arrays[i] for i in src_of], after)


SEM = pl.BlockSpec(memory_space=pltpu.SEMAPHORE)
IN_HBM = pl.BlockSpec(memory_space=pltpu.HBM)
DATAFLOW = pltpu.SideEffectType.DATAFLOW_SIDE_EFFECTING


def split_start(bufs, plan, n_copies, after, name):
    n = len(bufs)

    def body(*refs):
        send, recv, token = refs[n + 1], refs[n + 2], refs[-1]
        x, y, c, chips = _position()
        for k, (src, dst, dev) in enumerate(plan(refs[:n], x, y, c, chips)):
            pltpu.make_async_remote_copy(src_ref=src, dst_ref=dst, send_sem=send.at[k], recv_sem=recv.at[k],
                                         device_id=dev, device_id_type=MESH).start()
        token[...] = jnp.zeros_like(token)

    out = pl.pallas_call(
        body, name=name,
        out_shape=(pltpu.SemaphoreType.DMA((n_copies,)), pltpu.SemaphoreType.DMA((n_copies,)),
                   *[pltpu.HBM(b.shape, b.dtype) for b in bufs], SDS((8, LANES), F32)),
        in_specs=[IN_HBM] * n + [pl.BlockSpec(memory_space=pl.ANY)],
        out_specs=(SEM, SEM, *[IN_HBM] * n, pl.BlockSpec(memory_space=pltpu.VMEM)),
        input_output_aliases={i: 2 + i for i in range(n)},
        compiler_params=pltpu.CompilerParams(has_side_effects=DATAFLOW),
    )(*[pltpu.with_memory_space_constraint(b, pltpu.HBM) for b in bufs], after)
    return out[0], out[1], list(out[2:2 + n]), out[-1]


def split_wait(send, recv, bufs, plan, after, name):
    n = len(bufs)

    def body(*refs):
        send_ref, recv_ref = refs[n], refs[n + 1]
        x, y, c, chips = _position()
        for k, (src, dst, dev) in enumerate(plan(refs[:n], x, y, c, chips)):
            cp = pltpu.make_async_remote_copy(src_ref=src, dst_ref=dst, send_sem=send_ref.at[k], recv_sem=recv_ref.at[k],
                                              device_id=dev, device_id_type=MESH)
            cp.wait_send()
            cp.wait_recv()

    return list(pl.pallas_call(
        body, name=name, out_shape=tuple(pltpu.HBM(b.shape, b.dtype) for b in bufs),
        in_specs=[IN_HBM] * n + [SEM, SEM, pl.BlockSpec(memory_space=pl.ANY)], out_specs=tuple([IN_HBM] * n),
        input_output_aliases={i: i for i in range(n)},
        compiler_params=pltpu.CompilerParams(has_side_effects=DATAFLOW),
    )(*bufs, send, recv, after))


def _gather_plan(shapes, landing):
    n = len(shapes)

    def plan(refs, x, y, c, chips):
        out = []
        for t in range(n):
            half = shapes[t][0] // 2
            rows = pl.ds(c * half, half)
            for cx, cy in chips:
                slot = 2 * cx + cy if landing else 2 * x + y
                out.append((refs[t].at[rows], refs[n + t].at[slot, rows], (cx, cy, c)))
        return out

    return plan


def gather_start(shards, placed, after, name):
    shapes = [s.shape for s in shards]
    send, recv, bufs, token = split_start(list(shards) + list(placed), _gather_plan(shapes, False), 3 * len(shards), after, name)
    return (send, recv, bufs, shapes), token


def gather_wait(state, after, name):
    send, recv, bufs, shapes = state
    return split_wait(send, recv, bufs, _gather_plan(shapes, True), after, name)[len(shapes):]


def gather_pass_on(placed, shapes, name):
    n = len(placed)

    def body(*refs):
        outs, send, recv = refs[n:2 * n], refs[2 * n], refs[2 * n + 1]
        x, y, c, chips = _position()
        cps = []
        for t in range(n):
            half = shapes[t][0] // 2
            for j, (cx, cy) in enumerate(chips):
                piece = outs[t].at[2 * cx + cy, pl.ds(c * half, half)]
                cp = pltpu.make_async_remote_copy(src_ref=piece, dst_ref=piece, send_sem=send.at[3 * t + j], recv_sem=recv.at[3 * t + j],
                                                  device_id=(x, y, 1 - c), device_id_type=MESH)
                cp.start()
                cps.append(cp)
        for t in range(n):
            half = shapes[t][0] // 2
            for j, (cx, cy) in enumerate(chips):
                piece = outs[t].at[2 * cx + cy, pl.ds((1 - c) * half, half)]
                pltpu.make_async_remote_copy(src_ref=piece, dst_ref=piece, send_sem=send.at[3 * t + j], recv_sem=recv.at[3 * t + j],
                                             device_id=(x, y, 1 - c), device_id_type=MESH).wait_recv()
        for cp in cps:
            cp.wait_send()

    return pl.pallas_call(
        body, name=name,
        in_specs=[HBM] * n, out_specs=[HBM] * n,
        out_shape=[SDS(p.shape, p.dtype) for p in placed],
        input_output_aliases={t: t for t in range(n)},
        scratch_shapes=[pltpu.SemaphoreType.DMA((3 * n,))] * 2,
    )(*placed)


def _flip(k, x, y, c):
    return ((1 - x) if k & 4 else x, (1 - y) if k & 2 else y, (1 - c) if k & 1 else c)


def small_allgather(v, reduce):
    R, C = v.shape

    def body(v_ref, o_ref, *scratch):
        if reduce:
            buf, send, recv = scratch
        else:
            buf, (send, recv) = o_ref, scratch
        x, y, c, _ = _position()
        me = 4 * x + 2 * y + c
        buf[me] = v_ref[...]
        sends = []
        for k in range(1, N_DEV):
            cp = pltpu.make_async_remote_copy(src_ref=v_ref, dst_ref=buf.at[me], send_sem=send.at[k - 1], recv_sem=recv.at[k - 1],
                                              device_id=_flip(k, x, y, c), device_id_type=MESH)
            cp.start()
            sends.append(cp)
        for k in range(1, N_DEV):
            px, py, pc = _flip(k, x, y, c)
            pltpu.make_async_remote_copy(src_ref=v_ref, dst_ref=buf.at[4 * px + 2 * py + pc], send_sem=send.at[k - 1],
                                         recv_sem=recv.at[k - 1], device_id=(px, py, pc), device_id_type=MESH).wait_recv()
        for cp in sends:
            cp.wait_send()
        if reduce:
            acc = buf[0]
            for i in range(1, N_DEV):
                acc = acc + buf[i]
            o_ref[...] = acc

    vm = pl.BlockSpec(memory_space=pltpu.VMEM)
    sems = [pltpu.SemaphoreType.DMA((N_DEV - 1,)), pltpu.SemaphoreType.DMA((N_DEV - 1,))]
    return pl.pallas_call(
        body, name="small_allreduce" if reduce else "small_allgather",
        in_specs=[vm], out_specs=vm,
        out_shape=SDS((R, C) if reduce else (N_DEV, R, C), F32),
        scratch_shapes=([pltpu.VMEM((N_DEV, R, C), F32)] if reduce else []) + sems,
    )(v)


def rs_exchange_sibling(gs):
    n = len(gs)

    def body(*refs):
        ins, outs, send, recv = refs[:n], refs[n:2 * n], refs[2 * n], refs[2 * n + 1]
        x, y, c, _ = _position()
        cps = []
        for t in range(n):
            cp = pltpu.make_async_remote_copy(src_ref=ins[t].at[:, 1 - c], dst_ref=outs[t], send_sem=send.at[t], recv_sem=recv.at[t],
                                              device_id=(x, y, 1 - c), device_id_type=MESH)
            cp.start()
            cps.append(cp)
        for cp in cps:
            cp.wait()

    return pl.pallas_call(
        body, name="rs_exchange_sibling", in_specs=[HBM] * n, out_specs=[HBM] * n,
        out_shape=[SDS((g.shape[0],) + g.shape[2:], g.dtype) for g in gs],
        scratch_shapes=[pltpu.SemaphoreType.DMA((n,)), pltpu.SemaphoreType.DMA((n,))],
    )(*gs)


def rs_pair_add(gs, rs, c):
    n = len(gs)

    def body(c_ref, *refs):
        for t in range(n):
            refs[2 * n + t][0] = (refs[t][0, 0].astype(F32) + refs[n + t][0].astype(F32)).astype(BF16)

    in_specs, out_specs, out_shape = [], [], []
    for g in gs:
        _, _, h, C = g.shape
        in_specs.append(pl.BlockSpec((1, 1, h // PAIR_SPLIT, C), lambda j, r, c_ref: (j, c_ref[0], r, 0)))
    for g in gs:
        _, _, h, C = g.shape
        spec = pl.BlockSpec((1, h // PAIR_SPLIT, C), lambda j, r, c_ref: (j, r, 0))
        in_specs.append(spec)
        out_specs.append(spec)
        out_shape.append(SDS((N_CHIPS, h, C), BF16))
    return pl.pallas_call(
        body, name="rs_pair_add",
        grid_spec=pltpu.PrefetchScalarGridSpec(num_scalar_prefetch=1, grid=(N_CHIPS, PAIR_SPLIT), in_specs=in_specs, out_specs=out_specs),
        out_shape=out_shape, compiler_params=_params(("parallel", "parallel")),
    )(c, *gs, *rs)


def _rs_plan(n):
    def plan(refs, x, y, c, chips):
        return [(refs[t].at[2 * cx + cy], refs[n + t].at[j], (cx, cy, c)) for t in range(n) for j, (cx, cy) in enumerate(chips)]

    return plan


def rs_chip_add(ps, qs, me_c):
    n = len(ps)

    def body(me_ref, *refs):
        for t in range(n):
            q = refs[n + t]
            refs[2 * n + t][0] = ((refs[t][0].astype(F32) + q[0].astype(F32)) + q[1].astype(F32)) + q[2].astype(F32)

    in_specs, out_specs, out_shape = [], [], []
    for p in ps:
        _, h, C = p.shape
        in_specs.append(pl.BlockSpec((1, h // ROW_SPLIT, C), lambda r, me_ref: (me_ref[0], r, 0)))
    for p in ps:
        _, h, C = p.shape
        in_specs.append(pl.BlockSpec((3, h // ROW_SPLIT, C), lambda r, me_ref: (0, r, 0)))
        out_specs.append(pl.BlockSpec((1, h // ROW_SPLIT, C), lambda r, me_ref: (me_ref[1], r, 0)))
        out_shape.append(SDS((2, h, C), F32))
    return pl.pallas_call(
        body, name="rs_chip_add",
        grid_spec=pltpu.PrefetchScalarGridSpec(num_scalar_prefetch=1, grid=(ROW_SPLIT,), in_specs=in_specs, out_specs=out_specs),
        out_shape=out_shape, compiler_params=_params(("parallel",)),
    )(me_c, *ps, *qs)


def rs_share(rs):
    n = len(rs)

    def body(*refs):
        outs, send, recv = refs[n:2 * n], refs[2 * n], refs[2 * n + 1]
        x, y, c, _ = _position()
        cps = []
        for t in range(n):
            cp = pltpu.make_async_remote_copy(src_ref=outs[t].at[c], dst_ref=outs[t].at[c], send_sem=send.at[t], recv_sem=recv.at[t],
                                              device_id=(x, y, 1 - c), device_id_type=MESH)
            cp.start()
            cps.append(cp)
        for cp in cps:
            cp.wait()

    return pl.pallas_call(
        body, name="rs_share", in_specs=[HBM] * n, out_specs=[HBM] * n,
        out_shape=[SDS(r.shape, r.dtype) for r in rs],
        input_output_aliases={t: t for t in range(n)},
        scratch_shapes=[pltpu.SemaphoreType.DMA((n,))] * 2,
    )(*rs)


def rs_begin(gs, after, name):
    c = lax.axis_index("c")
    n = len(gs)
    g5 = [g.reshape(N_CHIPS, 2, g.shape[1] // 2, g.shape[2]) for g in gs]
    from_sibling = rs_exchange_sibling(g5)
    pair = rs_pair_add(g5, from_sibling, jnp.reshape(c, (1,)).astype(jnp.int32))
    lands = [lax.empty((3,) + p.shape[1:], p.dtype) for p in pair]
    send, recv, bufs, token = split_start(list(pair) + lands, _rs_plan(n), 3 * n, from_sibling[0] if after is None else after, name)
    return (send, recv, bufs, [g.shape for g in gs]), token


def rs_end(state, after, name):
    x, y, c = lax.axis_index("x"), lax.axis_index("y"), lax.axis_index("c")
    send, recv, bufs, shapes = state
    n = len(shapes)
    bufs = split_wait(send, recv, bufs, _rs_plan(n), after, name)
    half = rs_chip_add(bufs[:n], bufs[n:], jnp.stack([2 * x + y, c]).astype(jnp.int32))
    both = rs_share(half)
    return [b.reshape(s[1], s[2]) for b, s in zip(both, shapes)]


def _pad_last(a, n):
    return jnp.pad(a, [(0, 0)] * (a.ndim - 1) + [(0, n - a.shape[-1])])


def _heads_to_groups(w):
    k = w.shape[0]
    return _pad_last(w.reshape(k, ML_HEADS, ML_HEAD_DIM).transpose(1, 0, 2), GROUP)


def _groups_to_heads(g):
    return g[:, :, :ML_HEAD_DIM].transpose(1, 0, 2).reshape(g.shape[1], D_TOK)


def _cols_to_groups(w):
    k, n = w.shape
    return w.reshape(k, n // GROUP, GROUP).transpose(1, 0, 2)


def _groups_to_cols(g):
    n, k, _ = g.shape
    return g.transpose(1, 0, 2).reshape(k, n * GROUP)


def _chips_to_cols(a):
    return a.transpose(1, 0, 2).reshape(a.shape[1], -1)


def _cols_to_chips(w):
    k, n = w.shape
    return w.reshape(k, N_CHIPS, n // N_CHIPS).transpose(1, 0, 2)


def _mlstm_in_groups(w):
    parts = [_heads_to_groups(w[:, i * D_TOK:(i + 1) * D_TOK]) for i in range(4)]
    gates = _pad_last(w[:, 4 * D_TOK:4 * D_TOK + 2 * ML_HEADS], GROUP)[None]
    qmem = w[:, 4 * D_TOK + 2 * ML_HEADS:][None]
    return jnp.concatenate(parts + [qmem, gates], axis=0)


def _mlstm_in_ungroup(g):
    parts = [_groups_to_heads(g[4 * i:4 * i + 4]) for i in range(4)]
    return jnp.concatenate(parts + [g[17][:, :2 * ML_HEADS], g[16]], axis=1)


def _taps_to_groups(w, width):
    taps = w.shape[0]
    g = _pad_last(w.reshape(taps, -1, width), GROUP).transpose(1, 0, 2)
    return jnp.pad(g, ((0, 0), (0, 8 - taps), (0, 0)))


def _groups_to_taps(g, taps, width):
    return g[:, :taps, :width].transpose(1, 0, 2).reshape(taps, -1)


SMALL_IN_COLS = 384
SMALL_OUT_COLS = 1536
SECTION = 8


class _Gathered:
    def __init__(self, make_src, groups, me):
        self.groups, self.states, self.ready = groups, [], {}
        self.group_of = {k: gi for gi, g in enumerate(groups) for k in g}
        token, self.first = me, None
        for gi, g in enumerate(groups):
            srcs = [make_src(k, None if gi == 0 else token[0:1, 0:1]) for k in g]
            placed = place_own([(a, ()) for a in srcs], me, token, f"place_own_{gi}")
            state, token = gather_start(srcs, placed, token, f"gather_start_{gi}")
            self.states.append(state)
            if gi == 0:
                self.first = token[0:1, 0:1]
        self.started = token

    def _get(self, key, after):
        gi = self.group_of[key]
        if gi not in self.ready:
            got = gather_wait(self.states[gi], after if gi else self.started, f"gather_wait_{gi}")
            self.ready[gi] = dict(zip(self.groups[gi], gather_pass_on(got, self.states[gi][3], f"gather_pass_on_{gi}")))
        return self.ready[gi][key]

    def ffn(self, l, i, after):
        return tuple(self._get((n, l, i), after) for n in ("wg", "wu", "wd"))

    def mixer(self, l, after):
        win = _chips_to_cols(self._get(("win", l), after))
        win = _cols_to_groups(win) if l % 2 == 0 else _mlstm_in_groups(win)
        wkv = _cols_to_groups(self._get(("wkv", l), after).reshape(D_MODEL, 2 * D_XA))
        wout = self._get(("wout", l), after)
        if l % 2:
            wout = wout.reshape(D_MODEL, D_MODEL)
            tok = jnp.pad(wout[:D_TOK].reshape(ML_HEADS, ML_HEAD_DIM, D_MODEL), ((0, 0), (0, GROUP - ML_HEAD_DIM), (0, 0)))
            wout = jnp.concatenate([tok, wout[D_TOK:][None]], axis=0)
        return win, wkv, wout


class _GradSink:
    def __init__(self, apply):
        self.queue, self.apply, self.count, self.done = [], apply, 0, None

    @staticmethod
    def _by_chip(key, g):
        if key[0] == "wkv":
            return _groups_to_cols(g).reshape(N_CHIPS, D_MODEL // N_CHIPS, 2 * D_XA)
        if key[0] == "win":
            return _cols_to_chips(_groups_to_cols(g) if key[1] % 2 == 0 else _mlstm_in_ungroup(g))
        if key[0] == "wout" and key[1] % 2:
            full = jnp.concatenate([g[:ML_HEADS, :ML_HEAD_DIM].reshape(D_TOK, D_MODEL), g[ML_HEADS]], axis=0)
            return full.reshape(N_CHIPS, D_MODEL // N_CHIPS, D_MODEL)
        return g

    def push(self, grads):
        keys = list(grads)
        state, token = rs_begin([self._by_chip(k, grads[k]) for k in keys], self.done, f"rs_start_{self.count}")
        if self.queue:
            self._finish(token)
        self.queue.append((keys, state, self.count))
        self.count += 1
        return token

    def flush(self):
        self._finish(self.done)

    def _finish(self, after):
        keys, state, i = self.queue.pop(0)
        for key, g in zip(keys, rs_end(state, after, f"rs_wait_{i}")):
            self.done = self.apply(key, g, self.done)


def _local_step(x, mem, tgt, P, weights, sink):
    memb = mem.astype(BF16)
    saved = []
    pin0 = getattr(weights, "first", None)
    X, Xb = x, (x if pin0 is None else x + pin0).astype(BF16)
    after = Xb
    for l in range(DEPTH):
        s = {}
        s["x0b"] = Xb
        s["wa"] = weights.ffn(l, 0, after)
        s["g1a"], s["u1a"], s["ha"], s["z1"], X1, X1b = ffn_fwd(Xb, X, *s["wa"], P["ln_g"][l][0], P["ln_b"][l][0])
        s["x1b"] = X1b
        s["wm"] = win, wkv, wout = weights.mixer(l, X1b)
        u = proj(X1b, win, "mixer_in")
        kv = proj(memb, wkv, "mem_kv")
        s["u"], s["kv"] = u, kv
        if l % 2 == 0:
            tok = conv_mixer_fwd(u, P["convw"])
            qg = 9
        else:
            s["qk"] = qk_conv_fwd(u, P["qkw"])
            s["hm"], s["cst"], s["mst"] = mlstm_fwd(s["qk"], u, P["bg"])
            tok = head_norm_fwd(s["hm"], u, P["hg"])
            qg = 16
        xa = xattn_fwd(u, qg, kv)
        s["m"] = jnp.concatenate([tok, xa], axis=0)
        s["z2"], X2, X2b = contract_ln(s["m"], wout, X1, P["ln_g"][l][1], P["ln_b"][l][1], 1.0, "mixer_out_ln")
        s["x2b"] = X2b
        s["wb"] = weights.ffn(l, 1, X2b)
        s["g1b"], s["u1b"], s["hb"], s["z3"], X, Xb = ffn_fwd(X2b, X2, *s["wb"], P["ln_g"][l][2], P["ln_b"][l][2])
        after = Xb
        saved.append(s)

    loss, dX = loss_grad(X, tgt)

    G = {"ln_g": [[None] * 3 for _ in range(DEPTH)], "ln_b": [[None] * 3 for _ in range(DEPTH)]}
    pin = [jnp.zeros((1, 1), F32)]

    def ffn_backward(l, i, dX, z, xinb, g1, u1, h, w):
        k = 2 * i
        dgb, dub, dx, dyb, G["ln_g"][l][k], G["ln_b"][l][k] = ffn_bwd(dX, z, P["ln_g"][l][k] + pin[0], w[2], w[0], w[1], g1, u1)
        grads = {("wd", l, i): wgrad(h, dyb, BF16, "wgrad_down"), ("wg", l, i): wgrad(dgb, xinb, BF16, "wgrad_gate"),
                 ("wu", l, i): wgrad(dub, xinb, BF16, "wgrad_up")}
        return dx, grads

    for l in reversed(range(DEPTH)):
        s = saved[l]
        win, wkv, wout = s["wm"]
        dX, grads = ffn_backward(l, 1, dX, s["z3"], s["x2b"], s["g1b"], s["u1b"], s["hb"], s["wb"])
        dm, dz2, dz2b, G["ln_g"][l][1], G["ln_b"][l][1] = mixer_out_bwd(dX, s["z2"], P["ln_g"][l][1], wout)
        grads[("wout", l)] = wgrad(s["m"], dz2b, BF16, "wgrad_out")
        u, kv = s["u"], s["kv"]
        if l % 2 == 0:
            db, dc, dxi, G["convw"] = conv_mixer_bwd(u, P["convw"], dm)
            dq, dkv = xattn_bwd(u, 9, kv, dm, 3)
            du = jnp.concatenate([db, dc, dxi, dq], axis=0)
        else:
            dh, do, G["hg"] = head_norm_bwd(s["hm"], u, P["hg"], dm)
            dqk, dv, dgate, G["bg"] = mlstm_bwd(s["qk"], u, P["bg"], s["cst"], s["mst"], dh)
            duqk, G["qkw"] = qk_conv_bwd(u, P["qkw"], dqk)
            dq, dkv = xattn_bwd(u, 16, kv, dm, 4)
            du = jnp.concatenate([duqk, dv, do, dq, dgate], axis=0)
        grads[("win", l)] = wgrad(s["x1b"], du, BF16, "wgrad_in")
        grads[("wkv", l)] = wgrad(memb, dkv.astype(BF16), BF16, "wgrad_kv")
        dX = contract_t(du, win, dz2, "mixer_in_bwd")
        pin[0] = sink.push(grads)[0:1, 0:1]
        dX, grads = ffn_backward(l, 0, dX, s["z1"], s["x0b"], s["g1a"], s["u1a"], s["ha"], s["wa"])
        pin[0] = sink.push(grads)[0:1, 0:1]
    sink.flush()
    return loss, dX, G


def kernel(x, mem, ln_g, ln_b, ffn_w_gate, ffn_w_up, ffn_w_down, w_kv_mem, w_out, w_in_conv, conv_w, w_in_mlstm, b_gates, qk_conv_w, head_norm_g, loss_target, m_ln_g, m_ln_b, m_ffn_w_gate, m_ffn_w_up, m_ffn_w_down, m_w_kv_mem, m_w_out, m_w_in_conv, m_conv_w, m_w_in_mlstm, m_b_gates, m_qk_conv_w, m_head_norm_g, v_ln_g, v_ln_b, v_ffn_w_gate, v_ffn_w_up, v_ffn_w_down, v_w_kv_mem, v_w_out, v_w_in_conv, v_conv_w, v_w_in_mlstm, v_b_gates, v_qk_conv_w, v_head_norm_g):
    cx, cy = lax.axis_index("x"), lax.axis_index("y")
    chip = 2 * cx + cy

    def make_src(key, pin):
        if key[0] in ("wg", "wu"):
            w = jnp.swapaxes((ffn_w_gate if key[0] == "wg" else ffn_w_up)[key[1], key[2]], 0, 1)
        elif key[0] == "wd":
            w = ffn_w_down[key[1], key[2]]
        elif key[0] == "win":
            w = (w_in_conv, w_in_mlstm)[key[1]][0]
        else:
            w = (w_kv_mem if key[0] == "wkv" else w_out)[key[1]]
        return (w if pin is None else w + pin).astype(BF16)

    ffn_keys = lambda l, i: [("wg", l, i), ("wu", l, i), ("wd", l, i)]
    mixer_keys = lambda l: [("win", l), ("wkv", l), ("wout", l)]
    groups = [ffn_keys(0, 0), mixer_keys(0) + mixer_keys(1), ffn_keys(0, 1), ffn_keys(1, 0), ffn_keys(1, 1)]
    gathered = _Gathered(make_src, groups, jnp.reshape(chip, (1,)).astype(jnp.int32))

    def section(a, width):
        a = a.reshape(-1, a.shape[-1])
        return jnp.pad(a, ((0, SECTION - a.shape[0]), (0, width - a.shape[1])))

    small = jnp.concatenate([section(a, SMALL_IN_COLS) for a in (ln_g, ln_b, conv_w, qk_conv_w)], axis=0) + gathered.first
    smalls = small_allgather(small, reduce=False)[0::2]
    ln_g_full = _chips_to_cols(smalls[:, 0:6, 0:256]).reshape(DEPTH, 3, 1, D_MODEL)
    ln_b_full = _chips_to_cols(smalls[:, 8:14, 0:256]).reshape(DEPTH, 3, 1, D_MODEL)
    conv_w_full = _chips_to_cols(smalls[:, 16:19, 0:192])
    qk_w_full = _chips_to_cols(smalls[:, 24:28, 0:384])

    P = {"ln_g": ln_g_full, "ln_b": ln_b_full, "convw": _taps_to_groups(conv_w_full, GROUP),
         "qkw": _taps_to_groups(qk_w_full, ML_HEAD_DIM), "bg": _pad_last(b_gates, GROUP),
         "hg": _pad_last(head_norm_g[0], GROUP)[:, None, :]}

    weights = {"ln_g": ln_g, "ln_b": ln_b, "ffn_w_gate": ffn_w_gate, "ffn_w_up": ffn_w_up, "ffn_w_down": ffn_w_down,
               "w_kv_mem": w_kv_mem, "w_out": w_out, "w_in_conv": w_in_conv, "conv_w": conv_w, "w_in_mlstm": w_in_mlstm,
               "b_gates": b_gates, "qk_conv_w": qk_conv_w, "head_norm_g": head_norm_g}
    ms = {"ln_g": m_ln_g, "ln_b": m_ln_b, "ffn_w_gate": m_ffn_w_gate, "ffn_w_up": m_ffn_w_up, "ffn_w_down": m_ffn_w_down,
          "w_kv_mem": m_w_kv_mem, "w_out": m_w_out, "w_in_conv": m_w_in_conv, "conv_w": m_conv_w, "w_in_mlstm": m_w_in_mlstm,
          "b_gates": m_b_gates, "qk_conv_w": m_qk_conv_w, "head_norm_g": m_head_norm_g}
    vs = {"ln_g": v_ln_g, "ln_b": v_ln_b, "ffn_w_gate": v_ffn_w_gate, "ffn_w_up": v_ffn_w_up, "ffn_w_down": v_ffn_w_down,
          "w_kv_mem": v_w_kv_mem, "w_out": v_w_out, "w_in_conv": v_w_in_conv, "conv_w": v_conv_w, "w_in_mlstm": v_w_in_mlstm,
          "b_gates": v_b_gates, "qk_conv_w": v_qk_conv_w, "head_norm_g": v_head_norm_g}
    names = list(weights)
    owner = {"wg": ("ffn_w_gate", True), "wu": ("ffn_w_up", True), "wd": ("ffn_w_down", False), "wkv": ("w_kv_mem", False),
             "wout": ("w_out", False), "win": None}
    updated = {}

    def apply(key, g, after):
        name, transposed = owner[key[0]] or (("w_in_conv", "w_in_mlstm")[key[1]], False)
        idx = (0,) if key[0] == "win" else tuple(key[1:])
        view = (lambda a: jnp.swapaxes(a, -1, -2)) if transposed else (lambda a: a)
        updated[name], token = adamw_into(view(weights[name]), view(ms[name]), view(vs[name]), g, updated.get(name), idx, after,
                                          "adamw_" + name + "_" + "_".join(map(str, idx)))
        return token

    sink = _GradSink(apply)
    loss, grad_x, G = _local_step(x[0], mem[0], loss_target[0], P, gathered, sink)

    dln_g = jnp.concatenate([G["ln_g"][l][k] for l in range(DEPTH) for k in range(3)], axis=0)
    dln_b = jnp.concatenate([G["ln_b"][l][k] for l in range(DEPTH) for k in range(3)], axis=0)
    lane = lax.broadcasted_iota(jnp.int32, (1, GROUP), 1)
    misc = jnp.where(lane < 8, G["bg"], 0.0) + jnp.where(lane == 8, loss, 0.0) + sink.done[0:1, 0:1]
    parts = (dln_g, dln_b, _groups_to_taps(G["convw"], 3, GROUP), misc, _groups_to_taps(G["qkw"], 4, ML_HEAD_DIM),
             G["hg"][:, 0, :ML_HEAD_DIM])
    tot = small_allgather(jnp.concatenate([section(a, SMALL_OUT_COLS) for a in parts], axis=0), reduce=True)
    loss_total = tot[24, 8]

    small_grads = {
        "ln_g": lax.dynamic_slice(tot[0:6, 0:D_MODEL], (0, chip * 256), (6, 256)).reshape(DEPTH, 3, 256),
        "ln_b": lax.dynamic_slice(tot[8:14, 0:D_MODEL], (0, chip * 256), (6, 256)).reshape(DEPTH, 3, 256),
        "conv_w": lax.dynamic_slice(tot[16:19, 0:D_TOK], (0, chip * 192), (3, 192))[None],
        "b_gates": tot[24:25, 0:8],
        "qk_conv_w": lax.dynamic_slice(tot[32:36, 0:2 * D_TOK], (0, chip * 384), (4, 384))[None],
        "head_norm_g": tot[40:44, 0:ML_HEAD_DIM][None],
    }
    grads, deltas, new_m, new_v = [], [], [], []
    for nme in names:
        if nme in updated:
            back = (lambda a: jnp.swapaxes(a, -1, -2)) if nme in ("ffn_w_gate", "ffn_w_up") else (lambda a: a)
            g, d, nm, nv = (back(a) for a in updated[nme])
        else:
            w, g = weights[nme], small_grads[nme]
            two = (math.prod(w.shape[:-1]), w.shape[-1])
            d, nm, nv = (a.reshape(w.shape) for a in adamw(w.reshape(two), g.reshape(two), ms[nme].reshape(two),
                                                           vs[nme].reshape(two), "adamw_" + nme))
        grads.append(g)
        deltas.append(d)
        new_m.append(nm)
        new_v.append(nv)
    return (loss_total, grad_x[None], *grads, *deltas, *new_m, *new_v)
```

```python
import functools
import math

import jax
import jax.numpy as jnp
from jax import lax
from jax.experimental import pallas as pl
from jax.experimental.pallas import tpu as pltpu

F32 = jnp.float32
BF16 = jnp.bfloat16
SDS = jax.ShapeDtypeStruct

D_MODEL = 1024
DEPTH = 2
N_MEM = 256
XA_HEADS = 4
XA_HEAD_DIM = 64
D_XA = 256
D_TOK = 768
ML_HEADS = 4
ML_HEAD_DIM = 192
ML_CHUNK = 64
D_FF = 2816
LN_EPS = 1e-5
ALPHA = (2.0 * DEPTH) ** 0.25
N_CHIPS = 4
N_DEV = 8
FF_SHARD = D_FF // N_CHIPS
GROUP = 256
NEG = -1e30

ADAM_LR = 0.001
ADAM_B1 = 0.9
ADAM_B2 = 0.999
ADAM_EPS = 1e-08
ADAM_WD = 0.01
ADAM_STEP = 10

VMEM_LIMIT = 56 * 1024 * 1024

NN = ((1,), (0,))
NT = ((1,), (1,))
TN = ((0,), (0,))
MESH = pl.DeviceIdType.MESH


def _dot(a, b, dims):
    return lax.dot_general(a, b, (dims, ((), ())), preferred_element_type=F32)


def _bdot(a, b, ca, cb):
    dims = (((ca,), (cb,)), ((0,), (0,)))
    ah, bh = a.astype(BF16), b.astype(BF16)
    al, bl = (a - ah.astype(F32)).astype(BF16), (b - bh.astype(F32)).astype(BF16)
    dot = functools.partial(lax.dot_general, dimension_numbers=dims, preferred_element_type=F32)
    return dot(ah, bh) + dot(al, bh) + dot(ah, bl)


def _bdot1(a, b, ca, cb):
    return lax.dot_general(a.astype(BF16), b.astype(BF16), (((ca,), (cb,)), ((0,), (0,))), preferred_element_type=F32)


def _sigmoid(x):
    return 1.0 / (1.0 + jnp.exp(-x))


def _params(sem, vmem=VMEM_LIMIT):
    return pltpu.CompilerParams(dimension_semantics=sem, vmem_limit_bytes=vmem)


def _tile(n, want):
    t = min(n, want)
    assert n % t == 0, (n, t)
    return t


def _layer_norm(z, gamma, beta):
    mu = jnp.mean(z, axis=-1, keepdims=True)
    zc = z - mu
    var = jnp.mean(zc * zc, axis=-1, keepdims=True)
    return zc * lax.rsqrt(var + LN_EPS) * gamma + beta


def _column_halves(n):
    mid = -(-n // (2 * 128)) * 128
    return ((0, mid), (mid, n))


def _resident(shape):
    return pl.BlockSpec(shape, lambda *_: (0,) * len(shape), pipeline_mode=pl.Buffered(1))


def _group_block(G, want):
    return max(d for d in range(1, max(1, min(G, want)) + 1) if G % d == 0)


def ffn_fwd(xb, x, wg, wu, wd, gamma, beta):
    S, K = xb.shape
    G, N, _ = wg.shape
    ts = _tile(S, 512)

    def body(xb_ref, x_ref, wg_ref, wu_ref, wd_ref, gm_ref, bt_ref, g_ref, u_ref, h_ref, z_ref, xn_ref, xnb_ref):
        j = pl.program_id(1)
        xv = xb_ref[...]
        g = _dot(xv, wg_ref[j], NT)
        u = _dot(xv, wu_ref[j], NT)
        h = (g * _sigmoid(g) * u).astype(BF16)
        g_ref[0] = g.astype(BF16)
        u_ref[0] = u.astype(BF16)
        h_ref[0] = h
        y = _dot(h, wd_ref[j], NN)

        @pl.when(j == 0)
        def _():
            z_ref[...] = y

        @pl.when(j > 0)
        def _():
            z_ref[...] += y

        @pl.when(j == G - 1)
        def _():
            z = ALPHA * x_ref[...] + 0.5 * z_ref[...]
            xn = _layer_norm(z, gm_ref[...], bt_ref[...])
            z_ref[...] = z
            xn_ref[...] = xn
            xnb_ref[...] = xn.astype(BF16)

    row = pl.BlockSpec((ts, K), lambda s, j: (s, 0))
    vec = pl.BlockSpec((1, K), lambda s, j: (0, 0))
    wspec = _resident((G, N, K))
    ospec = pl.BlockSpec((1, ts, N), lambda s, j: (j, s, 0))
    return pl.pallas_call(
        body, name="ffn_fwd", grid=(S // ts, G),
        in_specs=[row, row, wspec, wspec, wspec, vec, vec],
        out_specs=[ospec, ospec, ospec, row, row, row],
        out_shape=[SDS((G, S, N), BF16), SDS((G, S, N), BF16), SDS((G, S, N), BF16),
                   SDS((S, K), F32), SDS((S, K), F32), SDS((S, K), BF16)],
        compiler_params=_params(("parallel", "arbitrary")),
    )(xb, x, wg, wu, wd, gamma, beta)


def proj(xb, w, name):
    S, K = xb.shape
    G, _, N = w.shape
    ts = _tile(S, 1024)
    gb = _group_block(G, 6)

    def body(x_ref, w_ref, y_ref):
        xv = x_ref[...]
        for j in range(gb):
            y_ref[j] = _dot(xv, w_ref[j], NN)

    return pl.pallas_call(
        body, name=name, grid=(S // ts, G // gb),
        in_specs=[pl.BlockSpec((ts, K), lambda s, g: (s, 0)), pl.BlockSpec((gb, K, N), lambda s, g: (g, 0, 0))],
        out_specs=pl.BlockSpec((gb, ts, N), lambda s, g: (g, s, 0)),
        out_shape=SDS((G, S, N), F32),
        compiler_params=_params(("parallel", "parallel")),
    )(xb, w)


def contract_ln(a, w, xres, gamma, beta, scale, name):
    G, S, Kg = a.shape
    N = w.shape[2]
    ts = _tile(S, 1024)

    def body(a_ref, w_ref, x_ref, g_ref, b_ref, z_ref, xn_ref, xb_ref):
        acc = _dot(a_ref[0], w_ref[0], NN)
        for j in range(1, G):
            acc = acc + _dot(a_ref[j], w_ref[j], NN)
        z = ALPHA * x_ref[...] + scale * acc
        xn = _layer_norm(z, g_ref[...], b_ref[...])
        z_ref[...] = z
        xn_ref[...] = xn
        xb_ref[...] = xn.astype(BF16)

    row = pl.BlockSpec((ts, N), lambda s: (s, 0))
    vec = pl.BlockSpec((1, N), lambda s: (0, 0))
    return pl.pallas_call(
        body, name=name, grid=(S // ts,),
        in_specs=[pl.BlockSpec((G, ts, Kg), lambda s: (0, s, 0)), pl.BlockSpec((G, Kg, N), lambda s: (0, 0, 0)), row, vec, vec],
        out_specs=[row, row, row],
        out_shape=[SDS((S, N), F32), SDS((S, N), F32), SDS((S, N), BF16)],
        compiler_params=_params(("parallel",)),
    )(a, w, xres, gamma, beta)


def _layer_norm_bwd(dx, z, gamma):
    mu = jnp.mean(z, axis=-1, keepdims=True)
    zc = z - mu
    var = jnp.mean(zc * zc, axis=-1, keepdims=True)
    rstd = lax.rsqrt(var + LN_EPS)
    xhat = zc * rstd
    dxh = dx * gamma
    m1 = jnp.mean(dxh, axis=-1, keepdims=True)
    m2 = jnp.mean(dxh * xhat, axis=-1, keepdims=True)
    return rstd * (dxh - m1 - xhat * m2), jnp.sum(dx * xhat, axis=0, keepdims=True), jnp.sum(dx, axis=0, keepdims=True)


def ffn_bwd(dxn, z, gamma, wd, wg, wu, g1, u1):
    S, K = dxn.shape
    G, N, _ = wd.shape
    ts = _tile(S, 512)

    def body(dxn_ref, z_ref, gm_ref, wd_ref, wg_ref, wu_ref, g_ref, u_ref, dg_ref, du_ref, dx_ref, dy_ref, dgm_ref, dbt_ref):
        s, j = pl.program_id(0), pl.program_id(1)

        @pl.when((s == 0) & (j == 0))
        def _():
            dgm_ref[...] = jnp.zeros_like(dgm_ref)
            dbt_ref[...] = jnp.zeros_like(dbt_ref)

        @pl.when(j == 0)
        def _():
            dz, dgm, dbt = _layer_norm_bwd(dxn_ref[...], z_ref[...], gm_ref[...])
            dgm_ref[...] += dgm
            dbt_ref[...] += dbt
            dx_ref[...] = ALPHA * dz
            dy_ref[...] = (0.5 * dz).astype(BF16)

        dy = dy_ref[...]
        part = None
        for a, b in _column_halves(N):
            dh = _dot(dy, wd_ref[j, a:b, :], NT)
            g = g_ref[0, :, a:b].astype(F32)
            sig = _sigmoid(g)
            dg = (dh * u_ref[0, :, a:b].astype(F32) * (sig * (1.0 + g * (1.0 - sig)))).astype(BF16)
            du = (dh * (g * sig)).astype(BF16)
            dg_ref[0, :, a:b] = dg
            du_ref[0, :, a:b] = du
            p = _dot(dg, wg_ref[j, a:b, :], NN) + _dot(du, wu_ref[j, a:b, :], NN)
            part = p if part is None else part + p
        dx_ref[...] += part

    row = pl.BlockSpec((ts, K), lambda s, j: (s, 0))
    vec = pl.BlockSpec((1, K), lambda s, j: (0, 0))
    gspec = pl.BlockSpec((1, ts, N), lambda s, j: (j, s, 0))
    wspec = _resident((G, N, K))
    return pl.pallas_call(
        body, name="ffn_bwd", grid=(S // ts, G),
        in_specs=[row, row, vec, wspec, wspec, wspec, gspec, gspec],
        out_specs=[gspec, gspec, row, row, vec, vec],
        out_shape=[SDS((G, S, N), BF16), SDS((G, S, N), BF16), SDS((S, K), F32), SDS((S, K), BF16),
                   SDS((1, K), F32), SDS((1, K), F32)],
        compiler_params=_params(("arbitrary", "arbitrary")),
    )(dxn, z, gamma, wd, wg, wu, g1, u1)


def mixer_out_bwd(dxn, z, gamma, w):
    S, N = dxn.shape
    G, Kg, _ = w.shape
    ts = _tile(S, 512)

    def body(dxn_ref, z_ref, gm_ref, w_ref, dm_ref, dz_ref, dzb_ref, dgm_ref, dbt_ref):
        @pl.when(pl.program_id(0) == 0)
        def _():
            dgm_ref[...] = jnp.zeros_like(dgm_ref)
            dbt_ref[...] = jnp.zeros_like(dbt_ref)

        dz, dgm, dbt = _layer_norm_bwd(dxn_ref[...], z_ref[...], gm_ref[...])
        dgm_ref[...] += dgm
        dbt_ref[...] += dbt
        dzb = dz.astype(BF16)
        dz_ref[...] = dz
        dzb_ref[...] = dzb
        for j in range(G):
            dm_ref[j] = _dot(dzb, w_ref[j], NT)

    row = pl.BlockSpec((ts, N), lambda s: (s, 0))
    vec = pl.BlockSpec((1, N), lambda s: (0, 0))
    return pl.pallas_call(
        body, name="mixer_out_bwd", grid=(S // ts,),
        in_specs=[row, row, vec, pl.BlockSpec((G, Kg, N), lambda s: (0, 0, 0))],
        out_specs=[pl.BlockSpec((G, ts, Kg), lambda s: (0, s, 0)), row, row, vec, vec],
        out_shape=[SDS((G, S, Kg), F32), SDS((S, N), F32), SDS((S, N), BF16), SDS((1, N), F32), SDS((1, N), F32)],
        compiler_params=_params(("arbitrary",)),
    )(dxn, z, gamma, w)


def contract_t(da, w, res, name):
    G, S, Ng = da.shape
    K = w.shape[1]
    ts = _tile(S, 512)
    gb = _group_block(G, 6)

    def body(da_ref, w_ref, r_ref, o_ref):
        g = pl.program_id(1)
        part = _dot(da_ref[0], w_ref[0], NT)
        for j in range(1, gb):
            part = part + _dot(da_ref[j], w_ref[j], NT)

        @pl.when(g == 0)
        def _():
            o_ref[...] = ALPHA * r_ref[...] + part

        @pl.when(g > 0)
        def _():
            o_ref[...] += part

    row = pl.BlockSpec((ts, K), lambda s, g: (s, 0))
    return pl.pallas_call(
        body, name=name, grid=(S // ts, G // gb),
        in_specs=[pl.BlockSpec((gb, ts, Ng), lambda s, g: (g, s, 0)), pl.BlockSpec((gb, K, Ng), lambda s, g: (g, 0, 0)), row],
        out_specs=row,
        out_shape=SDS((S, K), F32),
        compiler_params=_params(("parallel", "arbitrary")),
    )(da, w, res)


WGRAD_ACC_ELEMS = 6 * 1024 * 256


def wgrad(a, b, out_dtype, name):
    ga, gb = a.ndim == 3, b.ndim == 3
    G = a.shape[0] if ga else b.shape[0]
    S, K = a.shape[-2:]
    N = b.shape[-1]
    ts = _tile(S, 2048)
    ns = S // ts
    ng = _group_block(G, WGRAD_ACC_ELEMS // (K * N))

    def body(a_ref, b_ref, o_ref, acc):
        s = pl.program_id(1)

        @pl.when(s == 0)
        def _():
            acc[...] = jnp.zeros_like(acc)

        for j in range(ng):
            acc[j] += _dot(a_ref[j] if ga else a_ref[...], b_ref[j] if gb else b_ref[...], TN)

        @pl.when(s == ns - 1)
        def _():
            o_ref[...] = acc[...].astype(out_dtype)

    aspec = pl.BlockSpec((ng, ts, K), lambda g, s: (g, s, 0)) if ga else pl.BlockSpec((ts, K), lambda g, s: (s, 0))
    bspec = pl.BlockSpec((ng, ts, N), lambda g, s: (g, s, 0)) if gb else pl.BlockSpec((ts, N), lambda g, s: (s, 0))
    return pl.pallas_call(
        body, name=name, grid=(G // ng, ns),
        in_specs=[aspec, bspec],
        out_specs=pl.BlockSpec((ng, K, N), lambda g, s: (g, 0, 0)),
        out_shape=SDS((G, K, N), out_dtype),
        scratch_shapes=[pltpu.VMEM((ng, K, N), F32)],
        compiler_params=_params(("parallel", "arbitrary")),
    )(a, b)


def loss_grad(xn, tgt):
    S, N = xn.shape
    ts = _tile(S, 1024)

    def body(x_ref, t_ref, l_ref, dx_ref):
        @pl.when(pl.program_id(0) == 0)
        def _():
            l_ref[...] = jnp.zeros_like(l_ref)

        e = x_ref[...] - t_ref[...]
        dx_ref[...] = e * (1.0 / N)
        l_ref[...] += 0.5 * jnp.sum(jnp.mean(e * e, axis=-1, keepdims=True), axis=0, keepdims=True)

    row = pl.BlockSpec((ts, N), lambda s: (s, 0))
    return pl.pallas_call(
        body, name="loss_grad", grid=(S // ts,),
        in_specs=[row, row],
        out_specs=[pl.BlockSpec((1, 1), lambda s: (0, 0)), row],
        out_shape=[SDS((1, 1), F32), SDS((S, N), F32)],
        compiler_params=_params(("arbitrary",)),
    )(xn, tgt)


def _shift_down(x, k):
    if k == 0:
        return x
    rows = lax.broadcasted_iota(jnp.int32, x.shape, 0)
    return jnp.where(rows >= k, pltpu.roll(x, k, 0), 0.0)


def _shift_up(x, k):
    if k == 0:
        return x
    n = x.shape[0]
    rows = lax.broadcasted_iota(jnp.int32, x.shape, 0)
    return jnp.where(rows < n - k, pltpu.roll(x, n - k, 0), 0.0)


LANES = 128


def conv_mixer_fwd(u, cw):
    _, S, _ = u.shape
    nh = GROUP // LANES

    def body(b_ref, c_ref, x_ref, w_ref, o_ref):
        p = c_ref[0] * x_ref[0]
        w = w_ref[0]
        conv = w[2:3] * p + w[1:2] * _shift_down(p, 1) + w[0:1] * _shift_down(p, 2)
        o_ref[0] = (b_ref[0] * conv).astype(BF16)

    def uspec(off):
        return pl.BlockSpec((1, S, LANES), lambda g, h: (g + off, 0, h))

    return pl.pallas_call(
        body, name="conv_mixer_fwd", grid=(3, nh),
        in_specs=[uspec(0), uspec(3), uspec(6), pl.BlockSpec((1, 8, LANES), lambda g, h: (g, 0, h))],
        out_specs=pl.BlockSpec((1, S, LANES), lambda g, h: (g, 0, h)),
        out_shape=SDS((3, S, GROUP), BF16),
        compiler_params=_params(("parallel", "parallel")),
    )(u, u, u, cw)


def conv_mixer_bwd(u, cw, dm):
    _, S, _ = u.shape
    nh = GROUP // LANES

    def body(b_ref, c_ref, x_ref, w_ref, d_ref, db_ref, dc_ref, dx_ref, dw_ref):
        cg, xi = c_ref[0], x_ref[0]
        p = cg * xi
        p1, p2 = _shift_down(p, 1), _shift_down(p, 2)
        w = w_ref[0]
        conv = w[2:3] * p + w[1:2] * p1 + w[0:1] * p2
        dt = d_ref[0]
        db_ref[0] = (dt * conv).astype(BF16)
        dcv = dt * b_ref[0]
        dp = w[2:3] * dcv + w[1:2] * _shift_up(dcv, 1) + w[0:1] * _shift_up(dcv, 2)
        dc_ref[0] = (dp * xi).astype(BF16)
        dx_ref[0] = (dp * cg).astype(BF16)
        dw = jnp.concatenate([jnp.sum(dcv * p2, axis=0, keepdims=True), jnp.sum(dcv * p1, axis=0, keepdims=True),
                              jnp.sum(dcv * p, axis=0, keepdims=True), jnp.zeros((5, LANES), F32)], axis=0)
        dw_ref[0] = dw

    def uspec(off):
        return pl.BlockSpec((1, S, LANES), lambda g, h: (g + off, 0, h))

    ospec = pl.BlockSpec((1, S, LANES), lambda g, h: (g, 0, h))
    wspec = pl.BlockSpec((1, 8, LANES), lambda g, h: (g, 0, h))
    return pl.pallas_call(
        body, name="conv_mixer_bwd", grid=(3, nh),
        in_specs=[uspec(0), uspec(3), uspec(6), wspec, ospec],
        out_specs=[ospec, ospec, ospec, wspec],
        out_shape=[SDS((3, S, GROUP), BF16)] * 3 + [SDS((3, 8, GROUP), F32)],
        compiler_params=_params(("parallel", "parallel")),
    )(u, u, u, cw, dm)


def qk_conv_fwd(u, qw):
    _, S, _ = u.shape
    nh = GROUP // LANES

    def body(u_ref, w_ref, o_ref):
        x = u_ref[0]
        w = w_ref[0]
        pre = w[3:4] * x + w[2:3] * _shift_down(x, 1) + w[1:2] * _shift_down(x, 2) + w[0:1] * _shift_down(x, 3)
        o_ref[0] = pre * _sigmoid(pre)

    spec = pl.BlockSpec((1, S, LANES), lambda g, h: (g, 0, h))
    return pl.pallas_call(
        body, name="qk_conv_fwd", grid=(8, nh),
        in_specs=[spec, pl.BlockSpec((1, 8, LANES), lambda g, h: (g, 0, h))],
        out_specs=spec,
        out_shape=SDS((8, S, GROUP), F32),
        compiler_params=_params(("parallel", "parallel")),
    )(u, qw)


def qk_conv_bwd(u, qw, dqk):
    _, S, _ = u.shape
    nh = GROUP // LANES

    def body(u_ref, w_ref, d_ref, du_ref, dw_ref):
        x = u_ref[0]
        w = w_ref[0]
        x1, x2, x3 = _shift_down(x, 1), _shift_down(x, 2), _shift_down(x, 3)
        pre = w[3:4] * x + w[2:3] * x1 + w[1:2] * x2 + w[0:1] * x3
        sig = _sigmoid(pre)
        dpre = d_ref[0] * (sig * (1.0 + pre * (1.0 - sig)))
        du = w[3:4] * dpre + w[2:3] * _shift_up(dpre, 1) + w[1:2] * _shift_up(dpre, 2) + w[0:1] * _shift_up(dpre, 3)
        du_ref[0] = du.astype(BF16)
        dw = jnp.concatenate([jnp.sum(dpre * x3, axis=0, keepdims=True), jnp.sum(dpre * x2, axis=0, keepdims=True),
                              jnp.sum(dpre * x1, axis=0, keepdims=True), jnp.sum(dpre * x, axis=0, keepdims=True),
                              jnp.zeros((4, LANES), F32)], axis=0)
        dw_ref[0] = dw

    spec = pl.BlockSpec((1, S, LANES), lambda g, h: (g, 0, h))
    wspec = pl.BlockSpec((1, 8, LANES), lambda g, h: (g, 0, h))
    return pl.pallas_call(
        body, name="qk_conv_bwd", grid=(8, nh),
        in_specs=[spec, wspec, spec],
        out_specs=[spec, wspec],
        out_shape=[SDS((8, S, GROUP), BF16), SDS((8, 8, GROUP), F32)],
        compiler_params=_params(("parallel", "parallel")),
    )(u, qw, dqk)


def _head_masks():
    lane = lax.broadcasted_iota(jnp.int32, (1, D_XA), 1)
    return [(lane >= h * XA_HEAD_DIM) & (lane < (h + 1) * XA_HEAD_DIM) for h in range(XA_HEADS)]


def xattn_fwd(u, qg, kv):
    _, S, _ = u.shape
    ts = _tile(S, 1024)
    scale = XA_HEAD_DIM ** -0.5

    def body(q_ref, kv_ref, o_ref):
        q = q_ref[0]
        k = kv_ref[0].astype(BF16)
        v = kv_ref[1]
        o = jnp.zeros((ts, D_XA), F32)
        for m in _head_masks():
            s = _dot(jnp.where(m, q, 0.0).astype(BF16), k, NT) * scale
            s = s - jnp.max(s, axis=-1, keepdims=True)
            e = jnp.exp(s)
            p = e / jnp.sum(e, axis=-1, keepdims=True)
            o = o + _dot(p.astype(BF16), jnp.where(m, v, 0.0).astype(BF16), NN)
        o_ref[0] = o.astype(BF16)

    return pl.pallas_call(
        body, name="xattn_fwd", grid=(S // ts,),
        in_specs=[pl.BlockSpec((1, ts, GROUP), lambda s: (qg, s, 0)), pl.BlockSpec((2, N_MEM, GROUP), lambda s: (0, 0, 0))],
        out_specs=pl.BlockSpec((1, ts, GROUP), lambda s: (0, s, 0)),
        out_shape=SDS((1, S, GROUP), BF16),
        compiler_params=_params(("parallel",)),
    )(u, kv)


def xattn_bwd(u, qg, kv, dm, dg):
    _, S, _ = u.shape
    ts = _tile(S, 1024)
    scale = XA_HEAD_DIM ** -0.5

    def body(q_ref, kv_ref, do_ref, dq_ref, dkv_ref):
        @pl.when(pl.program_id(0) == 0)
        def _():
            dkv_ref[...] = jnp.zeros_like(dkv_ref)

        q = q_ref[0]
        k = kv_ref[0]
        v = kv_ref[1]
        kb = k.astype(BF16)
        do = do_ref[0]
        dq = jnp.zeros((ts, D_XA), F32)
        dk = jnp.zeros((N_MEM, D_XA), F32)
        dv = jnp.zeros((N_MEM, D_XA), F32)
        for m in _head_masks():
            qm = jnp.where(m, q, 0.0).astype(BF16)
            s = _dot(qm, kb, NT) * scale
            s = s - jnp.max(s, axis=-1, keepdims=True)
            e = jnp.exp(s)
            p = e / jnp.sum(e, axis=-1, keepdims=True)
            dom = jnp.where(m, do, 0.0).astype(BF16)
            dp = _dot(dom, jnp.where(m, v, 0.0).astype(BF16), NT)
            ds = (p * (dp - jnp.sum(dp * p, axis=-1, keepdims=True)) * scale).astype(BF16)
            dq = dq + _dot(ds, jnp.where(m, k, 0.0).astype(BF16), NN)
            dk = dk + _dot(ds, qm, TN)
            dv = dv + _dot(p.astype(BF16), dom, TN)
        dq_ref[0] = dq.astype(BF16)
        dkv_ref[0] += dk
        dkv_ref[1] += dv

    return pl.pallas_call(
        body, name="xattn_bwd", grid=(S // ts,),
        in_specs=[pl.BlockSpec((1, ts, GROUP), lambda s: (qg, s, 0)), pl.BlockSpec((2, N_MEM, GROUP), lambda s: (0, 0, 0)),
                  pl.BlockSpec((1, ts, GROUP), lambda s: (dg, s, 0))],
        out_specs=[pl.BlockSpec((1, ts, GROUP), lambda s: (0, s, 0)), pl.BlockSpec((2, N_MEM, GROUP), lambda s: (0, 0, 0))],
        out_shape=[SDS((1, S, GROUP), BF16), SDS((2, N_MEM, GROUP), F32)],
        compiler_params=_params(("arbitrary",)),
    )(u, kv, dm)


ML_BLOCK_CHUNKS = 4
H4 = ML_HEADS
L = ML_CHUNK
NLANE = ML_HEAD_DIM


def _chunk_consts():
    r = lax.broadcasted_iota(jnp.int32, (1, L, L), 1)
    c = lax.broadcasted_iota(jnp.int32, (1, L, L), 2)
    return r >= c, r <= c, r == c


def _gate_cols(gb):
    lane = lax.broadcasted_iota(jnp.int32, gb.shape, 1)
    li = jnp.stack([jnp.sum(jnp.where(lane == h, gb, 0.0), axis=1, keepdims=True) for h in range(H4)])
    gf = jnp.stack([jnp.sum(jnp.where(lane == H4 + h, gb, 0.0), axis=1, keepdims=True) for h in range(H4)])
    return li, gf


def _log_sigmoid(x):
    return jnp.minimum(x, 0.0) - jnp.log(1.0 + jnp.exp(-jnp.abs(x)))


def _chunk_forward(q, k, v_aug, li_col, lf_col, c_prev, m_prev):
    tri, tri_t, eye = _chunk_consts()
    lf_row = jnp.sum(jnp.where(eye, lf_col, 0.0), axis=1, keepdims=True)
    li_row = jnp.sum(jnp.where(eye, li_col, 0.0), axis=1, keepdims=True)
    bcum_col = jnp.sum(jnp.where(tri, lf_row, 0.0), axis=2, keepdims=True)
    bcum_row = jnp.sum(jnp.where(tri_t, lf_col, 0.0), axis=1, keepdims=True)
    log_d = jnp.where(tri, bcum_col - bcum_row + li_row, NEG)
    log_inter = bcum_col + m_prev
    m_t = jnp.maximum(log_inter, jnp.max(log_d, axis=2, keepdims=True))
    w_intra = jnp.exp(log_d - m_t)
    w_inter = jnp.exp(log_inter - m_t)
    sc = _bdot(q, k, 2, 2) * w_intra
    qc = _bdot1(q, c_prev, 2, 1)
    num = _bdot(sc, v_aug, 2, 1) + w_inter * qc
    lane = lax.broadcasted_iota(jnp.int32, num.shape, 2)
    den = jnp.sum(jnp.where(lane == NLANE, num, 0.0), axis=2, keepdims=True)
    e_m = jnp.exp(-m_t)
    b_last = jnp.sum(lf_row, axis=2, keepdims=True)
    log_w = b_last - bcum_col + li_col
    m_new = jnp.maximum(b_last + m_prev, jnp.max(log_w, axis=1, keepdims=True))
    w_k = jnp.exp(log_w - m_new)
    decay = jnp.exp(b_last + m_prev - m_new)
    return dict(w_intra=w_intra, w_inter=w_inter, sc=sc, qc=qc, num=num, den=den, e_m=e_m, lane=lane,
                w_k=w_k, decay=decay, m_new=m_new)


def mlstm_fwd(qk, u, bg):
    _, S, _ = qk.shape
    nc = S // L
    cb = min(ML_BLOCK_CHUNKS, nc)
    rows = cb * L
    kscale = ML_HEAD_DIM ** -0.5

    def body(qk_ref, v_ref, g_ref, bg_ref, h_ref, cst_ref, mst_ref, c_sc, m_sc):
        @pl.when(pl.program_id(0) == 0)
        def _():
            c_sc[...] = jnp.zeros_like(c_sc)
            m_sc[...] = jnp.zeros_like(m_sc)

        for c in range(cb):
            sl = pl.ds(c * L, L)
            q = qk_ref[0:H4, sl, :]
            k = qk_ref[H4:2 * H4, sl, :] * kscale
            v = v_ref[:, sl, :]
            lane = lax.broadcasted_iota(jnp.int32, v.shape, 2)
            v_aug = jnp.where(lane == NLANE, 1.0, v)
            li_col, gf = _gate_cols(g_ref[0, sl, :] + bg_ref[...])
            lf_col = _log_sigmoid(gf)
            c_prev = c_sc[...]
            m_prev = m_sc[...]
            f = _chunk_forward(q, k, v_aug, li_col, lf_col, c_prev, m_prev)
            r = 1.0 / jnp.maximum(jnp.abs(f["den"]), f["e_m"])
            h_ref[:, sl, :] = jnp.where(lane < NLANE, f["num"] * r, 0.0)
            cst_ref[c] = c_prev
            mst_ref[c] = jnp.broadcast_to(m_prev, (H4, 1, LANES))
            c_sc[...] = f["decay"] * c_prev + _bdot(k * f["w_k"], v_aug, 1, 1)
            m_sc[...] = f["m_new"]

    def hspec(blk):
        return pl.BlockSpec((H4, rows, GROUP), lambda i: (blk, i, 0))

    return pl.pallas_call(
        body, name="mlstm_fwd", grid=(nc // cb,),
        in_specs=[pl.BlockSpec((2 * H4, rows, GROUP), lambda i: (0, i, 0)), hspec(2),
                  pl.BlockSpec((1, rows, GROUP), lambda i: (17, i, 0)), pl.BlockSpec((1, GROUP), lambda i: (0, 0))],
        out_specs=[hspec(0), pl.BlockSpec((cb, H4, GROUP, GROUP), lambda i: (i, 0, 0, 0)),
                   pl.BlockSpec((cb, H4, 1, LANES), lambda i: (i, 0, 0, 0))],
        out_shape=[SDS((H4, S, GROUP), F32), SDS((nc, H4, GROUP, GROUP), F32), SDS((nc, H4, 1, LANES), F32)],
        scratch_shapes=[pltpu.VMEM((H4, GROUP, GROUP), F32), pltpu.VMEM((H4, 1, 1), F32)],
        compiler_params=_params(("arbitrary",)),
    )(qk, u, u, bg)


def mlstm_bwd(qk, u, bg, cst, mst, dh):
    _, S, _ = qk.shape
    nc = S // L
    cb = min(ML_BLOCK_CHUNKS, nc)
    rows = cb * L
    nb = nc // cb
    kscale = ML_HEAD_DIM ** -0.5

    def body(qk_ref, v_ref, g_ref, bg_ref, cst_ref, mst_ref, dh_ref, dqk_ref, dv_ref, dg_ref, dbg_ref, dc_sc):
        @pl.when(pl.program_id(0) == 0)
        def _():
            dc_sc[...] = jnp.zeros_like(dc_sc)
            dbg_ref[...] = jnp.zeros_like(dbg_ref)

        tri, tri_t, eye = _chunk_consts()
        for c in reversed(range(cb)):
            sl = pl.ds(c * L, L)
            q = qk_ref[0:H4, sl, :]
            k = qk_ref[H4:2 * H4, sl, :] * kscale
            v = v_ref[:, sl, :]
            lane = lax.broadcasted_iota(jnp.int32, v.shape, 2)
            v_aug = jnp.where(lane == NLANE, 1.0, v)
            li_col, gf = _gate_cols(g_ref[0, sl, :] + bg_ref[...])
            lf_col = _log_sigmoid(gf)
            c_prev = cst_ref[c]
            m_prev = mst_ref[c][:, :, 0:1]
            f = _chunk_forward(q, k, v_aug, li_col, lf_col, c_prev, m_prev)
            w_intra, w_inter, sc, num, den, e_m = f["w_intra"], f["w_inter"], f["sc"], f["num"], f["den"], f["e_m"]
            absd = jnp.abs(den)
            r = 1.0 / jnp.maximum(absd, e_m)
            dhv = dh_ref[:, sl, :]
            s1 = jnp.sum(jnp.where(lane < NLANE, dhv * num, 0.0), axis=2, keepdims=True)
            dden = jnp.where(absd > e_m, -s1 * r * r * jnp.sign(den), 0.0)
            dnum = jnp.where(lane == NLANE, dden, jnp.where(lane < NLANE, dhv * r, 0.0))
            dsc = _bdot1(dnum, v_aug, 2, 2)
            dv = _bdot1(sc, dnum, 1, 1)
            gmat = dsc * sc
            dqk = dsc * w_intra
            dq = _bdot1(dqk, k, 2, 1) + w_inter * _bdot1(dnum, c_prev, 2, 2)
            dk = _bdot1(dqk, q, 1, 1)
            dc_prev = _bdot(q * w_inter, dnum, 1, 1)
            dlog_inter = jnp.sum(dnum * f["qc"], axis=2, keepdims=True) * w_inter
            dbcum_col = dlog_inter + jnp.sum(gmat, axis=2, keepdims=True)
            g_row = jnp.sum(gmat, axis=1, keepdims=True)
            dcn = dc_sc[...]
            w_k, decay = f["w_k"], f["decay"]
            kw = k * w_k
            dc_prev = dc_prev + decay * dcn
            db_last = jnp.sum(jnp.sum(dcn * c_prev, axis=2, keepdims=True), axis=1, keepdims=True) * decay
            dkw = _bdot(v_aug, dcn, 2, 2)
            dv = dv + _bdot1(kw, dcn, 2, 1)
            dk = dk + dkw * w_k
            dlogw = jnp.sum(dkw * k, axis=2, keepdims=True) * w_k
            db_last = db_last + jnp.sum(dlogw, axis=1, keepdims=True)
            dbcum_col = dbcum_col - dlogw
            rowi = lax.broadcasted_iota(jnp.int32, (1, L, 1), 1)
            dbcum_col = dbcum_col + jnp.where(rowi == L - 1, db_last, 0.0)
            dbcum_row = jnp.sum(jnp.where(eye, dbcum_col, 0.0), axis=1, keepdims=True) - g_row
            dlf_col = jnp.sum(jnp.where(tri_t, dbcum_row, 0.0), axis=2, keepdims=True)
            dli_col = dlogw + jnp.sum(jnp.where(eye, g_row, 0.0), axis=2, keepdims=True)
            dgf_col = dlf_col * _sigmoid(-gf)
            lane_g = lax.broadcasted_iota(jnp.int32, (L, GROUP), 1)
            dg = jnp.zeros((L, GROUP), F32)
            for h in range(H4):
                dg = dg + jnp.where(lane_g == h, dli_col[h], 0.0) + jnp.where(lane_g == H4 + h, dgf_col[h], 0.0)
            dqk_ref[0:H4, sl, :] = dq
            dqk_ref[H4:2 * H4, sl, :] = dk * kscale
            dv_ref[:, sl, :] = jnp.where(lane < NLANE, dv, 0.0).astype(BF16)
            dg_ref[0, sl, :] = dg.astype(BF16)
            dbg_ref[...] += jnp.sum(dg, axis=0, keepdims=True)
            dc_sc[...] = dc_prev

    def hspec(blk):
        return pl.BlockSpec((H4, rows, GROUP), lambda i: (blk, nb - 1 - i, 0))

    gspec = pl.BlockSpec((1, rows, GROUP), lambda i: (17, nb - 1 - i, 0))
    qkspec = pl.BlockSpec((2 * H4, rows, GROUP), lambda i: (0, nb - 1 - i, 0))
    return pl.pallas_call(
        body, name="mlstm_bwd", grid=(nb,),
        in_specs=[qkspec, hspec(2), gspec, pl.BlockSpec((1, GROUP), lambda i: (0, 0)),
                  pl.BlockSpec((cb, H4, GROUP, GROUP), lambda i: (nb - 1 - i, 0, 0, 0)),
                  pl.BlockSpec((cb, H4, 1, LANES), lambda i: (nb - 1 - i, 0, 0, 0)), hspec(0)],
        out_specs=[qkspec, hspec(0), pl.BlockSpec((1, rows, GROUP), lambda i: (0, nb - 1 - i, 0)),
                   pl.BlockSpec((1, GROUP), lambda i: (0, 0))],
        out_shape=[SDS((2 * H4, S, GROUP), F32), SDS((H4, S, GROUP), BF16),
                   SDS((1, S, GROUP), BF16), SDS((1, GROUP), F32)],
        scratch_shapes=[pltpu.VMEM((H4, GROUP, GROUP), F32)],
        compiler_params=_params(("arbitrary",)),
    )(qk, u, u, bg, cst, mst, dh)


def head_norm_fwd(hm, u, hg):
    _, S, _ = hm.shape
    ts = _tile(S, 2048)

    def body(h_ref, o_ref, g_ref, t_ref):
        h = h_ref[0]
        lane = lax.broadcasted_iota(jnp.int32, h.shape, 1)
        valid = lane < ML_HEAD_DIM
        mu = jnp.sum(h, axis=-1, keepdims=True) * (1.0 / ML_HEAD_DIM)
        hc = jnp.where(valid, h - mu, 0.0)
        var = jnp.sum(hc * hc, axis=-1, keepdims=True) * (1.0 / ML_HEAD_DIM)
        hn = hc * lax.rsqrt(var + LN_EPS) * g_ref[0]
        t_ref[0] = (_sigmoid(o_ref[0]) * hn).astype(BF16)

    return pl.pallas_call(
        body, name="head_norm_fwd", grid=(H4, S // ts),
        in_specs=[pl.BlockSpec((1, ts, GROUP), lambda h, s: (h, s, 0)), pl.BlockSpec((1, ts, GROUP), lambda h, s: (12 + h, s, 0)),
                  pl.BlockSpec((1, 1, GROUP), lambda h, s: (h, 0, 0))],
        out_specs=pl.BlockSpec((1, ts, GROUP), lambda h, s: (h, s, 0)),
        out_shape=SDS((H4, S, GROUP), BF16),
        compiler_params=_params(("parallel", "parallel")),
    )(hm, u, hg)


def head_norm_bwd(hm, u, hg, dm):
    _, S, _ = hm.shape
    ts = _tile(S, 2048)

    def body(h_ref, o_ref, g_ref, d_ref, dh_ref, do_ref, dg_ref):
        @pl.when(pl.program_id(1) == 0)
        def _():
            dg_ref[...] = jnp.zeros_like(dg_ref)

        h = h_ref[0]
        lane = lax.broadcasted_iota(jnp.int32, h.shape, 1)
        valid = lane < ML_HEAD_DIM
        inv = 1.0 / ML_HEAD_DIM
        mu = jnp.sum(h, axis=-1, keepdims=True) * inv
        hc = jnp.where(valid, h - mu, 0.0)
        var = jnp.sum(hc * hc, axis=-1, keepdims=True) * inv
        rstd = lax.rsqrt(var + LN_EPS)
        xhat = hc * rstd
        g = g_ref[0]
        sig = _sigmoid(o_ref[0])
        dt = jnp.where(valid, d_ref[0], 0.0)
        do_ref[0] = (dt * xhat * g * sig * (1.0 - sig)).astype(BF16)
        dhn = dt * sig
        dg_ref[0] += jnp.sum(dhn * xhat, axis=0, keepdims=True)
        dxh = dhn * g
        m1 = jnp.sum(dxh, axis=-1, keepdims=True) * inv
        m2 = jnp.sum(dxh * xhat, axis=-1, keepdims=True) * inv
        dh_ref[0] = jnp.where(valid, rstd * (dxh - m1 - xhat * m2), 0.0)

    spec = pl.BlockSpec((1, ts, GROUP), lambda h, s: (h, s, 0))
    gspec = pl.BlockSpec((1, 1, GROUP), lambda h, s: (h, 0, 0))
    return pl.pallas_call(
        body, name="head_norm_bwd", grid=(H4, S // ts),
        in_specs=[spec, pl.BlockSpec((1, ts, GROUP), lambda h, s: (12 + h, s, 0)), gspec, spec],
        out_specs=[spec, spec, gspec],
        out_shape=[SDS((H4, S, GROUP), F32), SDS((H4, S, GROUP), BF16), SDS((H4, 1, GROUP), F32)],
        compiler_params=_params(("parallel", "arbitrary")),
    )(hm, u, hg, dm)


def _adamw_math(w, g, m, v):
    c1 = 1.0 / (1.0 - ADAM_B1 ** ADAM_STEP)
    c2 = 1.0 / (1.0 - ADAM_B2 ** ADAM_STEP)
    nm = ADAM_B1 * m + (1.0 - ADAM_B1) * g
    nv = ADAM_B2 * v + (1.0 - ADAM_B2) * (g * g)
    return -ADAM_LR * ((nm * c1) / (jnp.sqrt(nv * c2) + ADAM_EPS) + ADAM_WD * w), nm, nv


def _row_tile(R, cap=512):
    return R if R <= cap else max(d for d in range(8, cap + 1, 8) if R % d == 0)


def adamw_into(w, m, v, g, outs, idx, after, name):
    R, C = g.shape
    tr = _row_tile(R)
    lead = (0,) * len(idx)

    def body(w_ref, m_ref, v_ref, g_ref, *rest):
        go_ref, d_ref, nm_ref, nv_ref, token = rest[-5:]
        token[...] = jnp.zeros_like(token)
        gv = g_ref[...]
        d, nm, nv = _adamw_math(w_ref[lead], gv, m_ref[lead], v_ref[lead])
        go_ref[lead] = gv
        d_ref[lead] = d
        nm_ref[lead] = nm
        nv_ref[lead] = nv

    blk = pl.BlockSpec((1,) * len(idx) + (tr, C), lambda r: idx + (r, 0))
    any_space = pl.BlockSpec(memory_space=pl.ANY)
    in_specs, args, aliases = [blk, blk, blk, pl.BlockSpec((tr, C), lambda r: (r, 0)), any_space], [w, m, v, g, g if after is None else after], {}
    if outs is not None:
        in_specs += [any_space] * 4
        args += list(outs)
        aliases = {5 + i: i for i in range(4)}
    out = pl.pallas_call(
        body, name=name, grid=(R // tr,),
        in_specs=in_specs, out_specs=[blk] * 4 + [pl.BlockSpec((8, LANES), lambda r: (0, 0))],
        out_shape=[SDS(w.shape, F32)] * 4 + [SDS((8, LANES), F32)],
        input_output_aliases=aliases, compiler_params=_params(("arbitrary",)),
    )(*args)
    return out[:4], out[4]


def adamw(w, g, m, v, name):
    R, C = w.shape
    tr = _row_tile(R)

    def body(w_ref, g_ref, m_ref, v_ref, d_ref, nm_ref, nv_ref):
        d_ref[...], nm_ref[...], nv_ref[...] = _adamw_math(w_ref[...], g_ref[...], m_ref[...], v_ref[...])

    spec = pl.BlockSpec((tr, C), lambda i: (i, 0))
    return pl.pallas_call(
        body, name=name, grid=(R // tr,),
        in_specs=[spec] * 4, out_specs=[spec] * 3,
        out_shape=[SDS((R, C), F32)] * 3,
        compiler_params=_params(("parallel",)),
    )(w, g, m, v)


HBM = pl.BlockSpec(memory_space=pl.ANY)
ROW_SPLIT = 4
PAIR_SPLIT = 1


def _position():
    x, y, c = lax.axis_index("x"), lax.axis_index("y"), lax.axis_index("c")
    return x, y, c, [(1 - x, y), (x, 1 - y), (1 - x, 1 - y)]


def _unique(items):
    arrays = []
    for a, _ in items:
        if not any(a is b for b in arrays):
            arrays.append(a)
    return arrays, [next(i for i, b in enumerate(arrays) if b is a) for a, _ in items]


def place_own(items, me, after, name):
    arrays, src_of = _unique(items)
    n = len(items)
    shapes = [a.shape[len(p):] for a, p in items]

    def body(me_ref, *refs):
        for t in range(n):
            refs[n + 1 + t][0] = refs[t][(0,) * len(items[t][1])]

    in_specs, out_specs = [], []
    for (a, p), shp in zip(items, shapes):
        blk = shp[:-2] + (shp[-2] // ROW_SPLIT, shp[-1])
        lead = (0,) * (len(shp) - 2)
        in_specs.append(pl.BlockSpec((1,) * len(p) + blk, functools.partial(lambda r, me_ref, p, lead: p + lead + (r, 0), p=p, lead=lead)))
        out_specs.append(pl.BlockSpec((1,) + blk, functools.partial(lambda r, me_ref, lead: (me_ref[0],) + lead + (r, 0), lead=lead)))
    in_specs.append(pl.BlockSpec(memory_space=pl.ANY))
    return pl.pallas_call(
        body, name=name,
        grid_spec=pltpu.PrefetchScalarGridSpec(num_scalar_prefetch=1, grid=(ROW_SPLIT,), in_specs=in_specs, out_specs=out_specs),
        out_shape=[SDS((N_CHIPS,) + tuple(shp), a.dtype) for shp, (a, _) in zip(shapes, items)],
        compiler_params=_params(("parallel",)),
    )(me, *[arrays[i] for i in src_of], after)


SEM = pl.BlockSpec(memory_space=pltpu.SEMAPHORE)
IN_HBM = pl.BlockSpec(memory_space=pltpu.HBM)
DATAFLOW = pltpu.SideEffectType.DATAFLOW_SIDE_EFFECTING


def split_start(bufs, plan, n_copies, after, name):
    n = len(bufs)

    def body(*refs):
        send, recv, token = refs[n + 1], refs[n + 2], refs[-1]
        x, y, c, chips = _position()
        for k, (src, dst, dev) in enumerate(plan(refs[:n], x, y, c, chips)):
            pltpu.make_async_remote_copy(src_ref=src, dst_ref=dst, send_sem=send.at[k], recv_sem=recv.at[k],
                                         device_id=dev, device_id_type=MESH).start()
        token[...] = jnp.zeros_like(token)

    out = pl.pallas_call(
        body, name=name,
        out_shape=(pltpu.SemaphoreType.DMA((n_copies,)), pltpu.SemaphoreType.DMA((n_copies,)),
                   *[pltpu.HBM(b.shape, b.dtype) for b in bufs], SDS((8, LANES), F32)),
        in_specs=[IN_HBM] * n + [pl.BlockSpec(memory_space=pl.ANY)],
        out_specs=(SEM, SEM, *[IN_HBM] * n, pl.BlockSpec(memory_space=pltpu.VMEM)),
        input_output_aliases={i: 2 + i for i in range(n)},
        compiler_params=pltpu.CompilerParams(has_side_effects=DATAFLOW),
    )(*[pltpu.with_memory_space_constraint(b, pltpu.HBM) for b in bufs], after)
    return out[0], out[1], list(out[2:2 + n]), out[-1]


def split_wait(send, recv, bufs, plan, after, name):
    n = len(bufs)

    def body(*refs):
        send_ref, recv_ref = refs[n], refs[n + 1]
        x, y, c, chips = _position()
        for k, (src, dst, dev) in enumerate(plan(refs[:n], x, y, c, chips)):
            cp = pltpu.make_async_remote_copy(src_ref=src, dst_ref=dst, send_sem=send_ref.at[k], recv_sem=recv_ref.at[k],
                                              device_id=dev, device_id_type=MESH)
            cp.wait_send()
            cp.wait_recv()

    return list(pl.pallas_call(
        body, name=name, out_shape=tuple(pltpu.HBM(b.shape, b.dtype) for b in bufs),
        in_specs=[IN_HBM] * n + [SEM, SEM, pl.BlockSpec(memory_space=pl.ANY)], out_specs=tuple([IN_HBM] * n),
        input_output_aliases={i: i for i in range(n)},
        compiler_params=pltpu.CompilerParams(has_side_effects=DATAFLOW),
    )(*bufs, send, recv, after))


def _gather_plan(shapes, landing):
    n = len(shapes)

    def plan(refs, x, y, c, chips):
        out = []
        for t in range(n):
            half = shapes[t][0] // 2
            rows = pl.ds(c * half, half)
            for cx, cy in chips:
                slot = 2 * cx + cy if landing else 2 * x + y
                out.append((refs[t].at[rows], refs[n + t].at[slot, rows], (cx, cy, c)))
        return out

    return plan


def gather_start(shards, placed, after, name):
    shapes = [s.shape for s in shards]
    send, recv, bufs, token = split_start(list(shards) + list(placed), _gather_plan(shapes, False), 3 * len(shards), after, name)
    return (send, recv, bufs, shapes), token


def gather_wait(state, after, name):
    send, recv, bufs, shapes = state
    return split_wait(send, recv, bufs, _gather_plan(shapes, True), after, name)[len(shapes):]


def gather_pass_on(placed, shapes, name):
    n = len(placed)

    def body(*refs):
        outs, send, recv = refs[n:2 * n], refs[2 * n], refs[2 * n + 1]
        x, y, c, chips = _position()
        cps = []
        for t in range(n):
            half = shapes[t][0] // 2
            for j, (cx, cy) in enumerate(chips):
                piece = outs[t].at[2 * cx + cy, pl.ds(c * half, half)]
                cp = pltpu.make_async_remote_copy(src_ref=piece, dst_ref=piece, send_sem=send.at[3 * t + j], recv_sem=recv.at[3 * t + j],
                                                  device_id=(x, y, 1 - c), device_id_type=MESH)
                cp.start()
                cps.append(cp)
        for t in range(n):
            half = shapes[t][0] // 2
            for j, (cx, cy) in enumerate(chips):
                piece = outs[t].at[2 * cx + cy, pl.ds((1 - c) * half, half)]
                pltpu.make_async_remote_copy(src_ref=piece, dst_ref=piece, send_sem=send.at[3 * t + j], recv_sem=recv.at[3 * t + j],
                                             device_id=(x, y, 1 - c), device_id_type=MESH).wait_recv()
        for cp in cps:
            cp.wait_send()

    return pl.pallas_call(
        body, name=name,
        in_specs=[HBM] * n, out_specs=[HBM] * n,
        out_shape=[SDS(p.shape, p.dtype) for p in placed],
        input_output_aliases={t: t for t in range(n)},
        scratch_shapes=[pltpu.SemaphoreType.DMA((3 * n,))] * 2,
    )(*placed)


def _flip(k, x, y, c):
    return ((1 - x) if k & 4 else x, (1 - y) if k & 2 else y, (1 - c) if k & 1 else c)


def small_allgather(v, reduce):
    R, C = v.shape

    def body(v_ref, o_ref, *scratch):
        if reduce:
            buf, send, recv = scratch
        else:
            buf, (send, recv) = o_ref, scratch
        x, y, c, _ = _position()
        me = 4 * x + 2 * y + c
        buf[me] = v_ref[...]
        sends = []
        for k in range(1, N_DEV):
            cp = pltpu.make_async_remote_copy(src_ref=v_ref, dst_ref=buf.at[me], send_sem=send.at[k - 1], recv_sem=recv.at[k - 1],
                                              device_id=_flip(k, x, y, c), device_id_type=MESH)
            cp.start()
            sends.append(cp)
        for k in range(1, N_DEV):
            px, py, pc = _flip(k, x, y, c)
            pltpu.make_async_remote_copy(src_ref=v_ref, dst_ref=buf.at[4 * px + 2 * py + pc], send_sem=send.at[k - 1],
                                         recv_sem=recv.at[k - 1], device_id=(px, py, pc), device_id_type=MESH).wait_recv()
        for cp in sends:
            cp.wait_send()
        if reduce:
            acc = buf[0]
            for i in range(1, N_DEV):
                acc = acc + buf[i]
            o_ref[...] = acc

    vm = pl.BlockSpec(memory_space=pltpu.VMEM)
    sems = [pltpu.SemaphoreType.DMA((N_DEV - 1,)), pltpu.SemaphoreType.DMA((N_DEV - 1,))]
    return pl.pallas_call(
        body, name="small_allreduce" if reduce else "small_allgather",
        in_specs=[vm], out_specs=vm,
        out_shape=SDS((R, C) if reduce else (N_DEV, R, C), F32),
        scratch_shapes=([pltpu.VMEM((N_DEV, R, C), F32)] if reduce else []) + sems,
    )(v)


def rs_exchange_sibling(gs):
    n = len(gs)

    def body(*refs):
        ins, outs, send, recv = refs[:n], refs[n:2 * n], refs[2 * n], refs[2 * n + 1]
        x, y, c, _ = _position()
        cps = []
        for t in range(n):
            cp = pltpu.make_async_remote_copy(src_ref=ins[t].at[:, 1 - c], dst_ref=outs[t], send_sem=send.at[t], recv_sem=recv.at[t],
                                              device_id=(x, y, 1 - c), device_id_type=MESH)
            cp.start()
            cps.append(cp)
        for cp in cps:
            cp.wait()

    return pl.pallas_call(
        body, name="rs_exchange_sibling", in_specs=[HBM] * n, out_specs=[HBM] * n,
        out_shape=[SDS((g.shape[0],) + g.shape[2:], g.dtype) for g in gs],
        scratch_shapes=[pltpu.SemaphoreType.DMA((n,)), pltpu.SemaphoreType.DMA((n,))],
    )(*gs)


def rs_pair_add(gs, rs, c):
    n = len(gs)

    def body(c_ref, *refs):
        for t in range(n):
            refs[2 * n + t][0] = (refs[t][0, 0].astype(F32) + refs[n + t][0].astype(F32)).astype(BF16)

    in_specs, out_specs, out_shape = [], [], []
    for g in gs:
        _, _, h, C = g.shape
        in_specs.append(pl.BlockSpec((1, 1, h // PAIR_SPLIT, C), lambda j, r, c_ref: (j, c_ref[0], r, 0)))
    for g in gs:
        _, _, h, C = g.shape
        spec = pl.BlockSpec((1, h // PAIR_SPLIT, C), lambda j, r, c_ref: (j, r, 0))
        in_specs.append(spec)
        out_specs.append(spec)
        out_shape.append(SDS((N_CHIPS, h, C), BF16))
    return pl.pallas_call(
        body, name="rs_pair_add",
        grid_spec=pltpu.PrefetchScalarGridSpec(num_scalar_prefetch=1, grid=(N_CHIPS, PAIR_SPLIT), in_specs=in_specs, out_specs=out_specs),
        out_shape=out_shape, compiler_params=_params(("parallel", "parallel")),
    )(c, *gs, *rs)


def _rs_plan(n):
    def plan(refs, x, y, c, chips):
        return [(refs[t].at[2 * cx + cy], refs[n + t].at[j], (cx, cy, c)) for t in range(n) for j, (cx, cy) in enumerate(chips)]

    return plan


def rs_chip_add(ps, qs, me_c):
    n = len(ps)

    def body(me_ref, *refs):
        for t in range(n):
            q = refs[n + t]
            refs[2 * n + t][0] = ((refs[t][0].astype(F32) + q[0].astype(F32)) + q[1].astype(F32)) + q[2].astype(F32)

    in_specs, out_specs, out_shape = [], [], []
    for p in ps:
        _, h, C = p.shape
        in_specs.append(pl.BlockSpec((1, h // ROW_SPLIT, C), lambda r, me_ref: (me_ref[0], r, 0)))
    for p in ps:
        _, h, C = p.shape
        in_specs.append(pl.BlockSpec((3, h // ROW_SPLIT, C), lambda r, me_ref: (0, r, 0)))
        out_specs.append(pl.BlockSpec((1, h // ROW_SPLIT, C), lambda r, me_ref: (me_ref[1], r, 0)))
        out_shape.append(SDS((2, h, C), F32))
    return pl.pallas_call(
        body, name="rs_chip_add",
        grid_spec=pltpu.PrefetchScalarGridSpec(num_scalar_prefetch=1, grid=(ROW_SPLIT,), in_specs=in_specs, out_specs=out_specs),
        out_shape=out_shape, compiler_params=_params(("parallel",)),
    )(me_c, *ps, *qs)


def rs_share(rs):
    n = len(rs)

    def body(*refs):
        outs, send, recv = refs[n:2 * n], refs[2 * n], refs[2 * n + 1]
        x, y, c, _ = _position()
        cps = []
        for t in range(n):
            cp = pltpu.make_async_remote_copy(src_ref=outs[t].at[c], dst_ref=outs[t].at[c], send_sem=send.at[t], recv_sem=recv.at[t],
                                              device_id=(x, y, 1 - c), device_id_type=MESH)
            cp.start()
            cps.append(cp)
        for cp in cps:
            cp.wait()

    return pl.pallas_call(
        body, name="rs_share", in_specs=[HBM] * n, out_specs=[HBM] * n,
        out_shape=[SDS(r.shape, r.dtype) for r in rs],
        input_output_aliases={t: t for t in range(n)},
        scratch_shapes=[pltpu.SemaphoreType.DMA((n,))] * 2,
    )(*rs)


def rs_begin(gs, after, name):
    c = lax.axis_index("c")
    n = len(gs)
    g5 = [g.reshape(N_CHIPS, 2, g.shape[1] // 2, g.shape[2]) for g in gs]
    from_sibling = rs_exchange_sibling(g5)
    pair = rs_pair_add(g5, from_sibling, jnp.reshape(c, (1,)).astype(jnp.int32))
    lands = [lax.empty((3,) + p.shape[1:], p.dtype) for p in pair]
    send, recv, bufs, token = split_start(list(pair) + lands, _rs_plan(n), 3 * n, from_sibling[0] if after is None else after, name)
    return (send, recv, bufs, [g.shape for g in gs]), token


def rs_end(state, after, name):
    x, y, c = lax.axis_index("x"), lax.axis_index("y"), lax.axis_index("c")
    send, recv, bufs, shapes = state
    n = len(shapes)
    bufs = split_wait(send, recv, bufs, _rs_plan(n), after, name)
    half = rs_chip_add(bufs[:n], bufs[n:], jnp.stack([2 * x + y, c]).astype(jnp.int32))
    both = rs_share(half)
    return [b.reshape(s[1], s[2]) for b, s in zip(both, shapes)]


def _pad_last(a, n):
    return jnp.pad(a, [(0, 0)] * (a.ndim - 1) + [(0, n - a.shape[-1])])


def _heads_to_groups(w):
    k = w.shape[0]
    return _pad_last(w.reshape(k, ML_HEADS, ML_HEAD_DIM).transpose(1, 0, 2), GROUP)


def _groups_to_heads(g):
    return g[:, :, :ML_HEAD_DIM].transpose(1, 0, 2).reshape(g.shape[1], D_TOK)


def _cols_to_groups(w):
    k, n = w.shape
    return w.reshape(k, n // GROUP, GROUP).transpose(1, 0, 2)


def _groups_to_cols(g):
    n, k, _ = g.shape
    return g.transpose(1, 0, 2).reshape(k, n * GROUP)


def _chips_to_cols(a):
    return a.transpose(1, 0, 2).reshape(a.shape[1], -1)


def _cols_to_chips(w):
    k, n = w.shape
    return w.reshape(k, N_CHIPS, n // N_CHIPS).transpose(1, 0, 2)


def _mlstm_in_groups(w):
    parts = [_heads_to_groups(w[:, i * D_TOK:(i + 1) * D_TOK]) for i in range(4)]
    gates = _pad_last(w[:, 4 * D_TOK:4 * D_TOK + 2 * ML_HEADS], GROUP)[None]
    qmem = w[:, 4 * D_TOK + 2 * ML_HEADS:][None]
    return jnp.concatenate(parts + [qmem, gates], axis=0)


def _mlstm_in_ungroup(g):
    parts = [_groups_to_heads(g[4 * i:4 * i + 4]) for i in range(4)]
    return jnp.concatenate(parts + [g[17][:, :2 * ML_HEADS], g[16]], axis=1)


def _taps_to_groups(w, width):
    taps = w.shape[0]
    g = _pad_last(w.reshape(taps, -1, width), GROUP).transpose(1, 0, 2)
    return jnp.pad(g, ((0, 0), (0, 8 - taps), (0, 0)))


def _groups_to_taps(g, taps, width):
    return g[:, :taps, :width].transpose(1, 0, 2).reshape(taps, -1)


SMALL_IN_COLS = 384
SMALL_OUT_COLS = 1536
SECTION = 8


class _Gathered:
    def __init__(self, make_src, groups, me, after):
        self.groups, self.states, self.ready = groups, [], {}
        self.group_of = {k: gi for gi, g in enumerate(groups) for k in g}
        token, self.first = after, None
        for gi, g in enumerate(groups):
            srcs = [make_src(k, None if gi == 0 else token[0:1, 0:1]) for k in g]
            placed = place_own([(a, ()) for a in srcs], me, token, f"place_own_{gi}")
            state, token = gather_start(srcs, placed, token, f"gather_start_{gi}")
            self.states.append(state)
            if gi == 0:
                self.first = token[0:1, 0:1]
        self.started = token

    def _get(self, key, after):
        gi = self.group_of[key]
        if gi not in self.ready:
            got = gather_wait(self.states[gi], after if gi else self.started, f"gather_wait_{gi}")
            self.ready[gi] = dict(zip(self.groups[gi], gather_pass_on(got, self.states[gi][3], f"gather_pass_on_{gi}")))
        return self.ready[gi][key]

    def ffn(self, l, i, after):
        return tuple(self._get((n, l, i), after) for n in ("wg", "wu", "wd"))

    def mixer(self, l, after):
        win = _chips_to_cols(self._get(("win", l), after))
        win = _cols_to_groups(win) if l % 2 == 0 else _mlstm_in_groups(win)
        wkv = _cols_to_groups(self._get(("wkv", l), after).reshape(D_MODEL, 2 * D_XA))
        wout = self._get(("wout", l), after)
        if l % 2:
            wout = wout.reshape(D_MODEL, D_MODEL)
            tok = jnp.pad(wout[:D_TOK].reshape(ML_HEADS, ML_HEAD_DIM, D_MODEL), ((0, 0), (0, GROUP - ML_HEAD_DIM), (0, 0)))
            wout = jnp.concatenate([tok, wout[D_TOK:][None]], axis=0)
        return win, wkv, wout


class _GradSink:
    def __init__(self, apply):
        self.queue, self.apply, self.count, self.done = [], apply, 0, None

    @staticmethod
    def _by_chip(key, g):
        if key[0] == "wkv":
            return _groups_to_cols(g).reshape(N_CHIPS, D_MODEL // N_CHIPS, 2 * D_XA)
        if key[0] == "win":
            return _cols_to_chips(_groups_to_cols(g) if key[1] % 2 == 0 else _mlstm_in_ungroup(g))
        if key[0] == "wout" and key[1] % 2:
            full = jnp.concatenate([g[:ML_HEADS, :ML_HEAD_DIM].reshape(D_TOK, D_MODEL), g[ML_HEADS]], axis=0)
            return full.reshape(N_CHIPS, D_MODEL // N_CHIPS, D_MODEL)
        return g

    def push(self, grads):
        keys = list(grads)
        state, token = rs_begin([self._by_chip(k, grads[k]) for k in keys], self.done, f"rs_start_{self.count}")
        if self.queue:
            self._finish(token)
        self.queue.append((keys, state, self.count))
        self.count += 1
        return token

    def flush(self):
        self._finish(self.done)

    def _finish(self, after):
        keys, state, i = self.queue.pop(0)
        for key, g in zip(keys, rs_end(state, after, f"rs_wait_{i}")):
            self.done = self.apply(key, g, self.done)


def _local_step(x, mem, tgt, P, weights, sink):
    memb = mem.astype(BF16)
    saved = []
    pin0 = getattr(weights, "first", None)
    X, Xb = x, (x if pin0 is None else x + pin0).astype(BF16)
    after = Xb
    for l in range(DEPTH):
        s = {}
        s["x0b"] = Xb
        s["wa"] = weights.ffn(l, 0, after)
        s["g1a"], s["u1a"], s["ha"], s["z1"], X1, X1b = ffn_fwd(Xb, X, *s["wa"], P["ln_g"][l][0], P["ln_b"][l][0])
        s["x1b"] = X1b
        s["wm"] = win, wkv, wout = weights.mixer(l, X1b)
        u = proj(X1b, win, "mixer_in")
        kv = proj(memb, wkv, "mem_kv")
        s["u"], s["kv"] = u, kv
        if l % 2 == 0:
            tok = conv_mixer_fwd(u, P["convw"])
            qg = 9
        else:
            s["qk"] = qk_conv_fwd(u, P["qkw"])
            s["hm"], s["cst"], s["mst"] = mlstm_fwd(s["qk"], u, P["bg"])
            tok = head_norm_fwd(s["hm"], u, P["hg"])
            qg = 16
        xa = xattn_fwd(u, qg, kv)
        s["m"] = jnp.concatenate([tok, xa], axis=0)
        s["z2"], X2, X2b = contract_ln(s["m"], wout, X1, P["ln_g"][l][1], P["ln_b"][l][1], 1.0, "mixer_out_ln")
        s["x2b"] = X2b
        s["wb"] = weights.ffn(l, 1, X2b)
        s["g1b"], s["u1b"], s["hb"], s["z3"], X, Xb = ffn_fwd(X2b, X2, *s["wb"], P["ln_g"][l][2], P["ln_b"][l][2])
        after = Xb
        saved.append(s)

    loss, dX = loss_grad(X, tgt)

    G = {"ln_g": [[None] * 3 for _ in range(DEPTH)], "ln_b": [[None] * 3 for _ in range(DEPTH)]}
    pin = [jnp.zeros((1, 1), F32)]

    def ffn_backward(l, i, dX, z, xinb, g1, u1, h, w):
        k = 2 * i
        dgb, dub, dx, dyb, G["ln_g"][l][k], G["ln_b"][l][k] = ffn_bwd(dX, z, P["ln_g"][l][k] + pin[0], w[2], w[0], w[1], g1, u1)
        grads = {("wd", l, i): wgrad(h, dyb, BF16, "wgrad_down"), ("wg", l, i): wgrad(dgb, xinb, BF16, "wgrad_gate"),
                 ("wu", l, i): wgrad(dub, xinb, BF16, "wgrad_up")}
        return dx, grads

    for l in reversed(range(DEPTH)):
        s = saved[l]
        win, wkv, wout = s["wm"]
        dX, grads = ffn_backward(l, 1, dX, s["z3"], s["x2b"], s["g1b"], s["u1b"], s["hb"], s["wb"])
        dm, dz2, dz2b, G["ln_g"][l][1], G["ln_b"][l][1] = mixer_out_bwd(dX, s["z2"], P["ln_g"][l][1], wout)
        grads[("wout", l)] = wgrad(s["m"], dz2b, BF16, "wgrad_out")
        u, kv = s["u"], s["kv"]
        if l % 2 == 0:
            db, dc, dxi, G["convw"] = conv_mixer_bwd(u, P["convw"], dm)
            dq, dkv = xattn_bwd(u, 9, kv, dm, 3)
            du = jnp.concatenate([db, dc, dxi, dq], axis=0)
        else:
            dh, do, G["hg"] = head_norm_bwd(s["hm"], u, P["hg"], dm)
            dqk, dv, dgate, G["bg"] = mlstm_bwd(s["qk"], u, P["bg"], s["cst"], s["mst"], dh)
            duqk, G["qkw"] = qk_conv_bwd(u, P["qkw"], dqk)
            dq, dkv = xattn_bwd(u, 16, kv, dm, 4)
            du = jnp.concatenate([duqk, dv, do, dq, dgate], axis=0)
        grads[("win", l)] = wgrad(s["x1b"], du, BF16, "wgrad_in")
        grads[("wkv", l)] = wgrad(memb, dkv.astype(BF16), BF16, "wgrad_kv")
        dX = contract_t(du, win, dz2, "mixer_in_bwd")
        pin[0] = sink.push(grads)[0:1, 0:1]
        dX, grads = ffn_backward(l, 0, dX, s["z1"], s["x0b"], s["g1a"], s["u1a"], s["ha"], s["wa"])
        pin[0] = sink.push(grads)[0:1, 0:1]
    sink.flush()
    return loss, dX, G


def kernel(x, mem, ln_g, ln_b, ffn_w_gate, ffn_w_up, ffn_w_down, w_kv_mem, w_out, w_in_conv, conv_w, w_in_mlstm, b_gates, qk_conv_w, head_norm_g, loss_target, m_ln_g, m_ln_b, m_ffn_w_gate, m_ffn_w_up, m_ffn_w_down, m_w_kv_mem, m_w_out, m_w_in_conv, m_conv_w, m_w_in_mlstm, m_b_gates, m_qk_conv_w, m_head_norm_g, v_ln_g, v_ln_b, v_ffn_w_gate, v_ffn_w_up, v_ffn_w_down, v_w_kv_mem, v_w_out, v_w_in_conv, v_conv_w, v_w_in_mlstm, v_b_gates, v_qk_conv_w, v_head_norm_g):
    cx, cy = lax.axis_index("x"), lax.axis_index("y")
    chip = 2 * cx + cy

    def make_src(key, pin):
        if key[0] in ("wg", "wu"):
            w = jnp.swapaxes((ffn_w_gate if key[0] == "wg" else ffn_w_up)[key[1], key[2]], 0, 1)
        elif key[0] == "wd":
            w = ffn_w_down[key[1], key[2]]
        elif key[0] == "win":
            w = (w_in_conv, w_in_mlstm)[key[1]][0]
        else:
            w = (w_kv_mem if key[0] == "wkv" else w_out)[key[1]]
        return (w if pin is None else w + pin).astype(BF16)

    ffn_keys = lambda l, i: [("wg", l, i), ("wu", l, i), ("wd", l, i)]
    mixer_keys = lambda l: [("win", l), ("wkv", l), ("wout", l)]
    groups = [ffn_keys(0, 0), mixer_keys(0) + mixer_keys(1), ffn_keys(0, 1), ffn_keys(1, 0), ffn_keys(1, 1)]
    def section(a, width):
        a = a.reshape(-1, a.shape[-1])
        return jnp.pad(a, ((0, SECTION - a.shape[0]), (0, width - a.shape[1])))

    small = jnp.concatenate([section(a, SMALL_IN_COLS) for a in (ln_g, ln_b, conv_w, qk_conv_w)], axis=0)
    smalls = small_allgather(small, reduce=False)
    gathered = _Gathered(make_src, groups, jnp.reshape(chip, (1,)).astype(jnp.int32), smalls)
    smalls = smalls[0::2]
    ln_g_full = _chips_to_cols(smalls[:, 0:6, 0:256]).reshape(DEPTH, 3, 1, D_MODEL)
    ln_b_full = _chips_to_cols(smalls[:, 8:14, 0:256]).reshape(DEPTH, 3, 1, D_MODEL)
    conv_w_full = _chips_to_cols(smalls[:, 16:19, 0:192])
    qk_w_full = _chips_to_cols(smalls[:, 24:28, 0:384])

    P = {"ln_g": ln_g_full, "ln_b": ln_b_full, "convw": _taps_to_groups(conv_w_full, GROUP),
         "qkw": _taps_to_groups(qk_w_full, ML_HEAD_DIM), "bg": _pad_last(b_gates, GROUP),
         "hg": _pad_last(head_norm_g[0], GROUP)[:, None, :]}

    weights = {"ln_g": ln_g, "ln_b": ln_b, "ffn_w_gate": ffn_w_gate, "ffn_w_up": ffn_w_up, "ffn_w_down": ffn_w_down,
               "w_kv_mem": w_kv_mem, "w_out": w_out, "w_in_conv": w_in_conv, "conv_w": conv_w, "w_in_mlstm": w_in_mlstm,
               "b_gates": b_gates, "qk_conv_w": qk_conv_w, "head_norm_g": head_norm_g}
    ms = {"ln_g": m_ln_g, "ln_b": m_ln_b, "ffn_w_gate": m_ffn_w_gate, "ffn_w_up": m_ffn_w_up, "ffn_w_down": m_ffn_w_down,
          "w_kv_mem": m_w_kv_mem, "w_out": m_w_out, "w_in_conv": m_w_in_conv, "conv_w": m_conv_w, "w_in_mlstm": m_w_in_mlstm,
          "b_gates": m_b_gates, "qk_conv_w": m_qk_conv_w, "head_norm_g": m_head_norm_g}
    vs = {"ln_g": v_ln_g, "ln_b": v_ln_b, "ffn_w_gate": v_ffn_w_gate, "ffn_w_up": v_ffn_w_up, "ffn_w_down": v_ffn_w_down,
          "w_kv_mem": v_w_kv_mem, "w_out": v_w_out, "w_in_conv": v_w_in_conv, "conv_w": v_conv_w, "w_in_mlstm": v_w_in_mlstm,
          "b_gates": v_b_gates, "qk_conv_w": v_qk_conv_w, "head_norm_g": v_head_norm_g}
    names = list(weights)
    owner = {"wg": ("ffn_w_gate", True), "wu": ("ffn_w_up", True), "wd": ("ffn_w_down", False), "wkv": ("w_kv_mem", False),
             "wout": ("w_out", False), "win": None}
    updated = {}

    def apply(key, g, after):
        name, transposed = owner[key[0]] or (("w_in_conv", "w_in_mlstm")[key[1]], False)
        idx = (0,) if key[0] == "win" else tuple(key[1:])
        view = (lambda a: jnp.swapaxes(a, -1, -2)) if transposed else (lambda a: a)
        updated[name], token = adamw_into(view(weights[name]), view(ms[name]), view(vs[name]), g, updated.get(name), idx, after,
                                          "adamw_" + name + "_" + "_".join(map(str, idx)))
        return token

    sink = _GradSink(apply)
    loss, grad_x, G = _local_step(x[0], mem[0], loss_target[0], P, gathered, sink)

    dln_g = jnp.concatenate([G["ln_g"][l][k] for l in range(DEPTH) for k in range(3)], axis=0)
    dln_b = jnp.concatenate([G["ln_b"][l][k] for l in range(DEPTH) for k in range(3)], axis=0)
    lane = lax.broadcasted_iota(jnp.int32, (1, GROUP), 1)
    misc = jnp.where(lane < 8, G["bg"], 0.0) + jnp.where(lane == 8, loss, 0.0) + sink.done[0:1, 0:1]
    parts = (dln_g, dln_b, _groups_to_taps(G["convw"], 3, GROUP), misc, _groups_to_taps(G["qkw"], 4, ML_HEAD_DIM),
             G["hg"][:, 0, :ML_HEAD_DIM])
    tot = small_allgather(jnp.concatenate([section(a, SMALL_OUT_COLS) for a in parts], axis=0), reduce=True)
    loss_total = tot[24, 8]

    small_grads = {
        "ln_g": lax.dynamic_slice(tot[0:6, 0:D_MODEL], (0, chip * 256), (6, 256)).reshape(DEPTH, 3, 256),
        "ln_b": lax.dynamic_slice(tot[8:14, 0:D_MODEL], (0, chip * 256), (6, 256)).reshape(DEPTH, 3, 256),
        "conv_w": lax.dynamic_slice(tot[16:19, 0:D_TOK], (0, chip * 192), (3, 192))[None],
        "b_gates": tot[24:25, 0:8],
        "qk_conv_w": lax.dynamic_slice(tot[32:36, 0:2 * D_TOK], (0, chip * 384), (4, 384))[None],
        "head_norm_g": tot[40:44, 0:ML_HEAD_DIM][None],
    }
    grads, deltas, new_m, new_v = [], [], [], []
    for nme in names:
        if nme in updated:
            back = (lambda a: jnp.swapaxes(a, -1, -2)) if nme in ("ffn_w_gate", "ffn_w_up") else (lambda a: a)
            g, d, nm, nv = (back(a) for a in updated[nme])
        else:
            w, g = weights[nme], small_grads[nme]
            two = (math.prod(w.shape[:-1]), w.shape[-1])
            d, nm, nv = (a.reshape(w.shape) for a in adamw(w.reshape(two), g.reshape(two), ms[nme].reshape(two),
                                                           vs[nme].reshape(two), "adamw_" + nme))
        grads.append(g)
        deltas.append(d)
        new_m.append(nm)
        new_v.append(nv)
    return (loss_total, grad_x[None], *grads, *deltas, *new_m, *new_v)
```

```python
import functools
import math

import jax
import jax.numpy as jnp
from jax import lax
from jax.experimental import pallas as pl
from jax.experimental.pallas import tpu as pltpu

F32 = jnp.float32
BF16 = jnp.bfloat16
SDS = jax.ShapeDtypeStruct

D_MODEL = 1024
DEPTH = 2
N_MEM = 256
XA_HEADS = 4
XA_HEAD_DIM = 64
D_XA = 256
D_TOK = 768
ML_HEADS = 4
ML_HEAD_DIM = 192
ML_CHUNK = 64
D_FF = 2816
LN_EPS = 1e-5
ALPHA = (2.0 * DEPTH) ** 0.25
N_CHIPS = 4
N_DEV = 8
FF_SHARD = D_FF // N_CHIPS
GROUP = 256
NEG = -1e30

ADAM_LR = 0.001
ADAM_B1 = 0.9
ADAM_B2 = 0.999
ADAM_EPS = 1e-08
ADAM_WD = 0.01
ADAM_STEP = 10

VMEM_LIMIT = 56 * 1024 * 1024

NN = ((1,), (0,))
NT = ((1,), (1,))
TN = ((0,), (0,))
MESH = pl.DeviceIdType.MESH


def _dot(a, b, dims):
    return lax.dot_general(a, b, (dims, ((), ())), preferred_element_type=F32)


def _bdot(a, b, ca, cb):
    dims = (((ca,), (cb,)), ((0,), (0,)))
    ah, bh = a.astype(BF16), b.astype(BF16)
    al, bl = (a - ah.astype(F32)).astype(BF16), (b - bh.astype(F32)).astype(BF16)
    dot = functools.partial(lax.dot_general, dimension_numbers=dims, preferred_element_type=F32)
    return dot(ah, bh) + dot(al, bh) + dot(ah, bl)


def _bdot1(a, b, ca, cb):
    return lax.dot_general(a.astype(BF16), b.astype(BF16), (((ca,), (cb,)), ((0,), (0,))), preferred_element_type=F32)


def _sigmoid(x):
    return 1.0 / (1.0 + jnp.exp(-x))


def _params(sem, vmem=VMEM_LIMIT):
    return pltpu.CompilerParams(dimension_semantics=sem, vmem_limit_bytes=vmem)


def _tile(n, want):
    t = min(n, want)
    assert n % t == 0, (n, t)
    return t


def _layer_norm(z, gamma, beta):
    mu = jnp.mean(z, axis=-1, keepdims=True)
    zc = z - mu
    var = jnp.mean(zc * zc, axis=-1, keepdims=True)
    return zc * lax.rsqrt(var + LN_EPS) * gamma + beta


def _column_halves(n):
    mid = -(-n // (2 * 128)) * 128
    return ((0, mid), (mid, n))


def _resident(shape):
    return pl.BlockSpec(shape, lambda *_: (0,) * len(shape), pipeline_mode=pl.Buffered(1))


def _group_block(G, want):
    return max(d for d in range(1, max(1, min(G, want)) + 1) if G % d == 0)


def ffn_fwd(xb, x, wg, wu, wd, gamma, beta):
    S, K = xb.shape
    G, N, _ = wg.shape
    ts = _tile(S, 512)

    def body(xb_ref, x_ref, wg_ref, wu_ref, wd_ref, gm_ref, bt_ref, g_ref, u_ref, h_ref, z_ref, xn_ref, xnb_ref):
        j = pl.program_id(1)
        xv = xb_ref[...]
        g = _dot(xv, wg_ref[j], NT)
        u = _dot(xv, wu_ref[j], NT)
        h = (g * _sigmoid(g) * u).astype(BF16)
        g_ref[0] = g.astype(BF16)
        u_ref[0] = u.astype(BF16)
        h_ref[0] = h
        y = _dot(h, wd_ref[j], NN)

        @pl.when(j == 0)
        def _():
            z_ref[...] = y

        @pl.when(j > 0)
        def _():
            z_ref[...] += y

        @pl.when(j == G - 1)
        def _():
            z = ALPHA * x_ref[...] + 0.5 * z_ref[...]
            xn = _layer_norm(z, gm_ref[...], bt_ref[...])
            z_ref[...] = z
            xn_ref[...] = xn
            xnb_ref[...] = xn.astype(BF16)

    row = pl.BlockSpec((ts, K), lambda s, j: (s, 0))
    vec = pl.BlockSpec((1, K), lambda s, j: (0, 0))
    wspec = _resident((G, N, K))
    ospec = pl.BlockSpec((1, ts, N), lambda s, j: (j, s, 0))
    return pl.pallas_call(
        body, name="ffn_fwd", grid=(S // ts, G),
        in_specs=[row, row, wspec, wspec, wspec, vec, vec],
        out_specs=[ospec, ospec, ospec, row, row, row],
        out_shape=[SDS((G, S, N), BF16), SDS((G, S, N), BF16), SDS((G, S, N), BF16),
                   SDS((S, K), F32), SDS((S, K), F32), SDS((S, K), BF16)],
        compiler_params=_params(("parallel", "arbitrary")),
    )(xb, x, wg, wu, wd, gamma, beta)


def proj(xb, w, name):
    S, K = xb.shape
    G, _, N = w.shape
    ts = _tile(S, 1024)
    gb = _group_block(G, 6)

    def body(x_ref, w_ref, y_ref):
        xv = x_ref[...]
        for j in range(gb):
            y_ref[j] = _dot(xv, w_ref[j], NN)

    return pl.pallas_call(
        body, name=name, grid=(S // ts, G // gb),
        in_specs=[pl.BlockSpec((ts, K), lambda s, g: (s, 0)), pl.BlockSpec((gb, K, N), lambda s, g: (g, 0, 0))],
        out_specs=pl.BlockSpec((gb, ts, N), lambda s, g: (g, s, 0)),
        out_shape=SDS((G, S, N), F32),
        compiler_params=_params(("parallel", "parallel")),
    )(xb, w)


def contract_ln(a, w, xres, gamma, beta, scale, name):
    G, S, Kg = a.shape
    N = w.shape[2]
    ts = _tile(S, 1024)

    def body(a_ref, w_ref, x_ref, g_ref, b_ref, z_ref, xn_ref, xb_ref):
        acc = _dot(a_ref[0], w_ref[0], NN)
        for j in range(1, G):
            acc = acc + _dot(a_ref[j], w_ref[j], NN)
        z = ALPHA * x_ref[...] + scale * acc
        xn = _layer_norm(z, g_ref[...], b_ref[...])
        z_ref[...] = z
        xn_ref[...] = xn
        xb_ref[...] = xn.astype(BF16)

    row = pl.BlockSpec((ts, N), lambda s: (s, 0))
    vec = pl.BlockSpec((1, N), lambda s: (0, 0))
    return pl.pallas_call(
        body, name=name, grid=(S // ts,),
        in_specs=[pl.BlockSpec((G, ts, Kg), lambda s: (0, s, 0)), pl.BlockSpec((G, Kg, N), lambda s: (0, 0, 0)), row, vec, vec],
        out_specs=[row, row, row],
        out_shape=[SDS((S, N), F32), SDS((S, N), F32), SDS((S, N), BF16)],
        compiler_params=_params(("parallel",)),
    )(a, w, xres, gamma, beta)


def _layer_norm_bwd(dx, z, gamma):
    mu = jnp.mean(z, axis=-1, keepdims=True)
    zc = z - mu
    var = jnp.mean(zc * zc, axis=-1, keepdims=True)
    rstd = lax.rsqrt(var + LN_EPS)
    xhat = zc * rstd
    dxh = dx * gamma
    m1 = jnp.mean(dxh, axis=-1, keepdims=True)
    m2 = jnp.mean(dxh * xhat, axis=-1, keepdims=True)
    return rstd * (dxh - m1 - xhat * m2), jnp.sum(dx * xhat, axis=0, keepdims=True), jnp.sum(dx, axis=0, keepdims=True)


def ffn_bwd(dxn, z, gamma, wd, wg, wu, g1, u1):
    S, K = dxn.shape
    G, N, _ = wd.shape
    ts = _tile(S, 512)

    def body(dxn_ref, z_ref, gm_ref, wd_ref, wg_ref, wu_ref, g_ref, u_ref, dg_ref, du_ref, dx_ref, dy_ref, dgm_ref, dbt_ref):
        s, j = pl.program_id(0), pl.program_id(1)

        @pl.when((s == 0) & (j == 0))
        def _():
            dgm_ref[...] = jnp.zeros_like(dgm_ref)
            dbt_ref[...] = jnp.zeros_like(dbt_ref)

        @pl.when(j == 0)
        def _():
            dz, dgm, dbt = _layer_norm_bwd(dxn_ref[...], z_ref[...], gm_ref[...])
            dgm_ref[...] += dgm
            dbt_ref[...] += dbt
            dx_ref[...] = ALPHA * dz
            dy_ref[...] = (0.5 * dz).astype(BF16)

        dy = dy_ref[...]
        part = None
        for a, b in _column_halves(N):
            dh = _dot(dy, wd_ref[j, a:b, :], NT)
            g = g_ref[0, :, a:b].astype(F32)
            sig = _sigmoid(g)
            dg = (dh * u_ref[0, :, a:b].astype(F32) * (sig * (1.0 + g * (1.0 - sig)))).astype(BF16)
            du = (dh * (g * sig)).astype(BF16)
            dg_ref[0, :, a:b] = dg
            du_ref[0, :, a:b] = du
            p = _dot(dg, wg_ref[j, a:b, :], NN) + _dot(du, wu_ref[j, a:b, :], NN)
            part = p if part is None else part + p
        dx_ref[...] += part

    row = pl.BlockSpec((ts, K), lambda s, j: (s, 0))
    vec = pl.BlockSpec((1, K), lambda s, j: (0, 0))
    gspec = pl.BlockSpec((1, ts, N), lambda s, j: (j, s, 0))
    wspec = _resident((G, N, K))
    return pl.pallas_call(
        body, name="ffn_bwd", grid=(S // ts, G),
        in_specs=[row, row, vec, wspec, wspec, wspec, gspec, gspec],
        out_specs=[gspec, gspec, row, row, vec, vec],
        out_shape=[SDS((G, S, N), BF16), SDS((G, S, N), BF16), SDS((S, K), F32), SDS((S, K), BF16),
                   SDS((1, K), F32), SDS((1, K), F32)],
        compiler_params=_params(("arbitrary", "arbitrary")),
    )(dxn, z, gamma, wd, wg, wu, g1, u1)


def mixer_out_bwd(dxn, z, gamma, w):
    S, N = dxn.shape
    G, Kg, _ = w.shape
    ts = _tile(S, 512)

    def body(dxn_ref, z_ref, gm_ref, w_ref, dm_ref, dz_ref, dzb_ref, dgm_ref, dbt_ref):
        @pl.when(pl.program_id(0) == 0)
        def _():
            dgm_ref[...] = jnp.zeros_like(dgm_ref)
            dbt_ref[...] = jnp.zeros_like(dbt_ref)

        dz, dgm, dbt = _layer_norm_bwd(dxn_ref[...], z_ref[...], gm_ref[...])
        dgm_ref[...] += dgm
        dbt_ref[...] += dbt
        dzb = dz.astype(BF16)
        dz_ref[...] = dz
        dzb_ref[...] = dzb
        for j in range(G):
            dm_ref[j] = _dot(dzb, w_ref[j], NT)

    row = pl.BlockSpec((ts, N), lambda s: (s, 0))
    vec = pl.BlockSpec((1, N), lambda s: (0, 0))
    return pl.pallas_call(
        body, name="mixer_out_bwd", grid=(S // ts,),
        in_specs=[row, row, vec, pl.BlockSpec((G, Kg, N), lambda s: (0, 0, 0))],
        out_specs=[pl.BlockSpec((G, ts, Kg), lambda s: (0, s, 0)), row, row, vec, vec],
        out_shape=[SDS((G, S, Kg), F32), SDS((S, N), F32), SDS((S, N), BF16), SDS((1, N), F32), SDS((1, N), F32)],
        compiler_params=_params(("arbitrary",)),
    )(dxn, z, gamma, w)


def contract_t(da, w, res, name):
    G, S, Ng = da.shape
    K = w.shape[1]
    ts = _tile(S, 1024)
    gb = _group_block(G, 6)

    def body(da_ref, w_ref, r_ref, o_ref):
        g = pl.program_id(1)
        part = _dot(da_ref[0], w_ref[0], NT)
        for j in range(1, gb):
            part = part + _dot(da_ref[j], w_ref[j], NT)

        @pl.when(g == 0)
        def _():
            o_ref[...] = ALPHA * r_ref[...] + part

        @pl.when(g > 0)
        def _():
            o_ref[...] += part

    row = pl.BlockSpec((ts, K), lambda s, g: (s, 0))
    return pl.pallas_call(
        body, name=name, grid=(S // ts, G // gb),
        in_specs=[pl.BlockSpec((gb, ts, Ng), lambda s, g: (g, s, 0)), pl.BlockSpec((gb, K, Ng), lambda s, g: (g, 0, 0)), row],
        out_specs=row,
        out_shape=SDS((S, K), F32),
        compiler_params=_params(("parallel", "arbitrary")),
    )(da, w, res)


WGRAD_ACC_ELEMS = 6 * 1024 * 256


def wgrad(a, b, out_dtype, name):
    ga, gb = a.ndim == 3, b.ndim == 3
    G = a.shape[0] if ga else b.shape[0]
    S, K = a.shape[-2:]
    N = b.shape[-1]
    ts = _tile(S, 2048)
    ns = S // ts
    ng = _group_block(G, WGRAD_ACC_ELEMS // (K * N))

    def body(a_ref, b_ref, o_ref, acc):
        s = pl.program_id(1)

        @pl.when(s == 0)
        def _():
            acc[...] = jnp.zeros_like(acc)

        for j in range(ng):
            acc[j] += _dot(a_ref[j] if ga else a_ref[...], b_ref[j] if gb else b_ref[...], TN)

        @pl.when(s == ns - 1)
        def _():
            o_ref[...] = acc[...].astype(out_dtype)

    aspec = pl.BlockSpec((ng, ts, K), lambda g, s: (g, s, 0)) if ga else pl.BlockSpec((ts, K), lambda g, s: (s, 0))
    bspec = pl.BlockSpec((ng, ts, N), lambda g, s: (g, s, 0)) if gb else pl.BlockSpec((ts, N), lambda g, s: (s, 0))
    return pl.pallas_call(
        body, name=name, grid=(G // ng, ns),
        in_specs=[aspec, bspec],
        out_specs=pl.BlockSpec((ng, K, N), lambda g, s: (g, 0, 0)),
        out_shape=SDS((G, K, N), out_dtype),
        scratch_shapes=[pltpu.VMEM((ng, K, N), F32)],
        compiler_params=_params(("parallel", "arbitrary")),
    )(a, b)


def loss_grad(xn, tgt):
    S, N = xn.shape
    ts = _tile(S, 1024)

    def body(x_ref, t_ref, l_ref, dx_ref):
        @pl.when(pl.program_id(0) == 0)
        def _():
            l_ref[...] = jnp.zeros_like(l_ref)

        e = x_ref[...] - t_ref[...]
        dx_ref[...] = e * (1.0 / N)
        l_ref[...] += 0.5 * jnp.sum(jnp.mean(e * e, axis=-1, keepdims=True), axis=0, keepdims=True)

    row = pl.BlockSpec((ts, N), lambda s: (s, 0))
    return pl.pallas_call(
        body, name="loss_grad", grid=(S // ts,),
        in_specs=[row, row],
        out_specs=[pl.BlockSpec((1, 1), lambda s: (0, 0)), row],
        out_shape=[SDS((1, 1), F32), SDS((S, N), F32)],
        compiler_params=_params(("arbitrary",)),
    )(xn, tgt)


def _shift_down(x, k):
    if k == 0:
        return x
    rows = lax.broadcasted_iota(jnp.int32, x.shape, 0)
    return jnp.where(rows >= k, pltpu.roll(x, k, 0), 0.0)


def _shift_up(x, k):
    if k == 0:
        return x
    n = x.shape[0]
    rows = lax.broadcasted_iota(jnp.int32, x.shape, 0)
    return jnp.where(rows < n - k, pltpu.roll(x, n - k, 0), 0.0)


LANES = 128


def conv_mixer_fwd(u, cw):
    _, S, _ = u.shape
    nh = GROUP // LANES

    def body(b_ref, c_ref, x_ref, w_ref, o_ref):
        p = c_ref[0] * x_ref[0]
        w = w_ref[0]
        conv = w[2:3] * p + w[1:2] * _shift_down(p, 1) + w[0:1] * _shift_down(p, 2)
        o_ref[0] = (b_ref[0] * conv).astype(BF16)

    def uspec(off):
        return pl.BlockSpec((1, S, LANES), lambda g, h: (g + off, 0, h))

    return pl.pallas_call(
        body, name="conv_mixer_fwd", grid=(3, nh),
        in_specs=[uspec(0), uspec(3), uspec(6), pl.BlockSpec((1, 8, LANES), lambda g, h: (g, 0, h))],
        out_specs=pl.BlockSpec((1, S, LANES), lambda g, h: (g, 0, h)),
        out_shape=SDS((3, S, GROUP), BF16),
        compiler_params=_params(("parallel", "parallel")),
    )(u, u, u, cw)


def conv_mixer_bwd(u, cw, dm):
    _, S, _ = u.shape
    nh = GROUP // LANES

    def body(b_ref, c_ref, x_ref, w_ref, d_ref, db_ref, dc_ref, dx_ref, dw_ref):
        cg, xi = c_ref[0], x_ref[0]
        p = cg * xi
        p1, p2 = _shift_down(p, 1), _shift_down(p, 2)
        w = w_ref[0]
        conv = w[2:3] * p + w[1:2] * p1 + w[0:1] * p2
        dt = d_ref[0]
        db_ref[0] = (dt * conv).astype(BF16)
        dcv = dt * b_ref[0]
        dp = w[2:3] * dcv + w[1:2] * _shift_up(dcv, 1) + w[0:1] * _shift_up(dcv, 2)
        dc_ref[0] = (dp * xi).astype(BF16)
        dx_ref[0] = (dp * cg).astype(BF16)
        dw = jnp.concatenate([jnp.sum(dcv * p2, axis=0, keepdims=True), jnp.sum(dcv * p1, axis=0, keepdims=True),
                              jnp.sum(dcv * p, axis=0, keepdims=True), jnp.zeros((5, LANES), F32)], axis=0)
        dw_ref[0] = dw

    def uspec(off):
        return pl.BlockSpec((1, S, LANES), lambda g, h: (g + off, 0, h))

    ospec = pl.BlockSpec((1, S, LANES), lambda g, h: (g, 0, h))
    wspec = pl.BlockSpec((1, 8, LANES), lambda g, h: (g, 0, h))
    return pl.pallas_call(
        body, name="conv_mixer_bwd", grid=(3, nh),
        in_specs=[uspec(0), uspec(3), uspec(6), wspec, ospec],
        out_specs=[ospec, ospec, ospec, wspec],
        out_shape=[SDS((3, S, GROUP), BF16)] * 3 + [SDS((3, 8, GROUP), F32)],
        compiler_params=_params(("parallel", "parallel")),
    )(u, u, u, cw, dm)


def qk_conv_fwd(u, qw):
    _, S, _ = u.shape
    nh = GROUP // LANES

    def body(u_ref, w_ref, o_ref):
        x = u_ref[0]
        w = w_ref[0]
        pre = w[3:4] * x + w[2:3] * _shift_down(x, 1) + w[1:2] * _shift_down(x, 2) + w[0:1] * _shift_down(x, 3)
        o_ref[0] = pre * _sigmoid(pre)

    spec = pl.BlockSpec((1, S, LANES), lambda g, h: (g, 0, h))
    return pl.pallas_call(
        body, name="qk_conv_fwd", grid=(8, nh),
        in_specs=[spec, pl.BlockSpec((1, 8, LANES), lambda g, h: (g, 0, h))],
        out_specs=spec,
        out_shape=SDS((8, S, GROUP), F32),
        compiler_params=_params(("parallel", "parallel")),
    )(u, qw)


def qk_conv_bwd(u, qw, dqk, du):
    _, S, _ = u.shape
    nh = GROUP // LANES

    def body(u_ref, w_ref, d_ref, du_in_ref, du_ref, dw_ref):
        x = u_ref[0]
        w = w_ref[0]
        x1, x2, x3 = _shift_down(x, 1), _shift_down(x, 2), _shift_down(x, 3)
        pre = w[3:4] * x + w[2:3] * x1 + w[1:2] * x2 + w[0:1] * x3
        sig = _sigmoid(pre)
        dpre = d_ref[0] * (sig * (1.0 + pre * (1.0 - sig)))
        du = w[3:4] * dpre + w[2:3] * _shift_up(dpre, 1) + w[1:2] * _shift_up(dpre, 2) + w[0:1] * _shift_up(dpre, 3)
        du_ref[0] = du.astype(BF16)
        dw = jnp.concatenate([jnp.sum(dpre * x3, axis=0, keepdims=True), jnp.sum(dpre * x2, axis=0, keepdims=True),
                              jnp.sum(dpre * x1, axis=0, keepdims=True), jnp.sum(dpre * x, axis=0, keepdims=True),
                              jnp.zeros((4, LANES), F32)], axis=0)
        dw_ref[0] = dw

    spec = pl.BlockSpec((1, S, LANES), lambda g, h: (g, 0, h))
    wspec = pl.BlockSpec((1, 8, LANES), lambda g, h: (g, 0, h))
    return pl.pallas_call(
        body, name="qk_conv_bwd", grid=(8, nh),
        in_specs=[spec, wspec, spec, pl.BlockSpec(memory_space=pl.ANY)],
        out_specs=[spec, wspec],
        out_shape=[SDS(du.shape, BF16), SDS((8, 8, GROUP), F32)],
        input_output_aliases={3: 0},
        compiler_params=_params(("parallel", "parallel")),
    )(u, qw, dqk, du)


def _head_masks():
    lane = lax.broadcasted_iota(jnp.int32, (1, D_XA), 1)
    return [(lane >= h * XA_HEAD_DIM) & (lane < (h + 1) * XA_HEAD_DIM) for h in range(XA_HEADS)]


def xattn_fwd(u, qg, kv):
    _, S, _ = u.shape
    ts = _tile(S, 1024)
    scale = XA_HEAD_DIM ** -0.5

    def body(q_ref, kv_ref, o_ref):
        q = q_ref[0]
        k = kv_ref[0].astype(BF16)
        v = kv_ref[1]
        o = jnp.zeros((ts, D_XA), F32)
        for m in _head_masks():
            s = _dot(jnp.where(m, q, 0.0).astype(BF16), k, NT) * scale
            s = s - jnp.max(s, axis=-1, keepdims=True)
            e = jnp.exp(s)
            p = e / jnp.sum(e, axis=-1, keepdims=True)
            o = o + _dot(p.astype(BF16), jnp.where(m, v, 0.0).astype(BF16), NN)
        o_ref[0] = o.astype(BF16)

    return pl.pallas_call(
        body, name="xattn_fwd", grid=(S // ts,),
        in_specs=[pl.BlockSpec((1, ts, GROUP), lambda s: (qg, s, 0)), pl.BlockSpec((2, N_MEM, GROUP), lambda s: (0, 0, 0))],
        out_specs=pl.BlockSpec((1, ts, GROUP), lambda s: (0, s, 0)),
        out_shape=SDS((1, S, GROUP), BF16),
        compiler_params=_params(("parallel",)),
    )(u, kv)


def xattn_bwd(u, qg, kv, dm, dg, du=None, dgate=None):
    _, S, _ = u.shape
    ts = _tile(S, 1024)
    scale = XA_HEAD_DIM ** -0.5

    def body(q_ref, kv_ref, do_ref, *refs):
        dq_ref, dkv_ref = refs[-2:]

        @pl.when(pl.program_id(0) == 0)
        def _():
            dkv_ref[...] = jnp.zeros_like(dkv_ref)

        q = q_ref[0]
        k = kv_ref[0]
        v = kv_ref[1]
        kb = k.astype(BF16)
        do = do_ref[0]
        dq = jnp.zeros((ts, D_XA), F32)
        dk = jnp.zeros((N_MEM, D_XA), F32)
        dv = jnp.zeros((N_MEM, D_XA), F32)
        for m in _head_masks():
            qm = jnp.where(m, q, 0.0).astype(BF16)
            s = _dot(qm, kb, NT) * scale
            s = s - jnp.max(s, axis=-1, keepdims=True)
            e = jnp.exp(s)
            p = e / jnp.sum(e, axis=-1, keepdims=True)
            dom = jnp.where(m, do, 0.0).astype(BF16)
            dp = _dot(dom, jnp.where(m, v, 0.0).astype(BF16), NT)
            ds = (p * (dp - jnp.sum(dp * p, axis=-1, keepdims=True)) * scale).astype(BF16)
            dq = dq + _dot(ds, jnp.where(m, k, 0.0).astype(BF16), NN)
            dk = dk + _dot(ds, qm, TN)
            dv = dv + _dot(p.astype(BF16), dom, TN)
        dq_ref[0] = dq.astype(BF16)
        if du is not None:
            dq_ref[1] = refs[0][0]
        dkv_ref[0] += dk
        dkv_ref[1] += dv

    in_specs = [pl.BlockSpec((1, ts, GROUP), lambda s: (qg, s, 0)), pl.BlockSpec((2, N_MEM, GROUP), lambda s: (0, 0, 0)),
                pl.BlockSpec((1, ts, GROUP), lambda s: (dg, s, 0))]
    args, aliases = [u, kv, dm], {}
    dq_spec, dq_shape = pl.BlockSpec((1, ts, GROUP), lambda s: (0, s, 0)), SDS((1, S, GROUP), BF16)
    if du is not None:
        in_specs += [pl.BlockSpec((1, ts, GROUP), lambda s: (0, s, 0)), pl.BlockSpec(memory_space=pl.ANY)]
        args += [dgate, du]
        aliases = {4: 0}
        dq_spec, dq_shape = pl.BlockSpec((2, ts, GROUP), lambda s: (qg // 2, s, 0)), SDS(du.shape, BF16)
    return pl.pallas_call(
        body, name="xattn_bwd", grid=(S // ts,),
        in_specs=in_specs,
        out_specs=[dq_spec, pl.BlockSpec((2, N_MEM, GROUP), lambda s: (0, 0, 0))],
        out_shape=[dq_shape, SDS((2, N_MEM, GROUP), F32)],
        input_output_aliases=aliases,
        compiler_params=_params(("arbitrary",)),
    )(*args)


ML_BLOCK_CHUNKS = 4
H4 = ML_HEADS
L = ML_CHUNK
NLANE = ML_HEAD_DIM


def _chunk_consts():
    r = lax.broadcasted_iota(jnp.int32, (1, L, L), 1)
    c = lax.broadcasted_iota(jnp.int32, (1, L, L), 2)
    return r >= c, r <= c, r == c


def _gate_cols(gb):
    lane = lax.broadcasted_iota(jnp.int32, gb.shape, 1)
    li = jnp.stack([jnp.sum(jnp.where(lane == h, gb, 0.0), axis=1, keepdims=True) for h in range(H4)])
    gf = jnp.stack([jnp.sum(jnp.where(lane == H4 + h, gb, 0.0), axis=1, keepdims=True) for h in range(H4)])
    return li, gf


def _log_sigmoid(x):
    return jnp.minimum(x, 0.0) - jnp.log(1.0 + jnp.exp(-jnp.abs(x)))


def _chunk_forward(q, k, v_aug, li_col, lf_col, c_prev, m_prev):
    tri, tri_t, eye = _chunk_consts()
    lf_row = jnp.sum(jnp.where(eye, lf_col, 0.0), axis=1, keepdims=True)
    li_row = jnp.sum(jnp.where(eye, li_col, 0.0), axis=1, keepdims=True)
    bcum_col = jnp.sum(jnp.where(tri, lf_row, 0.0), axis=2, keepdims=True)
    bcum_row = jnp.sum(jnp.where(tri_t, lf_col, 0.0), axis=1, keepdims=True)
    log_d = jnp.where(tri, bcum_col - bcum_row + li_row, NEG)
    log_inter = bcum_col + m_prev
    m_t = jnp.maximum(log_inter, jnp.max(log_d, axis=2, keepdims=True))
    w_intra = jnp.exp(log_d - m_t)
    w_inter = jnp.exp(log_inter - m_t)
    sc = _bdot(q, k, 2, 2) * w_intra
    qc = _bdot1(q, c_prev, 2, 1)
    num = _bdot(sc, v_aug, 2, 1) + w_inter * qc
    lane = lax.broadcasted_iota(jnp.int32, num.shape, 2)
    den = jnp.sum(jnp.where(lane == NLANE, num, 0.0), axis=2, keepdims=True)
    e_m = jnp.exp(-m_t)
    b_last = jnp.sum(lf_row, axis=2, keepdims=True)
    log_w = b_last - bcum_col + li_col
    m_new = jnp.maximum(b_last + m_prev, jnp.max(log_w, axis=1, keepdims=True))
    w_k = jnp.exp(log_w - m_new)
    decay = jnp.exp(b_last + m_prev - m_new)
    return dict(w_intra=w_intra, w_inter=w_inter, sc=sc, qc=qc, num=num, den=den, e_m=e_m, lane=lane,
                w_k=w_k, decay=decay, m_new=m_new)


def mlstm_fwd(qk, u, bg):
    _, S, _ = qk.shape
    nc = S // L
    cb = min(ML_BLOCK_CHUNKS, nc)
    rows = cb * L
    kscale = ML_HEAD_DIM ** -0.5

    def body(qk_ref, v_ref, g_ref, bg_ref, h_ref, cst_ref, mst_ref, c_sc, m_sc):
        @pl.when(pl.program_id(0) == 0)
        def _():
            c_sc[...] = jnp.zeros_like(c_sc)
            m_sc[...] = jnp.zeros_like(m_sc)

        for c in range(cb):
            sl = pl.ds(c * L, L)
            q = qk_ref[0:H4, sl, :]
            k = qk_ref[H4:2 * H4, sl, :] * kscale
            v = v_ref[:, sl, :]
            lane = lax.broadcasted_iota(jnp.int32, v.shape, 2)
            v_aug = jnp.where(lane == NLANE, 1.0, v)
            li_col, gf = _gate_cols(g_ref[0, sl, :] + bg_ref[...])
            lf_col = _log_sigmoid(gf)
            c_prev = c_sc[...]
            m_prev = m_sc[...]
            f = _chunk_forward(q, k, v_aug, li_col, lf_col, c_prev, m_prev)
            r = 1.0 / jnp.maximum(jnp.abs(f["den"]), f["e_m"])
            h_ref[:, sl, :] = jnp.where(lane < NLANE, f["num"] * r, 0.0)
            cst_ref[c] = c_prev
            mst_ref[c] = jnp.broadcast_to(m_prev, (H4, 1, LANES))
            c_sc[...] = f["decay"] * c_prev + _bdot(k * f["w_k"], v_aug, 1, 1)
            m_sc[...] = f["m_new"]

    def hspec(blk):
        return pl.BlockSpec((H4, rows, GROUP), lambda i: (blk, i, 0))

    return pl.pallas_call(
        body, name="mlstm_fwd", grid=(nc // cb,),
        in_specs=[pl.BlockSpec((2 * H4, rows, GROUP), lambda i: (0, i, 0)), hspec(2),
                  pl.BlockSpec((1, rows, GROUP), lambda i: (17, i, 0)), pl.BlockSpec((1, GROUP), lambda i: (0, 0))],
        out_specs=[hspec(0), pl.BlockSpec((cb, H4, GROUP, GROUP), lambda i: (i, 0, 0, 0)),
                   pl.BlockSpec((cb, H4, 1, LANES), lambda i: (i, 0, 0, 0))],
        out_shape=[SDS((H4, S, GROUP), F32), SDS((nc, H4, GROUP, GROUP), F32), SDS((nc, H4, 1, LANES), F32)],
        scratch_shapes=[pltpu.VMEM((H4, GROUP, GROUP), F32), pltpu.VMEM((H4, 1, 1), F32)],
        compiler_params=_params(("arbitrary",)),
    )(qk, u, u, bg)


def mlstm_bwd(qk, u, bg, cst, mst, dh, du):
    _, S, _ = qk.shape
    nc = S // L
    cb = min(ML_BLOCK_CHUNKS, nc)
    rows = cb * L
    nb = nc // cb
    kscale = ML_HEAD_DIM ** -0.5

    def body(qk_ref, v_ref, g_ref, bg_ref, cst_ref, mst_ref, dh_ref, du_in_ref, dqk_ref, dv_ref, dg_ref, dbg_ref, dc_sc):
        @pl.when(pl.program_id(0) == 0)
        def _():
            dc_sc[...] = jnp.zeros_like(dc_sc)
            dbg_ref[...] = jnp.zeros_like(dbg_ref)

        tri, tri_t, eye = _chunk_consts()
        for c in reversed(range(cb)):
            sl = pl.ds(c * L, L)
            q = qk_ref[0:H4, sl, :]
            k = qk_ref[H4:2 * H4, sl, :] * kscale
            v = v_ref[:, sl, :]
            lane = lax.broadcasted_iota(jnp.int32, v.shape, 2)
            v_aug = jnp.where(lane == NLANE, 1.0, v)
            li_col, gf = _gate_cols(g_ref[0, sl, :] + bg_ref[...])
            lf_col = _log_sigmoid(gf)
            c_prev = cst_ref[c]
            m_prev = mst_ref[c][:, :, 0:1]
            f = _chunk_forward(q, k, v_aug, li_col, lf_col, c_prev, m_prev)
            w_intra, w_inter, sc, num, den, e_m = f["w_intra"], f["w_inter"], f["sc"], f["num"], f["den"], f["e_m"]
            absd = jnp.abs(den)
            r = 1.0 / jnp.maximum(absd, e_m)
            dhv = dh_ref[:, sl, :]
            s1 = jnp.sum(jnp.where(lane < NLANE, dhv * num, 0.0), axis=2, keepdims=True)
            dden = jnp.where(absd > e_m, -s1 * r * r * jnp.sign(den), 0.0)
            dnum = jnp.where(lane == NLANE, dden, jnp.where(lane < NLANE, dhv * r, 0.0))
            dsc = _bdot1(dnum, v_aug, 2, 2)
            dv = _bdot1(sc, dnum, 1, 1)
            gmat = dsc * sc
            dqk = dsc * w_intra
            dq = _bdot1(dqk, k, 2, 1) + w_inter * _bdot1(dnum, c_prev, 2, 2)
            dk = _bdot1(dqk, q, 1, 1)
            dc_prev = _bdot(q * w_inter, dnum, 1, 1)
            dlog_inter = jnp.sum(dnum * f["qc"], axis=2, keepdims=True) * w_inter
            dbcum_col = dlog_inter + jnp.sum(gmat, axis=2, keepdims=True)
            g_row = jnp.sum(gmat, axis=1, keepdims=True)
            dcn = dc_sc[...]
            w_k, decay = f["w_k"], f["decay"]
            kw = k * w_k
            dc_prev = dc_prev + decay * dcn
            db_last = jnp.sum(jnp.sum(dcn * c_prev, axis=2, keepdims=True), axis=1, keepdims=True) * decay
            dkw = _bdot(v_aug, dcn, 2, 2)
            dv = dv + _bdot1(kw, dcn, 2, 1)
            dk = dk + dkw * w_k
            dlogw = jnp.sum(dkw * k, axis=2, keepdims=True) * w_k
            db_last = db_last + jnp.sum(dlogw, axis=1, keepdims=True)
            dbcum_col = dbcum_col - dlogw
            rowi = lax.broadcasted_iota(jnp.int32, (1, L, 1), 1)
            dbcum_col = dbcum_col + jnp.where(rowi == L - 1, db_last, 0.0)
            dbcum_row = jnp.sum(jnp.where(eye, dbcum_col, 0.0), axis=1, keepdims=True) - g_row
            dlf_col = jnp.sum(jnp.where(tri_t, dbcum_row, 0.0), axis=2, keepdims=True)
            dli_col = dlogw + jnp.sum(jnp.where(eye, g_row, 0.0), axis=2, keepdims=True)
            dgf_col = dlf_col * _sigmoid(-gf)
            lane_g = lax.broadcasted_iota(jnp.int32, (L, GROUP), 1)
            dg = jnp.zeros((L, GROUP), F32)
            for h in range(H4):
                dg = dg + jnp.where(lane_g == h, dli_col[h], 0.0) + jnp.where(lane_g == H4 + h, dgf_col[h], 0.0)
            dqk_ref[0:H4, sl, :] = dq
            dqk_ref[H4:2 * H4, sl, :] = dk * kscale
            dv_ref[:, sl, :] = jnp.where(lane < NLANE, dv, 0.0).astype(BF16)
            dg_ref[0, sl, :] = dg.astype(BF16)
            dbg_ref[...] += jnp.sum(dg, axis=0, keepdims=True)
            dc_sc[...] = dc_prev

    def hspec(blk):
        return pl.BlockSpec((H4, rows, GROUP), lambda i: (blk, nb - 1 - i, 0))

    gspec = pl.BlockSpec((1, rows, GROUP), lambda i: (17, nb - 1 - i, 0))
    qkspec = pl.BlockSpec((2 * H4, rows, GROUP), lambda i: (0, nb - 1 - i, 0))
    return pl.pallas_call(
        body, name="mlstm_bwd", grid=(nb,),
        in_specs=[qkspec, hspec(2), gspec, pl.BlockSpec((1, GROUP), lambda i: (0, 0)),
                  pl.BlockSpec((cb, H4, GROUP, GROUP), lambda i: (nb - 1 - i, 0, 0, 0)),
                  pl.BlockSpec((cb, H4, 1, LANES), lambda i: (nb - 1 - i, 0, 0, 0)), hspec(0), pl.BlockSpec(memory_space=pl.ANY)],
        out_specs=[qkspec, hspec(2), pl.BlockSpec((1, rows, GROUP), lambda i: (0, nb - 1 - i, 0)),
                   pl.BlockSpec((1, GROUP), lambda i: (0, 0))],
        input_output_aliases={7: 1},
        out_shape=[SDS((2 * H4, S, GROUP), F32), SDS(du.shape, BF16),
                   SDS((1, S, GROUP), BF16), SDS((1, GROUP), F32)],
        scratch_shapes=[pltpu.VMEM((H4, GROUP, GROUP), F32)],
        compiler_params=_params(("arbitrary",)),
    )(qk, u, u, bg, cst, mst, dh, du)


def head_norm_fwd(hm, u, hg):
    _, S, _ = hm.shape
    ts = _tile(S, 2048)

    def body(h_ref, o_ref, g_ref, t_ref):
        h = h_ref[0]
        lane = lax.broadcasted_iota(jnp.int32, h.shape, 1)
        valid = lane < ML_HEAD_DIM
        mu = jnp.sum(h, axis=-1, keepdims=True) * (1.0 / ML_HEAD_DIM)
        hc = jnp.where(valid, h - mu, 0.0)
        var = jnp.sum(hc * hc, axis=-1, keepdims=True) * (1.0 / ML_HEAD_DIM)
        hn = hc * lax.rsqrt(var + LN_EPS) * g_ref[0]
        t_ref[0] = (_sigmoid(o_ref[0]) * hn).astype(BF16)

    return pl.pallas_call(
        body, name="head_norm_fwd", grid=(H4, S // ts),
        in_specs=[pl.BlockSpec((1, ts, GROUP), lambda h, s: (h, s, 0)), pl.BlockSpec((1, ts, GROUP), lambda h, s: (12 + h, s, 0)),
                  pl.BlockSpec((1, 1, GROUP), lambda h, s: (h, 0, 0))],
        out_specs=pl.BlockSpec((1, ts, GROUP), lambda h, s: (h, s, 0)),
        out_shape=SDS((H4, S, GROUP), BF16),
        compiler_params=_params(("parallel", "parallel")),
    )(hm, u, hg)


def head_norm_bwd(hm, u, hg, dm):
    _, S, _ = hm.shape
    ts = _tile(S, 2048)

    def body(h_ref, o_ref, g_ref, d_ref, dh_ref, do_ref, dg_ref):
        @pl.when(pl.program_id(1) == 0)
        def _():
            dg_ref[...] = jnp.zeros_like(dg_ref)

        h = h_ref[0]
        lane = lax.broadcasted_iota(jnp.int32, h.shape, 1)
        valid = lane < ML_HEAD_DIM
        inv = 1.0 / ML_HEAD_DIM
        mu = jnp.sum(h, axis=-1, keepdims=True) * inv
        hc = jnp.where(valid, h - mu, 0.0)
        var = jnp.sum(hc * hc, axis=-1, keepdims=True) * inv
        rstd = lax.rsqrt(var + LN_EPS)
        xhat = hc * rstd
        g = g_ref[0]
        sig = _sigmoid(o_ref[0])
        dt = jnp.where(valid, d_ref[0], 0.0)
        do_ref[0] = (dt * xhat * g * sig * (1.0 - sig)).astype(BF16)
        dhn = dt * sig
        dg_ref[0] += jnp.sum(dhn * xhat, axis=0, keepdims=True)
        dxh = dhn * g
        m1 = jnp.sum(dxh, axis=-1, keepdims=True) * inv
        m2 = jnp.sum(dxh * xhat, axis=-1, keepdims=True) * inv
        dh_ref[0] = jnp.where(valid, rstd * (dxh - m1 - xhat * m2), 0.0)

    spec = pl.BlockSpec((1, ts, GROUP), lambda h, s: (h, s, 0))
    gspec = pl.BlockSpec((1, 1, GROUP), lambda h, s: (h, 0, 0))
    return pl.pallas_call(
        body, name="head_norm_bwd", grid=(H4, S // ts),
        in_specs=[spec, pl.BlockSpec((1, ts, GROUP), lambda h, s: (12 + h, s, 0)), gspec, spec],
        out_specs=[spec, pl.BlockSpec((1, ts, GROUP), lambda h, s: (12 + h, s, 0)), gspec],
        out_shape=[SDS((H4, S, GROUP), F32), SDS((u.shape[0], S, GROUP), BF16), SDS((H4, 1, GROUP), F32)],
        compiler_params=_params(("parallel", "arbitrary")),
    )(hm, u, hg, dm)


def _adamw_math(w, g, m, v):
    c1 = 1.0 / (1.0 - ADAM_B1 ** ADAM_STEP)
    c2 = 1.0 / (1.0 - ADAM_B2 ** ADAM_STEP)
    nm = ADAM_B1 * m + (1.0 - ADAM_B1) * g
    nv = ADAM_B2 * v + (1.0 - ADAM_B2) * (g * g)
    return -ADAM_LR * ((nm * c1) / (jnp.sqrt(nv * c2) + ADAM_EPS) + ADAM_WD * w), nm, nv


def _row_tile(R, cap=512):
    return R if R <= cap else max(d for d in range(8, cap + 1, 8) if R % d == 0)


def adamw_into(w, m, v, g, outs, idx, after, name):
    R, C = g.shape
    tr = _row_tile(R)
    lead = (0,) * len(idx)

    def body(w_ref, m_ref, v_ref, g_ref, *rest):
        go_ref, d_ref, nm_ref, nv_ref, token = rest[-5:]
        token[...] = jnp.zeros_like(token)
        gv = g_ref[...]
        d, nm, nv = _adamw_math(w_ref[lead], gv, m_ref[lead], v_ref[lead])
        go_ref[lead] = gv
        d_ref[lead] = d
        nm_ref[lead] = nm
        nv_ref[lead] = nv

    blk = pl.BlockSpec((1,) * len(idx) + (tr, C), lambda r: idx + (r, 0))
    any_space = pl.BlockSpec(memory_space=pl.ANY)
    in_specs, args, aliases = [blk, blk, blk, pl.BlockSpec((tr, C), lambda r: (r, 0)), any_space], [w, m, v, g, g if after is None else after], {}
    if outs is not None:
        in_specs += [any_space] * 4
        args += list(outs)
        aliases = {5 + i: i for i in range(4)}
    out = pl.pallas_call(
        body, name=name, grid=(R // tr,),
        in_specs=in_specs, out_specs=[blk] * 4 + [pl.BlockSpec((8, LANES), lambda r: (0, 0))],
        out_shape=[SDS(w.shape, F32)] * 4 + [SDS((8, LANES), F32)],
        input_output_aliases=aliases, compiler_params=_params(("arbitrary",)),
    )(*args)
    return out[:4], out[4]


def adamw(w, g, m, v, name):
    R, C = w.shape
    tr = _row_tile(R)

    def body(w_ref, g_ref, m_ref, v_ref, d_ref, nm_ref, nv_ref):
        d_ref[...], nm_ref[...], nv_ref[...] = _adamw_math(w_ref[...], g_ref[...], m_ref[...], v_ref[...])

    spec = pl.BlockSpec((tr, C), lambda i: (i, 0))
    return pl.pallas_call(
        body, name=name, grid=(R // tr,),
        in_specs=[spec] * 4, out_specs=[spec] * 3,
        out_shape=[SDS((R, C), F32)] * 3,
        compiler_params=_params(("parallel",)),
    )(w, g, m, v)


HBM = pl.BlockSpec(memory_space=pl.ANY)
ROW_SPLIT = 4
PAIR_SPLIT = 1


def _position():
    x, y, c = lax.axis_index("x"), lax.axis_index("y"), lax.axis_index("c")
    return x, y, c, [(1 - x, y), (x, 1 - y), (1 - x, 1 - y)]


def _unique(items):
    arrays = []
    for a, _ in items:
        if not any(a is b for b in arrays):
            arrays.append(a)
    return arrays, [next(i for i, b in enumerate(arrays) if b is a) for a, _ in items]


def place_own(items, me, after, name):
    arrays, src_of = _unique(items)
    n = len(items)
    shapes = [a.shape[len(p):] for a, p in items]

    def body(me_ref, *refs):
        for t in range(n):
            refs[n + 1 + t][0] = refs[t][(0,) * len(items[t][1])]

    in_specs, out_specs = [], []
    for (a, p), shp in zip(items, shapes):
        blk = shp[:-2] + (shp[-2] // ROW_SPLIT, shp[-1])
        lead = (0,) * (len(shp) - 2)
        in_specs.append(pl.BlockSpec((1,) * len(p) + blk, functools.partial(lambda r, me_ref, p, lead: p + lead + (r, 0), p=p, lead=lead)))
        out_specs.append(pl.BlockSpec((1,) + blk, functools.partial(lambda r, me_ref, lead: (me_ref[0],) + lead + (r, 0), lead=lead)))
    in_specs.append(pl.BlockSpec(memory_space=pl.ANY))
    return pl.pallas_call(
        body, name=name,
        grid_spec=pltpu.PrefetchScalarGridSpec(num_scalar_prefetch=1, grid=(ROW_SPLIT,), in_specs=in_specs, out_specs=out_specs),
        out_shape=[SDS((N_CHIPS,) + tuple(shp), a.dtype) for shp, (a, _) in zip(shapes, items)],
        compiler_params=_params(("parallel",)),
    )(me, *[arrays[i] for i in src_of], after)


SEM = pl.BlockSpec(memory_space=pltpu.SEMAPHORE)
IN_HBM = pl.BlockSpec(memory_space=pltpu.HBM)
DATAFLOW = pltpu.SideEffectType.DATAFLOW_SIDE_EFFECTING


def split_start(bufs, plan, n_copies, after, name):
    n = len(bufs)

    def body(*refs):
        send, recv, token = refs[n + 1], refs[n + 2], refs[-1]
        x, y, c, chips = _position()
        for k, (src, dst, dev) in enumerate(plan(refs[:n], x, y, c, chips)):
            pltpu.make_async_remote_copy(src_ref=src, dst_ref=dst, send_sem=send.at[k], recv_sem=recv.at[k],
                                         device_id=dev, device_id_type=MESH).start()
        token[...] = jnp.zeros_like(token)

    out = pl.pallas_call(
        body, name=name,
        out_shape=(pltpu.SemaphoreType.DMA((n_copies,)), pltpu.SemaphoreType.DMA((n_copies,)),
                   *[pltpu.HBM(b.shape, b.dtype) for b in bufs], SDS((8, LANES), F32)),
        in_specs=[IN_HBM] * n + [pl.BlockSpec(memory_space=pl.ANY)],
        out_specs=(SEM, SEM, *[IN_HBM] * n, pl.BlockSpec(memory_space=pltpu.VMEM)),
        input_output_aliases={i: 2 + i for i in range(n)},
        compiler_params=pltpu.CompilerParams(has_side_effects=DATAFLOW),
    )(*[pltpu.with_memory_space_constraint(b, pltpu.HBM) for b in bufs], after)
    return out[0], out[1], list(out[2:2 + n]), out[-1]


def split_wait(send, recv, bufs, plan, after, name):
    n = len(bufs)

    def body(*refs):
        send_ref, recv_ref = refs[n], refs[n + 1]
        x, y, c, chips = _position()
        for k, (src, dst, dev) in enumerate(plan(refs[:n], x, y, c, chips)):
            cp = pltpu.make_async_remote_copy(src_ref=src, dst_ref=dst, send_sem=send_ref.at[k], recv_sem=recv_ref.at[k],
                                              device_id=dev, device_id_type=MESH)
            cp.wait_send()
            cp.wait_recv()

    return list(pl.pallas_call(
        body, name=name, out_shape=tuple(pltpu.HBM(b.shape, b.dtype) for b in bufs),
        in_specs=[IN_HBM] * n + [SEM, SEM, pl.BlockSpec(memory_space=pl.ANY)], out_specs=tuple([IN_HBM] * n),
        input_output_aliases={i: i for i in range(n)},
        compiler_params=pltpu.CompilerParams(has_side_effects=DATAFLOW),
    )(*bufs, send, recv, after))


def _gather_plan(shapes, landing):
    n = len(shapes)

    def plan(refs, x, y, c, chips):
        out = []
        for t in range(n):
            half = shapes[t][0] // 2
            rows = pl.ds(c * half, half)
            for cx, cy in chips:
                slot = 2 * cx + cy if landing else 2 * x + y
                out.append((refs[t].at[rows], refs[n + t].at[slot, rows], (cx, cy, c)))
        return out

    return plan


def gather_start(shards, placed, after, name):
    shapes = [s.shape for s in shards]
    send, recv, bufs, token = split_start(list(shards) + list(placed), _gather_plan(shapes, False), 3 * len(shards), after, name)
    return (send, recv, bufs, shapes), token


def gather_wait(state, after, name):
    send, recv, bufs, shapes = state
    return split_wait(send, recv, bufs, _gather_plan(shapes, True), after, name)[len(shapes):]


def gather_pass_on(placed, shapes, name):
    n = len(placed)

    def body(*refs):
        outs, send, recv = refs[n:2 * n], refs[2 * n], refs[2 * n + 1]
        x, y, c, chips = _position()
        cps = []
        for t in range(n):
            half = shapes[t][0] // 2
            for j, (cx, cy) in enumerate(chips):
                piece = outs[t].at[2 * cx + cy, pl.ds(c * half, half)]
                cp = pltpu.make_async_remote_copy(src_ref=piece, dst_ref=piece, send_sem=send.at[3 * t + j], recv_sem=recv.at[3 * t + j],
                                                  device_id=(x, y, 1 - c), device_id_type=MESH)
                cp.start()
                cps.append(cp)
        for t in range(n):
            half = shapes[t][0] // 2
            for j, (cx, cy) in enumerate(chips):
                piece = outs[t].at[2 * cx + cy, pl.ds((1 - c) * half, half)]
                pltpu.make_async_remote_copy(src_ref=piece, dst_ref=piece, send_sem=send.at[3 * t + j], recv_sem=recv.at[3 * t + j],
                                             device_id=(x, y, 1 - c), device_id_type=MESH).wait_recv()
        for cp in cps:
            cp.wait_send()

    return pl.pallas_call(
        body, name=name,
        in_specs=[HBM] * n, out_specs=[HBM] * n,
        out_shape=[SDS(p.shape, p.dtype) for p in placed],
        input_output_aliases={t: t for t in range(n)},
        scratch_shapes=[pltpu.SemaphoreType.DMA((3 * n,))] * 2,
    )(*placed)


def _flip(k, x, y, c):
    return ((1 - x) if k & 4 else x, (1 - y) if k & 2 else y, (1 - c) if k & 1 else c)


def small_allgather(v, reduce):
    R, C = v.shape

    def body(v_ref, o_ref, *scratch):
        if reduce:
            buf, send, recv = scratch
        else:
            buf, (send, recv) = o_ref, scratch
        x, y, c, _ = _position()
        me = 4 * x + 2 * y + c
        buf[me] = v_ref[...]
        sends = []
        for k in range(1, N_DEV):
            cp = pltpu.make_async_remote_copy(src_ref=v_ref, dst_ref=buf.at[me], send_sem=send.at[k - 1], recv_sem=recv.at[k - 1],
                                              device_id=_flip(k, x, y, c), device_id_type=MESH)
            cp.start()
            sends.append(cp)
        for k in range(1, N_DEV):
            px, py, pc = _flip(k, x, y, c)
            pltpu.make_async_remote_copy(src_ref=v_ref, dst_ref=buf.at[4 * px + 2 * py + pc], send_sem=send.at[k - 1],
                                         recv_sem=recv.at[k - 1], device_id=(px, py, pc), device_id_type=MESH).wait_recv()
        for cp in sends:
            cp.wait_send()
        if reduce:
            acc = buf[0]
            for i in range(1, N_DEV):
                acc = acc + buf[i]
            o_ref[...] = acc

    vm = pl.BlockSpec(memory_space=pltpu.VMEM)
    sems = [pltpu.SemaphoreType.DMA((N_DEV - 1,)), pltpu.SemaphoreType.DMA((N_DEV - 1,))]
    return pl.pallas_call(
        body, name="small_allreduce" if reduce else "small_allgather",
        in_specs=[vm], out_specs=vm,
        out_shape=SDS((R, C) if reduce else (N_DEV, R, C), F32),
        scratch_shapes=([pltpu.VMEM((N_DEV, R, C), F32)] if reduce else []) + sems,
    )(v)


def rs_exchange_sibling(gs):
    n = len(gs)

    def body(*refs):
        ins, outs, send, recv = refs[:n], refs[n:2 * n], refs[2 * n], refs[2 * n + 1]
        x, y, c, _ = _position()
        cps = []
        for t in range(n):
            cp = pltpu.make_async_remote_copy(src_ref=ins[t].at[:, 1 - c], dst_ref=outs[t], send_sem=send.at[t], recv_sem=recv.at[t],
                                              device_id=(x, y, 1 - c), device_id_type=MESH)
            cp.start()
            cps.append(cp)
        for cp in cps:
            cp.wait()

    return pl.pallas_call(
        body, name="rs_exchange_sibling", in_specs=[HBM] * n, out_specs=[HBM] * n,
        out_shape=[SDS((g.shape[0],) + g.shape[2:], g.dtype) for g in gs],
        scratch_shapes=[pltpu.SemaphoreType.DMA((n,)), pltpu.SemaphoreType.DMA((n,))],
    )(*gs)


def rs_pair_add(gs, rs, c):
    n = len(gs)

    def body(c_ref, *refs):
        for t in range(n):
            refs[2 * n + t][0] = (refs[t][0, 0].astype(F32) + refs[n + t][0].astype(F32)).astype(BF16)

    in_specs, out_specs, out_shape = [], [], []
    for g in gs:
        _, _, h, C = g.shape
        in_specs.append(pl.BlockSpec((1, 1, h // PAIR_SPLIT, C), lambda j, r, c_ref: (j, c_ref[0], r, 0)))
    for g in gs:
        _, _, h, C = g.shape
        spec = pl.BlockSpec((1, h // PAIR_SPLIT, C), lambda j, r, c_ref: (j, r, 0))
        in_specs.append(spec)
        out_specs.append(spec)
        out_shape.append(SDS((N_CHIPS, h, C), BF16))
    return pl.pallas_call(
        body, name="rs_pair_add",
        grid_spec=pltpu.PrefetchScalarGridSpec(num_scalar_prefetch=1, grid=(N_CHIPS, PAIR_SPLIT), in_specs=in_specs, out_specs=out_specs),
        out_shape=out_shape, compiler_params=_params(("parallel", "parallel")),
    )(c, *gs, *rs)


def _rs_plan(n):
    def plan(refs, x, y, c, chips):
        return [(refs[t].at[2 * cx + cy], refs[n + t].at[j], (cx, cy, c)) for t in range(n) for j, (cx, cy) in enumerate(chips)]

    return plan


def rs_chip_add(ps, qs, me_c):
    n = len(ps)

    def body(me_ref, *refs):
        for t in range(n):
            q = refs[n + t]
            refs[2 * n + t][0] = ((refs[t][0].astype(F32) + q[0].astype(F32)) + q[1].astype(F32)) + q[2].astype(F32)

    in_specs, out_specs, out_shape = [], [], []
    for p in ps:
        _, h, C = p.shape
        in_specs.append(pl.BlockSpec((1, h // ROW_SPLIT, C), lambda r, me_ref: (me_ref[0], r, 0)))
    for p in ps:
        _, h, C = p.shape
        in_specs.append(pl.BlockSpec((3, h // ROW_SPLIT, C), lambda r, me_ref: (0, r, 0)))
        out_specs.append(pl.BlockSpec((1, h // ROW_SPLIT, C), lambda r, me_ref: (me_ref[1], r, 0)))
        out_shape.append(SDS((2, h, C), F32))
    return pl.pallas_call(
        body, name="rs_chip_add",
        grid_spec=pltpu.PrefetchScalarGridSpec(num_scalar_prefetch=1, grid=(ROW_SPLIT,), in_specs=in_specs, out_specs=out_specs),
        out_shape=out_shape, compiler_params=_params(("parallel",)),
    )(me_c, *ps, *qs)


def rs_share(rs):
    n = len(rs)

    def body(*refs):
        outs, send, recv = refs[n:2 * n], refs[2 * n], refs[2 * n + 1]
        x, y, c, _ = _position()
        cps = []
        for t in range(n):
            cp = pltpu.make_async_remote_copy(src_ref=outs[t].at[c], dst_ref=outs[t].at[c], send_sem=send.at[t], recv_sem=recv.at[t],
                                              device_id=(x, y, 1 - c), device_id_type=MESH)
            cp.start()
            cps.append(cp)
        for cp in cps:
            cp.wait()

    return pl.pallas_call(
        body, name="rs_share", in_specs=[HBM] * n, out_specs=[HBM] * n,
        out_shape=[SDS(r.shape, r.dtype) for r in rs],
        input_output_aliases={t: t for t in range(n)},
        scratch_shapes=[pltpu.SemaphoreType.DMA((n,))] * 2,
    )(*rs)


def rs_begin(gs, after, name):
    c = lax.axis_index("c")
    n = len(gs)
    g5 = [g.reshape(N_CHIPS, 2, g.shape[1] // 2, g.shape[2]) for g in gs]
    from_sibling = rs_exchange_sibling(g5)
    pair = rs_pair_add(g5, from_sibling, jnp.reshape(c, (1,)).astype(jnp.int32))
    lands = [lax.empty((3,) + p.shape[1:], p.dtype) for p in pair]
    send, recv, bufs, token = split_start(list(pair) + lands, _rs_plan(n), 3 * n, from_sibling[0] if after is None else after, name)
    return (send, recv, bufs, [g.shape for g in gs]), token


def rs_end(state, after, name):
    x, y, c = lax.axis_index("x"), lax.axis_index("y"), lax.axis_index("c")
    send, recv, bufs, shapes = state
    n = len(shapes)
    bufs = split_wait(send, recv, bufs, _rs_plan(n), after, name)
    half = rs_chip_add(bufs[:n], bufs[n:], jnp.stack([2 * x + y, c]).astype(jnp.int32))
    both = rs_share(half)
    return [b.reshape(s[1], s[2]) for b, s in zip(both, shapes)]


def _pad_last(a, n):
    return jnp.pad(a, [(0, 0)] * (a.ndim - 1) + [(0, n - a.shape[-1])])


def _heads_to_groups(w):
    k = w.shape[0]
    return _pad_last(w.reshape(k, ML_HEADS, ML_HEAD_DIM).transpose(1, 0, 2), GROUP)


def _groups_to_heads(g):
    return g[:, :, :ML_HEAD_DIM].transpose(1, 0, 2).reshape(g.shape[1], D_TOK)


def _cols_to_groups(w):
    k, n = w.shape
    return w.reshape(k, n // GROUP, GROUP).transpose(1, 0, 2)


def _groups_to_cols(g):
    n, k, _ = g.shape
    return g.transpose(1, 0, 2).reshape(k, n * GROUP)


def _chips_to_cols(a):
    return a.transpose(1, 0, 2).reshape(a.shape[1], -1)


def _cols_to_chips(w):
    k, n = w.shape
    return w.reshape(k, N_CHIPS, n // N_CHIPS).transpose(1, 0, 2)


def _mlstm_in_groups(w):
    parts = [_heads_to_groups(w[:, i * D_TOK:(i + 1) * D_TOK]) for i in range(4)]
    gates = _pad_last(w[:, 4 * D_TOK:4 * D_TOK + 2 * ML_HEADS], GROUP)[None]
    qmem = w[:, 4 * D_TOK + 2 * ML_HEADS:][None]
    return jnp.concatenate(parts + [qmem, gates], axis=0)


def _mlstm_in_ungroup(g):
    parts = [_groups_to_heads(g[4 * i:4 * i + 4]) for i in range(4)]
    return jnp.concatenate(parts + [g[17][:, :2 * ML_HEADS], g[16]], axis=1)


def _taps_to_groups(w, width):
    taps = w.shape[0]
    g = _pad_last(w.reshape(taps, -1, width), GROUP).transpose(1, 0, 2)
    return jnp.pad(g, ((0, 0), (0, 8 - taps), (0, 0)))


def _groups_to_taps(g, taps, width):
    return g[:, :taps, :width].transpose(1, 0, 2).reshape(taps, -1)


SMALL_IN_COLS = 384
SMALL_OUT_COLS = 1536
SECTION = 8


class _Gathered:
    def __init__(self, make_src, groups, me, after):
        self.groups, self.states, self.ready = groups, [], {}
        self.group_of = {k: gi for gi, g in enumerate(groups) for k in g}
        token, self.first = after, None
        for gi, g in enumerate(groups):
            srcs = [make_src(k, None if gi == 0 else token[0:1, 0:1]) for k in g]
            placed = place_own([(a, ()) for a in srcs], me, token, f"place_own_{gi}")
            state, token = gather_start(srcs, placed, token, f"gather_start_{gi}")
            self.states.append(state)
            if gi == 0:
                self.first = token[0:1, 0:1]
        self.started = token

    def _get(self, key, after):
        gi = self.group_of[key]
        if gi not in self.ready:
            got = gather_wait(self.states[gi], after if gi else self.started, f"gather_wait_{gi}")
            self.ready[gi] = dict(zip(self.groups[gi], gather_pass_on(got, self.states[gi][3], f"gather_pass_on_{gi}")))
        return self.ready[gi][key]

    def ffn(self, l, i, after):
        return tuple(self._get((n, l, i), after) for n in ("wg", "wu", "wd"))

    def mixer(self, l, after):
        win = _chips_to_cols(self._get(("win", l), after))
        win = _cols_to_groups(win) if l % 2 == 0 else _mlstm_in_groups(win)
        wkv = _cols_to_groups(self._get(("wkv", l), after).reshape(D_MODEL, 2 * D_XA))
        wout = self._get(("wout", l), after)
        if l % 2:
            wout = wout.reshape(D_MODEL, D_MODEL)
            tok = jnp.pad(wout[:D_TOK].reshape(ML_HEADS, ML_HEAD_DIM, D_MODEL), ((0, 0), (0, GROUP - ML_HEAD_DIM), (0, 0)))
            wout = jnp.concatenate([tok, wout[D_TOK:][None]], axis=0)
        return win, wkv, wout


class _GradSink:
    def __init__(self, apply):
        self.queue, self.apply, self.count, self.done = [], apply, 0, None

    @staticmethod
    def _by_chip(key, g):
        if key[0] == "wkv":
            return _groups_to_cols(g).reshape(N_CHIPS, D_MODEL // N_CHIPS, 2 * D_XA)
        if key[0] == "win":
            return _cols_to_chips(_groups_to_cols(g) if key[1] % 2 == 0 else _mlstm_in_ungroup(g))
        if key[0] == "wout" and key[1] % 2:
            full = jnp.concatenate([g[:ML_HEADS, :ML_HEAD_DIM].reshape(D_TOK, D_MODEL), g[ML_HEADS]], axis=0)
            return full.reshape(N_CHIPS, D_MODEL // N_CHIPS, D_MODEL)
        return g

    def push(self, grads):
        keys = list(grads)
        state, token = rs_begin([self._by_chip(k, grads[k]) for k in keys], self.done, f"rs_start_{self.count}")
        if self.queue:
            self._finish(token)
        self.queue.append((keys, state, self.count))
        self.count += 1
        return token

    def flush(self):
        self._finish(self.done)

    def _finish(self, after):
        keys, state, i = self.queue.pop(0)
        for key, g in zip(keys, rs_end(state, after, f"rs_wait_{i}")):
            self.done = self.apply(key, g, self.done)


def _local_step(x, mem, tgt, P, weights, sink):
    memb = mem.astype(BF16)
    saved = []
    pin0 = getattr(weights, "first", None)
    X, Xb = x, (x if pin0 is None else x + pin0).astype(BF16)
    after = Xb
    for l in range(DEPTH):
        s = {}
        s["x0b"] = Xb
        s["wa"] = weights.ffn(l, 0, after)
        s["g1a"], s["u1a"], s["ha"], s["z1"], X1, X1b = ffn_fwd(Xb, X, *s["wa"], P["ln_g"][l][0], P["ln_b"][l][0])
        s["x1b"] = X1b
        s["wm"] = win, wkv, wout = weights.mixer(l, X1b)
        u = proj(X1b, win, "mixer_in")
        kv = proj(memb, wkv, "mem_kv")
        s["u"], s["kv"] = u, kv
        if l % 2 == 0:
            tok = conv_mixer_fwd(u, P["convw"])
            qg = 9
        else:
            s["qk"] = qk_conv_fwd(u, P["qkw"])
            s["hm"], s["cst"], s["mst"] = mlstm_fwd(s["qk"], u, P["bg"])
            tok = head_norm_fwd(s["hm"], u, P["hg"])
            qg = 16
        xa = xattn_fwd(u, qg, kv)
        s["m"] = jnp.concatenate([tok, xa], axis=0)
        s["z2"], X2, X2b = contract_ln(s["m"], wout, X1, P["ln_g"][l][1], P["ln_b"][l][1], 1.0, "mixer_out_ln")
        s["x2b"] = X2b
        s["wb"] = weights.ffn(l, 1, X2b)
        s["g1b"], s["u1b"], s["hb"], s["z3"], X, Xb = ffn_fwd(X2b, X2, *s["wb"], P["ln_g"][l][2], P["ln_b"][l][2])
        after = Xb
        saved.append(s)

    loss, dX = loss_grad(X, tgt)

    G = {"ln_g": [[None] * 3 for _ in range(DEPTH)], "ln_b": [[None] * 3 for _ in range(DEPTH)]}
    pin = [jnp.zeros((1, 1), F32)]

    def ffn_backward(l, i, dX, z, xinb, g1, u1, h, w):
        k = 2 * i
        dgb, dub, dx, dyb, G["ln_g"][l][k], G["ln_b"][l][k] = ffn_bwd(dX, z, P["ln_g"][l][k] + pin[0], w[2], w[0], w[1], g1, u1)
        grads = {("wd", l, i): wgrad(h, dyb, BF16, "wgrad_down"), ("wg", l, i): wgrad(dgb, xinb, BF16, "wgrad_gate"),
                 ("wu", l, i): wgrad(dub, xinb, BF16, "wgrad_up")}
        return dx, grads

    for l in reversed(range(DEPTH)):
        s = saved[l]
        win, wkv, wout = s["wm"]
        dX, grads = ffn_backward(l, 1, dX, s["z3"], s["x2b"], s["g1b"], s["u1b"], s["hb"], s["wb"])
        dm, dz2, dz2b, G["ln_g"][l][1], G["ln_b"][l][1] = mixer_out_bwd(dX, s["z2"], P["ln_g"][l][1], wout)
        grads[("wout", l)] = wgrad(s["m"], dz2b, BF16, "wgrad_out")
        u, kv = s["u"], s["kv"]
        if l % 2 == 0:
            db, dc, dxi, G["convw"] = conv_mixer_bwd(u, P["convw"], dm)
            dq, dkv = xattn_bwd(u, 9, kv, dm, 3)
            du = jnp.concatenate([db, dc, dxi, dq], axis=0)
        else:
            dh, du, G["hg"] = head_norm_bwd(s["hm"], u, P["hg"], dm)
            dqk, du, dgate, G["bg"] = mlstm_bwd(s["qk"], u, P["bg"], s["cst"], s["mst"], dh, du)
            du, G["qkw"] = qk_conv_bwd(u, P["qkw"], dqk, du)
            du, dkv = xattn_bwd(u, 16, kv, dm, 4, du, dgate)
        grads[("win", l)] = wgrad(s["x1b"], du, BF16, "wgrad_in")
        grads[("wkv", l)] = wgrad(memb, dkv.astype(BF16), BF16, "wgrad_kv")
        dX = contract_t(du, win, dz2, "mixer_in_bwd")
        pin[0] = sink.push(grads)[0:1, 0:1]
        dX, grads = ffn_backward(l, 0, dX, s["z1"], s["x0b"], s["g1a"], s["u1a"], s["ha"], s["wa"])
        pin[0] = sink.push(grads)[0:1, 0:1]
    sink.flush()
    return loss, dX, G


def kernel(x, mem, ln_g, ln_b, ffn_w_gate, ffn_w_up, ffn_w_down, w_kv_mem, w_out, w_in_conv, conv_w, w_in_mlstm, b_gates, qk_conv_w, head_norm_g, loss_target, m_ln_g, m_ln_b, m_ffn_w_gate, m_ffn_w_up, m_ffn_w_down, m_w_kv_mem, m_w_out, m_w_in_conv, m_conv_w, m_w_in_mlstm, m_b_gates, m_qk_conv_w, m_head_norm_g, v_ln_g, v_ln_b, v_ffn_w_gate, v_ffn_w_up, v_ffn_w_down, v_w_kv_mem, v_w_out, v_w_in_conv, v_conv_w, v_w_in_mlstm, v_b_gates, v_qk_conv_w, v_head_norm_g):
    cx, cy = lax.axis_index("x"), lax.axis_index("y")
    chip = 2 * cx + cy

    def make_src(key, pin):
        if key[0] in ("wg", "wu"):
            w = jnp.swapaxes((ffn_w_gate if key[0] == "wg" else ffn_w_up)[key[1], key[2]], 0, 1)
        elif key[0] == "wd":
            w = ffn_w_down[key[1], key[2]]
        elif key[0] == "win":
            w = (w_in_conv, w_in_mlstm)[key[1]][0]
        else:
            w = (w_kv_mem if key[0] == "wkv" else w_out)[key[1]]
        return (w if pin is None else w + pin).astype(BF16)

    ffn_keys = lambda l, i: [("wg", l, i), ("wu", l, i), ("wd", l, i)]
    mixer_keys = lambda l: [("win", l), ("wkv", l), ("wout", l)]
    groups = [ffn_keys(0, 0), mixer_keys(0) + mixer_keys(1), ffn_keys(0, 1), ffn_keys(1, 0), ffn_keys(1, 1)]
    def section(a, width):
        a = a.reshape(-1, a.shape[-1])
        return jnp.pad(a, ((0, SECTION - a.shape[0]), (0, width - a.shape[1])))

    small = jnp.concatenate([section(a, SMALL_IN_COLS) for a in (ln_g, ln_b, conv_w, qk_conv_w)], axis=0)
    smalls = small_allgather(small, reduce=False)
    gathered = _Gathered(make_src, groups, jnp.reshape(chip, (1,)).astype(jnp.int32), smalls)
    smalls = smalls[0::2]
    ln_g_full = _chips_to_cols(smalls[:, 0:6, 0:256]).reshape(DEPTH, 3, 1, D_MODEL)
    ln_b_full = _chips_to_cols(smalls[:, 8:14, 0:256]).reshape(DEPTH, 3, 1, D_MODEL)
    conv_w_full = _chips_to_cols(smalls[:, 16:19, 0:192])
    qk_w_full = _chips_to_cols(smalls[:, 24:28, 0:384])

    P = {"ln_g": ln_g_full, "ln_b": ln_b_full, "convw": _taps_to_groups(conv_w_full, GROUP),
         "qkw": _taps_to_groups(qk_w_full, ML_HEAD_DIM), "bg": _pad_last(b_gates, GROUP),
         "hg": _pad_last(head_norm_g[0], GROUP)[:, None, :]}

    weights = {"ln_g": ln_g, "ln_b": ln_b, "ffn_w_gate": ffn_w_gate, "ffn_w_up": ffn_w_up, "ffn_w_down": ffn_w_down,
               "w_kv_mem": w_kv_mem, "w_out": w_out, "w_in_conv": w_in_conv, "conv_w": conv_w, "w_in_mlstm": w_in_mlstm,
               "b_gates": b_gates, "qk_conv_w": qk_conv_w, "head_norm_g": head_norm_g}
    ms = {"ln_g": m_ln_g, "ln_b": m_ln_b, "ffn_w_gate": m_ffn_w_gate, "ffn_w_up": m_ffn_w_up, "ffn_w_down": m_ffn_w_down,
          "w_kv_mem": m_w_kv_mem, "w_out": m_w_out, "w_in_conv": m_w_in_conv, "conv_w": m_conv_w, "w_in_mlstm": m_w_in_mlstm,
          "b_gates": m_b_gates, "qk_conv_w": m_qk_conv_w, "head_norm_g": m_head_norm_g}
    vs = {"ln_g": v_ln_g, "ln_b": v_ln_b, "ffn_w_gate": v_ffn_w_gate, "ffn_w_up": v_ffn_w_up, "ffn_w_down": v_ffn_w_down,
          "w_kv_mem": v_w_kv_mem, "w_out": v_w_out, "w_in_conv": v_w_in_conv, "conv_w": v_conv_w, "w_in_mlstm": v_w_in_mlstm,
          "b_gates": v_b_gates, "qk_conv_w": v_qk_conv_w, "head_norm_g": v_head_norm_g}
    names = list(weights)
    owner = {"wg": ("ffn_w_gate", True), "wu": ("ffn_w_up", True), "wd": ("ffn_w_down", False), "wkv": ("w_kv_mem", False),
             "wout": ("w_out", False), "win": None}
    updated = {}

    def apply(key, g, after):
        name, transposed = owner[key[0]] or (("w_in_conv", "w_in_mlstm")[key[1]], False)
        idx = (0,) if key[0] == "win" else tuple(key[1:])
        view = (lambda a: jnp.swapaxes(a, -1, -2)) if transposed else (lambda a: a)
        updated[name], token = adamw_into(view(weights[name]), view(ms[name]), view(vs[name]), g, updated.get(name), idx, after,
                                          "adamw_" + name + "_" + "_".join(map(str, idx)))
        return token

    sink = _GradSink(apply)
    loss, grad_x, G = _local_step(x[0], mem[0], loss_target[0], P, gathered, sink)

    dln_g = jnp.concatenate([G["ln_g"][l][k] for l in range(DEPTH) for k in range(3)], axis=0)
    dln_b = jnp.concatenate([G["ln_b"][l][k] for l in range(DEPTH) for k in range(3)], axis=0)
    lane = lax.broadcasted_iota(jnp.int32, (1, GROUP), 1)
    misc = jnp.where(lane < 8, G["bg"], 0.0) + jnp.where(lane == 8, loss, 0.0) + sink.done[0:1, 0:1]
    parts = (dln_g, dln_b, _groups_to_taps(G["convw"], 3, GROUP), misc, _groups_to_taps(G["qkw"], 4, ML_HEAD_DIM),
             G["hg"][:, 0, :ML_HEAD_DIM])
    tot = small_allgather(jnp.concatenate([section(a, SMALL_OUT_COLS) for a in parts], axis=0), reduce=True)
    loss_total = tot[24, 8]

    small_grads = {
        "ln_g": lax.dynamic_slice(tot[0:6, 0:D_MODEL], (0, chip * 256), (6, 256)).reshape(DEPTH, 3, 256),
        "ln_b": lax.dynamic_slice(tot[8:14, 0:D_MODEL], (0, chip * 256), (6, 256)).reshape(DEPTH, 3, 256),
        "conv_w": lax.dynamic_slice(tot[16:19, 0:D_TOK], (0, chip * 192), (3, 192))[None],
        "b_gates": tot[24:25, 0:8],
        "qk_conv_w": lax.dynamic_slice(tot[32:36, 0:2 * D_TOK], (0, chip * 384), (4, 384))[None],
        "head_norm_g": tot[40:44, 0:ML_HEAD_DIM][None],
    }
    grads, deltas, new_m, new_v = [], [], [], []
    for nme in names:
        if nme in updated:
            back = (lambda a: jnp.swapaxes(a, -1, -2)) if nme in ("ffn_w_gate", "ffn_w_up") else (lambda a: a)
            g, d, nm, nv = (back(a) for a in updated[nme])
        else:
            w, g = weights[nme], small_grads[nme]
            two = (math.prod(w.shape[:-1]), w.shape[-1])
            d, nm, nv = (a.reshape(w.shape) for a in adamw(w.reshape(two), g.reshape(two), ms[nme].reshape(two),
                                                           vs[nme].reshape(two), "adamw_" + nme))
        grads.append(g)
        deltas.append(d)
        new_m.append(nm)
        new_v.append(nv)
    return (loss_total, grad_x[None], *grads, *deltas, *new_m, *new_v)
```

```python
import functools
import math

import jax
import jax.numpy as jnp
from jax import lax
from jax.experimental import pallas as pl
from jax.experimental.pallas import tpu as pltpu

F32 = jnp.float32
BF16 = jnp.bfloat16
SDS = jax.ShapeDtypeStruct

D_MODEL = 1024
DEPTH = 2
N_MEM = 256
XA_HEADS = 4
XA_HEAD_DIM = 64
D_XA = 256
D_TOK = 768
ML_HEADS = 4
ML_HEAD_DIM = 192
ML_CHUNK = 64
D_FF = 2816
LN_EPS = 1e-5
ALPHA = (2.0 * DEPTH) ** 0.25
N_CHIPS = 4
N_DEV = 8
FF_SHARD = D_FF // N_CHIPS
GROUP = 256
NEG = -1e30

ADAM_LR = 0.001
ADAM_B1 = 0.9
ADAM_B2 = 0.999
ADAM_EPS = 1e-08
ADAM_WD = 0.01
ADAM_STEP = 10

VMEM_LIMIT = 56 * 1024 * 1024

NN = ((1,), (0,))
NT = ((1,), (1,))
TN = ((0,), (0,))
MESH = pl.DeviceIdType.MESH


def _dot(a, b, dims):
    return lax.dot_general(a, b, (dims, ((), ())), preferred_element_type=F32)


def _bdot(a, b, ca, cb):
    dims = (((ca,), (cb,)), ((0,), (0,)))
    ah, bh = a.astype(BF16), b.astype(BF16)
    al, bl = (a - ah.astype(F32)).astype(BF16), (b - bh.astype(F32)).astype(BF16)
    dot = functools.partial(lax.dot_general, dimension_numbers=dims, preferred_element_type=F32)
    return dot(ah, bh) + dot(al, bh) + dot(ah, bl)


def _bdot1(a, b, ca, cb):
    return lax.dot_general(a.astype(BF16), b.astype(BF16), (((ca,), (cb,)), ((0,), (0,))), preferred_element_type=F32)


def _sigmoid(x):
    return 1.0 / (1.0 + jnp.exp(-x))


def _params(sem, vmem=VMEM_LIMIT):
    return pltpu.CompilerParams(dimension_semantics=sem, vmem_limit_bytes=vmem)


def _tile(n, want):
    t = min(n, want)
    assert n % t == 0, (n, t)
    return t


def _layer_norm(z, gamma, beta):
    mu = jnp.mean(z, axis=-1, keepdims=True)
    zc = z - mu
    var = jnp.mean(zc * zc, axis=-1, keepdims=True)
    return zc * lax.rsqrt(var + LN_EPS) * gamma + beta


def _column_halves(n):
    mid = -(-n // (2 * 128)) * 128
    return ((0, mid), (mid, n))


def _resident(shape):
    return pl.BlockSpec(shape, lambda *_: (0,) * len(shape), pipeline_mode=pl.Buffered(1))


def _group_block(G, want):
    return max(d for d in range(1, max(1, min(G, want)) + 1) if G % d == 0)


def ffn_fwd(xb, x, wg, wu, wd, gamma, beta):
    S, K = xb.shape
    G, N, _ = wg.shape
    ts = _tile(S, 512)

    def body(xb_ref, x_ref, wg_ref, wu_ref, wd_ref, gm_ref, bt_ref, g_ref, u_ref, h_ref, z_ref, xn_ref, xnb_ref):
        j = pl.program_id(1)
        xv = xb_ref[...]
        g = _dot(xv, wg_ref[j], NT)
        u = _dot(xv, wu_ref[j], NT)
        h = (g * _sigmoid(g) * u).astype(BF16)
        g_ref[0] = g.astype(BF16)
        u_ref[0] = u.astype(BF16)
        h_ref[0] = h
        y = _dot(h, wd_ref[j], NN)

        @pl.when(j == 0)
        def _():
            z_ref[...] = y

        @pl.when(j > 0)
        def _():
            z_ref[...] += y

        @pl.when(j == G - 1)
        def _():
            z = ALPHA * x_ref[...] + 0.5 * z_ref[...]
            xn = _layer_norm(z, gm_ref[...], bt_ref[...])
            z_ref[...] = z
            xn_ref[...] = xn
            xnb_ref[...] = xn.astype(BF16)

    row = pl.BlockSpec((ts, K), lambda s, j: (s, 0))
    vec = pl.BlockSpec((1, K), lambda s, j: (0, 0))
    wspec = _resident((G, N, K))
    ospec = pl.BlockSpec((1, ts, N), lambda s, j: (j, s, 0))
    return pl.pallas_call(
        body, name="ffn_fwd", grid=(S // ts, G),
        in_specs=[row, row, wspec, wspec, wspec, vec, vec],
        out_specs=[ospec, ospec, ospec, row, row, row],
        out_shape=[SDS((G, S, N), BF16), SDS((G, S, N), BF16), SDS((G, S, N), BF16),
                   SDS((S, K), F32), SDS((S, K), F32), SDS((S, K), BF16)],
        compiler_params=_params(("parallel", "arbitrary")),
    )(xb, x, wg, wu, wd, gamma, beta)


def proj(xb, w, name):
    S, K = xb.shape
    G, _, N = w.shape
    ts = _tile(S, 1024)
    gb = _group_block(G, 6)

    def body(x_ref, w_ref, y_ref):
        xv = x_ref[...]
        for j in range(gb):
            y_ref[j] = _dot(xv, w_ref[j], NN)

    return pl.pallas_call(
        body, name=name, grid=(S // ts, G // gb),
        in_specs=[pl.BlockSpec((ts, K), lambda s, g: (s, 0)), pl.BlockSpec((gb, K, N), lambda s, g: (g, 0, 0))],
        out_specs=pl.BlockSpec((gb, ts, N), lambda s, g: (g, s, 0)),
        out_shape=SDS((G, S, N), F32),
        compiler_params=_params(("parallel", "parallel")),
    )(xb, w)


def contract_ln(a, w, xres, gamma, beta, scale, name):
    G, S, Kg = a.shape
    N = w.shape[2]
    ts = _tile(S, 1024)

    def body(a_ref, w_ref, x_ref, g_ref, b_ref, z_ref, xn_ref, xb_ref):
        acc = _dot(a_ref[0], w_ref[0], NN)
        for j in range(1, G):
            acc = acc + _dot(a_ref[j], w_ref[j], NN)
        z = ALPHA * x_ref[...] + scale * acc
        xn = _layer_norm(z, g_ref[...], b_ref[...])
        z_ref[...] = z
        xn_ref[...] = xn
        xb_ref[...] = xn.astype(BF16)

    row = pl.BlockSpec((ts, N), lambda s: (s, 0))
    vec = pl.BlockSpec((1, N), lambda s: (0, 0))
    return pl.pallas_call(
        body, name=name, grid=(S // ts,),
        in_specs=[pl.BlockSpec((G, ts, Kg), lambda s: (0, s, 0)), pl.BlockSpec((G, Kg, N), lambda s: (0, 0, 0)), row, vec, vec],
        out_specs=[row, row, row],
        out_shape=[SDS((S, N), F32), SDS((S, N), F32), SDS((S, N), BF16)],
        compiler_params=_params(("parallel",)),
    )(a, w, xres, gamma, beta)


def _layer_norm_bwd(dx, z, gamma):
    mu = jnp.mean(z, axis=-1, keepdims=True)
    zc = z - mu
    var = jnp.mean(zc * zc, axis=-1, keepdims=True)
    rstd = lax.rsqrt(var + LN_EPS)
    xhat = zc * rstd
    dxh = dx * gamma
    m1 = jnp.mean(dxh, axis=-1, keepdims=True)
    m2 = jnp.mean(dxh * xhat, axis=-1, keepdims=True)
    return rstd * (dxh - m1 - xhat * m2), jnp.sum(dx * xhat, axis=0, keepdims=True), jnp.sum(dx, axis=0, keepdims=True)


def ffn_bwd(dxn, z, gamma, wd, wg, wu, g1, u1):
    S, K = dxn.shape
    G, N, _ = wd.shape
    ts = _tile(S, 512)

    def body(dxn_ref, z_ref, gm_ref, wd_ref, wg_ref, wu_ref, g_ref, u_ref, dg_ref, du_ref, dx_ref, dy_ref, dgm_ref, dbt_ref):
        s, j = pl.program_id(0), pl.program_id(1)

        @pl.when((s == 0) & (j == 0))
        def _():
            dgm_ref[...] = jnp.zeros_like(dgm_ref)
            dbt_ref[...] = jnp.zeros_like(dbt_ref)

        @pl.when(j == 0)
        def _():
            dz, dgm, dbt = _layer_norm_bwd(dxn_ref[...], z_ref[...], gm_ref[...])
            dgm_ref[...] += dgm
            dbt_ref[...] += dbt
            dx_ref[...] = ALPHA * dz
            dy_ref[...] = (0.5 * dz).astype(BF16)

        dy = dy_ref[...]
        part = None
        for a, b in _column_halves(N):
            dh = _dot(dy, wd_ref[j, a:b, :], NT)
            g = g_ref[0, :, a:b].astype(F32)
            sig = _sigmoid(g)
            dg = (dh * u_ref[0, :, a:b].astype(F32) * (sig * (1.0 + g * (1.0 - sig)))).astype(BF16)
            du = (dh * (g * sig)).astype(BF16)
            dg_ref[0, :, a:b] = dg
            du_ref[0, :, a:b] = du
            p = _dot(dg, wg_ref[j, a:b, :], NN) + _dot(du, wu_ref[j, a:b, :], NN)
            part = p if part is None else part + p
        dx_ref[...] += part

    row = pl.BlockSpec((ts, K), lambda s, j: (s, 0))
    vec = pl.BlockSpec((1, K), lambda s, j: (0, 0))
    gspec = pl.BlockSpec((1, ts, N), lambda s, j: (j, s, 0))
    wspec = _resident((G, N, K))
    return pl.pallas_call(
        body, name="ffn_bwd", grid=(S // ts, G),
        in_specs=[row, row, vec, wspec, wspec, wspec, gspec, gspec],
        out_specs=[gspec, gspec, row, row, vec, vec],
        out_shape=[SDS((G, S, N), BF16), SDS((G, S, N), BF16), SDS((S, K), F32), SDS((S, K), BF16),
                   SDS((1, K), F32), SDS((1, K), F32)],
        compiler_params=_params(("arbitrary", "arbitrary")),
    )(dxn, z, gamma, wd, wg, wu, g1, u1)


def mixer_out_bwd(dxn, z, gamma, w):
    S, N = dxn.shape
    G, Kg, _ = w.shape
    ts = _tile(S, 512)

    def body(dxn_ref, z_ref, gm_ref, w_ref, dm_ref, dz_ref, dzb_ref, dgm_ref, dbt_ref):
        @pl.when(pl.program_id(0) == 0)
        def _():
            dgm_ref[...] = jnp.zeros_like(dgm_ref)
            dbt_ref[...] = jnp.zeros_like(dbt_ref)

        dz, dgm, dbt = _layer_norm_bwd(dxn_ref[...], z_ref[...], gm_ref[...])
        dgm_ref[...] += dgm
        dbt_ref[...] += dbt
        dzb = dz.astype(BF16)
        dz_ref[...] = dz
        dzb_ref[...] = dzb
        for j in range(G):
            dm_ref[j] = _dot(dzb, w_ref[j], NT)

    row = pl.BlockSpec((ts, N), lambda s: (s, 0))
    vec = pl.BlockSpec((1, N), lambda s: (0, 0))
    return pl.pallas_call(
        body, name="mixer_out_bwd", grid=(S // ts,),
        in_specs=[row, row, vec, pl.BlockSpec((G, Kg, N), lambda s: (0, 0, 0))],
        out_specs=[pl.BlockSpec((G, ts, Kg), lambda s: (0, s, 0)), row, row, vec, vec],
        out_shape=[SDS((G, S, Kg), F32), SDS((S, N), F32), SDS((S, N), BF16), SDS((1, N), F32), SDS((1, N), F32)],
        compiler_params=_params(("arbitrary",)),
    )(dxn, z, gamma, w)


def contract_t(da, w, res, name):
    G, S, Ng = da.shape
    K = w.shape[1]
    ts = _tile(S, 1024)
    gb = _group_block(G, 6)

    def body(da_ref, w_ref, r_ref, o_ref):
        g = pl.program_id(1)
        part = _dot(da_ref[0], w_ref[0], NT)
        for j in range(1, gb):
            part = part + _dot(da_ref[j], w_ref[j], NT)

        @pl.when(g == 0)
        def _():
            o_ref[...] = ALPHA * r_ref[...] + part

        @pl.when(g > 0)
        def _():
            o_ref[...] += part

    row = pl.BlockSpec((ts, K), lambda s, g: (s, 0))
    return pl.pallas_call(
        body, name=name, grid=(S // ts, G // gb),
        in_specs=[pl.BlockSpec((gb, ts, Ng), lambda s, g: (g, s, 0)), pl.BlockSpec((gb, K, Ng), lambda s, g: (g, 0, 0)), row],
        out_specs=row,
        out_shape=SDS((S, K), F32),
        compiler_params=_params(("parallel", "arbitrary")),
    )(da, w, res)


WGRAD_ACC_ELEMS = 6 * 1024 * 256


def wgrad(a, b, out_dtype, name):
    ga, gb = a.ndim == 3, b.ndim == 3
    G = a.shape[0] if ga else b.shape[0]
    S, K = a.shape[-2:]
    N = b.shape[-1]
    ts = _tile(S, 2048)
    ns = S // ts
    ng = _group_block(G, WGRAD_ACC_ELEMS // (K * N))

    def body(a_ref, b_ref, o_ref, acc):
        s = pl.program_id(1)

        @pl.when(s == 0)
        def _():
            acc[...] = jnp.zeros_like(acc)

        for j in range(ng):
            acc[j] += _dot(a_ref[j] if ga else a_ref[...], b_ref[j] if gb else b_ref[...], TN)

        @pl.when(s == ns - 1)
        def _():
            o_ref[...] = acc[...].astype(out_dtype)

    aspec = pl.BlockSpec((ng, ts, K), lambda g, s: (g, s, 0)) if ga else pl.BlockSpec((ts, K), lambda g, s: (s, 0))
    bspec = pl.BlockSpec((ng, ts, N), lambda g, s: (g, s, 0)) if gb else pl.BlockSpec((ts, N), lambda g, s: (s, 0))
    return pl.pallas_call(
        body, name=name, grid=(G // ng, ns),
        in_specs=[aspec, bspec],
        out_specs=pl.BlockSpec((ng, K, N), lambda g, s: (g, 0, 0)),
        out_shape=SDS((G, K, N), out_dtype),
        scratch_shapes=[pltpu.VMEM((ng, K, N), F32)],
        compiler_params=_params(("parallel", "arbitrary")),
    )(a, b)


def loss_grad(xn, tgt):
    S, N = xn.shape
    ts = _tile(S, 1024)

    def body(x_ref, t_ref, l_ref, dx_ref):
        @pl.when(pl.program_id(0) == 0)
        def _():
            l_ref[...] = jnp.zeros_like(l_ref)

        e = x_ref[...] - t_ref[...]
        dx_ref[...] = e * (1.0 / N)
        l_ref[...] += 0.5 * jnp.sum(jnp.mean(e * e, axis=-1, keepdims=True), axis=0, keepdims=True)

    row = pl.BlockSpec((ts, N), lambda s: (s, 0))
    return pl.pallas_call(
        body, name="loss_grad", grid=(S // ts,),
        in_specs=[row, row],
        out_specs=[pl.BlockSpec((1, 1), lambda s: (0, 0)), row],
        out_shape=[SDS((1, 1), F32), SDS((S, N), F32)],
        compiler_params=_params(("arbitrary",)),
    )(xn, tgt)


def _shift_down(x, k):
    if k == 0:
        return x
    rows = lax.broadcasted_iota(jnp.int32, x.shape, 0)
    return jnp.where(rows >= k, pltpu.roll(x, k, 0), 0.0)


def _shift_up(x, k):
    if k == 0:
        return x
    n = x.shape[0]
    rows = lax.broadcasted_iota(jnp.int32, x.shape, 0)
    return jnp.where(rows < n - k, pltpu.roll(x, n - k, 0), 0.0)


LANES = 128


def conv_mixer_fwd(u, cw):
    _, S, _ = u.shape
    nh = GROUP // LANES

    def body(b_ref, c_ref, x_ref, w_ref, o_ref):
        p = c_ref[0] * x_ref[0]
        w = w_ref[0]
        conv = w[2:3] * p + w[1:2] * _shift_down(p, 1) + w[0:1] * _shift_down(p, 2)
        o_ref[0] = (b_ref[0] * conv).astype(BF16)

    def uspec(off):
        return pl.BlockSpec((1, S, LANES), lambda g, h: (g + off, 0, h))

    return pl.pallas_call(
        body, name="conv_mixer_fwd", grid=(3, nh),
        in_specs=[uspec(0), uspec(3), uspec(6), pl.BlockSpec((1, 8, LANES), lambda g, h: (g, 0, h))],
        out_specs=pl.BlockSpec((1, S, LANES), lambda g, h: (g, 0, h)),
        out_shape=SDS((3, S, GROUP), BF16),
        compiler_params=_params(("parallel", "parallel")),
    )(u, u, u, cw)


def conv_mixer_bwd(u, cw, dm):
    _, S, _ = u.shape
    nh = GROUP // LANES

    def body(b_ref, c_ref, x_ref, w_ref, d_ref, db_ref, dc_ref, dx_ref, dw_ref):
        cg, xi = c_ref[0], x_ref[0]
        p = cg * xi
        p1, p2 = _shift_down(p, 1), _shift_down(p, 2)
        w = w_ref[0]
        conv = w[2:3] * p + w[1:2] * p1 + w[0:1] * p2
        dt = d_ref[0]
        db_ref[0] = (dt * conv).astype(BF16)
        dcv = dt * b_ref[0]
        dp = w[2:3] * dcv + w[1:2] * _shift_up(dcv, 1) + w[0:1] * _shift_up(dcv, 2)
        dc_ref[0] = (dp * xi).astype(BF16)
        dx_ref[0] = (dp * cg).astype(BF16)
        dw = jnp.concatenate([jnp.sum(dcv * p2, axis=0, keepdims=True), jnp.sum(dcv * p1, axis=0, keepdims=True),
                              jnp.sum(dcv * p, axis=0, keepdims=True), jnp.zeros((5, LANES), F32)], axis=0)
        dw_ref[0] = dw

    def uspec(off):
        return pl.BlockSpec((1, S, LANES), lambda g, h: (g + off, 0, h))

    ospec = pl.BlockSpec((1, S, LANES), lambda g, h: (g, 0, h))
    wspec = pl.BlockSpec((1, 8, LANES), lambda g, h: (g, 0, h))
    return pl.pallas_call(
        body, name="conv_mixer_bwd", grid=(3, nh),
        in_specs=[uspec(0), uspec(3), uspec(6), wspec, ospec],
        out_specs=[ospec, ospec, ospec, wspec],
        out_shape=[SDS((3, S, GROUP), BF16)] * 3 + [SDS((3, 8, GROUP), F32)],
        compiler_params=_params(("parallel", "parallel")),
    )(u, u, u, cw, dm)


def qk_conv_fwd(u, qw):
    _, S, _ = u.shape
    nh = GROUP // LANES

    def body(u_ref, w_ref, o_ref):
        x = u_ref[0]
        w = w_ref[0]
        pre = w[3:4] * x + w[2:3] * _shift_down(x, 1) + w[1:2] * _shift_down(x, 2) + w[0:1] * _shift_down(x, 3)
        o_ref[0] = pre * _sigmoid(pre)

    spec = pl.BlockSpec((1, S, LANES), lambda g, h: (g, 0, h))
    return pl.pallas_call(
        body, name="qk_conv_fwd", grid=(8, nh),
        in_specs=[spec, pl.BlockSpec((1, 8, LANES), lambda g, h: (g, 0, h))],
        out_specs=spec,
        out_shape=SDS((8, S, GROUP), F32),
        compiler_params=_params(("parallel", "parallel")),
    )(u, qw)


def qk_conv_bwd(u, qw, dqk, du):
    _, S, _ = u.shape
    nh = GROUP // LANES

    def body(u_ref, w_ref, d_ref, du_in_ref, du_ref, dw_ref):
        x = u_ref[0]
        w = w_ref[0]
        x1, x2, x3 = _shift_down(x, 1), _shift_down(x, 2), _shift_down(x, 3)
        pre = w[3:4] * x + w[2:3] * x1 + w[1:2] * x2 + w[0:1] * x3
        sig = _sigmoid(pre)
        dpre = d_ref[0] * (sig * (1.0 + pre * (1.0 - sig)))
        du = w[3:4] * dpre + w[2:3] * _shift_up(dpre, 1) + w[1:2] * _shift_up(dpre, 2) + w[0:1] * _shift_up(dpre, 3)
        du_ref[0] = du.astype(BF16)
        dw = jnp.concatenate([jnp.sum(dpre * x3, axis=0, keepdims=True), jnp.sum(dpre * x2, axis=0, keepdims=True),
                              jnp.sum(dpre * x1, axis=0, keepdims=True), jnp.sum(dpre * x, axis=0, keepdims=True),
                              jnp.zeros((4, LANES), F32)], axis=0)
        dw_ref[0] = dw

    spec = pl.BlockSpec((1, S, LANES), lambda g, h: (g, 0, h))
    wspec = pl.BlockSpec((1, 8, LANES), lambda g, h: (g, 0, h))
    return pl.pallas_call(
        body, name="qk_conv_bwd", grid=(8, nh),
        in_specs=[spec, wspec, spec, pl.BlockSpec(memory_space=pl.ANY)],
        out_specs=[spec, wspec],
        out_shape=[SDS(du.shape, BF16), SDS((8, 8, GROUP), F32)],
        input_output_aliases={3: 0},
        compiler_params=_params(("parallel", "parallel")),
    )(u, qw, dqk, du)


def _head_masks():
    lane = lax.broadcasted_iota(jnp.int32, (1, D_XA), 1)
    return [(lane >= h * XA_HEAD_DIM) & (lane < (h + 1) * XA_HEAD_DIM) for h in range(XA_HEADS)]


def xattn_fwd(u, qg, kv):
    _, S, _ = u.shape
    ts = _tile(S, 1024)
    scale = XA_HEAD_DIM ** -0.5

    def body(q_ref, kv_ref, o_ref):
        q = q_ref[0]
        k = kv_ref[0].astype(BF16)
        v = kv_ref[1]
        o = jnp.zeros((ts, D_XA), F32)
        for m in _head_masks():
            s = _dot(jnp.where(m, q, 0.0).astype(BF16), k, NT) * scale
            s = s - jnp.max(s, axis=-1, keepdims=True)
            e = jnp.exp(s)
            p = e / jnp.sum(e, axis=-1, keepdims=True)
            o = o + _dot(p.astype(BF16), jnp.where(m, v, 0.0).astype(BF16), NN)
        o_ref[0] = o.astype(BF16)

    return pl.pallas_call(
        body, name="xattn_fwd", grid=(S // ts,),
        in_specs=[pl.BlockSpec((1, ts, GROUP), lambda s: (qg, s, 0)), pl.BlockSpec((2, N_MEM, GROUP), lambda s: (0, 0, 0))],
        out_specs=pl.BlockSpec((1, ts, GROUP), lambda s: (0, s, 0)),
        out_shape=SDS((1, S, GROUP), BF16),
        compiler_params=_params(("parallel",)),
    )(u, kv)


def xattn_bwd(u, qg, kv, dm, dg, du=None, dgate=None):
    _, S, _ = u.shape
    ts = _tile(S, 1024)
    scale = XA_HEAD_DIM ** -0.5

    def body(q_ref, kv_ref, do_ref, *refs):
        dq_ref, dkv_ref = refs[-2:]

        @pl.when(pl.program_id(0) == 0)
        def _():
            dkv_ref[...] = jnp.zeros_like(dkv_ref)

        q = q_ref[0]
        k = kv_ref[0]
        v = kv_ref[1]
        kb = k.astype(BF16)
        do = do_ref[0]
        dq = jnp.zeros((ts, D_XA), F32)
        dk = jnp.zeros((N_MEM, D_XA), F32)
        dv = jnp.zeros((N_MEM, D_XA), F32)
        for m in _head_masks():
            qm = jnp.where(m, q, 0.0).astype(BF16)
            s = _dot(qm, kb, NT) * scale
            s = s - jnp.max(s, axis=-1, keepdims=True)
            e = jnp.exp(s)
            p = e / jnp.sum(e, axis=-1, keepdims=True)
            dom = jnp.where(m, do, 0.0).astype(BF16)
            dp = _dot(dom, jnp.where(m, v, 0.0).astype(BF16), NT)
            ds = (p * (dp - jnp.sum(dp * p, axis=-1, keepdims=True)) * scale).astype(BF16)
            dq = dq + _dot(ds, jnp.where(m, k, 0.0).astype(BF16), NN)
            dk = dk + _dot(ds, qm, TN)
            dv = dv + _dot(p.astype(BF16), dom, TN)
        dq_ref[0] = dq.astype(BF16)
        if du is not None:
            dq_ref[1] = refs[0][0]
        dkv_ref[0] += dk
        dkv_ref[1] += dv

    in_specs = [pl.BlockSpec((1, ts, GROUP), lambda s: (qg, s, 0)), pl.BlockSpec((2, N_MEM, GROUP), lambda s: (0, 0, 0)),
                pl.BlockSpec((1, ts, GROUP), lambda s: (dg, s, 0))]
    args, aliases = [u, kv, dm], {}
    dq_spec, dq_shape = pl.BlockSpec((1, ts, GROUP), lambda s: (0, s, 0)), SDS((1, S, GROUP), BF16)
    if du is not None:
        in_specs += [pl.BlockSpec((1, ts, GROUP), lambda s: (0, s, 0)), pl.BlockSpec(memory_space=pl.ANY)]
        args += [dgate, du]
        aliases = {4: 0}
        dq_spec, dq_shape = pl.BlockSpec((2, ts, GROUP), lambda s: (qg // 2, s, 0)), SDS(du.shape, BF16)
    return pl.pallas_call(
        body, name="xattn_bwd", grid=(S // ts,),
        in_specs=in_specs,
        out_specs=[dq_spec, pl.BlockSpec((2, N_MEM, GROUP), lambda s: (0, 0, 0))],
        out_shape=[dq_shape, SDS((2, N_MEM, GROUP), F32)],
        input_output_aliases=aliases,
        compiler_params=_params(("arbitrary",)),
    )(*args)


ML_BLOCK_CHUNKS = 4
H4 = ML_HEADS
L = ML_CHUNK
NLANE = ML_HEAD_DIM


def _chunk_consts():
    r = lax.broadcasted_iota(jnp.int32, (1, L, L), 1)
    c = lax.broadcasted_iota(jnp.int32, (1, L, L), 2)
    return r >= c, r <= c, r == c


def _gate_cols(gb):
    lane = lax.broadcasted_iota(jnp.int32, gb.shape, 1)
    li = jnp.stack([jnp.sum(jnp.where(lane == h, gb, 0.0), axis=1, keepdims=True) for h in range(H4)])
    gf = jnp.stack([jnp.sum(jnp.where(lane == H4 + h, gb, 0.0), axis=1, keepdims=True) for h in range(H4)])
    return li, gf


def _log_sigmoid(x):
    return jnp.minimum(x, 0.0) - jnp.log(1.0 + jnp.exp(-jnp.abs(x)))


def _chunk_forward(q, k, v_aug, li_col, lf_col, c_prev, m_prev):
    tri, tri_t, eye = _chunk_consts()
    lf_row = jnp.sum(jnp.where(eye, lf_col, 0.0), axis=1, keepdims=True)
    li_row = jnp.sum(jnp.where(eye, li_col, 0.0), axis=1, keepdims=True)
    bcum_col = jnp.sum(jnp.where(tri, lf_row, 0.0), axis=2, keepdims=True)
    bcum_row = jnp.sum(jnp.where(tri_t, lf_col, 0.0), axis=1, keepdims=True)
    log_d = jnp.where(tri, bcum_col - bcum_row + li_row, NEG)
    log_inter = bcum_col + m_prev
    m_t = jnp.maximum(log_inter, jnp.max(log_d, axis=2, keepdims=True))
    w_intra = jnp.exp(log_d - m_t)
    w_inter = jnp.exp(log_inter - m_t)
    sc = _bdot(q, k, 2, 2) * w_intra
    qc = _bdot1(q, c_prev, 2, 1)
    num = _bdot(sc, v_aug, 2, 1) + w_inter * qc
    lane = lax.broadcasted_iota(jnp.int32, num.shape, 2)
    den = jnp.sum(jnp.where(lane == NLANE, num, 0.0), axis=2, keepdims=True)
    e_m = jnp.exp(-m_t)
    b_last = jnp.sum(lf_row, axis=2, keepdims=True)
    log_w = b_last - bcum_col + li_col
    m_new = jnp.maximum(b_last + m_prev, jnp.max(log_w, axis=1, keepdims=True))
    w_k = jnp.exp(log_w - m_new)
    decay = jnp.exp(b_last + m_prev - m_new)
    return dict(w_intra=w_intra, w_inter=w_inter, sc=sc, qc=qc, num=num, den=den, e_m=e_m, lane=lane,
                w_k=w_k, decay=decay, m_new=m_new)


def mlstm_fwd(qk, u, bg):
    _, S, _ = qk.shape
    nc = S // L
    cb = min(ML_BLOCK_CHUNKS, nc)
    rows = cb * L
    kscale = ML_HEAD_DIM ** -0.5

    def body(qk_ref, v_ref, g_ref, bg_ref, h_ref, cst_ref, mst_ref, c_sc, m_sc):
        @pl.when(pl.program_id(0) == 0)
        def _():
            c_sc[...] = jnp.zeros_like(c_sc)
            m_sc[...] = jnp.zeros_like(m_sc)

        for c in range(cb):
            sl = pl.ds(c * L, L)
            q = qk_ref[0:H4, sl, :]
            k = qk_ref[H4:2 * H4, sl, :] * kscale
            v = v_ref[:, sl, :]
            lane = lax.broadcasted_iota(jnp.int32, v.shape, 2)
            v_aug = jnp.where(lane == NLANE, 1.0, v)
            li_col, gf = _gate_cols(g_ref[0, sl, :] + bg_ref[...])
            lf_col = _log_sigmoid(gf)
            c_prev = c_sc[...]
            m_prev = m_sc[...]
            f = _chunk_forward(q, k, v_aug, li_col, lf_col, c_prev, m_prev)
            r = 1.0 / jnp.maximum(jnp.abs(f["den"]), f["e_m"])
            h_ref[:, sl, :] = jnp.where(lane < NLANE, f["num"] * r, 0.0)
            cst_ref[c] = c_prev
            mst_ref[c] = jnp.broadcast_to(m_prev, (H4, 1, LANES))
            c_sc[...] = f["decay"] * c_prev + _bdot(k * f["w_k"], v_aug, 1, 1)
            m_sc[...] = f["m_new"]

    def hspec(blk):
        return pl.BlockSpec((H4, rows, GROUP), lambda i: (blk, i, 0))

    return pl.pallas_call(
        body, name="mlstm_fwd", grid=(nc // cb,),
        in_specs=[pl.BlockSpec((2 * H4, rows, GROUP), lambda i: (0, i, 0)), hspec(2),
                  pl.BlockSpec((1, rows, GROUP), lambda i: (17, i, 0)), pl.BlockSpec((1, GROUP), lambda i: (0, 0))],
        out_specs=[hspec(0), pl.BlockSpec((cb, H4, GROUP, GROUP), lambda i: (i, 0, 0, 0)),
                   pl.BlockSpec((cb, H4, 1, LANES), lambda i: (i, 0, 0, 0))],
        out_shape=[SDS((H4, S, GROUP), F32), SDS((nc, H4, GROUP, GROUP), F32), SDS((nc, H4, 1, LANES), F32)],
        scratch_shapes=[pltpu.VMEM((H4, GROUP, GROUP), F32), pltpu.VMEM((H4, 1, 1), F32)],
        compiler_params=_params(("arbitrary",)),
    )(qk, u, u, bg)


def mlstm_bwd(qk, u, bg, cst, mst, dh, du):
    _, S, _ = qk.shape
    nc = S // L
    cb = min(ML_BLOCK_CHUNKS, nc)
    rows = cb * L
    nb = nc // cb
    kscale = ML_HEAD_DIM ** -0.5

    def body(qk_ref, v_ref, g_ref, bg_ref, cst_ref, mst_ref, dh_ref, du_in_ref, dqk_ref, dv_ref, dg_ref, dbg_ref, dc_sc):
        @pl.when(pl.program_id(0) == 0)
        def _():
            dc_sc[...] = jnp.zeros_like(dc_sc)
            dbg_ref[...] = jnp.zeros_like(dbg_ref)

        tri, tri_t, eye = _chunk_consts()
        for c in reversed(range(cb)):
            sl = pl.ds(c * L, L)
            q = qk_ref[0:H4, sl, :]
            k = qk_ref[H4:2 * H4, sl, :] * kscale
            v = v_ref[:, sl, :]
            lane = lax.broadcasted_iota(jnp.int32, v.shape, 2)
            v_aug = jnp.where(lane == NLANE, 1.0, v)
            li_col, gf = _gate_cols(g_ref[0, sl, :] + bg_ref[...])
            lf_col = _log_sigmoid(gf)
            c_prev = cst_ref[c]
            m_prev = mst_ref[c][:, :, 0:1]
            f = _chunk_forward(q, k, v_aug, li_col, lf_col, c_prev, m_prev)
            w_intra, w_inter, sc, num, den, e_m = f["w_intra"], f["w_inter"], f["sc"], f["num"], f["den"], f["e_m"]
            absd = jnp.abs(den)
            r = 1.0 / jnp.maximum(absd, e_m)
            dhv = dh_ref[:, sl, :]
            s1 = jnp.sum(jnp.where(lane < NLANE, dhv * num, 0.0), axis=2, keepdims=True)
            dden = jnp.where(absd > e_m, -s1 * r * r * jnp.sign(den), 0.0)
            dnum = jnp.where(lane == NLANE, dden, jnp.where(lane < NLANE, dhv * r, 0.0))
            dsc = _bdot1(dnum, v_aug, 2, 2)
            dv = _bdot1(sc, dnum, 1, 1)
            gmat = dsc * sc
            dqk = dsc * w_intra
            dq = _bdot1(dqk, k, 2, 1) + w_inter * _bdot1(dnum, c_prev, 2, 2)
            dk = _bdot1(dqk, q, 1, 1)
            dc_prev = _bdot(q * w_inter, dnum, 1, 1)
            dlog_inter = jnp.sum(dnum * f["qc"], axis=2, keepdims=True) * w_inter
            dbcum_col = dlog_inter + jnp.sum(gmat, axis=2, keepdims=True)
            g_row = jnp.sum(gmat, axis=1, keepdims=True)
            dcn = dc_sc[...]
            w_k, decay = f["w_k"], f["decay"]
            kw = k * w_k
            dc_prev = dc_prev + decay * dcn
            db_last = jnp.sum(jnp.sum(dcn * c_prev, axis=2, keepdims=True), axis=1, keepdims=True) * decay
            dkw = _bdot(v_aug, dcn, 2, 2)
            dv = dv + _bdot1(kw, dcn, 2, 1)
            dk = dk + dkw * w_k
            dlogw = jnp.sum(dkw * k, axis=2, keepdims=True) * w_k
            db_last = db_last + jnp.sum(dlogw, axis=1, keepdims=True)
            dbcum_col = dbcum_col - dlogw
            rowi = lax.broadcasted_iota(jnp.int32, (1, L, 1), 1)
            dbcum_col = dbcum_col + jnp.where(rowi == L - 1, db_last, 0.0)
            dbcum_row = jnp.sum(jnp.where(eye, dbcum_col, 0.0), axis=1, keepdims=True) - g_row
            dlf_col = jnp.sum(jnp.where(tri_t, dbcum_row, 0.0), axis=2, keepdims=True)
            dli_col = dlogw + jnp.sum(jnp.where(eye, g_row, 0.0), axis=2, keepdims=True)
            dgf_col = dlf_col * _sigmoid(-gf)
            lane_g = lax.broadcasted_iota(jnp.int32, (L, GROUP), 1)
            dg = jnp.zeros((L, GROUP), F32)
            for h in range(H4):
                dg = dg + jnp.where(lane_g == h, dli_col[h], 0.0) + jnp.where(lane_g == H4 + h, dgf_col[h], 0.0)
            dqk_ref[0:H4, sl, :] = dq
            dqk_ref[H4:2 * H4, sl, :] = dk * kscale
            dv_ref[:, sl, :] = jnp.where(lane < NLANE, dv, 0.0).astype(BF16)
            dg_ref[0, sl, :] = dg.astype(BF16)
            dbg_ref[...] += jnp.sum(dg, axis=0, keepdims=True)
            dc_sc[...] = dc_prev

    def hspec(blk):
        return pl.BlockSpec((H4, rows, GROUP), lambda i: (blk, nb - 1 - i, 0))

    gspec = pl.BlockSpec((1, rows, GROUP), lambda i: (17, nb - 1 - i, 0))
    qkspec = pl.BlockSpec((2 * H4, rows, GROUP), lambda i: (0, nb - 1 - i, 0))
    return pl.pallas_call(
        body, name="mlstm_bwd", grid=(nb,),
        in_specs=[qkspec, hspec(2), gspec, pl.BlockSpec((1, GROUP), lambda i: (0, 0)),
                  pl.BlockSpec((cb, H4, GROUP, GROUP), lambda i: (nb - 1 - i, 0, 0, 0)),
                  pl.BlockSpec((cb, H4, 1, LANES), lambda i: (nb - 1 - i, 0, 0, 0)), hspec(0), pl.BlockSpec(memory_space=pl.ANY)],
        out_specs=[qkspec, hspec(2), pl.BlockSpec((1, rows, GROUP), lambda i: (0, nb - 1 - i, 0)),
                   pl.BlockSpec((1, GROUP), lambda i: (0, 0))],
        input_output_aliases={7: 1},
        out_shape=[SDS((2 * H4, S, GROUP), F32), SDS(du.shape, BF16),
                   SDS((1, S, GROUP), BF16), SDS((1, GROUP), F32)],
        scratch_shapes=[pltpu.VMEM((H4, GROUP, GROUP), F32)],
        compiler_params=_params(("arbitrary",)),
    )(qk, u, u, bg, cst, mst, dh, du)


def head_norm_fwd(hm, u, hg):
    _, S, _ = hm.shape
    ts = _tile(S, 2048)

    def body(h_ref, o_ref, g_ref, t_ref):
        h = h_ref[0]
        lane = lax.broadcasted_iota(jnp.int32, h.shape, 1)
        valid = lane < ML_HEAD_DIM
        mu = jnp.sum(h, axis=-1, keepdims=True) * (1.0 / ML_HEAD_DIM)
        hc = jnp.where(valid, h - mu, 0.0)
        var = jnp.sum(hc * hc, axis=-1, keepdims=True) * (1.0 / ML_HEAD_DIM)
        hn = hc * lax.rsqrt(var + LN_EPS) * g_ref[0]
        t_ref[0] = (_sigmoid(o_ref[0]) * hn).astype(BF16)

    return pl.pallas_call(
        body, name="head_norm_fwd", grid=(H4, S // ts),
        in_specs=[pl.BlockSpec((1, ts, GROUP), lambda h, s: (h, s, 0)), pl.BlockSpec((1, ts, GROUP), lambda h, s: (12 + h, s, 0)),
                  pl.BlockSpec((1, 1, GROUP), lambda h, s: (h, 0, 0))],
        out_specs=pl.BlockSpec((1, ts, GROUP), lambda h, s: (h, s, 0)),
        out_shape=SDS((H4, S, GROUP), BF16),
        compiler_params=_params(("parallel", "parallel")),
    )(hm, u, hg)


def head_norm_bwd(hm, u, hg, dm):
    _, S, _ = hm.shape
    ts = _tile(S, 2048)

    def body(h_ref, o_ref, g_ref, d_ref, dh_ref, do_ref, dg_ref):
        @pl.when(pl.program_id(1) == 0)
        def _():
            dg_ref[...] = jnp.zeros_like(dg_ref)

        h = h_ref[0]
        lane = lax.broadcasted_iota(jnp.int32, h.shape, 1)
        valid = lane < ML_HEAD_DIM
        inv = 1.0 / ML_HEAD_DIM
        mu = jnp.sum(h, axis=-1, keepdims=True) * inv
        hc = jnp.where(valid, h - mu, 0.0)
        var = jnp.sum(hc * hc, axis=-1, keepdims=True) * inv
        rstd = lax.rsqrt(var + LN_EPS)
        xhat = hc * rstd
        g = g_ref[0]
        sig = _sigmoid(o_ref[0])
        dt = jnp.where(valid, d_ref[0], 0.0)
        do_ref[0] = (dt * xhat * g * sig * (1.0 - sig)).astype(BF16)
        dhn = dt * sig
        dg_ref[0] += jnp.sum(dhn * xhat, axis=0, keepdims=True)
        dxh = dhn * g
        m1 = jnp.sum(dxh, axis=-1, keepdims=True) * inv
        m2 = jnp.sum(dxh * xhat, axis=-1, keepdims=True) * inv
        dh_ref[0] = jnp.where(valid, rstd * (dxh - m1 - xhat * m2), 0.0)

    spec = pl.BlockSpec((1, ts, GROUP), lambda h, s: (h, s, 0))
    gspec = pl.BlockSpec((1, 1, GROUP), lambda h, s: (h, 0, 0))
    return pl.pallas_call(
        body, name="head_norm_bwd", grid=(H4, S // ts),
        in_specs=[spec, pl.BlockSpec((1, ts, GROUP), lambda h, s: (12 + h, s, 0)), gspec, spec],
        out_specs=[spec, pl.BlockSpec((1, ts, GROUP), lambda h, s: (12 + h, s, 0)), gspec],
        out_shape=[SDS((H4, S, GROUP), F32), SDS((u.shape[0], S, GROUP), BF16), SDS((H4, 1, GROUP), F32)],
        compiler_params=_params(("parallel", "arbitrary")),
    )(hm, u, hg, dm)


def _adamw_math(w, g, m, v):
    c1 = 1.0 / (1.0 - ADAM_B1 ** ADAM_STEP)
    c2 = 1.0 / (1.0 - ADAM_B2 ** ADAM_STEP)
    nm = ADAM_B1 * m + (1.0 - ADAM_B1) * g
    nv = ADAM_B2 * v + (1.0 - ADAM_B2) * (g * g)
    return -ADAM_LR * ((nm * c1) / (jnp.sqrt(nv * c2) + ADAM_EPS) + ADAM_WD * w), nm, nv


def _row_tile(R, cap=512):
    return R if R <= cap else max(d for d in range(8, cap + 1, 8) if R % d == 0)


def adamw_into(w, m, v, g, outs, idx, after, name):
    R, C = g.shape
    tr = _row_tile(R)
    lead = (0,) * len(idx)

    def body(w_ref, m_ref, v_ref, g_ref, *rest):
        go_ref, d_ref, nm_ref, nv_ref, token = rest[-5:]
        token[...] = jnp.zeros_like(token)
        gv = g_ref[...]
        d, nm, nv = _adamw_math(w_ref[lead], gv, m_ref[lead], v_ref[lead])
        go_ref[lead] = gv
        d_ref[lead] = d
        nm_ref[lead] = nm
        nv_ref[lead] = nv

    blk = pl.BlockSpec((1,) * len(idx) + (tr, C), lambda r: idx + (r, 0))
    any_space = pl.BlockSpec(memory_space=pl.ANY)
    in_specs, args, aliases = [blk, blk, blk, pl.BlockSpec((tr, C), lambda r: (r, 0)), any_space], [w, m, v, g, g if after is None else after], {}
    if outs is not None:
        in_specs += [any_space] * 4
        args += list(outs)
        aliases = {5 + i: i for i in range(4)}
    out = pl.pallas_call(
        body, name=name, grid=(R // tr,),
        in_specs=in_specs, out_specs=[blk] * 4 + [pl.BlockSpec((8, LANES), lambda r: (0, 0))],
        out_shape=[SDS(w.shape, F32)] * 4 + [SDS((8, LANES), F32)],
        input_output_aliases=aliases, compiler_params=_params(("arbitrary",)),
    )(*args)
    return out[:4], out[4]


def adamw(w, g, m, v, name):
    R, C = w.shape
    tr = _row_tile(R)

    def body(w_ref, g_ref, m_ref, v_ref, d_ref, nm_ref, nv_ref):
        d_ref[...], nm_ref[...], nv_ref[...] = _adamw_math(w_ref[...], g_ref[...], m_ref[...], v_ref[...])

    spec = pl.BlockSpec((tr, C), lambda i: (i, 0))
    return pl.pallas_call(
        body, name=name, grid=(R // tr,),
        in_specs=[spec] * 4, out_specs=[spec] * 3,
        out_shape=[SDS((R, C), F32)] * 3,
        compiler_params=_params(("parallel",)),
    )(w, g, m, v)


HBM = pl.BlockSpec(memory_space=pl.ANY)
ROW_SPLIT = 4
PAIR_SPLIT = 1


def _position():
    x, y, c = lax.axis_index("x"), lax.axis_index("y"), lax.axis_index("c")
    return x, y, c, [(1 - x, y), (x, 1 - y), (1 - x, 1 - y)]


def _unique(items):
    arrays = []
    for a, _ in items:
        if not any(a is b for b in arrays):
            arrays.append(a)
    return arrays, [next(i for i, b in enumerate(arrays) if b is a) for a, _ in items]


def place_own(items, me, after, name):
    arrays, src_of = _unique(items)
    n = len(items)
    shapes = [a.shape[len(p):] for a, p in items]

    def body(me_ref, *refs):
        for t in range(n):
            refs[n + 1 + t][0] = refs[t][(0,) * len(items[t][1])]

    in_specs, out_specs = [], []
    for (a, p), shp in zip(items, shapes):
        blk = shp[:-2] + (shp[-2] // ROW_SPLIT, shp[-1])
        lead = (0,) * (len(shp) - 2)
        in_specs.append(pl.BlockSpec((1,) * len(p) + blk, functools.partial(lambda r, me_ref, p, lead: p + lead + (r, 0), p=p, lead=lead)))
        out_specs.append(pl.BlockSpec((1,) + blk, functools.partial(lambda r, me_ref, lead: (me_ref[0],) + lead + (r, 0), lead=lead)))
    in_specs.append(pl.BlockSpec(memory_space=pl.ANY))
    return pl.pallas_call(
        body, name=name,
        grid_spec=pltpu.PrefetchScalarGridSpec(num_scalar_prefetch=1, grid=(ROW_SPLIT,), in_specs=in_specs, out_specs=out_specs),
        out_shape=[SDS((N_CHIPS,) + tuple(shp), a.dtype) for shp, (a, _) in zip(shapes, items)],
        compiler_params=_params(("parallel",)),
    )(me, *[arrays[i] for i in src_of], after)


SEM = pl.BlockSpec(memory_space=pltpu.SEMAPHORE)
IN_HBM = pl.BlockSpec(memory_space=pltpu.HBM)
DATAFLOW = pltpu.SideEffectType.DATAFLOW_SIDE_EFFECTING


def split_start(bufs, plan, n_copies, after, name):
    n = len(bufs)

    def body(*refs):
        send, recv, token = refs[n + 1], refs[n + 2], refs[-1]
        x, y, c, chips = _position()
        for k, (src, dst, dev) in enumerate(plan(refs[:n], x, y, c, chips)):
            pltpu.make_async_remote_copy(src_ref=src, dst_ref=dst, send_sem=send.at[k], recv_sem=recv.at[k],
                                         device_id=dev, device_id_type=MESH).start()
        token[...] = jnp.zeros_like(token)

    out = pl.pallas_call(
        body, name=name,
        out_shape=(pltpu.SemaphoreType.DMA((n_copies,)), pltpu.SemaphoreType.DMA((n_copies,)),
                   *[pltpu.HBM(b.shape, b.dtype) for b in bufs], SDS((8, LANES), F32)),
        in_specs=[IN_HBM] * n + [pl.BlockSpec(memory_space=pl.ANY)],
        out_specs=(SEM, SEM, *[IN_HBM] * n, pl.BlockSpec(memory_space=pltpu.VMEM)),
        input_output_aliases={i: 2 + i for i in range(n)},
        compiler_params=pltpu.CompilerParams(has_side_effects=DATAFLOW),
    )(*[pltpu.with_memory_space_constraint(b, pltpu.HBM) for b in bufs], after)
    return out[0], out[1], list(out[2:2 + n]), out[-1]


def split_wait(send, recv, bufs, plan, after, name):
    n = len(bufs)

    def body(*refs):
        send_ref, recv_ref = refs[n], refs[n + 1]
        x, y, c, chips = _position()
        for k, (src, dst, dev) in enumerate(plan(refs[:n], x, y, c, chips)):
            cp = pltpu.make_async_remote_copy(src_ref=src, dst_ref=dst, send_sem=send_ref.at[k], recv_sem=recv_ref.at[k],
                                              device_id=dev, device_id_type=MESH)
            cp.wait_send()
            cp.wait_recv()

    return list(pl.pallas_call(
        body, name=name, out_shape=tuple(pltpu.HBM(b.shape, b.dtype) for b in bufs),
        in_specs=[IN_HBM] * n + [SEM, SEM, pl.BlockSpec(memory_space=pl.ANY)], out_specs=tuple([IN_HBM] * n),
        input_output_aliases={i: i for i in range(n)},
        compiler_params=pltpu.CompilerParams(has_side_effects=DATAFLOW),
    )(*bufs, send, recv, after))


def _gather_plan(shapes, landing):
    n = len(shapes)

    def plan(refs, x, y, c, chips):
        out = []
        for t in range(n):
            half = shapes[t][0] // 2
            rows = pl.ds(c * half, half)
            for cx, cy in chips:
                slot = 2 * cx + cy if landing else 2 * x + y
                out.append((refs[t].at[rows], refs[n + t].at[slot, rows], (cx, cy, c)))
        return out

    return plan


def gather_start(shards, placed, after, name):
    shapes = [s.shape for s in shards]
    send, recv, bufs, token = split_start(list(shards) + list(placed), _gather_plan(shapes, False), 3 * len(shards), after, name)
    return (send, recv, bufs, shapes), token


def gather_wait(state, after, name):
    send, recv, bufs, shapes = state
    return split_wait(send, recv, bufs, _gather_plan(shapes, True), after, name)[len(shapes):]


def _pass_on_plan(shapes, landing):
    n = len(shapes)

    def plan(refs, x, y, c, chips):
        out = []
        for t in range(n):
            half = shapes[t][0] // 2
            for cx, cy in chips:
                mine = refs[t].at[2 * cx + cy, pl.ds(c * half, half)]
                theirs = refs[t].at[2 * cx + cy, pl.ds((1 - c) * half, half)]
                out.append((mine, theirs if landing else mine, (x, y, 1 - c)))
        return out

    return plan


def gather_pass_on(placed, shapes, name):
    n = len(placed)

    def body(*refs):
        outs, send, recv = refs[n:2 * n], refs[2 * n], refs[2 * n + 1]
        x, y, c, chips = _position()
        cps = []
        for t in range(n):
            half = shapes[t][0] // 2
            for j, (cx, cy) in enumerate(chips):
                piece = outs[t].at[2 * cx + cy, pl.ds(c * half, half)]
                cp = pltpu.make_async_remote_copy(src_ref=piece, dst_ref=piece, send_sem=send.at[3 * t + j], recv_sem=recv.at[3 * t + j],
                                                  device_id=(x, y, 1 - c), device_id_type=MESH)
                cp.start()
                cps.append(cp)
        for t in range(n):
            half = shapes[t][0] // 2
            for j, (cx, cy) in enumerate(chips):
                piece = outs[t].at[2 * cx + cy, pl.ds((1 - c) * half, half)]
                pltpu.make_async_remote_copy(src_ref=piece, dst_ref=piece, send_sem=send.at[3 * t + j], recv_sem=recv.at[3 * t + j],
                                             device_id=(x, y, 1 - c), device_id_type=MESH).wait_recv()
        for cp in cps:
            cp.wait_send()

    return pl.pallas_call(
        body, name=name,
        in_specs=[HBM] * n, out_specs=[HBM] * n,
        out_shape=[SDS(p.shape, p.dtype) for p in placed],
        input_output_aliases={t: t for t in range(n)},
        scratch_shapes=[pltpu.SemaphoreType.DMA((3 * n,))] * 2,
    )(*placed)


def _flip(k, x, y, c):
    return ((1 - x) if k & 4 else x, (1 - y) if k & 2 else y, (1 - c) if k & 1 else c)


def small_allgather(v, reduce):
    R, C = v.shape

    def body(v_ref, o_ref, *scratch):
        if reduce:
            buf, send, recv = scratch
        else:
            buf, (send, recv) = o_ref, scratch
        x, y, c, _ = _position()
        me = 4 * x + 2 * y + c
        buf[me] = v_ref[...]
        sends = []
        for k in range(1, N_DEV):
            cp = pltpu.make_async_remote_copy(src_ref=v_ref, dst_ref=buf.at[me], send_sem=send.at[k - 1], recv_sem=recv.at[k - 1],
                                              device_id=_flip(k, x, y, c), device_id_type=MESH)
            cp.start()
            sends.append(cp)
        for k in range(1, N_DEV):
            px, py, pc = _flip(k, x, y, c)
            pltpu.make_async_remote_copy(src_ref=v_ref, dst_ref=buf.at[4 * px + 2 * py + pc], send_sem=send.at[k - 1],
                                         recv_sem=recv.at[k - 1], device_id=(px, py, pc), device_id_type=MESH).wait_recv()
        for cp in sends:
            cp.wait_send()
        if reduce:
            acc = buf[0]
            for i in range(1, N_DEV):
                acc = acc + buf[i]
            o_ref[...] = acc

    vm = pl.BlockSpec(memory_space=pltpu.VMEM)
    sems = [pltpu.SemaphoreType.DMA((N_DEV - 1,)), pltpu.SemaphoreType.DMA((N_DEV - 1,))]
    return pl.pallas_call(
        body, name="small_allreduce" if reduce else "small_allgather",
        in_specs=[vm], out_specs=vm,
        out_shape=SDS((R, C) if reduce else (N_DEV, R, C), F32),
        scratch_shapes=([pltpu.VMEM((N_DEV, R, C), F32)] if reduce else []) + sems,
    )(v)


def rs_exchange_sibling(gs):
    n = len(gs)

    def body(*refs):
        ins, outs, send, recv = refs[:n], refs[n:2 * n], refs[2 * n], refs[2 * n + 1]
        x, y, c, _ = _position()
        cps = []
        for t in range(n):
            cp = pltpu.make_async_remote_copy(src_ref=ins[t].at[:, 1 - c], dst_ref=outs[t], send_sem=send.at[t], recv_sem=recv.at[t],
                                              device_id=(x, y, 1 - c), device_id_type=MESH)
            cp.start()
            cps.append(cp)
        for cp in cps:
            cp.wait()

    return pl.pallas_call(
        body, name="rs_exchange_sibling", in_specs=[HBM] * n, out_specs=[HBM] * n,
        out_shape=[SDS((g.shape[0],) + g.shape[2:], g.dtype) for g in gs],
        scratch_shapes=[pltpu.SemaphoreType.DMA((n,)), pltpu.SemaphoreType.DMA((n,))],
    )(*gs)


def rs_pair_add(gs, rs, c):
    n = len(gs)

    def body(c_ref, *refs):
        for t in range(n):
            refs[2 * n + t][0] = (refs[t][0, 0].astype(F32) + refs[n + t][0].astype(F32)).astype(BF16)

    in_specs, out_specs, out_shape = [], [], []
    for g in gs:
        _, _, h, C = g.shape
        in_specs.append(pl.BlockSpec((1, 1, h // PAIR_SPLIT, C), lambda j, r, c_ref: (j, c_ref[0], r, 0)))
    for g in gs:
        _, _, h, C = g.shape
        spec = pl.BlockSpec((1, h // PAIR_SPLIT, C), lambda j, r, c_ref: (j, r, 0))
        in_specs.append(spec)
        out_specs.append(spec)
        out_shape.append(SDS((N_CHIPS, h, C), BF16))
    return pl.pallas_call(
        body, name="rs_pair_add",
        grid_spec=pltpu.PrefetchScalarGridSpec(num_scalar_prefetch=1, grid=(N_CHIPS, PAIR_SPLIT), in_specs=in_specs, out_specs=out_specs),
        out_shape=out_shape, compiler_params=_params(("parallel", "parallel")),
    )(c, *gs, *rs)


def _rs_plan(n):
    def plan(refs, x, y, c, chips):
        return [(refs[t].at[2 * cx + cy], refs[n + t].at[j], (cx, cy, c)) for t in range(n) for j, (cx, cy) in enumerate(chips)]

    return plan


def rs_chip_add(ps, qs, me_c):
    n = len(ps)

    def body(me_ref, *refs):
        for t in range(n):
            q = refs[n + t]
            refs[2 * n + t][0] = ((refs[t][0].astype(F32) + q[0].astype(F32)) + q[1].astype(F32)) + q[2].astype(F32)

    in_specs, out_specs, out_shape = [], [], []
    for p in ps:
        _, h, C = p.shape
        in_specs.append(pl.BlockSpec((1, h // ROW_SPLIT, C), lambda r, me_ref: (me_ref[0], r, 0)))
    for p in ps:
        _, h, C = p.shape
        in_specs.append(pl.BlockSpec((3, h // ROW_SPLIT, C), lambda r, me_ref: (0, r, 0)))
        out_specs.append(pl.BlockSpec((1, h // ROW_SPLIT, C), lambda r, me_ref: (me_ref[1], r, 0)))
        out_shape.append(SDS((2, h, C), F32))
    return pl.pallas_call(
        body, name="rs_chip_add",
        grid_spec=pltpu.PrefetchScalarGridSpec(num_scalar_prefetch=1, grid=(ROW_SPLIT,), in_specs=in_specs, out_specs=out_specs),
        out_shape=out_shape, compiler_params=_params(("parallel",)),
    )(me_c, *ps, *qs)


def rs_share(rs):
    n = len(rs)

    def body(*refs):
        outs, send, recv = refs[n:2 * n], refs[2 * n], refs[2 * n + 1]
        x, y, c, _ = _position()
        cps = []
        for t in range(n):
            cp = pltpu.make_async_remote_copy(src_ref=outs[t].at[c], dst_ref=outs[t].at[c], send_sem=send.at[t], recv_sem=recv.at[t],
                                              device_id=(x, y, 1 - c), device_id_type=MESH)
            cp.start()
            cps.append(cp)
        for cp in cps:
            cp.wait()

    return pl.pallas_call(
        body, name="rs_share", in_specs=[HBM] * n, out_specs=[HBM] * n,
        out_shape=[SDS(r.shape, r.dtype) for r in rs],
        input_output_aliases={t: t for t in range(n)},
        scratch_shapes=[pltpu.SemaphoreType.DMA((n,))] * 2,
    )(*rs)


def rs_begin(gs, after, name):
    c = lax.axis_index("c")
    n = len(gs)
    g5 = [g.reshape(N_CHIPS, 2, g.shape[1] // 2, g.shape[2]) for g in gs]
    from_sibling = rs_exchange_sibling(g5)
    pair = rs_pair_add(g5, from_sibling, jnp.reshape(c, (1,)).astype(jnp.int32))
    lands = [lax.empty((3,) + p.shape[1:], p.dtype) for p in pair]
    send, recv, bufs, token = split_start(list(pair) + lands, _rs_plan(n), 3 * n, from_sibling[0] if after is None else after, name)
    return (send, recv, bufs, [g.shape for g in gs]), token


def rs_end(state, after, name):
    x, y, c = lax.axis_index("x"), lax.axis_index("y"), lax.axis_index("c")
    send, recv, bufs, shapes = state
    n = len(shapes)
    bufs = split_wait(send, recv, bufs, _rs_plan(n), after, name)
    half = rs_chip_add(bufs[:n], bufs[n:], jnp.stack([2 * x + y, c]).astype(jnp.int32))
    both = rs_share(half)
    return [b.reshape(s[1], s[2]) for b, s in zip(both, shapes)]


def _pad_last(a, n):
    return jnp.pad(a, [(0, 0)] * (a.ndim - 1) + [(0, n - a.shape[-1])])


def _heads_to_groups(w):
    k = w.shape[0]
    return _pad_last(w.reshape(k, ML_HEADS, ML_HEAD_DIM).transpose(1, 0, 2), GROUP)


def _groups_to_heads(g):
    return g[:, :, :ML_HEAD_DIM].transpose(1, 0, 2).reshape(g.shape[1], D_TOK)


def _cols_to_groups(w):
    k, n = w.shape
    return w.reshape(k, n // GROUP, GROUP).transpose(1, 0, 2)


def _groups_to_cols(g):
    n, k, _ = g.shape
    return g.transpose(1, 0, 2).reshape(k, n * GROUP)


def _chips_to_cols(a):
    return a.transpose(1, 0, 2).reshape(a.shape[1], -1)


def _cols_to_chips(w):
    k, n = w.shape
    return w.reshape(k, N_CHIPS, n // N_CHIPS).transpose(1, 0, 2)


def _mlstm_in_groups(w):
    parts = [_heads_to_groups(w[:, i * D_TOK:(i + 1) * D_TOK]) for i in range(4)]
    gates = _pad_last(w[:, 4 * D_TOK:4 * D_TOK + 2 * ML_HEADS], GROUP)[None]
    qmem = w[:, 4 * D_TOK + 2 * ML_HEADS:][None]
    return jnp.concatenate(parts + [qmem, gates], axis=0)


def _mlstm_in_ungroup(g):
    parts = [_groups_to_heads(g[4 * i:4 * i + 4]) for i in range(4)]
    return jnp.concatenate(parts + [g[17][:, :2 * ML_HEADS], g[16]], axis=1)


def _taps_to_groups(w, width):
    taps = w.shape[0]
    g = _pad_last(w.reshape(taps, -1, width), GROUP).transpose(1, 0, 2)
    return jnp.pad(g, ((0, 0), (0, 8 - taps), (0, 0)))


def _groups_to_taps(g, taps, width):
    return g[:, :taps, :width].transpose(1, 0, 2).reshape(taps, -1)


SMALL_IN_COLS = 384
SMALL_OUT_COLS = 1536
SECTION = 8


class _Gathered:
    def __init__(self, make_src, groups, me, after):
        self.groups, self.states, self.ready = groups, [], {}
        self.group_of = {k: gi for gi, g in enumerate(groups) for k in g}
        token, self.first, self.passing, self.pin = after, None, {}, None
        for gi, g in enumerate(groups):
            srcs = [make_src(k, None if gi == 0 else token[0:1, 0:1]) for k in g]
            placed = place_own([(a, ()) for a in srcs], me, token, f"place_own_{gi}")
            state, token = gather_start(srcs, placed, token, f"gather_start_{gi}")
            self.states.append(state)
            if gi == 0:
                self.first = token[0:1, 0:1]
        self.started = token

    def _get(self, key, after):
        gi = self.group_of[key]
        if gi not in self.ready:
            shapes = self.states[gi][3]
            if gi in self.passing:
                send, recv, bufs = self.passing.pop(gi)
                got = split_wait(send, recv, bufs, _pass_on_plan(shapes, True), after, f"pass_on_wait_{gi}")
            else:
                got = gather_wait(self.states[gi], after if gi else self.started, f"gather_wait_{gi}")
                got = gather_pass_on(got, shapes, f"gather_pass_on_{gi}")
            self.ready[gi] = dict(zip(self.groups[gi], got))
            nxt = gi + 1
            if gi >= 1 and nxt < len(self.groups):
                landed = gather_wait(self.states[nxt], got[0], f"gather_wait_{nxt}")
                send, recv, bufs, token = split_start(landed, _pass_on_plan(self.states[nxt][3], False), 3 * len(landed),
                                                      got[0], f"pass_on_start_{nxt}")
                self.passing[nxt], self.pin = (send, recv, bufs), token[0:1, 0:1]
        return self.ready[gi][key]

    def ffn(self, l, i, after):
        return tuple(self._get((n, l, i), after) for n in ("wg", "wu", "wd"))

    def mixer(self, l, after):
        win = _chips_to_cols(self._get(("win", l), after))
        win = _cols_to_groups(win) if l % 2 == 0 else _mlstm_in_groups(win)
        wkv = _cols_to_groups(self._get(("wkv", l), after).reshape(D_MODEL, 2 * D_XA))
        wout = self._get(("wout", l), after)
        if l % 2:
            wout = wout.reshape(D_MODEL, D_MODEL)
            tok = jnp.pad(wout[:D_TOK].reshape(ML_HEADS, ML_HEAD_DIM, D_MODEL), ((0, 0), (0, GROUP - ML_HEAD_DIM), (0, 0)))
            wout = jnp.concatenate([tok, wout[D_TOK:][None]], axis=0)
        return win, wkv, wout


class _GradSink:
    def __init__(self, apply):
        self.queue, self.apply, self.count, self.done = [], apply, 0, None

    @staticmethod
    def _by_chip(key, g):
        if key[0] == "wkv":
            return _groups_to_cols(g).reshape(N_CHIPS, D_MODEL // N_CHIPS, 2 * D_XA)
        if key[0] == "win":
            return _cols_to_chips(_groups_to_cols(g) if key[1] % 2 == 0 else _mlstm_in_ungroup(g))
        if key[0] == "wout" and key[1] % 2:
            full = jnp.concatenate([g[:ML_HEADS, :ML_HEAD_DIM].reshape(D_TOK, D_MODEL), g[ML_HEADS]], axis=0)
            return full.reshape(N_CHIPS, D_MODEL // N_CHIPS, D_MODEL)
        return g

    def push(self, grads):
        keys = list(grads)
        state, token = rs_begin([self._by_chip(k, grads[k]) for k in keys], self.done, f"rs_start_{self.count}")
        if self.queue:
            self._finish(token)
        self.queue.append((keys, state, self.count))
        self.count += 1
        return token

    def flush(self):
        self._finish(self.done)

    def _finish(self, after):
        keys, state, i = self.queue.pop(0)
        for key, g in zip(keys, rs_end(state, after, f"rs_wait_{i}")):
            self.done = self.apply(key, g, self.done)


def _local_step(x, mem, tgt, P, weights, sink):
    memb = mem.astype(BF16)
    saved = []
    pin0 = getattr(weights, "first", None)
    X, Xb = x, (x if pin0 is None else x + pin0).astype(BF16)
    after = Xb

    def pinned(gamma):
        pin = getattr(weights, "pin", None)
        return gamma if pin is None else gamma + pin

    for l in range(DEPTH):
        s = {}
        s["x0b"] = Xb
        s["wa"] = weights.ffn(l, 0, after)
        s["g1a"], s["u1a"], s["ha"], s["z1"], X1, X1b = ffn_fwd(Xb, X, *s["wa"], pinned(P["ln_g"][l][0]), P["ln_b"][l][0])
        s["x1b"] = X1b
        s["wm"] = win, wkv, wout = weights.mixer(l, X1b)
        u = proj(X1b, win, "mixer_in")
        kv = proj(memb, wkv, "mem_kv")
        s["u"], s["kv"] = u, kv
        if l % 2 == 0:
            tok = conv_mixer_fwd(u, P["convw"])
            qg = 9
        else:
            s["qk"] = qk_conv_fwd(u, P["qkw"])
            s["hm"], s["cst"], s["mst"] = mlstm_fwd(s["qk"], u, P["bg"])
            tok = head_norm_fwd(s["hm"], u, P["hg"])
            qg = 16
        xa = xattn_fwd(u, qg, kv)
        s["m"] = jnp.concatenate([tok, xa], axis=0)
        s["z2"], X2, X2b = contract_ln(s["m"], wout, X1, pinned(P["ln_g"][l][1]), P["ln_b"][l][1], 1.0, "mixer_out_ln")
        s["x2b"] = X2b
        s["wb"] = weights.ffn(l, 1, X2b)
        s["g1b"], s["u1b"], s["hb"], s["z3"], X, Xb = ffn_fwd(X2b, X2, *s["wb"], pinned(P["ln_g"][l][2]), P["ln_b"][l][2])
        after = Xb
        saved.append(s)

    loss, dX = loss_grad(X, tgt)

    G = {"ln_g": [[None] * 3 for _ in range(DEPTH)], "ln_b": [[None] * 3 for _ in range(DEPTH)]}
    pin = [jnp.zeros((1, 1), F32)]

    def ffn_backward(l, i, dX, z, xinb, g1, u1, h, w):
        k = 2 * i
        dgb, dub, dx, dyb, G["ln_g"][l][k], G["ln_b"][l][k] = ffn_bwd(dX, z, P["ln_g"][l][k] + pin[0], w[2], w[0], w[1], g1, u1)
        grads = {("wd", l, i): wgrad(h, dyb, BF16, "wgrad_down"), ("wg", l, i): wgrad(dgb, xinb, BF16, "wgrad_gate"),
                 ("wu", l, i): wgrad(dub, xinb, BF16, "wgrad_up")}
        return dx, grads

    for l in reversed(range(DEPTH)):
        s = saved[l]
        win, wkv, wout = s["wm"]
        dX, grads = ffn_backward(l, 1, dX, s["z3"], s["x2b"], s["g1b"], s["u1b"], s["hb"], s["wb"])
        dm, dz2, dz2b, G["ln_g"][l][1], G["ln_b"][l][1] = mixer_out_bwd(dX, s["z2"], P["ln_g"][l][1], wout)
        grads[("wout", l)] = wgrad(s["m"], dz2b, BF16, "wgrad_out")
        u, kv = s["u"], s["kv"]
        if l % 2 == 0:
            db, dc, dxi, G["convw"] = conv_mixer_bwd(u, P["convw"], dm)
            dq, dkv = xattn_bwd(u, 9, kv, dm, 3)
            du = jnp.concatenate([db, dc, dxi, dq], axis=0)
        else:
            dh, du, G["hg"] = head_norm_bwd(s["hm"], u, P["hg"], dm)
            dqk, du, dgate, G["bg"] = mlstm_bwd(s["qk"], u, P["bg"], s["cst"], s["mst"], dh, du)
            du, G["qkw"] = qk_conv_bwd(u, P["qkw"], dqk, du)
            du, dkv = xattn_bwd(u, 16, kv, dm, 4, du, dgate)
        grads[("win", l)] = wgrad(s["x1b"], du, BF16, "wgrad_in")
        grads[("wkv", l)] = wgrad(memb, dkv.astype(BF16), BF16, "wgrad_kv")
        dX = contract_t(du, win, dz2, "mixer_in_bwd")
        pin[0] = sink.push(grads)[0:1, 0:1]
        dX, grads = ffn_backward(l, 0, dX, s["z1"], s["x0b"], s["g1a"], s["u1a"], s["ha"], s["wa"])
        pin[0] = sink.push(grads)[0:1, 0:1]
    sink.flush()
    return loss, dX, G


def kernel(x, mem, ln_g, ln_b, ffn_w_gate, ffn_w_up, ffn_w_down, w_kv_mem, w_out, w_in_conv, conv_w, w_in_mlstm, b_gates, qk_conv_w, head_norm_g, loss_target, m_ln_g, m_ln_b, m_ffn_w_gate, m_ffn_w_up, m_ffn_w_down, m_w_kv_mem, m_w_out, m_w_in_conv, m_conv_w, m_w_in_mlstm, m_b_gates, m_qk_conv_w, m_head_norm_g, v_ln_g, v_ln_b, v_ffn_w_gate, v_ffn_w_up, v_ffn_w_down, v_w_kv_mem, v_w_out, v_w_in_conv, v_conv_w, v_w_in_mlstm, v_b_gates, v_qk_conv_w, v_head_norm_g):
    cx, cy = lax.axis_index("x"), lax.axis_index("y")
    chip = 2 * cx + cy

    def make_src(key, pin):
        if key[0] in ("wg", "wu"):
            w = jnp.swapaxes((ffn_w_gate if key[0] == "wg" else ffn_w_up)[key[1], key[2]], 0, 1)
        elif key[0] == "wd":
            w = ffn_w_down[key[1], key[2]]
        elif key[0] == "win":
            w = (w_in_conv, w_in_mlstm)[key[1]][0]
        else:
            w = (w_kv_mem if key[0] == "wkv" else w_out)[key[1]]
        return (w if pin is None else w + pin).astype(BF16)

    ffn_keys = lambda l, i: [("wg", l, i), ("wu", l, i), ("wd", l, i)]
    mixer_keys = lambda l: [("win", l), ("wkv", l), ("wout", l)]
    groups = [ffn_keys(0, 0), mixer_keys(0) + mixer_keys(1), ffn_keys(0, 1), ffn_keys(1, 0), ffn_keys(1, 1)]
    def section(a, width):
        a = a.reshape(-1, a.shape[-1])
        return jnp.pad(a, ((0, SECTION - a.shape[0]), (0, width - a.shape[1])))

    small = jnp.concatenate([section(a, SMALL_IN_COLS) for a in (ln_g, ln_b, conv_w, qk_conv_w)], axis=0)
    smalls = small_allgather(small, reduce=False)
    gathered = _Gathered(make_src, groups, jnp.reshape(chip, (1,)).astype(jnp.int32), smalls)
    smalls = smalls[0::2]
    ln_g_full = _chips_to_cols(smalls[:, 0:6, 0:256]).reshape(DEPTH, 3, 1, D_MODEL)
    ln_b_full = _chips_to_cols(smalls[:, 8:14, 0:256]).reshape(DEPTH, 3, 1, D_MODEL)
    conv_w_full = _chips_to_cols(smalls[:, 16:19, 0:192])
    qk_w_full = _chips_to_cols(smalls[:, 24:28, 0:384])

    P = {"ln_g": ln_g_full, "ln_b": ln_b_full, "convw": _taps_to_groups(conv_w_full, GROUP),
         "qkw": _taps_to_groups(qk_w_full, ML_HEAD_DIM), "bg": _pad_last(b_gates, GROUP),
         "hg": _pad_last(head_norm_g[0], GROUP)[:, None, :]}

    weights = {"ln_g": ln_g, "ln_b": ln_b, "ffn_w_gate": ffn_w_gate, "ffn_w_up": ffn_w_up, "ffn_w_down": ffn_w_down,
               "w_kv_mem": w_kv_mem, "w_out": w_out, "w_in_conv": w_in_conv, "conv_w": conv_w, "w_in_mlstm": w_in_mlstm,
               "b_gates": b_gates, "qk_conv_w": qk_conv_w, "head_norm_g": head_norm_g}
    ms = {"ln_g": m_ln_g, "ln_b": m_ln_b, "ffn_w_gate": m_ffn_w_gate, "ffn_w_up": m_ffn_w_up, "ffn_w_down": m_ffn_w_down,
          "w_kv_mem": m_w_kv_mem, "w_out": m_w_out, "w_in_conv": m_w_in_conv, "conv_w": m_conv_w, "w_in_mlstm": m_w_in_mlstm,
          "b_gates": m_b_gates, "qk_conv_w": m_qk_conv_w, "head_norm_g": m_head_norm_g}
    vs = {"ln_g": v_ln_g, "ln_b": v_ln_b, "ffn_w_gate": v_ffn_w_gate, "ffn_w_up": v_ffn_w_up, "ffn_w_down": v_ffn_w_down,
          "w_kv_mem": v_w_kv_mem, "w_out": v_w_out, "w_in_conv": v_w_in_conv, "conv_w": v_conv_w, "w_in_mlstm": v_w_in_mlstm,
          "b_gates": v_b_gates, "qk_conv_w": v_qk_conv_w, "head_norm_g": v_head_norm_g}
    names = list(weights)
    owner = {"wg": ("ffn_w_gate", True), "wu": ("ffn_w_up", True), "wd": ("ffn_w_down", False), "wkv": ("w_kv_mem", False),
             "wout": ("w_out", False), "win": None}
    updated = {}

    def apply(key, g, after):
        name, transposed = owner[key[0]] or (("w_in_conv", "w_in_mlstm")[key[1]], False)
        idx = (0,) if key[0] == "win" else tuple(key[1:])
        view = (lambda a: jnp.swapaxes(a, -1, -2)) if transposed else (lambda a: a)
        updated[name], token = adamw_into(view(weights[name]), view(ms[name]), view(vs[name]), g, updated.get(name), idx, after,
                                          "adamw_" + name + "_" + "_".join(map(str, idx)))
        return token

    sink = _GradSink(apply)
    loss, grad_x, G = _local_step(x[0], mem[0], loss_target[0], P, gathered, sink)

    dln_g = jnp.concatenate([G["ln_g"][l][k] for l in range(DEPTH) for k in range(3)], axis=0)
    dln_b = jnp.concatenate([G["ln_b"][l][k] for l in range(DEPTH) for k in range(3)], axis=0)
    lane = lax.broadcasted_iota(jnp.int32, (1, GROUP), 1)
    misc = jnp.where(lane < 8, G["bg"], 0.0) + jnp.where(lane == 8, loss, 0.0) + sink.done[0:1, 0:1]
    parts = (dln_g, dln_b, _groups_to_taps(G["convw"], 3, GROUP), misc, _groups_to_taps(G["qkw"], 4, ML_HEAD_DIM),
             G["hg"][:, 0, :ML_HEAD_DIM])
    tot = small_allgather(jnp.concatenate([section(a, SMALL_OUT_COLS) for a in parts], axis=0), reduce=True)
    loss_total = tot[24, 8]

    small_grads = {
        "ln_g": lax.dynamic_slice(tot[0:6, 0:D_MODEL], (0, chip * 256), (6, 256)).reshape(DEPTH, 3, 256),
        "ln_b": lax.dynamic_slice(tot[8:14, 0:D_MODEL], (0, chip * 256), (6, 256)).reshape(DEPTH, 3, 256),
        "conv_w": lax.dynamic_slice(tot[16:19, 0:D_TOK], (0, chip * 192), (3, 192))[None],
        "b_gates": tot[24:25, 0:8],
        "qk_conv_w": lax.dynamic_slice(tot[32:36, 0:2 * D_TOK], (0, chip * 384), (4, 384))[None],
        "head_norm_g": tot[40:44, 0:ML_HEAD_DIM][None],
    }
    grads, deltas, new_m, new_v = [], [], [], []
    for nme in names:
        if nme in updated:
            back = (lambda a: jnp.swapaxes(a, -1, -2)) if nme in ("ffn_w_gate", "ffn_w_up") else (lambda a: a)
            g, d, nm, nv = (back(a) for a in updated[nme])
        else:
            w, g = weights[nme], small_grads[nme]
            two = (math.prod(w.shape[:-1]), w.shape[-1])
            d, nm, nv = (a.reshape(w.shape) for a in adamw(w.reshape(two), g.reshape(two), ms[nme].reshape(two),
                                                           vs[nme].reshape(two), "adamw_" + nme))
        grads.append(g)
        deltas.append(d)
        new_m.append(nm)
        new_v.append(nv)
    return (loss_total, grad_x[None], *grads, *deltas, *new_m, *new_v)
```

```python
import functools
import math

import jax
import jax.numpy as jnp
from jax import lax
from jax.experimental import pallas as pl
from jax.experimental.pallas import tpu as pltpu

F32 = jnp.float32
BF16 = jnp.bfloat16
SDS = jax.ShapeDtypeStruct

D_MODEL = 1024
DEPTH = 2
N_MEM = 256
XA_HEADS = 4
XA_HEAD_DIM = 64
D_XA = 256
D_TOK = 768
ML_HEADS = 4
ML_HEAD_DIM = 192
ML_CHUNK = 64
D_FF = 2816
LN_EPS = 1e-5
ALPHA = (2.0 * DEPTH) ** 0.25
N_CHIPS = 4
N_DEV = 8
FF_SHARD = D_FF // N_CHIPS
GROUP = 256
NEG = -1e30

ADAM_LR = 0.001
ADAM_B1 = 0.9
ADAM_B2 = 0.999
ADAM_EPS = 1e-08
ADAM_WD = 0.01
ADAM_STEP = 10

VMEM_LIMIT = 56 * 1024 * 1024

NN = ((1,), (0,))
NT = ((1,), (1,))
TN = ((0,), (0,))
MESH = pl.DeviceIdType.MESH


def _dot(a, b, dims):
    return lax.dot_general(a, b, (dims, ((), ())), preferred_element_type=F32)


def _bdot(a, b, ca, cb):
    dims = (((ca,), (cb,)), ((0,), (0,)))
    ah, bh = a.astype(BF16), b.astype(BF16)
    al, bl = (a - ah.astype(F32)).astype(BF16), (b - bh.astype(F32)).astype(BF16)
    dot = functools.partial(lax.dot_general, dimension_numbers=dims, preferred_element_type=F32)
    return dot(ah, bh) + dot(al, bh) + dot(ah, bl)


def _bdot1(a, b, ca, cb):
    return lax.dot_general(a.astype(BF16), b.astype(BF16), (((ca,), (cb,)), ((0,), (0,))), preferred_element_type=F32)


def _sigmoid(x):
    return 1.0 / (1.0 + jnp.exp(-x))


def _params(sem, vmem=VMEM_LIMIT):
    return pltpu.CompilerParams(dimension_semantics=sem, vmem_limit_bytes=vmem)


def _tile(n, want):
    t = min(n, want)
    assert n % t == 0, (n, t)
    return t


def _layer_norm(z, gamma, beta):
    mu = jnp.mean(z, axis=-1, keepdims=True)
    zc = z - mu
    var = jnp.mean(zc * zc, axis=-1, keepdims=True)
    return zc * lax.rsqrt(var + LN_EPS) * gamma + beta


def _column_halves(n):
    mid = -(-n // (2 * 128)) * 128
    return ((0, mid), (mid, n))


def _resident(shape):
    return pl.BlockSpec(shape, lambda *_: (0,) * len(shape), pipeline_mode=pl.Buffered(1))


def _group_block(G, want):
    return max(d for d in range(1, max(1, min(G, want)) + 1) if G % d == 0)


def ffn_fwd(xb, x, wg, wu, wd, gamma, beta):
    S, K = xb.shape
    G, N, _ = wg.shape
    ts = _tile(S, 512)

    def body(xb_ref, x_ref, wg_ref, wu_ref, wd_ref, gm_ref, bt_ref, g_ref, u_ref, h_ref, z_ref, xn_ref, xnb_ref):
        j = pl.program_id(1)
        xv = xb_ref[...]
        g = _dot(xv, wg_ref[j], NT)
        u = _dot(xv, wu_ref[j], NT)
        h = (g * _sigmoid(g) * u).astype(BF16)
        g_ref[0] = g.astype(BF16)
        u_ref[0] = u.astype(BF16)
        h_ref[0] = h
        y = _dot(h, wd_ref[j], NN)

        @pl.when(j == 0)
        def _():
            z_ref[...] = y

        @pl.when(j > 0)
        def _():
            z_ref[...] += y

        @pl.when(j == G - 1)
        def _():
            z = ALPHA * x_ref[...] + 0.5 * z_ref[...]
            xn = _layer_norm(z, gm_ref[...], bt_ref[...])
            z_ref[...] = z
            xn_ref[...] = xn
            xnb_ref[...] = xn.astype(BF16)

    row = pl.BlockSpec((ts, K), lambda s, j: (s, 0))
    vec = pl.BlockSpec((1, K), lambda s, j: (0, 0))
    wspec = _resident((G, N, K))
    ospec = pl.BlockSpec((1, ts, N), lambda s, j: (j, s, 0))
    return pl.pallas_call(
        body, name="ffn_fwd", grid=(S // ts, G),
        in_specs=[row, row, wspec, wspec, wspec, vec, vec],
        out_specs=[ospec, ospec, ospec, row, row, row],
        out_shape=[SDS((G, S, N), BF16), SDS((G, S, N), BF16), SDS((G, S, N), BF16),
                   SDS((S, K), F32), SDS((S, K), F32), SDS((S, K), BF16)],
        compiler_params=_params(("parallel", "arbitrary")),
    )(xb, x, wg, wu, wd, gamma, beta)


def proj(xb, w, name):
    S, K = xb.shape
    G, _, N = w.shape
    ts = _tile(S, 1024)
    gb = _group_block(G, 6)

    def body(x_ref, w_ref, y_ref):
        xv = x_ref[...]
        for j in range(gb):
            y_ref[j] = _dot(xv, w_ref[j], NN)

    return pl.pallas_call(
        body, name=name, grid=(S // ts, G // gb),
        in_specs=[pl.BlockSpec((ts, K), lambda s, g: (s, 0)), pl.BlockSpec((gb, K, N), lambda s, g: (g, 0, 0))],
        out_specs=pl.BlockSpec((gb, ts, N), lambda s, g: (g, s, 0)),
        out_shape=SDS((G, S, N), F32),
        compiler_params=_params(("parallel", "parallel")),
    )(xb, w)


def contract_ln(a, w, xres, gamma, beta, scale, name):
    G, S, Kg = a.shape
    N = w.shape[2]
    ts = _tile(S, 1024)

    def body(a_ref, w_ref, x_ref, g_ref, b_ref, z_ref, xn_ref, xb_ref):
        acc = _dot(a_ref[0], w_ref[0], NN)
        for j in range(1, G):
            acc = acc + _dot(a_ref[j], w_ref[j], NN)
        z = ALPHA * x_ref[...] + scale * acc
        xn = _layer_norm(z, g_ref[...], b_ref[...])
        z_ref[...] = z
        xn_ref[...] = xn
        xb_ref[...] = xn.astype(BF16)

    row = pl.BlockSpec((ts, N), lambda s: (s, 0))
    vec = pl.BlockSpec((1, N), lambda s: (0, 0))
    return pl.pallas_call(
        body, name=name, grid=(S // ts,),
        in_specs=[pl.BlockSpec((G, ts, Kg), lambda s: (0, s, 0)), pl.BlockSpec((G, Kg, N), lambda s: (0, 0, 0)), row, vec, vec],
        out_specs=[row, row, row],
        out_shape=[SDS((S, N), F32), SDS((S, N), F32), SDS((S, N), BF16)],
        compiler_params=_params(("parallel",)),
    )(a, w, xres, gamma, beta)


def _layer_norm_bwd(dx, z, gamma):
    mu = jnp.mean(z, axis=-1, keepdims=True)
    zc = z - mu
    var = jnp.mean(zc * zc, axis=-1, keepdims=True)
    rstd = lax.rsqrt(var + LN_EPS)
    xhat = zc * rstd
    dxh = dx * gamma
    m1 = jnp.mean(dxh, axis=-1, keepdims=True)
    m2 = jnp.mean(dxh * xhat, axis=-1, keepdims=True)
    return rstd * (dxh - m1 - xhat * m2), jnp.sum(dx * xhat, axis=0, keepdims=True), jnp.sum(dx, axis=0, keepdims=True)


def ffn_bwd(dxn, z, gamma, wd, wg, wu, g1, u1):
    S, K = dxn.shape
    G, N, _ = wd.shape
    ts = _tile(S, 512)

    def body(dxn_ref, z_ref, gm_ref, wd_ref, wg_ref, wu_ref, g_ref, u_ref, dg_ref, du_ref, dx_ref, dy_ref, dgm_ref, dbt_ref):
        s, j = pl.program_id(0), pl.program_id(1)

        @pl.when((s == 0) & (j == 0))
        def _():
            dgm_ref[...] = jnp.zeros_like(dgm_ref)
            dbt_ref[...] = jnp.zeros_like(dbt_ref)

        @pl.when(j == 0)
        def _():
            dz, dgm, dbt = _layer_norm_bwd(dxn_ref[...], z_ref[...], gm_ref[...])
            dgm_ref[...] += dgm
            dbt_ref[...] += dbt
            dx_ref[...] = ALPHA * dz
            dy_ref[...] = (0.5 * dz).astype(BF16)

        dy = dy_ref[...]
        part = None
        for a, b in _column_halves(N):
            dh = _dot(dy, wd_ref[j, a:b, :], NT)
            g = g_ref[0, :, a:b].astype(F32)
            sig = _sigmoid(g)
            dg = (dh * u_ref[0, :, a:b].astype(F32) * (sig * (1.0 + g * (1.0 - sig)))).astype(BF16)
            du = (dh * (g * sig)).astype(BF16)
            dg_ref[0, :, a:b] = dg
            du_ref[0, :, a:b] = du
            p = _dot(dg, wg_ref[j, a:b, :], NN) + _dot(du, wu_ref[j, a:b, :], NN)
            part = p if part is None else part + p
        dx_ref[...] += part

    row = pl.BlockSpec((ts, K), lambda s, j: (s, 0))
    vec = pl.BlockSpec((1, K), lambda s, j: (0, 0))
    gspec = pl.BlockSpec((1, ts, N), lambda s, j: (j, s, 0))
    wspec = _resident((G, N, K))
    return pl.pallas_call(
        body, name="ffn_bwd", grid=(S // ts, G),
        in_specs=[row, row, vec, wspec, wspec, wspec, gspec, gspec],
        out_specs=[gspec, gspec, row, row, vec, vec],
        out_shape=[SDS((G, S, N), BF16), SDS((G, S, N), BF16), SDS((S, K), F32), SDS((S, K), BF16),
                   SDS((1, K), F32), SDS((1, K), F32)],
        compiler_params=_params(("arbitrary", "arbitrary")),
    )(dxn, z, gamma, wd, wg, wu, g1, u1)


def mixer_out_bwd(dxn, z, gamma, w):
    S, N = dxn.shape
    G, Kg, _ = w.shape
    ts = _tile(S, 512)

    def body(dxn_ref, z_ref, gm_ref, w_ref, dm_ref, dz_ref, dzb_ref, dgm_ref, dbt_ref):
        @pl.when(pl.program_id(0) == 0)
        def _():
            dgm_ref[...] = jnp.zeros_like(dgm_ref)
            dbt_ref[...] = jnp.zeros_like(dbt_ref)

        dz, dgm, dbt = _layer_norm_bwd(dxn_ref[...], z_ref[...], gm_ref[...])
        dgm_ref[...] += dgm
        dbt_ref[...] += dbt
        dzb = dz.astype(BF16)
        dz_ref[...] = dz
        dzb_ref[...] = dzb
        for j in range(G):
            dm_ref[j] = _dot(dzb, w_ref[j], NT)

    row = pl.BlockSpec((ts, N), lambda s: (s, 0))
    vec = pl.BlockSpec((1, N), lambda s: (0, 0))
    return pl.pallas_call(
        body, name="mixer_out_bwd", grid=(S // ts,),
        in_specs=[row, row, vec, pl.BlockSpec((G, Kg, N), lambda s: (0, 0, 0))],
        out_specs=[pl.BlockSpec((G, ts, Kg), lambda s: (0, s, 0)), row, row, vec, vec],
        out_shape=[SDS((G, S, Kg), F32), SDS((S, N), F32), SDS((S, N), BF16), SDS((1, N), F32), SDS((1, N), F32)],
        compiler_params=_params(("arbitrary",)),
    )(dxn, z, gamma, w)


def contract_t(da, w, res, name):
    G, S, Ng = da.shape
    K = w.shape[1]
    ts = _tile(S, 1024)
    gb = _group_block(G, 6)

    def body(da_ref, w_ref, r_ref, o_ref):
        g = pl.program_id(1)
        part = _dot(da_ref[0], w_ref[0], NT)
        for j in range(1, gb):
            part = part + _dot(da_ref[j], w_ref[j], NT)

        @pl.when(g == 0)
        def _():
            o_ref[...] = ALPHA * r_ref[...] + part

        @pl.when(g > 0)
        def _():
            o_ref[...] += part

    row = pl.BlockSpec((ts, K), lambda s, g: (s, 0))
    return pl.pallas_call(
        body, name=name, grid=(S // ts, G // gb),
        in_specs=[pl.BlockSpec((gb, ts, Ng), lambda s, g: (g, s, 0)), pl.BlockSpec((gb, K, Ng), lambda s, g: (g, 0, 0)), row],
        out_specs=row,
        out_shape=SDS((S, K), F32),
        compiler_params=_params(("parallel", "arbitrary")),
    )(da, w, res)


WGRAD_ACC_ELEMS = 6 * 1024 * 256


def wgrad(a, b, out_dtype, name):
    ga, gb = a.ndim == 3, b.ndim == 3
    G = a.shape[0] if ga else b.shape[0]
    S, K = a.shape[-2:]
    N = b.shape[-1]
    ts = _tile(S, 2048)
    ns = S // ts
    ng = _group_block(G, WGRAD_ACC_ELEMS // (K * N))

    def body(a_ref, b_ref, o_ref, acc):
        s = pl.program_id(1)

        @pl.when(s == 0)
        def _():
            acc[...] = jnp.zeros_like(acc)

        for j in range(ng):
            acc[j] += _dot(a_ref[j] if ga else a_ref[...], b_ref[j] if gb else b_ref[...], TN)

        @pl.when(s == ns - 1)
        def _():
            o_ref[...] = acc[...].astype(out_dtype)

    aspec = pl.BlockSpec((ng, ts, K), lambda g, s: (g, s, 0)) if ga else pl.BlockSpec((ts, K), lambda g, s: (s, 0))
    bspec = pl.BlockSpec((ng, ts, N), lambda g, s: (g, s, 0)) if gb else pl.BlockSpec((ts, N), lambda g, s: (s, 0))
    return pl.pallas_call(
        body, name=name, grid=(G // ng, ns),
        in_specs=[aspec, bspec],
        out_specs=pl.BlockSpec((ng, K, N), lambda g, s: (g, 0, 0)),
        out_shape=SDS((G, K, N), out_dtype),
        scratch_shapes=[pltpu.VMEM((ng, K, N), F32)],
        compiler_params=_params(("parallel", "arbitrary")),
    )(a, b)


def loss_grad(xn, tgt):
    S, N = xn.shape
    ts = _tile(S, 1024)

    def body(x_ref, t_ref, l_ref, dx_ref):
        @pl.when(pl.program_id(0) == 0)
        def _():
            l_ref[...] = jnp.zeros_like(l_ref)

        e = x_ref[...] - t_ref[...]
        dx_ref[...] = e * (1.0 / N)
        l_ref[...] += 0.5 * jnp.sum(jnp.mean(e * e, axis=-1, keepdims=True), axis=0, keepdims=True)

    row = pl.BlockSpec((ts, N), lambda s: (s, 0))
    return pl.pallas_call(
        body, name="loss_grad", grid=(S // ts,),
        in_specs=[row, row],
        out_specs=[pl.BlockSpec((1, 1), lambda s: (0, 0)), row],
        out_shape=[SDS((1, 1), F32), SDS((S, N), F32)],
        compiler_params=_params(("arbitrary",)),
    )(xn, tgt)


def _shift_down(x, k):
    if k == 0:
        return x
    rows = lax.broadcasted_iota(jnp.int32, x.shape, 0)
    return jnp.where(rows >= k, pltpu.roll(x, k, 0), 0.0)


def _shift_up(x, k):
    if k == 0:
        return x
    n = x.shape[0]
    rows = lax.broadcasted_iota(jnp.int32, x.shape, 0)
    return jnp.where(rows < n - k, pltpu.roll(x, n - k, 0), 0.0)


LANES = 128


def conv_mixer_fwd(u, cw):
    _, S, _ = u.shape
    nh = GROUP // LANES

    def body(b_ref, c_ref, x_ref, w_ref, o_ref):
        p = c_ref[0] * x_ref[0]
        w = w_ref[0]
        conv = w[2:3] * p + w[1:2] * _shift_down(p, 1) + w[0:1] * _shift_down(p, 2)
        o_ref[0] = (b_ref[0] * conv).astype(BF16)

    def uspec(off):
        return pl.BlockSpec((1, S, LANES), lambda g, h: (g + off, 0, h))

    return pl.pallas_call(
        body, name="conv_mixer_fwd", grid=(3, nh),
        in_specs=[uspec(0), uspec(3), uspec(6), pl.BlockSpec((1, 8, LANES), lambda g, h: (g, 0, h))],
        out_specs=pl.BlockSpec((1, S, LANES), lambda g, h: (g, 0, h)),
        out_shape=SDS((4, S, GROUP), BF16),
        compiler_params=_params(("parallel", "parallel")),
    )(u, u, u, cw)


def conv_mixer_bwd(u, cw, dm):
    _, S, _ = u.shape
    nh = GROUP // LANES

    def body(b_ref, c_ref, x_ref, w_ref, d_ref, db_ref, dc_ref, dx_ref, dw_ref):
        cg, xi = c_ref[0], x_ref[0]
        p = cg * xi
        p1, p2 = _shift_down(p, 1), _shift_down(p, 2)
        w = w_ref[0]
        conv = w[2:3] * p + w[1:2] * p1 + w[0:1] * p2
        dt = d_ref[0]
        db_ref[0] = (dt * conv).astype(BF16)
        dcv = dt * b_ref[0]
        dp = w[2:3] * dcv + w[1:2] * _shift_up(dcv, 1) + w[0:1] * _shift_up(dcv, 2)
        dc_ref[0] = (dp * xi).astype(BF16)
        dx_ref[0] = (dp * cg).astype(BF16)
        dw = jnp.concatenate([jnp.sum(dcv * p2, axis=0, keepdims=True), jnp.sum(dcv * p1, axis=0, keepdims=True),
                              jnp.sum(dcv * p, axis=0, keepdims=True), jnp.zeros((5, LANES), F32)], axis=0)
        dw_ref[0] = dw

    def uspec(off):
        return pl.BlockSpec((1, S, LANES), lambda g, h: (g + off, 0, h))

    ospec = pl.BlockSpec((1, S, LANES), lambda g, h: (g, 0, h))
    wspec = pl.BlockSpec((1, 8, LANES), lambda g, h: (g, 0, h))
    return pl.pallas_call(
        body, name="conv_mixer_bwd", grid=(3, nh),
        in_specs=[uspec(0), uspec(3), uspec(6), wspec, ospec],
        out_specs=[ospec, ospec, ospec, wspec],
        out_shape=[SDS((3, S, GROUP), BF16)] * 3 + [SDS((3, 8, GROUP), F32)],
        compiler_params=_params(("parallel", "parallel")),
    )(u, u, u, cw, dm)


def qk_conv_fwd(u, qw):
    _, S, _ = u.shape
    nh = GROUP // LANES

    def body(u_ref, w_ref, o_ref):
        x = u_ref[0]
        w = w_ref[0]
        pre = w[3:4] * x + w[2:3] * _shift_down(x, 1) + w[1:2] * _shift_down(x, 2) + w[0:1] * _shift_down(x, 3)
        o_ref[0] = pre * _sigmoid(pre)

    spec = pl.BlockSpec((1, S, LANES), lambda g, h: (g, 0, h))
    return pl.pallas_call(
        body, name="qk_conv_fwd", grid=(8, nh),
        in_specs=[spec, pl.BlockSpec((1, 8, LANES), lambda g, h: (g, 0, h))],
        out_specs=spec,
        out_shape=SDS((8, S, GROUP), F32),
        compiler_params=_params(("parallel", "parallel")),
    )(u, qw)


def qk_conv_bwd(u, qw, dqk, du):
    _, S, _ = u.shape
    nh = GROUP // LANES

    def body(u_ref, w_ref, d_ref, du_in_ref, du_ref, dw_ref):
        x = u_ref[0]
        w = w_ref[0]
        x1, x2, x3 = _shift_down(x, 1), _shift_down(x, 2), _shift_down(x, 3)
        pre = w[3:4] * x + w[2:3] * x1 + w[1:2] * x2 + w[0:1] * x3
        sig = _sigmoid(pre)
        dpre = d_ref[0] * (sig * (1.0 + pre * (1.0 - sig)))
        du = w[3:4] * dpre + w[2:3] * _shift_up(dpre, 1) + w[1:2] * _shift_up(dpre, 2) + w[0:1] * _shift_up(dpre, 3)
        du_ref[0] = du.astype(BF16)
        dw = jnp.concatenate([jnp.sum(dpre * x3, axis=0, keepdims=True), jnp.sum(dpre * x2, axis=0, keepdims=True),
                              jnp.sum(dpre * x1, axis=0, keepdims=True), jnp.sum(dpre * x, axis=0, keepdims=True),
                              jnp.zeros((4, LANES), F32)], axis=0)
        dw_ref[0] = dw

    spec = pl.BlockSpec((1, S, LANES), lambda g, h: (g, 0, h))
    wspec = pl.BlockSpec((1, 8, LANES), lambda g, h: (g, 0, h))
    return pl.pallas_call(
        body, name="qk_conv_bwd", grid=(8, nh),
        in_specs=[spec, wspec, spec, pl.BlockSpec(memory_space=pl.ANY)],
        out_specs=[spec, wspec],
        out_shape=[SDS(du.shape, BF16), SDS((8, 8, GROUP), F32)],
        input_output_aliases={3: 0},
        compiler_params=_params(("parallel", "parallel")),
    )(u, qw, dqk, du)


def _head_masks():
    lane = lax.broadcasted_iota(jnp.int32, (1, D_XA), 1)
    return [(lane >= h * XA_HEAD_DIM) & (lane < (h + 1) * XA_HEAD_DIM) for h in range(XA_HEADS)]


def xattn_fwd(u, qg, kv, tok):
    _, S, _ = u.shape
    ts = _tile(S, 1024)
    scale = XA_HEAD_DIM ** -0.5

    def body(q_ref, kv_ref, tok_ref, o_ref):
        q = q_ref[0]
        k = kv_ref[0].astype(BF16)
        v = kv_ref[1]
        o = jnp.zeros((ts, D_XA), F32)
        for m in _head_masks():
            s = _dot(jnp.where(m, q, 0.0).astype(BF16), k, NT) * scale
            s = s - jnp.max(s, axis=-1, keepdims=True)
            e = jnp.exp(s)
            p = e / jnp.sum(e, axis=-1, keepdims=True)
            o = o + _dot(p.astype(BF16), jnp.where(m, v, 0.0).astype(BF16), NN)
        o_ref[0] = o.astype(BF16)

    slot = tok.shape[0] - 1
    return pl.pallas_call(
        body, name="xattn_fwd", grid=(S // ts,),
        in_specs=[pl.BlockSpec((1, ts, GROUP), lambda s: (qg, s, 0)), pl.BlockSpec((2, N_MEM, GROUP), lambda s: (0, 0, 0)),
                  pl.BlockSpec(memory_space=pl.ANY)],
        out_specs=pl.BlockSpec((1, ts, GROUP), lambda s: (slot, s, 0)),
        out_shape=SDS(tok.shape, BF16),
        input_output_aliases={2: 0},
        compiler_params=_params(("parallel",)),
    )(u, kv, tok)


def xattn_bwd(u, qg, kv, dm, dg, du=None, dgate=None):
    _, S, _ = u.shape
    ts = _tile(S, 1024)
    scale = XA_HEAD_DIM ** -0.5

    def body(q_ref, kv_ref, do_ref, *refs):
        dq_ref, dkv_ref = refs[-2:]

        @pl.when(pl.program_id(0) == 0)
        def _():
            dkv_ref[...] = jnp.zeros_like(dkv_ref)

        q = q_ref[0]
        k = kv_ref[0]
        v = kv_ref[1]
        kb = k.astype(BF16)
        do = do_ref[0]
        dq = jnp.zeros((ts, D_XA), F32)
        dk = jnp.zeros((N_MEM, D_XA), F32)
        dv = jnp.zeros((N_MEM, D_XA), F32)
        for m in _head_masks():
            qm = jnp.where(m, q, 0.0).astype(BF16)
            s = _dot(qm, kb, NT) * scale
            s = s - jnp.max(s, axis=-1, keepdims=True)
            e = jnp.exp(s)
            p = e / jnp.sum(e, axis=-1, keepdims=True)
            dom = jnp.where(m, do, 0.0).astype(BF16)
            dp = _dot(dom, jnp.where(m, v, 0.0).astype(BF16), NT)
            ds = (p * (dp - jnp.sum(dp * p, axis=-1, keepdims=True)) * scale).astype(BF16)
            dq = dq + _dot(ds, jnp.where(m, k, 0.0).astype(BF16), NN)
            dk = dk + _dot(ds, qm, TN)
            dv = dv + _dot(p.astype(BF16), dom, TN)
        dq_ref[0] = dq.astype(BF16)
        if du is not None:
            dq_ref[1] = refs[0][0]
        dkv_ref[0] += dk
        dkv_ref[1] += dv

    in_specs = [pl.BlockSpec((1, ts, GROUP), lambda s: (qg, s, 0)), pl.BlockSpec((2, N_MEM, GROUP), lambda s: (0, 0, 0)),
                pl.BlockSpec((1, ts, GROUP), lambda s: (dg, s, 0))]
    args, aliases = [u, kv, dm], {}
    dq_spec, dq_shape = pl.BlockSpec((1, ts, GROUP), lambda s: (0, s, 0)), SDS((1, S, GROUP), BF16)
    if du is not None:
        in_specs += [pl.BlockSpec((1, ts, GROUP), lambda s: (0, s, 0)), pl.BlockSpec(memory_space=pl.ANY)]
        args += [dgate, du]
        aliases = {4: 0}
        dq_spec, dq_shape = pl.BlockSpec((2, ts, GROUP), lambda s: (qg // 2, s, 0)), SDS(du.shape, BF16)
    return pl.pallas_call(
        body, name="xattn_bwd", grid=(S // ts,),
        in_specs=in_specs,
        out_specs=[dq_spec, pl.BlockSpec((2, N_MEM, GROUP), lambda s: (0, 0, 0))],
        out_shape=[dq_shape, SDS((2, N_MEM, GROUP), F32)],
        input_output_aliases=aliases,
        compiler_params=_params(("arbitrary",)),
    )(*args)


ML_BLOCK_CHUNKS = 4
H4 = ML_HEADS
L = ML_CHUNK
NLANE = ML_HEAD_DIM


def _chunk_consts():
    r = lax.broadcasted_iota(jnp.int32, (1, L, L), 1)
    c = lax.broadcasted_iota(jnp.int32, (1, L, L), 2)
    return r >= c, r <= c, r == c


def _gate_cols(gb):
    lane = lax.broadcasted_iota(jnp.int32, gb.shape, 1)
    li = jnp.stack([jnp.sum(jnp.where(lane == h, gb, 0.0), axis=1, keepdims=True) for h in range(H4)])
    gf = jnp.stack([jnp.sum(jnp.where(lane == H4 + h, gb, 0.0), axis=1, keepdims=True) for h in range(H4)])
    return li, gf


def _log_sigmoid(x):
    return jnp.minimum(x, 0.0) - jnp.log(1.0 + jnp.exp(-jnp.abs(x)))


def _chunk_forward(q, k, v_aug, li_col, lf_col, c_prev, m_prev):
    tri, tri_t, eye = _chunk_consts()
    lf_row = jnp.sum(jnp.where(eye, lf_col, 0.0), axis=1, keepdims=True)
    li_row = jnp.sum(jnp.where(eye, li_col, 0.0), axis=1, keepdims=True)
    bcum_col = jnp.sum(jnp.where(tri, lf_row, 0.0), axis=2, keepdims=True)
    bcum_row = jnp.sum(jnp.where(tri_t, lf_col, 0.0), axis=1, keepdims=True)
    log_d = jnp.where(tri, bcum_col - bcum_row + li_row, NEG)
    log_inter = bcum_col + m_prev
    m_t = jnp.maximum(log_inter, jnp.max(log_d, axis=2, keepdims=True))
    w_intra = jnp.exp(log_d - m_t)
    w_inter = jnp.exp(log_inter - m_t)
    sc = _bdot(q, k, 2, 2) * w_intra
    qc = _bdot1(q, c_prev, 2, 1)
    num = _bdot(sc, v_aug, 2, 1) + w_inter * qc
    lane = lax.broadcasted_iota(jnp.int32, num.shape, 2)
    den = jnp.sum(jnp.where(lane == NLANE, num, 0.0), axis=2, keepdims=True)
    e_m = jnp.exp(-m_t)
    b_last = jnp.sum(lf_row, axis=2, keepdims=True)
    log_w = b_last - bcum_col + li_col
    m_new = jnp.maximum(b_last + m_prev, jnp.max(log_w, axis=1, keepdims=True))
    w_k = jnp.exp(log_w - m_new)
    decay = jnp.exp(b_last + m_prev - m_new)
    return dict(w_intra=w_intra, w_inter=w_inter, sc=sc, qc=qc, num=num, den=den, e_m=e_m, lane=lane,
                w_k=w_k, decay=decay, m_new=m_new)


def mlstm_fwd(qk, u, bg):
    _, S, _ = qk.shape
    nc = S // L
    cb = min(ML_BLOCK_CHUNKS, nc)
    rows = cb * L
    kscale = ML_HEAD_DIM ** -0.5

    def body(qk_ref, v_ref, g_ref, bg_ref, h_ref, cst_ref, mst_ref, c_sc, m_sc):
        @pl.when(pl.program_id(0) == 0)
        def _():
            c_sc[...] = jnp.zeros_like(c_sc)
            m_sc[...] = jnp.zeros_like(m_sc)

        for c in range(cb):
            sl = pl.ds(c * L, L)
            q = qk_ref[0:H4, sl, :]
            k = qk_ref[H4:2 * H4, sl, :] * kscale
            v = v_ref[:, sl, :]
            lane = lax.broadcasted_iota(jnp.int32, v.shape, 2)
            v_aug = jnp.where(lane == NLANE, 1.0, v)
            li_col, gf = _gate_cols(g_ref[0, sl, :] + bg_ref[...])
            lf_col = _log_sigmoid(gf)
            c_prev = c_sc[...]
            m_prev = m_sc[...]
            f = _chunk_forward(q, k, v_aug, li_col, lf_col, c_prev, m_prev)
            r = 1.0 / jnp.maximum(jnp.abs(f["den"]), f["e_m"])
            h_ref[:, sl, :] = jnp.where(lane < NLANE, f["num"] * r, 0.0)
            cst_ref[c] = c_prev
            mst_ref[c] = jnp.broadcast_to(m_prev, (H4, 1, LANES))
            c_sc[...] = f["decay"] * c_prev + _bdot(k * f["w_k"], v_aug, 1, 1)
            m_sc[...] = f["m_new"]

    def hspec(blk):
        return pl.BlockSpec((H4, rows, GROUP), lambda i: (blk, i, 0))

    return pl.pallas_call(
        body, name="mlstm_fwd", grid=(nc // cb,),
        in_specs=[pl.BlockSpec((2 * H4, rows, GROUP), lambda i: (0, i, 0)), hspec(2),
                  pl.BlockSpec((1, rows, GROUP), lambda i: (17, i, 0)), pl.BlockSpec((1, GROUP), lambda i: (0, 0))],
        out_specs=[hspec(0), pl.BlockSpec((cb, H4, GROUP, GROUP), lambda i: (i, 0, 0, 0)),
                   pl.BlockSpec((cb, H4, 1, LANES), lambda i: (i, 0, 0, 0))],
        out_shape=[SDS((H4, S, GROUP), F32), SDS((nc, H4, GROUP, GROUP), F32), SDS((nc, H4, 1, LANES), F32)],
        scratch_shapes=[pltpu.VMEM((H4, GROUP, GROUP), F32), pltpu.VMEM((H4, 1, 1), F32)],
        compiler_params=_params(("arbitrary",)),
    )(qk, u, u, bg)


def mlstm_bwd(qk, u, bg, cst, mst, dh, du):
    _, S, _ = qk.shape
    nc = S // L
    cb = min(ML_BLOCK_CHUNKS, nc)
    rows = cb * L
    nb = nc // cb
    kscale = ML_HEAD_DIM ** -0.5

    def body(qk_ref, v_ref, g_ref, bg_ref, cst_ref, mst_ref, dh_ref, du_in_ref, dqk_ref, dv_ref, dg_ref, dbg_ref, dc_sc):
        @pl.when(pl.program_id(0) == 0)
        def _():
            dc_sc[...] = jnp.zeros_like(dc_sc)
            dbg_ref[...] = jnp.zeros_like(dbg_ref)

        tri, tri_t, eye = _chunk_consts()
        for c in reversed(range(cb)):
            sl = pl.ds(c * L, L)
            q = qk_ref[0:H4, sl, :]
            k = qk_ref[H4:2 * H4, sl, :] * kscale
            v = v_ref[:, sl, :]
            lane = lax.broadcasted_iota(jnp.int32, v.shape, 2)
            v_aug = jnp.where(lane == NLANE, 1.0, v)
            li_col, gf = _gate_cols(g_ref[0, sl, :] + bg_ref[...])
            lf_col = _log_sigmoid(gf)
            c_prev = cst_ref[c]
            m_prev = mst_ref[c][:, :, 0:1]
            f = _chunk_forward(q, k, v_aug, li_col, lf_col, c_prev, m_prev)
            w_intra, w_inter, sc, num, den, e_m = f["w_intra"], f["w_inter"], f["sc"], f["num"], f["den"], f["e_m"]
            absd = jnp.abs(den)
            r = 1.0 / jnp.maximum(absd, e_m)
            dhv = dh_ref[:, sl, :]
            s1 = jnp.sum(jnp.where(lane < NLANE, dhv * num, 0.0), axis=2, keepdims=True)
            dden = jnp.where(absd > e_m, -s1 * r * r * jnp.sign(den), 0.0)
            dnum = jnp.where(lane == NLANE, dden, jnp.where(lane < NLANE, dhv * r, 0.0))
            dsc = _bdot1(dnum, v_aug, 2, 2)
            dv = _bdot1(sc, dnum, 1, 1)
            gmat = dsc * sc
            dqk = dsc * w_intra
            dq = _bdot1(dqk, k, 2, 1) + w_inter * _bdot1(dnum, c_prev, 2, 2)
            dk = _bdot1(dqk, q, 1, 1)
            dc_prev = _bdot(q * w_inter, dnum, 1, 1)
            dlog_inter = jnp.sum(dnum * f["qc"], axis=2, keepdims=True) * w_inter
            dbcum_col = dlog_inter + jnp.sum(gmat, axis=2, keepdims=True)
            g_row = jnp.sum(gmat, axis=1, keepdims=True)
            dcn = dc_sc[...]
            w_k, decay = f["w_k"], f["decay"]
            kw = k * w_k
            dc_prev = dc_prev + decay * dcn
            db_last = jnp.sum(jnp.sum(dcn * c_prev, axis=2, keepdims=True), axis=1, keepdims=True) * decay
            dkw = _bdot(v_aug, dcn, 2, 2)
            dv = dv + _bdot1(kw, dcn, 2, 1)
            dk = dk + dkw * w_k
            dlogw = jnp.sum(dkw * k, axis=2, keepdims=True) * w_k
            db_last = db_last + jnp.sum(dlogw, axis=1, keepdims=True)
            dbcum_col = dbcum_col - dlogw
            rowi = lax.broadcasted_iota(jnp.int32, (1, L, 1), 1)
            dbcum_col = dbcum_col + jnp.where(rowi == L - 1, db_last, 0.0)
            dbcum_row = jnp.sum(jnp.where(eye, dbcum_col, 0.0), axis=1, keepdims=True) - g_row
            dlf_col = jnp.sum(jnp.where(tri_t, dbcum_row, 0.0), axis=2, keepdims=True)
            dli_col = dlogw + jnp.sum(jnp.where(eye, g_row, 0.0), axis=2, keepdims=True)
            dgf_col = dlf_col * _sigmoid(-gf)
            lane_g = lax.broadcasted_iota(jnp.int32, (L, GROUP), 1)
            dg = jnp.zeros((L, GROUP), F32)
            for h in range(H4):
                dg = dg + jnp.where(lane_g == h, dli_col[h], 0.0) + jnp.where(lane_g == H4 + h, dgf_col[h], 0.0)
            dqk_ref[0:H4, sl, :] = dq
            dqk_ref[H4:2 * H4, sl, :] = dk * kscale
            dv_ref[:, sl, :] = jnp.where(lane < NLANE, dv, 0.0).astype(BF16)
            dg_ref[0, sl, :] = dg.astype(BF16)
            dbg_ref[...] += jnp.sum(dg, axis=0, keepdims=True)
            dc_sc[...] = dc_prev

    def hspec(blk):
        return pl.BlockSpec((H4, rows, GROUP), lambda i: (blk, nb - 1 - i, 0))

    gspec = pl.BlockSpec((1, rows, GROUP), lambda i: (17, nb - 1 - i, 0))
    qkspec = pl.BlockSpec((2 * H4, rows, GROUP), lambda i: (0, nb - 1 - i, 0))
    return pl.pallas_call(
        body, name="mlstm_bwd", grid=(nb,),
        in_specs=[qkspec, hspec(2), gspec, pl.BlockSpec((1, GROUP), lambda i: (0, 0)),
                  pl.BlockSpec((cb, H4, GROUP, GROUP), lambda i: (nb - 1 - i, 0, 0, 0)),
                  pl.BlockSpec((cb, H4, 1, LANES), lambda i: (nb - 1 - i, 0, 0, 0)), hspec(0), pl.BlockSpec(memory_space=pl.ANY)],
        out_specs=[qkspec, hspec(2), pl.BlockSpec((1, rows, GROUP), lambda i: (0, nb - 1 - i, 0)),
                   pl.BlockSpec((1, GROUP), lambda i: (0, 0))],
        input_output_aliases={7: 1},
        out_shape=[SDS((2 * H4, S, GROUP), F32), SDS(du.shape, BF16),
                   SDS((1, S, GROUP), BF16), SDS((1, GROUP), F32)],
        scratch_shapes=[pltpu.VMEM((H4, GROUP, GROUP), F32)],
        compiler_params=_params(("arbitrary",)),
    )(qk, u, u, bg, cst, mst, dh, du)


def head_norm_fwd(hm, u, hg):
    _, S, _ = hm.shape
    ts = _tile(S, 2048)

    def body(h_ref, o_ref, g_ref, t_ref):
        h = h_ref[0]
        lane = lax.broadcasted_iota(jnp.int32, h.shape, 1)
        valid = lane < ML_HEAD_DIM
        mu = jnp.sum(h, axis=-1, keepdims=True) * (1.0 / ML_HEAD_DIM)
        hc = jnp.where(valid, h - mu, 0.0)
        var = jnp.sum(hc * hc, axis=-1, keepdims=True) * (1.0 / ML_HEAD_DIM)
        hn = hc * lax.rsqrt(var + LN_EPS) * g_ref[0]
        t_ref[0] = (_sigmoid(o_ref[0]) * hn).astype(BF16)

    return pl.pallas_call(
        body, name="head_norm_fwd", grid=(H4, S // ts),
        in_specs=[pl.BlockSpec((1, ts, GROUP), lambda h, s: (h, s, 0)), pl.BlockSpec((1, ts, GROUP), lambda h, s: (12 + h, s, 0)),
                  pl.BlockSpec((1, 1, GROUP), lambda h, s: (h, 0, 0))],
        out_specs=pl.BlockSpec((1, ts, GROUP), lambda h, s: (h, s, 0)),
        out_shape=SDS((H4 + 1, S, GROUP), BF16),
        compiler_params=_params(("parallel", "parallel")),
    )(hm, u, hg)


def head_norm_bwd(hm, u, hg, dm):
    _, S, _ = hm.shape
    ts = _tile(S, 2048)

    def body(h_ref, o_ref, g_ref, d_ref, dh_ref, do_ref, dg_ref):
        @pl.when(pl.program_id(1) == 0)
        def _():
            dg_ref[...] = jnp.zeros_like(dg_ref)

        h = h_ref[0]
        lane = lax.broadcasted_iota(jnp.int32, h.shape, 1)
        valid = lane < ML_HEAD_DIM
        inv = 1.0 / ML_HEAD_DIM
        mu = jnp.sum(h, axis=-1, keepdims=True) * inv
        hc = jnp.where(valid, h - mu, 0.0)
        var = jnp.sum(hc * hc, axis=-1, keepdims=True) * inv
        rstd = lax.rsqrt(var + LN_EPS)
        xhat = hc * rstd
        g = g_ref[0]
        sig = _sigmoid(o_ref[0])
        dt = jnp.where(valid, d_ref[0], 0.0)
        do_ref[0] = (dt * xhat * g * sig * (1.0 - sig)).astype(BF16)
        dhn = dt * sig
        dg_ref[0] += jnp.sum(dhn * xhat, axis=0, keepdims=True)
        dxh = dhn * g
        m1 = jnp.sum(dxh, axis=-1, keepdims=True) * inv
        m2 = jnp.sum(dxh * xhat, axis=-1, keepdims=True) * inv
        dh_ref[0] = jnp.where(valid, rstd * (dxh - m1 - xhat * m2), 0.0)

    spec = pl.BlockSpec((1, ts, GROUP), lambda h, s: (h, s, 0))
    gspec = pl.BlockSpec((1, 1, GROUP), lambda h, s: (h, 0, 0))
    return pl.pallas_call(
        body, name="head_norm_bwd", grid=(H4, S // ts),
        in_specs=[spec, pl.BlockSpec((1, ts, GROUP), lambda h, s: (12 + h, s, 0)), gspec, spec],
        out_specs=[spec, pl.BlockSpec((1, ts, GROUP), lambda h, s: (12 + h, s, 0)), gspec],
        out_shape=[SDS((H4, S, GROUP), F32), SDS((u.shape[0], S, GROUP), BF16), SDS((H4, 1, GROUP), F32)],
        compiler_params=_params(("parallel", "arbitrary")),
    )(hm, u, hg, dm)


def _adamw_math(w, g, m, v):
    c1 = 1.0 / (1.0 - ADAM_B1 ** ADAM_STEP)
    c2 = 1.0 / (1.0 - ADAM_B2 ** ADAM_STEP)
    nm = ADAM_B1 * m + (1.0 - ADAM_B1) * g
    nv = ADAM_B2 * v + (1.0 - ADAM_B2) * (g * g)
    return -ADAM_LR * ((nm * c1) / (jnp.sqrt(nv * c2) + ADAM_EPS) + ADAM_WD * w), nm, nv


def _row_tile(R, cap=512):
    return R if R <= cap else max(d for d in range(8, cap + 1, 8) if R % d == 0)


def adamw_into(w, m, v, g, outs, idx, after, name):
    R, C = g.shape
    tr = _row_tile(R)
    lead = (0,) * len(idx)

    def body(w_ref, m_ref, v_ref, g_ref, *rest):
        go_ref, d_ref, nm_ref, nv_ref, token = rest[-5:]
        token[...] = jnp.zeros_like(token)
        gv = g_ref[...]
        d, nm, nv = _adamw_math(w_ref[lead], gv, m_ref[lead], v_ref[lead])
        go_ref[lead] = gv
        d_ref[lead] = d
        nm_ref[lead] = nm
        nv_ref[lead] = nv

    blk = pl.BlockSpec((1,) * len(idx) + (tr, C), lambda r: idx + (r, 0))
    any_space = pl.BlockSpec(memory_space=pl.ANY)
    in_specs, args, aliases = [blk, blk, blk, pl.BlockSpec((tr, C), lambda r: (r, 0)), any_space], [w, m, v, g, g if after is None else after], {}
    if outs is not None:
        in_specs += [any_space] * 4
        args += list(outs)
        aliases = {5 + i: i for i in range(4)}
    out = pl.pallas_call(
        body, name=name, grid=(R // tr,),
        in_specs=in_specs, out_specs=[blk] * 4 + [pl.BlockSpec((8, LANES), lambda r: (0, 0))],
        out_shape=[SDS(w.shape, F32)] * 4 + [SDS((8, LANES), F32)],
        input_output_aliases=aliases, compiler_params=_params(("arbitrary",)),
    )(*args)
    return out[:4], out[4]


def adamw(w, g, m, v, name):
    R, C = w.shape
    tr = _row_tile(R)

    def body(w_ref, g_ref, m_ref, v_ref, d_ref, nm_ref, nv_ref):
        d_ref[...], nm_ref[...], nv_ref[...] = _adamw_math(w_ref[...], g_ref[...], m_ref[...], v_ref[...])

    spec = pl.BlockSpec((tr, C), lambda i: (i, 0))
    return pl.pallas_call(
        body, name=name, grid=(R // tr,),
        in_specs=[spec] * 4, out_specs=[spec] * 3,
        out_shape=[SDS((R, C), F32)] * 3,
        compiler_params=_params(("parallel",)),
    )(w, g, m, v)


HBM = pl.BlockSpec(memory_space=pl.ANY)
ROW_SPLIT = 4
PAIR_SPLIT = 1


def _position():
    x, y, c = lax.axis_index("x"), lax.axis_index("y"), lax.axis_index("c")
    return x, y, c, [(1 - x, y), (x, 1 - y), (1 - x, 1 - y)]


def _unique(items):
    arrays = []
    for a, _ in items:
        if not any(a is b for b in arrays):
            arrays.append(a)
    return arrays, [next(i for i, b in enumerate(arrays) if b is a) for a, _ in items]


def place_own(items, me, after, name):
    arrays, src_of = _unique(items)
    n = len(items)
    shapes = [a.shape[len(p):] for a, p in items]

    def body(me_ref, *refs):
        for t in range(n):
            refs[n + 1 + t][0] = refs[t][(0,) * len(items[t][1])]

    in_specs, out_specs = [], []
    for (a, p), shp in zip(items, shapes):
        blk = shp[:-2] + (shp[-2] // ROW_SPLIT, shp[-1])
        lead = (0,) * (len(shp) - 2)
        in_specs.append(pl.BlockSpec((1,) * len(p) + blk, functools.partial(lambda r, me_ref, p, lead: p + lead + (r, 0), p=p, lead=lead)))
        out_specs.append(pl.BlockSpec((1,) + blk, functools.partial(lambda r, me_ref, lead: (me_ref[0],) + lead + (r, 0), lead=lead)))
    in_specs.append(pl.BlockSpec(memory_space=pl.ANY))
    return pl.pallas_call(
        body, name=name,
        grid_spec=pltpu.PrefetchScalarGridSpec(num_scalar_prefetch=1, grid=(ROW_SPLIT,), in_specs=in_specs, out_specs=out_specs),
        out_shape=[SDS((N_CHIPS,) + tuple(shp), a.dtype) for shp, (a, _) in zip(shapes, items)],
        compiler_params=_params(("parallel",)),
    )(me, *[arrays[i] for i in src_of], after)


SEM = pl.BlockSpec(memory_space=pltpu.SEMAPHORE)
IN_HBM = pl.BlockSpec(memory_space=pltpu.HBM)
DATAFLOW = pltpu.SideEffectType.DATAFLOW_SIDE_EFFECTING


def split_start(bufs, plan, n_copies, after, name):
    n = len(bufs)

    def body(*refs):
        send, recv, token = refs[n + 1], refs[n + 2], refs[-1]
        x, y, c, chips = _position()
        for k, (src, dst, dev) in enumerate(plan(refs[:n], x, y, c, chips)):
            pltpu.make_async_remote_copy(src_ref=src, dst_ref=dst, send_sem=send.at[k], recv_sem=recv.at[k],
                                         device_id=dev, device_id_type=MESH).start()
        token[...] = jnp.zeros_like(token)

    out = pl.pallas_call(
        body, name=name,
        out_shape=(pltpu.SemaphoreType.DMA((n_copies,)), pltpu.SemaphoreType.DMA((n_copies,)),
                   *[pltpu.HBM(b.shape, b.dtype) for b in bufs], SDS((8, LANES), F32)),
        in_specs=[IN_HBM] * n + [pl.BlockSpec(memory_space=pl.ANY)],
        out_specs=(SEM, SEM, *[IN_HBM] * n, pl.BlockSpec(memory_space=pltpu.VMEM)),
        input_output_aliases={i: 2 + i for i in range(n)},
        compiler_params=pltpu.CompilerParams(has_side_effects=DATAFLOW),
    )(*[pltpu.with_memory_space_constraint(b, pltpu.HBM) for b in bufs], after)
    return out[0], out[1], list(out[2:2 + n]), out[-1]


def split_wait(send, recv, bufs, plan, after, name):
    n = len(bufs)

    def body(*refs):
        send_ref, recv_ref = refs[n], refs[n + 1]
        x, y, c, chips = _position()
        for k, (src, dst, dev) in enumerate(plan(refs[:n], x, y, c, chips)):
            cp = pltpu.make_async_remote_copy(src_ref=src, dst_ref=dst, send_sem=send_ref.at[k], recv_sem=recv_ref.at[k],
                                              device_id=dev, device_id_type=MESH)
            cp.wait_send()
            cp.wait_recv()

    return list(pl.pallas_call(
        body, name=name, out_shape=tuple(pltpu.HBM(b.shape, b.dtype) for b in bufs),
        in_specs=[IN_HBM] * n + [SEM, SEM, pl.BlockSpec(memory_space=pl.ANY)], out_specs=tuple([IN_HBM] * n),
        input_output_aliases={i: i for i in range(n)},
        compiler_params=pltpu.CompilerParams(has_side_effects=DATAFLOW),
    )(*bufs, send, recv, after))


def _gather_plan(shapes, landing):
    n = len(shapes)

    def plan(refs, x, y, c, chips):
        out = []
        for t in range(n):
            half = shapes[t][0] // 2
            rows = pl.ds(c * half, half)
            for cx, cy in chips:
                slot = 2 * cx + cy if landing else 2 * x + y
                out.append((refs[t].at[rows], refs[n + t].at[slot, rows], (cx, cy, c)))
        return out

    return plan


def gather_start(shards, placed, after, name):
    shapes = [s.shape for s in shards]
    send, recv, bufs, token = split_start(list(shards) + list(placed), _gather_plan(shapes, False), 3 * len(shards), after, name)
    return (send, recv, bufs, shapes), token


def gather_wait(state, after, name):
    send, recv, bufs, shapes = state
    return split_wait(send, recv, bufs, _gather_plan(shapes, True), after, name)[len(shapes):]


def gather_pass_on(placed, shapes, name):
    n = len(placed)

    def body(*refs):
        outs, send, recv = refs[n:2 * n], refs[2 * n], refs[2 * n + 1]
        x, y, c, chips = _position()
        cps = []
        for t in range(n):
            half = shapes[t][0] // 2
            for j, (cx, cy) in enumerate(chips):
                piece = outs[t].at[2 * cx + cy, pl.ds(c * half, half)]
                cp = pltpu.make_async_remote_copy(src_ref=piece, dst_ref=piece, send_sem=send.at[3 * t + j], recv_sem=recv.at[3 * t + j],
                                                  device_id=(x, y, 1 - c), device_id_type=MESH)
                cp.start()
                cps.append(cp)
        for t in range(n):
            half = shapes[t][0] // 2
            for j, (cx, cy) in enumerate(chips):
                piece = outs[t].at[2 * cx + cy, pl.ds((1 - c) * half, half)]
                pltpu.make_async_remote_copy(src_ref=piece, dst_ref=piece, send_sem=send.at[3 * t + j], recv_sem=recv.at[3 * t + j],
                                             device_id=(x, y, 1 - c), device_id_type=MESH).wait_recv()
        for cp in cps:
            cp.wait_send()

    return pl.pallas_call(
        body, name=name,
        in_specs=[HBM] * n, out_specs=[HBM] * n,
        out_shape=[SDS(p.shape, p.dtype) for p in placed],
        input_output_aliases={t: t for t in range(n)},
        scratch_shapes=[pltpu.SemaphoreType.DMA((3 * n,))] * 2,
    )(*placed)


def _flip(k, x, y, c):
    return ((1 - x) if k & 4 else x, (1 - y) if k & 2 else y, (1 - c) if k & 1 else c)


def small_allgather(v, reduce):
    R, C = v.shape

    def body(v_ref, o_ref, *scratch):
        if reduce:
            buf, send, recv = scratch
        else:
            buf, (send, recv) = o_ref, scratch
        x, y, c, _ = _position()
        me = 4 * x + 2 * y + c
        buf[me] = v_ref[...]
        sends = []
        for k in range(1, N_DEV):
            cp = pltpu.make_async_remote_copy(src_ref=v_ref, dst_ref=buf.at[me], send_sem=send.at[k - 1], recv_sem=recv.at[k - 1],
                                              device_id=_flip(k, x, y, c), device_id_type=MESH)
            cp.start()
            sends.append(cp)
        for k in range(1, N_DEV):
            px, py, pc = _flip(k, x, y, c)
            pltpu.make_async_remote_copy(src_ref=v_ref, dst_ref=buf.at[4 * px + 2 * py + pc], send_sem=send.at[k - 1],
                                         recv_sem=recv.at[k - 1], device_id=(px, py, pc), device_id_type=MESH).wait_recv()
        for cp in sends:
            cp.wait_send()
        if reduce:
            acc = buf[0]
            for i in range(1, N_DEV):
                acc = acc + buf[i]
            o_ref[...] = acc

    vm = pl.BlockSpec(memory_space=pltpu.VMEM)
    sems = [pltpu.SemaphoreType.DMA((N_DEV - 1,)), pltpu.SemaphoreType.DMA((N_DEV - 1,))]
    return pl.pallas_call(
        body, name="small_allreduce" if reduce else "small_allgather",
        in_specs=[vm], out_specs=vm,
        out_shape=SDS((R, C) if reduce else (N_DEV, R, C), F32),
        scratch_shapes=([pltpu.VMEM((N_DEV, R, C), F32)] if reduce else []) + sems,
    )(v)


def rs_exchange_sibling(gs):
    n = len(gs)

    def body(*refs):
        ins, outs, send, recv = refs[:n], refs[n:2 * n], refs[2 * n], refs[2 * n + 1]
        x, y, c, _ = _position()
        cps = []
        for t in range(n):
            cp = pltpu.make_async_remote_copy(src_ref=ins[t].at[:, 1 - c], dst_ref=outs[t], send_sem=send.at[t], recv_sem=recv.at[t],
                                              device_id=(x, y, 1 - c), device_id_type=MESH)
            cp.start()
            cps.append(cp)
        for cp in cps:
            cp.wait()

    return pl.pallas_call(
        body, name="rs_exchange_sibling", in_specs=[HBM] * n, out_specs=[HBM] * n,
        out_shape=[SDS((g.shape[0],) + g.shape[2:], g.dtype) for g in gs],
        scratch_shapes=[pltpu.SemaphoreType.DMA((n,)), pltpu.SemaphoreType.DMA((n,))],
    )(*gs)


def rs_pair_add(gs, rs, c):
    n = len(gs)

    def body(c_ref, *refs):
        for t in range(n):
            refs[2 * n + t][0] = (refs[t][0, 0].astype(F32) + refs[n + t][0].astype(F32)).astype(BF16)

    in_specs, out_specs, out_shape = [], [], []
    for g in gs:
        _, _, h, C = g.shape
        in_specs.append(pl.BlockSpec((1, 1, h // PAIR_SPLIT, C), lambda j, r, c_ref: (j, c_ref[0], r, 0)))
    for g in gs:
        _, _, h, C = g.shape
        spec = pl.BlockSpec((1, h // PAIR_SPLIT, C), lambda j, r, c_ref: (j, r, 0))
        in_specs.append(spec)
        out_specs.append(spec)
        out_shape.append(SDS((N_CHIPS, h, C), BF16))
    return pl.pallas_call(
        body, name="rs_pair_add",
        grid_spec=pltpu.PrefetchScalarGridSpec(num_scalar_prefetch=1, grid=(N_CHIPS, PAIR_SPLIT), in_specs=in_specs, out_specs=out_specs),
        out_shape=out_shape, compiler_params=_params(("parallel", "parallel")),
    )(c, *gs, *rs)


def _rs_plan(n):
    def plan(refs, x, y, c, chips):
        return [(refs[t].at[2 * cx + cy], refs[n + t].at[j], (cx, cy, c)) for t in range(n) for j, (cx, cy) in enumerate(chips)]

    return plan


def rs_chip_add(ps, qs, me_c):
    n = len(ps)

    def body(me_ref, *refs):
        for t in range(n):
            q = refs[n + t]
            refs[2 * n + t][0] = ((refs[t][0].astype(F32) + q[0].astype(F32)) + q[1].astype(F32)) + q[2].astype(F32)

    in_specs, out_specs, out_shape = [], [], []
    for p in ps:
        _, h, C = p.shape
        in_specs.append(pl.BlockSpec((1, h // ROW_SPLIT, C), lambda r, me_ref: (me_ref[0], r, 0)))
    for p in ps:
        _, h, C = p.shape
        in_specs.append(pl.BlockSpec((3, h // ROW_SPLIT, C), lambda r, me_ref: (0, r, 0)))
        out_specs.append(pl.BlockSpec((1, h // ROW_SPLIT, C), lambda r, me_ref: (me_ref[1], r, 0)))
        out_shape.append(SDS((2, h, C), F32))
    return pl.pallas_call(
        body, name="rs_chip_add",
        grid_spec=pltpu.PrefetchScalarGridSpec(num_scalar_prefetch=1, grid=(ROW_SPLIT,), in_specs=in_specs, out_specs=out_specs),
        out_shape=out_shape, compiler_params=_params(("parallel",)),
    )(me_c, *ps, *qs)


def rs_share(rs):
    n = len(rs)

    def body(*refs):
        outs, send, recv = refs[n:2 * n], refs[2 * n], refs[2 * n + 1]
        x, y, c, _ = _position()
        cps = []
        for t in range(n):
            cp = pltpu.make_async_remote_copy(src_ref=outs[t].at[c], dst_ref=outs[t].at[c], send_sem=send.at[t], recv_sem=recv.at[t],
                                              device_id=(x, y, 1 - c), device_id_type=MESH)
            cp.start()
            cps.append(cp)
        for cp in cps:
            cp.wait()

    return pl.pallas_call(
        body, name="rs_share", in_specs=[HBM] * n, out_specs=[HBM] * n,
        out_shape=[SDS(r.shape, r.dtype) for r in rs],
        input_output_aliases={t: t for t in range(n)},
        scratch_shapes=[pltpu.SemaphoreType.DMA((n,))] * 2,
    )(*rs)


def rs_begin(gs, after, name):
    c = lax.axis_index("c")
    n = len(gs)
    g5 = [g.reshape(N_CHIPS, 2, g.shape[1] // 2, g.shape[2]) for g in gs]
    from_sibling = rs_exchange_sibling(g5)
    pair = rs_pair_add(g5, from_sibling, jnp.reshape(c, (1,)).astype(jnp.int32))
    lands = [lax.empty((3,) + p.shape[1:], p.dtype) for p in pair]
    send, recv, bufs, token = split_start(list(pair) + lands, _rs_plan(n), 3 * n, from_sibling[0] if after is None else after, name)
    return (send, recv, bufs, [g.shape for g in gs]), token


def rs_end(state, after, name):
    x, y, c = lax.axis_index("x"), lax.axis_index("y"), lax.axis_index("c")
    send, recv, bufs, shapes = state
    n = len(shapes)
    bufs = split_wait(send, recv, bufs, _rs_plan(n), after, name)
    half = rs_chip_add(bufs[:n], bufs[n:], jnp.stack([2 * x + y, c]).astype(jnp.int32))
    both = rs_share(half)
    return [b.reshape(s[1], s[2]) for b, s in zip(both, shapes)]


def _pad_last(a, n):
    return jnp.pad(a, [(0, 0)] * (a.ndim - 1) + [(0, n - a.shape[-1])])


def _heads_to_groups(w):
    k = w.shape[0]
    return _pad_last(w.reshape(k, ML_HEADS, ML_HEAD_DIM).transpose(1, 0, 2), GROUP)


def _groups_to_heads(g):
    return g[:, :, :ML_HEAD_DIM].transpose(1, 0, 2).reshape(g.shape[1], D_TOK)


def _cols_to_groups(w):
    k, n = w.shape
    return w.reshape(k, n // GROUP, GROUP).transpose(1, 0, 2)


def _groups_to_cols(g):
    n, k, _ = g.shape
    return g.transpose(1, 0, 2).reshape(k, n * GROUP)


def _chips_to_cols(a):
    return a.transpose(1, 0, 2).reshape(a.shape[1], -1)


def _cols_to_chips(w):
    k, n = w.shape
    return w.reshape(k, N_CHIPS, n // N_CHIPS).transpose(1, 0, 2)


def _mlstm_in_groups(w):
    parts = [_heads_to_groups(w[:, i * D_TOK:(i + 1) * D_TOK]) for i in range(4)]
    gates = _pad_last(w[:, 4 * D_TOK:4 * D_TOK + 2 * ML_HEADS], GROUP)[None]
    qmem = w[:, 4 * D_TOK + 2 * ML_HEADS:][None]
    return jnp.concatenate(parts + [qmem, gates], axis=0)


def _mlstm_in_ungroup(g):
    parts = [_groups_to_heads(g[4 * i:4 * i + 4]) for i in range(4)]
    return jnp.concatenate(parts + [g[17][:, :2 * ML_HEADS], g[16]], axis=1)


def _taps_to_groups(w, width):
    taps = w.shape[0]
    g = _pad_last(w.reshape(taps, -1, width), GROUP).transpose(1, 0, 2)
    return jnp.pad(g, ((0, 0), (0, 8 - taps), (0, 0)))


def _groups_to_taps(g, taps, width):
    return g[:, :taps, :width].transpose(1, 0, 2).reshape(taps, -1)


SMALL_IN_COLS = 384
SMALL_OUT_COLS = 1536
SECTION = 8


class _Gathered:
    def __init__(self, make_src, groups, me, after):
        self.groups, self.states, self.ready = groups, [], {}
        self.group_of = {k: gi for gi, g in enumerate(groups) for k in g}
        token, self.first = after, None
        for gi, g in enumerate(groups):
            srcs = [make_src(k, None if gi == 0 else token[0:1, 0:1]) for k in g]
            placed = place_own([(a, ()) for a in srcs], me, token, f"place_own_{gi}")
            state, token = gather_start(srcs, placed, token, f"gather_start_{gi}")
            self.states.append(state)
            if gi == 0:
                self.first = token[0:1, 0:1]
        self.started = token

    def _get(self, key, after):
        gi = self.group_of[key]
        if gi not in self.ready:
            got = gather_wait(self.states[gi], after if gi else self.started, f"gather_wait_{gi}")
            self.ready[gi] = dict(zip(self.groups[gi], gather_pass_on(got, self.states[gi][3], f"gather_pass_on_{gi}")))
        return self.ready[gi][key]

    def ffn(self, l, i, after):
        return tuple(self._get((n, l, i), after) for n in ("wg", "wu", "wd"))

    def mixer(self, l, after):
        win = _chips_to_cols(self._get(("win", l), after))
        win = _cols_to_groups(win) if l % 2 == 0 else _mlstm_in_groups(win)
        wkv = _cols_to_groups(self._get(("wkv", l), after).reshape(D_MODEL, 2 * D_XA))
        wout = self._get(("wout", l), after)
        if l % 2:
            wout = wout.reshape(D_MODEL, D_MODEL)
            tok = jnp.pad(wout[:D_TOK].reshape(ML_HEADS, ML_HEAD_DIM, D_MODEL), ((0, 0), (0, GROUP - ML_HEAD_DIM), (0, 0)))
            wout = jnp.concatenate([tok, wout[D_TOK:][None]], axis=0)
        return win, wkv, wout


class _GradSink:
    def __init__(self, apply):
        self.queue, self.apply, self.count, self.done = [], apply, 0, None

    @staticmethod
    def _by_chip(key, g):
        if key[0] == "wkv":
            return _groups_to_cols(g).reshape(N_CHIPS, D_MODEL // N_CHIPS, 2 * D_XA)
        if key[0] == "win":
            return _cols_to_chips(_groups_to_cols(g) if key[1] % 2 == 0 else _mlstm_in_ungroup(g))
        if key[0] == "wout" and key[1] % 2:
            full = jnp.concatenate([g[:ML_HEADS, :ML_HEAD_DIM].reshape(D_TOK, D_MODEL), g[ML_HEADS]], axis=0)
            return full.reshape(N_CHIPS, D_MODEL // N_CHIPS, D_MODEL)
        return g

    def push(self, grads):
        keys = list(grads)
        state, token = rs_begin([self._by_chip(k, grads[k]) for k in keys], self.done, f"rs_start_{self.count}")
        if self.queue:
            self._finish(token)
        self.queue.append((keys, state, self.count))
        self.count += 1
        return token

    def flush(self):
        self._finish(self.done)

    def _finish(self, after):
        keys, state, i = self.queue.pop(0)
        for key, g in zip(keys, rs_end(state, after, f"rs_wait_{i}")):
            self.done = self.apply(key, g, self.done)


def _local_step(x, mem, tgt, P, weights, sink):
    memb = mem.astype(BF16)
    saved = []
    pin0 = getattr(weights, "first", None)
    X, Xb = x, (x if pin0 is None else x + pin0).astype(BF16)
    after = Xb
    for l in range(DEPTH):
        s = {}
        s["x0b"] = Xb
        s["wa"] = weights.ffn(l, 0, after)
        s["g1a"], s["u1a"], s["ha"], s["z1"], X1, X1b = ffn_fwd(Xb, X, *s["wa"], P["ln_g"][l][0], P["ln_b"][l][0])
        s["x1b"] = X1b
        s["wm"] = win, wkv, wout = weights.mixer(l, X1b)
        u = proj(X1b, win, "mixer_in")
        kv = proj(memb, wkv, "mem_kv")
        s["u"], s["kv"] = u, kv
        if l % 2 == 0:
            tok = conv_mixer_fwd(u, P["convw"])
            qg = 9
        else:
            s["qk"] = qk_conv_fwd(u, P["qkw"])
            s["hm"], s["cst"], s["mst"] = mlstm_fwd(s["qk"], u, P["bg"])
            tok = head_norm_fwd(s["hm"], u, P["hg"])
            qg = 16
        s["m"] = xattn_fwd(u, qg, kv, tok)
        s["z2"], X2, X2b = contract_ln(s["m"], wout, X1, P["ln_g"][l][1], P["ln_b"][l][1], 1.0, "mixer_out_ln")
        s["x2b"] = X2b
        s["wb"] = weights.ffn(l, 1, X2b)
        s["g1b"], s["u1b"], s["hb"], s["z3"], X, Xb = ffn_fwd(X2b, X2, *s["wb"], P["ln_g"][l][2], P["ln_b"][l][2])
        after = Xb
        saved.append(s)

    loss, dX = loss_grad(X, tgt)

    G = {"ln_g": [[None] * 3 for _ in range(DEPTH)], "ln_b": [[None] * 3 for _ in range(DEPTH)]}
    pin = [jnp.zeros((1, 1), F32)]

    def ffn_backward(l, i, dX, z, xinb, g1, u1, h, w):
        k = 2 * i
        dgb, dub, dx, dyb, G["ln_g"][l][k], G["ln_b"][l][k] = ffn_bwd(dX, z, P["ln_g"][l][k] + pin[0], w[2], w[0], w[1], g1, u1)
        grads = {("wd", l, i): wgrad(h, dyb, BF16, "wgrad_down"), ("wg", l, i): wgrad(dgb, xinb, BF16, "wgrad_gate"),
                 ("wu", l, i): wgrad(dub, xinb, BF16, "wgrad_up")}
        return dx, grads

    for l in reversed(range(DEPTH)):
        s = saved[l]
        win, wkv, wout = s["wm"]
        dX, grads = ffn_backward(l, 1, dX, s["z3"], s["x2b"], s["g1b"], s["u1b"], s["hb"], s["wb"])
        dm, dz2, dz2b, G["ln_g"][l][1], G["ln_b"][l][1] = mixer_out_bwd(dX, s["z2"], P["ln_g"][l][1], wout)
        grads[("wout", l)] = wgrad(s["m"], dz2b, BF16, "wgrad_out")
        u, kv = s["u"], s["kv"]
        if l % 2 == 0:
            db, dc, dxi, G["convw"] = conv_mixer_bwd(u, P["convw"], dm)
            dq, dkv = xattn_bwd(u, 9, kv, dm, 3)
            du = jnp.concatenate([db, dc, dxi, dq], axis=0)
        else:
            dh, du, G["hg"] = head_norm_bwd(s["hm"], u, P["hg"], dm)
            dqk, du, dgate, G["bg"] = mlstm_bwd(s["qk"], u, P["bg"], s["cst"], s["mst"], dh, du)
            du, G["qkw"] = qk_conv_bwd(u, P["qkw"], dqk, du)
            du, dkv = xattn_bwd(u, 16, kv, dm, 4, du, dgate)
        grads[("win", l)] = wgrad(s["x1b"], du, BF16, "wgrad_in")
        grads[("wkv", l)] = wgrad(memb, dkv.astype(BF16), BF16, "wgrad_kv")
        dX = contract_t(du, win, dz2, "mixer_in_bwd")
        pin[0] = sink.push(grads)[0:1, 0:1]
        dX, grads = ffn_backward(l, 0, dX, s["z1"], s["x0b"], s["g1a"], s["u1a"], s["ha"], s["wa"])
        pin[0] = sink.push(grads)[0:1, 0:1]
    sink.flush()
    return loss, dX, G


def kernel(x, mem, ln_g, ln_b, ffn_w_gate, ffn_w_up, ffn_w_down, w_kv_mem, w_out, w_in_conv, conv_w, w_in_mlstm, b_gates, qk_conv_w, head_norm_g, loss_target, m_ln_g, m_ln_b, m_ffn_w_gate, m_ffn_w_up, m_ffn_w_down, m_w_kv_mem, m_w_out, m_w_in_conv, m_conv_w, m_w_in_mlstm, m_b_gates, m_qk_conv_w, m_head_norm_g, v_ln_g, v_ln_b, v_ffn_w_gate, v_ffn_w_up, v_ffn_w_down, v_w_kv_mem, v_w_out, v_w_in_conv, v_conv_w, v_w_in_mlstm, v_b_gates, v_qk_conv_w, v_head_norm_g):
    cx, cy = lax.axis_index("x"), lax.axis_index("y")
    chip = 2 * cx + cy

    def make_src(key, pin):
        if key[0] in ("wg", "wu"):
            w = jnp.swapaxes((ffn_w_gate if key[0] == "wg" else ffn_w_up)[key[1], key[2]], 0, 1)
        elif key[0] == "wd":
            w = ffn_w_down[key[1], key[2]]
        elif key[0] == "win":
            w = (w_in_conv, w_in_mlstm)[key[1]][0]
        else:
            w = (w_kv_mem if key[0] == "wkv" else w_out)[key[1]]
        return (w if pin is None else w + pin).astype(BF16)

    ffn_keys = lambda l, i: [("wg", l, i), ("wu", l, i), ("wd", l, i)]
    mixer_keys = lambda l: [("win", l), ("wkv", l), ("wout", l)]
    groups = [ffn_keys(0, 0), mixer_keys(0) + mixer_keys(1), ffn_keys(0, 1), ffn_keys(1, 0), ffn_keys(1, 1)]
    def section(a, width):
        a = a.reshape(-1, a.shape[-1])
        return jnp.pad(a, ((0, SECTION - a.shape[0]), (0, width - a.shape[1])))

    small = jnp.concatenate([section(a, SMALL_IN_COLS) for a in (ln_g, ln_b, conv_w, qk_conv_w)], axis=0)
    smalls = small_allgather(small, reduce=False)
    gathered = _Gathered(make_src, groups, jnp.reshape(chip, (1,)).astype(jnp.int32), smalls)
    smalls = smalls[0::2]
    ln_g_full = _chips_to_cols(smalls[:, 0:6, 0:256]).reshape(DEPTH, 3, 1, D_MODEL)
    ln_b_full = _chips_to_cols(smalls[:, 8:14, 0:256]).reshape(DEPTH, 3, 1, D_MODEL)
    conv_w_full = _chips_to_cols(smalls[:, 16:19, 0:192])
    qk_w_full = _chips_to_cols(smalls[:, 24:28, 0:384])

    P = {"ln_g": ln_g_full, "ln_b": ln_b_full, "convw": _taps_to_groups(conv_w_full, GROUP),
         "qkw": _taps_to_groups(qk_w_full, ML_HEAD_DIM), "bg": _pad_last(b_gates, GROUP),
         "hg": _pad_last(head_norm_g[0], GROUP)[:, None, :]}

    weights = {"ln_g": ln_g, "ln_b": ln_b, "ffn_w_gate": ffn_w_gate, "ffn_w_up": ffn_w_up, "ffn_w_down": ffn_w_down,
               "w_kv_mem": w_kv_mem, "w_out": w_out, "w_in_conv": w_in_conv, "conv_w": conv_w, "w_in_mlstm": w_in_mlstm,
               "b_gates": b_gates, "qk_conv_w": qk_conv_w, "head_norm_g": head_norm_g}
    ms = {"ln_g": m_ln_g, "ln_b": m_ln_b, "ffn_w_gate": m_ffn_w_gate, "ffn_w_up": m_ffn_w_up, "ffn_w_down": m_ffn_w_down,
          "w_kv_mem": m_w_kv_mem, "w_out": m_w_out, "w_in_conv": m_w_in_conv, "conv_w": m_conv_w, "w_in_mlstm": m_w_in_mlstm,
          "b_gates": m_b_gates, "qk_conv_w": m_qk_conv_w, "head_norm_g": m_head_norm_g}
    vs = {"ln_g": v_ln_g, "ln_b": v_ln_b, "ffn_w_gate": v_ffn_w_gate, "ffn_w_up": v_ffn_w_up, "ffn_w_down": v_ffn_w_down,
          "w_kv_mem": v_w_kv_mem, "w_out": v_w_out, "w_in_conv": v_w_in_conv, "conv_w": v_conv_w, "w_in_mlstm": v_w_in_mlstm,
          "b_gates": v_b_gates, "qk_conv_w": v_qk_conv_w, "head_norm_g": v_head_norm_g}
    names = list(weights)
    owner = {"wg": ("ffn_w_gate", True), "wu": ("ffn_w_up", True), "wd": ("ffn_w_down", False), "wkv": ("w_kv_mem", False),
             "wout": ("w_out", False), "win": None}
    updated = {}

    def apply(key, g, after):
        name, transposed = owner[key[0]] or (("w_in_conv", "w_in_mlstm")[key[1]], False)
        idx = (0,) if key[0] == "win" else tuple(key[1:])
        view = (lambda a: jnp.swapaxes(a, -1, -2)) if transposed else (lambda a: a)
        updated[name], token = adamw_into(view(weights[name]), view(ms[name]), view(vs[name]), g, updated.get(name), idx, after,
                                          "adamw_" + name + "_" + "_".join(map(str, idx)))
        return token

    sink = _GradSink(apply)
    loss, grad_x, G = _local_step(x[0], mem[0], loss_target[0], P, gathered, sink)

    dln_g = jnp.concatenate([G["ln_g"][l][k] for l in range(DEPTH) for k in range(3)], axis=0)
    dln_b = jnp.concatenate([G["ln_b"][l][k] for l in range(DEPTH) for k in range(3)], axis=0)
    lane = lax.broadcasted_iota(jnp.int32, (1, GROUP), 1)
    misc = jnp.where(lane < 8, G["bg"], 0.0) + jnp.where(lane == 8, loss, 0.0) + sink.done[0:1, 0:1]
    parts = (dln_g, dln_b, _groups_to_taps(G["convw"], 3, GROUP), misc, _groups_to_taps(G["qkw"], 4, ML_HEAD_DIM),
             G["hg"][:, 0, :ML_HEAD_DIM])
    tot = small_allgather(jnp.concatenate([section(a, SMALL_OUT_COLS) for a in parts], axis=0), reduce=True)
    loss_total = tot[24, 8]

    small_grads = {
        "ln_g": lax.dynamic_slice(tot[0:6, 0:D_MODEL], (0, chip * 256), (6, 256)).reshape(DEPTH, 3, 256),
        "ln_b": lax.dynamic_slice(tot[8:14, 0:D_MODEL], (0, chip * 256), (6, 256)).reshape(DEPTH, 3, 256),
        "conv_w": lax.dynamic_slice(tot[16:19, 0:D_TOK], (0, chip * 192), (3, 192))[None],
        "b_gates": tot[24:25, 0:8],
        "qk_conv_w": lax.dynamic_slice(tot[32:36, 0:2 * D_TOK], (0, chip * 384), (4, 384))[None],
        "head_norm_g": tot[40:44, 0:ML_HEAD_DIM][None],
    }
    grads, deltas, new_m, new_v = [], [], [], []
    for nme in names:
        if nme in updated:
            back = (lambda a: jnp.swapaxes(a, -1, -2)) if nme in ("ffn_w_gate", "ffn_w_up") else (lambda a: a)
            g, d, nm, nv = (back(a) for a in updated[nme])
        else:
            w, g = weights[nme], small_grads[nme]
            two = (math.prod(w.shape[:-1]), w.shape[-1])
            d, nm, nv = (a.reshape(w.shape) for a in adamw(w.reshape(two), g.reshape(two), ms[nme].reshape(two),
                                                           vs[nme].reshape(two), "adamw_" + nme))
        grads.append(g)
        deltas.append(d)
        new_m.append(nm)
        new_v.append(nv)
    return (loss_total, grad_x[None], *grads, *deltas, *new_m, *new_v)
```

```python
import functools
import math

import jax
import jax.numpy as jnp
from jax import lax
from jax.experimental import pallas as pl
from jax.experimental.pallas import tpu as pltpu

F32 = jnp.float32
BF16 = jnp.bfloat16
SDS = jax.ShapeDtypeStruct

D_MODEL = 1024
DEPTH = 2
N_MEM = 256
XA_HEADS = 4
XA_HEAD_DIM = 64
D_XA = 256
D_TOK = 768
ML_HEADS = 4
ML_HEAD_DIM = 192
ML_CHUNK = 64
D_FF = 2816
LN_EPS = 1e-5
ALPHA = (2.0 * DEPTH) ** 0.25
N_CHIPS = 4
N_DEV = 8
FF_SHARD = D_FF // N_CHIPS
GROUP = 256
NEG = -1e30

ADAM_LR = 0.001
ADAM_B1 = 0.9
ADAM_B2 = 0.999
ADAM_EPS = 1e-08
ADAM_WD = 0.01
ADAM_STEP = 10

VMEM_LIMIT = 56 * 1024 * 1024

NN = ((1,), (0,))
NT = ((1,), (1,))
TN = ((0,), (0,))
MESH = pl.DeviceIdType.MESH


def _dot(a, b, dims):
    return lax.dot_general(a, b, (dims, ((), ())), preferred_element_type=F32)


def _bdot(a, b, ca, cb):
    dims = (((ca,), (cb,)), ((0,), (0,)))
    ah, bh = a.astype(BF16), b.astype(BF16)
    al, bl = (a - ah.astype(F32)).astype(BF16), (b - bh.astype(F32)).astype(BF16)
    dot = functools.partial(lax.dot_general, dimension_numbers=dims, preferred_element_type=F32)
    return dot(ah, bh) + dot(al, bh) + dot(ah, bl)


def _bdot1(a, b, ca, cb):
    return lax.dot_general(a.astype(BF16), b.astype(BF16), (((ca,), (cb,)), ((0,), (0,))), preferred_element_type=F32)


def _sigmoid(x):
    return 1.0 / (1.0 + jnp.exp(-x))


def _params(sem, vmem=VMEM_LIMIT):
    return pltpu.CompilerParams(dimension_semantics=sem, vmem_limit_bytes=vmem)


def _tile(n, want):
    t = min(n, want)
    assert n % t == 0, (n, t)
    return t


def _layer_norm(z, gamma, beta):
    mu = jnp.mean(z, axis=-1, keepdims=True)
    zc = z - mu
    var = jnp.mean(zc * zc, axis=-1, keepdims=True)
    return zc * lax.rsqrt(var + LN_EPS) * gamma + beta


def _column_halves(n):
    mid = -(-n // (2 * 128)) * 128
    return ((0, mid), (mid, n))


def _resident(shape):
    return pl.BlockSpec(shape, lambda *_: (0,) * len(shape), pipeline_mode=pl.Buffered(1))


def _group_block(G, want):
    return max(d for d in range(1, max(1, min(G, want)) + 1) if G % d == 0)


def ffn_fwd(xb, x, wg, wu, wd, gamma, beta):
    S, K = xb.shape
    G, N, _ = wg.shape
    ts = _tile(S, 512)

    def body(xb_ref, x_ref, wg_ref, wu_ref, wd_ref, gm_ref, bt_ref, g_ref, u_ref, h_ref, z_ref, xn_ref, xnb_ref):
        j = pl.program_id(1)
        xv = xb_ref[...]
        g = _dot(xv, wg_ref[j], NT)
        u = _dot(xv, wu_ref[j], NT)
        h = (g * _sigmoid(g) * u).astype(BF16)
        g_ref[0] = g.astype(BF16)
        u_ref[0] = u.astype(BF16)
        h_ref[0] = h
        y = _dot(h, wd_ref[j], NN)

        @pl.when(j == 0)
        def _():
            z_ref[...] = y

        @pl.when(j > 0)
        def _():
            z_ref[...] += y

        @pl.when(j == G - 1)
        def _():
            z = ALPHA * x_ref[...] + 0.5 * z_ref[...]
            xn = _layer_norm(z, gm_ref[...], bt_ref[...])
            z_ref[...] = z
            xn_ref[...] = xn
            xnb_ref[...] = xn.astype(BF16)

    row = pl.BlockSpec((ts, K), lambda s, j: (s, 0))
    vec = pl.BlockSpec((1, K), lambda s, j: (0, 0))
    wspec = _resident((G, N, K))
    ospec = pl.BlockSpec((1, ts, N), lambda s, j: (j, s, 0))
    return pl.pallas_call(
        body, name="ffn_fwd", grid=(S // ts, G),
        in_specs=[row, row, wspec, wspec, wspec, vec, vec],
        out_specs=[ospec, ospec, ospec, row, row, row],
        out_shape=[SDS((G, S, N), BF16), SDS((G, S, N), BF16), SDS((G, S, N), BF16),
                   SDS((S, K), F32), SDS((S, K), F32), SDS((S, K), BF16)],
        compiler_params=_params(("parallel", "arbitrary")),
    )(xb, x, wg, wu, wd, gamma, beta)


def proj(xb, w, name):
    S, K = xb.shape
    G, _, N = w.shape
    ts = _tile(S, 1024)
    gb = _group_block(G, 6)

    def body(x_ref, w_ref, y_ref):
        xv = x_ref[...]
        for j in range(gb):
            y_ref[j] = _dot(xv, w_ref[j], NN)

    return pl.pallas_call(
        body, name=name, grid=(S // ts, G // gb),
        in_specs=[pl.BlockSpec((ts, K), lambda s, g: (s, 0)), pl.BlockSpec((gb, K, N), lambda s, g: (g, 0, 0))],
        out_specs=pl.BlockSpec((gb, ts, N), lambda s, g: (g, s, 0)),
        out_shape=SDS((G, S, N), F32),
        compiler_params=_params(("parallel", "parallel")),
    )(xb, w)


def contract_ln(a, w, xres, gamma, beta, scale, name):
    G, S, Kg = a.shape
    N = w.shape[2]
    ts = _tile(S, 1024)

    def body(a_ref, w_ref, x_ref, g_ref, b_ref, z_ref, xn_ref, xb_ref):
        acc = _dot(a_ref[0], w_ref[0], NN)
        for j in range(1, G):
            acc = acc + _dot(a_ref[j], w_ref[j], NN)
        z = ALPHA * x_ref[...] + scale * acc
        xn = _layer_norm(z, g_ref[...], b_ref[...])
        z_ref[...] = z
        xn_ref[...] = xn
        xb_ref[...] = xn.astype(BF16)

    row = pl.BlockSpec((ts, N), lambda s: (s, 0))
    vec = pl.BlockSpec((1, N), lambda s: (0, 0))
    return pl.pallas_call(
        body, name=name, grid=(S // ts,),
        in_specs=[pl.BlockSpec((G, ts, Kg), lambda s: (0, s, 0)), pl.BlockSpec((G, Kg, N), lambda s: (0, 0, 0)), row, vec, vec],
        out_specs=[row, row, row],
        out_shape=[SDS((S, N), F32), SDS((S, N), F32), SDS((S, N), BF16)],
        compiler_params=_params(("parallel",)),
    )(a, w, xres, gamma, beta)


def _layer_norm_bwd(dx, z, gamma):
    mu = jnp.mean(z, axis=-1, keepdims=True)
    zc = z - mu
    var = jnp.mean(zc * zc, axis=-1, keepdims=True)
    rstd = lax.rsqrt(var + LN_EPS)
    xhat = zc * rstd
    dxh = dx * gamma
    m1 = jnp.mean(dxh, axis=-1, keepdims=True)
    m2 = jnp.mean(dxh * xhat, axis=-1, keepdims=True)
    return rstd * (dxh - m1 - xhat * m2), jnp.sum(dx * xhat, axis=0, keepdims=True), jnp.sum(dx, axis=0, keepdims=True)


def ffn_bwd(dxn, z, gamma, wd, wg, wu, g1, u1):
    S, K = dxn.shape
    G, N, _ = wd.shape
    ts = _tile(S, 512)

    def body(dxn_ref, z_ref, gm_ref, wd_ref, wg_ref, wu_ref, g_ref, u_ref, dg_ref, du_ref, dx_ref, dy_ref, dgm_ref, dbt_ref):
        s, j = pl.program_id(0), pl.program_id(1)

        @pl.when((s == 0) & (j == 0))
        def _():
            dgm_ref[...] = jnp.zeros_like(dgm_ref)
            dbt_ref[...] = jnp.zeros_like(dbt_ref)

        @pl.when(j == 0)
        def _():
            dz, dgm, dbt = _layer_norm_bwd(dxn_ref[...], z_ref[...], gm_ref[...])
            dgm_ref[...] += dgm
            dbt_ref[...] += dbt
            dx_ref[...] = ALPHA * dz
            dy_ref[...] = (0.5 * dz).astype(BF16)

        dy = dy_ref[...]
        part = None
        for a, b in _column_halves(N):
            dh = _dot(dy, wd_ref[j, a:b, :], NT)
            g = g_ref[0, :, a:b].astype(F32)
            sig = _sigmoid(g)
            dg = (dh * u_ref[0, :, a:b].astype(F32) * (sig * (1.0 + g * (1.0 - sig)))).astype(BF16)
            du = (dh * (g * sig)).astype(BF16)
            dg_ref[0, :, a:b] = dg
            du_ref[0, :, a:b] = du
            p = _dot(dg, wg_ref[j, a:b, :], NN) + _dot(du, wu_ref[j, a:b, :], NN)
            part = p if part is None else part + p
        dx_ref[...] += part

    row = pl.BlockSpec((ts, K), lambda s, j: (s, 0))
    vec = pl.BlockSpec((1, K), lambda s, j: (0, 0))
    gspec = pl.BlockSpec((1, ts, N), lambda s, j: (j, s, 0))
    wspec = _resident((G, N, K))
    return pl.pallas_call(
        body, name="ffn_bwd", grid=(S // ts, G),
        in_specs=[row, row, vec, wspec, wspec, wspec, gspec, gspec],
        out_specs=[gspec, gspec, row, row, vec, vec],
        out_shape=[SDS((G, S, N), BF16), SDS((G, S, N), BF16), SDS((S, K), F32), SDS((S, K), BF16),
                   SDS((1, K), F32), SDS((1, K), F32)],
        compiler_params=_params(("arbitrary", "arbitrary")),
    )(dxn, z, gamma, wd, wg, wu, g1, u1)


def mixer_out_bwd(dxn, z, gamma, w):
    S, N = dxn.shape
    G, Kg, _ = w.shape
    ts = _tile(S, 512)

    def body(dxn_ref, z_ref, gm_ref, w_ref, dm_ref, dz_ref, dzb_ref, dgm_ref, dbt_ref):
        @pl.when(pl.program_id(0) == 0)
        def _():
            dgm_ref[...] = jnp.zeros_like(dgm_ref)
            dbt_ref[...] = jnp.zeros_like(dbt_ref)

        dz, dgm, dbt = _layer_norm_bwd(dxn_ref[...], z_ref[...], gm_ref[...])
        dgm_ref[...] += dgm
        dbt_ref[...] += dbt
        dzb = dz.astype(BF16)
        dz_ref[...] = dz
        dzb_ref[...] = dzb
        for j in range(G):
            dm_ref[j] = _dot(dzb, w_ref[j], NT)

    row = pl.BlockSpec((ts, N), lambda s: (s, 0))
    vec = pl.BlockSpec((1, N), lambda s: (0, 0))
    return pl.pallas_call(
        body, name="mixer_out_bwd", grid=(S // ts,),
        in_specs=[row, row, vec, pl.BlockSpec((G, Kg, N), lambda s: (0, 0, 0))],
        out_specs=[pl.BlockSpec((G, ts, Kg), lambda s: (0, s, 0)), row, row, vec, vec],
        out_shape=[SDS((G, S, Kg), F32), SDS((S, N), F32), SDS((S, N), BF16), SDS((1, N), F32), SDS((1, N), F32)],
        compiler_params=_params(("arbitrary",)),
    )(dxn, z, gamma, w)


def contract_t(da, w, res, name):
    G, S, Ng = da.shape
    K = w.shape[1]
    ts = _tile(S, 1024)
    gb = _group_block(G, 6)

    def body(da_ref, w_ref, r_ref, o_ref):
        g = pl.program_id(1)
        part = _dot(da_ref[0], w_ref[0], NT)
        for j in range(1, gb):
            part = part + _dot(da_ref[j], w_ref[j], NT)

        @pl.when(g == 0)
        def _():
            o_ref[...] = ALPHA * r_ref[...] + part

        @pl.when(g > 0)
        def _():
            o_ref[...] += part

    row = pl.BlockSpec((ts, K), lambda s, g: (s, 0))
    return pl.pallas_call(
        body, name=name, grid=(S // ts, G // gb),
        in_specs=[pl.BlockSpec((gb, ts, Ng), lambda s, g: (g, s, 0)), pl.BlockSpec((gb, K, Ng), lambda s, g: (g, 0, 0)), row],
        out_specs=row,
        out_shape=SDS((S, K), F32),
        compiler_params=_params(("parallel", "arbitrary")),
    )(da, w, res)


WGRAD_ACC_ELEMS = 6 * 1024 * 256


def wgrad(a, b, out_dtype, name):
    ga, gb = a.ndim == 3, b.ndim == 3
    G = a.shape[0] if ga else b.shape[0]
    S, K = a.shape[-2:]
    N = b.shape[-1]
    ts = _tile(S, 2048)
    ns = S // ts
    ng = _group_block(G, WGRAD_ACC_ELEMS // (K * N))

    def body(a_ref, b_ref, o_ref, acc):
        s = pl.program_id(1)

        @pl.when(s == 0)
        def _():
            acc[...] = jnp.zeros_like(acc)

        for j in range(ng):
            acc[j] += _dot(a_ref[j] if ga else a_ref[...], b_ref[j] if gb else b_ref[...], TN)

        @pl.when(s == ns - 1)
        def _():
            o_ref[...] = acc[...].astype(out_dtype)

    aspec = pl.BlockSpec((ng, ts, K), lambda g, s: (g, s, 0)) if ga else pl.BlockSpec((ts, K), lambda g, s: (s, 0))
    bspec = pl.BlockSpec((ng, ts, N), lambda g, s: (g, s, 0)) if gb else pl.BlockSpec((ts, N), lambda g, s: (s, 0))
    return pl.pallas_call(
        body, name=name, grid=(G // ng, ns),
        in_specs=[aspec, bspec],
        out_specs=pl.BlockSpec((ng, K, N), lambda g, s: (g, 0, 0)),
        out_shape=SDS((G, K, N), out_dtype),
        scratch_shapes=[pltpu.VMEM((ng, K, N), F32)],
        compiler_params=_params(("parallel", "arbitrary")),
    )(a, b)


def loss_grad(xn, tgt):
    S, N = xn.shape
    ts = _tile(S, 1024)

    def body(x_ref, t_ref, l_ref, dx_ref):
        @pl.when(pl.program_id(0) == 0)
        def _():
            l_ref[...] = jnp.zeros_like(l_ref)

        e = x_ref[...] - t_ref[...]
        dx_ref[...] = e * (1.0 / N)
        l_ref[...] += 0.5 * jnp.sum(jnp.mean(e * e, axis=-1, keepdims=True), axis=0, keepdims=True)

    row = pl.BlockSpec((ts, N), lambda s: (s, 0))
    return pl.pallas_call(
        body, name="loss_grad", grid=(S // ts,),
        in_specs=[row, row],
        out_specs=[pl.BlockSpec((1, 1), lambda s: (0, 0)), row],
        out_shape=[SDS((1, 1), F32), SDS((S, N), F32)],
        compiler_params=_params(("arbitrary",)),
    )(xn, tgt)


def _shift_down(x, k):
    if k == 0:
        return x
    rows = lax.broadcasted_iota(jnp.int32, x.shape, 0)
    return jnp.where(rows >= k, pltpu.roll(x, k, 0), 0.0)


def _shift_up(x, k):
    if k == 0:
        return x
    n = x.shape[0]
    rows = lax.broadcasted_iota(jnp.int32, x.shape, 0)
    return jnp.where(rows < n - k, pltpu.roll(x, n - k, 0), 0.0)


LANES = 128


def conv_mixer_fwd(u, cw):
    _, S, _ = u.shape
    nh = GROUP // LANES

    def body(b_ref, c_ref, x_ref, w_ref, o_ref):
        p = c_ref[0] * x_ref[0]
        w = w_ref[0]
        conv = w[2:3] * p + w[1:2] * _shift_down(p, 1) + w[0:1] * _shift_down(p, 2)
        o_ref[0] = (b_ref[0] * conv).astype(BF16)

    def uspec(off):
        return pl.BlockSpec((1, S, LANES), lambda g, h: (g + off, 0, h))

    return pl.pallas_call(
        body, name="conv_mixer_fwd", grid=(3, nh),
        in_specs=[uspec(0), uspec(3), uspec(6), pl.BlockSpec((1, 8, LANES), lambda g, h: (g, 0, h))],
        out_specs=pl.BlockSpec((1, S, LANES), lambda g, h: (g, 0, h)),
        out_shape=SDS((4, S, GROUP), BF16),
        compiler_params=_params(("parallel", "parallel")),
    )(u, u, u, cw)


def conv_mixer_bwd(u, cw, dm):
    _, S, _ = u.shape
    nh = GROUP // LANES

    def body(b_ref, c_ref, x_ref, w_ref, d_ref, db_ref, dc_ref, dx_ref, dw_ref):
        cg, xi = c_ref[0], x_ref[0]
        p = cg * xi
        p1, p2 = _shift_down(p, 1), _shift_down(p, 2)
        w = w_ref[0]
        conv = w[2:3] * p + w[1:2] * p1 + w[0:1] * p2
        dt = d_ref[0]
        db_ref[0] = (dt * conv).astype(BF16)
        dcv = dt * b_ref[0]
        dp = w[2:3] * dcv + w[1:2] * _shift_up(dcv, 1) + w[0:1] * _shift_up(dcv, 2)
        dc_ref[0] = (dp * xi).astype(BF16)
        dx_ref[0] = (dp * cg).astype(BF16)
        dw = jnp.concatenate([jnp.sum(dcv * p2, axis=0, keepdims=True), jnp.sum(dcv * p1, axis=0, keepdims=True),
                              jnp.sum(dcv * p, axis=0, keepdims=True), jnp.zeros((5, LANES), F32)], axis=0)
        dw_ref[0] = dw

    def uspec(off):
        return pl.BlockSpec((1, S, LANES), lambda g, h: (g + off, 0, h))

    ospec = pl.BlockSpec((1, S, LANES), lambda g, h: (g, 0, h))
    wspec = pl.BlockSpec((1, 8, LANES), lambda g, h: (g, 0, h))
    return pl.pallas_call(
        body, name="conv_mixer_bwd", grid=(3, nh),
        in_specs=[uspec(0), uspec(3), uspec(6), wspec, ospec],
        out_specs=[ospec, ospec, ospec, wspec],
        out_shape=[SDS((3, S, GROUP), BF16)] * 3 + [SDS((3, 8, GROUP), F32)],
        compiler_params=_params(("parallel", "parallel")),
    )(u, u, u, cw, dm)


def qk_conv_fwd(u, qw):
    _, S, _ = u.shape
    nh = GROUP // LANES

    def body(u_ref, w_ref, o_ref):
        x = u_ref[0]
        w = w_ref[0]
        pre = w[3:4] * x + w[2:3] * _shift_down(x, 1) + w[1:2] * _shift_down(x, 2) + w[0:1] * _shift_down(x, 3)
        o_ref[0] = pre * _sigmoid(pre)

    spec = pl.BlockSpec((1, S, LANES), lambda g, h: (g, 0, h))
    return pl.pallas_call(
        body, name="qk_conv_fwd", grid=(8, nh),
        in_specs=[spec, pl.BlockSpec((1, 8, LANES), lambda g, h: (g, 0, h))],
        out_specs=spec,
        out_shape=SDS((8, S, GROUP), F32),
        compiler_params=_params(("parallel", "parallel")),
    )(u, qw)


def qk_conv_bwd(u, qw, dqk, du):
    _, S, _ = u.shape
    nh = GROUP // LANES

    def body(u_ref, w_ref, d_ref, du_in_ref, du_ref, dw_ref):
        x = u_ref[0]
        w = w_ref[0]
        x1, x2, x3 = _shift_down(x, 1), _shift_down(x, 2), _shift_down(x, 3)
        pre = w[3:4] * x + w[2:3] * x1 + w[1:2] * x2 + w[0:1] * x3
        sig = _sigmoid(pre)
        dpre = d_ref[0].astype(F32) * (sig * (1.0 + pre * (1.0 - sig)))
        du = w[3:4] * dpre + w[2:3] * _shift_up(dpre, 1) + w[1:2] * _shift_up(dpre, 2) + w[0:1] * _shift_up(dpre, 3)
        du_ref[0] = du.astype(BF16)
        dw = jnp.concatenate([jnp.sum(dpre * x3, axis=0, keepdims=True), jnp.sum(dpre * x2, axis=0, keepdims=True),
                              jnp.sum(dpre * x1, axis=0, keepdims=True), jnp.sum(dpre * x, axis=0, keepdims=True),
                              jnp.zeros((4, LANES), F32)], axis=0)
        dw_ref[0] = dw

    spec = pl.BlockSpec((1, S, LANES), lambda g, h: (g, 0, h))
    wspec = pl.BlockSpec((1, 8, LANES), lambda g, h: (g, 0, h))
    return pl.pallas_call(
        body, name="qk_conv_bwd", grid=(8, nh),
        in_specs=[spec, wspec, spec, pl.BlockSpec(memory_space=pl.ANY)],
        out_specs=[spec, wspec],
        out_shape=[SDS(du.shape, BF16), SDS((8, 8, GROUP), F32)],
        input_output_aliases={3: 0},
        compiler_params=_params(("parallel", "parallel")),
    )(u, qw, dqk, du)


def _head_masks():
    lane = lax.broadcasted_iota(jnp.int32, (1, D_XA), 1)
    return [(lane >= h * XA_HEAD_DIM) & (lane < (h + 1) * XA_HEAD_DIM) for h in range(XA_HEADS)]


def xattn_fwd(u, qg, kv, tok):
    _, S, _ = u.shape
    ts = _tile(S, 1024)
    scale = XA_HEAD_DIM ** -0.5

    def body(q_ref, kv_ref, tok_ref, o_ref):
        q = q_ref[0]
        k = kv_ref[0].astype(BF16)
        v = kv_ref[1]
        o = jnp.zeros((ts, D_XA), F32)
        for m in _head_masks():
            s = _dot(jnp.where(m, q, 0.0).astype(BF16), k, NT) * scale
            s = s - jnp.max(s, axis=-1, keepdims=True)
            e = jnp.exp(s)
            p = e / jnp.sum(e, axis=-1, keepdims=True)
            o = o + _dot(p.astype(BF16), jnp.where(m, v, 0.0).astype(BF16), NN)
        o_ref[0] = o.astype(BF16)

    slot = tok.shape[0] - 1
    return pl.pallas_call(
        body, name="xattn_fwd", grid=(S // ts,),
        in_specs=[pl.BlockSpec((1, ts, GROUP), lambda s: (qg, s, 0)), pl.BlockSpec((2, N_MEM, GROUP), lambda s: (0, 0, 0)),
                  pl.BlockSpec(memory_space=pl.ANY)],
        out_specs=pl.BlockSpec((1, ts, GROUP), lambda s: (slot, s, 0)),
        out_shape=SDS(tok.shape, BF16),
        input_output_aliases={2: 0},
        compiler_params=_params(("parallel",)),
    )(u, kv, tok)


def xattn_bwd(u, qg, kv, dm, dg, du=None, dgate=None):
    _, S, _ = u.shape
    ts = _tile(S, 1024)
    scale = XA_HEAD_DIM ** -0.5

    def body(q_ref, kv_ref, do_ref, *refs):
        dq_ref, dkv_ref = refs[-2:]

        @pl.when(pl.program_id(0) == 0)
        def _():
            dkv_ref[...] = jnp.zeros_like(dkv_ref)

        q = q_ref[0]
        k = kv_ref[0]
        v = kv_ref[1]
        kb = k.astype(BF16)
        do = do_ref[0]
        dq = jnp.zeros((ts, D_XA), F32)
        dk = jnp.zeros((N_MEM, D_XA), F32)
        dv = jnp.zeros((N_MEM, D_XA), F32)
        for m in _head_masks():
            qm = jnp.where(m, q, 0.0).astype(BF16)
            s = _dot(qm, kb, NT) * scale
            s = s - jnp.max(s, axis=-1, keepdims=True)
            e = jnp.exp(s)
            p = e / jnp.sum(e, axis=-1, keepdims=True)
            dom = jnp.where(m, do, 0.0).astype(BF16)
            dp = _dot(dom, jnp.where(m, v, 0.0).astype(BF16), NT)
            ds = (p * (dp - jnp.sum(dp * p, axis=-1, keepdims=True)) * scale).astype(BF16)
            dq = dq + _dot(ds, jnp.where(m, k, 0.0).astype(BF16), NN)
            dk = dk + _dot(ds, qm, TN)
            dv = dv + _dot(p.astype(BF16), dom, TN)
        dq_ref[0] = dq.astype(BF16)
        if du is not None:
            dq_ref[1] = refs[0][0]
        dkv_ref[0] += dk
        dkv_ref[1] += dv

    in_specs = [pl.BlockSpec((1, ts, GROUP), lambda s: (qg, s, 0)), pl.BlockSpec((2, N_MEM, GROUP), lambda s: (0, 0, 0)),
                pl.BlockSpec((1, ts, GROUP), lambda s: (dg, s, 0))]
    args, aliases = [u, kv, dm], {}
    dq_spec, dq_shape = pl.BlockSpec((1, ts, GROUP), lambda s: (0, s, 0)), SDS((1, S, GROUP), BF16)
    if du is not None:
        in_specs += [pl.BlockSpec((1, ts, GROUP), lambda s: (0, s, 0)), pl.BlockSpec(memory_space=pl.ANY)]
        args += [dgate, du]
        aliases = {4: 0}
        dq_spec, dq_shape = pl.BlockSpec((2, ts, GROUP), lambda s: (qg // 2, s, 0)), SDS(du.shape, BF16)
    return pl.pallas_call(
        body, name="xattn_bwd", grid=(S // ts,),
        in_specs=in_specs,
        out_specs=[dq_spec, pl.BlockSpec((2, N_MEM, GROUP), lambda s: (0, 0, 0))],
        out_shape=[dq_shape, SDS((2, N_MEM, GROUP), F32)],
        input_output_aliases=aliases,
        compiler_params=_params(("arbitrary",)),
    )(*args)


ML_BLOCK_CHUNKS = 4
H4 = ML_HEADS
L = ML_CHUNK
NLANE = ML_HEAD_DIM


def _chunk_consts():
    r = lax.broadcasted_iota(jnp.int32, (1, L, L), 1)
    c = lax.broadcasted_iota(jnp.int32, (1, L, L), 2)
    return r >= c, r <= c, r == c


def _gate_cols(gb):
    lane = lax.broadcasted_iota(jnp.int32, gb.shape, 1)
    li = jnp.stack([jnp.sum(jnp.where(lane == h, gb, 0.0), axis=1, keepdims=True) for h in range(H4)])
    gf = jnp.stack([jnp.sum(jnp.where(lane == H4 + h, gb, 0.0), axis=1, keepdims=True) for h in range(H4)])
    return li, gf


def _log_sigmoid(x):
    return jnp.minimum(x, 0.0) - jnp.log(1.0 + jnp.exp(-jnp.abs(x)))


def _chunk_forward(q, k, v_aug, li_col, lf_col, c_prev, m_prev):
    tri, tri_t, eye = _chunk_consts()
    lf_row = jnp.sum(jnp.where(eye, lf_col, 0.0), axis=1, keepdims=True)
    li_row = jnp.sum(jnp.where(eye, li_col, 0.0), axis=1, keepdims=True)
    bcum_col = jnp.sum(jnp.where(tri, lf_row, 0.0), axis=2, keepdims=True)
    bcum_row = jnp.sum(jnp.where(tri_t, lf_col, 0.0), axis=1, keepdims=True)
    log_d = jnp.where(tri, bcum_col - bcum_row + li_row, NEG)
    log_inter = bcum_col + m_prev
    m_t = jnp.maximum(log_inter, jnp.max(log_d, axis=2, keepdims=True))
    w_intra = jnp.exp(log_d - m_t)
    w_inter = jnp.exp(log_inter - m_t)
    sc = _bdot(q, k, 2, 2) * w_intra
    qc = _bdot1(q, c_prev, 2, 1)
    num = _bdot(sc, v_aug, 2, 1) + w_inter * qc
    lane = lax.broadcasted_iota(jnp.int32, num.shape, 2)
    den = jnp.sum(jnp.where(lane == NLANE, num, 0.0), axis=2, keepdims=True)
    e_m = jnp.exp(-m_t)
    b_last = jnp.sum(lf_row, axis=2, keepdims=True)
    log_w = b_last - bcum_col + li_col
    m_new = jnp.maximum(b_last + m_prev, jnp.max(log_w, axis=1, keepdims=True))
    w_k = jnp.exp(log_w - m_new)
    decay = jnp.exp(b_last + m_prev - m_new)
    return dict(w_intra=w_intra, w_inter=w_inter, sc=sc, qc=qc, num=num, den=den, e_m=e_m, lane=lane,
                w_k=w_k, decay=decay, m_new=m_new)


def mlstm_fwd(qk, u, bg):
    _, S, _ = qk.shape
    nc = S // L
    cb = min(ML_BLOCK_CHUNKS, nc)
    rows = cb * L
    kscale = ML_HEAD_DIM ** -0.5

    def body(qk_ref, v_ref, g_ref, bg_ref, h_ref, cst_ref, mst_ref, c_sc, m_sc):
        @pl.when(pl.program_id(0) == 0)
        def _():
            c_sc[...] = jnp.zeros_like(c_sc)
            m_sc[...] = jnp.zeros_like(m_sc)

        for c in range(cb):
            sl = pl.ds(c * L, L)
            q = qk_ref[0:H4, sl, :]
            k = qk_ref[H4:2 * H4, sl, :] * kscale
            v = v_ref[:, sl, :]
            lane = lax.broadcasted_iota(jnp.int32, v.shape, 2)
            v_aug = jnp.where(lane == NLANE, 1.0, v)
            li_col, gf = _gate_cols(g_ref[0, sl, :] + bg_ref[...])
            lf_col = _log_sigmoid(gf)
            c_prev = c_sc[...]
            m_prev = m_sc[...]
            f = _chunk_forward(q, k, v_aug, li_col, lf_col, c_prev, m_prev)
            r = 1.0 / jnp.maximum(jnp.abs(f["den"]), f["e_m"])
            h_ref[:, sl, :] = jnp.where(lane < NLANE, f["num"] * r, 0.0)
            cst_ref[c] = c_prev
            mst_ref[c] = jnp.broadcast_to(m_prev, (H4, 1, LANES))
            c_sc[...] = f["decay"] * c_prev + _bdot(k * f["w_k"], v_aug, 1, 1)
            m_sc[...] = f["m_new"]

    def hspec(blk):
        return pl.BlockSpec((H4, rows, GROUP), lambda i: (blk, i, 0))

    return pl.pallas_call(
        body, name="mlstm_fwd", grid=(nc // cb,),
        in_specs=[pl.BlockSpec((2 * H4, rows, GROUP), lambda i: (0, i, 0)), hspec(2),
                  pl.BlockSpec((1, rows, GROUP), lambda i: (17, i, 0)), pl.BlockSpec((1, GROUP), lambda i: (0, 0))],
        out_specs=[hspec(0), pl.BlockSpec((cb, H4, GROUP, GROUP), lambda i: (i, 0, 0, 0)),
                   pl.BlockSpec((cb, H4, 1, LANES), lambda i: (i, 0, 0, 0))],
        out_shape=[SDS((H4, S, GROUP), F32), SDS((nc, H4, GROUP, GROUP), F32), SDS((nc, H4, 1, LANES), F32)],
        scratch_shapes=[pltpu.VMEM((H4, GROUP, GROUP), F32), pltpu.VMEM((H4, 1, 1), F32)],
        compiler_params=_params(("arbitrary",)),
    )(qk, u, u, bg)


def mlstm_bwd(qk, u, bg, cst, mst, dh, du):
    _, S, _ = qk.shape
    nc = S // L
    cb = min(ML_BLOCK_CHUNKS, nc)
    rows = cb * L
    nb = nc // cb
    kscale = ML_HEAD_DIM ** -0.5

    def body(qk_ref, v_ref, g_ref, bg_ref, cst_ref, mst_ref, dh_ref, du_in_ref, dqk_ref, dv_ref, dg_ref, dbg_ref, dc_sc):
        @pl.when(pl.program_id(0) == 0)
        def _():
            dc_sc[...] = jnp.zeros_like(dc_sc)
            dbg_ref[...] = jnp.zeros_like(dbg_ref)

        tri, tri_t, eye = _chunk_consts()
        for c in reversed(range(cb)):
            sl = pl.ds(c * L, L)
            q = qk_ref[0:H4, sl, :]
            k = qk_ref[H4:2 * H4, sl, :] * kscale
            v = v_ref[:, sl, :]
            lane = lax.broadcasted_iota(jnp.int32, v.shape, 2)
            v_aug = jnp.where(lane == NLANE, 1.0, v)
            li_col, gf = _gate_cols(g_ref[0, sl, :] + bg_ref[...])
            lf_col = _log_sigmoid(gf)
            c_prev = cst_ref[c]
            m_prev = mst_ref[c][:, :, 0:1]
            f = _chunk_forward(q, k, v_aug, li_col, lf_col, c_prev, m_prev)
            w_intra, w_inter, sc, num, den, e_m = f["w_intra"], f["w_inter"], f["sc"], f["num"], f["den"], f["e_m"]
            absd = jnp.abs(den)
            r = 1.0 / jnp.maximum(absd, e_m)
            dhv = dh_ref[:, sl, :]
            s1 = jnp.sum(jnp.where(lane < NLANE, dhv * num, 0.0), axis=2, keepdims=True)
            dden = jnp.where(absd > e_m, -s1 * r * r * jnp.sign(den), 0.0)
            dnum = jnp.where(lane == NLANE, dden, jnp.where(lane < NLANE, dhv * r, 0.0))
            dsc = _bdot1(dnum, v_aug, 2, 2)
            dv = _bdot1(sc, dnum, 1, 1)
            gmat = dsc * sc
            dqk = dsc * w_intra
            dq = _bdot1(dqk, k, 2, 1) + w_inter * _bdot1(dnum, c_prev, 2, 2)
            dk = _bdot1(dqk, q, 1, 1)
            dc_prev = _bdot(q * w_inter, dnum, 1, 1)
            dlog_inter = jnp.sum(dnum * f["qc"], axis=2, keepdims=True) * w_inter
            dbcum_col = dlog_inter + jnp.sum(gmat, axis=2, keepdims=True)
            g_row = jnp.sum(gmat, axis=1, keepdims=True)
            dcn = dc_sc[...]
            w_k, decay = f["w_k"], f["decay"]
            kw = k * w_k
            dc_prev = dc_prev + decay * dcn
            db_last = jnp.sum(jnp.sum(dcn * c_prev, axis=2, keepdims=True), axis=1, keepdims=True) * decay
            dkw = _bdot(v_aug, dcn, 2, 2)
            dv = dv + _bdot1(kw, dcn, 2, 1)
            dk = dk + dkw * w_k
            dlogw = jnp.sum(dkw * k, axis=2, keepdims=True) * w_k
            db_last = db_last + jnp.sum(dlogw, axis=1, keepdims=True)
            dbcum_col = dbcum_col - dlogw
            rowi = lax.broadcasted_iota(jnp.int32, (1, L, 1), 1)
            dbcum_col = dbcum_col + jnp.where(rowi == L - 1, db_last, 0.0)
            dbcum_row = jnp.sum(jnp.where(eye, dbcum_col, 0.0), axis=1, keepdims=True) - g_row
            dlf_col = jnp.sum(jnp.where(tri_t, dbcum_row, 0.0), axis=2, keepdims=True)
            dli_col = dlogw + jnp.sum(jnp.where(eye, g_row, 0.0), axis=2, keepdims=True)
            dgf_col = dlf_col * _sigmoid(-gf)
            lane_g = lax.broadcasted_iota(jnp.int32, (L, GROUP), 1)
            dg = jnp.zeros((L, GROUP), F32)
            for h in range(H4):
                dg = dg + jnp.where(lane_g == h, dli_col[h], 0.0) + jnp.where(lane_g == H4 + h, dgf_col[h], 0.0)
            dqk_ref[0:H4, sl, :] = dq.astype(BF16)
            dqk_ref[H4:2 * H4, sl, :] = (dk * kscale).astype(BF16)
            dv_ref[:, sl, :] = jnp.where(lane < NLANE, dv, 0.0).astype(BF16)
            dg_ref[0, sl, :] = dg.astype(BF16)
            dbg_ref[...] += jnp.sum(dg, axis=0, keepdims=True)
            dc_sc[...] = dc_prev

    def hspec(blk):
        return pl.BlockSpec((H4, rows, GROUP), lambda i: (blk, nb - 1 - i, 0))

    gspec = pl.BlockSpec((1, rows, GROUP), lambda i: (17, nb - 1 - i, 0))
    qkspec = pl.BlockSpec((2 * H4, rows, GROUP), lambda i: (0, nb - 1 - i, 0))
    return pl.pallas_call(
        body, name="mlstm_bwd", grid=(nb,),
        in_specs=[qkspec, hspec(2), gspec, pl.BlockSpec((1, GROUP), lambda i: (0, 0)),
                  pl.BlockSpec((cb, H4, GROUP, GROUP), lambda i: (nb - 1 - i, 0, 0, 0)),
                  pl.BlockSpec((cb, H4, 1, LANES), lambda i: (nb - 1 - i, 0, 0, 0)), hspec(0), pl.BlockSpec(memory_space=pl.ANY)],
        out_specs=[qkspec, hspec(2), pl.BlockSpec((1, rows, GROUP), lambda i: (0, nb - 1 - i, 0)),
                   pl.BlockSpec((1, GROUP), lambda i: (0, 0))],
        input_output_aliases={7: 1},
        out_shape=[SDS((2 * H4, S, GROUP), BF16), SDS(du.shape, BF16),
                   SDS((1, S, GROUP), BF16), SDS((1, GROUP), F32)],
        scratch_shapes=[pltpu.VMEM((H4, GROUP, GROUP), F32)],
        compiler_params=_params(("arbitrary",)),
    )(qk, u, u, bg, cst, mst, dh, du)


def head_norm_fwd(hm, u, hg):
    _, S, _ = hm.shape
    ts = _tile(S, 2048)

    def body(h_ref, o_ref, g_ref, t_ref):
        h = h_ref[0]
        lane = lax.broadcasted_iota(jnp.int32, h.shape, 1)
        valid = lane < ML_HEAD_DIM
        mu = jnp.sum(h, axis=-1, keepdims=True) * (1.0 / ML_HEAD_DIM)
        hc = jnp.where(valid, h - mu, 0.0)
        var = jnp.sum(hc * hc, axis=-1, keepdims=True) * (1.0 / ML_HEAD_DIM)
        hn = hc * lax.rsqrt(var + LN_EPS) * g_ref[0]
        t_ref[0] = (_sigmoid(o_ref[0]) * hn).astype(BF16)

    return pl.pallas_call(
        body, name="head_norm_fwd", grid=(H4, S // ts),
        in_specs=[pl.BlockSpec((1, ts, GROUP), lambda h, s: (h, s, 0)), pl.BlockSpec((1, ts, GROUP), lambda h, s: (12 + h, s, 0)),
                  pl.BlockSpec((1, 1, GROUP), lambda h, s: (h, 0, 0))],
        out_specs=pl.BlockSpec((1, ts, GROUP), lambda h, s: (h, s, 0)),
        out_shape=SDS((H4 + 1, S, GROUP), BF16),
        compiler_params=_params(("parallel", "parallel")),
    )(hm, u, hg)


def head_norm_bwd(hm, u, hg, dm):
    _, S, _ = hm.shape
    ts = _tile(S, 2048)

    def body(h_ref, o_ref, g_ref, d_ref, dh_ref, do_ref, dg_ref):
        @pl.when(pl.program_id(1) == 0)
        def _():
            dg_ref[...] = jnp.zeros_like(dg_ref)

        h = h_ref[0]
        lane = lax.broadcasted_iota(jnp.int32, h.shape, 1)
        valid = lane < ML_HEAD_DIM
        inv = 1.0 / ML_HEAD_DIM
        mu = jnp.sum(h, axis=-1, keepdims=True) * inv
        hc = jnp.where(valid, h - mu, 0.0)
        var = jnp.sum(hc * hc, axis=-1, keepdims=True) * inv
        rstd = lax.rsqrt(var + LN_EPS)
        xhat = hc * rstd
        g = g_ref[0]
        sig = _sigmoid(o_ref[0])
        dt = jnp.where(valid, d_ref[0], 0.0)
        do_ref[0] = (dt * xhat * g * sig * (1.0 - sig)).astype(BF16)
        dhn = dt * sig
        dg_ref[0] += jnp.sum(dhn * xhat, axis=0, keepdims=True)
        dxh = dhn * g
        m1 = jnp.sum(dxh, axis=-1, keepdims=True) * inv
        m2 = jnp.sum(dxh * xhat, axis=-1, keepdims=True) * inv
        dh_ref[0] = jnp.where(valid, rstd * (dxh - m1 - xhat * m2), 0.0)

    spec = pl.BlockSpec((1, ts, GROUP), lambda h, s: (h, s, 0))
    gspec = pl.BlockSpec((1, 1, GROUP), lambda h, s: (h, 0, 0))
    return pl.pallas_call(
        body, name="head_norm_bwd", grid=(H4, S // ts),
        in_specs=[spec, pl.BlockSpec((1, ts, GROUP), lambda h, s: (12 + h, s, 0)), gspec, spec],
        out_specs=[spec, pl.BlockSpec((1, ts, GROUP), lambda h, s: (12 + h, s, 0)), gspec],
        out_shape=[SDS((H4, S, GROUP), F32), SDS((u.shape[0], S, GROUP), BF16), SDS((H4, 1, GROUP), F32)],
        compiler_params=_params(("parallel", "arbitrary")),
    )(hm, u, hg, dm)


def _adamw_math(w, g, m, v):
    c1 = 1.0 / (1.0 - ADAM_B1 ** ADAM_STEP)
    c2 = 1.0 / (1.0 - ADAM_B2 ** ADAM_STEP)
    nm = ADAM_B1 * m + (1.0 - ADAM_B1) * g
    nv = ADAM_B2 * v + (1.0 - ADAM_B2) * (g * g)
    return -ADAM_LR * ((nm * c1) / (jnp.sqrt(nv * c2) + ADAM_EPS) + ADAM_WD * w), nm, nv


def _row_tile(R, cap=512):
    return R if R <= cap else max(d for d in range(8, cap + 1, 8) if R % d == 0)


def adamw_into(w, m, v, g, outs, idx, after, name):
    R, C = g.shape
    tr = _row_tile(R)
    lead = (0,) * len(idx)

    def body(w_ref, m_ref, v_ref, g_ref, *rest):
        go_ref, d_ref, nm_ref, nv_ref, token = rest[-5:]
        token[...] = jnp.zeros_like(token)
        gv = g_ref[...]
        d, nm, nv = _adamw_math(w_ref[lead], gv, m_ref[lead], v_ref[lead])
        go_ref[lead] = gv
        d_ref[lead] = d
        nm_ref[lead] = nm
        nv_ref[lead] = nv

    blk = pl.BlockSpec((1,) * len(idx) + (tr, C), lambda r: idx + (r, 0))
    any_space = pl.BlockSpec(memory_space=pl.ANY)
    in_specs, args, aliases = [blk, blk, blk, pl.BlockSpec((tr, C), lambda r: (r, 0)), any_space], [w, m, v, g, g if after is None else after], {}
    if outs is not None:
        in_specs += [any_space] * 4
        args += list(outs)
        aliases = {5 + i: i for i in range(4)}
    out = pl.pallas_call(
        body, name=name, grid=(R // tr,),
        in_specs=in_specs, out_specs=[blk] * 4 + [pl.BlockSpec((8, LANES), lambda r: (0, 0))],
        out_shape=[SDS(w.shape, F32)] * 4 + [SDS((8, LANES), F32)],
        input_output_aliases=aliases, compiler_params=_params(("arbitrary",)),
    )(*args)
    return out[:4], out[4]


def adamw(w, g, m, v, name):
    R, C = w.shape
    tr = _row_tile(R)

    def body(w_ref, g_ref, m_ref, v_ref, d_ref, nm_ref, nv_ref):
        d_ref[...], nm_ref[...], nv_ref[...] = _adamw_math(w_ref[...], g_ref[...], m_ref[...], v_ref[...])

    spec = pl.BlockSpec((tr, C), lambda i: (i, 0))
    return pl.pallas_call(
        body, name=name, grid=(R // tr,),
        in_specs=[spec] * 4, out_specs=[spec] * 3,
        out_shape=[SDS((R, C), F32)] * 3,
        compiler_params=_params(("parallel",)),
    )(w, g, m, v)


HBM = pl.BlockSpec(memory_space=pl.ANY)
ROW_SPLIT = 4
PAIR_SPLIT = 1


def _position():
    x, y, c = lax.axis_index("x"), lax.axis_index("y"), lax.axis_index("c")
    return x, y, c, [(1 - x, y), (x, 1 - y), (1 - x, 1 - y)]


def _unique(items):
    arrays = []
    for a, _ in items:
        if not any(a is b for b in arrays):
            arrays.append(a)
    return arrays, [next(i for i, b in enumerate(arrays) if b is a) for a, _ in items]


def place_own(items, me, after, name):
    arrays, src_of = _unique(items)
    n = len(items)
    shapes = [a.shape[len(p):] for a, p in items]

    def body(me_ref, *refs):
        for t in range(n):
            refs[n + 1 + t][0] = refs[t][(0,) * len(items[t][1])]

    in_specs, out_specs = [], []
    for (a, p), shp in zip(items, shapes):
        blk = shp[:-2] + (shp[-2] // ROW_SPLIT, shp[-1])
        lead = (0,) * (len(shp) - 2)
        in_specs.append(pl.BlockSpec((1,) * len(p) + blk, functools.partial(lambda r, me_ref, p, lead: p + lead + (r, 0), p=p, lead=lead)))
        out_specs.append(pl.BlockSpec((1,) + blk, functools.partial(lambda r, me_ref, lead: (me_ref[0],) + lead + (r, 0), lead=lead)))
    in_specs.append(pl.BlockSpec(memory_space=pl.ANY))
    return pl.pallas_call(
        body, name=name,
        grid_spec=pltpu.PrefetchScalarGridSpec(num_scalar_prefetch=1, grid=(ROW_SPLIT,), in_specs=in_specs, out_specs=out_specs),
        out_shape=[SDS((N_CHIPS,) + tuple(shp), a.dtype) for shp, (a, _) in zip(shapes, items)],
        compiler_params=_params(("parallel",)),
    )(me, *[arrays[i] for i in src_of], after)


SEM = pl.BlockSpec(memory_space=pltpu.SEMAPHORE)
IN_HBM = pl.BlockSpec(memory_space=pltpu.HBM)
DATAFLOW = pltpu.SideEffectType.DATAFLOW_SIDE_EFFECTING


def split_start(bufs, plan, n_copies, after, name):
    n = len(bufs)

    def body(*refs):
        send, recv, token = refs[n + 1], refs[n + 2], refs[-1]
        x, y, c, chips = _position()
        for k, (src, dst, dev) in enumerate(plan(refs[:n], x, y, c, chips)):
            pltpu.make_async_remote_copy(src_ref=src, dst_ref=dst, send_sem=send.at[k], recv_sem=recv.at[k],
                                         device_id=dev, device_id_type=MESH).start()
        token[...] = jnp.zeros_like(token)

    out = pl.pallas_call(
        body, name=name,
        out_shape=(pltpu.SemaphoreType.DMA((n_copies,)), pltpu.SemaphoreType.DMA((n_copies,)),
                   *[pltpu.HBM(b.shape, b.dtype) for b in bufs], SDS((8, LANES), F32)),
        in_specs=[IN_HBM] * n + [pl.BlockSpec(memory_space=pl.ANY)],
        out_specs=(SEM, SEM, *[IN_HBM] * n, pl.BlockSpec(memory_space=pltpu.VMEM)),
        input_output_aliases={i: 2 + i for i in range(n)},
        compiler_params=pltpu.CompilerParams(has_side_effects=DATAFLOW),
    )(*[pltpu.with_memory_space_constraint(b, pltpu.HBM) for b in bufs], after)
    return out[0], out[1], list(out[2:2 + n]), out[-1]


def split_wait(send, recv, bufs, plan, after, name):
    n = len(bufs)

    def body(*refs):
        send_ref, recv_ref = refs[n], refs[n + 1]
        x, y, c, chips = _position()
        for k, (src, dst, dev) in enumerate(plan(refs[:n], x, y, c, chips)):
            cp = pltpu.make_async_remote_copy(src_ref=src, dst_ref=dst, send_sem=send_ref.at[k], recv_sem=recv_ref.at[k],
                                              device_id=dev, device_id_type=MESH)
            cp.wait_send()
            cp.wait_recv()

    return list(pl.pallas_call(
        body, name=name, out_shape=tuple(pltpu.HBM(b.shape, b.dtype) for b in bufs),
        in_specs=[IN_HBM] * n + [SEM, SEM, pl.BlockSpec(memory_space=pl.ANY)], out_specs=tuple([IN_HBM] * n),
        input_output_aliases={i: i for i in range(n)},
        compiler_params=pltpu.CompilerParams(has_side_effects=DATAFLOW),
    )(*bufs, send, recv, after))


def _gather_plan(shapes, landing):
    n = len(shapes)

    def plan(refs, x, y, c, chips):
        out = []
        for t in range(n):
            half = shapes[t][0] // 2
            rows = pl.ds(c * half, half)
            for cx, cy in chips:
                slot = 2 * cx + cy if landing else 2 * x + y
                out.append((refs[t].at[rows], refs[n + t].at[slot, rows], (cx, cy, c)))
        return out

    return plan


def gather_start(shards, placed, after, name):
    shapes = [s.shape for s in shards]
    send, recv, bufs, token = split_start(list(shards) + list(placed), _gather_plan(shapes, False), 3 * len(shards), after, name)
    return (send, recv, bufs, shapes), token


def gather_wait(state, after, name):
    send, recv, bufs, shapes = state
    return split_wait(send, recv, bufs, _gather_plan(shapes, True), after, name)[len(shapes):]


def gather_pass_on(placed, shapes, name):
    n = len(placed)

    def body(*refs):
        outs, send, recv = refs[n:2 * n], refs[2 * n], refs[2 * n + 1]
        x, y, c, chips = _position()
        cps = []
        for t in range(n):
            half = shapes[t][0] // 2
            for j, (cx, cy) in enumerate(chips):
                piece = outs[t].at[2 * cx + cy, pl.ds(c * half, half)]
                cp = pltpu.make_async_remote_copy(src_ref=piece, dst_ref=piece, send_sem=send.at[3 * t + j], recv_sem=recv.at[3 * t + j],
                                                  device_id=(x, y, 1 - c), device_id_type=MESH)
                cp.start()
                cps.append(cp)
        for t in range(n):
            half = shapes[t][0] // 2
            for j, (cx, cy) in enumerate(chips):
                piece = outs[t].at[2 * cx + cy, pl.ds((1 - c) * half, half)]
                pltpu.make_async_remote_copy(src_ref=piece, dst_ref=piece, send_sem=send.at[3 * t + j], recv_sem=recv.at[3 * t + j],
                                             device_id=(x, y, 1 - c), device_id_type=MESH).wait_recv()
        for cp in cps:
            cp.wait_send()

    return pl.pallas_call(
        body, name=name,
        in_specs=[HBM] * n, out_specs=[HBM] * n,
        out_shape=[SDS(p.shape, p.dtype) for p in placed],
        input_output_aliases={t: t for t in range(n)},
        scratch_shapes=[pltpu.SemaphoreType.DMA((3 * n,))] * 2,
    )(*placed)


def _flip(k, x, y, c):
    return ((1 - x) if k & 4 else x, (1 - y) if k & 2 else y, (1 - c) if k & 1 else c)


def small_allgather(v, reduce):
    R, C = v.shape

    def body(v_ref, o_ref, *scratch):
        if reduce:
            buf, send, recv = scratch
        else:
            buf, (send, recv) = o_ref, scratch
        x, y, c, _ = _position()
        me = 4 * x + 2 * y + c
        buf[me] = v_ref[...]
        sends = []
        for k in range(1, N_DEV):
            cp = pltpu.make_async_remote_copy(src_ref=v_ref, dst_ref=buf.at[me], send_sem=send.at[k - 1], recv_sem=recv.at[k - 1],
                                              device_id=_flip(k, x, y, c), device_id_type=MESH)
            cp.start()
            sends.append(cp)
        for k in range(1, N_DEV):
            px, py, pc = _flip(k, x, y, c)
            pltpu.make_async_remote_copy(src_ref=v_ref, dst_ref=buf.at[4 * px + 2 * py + pc], send_sem=send.at[k - 1],
                                         recv_sem=recv.at[k - 1], device_id=(px, py, pc), device_id_type=MESH).wait_recv()
        for cp in sends:
            cp.wait_send()
        if reduce:
            acc = buf[0]
            for i in range(1, N_DEV):
                acc = acc + buf[i]
            o_ref[...] = acc

    vm = pl.BlockSpec(memory_space=pltpu.VMEM)
    sems = [pltpu.SemaphoreType.DMA((N_DEV - 1,)), pltpu.SemaphoreType.DMA((N_DEV - 1,))]
    return pl.pallas_call(
        body, name="small_allreduce" if reduce else "small_allgather",
        in_specs=[vm], out_specs=vm,
        out_shape=SDS((R, C) if reduce else (N_DEV, R, C), F32),
        scratch_shapes=([pltpu.VMEM((N_DEV, R, C), F32)] if reduce else []) + sems,
    )(v)


def rs_exchange_sibling(gs):
    n = len(gs)

    def body(*refs):
        ins, outs, send, recv = refs[:n], refs[n:2 * n], refs[2 * n], refs[2 * n + 1]
        x, y, c, _ = _position()
        cps = []
        for t in range(n):
            cp = pltpu.make_async_remote_copy(src_ref=ins[t].at[:, 1 - c], dst_ref=outs[t], send_sem=send.at[t], recv_sem=recv.at[t],
                                              device_id=(x, y, 1 - c), device_id_type=MESH)
            cp.start()
            cps.append(cp)
        for cp in cps:
            cp.wait()

    return pl.pallas_call(
        body, name="rs_exchange_sibling", in_specs=[HBM] * n, out_specs=[HBM] * n,
        out_shape=[SDS((g.shape[0],) + g.shape[2:], g.dtype) for g in gs],
        scratch_shapes=[pltpu.SemaphoreType.DMA((n,)), pltpu.SemaphoreType.DMA((n,))],
    )(*gs)


def rs_pair_add(gs, rs, c):
    n = len(gs)

    def body(c_ref, *refs):
        for t in range(n):
            refs[2 * n + t][0] = (refs[t][0, 0].astype(F32) + refs[n + t][0].astype(F32)).astype(BF16)

    in_specs, out_specs, out_shape = [], [], []
    for g in gs:
        _, _, h, C = g.shape
        in_specs.append(pl.BlockSpec((1, 1, h // PAIR_SPLIT, C), lambda j, r, c_ref: (j, c_ref[0], r, 0)))
    for g in gs:
        _, _, h, C = g.shape
        spec = pl.BlockSpec((1, h // PAIR_SPLIT, C), lambda j, r, c_ref: (j, r, 0))
        in_specs.append(spec)
        out_specs.append(spec)
        out_shape.append(SDS((N_CHIPS, h, C), BF16))
    return pl.pallas_call(
        body, name="rs_pair_add",
        grid_spec=pltpu.PrefetchScalarGridSpec(num_scalar_prefetch=1, grid=(N_CHIPS, PAIR_SPLIT), in_specs=in_specs, out_specs=out_specs),
        out_shape=out_shape, compiler_params=_params(("parallel", "parallel")),
    )(c, *gs, *rs)


def _rs_plan(n):
    def plan(refs, x, y, c, chips):
        return [(refs[t].at[2 * cx + cy], refs[n + t].at[j], (cx, cy, c)) for t in range(n) for j, (cx, cy) in enumerate(chips)]

    return plan


def rs_chip_add(ps, qs, me_c):
    n = len(ps)

    def body(me_ref, *refs):
        for t in range(n):
            q = refs[n + t]
            refs[2 * n + t][0] = ((refs[t][0].astype(F32) + q[0].astype(F32)) + q[1].astype(F32)) + q[2].astype(F32)

    in_specs, out_specs, out_shape = [], [], []
    for p in ps:
        _, h, C = p.shape
        in_specs.append(pl.BlockSpec((1, h // ROW_SPLIT, C), lambda r, me_ref: (me_ref[0], r, 0)))
    for p in ps:
        _, h, C = p.shape
        in_specs.append(pl.BlockSpec((3, h // ROW_SPLIT, C), lambda r, me_ref: (0, r, 0)))
        out_specs.append(pl.BlockSpec((1, h // ROW_SPLIT, C), lambda r, me_ref: (me_ref[1], r, 0)))
        out_shape.append(SDS((2, h, C), F32))
    return pl.pallas_call(
        body, name="rs_chip_add",
        grid_spec=pltpu.PrefetchScalarGridSpec(num_scalar_prefetch=1, grid=(ROW_SPLIT,), in_specs=in_specs, out_specs=out_specs),
        out_shape=out_shape, compiler_params=_params(("parallel",)),
    )(me_c, *ps, *qs)


def rs_share(rs):
    n = len(rs)

    def body(*refs):
        outs, send, recv = refs[n:2 * n], refs[2 * n], refs[2 * n + 1]
        x, y, c, _ = _position()
        cps = []
        for t in range(n):
            cp = pltpu.make_async_remote_copy(src_ref=outs[t].at[c], dst_ref=outs[t].at[c], send_sem=send.at[t], recv_sem=recv.at[t],
                                              device_id=(x, y, 1 - c), device_id_type=MESH)
            cp.start()
            cps.append(cp)
        for cp in cps:
            cp.wait()

    return pl.pallas_call(
        body, name="rs_share", in_specs=[HBM] * n, out_specs=[HBM] * n,
        out_shape=[SDS(r.shape, r.dtype) for r in rs],
        input_output_aliases={t: t for t in range(n)},
        scratch_shapes=[pltpu.SemaphoreType.DMA((n,))] * 2,
    )(*rs)


def rs_begin(gs, after, name):
    c = lax.axis_index("c")
    n = len(gs)
    g5 = [g.reshape(N_CHIPS, 2, g.shape[1] // 2, g.shape[2]) for g in gs]
    from_sibling = rs_exchange_sibling(g5)
    pair = rs_pair_add(g5, from_sibling, jnp.reshape(c, (1,)).astype(jnp.int32))
    lands = [lax.empty((3,) + p.shape[1:], p.dtype) for p in pair]
    send, recv, bufs, token = split_start(list(pair) + lands, _rs_plan(n), 3 * n, from_sibling[0] if after is None else after, name)
    return (send, recv, bufs, [g.shape for g in gs]), token


def rs_end(state, after, name):
    x, y, c = lax.axis_index("x"), lax.axis_index("y"), lax.axis_index("c")
    send, recv, bufs, shapes = state
    n = len(shapes)
    bufs = split_wait(send, recv, bufs, _rs_plan(n), after, name)
    half = rs_chip_add(bufs[:n], bufs[n:], jnp.stack([2 * x + y, c]).astype(jnp.int32))
    both = rs_share(half)
    return [b.reshape(s[1], s[2]) for b, s in zip(both, shapes)]


def _pad_last(a, n):
    return jnp.pad(a, [(0, 0)] * (a.ndim - 1) + [(0, n - a.shape[-1])])


def _heads_to_groups(w):
    k = w.shape[0]
    return _pad_last(w.reshape(k, ML_HEADS, ML_HEAD_DIM).transpose(1, 0, 2), GROUP)


def _groups_to_heads(g):
    return g[:, :, :ML_HEAD_DIM].transpose(1, 0, 2).reshape(g.shape[1], D_TOK)


def _cols_to_groups(w):
    k, n = w.shape
    return w.reshape(k, n // GROUP, GROUP).transpose(1, 0, 2)


def _groups_to_cols(g):
    n, k, _ = g.shape
    return g.transpose(1, 0, 2).reshape(k, n * GROUP)


def _chips_to_cols(a):
    return a.transpose(1, 0, 2).reshape(a.shape[1], -1)


def _cols_to_chips(w):
    k, n = w.shape
    return w.reshape(k, N_CHIPS, n // N_CHIPS).transpose(1, 0, 2)


def _mlstm_in_groups(w):
    parts = [_heads_to_groups(w[:, i * D_TOK:(i + 1) * D_TOK]) for i in range(4)]
    gates = _pad_last(w[:, 4 * D_TOK:4 * D_TOK + 2 * ML_HEADS], GROUP)[None]
    qmem = w[:, 4 * D_TOK + 2 * ML_HEADS:][None]
    return jnp.concatenate(parts + [qmem, gates], axis=0)


def _mlstm_in_ungroup(g):
    parts = [_groups_to_heads(g[4 * i:4 * i + 4]) for i in range(4)]
    return jnp.concatenate(parts + [g[17][:, :2 * ML_HEADS], g[16]], axis=1)


def _taps_to_groups(w, width):
    taps = w.shape[0]
    g = _pad_last(w.reshape(taps, -1, width), GROUP).transpose(1, 0, 2)
    return jnp.pad(g, ((0, 0), (0, 8 - taps), (0, 0)))


def _groups_to_taps(g, taps, width):
    return g[:, :taps, :width].transpose(1, 0, 2).reshape(taps, -1)


SMALL_IN_COLS = 384
SMALL_OUT_COLS = 1536
SECTION = 8


class _Gathered:
    def __init__(self, make_src, groups, me, after):
        self.groups, self.states, self.ready = groups, [], {}
        self.group_of = {k: gi for gi, g in enumerate(groups) for k in g}
        token, self.first = after, None
        for gi, g in enumerate(groups):
            srcs = [make_src(k, None if gi == 0 else token[0:1, 0:1]) for k in g]
            placed = place_own([(a, ()) for a in srcs], me, token, f"place_own_{gi}")
            state, token = gather_start(srcs, placed, token, f"gather_start_{gi}")
            self.states.append(state)
            if gi == 0:
                self.first = token[0:1, 0:1]
        self.started = token

    def _get(self, key, after):
        gi = self.group_of[key]
        if gi not in self.ready:
            got = gather_wait(self.states[gi], after if gi else self.started, f"gather_wait_{gi}")
            self.ready[gi] = dict(zip(self.groups[gi], gather_pass_on(got, self.states[gi][3], f"gather_pass_on_{gi}")))
        return self.ready[gi][key]

    def ffn(self, l, i, after):
        return tuple(self._get((n, l, i), after) for n in ("wg", "wu", "wd"))

    def mixer(self, l, after):
        win = _chips_to_cols(self._get(("win", l), after))
        win = _cols_to_groups(win) if l % 2 == 0 else _mlstm_in_groups(win)
        wkv = _cols_to_groups(self._get(("wkv", l), after).reshape(D_MODEL, 2 * D_XA))
        wout = self._get(("wout", l), after)
        if l % 2:
            wout = wout.reshape(D_MODEL, D_MODEL)
            tok = jnp.pad(wout[:D_TOK].reshape(ML_HEADS, ML_HEAD_DIM, D_MODEL), ((0, 0), (0, GROUP - ML_HEAD_DIM), (0, 0)))
            wout = jnp.concatenate([tok, wout[D_TOK:][None]], axis=0)
        return win, wkv, wout


class _GradSink:
    def __init__(self, apply):
        self.queue, self.apply, self.count, self.done = [], apply, 0, None

    @staticmethod
    def _by_chip(key, g):
        if key[0] == "wkv":
            return _groups_to_cols(g).reshape(N_CHIPS, D_MODEL // N_CHIPS, 2 * D_XA)
        if key[0] == "win":
            return _cols_to_chips(_groups_to_cols(g) if key[1] % 2 == 0 else _mlstm_in_ungroup(g))
        if key[0] == "wout" and key[1] % 2:
            full = jnp.concatenate([g[:ML_HEADS, :ML_HEAD_DIM].reshape(D_TOK, D_MODEL), g[ML_HEADS]], axis=0)
            return full.reshape(N_CHIPS, D_MODEL // N_CHIPS, D_MODEL)
        return g

    def push(self, grads):
        keys = list(grads)
        state, token = rs_begin([self._by_chip(k, grads[k]) for k in keys], self.done, f"rs_start_{self.count}")
        if self.queue:
            self._finish(token)
        self.queue.append((keys, state, self.count))
        self.count += 1
        return token

    def flush(self):
        self._finish(self.done)

    def _finish(self, after):
        keys, state, i = self.queue.pop(0)
        for key, g in zip(keys, rs_end(state, after, f"rs_wait_{i}")):
            self.done = self.apply(key, g, self.done)


def _local_step(x, mem, tgt, P, weights, sink):
    memb = mem.astype(BF16)
    saved = []
    pin0 = getattr(weights, "first", None)
    X, Xb = x, (x if pin0 is None else x + pin0).astype(BF16)
    after = Xb
    for l in range(DEPTH):
        s = {}
        s["x0b"] = Xb
        s["wa"] = weights.ffn(l, 0, after)
        s["g1a"], s["u1a"], s["ha"], s["z1"], X1, X1b = ffn_fwd(Xb, X, *s["wa"], P["ln_g"][l][0], P["ln_b"][l][0])
        s["x1b"] = X1b
        s["wm"] = win, wkv, wout = weights.mixer(l, X1b)
        u = proj(X1b, win, "mixer_in")
        kv = proj(memb, wkv, "mem_kv")
        s["u"], s["kv"] = u, kv
        if l % 2 == 0:
            tok = conv_mixer_fwd(u, P["convw"])
            qg = 9
        else:
            s["qk"] = qk_conv_fwd(u, P["qkw"])
            s["hm"], s["cst"], s["mst"] = mlstm_fwd(s["qk"], u, P["bg"])
            tok = head_norm_fwd(s["hm"], u, P["hg"])
            qg = 16
        s["m"] = xattn_fwd(u, qg, kv, tok)
        s["z2"], X2, X2b = contract_ln(s["m"], wout, X1, P["ln_g"][l][1], P["ln_b"][l][1], 1.0, "mixer_out_ln")
        s["x2b"] = X2b
        s["wb"] = weights.ffn(l, 1, X2b)
        s["g1b"], s["u1b"], s["hb"], s["z3"], X, Xb = ffn_fwd(X2b, X2, *s["wb"], P["ln_g"][l][2], P["ln_b"][l][2])
        after = Xb
        saved.append(s)

    loss, dX = loss_grad(X, tgt)

    G = {"ln_g": [[None] * 3 for _ in range(DEPTH)], "ln_b": [[None] * 3 for _ in range(DEPTH)]}
    pin = [jnp.zeros((1, 1), F32)]

    def ffn_backward(l, i, dX, z, xinb, g1, u1, h, w):
        k = 2 * i
        dgb, dub, dx, dyb, G["ln_g"][l][k], G["ln_b"][l][k] = ffn_bwd(dX, z, P["ln_g"][l][k] + pin[0], w[2], w[0], w[1], g1, u1)
        grads = {("wd", l, i): wgrad(h, dyb, BF16, "wgrad_down"), ("wg", l, i): wgrad(dgb, xinb, BF16, "wgrad_gate"),
                 ("wu", l, i): wgrad(dub, xinb, BF16, "wgrad_up")}
        return dx, grads

    for l in reversed(range(DEPTH)):
        s = saved[l]
        win, wkv, wout = s["wm"]
        dX, grads = ffn_backward(l, 1, dX, s["z3"], s["x2b"], s["g1b"], s["u1b"], s["hb"], s["wb"])
        dm, dz2, dz2b, G["ln_g"][l][1], G["ln_b"][l][1] = mixer_out_bwd(dX, s["z2"], P["ln_g"][l][1], wout)
        grads[("wout", l)] = wgrad(s["m"], dz2b, BF16, "wgrad_out")
        u, kv = s["u"], s["kv"]
        if l % 2 == 0:
            db, dc, dxi, G["convw"] = conv_mixer_bwd(u, P["convw"], dm)
            dq, dkv = xattn_bwd(u, 9, kv, dm, 3)
            du = jnp.concatenate([db, dc, dxi, dq], axis=0)
        else:
            dh, du, G["hg"] = head_norm_bwd(s["hm"], u, P["hg"], dm)
            dqk, du, dgate, G["bg"] = mlstm_bwd(s["qk"], u, P["bg"], s["cst"], s["mst"], dh, du)
            du, G["qkw"] = qk_conv_bwd(u, P["qkw"], dqk, du)
            du, dkv = xattn_bwd(u, 16, kv, dm, 4, du, dgate)
        grads[("win", l)] = wgrad(s["x1b"], du, BF16, "wgrad_in")
        grads[("wkv", l)] = wgrad(memb, dkv.astype(BF16), BF16, "wgrad_kv")
        dX = contract_t(du, win, dz2, "mixer_in_bwd")
        pin[0] = sink.push(grads)[0:1, 0:1]
        dX, grads = ffn_backward(l, 0, dX, s["z1"], s["x0b"], s["g1a"], s["u1a"], s["ha"], s["wa"])
        pin[0] = sink.push(grads)[0:1, 0:1]
    sink.flush()
    return loss, dX, G


def kernel(x, mem, ln_g, ln_b, ffn_w_gate, ffn_w_up, ffn_w_down, w_kv_mem, w_out, w_in_conv, conv_w, w_in_mlstm, b_gates, qk_conv_w, head_norm_g, loss_target, m_ln_g, m_ln_b, m_ffn_w_gate, m_ffn_w_up, m_ffn_w_down, m_w_kv_mem, m_w_out, m_w_in_conv, m_conv_w, m_w_in_mlstm, m_b_gates, m_qk_conv_w, m_head_norm_g, v_ln_g, v_ln_b, v_ffn_w_gate, v_ffn_w_up, v_ffn_w_down, v_w_kv_mem, v_w_out, v_w_in_conv, v_conv_w, v_w_in_mlstm, v_b_gates, v_qk_conv_w, v_head_norm_g):
    cx, cy = lax.axis_index("x"), lax.axis_index("y")
    chip = 2 * cx + cy

    def make_src(key, pin):
        if key[0] in ("wg", "wu"):
            w = jnp.swapaxes((ffn_w_gate if key[0] == "wg" else ffn_w_up)[key[1], key[2]], 0, 1)
        elif key[0] == "wd":
            w = ffn_w_down[key[1], key[2]]
        elif key[0] == "win":
            w = (w_in_conv, w_in_mlstm)[key[1]][0]
        else:
            w = (w_kv_mem if key[0] == "wkv" else w_out)[key[1]]
        return (w if pin is None else w + pin).astype(BF16)

    ffn_keys = lambda l, i: [("wg", l, i), ("wu", l, i), ("wd", l, i)]
    mixer_keys = lambda l: [("win", l), ("wkv", l), ("wout", l)]
    groups = [ffn_keys(0, 0), mixer_keys(0) + mixer_keys(1), ffn_keys(0, 1), ffn_keys(1, 0), ffn_keys(1, 1)]
    def section(a, width):
        a = a.reshape(-1, a.shape[-1])
        return jnp.pad(a, ((0, SECTION - a.shape[0]), (0, width - a.shape[1])))

    small = jnp.concatenate([section(a, SMALL_IN_COLS) for a in (ln_g, ln_b, conv_w, qk_conv_w)], axis=0)
    smalls = small_allgather(small, reduce=False)
    gathered = _Gathered(make_src, groups, jnp.reshape(chip, (1,)).astype(jnp.int32), smalls)
    smalls = smalls[0::2]
    ln_g_full = _chips_to_cols(smalls[:, 0:6, 0:256]).reshape(DEPTH, 3, 1, D_MODEL)
    ln_b_full = _chips_to_cols(smalls[:, 8:14, 0:256]).reshape(DEPTH, 3, 1, D_MODEL)
    conv_w_full = _chips_to_cols(smalls[:, 16:19, 0:192])
    qk_w_full = _chips_to_cols(smalls[:, 24:28, 0:384])

    P = {"ln_g": ln_g_full, "ln_b": ln_b_full, "convw": _taps_to_groups(conv_w_full, GROUP),
         "qkw": _taps_to_groups(qk_w_full, ML_HEAD_DIM), "bg": _pad_last(b_gates, GROUP),
         "hg": _pad_last(head_norm_g[0], GROUP)[:, None, :]}

    weights = {"ln_g": ln_g, "ln_b": ln_b, "ffn_w_gate": ffn_w_gate, "ffn_w_up": ffn_w_up, "ffn_w_down": ffn_w_down,
               "w_kv_mem": w_kv_mem, "w_out": w_out, "w_in_conv": w_in_conv, "conv_w": conv_w, "w_in_mlstm": w_in_mlstm,
               "b_gates": b_gates, "qk_conv_w": qk_conv_w, "head_norm_g": head_norm_g}
    ms = {"ln_g": m_ln_g, "ln_b": m_ln_b, "ffn_w_gate": m_ffn_w_gate, "ffn_w_up": m_ffn_w_up, "ffn_w_down": m_ffn_w_down,
          "w_kv_mem": m_w_kv_mem, "w_out": m_w_out, "w_in_conv": m_w_in_conv, "conv_w": m_conv_w, "w_in_mlstm": m_w_in_mlstm,
          "b_gates": m_b_gates, "qk_conv_w": m_qk_conv_w, "head_norm_g": m_head_norm_g}
    vs = {"ln_g": v_ln_g, "ln_b": v_ln_b, "ffn_w_gate": v_ffn_w_gate, "ffn_w_up": v_ffn_w_up, "ffn_w_down": v_ffn_w_down,
          "w_kv_mem": v_w_kv_mem, "w_out": v_w_out, "w_in_conv": v_w_in_conv, "conv_w": v_conv_w, "w_in_mlstm": v_w_in_mlstm,
          "b_gates": v_b_gates, "qk_conv_w": v_qk_conv_w, "head_norm_g": v_head_norm_g}
    names = list(weights)
    owner = {"wg": ("ffn_w_gate", True), "wu": ("ffn_w_up", True), "wd": ("ffn_w_down", False), "wkv": ("w_kv_mem", False),
             "wout": ("w_out", False), "win": None}
    updated = {}

    def apply(key, g, after):
        name, transposed = owner[key[0]] or (("w_in_conv", "w_in_mlstm")[key[1]], False)
        idx = (0,) if key[0] == "win" else tuple(key[1:])
        view = (lambda a: jnp.swapaxes(a, -1, -2)) if transposed else (lambda a: a)
        updated[name], token = adamw_into(view(weights[name]), view(ms[name]), view(vs[name]), g, updated.get(name), idx, after,
                                          "adamw_" + name + "_" + "_".join(map(str, idx)))
        return token

    sink = _GradSink(apply)
    loss, grad_x, G = _local_step(x[0], mem[0], loss_target[0], P, gathered, sink)

    dln_g = jnp.concatenate([G["ln_g"][l][k] for l in range(DEPTH) for k in range(3)], axis=0)
    dln_b = jnp.concatenate([G["ln_b"][l][k] for l in range(DEPTH) for k in range(3)], axis=0)
    lane = lax.broadcasted_iota(jnp.int32, (1, GROUP), 1)
    misc = jnp.where(lane < 8, G["bg"], 0.0) + jnp.where(lane == 8, loss, 0.0) + sink.done[0:1, 0:1]
    parts = (dln_g, dln_b, _groups_to_taps(G["convw"], 3, GROUP), misc, _groups_to_taps(G["qkw"], 4, ML_HEAD_DIM),
             G["hg"][:, 0, :ML_HEAD_DIM])
    tot = small_allgather(jnp.concatenate([section(a, SMALL_OUT_COLS) for a in parts], axis=0), reduce=True)
    loss_total = tot[24, 8]

    small_grads = {
        "ln_g": lax.dynamic_slice(tot[0:6, 0:D_MODEL], (0, chip * 256), (6, 256)).reshape(DEPTH, 3, 256),
        "ln_b": lax.dynamic_slice(tot[8:14, 0:D_MODEL], (0, chip * 256), (6, 256)).reshape(DEPTH, 3, 256),
        "conv_w": lax.dynamic_slice(tot[16:19, 0:D_TOK], (0, chip * 192), (3, 192))[None],
        "b_gates": tot[24:25, 0:8],
        "qk_conv_w": lax.dynamic_slice(tot[32:36, 0:2 * D_TOK], (0, chip * 384), (4, 384))[None],
        "head_norm_g": tot[40:44, 0:ML_HEAD_DIM][None],
    }
    grads, deltas, new_m, new_v = [], [], [], []
    for nme in names:
        if nme in updated:
            back = (lambda a: jnp.swapaxes(a, -1, -2)) if nme in ("ffn_w_gate", "ffn_w_up") else (lambda a: a)
            g, d, nm, nv = (back(a) for a in updated[nme])
        else:
            w, g = weights[nme], small_grads[nme]
            two = (math.prod(w.shape[:-1]), w.shape[-1])
            d, nm, nv = (a.reshape(w.shape) for a in adamw(w.reshape(two), g.reshape(two), ms[nme].reshape(two),
                                                           vs[nme].reshape(two), "adamw_" + nme))
        grads.append(g)
        deltas.append(d)
        new_m.append(nm)
        new_v.append(nv)
    return (loss_total, grad_x[None], *grads, *deltas, *new_m, *new_v)
```

```python
import functools
import math

import jax
import jax.numpy as jnp
from jax import lax
from jax.experimental import pallas as pl
from jax.experimental.pallas import tpu as pltpu

F32 = jnp.float32
BF16 = jnp.bfloat16
SDS = jax.ShapeDtypeStruct

D_MODEL = 1024
DEPTH = 2
N_MEM = 256
XA_HEADS = 4
XA_HEAD_DIM = 64
D_XA = 256
D_TOK = 768
ML_HEADS = 4
ML_HEAD_DIM = 192
ML_CHUNK = 64
D_FF = 2816
LN_EPS = 1e-5
ALPHA = (2.0 * DEPTH) ** 0.25
N_CHIPS = 4
N_DEV = 8
FF_SHARD = D_FF // N_CHIPS
GROUP = 256
NEG = -1e30

ADAM_LR = 0.001
ADAM_B1 = 0.9
ADAM_B2 = 0.999
ADAM_EPS = 1e-08
ADAM_WD = 0.01
ADAM_STEP = 10

VMEM_LIMIT = 56 * 1024 * 1024

NN = ((1,), (0,))
NT = ((1,), (1,))
TN = ((0,), (0,))
MESH = pl.DeviceIdType.MESH


def _dot(a, b, dims):
    return lax.dot_general(a, b, (dims, ((), ())), preferred_element_type=F32)


def _bdot(a, b, ca, cb):
    dims = (((ca,), (cb,)), ((0,), (0,)))
    ah, bh = a.astype(BF16), b.astype(BF16)
    al, bl = (a - ah.astype(F32)).astype(BF16), (b - bh.astype(F32)).astype(BF16)
    dot = functools.partial(lax.dot_general, dimension_numbers=dims, preferred_element_type=F32)
    return dot(ah, bh) + dot(al, bh) + dot(ah, bl)


def _bdot1(a, b, ca, cb):
    return lax.dot_general(a.astype(BF16), b.astype(BF16), (((ca,), (cb,)), ((0,), (0,))), preferred_element_type=F32)


def _sigmoid(x):
    return 1.0 / (1.0 + jnp.exp(-x))


def _params(sem, vmem=VMEM_LIMIT):
    return pltpu.CompilerParams(dimension_semantics=sem, vmem_limit_bytes=vmem)


def _tile(n, want):
    t = min(n, want)
    assert n % t == 0, (n, t)
    return t


def _layer_norm(z, gamma, beta):
    mu = jnp.mean(z, axis=-1, keepdims=True)
    zc = z - mu
    var = jnp.mean(zc * zc, axis=-1, keepdims=True)
    return zc * lax.rsqrt(var + LN_EPS) * gamma + beta


def _column_halves(n):
    mid = -(-n // (2 * 128)) * 128
    return ((0, mid), (mid, n))


def _resident(shape):
    return pl.BlockSpec(shape, lambda *_: (0,) * len(shape), pipeline_mode=pl.Buffered(1))


def _group_block(G, want):
    return max(d for d in range(1, max(1, min(G, want)) + 1) if G % d == 0)


def ffn_fwd(xb, x, wg, wu, wd, gamma, beta):
    S, K = xb.shape
    G, N, _ = wg.shape
    ts = _tile(S, 512)

    def body(xb_ref, x_ref, wg_ref, wu_ref, wd_ref, gm_ref, bt_ref, g_ref, u_ref, h_ref, z_ref, xn_ref, xnb_ref):
        j = pl.program_id(1)
        xv = xb_ref[...]
        g = _dot(xv, wg_ref[j], NT)
        u = _dot(xv, wu_ref[j], NT)
        h = (g * _sigmoid(g) * u).astype(BF16)
        g_ref[0] = g.astype(BF16)
        u_ref[0] = u.astype(BF16)
        h_ref[0] = h
        y = _dot(h, wd_ref[j], NN)

        @pl.when(j == 0)
        def _():
            z_ref[...] = y

        @pl.when(j > 0)
        def _():
            z_ref[...] += y

        @pl.when(j == G - 1)
        def _():
            z = ALPHA * x_ref[...] + 0.5 * z_ref[...]
            xn = _layer_norm(z, gm_ref[...], bt_ref[...])
            z_ref[...] = z
            xn_ref[...] = xn
            xnb_ref[...] = xn.astype(BF16)

    row = pl.BlockSpec((ts, K), lambda s, j: (s, 0))
    vec = pl.BlockSpec((1, K), lambda s, j: (0, 0))
    wspec = _resident((G, N, K))
    ospec = pl.BlockSpec((1, ts, N), lambda s, j: (j, s, 0))
    return pl.pallas_call(
        body, name="ffn_fwd", grid=(S // ts, G),
        in_specs=[row, row, wspec, wspec, wspec, vec, vec],
        out_specs=[ospec, ospec, ospec, row, row, row],
        out_shape=[SDS((G, S, N), BF16), SDS((G, S, N), BF16), SDS((G, S, N), BF16),
                   SDS((S, K), F32), SDS((S, K), F32), SDS((S, K), BF16)],
        compiler_params=_params(("parallel", "arbitrary")),
    )(xb, x, wg, wu, wd, gamma, beta)


def proj(xb, w, name):
    S, K = xb.shape
    G, _, N = w.shape
    ts = _tile(S, 1024)
    gb = _group_block(G, 6)

    def body(x_ref, w_ref, y_ref):
        xv = x_ref[...]
        for j in range(gb):
            y_ref[j] = _dot(xv, w_ref[j], NN)

    return pl.pallas_call(
        body, name=name, grid=(S // ts, G // gb),
        in_specs=[pl.BlockSpec((ts, K), lambda s, g: (s, 0)), pl.BlockSpec((gb, K, N), lambda s, g: (g, 0, 0))],
        out_specs=pl.BlockSpec((gb, ts, N), lambda s, g: (g, s, 0)),
        out_shape=SDS((G, S, N), F32),
        compiler_params=_params(("parallel", "parallel")),
    )(xb, w)


def contract_ln(a, w, xres, gamma, beta, scale, name):
    G, S, Kg = a.shape
    N = w.shape[2]
    ts = _tile(S, 1024)

    def body(a_ref, w_ref, x_ref, g_ref, b_ref, z_ref, xn_ref, xb_ref):
        acc = _dot(a_ref[0], w_ref[0], NN)
        for j in range(1, G):
            acc = acc + _dot(a_ref[j], w_ref[j], NN)
        z = ALPHA * x_ref[...] + scale * acc
        xn = _layer_norm(z, g_ref[...], b_ref[...])
        z_ref[...] = z
        xn_ref[...] = xn
        xb_ref[...] = xn.astype(BF16)

    row = pl.BlockSpec((ts, N), lambda s: (s, 0))
    vec = pl.BlockSpec((1, N), lambda s: (0, 0))
    return pl.pallas_call(
        body, name=name, grid=(S // ts,),
        in_specs=[pl.BlockSpec((G, ts, Kg), lambda s: (0, s, 0)), pl.BlockSpec((G, Kg, N), lambda s: (0, 0, 0)), row, vec, vec],
        out_specs=[row, row, row],
        out_shape=[SDS((S, N), F32), SDS((S, N), F32), SDS((S, N), BF16)],
        compiler_params=_params(("parallel",)),
    )(a, w, xres, gamma, beta)


def _layer_norm_bwd(dx, z, gamma):
    mu = jnp.mean(z, axis=-1, keepdims=True)
    zc = z - mu
    var = jnp.mean(zc * zc, axis=-1, keepdims=True)
    rstd = lax.rsqrt(var + LN_EPS)
    xhat = zc * rstd
    dxh = dx * gamma
    m1 = jnp.mean(dxh, axis=-1, keepdims=True)
    m2 = jnp.mean(dxh * xhat, axis=-1, keepdims=True)
    return rstd * (dxh - m1 - xhat * m2), jnp.sum(dx * xhat, axis=0, keepdims=True), jnp.sum(dx, axis=0, keepdims=True)


def ffn_bwd(dxn, z, gamma, wd, wg, wu, g1, u1):
    S, K = dxn.shape
    G, N, _ = wd.shape
    ts = _tile(S, 512)

    def body(dxn_ref, z_ref, gm_ref, wd_ref, wg_ref, wu_ref, g_ref, u_ref, dg_ref, du_ref, dx_ref, dy_ref, dgm_ref, dbt_ref):
        s, j = pl.program_id(0), pl.program_id(1)

        @pl.when((s == 0) & (j == 0))
        def _():
            dgm_ref[...] = jnp.zeros_like(dgm_ref)
            dbt_ref[...] = jnp.zeros_like(dbt_ref)

        @pl.when(j == 0)
        def _():
            dz, dgm, dbt = _layer_norm_bwd(dxn_ref[...], z_ref[...], gm_ref[...])
            dgm_ref[...] += dgm
            dbt_ref[...] += dbt
            dx_ref[...] = ALPHA * dz
            dy_ref[...] = (0.5 * dz).astype(BF16)

        dy = dy_ref[...]
        part = None
        for a, b in _column_halves(N):
            dh = _dot(dy, wd_ref[j, a:b, :], NT)
            g = g_ref[0, :, a:b].astype(F32)
            sig = _sigmoid(g)
            dg = (dh * u_ref[0, :, a:b].astype(F32) * (sig * (1.0 + g * (1.0 - sig)))).astype(BF16)
            du = (dh * (g * sig)).astype(BF16)
            dg_ref[0, :, a:b] = dg
            du_ref[0, :, a:b] = du
            p = _dot(dg, wg_ref[j, a:b, :], NN) + _dot(du, wu_ref[j, a:b, :], NN)
            part = p if part is None else part + p
        dx_ref[...] += part

    row = pl.BlockSpec((ts, K), lambda s, j: (s, 0))
    vec = pl.BlockSpec((1, K), lambda s, j: (0, 0))
    gspec = pl.BlockSpec((1, ts, N), lambda s, j: (j, s, 0))
    wspec = _resident((G, N, K))
    return pl.pallas_call(
        body, name="ffn_bwd", grid=(S // ts, G),
        in_specs=[row, row, vec, wspec, wspec, wspec, gspec, gspec],
        out_specs=[gspec, gspec, row, row, vec, vec],
        out_shape=[SDS((G, S, N), BF16), SDS((G, S, N), BF16), SDS((S, K), F32), SDS((S, K), BF16),
                   SDS((1, K), F32), SDS((1, K), F32)],
        compiler_params=_params(("arbitrary", "arbitrary")),
    )(dxn, z, gamma, wd, wg, wu, g1, u1)


def mixer_out_bwd(dxn, z, gamma, w):
    S, N = dxn.shape
    G, Kg, _ = w.shape
    ts = _tile(S, 512)

    def body(dxn_ref, z_ref, gm_ref, w_ref, dm_ref, dz_ref, dzb_ref, dgm_ref, dbt_ref):
        @pl.when(pl.program_id(0) == 0)
        def _():
            dgm_ref[...] = jnp.zeros_like(dgm_ref)
            dbt_ref[...] = jnp.zeros_like(dbt_ref)

        dz, dgm, dbt = _layer_norm_bwd(dxn_ref[...], z_ref[...], gm_ref[...])
        dgm_ref[...] += dgm
        dbt_ref[...] += dbt
        dzb = dz.astype(BF16)
        dz_ref[...] = dz
        dzb_ref[...] = dzb
        for j in range(G):
            dm_ref[j] = _dot(dzb, w_ref[j], NT)

    row = pl.BlockSpec((ts, N), lambda s: (s, 0))
    vec = pl.BlockSpec((1, N), lambda s: (0, 0))
    return pl.pallas_call(
        body, name="mixer_out_bwd", grid=(S // ts,),
        in_specs=[row, row, vec, pl.BlockSpec((G, Kg, N), lambda s: (0, 0, 0))],
        out_specs=[pl.BlockSpec((G, ts, Kg), lambda s: (0, s, 0)), row, row, vec, vec],
        out_shape=[SDS((G, S, Kg), F32), SDS((S, N), F32), SDS((S, N), BF16), SDS((1, N), F32), SDS((1, N), F32)],
        compiler_params=_params(("arbitrary",)),
    )(dxn, z, gamma, w)


def contract_t(da, w, res, name):
    G, S, Ng = da.shape
    K = w.shape[1]
    ts = _tile(S, 1024)
    gb = _group_block(G, 6)

    def body(da_ref, w_ref, r_ref, o_ref):
        g = pl.program_id(1)
        part = _dot(da_ref[0], w_ref[0], NT)
        for j in range(1, gb):
            part = part + _dot(da_ref[j], w_ref[j], NT)

        @pl.when(g == 0)
        def _():
            o_ref[...] = ALPHA * r_ref[...] + part

        @pl.when(g > 0)
        def _():
            o_ref[...] += part

    row = pl.BlockSpec((ts, K), lambda s, g: (s, 0))
    return pl.pallas_call(
        body, name=name, grid=(S // ts, G // gb),
        in_specs=[pl.BlockSpec((gb, ts, Ng), lambda s, g: (g, s, 0)), pl.BlockSpec((gb, K, Ng), lambda s, g: (g, 0, 0)), row],
        out_specs=row,
        out_shape=SDS((S, K), F32),
        compiler_params=_params(("parallel", "arbitrary")),
    )(da, w, res)


WGRAD_ACC_ELEMS = 6 * 1024 * 256


def wgrad(a, b, out_dtype, name):
    ga, gb = a.ndim == 3, b.ndim == 3
    G = a.shape[0] if ga else b.shape[0]
    S, K = a.shape[-2:]
    N = b.shape[-1]
    ts = _tile(S, 2048)
    ns = S // ts
    ng = _group_block(G, WGRAD_ACC_ELEMS // (K * N))

    def body(a_ref, b_ref, o_ref, acc):
        s = pl.program_id(1)

        @pl.when(s == 0)
        def _():
            acc[...] = jnp.zeros_like(acc)

        for j in range(ng):
            acc[j] += _dot(a_ref[j] if ga else a_ref[...], b_ref[j] if gb else b_ref[...], TN)

        @pl.when(s == ns - 1)
        def _():
            o_ref[...] = acc[...].astype(out_dtype)

    aspec = pl.BlockSpec((ng, ts, K), lambda g, s: (g, s, 0)) if ga else pl.BlockSpec((ts, K), lambda g, s: (s, 0))
    bspec = pl.BlockSpec((ng, ts, N), lambda g, s: (g, s, 0)) if gb else pl.BlockSpec((ts, N), lambda g, s: (s, 0))
    return pl.pallas_call(
        body, name=name, grid=(G // ng, ns),
        in_specs=[aspec, bspec],
        out_specs=pl.BlockSpec((ng, K, N), lambda g, s: (g, 0, 0)),
        out_shape=SDS((G, K, N), out_dtype),
        scratch_shapes=[pltpu.VMEM((ng, K, N), F32)],
        compiler_params=_params(("parallel", "arbitrary")),
    )(a, b)


def loss_grad(xn, tgt):
    S, N = xn.shape
    ts = _tile(S, 1024)

    def body(x_ref, t_ref, l_ref, dx_ref):
        @pl.when(pl.program_id(0) == 0)
        def _():
            l_ref[...] = jnp.zeros_like(l_ref)

        e = x_ref[...] - t_ref[...]
        dx_ref[...] = e * (1.0 / N)
        l_ref[...] += 0.5 * jnp.sum(jnp.mean(e * e, axis=-1, keepdims=True), axis=0, keepdims=True)

    row = pl.BlockSpec((ts, N), lambda s: (s, 0))
    return pl.pallas_call(
        body, name="loss_grad", grid=(S // ts,),
        in_specs=[row, row],
        out_specs=[pl.BlockSpec((1, 1), lambda s: (0, 0)), row],
        out_shape=[SDS((1, 1), F32), SDS((S, N), F32)],
        compiler_params=_params(("arbitrary",)),
    )(xn, tgt)


def _shift_down(x, k):
    if k == 0:
        return x
    rows = lax.broadcasted_iota(jnp.int32, x.shape, 0)
    return jnp.where(rows >= k, pltpu.roll(x, k, 0), 0.0)


def _shift_up(x, k):
    if k == 0:
        return x
    n = x.shape[0]
    rows = lax.broadcasted_iota(jnp.int32, x.shape, 0)
    return jnp.where(rows < n - k, pltpu.roll(x, n - k, 0), 0.0)


LANES = 128


def conv_mixer_fwd(u, cw):
    _, S, _ = u.shape

    def body(b_ref, c_ref, x_ref, w_ref, o_ref):
        p = c_ref[0] * x_ref[0]
        w = w_ref[0]
        conv = w[2:3] * p + w[1:2] * _shift_down(p, 1) + w[0:1] * _shift_down(p, 2)
        o_ref[0] = (b_ref[0] * conv).astype(BF16)

    def uspec(off):
        return pl.BlockSpec((1, S, GROUP), lambda g: (g + off, 0, 0))

    return pl.pallas_call(
        body, name="conv_mixer_fwd", grid=(3,),
        in_specs=[uspec(0), uspec(3), uspec(6), pl.BlockSpec((1, 8, GROUP), lambda g: (g, 0, 0))],
        out_specs=pl.BlockSpec((1, S, GROUP), lambda g: (g, 0, 0)),
        out_shape=SDS((4, S, GROUP), BF16),
        compiler_params=_params(("parallel",)),
    )(u, u, u, cw)


def conv_mixer_bwd(u, cw, dm):
    _, S, _ = u.shape
    nh = GROUP // LANES

    def body(b_ref, c_ref, x_ref, w_ref, d_ref, db_ref, dc_ref, dx_ref, dw_ref):
        cg, xi = c_ref[0], x_ref[0]
        p = cg * xi
        p1, p2 = _shift_down(p, 1), _shift_down(p, 2)
        w = w_ref[0]
        conv = w[2:3] * p + w[1:2] * p1 + w[0:1] * p2
        dt = d_ref[0]
        db_ref[0] = (dt * conv).astype(BF16)
        dcv = dt * b_ref[0]
        dp = w[2:3] * dcv + w[1:2] * _shift_up(dcv, 1) + w[0:1] * _shift_up(dcv, 2)
        dc_ref[0] = (dp * xi).astype(BF16)
        dx_ref[0] = (dp * cg).astype(BF16)
        dw = jnp.concatenate([jnp.sum(dcv * p2, axis=0, keepdims=True), jnp.sum(dcv * p1, axis=0, keepdims=True),
                              jnp.sum(dcv * p, axis=0, keepdims=True), jnp.zeros((5, LANES), F32)], axis=0)
        dw_ref[0] = dw

    def uspec(off):
        return pl.BlockSpec((1, S, LANES), lambda g, h: (g + off, 0, h))

    ospec = pl.BlockSpec((1, S, LANES), lambda g, h: (g, 0, h))
    wspec = pl.BlockSpec((1, 8, LANES), lambda g, h: (g, 0, h))
    return pl.pallas_call(
        body, name="conv_mixer_bwd", grid=(3, nh),
        in_specs=[uspec(0), uspec(3), uspec(6), wspec, ospec],
        out_specs=[ospec, ospec, ospec, wspec],
        out_shape=[SDS((3, S, GROUP), BF16)] * 3 + [SDS((3, 8, GROUP), F32)],
        compiler_params=_params(("parallel", "parallel")),
    )(u, u, u, cw, dm)


def qk_conv_fwd(u, qw):
    _, S, _ = u.shape

    def body(u_ref, w_ref, o_ref):
        x = u_ref[0]
        w = w_ref[0]
        pre = w[3:4] * x + w[2:3] * _shift_down(x, 1) + w[1:2] * _shift_down(x, 2) + w[0:1] * _shift_down(x, 3)
        o_ref[0] = pre * _sigmoid(pre)

    spec = pl.BlockSpec((1, S, GROUP), lambda g: (g, 0, 0))
    return pl.pallas_call(
        body, name="qk_conv_fwd", grid=(8,),
        in_specs=[spec, pl.BlockSpec((1, 8, GROUP), lambda g: (g, 0, 0))],
        out_specs=spec,
        out_shape=SDS((8, S, GROUP), F32),
        compiler_params=_params(("parallel",)),
    )(u, qw)


def qk_conv_bwd(u, qw, dqk, du):
    _, S, _ = u.shape
    nh = GROUP // LANES

    def body(u_ref, w_ref, d_ref, du_in_ref, du_ref, dw_ref):
        x = u_ref[0]
        w = w_ref[0]
        x1, x2, x3 = _shift_down(x, 1), _shift_down(x, 2), _shift_down(x, 3)
        pre = w[3:4] * x + w[2:3] * x1 + w[1:2] * x2 + w[0:1] * x3
        sig = _sigmoid(pre)
        dpre = d_ref[0].astype(F32) * (sig * (1.0 + pre * (1.0 - sig)))
        du = w[3:4] * dpre + w[2:3] * _shift_up(dpre, 1) + w[1:2] * _shift_up(dpre, 2) + w[0:1] * _shift_up(dpre, 3)
        du_ref[0] = du.astype(BF16)
        dw = jnp.concatenate([jnp.sum(dpre * x3, axis=0, keepdims=True), jnp.sum(dpre * x2, axis=0, keepdims=True),
                              jnp.sum(dpre * x1, axis=0, keepdims=True), jnp.sum(dpre * x, axis=0, keepdims=True),
                              jnp.zeros((4, LANES), F32)], axis=0)
        dw_ref[0] = dw

    spec = pl.BlockSpec((1, S, LANES), lambda g, h: (g, 0, h))
    wspec = pl.BlockSpec((1, 8, LANES), lambda g, h: (g, 0, h))
    return pl.pallas_call(
        body, name="qk_conv_bwd", grid=(8, nh),
        in_specs=[spec, wspec, spec, pl.BlockSpec(memory_space=pl.ANY)],
        out_specs=[spec, wspec],
        out_shape=[SDS(du.shape, BF16), SDS((8, 8, GROUP), F32)],
        input_output_aliases={3: 0},
        compiler_params=_params(("parallel", "parallel")),
    )(u, qw, dqk, du)


def _head_masks():
    lane = lax.broadcasted_iota(jnp.int32, (1, D_XA), 1)
    return [(lane >= h * XA_HEAD_DIM) & (lane < (h + 1) * XA_HEAD_DIM) for h in range(XA_HEADS)]


def xattn_fwd(u, qg, kv, tok):
    _, S, _ = u.shape
    ts = _tile(S, 1024)
    scale = XA_HEAD_DIM ** -0.5

    def body(q_ref, kv_ref, tok_ref, o_ref):
        q = q_ref[0]
        k = kv_ref[0].astype(BF16)
        v = kv_ref[1]
        o = jnp.zeros((ts, D_XA), F32)
        for m in _head_masks():
            s = _dot(jnp.where(m, q, 0.0).astype(BF16), k, NT) * scale
            s = s - jnp.max(s, axis=-1, keepdims=True)
            e = jnp.exp(s)
            p = e / jnp.sum(e, axis=-1, keepdims=True)
            o = o + _dot(p.astype(BF16), jnp.where(m, v, 0.0).astype(BF16), NN)
        o_ref[0] = o.astype(BF16)

    slot = tok.shape[0] - 1
    return pl.pallas_call(
        body, name="xattn_fwd", grid=(S // ts,),
        in_specs=[pl.BlockSpec((1, ts, GROUP), lambda s: (qg, s, 0)), pl.BlockSpec((2, N_MEM, GROUP), lambda s: (0, 0, 0)),
                  pl.BlockSpec(memory_space=pl.ANY)],
        out_specs=pl.BlockSpec((1, ts, GROUP), lambda s: (slot, s, 0)),
        out_shape=SDS(tok.shape, BF16),
        input_output_aliases={2: 0},
        compiler_params=_params(("parallel",)),
    )(u, kv, tok)


def xattn_bwd(u, qg, kv, dm, dg, du=None, dgate=None):
    _, S, _ = u.shape
    ts = _tile(S, 1024)
    scale = XA_HEAD_DIM ** -0.5

    def body(q_ref, kv_ref, do_ref, *refs):
        dq_ref, dkv_ref = refs[-2:]

        @pl.when(pl.program_id(0) == 0)
        def _():
            dkv_ref[...] = jnp.zeros_like(dkv_ref)

        q = q_ref[0]
        k = kv_ref[0]
        v = kv_ref[1]
        kb = k.astype(BF16)
        do = do_ref[0]
        dq = jnp.zeros((ts, D_XA), F32)
        dk = jnp.zeros((N_MEM, D_XA), F32)
        dv = jnp.zeros((N_MEM, D_XA), F32)
        for m in _head_masks():
            qm = jnp.where(m, q, 0.0).astype(BF16)
            s = _dot(qm, kb, NT) * scale
            s = s - jnp.max(s, axis=-1, keepdims=True)
            e = jnp.exp(s)
            p = e / jnp.sum(e, axis=-1, keepdims=True)
            dom = jnp.where(m, do, 0.0).astype(BF16)
            dp = _dot(dom, jnp.where(m, v, 0.0).astype(BF16), NT)
            ds = (p * (dp - jnp.sum(dp * p, axis=-1, keepdims=True)) * scale).astype(BF16)
            dq = dq + _dot(ds, jnp.where(m, k, 0.0).astype(BF16), NN)
            dk = dk + _dot(ds, qm, TN)
            dv = dv + _dot(p.astype(BF16), dom, TN)
        dq_ref[0] = dq.astype(BF16)
        if du is not None:
            dq_ref[1] = refs[0][0]
        dkv_ref[0] += dk
        dkv_ref[1] += dv

    in_specs = [pl.BlockSpec((1, ts, GROUP), lambda s: (qg, s, 0)), pl.BlockSpec((2, N_MEM, GROUP), lambda s: (0, 0, 0)),
                pl.BlockSpec((1, ts, GROUP), lambda s: (dg, s, 0))]
    args, aliases = [u, kv, dm], {}
    dq_spec, dq_shape = pl.BlockSpec((1, ts, GROUP), lambda s: (0, s, 0)), SDS((1, S, GROUP), BF16)
    if du is not None:
        in_specs += [pl.BlockSpec((1, ts, GROUP), lambda s: (0, s, 0)), pl.BlockSpec(memory_space=pl.ANY)]
        args += [dgate, du]
        aliases = {4: 0}
        dq_spec, dq_shape = pl.BlockSpec((2, ts, GROUP), lambda s: (qg // 2, s, 0)), SDS(du.shape, BF16)
    return pl.pallas_call(
        body, name="xattn_bwd", grid=(S // ts,),
        in_specs=in_specs,
        out_specs=[dq_spec, pl.BlockSpec((2, N_MEM, GROUP), lambda s: (0, 0, 0))],
        out_shape=[dq_shape, SDS((2, N_MEM, GROUP), F32)],
        input_output_aliases=aliases,
        compiler_params=_params(("arbitrary",)),
    )(*args)


ML_BLOCK_CHUNKS = 4
H4 = ML_HEADS
L = ML_CHUNK
NLANE = ML_HEAD_DIM


def _chunk_consts():
    r = lax.broadcasted_iota(jnp.int32, (1, L, L), 1)
    c = lax.broadcasted_iota(jnp.int32, (1, L, L), 2)
    return r >= c, r <= c, r == c


def _gate_cols(gb):
    lane = lax.broadcasted_iota(jnp.int32, gb.shape, 1)
    li = jnp.stack([jnp.sum(jnp.where(lane == h, gb, 0.0), axis=1, keepdims=True) for h in range(H4)])
    gf = jnp.stack([jnp.sum(jnp.where(lane == H4 + h, gb, 0.0), axis=1, keepdims=True) for h in range(H4)])
    return li, gf


def _log_sigmoid(x):
    return jnp.minimum(x, 0.0) - jnp.log(1.0 + jnp.exp(-jnp.abs(x)))


def _chunk_forward(q, k, v_aug, li_col, lf_col, c_prev, m_prev):
    tri, tri_t, eye = _chunk_consts()
    lf_row = jnp.sum(jnp.where(eye, lf_col, 0.0), axis=1, keepdims=True)
    li_row = jnp.sum(jnp.where(eye, li_col, 0.0), axis=1, keepdims=True)
    bcum_col = jnp.sum(jnp.where(tri, lf_row, 0.0), axis=2, keepdims=True)
    bcum_row = jnp.sum(jnp.where(tri_t, lf_col, 0.0), axis=1, keepdims=True)
    log_d = jnp.where(tri, bcum_col - bcum_row + li_row, NEG)
    log_inter = bcum_col + m_prev
    m_t = jnp.maximum(log_inter, jnp.max(log_d, axis=2, keepdims=True))
    w_intra = jnp.exp(log_d - m_t)
    w_inter = jnp.exp(log_inter - m_t)
    sc = _bdot(q, k, 2, 2) * w_intra
    qc = _bdot1(q, c_prev, 2, 1)
    num = _bdot(sc, v_aug, 2, 1) + w_inter * qc
    lane = lax.broadcasted_iota(jnp.int32, num.shape, 2)
    den = jnp.sum(jnp.where(lane == NLANE, num, 0.0), axis=2, keepdims=True)
    e_m = jnp.exp(-m_t)
    b_last = jnp.sum(lf_row, axis=2, keepdims=True)
    log_w = b_last - bcum_col + li_col
    m_new = jnp.maximum(b_last + m_prev, jnp.max(log_w, axis=1, keepdims=True))
    w_k = jnp.exp(log_w - m_new)
    decay = jnp.exp(b_last + m_prev - m_new)
    return dict(w_intra=w_intra, w_inter=w_inter, sc=sc, qc=qc, num=num, den=den, e_m=e_m, lane=lane,
                w_k=w_k, decay=decay, m_new=m_new)


def mlstm_fwd(qk, u, bg):
    _, S, _ = qk.shape
    nc = S // L
    cb = min(ML_BLOCK_CHUNKS, nc)
    rows = cb * L
    kscale = ML_HEAD_DIM ** -0.5

    def body(qk_ref, v_ref, g_ref, bg_ref, h_ref, cst_ref, mst_ref, c_sc, m_sc):
        @pl.when(pl.program_id(0) == 0)
        def _():
            c_sc[...] = jnp.zeros_like(c_sc)
            m_sc[...] = jnp.zeros_like(m_sc)

        for c in range(cb):
            sl = pl.ds(c * L, L)
            q = qk_ref[0:H4, sl, :]
            k = qk_ref[H4:2 * H4, sl, :] * kscale
            v = v_ref[:, sl, :]
            lane = lax.broadcasted_iota(jnp.int32, v.shape, 2)
            v_aug = jnp.where(lane == NLANE, 1.0, v)
            li_col, gf = _gate_cols(g_ref[0, sl, :] + bg_ref[...])
            lf_col = _log_sigmoid(gf)
            c_prev = c_sc[...]
            m_prev = m_sc[...]
            f = _chunk_forward(q, k, v_aug, li_col, lf_col, c_prev, m_prev)
            r = 1.0 / jnp.maximum(jnp.abs(f["den"]), f["e_m"])
            h_ref[:, sl, :] = jnp.where(lane < NLANE, f["num"] * r, 0.0)
            cst_ref[c] = c_prev
            mst_ref[c] = jnp.broadcast_to(m_prev, (H4, 1, LANES))
            c_sc[...] = f["decay"] * c_prev + _bdot(k * f["w_k"], v_aug, 1, 1)
            m_sc[...] = f["m_new"]

    def hspec(blk):
        return pl.BlockSpec((H4, rows, GROUP), lambda i: (blk, i, 0))

    return pl.pallas_call(
        body, name="mlstm_fwd", grid=(nc // cb,),
        in_specs=[pl.BlockSpec((2 * H4, rows, GROUP), lambda i: (0, i, 0)), hspec(2),
                  pl.BlockSpec((1, rows, GROUP), lambda i: (17, i, 0)), pl.BlockSpec((1, GROUP), lambda i: (0, 0))],
        out_specs=[hspec(0), pl.BlockSpec((cb, H4, GROUP, GROUP), lambda i: (i, 0, 0, 0)),
                   pl.BlockSpec((cb, H4, 1, LANES), lambda i: (i, 0, 0, 0))],
        out_shape=[SDS((H4, S, GROUP), F32), SDS((nc, H4, GROUP, GROUP), F32), SDS((nc, H4, 1, LANES), F32)],
        scratch_shapes=[pltpu.VMEM((H4, GROUP, GROUP), F32), pltpu.VMEM((H4, 1, 1), F32)],
        compiler_params=_params(("arbitrary",)),
    )(qk, u, u, bg)


def mlstm_bwd(qk, u, bg, cst, mst, dh, du):
    _, S, _ = qk.shape
    nc = S // L
    cb = min(ML_BLOCK_CHUNKS, nc)
    rows = cb * L
    nb = nc // cb
    kscale = ML_HEAD_DIM ** -0.5

    def body(qk_ref, v_ref, g_ref, bg_ref, cst_ref, mst_ref, dh_ref, du_in_ref, dqk_ref, dv_ref, dg_ref, dbg_ref, dc_sc):
        @pl.when(pl.program_id(0) == 0)
        def _():
            dc_sc[...] = jnp.zeros_like(dc_sc)
            dbg_ref[...] = jnp.zeros_like(dbg_ref)

        tri, tri_t, eye = _chunk_consts()
        for c in reversed(range(cb)):
            sl = pl.ds(c * L, L)
            q = qk_ref[0:H4, sl, :]
            k = qk_ref[H4:2 * H4, sl, :] * kscale
            v = v_ref[:, sl, :]
            lane = lax.broadcasted_iota(jnp.int32, v.shape, 2)
            v_aug = jnp.where(lane == NLANE, 1.0, v)
            li_col, gf = _gate_cols(g_ref[0, sl, :] + bg_ref[...])
            lf_col = _log_sigmoid(gf)
            c_prev = cst_ref[c]
            m_prev = mst_ref[c][:, :, 0:1]
            f = _chunk_forward(q, k, v_aug, li_col, lf_col, c_prev, m_prev)
            w_intra, w_inter, sc, num, den, e_m = f["w_intra"], f["w_inter"], f["sc"], f["num"], f["den"], f["e_m"]
            absd = jnp.abs(den)
            r = 1.0 / jnp.maximum(absd, e_m)
            dhv = dh_ref[:, sl, :]
            s1 = jnp.sum(jnp.where(lane < NLANE, dhv * num, 0.0), axis=2, keepdims=True)
            dden = jnp.where(absd > e_m, -s1 * r * r * jnp.sign(den), 0.0)
            dnum = jnp.where(lane == NLANE, dden, jnp.where(lane < NLANE, dhv * r, 0.0))
            dsc = _bdot1(dnum, v_aug, 2, 2)
            dv = _bdot1(sc, dnum, 1, 1)
            gmat = dsc * sc
            dqk = dsc * w_intra
            dq = _bdot1(dqk, k, 2, 1) + w_inter * _bdot1(dnum, c_prev, 2, 2)
            dk = _bdot1(dqk, q, 1, 1)
            dc_prev = _bdot(q * w_inter, dnum, 1, 1)
            dlog_inter = jnp.sum(dnum * f["qc"], axis=2, keepdims=True) * w_inter
            dbcum_col = dlog_inter + jnp.sum(gmat, axis=2, keepdims=True)
            g_row = jnp.sum(gmat, axis=1, keepdims=True)
            dcn = dc_sc[...]
            w_k, decay = f["w_k"], f["decay"]
            kw = k * w_k
            dc_prev = dc_prev + decay * dcn
            db_last = jnp.sum(jnp.sum(dcn * c_prev, axis=2, keepdims=True), axis=1, keepdims=True) * decay
            dkw = _bdot(v_aug, dcn, 2, 2)
            dv = dv + _bdot1(kw, dcn, 2, 1)
            dk = dk + dkw * w_k
            dlogw = jnp.sum(dkw * k, axis=2, keepdims=True) * w_k
            db_last = db_last + jnp.sum(dlogw, axis=1, keepdims=True)
            dbcum_col = dbcum_col - dlogw
            rowi = lax.broadcasted_iota(jnp.int32, (1, L, 1), 1)
            dbcum_col = dbcum_col + jnp.where(rowi == L - 1, db_last, 0.0)
            dbcum_row = jnp.sum(jnp.where(eye, dbcum_col, 0.0), axis=1, keepdims=True) - g_row
            dlf_col = jnp.sum(jnp.where(tri_t, dbcum_row, 0.0), axis=2, keepdims=True)
            dli_col = dlogw + jnp.sum(jnp.where(eye, g_row, 0.0), axis=2, keepdims=True)
            dgf_col = dlf_col * _sigmoid(-gf)
            lane_g = lax.broadcasted_iota(jnp.int32, (L, GROUP), 1)
            dg = jnp.zeros((L, GROUP), F32)
            for h in range(H4):
                dg = dg + jnp.where(lane_g == h, dli_col[h], 0.0) + jnp.where(lane_g == H4 + h, dgf_col[h], 0.0)
            dqk_ref[0:H4, sl, :] = dq.astype(BF16)
            dqk_ref[H4:2 * H4, sl, :] = (dk * kscale).astype(BF16)
            dv_ref[:, sl, :] = jnp.where(lane < NLANE, dv, 0.0).astype(BF16)
            dg_ref[0, sl, :] = dg.astype(BF16)
            dbg_ref[...] += jnp.sum(dg, axis=0, keepdims=True)
            dc_sc[...] = dc_prev

    def hspec(blk):
        return pl.BlockSpec((H4, rows, GROUP), lambda i: (blk, nb - 1 - i, 0))

    gspec = pl.BlockSpec((1, rows, GROUP), lambda i: (17, nb - 1 - i, 0))
    qkspec = pl.BlockSpec((2 * H4, rows, GROUP), lambda i: (0, nb - 1 - i, 0))
    return pl.pallas_call(
        body, name="mlstm_bwd", grid=(nb,),
        in_specs=[qkspec, hspec(2), gspec, pl.BlockSpec((1, GROUP), lambda i: (0, 0)),
                  pl.BlockSpec((cb, H4, GROUP, GROUP), lambda i: (nb - 1 - i, 0, 0, 0)),
                  pl.BlockSpec((cb, H4, 1, LANES), lambda i: (nb - 1 - i, 0, 0, 0)), hspec(0), pl.BlockSpec(memory_space=pl.ANY)],
        out_specs=[qkspec, hspec(2), pl.BlockSpec((1, rows, GROUP), lambda i: (0, nb - 1 - i, 0)),
                   pl.BlockSpec((1, GROUP), lambda i: (0, 0))],
        input_output_aliases={7: 1},
        out_shape=[SDS((2 * H4, S, GROUP), BF16), SDS(du.shape, BF16),
                   SDS((1, S, GROUP), BF16), SDS((1, GROUP), F32)],
        scratch_shapes=[pltpu.VMEM((H4, GROUP, GROUP), F32)],
        compiler_params=_params(("arbitrary",)),
    )(qk, u, u, bg, cst, mst, dh, du)


def head_norm_fwd(hm, u, hg):
    _, S, _ = hm.shape
    ts = _tile(S, 2048)

    def body(h_ref, o_ref, g_ref, t_ref):
        h = h_ref[0]
        lane = lax.broadcasted_iota(jnp.int32, h.shape, 1)
        valid = lane < ML_HEAD_DIM
        mu = jnp.sum(h, axis=-1, keepdims=True) * (1.0 / ML_HEAD_DIM)
        hc = jnp.where(valid, h - mu, 0.0)
        var = jnp.sum(hc * hc, axis=-1, keepdims=True) * (1.0 / ML_HEAD_DIM)
        hn = hc * lax.rsqrt(var + LN_EPS) * g_ref[0]
        t_ref[0] = (_sigmoid(o_ref[0]) * hn).astype(BF16)

    return pl.pallas_call(
        body, name="head_norm_fwd", grid=(H4, S // ts),
        in_specs=[pl.BlockSpec((1, ts, GROUP), lambda h, s: (h, s, 0)), pl.BlockSpec((1, ts, GROUP), lambda h, s: (12 + h, s, 0)),
                  pl.BlockSpec((1, 1, GROUP), lambda h, s: (h, 0, 0))],
        out_specs=pl.BlockSpec((1, ts, GROUP), lambda h, s: (h, s, 0)),
        out_shape=SDS((H4 + 1, S, GROUP), BF16),
        compiler_params=_params(("parallel", "parallel")),
    )(hm, u, hg)


def head_norm_bwd(hm, u, hg, dm):
    _, S, _ = hm.shape
    ts = _tile(S, 2048)

    def body(h_ref, o_ref, g_ref, d_ref, dh_ref, do_ref, dg_ref):
        @pl.when(pl.program_id(1) == 0)
        def _():
            dg_ref[...] = jnp.zeros_like(dg_ref)

        h = h_ref[0]
        lane = lax.broadcasted_iota(jnp.int32, h.shape, 1)
        valid = lane < ML_HEAD_DIM
        inv = 1.0 / ML_HEAD_DIM
        mu = jnp.sum(h, axis=-1, keepdims=True) * inv
        hc = jnp.where(valid, h - mu, 0.0)
        var = jnp.sum(hc * hc, axis=-1, keepdims=True) * inv
        rstd = lax.rsqrt(var + LN_EPS)
        xhat = hc * rstd
        g = g_ref[0]
        sig = _sigmoid(o_ref[0])
        dt = jnp.where(valid, d_ref[0], 0.0)
        do_ref[0] = (dt * xhat * g * sig * (1.0 - sig)).astype(BF16)
        dhn = dt * sig
        dg_ref[0] += jnp.sum(dhn * xhat, axis=0, keepdims=True)
        dxh = dhn * g
        m1 = jnp.sum(dxh, axis=-1, keepdims=True) * inv
        m2 = jnp.sum(dxh * xhat, axis=-1, keepdims=True) * inv
        dh_ref[0] = jnp.where(valid, rstd * (dxh - m1 - xhat * m2), 0.0)

    spec = pl.BlockSpec((1, ts, GROUP), lambda h, s: (h, s, 0))
    gspec = pl.BlockSpec((1, 1, GROUP), lambda h, s: (h, 0, 0))
    return pl.pallas_call(
        body, name="head_norm_bwd", grid=(H4, S // ts),
        in_specs=[spec, pl.BlockSpec((1, ts, GROUP), lambda h, s: (12 + h, s, 0)), gspec, spec],
        out_specs=[spec, pl.BlockSpec((1, ts, GROUP), lambda h, s: (12 + h, s, 0)), gspec],
        out_shape=[SDS((H4, S, GROUP), F32), SDS((u.shape[0], S, GROUP), BF16), SDS((H4, 1, GROUP), F32)],
        compiler_params=_params(("parallel", "arbitrary")),
    )(hm, u, hg, dm)


def _adamw_math(w, g, m, v):
    c1 = 1.0 / (1.0 - ADAM_B1 ** ADAM_STEP)
    c2 = 1.0 / (1.0 - ADAM_B2 ** ADAM_STEP)
    nm = ADAM_B1 * m + (1.0 - ADAM_B1) * g
    nv = ADAM_B2 * v + (1.0 - ADAM_B2) * (g * g)
    return -ADAM_LR * ((nm * c1) / (jnp.sqrt(nv * c2) + ADAM_EPS) + ADAM_WD * w), nm, nv


def _row_tile(R, cap=512):
    return R if R <= cap else max(d for d in range(8, cap + 1, 8) if R % d == 0)


def adamw_into(w, m, v, g, outs, idx, after, name):
    R, C = g.shape
    tr = _row_tile(R)
    lead = (0,) * len(idx)

    def body(w_ref, m_ref, v_ref, g_ref, *rest):
        go_ref, d_ref, nm_ref, nv_ref, token = rest[-5:]
        token[...] = jnp.zeros_like(token)
        gv = g_ref[...]
        d, nm, nv = _adamw_math(w_ref[lead], gv, m_ref[lead], v_ref[lead])
        go_ref[lead] = gv
        d_ref[lead] = d
        nm_ref[lead] = nm
        nv_ref[lead] = nv

    blk = pl.BlockSpec((1,) * len(idx) + (tr, C), lambda r: idx + (r, 0))
    any_space = pl.BlockSpec(memory_space=pl.ANY)
    in_specs, args, aliases = [blk, blk, blk, pl.BlockSpec((tr, C), lambda r: (r, 0)), any_space], [w, m, v, g, g if after is None else after], {}
    if outs is not None:
        in_specs += [any_space] * 4
        args += list(outs)
        aliases = {5 + i: i for i in range(4)}
    out = pl.pallas_call(
        body, name=name, grid=(R // tr,),
        in_specs=in_specs, out_specs=[blk] * 4 + [pl.BlockSpec((8, LANES), lambda r: (0, 0))],
        out_shape=[SDS(w.shape, F32)] * 4 + [SDS((8, LANES), F32)],
        input_output_aliases=aliases, compiler_params=_params(("arbitrary",)),
    )(*args)
    return out[:4], out[4]


def adamw(w, g, m, v, name):
    R, C = w.shape
    tr = _row_tile(R)

    def body(w_ref, g_ref, m_ref, v_ref, d_ref, nm_ref, nv_ref):
        d_ref[...], nm_ref[...], nv_ref[...] = _adamw_math(w_ref[...], g_ref[...], m_ref[...], v_ref[...])

    spec = pl.BlockSpec((tr, C), lambda i: (i, 0))
    return pl.pallas_call(
        body, name=name, grid=(R // tr,),
        in_specs=[spec] * 4, out_specs=[spec] * 3,
        out_shape=[SDS((R, C), F32)] * 3,
        compiler_params=_params(("parallel",)),
    )(w, g, m, v)


HBM = pl.BlockSpec(memory_space=pl.ANY)
ROW_SPLIT = 4
PAIR_SPLIT = 1


def _position():
    x, y, c = lax.axis_index("x"), lax.axis_index("y"), lax.axis_index("c")
    return x, y, c, [(1 - x, y), (x, 1 - y), (1 - x, 1 - y)]


def _unique(items):
    arrays = []
    for a, _ in items:
        if not any(a is b for b in arrays):
            arrays.append(a)
    return arrays, [next(i for i, b in enumerate(arrays) if b is a) for a, _ in items]


def place_own(items, me, after, name):
    arrays, src_of = _unique(items)
    n = len(items)
    shapes = [a.shape[len(p):] for a, p in items]

    def body(me_ref, *refs):
        for t in range(n):
            refs[n + 1 + t][0] = refs[t][(0,) * len(items[t][1])]

    in_specs, out_specs = [], []
    for (a, p), shp in zip(items, shapes):
        blk = shp[:-2] + (shp[-2] // ROW_SPLIT, shp[-1])
        lead = (0,) * (len(shp) - 2)
        in_specs.append(pl.BlockSpec((1,) * len(p) + blk, functools.partial(lambda r, me_ref, p, lead: p + lead + (r, 0), p=p, lead=lead)))
        out_specs.append(pl.BlockSpec((1,) + blk, functools.partial(lambda r, me_ref, lead: (me_ref[0],) + lead + (r, 0), lead=lead)))
    in_specs.append(pl.BlockSpec(memory_space=pl.ANY))
    return pl.pallas_call(
        body, name=name,
        grid_spec=pltpu.PrefetchScalarGridSpec(num_scalar_prefetch=1, grid=(ROW_SPLIT,), in_specs=in_specs, out_specs=out_specs),
        out_shape=[SDS((N_CHIPS,) + tuple(shp), a.dtype) for shp, (a, _) in zip(shapes, items)],
        compiler_params=_params(("parallel",)),
    )(me, *[arrays[i] for i in src_of], after)


SEM = pl.BlockSpec(memory_space=pltpu.SEMAPHORE)
IN_HBM = pl.BlockSpec(memory_space=pltpu.HBM)
DATAFLOW = pltpu.SideEffectType.DATAFLOW_SIDE_EFFECTING


def split_start(bufs, plan, n_copies, after, name):
    n = len(bufs)

    def body(*refs):
        send, recv, token = refs[n + 1], refs[n + 2], refs[-1]
        x, y, c, chips = _position()
        for k, (src, dst, dev) in enumerate(plan(refs[:n], x, y, c, chips)):
            pltpu.make_async_remote_copy(src_ref=src, dst_ref=dst, send_sem=send.at[k], recv_sem=recv.at[k],
                                         device_id=dev, device_id_type=MESH).start()
        token[...] = jnp.zeros_like(token)

    out = pl.pallas_call(
        body, name=name,
        out_shape=(pltpu.SemaphoreType.DMA((n_copies,)), pltpu.SemaphoreType.DMA((n_copies,)),
                   *[pltpu.HBM(b.shape, b.dtype) for b in bufs], SDS((8, LANES), F32)),
        in_specs=[IN_HBM] * n + [pl.BlockSpec(memory_space=pl.ANY)],
        out_specs=(SEM, SEM, *[IN_HBM] * n, pl.BlockSpec(memory_space=pltpu.VMEM)),
        input_output_aliases={i: 2 + i for i in range(n)},
        compiler_params=pltpu.CompilerParams(has_side_effects=DATAFLOW),
    )(*[pltpu.with_memory_space_constraint(b, pltpu.HBM) for b in bufs], after)
    return out[0], out[1], list(out[2:2 + n]), out[-1]


def split_wait(send, recv, bufs, plan, after, name):
    n = len(bufs)

    def body(*refs):
        send_ref, recv_ref = refs[n], refs[n + 1]
        x, y, c, chips = _position()
        for k, (src, dst, dev) in enumerate(plan(refs[:n], x, y, c, chips)):
            cp = pltpu.make_async_remote_copy(src_ref=src, dst_ref=dst, send_sem=send_ref.at[k], recv_sem=recv_ref.at[k],
                                              device_id=dev, device_id_type=MESH)
            cp.wait_send()
            cp.wait_recv()

    return list(pl.pallas_call(
        body, name=name, out_shape=tuple(pltpu.HBM(b.shape, b.dtype) for b in bufs),
        in_specs=[IN_HBM] * n + [SEM, SEM, pl.BlockSpec(memory_space=pl.ANY)], out_specs=tuple([IN_HBM] * n),
        input_output_aliases={i: i for i in range(n)},
        compiler_params=pltpu.CompilerParams(has_side_effects=DATAFLOW),
    )(*bufs, send, recv, after))


def _gather_plan(shapes, landing):
    n = len(shapes)

    def plan(refs, x, y, c, chips):
        out = []
        for t in range(n):
            half = shapes[t][0] // 2
            rows = pl.ds(c * half, half)
            for cx, cy in chips:
                slot = 2 * cx + cy if landing else 2 * x + y
                out.append((refs[t].at[rows], refs[n + t].at[slot, rows], (cx, cy, c)))
        return out

    return plan


def gather_start(shards, placed, after, name):
    shapes = [s.shape for s in shards]
    send, recv, bufs, token = split_start(list(shards) + list(placed), _gather_plan(shapes, False), 3 * len(shards), after, name)
    return (send, recv, bufs, shapes), token


def gather_wait(state, after, name):
    send, recv, bufs, shapes = state
    return split_wait(send, recv, bufs, _gather_plan(shapes, True), after, name)[len(shapes):]


def gather_pass_on(placed, shapes, name):
    n = len(placed)

    def body(*refs):
        outs, send, recv = refs[n:2 * n], refs[2 * n], refs[2 * n + 1]
        x, y, c, chips = _position()
        cps = []
        for t in range(n):
            half = shapes[t][0] // 2
            for j, (cx, cy) in enumerate(chips):
                piece = outs[t].at[2 * cx + cy, pl.ds(c * half, half)]
                cp = pltpu.make_async_remote_copy(src_ref=piece, dst_ref=piece, send_sem=send.at[3 * t + j], recv_sem=recv.at[3 * t + j],
                                                  device_id=(x, y, 1 - c), device_id_type=MESH)
                cp.start()
                cps.append(cp)
        for t in range(n):
            half = shapes[t][0] // 2
            for j, (cx, cy) in enumerate(chips):
                piece = outs[t].at[2 * cx + cy, pl.ds((1 - c) * half, half)]
                pltpu.make_async_remote_copy(src_ref=piece, dst_ref=piece, send_sem=send.at[3 * t + j], recv_sem=recv.at[3 * t + j],
                                             device_id=(x, y, 1 - c), device_id_type=MESH).wait_recv()
        for cp in cps:
            cp.wait_send()

    return pl.pallas_call(
        body, name=name,
        in_specs=[HBM] * n, out_specs=[HBM] * n,
        out_shape=[SDS(p.shape, p.dtype) for p in placed],
        input_output_aliases={t: t for t in range(n)},
        scratch_shapes=[pltpu.SemaphoreType.DMA((3 * n,))] * 2,
    )(*placed)


def _flip(k, x, y, c):
    return ((1 - x) if k & 4 else x, (1 - y) if k & 2 else y, (1 - c) if k & 1 else c)


def small_allgather(v, reduce):
    R, C = v.shape

    def body(v_ref, o_ref, *scratch):
        if reduce:
            buf, send, recv = scratch
        else:
            buf, (send, recv) = o_ref, scratch
        x, y, c, _ = _position()
        me = 4 * x + 2 * y + c
        buf[me] = v_ref[...]
        sends = []
        for k in range(1, N_DEV):
            cp = pltpu.make_async_remote_copy(src_ref=v_ref, dst_ref=buf.at[me], send_sem=send.at[k - 1], recv_sem=recv.at[k - 1],
                                              device_id=_flip(k, x, y, c), device_id_type=MESH)
            cp.start()
            sends.append(cp)
        for k in range(1, N_DEV):
            px, py, pc = _flip(k, x, y, c)
            pltpu.make_async_remote_copy(src_ref=v_ref, dst_ref=buf.at[4 * px + 2 * py + pc], send_sem=send.at[k - 1],
                                         recv_sem=recv.at[k - 1], device_id=(px, py, pc), device_id_type=MESH).wait_recv()
        for cp in sends:
            cp.wait_send()
        if reduce:
            acc = buf[0]
            for i in range(1, N_DEV):
                acc = acc + buf[i]
            o_ref[...] = acc

    vm = pl.BlockSpec(memory_space=pltpu.VMEM)
    sems = [pltpu.SemaphoreType.DMA((N_DEV - 1,)), pltpu.SemaphoreType.DMA((N_DEV - 1,))]
    return pl.pallas_call(
        body, name="small_allreduce" if reduce else "small_allgather",
        in_specs=[vm], out_specs=vm,
        out_shape=SDS((R, C) if reduce else (N_DEV, R, C), F32),
        scratch_shapes=([pltpu.VMEM((N_DEV, R, C), F32)] if reduce else []) + sems,
    )(v)


def rs_exchange_sibling(gs):
    n = len(gs)

    def body(*refs):
        ins, outs, send, recv = refs[:n], refs[n:2 * n], refs[2 * n], refs[2 * n + 1]
        x, y, c, _ = _position()
        cps = []
        for t in range(n):
            cp = pltpu.make_async_remote_copy(src_ref=ins[t].at[:, 1 - c], dst_ref=outs[t], send_sem=send.at[t], recv_sem=recv.at[t],
                                              device_id=(x, y, 1 - c), device_id_type=MESH)
            cp.start()
            cps.append(cp)
        for cp in cps:
            cp.wait()

    return pl.pallas_call(
        body, name="rs_exchange_sibling", in_specs=[HBM] * n, out_specs=[HBM] * n,
        out_shape=[SDS((g.shape[0],) + g.shape[2:], g.dtype) for g in gs],
        scratch_shapes=[pltpu.SemaphoreType.DMA((n,)), pltpu.SemaphoreType.DMA((n,))],
    )(*gs)


def rs_pair_add(gs, rs, c):
    n = len(gs)

    def body(c_ref, *refs):
        for t in range(n):
            refs[2 * n + t][0] = (refs[t][0, 0].astype(F32) + refs[n + t][0].astype(F32)).astype(BF16)

    in_specs, out_specs, out_shape = [], [], []
    for g in gs:
        _, _, h, C = g.shape
        in_specs.append(pl.BlockSpec((1, 1, h // PAIR_SPLIT, C), lambda j, r, c_ref: (j, c_ref[0], r, 0)))
    for g in gs:
        _, _, h, C = g.shape
        spec = pl.BlockSpec((1, h // PAIR_SPLIT, C), lambda j, r, c_ref: (j, r, 0))
        in_specs.append(spec)
        out_specs.append(spec)
        out_shape.append(SDS((N_CHIPS, h, C), BF16))
    return pl.pallas_call(
        body, name="rs_pair_add",
        grid_spec=pltpu.PrefetchScalarGridSpec(num_scalar_prefetch=1, grid=(N_CHIPS, PAIR_SPLIT), in_specs=in_specs, out_specs=out_specs),
        out_shape=out_shape, compiler_params=_params(("parallel", "parallel")),
    )(c, *gs, *rs)


def _rs_plan(n):
    def plan(refs, x, y, c, chips):
        return [(refs[t].at[2 * cx + cy], refs[n + t].at[j], (cx, cy, c)) for t in range(n) for j, (cx, cy) in enumerate(chips)]

    return plan


def rs_chip_add(ps, qs, me_c):
    n = len(ps)

    def body(me_ref, *refs):
        for t in range(n):
            q = refs[n + t]
            refs[2 * n + t][0] = ((refs[t][0].astype(F32) + q[0].astype(F32)) + q[1].astype(F32)) + q[2].astype(F32)

    in_specs, out_specs, out_shape = [], [], []
    for p in ps:
        _, h, C = p.shape
        in_specs.append(pl.BlockSpec((1, h // ROW_SPLIT, C), lambda r, me_ref: (me_ref[0], r, 0)))
    for p in ps:
        _, h, C = p.shape
        in_specs.append(pl.BlockSpec((3, h // ROW_SPLIT, C), lambda r, me_ref: (0, r, 0)))
        out_specs.append(pl.BlockSpec((1, h // ROW_SPLIT, C), lambda r, me_ref: (me_ref[1], r, 0)))
        out_shape.append(SDS((2, h, C), F32))
    return pl.pallas_call(
        body, name="rs_chip_add",
        grid_spec=pltpu.PrefetchScalarGridSpec(num_scalar_prefetch=1, grid=(ROW_SPLIT,), in_specs=in_specs, out_specs=out_specs),
        out_shape=out_shape, compiler_params=_params(("parallel",)),
    )(me_c, *ps, *qs)


def rs_share(rs):
    n = len(rs)

    def body(*refs):
        outs, send, recv = refs[n:2 * n], refs[2 * n], refs[2 * n + 1]
        x, y, c, _ = _position()
        cps = []
        for t in range(n):
            cp = pltpu.make_async_remote_copy(src_ref=outs[t].at[c], dst_ref=outs[t].at[c], send_sem=send.at[t], recv_sem=recv.at[t],
                                              device_id=(x, y, 1 - c), device_id_type=MESH)
            cp.start()
            cps.append(cp)
        for cp in cps:
            cp.wait()

    return pl.pallas_call(
        body, name="rs_share", in_specs=[HBM] * n, out_specs=[HBM] * n,
        out_shape=[SDS(r.shape, r.dtype) for r in rs],
        input_output_aliases={t: t for t in range(n)},
        scratch_shapes=[pltpu.SemaphoreType.DMA((n,))] * 2,
    )(*rs)


def rs_begin(gs, after, name):
    c = lax.axis_index("c")
    n = len(gs)
    g5 = [g.reshape(N_CHIPS, 2, g.shape[1] // 2, g.shape[2]) for g in gs]
    from_sibling = rs_exchange_sibling(g5)
    pair = rs_pair_add(g5, from_sibling, jnp.reshape(c, (1,)).astype(jnp.int32))
    lands = [lax.empty((3,) + p.shape[1:], p.dtype) for p in pair]
    send, recv, bufs, token = split_start(list(pair) + lands, _rs_plan(n), 3 * n, from_sibling[0] if after is None else after, name)
    return (send, recv, bufs, [g.shape for g in gs]), token


def rs_end(state, after, name):
    x, y, c = lax.axis_index("x"), lax.axis_index("y"), lax.axis_index("c")
    send, recv, bufs, shapes = state
    n = len(shapes)
    bufs = split_wait(send, recv, bufs, _rs_plan(n), after, name)
    half = rs_chip_add(bufs[:n], bufs[n:], jnp.stack([2 * x + y, c]).astype(jnp.int32))
    both = rs_share(half)
    return [b.reshape(s[1], s[2]) for b, s in zip(both, shapes)]


def _pad_last(a, n):
    return jnp.pad(a, [(0, 0)] * (a.ndim - 1) + [(0, n - a.shape[-1])])


def _heads_to_groups(w):
    k = w.shape[0]
    return _pad_last(w.reshape(k, ML_HEADS, ML_HEAD_DIM).transpose(1, 0, 2), GROUP)


def _groups_to_heads(g):
    return g[:, :, :ML_HEAD_DIM].transpose(1, 0, 2).reshape(g.shape[1], D_TOK)


def _cols_to_groups(w):
    k, n = w.shape
    return w.reshape(k, n // GROUP, GROUP).transpose(1, 0, 2)


def _groups_to_cols(g):
    n, k, _ = g.shape
    return g.transpose(1, 0, 2).reshape(k, n * GROUP)


def _chips_to_cols(a):
    return a.transpose(1, 0, 2).reshape(a.shape[1], -1)


def _cols_to_chips(w):
    k, n = w.shape
    return w.reshape(k, N_CHIPS, n // N_CHIPS).transpose(1, 0, 2)


def _mlstm_in_groups(w):
    parts = [_heads_to_groups(w[:, i * D_TOK:(i + 1) * D_TOK]) for i in range(4)]
    gates = _pad_last(w[:, 4 * D_TOK:4 * D_TOK + 2 * ML_HEADS], GROUP)[None]
    qmem = w[:, 4 * D_TOK + 2 * ML_HEADS:][None]
    return jnp.concatenate(parts + [qmem, gates], axis=0)


def _mlstm_in_ungroup(g):
    parts = [_groups_to_heads(g[4 * i:4 * i + 4]) for i in range(4)]
    return jnp.concatenate(parts + [g[17][:, :2 * ML_HEADS], g[16]], axis=1)


def _taps_to_groups(w, width):
    taps = w.shape[0]
    g = _pad_last(w.reshape(taps, -1, width), GROUP).transpose(1, 0, 2)
    return jnp.pad(g, ((0, 0), (0, 8 - taps), (0, 0)))


def _groups_to_taps(g, taps, width):
    return g[:, :taps, :width].transpose(1, 0, 2).reshape(taps, -1)


SMALL_IN_COLS = 384
SMALL_OUT_COLS = 1536
SECTION = 8


class _Gathered:
    def __init__(self, make_src, groups, me, after):
        self.groups, self.states, self.ready = groups, [], {}
        self.group_of = {k: gi for gi, g in enumerate(groups) for k in g}
        token, self.first = after, None
        for gi, g in enumerate(groups):
            srcs = [make_src(k, None if gi == 0 else token[0:1, 0:1]) for k in g]
            placed = place_own([(a, ()) for a in srcs], me, token, f"place_own_{gi}")
            state, token = gather_start(srcs, placed, token, f"gather_start_{gi}")
            self.states.append(state)
            if gi == 0:
                self.first = token[0:1, 0:1]
        self.started = token

    def _get(self, key, after):
        gi = self.group_of[key]
        if gi not in self.ready:
            got = gather_wait(self.states[gi], after if gi else self.started, f"gather_wait_{gi}")
            self.ready[gi] = dict(zip(self.groups[gi], gather_pass_on(got, self.states[gi][3], f"gather_pass_on_{gi}")))
        return self.ready[gi][key]

    def ffn(self, l, i, after):
        return tuple(self._get((n, l, i), after) for n in ("wg", "wu", "wd"))

    def mixer(self, l, after):
        win = _chips_to_cols(self._get(("win", l), after))
        win = _cols_to_groups(win) if l % 2 == 0 else _mlstm_in_groups(win)
        wkv = _cols_to_groups(self._get(("wkv", l), after).reshape(D_MODEL, 2 * D_XA))
        wout = self._get(("wout", l), after)
        if l % 2:
            wout = wout.reshape(D_MODEL, D_MODEL)
            tok = jnp.pad(wout[:D_TOK].reshape(ML_HEADS, ML_HEAD_DIM, D_MODEL), ((0, 0), (0, GROUP - ML_HEAD_DIM), (0, 0)))
            wout = jnp.concatenate([tok, wout[D_TOK:][None]], axis=0)
        return win, wkv, wout


class _GradSink:
    def __init__(self, apply):
        self.queue, self.apply, self.count, self.done = [], apply, 0, None

    @staticmethod
    def _by_chip(key, g):
        if key[0] == "wkv":
            return _groups_to_cols(g).reshape(N_CHIPS, D_MODEL // N_CHIPS, 2 * D_XA)
        if key[0] == "win":
            return _cols_to_chips(_groups_to_cols(g) if key[1] % 2 == 0 else _mlstm_in_ungroup(g))
        if key[0] == "wout" and key[1] % 2:
            full = jnp.concatenate([g[:ML_HEADS, :ML_HEAD_DIM].reshape(D_TOK, D_MODEL), g[ML_HEADS]], axis=0)
            return full.reshape(N_CHIPS, D_MODEL // N_CHIPS, D_MODEL)
        return g

    def push(self, grads):
        keys = list(grads)
        state, token = rs_begin([self._by_chip(k, grads[k]) for k in keys], self.done, f"rs_start_{self.count}")
        if self.queue:
            self._finish(token)
        self.queue.append((keys, state, self.count))
        self.count += 1
        return token

    def flush(self):
        self._finish(self.done)

    def _finish(self, after):
        keys, state, i = self.queue.pop(0)
        for key, g in zip(keys, rs_end(state, after, f"rs_wait_{i}")):
            self.done = self.apply(key, g, self.done)


def _local_step(x, mem, tgt, P, weights, sink):
    memb = mem.astype(BF16)
    saved = []
    pin0 = getattr(weights, "first", None)
    X, Xb = x, (x if pin0 is None else x + pin0).astype(BF16)
    after = Xb
    for l in range(DEPTH):
        s = {}
        s["x0b"] = Xb
        s["wa"] = weights.ffn(l, 0, after)
        s["g1a"], s["u1a"], s["ha"], s["z1"], X1, X1b = ffn_fwd(Xb, X, *s["wa"], P["ln_g"][l][0], P["ln_b"][l][0])
        s["x1b"] = X1b
        s["wm"] = win, wkv, wout = weights.mixer(l, X1b)
        u = proj(X1b, win, "mixer_in")
        kv = proj(memb, wkv, "mem_kv")
        s["u"], s["kv"] = u, kv
        if l % 2 == 0:
            tok = conv_mixer_fwd(u, P["convw"])
            qg = 9
        else:
            s["qk"] = qk_conv_fwd(u, P["qkw"])
            s["hm"], s["cst"], s["mst"] = mlstm_fwd(s["qk"], u, P["bg"])
            tok = head_norm_fwd(s["hm"], u, P["hg"])
            qg = 16
        s["m"] = xattn_fwd(u, qg, kv, tok)
        s["z2"], X2, X2b = contract_ln(s["m"], wout, X1, P["ln_g"][l][1], P["ln_b"][l][1], 1.0, "mixer_out_ln")
        s["x2b"] = X2b
        s["wb"] = weights.ffn(l, 1, X2b)
        s["g1b"], s["u1b"], s["hb"], s["z3"], X, Xb = ffn_fwd(X2b, X2, *s["wb"], P["ln_g"][l][2], P["ln_b"][l][2])
        after = Xb
        saved.append(s)

    loss, dX = loss_grad(X, tgt)

    G = {"ln_g": [[None] * 3 for _ in range(DEPTH)], "ln_b": [[None] * 3 for _ in range(DEPTH)]}
    pin = [jnp.zeros((1, 1), F32)]

    def ffn_backward(l, i, dX, z, xinb, g1, u1, h, w):
        k = 2 * i
        dgb, dub, dx, dyb, G["ln_g"][l][k], G["ln_b"][l][k] = ffn_bwd(dX, z, P["ln_g"][l][k] + pin[0], w[2], w[0], w[1], g1, u1)
        grads = {("wd", l, i): wgrad(h, dyb, BF16, "wgrad_down"), ("wg", l, i): wgrad(dgb, xinb, BF16, "wgrad_gate"),
                 ("wu", l, i): wgrad(dub, xinb, BF16, "wgrad_up")}
        return dx, grads

    for l in reversed(range(DEPTH)):
        s = saved[l]
        win, wkv, wout = s["wm"]
        dX, grads = ffn_backward(l, 1, dX, s["z3"], s["x2b"], s["g1b"], s["u1b"], s["hb"], s["wb"])
        dm, dz2, dz2b, G["ln_g"][l][1], G["ln_b"][l][1] = mixer_out_bwd(dX, s["z2"], P["ln_g"][l][1], wout)
        grads[("wout", l)] = wgrad(s["m"], dz2b, BF16, "wgrad_out")
        u, kv = s["u"], s["kv"]
        if l % 2 == 0:
            db, dc, dxi, G["convw"] = conv_mixer_bwd(u, P["convw"], dm)
            dq, dkv = xattn_bwd(u, 9, kv, dm, 3)
            du = jnp.concatenate([db, dc, dxi, dq], axis=0)
        else:
            dh, du, G["hg"] = head_norm_bwd(s["hm"], u, P["hg"], dm)
            dqk, du, dgate, G["bg"] = mlstm_bwd(s["qk"], u, P["bg"], s["cst"], s["mst"], dh, du)
            du, G["qkw"] = qk_conv_bwd(u, P["qkw"], dqk, du)
            du, dkv = xattn_bwd(u, 16, kv, dm, 4, du, dgate)
        grads[("win", l)] = wgrad(s["x1b"], du, BF16, "wgrad_in")
        grads[("wkv", l)] = wgrad(memb, dkv.astype(BF16), BF16, "wgrad_kv")
        dX = contract_t(du, win, dz2, "mixer_in_bwd")
        pin[0] = sink.push(grads)[0:1, 0:1]
        dX, grads = ffn_backward(l, 0, dX, s["z1"], s["x0b"], s["g1a"], s["u1a"], s["ha"], s["wa"])
        pin[0] = sink.push(grads)[0:1, 0:1]
    sink.flush()
    return loss, dX, G


def kernel(x, mem, ln_g, ln_b, ffn_w_gate, ffn_w_up, ffn_w_down, w_kv_mem, w_out, w_in_conv, conv_w, w_in_mlstm, b_gates, qk_conv_w, head_norm_g, loss_target, m_ln_g, m_ln_b, m_ffn_w_gate, m_ffn_w_up, m_ffn_w_down, m_w_kv_mem, m_w_out, m_w_in_conv, m_conv_w, m_w_in_mlstm, m_b_gates, m_qk_conv_w, m_head_norm_g, v_ln_g, v_ln_b, v_ffn_w_gate, v_ffn_w_up, v_ffn_w_down, v_w_kv_mem, v_w_out, v_w_in_conv, v_conv_w, v_w_in_mlstm, v_b_gates, v_qk_conv_w, v_head_norm_g):
    cx, cy = lax.axis_index("x"), lax.axis_index("y")
    chip = 2 * cx + cy

    def make_src(key, pin):
        if key[0] in ("wg", "wu"):
            w = jnp.swapaxes((ffn_w_gate if key[0] == "wg" else ffn_w_up)[key[1], key[2]], 0, 1)
        elif key[0] == "wd":
            w = ffn_w_down[key[1], key[2]]
        elif key[0] == "win":
            w = (w_in_conv, w_in_mlstm)[key[1]][0]
        else:
            w = (w_kv_mem if key[0] == "wkv" else w_out)[key[1]]
        return (w if pin is None else w + pin).astype(BF16)

    ffn_keys = lambda l, i: [("wg", l, i), ("wu", l, i), ("wd", l, i)]
    mixer_keys = lambda l: [("win", l), ("wkv", l), ("wout", l)]
    groups = [ffn_keys(0, 0), mixer_keys(0) + mixer_keys(1), ffn_keys(0, 1), ffn_keys(1, 0), ffn_keys(1, 1)]
    def section(a, width):
        a = a.reshape(-1, a.shape[-1])
        return jnp.pad(a, ((0, SECTION - a.shape[0]), (0, width - a.shape[1])))

    small = jnp.concatenate([section(a, SMALL_IN_COLS) for a in (ln_g, ln_b, conv_w, qk_conv_w)], axis=0)
    smalls = small_allgather(small, reduce=False)
    gathered = _Gathered(make_src, groups, jnp.reshape(chip, (1,)).astype(jnp.int32), smalls)
    smalls = smalls[0::2]
    ln_g_full = _chips_to_cols(smalls[:, 0:6, 0:256]).reshape(DEPTH, 3, 1, D_MODEL)
    ln_b_full = _chips_to_cols(smalls[:, 8:14, 0:256]).reshape(DEPTH, 3, 1, D_MODEL)
    conv_w_full = _chips_to_cols(smalls[:, 16:19, 0:192])
    qk_w_full = _chips_to_cols(smalls[:, 24:28, 0:384])

    P = {"ln_g": ln_g_full, "ln_b": ln_b_full, "convw": _taps_to_groups(conv_w_full, GROUP),
         "qkw": _taps_to_groups(qk_w_full, ML_HEAD_DIM), "bg": _pad_last(b_gates, GROUP),
         "hg": _pad_last(head_norm_g[0], GROUP)[:, None, :]}

    weights = {"ln_g": ln_g, "ln_b": ln_b, "ffn_w_gate": ffn_w_gate, "ffn_w_up": ffn_w_up, "ffn_w_down": ffn_w_down,
               "w_kv_mem": w_kv_mem, "w_out": w_out, "w_in_conv": w_in_conv, "conv_w": conv_w, "w_in_mlstm": w_in_mlstm,
               "b_gates": b_gates, "qk_conv_w": qk_conv_w, "head_norm_g": head_norm_g}
    ms = {"ln_g": m_ln_g, "ln_b": m_ln_b, "ffn_w_gate": m_ffn_w_gate, "ffn_w_up": m_ffn_w_up, "ffn_w_down": m_ffn_w_down,
          "w_kv_mem": m_w_kv_mem, "w_out": m_w_out, "w_in_conv": m_w_in_conv, "conv_w": m_conv_w, "w_in_mlstm": m_w_in_mlstm,
          "b_gates": m_b_gates, "qk_conv_w": m_qk_conv_w, "head_norm_g": m_head_norm_g}
    vs = {"ln_g": v_ln_g, "ln_b": v_ln_b, "ffn_w_gate": v_ffn_w_gate, "ffn_w_up": v_ffn_w_up, "ffn_w_down": v_ffn_w_down,
          "w_kv_mem": v_w_kv_mem, "w_out": v_w_out, "w_in_conv": v_w_in_conv, "conv_w": v_conv_w, "w_in_mlstm": v_w_in_mlstm,
          "b_gates": v_b_gates, "qk_conv_w": v_qk_conv_w, "head_norm_g": v_head_norm_g}
    names = list(weights)
    owner = {"wg": ("ffn_w_gate", True), "wu": ("ffn_w_up", True), "wd": ("ffn_w_down", False), "wkv": ("w_kv_mem", False),
             "wout": ("w_out", False), "win": None}
    updated = {}

    def apply(key, g, after):
        name, transposed = owner[key[0]] or (("w_in_conv", "w_in_mlstm")[key[1]], False)
        idx = (0,) if key[0] == "win" else tuple(key[1:])
        view = (lambda a: jnp.swapaxes(a, -1, -2)) if transposed else (lambda a: a)
        updated[name], token = adamw_into(view(weights[name]), view(ms[name]), view(vs[name]), g, updated.get(name), idx, after,
                                          "adamw_" + name + "_" + "_".join(map(str, idx)))
        return token

    sink = _GradSink(apply)
    loss, grad_x, G = _local_step(x[0], mem[0], loss_target[0], P, gathered, sink)

    dln_g = jnp.concatenate([G["ln_g"][l][k] for l in range(DEPTH) for k in range(3)], axis=0)
    dln_b = jnp.concatenate([G["ln_b"][l][k] for l in range(DEPTH) for k in range(3)], axis=0)
    lane = lax.broadcasted_iota(jnp.int32, (1, GROUP), 1)
    misc = jnp.where(lane < 8, G["bg"], 0.0) + jnp.where(lane == 8, loss, 0.0) + sink.done[0:1, 0:1]
    parts = (dln_g, dln_b, _groups_to_taps(G["convw"], 3, GROUP), misc, _groups_to_taps(G["qkw"], 4, ML_HEAD_DIM),
             G["hg"][:, 0, :ML_HEAD_DIM])
    tot = small_allgather(jnp.concatenate([section(a, SMALL_OUT_COLS) for a in parts], axis=0), reduce=True)
    loss_total = tot[24, 8]

    small_grads = {
        "ln_g": lax.dynamic_slice(tot[0:6, 0:D_MODEL], (0, chip * 256), (6, 256)).reshape(DEPTH, 3, 256),
        "ln_b": lax.dynamic_slice(tot[8:14, 0:D_MODEL], (0, chip * 256), (6, 256)).reshape(DEPTH, 3, 256),
        "conv_w": lax.dynamic_slice(tot[16:19, 0:D_TOK], (0, chip * 192), (3, 192))[None],
        "b_gates": tot[24:25, 0:8],
        "qk_conv_w": lax.dynamic_slice(tot[32:36, 0:2 * D_TOK], (0, chip * 384), (4, 384))[None],
        "head_norm_g": tot[40:44, 0:ML_HEAD_DIM][None],
    }
    grads, deltas, new_m, new_v = [], [], [], []
    for nme in names:
        if nme in updated:
            back = (lambda a: jnp.swapaxes(a, -1, -2)) if nme in ("ffn_w_gate", "ffn_w_up") else (lambda a: a)
            g, d, nm, nv = (back(a) for a in updated[nme])
        else:
            w, g = weights[nme], small_grads[nme]
            two = (math.prod(w.shape[:-1]), w.shape[-1])
            d, nm, nv = (a.reshape(w.shape) for a in adamw(w.reshape(two), g.reshape(two), ms[nme].reshape(two),
                                                           vs[nme].reshape(two), "adamw_" + nme))
        grads.append(g)
        deltas.append(d)
        new_m.append(nm)
        new_v.append(nv)
    return (loss_total, grad_x[None], *grads, *deltas, *new_m, *new_v)
```

```python
import functools
import math

import jax
import jax.numpy as jnp
from jax import lax
from jax.experimental import pallas as pl
from jax.experimental.pallas import tpu as pltpu

F32 = jnp.float32
BF16 = jnp.bfloat16
SDS = jax.ShapeDtypeStruct

D_MODEL = 1024
DEPTH = 2
N_MEM = 256
XA_HEADS = 4
XA_HEAD_DIM = 64
D_XA = 256
D_TOK = 768
ML_HEADS = 4
ML_HEAD_DIM = 192
ML_CHUNK = 64
D_FF = 2816
LN_EPS = 1e-5
ALPHA = (2.0 * DEPTH) ** 0.25
N_CHIPS = 4
N_DEV = 8
FF_SHARD = D_FF // N_CHIPS
GROUP = 256
NEG = -1e30

ADAM_LR = 0.001
ADAM_B1 = 0.9
ADAM_B2 = 0.999
ADAM_EPS = 1e-08
ADAM_WD = 0.01
ADAM_STEP = 10

VMEM_LIMIT = 56 * 1024 * 1024

NN = ((1,), (0,))
NT = ((1,), (1,))
TN = ((0,), (0,))
MESH = pl.DeviceIdType.MESH


def _dot(a, b, dims):
    return lax.dot_general(a, b, (dims, ((), ())), preferred_element_type=F32)


def _bdot(a, b, ca, cb):
    dims = (((ca,), (cb,)), ((0,), (0,)))
    ah, bh = a.astype(BF16), b.astype(BF16)
    al, bl = (a - ah.astype(F32)).astype(BF16), (b - bh.astype(F32)).astype(BF16)
    dot = functools.partial(lax.dot_general, dimension_numbers=dims, preferred_element_type=F32)
    return dot(ah, bh) + dot(al, bh) + dot(ah, bl)


def _bdot1(a, b, ca, cb):
    return lax.dot_general(a.astype(BF16), b.astype(BF16), (((ca,), (cb,)), ((0,), (0,))), preferred_element_type=F32)


def _sigmoid(x):
    return 1.0 / (1.0 + jnp.exp(-x))


def _params(sem, vmem=VMEM_LIMIT):
    return pltpu.CompilerParams(dimension_semantics=sem, vmem_limit_bytes=vmem)


def _tile(n, want):
    t = min(n, want)
    assert n % t == 0, (n, t)
    return t


def _layer_norm(z, gamma, beta):
    mu = jnp.mean(z, axis=-1, keepdims=True)
    zc = z - mu
    var = jnp.mean(zc * zc, axis=-1, keepdims=True)
    return zc * lax.rsqrt(var + LN_EPS) * gamma + beta


def _column_halves(n):
    mid = -(-n // (2 * 128)) * 128
    return ((0, mid), (mid, n))


def _resident(shape):
    return pl.BlockSpec(shape, lambda *_: (0,) * len(shape), pipeline_mode=pl.Buffered(1))


def _group_block(G, want):
    return max(d for d in range(1, max(1, min(G, want)) + 1) if G % d == 0)


def ffn_fwd(xb, x, wg, wu, wd, gamma, beta):
    S, K = xb.shape
    G, N, _ = wg.shape
    ts = _tile(S, 512)

    def body(xb_ref, x_ref, wg_ref, wu_ref, wd_ref, gm_ref, bt_ref, g_ref, u_ref, h_ref, z_ref, xn_ref, xnb_ref):
        j = pl.program_id(1)
        xv = xb_ref[...]
        g = _dot(xv, wg_ref[j], NT)
        u = _dot(xv, wu_ref[j], NT)
        h = (g * _sigmoid(g) * u).astype(BF16)
        g_ref[0] = g.astype(BF16)
        u_ref[0] = u.astype(BF16)
        h_ref[0] = h
        y = _dot(h, wd_ref[j], NN)

        @pl.when(j == 0)
        def _():
            z_ref[...] = y

        @pl.when(j > 0)
        def _():
            z_ref[...] += y

        @pl.when(j == G - 1)
        def _():
            z = ALPHA * x_ref[...] + 0.5 * z_ref[...]
            xn = _layer_norm(z, gm_ref[...], bt_ref[...])
            z_ref[...] = z
            xn_ref[...] = xn
            xnb_ref[...] = xn.astype(BF16)

    row = pl.BlockSpec((ts, K), lambda s, j: (s, 0))
    vec = pl.BlockSpec((1, K), lambda s, j: (0, 0))
    wspec = _resident((G, N, K))
    ospec = pl.BlockSpec((1, ts, N), lambda s, j: (j, s, 0))
    return pl.pallas_call(
        body, name="ffn_fwd", grid=(S // ts, G),
        in_specs=[row, row, wspec, wspec, wspec, vec, vec],
        out_specs=[ospec, ospec, ospec, row, row, row],
        out_shape=[SDS((G, S, N), BF16), SDS((G, S, N), BF16), SDS((G, S, N), BF16),
                   SDS((S, K), F32), SDS((S, K), F32), SDS((S, K), BF16)],
        compiler_params=_params(("parallel", "arbitrary")),
    )(xb, x, wg, wu, wd, gamma, beta)


def proj(xb, w, name):
    S, K = xb.shape
    G, _, N = w.shape
    ts = _tile(S, 1024)
    gb = _group_block(G, 6)

    def body(x_ref, w_ref, y_ref):
        xv = x_ref[...]
        for j in range(gb):
            y_ref[j] = _dot(xv, w_ref[j], NN)

    return pl.pallas_call(
        body, name=name, grid=(S // ts, G // gb),
        in_specs=[pl.BlockSpec((ts, K), lambda s, g: (s, 0)), pl.BlockSpec((gb, K, N), lambda s, g: (g, 0, 0))],
        out_specs=pl.BlockSpec((gb, ts, N), lambda s, g: (g, s, 0)),
        out_shape=SDS((G, S, N), F32),
        compiler_params=_params(("parallel", "parallel")),
    )(xb, w)


def contract_ln(a, w, xres, gamma, beta, scale, name):
    G, S, Kg = a.shape
    N = w.shape[2]
    ts = _tile(S, 1024)

    def body(a_ref, w_ref, x_ref, g_ref, b_ref, z_ref, xn_ref, xb_ref):
        acc = _dot(a_ref[0], w_ref[0], NN)
        for j in range(1, G):
            acc = acc + _dot(a_ref[j], w_ref[j], NN)
        z = ALPHA * x_ref[...] + scale * acc
        xn = _layer_norm(z, g_ref[...], b_ref[...])
        z_ref[...] = z
        xn_ref[...] = xn
        xb_ref[...] = xn.astype(BF16)

    row = pl.BlockSpec((ts, N), lambda s: (s, 0))
    vec = pl.BlockSpec((1, N), lambda s: (0, 0))
    return pl.pallas_call(
        body, name=name, grid=(S // ts,),
        in_specs=[pl.BlockSpec((G, ts, Kg), lambda s: (0, s, 0)), pl.BlockSpec((G, Kg, N), lambda s: (0, 0, 0)), row, vec, vec],
        out_specs=[row, row, row],
        out_shape=[SDS((S, N), F32), SDS((S, N), F32), SDS((S, N), BF16)],
        compiler_params=_params(("parallel",)),
    )(a, w, xres, gamma, beta)


def _layer_norm_bwd(dx, z, gamma):
    mu = jnp.mean(z, axis=-1, keepdims=True)
    zc = z - mu
    var = jnp.mean(zc * zc, axis=-1, keepdims=True)
    rstd = lax.rsqrt(var + LN_EPS)
    xhat = zc * rstd
    dxh = dx * gamma
    m1 = jnp.mean(dxh, axis=-1, keepdims=True)
    m2 = jnp.mean(dxh * xhat, axis=-1, keepdims=True)
    return rstd * (dxh - m1 - xhat * m2), jnp.sum(dx * xhat, axis=0, keepdims=True), jnp.sum(dx, axis=0, keepdims=True)


def ffn_bwd(dxn, z, gamma, wd, wg, wu, g1, u1):
    S, K = dxn.shape
    G, N, _ = wd.shape
    ts = _tile(S, 512)

    def body(dxn_ref, z_ref, gm_ref, wd_ref, wg_ref, wu_ref, g_ref, u_ref, dg_ref, du_ref, dx_ref, dy_ref, dgm_ref, dbt_ref):
        s, j = pl.program_id(0), pl.program_id(1)

        @pl.when((s == 0) & (j == 0))
        def _():
            dgm_ref[...] = jnp.zeros_like(dgm_ref)
            dbt_ref[...] = jnp.zeros_like(dbt_ref)

        @pl.when(j == 0)
        def _():
            dz, dgm, dbt = _layer_norm_bwd(dxn_ref[...], z_ref[...], gm_ref[...])
            dgm_ref[...] += dgm
            dbt_ref[...] += dbt
            dx_ref[...] = ALPHA * dz
            dy_ref[...] = (0.5 * dz).astype(BF16)

        dy = dy_ref[...]
        part = None
        for a, b in _column_halves(N):
            dh = _dot(dy, wd_ref[j, a:b, :], NT)
            g = g_ref[0, :, a:b].astype(F32)
            sig = _sigmoid(g)
            dg = (dh * u_ref[0, :, a:b].astype(F32) * (sig * (1.0 + g * (1.0 - sig)))).astype(BF16)
            du = (dh * (g * sig)).astype(BF16)
            dg_ref[0, :, a:b] = dg
            du_ref[0, :, a:b] = du
            p = _dot(dg, wg_ref[j, a:b, :], NN) + _dot(du, wu_ref[j, a:b, :], NN)
            part = p if part is None else part + p
        dx_ref[...] += part

    row = pl.BlockSpec((ts, K), lambda s, j: (s, 0))
    vec = pl.BlockSpec((1, K), lambda s, j: (0, 0))
    gspec = pl.BlockSpec((1, ts, N), lambda s, j: (j, s, 0))
    wspec = _resident((G, N, K))
    return pl.pallas_call(
        body, name="ffn_bwd", grid=(S // ts, G),
        in_specs=[row, row, vec, wspec, wspec, wspec, gspec, gspec],
        out_specs=[gspec, gspec, row, row, vec, vec],
        out_shape=[SDS((G, S, N), BF16), SDS((G, S, N), BF16), SDS((S, K), F32), SDS((S, K), BF16),
                   SDS((1, K), F32), SDS((1, K), F32)],
        compiler_params=_params(("arbitrary", "arbitrary")),
    )(dxn, z, gamma, wd, wg, wu, g1, u1)


def mixer_out_bwd(dxn, z, gamma, w):
    S, N = dxn.shape
    G, Kg, _ = w.shape
    ts = _tile(S, 512)

    def body(dxn_ref, z_ref, gm_ref, w_ref, dm_ref, dz_ref, dzb_ref, dgm_ref, dbt_ref):
        @pl.when(pl.program_id(0) == 0)
        def _():
            dgm_ref[...] = jnp.zeros_like(dgm_ref)
            dbt_ref[...] = jnp.zeros_like(dbt_ref)

        dz, dgm, dbt = _layer_norm_bwd(dxn_ref[...], z_ref[...], gm_ref[...])
        dgm_ref[...] += dgm
        dbt_ref[...] += dbt
        dzb = dz.astype(BF16)
        dz_ref[...] = dz
        dzb_ref[...] = dzb
        for j in range(G):
            dm_ref[j] = _dot(dzb, w_ref[j], NT)

    row = pl.BlockSpec((ts, N), lambda s: (s, 0))
    vec = pl.BlockSpec((1, N), lambda s: (0, 0))
    return pl.pallas_call(
        body, name="mixer_out_bwd", grid=(S // ts,),
        in_specs=[row, row, vec, pl.BlockSpec((G, Kg, N), lambda s: (0, 0, 0))],
        out_specs=[pl.BlockSpec((G, ts, Kg), lambda s: (0, s, 0)), row, row, vec, vec],
        out_shape=[SDS((G, S, Kg), F32), SDS((S, N), F32), SDS((S, N), BF16), SDS((1, N), F32), SDS((1, N), F32)],
        compiler_params=_params(("arbitrary",)),
    )(dxn, z, gamma, w)


def contract_t(da, w, res, name):
    G, S, Ng = da.shape
    K = w.shape[1]
    ts = _tile(S, 1024)
    gb = _group_block(G, 6)

    def body(da_ref, w_ref, r_ref, o_ref):
        g = pl.program_id(1)
        part = _dot(da_ref[0], w_ref[0], NT)
        for j in range(1, gb):
            part = part + _dot(da_ref[j], w_ref[j], NT)

        @pl.when(g == 0)
        def _():
            o_ref[...] = ALPHA * r_ref[...] + part

        @pl.when(g > 0)
        def _():
            o_ref[...] += part

    row = pl.BlockSpec((ts, K), lambda s, g: (s, 0))
    return pl.pallas_call(
        body, name=name, grid=(S // ts, G // gb),
        in_specs=[pl.BlockSpec((gb, ts, Ng), lambda s, g: (g, s, 0)), pl.BlockSpec((gb, K, Ng), lambda s, g: (g, 0, 0)), row],
        out_specs=row,
        out_shape=SDS((S, K), F32),
        compiler_params=_params(("parallel", "arbitrary")),
    )(da, w, res)


WGRAD_ACC_ELEMS = 6 * 1024 * 256


def wgrad(a, b, out_dtype, name):
    ga, gb = a.ndim == 3, b.ndim == 3
    G = a.shape[0] if ga else b.shape[0]
    S, K = a.shape[-2:]
    N = b.shape[-1]
    ts = _tile(S, 2048)
    ns = S // ts
    ng = _group_block(G, WGRAD_ACC_ELEMS // (K * N))

    def body(a_ref, b_ref, o_ref, acc):
        s = pl.program_id(1)

        @pl.when(s == 0)
        def _():
            acc[...] = jnp.zeros_like(acc)

        for j in range(ng):
            acc[j] += _dot(a_ref[j] if ga else a_ref[...], b_ref[j] if gb else b_ref[...], TN)

        @pl.when(s == ns - 1)
        def _():
            o_ref[...] = acc[...].astype(out_dtype)

    aspec = pl.BlockSpec((ng, ts, K), lambda g, s: (g, s, 0)) if ga else pl.BlockSpec((ts, K), lambda g, s: (s, 0))
    bspec = pl.BlockSpec((ng, ts, N), lambda g, s: (g, s, 0)) if gb else pl.BlockSpec((ts, N), lambda g, s: (s, 0))
    return pl.pallas_call(
        body, name=name, grid=(G // ng, ns),
        in_specs=[aspec, bspec],
        out_specs=pl.BlockSpec((ng, K, N), lambda g, s: (g, 0, 0)),
        out_shape=SDS((G, K, N), out_dtype),
        scratch_shapes=[pltpu.VMEM((ng, K, N), F32)],
        compiler_params=_params(("parallel", "arbitrary")),
    )(a, b)


def loss_grad(xn, tgt):
    S, N = xn.shape
    ts = _tile(S, 1024)

    def body(x_ref, t_ref, l_ref, dx_ref):
        @pl.when(pl.program_id(0) == 0)
        def _():
            l_ref[...] = jnp.zeros_like(l_ref)

        e = x_ref[...] - t_ref[...]
        dx_ref[...] = e * (1.0 / N)
        l_ref[...] += 0.5 * jnp.sum(jnp.mean(e * e, axis=-1, keepdims=True), axis=0, keepdims=True)

    row = pl.BlockSpec((ts, N), lambda s: (s, 0))
    return pl.pallas_call(
        body, name="loss_grad", grid=(S // ts,),
        in_specs=[row, row],
        out_specs=[pl.BlockSpec((1, 1), lambda s: (0, 0)), row],
        out_shape=[SDS((1, 1), F32), SDS((S, N), F32)],
        compiler_params=_params(("arbitrary",)),
    )(xn, tgt)


def _shift_down(x, k):
    if k == 0:
        return x
    rows = lax.broadcasted_iota(jnp.int32, x.shape, 0)
    return jnp.where(rows >= k, pltpu.roll(x, k, 0), 0.0)


def _shift_up(x, k):
    if k == 0:
        return x
    n = x.shape[0]
    rows = lax.broadcasted_iota(jnp.int32, x.shape, 0)
    return jnp.where(rows < n - k, pltpu.roll(x, n - k, 0), 0.0)


LANES = 128


def conv_mixer_fwd(u, cw):
    _, S, _ = u.shape

    def body(b_ref, c_ref, x_ref, w_ref, o_ref):
        p = c_ref[0] * x_ref[0]
        w = w_ref[0]
        conv = w[2:3] * p + w[1:2] * _shift_down(p, 1) + w[0:1] * _shift_down(p, 2)
        o_ref[0] = (b_ref[0] * conv).astype(BF16)

    def uspec(off):
        return pl.BlockSpec((1, S, GROUP), lambda g: (g + off, 0, 0))

    return pl.pallas_call(
        body, name="conv_mixer_fwd", grid=(3,),
        in_specs=[uspec(0), uspec(3), uspec(6), pl.BlockSpec((1, 8, GROUP), lambda g: (g, 0, 0))],
        out_specs=pl.BlockSpec((1, S, GROUP), lambda g: (g, 0, 0)),
        out_shape=SDS((4, S, GROUP), BF16),
        compiler_params=_params(("parallel",)),
    )(u, u, u, cw)


def conv_mixer_bwd(u, cw, dm):
    _, S, _ = u.shape
    nh = GROUP // LANES

    def body(b_ref, c_ref, x_ref, w_ref, d_ref, db_ref, dc_ref, dx_ref, dw_ref):
        cg, xi = c_ref[0], x_ref[0]
        p = cg * xi
        p1, p2 = _shift_down(p, 1), _shift_down(p, 2)
        w = w_ref[0]
        conv = w[2:3] * p + w[1:2] * p1 + w[0:1] * p2
        dt = d_ref[0]
        db_ref[0] = (dt * conv).astype(BF16)
        dcv = dt * b_ref[0]
        dp = w[2:3] * dcv + w[1:2] * _shift_up(dcv, 1) + w[0:1] * _shift_up(dcv, 2)
        dc_ref[0] = (dp * xi).astype(BF16)
        dx_ref[0] = (dp * cg).astype(BF16)
        dw = jnp.concatenate([jnp.sum(dcv * p2, axis=0, keepdims=True), jnp.sum(dcv * p1, axis=0, keepdims=True),
                              jnp.sum(dcv * p, axis=0, keepdims=True), jnp.zeros((5, LANES), F32)], axis=0)
        dw_ref[0] = dw

    def uspec(off):
        return pl.BlockSpec((1, S, LANES), lambda g, h: (g + off, 0, h))

    ospec = pl.BlockSpec((1, S, LANES), lambda g, h: (g, 0, h))
    wspec = pl.BlockSpec((1, 8, LANES), lambda g, h: (g, 0, h))
    return pl.pallas_call(
        body, name="conv_mixer_bwd", grid=(3, nh),
        in_specs=[uspec(0), uspec(3), uspec(6), wspec, ospec],
        out_specs=[ospec, ospec, ospec, wspec],
        out_shape=[SDS((3, S, GROUP), BF16)] * 3 + [SDS((3, 8, GROUP), F32)],
        compiler_params=_params(("parallel", "parallel")),
    )(u, u, u, cw, dm)


def qk_conv_fwd(u, qw):
    _, S, _ = u.shape

    def body(u_ref, w_ref, o_ref):
        x = u_ref[0]
        w = w_ref[0]
        pre = w[3:4] * x + w[2:3] * _shift_down(x, 1) + w[1:2] * _shift_down(x, 2) + w[0:1] * _shift_down(x, 3)
        o_ref[0] = pre * _sigmoid(pre)

    spec = pl.BlockSpec((1, S, GROUP), lambda g: (g, 0, 0))
    return pl.pallas_call(
        body, name="qk_conv_fwd", grid=(8,),
        in_specs=[spec, pl.BlockSpec((1, 8, GROUP), lambda g: (g, 0, 0))],
        out_specs=spec,
        out_shape=SDS((8, S, GROUP), F32),
        compiler_params=_params(("parallel",)),
    )(u, qw)


def qk_conv_bwd(u, qw, dqk, du):
    _, S, _ = u.shape
    nh = GROUP // LANES

    def body(u_ref, w_ref, d_ref, du_in_ref, du_ref, dw_ref):
        x = u_ref[0]
        w = w_ref[0]
        x1, x2, x3 = _shift_down(x, 1), _shift_down(x, 2), _shift_down(x, 3)
        pre = w[3:4] * x + w[2:3] * x1 + w[1:2] * x2 + w[0:1] * x3
        sig = _sigmoid(pre)
        dpre = d_ref[0].astype(F32) * (sig * (1.0 + pre * (1.0 - sig)))
        du = w[3:4] * dpre + w[2:3] * _shift_up(dpre, 1) + w[1:2] * _shift_up(dpre, 2) + w[0:1] * _shift_up(dpre, 3)
        du_ref[0] = du.astype(BF16)
        dw = jnp.concatenate([jnp.sum(dpre * x3, axis=0, keepdims=True), jnp.sum(dpre * x2, axis=0, keepdims=True),
                              jnp.sum(dpre * x1, axis=0, keepdims=True), jnp.sum(dpre * x, axis=0, keepdims=True),
                              jnp.zeros((4, LANES), F32)], axis=0)
        dw_ref[0] = dw

    spec = pl.BlockSpec((1, S, LANES), lambda g, h: (g, 0, h))
    wspec = pl.BlockSpec((1, 8, LANES), lambda g, h: (g, 0, h))
    return pl.pallas_call(
        body, name="qk_conv_bwd", grid=(8, nh),
        in_specs=[spec, wspec, spec, pl.BlockSpec(memory_space=pl.ANY)],
        out_specs=[spec, wspec],
        out_shape=[SDS(du.shape, BF16), SDS((8, 8, GROUP), F32)],
        input_output_aliases={3: 0},
        compiler_params=_params(("parallel", "parallel")),
    )(u, qw, dqk, du)


def _head_masks():
    lane = lax.broadcasted_iota(jnp.int32, (1, D_XA), 1)
    return [(lane >= h * XA_HEAD_DIM) & (lane < (h + 1) * XA_HEAD_DIM) for h in range(XA_HEADS)]


def xattn_fwd(u, qg, kv, tok):
    _, S, _ = u.shape
    ts = _tile(S, 1024)
    scale = XA_HEAD_DIM ** -0.5

    def body(q_ref, kv_ref, tok_ref, o_ref):
        q = q_ref[0]
        k = kv_ref[0].astype(BF16)
        v = kv_ref[1]
        o = jnp.zeros((ts, D_XA), F32)
        for m in _head_masks():
            s = _dot(jnp.where(m, q, 0.0).astype(BF16), k, NT) * scale
            s = s - jnp.max(s, axis=-1, keepdims=True)
            e = jnp.exp(s)
            p = e / jnp.sum(e, axis=-1, keepdims=True)
            o = o + _dot(p.astype(BF16), jnp.where(m, v, 0.0).astype(BF16), NN)
        o_ref[0] = o.astype(BF16)

    slot = tok.shape[0] - 1
    return pl.pallas_call(
        body, name="xattn_fwd", grid=(S // ts,),
        in_specs=[pl.BlockSpec((1, ts, GROUP), lambda s: (qg, s, 0)), pl.BlockSpec((2, N_MEM, GROUP), lambda s: (0, 0, 0)),
                  pl.BlockSpec(memory_space=pl.ANY)],
        out_specs=pl.BlockSpec((1, ts, GROUP), lambda s: (slot, s, 0)),
        out_shape=SDS(tok.shape, BF16),
        input_output_aliases={2: 0},
        compiler_params=_params(("parallel",)),
    )(u, kv, tok)


def xattn_bwd(u, qg, kv, dm, dg, du=None, dgate=None):
    _, S, _ = u.shape
    ts = _tile(S, 1024)
    scale = XA_HEAD_DIM ** -0.5

    def body(q_ref, kv_ref, do_ref, *refs):
        dq_ref, dkv_ref = refs[-2:]

        @pl.when(pl.program_id(0) == 0)
        def _():
            dkv_ref[...] = jnp.zeros_like(dkv_ref)

        q = q_ref[0]
        k = kv_ref[0]
        v = kv_ref[1]
        kb = k.astype(BF16)
        do = do_ref[0]
        dq = jnp.zeros((ts, D_XA), F32)
        dk = jnp.zeros((N_MEM, D_XA), F32)
        dv = jnp.zeros((N_MEM, D_XA), F32)
        for m in _head_masks():
            qm = jnp.where(m, q, 0.0).astype(BF16)
            s = _dot(qm, kb, NT) * scale
            s = s - jnp.max(s, axis=-1, keepdims=True)
            e = jnp.exp(s)
            p = e / jnp.sum(e, axis=-1, keepdims=True)
            dom = jnp.where(m, do, 0.0).astype(BF16)
            dp = _dot(dom, jnp.where(m, v, 0.0).astype(BF16), NT)
            ds = (p * (dp - jnp.sum(dp * p, axis=-1, keepdims=True)) * scale).astype(BF16)
            dq = dq + _dot(ds, jnp.where(m, k, 0.0).astype(BF16), NN)
            dk = dk + _dot(ds, qm, TN)
            dv = dv + _dot(p.astype(BF16), dom, TN)
        dq_ref[0] = dq.astype(BF16)
        if du is not None:
            dq_ref[1] = refs[0][0]
        dkv_ref[0] += dk
        dkv_ref[1] += dv

    in_specs = [pl.BlockSpec((1, ts, GROUP), lambda s: (qg, s, 0)), pl.BlockSpec((2, N_MEM, GROUP), lambda s: (0, 0, 0)),
                pl.BlockSpec((1, ts, GROUP), lambda s: (dg, s, 0))]
    args, aliases = [u, kv, dm], {}
    dq_spec, dq_shape = pl.BlockSpec((1, ts, GROUP), lambda s: (0, s, 0)), SDS((1, S, GROUP), BF16)
    if du is not None:
        in_specs += [pl.BlockSpec((1, ts, GROUP), lambda s: (0, s, 0)), pl.BlockSpec(memory_space=pl.ANY)]
        args += [dgate, du]
        aliases = {4: 0}
        dq_spec, dq_shape = pl.BlockSpec((2, ts, GROUP), lambda s: (qg // 2, s, 0)), SDS(du.shape, BF16)
    return pl.pallas_call(
        body, name="xattn_bwd", grid=(S // ts,),
        in_specs=in_specs,
        out_specs=[dq_spec, pl.BlockSpec((2, N_MEM, GROUP), lambda s: (0, 0, 0))],
        out_shape=[dq_shape, SDS((2, N_MEM, GROUP), F32)],
        input_output_aliases=aliases,
        compiler_params=_params(("arbitrary",)),
    )(*args)


ML_BLOCK_CHUNKS = 4
H4 = ML_HEADS
L = ML_CHUNK
NLANE = ML_HEAD_DIM


def _chunk_consts():
    r = lax.broadcasted_iota(jnp.int32, (1, L, L), 1)
    c = lax.broadcasted_iota(jnp.int32, (1, L, L), 2)
    return r >= c, r <= c, r == c


def _gate_cols(gb):
    lane = lax.broadcasted_iota(jnp.int32, gb.shape, 1)
    li = jnp.stack([jnp.sum(jnp.where(lane == h, gb, 0.0), axis=1, keepdims=True) for h in range(H4)])
    gf = jnp.stack([jnp.sum(jnp.where(lane == H4 + h, gb, 0.0), axis=1, keepdims=True) for h in range(H4)])
    return li, gf


def _log_sigmoid(x):
    return jnp.minimum(x, 0.0) - jnp.log(1.0 + jnp.exp(-jnp.abs(x)))


def _chunk_forward(q, k, v_aug, li_col, lf_col, c_prev, m_prev):
    tri, tri_t, eye = _chunk_consts()
    lf_row = jnp.sum(jnp.where(eye, lf_col, 0.0), axis=1, keepdims=True)
    li_row = jnp.sum(jnp.where(eye, li_col, 0.0), axis=1, keepdims=True)
    bcum_col = jnp.sum(jnp.where(tri, lf_row, 0.0), axis=2, keepdims=True)
    bcum_row = jnp.sum(jnp.where(tri_t, lf_col, 0.0), axis=1, keepdims=True)
    log_d = jnp.where(tri, bcum_col - bcum_row + li_row, NEG)
    log_inter = bcum_col + m_prev
    m_t = jnp.maximum(log_inter, jnp.max(log_d, axis=2, keepdims=True))
    w_intra = jnp.exp(log_d - m_t)
    w_inter = jnp.exp(log_inter - m_t)
    sc = _bdot(q, k, 2, 2) * w_intra
    qc = _bdot1(q, c_prev, 2, 1)
    num = _bdot(sc, v_aug, 2, 1) + w_inter * qc
    lane = lax.broadcasted_iota(jnp.int32, num.shape, 2)
    den = jnp.sum(jnp.where(lane == NLANE, num, 0.0), axis=2, keepdims=True)
    e_m = jnp.exp(-m_t)
    b_last = jnp.sum(lf_row, axis=2, keepdims=True)
    log_w = b_last - bcum_col + li_col
    m_new = jnp.maximum(b_last + m_prev, jnp.max(log_w, axis=1, keepdims=True))
    w_k = jnp.exp(log_w - m_new)
    decay = jnp.exp(b_last + m_prev - m_new)
    return dict(w_intra=w_intra, w_inter=w_inter, sc=sc, qc=qc, num=num, den=den, e_m=e_m, lane=lane,
                w_k=w_k, decay=decay, m_new=m_new)


def mlstm_fwd(qk, u, bg):
    _, S, _ = qk.shape
    nc = S // L
    cb = min(ML_BLOCK_CHUNKS, nc)
    rows = cb * L
    kscale = ML_HEAD_DIM ** -0.5

    def body(qk_ref, v_ref, g_ref, bg_ref, h_ref, cst_ref, mst_ref, c_sc, m_sc):
        @pl.when(pl.program_id(0) == 0)
        def _():
            c_sc[...] = jnp.zeros_like(c_sc)
            m_sc[...] = jnp.zeros_like(m_sc)

        for c in range(cb):
            sl = pl.ds(c * L, L)
            q = qk_ref[0:H4, sl, :]
            k = qk_ref[H4:2 * H4, sl, :] * kscale
            v = v_ref[:, sl, :]
            lane = lax.broadcasted_iota(jnp.int32, v.shape, 2)
            v_aug = jnp.where(lane == NLANE, 1.0, v)
            li_col, gf = _gate_cols(g_ref[0, sl, :] + bg_ref[...])
            lf_col = _log_sigmoid(gf)
            c_prev = c_sc[...]
            m_prev = m_sc[...]
            f = _chunk_forward(q, k, v_aug, li_col, lf_col, c_prev, m_prev)
            r = 1.0 / jnp.maximum(jnp.abs(f["den"]), f["e_m"])
            h_ref[:, sl, :] = jnp.where(lane < NLANE, f["num"] * r, 0.0)
            cst_ref[c] = c_prev
            mst_ref[c] = jnp.broadcast_to(m_prev, (H4, 1, LANES))
            c_sc[...] = f["decay"] * c_prev + _bdot(k * f["w_k"], v_aug, 1, 1)
            m_sc[...] = f["m_new"]

    def hspec(blk):
        return pl.BlockSpec((H4, rows, GROUP), lambda i: (blk, i, 0))

    return pl.pallas_call(
        body, name="mlstm_fwd", grid=(nc // cb,),
        in_specs=[pl.BlockSpec((2 * H4, rows, GROUP), lambda i: (0, i, 0)), hspec(2),
                  pl.BlockSpec((1, rows, GROUP), lambda i: (17, i, 0)), pl.BlockSpec((1, GROUP), lambda i: (0, 0))],
        out_specs=[hspec(0), pl.BlockSpec((cb, H4, GROUP, GROUP), lambda i: (i, 0, 0, 0)),
                   pl.BlockSpec((cb, H4, 1, LANES), lambda i: (i, 0, 0, 0))],
        out_shape=[SDS((H4, S, GROUP), F32), SDS((nc, H4, GROUP, GROUP), F32), SDS((nc, H4, 1, LANES), F32)],
        scratch_shapes=[pltpu.VMEM((H4, GROUP, GROUP), F32), pltpu.VMEM((H4, 1, 1), F32)],
        compiler_params=_params(("arbitrary",)),
    )(qk, u, u, bg)


def mlstm_bwd(qk, u, bg, cst, mst, dh, du):
    _, S, _ = qk.shape
    nc = S // L
    cb = min(ML_BLOCK_CHUNKS, nc)
    rows = cb * L
    nb = nc // cb
    kscale = ML_HEAD_DIM ** -0.5

    def body(qk_ref, v_ref, g_ref, bg_ref, cst_ref, mst_ref, dh_ref, du_in_ref, dqk_ref, dv_ref, dg_ref, dbg_ref, dc_sc):
        @pl.when(pl.program_id(0) == 0)
        def _():
            dc_sc[...] = jnp.zeros_like(dc_sc)
            dbg_ref[...] = jnp.zeros_like(dbg_ref)

        tri, tri_t, eye = _chunk_consts()
        for c in reversed(range(cb)):
            sl = pl.ds(c * L, L)
            q = qk_ref[0:H4, sl, :]
            k = qk_ref[H4:2 * H4, sl, :] * kscale
            v = v_ref[:, sl, :]
            lane = lax.broadcasted_iota(jnp.int32, v.shape, 2)
            v_aug = jnp.where(lane == NLANE, 1.0, v)
            li_col, gf = _gate_cols(g_ref[0, sl, :] + bg_ref[...])
            lf_col = _log_sigmoid(gf)
            c_prev = cst_ref[c]
            m_prev = mst_ref[c][:, :, 0:1]
            f = _chunk_forward(q, k, v_aug, li_col, lf_col, c_prev, m_prev)
            w_intra, w_inter, sc, num, den, e_m = f["w_intra"], f["w_inter"], f["sc"], f["num"], f["den"], f["e_m"]
            absd = jnp.abs(den)
            r = 1.0 / jnp.maximum(absd, e_m)
            dhv = dh_ref[:, sl, :]
            s1 = jnp.sum(jnp.where(lane < NLANE, dhv * num, 0.0), axis=2, keepdims=True)
            dden = jnp.where(absd > e_m, -s1 * r * r * jnp.sign(den), 0.0)
            dnum = jnp.where(lane == NLANE, dden, jnp.where(lane < NLANE, dhv * r, 0.0))
            dsc = _bdot1(dnum, v_aug, 2, 2)
            dv = _bdot1(sc, dnum, 1, 1)
            gmat = dsc * sc
            dqk = dsc * w_intra
            dq = _bdot1(dqk, k, 2, 1) + w_inter * _bdot1(dnum, c_prev, 2, 2)
            dk = _bdot1(dqk, q, 1, 1)
            dc_prev = _bdot(q * w_inter, dnum, 1, 1)
            dlog_inter = jnp.sum(dnum * f["qc"], axis=2, keepdims=True) * w_inter
            dbcum_col = dlog_inter + jnp.sum(gmat, axis=2, keepdims=True)
            g_row = jnp.sum(gmat, axis=1, keepdims=True)
            dcn = dc_sc[...]
            w_k, decay = f["w_k"], f["decay"]
            kw = k * w_k
            dc_prev = dc_prev + decay * dcn
            db_last = jnp.sum(jnp.sum(dcn * c_prev, axis=2, keepdims=True), axis=1, keepdims=True) * decay
            dkw = _bdot(v_aug, dcn, 2, 2)
            dv = dv + _bdot1(kw, dcn, 2, 1)
            dk = dk + dkw * w_k
            dlogw = jnp.sum(dkw * k, axis=2, keepdims=True) * w_k
            db_last = db_last + jnp.sum(dlogw, axis=1, keepdims=True)
            dbcum_col = dbcum_col - dlogw
            rowi = lax.broadcasted_iota(jnp.int32, (1, L, 1), 1)
            dbcum_col = dbcum_col + jnp.where(rowi == L - 1, db_last, 0.0)
            dbcum_row = jnp.sum(jnp.where(eye, dbcum_col, 0.0), axis=1, keepdims=True) - g_row
            dlf_col = jnp.sum(jnp.where(tri_t, dbcum_row, 0.0), axis=2, keepdims=True)
            dli_col = dlogw + jnp.sum(jnp.where(eye, g_row, 0.0), axis=2, keepdims=True)
            dgf_col = dlf_col * _sigmoid(-gf)
            lane_g = lax.broadcasted_iota(jnp.int32, (L, GROUP), 1)
            dg = jnp.zeros((L, GROUP), F32)
            for h in range(H4):
                dg = dg + jnp.where(lane_g == h, dli_col[h], 0.0) + jnp.where(lane_g == H4 + h, dgf_col[h], 0.0)
            dqk_ref[0:H4, sl, :] = dq.astype(BF16)
            dqk_ref[H4:2 * H4, sl, :] = (dk * kscale).astype(BF16)
            dv_ref[:, sl, :] = jnp.where(lane < NLANE, dv, 0.0).astype(BF16)
            dg_ref[0, sl, :] = dg.astype(BF16)
            dbg_ref[...] += jnp.sum(dg, axis=0, keepdims=True)
            dc_sc[...] = dc_prev

    def hspec(blk):
        return pl.BlockSpec((H4, rows, GROUP), lambda i: (blk, nb - 1 - i, 0))

    gspec = pl.BlockSpec((1, rows, GROUP), lambda i: (17, nb - 1 - i, 0))
    qkspec = pl.BlockSpec((2 * H4, rows, GROUP), lambda i: (0, nb - 1 - i, 0))
    return pl.pallas_call(
        body, name="mlstm_bwd", grid=(nb,),
        in_specs=[qkspec, hspec(2), gspec, pl.BlockSpec((1, GROUP), lambda i: (0, 0)),
                  pl.BlockSpec((cb, H4, GROUP, GROUP), lambda i: (nb - 1 - i, 0, 0, 0)),
                  pl.BlockSpec((cb, H4, 1, LANES), lambda i: (nb - 1 - i, 0, 0, 0)), hspec(0), pl.BlockSpec(memory_space=pl.ANY)],
        out_specs=[qkspec, hspec(2), pl.BlockSpec((1, rows, GROUP), lambda i: (0, nb - 1 - i, 0)),
                   pl.BlockSpec((1, GROUP), lambda i: (0, 0))],
        input_output_aliases={7: 1},
        out_shape=[SDS((2 * H4, S, GROUP), BF16), SDS(du.shape, BF16),
                   SDS((1, S, GROUP), BF16), SDS((1, GROUP), F32)],
        scratch_shapes=[pltpu.VMEM((H4, GROUP, GROUP), F32)],
        compiler_params=_params(("arbitrary",)),
    )(qk, u, u, bg, cst, mst, dh, du)


def head_norm_fwd(hm, u, hg):
    _, S, _ = hm.shape
    ts = _tile(S, 2048)

    def body(h_ref, o_ref, g_ref, t_ref):
        h = h_ref[0]
        lane = lax.broadcasted_iota(jnp.int32, h.shape, 1)
        valid = lane < ML_HEAD_DIM
        mu = jnp.sum(h, axis=-1, keepdims=True) * (1.0 / ML_HEAD_DIM)
        hc = jnp.where(valid, h - mu, 0.0)
        var = jnp.sum(hc * hc, axis=-1, keepdims=True) * (1.0 / ML_HEAD_DIM)
        hn = hc * lax.rsqrt(var + LN_EPS) * g_ref[0]
        t_ref[0] = (_sigmoid(o_ref[0]) * hn).astype(BF16)

    return pl.pallas_call(
        body, name="head_norm_fwd", grid=(H4, S // ts),
        in_specs=[pl.BlockSpec((1, ts, GROUP), lambda h, s: (h, s, 0)), pl.BlockSpec((1, ts, GROUP), lambda h, s: (12 + h, s, 0)),
                  pl.BlockSpec((1, 1, GROUP), lambda h, s: (h, 0, 0))],
        out_specs=pl.BlockSpec((1, ts, GROUP), lambda h, s: (h, s, 0)),
        out_shape=SDS((H4 + 1, S, GROUP), BF16),
        compiler_params=_params(("parallel", "parallel")),
    )(hm, u, hg)


def head_norm_bwd(hm, u, hg, dm):
    _, S, _ = hm.shape
    ts = _tile(S, 2048)

    def body(h_ref, o_ref, g_ref, d_ref, dh_ref, do_ref, dg_ref):
        @pl.when(pl.program_id(1) == 0)
        def _():
            dg_ref[...] = jnp.zeros_like(dg_ref)

        h = h_ref[0]
        lane = lax.broadcasted_iota(jnp.int32, h.shape, 1)
        valid = lane < ML_HEAD_DIM
        inv = 1.0 / ML_HEAD_DIM
        mu = jnp.sum(h, axis=-1, keepdims=True) * inv
        hc = jnp.where(valid, h - mu, 0.0)
        var = jnp.sum(hc * hc, axis=-1, keepdims=True) * inv
        rstd = lax.rsqrt(var + LN_EPS)
        xhat = hc * rstd
        g = g_ref[0]
        sig = _sigmoid(o_ref[0])
        dt = jnp.where(valid, d_ref[0], 0.0)
        do_ref[0] = (dt * xhat * g * sig * (1.0 - sig)).astype(BF16)
        dhn = dt * sig
        dg_ref[0] += jnp.sum(dhn * xhat, axis=0, keepdims=True)
        dxh = dhn * g
        m1 = jnp.sum(dxh, axis=-1, keepdims=True) * inv
        m2 = jnp.sum(dxh * xhat, axis=-1, keepdims=True) * inv
        dh_ref[0] = jnp.where(valid, rstd * (dxh - m1 - xhat * m2), 0.0)

    spec = pl.BlockSpec((1, ts, GROUP), lambda h, s: (h, s, 0))
    gspec = pl.BlockSpec((1, 1, GROUP), lambda h, s: (h, 0, 0))
    return pl.pallas_call(
        body, name="head_norm_bwd", grid=(H4, S // ts),
        in_specs=[spec, pl.BlockSpec((1, ts, GROUP), lambda h, s: (12 + h, s, 0)), gspec, spec],
        out_specs=[spec, pl.BlockSpec((1, ts, GROUP), lambda h, s: (12 + h, s, 0)), gspec],
        out_shape=[SDS((H4, S, GROUP), F32), SDS((u.shape[0], S, GROUP), BF16), SDS((H4, 1, GROUP), F32)],
        compiler_params=_params(("parallel", "arbitrary")),
    )(hm, u, hg, dm)


def _adamw_math(w, g, m, v):
    c1 = 1.0 / (1.0 - ADAM_B1 ** ADAM_STEP)
    c2 = 1.0 / (1.0 - ADAM_B2 ** ADAM_STEP)
    nm = ADAM_B1 * m + (1.0 - ADAM_B1) * g
    nv = ADAM_B2 * v + (1.0 - ADAM_B2) * (g * g)
    return -ADAM_LR * ((nm * c1) / (jnp.sqrt(nv * c2) + ADAM_EPS) + ADAM_WD * w), nm, nv


def _row_tile(R, cap=512):
    return R if R <= cap else max(d for d in range(8, cap + 1, 8) if R % d == 0)


def adamw_into(w, m, v, g, outs, idx, after, name):
    R, C = g.shape
    tr = _row_tile(R)
    lead = (0,) * len(idx)

    def body(w_ref, m_ref, v_ref, g_ref, *rest):
        go_ref, d_ref, nm_ref, nv_ref, token = rest[-5:]
        token[...] = jnp.zeros_like(token)
        gv = g_ref[...]
        d, nm, nv = _adamw_math(w_ref[lead], gv, m_ref[lead], v_ref[lead])
        go_ref[lead] = gv
        d_ref[lead] = d
        nm_ref[lead] = nm
        nv_ref[lead] = nv

    blk = pl.BlockSpec((1,) * len(idx) + (tr, C), lambda r: idx + (r, 0))
    any_space = pl.BlockSpec(memory_space=pl.ANY)
    in_specs, args, aliases = [blk, blk, blk, pl.BlockSpec((tr, C), lambda r: (r, 0)), any_space], [w, m, v, g, g if after is None else after], {}
    if outs is not None:
        in_specs += [any_space] * 4
        args += list(outs)
        aliases = {5 + i: i for i in range(4)}
    out = pl.pallas_call(
        body, name=name, grid=(R // tr,),
        in_specs=in_specs, out_specs=[blk] * 4 + [pl.BlockSpec((8, LANES), lambda r: (0, 0))],
        out_shape=[SDS(w.shape, F32)] * 4 + [SDS((8, LANES), F32)],
        input_output_aliases=aliases, compiler_params=_params(("arbitrary",)),
    )(*args)
    return out[:4], out[4]


def adamw(w, g, m, v, name):
    R, C = w.shape
    tr = _row_tile(R)

    def body(w_ref, g_ref, m_ref, v_ref, d_ref, nm_ref, nv_ref):
        d_ref[...], nm_ref[...], nv_ref[...] = _adamw_math(w_ref[...], g_ref[...], m_ref[...], v_ref[...])

    spec = pl.BlockSpec((tr, C), lambda i: (i, 0))
    return pl.pallas_call(
        body, name=name, grid=(R // tr,),
        in_specs=[spec] * 4, out_specs=[spec] * 3,
        out_shape=[SDS((R, C), F32)] * 3,
        compiler_params=_params(("parallel",)),
    )(w, g, m, v)


HBM = pl.BlockSpec(memory_space=pl.ANY)
ROW_SPLIT = 2
PAIR_SPLIT = 1


def _position():
    x, y, c = lax.axis_index("x"), lax.axis_index("y"), lax.axis_index("c")
    return x, y, c, [(1 - x, y), (x, 1 - y), (1 - x, 1 - y)]


def _unique(items):
    arrays = []
    for a, _ in items:
        if not any(a is b for b in arrays):
            arrays.append(a)
    return arrays, [next(i for i, b in enumerate(arrays) if b is a) for a, _ in items]


def place_own(items, me, after, name):
    arrays, src_of = _unique(items)
    n = len(items)
    shapes = [a.shape[len(p):] for a, p in items]

    def body(me_ref, *refs):
        for t in range(n):
            refs[n + 1 + t][0] = refs[t][(0,) * len(items[t][1])]

    in_specs, out_specs = [], []
    for (a, p), shp in zip(items, shapes):
        blk = shp[:-2] + (shp[-2] // ROW_SPLIT, shp[-1])
        lead = (0,) * (len(shp) - 2)
        in_specs.append(pl.BlockSpec((1,) * len(p) + blk, functools.partial(lambda r, me_ref, p, lead: p + lead + (r, 0), p=p, lead=lead)))
        out_specs.append(pl.BlockSpec((1,) + blk, functools.partial(lambda r, me_ref, lead: (me_ref[0],) + lead + (r, 0), lead=lead)))
    in_specs.append(pl.BlockSpec(memory_space=pl.ANY))
    return pl.pallas_call(
        body, name=name,
        grid_spec=pltpu.PrefetchScalarGridSpec(num_scalar_prefetch=1, grid=(ROW_SPLIT,), in_specs=in_specs, out_specs=out_specs),
        out_shape=[SDS((N_CHIPS,) + tuple(shp), a.dtype) for shp, (a, _) in zip(shapes, items)],
        compiler_params=_params(("parallel",)),
    )(me, *[arrays[i] for i in src_of], after)


SEM = pl.BlockSpec(memory_space=pltpu.SEMAPHORE)
IN_HBM = pl.BlockSpec(memory_space=pltpu.HBM)
DATAFLOW = pltpu.SideEffectType.DATAFLOW_SIDE_EFFECTING


def split_start(bufs, plan, n_copies, after, name):
    n = len(bufs)

    def body(*refs):
        send, recv, token = refs[n + 1], refs[n + 2], refs[-1]
        x, y, c, chips = _position()
        for k, (src, dst, dev) in enumerate(plan(refs[:n], x, y, c, chips)):
            pltpu.make_async_remote_copy(src_ref=src, dst_ref=dst, send_sem=send.at[k], recv_sem=recv.at[k],
                                         device_id=dev, device_id_type=MESH).start()
        token[...] = jnp.zeros_like(token)

    out = pl.pallas_call(
        body, name=name,
        out_shape=(pltpu.SemaphoreType.DMA((n_copies,)), pltpu.SemaphoreType.DMA((n_copies,)),
                   *[pltpu.HBM(b.shape, b.dtype) for b in bufs], SDS((8, LANES), F32)),
        in_specs=[IN_HBM] * n + [pl.BlockSpec(memory_space=pl.ANY)],
        out_specs=(SEM, SEM, *[IN_HBM] * n, pl.BlockSpec(memory_space=pltpu.VMEM)),
        input_output_aliases={i: 2 + i for i in range(n)},
        compiler_params=pltpu.CompilerParams(has_side_effects=DATAFLOW),
    )(*[pltpu.with_memory_space_constraint(b, pltpu.HBM) for b in bufs], after)
    return out[0], out[1], list(out[2:2 + n]), out[-1]


def split_wait(send, recv, bufs, plan, after, name):
    n = len(bufs)

    def body(*refs):
        send_ref, recv_ref = refs[n], refs[n + 1]
        x, y, c, chips = _position()
        for k, (src, dst, dev) in enumerate(plan(refs[:n], x, y, c, chips)):
            cp = pltpu.make_async_remote_copy(src_ref=src, dst_ref=dst, send_sem=send_ref.at[k], recv_sem=recv_ref.at[k],
                                              device_id=dev, device_id_type=MESH)
            cp.wait_send()
            cp.wait_recv()

    return list(pl.pallas_call(
        body, name=name, out_shape=tuple(pltpu.HBM(b.shape, b.dtype) for b in bufs),
        in_specs=[IN_HBM] * n + [SEM, SEM, pl.BlockSpec(memory_space=pl.ANY)], out_specs=tuple([IN_HBM] * n),
        input_output_aliases={i: i for i in range(n)},
        compiler_params=pltpu.CompilerParams(has_side_effects=DATAFLOW),
    )(*bufs, send, recv, after))


def _gather_plan(shapes, landing):
    n = len(shapes)

    def plan(refs, x, y, c, chips):
        out = []
        for t in range(n):
            half = shapes[t][0] // 2
            rows = pl.ds(c * half, half)
            for cx, cy in chips:
                slot = 2 * cx + cy if landing else 2 * x + y
                out.append((refs[t].at[rows], refs[n + t].at[slot, rows], (cx, cy, c)))
        return out

    return plan


def gather_start(shards, placed, after, name):
    shapes = [s.shape for s in shards]
    send, recv, bufs, token = split_start(list(shards) + list(placed), _gather_plan(shapes, False), 3 * len(shards), after, name)
    return (send, recv, bufs, shapes), token


def gather_wait(state, after, name):
    send, recv, bufs, shapes = state
    return split_wait(send, recv, bufs, _gather_plan(shapes, True), after, name)[len(shapes):]


def gather_pass_on(placed, shapes, name):
    n = len(placed)

    def body(*refs):
        outs, send, recv = refs[n:2 * n], refs[2 * n], refs[2 * n + 1]
        x, y, c, chips = _position()
        cps = []
        for t in range(n):
            half = shapes[t][0] // 2
            for j, (cx, cy) in enumerate(chips):
                piece = outs[t].at[2 * cx + cy, pl.ds(c * half, half)]
                cp = pltpu.make_async_remote_copy(src_ref=piece, dst_ref=piece, send_sem=send.at[3 * t + j], recv_sem=recv.at[3 * t + j],
                                                  device_id=(x, y, 1 - c), device_id_type=MESH)
                cp.start()
                cps.append(cp)
        for t in range(n):
            half = shapes[t][0] // 2
            for j, (cx, cy) in enumerate(chips):
                piece = outs[t].at[2 * cx + cy, pl.ds((1 - c) * half, half)]
                pltpu.make_async_remote_copy(src_ref=piece, dst_ref=piece, send_sem=send.at[3 * t + j], recv_sem=recv.at[3 * t + j],
                                             device_id=(x, y, 1 - c), device_id_type=MESH).wait_recv()
        for cp in cps:
            cp.wait_send()

    return pl.pallas_call(
        body, name=name,
        in_specs=[HBM] * n, out_specs=[HBM] * n,
        out_shape=[SDS(p.shape, p.dtype) for p in placed],
        input_output_aliases={t: t for t in range(n)},
        scratch_shapes=[pltpu.SemaphoreType.DMA((3 * n,))] * 2,
    )(*placed)


def _flip(k, x, y, c):
    return ((1 - x) if k & 4 else x, (1 - y) if k & 2 else y, (1 - c) if k & 1 else c)


def small_allgather(v, reduce):
    R, C = v.shape

    def body(v_ref, o_ref, *scratch):
        if reduce:
            buf, send, recv = scratch
        else:
            buf, (send, recv) = o_ref, scratch
        x, y, c, _ = _position()
        me = 4 * x + 2 * y + c
        buf[me] = v_ref[...]
        sends = []
        for k in range(1, N_DEV):
            cp = pltpu.make_async_remote_copy(src_ref=v_ref, dst_ref=buf.at[me], send_sem=send.at[k - 1], recv_sem=recv.at[k - 1],
                                              device_id=_flip(k, x, y, c), device_id_type=MESH)
            cp.start()
            sends.append(cp)
        for k in range(1, N_DEV):
            px, py, pc = _flip(k, x, y, c)
            pltpu.make_async_remote_copy(src_ref=v_ref, dst_ref=buf.at[4 * px + 2 * py + pc], send_sem=send.at[k - 1],
                                         recv_sem=recv.at[k - 1], device_id=(px, py, pc), device_id_type=MESH).wait_recv()
        for cp in sends:
            cp.wait_send()
        if reduce:
            acc = buf[0]
            for i in range(1, N_DEV):
                acc = acc + buf[i]
            o_ref[...] = acc

    vm = pl.BlockSpec(memory_space=pltpu.VMEM)
    sems = [pltpu.SemaphoreType.DMA((N_DEV - 1,)), pltpu.SemaphoreType.DMA((N_DEV - 1,))]
    return pl.pallas_call(
        body, name="small_allreduce" if reduce else "small_allgather",
        in_specs=[vm], out_specs=vm,
        out_shape=SDS((R, C) if reduce else (N_DEV, R, C), F32),
        scratch_shapes=([pltpu.VMEM((N_DEV, R, C), F32)] if reduce else []) + sems,
    )(v)


def rs_exchange_sibling(gs):
    n = len(gs)

    def body(*refs):
        ins, outs, send, recv = refs[:n], refs[n:2 * n], refs[2 * n], refs[2 * n + 1]
        x, y, c, _ = _position()
        cps = []
        for t in range(n):
            cp = pltpu.make_async_remote_copy(src_ref=ins[t].at[:, 1 - c], dst_ref=outs[t], send_sem=send.at[t], recv_sem=recv.at[t],
                                              device_id=(x, y, 1 - c), device_id_type=MESH)
            cp.start()
            cps.append(cp)
        for cp in cps:
            cp.wait()

    return pl.pallas_call(
        body, name="rs_exchange_sibling", in_specs=[HBM] * n, out_specs=[HBM] * n,
        out_shape=[SDS((g.shape[0],) + g.shape[2:], g.dtype) for g in gs],
        scratch_shapes=[pltpu.SemaphoreType.DMA((n,)), pltpu.SemaphoreType.DMA((n,))],
    )(*gs)


def rs_pair_add(gs, rs, c):
    n = len(gs)

    def body(c_ref, *refs):
        for t in range(n):
            refs[2 * n + t][0] = (refs[t][0, 0].astype(F32) + refs[n + t][0].astype(F32)).astype(BF16)

    in_specs, out_specs, out_shape = [], [], []
    for g in gs:
        _, _, h, C = g.shape
        in_specs.append(pl.BlockSpec((1, 1, h // PAIR_SPLIT, C), lambda j, r, c_ref: (j, c_ref[0], r, 0)))
    for g in gs:
        _, _, h, C = g.shape
        spec = pl.BlockSpec((1, h // PAIR_SPLIT, C), lambda j, r, c_ref: (j, r, 0))
        in_specs.append(spec)
        out_specs.append(spec)
        out_shape.append(SDS((N_CHIPS, h, C), BF16))
    return pl.pallas_call(
        body, name="rs_pair_add",
        grid_spec=pltpu.PrefetchScalarGridSpec(num_scalar_prefetch=1, grid=(N_CHIPS, PAIR_SPLIT), in_specs=in_specs, out_specs=out_specs),
        out_shape=out_shape, compiler_params=_params(("parallel", "parallel")),
    )(c, *gs, *rs)


def _rs_plan(n):
    def plan(refs, x, y, c, chips):
        return [(refs[t].at[2 * cx + cy], refs[n + t].at[j], (cx, cy, c)) for t in range(n) for j, (cx, cy) in enumerate(chips)]

    return plan


def rs_chip_add(ps, qs, me_c):
    n = len(ps)

    def body(me_ref, *refs):
        for t in range(n):
            q = refs[n + t]
            refs[2 * n + t][0] = ((refs[t][0].astype(F32) + q[0].astype(F32)) + q[1].astype(F32)) + q[2].astype(F32)

    in_specs, out_specs, out_shape = [], [], []
    for p in ps:
        _, h, C = p.shape
        in_specs.append(pl.BlockSpec((1, h // ROW_SPLIT, C), lambda r, me_ref: (me_ref[0], r, 0)))
    for p in ps:
        _, h, C = p.shape
        in_specs.append(pl.BlockSpec((3, h // ROW_SPLIT, C), lambda r, me_ref: (0, r, 0)))
        out_specs.append(pl.BlockSpec((1, h // ROW_SPLIT, C), lambda r, me_ref: (me_ref[1], r, 0)))
        out_shape.append(SDS((2, h, C), F32))
    return pl.pallas_call(
        body, name="rs_chip_add",
        grid_spec=pltpu.PrefetchScalarGridSpec(num_scalar_prefetch=1, grid=(ROW_SPLIT,), in_specs=in_specs, out_specs=out_specs),
        out_shape=out_shape, compiler_params=_params(("parallel",)),
    )(me_c, *ps, *qs)


def rs_share(rs):
    n = len(rs)

    def body(*refs):
        outs, send, recv = refs[n:2 * n], refs[2 * n], refs[2 * n + 1]
        x, y, c, _ = _position()
        cps = []
        for t in range(n):
            cp = pltpu.make_async_remote_copy(src_ref=outs[t].at[c], dst_ref=outs[t].at[c], send_sem=send.at[t], recv_sem=recv.at[t],
                                              device_id=(x, y, 1 - c), device_id_type=MESH)
            cp.start()
            cps.append(cp)
        for cp in cps:
            cp.wait()

    return pl.pallas_call(
        body, name="rs_share", in_specs=[HBM] * n, out_specs=[HBM] * n,
        out_shape=[SDS(r.shape, r.dtype) for r in rs],
        input_output_aliases={t: t for t in range(n)},
        scratch_shapes=[pltpu.SemaphoreType.DMA((n,))] * 2,
    )(*rs)


def rs_begin(gs, after, name):
    c = lax.axis_index("c")
    n = len(gs)
    g5 = [g.reshape(N_CHIPS, 2, g.shape[1] // 2, g.shape[2]) for g in gs]
    from_sibling = rs_exchange_sibling(g5)
    pair = rs_pair_add(g5, from_sibling, jnp.reshape(c, (1,)).astype(jnp.int32))
    lands = [lax.empty((3,) + p.shape[1:], p.dtype) for p in pair]
    send, recv, bufs, token = split_start(list(pair) + lands, _rs_plan(n), 3 * n, from_sibling[0] if after is None else after, name)
    return (send, recv, bufs, [g.shape for g in gs]), token


def rs_end(state, after, name):
    x, y, c = lax.axis_index("x"), lax.axis_index("y"), lax.axis_index("c")
    send, recv, bufs, shapes = state
    n = len(shapes)
    bufs = split_wait(send, recv, bufs, _rs_plan(n), after, name)
    half = rs_chip_add(bufs[:n], bufs[n:], jnp.stack([2 * x + y, c]).astype(jnp.int32))
    both = rs_share(half)
    return [b.reshape(s[1], s[2]) for b, s in zip(both, shapes)]


def _pad_last(a, n):
    return jnp.pad(a, [(0, 0)] * (a.ndim - 1) + [(0, n - a.shape[-1])])


def _heads_to_groups(w):
    k = w.shape[0]
    return _pad_last(w.reshape(k, ML_HEADS, ML_HEAD_DIM).transpose(1, 0, 2), GROUP)


def _groups_to_heads(g):
    return g[:, :, :ML_HEAD_DIM].transpose(1, 0, 2).reshape(g.shape[1], D_TOK)


def _cols_to_groups(w):
    k, n = w.shape
    return w.reshape(k, n // GROUP, GROUP).transpose(1, 0, 2)


def _groups_to_cols(g):
    n, k, _ = g.shape
    return g.transpose(1, 0, 2).reshape(k, n * GROUP)


def _chips_to_cols(a):
    return a.transpose(1, 0, 2).reshape(a.shape[1], -1)


def _cols_to_chips(w):
    k, n = w.shape
    return w.reshape(k, N_CHIPS, n // N_CHIPS).transpose(1, 0, 2)


def _mlstm_in_groups(w):
    parts = [_heads_to_groups(w[:, i * D_TOK:(i + 1) * D_TOK]) for i in range(4)]
    gates = _pad_last(w[:, 4 * D_TOK:4 * D_TOK + 2 * ML_HEADS], GROUP)[None]
    qmem = w[:, 4 * D_TOK + 2 * ML_HEADS:][None]
    return jnp.concatenate(parts + [qmem, gates], axis=0)


def _mlstm_in_ungroup(g):
    parts = [_groups_to_heads(g[4 * i:4 * i + 4]) for i in range(4)]
    return jnp.concatenate(parts + [g[17][:, :2 * ML_HEADS], g[16]], axis=1)


def _taps_to_groups(w, width):
    taps = w.shape[0]
    g = _pad_last(w.reshape(taps, -1, width), GROUP).transpose(1, 0, 2)
    return jnp.pad(g, ((0, 0), (0, 8 - taps), (0, 0)))


def _groups_to_taps(g, taps, width):
    return g[:, :taps, :width].transpose(1, 0, 2).reshape(taps, -1)


SMALL_IN_COLS = 384
SMALL_OUT_COLS = 1536
SECTION = 8


class _Gathered:
    def __init__(self, make_src, groups, me, after):
        self.groups, self.states, self.ready = groups, [], {}
        self.group_of = {k: gi for gi, g in enumerate(groups) for k in g}
        token, self.first = after, None
        for gi, g in enumerate(groups):
            srcs = [make_src(k, None if gi == 0 else token[0:1, 0:1]) for k in g]
            placed = place_own([(a, ()) for a in srcs], me, token, f"place_own_{gi}")
            state, token = gather_start(srcs, placed, token, f"gather_start_{gi}")
            self.states.append(state)
            if gi == 0:
                self.first = token[0:1, 0:1]
        self.started = token

    def _get(self, key, after):
        gi = self.group_of[key]
        if gi not in self.ready:
            got = gather_wait(self.states[gi], after if gi else self.started, f"gather_wait_{gi}")
            self.ready[gi] = dict(zip(self.groups[gi], gather_pass_on(got, self.states[gi][3], f"gather_pass_on_{gi}")))
        return self.ready[gi][key]

    def ffn(self, l, i, after):
        return tuple(self._get((n, l, i), after) for n in ("wg", "wu", "wd"))

    def mixer(self, l, after):
        win = _chips_to_cols(self._get(("win", l), after))
        win = _cols_to_groups(win) if l % 2 == 0 else _mlstm_in_groups(win)
        wkv = _cols_to_groups(self._get(("wkv", l), after).reshape(D_MODEL, 2 * D_XA))
        wout = self._get(("wout", l), after)
        if l % 2:
            wout = wout.reshape(D_MODEL, D_MODEL)
            tok = jnp.pad(wout[:D_TOK].reshape(ML_HEADS, ML_HEAD_DIM, D_MODEL), ((0, 0), (0, GROUP - ML_HEAD_DIM), (0, 0)))
            wout = jnp.concatenate([tok, wout[D_TOK:][None]], axis=0)
        return win, wkv, wout


class _GradSink:
    def __init__(self, apply):
        self.queue, self.apply, self.count, self.done = [], apply, 0, None

    @staticmethod
    def _by_chip(key, g):
        if key[0] == "wkv":
            return _groups_to_cols(g).reshape(N_CHIPS, D_MODEL // N_CHIPS, 2 * D_XA)
        if key[0] == "win":
            return _cols_to_chips(_groups_to_cols(g) if key[1] % 2 == 0 else _mlstm_in_ungroup(g))
        if key[0] == "wout" and key[1] % 2:
            full = jnp.concatenate([g[:ML_HEADS, :ML_HEAD_DIM].reshape(D_TOK, D_MODEL), g[ML_HEADS]], axis=0)
            return full.reshape(N_CHIPS, D_MODEL // N_CHIPS, D_MODEL)
        return g

    def push(self, grads):
        keys = list(grads)
        state, token = rs_begin([self._by_chip(k, grads[k]) for k in keys], self.done, f"rs_start_{self.count}")
        if self.queue:
            self._finish(token)
        self.queue.append((keys, state, self.count))
        self.count += 1
        return token

    def flush(self):
        self._finish(self.done)

    def _finish(self, after):
        keys, state, i = self.queue.pop(0)
        for key, g in zip(keys, rs_end(state, after, f"rs_wait_{i}")):
            self.done = self.apply(key, g, self.done)


def _local_step(x, mem, tgt, P, weights, sink):
    memb = mem.astype(BF16)
    saved = []
    pin0 = getattr(weights, "first", None)
    X, Xb = x, (x if pin0 is None else x + pin0).astype(BF16)
    after = Xb
    for l in range(DEPTH):
        s = {}
        s["x0b"] = Xb
        s["wa"] = weights.ffn(l, 0, after)
        s["g1a"], s["u1a"], s["ha"], s["z1"], X1, X1b = ffn_fwd(Xb, X, *s["wa"], P["ln_g"][l][0], P["ln_b"][l][0])
        s["x1b"] = X1b
        s["wm"] = win, wkv, wout = weights.mixer(l, X1b)
        u = proj(X1b, win, "mixer_in")
        kv = proj(memb, wkv, "mem_kv")
        s["u"], s["kv"] = u, kv
        if l % 2 == 0:
            tok = conv_mixer_fwd(u, P["convw"])
            qg = 9
        else:
            s["qk"] = qk_conv_fwd(u, P["qkw"])
            s["hm"], s["cst"], s["mst"] = mlstm_fwd(s["qk"], u, P["bg"])
            tok = head_norm_fwd(s["hm"], u, P["hg"])
            qg = 16
        s["m"] = xattn_fwd(u, qg, kv, tok)
        s["z2"], X2, X2b = contract_ln(s["m"], wout, X1, P["ln_g"][l][1], P["ln_b"][l][1], 1.0, "mixer_out_ln")
        s["x2b"] = X2b
        s["wb"] = weights.ffn(l, 1, X2b)
        s["g1b"], s["u1b"], s["hb"], s["z3"], X, Xb = ffn_fwd(X2b, X2, *s["wb"], P["ln_g"][l][2], P["ln_b"][l][2])
        after = Xb
        saved.append(s)

    loss, dX = loss_grad(X, tgt)

    G = {"ln_g": [[None] * 3 for _ in range(DEPTH)], "ln_b": [[None] * 3 for _ in range(DEPTH)]}
    pin = [jnp.zeros((1, 1), F32)]

    def ffn_backward(l, i, dX, z, xinb, g1, u1, h, w):
        k = 2 * i
        dgb, dub, dx, dyb, G["ln_g"][l][k], G["ln_b"][l][k] = ffn_bwd(dX, z, P["ln_g"][l][k] + pin[0], w[2], w[0], w[1], g1, u1)
        grads = {("wd", l, i): wgrad(h, dyb, BF16, "wgrad_down"), ("wg", l, i): wgrad(dgb, xinb, BF16, "wgrad_gate"),
                 ("wu", l, i): wgrad(dub, xinb, BF16, "wgrad_up")}
        return dx, grads

    for l in reversed(range(DEPTH)):
        s = saved[l]
        win, wkv, wout = s["wm"]
        dX, grads = ffn_backward(l, 1, dX, s["z3"], s["x2b"], s["g1b"], s["u1b"], s["hb"], s["wb"])
        dm, dz2, dz2b, G["ln_g"][l][1], G["ln_b"][l][1] = mixer_out_bwd(dX, s["z2"], P["ln_g"][l][1], wout)
        grads[("wout", l)] = wgrad(s["m"], dz2b, BF16, "wgrad_out")
        u, kv = s["u"], s["kv"]
        if l % 2 == 0:
            db, dc, dxi, G["convw"] = conv_mixer_bwd(u, P["convw"], dm)
            dq, dkv = xattn_bwd(u, 9, kv, dm, 3)
            du = jnp.concatenate([db, dc, dxi, dq], axis=0)
        else:
            dh, du, G["hg"] = head_norm_bwd(s["hm"], u, P["hg"], dm)
            dqk, du, dgate, G["bg"] = mlstm_bwd(s["qk"], u, P["bg"], s["cst"], s["mst"], dh, du)
            du, G["qkw"] = qk_conv_bwd(u, P["qkw"], dqk, du)
            du, dkv = xattn_bwd(u, 16, kv, dm, 4, du, dgate)
        grads[("win", l)] = wgrad(s["x1b"], du, BF16, "wgrad_in")
        grads[("wkv", l)] = wgrad(memb, dkv.astype(BF16), BF16, "wgrad_kv")
        dX = contract_t(du, win, dz2, "mixer_in_bwd")
        pin[0] = sink.push(grads)[0:1, 0:1]
        dX, grads = ffn_backward(l, 0, dX, s["z1"], s["x0b"], s["g1a"], s["u1a"], s["ha"], s["wa"])
        pin[0] = sink.push(grads)[0:1, 0:1]
    sink.flush()
    return loss, dX, G


def kernel(x, mem, ln_g, ln_b, ffn_w_gate, ffn_w_up, ffn_w_down, w_kv_mem, w_out, w_in_conv, conv_w, w_in_mlstm, b_gates, qk_conv_w, head_norm_g, loss_target, m_ln_g, m_ln_b, m_ffn_w_gate, m_ffn_w_up, m_ffn_w_down, m_w_kv_mem, m_w_out, m_w_in_conv, m_conv_w, m_w_in_mlstm, m_b_gates, m_qk_conv_w, m_head_norm_g, v_ln_g, v_ln_b, v_ffn_w_gate, v_ffn_w_up, v_ffn_w_down, v_w_kv_mem, v_w_out, v_w_in_conv, v_conv_w, v_w_in_mlstm, v_b_gates, v_qk_conv_w, v_head_norm_g):
    cx, cy = lax.axis_index("x"), lax.axis_index("y")
    chip = 2 * cx + cy

    def make_src(key, pin):
        if key[0] in ("wg", "wu"):
            w = jnp.swapaxes((ffn_w_gate if key[0] == "wg" else ffn_w_up)[key[1], key[2]], 0, 1)
        elif key[0] == "wd":
            w = ffn_w_down[key[1], key[2]]
        elif key[0] == "win":
            w = (w_in_conv, w_in_mlstm)[key[1]][0]
        else:
            w = (w_kv_mem if key[0] == "wkv" else w_out)[key[1]]
        return (w if pin is None else w + pin).astype(BF16)

    ffn_keys = lambda l, i: [("wg", l, i), ("wu", l, i), ("wd", l, i)]
    mixer_keys = lambda l: [("win", l), ("wkv", l), ("wout", l)]
    groups = [ffn_keys(0, 0), mixer_keys(0) + mixer_keys(1), ffn_keys(0, 1), ffn_keys(1, 0), ffn_keys(1, 1)]
    def section(a, width):
        a = a.reshape(-1, a.shape[-1])
        return jnp.pad(a, ((0, SECTION - a.shape[0]), (0, width - a.shape[1])))

    small = jnp.concatenate([section(a, SMALL_IN_COLS) for a in (ln_g, ln_b, conv_w, qk_conv_w)], axis=0)
    smalls = small_allgather(small, reduce=False)
    gathered = _Gathered(make_src, groups, jnp.reshape(chip, (1,)).astype(jnp.int32), smalls)
    smalls = smalls[0::2]
    ln_g_full = _chips_to_cols(smalls[:, 0:6, 0:256]).reshape(DEPTH, 3, 1, D_MODEL)
    ln_b_full = _chips_to_cols(smalls[:, 8:14, 0:256]).reshape(DEPTH, 3, 1, D_MODEL)
    conv_w_full = _chips_to_cols(smalls[:, 16:19, 0:192])
    qk_w_full = _chips_to_cols(smalls[:, 24:28, 0:384])

    P = {"ln_g": ln_g_full, "ln_b": ln_b_full, "convw": _taps_to_groups(conv_w_full, GROUP),
         "qkw": _taps_to_groups(qk_w_full, ML_HEAD_DIM), "bg": _pad_last(b_gates, GROUP),
         "hg": _pad_last(head_norm_g[0], GROUP)[:, None, :]}

    weights = {"ln_g": ln_g, "ln_b": ln_b, "ffn_w_gate": ffn_w_gate, "ffn_w_up": ffn_w_up, "ffn_w_down": ffn_w_down,
               "w_kv_mem": w_kv_mem, "w_out": w_out, "w_in_conv": w_in_conv, "conv_w": conv_w, "w_in_mlstm": w_in_mlstm,
               "b_gates": b_gates, "qk_conv_w": qk_conv_w, "head_norm_g": head_norm_g}
    ms = {"ln_g": m_ln_g, "ln_b": m_ln_b, "ffn_w_gate": m_ffn_w_gate, "ffn_w_up": m_ffn_w_up, "ffn_w_down": m_ffn_w_down,
          "w_kv_mem": m_w_kv_mem, "w_out": m_w_out, "w_in_conv": m_w_in_conv, "conv_w": m_conv_w, "w_in_mlstm": m_w_in_mlstm,
          "b_gates": m_b_gates, "qk_conv_w": m_qk_conv_w, "head_norm_g": m_head_norm_g}
    vs = {"ln_g": v_ln_g, "ln_b": v_ln_b, "ffn_w_gate": v_ffn_w_gate, "ffn_w_up": v_ffn_w_up, "ffn_w_down": v_ffn_w_down,
          "w_kv_mem": v_w_kv_mem, "w_out": v_w_out, "w_in_conv": v_w_in_conv, "conv_w": v_conv_w, "w_in_mlstm": v_w_in_mlstm,
          "b_gates": v_b_gates, "qk_conv_w": v_qk_conv_w, "head_norm_g": v_head_norm_g}
    names = list(weights)
    owner = {"wg": ("ffn_w_gate", True), "wu": ("ffn_w_up", True), "wd": ("ffn_w_down", False), "wkv": ("w_kv_mem", False),
             "wout": ("w_out", False), "win": None}
    updated = {}

    def apply(key, g, after):
        name, transposed = owner[key[0]] or (("w_in_conv", "w_in_mlstm")[key[1]], False)
        idx = (0,) if key[0] == "win" else tuple(key[1:])
        view = (lambda a: jnp.swapaxes(a, -1, -2)) if transposed else (lambda a: a)
        updated[name], token = adamw_into(view(weights[name]), view(ms[name]), view(vs[name]), g, updated.get(name), idx, after,
                                          "adamw_" + name + "_" + "_".join(map(str, idx)))
        return token

    sink = _GradSink(apply)
    loss, grad_x, G = _local_step(x[0], mem[0], loss_target[0], P, gathered, sink)

    dln_g = jnp.concatenate([G["ln_g"][l][k] for l in range(DEPTH) for k in range(3)], axis=0)
    dln_b = jnp.concatenate([G["ln_b"][l][k] for l in range(DEPTH) for k in range(3)], axis=0)
    lane = lax.broadcasted_iota(jnp.int32, (1, GROUP), 1)
    misc = jnp.where(lane < 8, G["bg"], 0.0) + jnp.where(lane == 8, loss, 0.0) + sink.done[0:1, 0:1]
    parts = (dln_g, dln_b, _groups_to_taps(G["convw"], 3, GROUP), misc, _groups_to_taps(G["qkw"], 4, ML_HEAD_DIM),
             G["hg"][:, 0, :ML_HEAD_DIM])
    tot = small_allgather(jnp.concatenate([section(a, SMALL_OUT_COLS) for a in parts], axis=0), reduce=True)
    loss_total = tot[24, 8]

    small_grads = {
        "ln_g": lax.dynamic_slice(tot[0:6, 0:D_MODEL], (0, chip * 256), (6, 256)).reshape(DEPTH, 3, 256),
        "ln_b": lax.dynamic_slice(tot[8:14, 0:D_MODEL], (0, chip * 256), (6, 256)).reshape(DEPTH, 3, 256),
        "conv_w": lax.dynamic_slice(tot[16:19, 0:D_TOK], (0, chip * 192), (3, 192))[None],
        "b_gates": tot[24:25, 0:8],
        "qk_conv_w": lax.dynamic_slice(tot[32:36, 0:2 * D_TOK], (0, chip * 384), (4, 384))[None],
        "head_norm_g": tot[40:44, 0:ML_HEAD_DIM][None],
    }
    grads, deltas, new_m, new_v = [], [], [], []
    for nme in names:
        if nme in updated:
            back = (lambda a: jnp.swapaxes(a, -1, -2)) if nme in ("ffn_w_gate", "ffn_w_up") else (lambda a: a)
            g, d, nm, nv = (back(a) for a in updated[nme])
        else:
            w, g = weights[nme], small_grads[nme]
            two = (math.prod(w.shape[:-1]), w.shape[-1])
            d, nm, nv = (a.reshape(w.shape) for a in adamw(w.reshape(two), g.reshape(two), ms[nme].reshape(two),
                                                           vs[nme].reshape(two), "adamw_" + nme))
        grads.append(g)
        deltas.append(d)
        new_m.append(nm)
        new_v.append(nv)
    return (loss_total, grad_x[None], *grads, *deltas, *new_m, *new_v)
```

```python
import functools
import math

import jax
import jax.numpy as jnp
from jax import lax
from jax.experimental import pallas as pl
from jax.experimental.pallas import tpu as pltpu

F32 = jnp.float32
BF16 = jnp.bfloat16
SDS = jax.ShapeDtypeStruct

D_MODEL = 1024
DEPTH = 2
N_MEM = 256
XA_HEADS = 4
XA_HEAD_DIM = 64
D_XA = 256
D_TOK = 768
ML_HEADS = 4
ML_HEAD_DIM = 192
ML_CHUNK = 64
D_FF = 2816
LN_EPS = 1e-5
ALPHA = (2.0 * DEPTH) ** 0.25
N_CHIPS = 4
N_DEV = 8
FF_SHARD = D_FF // N_CHIPS
GROUP = 256
NEG = -1e30

ADAM_LR = 0.001
ADAM_B1 = 0.9
ADAM_B2 = 0.999
ADAM_EPS = 1e-08
ADAM_WD = 0.01
ADAM_STEP = 10

VMEM_LIMIT = 56 * 1024 * 1024

NN = ((1,), (0,))
NT = ((1,), (1,))
TN = ((0,), (0,))
MESH = pl.DeviceIdType.MESH


def _dot(a, b, dims):
    return lax.dot_general(a, b, (dims, ((), ())), preferred_element_type=F32)


def _bdot(a, b, ca, cb):
    dims = (((ca,), (cb,)), ((0,), (0,)))
    ah, bh = a.astype(BF16), b.astype(BF16)
    al, bl = (a - ah.astype(F32)).astype(BF16), (b - bh.astype(F32)).astype(BF16)
    dot = functools.partial(lax.dot_general, dimension_numbers=dims, preferred_element_type=F32)
    return dot(ah, bh) + dot(al, bh) + dot(ah, bl)


def _bdot1(a, b, ca, cb):
    return lax.dot_general(a.astype(BF16), b.astype(BF16), (((ca,), (cb,)), ((0,), (0,))), preferred_element_type=F32)


def _sigmoid(x):
    return 1.0 / (1.0 + jnp.exp(-x))


def _params(sem, vmem=VMEM_LIMIT):
    return pltpu.CompilerParams(dimension_semantics=sem, vmem_limit_bytes=vmem)


def _tile(n, want):
    t = min(n, want)
    assert n % t == 0, (n, t)
    return t


def _layer_norm(z, gamma, beta):
    mu = jnp.mean(z, axis=-1, keepdims=True)
    zc = z - mu
    var = jnp.mean(zc * zc, axis=-1, keepdims=True)
    return zc * lax.rsqrt(var + LN_EPS) * gamma + beta


def _column_halves(n):
    mid = -(-n // (2 * 128)) * 128
    return ((0, mid), (mid, n))


def _resident(shape):
    return pl.BlockSpec(shape, lambda *_: (0,) * len(shape), pipeline_mode=pl.Buffered(1))


def _group_block(G, want):
    return max(d for d in range(1, max(1, min(G, want)) + 1) if G % d == 0)


def ffn_fwd(xb, x, wg, wu, wd, gamma, beta):
    S, K = xb.shape
    G, N, _ = wg.shape
    ts = _tile(S, 1024)

    def body(xb_ref, x_ref, wg_ref, wu_ref, wd_ref, gm_ref, bt_ref, g_ref, u_ref, h_ref, z_ref, xn_ref, xnb_ref):
        j = pl.program_id(1)
        xv = xb_ref[...]
        g = _dot(xv, wg_ref[j], NT)
        u = _dot(xv, wu_ref[j], NT)
        h = (g * _sigmoid(g) * u).astype(BF16)
        g_ref[0] = g.astype(BF16)
        u_ref[0] = u.astype(BF16)
        h_ref[0] = h
        y = _dot(h, wd_ref[j], NN)

        @pl.when(j == 0)
        def _():
            z_ref[...] = y

        @pl.when(j > 0)
        def _():
            z_ref[...] += y

        @pl.when(j == G - 1)
        def _():
            z = ALPHA * x_ref[...] + 0.5 * z_ref[...]
            xn = _layer_norm(z, gm_ref[...], bt_ref[...])
            z_ref[...] = z
            xn_ref[...] = xn
            xnb_ref[...] = xn.astype(BF16)

    row = pl.BlockSpec((ts, K), lambda s, j: (s, 0), pipeline_mode=pl.Buffered(1))
    vec = pl.BlockSpec((1, K), lambda s, j: (0, 0))
    wspec = _resident((G, N, K))
    ospec = pl.BlockSpec((1, ts, N), lambda s, j: (j, s, 0))
    return pl.pallas_call(
        body, name="ffn_fwd", grid=(S // ts, G),
        in_specs=[row, row, wspec, wspec, wspec, vec, vec],
        out_specs=[ospec, ospec, ospec, row, row, row],
        out_shape=[SDS((G, S, N), BF16), SDS((G, S, N), BF16), SDS((G, S, N), BF16),
                   SDS((S, K), F32), SDS((S, K), F32), SDS((S, K), BF16)],
        compiler_params=_params(("parallel", "arbitrary")),
    )(xb, x, wg, wu, wd, gamma, beta)


def proj(xb, w, name):
    S, K = xb.shape
    G, _, N = w.shape
    ts = _tile(S, 1024)
    gb = _group_block(G, 6)

    def body(x_ref, w_ref, y_ref):
        xv = x_ref[...]
        for j in range(gb):
            y_ref[j] = _dot(xv, w_ref[j], NN)

    return pl.pallas_call(
        body, name=name, grid=(S // ts, G // gb),
        in_specs=[pl.BlockSpec((ts, K), lambda s, g: (s, 0)), pl.BlockSpec((gb, K, N), lambda s, g: (g, 0, 0))],
        out_specs=pl.BlockSpec((gb, ts, N), lambda s, g: (g, s, 0)),
        out_shape=SDS((G, S, N), F32),
        compiler_params=_params(("parallel", "parallel")),
    )(xb, w)


def contract_ln(a, w, xres, gamma, beta, scale, name):
    G, S, Kg = a.shape
    N = w.shape[2]
    ts = _tile(S, 1024)

    def body(a_ref, w_ref, x_ref, g_ref, b_ref, z_ref, xn_ref, xb_ref):
        acc = _dot(a_ref[0], w_ref[0], NN)
        for j in range(1, G):
            acc = acc + _dot(a_ref[j], w_ref[j], NN)
        z = ALPHA * x_ref[...] + scale * acc
        xn = _layer_norm(z, g_ref[...], b_ref[...])
        z_ref[...] = z
        xn_ref[...] = xn
        xb_ref[...] = xn.astype(BF16)

    row = pl.BlockSpec((ts, N), lambda s: (s, 0))
    vec = pl.BlockSpec((1, N), lambda s: (0, 0))
    return pl.pallas_call(
        body, name=name, grid=(S // ts,),
        in_specs=[pl.BlockSpec((G, ts, Kg), lambda s: (0, s, 0)), pl.BlockSpec((G, Kg, N), lambda s: (0, 0, 0)), row, vec, vec],
        out_specs=[row, row, row],
        out_shape=[SDS((S, N), F32), SDS((S, N), F32), SDS((S, N), BF16)],
        compiler_params=_params(("parallel",)),
    )(a, w, xres, gamma, beta)


def _layer_norm_bwd(dx, z, gamma):
    mu = jnp.mean(z, axis=-1, keepdims=True)
    zc = z - mu
    var = jnp.mean(zc * zc, axis=-1, keepdims=True)
    rstd = lax.rsqrt(var + LN_EPS)
    xhat = zc * rstd
    dxh = dx * gamma
    m1 = jnp.mean(dxh, axis=-1, keepdims=True)
    m2 = jnp.mean(dxh * xhat, axis=-1, keepdims=True)
    return rstd * (dxh - m1 - xhat * m2), jnp.sum(dx * xhat, axis=0, keepdims=True), jnp.sum(dx, axis=0, keepdims=True)


def ffn_bwd(dxn, z, gamma, wd, wg, wu, g1, u1):
    S, K = dxn.shape
    G, N, _ = wd.shape
    ts = _tile(S, 512)

    def body(dxn_ref, z_ref, gm_ref, wd_ref, wg_ref, wu_ref, g_ref, u_ref, dg_ref, du_ref, dx_ref, dy_ref, dgm_ref, dbt_ref):
        s, j = pl.program_id(0), pl.program_id(1)

        @pl.when((s == 0) & (j == 0))
        def _():
            dgm_ref[...] = jnp.zeros_like(dgm_ref)
            dbt_ref[...] = jnp.zeros_like(dbt_ref)

        @pl.when(j == 0)
        def _():
            dz, dgm, dbt = _layer_norm_bwd(dxn_ref[...], z_ref[...], gm_ref[...])
            dgm_ref[...] += dgm
            dbt_ref[...] += dbt
            dx_ref[...] = ALPHA * dz
            dy_ref[...] = (0.5 * dz).astype(BF16)

        dy = dy_ref[...]
        part = None
        for a, b in _column_halves(N):
            dh = _dot(dy, wd_ref[j, a:b, :], NT)
            g = g_ref[0, :, a:b].astype(F32)
            sig = _sigmoid(g)
            dg = (dh * u_ref[0, :, a:b].astype(F32) * (sig * (1.0 + g * (1.0 - sig)))).astype(BF16)
            du = (dh * (g * sig)).astype(BF16)
            dg_ref[0, :, a:b] = dg
            du_ref[0, :, a:b] = du
            p = _dot(dg, wg_ref[j, a:b, :], NN) + _dot(du, wu_ref[j, a:b, :], NN)
            part = p if part is None else part + p
        dx_ref[...] += part

    row = pl.BlockSpec((ts, K), lambda s, j: (s, 0))
    vec = pl.BlockSpec((1, K), lambda s, j: (0, 0))
    gspec = pl.BlockSpec((1, ts, N), lambda s, j: (j, s, 0))
    wspec = _resident((G, N, K))
    return pl.pallas_call(
        body, name="ffn_bwd", grid=(S // ts, G),
        in_specs=[row, row, vec, wspec, wspec, wspec, gspec, gspec],
        out_specs=[gspec, gspec, row, row, vec, vec],
        out_shape=[SDS((G, S, N), BF16), SDS((G, S, N), BF16), SDS((S, K), F32), SDS((S, K), BF16),
                   SDS((1, K), F32), SDS((1, K), F32)],
        compiler_params=_params(("arbitrary", "arbitrary")),
    )(dxn, z, gamma, wd, wg, wu, g1, u1)


def mixer_out_bwd(dxn, z, gamma, w):
    S, N = dxn.shape
    G, Kg, _ = w.shape
    ts = _tile(S, 512)

    def body(dxn_ref, z_ref, gm_ref, w_ref, dm_ref, dz_ref, dzb_ref, dgm_ref, dbt_ref):
        @pl.when(pl.program_id(0) == 0)
        def _():
            dgm_ref[...] = jnp.zeros_like(dgm_ref)
            dbt_ref[...] = jnp.zeros_like(dbt_ref)

        dz, dgm, dbt = _layer_norm_bwd(dxn_ref[...], z_ref[...], gm_ref[...])
        dgm_ref[...] += dgm
        dbt_ref[...] += dbt
        dzb = dz.astype(BF16)
        dz_ref[...] = dz
        dzb_ref[...] = dzb
        for j in range(G):
            dm_ref[j] = _dot(dzb, w_ref[j], NT)

    row = pl.BlockSpec((ts, N), lambda s: (s, 0))
    vec = pl.BlockSpec((1, N), lambda s: (0, 0))
    return pl.pallas_call(
        body, name="mixer_out_bwd", grid=(S // ts,),
        in_specs=[row, row, vec, pl.BlockSpec((G, Kg, N), lambda s: (0, 0, 0))],
        out_specs=[pl.BlockSpec((G, ts, Kg), lambda s: (0, s, 0)), row, row, vec, vec],
        out_shape=[SDS((G, S, Kg), F32), SDS((S, N), F32), SDS((S, N), BF16), SDS((1, N), F32), SDS((1, N), F32)],
        compiler_params=_params(("arbitrary",)),
    )(dxn, z, gamma, w)


def contract_t(da, w, res, name):
    G, S, Ng = da.shape
    K = w.shape[1]
    ts = _tile(S, 1024)
    gb = _group_block(G, 6)

    def body(da_ref, w_ref, r_ref, o_ref):
        g = pl.program_id(1)
        part = _dot(da_ref[0], w_ref[0], NT)
        for j in range(1, gb):
            part = part + _dot(da_ref[j], w_ref[j], NT)

        @pl.when(g == 0)
        def _():
            o_ref[...] = ALPHA * r_ref[...] + part

        @pl.when(g > 0)
        def _():
            o_ref[...] += part

    row = pl.BlockSpec((ts, K), lambda s, g: (s, 0))
    return pl.pallas_call(
        body, name=name, grid=(S // ts, G // gb),
        in_specs=[pl.BlockSpec((gb, ts, Ng), lambda s, g: (g, s, 0)), pl.BlockSpec((gb, K, Ng), lambda s, g: (g, 0, 0)), row],
        out_specs=row,
        out_shape=SDS((S, K), F32),
        compiler_params=_params(("parallel", "arbitrary")),
    )(da, w, res)


WGRAD_ACC_ELEMS = 6 * 1024 * 256


def wgrad(a, b, out_dtype, name):
    ga, gb = a.ndim == 3, b.ndim == 3
    G = a.shape[0] if ga else b.shape[0]
    S, K = a.shape[-2:]
    N = b.shape[-1]
    ts = _tile(S, 2048)
    ns = S // ts
    ng = _group_block(G, WGRAD_ACC_ELEMS // (K * N))

    def body(a_ref, b_ref, o_ref, acc):
        s = pl.program_id(1)

        @pl.when(s == 0)
        def _():
            acc[...] = jnp.zeros_like(acc)

        for j in range(ng):
            acc[j] += _dot(a_ref[j] if ga else a_ref[...], b_ref[j] if gb else b_ref[...], TN)

        @pl.when(s == ns - 1)
        def _():
            o_ref[...] = acc[...].astype(out_dtype)

    aspec = pl.BlockSpec((ng, ts, K), lambda g, s: (g, s, 0)) if ga else pl.BlockSpec((ts, K), lambda g, s: (s, 0))
    bspec = pl.BlockSpec((ng, ts, N), lambda g, s: (g, s, 0)) if gb else pl.BlockSpec((ts, N), lambda g, s: (s, 0))
    return pl.pallas_call(
        body, name=name, grid=(G // ng, ns),
        in_specs=[aspec, bspec],
        out_specs=pl.BlockSpec((ng, K, N), lambda g, s: (g, 0, 0)),
        out_shape=SDS((G, K, N), out_dtype),
        scratch_shapes=[pltpu.VMEM((ng, K, N), F32)],
        compiler_params=_params(("parallel", "arbitrary")),
    )(a, b)


def loss_grad(xn, tgt):
    S, N = xn.shape
    ts = _tile(S, 1024)

    def body(x_ref, t_ref, l_ref, dx_ref):
        @pl.when(pl.program_id(0) == 0)
        def _():
            l_ref[...] = jnp.zeros_like(l_ref)

        e = x_ref[...] - t_ref[...]
        dx_ref[...] = e * (1.0 / N)
        l_ref[...] += 0.5 * jnp.sum(jnp.mean(e * e, axis=-1, keepdims=True), axis=0, keepdims=True)

    row = pl.BlockSpec((ts, N), lambda s: (s, 0))
    return pl.pallas_call(
        body, name="loss_grad", grid=(S // ts,),
        in_specs=[row, row],
        out_specs=[pl.BlockSpec((1, 1), lambda s: (0, 0)), row],
        out_shape=[SDS((1, 1), F32), SDS((S, N), F32)],
        compiler_params=_params(("arbitrary",)),
    )(xn, tgt)


def _shift_down(x, k):
    if k == 0:
        return x
    rows = lax.broadcasted_iota(jnp.int32, x.shape, 0)
    return jnp.where(rows >= k, pltpu.roll(x, k, 0), 0.0)


def _shift_up(x, k):
    if k == 0:
        return x
    n = x.shape[0]
    rows = lax.broadcasted_iota(jnp.int32, x.shape, 0)
    return jnp.where(rows < n - k, pltpu.roll(x, n - k, 0), 0.0)


LANES = 128


def conv_mixer_fwd(u, cw):
    _, S, _ = u.shape

    def body(b_ref, c_ref, x_ref, w_ref, o_ref):
        p = c_ref[0] * x_ref[0]
        w = w_ref[0]
        conv = w[2:3] * p + w[1:2] * _shift_down(p, 1) + w[0:1] * _shift_down(p, 2)
        o_ref[0] = (b_ref[0] * conv).astype(BF16)

    def uspec(off):
        return pl.BlockSpec((1, S, GROUP), lambda g: (g + off, 0, 0))

    return pl.pallas_call(
        body, name="conv_mixer_fwd", grid=(3,),
        in_specs=[uspec(0), uspec(3), uspec(6), pl.BlockSpec((1, 8, GROUP), lambda g: (g, 0, 0))],
        out_specs=pl.BlockSpec((1, S, GROUP), lambda g: (g, 0, 0)),
        out_shape=SDS((4, S, GROUP), BF16),
        compiler_params=_params(("parallel",)),
    )(u, u, u, cw)


def conv_mixer_bwd(u, cw, dm):
    _, S, _ = u.shape
    nh = GROUP // LANES

    def body(b_ref, c_ref, x_ref, w_ref, d_ref, db_ref, dc_ref, dx_ref, dw_ref):
        cg, xi = c_ref[0], x_ref[0]
        p = cg * xi
        p1, p2 = _shift_down(p, 1), _shift_down(p, 2)
        w = w_ref[0]
        conv = w[2:3] * p + w[1:2] * p1 + w[0:1] * p2
        dt = d_ref[0]
        db_ref[0] = (dt * conv).astype(BF16)
        dcv = dt * b_ref[0]
        dp = w[2:3] * dcv + w[1:2] * _shift_up(dcv, 1) + w[0:1] * _shift_up(dcv, 2)
        dc_ref[0] = (dp * xi).astype(BF16)
        dx_ref[0] = (dp * cg).astype(BF16)
        dw = jnp.concatenate([jnp.sum(dcv * p2, axis=0, keepdims=True), jnp.sum(dcv * p1, axis=0, keepdims=True),
                              jnp.sum(dcv * p, axis=0, keepdims=True), jnp.zeros((5, LANES), F32)], axis=0)
        dw_ref[0] = dw

    def uspec(off):
        return pl.BlockSpec((1, S, LANES), lambda g, h: (g + off, 0, h))

    ospec = pl.BlockSpec((1, S, LANES), lambda g, h: (g, 0, h))
    wspec = pl.BlockSpec((1, 8, LANES), lambda g, h: (g, 0, h))
    return pl.pallas_call(
        body, name="conv_mixer_bwd", grid=(3, nh),
        in_specs=[uspec(0), uspec(3), uspec(6), wspec, ospec],
        out_specs=[ospec, ospec, ospec, wspec],
        out_shape=[SDS((3, S, GROUP), BF16)] * 3 + [SDS((3, 8, GROUP), F32)],
        compiler_params=_params(("parallel", "parallel")),
    )(u, u, u, cw, dm)


def qk_conv_fwd(u, qw):
    _, S, _ = u.shape

    def body(u_ref, w_ref, o_ref):
        x = u_ref[0]
        w = w_ref[0]
        pre = w[3:4] * x + w[2:3] * _shift_down(x, 1) + w[1:2] * _shift_down(x, 2) + w[0:1] * _shift_down(x, 3)
        o_ref[0] = pre * _sigmoid(pre)

    spec = pl.BlockSpec((1, S, GROUP), lambda g: (g, 0, 0))
    return pl.pallas_call(
        body, name="qk_conv_fwd", grid=(8,),
        in_specs=[spec, pl.BlockSpec((1, 8, GROUP), lambda g: (g, 0, 0))],
        out_specs=spec,
        out_shape=SDS((8, S, GROUP), F32),
        compiler_params=_params(("parallel",)),
    )(u, qw)


def qk_conv_bwd(u, qw, dqk, du):
    _, S, _ = u.shape
    nh = GROUP // LANES

    def body(u_ref, w_ref, d_ref, du_in_ref, du_ref, dw_ref):
        x = u_ref[0]
        w = w_ref[0]
        x1, x2, x3 = _shift_down(x, 1), _shift_down(x, 2), _shift_down(x, 3)
        pre = w[3:4] * x + w[2:3] * x1 + w[1:2] * x2 + w[0:1] * x3
        sig = _sigmoid(pre)
        dpre = d_ref[0].astype(F32) * (sig * (1.0 + pre * (1.0 - sig)))
        du = w[3:4] * dpre + w[2:3] * _shift_up(dpre, 1) + w[1:2] * _shift_up(dpre, 2) + w[0:1] * _shift_up(dpre, 3)
        du_ref[0] = du.astype(BF16)
        dw = jnp.concatenate([jnp.sum(dpre * x3, axis=0, keepdims=True), jnp.sum(dpre * x2, axis=0, keepdims=True),
                              jnp.sum(dpre * x1, axis=0, keepdims=True), jnp.sum(dpre * x, axis=0, keepdims=True),
                              jnp.zeros((4, LANES), F32)], axis=0)
        dw_ref[0] = dw

    spec = pl.BlockSpec((1, S, LANES), lambda g, h: (g, 0, h))
    wspec = pl.BlockSpec((1, 8, LANES), lambda g, h: (g, 0, h))
    return pl.pallas_call(
        body, name="qk_conv_bwd", grid=(8, nh),
        in_specs=[spec, wspec, spec, pl.BlockSpec(memory_space=pl.ANY)],
        out_specs=[spec, wspec],
        out_shape=[SDS(du.shape, BF16), SDS((8, 8, GROUP), F32)],
        input_output_aliases={3: 0},
        compiler_params=_params(("parallel", "parallel")),
    )(u, qw, dqk, du)


def _head_masks():
    lane = lax.broadcasted_iota(jnp.int32, (1, D_XA), 1)
    return [(lane >= h * XA_HEAD_DIM) & (lane < (h + 1) * XA_HEAD_DIM) for h in range(XA_HEADS)]


def xattn_fwd(u, qg, kv, tok):
    _, S, _ = u.shape
    ts = _tile(S, 1024)
    scale = XA_HEAD_DIM ** -0.5

    def body(q_ref, kv_ref, tok_ref, o_ref):
        q = q_ref[0]
        k = kv_ref[0].astype(BF16)
        v = kv_ref[1]
        o = jnp.zeros((ts, D_XA), F32)
        for m in _head_masks():
            s = _dot(jnp.where(m, q, 0.0).astype(BF16), k, NT) * scale
            s = s - jnp.max(s, axis=-1, keepdims=True)
            e = jnp.exp(s)
            p = e / jnp.sum(e, axis=-1, keepdims=True)
            o = o + _dot(p.astype(BF16), jnp.where(m, v, 0.0).astype(BF16), NN)
        o_ref[0] = o.astype(BF16)

    slot = tok.shape[0] - 1
    return pl.pallas_call(
        body, name="xattn_fwd", grid=(S // ts,),
        in_specs=[pl.BlockSpec((1, ts, GROUP), lambda s: (qg, s, 0)), pl.BlockSpec((2, N_MEM, GROUP), lambda s: (0, 0, 0)),
                  pl.BlockSpec(memory_space=pl.ANY)],
        out_specs=pl.BlockSpec((1, ts, GROUP), lambda s: (slot, s, 0)),
        out_shape=SDS(tok.shape, BF16),
        input_output_aliases={2: 0},
        compiler_params=_params(("parallel",)),
    )(u, kv, tok)


def xattn_bwd(u, qg, kv, dm, dg, du=None, dgate=None):
    _, S, _ = u.shape
    ts = _tile(S, 1024)
    scale = XA_HEAD_DIM ** -0.5

    def body(q_ref, kv_ref, do_ref, *refs):
        dq_ref, dkv_ref = refs[-2:]

        @pl.when(pl.program_id(0) == 0)
        def _():
            dkv_ref[...] = jnp.zeros_like(dkv_ref)

        q = q_ref[0]
        k = kv_ref[0]
        v = kv_ref[1]
        kb = k.astype(BF16)
        do = do_ref[0]
        dq = jnp.zeros((ts, D_XA), F32)
        dk = jnp.zeros((N_MEM, D_XA), F32)
        dv = jnp.zeros((N_MEM, D_XA), F32)
        for m in _head_masks():
            qm = jnp.where(m, q, 0.0).astype(BF16)
            s = _dot(qm, kb, NT) * scale
            s = s - jnp.max(s, axis=-1, keepdims=True)
            e = jnp.exp(s)
            p = e / jnp.sum(e, axis=-1, keepdims=True)
            dom = jnp.where(m, do, 0.0).astype(BF16)
            dp = _dot(dom, jnp.where(m, v, 0.0).astype(BF16), NT)
            ds = (p * (dp - jnp.sum(dp * p, axis=-1, keepdims=True)) * scale).astype(BF16)
            dq = dq + _dot(ds, jnp.where(m, k, 0.0).astype(BF16), NN)
            dk = dk + _dot(ds, qm, TN)
            dv = dv + _dot(p.astype(BF16), dom, TN)
        dq_ref[0] = dq.astype(BF16)
        if du is not None:
            dq_ref[1] = refs[0][0]
        dkv_ref[0] += dk
        dkv_ref[1] += dv

    in_specs = [pl.BlockSpec((1, ts, GROUP), lambda s: (qg, s, 0)), pl.BlockSpec((2, N_MEM, GROUP), lambda s: (0, 0, 0)),
                pl.BlockSpec((1, ts, GROUP), lambda s: (dg, s, 0))]
    args, aliases = [u, kv, dm], {}
    dq_spec, dq_shape = pl.BlockSpec((1, ts, GROUP), lambda s: (0, s, 0)), SDS((1, S, GROUP), BF16)
    if du is not None:
        in_specs += [pl.BlockSpec((1, ts, GROUP), lambda s: (0, s, 0)), pl.BlockSpec(memory_space=pl.ANY)]
        args += [dgate, du]
        aliases = {4: 0}
        dq_spec, dq_shape = pl.BlockSpec((2, ts, GROUP), lambda s: (qg // 2, s, 0)), SDS(du.shape, BF16)
    return pl.pallas_call(
        body, name="xattn_bwd", grid=(S // ts,),
        in_specs=in_specs,
        out_specs=[dq_spec, pl.BlockSpec((2, N_MEM, GROUP), lambda s: (0, 0, 0))],
        out_shape=[dq_shape, SDS((2, N_MEM, GROUP), F32)],
        input_output_aliases=aliases,
        compiler_params=_params(("arbitrary",)),
    )(*args)


ML_BLOCK_CHUNKS = 4
H4 = ML_HEADS
L = ML_CHUNK
NLANE = ML_HEAD_DIM


def _chunk_consts():
    r = lax.broadcasted_iota(jnp.int32, (1, L, L), 1)
    c = lax.broadcasted_iota(jnp.int32, (1, L, L), 2)
    return r >= c, r <= c, r == c


def _gate_cols(gb):
    lane = lax.broadcasted_iota(jnp.int32, gb.shape, 1)
    li = jnp.stack([jnp.sum(jnp.where(lane == h, gb, 0.0), axis=1, keepdims=True) for h in range(H4)])
    gf = jnp.stack([jnp.sum(jnp.where(lane == H4 + h, gb, 0.0), axis=1, keepdims=True) for h in range(H4)])
    return li, gf


def _log_sigmoid(x):
    return jnp.minimum(x, 0.0) - jnp.log(1.0 + jnp.exp(-jnp.abs(x)))


def _chunk_forward(q, k, v_aug, li_col, lf_col, c_prev, m_prev):
    tri, tri_t, eye = _chunk_consts()
    lf_row = jnp.sum(jnp.where(eye, lf_col, 0.0), axis=1, keepdims=True)
    li_row = jnp.sum(jnp.where(eye, li_col, 0.0), axis=1, keepdims=True)
    bcum_col = jnp.sum(jnp.where(tri, lf_row, 0.0), axis=2, keepdims=True)
    bcum_row = jnp.sum(jnp.where(tri_t, lf_col, 0.0), axis=1, keepdims=True)
    log_d = jnp.where(tri, bcum_col - bcum_row + li_row, NEG)
    log_inter = bcum_col + m_prev
    m_t = jnp.maximum(log_inter, jnp.max(log_d, axis=2, keepdims=True))
    w_intra = jnp.exp(log_d - m_t)
    w_inter = jnp.exp(log_inter - m_t)
    sc = _bdot(q, k, 2, 2) * w_intra
    qc = _bdot1(q, c_prev, 2, 1)
    num = _bdot(sc, v_aug, 2, 1) + w_inter * qc
    lane = lax.broadcasted_iota(jnp.int32, num.shape, 2)
    den = jnp.sum(jnp.where(lane == NLANE, num, 0.0), axis=2, keepdims=True)
    e_m = jnp.exp(-m_t)
    b_last = jnp.sum(lf_row, axis=2, keepdims=True)
    log_w = b_last - bcum_col + li_col
    m_new = jnp.maximum(b_last + m_prev, jnp.max(log_w, axis=1, keepdims=True))
    w_k = jnp.exp(log_w - m_new)
    decay = jnp.exp(b_last + m_prev - m_new)
    return dict(w_intra=w_intra, w_inter=w_inter, sc=sc, qc=qc, num=num, den=den, e_m=e_m, lane=lane,
                w_k=w_k, decay=decay, m_new=m_new)


def mlstm_fwd(qk, u, bg):
    _, S, _ = qk.shape
    nc = S // L
    cb = min(ML_BLOCK_CHUNKS, nc)
    rows = cb * L
    kscale = ML_HEAD_DIM ** -0.5

    def body(qk_ref, v_ref, g_ref, bg_ref, h_ref, cst_ref, mst_ref, c_sc, m_sc):
        @pl.when(pl.program_id(0) == 0)
        def _():
            c_sc[...] = jnp.zeros_like(c_sc)
            m_sc[...] = jnp.zeros_like(m_sc)

        for c in range(cb):
            sl = pl.ds(c * L, L)
            q = qk_ref[0:H4, sl, :]
            k = qk_ref[H4:2 * H4, sl, :] * kscale
            v = v_ref[:, sl, :]
            lane = lax.broadcasted_iota(jnp.int32, v.shape, 2)
            v_aug = jnp.where(lane == NLANE, 1.0, v)
            li_col, gf = _gate_cols(g_ref[0, sl, :] + bg_ref[...])
            lf_col = _log_sigmoid(gf)
            c_prev = c_sc[...]
            m_prev = m_sc[...]
            f = _chunk_forward(q, k, v_aug, li_col, lf_col, c_prev, m_prev)
            r = 1.0 / jnp.maximum(jnp.abs(f["den"]), f["e_m"])
            h_ref[:, sl, :] = jnp.where(lane < NLANE, f["num"] * r, 0.0)
            cst_ref[c] = c_prev
            mst_ref[c] = jnp.broadcast_to(m_prev, (H4, 1, LANES))
            c_sc[...] = f["decay"] * c_prev + _bdot(k * f["w_k"], v_aug, 1, 1)
            m_sc[...] = f["m_new"]

    def hspec(blk):
        return pl.BlockSpec((H4, rows, GROUP), lambda i: (blk, i, 0))

    return pl.pallas_call(
        body, name="mlstm_fwd", grid=(nc // cb,),
        in_specs=[pl.BlockSpec((2 * H4, rows, GROUP), lambda i: (0, i, 0)), hspec(2),
                  pl.BlockSpec((1, rows, GROUP), lambda i: (17, i, 0)), pl.BlockSpec((1, GROUP), lambda i: (0, 0))],
        out_specs=[hspec(0), pl.BlockSpec((cb, H4, GROUP, GROUP), lambda i: (i, 0, 0, 0)),
                   pl.BlockSpec((cb, H4, 1, LANES), lambda i: (i, 0, 0, 0))],
        out_shape=[SDS((H4, S, GROUP), F32), SDS((nc, H4, GROUP, GROUP), F32), SDS((nc, H4, 1, LANES), F32)],
        scratch_shapes=[pltpu.VMEM((H4, GROUP, GROUP), F32), pltpu.VMEM((H4, 1, 1), F32)],
        compiler_params=_params(("arbitrary",)),
    )(qk, u, u, bg)


def mlstm_bwd(qk, u, bg, cst, mst, dh, du):
    _, S, _ = qk.shape
    nc = S // L
    cb = min(ML_BLOCK_CHUNKS, nc)
    rows = cb * L
    nb = nc // cb
    kscale = ML_HEAD_DIM ** -0.5

    def body(qk_ref, v_ref, g_ref, bg_ref, cst_ref, mst_ref, dh_ref, du_in_ref, dqk_ref, dv_ref, dg_ref, dbg_ref, dc_sc):
        @pl.when(pl.program_id(0) == 0)
        def _():
            dc_sc[...] = jnp.zeros_like(dc_sc)
            dbg_ref[...] = jnp.zeros_like(dbg_ref)

        tri, tri_t, eye = _chunk_consts()
        for c in reversed(range(cb)):
            sl = pl.ds(c * L, L)
            q = qk_ref[0:H4, sl, :]
            k = qk_ref[H4:2 * H4, sl, :] * kscale
            v = v_ref[:, sl, :]
            lane = lax.broadcasted_iota(jnp.int32, v.shape, 2)
            v_aug = jnp.where(lane == NLANE, 1.0, v)
            li_col, gf = _gate_cols(g_ref[0, sl, :] + bg_ref[...])
            lf_col = _log_sigmoid(gf)
            c_prev = cst_ref[c]
            m_prev = mst_ref[c][:, :, 0:1]
            f = _chunk_forward(q, k, v_aug, li_col, lf_col, c_prev, m_prev)
            w_intra, w_inter, sc, num, den, e_m = f["w_intra"], f["w_inter"], f["sc"], f["num"], f["den"], f["e_m"]
            absd = jnp.abs(den)
            r = 1.0 / jnp.maximum(absd, e_m)
            dhv = dh_ref[:, sl, :]
            s1 = jnp.sum(jnp.where(lane < NLANE, dhv * num, 0.0), axis=2, keepdims=True)
            dden = jnp.where(absd > e_m, -s1 * r * r * jnp.sign(den), 0.0)
            dnum = jnp.where(lane == NLANE, dden, jnp.where(lane < NLANE, dhv * r, 0.0))
            dsc = _bdot1(dnum, v_aug, 2, 2)
            dv = _bdot1(sc, dnum, 1, 1)
            gmat = dsc * sc
            dqk = dsc * w_intra
            dq = _bdot1(dqk, k, 2, 1) + w_inter * _bdot1(dnum, c_prev, 2, 2)
            dk = _bdot1(dqk, q, 1, 1)
            dc_prev = _bdot(q * w_inter, dnum, 1, 1)
            dlog_inter = jnp.sum(dnum * f["qc"], axis=2, keepdims=True) * w_inter
            dbcum_col = dlog_inter + jnp.sum(gmat, axis=2, keepdims=True)
            g_row = jnp.sum(gmat, axis=1, keepdims=True)
            dcn = dc_sc[...]
            w_k, decay = f["w_k"], f["decay"]
            kw = k * w_k
            dc_prev = dc_prev + decay * dcn
            db_last = jnp.sum(jnp.sum(dcn * c_prev, axis=2, keepdims=True), axis=1, keepdims=True) * decay
            dkw = _bdot(v_aug, dcn, 2, 2)
            dv = dv + _bdot1(kw, dcn, 2, 1)
            dk = dk + dkw * w_k
            dlogw = jnp.sum(dkw * k, axis=2, keepdims=True) * w_k
            db_last = db_last + jnp.sum(dlogw, axis=1, keepdims=True)
            dbcum_col = dbcum_col - dlogw
            rowi = lax.broadcasted_iota(jnp.int32, (1, L, 1), 1)
            dbcum_col = dbcum_col + jnp.where(rowi == L - 1, db_last, 0.0)
            dbcum_row = jnp.sum(jnp.where(eye, dbcum_col, 0.0), axis=1, keepdims=True) - g_row
            dlf_col = jnp.sum(jnp.where(tri_t, dbcum_row, 0.0), axis=2, keepdims=True)
            dli_col = dlogw + jnp.sum(jnp.where(eye, g_row, 0.0), axis=2, keepdims=True)
            dgf_col = dlf_col * _sigmoid(-gf)
            lane_g = lax.broadcasted_iota(jnp.int32, (L, GROUP), 1)
            dg = jnp.zeros((L, GROUP), F32)
            for h in range(H4):
                dg = dg + jnp.where(lane_g == h, dli_col[h], 0.0) + jnp.where(lane_g == H4 + h, dgf_col[h], 0.0)
            dqk_ref[0:H4, sl, :] = dq.astype(BF16)
            dqk_ref[H4:2 * H4, sl, :] = (dk * kscale).astype(BF16)
            dv_ref[:, sl, :] = jnp.where(lane < NLANE, dv, 0.0).astype(BF16)
            dg_ref[0, sl, :] = dg.astype(BF16)
            dbg_ref[...] += jnp.sum(dg, axis=0, keepdims=True)
            dc_sc[...] = dc_prev

    def hspec(blk):
        return pl.BlockSpec((H4, rows, GROUP), lambda i: (blk, nb - 1 - i, 0))

    gspec = pl.BlockSpec((1, rows, GROUP), lambda i: (17, nb - 1 - i, 0))
    qkspec = pl.BlockSpec((2 * H4, rows, GROUP), lambda i: (0, nb - 1 - i, 0))
    return pl.pallas_call(
        body, name="mlstm_bwd", grid=(nb,),
        in_specs=[qkspec, hspec(2), gspec, pl.BlockSpec((1, GROUP), lambda i: (0, 0)),
                  pl.BlockSpec((cb, H4, GROUP, GROUP), lambda i: (nb - 1 - i, 0, 0, 0)),
                  pl.BlockSpec((cb, H4, 1, LANES), lambda i: (nb - 1 - i, 0, 0, 0)), hspec(0), pl.BlockSpec(memory_space=pl.ANY)],
        out_specs=[qkspec, hspec(2), pl.BlockSpec((1, rows, GROUP), lambda i: (0, nb - 1 - i, 0)),
                   pl.BlockSpec((1, GROUP), lambda i: (0, 0))],
        input_output_aliases={7: 1},
        out_shape=[SDS((2 * H4, S, GROUP), BF16), SDS(du.shape, BF16),
                   SDS((1, S, GROUP), BF16), SDS((1, GROUP), F32)],
        scratch_shapes=[pltpu.VMEM((H4, GROUP, GROUP), F32)],
        compiler_params=_params(("arbitrary",)),
    )(qk, u, u, bg, cst, mst, dh, du)


def head_norm_fwd(hm, u, hg):
    _, S, _ = hm.shape
    ts = _tile(S, 2048)

    def body(h_ref, o_ref, g_ref, t_ref):
        h = h_ref[0]
        lane = lax.broadcasted_iota(jnp.int32, h.shape, 1)
        valid = lane < ML_HEAD_DIM
        mu = jnp.sum(h, axis=-1, keepdims=True) * (1.0 / ML_HEAD_DIM)
        hc = jnp.where(valid, h - mu, 0.0)
        var = jnp.sum(hc * hc, axis=-1, keepdims=True) * (1.0 / ML_HEAD_DIM)
        hn = hc * lax.rsqrt(var + LN_EPS) * g_ref[0]
        t_ref[0] = (_sigmoid(o_ref[0]) * hn).astype(BF16)

    return pl.pallas_call(
        body, name="head_norm_fwd", grid=(H4, S // ts),
        in_specs=[pl.BlockSpec((1, ts, GROUP), lambda h, s: (h, s, 0)), pl.BlockSpec((1, ts, GROUP), lambda h, s: (12 + h, s, 0)),
                  pl.BlockSpec((1, 1, GROUP), lambda h, s: (h, 0, 0))],
        out_specs=pl.BlockSpec((1, ts, GROUP), lambda h, s: (h, s, 0)),
        out_shape=SDS((H4 + 1, S, GROUP), BF16),
        compiler_params=_params(("parallel", "parallel")),
    )(hm, u, hg)


def head_norm_bwd(hm, u, hg, dm):
    _, S, _ = hm.shape
    ts = _tile(S, 2048)

    def body(h_ref, o_ref, g_ref, d_ref, dh_ref, do_ref, dg_ref):
        @pl.when(pl.program_id(1) == 0)
        def _():
            dg_ref[...] = jnp.zeros_like(dg_ref)

        h = h_ref[0]
        lane = lax.broadcasted_iota(jnp.int32, h.shape, 1)
        valid = lane < ML_HEAD_DIM
        inv = 1.0 / ML_HEAD_DIM
        mu = jnp.sum(h, axis=-1, keepdims=True) * inv
        hc = jnp.where(valid, h - mu, 0.0)
        var = jnp.sum(hc * hc, axis=-1, keepdims=True) * inv
        rstd = lax.rsqrt(var + LN_EPS)
        xhat = hc * rstd
        g = g_ref[0]
        sig = _sigmoid(o_ref[0])
        dt = jnp.where(valid, d_ref[0], 0.0)
        do_ref[0] = (dt * xhat * g * sig * (1.0 - sig)).astype(BF16)
        dhn = dt * sig
        dg_ref[0] += jnp.sum(dhn * xhat, axis=0, keepdims=True)
        dxh = dhn * g
        m1 = jnp.sum(dxh, axis=-1, keepdims=True) * inv
        m2 = jnp.sum(dxh * xhat, axis=-1, keepdims=True) * inv
        dh_ref[0] = jnp.where(valid, rstd * (dxh - m1 - xhat * m2), 0.0)

    spec = pl.BlockSpec((1, ts, GROUP), lambda h, s: (h, s, 0))
    gspec = pl.BlockSpec((1, 1, GROUP), lambda h, s: (h, 0, 0))
    return pl.pallas_call(
        body, name="head_norm_bwd", grid=(H4, S // ts),
        in_specs=[spec, pl.BlockSpec((1, ts, GROUP), lambda h, s: (12 + h, s, 0)), gspec, spec],
        out_specs=[spec, pl.BlockSpec((1, ts, GROUP), lambda h, s: (12 + h, s, 0)), gspec],
        out_shape=[SDS((H4, S, GROUP), F32), SDS((u.shape[0], S, GROUP), BF16), SDS((H4, 1, GROUP), F32)],
        compiler_params=_params(("parallel", "arbitrary")),
    )(hm, u, hg, dm)


def _adamw_math(w, g, m, v):
    c1 = 1.0 / (1.0 - ADAM_B1 ** ADAM_STEP)
    c2 = 1.0 / (1.0 - ADAM_B2 ** ADAM_STEP)
    nm = ADAM_B1 * m + (1.0 - ADAM_B1) * g
    nv = ADAM_B2 * v + (1.0 - ADAM_B2) * (g * g)
    return -ADAM_LR * ((nm * c1) / (jnp.sqrt(nv * c2) + ADAM_EPS) + ADAM_WD * w), nm, nv


def _row_tile(R, cap=512):
    return R if R <= cap else max(d for d in range(8, cap + 1, 8) if R % d == 0)


def adamw_into(w, m, v, g, outs, idx, after, name):
    R, C = g.shape
    tr = _row_tile(R)
    lead = (0,) * len(idx)

    def body(w_ref, m_ref, v_ref, g_ref, *rest):
        go_ref, d_ref, nm_ref, nv_ref, token = rest[-5:]
        token[...] = jnp.zeros_like(token)
        gv = g_ref[...]
        d, nm, nv = _adamw_math(w_ref[lead], gv, m_ref[lead], v_ref[lead])
        go_ref[lead] = gv
        d_ref[lead] = d
        nm_ref[lead] = nm
        nv_ref[lead] = nv

    blk = pl.BlockSpec((1,) * len(idx) + (tr, C), lambda r: idx + (r, 0))
    any_space = pl.BlockSpec(memory_space=pl.ANY)
    in_specs, args, aliases = [blk, blk, blk, pl.BlockSpec((tr, C), lambda r: (r, 0)), any_space], [w, m, v, g, g if after is None else after], {}
    if outs is not None:
        in_specs += [any_space] * 4
        args += list(outs)
        aliases = {5 + i: i for i in range(4)}
    out = pl.pallas_call(
        body, name=name, grid=(R // tr,),
        in_specs=in_specs, out_specs=[blk] * 4 + [pl.BlockSpec((8, LANES), lambda r: (0, 0))],
        out_shape=[SDS(w.shape, F32)] * 4 + [SDS((8, LANES), F32)],
        input_output_aliases=aliases, compiler_params=_params(("arbitrary",)),
    )(*args)
    return out[:4], out[4]


def adamw(w, g, m, v, name):
    R, C = w.shape
    tr = _row_tile(R)

    def body(w_ref, g_ref, m_ref, v_ref, d_ref, nm_ref, nv_ref):
        d_ref[...], nm_ref[...], nv_ref[...] = _adamw_math(w_ref[...], g_ref[...], m_ref[...], v_ref[...])

    spec = pl.BlockSpec((tr, C), lambda i: (i, 0))
    return pl.pallas_call(
        body, name=name, grid=(R // tr,),
        in_specs=[spec] * 4, out_specs=[spec] * 3,
        out_shape=[SDS((R, C), F32)] * 3,
        compiler_params=_params(("parallel",)),
    )(w, g, m, v)


HBM = pl.BlockSpec(memory_space=pl.ANY)
ROW_SPLIT = 2
PAIR_SPLIT = 1


def _position():
    x, y, c = lax.axis_index("x"), lax.axis_index("y"), lax.axis_index("c")
    return x, y, c, [(1 - x, y), (x, 1 - y), (1 - x, 1 - y)]


def _unique(items):
    arrays = []
    for a, _ in items:
        if not any(a is b for b in arrays):
            arrays.append(a)
    return arrays, [next(i for i, b in enumerate(arrays) if b is a) for a, _ in items]


def place_own(items, me, after, name):
    arrays, src_of = _unique(items)
    n = len(items)
    shapes = [a.shape[len(p):] for a, p in items]

    def body(me_ref, *refs):
        for t in range(n):
            refs[n + 1 + t][0] = refs[t][(0,) * len(items[t][1])]

    in_specs, out_specs = [], []
    for (a, p), shp in zip(items, shapes):
        blk = shp[:-2] + (shp[-2] // ROW_SPLIT, shp[-1])
        lead = (0,) * (len(shp) - 2)
        in_specs.append(pl.BlockSpec((1,) * len(p) + blk, functools.partial(lambda r, me_ref, p, lead: p + lead + (r, 0), p=p, lead=lead)))
        out_specs.append(pl.BlockSpec((1,) + blk, functools.partial(lambda r, me_ref, lead: (me_ref[0],) + lead + (r, 0), lead=lead)))
    in_specs.append(pl.BlockSpec(memory_space=pl.ANY))
    return pl.pallas_call(
        body, name=name,
        grid_spec=pltpu.PrefetchScalarGridSpec(num_scalar_prefetch=1, grid=(ROW_SPLIT,), in_specs=in_specs, out_specs=out_specs),
        out_shape=[SDS((N_CHIPS,) + tuple(shp), a.dtype) for shp, (a, _) in zip(shapes, items)],
        compiler_params=_params(("parallel",)),
    )(me, *[arrays[i] for i in src_of], after)


SEM = pl.BlockSpec(memory_space=pltpu.SEMAPHORE)
IN_HBM = pl.BlockSpec(memory_space=pltpu.HBM)
DATAFLOW = pltpu.SideEffectType.DATAFLOW_SIDE_EFFECTING


def split_start(bufs, plan, n_copies, after, name):
    n = len(bufs)

    def body(*refs):
        send, recv, token = refs[n + 1], refs[n + 2], refs[-1]
        x, y, c, chips = _position()
        for k, (src, dst, dev) in enumerate(plan(refs[:n], x, y, c, chips)):
            pltpu.make_async_remote_copy(src_ref=src, dst_ref=dst, send_sem=send.at[k], recv_sem=recv.at[k],
                                         device_id=dev, device_id_type=MESH).start()
        token[...] = jnp.zeros_like(token)

    out = pl.pallas_call(
        body, name=name,
        out_shape=(pltpu.SemaphoreType.DMA((n_copies,)), pltpu.SemaphoreType.DMA((n_copies,)),
                   *[pltpu.HBM(b.shape, b.dtype) for b in bufs], SDS((8, LANES), F32)),
        in_specs=[IN_HBM] * n + [pl.BlockSpec(memory_space=pl.ANY)],
        out_specs=(SEM, SEM, *[IN_HBM] * n, pl.BlockSpec(memory_space=pltpu.VMEM)),
        input_output_aliases={i: 2 + i for i in range(n)},
        compiler_params=pltpu.CompilerParams(has_side_effects=DATAFLOW),
    )(*[pltpu.with_memory_space_constraint(b, pltpu.HBM) for b in bufs], after)
    return out[0], out[1], list(out[2:2 + n]), out[-1]


def split_wait(send, recv, bufs, plan, after, name):
    n = len(bufs)

    def body(*refs):
        send_ref, recv_ref = refs[n], refs[n + 1]
        x, y, c, chips = _position()
        for k, (src, dst, dev) in enumerate(plan(refs[:n], x, y, c, chips)):
            cp = pltpu.make_async_remote_copy(src_ref=src, dst_ref=dst, send_sem=send_ref.at[k], recv_sem=recv_ref.at[k],
                                              device_id=dev, device_id_type=MESH)
            cp.wait_send()
            cp.wait_recv()

    return list(pl.pallas_call(
        body, name=name, out_shape=tuple(pltpu.HBM(b.shape, b.dtype) for b in bufs),
        in_specs=[IN_HBM] * n + [SEM, SEM, pl.BlockSpec(memory_space=pl.ANY)], out_specs=tuple([IN_HBM] * n),
        input_output_aliases={i: i for i in range(n)},
        compiler_params=pltpu.CompilerParams(has_side_effects=DATAFLOW),
    )(*bufs, send, recv, after))


def _gather_plan(shapes, landing):
    n = len(shapes)

    def plan(refs, x, y, c, chips):
        out = []
        for t in range(n):
            half = shapes[t][0] // 2
            rows = pl.ds(c * half, half)
            for cx, cy in chips:
                slot = 2 * cx + cy if landing else 2 * x + y
                out.append((refs[t].at[rows], refs[n + t].at[slot, rows], (cx, cy, c)))
        return out

    return plan


def gather_start(shards, placed, after, name):
    shapes = [s.shape for s in shards]
    send, recv, bufs, token = split_start(list(shards) + list(placed), _gather_plan(shapes, False), 3 * len(shards), after, name)
    return (send, recv, bufs, shapes), token


def gather_wait(state, after, name):
    send, recv, bufs, shapes = state
    return split_wait(send, recv, bufs, _gather_plan(shapes, True), after, name)[len(shapes):]


def gather_pass_on(placed, shapes, name):
    n = len(placed)

    def body(*refs):
        outs, send, recv = refs[n:2 * n], refs[2 * n], refs[2 * n + 1]
        x, y, c, chips = _position()
        cps = []
        for t in range(n):
            half = shapes[t][0] // 2
            for j, (cx, cy) in enumerate(chips):
                piece = outs[t].at[2 * cx + cy, pl.ds(c * half, half)]
                cp = pltpu.make_async_remote_copy(src_ref=piece, dst_ref=piece, send_sem=send.at[3 * t + j], recv_sem=recv.at[3 * t + j],
                                                  device_id=(x, y, 1 - c), device_id_type=MESH)
                cp.start()
                cps.append(cp)
        for t in range(n):
            half = shapes[t][0] // 2
            for j, (cx, cy) in enumerate(chips):
                piece = outs[t].at[2 * cx + cy, pl.ds((1 - c) * half, half)]
                pltpu.make_async_remote_copy(src_ref=piece, dst_ref=piece, send_sem=send.at[3 * t + j], recv_sem=recv.at[3 * t + j],
                                             device_id=(x, y, 1 - c), device_id_type=MESH).wait_recv()
        for cp in cps:
            cp.wait_send()

    return pl.pallas_call(
        body, name=name,
        in_specs=[HBM] * n, out_specs=[HBM] * n,
        out_shape=[SDS(p.shape, p.dtype) for p in placed],
        input_output_aliases={t: t for t in range(n)},
        scratch_shapes=[pltpu.SemaphoreType.DMA((3 * n,))] * 2,
    )(*placed)


def _flip(k, x, y, c):
    return ((1 - x) if k & 4 else x, (1 - y) if k & 2 else y, (1 - c) if k & 1 else c)


def small_allgather(v, reduce):
    R, C = v.shape

    def body(v_ref, o_ref, *scratch):
        if reduce:
            buf, send, recv = scratch
        else:
            buf, (send, recv) = o_ref, scratch
        x, y, c, _ = _position()
        me = 4 * x + 2 * y + c
        buf[me] = v_ref[...]
        sends = []
        for k in range(1, N_DEV):
            cp = pltpu.make_async_remote_copy(src_ref=v_ref, dst_ref=buf.at[me], send_sem=send.at[k - 1], recv_sem=recv.at[k - 1],
                                              device_id=_flip(k, x, y, c), device_id_type=MESH)
            cp.start()
            sends.append(cp)
        for k in range(1, N_DEV):
            px, py, pc = _flip(k, x, y, c)
            pltpu.make_async_remote_copy(src_ref=v_ref, dst_ref=buf.at[4 * px + 2 * py + pc], send_sem=send.at[k - 1],
                                         recv_sem=recv.at[k - 1], device_id=(px, py, pc), device_id_type=MESH).wait_recv()
        for cp in sends:
            cp.wait_send()
        if reduce:
            acc = buf[0]
            for i in range(1, N_DEV):
                acc = acc + buf[i]
            o_ref[...] = acc

    vm = pl.BlockSpec(memory_space=pltpu.VMEM)
    sems = [pltpu.SemaphoreType.DMA((N_DEV - 1,)), pltpu.SemaphoreType.DMA((N_DEV - 1,))]
    return pl.pallas_call(
        body, name="small_allreduce" if reduce else "small_allgather",
        in_specs=[vm], out_specs=vm,
        out_shape=SDS((R, C) if reduce else (N_DEV, R, C), F32),
        scratch_shapes=([pltpu.VMEM((N_DEV, R, C), F32)] if reduce else []) + sems,
    )(v)


def rs_exchange_sibling(gs):
    n = len(gs)

    def body(*refs):
        ins, outs, send, recv = refs[:n], refs[n:2 * n], refs[2 * n], refs[2 * n + 1]
        x, y, c, _ = _position()
        cps = []
        for t in range(n):
            cp = pltpu.make_async_remote_copy(src_ref=ins[t].at[:, 1 - c], dst_ref=outs[t], send_sem=send.at[t], recv_sem=recv.at[t],
                                              device_id=(x, y, 1 - c), device_id_type=MESH)
            cp.start()
            cps.append(cp)
        for cp in cps:
            cp.wait()

    return pl.pallas_call(
        body, name="rs_exchange_sibling", in_specs=[HBM] * n, out_specs=[HBM] * n,
        out_shape=[SDS((g.shape[0],) + g.shape[2:], g.dtype) for g in gs],
        scratch_shapes=[pltpu.SemaphoreType.DMA((n,)), pltpu.SemaphoreType.DMA((n,))],
    )(*gs)


def rs_pair_add(gs, rs, c):
    n = len(gs)

    def body(c_ref, *refs):
        for t in range(n):
            refs[2 * n + t][0] = (refs[t][0, 0].astype(F32) + refs[n + t][0].astype(F32)).astype(BF16)

    in_specs, out_specs, out_shape = [], [], []
    for g in gs:
        _, _, h, C = g.shape
        in_specs.append(pl.BlockSpec((1, 1, h // PAIR_SPLIT, C), lambda j, r, c_ref: (j, c_ref[0], r, 0)))
    for g in gs:
        _, _, h, C = g.shape
        spec = pl.BlockSpec((1, h // PAIR_SPLIT, C), lambda j, r, c_ref: (j, r, 0))
        in_specs.append(spec)
        out_specs.append(spec)
        out_shape.append(SDS((N_CHIPS, h, C), BF16))
    return pl.pallas_call(
        body, name="rs_pair_add",
        grid_spec=pltpu.PrefetchScalarGridSpec(num_scalar_prefetch=1, grid=(N_CHIPS, PAIR_SPLIT), in_specs=in_specs, out_specs=out_specs),
        out_shape=out_shape, compiler_params=_params(("parallel", "parallel")),
    )(c, *gs, *rs)


def _rs_plan(n):
    def plan(refs, x, y, c, chips):
        return [(refs[t].at[2 * cx + cy], refs[n + t].at[j], (cx, cy, c)) for t in range(n) for j, (cx, cy) in enumerate(chips)]

    return plan


def rs_chip_add(ps, qs, me_c):
    n = len(ps)

    def body(me_ref, *refs):
        for t in range(n):
            q = refs[n + t]
            refs[2 * n + t][0] = ((refs[t][0].astype(F32) + q[0].astype(F32)) + q[1].astype(F32)) + q[2].astype(F32)

    in_specs, out_specs, out_shape = [], [], []
    for p in ps:
        _, h, C = p.shape
        in_specs.append(pl.BlockSpec((1, h // ROW_SPLIT, C), lambda r, me_ref: (me_ref[0], r, 0)))
    for p in ps:
        _, h, C = p.shape
        in_specs.append(pl.BlockSpec((3, h // ROW_SPLIT, C), lambda r, me_ref: (0, r, 0)))
        out_specs.append(pl.BlockSpec((1, h // ROW_SPLIT, C), lambda r, me_ref: (me_ref[1], r, 0)))
        out_shape.append(SDS((2, h, C), F32))
    return pl.pallas_call(
        body, name="rs_chip_add",
        grid_spec=pltpu.PrefetchScalarGridSpec(num_scalar_prefetch=1, grid=(ROW_SPLIT,), in_specs=in_specs, out_specs=out_specs),
        out_shape=out_shape, compiler_params=_params(("parallel",)),
    )(me_c, *ps, *qs)


def rs_share(rs):
    n = len(rs)

    def body(*refs):
        outs, send, recv = refs[n:2 * n], refs[2 * n], refs[2 * n + 1]
        x, y, c, _ = _position()
        cps = []
        for t in range(n):
            cp = pltpu.make_async_remote_copy(src_ref=outs[t].at[c], dst_ref=outs[t].at[c], send_sem=send.at[t], recv_sem=recv.at[t],
                                              device_id=(x, y, 1 - c), device_id_type=MESH)
            cp.start()
            cps.append(cp)
        for cp in cps:
            cp.wait()

    return pl.pallas_call(
        body, name="rs_share", in_specs=[HBM] * n, out_specs=[HBM] * n,
        out_shape=[SDS(r.shape, r.dtype) for r in rs],
        input_output_aliases={t: t for t in range(n)},
        scratch_shapes=[pltpu.SemaphoreType.DMA((n,))] * 2,
    )(*rs)


def rs_begin(gs, after, name):
    c = lax.axis_index("c")
    n = len(gs)
    g5 = [g.reshape(N_CHIPS, 2, g.shape[1] // 2, g.shape[2]) for g in gs]
    from_sibling = rs_exchange_sibling(g5)
    pair = rs_pair_add(g5, from_sibling, jnp.reshape(c, (1,)).astype(jnp.int32))
    lands = [lax.empty((3,) + p.shape[1:], p.dtype) for p in pair]
    send, recv, bufs, token = split_start(list(pair) + lands, _rs_plan(n), 3 * n, from_sibling[0] if after is None else after, name)
    return (send, recv, bufs, [g.shape for g in gs]), token


def rs_end(state, after, name):
    x, y, c = lax.axis_index("x"), lax.axis_index("y"), lax.axis_index("c")
    send, recv, bufs, shapes = state
    n = len(shapes)
    bufs = split_wait(send, recv, bufs, _rs_plan(n), after, name)
    half = rs_chip_add(bufs[:n], bufs[n:], jnp.stack([2 * x + y, c]).astype(jnp.int32))
    both = rs_share(half)
    return [b.reshape(s[1], s[2]) for b, s in zip(both, shapes)]


def _pad_last(a, n):
    return jnp.pad(a, [(0, 0)] * (a.ndim - 1) + [(0, n - a.shape[-1])])


def _heads_to_groups(w):
    k = w.shape[0]
    return _pad_last(w.reshape(k, ML_HEADS, ML_HEAD_DIM).transpose(1, 0, 2), GROUP)


def _groups_to_heads(g):
    return g[:, :, :ML_HEAD_DIM].transpose(1, 0, 2).reshape(g.shape[1], D_TOK)


def _cols_to_groups(w):
    k, n = w.shape
    return w.reshape(k, n // GROUP, GROUP).transpose(1, 0, 2)


def _groups_to_cols(g):
    n, k, _ = g.shape
    return g.transpose(1, 0, 2).reshape(k, n * GROUP)


def _chips_to_cols(a):
    return a.transpose(1, 0, 2).reshape(a.shape[1], -1)


def _cols_to_chips(w):
    k, n = w.shape
    return w.reshape(k, N_CHIPS, n // N_CHIPS).transpose(1, 0, 2)


def _mlstm_in_groups(w):
    parts = [_heads_to_groups(w[:, i * D_TOK:(i + 1) * D_TOK]) for i in range(4)]
    gates = _pad_last(w[:, 4 * D_TOK:4 * D_TOK + 2 * ML_HEADS], GROUP)[None]
    qmem = w[:, 4 * D_TOK + 2 * ML_HEADS:][None]
    return jnp.concatenate(parts + [qmem, gates], axis=0)


def _mlstm_in_ungroup(g):
    parts = [_groups_to_heads(g[4 * i:4 * i + 4]) for i in range(4)]
    return jnp.concatenate(parts + [g[17][:, :2 * ML_HEADS], g[16]], axis=1)


def _taps_to_groups(w, width):
    taps = w.shape[0]
    g = _pad_last(w.reshape(taps, -1, width), GROUP).transpose(1, 0, 2)
    return jnp.pad(g, ((0, 0), (0, 8 - taps), (0, 0)))


def _groups_to_taps(g, taps, width):
    return g[:, :taps, :width].transpose(1, 0, 2).reshape(taps, -1)


SMALL_IN_COLS = 384
SMALL_OUT_COLS = 1536
SECTION = 8


class _Gathered:
    def __init__(self, make_src, groups, me, after):
        self.groups, self.states, self.ready = groups, [], {}
        self.group_of = {k: gi for gi, g in enumerate(groups) for k in g}
        token, self.first = after, None
        for gi, g in enumerate(groups):
            srcs = [make_src(k, None if gi == 0 else token[0:1, 0:1]) for k in g]
            placed = place_own([(a, ()) for a in srcs], me, token, f"place_own_{gi}")
            state, token = gather_start(srcs, placed, token, f"gather_start_{gi}")
            self.states.append(state)
            if gi == 0:
                self.first = token[0:1, 0:1]
        self.started = token

    def _get(self, key, after):
        gi = self.group_of[key]
        if gi not in self.ready:
            got = gather_wait(self.states[gi], after if gi else self.started, f"gather_wait_{gi}")
            self.ready[gi] = dict(zip(self.groups[gi], gather_pass_on(got, self.states[gi][3], f"gather_pass_on_{gi}")))
        return self.ready[gi][key]

    def ffn(self, l, i, after):
        return tuple(self._get((n, l, i), after) for n in ("wg", "wu", "wd"))

    def mixer(self, l, after):
        win = _chips_to_cols(self._get(("win", l), after))
        win = _cols_to_groups(win) if l % 2 == 0 else _mlstm_in_groups(win)
        wkv = _cols_to_groups(self._get(("wkv", l), after).reshape(D_MODEL, 2 * D_XA))
        wout = self._get(("wout", l), after)
        if l % 2:
            wout = wout.reshape(D_MODEL, D_MODEL)
            tok = jnp.pad(wout[:D_TOK].reshape(ML_HEADS, ML_HEAD_DIM, D_MODEL), ((0, 0), (0, GROUP - ML_HEAD_DIM), (0, 0)))
            wout = jnp.concatenate([tok, wout[D_TOK:][None]], axis=0)
        return win, wkv, wout


class _GradSink:
    def __init__(self, apply):
        self.queue, self.apply, self.count, self.done = [], apply, 0, None

    @staticmethod
    def _by_chip(key, g):
        if key[0] == "wkv":
            return _groups_to_cols(g).reshape(N_CHIPS, D_MODEL // N_CHIPS, 2 * D_XA)
        if key[0] == "win":
            return _cols_to_chips(_groups_to_cols(g) if key[1] % 2 == 0 else _mlstm_in_ungroup(g))
        if key[0] == "wout" and key[1] % 2:
            full = jnp.concatenate([g[:ML_HEADS, :ML_HEAD_DIM].reshape(D_TOK, D_MODEL), g[ML_HEADS]], axis=0)
            return full.reshape(N_CHIPS, D_MODEL // N_CHIPS, D_MODEL)
        return g

    def push(self, grads):
        keys = list(grads)
        state, token = rs_begin([self._by_chip(k, grads[k]) for k in keys], self.done, f"rs_start_{self.count}")
        if self.queue:
            self._finish(token)
        self.queue.append((keys, state, self.count))
        self.count += 1
        return token

    def flush(self):
        self._finish(self.done)

    def _finish(self, after):
        keys, state, i = self.queue.pop(0)
        for key, g in zip(keys, rs_end(state, after, f"rs_wait_{i}")):
            self.done = self.apply(key, g, self.done)


def _local_step(x, mem, tgt, P, weights, sink):
    memb = mem.astype(BF16)
    saved = []
    pin0 = getattr(weights, "first", None)
    X, Xb = x, (x if pin0 is None else x + pin0).astype(BF16)
    after = Xb
    for l in range(DEPTH):
        s = {}
        s["x0b"] = Xb
        s["wa"] = weights.ffn(l, 0, after)
        s["g1a"], s["u1a"], s["ha"], s["z1"], X1, X1b = ffn_fwd(Xb, X, *s["wa"], P["ln_g"][l][0], P["ln_b"][l][0])
        s["x1b"] = X1b
        s["wm"] = win, wkv, wout = weights.mixer(l, X1b)
        u = proj(X1b, win, "mixer_in")
        kv = proj(memb, wkv, "mem_kv")
        s["u"], s["kv"] = u, kv
        if l % 2 == 0:
            tok = conv_mixer_fwd(u, P["convw"])
            qg = 9
        else:
            s["qk"] = qk_conv_fwd(u, P["qkw"])
            s["hm"], s["cst"], s["mst"] = mlstm_fwd(s["qk"], u, P["bg"])
            tok = head_norm_fwd(s["hm"], u, P["hg"])
            qg = 16
        s["m"] = xattn_fwd(u, qg, kv, tok)
        s["z2"], X2, X2b = contract_ln(s["m"], wout, X1, P["ln_g"][l][1], P["ln_b"][l][1], 1.0, "mixer_out_ln")
        s["x2b"] = X2b
        s["wb"] = weights.ffn(l, 1, X2b)
        s["g1b"], s["u1b"], s["hb"], s["z3"], X, Xb = ffn_fwd(X2b, X2, *s["wb"], P["ln_g"][l][2], P["ln_b"][l][2])
        after = Xb
        saved.append(s)

    loss, dX = loss_grad(X, tgt)

    G = {"ln_g": [[None] * 3 for _ in range(DEPTH)], "ln_b": [[None] * 3 for _ in range(DEPTH)]}
    pin = [jnp.zeros((1, 1), F32)]

    def ffn_backward(l, i, dX, z, xinb, g1, u1, h, w):
        k = 2 * i
        dgb, dub, dx, dyb, G["ln_g"][l][k], G["ln_b"][l][k] = ffn_bwd(dX, z, P["ln_g"][l][k] + pin[0], w[2], w[0], w[1], g1, u1)
        grads = {("wd", l, i): wgrad(h, dyb, BF16, "wgrad_down"), ("wg", l, i): wgrad(dgb, xinb, BF16, "wgrad_gate"),
                 ("wu", l, i): wgrad(dub, xinb, BF16, "wgrad_up")}
        return dx, grads

    for l in reversed(range(DEPTH)):
        s = saved[l]
        win, wkv, wout = s["wm"]
        dX, grads = ffn_backward(l, 1, dX, s["z3"], s["x2b"], s["g1b"], s["u1b"], s["hb"], s["wb"])
        dm, dz2, dz2b, G["ln_g"][l][1], G["ln_b"][l][1] = mixer_out_bwd(dX, s["z2"], P["ln_g"][l][1], wout)
        grads[("wout", l)] = wgrad(s["m"], dz2b, BF16, "wgrad_out")
        u, kv = s["u"], s["kv"]
        if l % 2 == 0:
            db, dc, dxi, G["convw"] = conv_mixer_bwd(u, P["convw"], dm)
            dq, dkv = xattn_bwd(u, 9, kv, dm, 3)
            du = jnp.concatenate([db, dc, dxi, dq], axis=0)
        else:
            dh, du, G["hg"] = head_norm_bwd(s["hm"], u, P["hg"], dm)
            dqk, du, dgate, G["bg"] = mlstm_bwd(s["qk"], u, P["bg"], s["cst"], s["mst"], dh, du)
            du, G["qkw"] = qk_conv_bwd(u, P["qkw"], dqk, du)
            du, dkv = xattn_bwd(u, 16, kv, dm, 4, du, dgate)
        grads[("win", l)] = wgrad(s["x1b"], du, BF16, "wgrad_in")
        grads[("wkv", l)] = wgrad(memb, dkv.astype(BF16), BF16, "wgrad_kv")
        dX = contract_t(du, win, dz2, "mixer_in_bwd")
        pin[0] = sink.push(grads)[0:1, 0:1]
        dX, grads = ffn_backward(l, 0, dX, s["z1"], s["x0b"], s["g1a"], s["u1a"], s["ha"], s["wa"])
        pin[0] = sink.push(grads)[0:1, 0:1]
    sink.flush()
    return loss, dX, G


def kernel(x, mem, ln_g, ln_b, ffn_w_gate, ffn_w_up, ffn_w_down, w_kv_mem, w_out, w_in_conv, conv_w, w_in_mlstm, b_gates, qk_conv_w, head_norm_g, loss_target, m_ln_g, m_ln_b, m_ffn_w_gate, m_ffn_w_up, m_ffn_w_down, m_w_kv_mem, m_w_out, m_w_in_conv, m_conv_w, m_w_in_mlstm, m_b_gates, m_qk_conv_w, m_head_norm_g, v_ln_g, v_ln_b, v_ffn_w_gate, v_ffn_w_up, v_ffn_w_down, v_w_kv_mem, v_w_out, v_w_in_conv, v_conv_w, v_w_in_mlstm, v_b_gates, v_qk_conv_w, v_head_norm_g):
    cx, cy = lax.axis_index("x"), lax.axis_index("y")
    chip = 2 * cx + cy

    def make_src(key, pin):
        if key[0] in ("wg", "wu"):
            w = jnp.swapaxes((ffn_w_gate if key[0] == "wg" else ffn_w_up)[key[1], key[2]], 0, 1)
        elif key[0] == "wd":
            w = ffn_w_down[key[1], key[2]]
        elif key[0] == "win":
            w = (w_in_conv, w_in_mlstm)[key[1]][0]
        else:
            w = (w_kv_mem if key[0] == "wkv" else w_out)[key[1]]
        return (w if pin is None else w + pin).astype(BF16)

    ffn_keys = lambda l, i: [("wg", l, i), ("wu", l, i), ("wd", l, i)]
    mixer_keys = lambda l: [("win", l), ("wkv", l), ("wout", l)]
    groups = [ffn_keys(0, 0), mixer_keys(0) + mixer_keys(1), ffn_keys(0, 1), ffn_keys(1, 0), ffn_keys(1, 1)]
    def section(a, width):
        a = a.reshape(-1, a.shape[-1])
        return jnp.pad(a, ((0, SECTION - a.shape[0]), (0, width - a.shape[1])))

    small = jnp.concatenate([section(a, SMALL_IN_COLS) for a in (ln_g, ln_b, conv_w, qk_conv_w)], axis=0)
    smalls = small_allgather(small, reduce=False)
    gathered = _Gathered(make_src, groups, jnp.reshape(chip, (1,)).astype(jnp.int32), smalls)
    smalls = smalls[0::2]
    ln_g_full = _chips_to_cols(smalls[:, 0:6, 0:256]).reshape(DEPTH, 3, 1, D_MODEL)
    ln_b_full = _chips_to_cols(smalls[:, 8:14, 0:256]).reshape(DEPTH, 3, 1, D_MODEL)
    conv_w_full = _chips_to_cols(smalls[:, 16:19, 0:192])
    qk_w_full = _chips_to_cols(smalls[:, 24:28, 0:384])

    P = {"ln_g": ln_g_full, "ln_b": ln_b_full, "convw": _taps_to_groups(conv_w_full, GROUP),
         "qkw": _taps_to_groups(qk_w_full, ML_HEAD_DIM), "bg": _pad_last(b_gates, GROUP),
         "hg": _pad_last(head_norm_g[0], GROUP)[:, None, :]}

    weights = {"ln_g": ln_g, "ln_b": ln_b, "ffn_w_gate": ffn_w_gate, "ffn_w_up": ffn_w_up, "ffn_w_down": ffn_w_down,
               "w_kv_mem": w_kv_mem, "w_out": w_out, "w_in_conv": w_in_conv, "conv_w": conv_w, "w_in_mlstm": w_in_mlstm,
               "b_gates": b_gates, "qk_conv_w": qk_conv_w, "head_norm_g": head_norm_g}
    ms = {"ln_g": m_ln_g, "ln_b": m_ln_b, "ffn_w_gate": m_ffn_w_gate, "ffn_w_up": m_ffn_w_up, "ffn_w_down": m_ffn_w_down,
          "w_kv_mem": m_w_kv_mem, "w_out": m_w_out, "w_in_conv": m_w_in_conv, "conv_w": m_conv_w, "w_in_mlstm": m_w_in_mlstm,
          "b_gates": m_b_gates, "qk_conv_w": m_qk_conv_w, "head_norm_g": m_head_norm_g}
    vs = {"ln_g": v_ln_g, "ln_b": v_ln_b, "ffn_w_gate": v_ffn_w_gate, "ffn_w_up": v_ffn_w_up, "ffn_w_down": v_ffn_w_down,
          "w_kv_mem": v_w_kv_mem, "w_out": v_w_out, "w_in_conv": v_w_in_conv, "conv_w": v_conv_w, "w_in_mlstm": v_w_in_mlstm,
          "b_gates": v_b_gates, "qk_conv_w": v_qk_conv_w, "head_norm_g": v_head_norm_g}
    names = list(weights)
    owner = {"wg": ("ffn_w_gate", True), "wu": ("ffn_w_up", True), "wd": ("ffn_w_down", False), "wkv": ("w_kv_mem", False),
             "wout": ("w_out", False), "win": None}
    updated = {}

    def apply(key, g, after):
        name, transposed = owner[key[0]] or (("w_in_conv", "w_in_mlstm")[key[1]], False)
        idx = (0,) if key[0] == "win" else tuple(key[1:])
        view = (lambda a: jnp.swapaxes(a, -1, -2)) if transposed else (lambda a: a)
        updated[name], token = adamw_into(view(weights[name]), view(ms[name]), view(vs[name]), g, updated.get(name), idx, after,
                                          "adamw_" + name + "_" + "_".join(map(str, idx)))
        return token

    sink = _GradSink(apply)
    loss, grad_x, G = _local_step(x[0], mem[0], loss_target[0], P, gathered, sink)

    dln_g = jnp.concatenate([G["ln_g"][l][k] for l in range(DEPTH) for k in range(3)], axis=0)
    dln_b = jnp.concatenate([G["ln_b"][l][k] for l in range(DEPTH) for k in range(3)], axis=0)
    lane = lax.broadcasted_iota(jnp.int32, (1, GROUP), 1)
    misc = jnp.where(lane < 8, G["bg"], 0.0) + jnp.where(lane == 8, loss, 0.0) + sink.done[0:1, 0:1]
    parts = (dln_g, dln_b, _groups_to_taps(G["convw"], 3, GROUP), misc, _groups_to_taps(G["qkw"], 4, ML_HEAD_DIM),
             G["hg"][:, 0, :ML_HEAD_DIM])
    tot = small_allgather(jnp.concatenate([section(a, SMALL_OUT_COLS) for a in parts], axis=0), reduce=True)
    loss_total = tot[24, 8]

    small_grads = {
        "ln_g": lax.dynamic_slice(tot[0:6, 0:D_MODEL], (0, chip * 256), (6, 256)).reshape(DEPTH, 3, 256),
        "ln_b": lax.dynamic_slice(tot[8:14, 0:D_MODEL], (0, chip * 256), (6, 256)).reshape(DEPTH, 3, 256),
        "conv_w": lax.dynamic_slice(tot[16:19, 0:D_TOK], (0, chip * 192), (3, 192))[None],
        "b_gates": tot[24:25, 0:8],
        "qk_conv_w": lax.dynamic_slice(tot[32:36, 0:2 * D_TOK], (0, chip * 384), (4, 384))[None],
        "head_norm_g": tot[40:44, 0:ML_HEAD_DIM][None],
    }
    grads, deltas, new_m, new_v = [], [], [], []
    for nme in names:
        if nme in updated:
            back = (lambda a: jnp.swapaxes(a, -1, -2)) if nme in ("ffn_w_gate", "ffn_w_up") else (lambda a: a)
            g, d, nm, nv = (back(a) for a in updated[nme])
        else:
            w, g = weights[nme], small_grads[nme]
            two = (math.prod(w.shape[:-1]), w.shape[-1])
            d, nm, nv = (a.reshape(w.shape) for a in adamw(w.reshape(two), g.reshape(two), ms[nme].reshape(two),
                                                           vs[nme].reshape(two), "adamw_" + nme))
        grads.append(g)
        deltas.append(d)
        new_m.append(nm)
        new_v.append(nv)
    return (loss_total, grad_x[None], *grads, *deltas, *new_m, *new_v)
```

```python
import functools
import math

import jax
import jax.numpy as jnp
from jax import lax
from jax.experimental import pallas as pl
from jax.experimental.pallas import tpu as pltpu

F32 = jnp.float32
BF16 = jnp.bfloat16
SDS = jax.ShapeDtypeStruct

D_MODEL = 1024
DEPTH = 2
N_MEM = 256
XA_HEADS = 4
XA_HEAD_DIM = 64
D_XA = 256
D_TOK = 768
ML_HEADS = 4
ML_HEAD_DIM = 192
ML_CHUNK = 64
D_FF = 2816
LN_EPS = 1e-5
ALPHA = (2.0 * DEPTH) ** 0.25
N_CHIPS = 4
N_DEV = 8
FF_SHARD = D_FF // N_CHIPS
GROUP = 256
NEG = -1e30

ADAM_LR = 0.001
ADAM_B1 = 0.9
ADAM_B2 = 0.999
ADAM_EPS = 1e-08
ADAM_WD = 0.01
ADAM_STEP = 10

VMEM_LIMIT = 56 * 1024 * 1024

NN = ((1,), (0,))
NT = ((1,), (1,))
TN = ((0,), (0,))
MESH = pl.DeviceIdType.MESH


def _dot(a, b, dims):
    return lax.dot_general(a, b, (dims, ((), ())), preferred_element_type=F32)


def _bdot(a, b, ca, cb):
    dims = (((ca,), (cb,)), ((0,), (0,)))
    ah, bh = a.astype(BF16), b.astype(BF16)
    al, bl = (a - ah.astype(F32)).astype(BF16), (b - bh.astype(F32)).astype(BF16)
    dot = functools.partial(lax.dot_general, dimension_numbers=dims, preferred_element_type=F32)
    return dot(ah, bh) + dot(al, bh) + dot(ah, bl)


def _bdot1(a, b, ca, cb):
    return lax.dot_general(a.astype(BF16), b.astype(BF16), (((ca,), (cb,)), ((0,), (0,))), preferred_element_type=F32)


def _sigmoid(x):
    return 1.0 / (1.0 + jnp.exp(-x))


def _params(sem, vmem=VMEM_LIMIT):
    return pltpu.CompilerParams(dimension_semantics=sem, vmem_limit_bytes=vmem)


def _tile(n, want):
    t = min(n, want)
    assert n % t == 0, (n, t)
    return t


def _layer_norm(z, gamma, beta):
    mu = jnp.mean(z, axis=-1, keepdims=True)
    zc = z - mu
    var = jnp.mean(zc * zc, axis=-1, keepdims=True)
    return zc * lax.rsqrt(var + LN_EPS) * gamma + beta


def _column_halves(n):
    mid = -(-n // (2 * 128)) * 128
    return ((0, mid), (mid, n))


def _resident(shape):
    return pl.BlockSpec(shape, lambda *_: (0,) * len(shape), pipeline_mode=pl.Buffered(1))


def _group_block(G, want):
    return max(d for d in range(1, max(1, min(G, want)) + 1) if G % d == 0)


def ffn_fwd(xb, x, wg, wu, wd, gamma, beta):
    S, K = xb.shape
    G, N, _ = wg.shape
    ts = _tile(S, 512)

    def body(xb_ref, x_ref, wg_ref, wu_ref, wd_ref, gm_ref, bt_ref, g_ref, u_ref, h_ref, z_ref, xn_ref, xnb_ref):
        j = pl.program_id(1)
        xv = xb_ref[...]
        g = _dot(xv, wg_ref[j], NT)
        u = _dot(xv, wu_ref[j], NT)
        h = (g * _sigmoid(g) * u).astype(BF16)
        g_ref[0] = g.astype(BF16)
        u_ref[0] = u.astype(BF16)
        h_ref[0] = h
        y = _dot(h, wd_ref[j], NN)

        @pl.when(j == 0)
        def _():
            z_ref[...] = y

        @pl.when(j > 0)
        def _():
            z_ref[...] += y

        @pl.when(j == G - 1)
        def _():
            z = ALPHA * x_ref[...] + 0.5 * z_ref[...]
            xn = _layer_norm(z, gm_ref[...], bt_ref[...])
            z_ref[...] = z
            xn_ref[...] = xn
            xnb_ref[...] = xn.astype(BF16)

    row = pl.BlockSpec((ts, K), lambda s, j: (s, 0))
    vec = pl.BlockSpec((1, K), lambda s, j: (0, 0))
    wspec = _resident((G, N, K))
    ospec = pl.BlockSpec((1, ts, N), lambda s, j: (j, s, 0))
    return pl.pallas_call(
        body, name="ffn_fwd", grid=(S // ts, G),
        in_specs=[row, row, wspec, wspec, wspec, vec, vec],
        out_specs=[ospec, ospec, ospec, row, row, row],
        out_shape=[SDS((G, S, N), BF16), SDS((G, S, N), BF16), SDS((G, S, N), BF16),
                   SDS((S, K), F32), SDS((S, K), F32), SDS((S, K), BF16)],
        compiler_params=_params(("parallel", "arbitrary")),
    )(xb, x, wg, wu, wd, gamma, beta)


def proj(xb, w, name):
    S, K = xb.shape
    G, _, N = w.shape
    ts = _tile(S, 1024)
    gb = _group_block(G, 6)

    def body(x_ref, w_ref, y_ref):
        xv = x_ref[...]
        for j in range(gb):
            y_ref[j] = _dot(xv, w_ref[j], NN)

    return pl.pallas_call(
        body, name=name, grid=(S // ts, G // gb),
        in_specs=[pl.BlockSpec((ts, K), lambda s, g: (s, 0)), pl.BlockSpec((gb, K, N), lambda s, g: (g, 0, 0))],
        out_specs=pl.BlockSpec((gb, ts, N), lambda s, g: (g, s, 0)),
        out_shape=SDS((G, S, N), F32),
        compiler_params=_params(("parallel", "parallel")),
    )(xb, w)


def contract_ln(a, w, xres, gamma, beta, scale, name):
    G, S, Kg = a.shape
    N = w.shape[2]
    ts = _tile(S, 1024)

    def body(a_ref, w_ref, x_ref, g_ref, b_ref, z_ref, xn_ref, xb_ref):
        acc = _dot(a_ref[0], w_ref[0], NN)
        for j in range(1, G):
            acc = acc + _dot(a_ref[j], w_ref[j], NN)
        z = ALPHA * x_ref[...] + scale * acc
        xn = _layer_norm(z, g_ref[...], b_ref[...])
        z_ref[...] = z
        xn_ref[...] = xn
        xb_ref[...] = xn.astype(BF16)

    row = pl.BlockSpec((ts, N), lambda s: (s, 0))
    vec = pl.BlockSpec((1, N), lambda s: (0, 0))
    return pl.pallas_call(
        body, name=name, grid=(S // ts,),
        in_specs=[pl.BlockSpec((G, ts, Kg), lambda s: (0, s, 0)), pl.BlockSpec((G, Kg, N), lambda s: (0, 0, 0)), row, vec, vec],
        out_specs=[row, row, row],
        out_shape=[SDS((S, N), F32), SDS((S, N), F32), SDS((S, N), BF16)],
        compiler_params=_params(("parallel",)),
    )(a, w, xres, gamma, beta)


def _layer_norm_bwd(dx, z, gamma):
    mu = jnp.mean(z, axis=-1, keepdims=True)
    zc = z - mu
    var = jnp.mean(zc * zc, axis=-1, keepdims=True)
    rstd = lax.rsqrt(var + LN_EPS)
    xhat = zc * rstd
    dxh = dx * gamma
    m1 = jnp.mean(dxh, axis=-1, keepdims=True)
    m2 = jnp.mean(dxh * xhat, axis=-1, keepdims=True)
    return rstd * (dxh - m1 - xhat * m2), jnp.sum(dx * xhat, axis=0, keepdims=True), jnp.sum(dx, axis=0, keepdims=True)


def ffn_bwd(dxn, z, gamma, wd, wg, wu, g1, u1):
    S, K = dxn.shape
    G, N, _ = wd.shape
    ts = _tile(S, 512)

    def body(dxn_ref, z_ref, gm_ref, wd_ref, wg_ref, wu_ref, g_ref, u_ref, dg_ref, du_ref, dx_ref, dy_ref, dgm_ref, dbt_ref):
        s, j = pl.program_id(0), pl.program_id(1)

        @pl.when((s == 0) & (j == 0))
        def _():
            dgm_ref[...] = jnp.zeros_like(dgm_ref)
            dbt_ref[...] = jnp.zeros_like(dbt_ref)

        @pl.when(j == 0)
        def _():
            dz, dgm, dbt = _layer_norm_bwd(dxn_ref[...], z_ref[...], gm_ref[...])
            dgm_ref[...] += dgm
            dbt_ref[...] += dbt
            dx_ref[...] = ALPHA * dz
            dy_ref[...] = (0.5 * dz).astype(BF16)

        dy = dy_ref[...]
        part = None
        for a, b in _column_halves(N):
            dh = _dot(dy, wd_ref[j, a:b, :], NT)
            g = g_ref[0, :, a:b].astype(F32)
            sig = _sigmoid(g)
            dg = (dh * u_ref[0, :, a:b].astype(F32) * (sig * (1.0 + g * (1.0 - sig)))).astype(BF16)
            du = (dh * (g * sig)).astype(BF16)
            dg_ref[0, :, a:b] = dg
            du_ref[0, :, a:b] = du
            p = _dot(dg, wg_ref[j, a:b, :], NN) + _dot(du, wu_ref[j, a:b, :], NN)
            part = p if part is None else part + p
        dx_ref[...] += part

    row = pl.BlockSpec((ts, K), lambda s, j: (s, 0))
    vec = pl.BlockSpec((1, K), lambda s, j: (0, 0))
    gspec = pl.BlockSpec((1, ts, N), lambda s, j: (j, s, 0))
    wspec = _resident((G, N, K))
    return pl.pallas_call(
        body, name="ffn_bwd", grid=(S // ts, G),
        in_specs=[row, row, vec, wspec, wspec, wspec, gspec, gspec],
        out_specs=[gspec, gspec, row, row, vec, vec],
        out_shape=[SDS((G, S, N), BF16), SDS((G, S, N), BF16), SDS((S, K), F32), SDS((S, K), BF16),
                   SDS((1, K), F32), SDS((1, K), F32)],
        compiler_params=_params(("arbitrary", "arbitrary")),
    )(dxn, z, gamma, wd, wg, wu, g1, u1)


def mixer_out_bwd(dxn, z, gamma, w):
    S, N = dxn.shape
    G, Kg, _ = w.shape
    ts = _tile(S, 512)

    def body(dxn_ref, z_ref, gm_ref, w_ref, dm_ref, dz_ref, dzb_ref, dgm_ref, dbt_ref):
        @pl.when(pl.program_id(0) == 0)
        def _():
            dgm_ref[...] = jnp.zeros_like(dgm_ref)
            dbt_ref[...] = jnp.zeros_like(dbt_ref)

        dz, dgm, dbt = _layer_norm_bwd(dxn_ref[...], z_ref[...], gm_ref[...])
        dgm_ref[...] += dgm
        dbt_ref[...] += dbt
        dzb = dz.astype(BF16)
        dz_ref[...] = dz
        dzb_ref[...] = dzb
        for j in range(G):
            dm_ref[j] = _dot(dzb, w_ref[j], NT)

    row = pl.BlockSpec((ts, N), lambda s: (s, 0))
    vec = pl.BlockSpec((1, N), lambda s: (0, 0))
    return pl.pallas_call(
        body, name="mixer_out_bwd", grid=(S // ts,),
        in_specs=[row, row, vec, pl.BlockSpec((G, Kg, N), lambda s: (0, 0, 0))],
        out_specs=[pl.BlockSpec((G, ts, Kg), lambda s: (0, s, 0)), row, row, vec, vec],
        out_shape=[SDS((G, S, Kg), F32), SDS((S, N), F32), SDS((S, N), BF16), SDS((1, N), F32), SDS((1, N), F32)],
        compiler_params=_params(("arbitrary",)),
    )(dxn, z, gamma, w)


def contract_t(da, w, res, name):
    G, S, Ng = da.shape
    K = w.shape[1]
    ts = _tile(S, 1024)
    gb = _group_block(G, 6)

    def body(da_ref, w_ref, r_ref, o_ref):
        g = pl.program_id(1)
        part = _dot(da_ref[0], w_ref[0], NT)
        for j in range(1, gb):
            part = part + _dot(da_ref[j], w_ref[j], NT)

        @pl.when(g == 0)
        def _():
            o_ref[...] = ALPHA * r_ref[...] + part

        @pl.when(g > 0)
        def _():
            o_ref[...] += part

    row = pl.BlockSpec((ts, K), lambda s, g: (s, 0))
    return pl.pallas_call(
        body, name=name, grid=(S // ts, G // gb),
        in_specs=[pl.BlockSpec((gb, ts, Ng), lambda s, g: (g, s, 0)), pl.BlockSpec((gb, K, Ng), lambda s, g: (g, 0, 0)), row],
        out_specs=row,
        out_shape=SDS((S, K), F32),
        compiler_params=_params(("parallel", "arbitrary")),
    )(da, w, res)


WGRAD_ACC_ELEMS = 6 * 1024 * 256


def wgrad(a, b, out_dtype, name):
    ga, gb = a.ndim == 3, b.ndim == 3
    G = a.shape[0] if ga else b.shape[0]
    S, K = a.shape[-2:]
    N = b.shape[-1]
    ts = _tile(S, 2048)
    ns = S // ts
    ng = _group_block(G, WGRAD_ACC_ELEMS // (K * N))

    def body(a_ref, b_ref, o_ref, acc):
        s = pl.program_id(1)

        @pl.when(s == 0)
        def _():
            acc[...] = jnp.zeros_like(acc)

        for j in range(ng):
            acc[j] += _dot(a_ref[j] if ga else a_ref[...], b_ref[j] if gb else b_ref[...], TN)

        @pl.when(s == ns - 1)
        def _():
            o_ref[...] = acc[...].astype(out_dtype)

    aspec = pl.BlockSpec((ng, ts, K), lambda g, s: (g, s, 0)) if ga else pl.BlockSpec((ts, K), lambda g, s: (s, 0))
    bspec = pl.BlockSpec((ng, ts, N), lambda g, s: (g, s, 0)) if gb else pl.BlockSpec((ts, N), lambda g, s: (s, 0))
    return pl.pallas_call(
        body, name=name, grid=(G // ng, ns),
        in_specs=[aspec, bspec],
        out_specs=pl.BlockSpec((ng, K, N), lambda g, s: (g, 0, 0)),
        out_shape=SDS((G, K, N), out_dtype),
        scratch_shapes=[pltpu.VMEM((ng, K, N), F32)],
        compiler_params=_params(("parallel", "arbitrary")),
    )(a, b)


def loss_grad(xn, tgt):
    S, N = xn.shape
    ts = _tile(S, 1024)

    def body(x_ref, t_ref, l_ref, dx_ref):
        @pl.when(pl.program_id(0) == 0)
        def _():
            l_ref[...] = jnp.zeros_like(l_ref)

        e = x_ref[...] - t_ref[...]
        dx_ref[...] = e * (1.0 / N)
        l_ref[...] += 0.5 * jnp.sum(jnp.mean(e * e, axis=-1, keepdims=True), axis=0, keepdims=True)

    row = pl.BlockSpec((ts, N), lambda s: (s, 0))
    return pl.pallas_call(
        body, name="loss_grad", grid=(S // ts,),
        in_specs=[row, row],
        out_specs=[pl.BlockSpec((1, 1), lambda s: (0, 0)), row],
        out_shape=[SDS((1, 1), F32), SDS((S, N), F32)],
        compiler_params=_params(("arbitrary",)),
    )(xn, tgt)


def _shift_down(x, k):
    if k == 0:
        return x
    rows = lax.broadcasted_iota(jnp.int32, x.shape, 0)
    return jnp.where(rows >= k, pltpu.roll(x, k, 0), 0.0)


def _shift_up(x, k):
    if k == 0:
        return x
    n = x.shape[0]
    rows = lax.broadcasted_iota(jnp.int32, x.shape, 0)
    return jnp.where(rows < n - k, pltpu.roll(x, n - k, 0), 0.0)


LANES = 128


def conv_mixer_fwd(u, cw):
    _, S, _ = u.shape

    def body(b_ref, c_ref, x_ref, w_ref, o_ref):
        p = c_ref[0] * x_ref[0]
        w = w_ref[0]
        conv = w[2:3] * p + w[1:2] * _shift_down(p, 1) + w[0:1] * _shift_down(p, 2)
        o_ref[0] = (b_ref[0] * conv).astype(BF16)

    def uspec(off):
        return pl.BlockSpec((1, S, GROUP), lambda g: (g + off, 0, 0))

    return pl.pallas_call(
        body, name="conv_mixer_fwd", grid=(3,),
        in_specs=[uspec(0), uspec(3), uspec(6), pl.BlockSpec((1, 8, GROUP), lambda g: (g, 0, 0))],
        out_specs=pl.BlockSpec((1, S, GROUP), lambda g: (g, 0, 0)),
        out_shape=SDS((4, S, GROUP), BF16),
        compiler_params=_params(("parallel",)),
    )(u, u, u, cw)


def conv_mixer_bwd(u, cw, dm):
    _, S, _ = u.shape
    nh = GROUP // LANES

    def body(b_ref, c_ref, x_ref, w_ref, d_ref, db_ref, dc_ref, dx_ref, dw_ref):
        cg, xi = c_ref[0], x_ref[0]
        p = cg * xi
        p1, p2 = _shift_down(p, 1), _shift_down(p, 2)
        w = w_ref[0]
        conv = w[2:3] * p + w[1:2] * p1 + w[0:1] * p2
        dt = d_ref[0]
        db_ref[0] = (dt * conv).astype(BF16)
        dcv = dt * b_ref[0]
        dp = w[2:3] * dcv + w[1:2] * _shift_up(dcv, 1) + w[0:1] * _shift_up(dcv, 2)
        dc_ref[0] = (dp * xi).astype(BF16)
        dx_ref[0] = (dp * cg).astype(BF16)
        dw = jnp.concatenate([jnp.sum(dcv * p2, axis=0, keepdims=True), jnp.sum(dcv * p1, axis=0, keepdims=True),
                              jnp.sum(dcv * p, axis=0, keepdims=True), jnp.zeros((5, LANES), F32)], axis=0)
        dw_ref[0] = dw

    def uspec(off):
        return pl.BlockSpec((1, S, LANES), lambda g, h: (g + off, 0, h))

    ospec = pl.BlockSpec((1, S, LANES), lambda g, h: (g, 0, h))
    wspec = pl.BlockSpec((1, 8, LANES), lambda g, h: (g, 0, h))
    return pl.pallas_call(
        body, name="conv_mixer_bwd", grid=(3, nh),
        in_specs=[uspec(0), uspec(3), uspec(6), wspec, ospec],
        out_specs=[ospec, ospec, ospec, wspec],
        out_shape=[SDS((3, S, GROUP), BF16)] * 3 + [SDS((3, 8, GROUP), F32)],
        compiler_params=_params(("parallel", "parallel")),
    )(u, u, u, cw, dm)


def qk_conv_fwd(u, qw):
    _, S, _ = u.shape

    def body(u_ref, w_ref, o_ref):
        x = u_ref[0]
        w = w_ref[0]
        pre = w[3:4] * x + w[2:3] * _shift_down(x, 1) + w[1:2] * _shift_down(x, 2) + w[0:1] * _shift_down(x, 3)
        o_ref[0] = pre * _sigmoid(pre)

    spec = pl.BlockSpec((1, S, GROUP), lambda g: (g, 0, 0))
    return pl.pallas_call(
        body, name="qk_conv_fwd", grid=(8,),
        in_specs=[spec, pl.BlockSpec((1, 8, GROUP), lambda g: (g, 0, 0))],
        out_specs=spec,
        out_shape=SDS((8, S, GROUP), F32),
        compiler_params=_params(("parallel",)),
    )(u, qw)


def qk_conv_bwd(u, qw, dqk, du):
    _, S, _ = u.shape
    nh = GROUP // LANES

    def body(u_ref, w_ref, d_ref, du_in_ref, du_ref, dw_ref):
        x = u_ref[0]
        w = w_ref[0]
        x1, x2, x3 = _shift_down(x, 1), _shift_down(x, 2), _shift_down(x, 3)
        pre = w[3:4] * x + w[2:3] * x1 + w[1:2] * x2 + w[0:1] * x3
        sig = _sigmoid(pre)
        dpre = d_ref[0].astype(F32) * (sig * (1.0 + pre * (1.0 - sig)))
        du = w[3:4] * dpre + w[2:3] * _shift_up(dpre, 1) + w[1:2] * _shift_up(dpre, 2) + w[0:1] * _shift_up(dpre, 3)
        du_ref[0] = du.astype(BF16)
        dw = jnp.concatenate([jnp.sum(dpre * x3, axis=0, keepdims=True), jnp.sum(dpre * x2, axis=0, keepdims=True),
                              jnp.sum(dpre * x1, axis=0, keepdims=True), jnp.sum(dpre * x, axis=0, keepdims=True),
                              jnp.zeros((4, LANES), F32)], axis=0)
        dw_ref[0] = dw

    spec = pl.BlockSpec((1, S, LANES), lambda g, h: (g, 0, h))
    wspec = pl.BlockSpec((1, 8, LANES), lambda g, h: (g, 0, h))
    return pl.pallas_call(
        body, name="qk_conv_bwd", grid=(8, nh),
        in_specs=[spec, wspec, spec, pl.BlockSpec(memory_space=pl.ANY)],
        out_specs=[spec, wspec],
        out_shape=[SDS(du.shape, BF16), SDS((8, 8, GROUP), F32)],
        input_output_aliases={3: 0},
        compiler_params=_params(("parallel", "parallel")),
    )(u, qw, dqk, du)


def _head_masks():
    lane = lax.broadcasted_iota(jnp.int32, (1, D_XA), 1)
    return [(lane >= h * XA_HEAD_DIM) & (lane < (h + 1) * XA_HEAD_DIM) for h in range(XA_HEADS)]


def xattn_fwd(u, qg, kv, tok):
    _, S, _ = u.shape
    ts = _tile(S, 1024)
    scale = XA_HEAD_DIM ** -0.5

    def body(q_ref, kv_ref, tok_ref, o_ref):
        q = q_ref[0]
        k = kv_ref[0].astype(BF16)
        v = kv_ref[1]
        o = jnp.zeros((ts, D_XA), F32)
        for m in _head_masks():
            s = _dot(jnp.where(m, q, 0.0).astype(BF16), k, NT) * scale
            s = s - jnp.max(s, axis=-1, keepdims=True)
            e = jnp.exp(s)
            p = e / jnp.sum(e, axis=-1, keepdims=True)
            o = o + _dot(p.astype(BF16), jnp.where(m, v, 0.0).astype(BF16), NN)
        o_ref[0] = o.astype(BF16)

    slot = tok.shape[0] - 1
    return pl.pallas_call(
        body, name="xattn_fwd", grid=(S // ts,),
        in_specs=[pl.BlockSpec((1, ts, GROUP), lambda s: (qg, s, 0)), pl.BlockSpec((2, N_MEM, GROUP), lambda s: (0, 0, 0)),
                  pl.BlockSpec(memory_space=pl.ANY)],
        out_specs=pl.BlockSpec((1, ts, GROUP), lambda s: (slot, s, 0)),
        out_shape=SDS(tok.shape, BF16),
        input_output_aliases={2: 0},
        compiler_params=_params(("parallel",)),
    )(u, kv, tok)


def xattn_bwd(u, qg, kv, dm, dg, du=None, dgate=None):
    _, S, _ = u.shape
    ts = _tile(S, 1024)
    scale = XA_HEAD_DIM ** -0.5

    def body(q_ref, kv_ref, do_ref, *refs):
        dq_ref, dkv_ref = refs[-2:]

        @pl.when(pl.program_id(0) == 0)
        def _():
            dkv_ref[...] = jnp.zeros_like(dkv_ref)

        q = q_ref[0]
        k = kv_ref[0]
        v = kv_ref[1]
        kb = k.astype(BF16)
        do = do_ref[0]
        dq = jnp.zeros((ts, D_XA), F32)
        dk = jnp.zeros((N_MEM, D_XA), F32)
        dv = jnp.zeros((N_MEM, D_XA), F32)
        for m in _head_masks():
            qm = jnp.where(m, q, 0.0).astype(BF16)
            s = _dot(qm, kb, NT) * scale
            s = s - jnp.max(s, axis=-1, keepdims=True)
            e = jnp.exp(s)
            p = e / jnp.sum(e, axis=-1, keepdims=True)
            dom = jnp.where(m, do, 0.0).astype(BF16)
            dp = _dot(dom, jnp.where(m, v, 0.0).astype(BF16), NT)
            ds = (p * (dp - jnp.sum(dp * p, axis=-1, keepdims=True)) * scale).astype(BF16)
            dq = dq + _dot(ds, jnp.where(m, k, 0.0).astype(BF16), NN)
            dk = dk + _dot(ds, qm, TN)
            dv = dv + _dot(p.astype(BF16), dom, TN)
        dq_ref[0] = dq.astype(BF16)
        if du is not None:
            dq_ref[1] = refs[0][0]
        dkv_ref[0] += dk
        dkv_ref[1] += dv

    in_specs = [pl.BlockSpec((1, ts, GROUP), lambda s: (qg, s, 0)), pl.BlockSpec((2, N_MEM, GROUP), lambda s: (0, 0, 0)),
                pl.BlockSpec((1, ts, GROUP), lambda s: (dg, s, 0))]
    args, aliases = [u, kv, dm], {}
    dq_spec, dq_shape = pl.BlockSpec((1, ts, GROUP), lambda s: (0, s, 0)), SDS((1, S, GROUP), BF16)
    if du is not None:
        in_specs += [pl.BlockSpec((1, ts, GROUP), lambda s: (0, s, 0)), pl.BlockSpec(memory_space=pl.ANY)]
        args += [dgate, du]
        aliases = {4: 0}
        dq_spec, dq_shape = pl.BlockSpec((2, ts, GROUP), lambda s: (qg // 2, s, 0)), SDS(du.shape, BF16)
    return pl.pallas_call(
        body, name="xattn_bwd", grid=(S // ts,),
        in_specs=in_specs,
        out_specs=[dq_spec, pl.BlockSpec((2, N_MEM, GROUP), lambda s: (0, 0, 0))],
        out_shape=[dq_shape, SDS((2, N_MEM, GROUP), F32)],
        input_output_aliases=aliases,
        compiler_params=_params(("arbitrary",)),
    )(*args)


ML_BLOCK_CHUNKS = 4
H4 = ML_HEADS
L = ML_CHUNK
NLANE = ML_HEAD_DIM


def _chunk_consts():
    r = lax.broadcasted_iota(jnp.int32, (1, L, L), 1)
    c = lax.broadcasted_iota(jnp.int32, (1, L, L), 2)
    return r >= c, r <= c, r == c


def _gate_cols(gb):
    lane = lax.broadcasted_iota(jnp.int32, gb.shape, 1)
    li = jnp.stack([jnp.sum(jnp.where(lane == h, gb, 0.0), axis=1, keepdims=True) for h in range(H4)])
    gf = jnp.stack([jnp.sum(jnp.where(lane == H4 + h, gb, 0.0), axis=1, keepdims=True) for h in range(H4)])
    return li, gf


def _log_sigmoid(x):
    return jnp.minimum(x, 0.0) - jnp.log(1.0 + jnp.exp(-jnp.abs(x)))


def _chunk_forward(q, k, v_aug, li_col, lf_col, c_prev, m_prev):
    tri, tri_t, eye = _chunk_consts()
    lf_row = jnp.sum(jnp.where(eye, lf_col, 0.0), axis=1, keepdims=True)
    li_row = jnp.sum(jnp.where(eye, li_col, 0.0), axis=1, keepdims=True)
    bcum_col = jnp.sum(jnp.where(tri, lf_row, 0.0), axis=2, keepdims=True)
    bcum_row = jnp.sum(jnp.where(tri_t, lf_col, 0.0), axis=1, keepdims=True)
    log_d = jnp.where(tri, bcum_col - bcum_row + li_row, NEG)
    log_inter = bcum_col + m_prev
    m_t = jnp.maximum(log_inter, jnp.max(log_d, axis=2, keepdims=True))
    w_intra = jnp.exp(log_d - m_t)
    w_inter = jnp.exp(log_inter - m_t)
    sc = _bdot(q, k, 2, 2) * w_intra
    qc = _bdot1(q, c_prev, 2, 1)
    num = _bdot(sc, v_aug, 2, 1) + w_inter * qc
    lane = lax.broadcasted_iota(jnp.int32, num.shape, 2)
    den = jnp.sum(jnp.where(lane == NLANE, num, 0.0), axis=2, keepdims=True)
    e_m = jnp.exp(-m_t)
    b_last = jnp.sum(lf_row, axis=2, keepdims=True)
    log_w = b_last - bcum_col + li_col
    m_new = jnp.maximum(b_last + m_prev, jnp.max(log_w, axis=1, keepdims=True))
    w_k = jnp.exp(log_w - m_new)
    decay = jnp.exp(b_last + m_prev - m_new)
    return dict(w_intra=w_intra, w_inter=w_inter, sc=sc, qc=qc, num=num, den=den, e_m=e_m, lane=lane,
                w_k=w_k, decay=decay, m_new=m_new)


def mlstm_fwd(qk, u, bg):
    _, S, _ = qk.shape
    nc = S // L
    cb = min(ML_BLOCK_CHUNKS, nc)
    rows = cb * L
    kscale = ML_HEAD_DIM ** -0.5

    def body(qk_ref, v_ref, g_ref, bg_ref, h_ref, cst_ref, mst_ref, c_sc, m_sc):
        @pl.when(pl.program_id(0) == 0)
        def _():
            c_sc[...] = jnp.zeros_like(c_sc)
            m_sc[...] = jnp.zeros_like(m_sc)

        for c in range(cb):
            sl = pl.ds(c * L, L)
            q = qk_ref[0:H4, sl, :]
            k = qk_ref[H4:2 * H4, sl, :] * kscale
            v = v_ref[:, sl, :]
            lane = lax.broadcasted_iota(jnp.int32, v.shape, 2)
            v_aug = jnp.where(lane == NLANE, 1.0, v)
            li_col, gf = _gate_cols(g_ref[0, sl, :] + bg_ref[...])
            lf_col = _log_sigmoid(gf)
            c_prev = c_sc[...]
            m_prev = m_sc[...]
            f = _chunk_forward(q, k, v_aug, li_col, lf_col, c_prev, m_prev)
            r = 1.0 / jnp.maximum(jnp.abs(f["den"]), f["e_m"])
            h_ref[:, sl, :] = jnp.where(lane < NLANE, f["num"] * r, 0.0)
            cst_ref[c] = c_prev
            mst_ref[c] = jnp.broadcast_to(m_prev, (H4, 1, LANES))
            c_sc[...] = f["decay"] * c_prev + _bdot(k * f["w_k"], v_aug, 1, 1)
            m_sc[...] = f["m_new"]

    def hspec(blk):
        return pl.BlockSpec((H4, rows, GROUP), lambda i: (blk, i, 0))

    return pl.pallas_call(
        body, name="mlstm_fwd", grid=(nc // cb,),
        in_specs=[pl.BlockSpec((2 * H4, rows, GROUP), lambda i: (0, i, 0)), hspec(2),
                  pl.BlockSpec((1, rows, GROUP), lambda i: (17, i, 0)), pl.BlockSpec((1, GROUP), lambda i: (0, 0))],
        out_specs=[hspec(0), pl.BlockSpec((cb, H4, GROUP, GROUP), lambda i: (i, 0, 0, 0)),
                   pl.BlockSpec((cb, H4, 1, LANES), lambda i: (i, 0, 0, 0))],
        out_shape=[SDS((H4, S, GROUP), F32), SDS((nc, H4, GROUP, GROUP), F32), SDS((nc, H4, 1, LANES), F32)],
        scratch_shapes=[pltpu.VMEM((H4, GROUP, GROUP), F32), pltpu.VMEM((H4, 1, 1), F32)],
        compiler_params=_params(("arbitrary",)),
    )(qk, u, u, bg)


def mlstm_bwd(qk, u, bg, cst, mst, dh, du):
    _, S, _ = qk.shape
    nc = S // L
    cb = min(ML_BLOCK_CHUNKS, nc)
    rows = cb * L
    nb = nc // cb
    kscale = ML_HEAD_DIM ** -0.5

    def body(qk_ref, v_ref, g_ref, bg_ref, cst_ref, mst_ref, dh_ref, du_in_ref, dqk_ref, dv_ref, dg_ref, dbg_ref, dc_sc):
        @pl.when(pl.program_id(0) == 0)
        def _():
            dc_sc[...] = jnp.zeros_like(dc_sc)
            dbg_ref[...] = jnp.zeros_like(dbg_ref)

        tri, tri_t, eye = _chunk_consts()
        for c in reversed(range(cb)):
            sl = pl.ds(c * L, L)
            q = qk_ref[0:H4, sl, :]
            k = qk_ref[H4:2 * H4, sl, :] * kscale
            v = v_ref[:, sl, :]
            lane = lax.broadcasted_iota(jnp.int32, v.shape, 2)
            v_aug = jnp.where(lane == NLANE, 1.0, v)
            li_col, gf = _gate_cols(g_ref[0, sl, :] + bg_ref[...])
            lf_col = _log_sigmoid(gf)
            c_prev = cst_ref[c]
            m_prev = mst_ref[c][:, :, 0:1]
            f = _chunk_forward(q, k, v_aug, li_col, lf_col, c_prev, m_prev)
            w_intra, w_inter, sc, num, den, e_m = f["w_intra"], f["w_inter"], f["sc"], f["num"], f["den"], f["e_m"]
            absd = jnp.abs(den)
            r = 1.0 / jnp.maximum(absd, e_m)
            dhv = dh_ref[:, sl, :]
            s1 = jnp.sum(jnp.where(lane < NLANE, dhv * num, 0.0), axis=2, keepdims=True)
            dden = jnp.where(absd > e_m, -s1 * r * r * jnp.sign(den), 0.0)
            dnum = jnp.where(lane == NLANE, dden, jnp.where(lane < NLANE, dhv * r, 0.0))
            dsc = _bdot1(dnum, v_aug, 2, 2)
            dv = _bdot1(sc, dnum, 1, 1)
            gmat = dsc * sc
            dqk = dsc * w_intra
            dq = _bdot1(dqk, k, 2, 1) + w_inter * _bdot1(dnum, c_prev, 2, 2)
            dk = _bdot1(dqk, q, 1, 1)
            dc_prev = _bdot(q * w_inter, dnum, 1, 1)
            dlog_inter = jnp.sum(dnum * f["qc"], axis=2, keepdims=True) * w_inter
            dbcum_col = dlog_inter + jnp.sum(gmat, axis=2, keepdims=True)
            g_row = jnp.sum(gmat, axis=1, keepdims=True)
            dcn = dc_sc[...]
            w_k, decay = f["w_k"], f["decay"]
            kw = k * w_k
            dc_prev = dc_prev + decay * dcn
            db_last = jnp.sum(jnp.sum(dcn * c_prev, axis=2, keepdims=True), axis=1, keepdims=True) * decay
            dkw = _bdot(v_aug, dcn, 2, 2)
            dv = dv + _bdot1(kw, dcn, 2, 1)
            dk = dk + dkw * w_k
            dlogw = jnp.sum(dkw * k, axis=2, keepdims=True) * w_k
            db_last = db_last + jnp.sum(dlogw, axis=1, keepdims=True)
            dbcum_col = dbcum_col - dlogw
            rowi = lax.broadcasted_iota(jnp.int32, (1, L, 1), 1)
            dbcum_col = dbcum_col + jnp.where(rowi == L - 1, db_last, 0.0)
            dbcum_row = jnp.sum(jnp.where(eye, dbcum_col, 0.0), axis=1, keepdims=True) - g_row
            dlf_col = jnp.sum(jnp.where(tri_t, dbcum_row, 0.0), axis=2, keepdims=True)
            dli_col = dlogw + jnp.sum(jnp.where(eye, g_row, 0.0), axis=2, keepdims=True)
            dgf_col = dlf_col * _sigmoid(-gf)
            lane_g = lax.broadcasted_iota(jnp.int32, (L, GROUP), 1)
            dg = jnp.zeros((L, GROUP), F32)
            for h in range(H4):
                dg = dg + jnp.where(lane_g == h, dli_col[h], 0.0) + jnp.where(lane_g == H4 + h, dgf_col[h], 0.0)
            dqk_ref[0:H4, sl, :] = dq.astype(BF16)
            dqk_ref[H4:2 * H4, sl, :] = (dk * kscale).astype(BF16)
            dv_ref[:, sl, :] = jnp.where(lane < NLANE, dv, 0.0).astype(BF16)
            dg_ref[0, sl, :] = dg.astype(BF16)
            dbg_ref[...] += jnp.sum(dg, axis=0, keepdims=True)
            dc_sc[...] = dc_prev

    def hspec(blk):
        return pl.BlockSpec((H4, rows, GROUP), lambda i: (blk, nb - 1 - i, 0))

    gspec = pl.BlockSpec((1, rows, GROUP), lambda i: (17, nb - 1 - i, 0))
    qkspec = pl.BlockSpec((2 * H4, rows, GROUP), lambda i: (0, nb - 1 - i, 0))
    return pl.pallas_call(
        body, name="mlstm_bwd", grid=(nb,),
        in_specs=[qkspec, hspec(2), gspec, pl.BlockSpec((1, GROUP), lambda i: (0, 0)),
                  pl.BlockSpec((cb, H4, GROUP, GROUP), lambda i: (nb - 1 - i, 0, 0, 0)),
                  pl.BlockSpec((cb, H4, 1, LANES), lambda i: (nb - 1 - i, 0, 0, 0)), hspec(0), pl.BlockSpec(memory_space=pl.ANY)],
        out_specs=[qkspec, hspec(2), pl.BlockSpec((1, rows, GROUP), lambda i: (0, nb - 1 - i, 0)),
                   pl.BlockSpec((1, GROUP), lambda i: (0, 0))],
        input_output_aliases={7: 1},
        out_shape=[SDS((2 * H4, S, GROUP), BF16), SDS(du.shape, BF16),
                   SDS((1, S, GROUP), BF16), SDS((1, GROUP), F32)],
        scratch_shapes=[pltpu.VMEM((H4, GROUP, GROUP), F32)],
        compiler_params=_params(("arbitrary",)),
    )(qk, u, u, bg, cst, mst, dh, du)


def head_norm_fwd(hm, u, hg):
    _, S, _ = hm.shape
    ts = _tile(S, 2048)

    def body(h_ref, o_ref, g_ref, t_ref):
        h = h_ref[0]
        lane = lax.broadcasted_iota(jnp.int32, h.shape, 1)
        valid = lane < ML_HEAD_DIM
        mu = jnp.sum(h, axis=-1, keepdims=True) * (1.0 / ML_HEAD_DIM)
        hc = jnp.where(valid, h - mu, 0.0)
        var = jnp.sum(hc * hc, axis=-1, keepdims=True) * (1.0 / ML_HEAD_DIM)
        hn = hc * lax.rsqrt(var + LN_EPS) * g_ref[0]
        t_ref[0] = (_sigmoid(o_ref[0]) * hn).astype(BF16)

    return pl.pallas_call(
        body, name="head_norm_fwd", grid=(H4, S // ts),
        in_specs=[pl.BlockSpec((1, ts, GROUP), lambda h, s: (h, s, 0)), pl.BlockSpec((1, ts, GROUP), lambda h, s: (12 + h, s, 0)),
                  pl.BlockSpec((1, 1, GROUP), lambda h, s: (h, 0, 0))],
        out_specs=pl.BlockSpec((1, ts, GROUP), lambda h, s: (h, s, 0)),
        out_shape=SDS((H4 + 1, S, GROUP), BF16),
        compiler_params=_params(("parallel", "parallel")),
    )(hm, u, hg)


def head_norm_bwd(hm, u, hg, dm):
    _, S, _ = hm.shape
    ts = _tile(S, 2048)

    def body(h_ref, o_ref, g_ref, d_ref, dh_ref, do_ref, dg_ref):
        @pl.when(pl.program_id(1) == 0)
        def _():
            dg_ref[...] = jnp.zeros_like(dg_ref)

        h = h_ref[0]
        lane = lax.broadcasted_iota(jnp.int32, h.shape, 1)
        valid = lane < ML_HEAD_DIM
        inv = 1.0 / ML_HEAD_DIM
        mu = jnp.sum(h, axis=-1, keepdims=True) * inv
        hc = jnp.where(valid, h - mu, 0.0)
        var = jnp.sum(hc * hc, axis=-1, keepdims=True) * inv
        rstd = lax.rsqrt(var + LN_EPS)
        xhat = hc * rstd
        g = g_ref[0]
        sig = _sigmoid(o_ref[0])
        dt = jnp.where(valid, d_ref[0], 0.0)
        do_ref[0] = (dt * xhat * g * sig * (1.0 - sig)).astype(BF16)
        dhn = dt * sig
        dg_ref[0] += jnp.sum(dhn * xhat, axis=0, keepdims=True)
        dxh = dhn * g
        m1 = jnp.sum(dxh, axis=-1, keepdims=True) * inv
        m2 = jnp.sum(dxh * xhat, axis=-1, keepdims=True) * inv
        dh_ref[0] = jnp.where(valid, rstd * (dxh - m1 - xhat * m2), 0.0)

    spec = pl.BlockSpec((1, ts, GROUP), lambda h, s: (h, s, 0))
    gspec = pl.BlockSpec((1, 1, GROUP), lambda h, s: (h, 0, 0))
    return pl.pallas_call(
        body, name="head_norm_bwd", grid=(H4, S // ts),
        in_specs=[spec, pl.BlockSpec((1, ts, GROUP), lambda h, s: (12 + h, s, 0)), gspec, spec],
        out_specs=[spec, pl.BlockSpec((1, ts, GROUP), lambda h, s: (12 + h, s, 0)), gspec],
        out_shape=[SDS((H4, S, GROUP), F32), SDS((u.shape[0], S, GROUP), BF16), SDS((H4, 1, GROUP), F32)],
        compiler_params=_params(("parallel", "arbitrary")),
    )(hm, u, hg, dm)


def _adamw_math(w, g, m, v):
    c1 = 1.0 / (1.0 - ADAM_B1 ** ADAM_STEP)
    c2 = 1.0 / (1.0 - ADAM_B2 ** ADAM_STEP)
    nm = ADAM_B1 * m + (1.0 - ADAM_B1) * g
    nv = ADAM_B2 * v + (1.0 - ADAM_B2) * (g * g)
    return -ADAM_LR * ((nm * c1) / (jnp.sqrt(nv * c2) + ADAM_EPS) + ADAM_WD * w), nm, nv


def _row_tile(R, cap=512):
    return R if R <= cap else max(d for d in range(8, cap + 1, 8) if R % d == 0)


def adamw_into(w, m, v, g, outs, idx, after, name):
    R, C = g.shape
    tr = _row_tile(R)
    lead = (0,) * len(idx)

    def body(w_ref, m_ref, v_ref, g_ref, *rest):
        go_ref, d_ref, nm_ref, nv_ref, token = rest[-5:]
        token[...] = jnp.zeros_like(token)
        gv = g_ref[...]
        d, nm, nv = _adamw_math(w_ref[lead], gv, m_ref[lead], v_ref[lead])
        go_ref[lead] = gv
        d_ref[lead] = d
        nm_ref[lead] = nm
        nv_ref[lead] = nv

    blk = pl.BlockSpec((1,) * len(idx) + (tr, C), lambda r: idx + (r, 0))
    any_space = pl.BlockSpec(memory_space=pl.ANY)
    in_specs, args, aliases = [blk, blk, blk, pl.BlockSpec((tr, C), lambda r: (r, 0)), any_space], [w, m, v, g, g if after is None else after], {}
    if outs is not None:
        in_specs += [any_space] * 4
        args += list(outs)
        aliases = {5 + i: i for i in range(4)}
    out = pl.pallas_call(
        body, name=name, grid=(R // tr,),
        in_specs=in_specs, out_specs=[blk] * 4 + [pl.BlockSpec((8, LANES), lambda r: (0, 0))],
        out_shape=[SDS(w.shape, F32)] * 4 + [SDS((8, LANES), F32)],
        input_output_aliases=aliases, compiler_params=_params(("arbitrary",)),
    )(*args)
    return out[:4], out[4]


def adamw(w, g, m, v, name):
    R, C = w.shape
    tr = _row_tile(R)

    def body(w_ref, g_ref, m_ref, v_ref, d_ref, nm_ref, nv_ref):
        d_ref[...], nm_ref[...], nv_ref[...] = _adamw_math(w_ref[...], g_ref[...], m_ref[...], v_ref[...])

    spec = pl.BlockSpec((tr, C), lambda i: (i, 0))
    return pl.pallas_call(
        body, name=name, grid=(R // tr,),
        in_specs=[spec] * 4, out_specs=[spec] * 3,
        out_shape=[SDS((R, C), F32)] * 3,
        compiler_params=_params(("parallel",)),
    )(w, g, m, v)


HBM = pl.BlockSpec(memory_space=pl.ANY)
ROW_SPLIT = 2
PAIR_SPLIT = 1


def _position():
    x, y, c = lax.axis_index("x"), lax.axis_index("y"), lax.axis_index("c")
    return x, y, c, [(1 - x, y), (x, 1 - y), (1 - x, 1 - y)]


def _unique(items):
    arrays = []
    for a, _ in items:
        if not any(a is b for b in arrays):
            arrays.append(a)
    return arrays, [next(i for i, b in enumerate(arrays) if b is a) for a, _ in items]


def place_own(items, me, after, name):
    arrays, src_of = _unique(items)
    n = len(items)
    shapes = [a.shape[len(p):] for a, p in items]

    def body(me_ref, *refs):
        for t in range(n):
            refs[n + 1 + t][0] = refs[t][(0,) * len(items[t][1])]

    in_specs, out_specs = [], []
    for (a, p), shp in zip(items, shapes):
        blk = shp[:-2] + (shp[-2] // ROW_SPLIT, shp[-1])
        lead = (0,) * (len(shp) - 2)
        in_specs.append(pl.BlockSpec((1,) * len(p) + blk, functools.partial(lambda r, me_ref, p, lead: p + lead + (r, 0), p=p, lead=lead)))
        out_specs.append(pl.BlockSpec((1,) + blk, functools.partial(lambda r, me_ref, lead: (me_ref[0],) + lead + (r, 0), lead=lead)))
    in_specs.append(pl.BlockSpec(memory_space=pl.ANY))
    return pl.pallas_call(
        body, name=name,
        grid_spec=pltpu.PrefetchScalarGridSpec(num_scalar_prefetch=1, grid=(ROW_SPLIT,), in_specs=in_specs, out_specs=out_specs),
        out_shape=[SDS((N_CHIPS,) + tuple(shp), a.dtype) for shp, (a, _) in zip(shapes, items)],
        compiler_params=_params(("parallel",)),
    )(me, *[arrays[i] for i in src_of], after)


SEM = pl.BlockSpec(memory_space=pltpu.SEMAPHORE)
IN_HBM = pl.BlockSpec(memory_space=pltpu.HBM)
DATAFLOW = pltpu.SideEffectType.DATAFLOW_SIDE_EFFECTING


def split_start(bufs, plan, n_copies, after, name):
    n = len(bufs)

    def body(*refs):
        send, recv, token = refs[n + 1], refs[n + 2], refs[-1]
        x, y, c, chips = _position()
        for k, (src, dst, dev) in enumerate(plan(refs[:n], x, y, c, chips)):
            pltpu.make_async_remote_copy(src_ref=src, dst_ref=dst, send_sem=send.at[k], recv_sem=recv.at[k],
                                         device_id=dev, device_id_type=MESH).start()
        token[...] = jnp.zeros_like(token)

    out = pl.pallas_call(
        body, name=name,
        out_shape=(pltpu.SemaphoreType.DMA((n_copies,)), pltpu.SemaphoreType.DMA((n_copies,)),
                   *[pltpu.HBM(b.shape, b.dtype) for b in bufs], SDS((8, LANES), F32)),
        in_specs=[IN_HBM] * n + [pl.BlockSpec(memory_space=pl.ANY)],
        out_specs=(SEM, SEM, *[IN_HBM] * n, pl.BlockSpec(memory_space=pltpu.VMEM)),
        input_output_aliases={i: 2 + i for i in range(n)},
        compiler_params=pltpu.CompilerParams(has_side_effects=DATAFLOW),
    )(*[pltpu.with_memory_space_constraint(b, pltpu.HBM) for b in bufs], after)
    return out[0], out[1], list(out[2:2 + n]), out[-1]


def split_wait(send, recv, bufs, plan, after, name):
    n = len(bufs)

    def body(*refs):
        send_ref, recv_ref = refs[n], refs[n + 1]
        x, y, c, chips = _position()
        for k, (src, dst, dev) in enumerate(plan(refs[:n], x, y, c, chips)):
            cp = pltpu.make_async_remote_copy(src_ref=src, dst_ref=dst, send_sem=send_ref.at[k], recv_sem=recv_ref.at[k],
                                              device_id=dev, device_id_type=MESH)
            cp.wait_send()
            cp.wait_recv()

    return list(pl.pallas_call(
        body, name=name, out_shape=tuple(pltpu.HBM(b.shape, b.dtype) for b in bufs),
        in_specs=[IN_HBM] * n + [SEM, SEM, pl.BlockSpec(memory_space=pl.ANY)], out_specs=tuple([IN_HBM] * n),
        input_output_aliases={i: i for i in range(n)},
        compiler_params=pltpu.CompilerParams(has_side_effects=DATAFLOW),
    )(*bufs, send, recv, after))


def _gather_plan(shapes, landing):
    n = len(shapes)

    def plan(refs, x, y, c, chips):
        out = []
        for t in range(n):
            half = shapes[t][0] // 2
            rows = pl.ds(c * half, half)
            for cx, cy in chips:
                slot = 2 * cx + cy if landing else 2 * x + y
                out.append((refs[t].at[rows], refs[n + t].at[slot, rows], (cx, cy, c)))
        return out

    return plan


def gather_start(shards, placed, after, name):
    shapes = [s.shape for s in shards]
    send, recv, bufs, token = split_start(list(shards) + list(placed), _gather_plan(shapes, False), 3 * len(shards), after, name)
    return (send, recv, bufs, shapes), token


def gather_wait(state, after, name):
    send, recv, bufs, shapes = state
    return split_wait(send, recv, bufs, _gather_plan(shapes, True), after, name)[len(shapes):]


def gather_pass_on(placed, shapes, name):
    n = len(placed)

    def body(*refs):
        outs, send, recv = refs[n:2 * n], refs[2 * n], refs[2 * n + 1]
        x, y, c, chips = _position()
        cps = []
        for t in range(n):
            half = shapes[t][0] // 2
            for j, (cx, cy) in enumerate(chips):
                piece = outs[t].at[2 * cx + cy, pl.ds(c * half, half)]
                cp = pltpu.make_async_remote_copy(src_ref=piece, dst_ref=piece, send_sem=send.at[3 * t + j], recv_sem=recv.at[3 * t + j],
                                                  device_id=(x, y, 1 - c), device_id_type=MESH)
                cp.start()
                cps.append(cp)
        for t in range(n):
            half = shapes[t][0] // 2
            for j, (cx, cy) in enumerate(chips):
                piece = outs[t].at[2 * cx + cy, pl.ds((1 - c) * half, half)]
                pltpu.make_async_remote_copy(src_ref=piece, dst_ref=piece, send_sem=send.at[3 * t + j], recv_sem=recv.at[3 * t + j],
                                             device_id=(x, y, 1 - c), device_id_type=MESH).wait_recv()
        for cp in cps:
            cp.wait_send()

    return pl.pallas_call(
        body, name=name,
        in_specs=[HBM] * n, out_specs=[HBM] * n,
        out_shape=[SDS(p.shape, p.dtype) for p in placed],
        input_output_aliases={t: t for t in range(n)},
        scratch_shapes=[pltpu.SemaphoreType.DMA((3 * n,))] * 2,
    )(*placed)


def _flip(k, x, y, c):
    return ((1 - x) if k & 4 else x, (1 - y) if k & 2 else y, (1 - c) if k & 1 else c)


def small_allgather(v, reduce):
    R, C = v.shape

    def body(v_ref, o_ref, *scratch):
        if reduce:
            buf, send, recv = scratch
        else:
            buf, (send, recv) = o_ref, scratch
        x, y, c, _ = _position()
        me = 4 * x + 2 * y + c
        buf[me] = v_ref[...]
        sends = []
        for k in range(1, N_DEV):
            cp = pltpu.make_async_remote_copy(src_ref=v_ref, dst_ref=buf.at[me], send_sem=send.at[k - 1], recv_sem=recv.at[k - 1],
                                              device_id=_flip(k, x, y, c), device_id_type=MESH)
            cp.start()
            sends.append(cp)
        for k in range(1, N_DEV):
            px, py, pc = _flip(k, x, y, c)
            pltpu.make_async_remote_copy(src_ref=v_ref, dst_ref=buf.at[4 * px + 2 * py + pc], send_sem=send.at[k - 1],
                                         recv_sem=recv.at[k - 1], device_id=(px, py, pc), device_id_type=MESH).wait_recv()
        for cp in sends:
            cp.wait_send()
        if reduce:
            acc = buf[0]
            for i in range(1, N_DEV):
                acc = acc + buf[i]
            o_ref[...] = acc

    vm = pl.BlockSpec(memory_space=pltpu.VMEM)
    sems = [pltpu.SemaphoreType.DMA((N_DEV - 1,)), pltpu.SemaphoreType.DMA((N_DEV - 1,))]
    return pl.pallas_call(
        body, name="small_allreduce" if reduce else "small_allgather",
        in_specs=[vm], out_specs=vm,
        out_shape=SDS((R, C) if reduce else (N_DEV, R, C), F32),
        scratch_shapes=([pltpu.VMEM((N_DEV, R, C), F32)] if reduce else []) + sems,
    )(v)


def rs_exchange_sibling(gs):
    n = len(gs)

    def body(*refs):
        ins, outs, send, recv = refs[:n], refs[n:2 * n], refs[2 * n], refs[2 * n + 1]
        x, y, c, _ = _position()
        cps = []
        for t in range(n):
            cp = pltpu.make_async_remote_copy(src_ref=ins[t].at[:, 1 - c], dst_ref=outs[t], send_sem=send.at[t], recv_sem=recv.at[t],
                                              device_id=(x, y, 1 - c), device_id_type=MESH)
            cp.start()
            cps.append(cp)
        for cp in cps:
            cp.wait()

    return pl.pallas_call(
        body, name="rs_exchange_sibling", in_specs=[HBM] * n, out_specs=[HBM] * n,
        out_shape=[SDS((g.shape[0],) + g.shape[2:], g.dtype) for g in gs],
        scratch_shapes=[pltpu.SemaphoreType.DMA((n,)), pltpu.SemaphoreType.DMA((n,))],
    )(*gs)


def rs_pair_add(gs, rs, c):
    n = len(gs)

    def body(c_ref, *refs):
        for t in range(n):
            refs[2 * n + t][0] = (refs[t][0, 0].astype(F32) + refs[n + t][0].astype(F32)).astype(BF16)

    in_specs, out_specs, out_shape = [], [], []
    for g in gs:
        _, _, h, C = g.shape
        in_specs.append(pl.BlockSpec((1, 1, h // PAIR_SPLIT, C), lambda j, r, c_ref: (j, c_ref[0], r, 0)))
    for g in gs:
        _, _, h, C = g.shape
        spec = pl.BlockSpec((1, h // PAIR_SPLIT, C), lambda j, r, c_ref: (j, r, 0))
        in_specs.append(spec)
        out_specs.append(spec)
        out_shape.append(SDS((N_CHIPS, h, C), BF16))
    return pl.pallas_call(
        body, name="rs_pair_add",
        grid_spec=pltpu.PrefetchScalarGridSpec(num_scalar_prefetch=1, grid=(N_CHIPS, PAIR_SPLIT), in_specs=in_specs, out_specs=out_specs),
        out_shape=out_shape, compiler_params=_params(("parallel", "parallel")),
    )(c, *gs, *rs)


def _rs_plan(n):
    def plan(refs, x, y, c, chips):
        return [(refs[t].at[2 * cx + cy], refs[n + t].at[j], (cx, cy, c)) for t in range(n) for j, (cx, cy) in enumerate(chips)]

    return plan


def rs_chip_add(ps, qs, me_c):
    n = len(ps)

    def body(me_ref, *refs):
        for t in range(n):
            q = refs[n + t]
            refs[2 * n + t][0] = ((refs[t][0].astype(F32) + q[0].astype(F32)) + q[1].astype(F32)) + q[2].astype(F32)

    in_specs, out_specs, out_shape = [], [], []
    for p in ps:
        _, h, C = p.shape
        in_specs.append(pl.BlockSpec((1, h // ROW_SPLIT, C), lambda r, me_ref: (me_ref[0], r, 0)))
    for p in ps:
        _, h, C = p.shape
        in_specs.append(pl.BlockSpec((3, h // ROW_SPLIT, C), lambda r, me_ref: (0, r, 0)))
        out_specs.append(pl.BlockSpec((1, h // ROW_SPLIT, C), lambda r, me_ref: (me_ref[1], r, 0)))
        out_shape.append(SDS((2, h, C), F32))
    return pl.pallas_call(
        body, name="rs_chip_add",
        grid_spec=pltpu.PrefetchScalarGridSpec(num_scalar_prefetch=1, grid=(ROW_SPLIT,), in_specs=in_specs, out_specs=out_specs),
        out_shape=out_shape, compiler_params=_params(("parallel",)),
    )(me_c, *ps, *qs)


def rs_share(rs):
    n = len(rs)

    def body(*refs):
        outs, send, recv = refs[n:2 * n], refs[2 * n], refs[2 * n + 1]
        x, y, c, _ = _position()
        cps = []
        for t in range(n):
            cp = pltpu.make_async_remote_copy(src_ref=outs[t].at[c], dst_ref=outs[t].at[c], send_sem=send.at[t], recv_sem=recv.at[t],
                                              device_id=(x, y, 1 - c), device_id_type=MESH)
            cp.start()
            cps.append(cp)
        for cp in cps:
            cp.wait()

    return pl.pallas_call(
        body, name="rs_share", in_specs=[HBM] * n, out_specs=[HBM] * n,
        out_shape=[SDS(r.shape, r.dtype) for r in rs],
        input_output_aliases={t: t for t in range(n)},
        scratch_shapes=[pltpu.SemaphoreType.DMA((n,))] * 2,
    )(*rs)


def rs_begin(gs, after, name):
    c = lax.axis_index("c")
    n = len(gs)
    g5 = [g.reshape(N_CHIPS, 2, g.shape[1] // 2, g.shape[2]) for g in gs]
    from_sibling = rs_exchange_sibling(g5)
    pair = rs_pair_add(g5, from_sibling, jnp.reshape(c, (1,)).astype(jnp.int32))
    lands = [lax.empty((3,) + p.shape[1:], p.dtype) for p in pair]
    send, recv, bufs, token = split_start(list(pair) + lands, _rs_plan(n), 3 * n, from_sibling[0] if after is None else after, name)
    return (send, recv, bufs, [g.shape for g in gs]), token


def rs_end(state, after, name):
    x, y, c = lax.axis_index("x"), lax.axis_index("y"), lax.axis_index("c")
    send, recv, bufs, shapes = state
    n = len(shapes)
    bufs = split_wait(send, recv, bufs, _rs_plan(n), after, name)
    half = rs_chip_add(bufs[:n], bufs[n:], jnp.stack([2 * x + y, c]).astype(jnp.int32))
    both = rs_share(half)
    return [b.reshape(s[1], s[2]) for b, s in zip(both, shapes)]


def _pad_last(a, n):
    return jnp.pad(a, [(0, 0)] * (a.ndim - 1) + [(0, n - a.shape[-1])])


def _heads_to_groups(w):
    k = w.shape[0]
    return _pad_last(w.reshape(k, ML_HEADS, ML_HEAD_DIM).transpose(1, 0, 2), GROUP)


def _groups_to_heads(g):
    return g[:, :, :ML_HEAD_DIM].transpose(1, 0, 2).reshape(g.shape[1], D_TOK)


def _cols_to_groups(w):
    k, n = w.shape
    return w.reshape(k, n // GROUP, GROUP).transpose(1, 0, 2)


def _groups_to_cols(g):
    n, k, _ = g.shape
    return g.transpose(1, 0, 2).reshape(k, n * GROUP)


def _chips_to_cols(a):
    return a.transpose(1, 0, 2).reshape(a.shape[1], -1)


def _cols_to_chips(w):
    k, n = w.shape
    return w.reshape(k, N_CHIPS, n // N_CHIPS).transpose(1, 0, 2)


def _mlstm_in_groups(w):
    parts = [_heads_to_groups(w[:, i * D_TOK:(i + 1) * D_TOK]) for i in range(4)]
    gates = _pad_last(w[:, 4 * D_TOK:4 * D_TOK + 2 * ML_HEADS], GROUP)[None]
    qmem = w[:, 4 * D_TOK + 2 * ML_HEADS:][None]
    return jnp.concatenate(parts + [qmem, gates], axis=0)


def _mlstm_in_ungroup(g):
    parts = [_groups_to_heads(g[4 * i:4 * i + 4]) for i in range(4)]
    return jnp.concatenate(parts + [g[17][:, :2 * ML_HEADS], g[16]], axis=1)


def _taps_to_groups(w, width):
    taps = w.shape[0]
    g = _pad_last(w.reshape(taps, -1, width), GROUP).transpose(1, 0, 2)
    return jnp.pad(g, ((0, 0), (0, 8 - taps), (0, 0)))


def _groups_to_taps(g, taps, width):
    return g[:, :taps, :width].transpose(1, 0, 2).reshape(taps, -1)


SMALL_IN_COLS = 384
SMALL_OUT_COLS = 1536
SECTION = 8


class _Gathered:
    def __init__(self, make_src, groups, me, after):
        self.groups, self.states, self.ready = groups, [], {}
        self.group_of = {k: gi for gi, g in enumerate(groups) for k in g}
        token, self.first = after, None
        for gi, g in enumerate(groups):
            srcs = [make_src(k, None if gi == 0 else token[0:1, 0:1]) for k in g]
            placed = place_own([(a, ()) for a in srcs], me, token, f"place_own_{gi}")
            state, token = gather_start(srcs, placed, token, f"gather_start_{gi}")
            self.states.append(state)
            if gi == 0:
                self.first = token[0:1, 0:1]
        self.started = token

    def _get(self, key, after):
        gi = self.group_of[key]
        if gi not in self.ready:
            together = [gi, gi + 1] if gi == len(self.groups) - 2 else [gi]
            got, shapes = [], []
            for g in together:
                got += gather_wait(self.states[g], after if g else self.started, f"gather_wait_{g}")
                shapes += self.states[g][3]
            passed = gather_pass_on(got, shapes, f"gather_pass_on_{gi}")
            for g in together:
                n = len(self.groups[g])
                self.ready[g], passed = dict(zip(self.groups[g], passed[:n])), passed[n:]
        return self.ready[gi][key]

    def ffn(self, l, i, after):
        return tuple(self._get((n, l, i), after) for n in ("wg", "wu", "wd"))

    def mixer(self, l, after):
        win = _chips_to_cols(self._get(("win", l), after))
        win = _cols_to_groups(win) if l % 2 == 0 else _mlstm_in_groups(win)
        wkv = _cols_to_groups(self._get(("wkv", l), after).reshape(D_MODEL, 2 * D_XA))
        wout = self._get(("wout", l), after)
        if l % 2:
            wout = wout.reshape(D_MODEL, D_MODEL)
            tok = jnp.pad(wout[:D_TOK].reshape(ML_HEADS, ML_HEAD_DIM, D_MODEL), ((0, 0), (0, GROUP - ML_HEAD_DIM), (0, 0)))
            wout = jnp.concatenate([tok, wout[D_TOK:][None]], axis=0)
        return win, wkv, wout


class _GradSink:
    def __init__(self, apply):
        self.queue, self.apply, self.count, self.done = [], apply, 0, None

    @staticmethod
    def _by_chip(key, g):
        if key[0] == "wkv":
            return _groups_to_cols(g).reshape(N_CHIPS, D_MODEL // N_CHIPS, 2 * D_XA)
        if key[0] == "win":
            return _cols_to_chips(_groups_to_cols(g) if key[1] % 2 == 0 else _mlstm_in_ungroup(g))
        if key[0] == "wout" and key[1] % 2:
            full = jnp.concatenate([g[:ML_HEADS, :ML_HEAD_DIM].reshape(D_TOK, D_MODEL), g[ML_HEADS]], axis=0)
            return full.reshape(N_CHIPS, D_MODEL // N_CHIPS, D_MODEL)
        return g

    def push(self, grads):
        keys = list(grads)
        state, token = rs_begin([self._by_chip(k, grads[k]) for k in keys], self.done, f"rs_start_{self.count}")
        if self.queue:
            self._finish(token)
        self.queue.append((keys, state, self.count))
        self.count += 1
        return token

    def flush(self):
        self._finish(self.done)

    def _finish(self, after):
        keys, state, i = self.queue.pop(0)
        for key, g in zip(keys, rs_end(state, after, f"rs_wait_{i}")):
            self.done = self.apply(key, g, self.done)


def _local_step(x, mem, tgt, P, weights, sink):
    memb = mem.astype(BF16)
    saved = []
    pin0 = getattr(weights, "first", None)
    X, Xb = x, (x if pin0 is None else x + pin0).astype(BF16)
    after = Xb
    for l in range(DEPTH):
        s = {}
        s["x0b"] = Xb
        s["wa"] = weights.ffn(l, 0, after)
        s["g1a"], s["u1a"], s["ha"], s["z1"], X1, X1b = ffn_fwd(Xb, X, *s["wa"], P["ln_g"][l][0], P["ln_b"][l][0])
        s["x1b"] = X1b
        s["wm"] = win, wkv, wout = weights.mixer(l, X1b)
        u = proj(X1b, win, "mixer_in")
        kv = proj(memb, wkv, "mem_kv")
        s["u"], s["kv"] = u, kv
        if l % 2 == 0:
            tok = conv_mixer_fwd(u, P["convw"])
            qg = 9
        else:
            s["qk"] = qk_conv_fwd(u, P["qkw"])
            s["hm"], s["cst"], s["mst"] = mlstm_fwd(s["qk"], u, P["bg"])
            tok = head_norm_fwd(s["hm"], u, P["hg"])
            qg = 16
        s["m"] = xattn_fwd(u, qg, kv, tok)
        s["z2"], X2, X2b = contract_ln(s["m"], wout, X1, P["ln_g"][l][1], P["ln_b"][l][1], 1.0, "mixer_out_ln")
        s["x2b"] = X2b
        s["wb"] = weights.ffn(l, 1, X2b)
        s["g1b"], s["u1b"], s["hb"], s["z3"], X, Xb = ffn_fwd(X2b, X2, *s["wb"], P["ln_g"][l][2], P["ln_b"][l][2])
        after = Xb
        saved.append(s)

    loss, dX = loss_grad(X, tgt)

    G = {"ln_g": [[None] * 3 for _ in range(DEPTH)], "ln_b": [[None] * 3 for _ in range(DEPTH)]}
    pin = [jnp.zeros((1, 1), F32)]

    def ffn_backward(l, i, dX, z, xinb, g1, u1, h, w):
        k = 2 * i
        dgb, dub, dx, dyb, G["ln_g"][l][k], G["ln_b"][l][k] = ffn_bwd(dX, z, P["ln_g"][l][k] + pin[0], w[2], w[0], w[1], g1, u1)
        grads = {("wd", l, i): wgrad(h, dyb, BF16, "wgrad_down"), ("wg", l, i): wgrad(dgb, xinb, BF16, "wgrad_gate"),
                 ("wu", l, i): wgrad(dub, xinb, BF16, "wgrad_up")}
        return dx, grads

    for l in reversed(range(DEPTH)):
        s = saved[l]
        win, wkv, wout = s["wm"]
        dX, grads = ffn_backward(l, 1, dX, s["z3"], s["x2b"], s["g1b"], s["u1b"], s["hb"], s["wb"])
        dm, dz2, dz2b, G["ln_g"][l][1], G["ln_b"][l][1] = mixer_out_bwd(dX, s["z2"], P["ln_g"][l][1], wout)
        grads[("wout", l)] = wgrad(s["m"], dz2b, BF16, "wgrad_out")
        u, kv = s["u"], s["kv"]
        if l % 2 == 0:
            db, dc, dxi, G["convw"] = conv_mixer_bwd(u, P["convw"], dm)
            dq, dkv = xattn_bwd(u, 9, kv, dm, 3)
            du = jnp.concatenate([db, dc, dxi, dq], axis=0)
        else:
            dh, du, G["hg"] = head_norm_bwd(s["hm"], u, P["hg"], dm)
            dqk, du, dgate, G["bg"] = mlstm_bwd(s["qk"], u, P["bg"], s["cst"], s["mst"], dh, du)
            du, G["qkw"] = qk_conv_bwd(u, P["qkw"], dqk, du)
            du, dkv = xattn_bwd(u, 16, kv, dm, 4, du, dgate)
        grads[("win", l)] = wgrad(s["x1b"], du, BF16, "wgrad_in")
        grads[("wkv", l)] = wgrad(memb, dkv.astype(BF16), BF16, "wgrad_kv")
        dX = contract_t(du, win, dz2, "mixer_in_bwd")
        pin[0] = sink.push(grads)[0:1, 0:1]
        dX, grads = ffn_backward(l, 0, dX, s["z1"], s["x0b"], s["g1a"], s["u1a"], s["ha"], s["wa"])
        pin[0] = sink.push(grads)[0:1, 0:1]
    sink.flush()
    return loss, dX, G


def kernel(x, mem, ln_g, ln_b, ffn_w_gate, ffn_w_up, ffn_w_down, w_kv_mem, w_out, w_in_conv, conv_w, w_in_mlstm, b_gates, qk_conv_w, head_norm_g, loss_target, m_ln_g, m_ln_b, m_ffn_w_gate, m_ffn_w_up, m_ffn_w_down, m_w_kv_mem, m_w_out, m_w_in_conv, m_conv_w, m_w_in_mlstm, m_b_gates, m_qk_conv_w, m_head_norm_g, v_ln_g, v_ln_b, v_ffn_w_gate, v_ffn_w_up, v_ffn_w_down, v_w_kv_mem, v_w_out, v_w_in_conv, v_conv_w, v_w_in_mlstm, v_b_gates, v_qk_conv_w, v_head_norm_g):
    cx, cy = lax.axis_index("x"), lax.axis_index("y")
    chip = 2 * cx + cy

    def make_src(key, pin):
        if key[0] in ("wg", "wu"):
            w = jnp.swapaxes((ffn_w_gate if key[0] == "wg" else ffn_w_up)[key[1], key[2]], 0, 1)
        elif key[0] == "wd":
            w = ffn_w_down[key[1], key[2]]
        elif key[0] == "win":
            w = (w_in_conv, w_in_mlstm)[key[1]][0]
        else:
            w = (w_kv_mem if key[0] == "wkv" else w_out)[key[1]]
        return (w if pin is None else w + pin).astype(BF16)

    ffn_keys = lambda l, i: [("wg", l, i), ("wu", l, i), ("wd", l, i)]
    mixer_keys = lambda l: [("win", l), ("wkv", l), ("wout", l)]
    groups = [ffn_keys(0, 0), mixer_keys(0) + mixer_keys(1), ffn_keys(0, 1), ffn_keys(1, 0), ffn_keys(1, 1)]
    def section(a, width):
        a = a.reshape(-1, a.shape[-1])
        return jnp.pad(a, ((0, SECTION - a.shape[0]), (0, width - a.shape[1])))

    small = jnp.concatenate([section(a, SMALL_IN_COLS) for a in (ln_g, ln_b, conv_w, qk_conv_w)], axis=0)
    smalls = small_allgather(small, reduce=False)
    gathered = _Gathered(make_src, groups, jnp.reshape(chip, (1,)).astype(jnp.int32), smalls)
    smalls = smalls[0::2]
    ln_g_full = _chips_to_cols(smalls[:, 0:6, 0:256]).reshape(DEPTH, 3, 1, D_MODEL)
    ln_b_full = _chips_to_cols(smalls[:, 8:14, 0:256]).reshape(DEPTH, 3, 1, D_MODEL)
    conv_w_full = _chips_to_cols(smalls[:, 16:19, 0:192])
    qk_w_full = _chips_to_cols(smalls[:, 24:28, 0:384])

    P = {"ln_g": ln_g_full, "ln_b": ln_b_full, "convw": _taps_to_groups(conv_w_full, GROUP),
         "qkw": _taps_to_groups(qk_w_full, ML_HEAD_DIM), "bg": _pad_last(b_gates, GROUP),
         "hg": _pad_last(head_norm_g[0], GROUP)[:, None, :]}

    weights = {"ln_g": ln_g, "ln_b": ln_b, "ffn_w_gate": ffn_w_gate, "ffn_w_up": ffn_w_up, "ffn_w_down": ffn_w_down,
               "w_kv_mem": w_kv_mem, "w_out": w_out, "w_in_conv": w_in_conv, "conv_w": conv_w, "w_in_mlstm": w_in_mlstm,
               "b_gates": b_gates, "qk_conv_w": qk_conv_w, "head_norm_g": head_norm_g}
    ms = {"ln_g": m_ln_g, "ln_b": m_ln_b, "ffn_w_gate": m_ffn_w_gate, "ffn_w_up": m_ffn_w_up, "ffn_w_down": m_ffn_w_down,
          "w_kv_mem": m_w_kv_mem, "w_out": m_w_out, "w_in_conv": m_w_in_conv, "conv_w": m_conv_w, "w_in_mlstm": m_w_in_mlstm,
          "b_gates": m_b_gates, "qk_conv_w": m_qk_conv_w, "head_norm_g": m_head_norm_g}
    vs = {"ln_g": v_ln_g, "ln_b": v_ln_b, "ffn_w_gate": v_ffn_w_gate, "ffn_w_up": v_ffn_w_up, "ffn_w_down": v_ffn_w_down,
          "w_kv_mem": v_w_kv_mem, "w_out": v_w_out, "w_in_conv": v_w_in_conv, "conv_w": v_conv_w, "w_in_mlstm": v_w_in_mlstm,
          "b_gates": v_b_gates, "qk_conv_w": v_qk_conv_w, "head_norm_g": v_head_norm_g}
    names = list(weights)
    owner = {"wg": ("ffn_w_gate", True), "wu": ("ffn_w_up", True), "wd": ("ffn_w_down", False), "wkv": ("w_kv_mem", False),
             "wout": ("w_out", False), "win": None}
    updated = {}

    def apply(key, g, after):
        name, transposed = owner[key[0]] or (("w_in_conv", "w_in_mlstm")[key[1]], False)
        idx = (0,) if key[0] == "win" else tuple(key[1:])
        view = (lambda a: jnp.swapaxes(a, -1, -2)) if transposed else (lambda a: a)
        updated[name], token = adamw_into(view(weights[name]), view(ms[name]), view(vs[name]), g, updated.get(name), idx, after,
                                          "adamw_" + name + "_" + "_".join(map(str, idx)))
        return token

    sink = _GradSink(apply)
    loss, grad_x, G = _local_step(x[0], mem[0], loss_target[0], P, gathered, sink)

    dln_g = jnp.concatenate([G["ln_g"][l][k] for l in range(DEPTH) for k in range(3)], axis=0)
    dln_b = jnp.concatenate([G["ln_b"][l][k] for l in range(DEPTH) for k in range(3)], axis=0)
    lane = lax.broadcasted_iota(jnp.int32, (1, GROUP), 1)
    misc = jnp.where(lane < 8, G["bg"], 0.0) + jnp.where(lane == 8, loss, 0.0) + sink.done[0:1, 0:1]
    parts = (dln_g, dln_b, _groups_to_taps(G["convw"], 3, GROUP), misc, _groups_to_taps(G["qkw"], 4, ML_HEAD_DIM),
             G["hg"][:, 0, :ML_HEAD_DIM])
    tot = small_allgather(jnp.concatenate([section(a, SMALL_OUT_COLS) for a in parts], axis=0), reduce=True)
    loss_total = tot[24, 8]

    small_grads = {
        "ln_g": lax.dynamic_slice(tot[0:6, 0:D_MODEL], (0, chip * 256), (6, 256)).reshape(DEPTH, 3, 256),
        "ln_b": lax.dynamic_slice(tot[8:14, 0:D_MODEL], (0, chip * 256), (6, 256)).reshape(DEPTH, 3, 256),
        "conv_w": lax.dynamic_slice(tot[16:19, 0:D_TOK], (0, chip * 192), (3, 192))[None],
        "b_gates": tot[24:25, 0:8],
        "qk_conv_w": lax.dynamic_slice(tot[32:36, 0:2 * D_TOK], (0, chip * 384), (4, 384))[None],
        "head_norm_g": tot[40:44, 0:ML_HEAD_DIM][None],
    }
    grads, deltas, new_m, new_v = [], [], [], []
    for nme in names:
        if nme in updated:
            back = (lambda a: jnp.swapaxes(a, -1, -2)) if nme in ("ffn_w_gate", "ffn_w_up") else (lambda a: a)
            g, d, nm, nv = (back(a) for a in updated[nme])
        else:
            w, g = weights[nme], small_grads[nme]
            two = (math.prod(w.shape[:-1]), w.shape[-1])
            d, nm, nv = (a.reshape(w.shape) for a in adamw(w.reshape(two), g.reshape(two), ms[nme].reshape(two),
                                                           vs[nme].reshape(two), "adamw_" + nme))
        grads.append(g)
        deltas.append(d)
        new_m.append(nm)
        new_v.append(nv)
    return (loss_total, grad_x[None], *grads, *deltas, *new_m, *new_v)
```
